```python
import math
import jax, jax.numpy as jnp
from jax import lax
import numpy as np

D_MODEL = 1024
BATCH = 8
SEQ = 4096
DEPTH = 1

HEAD_DIM = 64
N_ATTN_HEADS = 8
N_GMLP_GROUPS = 8
D_ATTN = N_ATTN_HEADS * HEAD_DIM
D_GMLP = N_GMLP_GROUPS * HEAD_DIM
D_MIX = D_ATTN + D_GMLP
D_IN = 3 * D_ATTN + 2 * D_GMLP
DILATIONS = ((128, 1), (512, 4), (2048, 16))
ROPE_THETA = 500000.0
ROPE_DIM = HEAD_DIM // 4
CHUNK = 128
D_FF = 2816
CONV_WIDTH = 3
D_PLE = 256
LN_EPS = 1e-5
ALPHA = (2.0 * DEPTH) ** 0.25
BETA = (8.0 * DEPTH) ** -0.25
NEG_INF = -1e30

kernel_name = "hybrid_dilated_attn_gmlp_deepnorm_layer"


def _layernorm(x, g, b):
    xf = x.astype(jnp.float32)
    mu = jnp.mean(xf, axis=-1, keepdims=True)
    var = jnp.mean(jnp.square(xf - mu), axis=-1, keepdims=True)
    y = (xf - mu) * lax.rsqrt(var + LN_EPS)
    return (y * g.astype(jnp.float32) + b.astype(jnp.float32)).astype(x.dtype)


def _partial_rope(t, positions):
    inv = ROPE_THETA ** (-jnp.arange(0, ROPE_DIM, 2, dtype=jnp.float32) / ROPE_DIM)
    ang = positions.astype(jnp.float32)[..., None] * inv
    cos = jnp.cos(ang)[:, :, None, :]
    sin = jnp.sin(ang)[:, :, None, :]
    half = ROPE_DIM // 2
    x1 = t[..., :half]
    x2 = t[..., half:ROPE_DIM]
    rot = jnp.concatenate([x1 * cos - x2 * sin, x2 * cos + x1 * sin], axis=-1)
    return jnp.concatenate([rot, t[..., ROPE_DIM:]], axis=-1)


def _dilated_branch(q, k, v, window, dilation):
    b, s, h, dh = q.shape
    d = dilation
    w = window // d
    L = s // d
    nb = -(-L // w)
    Lp = nb * w

    def sub(t):
        t = t.reshape(b, L, d, h, dh)
        return jnp.pad(t, ((0, 0), (0, Lp - L), (0, 0), (0, 0), (0, 0)))

    def band(t):
        tp = jnp.pad(t, ((0, 0), (w, 0), (0, 0), (0, 0), (0, 0)))
        prev = tp[:, :Lp].reshape(b, nb, w, d, h, dh)
        cur = t.reshape(b, nb, w, d, h, dh)
        return jnp.concatenate([prev, cur], axis=2)

    qb = sub(q).reshape(b, nb, w, d, h, dh)
    kb = band(sub(k))
    vb = band(sub(v))
    scores = jnp.einsum('bnqrhd,bnkrhd->bnrhqk', qb, kb) * (1.0 / math.sqrt(dh))
    qi = jnp.arange(nb)[:, None, None] * w + jnp.arange(w)[None, :, None]
    kj = jnp.arange(nb)[:, None, None] * w - w + jnp.arange(2 * w)[None, None, :]
    dist = qi - kj
    mask = (dist >= 0) & (dist <= w) & (kj >= 0)
    scores = jnp.where(mask[None, :, None, None], scores, NEG_INF)
    m = jnp.max(scores, axis=-1, keepdims=True)
    e = jnp.exp(scores - m)
    den = jnp.sum(e, axis=-1, keepdims=True)
    out = jnp.einsum('bnrhqk,bnkrhd->bnqrhd', e / den, vb)
    lse = jnp.transpose((m + jnp.log(den))[..., 0], (0, 1, 4, 2, 3))
    out = out.reshape(b, Lp, d, h, dh)[:, :L].reshape(b, s, h, dh)
    lse = lse.reshape(b, Lp, d, h)[:, :L].reshape(b, s, h)
    return out, lse


def _dilated_attention(q, k, v):
    outs, lses = [], []
    for window, dilation in DILATIONS:
        o, l = _dilated_branch(q, k, v, window, dilation)
        outs.append(o)
        lses.append(l)
    wts = jax.nn.softmax(jnp.stack(lses, axis=0), axis=0)
    return jnp.einsum('cbsh,cbshd->bshd', wts, jnp.stack(outs, axis=0))


def _chunked_gmlp(u, z, ln_z_g, ln_z_b, w_s, b_s):
    b, s, _ = z.shape
    zn = _layernorm(z, ln_z_g, ln_z_b)
    zc = zn.reshape(b, s // CHUNK, CHUNK, N_GMLP_GROUPS, HEAD_DIM)
    causal = jnp.tril(jnp.ones((CHUNK, CHUNK), dtype=w_s.dtype))
    mixed = jnp.einsum('gij,bcjgd->bcigd', w_s * causal, zc) + jnp.transpose(b_s)[None, None, :, :, None]
    return u * mixed.reshape(b, s, D_GMLP)


def _causal_dwconv(a, w, bias):
    s = a.shape[1]
    ap = jnp.pad(a, ((0, 0), (CONV_WIDTH - 1, 0), (0, 0)))
    out = bias
    for kk in range(CONV_WIDTH):
        out = out + w[kk] * ap[:, kk:kk + s]
    return out


def _fwd_setup_inputs(seed: int = 0) -> dict:
    key = jax.random.key(seed)
    ks = jax.random.split(key, 24)
    nrm = lambda k, shape, scale: jax.random.normal(k, shape, dtype=jnp.float32) * scale
    gain = lambda k, n: 1.0 + nrm(k, (DEPTH, n), 0.01)
    bias = lambda k, n: nrm(k, (DEPTH, n), 0.01)
    x = nrm(ks[0], (BATCH, SEQ, D_MODEL), 1.0)
    p = nrm(ks[1], (DEPTH, BATCH, SEQ, D_PLE), 1.0)
    offs = jax.random.randint(ks[2], (BATCH, 1), 0, 1024, dtype=jnp.int32)
    positions = offs + jnp.arange(SEQ, dtype=jnp.int32)[None, :]
    w_in = nrm(ks[3], (DEPTH, D_MODEL, D_IN), D_MODEL ** -0.5)
    col_scale = jnp.concatenate([jnp.ones((2 * D_ATTN,), jnp.float32),
                                 jnp.full((D_ATTN,), BETA, jnp.float32),
                                 jnp.ones((2 * D_GMLP,), jnp.float32)])
    w_in = w_in * col_scale
    return {
        "x": x,
        "p": p,
        "positions": positions,
        "w_in": w_in,
        "ln_z_g": gain(ks[4], D_GMLP),
        "ln_z_b": bias(ks[5], D_GMLP),
        "w_s": nrm(ks[6], (DEPTH, N_GMLP_GROUPS, CHUNK, CHUNK), CHUNK ** -0.5),
        "b_s": 1.0 + nrm(ks[7], (DEPTH, N_GMLP_GROUPS, CHUNK), 0.01),
        "w_o": nrm(ks[8], (DEPTH, D_MIX, D_MODEL), BETA * D_MIX ** -0.5),
        "ln1_g": gain(ks[9], D_MODEL),
        "ln1_b": bias(ks[10], D_MODEL),
        "w_ff_a": nrm(ks[11], (DEPTH, D_MODEL, D_FF), BETA * D_MODEL ** -0.5),
        "w_ff_b": nrm(ks[12], (DEPTH, D_MODEL, D_FF), BETA * D_MODEL ** -0.5),
        "conv_w": nrm(ks[13], (DEPTH, CONV_WIDTH, D_FF), CONV_WIDTH ** -0.5),
        "conv_b": bias(ks[14], D_FF),
        "w_ff_down": nrm(ks[15], (DEPTH, D_FF, D_MODEL), BETA * D_FF ** -0.5),
        "ln2_g": gain(ks[16], D_MODEL),
        "ln2_b": bias(ks[17], D_MODEL),
        "w_ple_gate": nrm(ks[18], (DEPTH, D_MODEL, D_MODEL), D_MODEL ** -0.5),
        "b_ple_gate": bias(ks[19], D_MODEL),
        "w_ple_in": nrm(ks[20], (DEPTH, D_PLE, D_MODEL), BETA * D_PLE ** -0.5),
        "ln3_g": gain(ks[21], D_MODEL),
        "ln3_b": bias(ks[22], D_MODEL),
    }


def _fwd_reference(x, p, positions, w_in, ln_z_g, ln_z_b, w_s, b_s, w_o, ln1_g, ln1_b,
              w_ff_a, w_ff_b, conv_w, conv_b, w_ff_down, ln2_g, ln2_b,
              w_ple_gate, b_ple_gate, w_ple_in, ln3_g, ln3_b):
    b, s, _ = x.shape
    for i in range(DEPTH):
        h = x @ w_in[i]
        q = h[..., :D_ATTN].reshape(b, s, N_ATTN_HEADS, HEAD_DIM)
        k = h[..., D_ATTN:2 * D_ATTN].reshape(b, s, N_ATTN_HEADS, HEAD_DIM)
        v = h[..., 2 * D_ATTN:3 * D_ATTN].reshape(b, s, N_ATTN_HEADS, HEAD_DIM)
        u = jax.nn.gelu(h[..., 3 * D_ATTN:3 * D_ATTN + D_GMLP], approximate=False)
        z = jax.nn.gelu(h[..., 3 * D_ATTN + D_GMLP:], approximate=False)
        q = _partial_rope(q, positions).astype(jnp.float32)
        k = _partial_rope(k, positions).astype(jnp.float32)
        attn = _dilated_attention(q, k, v.astype(jnp.float32)).astype(x.dtype).reshape(b, s, D_ATTN)
        gm = _chunked_gmlp(u, z, ln_z_g[i], ln_z_b[i], w_s[i], b_s[i])
        mix = jnp.concatenate([attn, gm], axis=-1) @ w_o[i]
        x = _layernorm(ALPHA * x + mix, ln1_g[i], ln1_b[i])
        a = _causal_dwconv(x @ w_ff_a[i], conv_w[i], conv_b[i])
        ff = (jax.nn.gelu(a, approximate=False) * (x @ w_ff_b[i])) @ w_ff_down[i]
        x = _layernorm(ALPHA * x + ff, ln2_g[i], ln2_b[i])
        gate = jax.nn.sigmoid(x @ w_ple_gate[i] + b_ple_gate[i])
        ple = gate * (p[i] @ w_ple_in[i])
        x = _layernorm(ALPHA * x + ple, ln3_g[i], ln3_b[i])
    return x


import jax as _jax
import jax.numpy as _jnp

TWIN_FORMAT = 'train_step'
FWD_PARAMS = ['x', 'p', 'positions', 'w_in', 'ln_z_g', 'ln_z_b', 'w_s', 'b_s', 'w_o', 'ln1_g', 'ln1_b', 'w_ff_a', 'w_ff_b', 'conv_w', 'conv_b', 'w_ff_down', 'ln2_g', 'ln2_b', 'w_ple_gate', 'b_ple_gate', 'w_ple_in', 'ln3_g', 'ln3_b']
TWIN_WEIGHTS = ['w_in', 'ln_z_g', 'ln_z_b', 'w_s', 'b_s', 'w_o', 'ln1_g', 'ln1_b', 'w_ff_a', 'w_ff_b', 'conv_w', 'conv_b', 'w_ff_down', 'ln2_g', 'ln2_b', 'w_ple_gate', 'b_ple_gate', 'w_ple_in', 'ln3_g', 'ln3_b']
TWIN_DIFF_INPUT = 'x'
TWIN_INPUTS = ['x', 'p', 'positions', 'w_in', 'ln_z_g', 'ln_z_b', 'w_s', 'b_s', 'w_o', 'ln1_g', 'ln1_b', 'w_ff_a', 'w_ff_b', 'conv_w', 'conv_b', 'w_ff_down', 'ln2_g', 'ln2_b', 'w_ple_gate', 'b_ple_gate', 'w_ple_in', 'ln3_g', 'ln3_b', 'loss_target', 'm_w_in', 'm_ln_z_g', 'm_ln_z_b', 'm_w_s', 'm_b_s', 'm_w_o', 'm_ln1_g', 'm_ln1_b', 'm_w_ff_a', 'm_w_ff_b', 'm_conv_w', 'm_conv_b', 'm_w_ff_down', 'm_ln2_g', 'm_ln2_b', 'm_w_ple_gate', 'm_b_ple_gate', 'm_w_ple_in', 'm_ln3_g', 'm_ln3_b', 'v_w_in', 'v_ln_z_g', 'v_ln_z_b', 'v_w_s', 'v_b_s', 'v_w_o', 'v_ln1_g', 'v_ln1_b', 'v_w_ff_a', 'v_w_ff_b', 'v_conv_w', 'v_conv_b', 'v_w_ff_down', 'v_ln2_g', 'v_ln2_b', 'v_w_ple_gate', 'v_b_ple_gate', 'v_w_ple_in', 'v_ln3_g', 'v_ln3_b']
TWIN_OUTPUTS = ['loss', 'grad_x', 'grad_w_in', 'grad_ln_z_g', 'grad_ln_z_b', 'grad_w_s', 'grad_b_s', 'grad_w_o', 'grad_ln1_g', 'grad_ln1_b', 'grad_w_ff_a', 'grad_w_ff_b', 'grad_conv_w', 'grad_conv_b', 'grad_w_ff_down', 'grad_ln2_g', 'grad_ln2_b', 'grad_w_ple_gate', 'grad_b_ple_gate', 'grad_w_ple_in', 'grad_ln3_g', 'grad_ln3_b', 'delta_w_in', 'delta_ln_z_g', 'delta_ln_z_b', 'delta_w_s', 'delta_b_s', 'delta_w_o', 'delta_ln1_g', 'delta_ln1_b', 'delta_w_ff_a', 'delta_w_ff_b', 'delta_conv_w', 'delta_conv_b', 'delta_w_ff_down', 'delta_ln2_g', 'delta_ln2_b', 'delta_w_ple_gate', 'delta_b_ple_gate', 'delta_w_ple_in', 'delta_ln3_g', 'delta_ln3_b', 'new_m_w_in', 'new_m_ln_z_g', 'new_m_ln_z_b', 'new_m_w_s', 'new_m_b_s', 'new_m_w_o', 'new_m_ln1_g', 'new_m_ln1_b', 'new_m_w_ff_a', 'new_m_w_ff_b', 'new_m_conv_w', 'new_m_conv_b', 'new_m_w_ff_down', 'new_m_ln2_g', 'new_m_ln2_b', 'new_m_w_ple_gate', 'new_m_b_ple_gate', 'new_m_w_ple_in', 'new_m_ln3_g', 'new_m_ln3_b', 'new_v_w_in', 'new_v_ln_z_g', 'new_v_ln_z_b', 'new_v_w_s', 'new_v_b_s', 'new_v_w_o', 'new_v_ln1_g', 'new_v_ln1_b', 'new_v_w_ff_a', 'new_v_w_ff_b', 'new_v_conv_w', 'new_v_conv_b', 'new_v_w_ff_down', 'new_v_ln2_g', 'new_v_ln2_b', 'new_v_w_ple_gate', 'new_v_b_ple_gate', 'new_v_w_ple_in', 'new_v_ln3_g', 'new_v_ln3_b']
TWIN_LEAF_KINDS = {'loss': 'loss', 'grad_x': 'grad_x', 'grad_w_in': 'grad_w', 'grad_ln_z_g': 'grad_w', 'grad_ln_z_b': 'grad_w', 'grad_w_s': 'grad_w', 'grad_b_s': 'grad_w', 'grad_w_o': 'grad_w', 'grad_ln1_g': 'grad_w', 'grad_ln1_b': 'grad_w', 'grad_w_ff_a': 'grad_w', 'grad_w_ff_b': 'grad_w', 'grad_conv_w': 'grad_w', 'grad_conv_b': 'grad_w', 'grad_w_ff_down': 'grad_w', 'grad_ln2_g': 'grad_w', 'grad_ln2_b': 'grad_w', 'grad_w_ple_gate': 'grad_w', 'grad_b_ple_gate': 'grad_w', 'grad_w_ple_in': 'grad_w', 'grad_ln3_g': 'grad_w', 'grad_ln3_b': 'grad_w', 'delta_w_in': 'delta_w', 'delta_ln_z_g': 'delta_w', 'delta_ln_z_b': 'delta_w', 'delta_w_s': 'delta_w', 'delta_b_s': 'delta_w', 'delta_w_o': 'delta_w', 'delta_ln1_g': 'delta_w', 'delta_ln1_b': 'delta_w', 'delta_w_ff_a': 'delta_w', 'delta_w_ff_b': 'delta_w', 'delta_conv_w': 'delta_w', 'delta_conv_b': 'delta_w', 'delta_w_ff_down': 'delta_w', 'delta_ln2_g': 'delta_w', 'delta_ln2_b': 'delta_w', 'delta_w_ple_gate': 'delta_w', 'delta_b_ple_gate': 'delta_w', 'delta_w_ple_in': 'delta_w', 'delta_ln3_g': 'delta_w', 'delta_ln3_b': 'delta_w', 'new_m_w_in': 'new_m', 'new_m_ln_z_g': 'new_m', 'new_m_ln_z_b': 'new_m', 'new_m_w_s': 'new_m', 'new_m_b_s': 'new_m', 'new_m_w_o': 'new_m', 'new_m_ln1_g': 'new_m', 'new_m_ln1_b': 'new_m', 'new_m_w_ff_a': 'new_m', 'new_m_w_ff_b': 'new_m', 'new_m_conv_w': 'new_m', 'new_m_conv_b': 'new_m', 'new_m_w_ff_down': 'new_m', 'new_m_ln2_g': 'new_m', 'new_m_ln2_b': 'new_m', 'new_m_w_ple_gate': 'new_m', 'new_m_b_ple_gate': 'new_m', 'new_m_w_ple_in': 'new_m', 'new_m_ln3_g': 'new_m', 'new_m_ln3_b': 'new_m', 'new_v_w_in': 'new_v', 'new_v_ln_z_g': 'new_v', 'new_v_ln_z_b': 'new_v', 'new_v_w_s': 'new_v', 'new_v_b_s': 'new_v', 'new_v_w_o': 'new_v', 'new_v_ln1_g': 'new_v', 'new_v_ln1_b': 'new_v', 'new_v_w_ff_a': 'new_v', 'new_v_w_ff_b': 'new_v', 'new_v_conv_w': 'new_v', 'new_v_conv_b': 'new_v', 'new_v_w_ff_down': 'new_v', 'new_v_ln2_g': 'new_v', 'new_v_ln2_b': 'new_v', 'new_v_w_ple_gate': 'new_v', 'new_v_b_ple_gate': 'new_v', 'new_v_w_ple_in': 'new_v', 'new_v_ln3_g': 'new_v', 'new_v_ln3_b': 'new_v'}


def _forward(args):
    return _fwd_reference(*[args[k] for k in FWD_PARAMS])


def _output_shape():
    out = _jax.eval_shape(lambda: _forward(_fwd_setup_inputs(0)))
    return out.shape, out.dtype

N_MICROBATCH = 1
ADAM_LR = 0.001
ADAM_B1 = 0.9
ADAM_B2 = 0.999
ADAM_EPS = 1e-08
ADAM_WD = 0.01
ADAM_STEP = 10
PER_EXAMPLE_BATCH_AXIS = {'x': 0, 'p': 1, 'positions': 0, 'loss_target': 0}
SHARED_INPUTS = []
_WEIGHT_DTYPES = {'w_in': _jnp.float32, 'ln_z_g': _jnp.float32, 'ln_z_b': _jnp.float32, 'w_s': _jnp.float32, 'b_s': _jnp.float32, 'w_o': _jnp.float32, 'ln1_g': _jnp.float32, 'ln1_b': _jnp.float32, 'w_ff_a': _jnp.float32, 'w_ff_b': _jnp.float32, 'conv_w': _jnp.float32, 'conv_b': _jnp.float32, 'w_ff_down': _jnp.float32, 'ln2_g': _jnp.float32, 'ln2_b': _jnp.float32, 'w_ple_gate': _jnp.float32, 'b_ple_gate': _jnp.float32, 'w_ple_in': _jnp.float32, 'ln3_g': _jnp.float32, 'ln3_b': _jnp.float32}
MOMENT_SCALE = {'w_in': 3.791800e-02, 'ln_z_g': 3.837266e-02, 'ln_z_b': 4.023049e-02, 'w_s': 2.704147e-02, 'b_s': 3.743282e-02, 'w_o': 8.216836e-02, 'ln1_g': 4.231663e-01, 'ln1_b': 2.446389e-01, 'w_ff_a': 1.957377e-02, 'w_ff_b': 1.898625e-02, 'conv_w': 1.157224e-02, 'conv_b': 1.903016e-02, 'w_ff_down': 3.145627e-02, 'ln2_g': 4.319318e-01, 'ln2_b': 2.426528e-01, 'w_ple_gate': 1.802825e-02, 'b_ple_gate': 2.112731e-02, 'w_ple_in': 7.813938e-02, 'ln3_g': 3.197895e+01, 'ln3_b': 2.696722e+00}


def _to_microbatches(a, axis):
    t = _jnp.moveaxis(a, axis, 0)
    t = t.reshape((N_MICROBATCH, t.shape[0] // N_MICROBATCH) + t.shape[1:])
    return _jnp.moveaxis(t, 1, axis + 1)


def setup_inputs(seed: int = 0) -> dict:
    inp = _fwd_setup_inputs(seed)
    key = _jax.random.fold_in(_jax.random.key(seed), 7919)
    shape, _ = _output_shape()
    out = dict(inp)
    out["loss_target"] = _jax.random.normal(_jax.random.fold_in(key, 0), shape, _jnp.float32)
    for i, name in enumerate(TWIN_WEIGHTS):
        w = inp[name].astype(_jnp.float32)
        if MOMENT_SCALE is None:
            s = _jnp.sqrt(_jnp.mean(_jnp.square(w)) + 1e-30)
        else:
            s = MOMENT_SCALE[name]
        km, kv = _jax.random.split(_jax.random.fold_in(key, i + 1))
        out[name] = w
        out["m_" + name] = s * _jax.random.normal(km, w.shape, _jnp.float32)
        out["v_" + name] = (s * s) * _jax.random.uniform(kv, w.shape, _jnp.float32, 0.5, 1.5)
    if N_MICROBATCH > 1:
        for name, axis in PER_EXAMPLE_BATCH_AXIS.items():
            out[name] = _to_microbatches(out[name], axis)
    return {'x': out['x'], 'p': out['p'], 'positions': out['positions'], 'w_in': out['w_in'], 'ln_z_g': out['ln_z_g'], 'ln_z_b': out['ln_z_b'], 'w_s': out['w_s'], 'b_s': out['b_s'], 'w_o': out['w_o'], 'ln1_g': out['ln1_g'], 'ln1_b': out['ln1_b'], 'w_ff_a': out['w_ff_a'], 'w_ff_b': out['w_ff_b'], 'conv_w': out['conv_w'], 'conv_b': out['conv_b'], 'w_ff_down': out['w_ff_down'], 'ln2_g': out['ln2_g'], 'ln2_b': out['ln2_b'], 'w_ple_gate': out['w_ple_gate'], 'b_ple_gate': out['b_ple_gate'], 'w_ple_in': out['w_ple_in'], 'ln3_g': out['ln3_g'], 'ln3_b': out['ln3_b'], 'loss_target': out['loss_target'], 'm_w_in': out['m_w_in'], 'm_ln_z_g': out['m_ln_z_g'], 'm_ln_z_b': out['m_ln_z_b'], 'm_w_s': out['m_w_s'], 'm_b_s': out['m_b_s'], 'm_w_o': out['m_w_o'], 'm_ln1_g': out['m_ln1_g'], 'm_ln1_b': out['m_ln1_b'], 'm_w_ff_a': out['m_w_ff_a'], 'm_w_ff_b': out['m_w_ff_b'], 'm_conv_w': out['m_conv_w'], 'm_conv_b': out['m_conv_b'], 'm_w_ff_down': out['m_w_ff_down'], 'm_ln2_g': out['m_ln2_g'], 'm_ln2_b': out['m_ln2_b'], 'm_w_ple_gate': out['m_w_ple_gate'], 'm_b_ple_gate': out['m_b_ple_gate'], 'm_w_ple_in': out['m_w_ple_in'], 'm_ln3_g': out['m_ln3_g'], 'm_ln3_b': out['m_ln3_b'], 'v_w_in': out['v_w_in'], 'v_ln_z_g': out['v_ln_z_g'], 'v_ln_z_b': out['v_ln_z_b'], 'v_w_s': out['v_w_s'], 'v_b_s': out['v_b_s'], 'v_w_o': out['v_w_o'], 'v_ln1_g': out['v_ln1_g'], 'v_ln1_b': out['v_ln1_b'], 'v_w_ff_a': out['v_w_ff_a'], 'v_w_ff_b': out['v_w_ff_b'], 'v_conv_w': out['v_conv_w'], 'v_conv_b': out['v_conv_b'], 'v_w_ff_down': out['v_w_ff_down'], 'v_ln2_g': out['v_ln2_g'], 'v_ln2_b': out['v_ln2_b'], 'v_w_ple_gate': out['v_w_ple_gate'], 'v_b_ple_gate': out['v_b_ple_gate'], 'v_w_ple_in': out['v_w_ple_in'], 'v_ln3_g': out['v_ln3_g'], 'v_ln3_b': out['v_ln3_b']}


def _loss(weights, diff, rest, loss_target):
    with _jax.named_scope("forward"):
        args = {**rest, TWIN_DIFF_INPUT: diff, **{k: w.astype(_WEIGHT_DTYPES[k]) for k, w in weights.items()}}
        y = _forward(args)
    with _jax.named_scope("loss_head"):
        err = _jnp.square(y.astype(_jnp.float32) - loss_target)
        return 0.5 * _jnp.sum(_jnp.mean(err, axis=-1)) if err.ndim else 0.5 * err


def _adamw(w, g, m, v):
    m = ADAM_B1 * m + (1.0 - ADAM_B1) * g
    v = ADAM_B2 * v + (1.0 - ADAM_B2) * _jnp.square(g)
    m_hat = m / (1.0 - ADAM_B1 ** ADAM_STEP)
    v_hat = v / (1.0 - ADAM_B2 ** ADAM_STEP)
    delta = -ADAM_LR * (m_hat / (_jnp.sqrt(v_hat) + ADAM_EPS) + ADAM_WD * w)
    return delta, m, v


def reference(x, p, positions, w_in, ln_z_g, ln_z_b, w_s, b_s, w_o, ln1_g, ln1_b, w_ff_a, w_ff_b, conv_w, conv_b, w_ff_down, ln2_g, ln2_b, w_ple_gate, b_ple_gate, w_ple_in, ln3_g, ln3_b, loss_target, m_w_in, m_ln_z_g, m_ln_z_b, m_w_s, m_b_s, m_w_o, m_ln1_g, m_ln1_b, m_w_ff_a, m_w_ff_b, m_conv_w, m_conv_b, m_w_ff_down, m_ln2_g, m_ln2_b, m_w_ple_gate, m_b_ple_gate, m_w_ple_in, m_ln3_g, m_ln3_b, v_w_in, v_ln_z_g, v_ln_z_b, v_w_s, v_b_s, v_w_o, v_ln1_g, v_ln1_b, v_w_ff_a, v_w_ff_b, v_conv_w, v_conv_b, v_w_ff_down, v_ln2_g, v_ln2_b, v_w_ple_gate, v_b_ple_gate, v_w_ple_in, v_ln3_g, v_ln3_b):
    given = dict(x=x, p=p, positions=positions, w_in=w_in, ln_z_g=ln_z_g, ln_z_b=ln_z_b, w_s=w_s, b_s=b_s, w_o=w_o, ln1_g=ln1_g, ln1_b=ln1_b, w_ff_a=w_ff_a, w_ff_b=w_ff_b, conv_w=conv_w, conv_b=conv_b, w_ff_down=w_ff_down, ln2_g=ln2_g, ln2_b=ln2_b, w_ple_gate=w_ple_gate, b_ple_gate=b_ple_gate, w_ple_in=w_ple_in, ln3_g=ln3_g, ln3_b=ln3_b, loss_target=loss_target, m_w_in=m_w_in, m_ln_z_g=m_ln_z_g, m_ln_z_b=m_ln_z_b, m_w_s=m_w_s, m_b_s=m_b_s, m_w_o=m_w_o, m_ln1_g=m_ln1_g, m_ln1_b=m_ln1_b, m_w_ff_a=m_w_ff_a, m_w_ff_b=m_w_ff_b, m_conv_w=m_conv_w, m_conv_b=m_conv_b, m_w_ff_down=m_w_ff_down, m_ln2_g=m_ln2_g, m_ln2_b=m_ln2_b, m_w_ple_gate=m_w_ple_gate, m_b_ple_gate=m_b_ple_gate, m_w_ple_in=m_w_ple_in, m_ln3_g=m_ln3_g, m_ln3_b=m_ln3_b, v_w_in=v_w_in, v_ln_z_g=v_ln_z_g, v_ln_z_b=v_ln_z_b, v_w_s=v_w_s, v_b_s=v_b_s, v_w_o=v_w_o, v_ln1_g=v_ln1_g, v_ln1_b=v_ln1_b, v_w_ff_a=v_w_ff_a, v_w_ff_b=v_w_ff_b, v_conv_w=v_conv_w, v_conv_b=v_conv_b, v_w_ff_down=v_w_ff_down, v_ln2_g=v_ln2_g, v_ln2_b=v_ln2_b, v_w_ple_gate=v_w_ple_gate, v_b_ple_gate=v_b_ple_gate, v_w_ple_in=v_w_ple_in, v_ln3_g=v_ln3_g, v_ln3_b=v_ln3_b)
    weights = {n: given[n] for n in TWIN_WEIGHTS}
    shared = {n: given[n] for n in SHARED_INPUTS}
    per_example = {n: given[n] for n in ['x', 'p', 'positions']}
    grad_fn = _jax.value_and_grad(_loss, argnums=(0, 1))

    def one_microbatch(ex, loss_target):
        ex = dict(ex)
        diff = ex.pop(TWIN_DIFF_INPUT)
        return grad_fn(weights, diff, {**shared, **ex}, loss_target)

    if N_MICROBATCH == 1:
        loss, (grad_w, grad_x) = one_microbatch(per_example, given["loss_target"])
    else:
        def body(carry, xs):
            loss_sum, grad_sum = carry
            l_k, (gw_k, gx_k) = one_microbatch(xs[0], xs[1])
            with _jax.named_scope("update"):
                return (loss_sum + l_k, _jax.tree.map(_jnp.add, grad_sum, gw_k)), gx_k

        init = (_jnp.zeros((), _jnp.float32), _jax.tree.map(_jnp.zeros_like, weights))
        (loss, grad_w), grad_x = _jax.lax.scan(body, init, (per_example, given["loss_target"]))
    with _jax.named_scope("update"):
        delta_w, new_m, new_v = {}, {}, {}
        for n in TWIN_WEIGHTS:
            delta_w[n], new_m[n], new_v[n] = _adamw(weights[n], grad_w[n], given["m_" + n], given["v_" + n])
    return (loss, grad_x, *[grad_w[n] for n in TWIN_WEIGHTS], *[delta_w[n] for n in TWIN_WEIGHTS],
            *[new_m[n] for n in TWIN_WEIGHTS], *[new_v[n] for n in TWIN_WEIGHTS])
```

```python
import functools
import math

import numpy as np
import jax
import jax.numpy as jnp
from jax import lax
from jax.experimental import pallas as pl
from jax.experimental.pallas import tpu as pltpu

F32 = jnp.float32
BF16 = jnp.bfloat16
MXU = BF16

D_MODEL = 1024
HEAD_DIM = 64
N_HEADS = 8
D_ATTN = 512
D_GMLP = 512
D_IN = 2560
DILATIONS = (1, 4, 16)
BLK = 128
ROPE_THETA = 500000.0
ROPE_DIM = 16
D_FF = 2816
D_PLE = 256
LN_EPS = 1e-5
ALPHA = 2.0 ** 0.25
NEG_INF = -1e30
N_SHARD = 4
W_IN_BLK = D_IN // N_SHARD
FF_BLK = D_FF // N_SHARD
ROW_BLK = D_MODEL // N_SHARD
ADAM_LR, ADAM_B1, ADAM_B2, ADAM_EPS, ADAM_WD, ADAM_STEP = 0.001, 0.9, 0.999, 1e-08, 0.01, 10

TM = 512
HALO = 8
VMEM_LIMIT = 56 * 1024 * 1024


def _cp(**kw):
    return pltpu.CompilerParams(vmem_limit_bytes=VMEM_LIMIT, **kw)


def _full(shape):
    n = len(shape)
    return pl.BlockSpec(shape, lambda *_: (0,) * n)


def _gelu(x):
    return 0.5 * x * (1.0 + lax.erf(x * (1.0 / math.sqrt(2.0))))


def _gelu_grad(x):
    return 0.5 * (1.0 + lax.erf(x * (1.0 / math.sqrt(2.0)))) + x * jnp.exp(-0.5 * x * x) * (1.0 / math.sqrt(2.0 * math.pi))


def _ln_fwd(r):
    mu = jnp.mean(r, axis=-1, keepdims=True)
    xc = r - mu
    var = jnp.mean(xc * xc, axis=-1, keepdims=True)
    rstd = lax.rsqrt(var + LN_EPS)
    return xc * rstd, rstd


def _ln_bwd(dy, xhat, rstd, g):
    dxh = dy * g
    m1 = jnp.mean(dxh, axis=-1, keepdims=True)
    m2 = jnp.mean(dxh * xhat, axis=-1, keepdims=True)
    return rstd * (dxh - m1 - xhat * m2)


def _dot(a, b):
    return jnp.dot(a.astype(MXU), b.astype(MXU), preferred_element_type=F32)


def _dot_nt(a, b):
    return lax.dot_general(a.astype(MXU), b.astype(MXU), (((1,), (1,)), ((), ())), preferred_element_type=F32)


def _dot_tn(a, b):
    return lax.dot_general(a.astype(MXU), b.astype(MXU), (((0,), (0,)), ((), ())), preferred_element_type=F32)


def _colsum(v):
    return jnp.sum(v, axis=0, keepdims=True)


def _rope_tables(positions, t):
    inv = np.float32(ROPE_THETA) ** (-np.arange(0, ROPE_DIM, 2, dtype=np.float32) / np.float32(ROPE_DIM))
    half = ROPE_DIM // 2
    pos_rep = jnp.repeat(positions.reshape(t // 16, 16), half, axis=1)
    inv_row = jnp.asarray(np.tile(inv, 16)[None, :], F32)

    def trig_body(pos_ref, inv_ref, cos_ref, sin_ref):
        ang = pos_ref[...].astype(F32) * inv_ref[...]
        cos_ref[...] = jnp.cos(ang)
        sin_ref[...] = jnp.sin(ang)

    cos8, sin8 = pl.pallas_call(
        trig_body, name="rope_trig",
        out_shape=(jax.ShapeDtypeStruct((t // 16, 128), F32), jax.ShapeDtypeStruct((t // 16, 128), F32)),
    )(pos_rep, inv_row)
    cos8 = cos8.reshape(t, half)
    sin8 = sin8.reshape(t, half)

    lane = np.arange(128) % HEAD_DIM
    sel = (np.arange(half)[:, None] == (lane % half)[None, :])
    e_cos = (sel & (lane < ROPE_DIM)[None, :]).astype(np.float32)
    e_s1 = -(sel & (lane < half)[None, :]).astype(np.float32)
    e_s2 = (sel & ((lane >= half) & (lane < ROPE_DIM))[None, :]).astype(np.float32)
    ones = (lane >= ROPE_DIM).astype(np.float32)[None, :]

    def expand_body(cos_ref, sin_ref, ec_ref, e1_ref, e2_ref, ones_ref, c_ref, s1_ref, s2_ref):
        hp = lax.Precision.HIGHEST
        c_ref[...] = jnp.dot(cos_ref[...], ec_ref[...], precision=hp, preferred_element_type=F32) + ones_ref[...]
        s1_ref[...] = jnp.dot(sin_ref[...], e1_ref[...], precision=hp, preferred_element_type=F32)
        s2_ref[...] = jnp.dot(sin_ref[...], e2_ref[...], precision=hp, preferred_element_type=F32)

    tab = jax.ShapeDtypeStruct((t, 128), F32)
    return pl.pallas_call(expand_body, name="rope_expand", out_shape=(tab, tab, tab), compiler_params=_cp())(
        cos8, sin8, jnp.asarray(e_cos), jnp.asarray(e_s1), jnp.asarray(e_s2), jnp.asarray(ones))


def _tile_heads(tab):
    return jnp.concatenate([tab] * (D_ATTN // 128), axis=1)


def _rope_apply(v, c, s1, s2):
    n = v.shape[1]
    half = ROPE_DIM // 2
    return v * c + pltpu.roll(v, n - half, 1) * s1 + pltpu.roll(v, half, 1) * s2


def _rope_apply_t(g, c, s1, s2):
    n = g.shape[1]
    half = ROPE_DIM // 2
    return g * c + pltpu.roll(g * s1, half, 1) + pltpu.roll(g * s2, n - half, 1)


def _qkvuz(x, w_in, c_tab, s1_tab, s2_tab, ln_z_g, ln_z_b, w_s, b_full):
    t = x.shape[0]
    nchunk = TM // BLK

    def body(x_ref, w_ref, c_ref, s1_ref, s2_ref, g_ref, b_ref, ws_ref, bf_ref,
             q_ref, k_ref, v_ref, hu_ref, hz_ref, mixed_ref, gm_ref, h_scr, wm_scr):
        @pl.when(pl.program_id(0) == 0)
        def _():
            row = lax.broadcasted_iota(jnp.int32, (BLK, BLK), 0)
            col = lax.broadcasted_iota(jnp.int32, (BLK, BLK), 1)
            for g in range(N_HEADS):
                wm_scr[g] = jnp.where(col <= row, ws_ref[g], 0.0).astype(MXU)

        xb = x_ref[...].astype(MXU)
        for j in range(N_SHARD):
            h_scr[:, j * W_IN_BLK:(j + 1) * W_IN_BLK] = jnp.dot(xb, w_ref[j], preferred_element_type=F32)
        c, s1, s2 = _tile_heads(c_ref[...]), _tile_heads(s1_ref[...]), _tile_heads(s2_ref[...])
        q = _rope_apply(h_scr[:, 0:D_ATTN], c, s1, s2)
        q_ref[...] = (q * (1.0 / math.sqrt(HEAD_DIM))).astype(MXU)
        k_ref[...] = _rope_apply(h_scr[:, D_ATTN:2 * D_ATTN], c, s1, s2).astype(MXU)
        v_ref[...] = h_scr[:, 2 * D_ATTN:3 * D_ATTN].astype(MXU)
        hu = h_scr[:, 3 * D_ATTN:3 * D_ATTN + D_GMLP]
        hz = h_scr[:, 3 * D_ATTN + D_GMLP:]
        hu_ref[...] = hu
        hz_ref[...] = hz
        zhat, _ = _ln_fwd(_gelu(hz))
        zn = (zhat * g_ref[...] + b_ref[...]).astype(MXU)
        for ch in range(nchunk):
            rows = slice(ch * BLK, (ch + 1) * BLK)
            for g in range(N_HEADS):
                cols = slice(g * HEAD_DIM, (g + 1) * HEAD_DIM)
                mixed_ref[rows, cols] = jnp.dot(wm_scr[g], zn[rows, cols], preferred_element_type=F32) + bf_ref[:, cols]
        gm_ref[...] = (_gelu(hu) * mixed_ref[...]).astype(MXU)

    tok = lambda w: pl.BlockSpec((TM, w), lambda i: (i, 0))
    outs = [jax.ShapeDtypeStruct((t, D_ATTN), MXU)] * 3 + [jax.ShapeDtypeStruct((t, D_GMLP), F32)] * 3 + [
        jax.ShapeDtypeStruct((t, D_GMLP), MXU)]
    return pl.pallas_call(
        body, name="qkvuz", grid=(t // TM,),
        in_specs=[tok(D_MODEL), _full(w_in.shape), tok(128), tok(128), tok(128), _full(ln_z_g.shape), _full(ln_z_b.shape),
                  _full(w_s.shape), _full(b_full.shape)],
        out_specs=[tok(D_ATTN)] * 7, out_shape=outs,
        scratch_shapes=[pltpu.VMEM((TM, D_IN), F32), pltpu.VMEM((N_HEADS, BLK, BLK), MXU)],
        compiler_params=_cp(dimension_semantics=("arbitrary",)),
    )(x, w_in, c_tab, s1_tab, s2_tab, ln_z_g, ln_z_b, w_s, b_full)


def _attn_view(a, d):
    return a.reshape(a.shape[0] // d, d * a.shape[1])


def _band_mask(n_q, n_k, q_off):
    i = lax.broadcasted_iota(jnp.int32, (n_q, n_k), 0) + q_off
    j = lax.broadcasted_iota(jnp.int32, (n_q, n_k), 1)
    return i, j


def _attn_fwd(q, k, v, d):
    t = q.shape[0]
    nb = t // d // BLK
    qv, kv, vv = _attn_view(q, d), _attn_view(k, d), _attn_view(v, d)

    def body(q_ref, kp_ref, kc_ref, vp_ref, vc_ref, o_ref, l_ref):
        n = pl.program_id(1)
        i, j = _band_mask(BLK, 2 * BLK, 0)
        valid = (j >= i) & (j <= i + BLK) & ((j >= BLK) | (n > 0))
        kcat = jnp.concatenate([kp_ref[...], kc_ref[...]], axis=0)
        vcat = jnp.concatenate([vp_ref[...], vc_ref[...]], axis=0)
        for h in range(N_HEADS):
            cols = slice(h * HEAD_DIM, (h + 1) * HEAD_DIM)
            s = jnp.where(valid, _dot_nt(q_ref[:, cols], kcat[:, cols]), NEG_INF)
            m = jnp.max(s, axis=-1, keepdims=True)
            e = jnp.exp(s - m)
            den = jnp.sum(e, axis=-1, keepdims=True)
            o_ref[:, cols] = _dot(e * (1.0 / den), vcat[:, cols])
            l_ref[:, cols] = jnp.broadcast_to(m + jnp.log(den), (BLK, HEAD_DIM))

    cur = pl.BlockSpec((BLK, D_ATTN), lambda r, n: (n, r))
    prev = pl.BlockSpec((BLK, D_ATTN), lambda r, n: (jnp.maximum(n - 1, 0), r))
    o, l = pl.pallas_call(
        body, name=f"attn_fwd_d{d}", grid=(d, nb), in_specs=[cur, prev, cur, prev, cur], out_specs=[cur, cur],
        out_shape=[jax.ShapeDtypeStruct(qv.shape, F32)] * 2,
        compiler_params=_cp(dimension_semantics=("arbitrary", "arbitrary")),
    )(qv, kv, kv, vv, vv)
    return o.reshape(t, D_ATTN), l.reshape(t, D_ATTN)


def _attn_bwd_dq(q, k, v, do, o, lse, d):
    t = q.shape[0]
    nb = t // d // BLK
    qv, kv, vv, dov, ov, lv = (_attn_view(a, d) for a in (q, k, v, do, o, lse))

    def body(q_ref, kp_ref, kc_ref, vp_ref, vc_ref, do_ref, o_ref, l_ref, dq_ref):
        n = pl.program_id(1)
        i, j = _band_mask(BLK, 2 * BLK, 0)
        valid = (j >= i) & (j <= i + BLK) & ((j >= BLK) | (n > 0))
        kcat = jnp.concatenate([kp_ref[...], kc_ref[...]], axis=0)
        vcat = jnp.concatenate([vp_ref[...], vc_ref[...]], axis=0)
        for h in range(N_HEADS):
            cols = slice(h * HEAD_DIM, (h + 1) * HEAD_DIM)
            s = _dot_nt(q_ref[:, cols], kcat[:, cols])
            p = jnp.where(valid, jnp.exp(s - l_ref[:, h * HEAD_DIM:h * HEAD_DIM + 1]), 0.0)
            doh = do_ref[:, cols]
            delta = jnp.sum(doh * o_ref[:, cols], axis=-1, keepdims=True)
            ds = p * (_dot_nt(doh, vcat[:, cols]) - delta)
            dq_ref[:, cols] = _dot(ds, kcat[:, cols])

    cur = pl.BlockSpec((BLK, D_ATTN), lambda r, n: (n, r))
    prev = pl.BlockSpec((BLK, D_ATTN), lambda r, n: (jnp.maximum(n - 1, 0), r))
    dq = pl.pallas_call(
        body, name=f"attn_dq_d{d}", grid=(d, nb), in_specs=[cur, prev, cur, prev, cur, cur, cur, cur], out_specs=cur,
        out_shape=jax.ShapeDtypeStruct(qv.shape, F32),
        compiler_params=_cp(dimension_semantics=("arbitrary", "arbitrary")),
    )(qv, kv, kv, vv, vv, dov, ov, lv)
    return dq.reshape(t, D_ATTN)


def _attn_bwd_dkv(q, k, v, do, o, lse, d):
    t = q.shape[0]
    nb = t // d // BLK
    qv, kv, vv, dov, ov, lv = (_attn_view(a, d) for a in (q, k, v, do, o, lse))

    def body(k_ref, v_ref, qc_ref, qn_ref, doc_ref, don_ref, oc_ref, on_ref, lc_ref, ln_ref, dk_ref, dv_ref):
        n = pl.program_id(1)
        ii = lax.broadcasted_iota(jnp.int32, (2 * BLK, BLK), 0)
        j = lax.broadcasted_iota(jnp.int32, (2 * BLK, BLK), 1)
        valid = (ii >= j) & (ii <= j + BLK) & ((ii < BLK) | (n < nb - 1))
        qcat = jnp.concatenate([qc_ref[...], qn_ref[...]], axis=0)
        docat = jnp.concatenate([doc_ref[...], don_ref[...]], axis=0)
        ocat = jnp.concatenate([oc_ref[...], on_ref[...]], axis=0)
        lcat = jnp.concatenate([lc_ref[...], ln_ref[...]], axis=0)
        for h in range(N_HEADS):
            cols = slice(h * HEAD_DIM, (h + 1) * HEAD_DIM)
            s = _dot_nt(qcat[:, cols], k_ref[:, cols])
            p = jnp.where(valid, jnp.exp(s - lcat[:, h * HEAD_DIM:h * HEAD_DIM + 1]), 0.0)
            doh = docat[:, cols]
            delta = jnp.sum(doh * ocat[:, cols], axis=-1, keepdims=True)
            ds = p * (_dot_nt(doh, v_ref[:, cols]) - delta)
            dv_ref[:, cols] = _dot_tn(p, doh)
            dk_ref[:, cols] = _dot_tn(ds, qcat[:, cols])

    cur = pl.BlockSpec((BLK, D_ATTN), lambda r, n: (n, r))
    nxt = pl.BlockSpec((BLK, D_ATTN), lambda r, n: (jnp.minimum(n + 1, nb - 1), r))
    dk, dv = pl.pallas_call(
        body, name=f"attn_dkv_d{d}", grid=(d, nb), in_specs=[cur, cur, cur, nxt, cur, nxt, cur, nxt, cur, nxt],
        out_specs=[cur, cur], out_shape=[jax.ShapeDtypeStruct(qv.shape, F32)] * 2,
        compiler_params=_cp(dimension_semantics=("arbitrary", "arbitrary")),
    )(kv, vv, qv, qv, dov, dov, ov, ov, lv, lv)
    return dk.reshape(t, D_ATTN), dv.reshape(t, D_ATTN)


def _mix_ln1(os_, ls_, gm, x, w_o, ln1_g, ln1_b):
    t = x.shape[0]

    def body(o1, o2, o3, l1, l2, l3, gm_ref, x_ref, wo_ref, g_ref, b_ref,
             attn_ref, lse_ref, cat_ref, xhat_ref, rstd_ref, x1b_ref):
        la, lb, lc = l1[...], l2[...], l3[...]
        m = jnp.maximum(jnp.maximum(la, lb), lc)
        ea, eb, ec = jnp.exp(la - m), jnp.exp(lb - m), jnp.exp(lc - m)
        den = ea + eb + ec
        attn = (ea * o1[...] + eb * o2[...] + ec * o3[...]) / den
        attn_ref[...] = attn
        lse_ref[...] = m + jnp.log(den)
        cat_ref[:, 0:D_ATTN] = attn.astype(MXU)
        cat_ref[:, D_ATTN:] = gm_ref[...]
        mix = jnp.dot(cat_ref[...], wo_ref[...], preferred_element_type=F32)
        xhat, rstd = _ln_fwd(ALPHA * x_ref[...] + mix)
        xhat_ref[...] = xhat
        rstd_ref[...] = rstd
        x1b_ref[...] = (xhat * g_ref[...] + b_ref[...]).astype(MXU)

    tok = lambda w: pl.BlockSpec((TM, w), lambda i: (i, 0))
    outs = [jax.ShapeDtypeStruct((t, D_ATTN), F32)] * 2 + [
        jax.ShapeDtypeStruct((t, D_MODEL), MXU), jax.ShapeDtypeStruct((t, D_MODEL), F32), jax.ShapeDtypeStruct((t, 1), F32),
        jax.ShapeDtypeStruct((t, D_MODEL), MXU)]
    return pl.pallas_call(
        body, name="mix_ln1", grid=(t // TM,),
        in_specs=[tok(D_ATTN)] * 7 + [tok(D_MODEL), _full(w_o.shape), _full(ln1_g.shape), _full(ln1_b.shape)],
        out_specs=[tok(D_ATTN), tok(D_ATTN), tok(D_MODEL), tok(D_MODEL), tok(1), tok(D_MODEL)], out_shape=outs,
        compiler_params=_cp(dimension_semantics=("arbitrary",)),
    )(*os_, *ls_, gm, x, w_o, ln1_g, ln1_b)


def _conv_fwd(a_ext, w_ref, b_ref, rows):
    return (b_ref[...] + w_ref[2:3, :] * a_ext[HALO:HALO + rows] + w_ref[1:2, :] * a_ext[HALO - 1:HALO - 1 + rows]
            + w_ref[0:1, :] * a_ext[HALO - 2:HALO - 2 + rows])


def _ffn_in(x1b, w_a, w_b, conv_w, conv_b):
    t = x1b.shape[0]
    hb = TM // HALO

    def body(x_ref, xh_ref, wa_ref, wb_ref, cw_ref, cb_ref, apre_ref, b_ref, f_ref):
        i = pl.program_id(1)
        a_pre = jnp.dot(x_ref[...], wa_ref[...], preferred_element_type=F32)
        a_halo = jnp.dot(xh_ref[...], wa_ref[...], preferred_element_type=F32)
        a_halo = jnp.where(i > 0, a_halo, 0.0)
        a = _conv_fwd(jnp.concatenate([a_halo, a_pre], axis=0), cw_ref, cb_ref, TM)
        b = jnp.dot(x_ref[...], wb_ref[...], preferred_element_type=F32)
        apre_ref[...] = a_pre
        b_ref[...] = b
        f_ref[...] = (_gelu(a) * b).astype(MXU)

    blk = lambda r, c: pl.BlockSpec((None, r, c), lambda j, i: (j, 0, 0))
    tokj = pl.BlockSpec((None, TM, FF_BLK), lambda j, i: (j, i, 0))
    outs = [jax.ShapeDtypeStruct((N_SHARD, t, FF_BLK), F32)] * 2 + [jax.ShapeDtypeStruct((N_SHARD, t, FF_BLK), MXU)]
    return pl.pallas_call(
        body, name="ffn_in", grid=(N_SHARD, t // TM),
        in_specs=[pl.BlockSpec((TM, D_MODEL), lambda j, i: (i, 0)),
                  pl.BlockSpec((HALO, D_MODEL), lambda j, i: (jnp.maximum(i * hb - 1, 0), 0)),
                  blk(D_MODEL, FF_BLK), blk(D_MODEL, FF_BLK), blk(3, FF_BLK), blk(1, FF_BLK)],
        out_specs=[tokj, tokj, tokj], out_shape=outs,
        compiler_params=_cp(dimension_semantics=("arbitrary", "arbitrary")),
    )(x1b, x1b, w_a, w_b, conv_w, conv_b)


def _ffn_out_ln2(f, w_down, xhat1, ln1_g, ln1_b, ln2_g, ln2_b):
    t = xhat1.shape[0]

    def body(f_ref, wd_ref, xh_ref, g1_ref, b1_ref, g2_ref, b2_ref, xhat_ref, rstd_ref, x2b_ref):
        ff = jnp.dot(f_ref[0], wd_ref[0], preferred_element_type=F32)
        for j in range(1, N_SHARD):
            ff = ff + jnp.dot(f_ref[j], wd_ref[j], preferred_element_type=F32)
        x1 = xh_ref[...] * g1_ref[...] + b1_ref[...]
        xhat, rstd = _ln_fwd(ALPHA * x1 + ff)
        xhat_ref[...] = xhat
        rstd_ref[...] = rstd
        x2b_ref[...] = (xhat * g2_ref[...] + b2_ref[...]).astype(MXU)

    tok = lambda w: pl.BlockSpec((TM, w), lambda i: (i, 0))
    vec = _full((1, D_MODEL))
    outs = [jax.ShapeDtypeStruct((t, D_MODEL), F32), jax.ShapeDtypeStruct((t, 1), F32), jax.ShapeDtypeStruct((t, D_MODEL), MXU)]
    return pl.pallas_call(
        body, name="ffn_out_ln2", grid=(t // TM,),
        in_specs=[pl.BlockSpec((N_SHARD, TM, FF_BLK), lambda i: (0, i, 0)), _full(w_down.shape), tok(D_MODEL), vec, vec, vec, vec],
        out_specs=[tok(D_MODEL), tok(1), tok(D_MODEL)], out_shape=outs,
        compiler_params=_cp(dimension_semantics=("arbitrary",)),
    )(f, w_down, xhat1, ln1_g, ln1_b, ln2_g, ln2_b)


STAT_ROWS = 8


def _ple_loss_bwd(xhat2, rstd2, p, target, ln2_g, ln2_b, w_g, b_g, w_p, ln3_g, ln3_b):
    t = xhat2.shape[0]

    def body(xh2_ref, rs2_ref, p_ref, t_ref, g2_ref, b2_ref, wg_ref, bg_ref, wp_ref, g3_ref, b3_ref,
             dr2_ref, dgp_ref, dpp_ref, stat_ref, pp_scr):
        @pl.when(pl.program_id(0) == 0)
        def _():
            stat_ref[...] = jnp.zeros_like(stat_ref)

        xhat2 = xh2_ref[...]
        x2 = xhat2 * g2_ref[...] + b2_ref[...]
        gate = jax.nn.sigmoid(jnp.dot(x2.astype(MXU), wg_ref[...], preferred_element_type=F32) + bg_ref[...])
        pb = p_ref[...].astype(MXU)
        for j in range(N_SHARD):
            pp_scr[:, j * ROW_BLK:(j + 1) * ROW_BLK] = jnp.dot(pb, wp_ref[j], preferred_element_type=F32)
        pp = pp_scr[...]
        xhat3, rstd3 = _ln_fwd(ALPHA * x2 + gate * pp)
        err = xhat3 * g3_ref[...] + b3_ref[...] - t_ref[...]
        dy = err * (1.0 / D_MODEL)
        dr3 = _ln_bwd(dy, xhat3, rstd3, g3_ref[...])
        dgp = dr3 * pp * gate * (1.0 - gate)
        dgp_ref[...] = dgp.astype(MXU)
        dpp_ref[...] = (dr3 * gate).astype(MXU)
        dx2 = ALPHA * dr3 + _dot_nt(dgp, wg_ref[...])
        dr2_ref[...] = _ln_bwd(dx2, xhat2, rs2_ref[...], g2_ref[...])
        stat_ref[0:1, :] += _colsum(dy * xhat3)
        stat_ref[1:2, :] += _colsum(dy)
        stat_ref[2:3, :] += _colsum(dgp)
        stat_ref[3:4, :] += _colsum(dx2 * xhat2)
        stat_ref[4:5, :] += _colsum(dx2)
        stat_ref[5:6, :] += _colsum(err * err)

    tok = lambda w: pl.BlockSpec((TM, w), lambda i: (i, 0))
    vec = _full((1, D_MODEL))
    outs = [jax.ShapeDtypeStruct((t, D_MODEL), F32), jax.ShapeDtypeStruct((t, D_MODEL), MXU), jax.ShapeDtypeStruct((t, D_MODEL), MXU),
            jax.ShapeDtypeStruct((STAT_ROWS, D_MODEL), F32)]
    return pl.pallas_call(
        body, name="ple_loss_bwd", grid=(t // TM,),
        in_specs=[tok(D_MODEL), tok(1), tok(D_PLE), tok(D_MODEL), vec, vec, _full(w_g.shape), vec, _full(w_p.shape), vec, vec],
        out_specs=[tok(D_MODEL), tok(D_MODEL), tok(D_MODEL), _full((STAT_ROWS, D_MODEL))], out_shape=outs,
        scratch_shapes=[pltpu.VMEM((TM, D_MODEL), F32)],
        compiler_params=_cp(dimension_semantics=("arbitrary",)),
    )(xhat2, rstd2, p, target, ln2_g, ln2_b, w_g, b_g, w_p, ln3_g, ln3_b)


def _ffn_bwd(dr2, a_pre, b, w_down, w_a, w_b, conv_w, conv_b, xhat1, rstd1, ln1_g):
    t = dr2.shape[0]
    nt = t // TM
    hb = TM // HALO
    last_h = t // HALO - 1

    def body(dr_ref, drn_ref, ap_ref, app_ref, apn_ref, b_ref, bn_ref, wd_ref, wa_ref, wb_ref, cw_ref, cb_ref,
             xh_ref, rs_ref, g1_ref, dap_ref, dbb_ref, dr1_ref, cstat_ref, lstat_ref, acc_scr):
        i, j = pl.program_id(0), pl.program_id(1)

        @pl.when((i == 0) & (j == 0))
        def _():
            cstat_ref[...] = jnp.zeros_like(cstat_ref)
            lstat_ref[...] = jnp.zeros_like(lstat_ref)

        ext = TM + HALO
        dr_ext = jnp.concatenate([dr_ref[...], drn_ref[...]], axis=0)
        df = _dot_nt(dr_ext, wd_ref[...])
        a_all = jnp.concatenate([jnp.where(i > 0, app_ref[...], 0.0), ap_ref[...], apn_ref[...]], axis=0)
        a = _conv_fwd(a_all, cw_ref, cb_ref, ext)
        b_ext = jnp.concatenate([b_ref[...], bn_ref[...]], axis=0)
        row = lax.broadcasted_iota(jnp.int32, (ext, 1), 0)
        da = jnp.where((row < TM) | (i < nt - 1), df * b_ext * _gelu_grad(a), 0.0)
        dbb = df[0:TM] * _gelu(a[0:TM])
        da_pre = cw_ref[2:3, :] * da[0:TM] + cw_ref[1:2, :] * da[1:TM + 1] + cw_ref[0:1, :] * da[2:TM + 2]
        dap_ref[...] = da_pre.astype(MXU)
        dbb_ref[...] = dbb.astype(MXU)
        da_m = da[0:TM]
        for kk in range(3):
            cstat_ref[j, kk:kk + 1, :] += _colsum(da_m * a_all[HALO - 2 + kk:HALO - 2 + kk + TM])
        cstat_ref[j, 3:4, :] += _colsum(da_m)
        part = _dot_nt(da_pre, wa_ref[...]) + _dot_nt(dbb, wb_ref[...])

        @pl.when(j == 0)
        def _():
            acc_scr[...] = ALPHA * dr_ref[...] + part

        @pl.when(j > 0)
        def _():
            acc_scr[...] += part

        @pl.when(j == N_SHARD - 1)
        def _():
            dx1 = acc_scr[...]
            xhat1 = xh_ref[...]
            lstat_ref[0:1, :] += _colsum(dx1 * xhat1)
            lstat_ref[1:2, :] += _colsum(dx1)
            dr1_ref[...] = _ln_bwd(dx1, xhat1, rs_ref[...], g1_ref[...])

    tok = lambda w: pl.BlockSpec((TM, w), lambda i, j: (i, 0))
    tokj = pl.BlockSpec((None, TM, FF_BLK), lambda i, j: (j, i, 0))
    prevj = pl.BlockSpec((None, HALO, FF_BLK), lambda i, j: (j, jnp.maximum(i * hb - 1, 0), 0))
    nextj = pl.BlockSpec((None, HALO, FF_BLK), lambda i, j: (j, jnp.minimum((i + 1) * hb, last_h), 0))
    blk = lambda r, c: pl.BlockSpec((None, r, c), lambda i, j: (j, 0, 0))
    outs = [jax.ShapeDtypeStruct((N_SHARD, t, FF_BLK), MXU)] * 2 + [
        jax.ShapeDtypeStruct((t, D_MODEL), F32), jax.ShapeDtypeStruct((N_SHARD, STAT_ROWS, FF_BLK), F32),
        jax.ShapeDtypeStruct((STAT_ROWS, D_MODEL), F32)]
    return pl.pallas_call(
        body, name="ffn_bwd", grid=(nt, N_SHARD),
        in_specs=[tok(D_MODEL), pl.BlockSpec((HALO, D_MODEL), lambda i, j: (jnp.minimum((i + 1) * hb, last_h), 0)),
                  tokj, prevj, nextj, tokj, nextj, blk(FF_BLK, D_MODEL), blk(D_MODEL, FF_BLK), blk(D_MODEL, FF_BLK),
                  blk(3, FF_BLK), blk(1, FF_BLK), tok(D_MODEL), tok(1), _full((1, D_MODEL))],
        out_specs=[tokj, tokj, tok(D_MODEL), _full((N_SHARD, STAT_ROWS, FF_BLK)), _full((STAT_ROWS, D_MODEL))], out_shape=outs,
        scratch_shapes=[pltpu.VMEM((TM, D_MODEL), F32)],
        compiler_params=_cp(dimension_semantics=("arbitrary", "arbitrary")),
    )(dr2, dr2, a_pre, a_pre, a_pre, b, b, w_down, w_a, w_b, conv_w, conv_b, xhat1, rstd1, ln1_g)


def _mix_bwd(dr1, w_o, hu, hz, mixed, ln_z_g, ln_z_b, w_s):
    t = dr1.shape[0]
    nchunk = TM // BLK

    def body(dr_ref, wo_ref, hu_ref, hz_ref, mx_ref, g_ref, b_ref, ws_ref, grp_ref,
             dattn_ref, duz_ref, dws_ref, dbs_ref, zstat_ref, wm_scr, dzn_scr, dbsum_scr):
        @pl.when(pl.program_id(0) == 0)
        def _():
            row = lax.broadcasted_iota(jnp.int32, (BLK, BLK), 0)
            col = lax.broadcasted_iota(jnp.int32, (BLK, BLK), 1)
            for g in range(N_HEADS):
                wm_scr[g] = jnp.where(col <= row, ws_ref[g], 0.0).astype(MXU)
            dws_ref[...] = jnp.zeros_like(dws_ref)
            dbsum_scr[...] = jnp.zeros_like(dbsum_scr)
            zstat_ref[...] = jnp.zeros_like(zstat_ref)

        dcat = _dot_nt(dr_ref[...], wo_ref[...])
        dattn_ref[...] = dcat[:, 0:D_ATTN]
        dgm = dcat[:, D_ATTN:]
        hu, hz = hu_ref[...], hz_ref[...]
        u = _gelu(hu)
        duz_ref[:, 0:D_GMLP] = (dgm * mx_ref[...] * _gelu_grad(hu)).astype(MXU)
        dmixed = dgm * u
        dmb = dmixed.astype(MXU)
        zhat, rstd = _ln_fwd(_gelu(hz))
        znb = (zhat * g_ref[...] + b_ref[...]).astype(MXU)
        dbs_acc = jnp.zeros((BLK, D_GMLP), F32)
        for ch in range(nchunk):
            rows = slice(ch * BLK, (ch + 1) * BLK)
            dbs_acc = dbs_acc + dmixed[rows]
            for g in range(N_HEADS):
                cols = slice(g * HEAD_DIM, (g + 1) * HEAD_DIM)
                dzn_scr[rows, cols] = _dot_tn(wm_scr[g], dmb[rows, cols])
                dws_ref[g] += _dot_nt(dmb[rows, cols], znb[rows, cols])
        dbsum_scr[...] += dbs_acc
        dzn = dzn_scr[...]
        zstat_ref[0:1, :] += _colsum(dzn * zhat)
        zstat_ref[1:2, :] += _colsum(dzn)
        duz_ref[:, D_GMLP:] = (_ln_bwd(dzn, zhat, rstd, g_ref[...]) * _gelu_grad(hz)).astype(MXU)

        @pl.when(pl.program_id(0) == nt - 1)
        def _():
            row = lax.broadcasted_iota(jnp.int32, (BLK, BLK), 0)
            col = lax.broadcasted_iota(jnp.int32, (BLK, BLK), 1)
            for g in range(N_HEADS):
                dws_ref[g] = jnp.where(col <= row, dws_ref[g], 0.0)
            dbs_ref[...] = lax.dot_general(grp_ref[...], dbsum_scr[...], (((1,), (1,)), ((), ())),
                                           precision=lax.Precision.HIGHEST, preferred_element_type=F32)

    nt = t // TM
    tok = lambda w: pl.BlockSpec((TM, w), lambda i: (i, 0))
    grp = jnp.asarray((np.arange(D_GMLP)[None, :] // HEAD_DIM == np.arange(N_HEADS)[:, None]).astype(np.float32))
    outs = [jax.ShapeDtypeStruct((t, D_ATTN), F32), jax.ShapeDtypeStruct((t, 2 * D_GMLP), MXU),
            jax.ShapeDtypeStruct((N_HEADS, BLK, BLK), F32), jax.ShapeDtypeStruct((N_HEADS, BLK), F32),
            jax.ShapeDtypeStruct((STAT_ROWS, D_GMLP), F32)]
    return pl.pallas_call(
        body, name="mix_bwd", grid=(t // TM,),
        in_specs=[tok(D_MODEL), _full(w_o.shape), tok(D_GMLP), tok(D_GMLP), tok(D_GMLP), _full(ln_z_g.shape), _full(ln_z_b.shape),
                  _full(w_s.shape), _full(grp.shape)],
        out_specs=[tok(D_ATTN), tok(2 * D_GMLP), _full((N_HEADS, BLK, BLK)), _full((N_HEADS, BLK)), _full((STAT_ROWS, D_GMLP))],
        out_shape=outs,
        scratch_shapes=[pltpu.VMEM((N_HEADS, BLK, BLK), MXU), pltpu.VMEM((TM, D_GMLP), F32), pltpu.VMEM((BLK, D_GMLP), F32)],
        compiler_params=_cp(dimension_semantics=("arbitrary",)),
    )(dr1, w_o, hu, hz, mixed, ln_z_g, ln_z_b, w_s, grp)


def _dx_in(dqs, dks, dvs, duz, dr1, w_in, c_tab, s1_tab, s2_tab):
    t = dr1.shape[0]

    def body(dq1, dq2, dq3, dk1, dk2, dk3, dv1, dv2, dv3, duz_ref, dr_ref, w_ref, c_ref, s1_ref, s2_ref, dh_ref, dx_ref):
        c, s1, s2 = _tile_heads(c_ref[...]), _tile_heads(s1_ref[...]), _tile_heads(s2_ref[...])
        dq = (dq1[...] + dq2[...] + dq3[...]) * (1.0 / math.sqrt(HEAD_DIM))
        dh_ref[:, 0:D_ATTN] = _rope_apply_t(dq, c, s1, s2).astype(MXU)
        dh_ref[:, D_ATTN:2 * D_ATTN] = _rope_apply_t(dk1[...] + dk2[...] + dk3[...], c, s1, s2).astype(MXU)
        dh_ref[:, 2 * D_ATTN:3 * D_ATTN] = (dv1[...] + dv2[...] + dv3[...]).astype(MXU)
        dh_ref[:, 3 * D_ATTN:] = duz_ref[...]
        dx = ALPHA * dr_ref[...]
        for j in range(N_SHARD):
            dx = dx + _dot_nt(dh_ref[:, j * W_IN_BLK:(j + 1) * W_IN_BLK], w_ref[j])
        dx_ref[...] = dx

    tok = lambda w: pl.BlockSpec((TM, w), lambda i: (i, 0))
    outs = [jax.ShapeDtypeStruct((t, D_IN), MXU), jax.ShapeDtypeStruct((t, D_MODEL), F32)]
    return pl.pallas_call(
        body, name="dx_in", grid=(t // TM,),
        in_specs=[tok(D_ATTN)] * 9 + [tok(2 * D_GMLP), tok(D_MODEL), _full(w_in.shape), tok(128), tok(128), tok(128)],
        out_specs=[tok(D_IN), tok(D_MODEL)], out_shape=outs,
        compiler_params=_cp(dimension_semantics=("arbitrary",)),
    )(*dqs, *dks, *dvs, duz, dr1, w_in, c_tab, s1_tab, s2_tab)


def _wgrad(name, x, dy, x_spec, dy_spec, out_spec, out_shape, grid):
    def body(x_ref, dy_ref, o_ref):
        o_ref[...] = _dot_tn(x_ref[...], dy_ref[...])

    return pl.pallas_call(
        body, name=name, grid=grid, in_specs=[x_spec, dy_spec], out_specs=out_spec,
        out_shape=jax.ShapeDtypeStruct(out_shape, F32),
        compiler_params=_cp(dimension_semantics=("arbitrary",) * len(grid)),
    )(x, dy)


def _local_step(x, p, positions, target, w_in, w_o, w_a, w_b, conv_w, w_down, w_g, w_p,
                ln_z_g, ln_z_b, w_s, b_s, ln1_g, ln1_b, conv_b, ln2_g, ln2_b, b_g, ln3_g, ln3_b):
    t = x.shape[0]
    half = TM
    c_tab, s1_tab, s2_tab = _rope_tables(positions, t)
    b_full = jnp.repeat(jnp.transpose(b_s[0]), HEAD_DIM, axis=1)
    conv_b4 = conv_b.reshape(N_SHARD, 1, FF_BLK)
    q, k, v, hu, hz, mixed, gm = _qkvuz(x, w_in, c_tab, s1_tab, s2_tab, ln_z_g, ln_z_b, w_s[0], b_full)
    branches = [_attn_fwd(q, k, v, d) for d in DILATIONS]
    attn, lse, cat, xhat1, rstd1, x1b = _mix_ln1([o for o, _ in branches], [l for _, l in branches], gm, x, w_o, ln1_g, ln1_b)
    a_pre, b_act, f = _ffn_in(x1b, w_a, w_b, conv_w, conv_b4)
    xhat2, rstd2, x2b = _ffn_out_ln2(f, w_down, xhat1, ln1_g, ln1_b, ln2_g, ln2_b)
    dr2, dgp, dpp, stat3 = _ple_loss_bwd(xhat2, rstd2, p, target, ln2_g, ln2_b, w_g, b_g, w_p, ln3_g, ln3_b)
    da_pre, dbb, dr1, cstat, stat1 = _ffn_bwd(dr2, a_pre, b_act, w_down, w_a, w_b, conv_w, conv_b4, xhat1, rstd1, ln1_g)
    dattn, duz, dws, dbs, zstat = _mix_bwd(dr1, w_o, hu, hz, mixed, ln_z_g, ln_z_b, w_s[0])
    dqs = [_attn_bwd_dq(q, k, v, dattn, attn, lse, d) for d in DILATIONS]
    dkvs = [_attn_bwd_dkv(q, k, v, dattn, attn, lse, d) for d in DILATIONS]
    dh, grad_x = _dx_in(dqs, [a for a, _ in dkvs], [b_ for _, b_ in dkvs], duz, dr1, w_in, c_tab, s1_tab, s2_tab)

    full_t = lambda w, im: pl.BlockSpec((t, w), im)
    g_w_in = _wgrad("dw_in", x, dh, full_t(half, lambda j, kk: (0, kk)), full_t(W_IN_BLK, lambda j, kk: (0, j)),
                    pl.BlockSpec((None, half, W_IN_BLK), lambda j, kk: (j, kk, 0)), (N_SHARD, D_MODEL, W_IN_BLK), (N_SHARD, 2))
    g_w_o = _wgrad("dw_o", cat, dr1, full_t(half, lambda kk, n: (0, kk)), full_t(half, lambda kk, n: (0, n)),
                   pl.BlockSpec((half, half), lambda kk, n: (kk, n)), (D_MODEL, D_MODEL), (2, 2))
    ffj = pl.BlockSpec((None, t, FF_BLK), lambda j, kk: (j, 0, 0))
    g_w_a = _wgrad("dw_a", x1b, da_pre, full_t(half, lambda j, kk: (0, kk)), ffj,
                   pl.BlockSpec((None, half, FF_BLK), lambda j, kk: (j, kk, 0)), (N_SHARD, D_MODEL, FF_BLK), (N_SHARD, 2))
    g_w_b = _wgrad("dw_b", x1b, dbb, full_t(half, lambda j, kk: (0, kk)), ffj,
                   pl.BlockSpec((None, half, FF_BLK), lambda j, kk: (j, kk, 0)), (N_SHARD, D_MODEL, FF_BLK), (N_SHARD, 2))
    g_w_down = _wgrad("dw_down", f, dr2, ffj, full_t(half, lambda j, n: (0, n)),
                      pl.BlockSpec((None, FF_BLK, half), lambda j, n: (j, 0, n)), (N_SHARD, FF_BLK, D_MODEL), (N_SHARD, 2))
    g_w_g = _wgrad("dw_g", x2b, dgp, full_t(half, lambda kk, n: (0, kk)), full_t(half, lambda kk, n: (0, n)),
                   pl.BlockSpec((half, half), lambda kk, n: (kk, n)), (D_MODEL, D_MODEL), (2, 2))
    g_w_p = _wgrad("dw_p", p, dpp, full_t(D_PLE, lambda j: (0, 0)), full_t(ROW_BLK, lambda j: (0, j)),
                   pl.BlockSpec((None, D_PLE, ROW_BLK), lambda j: (j, 0, 0)), (N_SHARD, D_PLE, ROW_BLK), (N_SHARD,))

    big = dict(w_in=g_w_in, w_o=g_w_o, w_ff_a=g_w_a, w_ff_b=g_w_b, w_ff_down=g_w_down, w_ple_gate=g_w_g, w_ple_in=g_w_p)
    small = dict(
        ln3_g=stat3[0:1], ln3_b=stat3[1:2], b_ple_gate=stat3[2:3], ln2_g=stat3[3:4], ln2_b=stat3[4:5],
        ln1_g=stat1[0:1], ln1_b=stat1[1:2], ln_z_g=zstat[0:1], ln_z_b=zstat[1:2],
        conv_w=cstat[:, 0:3, :], conv_b=cstat[:, 3, :].reshape(1, D_FF),
        w_s=dws, b_s=dbs)
    return grad_x, big, small, (stat3, stat1, zstat, cstat)


def _rows_tile(r, mult, cap=512):
    return max(d for d in range(mult, min(r, cap) + 1, mult) if r % d == 0)


def _cast_bf16(name, w):
    r, c = w.shape
    tr = _rows_tile(r, 16)

    def body(w_ref, o_ref):
        o_ref[...] = w_ref[...].astype(MXU)

    spec = pl.BlockSpec((tr, c), lambda i: (i, 0))
    return pl.pallas_call(body, name=name, grid=(r // tr,), in_specs=[spec], out_specs=spec,
                          out_shape=jax.ShapeDtypeStruct((r, c), MXU), compiler_params=_cp())(w)


def _pair_sum_bf16(name, a, b):
    n, h, c = a.shape
    a2, b2 = a.reshape(n * h, c), b.reshape(n * h, c)
    tr = _rows_tile(n * h, 16)

    def body(a_ref, b_ref, o_ref):
        o_ref[...] = (a_ref[...] + b_ref[...]).astype(BF16)

    spec = pl.BlockSpec((tr, c), lambda i: (i, 0))
    out = pl.pallas_call(body, name=name, grid=(n * h // tr,), in_specs=[spec, spec], out_specs=spec,
                         out_shape=jax.ShapeDtypeStruct((n * h, c), BF16), compiler_params=_cp())(a2, b2)
    return out.reshape(n, h, c)


def _chip_sum(name, parts):
    n, h, c = parts.shape
    tr = _rows_tile(h, 16)

    def body(p_ref, o_ref):
        acc = p_ref[0].astype(F32)
        for k in range(1, n):
            acc = acc + p_ref[k].astype(F32)
        o_ref[...] = acc

    return pl.pallas_call(body, name=name, grid=(h // tr,), in_specs=[pl.BlockSpec((n, tr, c), lambda i: (0, i, 0))],
                          out_specs=pl.BlockSpec((tr, c), lambda i: (i, 0)),
                          out_shape=jax.ShapeDtypeStruct((h, c), F32), compiler_params=_cp())(parts)


def _adamw_math(w, g, m, v):
    m = ADAM_B1 * m + (1.0 - ADAM_B1) * g
    v = ADAM_B2 * v + (1.0 - ADAM_B2) * (g * g)
    m_hat = m / (1.0 - ADAM_B1 ** ADAM_STEP)
    v_hat = v / (1.0 - ADAM_B2 ** ADAM_STEP)
    delta = -ADAM_LR * (m_hat / (jnp.sqrt(v_hat) + ADAM_EPS) + ADAM_WD * w)
    return delta, m, v


def _adamw_big(name, w, g, m, v):
    _, r, c = w.shape
    tr = _rows_tile(r, 8, cap=256)

    def body(w_ref, g_ref, m_ref, v_ref, d_ref, nm_ref, nv_ref):
        d_ref[...], nm_ref[...], nv_ref[...] = _adamw_math(w_ref[...], g_ref[...], m_ref[...], v_ref[...])

    s3 = pl.BlockSpec((None, tr, c), lambda i: (0, i, 0))
    s2 = pl.BlockSpec((tr, c), lambda i: (i, 0))
    return pl.pallas_call(body, name=name, grid=(r // tr,), in_specs=[s3, s2, s3, s3], out_specs=[s3, s3, s3],
                          out_shape=[jax.ShapeDtypeStruct(w.shape, F32)] * 3, compiler_params=_cp())(w, g, m, v)


MESH = pl.DeviceIdType.MESH
ANY = pl.BlockSpec(memory_space=pl.ANY)


def _place():
    x, y, c = lax.axis_index("x"), lax.axis_index("y"), lax.axis_index("c")
    chips = [(1 - x, y), (x, 1 - y), (1 - x, 1 - y)]
    return x, y, c, 2 * x + y, chips


def _remote(src, dst, send_sem, recv_sem, dev):
    return pltpu.make_async_remote_copy(src_ref=src, dst_ref=dst, send_sem=send_sem, recv_sem=recv_sem,
                                        device_id=dev, device_id_type=MESH)


def _half(ref, hc, rows):
    return ref.at[pl.ds(hc * (rows // 2), rows // 2)]


def _allgather(shards, split):
    n = len(shards)

    def body(*refs):
        ins, outs = refs[:n], refs[n:2 * n]
        send, recv, fsend, frecv, loc = refs[2 * n:]
        x, y, c, j, chips = _place()
        rows = [s.shape[0] for s in shards]

        def piece(ref, a, hc):
            return _half(ref, hc, rows[a]) if split[a] else ref

        def direct(a, t, slot, dev):
            src = piece(ins[a], a, c) if slot is None else piece(outs[a].at[slot], a, c)
            dst = piece(outs[a].at[j if slot is None else slot], a, c)
            return _remote(src, dst, send.at[a, t], recv.at[a, t], dev)

        local = [pltpu.make_async_copy(ins[a], outs[a].at[j], loc.at[a]) for a in range(n)]
        for cp in local:
            cp.start()
        sends = [direct(a, t, None, (*chips[t], c)) for a in range(n) for t in range(3)]
        for cp in sends:
            cp.start()
        fwd = []
        for t, (px, py) in enumerate(chips):
            jt = 2 * px + py
            for a in range(n):
                direct(a, t, jt, (px, py, c)).wait_recv()
                if split[a]:
                    blk = _half(outs[a].at[jt], c, rows[a])
                    cp = _remote(blk, blk, fsend.at[a, t], frecv.at[a, t], (x, y, 1 - c))
                    cp.start()
                    fwd.append(cp)
        for t, (px, py) in enumerate(chips):
            jt = 2 * px + py
            for a in range(n):
                if split[a]:
                    blk = _half(outs[a].at[jt], 1 - c, rows[a])
                    _remote(blk, blk, fsend.at[a, t], frecv.at[a, t], (x, y, 1 - c)).wait_recv()
        for cp in sends + fwd:
            cp.wait_send()
        for cp in local:
            cp.wait()

    sem = pltpu.SemaphoreType.DMA
    return pl.pallas_call(
        body, name="allgather_weights", in_specs=[ANY] * n, out_specs=[ANY] * n,
        out_shape=[jax.ShapeDtypeStruct((N_SHARD, *s.shape), s.dtype) for s in shards],
        scratch_shapes=[sem((n, 3)), sem((n, 3)), sem((n, 3)), sem((n, 3)), sem((n,))],
    )(*shards)


def _sibling_swap(grads):
    n = len(grads)

    def body(*refs):
        ins, mine, got = refs[:n], refs[n:2 * n], refs[2 * n:3 * n]
        send, recv, loc = refs[3 * n:]
        x, y, c, _, _ = _place()
        cps = []
        for a in range(n):
            h = grads[a].shape[1] // 2
            keep = pltpu.make_async_copy(ins[a].at[:, pl.ds(c * h, h)], mine[a], loc.at[a])
            give = _remote(ins[a].at[:, pl.ds((1 - c) * h, h)], got[a], send.at[a], recv.at[a], (x, y, 1 - c))
            keep.start()
            give.start()
            cps.append((keep, give))
        for keep, give in cps:
            give.wait_recv()
            give.wait_send()
            keep.wait()

    sem = pltpu.SemaphoreType.DMA
    halves = [jax.ShapeDtypeStruct((g.shape[0], g.shape[1] // 2, g.shape[2]), g.dtype) for g in grads]
    outs = pl.pallas_call(body, name="rs_sibling_swap", in_specs=[ANY] * n, out_specs=[ANY] * (2 * n), out_shape=halves + halves,
                          scratch_shapes=[sem((n,)), sem((n,)), sem((n,))])(*grads)
    return outs[:n], outs[n:]


def _chip_exchange(parts):
    n = len(parts)

    def body(*refs):
        ins, outs = refs[:n], refs[n:2 * n]
        send, recv, loc = refs[2 * n:]
        x, y, c, j, chips = _place()
        local = [pltpu.make_async_copy(ins[a].at[j], outs[a].at[j], loc.at[a]) for a in range(n)]
        for cp in local:
            cp.start()
        sends = []
        for t, (px, py) in enumerate(chips):
            jt = 2 * px + py
            for a in range(n):
                cp = _remote(ins[a].at[jt], outs[a].at[j], send.at[a, t], recv.at[a, t], (px, py, c))
                cp.start()
                sends.append(cp)
        for t, (px, py) in enumerate(chips):
            jt = 2 * px + py
            for a in range(n):
                _remote(ins[a].at[jt], outs[a].at[jt], send.at[a, t], recv.at[a, t], (px, py, c)).wait_recv()
        for cp in sends:
            cp.wait_send()
        for cp in local:
            cp.wait()

    sem = pltpu.SemaphoreType.DMA
    return pl.pallas_call(body, name="rs_chip_exchange", in_specs=[ANY] * n, out_specs=[ANY] * n,
                          out_shape=[jax.ShapeDtypeStruct(p.shape, p.dtype) for p in parts],
                          scratch_shapes=[sem((n, 3)), sem((n, 3)), sem((n,))])(*parts)


def _sibling_join(halves):
    n = len(halves)

    def body(*refs):
        ins, outs = refs[:n], refs[n:2 * n]
        send, recv, loc = refs[2 * n:]
        x, y, c, _, _ = _place()
        cps = []
        for a in range(n):
            h = halves[a].shape[0]
            keep = pltpu.make_async_copy(ins[a], outs[a].at[pl.ds(c * h, h)], loc.at[a])
            give = _remote(ins[a], outs[a].at[pl.ds(c * h, h)], send.at[a], recv.at[a], (x, y, 1 - c))
            keep.start()
            give.start()
            cps.append((keep, give, h))
        for a, (keep, give, h) in enumerate(cps):
            _remote(ins[a], outs[a].at[pl.ds((1 - c) * h, h)], send.at[a], recv.at[a], (x, y, 1 - c)).wait_recv()
            give.wait_send()
            keep.wait()

    sem = pltpu.SemaphoreType.DMA
    return pl.pallas_call(body, name="rs_sibling_join", in_specs=[ANY] * n, out_specs=[ANY] * n,
                          out_shape=[jax.ShapeDtypeStruct((2 * h.shape[0], h.shape[1]), h.dtype) for h in halves],
                          scratch_shapes=[sem((n,)), sem((n,)), sem((n,))])(*halves)


def _allreduce_small(arrs):
    n = len(arrs)

    def body(*refs):
        ins, outs = refs[:n], refs[n:2 * n]
        sib, chip = refs[2 * n:3 * n], refs[3 * n:4 * n]
        ssend, srecv, csend, crecv = refs[4 * n:]
        x, y, c, j, chips = _place()
        swaps = [_remote(ins[a], sib[a], ssend.at[a], srecv.at[a], (x, y, 1 - c)) for a in range(n)]
        for cp in swaps:
            cp.start()
        sends = []
        for a in range(n):
            swaps[a].wait_recv()
            chip[a][j] = ins[a][...] + sib[a][...]
            for t, (px, py) in enumerate(chips):
                cp = _remote(chip[a].at[j], chip[a].at[j], csend.at[a, t], crecv.at[a, t], (px, py, c))
                cp.start()
                sends.append(cp)
        for a in range(n):
            for t, (px, py) in enumerate(chips):
                jt = 2 * px + py
                _remote(chip[a].at[jt], chip[a].at[jt], csend.at[a, t], crecv.at[a, t], (px, py, c)).wait_recv()
            outs[a][...] = ((chip[a][0] + chip[a][1]) + chip[a][2]) + chip[a][3]
        for cp in swaps + sends:
            cp.wait_send()

    sem = pltpu.SemaphoreType.DMA
    vm = pl.BlockSpec(memory_space=pltpu.VMEM)
    return pl.pallas_call(
        body, name="allreduce_small", in_specs=[vm] * n, out_specs=[vm] * n,
        out_shape=[jax.ShapeDtypeStruct(a.shape, F32) for a in arrs],
        scratch_shapes=[pltpu.VMEM(a.shape, F32) for a in arrs] + [pltpu.VMEM((N_SHARD, *a.shape), F32) for a in arrs]
        + [sem((n,)), sem((n,)), sem((n, 3)), sem((n, 3))],
        compiler_params=_cp(),
    )(*arrs)


def _reduce_scatter(grads):
    names = list(grads)
    mine, got = _sibling_swap([grads[k] for k in names])
    pair = [_pair_sum_bf16(f"rs_pair_{k}", a, b) for k, a, b in zip(names, mine, got)]
    landed = _chip_exchange(pair)
    halves = [_chip_sum(f"rs_sum_{k}", p) for k, p in zip(names, landed)]
    return dict(zip(names, _sibling_join(halves)))


SMALL_1024 = ("ln1_g", "ln1_b", "ln2_g", "ln2_b", "b_ple_gate", "ln3_g", "ln3_b")


def _adamw_small(red3, red1, redz, g_conv_w, redc, red_ws, red_bs, params):
    shape2d = {"ln_z_g": (1, D_GMLP), "ln_z_b": (1, D_GMLP), "w_s": (N_HEADS * BLK, BLK), "b_s": (N_HEADS, BLK),
               "conv_w": (3, FF_BLK), "conv_b": (N_SHARD, FF_BLK), **{k: (1, D_MODEL) for k in SMALL_1024}}
    names = list(shape2d)
    flat = [a.reshape(shape2d[k]) for k in names for a in params[k]]

    def body(r3, r1, rz, gcw, rc, rws, rbs, *refs):
        ins, outs = refs[:3 * len(names)], refs[3 * len(names):]

        def grad_of(k):
            if k == "w_s":
                return rws[...]
            if k == "b_s":
                return rbs[...]
            if k == "conv_w":
                return gcw[0:3, :]
            if k == "conv_b":
                return jnp.concatenate([rc[j * STAT_ROWS + 3:j * STAT_ROWS + 4, :] for j in range(N_SHARD)], axis=0)
            src, row = {"ln3_g": (r3, 0), "ln3_b": (r3, 1), "b_ple_gate": (r3, 2), "ln2_g": (r3, 3), "ln2_b": (r3, 4),
                        "ln1_g": (r1, 0), "ln1_b": (r1, 1), "ln_z_g": (rz, 0), "ln_z_b": (rz, 1)}[k]
            return src[row:row + 1, :]

        for i, k in enumerate(names):
            w_ref, m_ref, v_ref = ins[3 * i:3 * i + 3]
            g_ref, d_ref, nm_ref, nv_ref = outs[4 * i:4 * i + 4]
            g = grad_of(k)
            g_ref[...] = g
            d_ref[...], nm_ref[...], nv_ref[...] = _adamw_math(w_ref[...], g, m_ref[...], v_ref[...])

    res = pl.pallas_call(
        body, name="adamw_small",
        out_shape=[jax.ShapeDtypeStruct(shape2d[k], F32) for k in names for _ in range(4)],
        compiler_params=_cp(),
    )(red3, red1, redz, g_conv_w, redc, red_ws, red_bs, *flat)
    return {k: tuple(r.reshape(params[k][0].shape) for r in res[4 * i:4 * i + 4]) for i, k in enumerate(names)}


WEIGHTS = ("w_in", "ln_z_g", "ln_z_b", "w_s", "b_s", "w_o", "ln1_g", "ln1_b", "w_ff_a", "w_ff_b", "conv_w", "conv_b",
           "w_ff_down", "ln2_g", "ln2_b", "w_ple_gate", "b_ple_gate", "w_ple_in", "ln3_g", "ln3_b")
BIG = ("w_in", "w_o", "w_ff_a", "w_ff_b", "w_ff_down", "w_ple_gate", "w_ple_in")


def kernel(x, p, positions, w_in, ln_z_g, ln_z_b, w_s, b_s, w_o, ln1_g, ln1_b, w_ff_a, w_ff_b, conv_w, conv_b, w_ff_down, ln2_g, ln2_b, w_ple_gate, b_ple_gate, w_ple_in, ln3_g, ln3_b, loss_target, m_w_in, m_ln_z_g, m_ln_z_b, m_w_s, m_b_s, m_w_o, m_ln1_g, m_ln1_b, m_w_ff_a, m_w_ff_b, m_conv_w, m_conv_b, m_w_ff_down, m_ln2_g, m_ln2_b, m_w_ple_gate, m_b_ple_gate, m_w_ple_in, m_ln3_g, m_ln3_b, v_w_in, v_ln_z_g, v_ln_z_b, v_w_s, v_b_s, v_w_o, v_ln1_g, v_ln1_b, v_w_ff_a, v_w_ff_b, v_conv_w, v_conv_b, v_w_ff_down, v_ln2_g, v_ln2_b, v_w_ple_gate, v_b_ple_gate, v_w_ple_in, v_ln3_g, v_ln3_b):
    args = locals()
    w = {k: args[k] for k in WEIGHTS}
    m = {k: args["m_" + k] for k in WEIGHTS}
    v = {k: args["v_" + k] for k in WEIGHTS}

    shards = [_cast_bf16(f"cast_{k}", w[k][0]) for k in BIG] + [w["conv_w"][0]]
    full = _allgather(shards, [True] * len(BIG) + [False])
    fw = dict(zip(BIG + ("conv_w",), full))

    grad_x, big, small, (stat3, stat1, zstat, cstat) = _local_step(
        x[0], p[0, 0], positions, loss_target[0], fw["w_in"], fw["w_o"].reshape(D_MODEL, D_MODEL), fw["w_ff_a"], fw["w_ff_b"],
        fw["conv_w"], fw["w_ff_down"], fw["w_ple_gate"].reshape(D_MODEL, D_MODEL), fw["w_ple_in"],
        ln_z_g, ln_z_b, w_s, b_s, ln1_g, ln1_b, conv_b, ln2_g, ln2_b, b_ple_gate, ln3_g, ln3_b)

    stacked = {k: big[k].reshape(N_SHARD, *w[k].shape[1:]) for k in BIG}
    red = _reduce_scatter(stacked)
    out = {}
    for k in BIG:
        d, nm, nv = _adamw_big(f"adamw_{k}", w[k], red[k], m[k], v[k])
        out[k] = (red[k].reshape(w[k].shape), d, nm, nv)

    red3, red1, redz, redc, red_ws, red_bs = _allreduce_small(
        [stat3, stat1, zstat, cstat.reshape(N_SHARD * STAT_ROWS, FF_BLK), small["w_s"].reshape(N_HEADS * BLK, BLK), small["b_s"]])
    loss = (0.5 / D_MODEL) * jnp.sum(red3[5])
    j = 2 * lax.axis_index("x") + lax.axis_index("y")
    g_conv_w = lax.dynamic_slice_in_dim(redc, j * STAT_ROWS, STAT_ROWS, 0)
    names_small = [k for k in WEIGHTS if k not in BIG]
    out.update(_adamw_small(red3, red1, redz, g_conv_w, redc, red_ws, red_bs, {k: (w[k], m[k], v[k]) for k in names_small}))

    return (loss, grad_x[None], *[out[k][0] for k in WEIGHTS], *[out[k][1] for k in WEIGHTS],
            *[out[k][2] for k in WEIGHTS], *[out[k][3] for k in WEIGHTS])
```

```python
import functools
import math

import numpy as np
import jax
import jax.numpy as jnp
from jax import lax
from jax.experimental import pallas as pl
from jax.experimental.pallas import tpu as pltpu

F32 = jnp.float32
BF16 = jnp.bfloat16
MXU = BF16

D_MODEL = 1024
HEAD_DIM = 64
N_HEADS = 8
D_ATTN = 512
D_GMLP = 512
D_IN = 2560
DILATIONS = (1, 4, 16)
BLK = 128
ROPE_THETA = 500000.0
ROPE_DIM = 16
D_FF = 2816
D_PLE = 256
LN_EPS = 1e-5
ALPHA = 2.0 ** 0.25
NEG_INF = -1e30
N_SHARD = 4
W_IN_BLK = D_IN // N_SHARD
FF_BLK = D_FF // N_SHARD
ROW_BLK = D_MODEL // N_SHARD
ADAM_LR, ADAM_B1, ADAM_B2, ADAM_EPS, ADAM_WD, ADAM_STEP = 0.001, 0.9, 0.999, 1e-08, 0.01, 10

TM = 512
HALO = 8
VMEM_LIMIT = 56 * 1024 * 1024


def _cp(**kw):
    return pltpu.CompilerParams(vmem_limit_bytes=VMEM_LIMIT, **kw)


def _full(shape):
    n = len(shape)
    return pl.BlockSpec(shape, lambda *_: (0,) * n)


def _gelu(x):
    return 0.5 * x * (1.0 + lax.erf(x * (1.0 / math.sqrt(2.0))))


def _gelu_grad(x):
    return 0.5 * (1.0 + lax.erf(x * (1.0 / math.sqrt(2.0)))) + x * jnp.exp(-0.5 * x * x) * (1.0 / math.sqrt(2.0 * math.pi))


def _ln_fwd(r):
    mu = jnp.mean(r, axis=-1, keepdims=True)
    xc = r - mu
    var = jnp.mean(xc * xc, axis=-1, keepdims=True)
    rstd = lax.rsqrt(var + LN_EPS)
    return xc * rstd, rstd


def _ln_bwd(dy, xhat, rstd, g):
    dxh = dy * g
    m1 = jnp.mean(dxh, axis=-1, keepdims=True)
    m2 = jnp.mean(dxh * xhat, axis=-1, keepdims=True)
    return rstd * (dxh - m1 - xhat * m2)


def _dot(a, b):
    return jnp.dot(a.astype(MXU), b.astype(MXU), preferred_element_type=F32)


def _dot_nt(a, b):
    return lax.dot_general(a.astype(MXU), b.astype(MXU), (((1,), (1,)), ((), ())), preferred_element_type=F32)


def _dot_tn(a, b):
    return lax.dot_general(a.astype(MXU), b.astype(MXU), (((0,), (0,)), ((), ())), preferred_element_type=F32)


def _colsum(v):
    return jnp.sum(v, axis=0, keepdims=True)


def _rope_tables(positions, t):
    inv = np.float32(ROPE_THETA) ** (-np.arange(0, ROPE_DIM, 2, dtype=np.float32) / np.float32(ROPE_DIM))
    half = ROPE_DIM // 2
    pos_rep = jnp.repeat(positions.reshape(t // 16, 16), half, axis=1)
    inv_row = jnp.asarray(np.tile(inv, 16)[None, :], F32)

    def trig_body(pos_ref, inv_ref, cos_ref, sin_ref):
        ang = pos_ref[...].astype(F32) * inv_ref[...]
        cos_ref[...] = jnp.cos(ang)
        sin_ref[...] = jnp.sin(ang)

    cos8, sin8 = pl.pallas_call(
        trig_body, name="rope_trig",
        out_shape=(jax.ShapeDtypeStruct((t // 16, 128), F32), jax.ShapeDtypeStruct((t // 16, 128), F32)),
    )(pos_rep, inv_row)
    cos8 = cos8.reshape(t, half)
    sin8 = sin8.reshape(t, half)

    lane = np.arange(128) % HEAD_DIM
    sel = (np.arange(half)[:, None] == (lane % half)[None, :])
    e_cos = (sel & (lane < ROPE_DIM)[None, :]).astype(np.float32)
    e_s1 = -(sel & (lane < half)[None, :]).astype(np.float32)
    e_s2 = (sel & ((lane >= half) & (lane < ROPE_DIM))[None, :]).astype(np.float32)
    ones = (lane >= ROPE_DIM).astype(np.float32)[None, :]

    def expand_body(cos_ref, sin_ref, ec_ref, e1_ref, e2_ref, ones_ref, c_ref, s1_ref, s2_ref):
        hp = lax.Precision.HIGHEST
        c_ref[...] = jnp.dot(cos_ref[...], ec_ref[...], precision=hp, preferred_element_type=F32) + ones_ref[...]
        s1_ref[...] = jnp.dot(sin_ref[...], e1_ref[...], precision=hp, preferred_element_type=F32)
        s2_ref[...] = jnp.dot(sin_ref[...], e2_ref[...], precision=hp, preferred_element_type=F32)

    tab = jax.ShapeDtypeStruct((t, 128), F32)
    return pl.pallas_call(expand_body, name="rope_expand", out_shape=(tab, tab, tab), compiler_params=_cp())(
        cos8, sin8, jnp.asarray(e_cos), jnp.asarray(e_s1), jnp.asarray(e_s2), jnp.asarray(ones))


def _tile_heads(tab):
    return jnp.concatenate([tab] * (D_ATTN // 128), axis=1)


def _rope_apply(v, c, s1, s2):
    n = v.shape[1]
    half = ROPE_DIM // 2
    return v * c + pltpu.roll(v, n - half, 1) * s1 + pltpu.roll(v, half, 1) * s2


def _rope_apply_t(g, c, s1, s2):
    n = g.shape[1]
    half = ROPE_DIM // 2
    return g * c + pltpu.roll(g * s1, half, 1) + pltpu.roll(g * s2, n - half, 1)


def _qkvuz(x, w_in, c_tab, s1_tab, s2_tab, ln_z_g, ln_z_b, w_s, b_full):
    t = x.shape[0]
    nchunk = TM // BLK

    def body(x_ref, w_ref, c_ref, s1_ref, s2_ref, g_ref, b_ref, ws_ref, bf_ref,
             q_ref, k_ref, v_ref, hu_ref, hz_ref, mixed_ref, gm_ref, h_scr, wm_scr):
        @pl.when(pl.program_id(0) == 0)
        def _():
            row = lax.broadcasted_iota(jnp.int32, (BLK, BLK), 0)
            col = lax.broadcasted_iota(jnp.int32, (BLK, BLK), 1)
            for g in range(N_HEADS):
                wm_scr[g] = jnp.where(col <= row, ws_ref[g], 0.0).astype(MXU)

        xb = x_ref[...].astype(MXU)
        for j in range(N_SHARD):
            h_scr[:, j * W_IN_BLK:(j + 1) * W_IN_BLK] = jnp.dot(xb, w_ref[j], preferred_element_type=F32)
        c, s1, s2 = _tile_heads(c_ref[...]), _tile_heads(s1_ref[...]), _tile_heads(s2_ref[...])
        q = _rope_apply(h_scr[:, 0:D_ATTN], c, s1, s2)
        q_ref[...] = (q * (1.0 / math.sqrt(HEAD_DIM))).astype(MXU)
        k_ref[...] = _rope_apply(h_scr[:, D_ATTN:2 * D_ATTN], c, s1, s2).astype(MXU)
        v_ref[...] = h_scr[:, 2 * D_ATTN:3 * D_ATTN].astype(MXU)
        hu = h_scr[:, 3 * D_ATTN:3 * D_ATTN + D_GMLP]
        hz = h_scr[:, 3 * D_ATTN + D_GMLP:]
        hu_ref[...] = hu
        hz_ref[...] = hz
        zhat, _ = _ln_fwd(_gelu(hz))
        zn = (zhat * g_ref[...] + b_ref[...]).astype(MXU)
        for ch in range(nchunk):
            rows = slice(ch * BLK, (ch + 1) * BLK)
            for g in range(N_HEADS):
                cols = slice(g * HEAD_DIM, (g + 1) * HEAD_DIM)
                mixed_ref[rows, cols] = jnp.dot(wm_scr[g], zn[rows, cols], preferred_element_type=F32) + bf_ref[:, cols]
        gm_ref[...] = (_gelu(hu) * mixed_ref[...]).astype(MXU)

    tok = lambda w: pl.BlockSpec((TM, w), lambda i: (i, 0))
    outs = [jax.ShapeDtypeStruct((t, D_ATTN), MXU)] * 3 + [jax.ShapeDtypeStruct((t, D_GMLP), F32)] * 3 + [
        jax.ShapeDtypeStruct((t, D_GMLP), MXU)]
    return pl.pallas_call(
        body, name="qkvuz", grid=(t // TM,),
        in_specs=[tok(D_MODEL), _full(w_in.shape), tok(128), tok(128), tok(128), _full(ln_z_g.shape), _full(ln_z_b.shape),
                  _full(w_s.shape), _full(b_full.shape)],
        out_specs=[tok(D_ATTN)] * 7, out_shape=outs,
        scratch_shapes=[pltpu.VMEM((TM, D_IN), F32), pltpu.VMEM((N_HEADS, BLK, BLK), MXU)],
        compiler_params=_cp(dimension_semantics=("arbitrary",)),
    )(x, w_in, c_tab, s1_tab, s2_tab, ln_z_g, ln_z_b, w_s, b_full)


def _attn_view(a, d):
    return a.reshape(a.shape[0] // d, d * a.shape[1])


def _band_mask(n_q, n_k, q_off):
    i = lax.broadcasted_iota(jnp.int32, (n_q, n_k), 0) + q_off
    j = lax.broadcasted_iota(jnp.int32, (n_q, n_k), 1)
    return i, j


def _attn_fwd(q, k, v, d):
    t = q.shape[0]
    nb = t // d // BLK
    qv, kv, vv = _attn_view(q, d), _attn_view(k, d), _attn_view(v, d)

    def body(q_ref, kp_ref, kc_ref, vp_ref, vc_ref, o_ref, l_ref):
        n = pl.program_id(1)
        i, j = _band_mask(BLK, 2 * BLK, 0)
        valid = (j >= i) & (j <= i + BLK) & ((j >= BLK) | (n > 0))
        kcat = jnp.concatenate([kp_ref[...], kc_ref[...]], axis=0)
        vcat = jnp.concatenate([vp_ref[...], vc_ref[...]], axis=0)
        for h in range(N_HEADS):
            cols = slice(h * HEAD_DIM, (h + 1) * HEAD_DIM)
            s = jnp.where(valid, _dot_nt(q_ref[:, cols], kcat[:, cols]), NEG_INF)
            m = jnp.max(s, axis=-1, keepdims=True)
            e = jnp.exp(s - m)
            den = jnp.sum(e, axis=-1, keepdims=True)
            o_ref[:, cols] = _dot(e * (1.0 / den), vcat[:, cols])
            l_ref[:, cols] = jnp.broadcast_to(m + jnp.log(den), (BLK, HEAD_DIM))

    cur = pl.BlockSpec((BLK, D_ATTN), lambda r, n: (n, r))
    prev = pl.BlockSpec((BLK, D_ATTN), lambda r, n: (jnp.maximum(n - 1, 0), r))
    o, l = pl.pallas_call(
        body, name=f"attn_fwd_d{d}", grid=(d, nb), in_specs=[cur, prev, cur, prev, cur], out_specs=[cur, cur],
        out_shape=[jax.ShapeDtypeStruct(qv.shape, F32)] * 2,
        compiler_params=_cp(dimension_semantics=("arbitrary", "arbitrary")),
    )(qv, kv, kv, vv, vv)
    return o.reshape(t, D_ATTN), l.reshape(t, D_ATTN)


def _attn_bwd_dq(q, k, v, do, o, lse, d):
    t = q.shape[0]
    nb = t // d // BLK
    qv, kv, vv, dov, ov, lv = (_attn_view(a, d) for a in (q, k, v, do, o, lse))

    def body(q_ref, kp_ref, kc_ref, vp_ref, vc_ref, do_ref, o_ref, l_ref, dq_ref):
        n = pl.program_id(1)
        i, j = _band_mask(BLK, 2 * BLK, 0)
        valid = (j >= i) & (j <= i + BLK) & ((j >= BLK) | (n > 0))
        kcat = jnp.concatenate([kp_ref[...], kc_ref[...]], axis=0)
        vcat = jnp.concatenate([vp_ref[...], vc_ref[...]], axis=0)
        for h in range(N_HEADS):
            cols = slice(h * HEAD_DIM, (h + 1) * HEAD_DIM)
            s = _dot_nt(q_ref[:, cols], kcat[:, cols])
            p = jnp.where(valid, jnp.exp(s - l_ref[:, h * HEAD_DIM:h * HEAD_DIM + 1]), 0.0)
            doh = do_ref[:, cols]
            delta = jnp.sum(doh * o_ref[:, cols], axis=-1, keepdims=True)
            ds = p * (_dot_nt(doh, vcat[:, cols]) - delta)
            dq_ref[:, cols] = _dot(ds, kcat[:, cols])

    cur = pl.BlockSpec((BLK, D_ATTN), lambda r, n: (n, r))
    prev = pl.BlockSpec((BLK, D_ATTN), lambda r, n: (jnp.maximum(n - 1, 0), r))
    dq = pl.pallas_call(
        body, name=f"attn_dq_d{d}", grid=(d, nb), in_specs=[cur, prev, cur, prev, cur, cur, cur, cur], out_specs=cur,
        out_shape=jax.ShapeDtypeStruct(qv.shape, F32),
        compiler_params=_cp(dimension_semantics=("arbitrary", "arbitrary")),
    )(qv, kv, kv, vv, vv, dov, ov, lv)
    return dq.reshape(t, D_ATTN)


def _attn_bwd_dkv(q, k, v, do, o, lse, d):
    t = q.shape[0]
    nb = t // d // BLK
    qv, kv, vv, dov, ov, lv = (_attn_view(a, d) for a in (q, k, v, do, o, lse))

    def body(k_ref, v_ref, qc_ref, qn_ref, doc_ref, don_ref, oc_ref, on_ref, lc_ref, ln_ref, dk_ref, dv_ref):
        n = pl.program_id(1)
        ii = lax.broadcasted_iota(jnp.int32, (2 * BLK, BLK), 0)
        j = lax.broadcasted_iota(jnp.int32, (2 * BLK, BLK), 1)
        valid = (ii >= j) & (ii <= j + BLK) & ((ii < BLK) | (n < nb - 1))
        qcat = jnp.concatenate([qc_ref[...], qn_ref[...]], axis=0)
        docat = jnp.concatenate([doc_ref[...], don_ref[...]], axis=0)
        ocat = jnp.concatenate([oc_ref[...], on_ref[...]], axis=0)
        lcat = jnp.concatenate([lc_ref[...], ln_ref[...]], axis=0)
        for h in range(N_HEADS):
            cols = slice(h * HEAD_DIM, (h + 1) * HEAD_DIM)
            s = _dot_nt(qcat[:, cols], k_ref[:, cols])
            p = jnp.where(valid, jnp.exp(s - lcat[:, h * HEAD_DIM:h * HEAD_DIM + 1]), 0.0)
            doh = docat[:, cols]
            delta = jnp.sum(doh * ocat[:, cols], axis=-1, keepdims=True)
            ds = p * (_dot_nt(doh, v_ref[:, cols]) - delta)
            dv_ref[:, cols] = _dot_tn(p, doh)
            dk_ref[:, cols] = _dot_tn(ds, qcat[:, cols])

    cur = pl.BlockSpec((BLK, D_ATTN), lambda r, n: (n, r))
    nxt = pl.BlockSpec((BLK, D_ATTN), lambda r, n: (jnp.minimum(n + 1, nb - 1), r))
    dk, dv = pl.pallas_call(
        body, name=f"attn_dkv_d{d}", grid=(d, nb), in_specs=[cur, cur, cur, nxt, cur, nxt, cur, nxt, cur, nxt],
        out_specs=[cur, cur], out_shape=[jax.ShapeDtypeStruct(qv.shape, F32)] * 2,
        compiler_params=_cp(dimension_semantics=("arbitrary", "arbitrary")),
    )(kv, vv, qv, qv, dov, dov, ov, ov, lv, lv)
    return dk.reshape(t, D_ATTN), dv.reshape(t, D_ATTN)


def _mix_ln1(os_, ls_, gm, x, w_o, ln1_g, ln1_b):
    t = x.shape[0]

    def body(o1, o2, o3, l1, l2, l3, gm_ref, x_ref, wo_ref, g_ref, b_ref,
             attn_ref, lse_ref, cat_ref, xhat_ref, rstd_ref, x1b_ref):
        la, lb, lc = l1[...], l2[...], l3[...]
        m = jnp.maximum(jnp.maximum(la, lb), lc)
        ea, eb, ec = jnp.exp(la - m), jnp.exp(lb - m), jnp.exp(lc - m)
        den = ea + eb + ec
        attn = (ea * o1[...] + eb * o2[...] + ec * o3[...]) / den
        attn_ref[...] = attn
        lse_ref[...] = m + jnp.log(den)
        cat_ref[:, 0:D_ATTN] = attn.astype(MXU)
        cat_ref[:, D_ATTN:] = gm_ref[...]
        mix = jnp.dot(cat_ref[...], wo_ref[...], preferred_element_type=F32)
        xhat, rstd = _ln_fwd(ALPHA * x_ref[...] + mix)
        xhat_ref[...] = xhat
        rstd_ref[...] = rstd
        x1b_ref[...] = (xhat * g_ref[...] + b_ref[...]).astype(MXU)

    tok = lambda w: pl.BlockSpec((TM, w), lambda i: (i, 0))
    outs = [jax.ShapeDtypeStruct((t, D_ATTN), F32)] * 2 + [
        jax.ShapeDtypeStruct((t, D_MODEL), MXU), jax.ShapeDtypeStruct((t, D_MODEL), F32), jax.ShapeDtypeStruct((t, 1), F32),
        jax.ShapeDtypeStruct((t, D_MODEL), MXU)]
    return pl.pallas_call(
        body, name="mix_ln1", grid=(t // TM,),
        in_specs=[tok(D_ATTN)] * 7 + [tok(D_MODEL), _full(w_o.shape), _full(ln1_g.shape), _full(ln1_b.shape)],
        out_specs=[tok(D_ATTN), tok(D_ATTN), tok(D_MODEL), tok(D_MODEL), tok(1), tok(D_MODEL)], out_shape=outs,
        compiler_params=_cp(dimension_semantics=("arbitrary",)),
    )(*os_, *ls_, gm, x, w_o, ln1_g, ln1_b)


def _conv_fwd(a_ext, w_ref, b_ref, rows):
    return (b_ref[...] + w_ref[2:3, :] * a_ext[HALO:HALO + rows] + w_ref[1:2, :] * a_ext[HALO - 1:HALO - 1 + rows]
            + w_ref[0:1, :] * a_ext[HALO - 2:HALO - 2 + rows])


def _ffn_in(x1b, w_a, w_b, conv_w, conv_b):
    t = x1b.shape[0]
    hb = TM // HALO

    def body(x_ref, xh_ref, wa_ref, wb_ref, cw_ref, cb_ref, apre_ref, b_ref, f_ref):
        i = pl.program_id(1)
        a_pre = jnp.dot(x_ref[...], wa_ref[...], preferred_element_type=F32)
        a_halo = jnp.dot(xh_ref[...], wa_ref[...], preferred_element_type=F32)
        a_halo = jnp.where(i > 0, a_halo, 0.0)
        a = _conv_fwd(jnp.concatenate([a_halo, a_pre], axis=0), cw_ref, cb_ref, TM)
        b = jnp.dot(x_ref[...], wb_ref[...], preferred_element_type=F32)
        apre_ref[...] = a_pre
        b_ref[...] = b
        f_ref[...] = (_gelu(a) * b).astype(MXU)

    blk = lambda r, c: pl.BlockSpec((None, r, c), lambda j, i: (j, 0, 0))
    tokj = pl.BlockSpec((None, TM, FF_BLK), lambda j, i: (j, i, 0))
    outs = [jax.ShapeDtypeStruct((N_SHARD, t, FF_BLK), F32)] * 2 + [jax.ShapeDtypeStruct((N_SHARD, t, FF_BLK), MXU)]
    return pl.pallas_call(
        body, name="ffn_in", grid=(N_SHARD, t // TM),
        in_specs=[pl.BlockSpec((TM, D_MODEL), lambda j, i: (i, 0)),
                  pl.BlockSpec((HALO, D_MODEL), lambda j, i: (jnp.maximum(i * hb - 1, 0), 0)),
                  blk(D_MODEL, FF_BLK), blk(D_MODEL, FF_BLK), blk(3, FF_BLK), blk(1, FF_BLK)],
        out_specs=[tokj, tokj, tokj], out_shape=outs,
        compiler_params=_cp(dimension_semantics=("arbitrary", "arbitrary")),
    )(x1b, x1b, w_a, w_b, conv_w, conv_b)


def _ffn_out_ln2(f, w_down, xhat1, ln1_g, ln1_b, ln2_g, ln2_b):
    t = xhat1.shape[0]

    def body(f_ref, wd_ref, xh_ref, g1_ref, b1_ref, g2_ref, b2_ref, xhat_ref, rstd_ref, x2b_ref):
        ff = jnp.dot(f_ref[0], wd_ref[0], preferred_element_type=F32)
        for j in range(1, N_SHARD):
            ff = ff + jnp.dot(f_ref[j], wd_ref[j], preferred_element_type=F32)
        x1 = xh_ref[...] * g1_ref[...] + b1_ref[...]
        xhat, rstd = _ln_fwd(ALPHA * x1 + ff)
        xhat_ref[...] = xhat
        rstd_ref[...] = rstd
        x2b_ref[...] = (xhat * g2_ref[...] + b2_ref[...]).astype(MXU)

    tok = lambda w: pl.BlockSpec((TM, w), lambda i: (i, 0))
    vec = _full((1, D_MODEL))
    outs = [jax.ShapeDtypeStruct((t, D_MODEL), F32), jax.ShapeDtypeStruct((t, 1), F32), jax.ShapeDtypeStruct((t, D_MODEL), MXU)]
    return pl.pallas_call(
        body, name="ffn_out_ln2", grid=(t // TM,),
        in_specs=[pl.BlockSpec((N_SHARD, TM, FF_BLK), lambda i: (0, i, 0)), _full(w_down.shape), tok(D_MODEL), vec, vec, vec, vec],
        out_specs=[tok(D_MODEL), tok(1), tok(D_MODEL)], out_shape=outs,
        compiler_params=_cp(dimension_semantics=("arbitrary",)),
    )(f, w_down, xhat1, ln1_g, ln1_b, ln2_g, ln2_b)


STAT_ROWS = 8


def _ple_loss_bwd(xhat2, rstd2, p, target, ln2_g, ln2_b, w_g, b_g, w_p, ln3_g, ln3_b):
    t = xhat2.shape[0]

    def body(xh2_ref, rs2_ref, p_ref, t_ref, g2_ref, b2_ref, wg_ref, bg_ref, wp_ref, g3_ref, b3_ref,
             dr2_ref, dgp_ref, dpp_ref, stat_ref, pp_scr):
        @pl.when(pl.program_id(0) == 0)
        def _():
            stat_ref[...] = jnp.zeros_like(stat_ref)

        xhat2 = xh2_ref[...]
        x2 = xhat2 * g2_ref[...] + b2_ref[...]
        gate = jax.nn.sigmoid(jnp.dot(x2.astype(MXU), wg_ref[...], preferred_element_type=F32) + bg_ref[...])
        pb = p_ref[...].astype(MXU)
        for j in range(N_SHARD):
            pp_scr[:, j * ROW_BLK:(j + 1) * ROW_BLK] = jnp.dot(pb, wp_ref[j], preferred_element_type=F32)
        pp = pp_scr[...]
        xhat3, rstd3 = _ln_fwd(ALPHA * x2 + gate * pp)
        err = xhat3 * g3_ref[...] + b3_ref[...] - t_ref[...]
        dy = err * (1.0 / D_MODEL)
        dr3 = _ln_bwd(dy, xhat3, rstd3, g3_ref[...])
        dgp = dr3 * pp * gate * (1.0 - gate)
        dgp_ref[...] = dgp.astype(MXU)
        dpp_ref[...] = (dr3 * gate).astype(MXU)
        dx2 = ALPHA * dr3 + _dot_nt(dgp, wg_ref[...])
        dr2_ref[...] = _ln_bwd(dx2, xhat2, rs2_ref[...], g2_ref[...])
        stat_ref[0:1, :] += _colsum(dy * xhat3)
        stat_ref[1:2, :] += _colsum(dy)
        stat_ref[2:3, :] += _colsum(dgp)
        stat_ref[3:4, :] += _colsum(dx2 * xhat2)
        stat_ref[4:5, :] += _colsum(dx2)
        stat_ref[5:6, :] += _colsum(err * err)

    tok = lambda w: pl.BlockSpec((TM, w), lambda i: (i, 0))
    vec = _full((1, D_MODEL))
    outs = [jax.ShapeDtypeStruct((t, D_MODEL), F32), jax.ShapeDtypeStruct((t, D_MODEL), MXU), jax.ShapeDtypeStruct((t, D_MODEL), MXU),
            jax.ShapeDtypeStruct((STAT_ROWS, D_MODEL), F32)]
    return pl.pallas_call(
        body, name="ple_loss_bwd", grid=(t // TM,),
        in_specs=[tok(D_MODEL), tok(1), tok(D_PLE), tok(D_MODEL), vec, vec, _full(w_g.shape), vec, _full(w_p.shape), vec, vec],
        out_specs=[tok(D_MODEL), tok(D_MODEL), tok(D_MODEL), _full((STAT_ROWS, D_MODEL))], out_shape=outs,
        scratch_shapes=[pltpu.VMEM((TM, D_MODEL), F32)],
        compiler_params=_cp(dimension_semantics=("arbitrary",)),
    )(xhat2, rstd2, p, target, ln2_g, ln2_b, w_g, b_g, w_p, ln3_g, ln3_b)


def _ffn_bwd(dr2, a_pre, b, w_down, w_a, w_b, conv_w, conv_b, xhat1, rstd1, ln1_g):
    t = dr2.shape[0]
    nt = t // TM
    hb = TM // HALO
    last_h = t // HALO - 1

    def body(dr_ref, drn_ref, ap_ref, app_ref, apn_ref, b_ref, bn_ref, wd_ref, wa_ref, wb_ref, cw_ref, cb_ref,
             xh_ref, rs_ref, g1_ref, dap_ref, dbb_ref, dr1_ref, cstat_ref, lstat_ref, acc_scr):
        i, j = pl.program_id(0), pl.program_id(1)

        @pl.when((i == 0) & (j == 0))
        def _():
            cstat_ref[...] = jnp.zeros_like(cstat_ref)
            lstat_ref[...] = jnp.zeros_like(lstat_ref)

        ext = TM + HALO
        dr_ext = jnp.concatenate([dr_ref[...], drn_ref[...]], axis=0)
        df = _dot_nt(dr_ext, wd_ref[...])
        a_all = jnp.concatenate([jnp.where(i > 0, app_ref[...], 0.0), ap_ref[...], apn_ref[...]], axis=0)
        a = _conv_fwd(a_all, cw_ref, cb_ref, ext)
        b_ext = jnp.concatenate([b_ref[...], bn_ref[...]], axis=0)
        row = lax.broadcasted_iota(jnp.int32, (ext, 1), 0)
        da = jnp.where((row < TM) | (i < nt - 1), df * b_ext * _gelu_grad(a), 0.0)
        dbb = df[0:TM] * _gelu(a[0:TM])
        da_pre = cw_ref[2:3, :] * da[0:TM] + cw_ref[1:2, :] * da[1:TM + 1] + cw_ref[0:1, :] * da[2:TM + 2]
        dap_ref[...] = da_pre.astype(MXU)
        dbb_ref[...] = dbb.astype(MXU)
        da_m = da[0:TM]
        for kk in range(3):
            cstat_ref[j, kk:kk + 1, :] += _colsum(da_m * a_all[HALO - 2 + kk:HALO - 2 + kk + TM])
        cstat_ref[j, 3:4, :] += _colsum(da_m)
        part = _dot_nt(da_pre, wa_ref[...]) + _dot_nt(dbb, wb_ref[...])

        @pl.when(j == 0)
        def _():
            acc_scr[...] = ALPHA * dr_ref[...] + part

        @pl.when(j > 0)
        def _():
            acc_scr[...] += part

        @pl.when(j == N_SHARD - 1)
        def _():
            dx1 = acc_scr[...]
            xhat1 = xh_ref[...]
            lstat_ref[0:1, :] += _colsum(dx1 * xhat1)
            lstat_ref[1:2, :] += _colsum(dx1)
            dr1_ref[...] = _ln_bwd(dx1, xhat1, rs_ref[...], g1_ref[...])

    tok = lambda w: pl.BlockSpec((TM, w), lambda i, j: (i, 0))
    tokj = pl.BlockSpec((None, TM, FF_BLK), lambda i, j: (j, i, 0))
    prevj = pl.BlockSpec((None, HALO, FF_BLK), lambda i, j: (j, jnp.maximum(i * hb - 1, 0), 0))
    nextj = pl.BlockSpec((None, HALO, FF_BLK), lambda i, j: (j, jnp.minimum((i + 1) * hb, last_h), 0))
    blk = lambda r, c: pl.BlockSpec((None, r, c), lambda i, j: (j, 0, 0))
    outs = [jax.ShapeDtypeStruct((N_SHARD, t, FF_BLK), MXU)] * 2 + [
        jax.ShapeDtypeStruct((t, D_MODEL), F32), jax.ShapeDtypeStruct((N_SHARD, STAT_ROWS, FF_BLK), F32),
        jax.ShapeDtypeStruct((STAT_ROWS, D_MODEL), F32)]
    return pl.pallas_call(
        body, name="ffn_bwd", grid=(nt, N_SHARD),
        in_specs=[tok(D_MODEL), pl.BlockSpec((HALO, D_MODEL), lambda i, j: (jnp.minimum((i + 1) * hb, last_h), 0)),
                  tokj, prevj, nextj, tokj, nextj, blk(FF_BLK, D_MODEL), blk(D_MODEL, FF_BLK), blk(D_MODEL, FF_BLK),
                  blk(3, FF_BLK), blk(1, FF_BLK), tok(D_MODEL), tok(1), _full((1, D_MODEL))],
        out_specs=[tokj, tokj, tok(D_MODEL), _full((N_SHARD, STAT_ROWS, FF_BLK)), _full((STAT_ROWS, D_MODEL))], out_shape=outs,
        scratch_shapes=[pltpu.VMEM((TM, D_MODEL), F32)],
        compiler_params=_cp(dimension_semantics=("arbitrary", "arbitrary")),
    )(dr2, dr2, a_pre, a_pre, a_pre, b, b, w_down, w_a, w_b, conv_w, conv_b, xhat1, rstd1, ln1_g)


def _mix_bwd(dr1, w_o, hu, hz, mixed, ln_z_g, ln_z_b, w_s):
    t = dr1.shape[0]
    nchunk = TM // BLK

    def body(dr_ref, wo_ref, hu_ref, hz_ref, mx_ref, g_ref, b_ref, ws_ref, grp_ref,
             dattn_ref, duz_ref, dws_ref, dbs_ref, zstat_ref, wm_scr, dzn_scr, dbsum_scr):
        @pl.when(pl.program_id(0) == 0)
        def _():
            row = lax.broadcasted_iota(jnp.int32, (BLK, BLK), 0)
            col = lax.broadcasted_iota(jnp.int32, (BLK, BLK), 1)
            for g in range(N_HEADS):
                wm_scr[g] = jnp.where(col <= row, ws_ref[g], 0.0).astype(MXU)
            dws_ref[...] = jnp.zeros_like(dws_ref)
            dbsum_scr[...] = jnp.zeros_like(dbsum_scr)
            zstat_ref[...] = jnp.zeros_like(zstat_ref)

        dcat = _dot_nt(dr_ref[...], wo_ref[...])
        dattn_ref[...] = dcat[:, 0:D_ATTN]
        dgm = dcat[:, D_ATTN:]
        hu, hz = hu_ref[...], hz_ref[...]
        u = _gelu(hu)
        duz_ref[:, 0:D_GMLP] = (dgm * mx_ref[...] * _gelu_grad(hu)).astype(MXU)
        dmixed = dgm * u
        dmb = dmixed.astype(MXU)
        zhat, rstd = _ln_fwd(_gelu(hz))
        znb = (zhat * g_ref[...] + b_ref[...]).astype(MXU)
        dbs_acc = jnp.zeros((BLK, D_GMLP), F32)
        for ch in range(nchunk):
            rows = slice(ch * BLK, (ch + 1) * BLK)
            dbs_acc = dbs_acc + dmixed[rows]
            for g in range(N_HEADS):
                cols = slice(g * HEAD_DIM, (g + 1) * HEAD_DIM)
                dzn_scr[rows, cols] = _dot_tn(wm_scr[g], dmb[rows, cols])
                dws_ref[g] += _dot_nt(dmb[rows, cols], znb[rows, cols])
        dbsum_scr[...] += dbs_acc
        dzn = dzn_scr[...]
        zstat_ref[0:1, :] += _colsum(dzn * zhat)
        zstat_ref[1:2, :] += _colsum(dzn)
        duz_ref[:, D_GMLP:] = (_ln_bwd(dzn, zhat, rstd, g_ref[...]) * _gelu_grad(hz)).astype(MXU)

        @pl.when(pl.program_id(0) == nt - 1)
        def _():
            row = lax.broadcasted_iota(jnp.int32, (BLK, BLK), 0)
            col = lax.broadcasted_iota(jnp.int32, (BLK, BLK), 1)
            for g in range(N_HEADS):
                dws_ref[g] = jnp.where(col <= row, dws_ref[g], 0.0)
            dbs_ref[...] = lax.dot_general(grp_ref[...], dbsum_scr[...], (((1,), (1,)), ((), ())),
                                           precision=lax.Precision.HIGHEST, preferred_element_type=F32)

    nt = t // TM
    tok = lambda w: pl.BlockSpec((TM, w), lambda i: (i, 0))
    grp = jnp.asarray((np.arange(D_GMLP)[None, :] // HEAD_DIM == np.arange(N_HEADS)[:, None]).astype(np.float32))
    outs = [jax.ShapeDtypeStruct((t, D_ATTN), F32), jax.ShapeDtypeStruct((t, 2 * D_GMLP), MXU),
            jax.ShapeDtypeStruct((N_HEADS, BLK, BLK), F32), jax.ShapeDtypeStruct((N_HEADS, BLK), F32),
            jax.ShapeDtypeStruct((STAT_ROWS, D_GMLP), F32)]
    return pl.pallas_call(
        body, name="mix_bwd", grid=(t // TM,),
        in_specs=[tok(D_MODEL), _full(w_o.shape), tok(D_GMLP), tok(D_GMLP), tok(D_GMLP), _full(ln_z_g.shape), _full(ln_z_b.shape),
                  _full(w_s.shape), _full(grp.shape)],
        out_specs=[tok(D_ATTN), tok(2 * D_GMLP), _full((N_HEADS, BLK, BLK)), _full((N_HEADS, BLK)), _full((STAT_ROWS, D_GMLP))],
        out_shape=outs,
        scratch_shapes=[pltpu.VMEM((N_HEADS, BLK, BLK), MXU), pltpu.VMEM((TM, D_GMLP), F32), pltpu.VMEM((BLK, D_GMLP), F32)],
        compiler_params=_cp(dimension_semantics=("arbitrary",)),
    )(dr1, w_o, hu, hz, mixed, ln_z_g, ln_z_b, w_s, grp)


def _dx_in(dqs, dks, dvs, duz, dr1, w_in, c_tab, s1_tab, s2_tab):
    t = dr1.shape[0]

    def body(dq1, dq2, dq3, dk1, dk2, dk3, dv1, dv2, dv3, duz_ref, dr_ref, w_ref, c_ref, s1_ref, s2_ref, dh_ref, dx_ref):
        c, s1, s2 = _tile_heads(c_ref[...]), _tile_heads(s1_ref[...]), _tile_heads(s2_ref[...])
        dq = (dq1[...] + dq2[...] + dq3[...]) * (1.0 / math.sqrt(HEAD_DIM))
        dh_ref[:, 0:D_ATTN] = _rope_apply_t(dq, c, s1, s2).astype(MXU)
        dh_ref[:, D_ATTN:2 * D_ATTN] = _rope_apply_t(dk1[...] + dk2[...] + dk3[...], c, s1, s2).astype(MXU)
        dh_ref[:, 2 * D_ATTN:3 * D_ATTN] = (dv1[...] + dv2[...] + dv3[...]).astype(MXU)
        dh_ref[:, 3 * D_ATTN:] = duz_ref[...]
        dx = ALPHA * dr_ref[...]
        for j in range(N_SHARD):
            dx = dx + _dot_nt(dh_ref[:, j * W_IN_BLK:(j + 1) * W_IN_BLK], w_ref[j])
        dx_ref[...] = dx

    tok = lambda w: pl.BlockSpec((TM, w), lambda i: (i, 0))
    outs = [jax.ShapeDtypeStruct((t, D_IN), MXU), jax.ShapeDtypeStruct((t, D_MODEL), F32)]
    return pl.pallas_call(
        body, name="dx_in", grid=(t // TM,),
        in_specs=[tok(D_ATTN)] * 9 + [tok(2 * D_GMLP), tok(D_MODEL), _full(w_in.shape), tok(128), tok(128), tok(128)],
        out_specs=[tok(D_IN), tok(D_MODEL)], out_shape=outs,
        compiler_params=_cp(dimension_semantics=("arbitrary",)),
    )(*dqs, *dks, *dvs, duz, dr1, w_in, c_tab, s1_tab, s2_tab)


def _wgrad(name, x, dy, x_spec, dy_spec, out_spec, out_shape, grid):
    def body(x_ref, dy_ref, o_ref):
        o_ref[...] = _dot_tn(x_ref[...], dy_ref[...])

    return pl.pallas_call(
        body, name=name, grid=grid, in_specs=[x_spec, dy_spec], out_specs=out_spec,
        out_shape=jax.ShapeDtypeStruct(out_shape, F32),
        compiler_params=_cp(dimension_semantics=("arbitrary",) * len(grid)),
    )(x, dy)


def _local_step(x, p, positions, target, w_in, w_o, w_a, w_b, conv_w, w_down, w_g, w_p,
                ln_z_g, ln_z_b, w_s, b_s, ln1_g, ln1_b, conv_b, ln2_g, ln2_b, b_g, ln3_g, ln3_b):
    t = x.shape[0]
    half = TM
    c_tab, s1_tab, s2_tab = _rope_tables(positions, t)
    b_full = jnp.repeat(jnp.transpose(b_s[0]), HEAD_DIM, axis=1)
    conv_b4 = conv_b.reshape(N_SHARD, 1, FF_BLK)
    q, k, v, hu, hz, mixed, gm = _qkvuz(x, w_in, c_tab, s1_tab, s2_tab, ln_z_g, ln_z_b, w_s[0], b_full)
    branches = [_attn_fwd(q, k, v, d) for d in DILATIONS]
    attn, lse, cat, xhat1, rstd1, x1b = _mix_ln1([o for o, _ in branches], [l for _, l in branches], gm, x, w_o, ln1_g, ln1_b)
    a_pre, b_act, f = _ffn_in(x1b, w_a, w_b, conv_w, conv_b4)
    xhat2, rstd2, x2b = _ffn_out_ln2(f, w_down, xhat1, ln1_g, ln1_b, ln2_g, ln2_b)
    dr2, dgp, dpp, stat3 = _ple_loss_bwd(xhat2, rstd2, p, target, ln2_g, ln2_b, w_g, b_g, w_p, ln3_g, ln3_b)
    da_pre, dbb, dr1, cstat, stat1 = _ffn_bwd(dr2, a_pre, b_act, w_down, w_a, w_b, conv_w, conv_b4, xhat1, rstd1, ln1_g)
    dattn, duz, dws, dbs, zstat = _mix_bwd(dr1, w_o, hu, hz, mixed, ln_z_g, ln_z_b, w_s[0])
    dqs = [_attn_bwd_dq(q, k, v, dattn, attn, lse, d) for d in DILATIONS]
    dkvs = [_attn_bwd_dkv(q, k, v, dattn, attn, lse, d) for d in DILATIONS]
    dh, grad_x = _dx_in(dqs, [a for a, _ in dkvs], [b_ for _, b_ in dkvs], duz, dr1, w_in, c_tab, s1_tab, s2_tab)

    full_t = lambda w, im: pl.BlockSpec((t, w), im)
    g_w_in = _wgrad("dw_in", x, dh, full_t(half, lambda j, kk: (0, kk)), full_t(W_IN_BLK, lambda j, kk: (0, j)),
                    pl.BlockSpec((None, half, W_IN_BLK), lambda j, kk: (j, kk, 0)), (N_SHARD, D_MODEL, W_IN_BLK), (N_SHARD, 2))
    g_w_o = _wgrad("dw_o", cat, dr1, full_t(half, lambda kk, n: (0, kk)), full_t(half, lambda kk, n: (0, n)),
                   pl.BlockSpec((half, half), lambda kk, n: (kk, n)), (D_MODEL, D_MODEL), (2, 2))
    ffj = pl.BlockSpec((None, t, FF_BLK), lambda j, kk: (j, 0, 0))
    g_w_a = _wgrad("dw_a", x1b, da_pre, full_t(half, lambda j, kk: (0, kk)), ffj,
                   pl.BlockSpec((None, half, FF_BLK), lambda j, kk: (j, kk, 0)), (N_SHARD, D_MODEL, FF_BLK), (N_SHARD, 2))
    g_w_b = _wgrad("dw_b", x1b, dbb, full_t(half, lambda j, kk: (0, kk)), ffj,
                   pl.BlockSpec((None, half, FF_BLK), lambda j, kk: (j, kk, 0)), (N_SHARD, D_MODEL, FF_BLK), (N_SHARD, 2))
    g_w_down = _wgrad("dw_down", f, dr2, ffj, full_t(half, lambda j, n: (0, n)),
                      pl.BlockSpec((None, FF_BLK, half), lambda j, n: (j, 0, n)), (N_SHARD, FF_BLK, D_MODEL), (N_SHARD, 2))
    g_w_g = _wgrad("dw_g", x2b, dgp, full_t(half, lambda kk, n: (0, kk)), full_t(half, lambda kk, n: (0, n)),
                   pl.BlockSpec((half, half), lambda kk, n: (kk, n)), (D_MODEL, D_MODEL), (2, 2))
    g_w_p = _wgrad("dw_p", p, dpp, full_t(D_PLE, lambda j: (0, 0)), full_t(ROW_BLK, lambda j: (0, j)),
                   pl.BlockSpec((None, D_PLE, ROW_BLK), lambda j: (j, 0, 0)), (N_SHARD, D_PLE, ROW_BLK), (N_SHARD,))

    big = dict(w_in=g_w_in, w_o=g_w_o, w_ff_a=g_w_a, w_ff_b=g_w_b, w_ff_down=g_w_down, w_ple_gate=g_w_g, w_ple_in=g_w_p)
    small = dict(
        ln3_g=stat3[0:1], ln3_b=stat3[1:2], b_ple_gate=stat3[2:3], ln2_g=stat3[3:4], ln2_b=stat3[4:5],
        ln1_g=stat1[0:1], ln1_b=stat1[1:2], ln_z_g=zstat[0:1], ln_z_b=zstat[1:2],
        conv_w=cstat[:, 0:3, :], conv_b=cstat[:, 3, :].reshape(1, D_FF),
        w_s=dws, b_s=dbs)
    return grad_x, big, small, (stat3, stat1, zstat, cstat)


def _rows_tile(r, mult, cap=512):
    return max(d for d in range(mult, min(r, cap) + 1, mult) if r % d == 0)


def _grid_spec(grid, in_specs, out_specs):
    return pltpu.PrefetchScalarGridSpec(num_scalar_prefetch=1, grid=grid, in_specs=in_specs, out_specs=out_specs)


def _place_shard(name, w, chip, dtype):
    r, c = w.shape
    tr = r if r % 16 else _rows_tile(r, 16)

    def body(s_ref, w_ref, o_ref):
        o_ref[...] = w_ref[...].astype(dtype)

    return pl.pallas_call(
        body, name=name,
        grid_spec=_grid_spec((r // tr,), [pl.BlockSpec((tr, c), lambda i, s: (i, 0))],
                             pl.BlockSpec((None, tr, c), lambda i, s: (s[0], i, 0))),
        out_shape=jax.ShapeDtypeStruct((N_SHARD, r, c), dtype), compiler_params=_cp())(chip, w)


def _pair_sum_bf16(name, mine, got, core):
    n, h, c = got.shape
    tr = _rows_tile(h, 16)
    nh = h // tr

    def body(s_ref, a_ref, b_ref, o_ref):
        o_ref[...] = (a_ref[...] + b_ref[...]).astype(BF16)

    spec = pl.BlockSpec((None, tr, c), lambda k, i, s: (k, i, 0))
    return pl.pallas_call(
        body, name=name,
        grid_spec=_grid_spec((n, nh), [pl.BlockSpec((None, tr, c), lambda k, i, s: (k, s[0] * nh + i, 0)), spec], spec),
        out_shape=jax.ShapeDtypeStruct((n, h, c), BF16), compiler_params=_cp())(core, mine, got)


def _chip_sum(name, own, landed, place):
    n, h, c = own.shape
    tr = _rows_tile(h, 16)
    nh = h // tr

    def body(s_ref, a_ref, b_ref, c_ref, d_ref, o_ref):
        o_ref[...] = ((a_ref[...].astype(F32) + b_ref[...].astype(F32)) + c_ref[...].astype(F32)) + d_ref[...].astype(F32)

    def slot(d):
        return pl.BlockSpec((None, tr, c), lambda i, s: ((s[0] + d) % n, i, 0))

    return pl.pallas_call(
        body, name=name,
        grid_spec=_grid_spec((nh,), [slot(0), slot(1), slot(2), slot(3)], pl.BlockSpec((tr, c), lambda i, s: (s[1] * nh + i, 0))),
        out_shape=jax.ShapeDtypeStruct((2 * h, c), F32), compiler_params=_cp())(place, own, landed, landed, landed)


def _adamw_math(w, g, m, v):
    m = ADAM_B1 * m + (1.0 - ADAM_B1) * g
    v = ADAM_B2 * v + (1.0 - ADAM_B2) * (g * g)
    m_hat = m / (1.0 - ADAM_B1 ** ADAM_STEP)
    v_hat = v / (1.0 - ADAM_B2 ** ADAM_STEP)
    delta = -ADAM_LR * (m_hat / (jnp.sqrt(v_hat) + ADAM_EPS) + ADAM_WD * w)
    return delta, m, v


def _adamw_big(name, w, g, m, v):
    _, r, c = w.shape
    tr = _rows_tile(r, 8, cap=256)

    def body(w_ref, g_ref, m_ref, v_ref, d_ref, nm_ref, nv_ref):
        d_ref[...], nm_ref[...], nv_ref[...] = _adamw_math(w_ref[...], g_ref[...], m_ref[...], v_ref[...])

    s3 = pl.BlockSpec((None, tr, c), lambda i: (0, i, 0))
    s2 = pl.BlockSpec((tr, c), lambda i: (i, 0))
    return pl.pallas_call(body, name=name, grid=(r // tr,), in_specs=[s3, s2, s3, s3], out_specs=[s3, s3, s3],
                          out_shape=[jax.ShapeDtypeStruct(w.shape, F32)] * 3, compiler_params=_cp())(w, g, m, v)


MESH = pl.DeviceIdType.MESH
ANY = pl.BlockSpec(memory_space=pl.ANY)


def _place():
    x, y, c = lax.axis_index("x"), lax.axis_index("y"), lax.axis_index("c")
    chips = [(1 - x, y), (x, 1 - y), (1 - x, 1 - y)]
    return x, y, c, 2 * x + y, chips


def _remote(src, dst, send_sem, recv_sem, dev):
    return pltpu.make_async_remote_copy(src_ref=src, dst_ref=dst, send_sem=send_sem, recv_sem=recv_sem,
                                        device_id=dev, device_id_type=MESH)


def _half(ref, hc, rows):
    return ref.at[pl.ds(hc * (rows // 2), rows // 2)]


def _allgather(stacks, split):
    n = len(stacks)

    def body(*refs):
        outs = refs[n:2 * n]
        send, recv, fsend, frecv = refs[2 * n:]
        x, y, c, j, chips = _place()
        rows = [s.shape[1] for s in stacks]

        def piece(a, slot, hc):
            return _half(outs[a].at[slot], hc, rows[a]) if split[a] else outs[a].at[slot]

        def direct(a, t, slot, dev):
            return _remote(piece(a, slot, c), piece(a, slot, c), send.at[a, t], recv.at[a, t], dev)

        sends = [direct(a, t, j, (*chips[t], c)) for a in range(n) for t in range(3)]
        for cp in sends:
            cp.start()
        fwd = []
        for t, (px, py) in enumerate(chips):
            jt = 2 * px + py
            for a in range(n):
                direct(a, t, jt, (px, py, c)).wait_recv()
                if split[a]:
                    cp = _remote(piece(a, jt, c), piece(a, jt, c), fsend.at[a, t], frecv.at[a, t], (x, y, 1 - c))
                    cp.start()
                    fwd.append(cp)
        for t, (px, py) in enumerate(chips):
            jt = 2 * px + py
            for a in range(n):
                if split[a]:
                    _remote(piece(a, jt, 1 - c), piece(a, jt, 1 - c), fsend.at[a, t], frecv.at[a, t], (x, y, 1 - c)).wait_recv()
        for cp in sends + fwd:
            cp.wait_send()

    sem = pltpu.SemaphoreType.DMA
    return pl.pallas_call(
        body, name="allgather_weights", in_specs=[ANY] * n, out_specs=[ANY] * n,
        out_shape=[jax.ShapeDtypeStruct(s.shape, s.dtype) for s in stacks],
        input_output_aliases={a: a for a in range(n)},
        scratch_shapes=[sem((n, 3)), sem((n, 3)), sem((n, 3)), sem((n, 3))],
    )(*stacks)


def _sibling_swap(grads):
    n = len(grads)

    def body(*refs):
        ins, got = refs[:n], refs[n:2 * n]
        send, recv = refs[2 * n:]
        x, y, c, _, _ = _place()
        cps = []
        for a in range(n):
            h = grads[a].shape[1] // 2
            cp = _remote(ins[a].at[:, pl.ds((1 - c) * h, h)], got[a], send.at[a], recv.at[a], (x, y, 1 - c))
            cp.start()
            cps.append(cp)
        for cp in cps:
            cp.wait_recv()
            cp.wait_send()

    sem = pltpu.SemaphoreType.DMA
    halves = [jax.ShapeDtypeStruct((g.shape[0], g.shape[1] // 2, g.shape[2]), g.dtype) for g in grads]
    return pl.pallas_call(body, name="rs_sibling_swap", in_specs=[ANY] * n, out_specs=[ANY] * n, out_shape=halves,
                          scratch_shapes=[sem((n,)), sem((n,))])(*grads)


def _chip_exchange(parts):
    n = len(parts)

    def body(*refs):
        ins, outs = refs[:n], refs[n:2 * n]
        send, recv = refs[2 * n:]
        x, y, c, j, chips = _place()
        sends = []
        for t, (px, py) in enumerate(chips):
            jt = 2 * px + py
            for a in range(n):
                cp = _remote(ins[a].at[jt], outs[a].at[j], send.at[a, t], recv.at[a, t], (px, py, c))
                cp.start()
                sends.append(cp)
        for t, (px, py) in enumerate(chips):
            jt = 2 * px + py
            for a in range(n):
                _remote(ins[a].at[jt], outs[a].at[jt], send.at[a, t], recv.at[a, t], (px, py, c)).wait_recv()
        for cp in sends:
            cp.wait_send()

    sem = pltpu.SemaphoreType.DMA
    return pl.pallas_call(body, name="rs_chip_exchange", in_specs=[ANY] * n, out_specs=[ANY] * n,
                          out_shape=[jax.ShapeDtypeStruct(p.shape, p.dtype) for p in parts],
                          scratch_shapes=[sem((n, 3)), sem((n, 3))])(*parts)


def _sibling_join(blocks):
    n = len(blocks)

    def body(*refs):
        outs = refs[n:2 * n]
        send, recv = refs[2 * n:]
        x, y, c, _, _ = _place()
        cps = []
        for a in range(n):
            h = blocks[a].shape[0] // 2
            mine = outs[a].at[pl.ds(c * h, h)]
            cp = _remote(mine, mine, send.at[a], recv.at[a], (x, y, 1 - c))
            cp.start()
            cps.append(cp)
        for a, cp in enumerate(cps):
            h = blocks[a].shape[0] // 2
            theirs = outs[a].at[pl.ds((1 - c) * h, h)]
            _remote(theirs, theirs, send.at[a], recv.at[a], (x, y, 1 - c)).wait_recv()
            cp.wait_send()

    sem = pltpu.SemaphoreType.DMA
    return pl.pallas_call(body, name="rs_sibling_join", in_specs=[ANY] * n, out_specs=[ANY] * n,
                          out_shape=[jax.ShapeDtypeStruct(b_.shape, b_.dtype) for b_ in blocks],
                          input_output_aliases={a: a for a in range(n)},
                          scratch_shapes=[sem((n,)), sem((n,))])(*blocks)


def _allreduce_small(arrs):
    n = len(arrs)

    def body(*refs):
        ins, outs = refs[:n], refs[n:2 * n]
        sib, chip = refs[2 * n:3 * n], refs[3 * n:4 * n]
        ssend, srecv, csend, crecv = refs[4 * n:]
        x, y, c, j, chips = _place()
        swaps = [_remote(ins[a], sib[a], ssend.at[a], srecv.at[a], (x, y, 1 - c)) for a in range(n)]
        for cp in swaps:
            cp.start()
        sends = []
        for a in range(n):
            swaps[a].wait_recv()
            chip[a][j] = ins[a][...] + sib[a][...]
            for t, (px, py) in enumerate(chips):
                cp = _remote(chip[a].at[j], chip[a].at[j], csend.at[a, t], crecv.at[a, t], (px, py, c))
                cp.start()
                sends.append(cp)
        for a in range(n):
            for t, (px, py) in enumerate(chips):
                jt = 2 * px + py
                _remote(chip[a].at[jt], chip[a].at[jt], csend.at[a, t], crecv.at[a, t], (px, py, c)).wait_recv()
            outs[a][...] = ((chip[a][0] + chip[a][1]) + chip[a][2]) + chip[a][3]
        for cp in swaps + sends:
            cp.wait_send()

    sem = pltpu.SemaphoreType.DMA
    vm = pl.BlockSpec(memory_space=pltpu.VMEM)
    return pl.pallas_call(
        body, name="allreduce_small", in_specs=[vm] * n, out_specs=[vm] * n,
        out_shape=[jax.ShapeDtypeStruct(a.shape, F32) for a in arrs],
        scratch_shapes=[pltpu.VMEM(a.shape, F32) for a in arrs] + [pltpu.VMEM((N_SHARD, *a.shape), F32) for a in arrs]
        + [sem((n,)), sem((n,)), sem((n, 3)), sem((n, 3))],
        compiler_params=_cp(),
    )(*arrs)


def _reduce_scatter(grads, place):
    names = list(grads)
    core = place[1:2]
    got = _sibling_swap([grads[k] for k in names])
    pair = [_pair_sum_bf16(f"rs_pair_{k}", grads[k], g, core) for k, g in zip(names, got)]
    landed = _chip_exchange(pair)
    blocks = [_chip_sum(f"rs_sum_{k}", own, l, place) for k, own, l in zip(names, pair, landed)]
    return dict(zip(names, _sibling_join(blocks)))


SMALL_1024 = ("ln1_g", "ln1_b", "ln2_g", "ln2_b", "b_ple_gate", "ln3_g", "ln3_b")


def _adamw_small(red3, red1, redz, g_conv_w, redc, red_ws, red_bs, params):
    shape2d = {"ln_z_g": (1, D_GMLP), "ln_z_b": (1, D_GMLP), "w_s": (N_HEADS * BLK, BLK), "b_s": (N_HEADS, BLK),
               "conv_w": (3, FF_BLK), "conv_b": (N_SHARD, FF_BLK), **{k: (1, D_MODEL) for k in SMALL_1024}}
    names = list(shape2d)
    flat = [a.reshape(shape2d[k]) for k in names for a in params[k]]

    def body(r3, r1, rz, gcw, rc, rws, rbs, *refs):
        ins, outs = refs[:3 * len(names)], refs[3 * len(names):]

        def grad_of(k):
            if k == "w_s":
                return rws[...]
            if k == "b_s":
                return rbs[...]
            if k == "conv_w":
                return gcw[0:3, :]
            if k == "conv_b":
                return jnp.concatenate([rc[j * STAT_ROWS + 3:j * STAT_ROWS + 4, :] for j in range(N_SHARD)], axis=0)
            src, row = {"ln3_g": (r3, 0), "ln3_b": (r3, 1), "b_ple_gate": (r3, 2), "ln2_g": (r3, 3), "ln2_b": (r3, 4),
                        "ln1_g": (r1, 0), "ln1_b": (r1, 1), "ln_z_g": (rz, 0), "ln_z_b": (rz, 1)}[k]
            return src[row:row + 1, :]

        for i, k in enumerate(names):
            w_ref, m_ref, v_ref = ins[3 * i:3 * i + 3]
            g_ref, d_ref, nm_ref, nv_ref = outs[4 * i:4 * i + 4]
            g = grad_of(k)
            g_ref[...] = g
            d_ref[...], nm_ref[...], nv_ref[...] = _adamw_math(w_ref[...], g, m_ref[...], v_ref[...])

    res = pl.pallas_call(
        body, name="adamw_small",
        out_shape=[jax.ShapeDtypeStruct(shape2d[k], F32) for k in names for _ in range(4)],
        compiler_params=_cp(),
    )(red3, red1, redz, g_conv_w, redc, red_ws, red_bs, *flat)
    return {k: tuple(r.reshape(params[k][0].shape) for r in res[4 * i:4 * i + 4]) for i, k in enumerate(names)}


WEIGHTS = ("w_in", "ln_z_g", "ln_z_b", "w_s", "b_s", "w_o", "ln1_g", "ln1_b", "w_ff_a", "w_ff_b", "conv_w", "conv_b",
           "w_ff_down", "ln2_g", "ln2_b", "w_ple_gate", "b_ple_gate", "w_ple_in", "ln3_g", "ln3_b")
BIG = ("w_in", "w_o", "w_ff_a", "w_ff_b", "w_ff_down", "w_ple_gate", "w_ple_in")


def kernel(x, p, positions, w_in, ln_z_g, ln_z_b, w_s, b_s, w_o, ln1_g, ln1_b, w_ff_a, w_ff_b, conv_w, conv_b, w_ff_down, ln2_g, ln2_b, w_ple_gate, b_ple_gate, w_ple_in, ln3_g, ln3_b, loss_target, m_w_in, m_ln_z_g, m_ln_z_b, m_w_s, m_b_s, m_w_o, m_ln1_g, m_ln1_b, m_w_ff_a, m_w_ff_b, m_conv_w, m_conv_b, m_w_ff_down, m_ln2_g, m_ln2_b, m_w_ple_gate, m_b_ple_gate, m_w_ple_in, m_ln3_g, m_ln3_b, v_w_in, v_ln_z_g, v_ln_z_b, v_w_s, v_b_s, v_w_o, v_ln1_g, v_ln1_b, v_w_ff_a, v_w_ff_b, v_conv_w, v_conv_b, v_w_ff_down, v_ln2_g, v_ln2_b, v_w_ple_gate, v_b_ple_gate, v_w_ple_in, v_ln3_g, v_ln3_b):
    args = locals()
    w = {k: args[k] for k in WEIGHTS}
    m = {k: args["m_" + k] for k in WEIGHTS}
    v = {k: args["v_" + k] for k in WEIGHTS}

    chip = 2 * lax.axis_index("x") + lax.axis_index("y")
    place = jnp.stack([chip, lax.axis_index("c")]).astype(jnp.int32)
    stacks = [_place_shard(f"cast_{k}", w[k][0], place, MXU) for k in BIG] + [_place_shard("place_conv_w", w["conv_w"][0], place, F32)]
    full = _allgather(stacks, [True] * len(BIG) + [False])
    fw = dict(zip(BIG + ("conv_w",), full))

    grad_x, big, small, (stat3, stat1, zstat, cstat) = _local_step(
        x[0], p[0, 0], positions, loss_target[0], fw["w_in"], fw["w_o"].reshape(D_MODEL, D_MODEL), fw["w_ff_a"], fw["w_ff_b"],
        fw["conv_w"], fw["w_ff_down"], fw["w_ple_gate"].reshape(D_MODEL, D_MODEL), fw["w_ple_in"],
        ln_z_g, ln_z_b, w_s, b_s, ln1_g, ln1_b, conv_b, ln2_g, ln2_b, b_ple_gate, ln3_g, ln3_b)

    stacked = {k: big[k].reshape(N_SHARD, *w[k].shape[1:]) for k in BIG}
    red = _reduce_scatter(stacked, place)
    out = {}
    for k in BIG:
        d, nm, nv = _adamw_big(f"adamw_{k}", w[k], red[k], m[k], v[k])
        out[k] = (red[k].reshape(w[k].shape), d, nm, nv)

    red3, red1, redz, redc, red_ws, red_bs = _allreduce_small(
        [stat3, stat1, zstat, cstat.reshape(N_SHARD * STAT_ROWS, FF_BLK), small["w_s"].reshape(N_HEADS * BLK, BLK), small["b_s"]])
    loss = (0.5 / D_MODEL) * jnp.sum(red3[5])
    g_conv_w = lax.dynamic_slice_in_dim(redc, chip * STAT_ROWS, STAT_ROWS, 0)
    names_small = [k for k in WEIGHTS if k not in BIG]
    out.update(_adamw_small(red3, red1, redz, g_conv_w, redc, red_ws, red_bs, {k: (w[k], m[k], v[k]) for k in names_small}))

    return (loss, grad_x[None], *[out[k][0] for k in WEIGHTS], *[out[k][1] for k in WEIGHTS],
            *[out[k][2] for k in WEIGHTS], *[out[k][3] for k in WEIGHTS])
```

```python
import functools
import math

import numpy as np
import jax
import jax.numpy as jnp
from jax import lax
from jax.experimental import pallas as pl
from jax.experimental.pallas import tpu as pltpu

F32 = jnp.float32
BF16 = jnp.bfloat16
MXU = BF16

D_MODEL = 1024
HEAD_DIM = 64
N_HEADS = 8
D_ATTN = 512
D_GMLP = 512
D_IN = 2560
DILATIONS = (1, 4, 16)
BLK = 128
ROPE_THETA = 500000.0
ROPE_DIM = 16
D_FF = 2816
D_PLE = 256
LN_EPS = 1e-5
ALPHA = 2.0 ** 0.25
NEG_INF = -1e30
N_SHARD = 4
W_IN_BLK = D_IN // N_SHARD
FF_BLK = D_FF // N_SHARD
ROW_BLK = D_MODEL // N_SHARD
ADAM_LR, ADAM_B1, ADAM_B2, ADAM_EPS, ADAM_WD, ADAM_STEP = 0.001, 0.9, 0.999, 1e-08, 0.01, 10

TM = 512
HALO = 8
VMEM_LIMIT = 56 * 1024 * 1024


def _cp(**kw):
    return pltpu.CompilerParams(vmem_limit_bytes=VMEM_LIMIT, **kw)


def _full(shape):
    n = len(shape)
    return pl.BlockSpec(shape, lambda *_: (0,) * n)


def _gelu(x):
    return 0.5 * x * (1.0 + lax.erf(x * (1.0 / math.sqrt(2.0))))


def _gelu_grad(x):
    return 0.5 * (1.0 + lax.erf(x * (1.0 / math.sqrt(2.0)))) + x * jnp.exp(-0.5 * x * x) * (1.0 / math.sqrt(2.0 * math.pi))


def _ln_fwd(r):
    mu = jnp.mean(r, axis=-1, keepdims=True)
    xc = r - mu
    var = jnp.mean(xc * xc, axis=-1, keepdims=True)
    rstd = lax.rsqrt(var + LN_EPS)
    return xc * rstd, rstd


def _ln_bwd(dy, xhat, rstd, g):
    dxh = dy * g
    m1 = jnp.mean(dxh, axis=-1, keepdims=True)
    m2 = jnp.mean(dxh * xhat, axis=-1, keepdims=True)
    return rstd * (dxh - m1 - xhat * m2)


def _dot(a, b):
    return jnp.dot(a.astype(MXU), b.astype(MXU), preferred_element_type=F32)


def _dot_nt(a, b):
    return lax.dot_general(a.astype(MXU), b.astype(MXU), (((1,), (1,)), ((), ())), preferred_element_type=F32)


def _dot_tn(a, b):
    return lax.dot_general(a.astype(MXU), b.astype(MXU), (((0,), (0,)), ((), ())), preferred_element_type=F32)


def _colsum(v):
    return jnp.sum(v, axis=0, keepdims=True)


def _rope_tables(positions, t):
    inv = np.float32(ROPE_THETA) ** (-np.arange(0, ROPE_DIM, 2, dtype=np.float32) / np.float32(ROPE_DIM))
    half = ROPE_DIM // 2
    pos_rep = jnp.repeat(positions.reshape(t // 16, 16), half, axis=1)
    inv_row = jnp.asarray(np.tile(inv, 16)[None, :], F32)

    def trig_body(pos_ref, inv_ref, cos_ref, sin_ref):
        ang = pos_ref[...].astype(F32) * inv_ref[...]
        cos_ref[...] = jnp.cos(ang)
        sin_ref[...] = jnp.sin(ang)

    cos8, sin8 = pl.pallas_call(
        trig_body, name="rope_trig",
        out_shape=(jax.ShapeDtypeStruct((t // 16, 128), F32), jax.ShapeDtypeStruct((t // 16, 128), F32)),
    )(pos_rep, inv_row)
    cos8 = cos8.reshape(t, half)
    sin8 = sin8.reshape(t, half)

    lane = np.arange(128) % HEAD_DIM
    sel = (np.arange(half)[:, None] == (lane % half)[None, :])
    e_cos = (sel & (lane < ROPE_DIM)[None, :]).astype(np.float32)
    e_s1 = -(sel & (lane < half)[None, :]).astype(np.float32)
    e_s2 = (sel & ((lane >= half) & (lane < ROPE_DIM))[None, :]).astype(np.float32)
    ones = (lane >= ROPE_DIM).astype(np.float32)[None, :]

    def expand_body(cos_ref, sin_ref, ec_ref, e1_ref, e2_ref, ones_ref, c_ref, s1_ref, s2_ref):
        hp = lax.Precision.HIGHEST
        c_ref[...] = jnp.dot(cos_ref[...], ec_ref[...], precision=hp, preferred_element_type=F32) + ones_ref[...]
        s1_ref[...] = jnp.dot(sin_ref[...], e1_ref[...], precision=hp, preferred_element_type=F32)
        s2_ref[...] = jnp.dot(sin_ref[...], e2_ref[...], precision=hp, preferred_element_type=F32)

    tab = jax.ShapeDtypeStruct((t, 128), F32)
    return pl.pallas_call(expand_body, name="rope_expand", out_shape=(tab, tab, tab), compiler_params=_cp())(
        cos8, sin8, jnp.asarray(e_cos), jnp.asarray(e_s1), jnp.asarray(e_s2), jnp.asarray(ones))


def _tile_heads(tab):
    return jnp.concatenate([tab] * (D_ATTN // 128), axis=1)


def _rope_apply(v, c, s1, s2):
    n = v.shape[1]
    half = ROPE_DIM // 2
    return v * c + pltpu.roll(v, n - half, 1) * s1 + pltpu.roll(v, half, 1) * s2


def _rope_apply_t(g, c, s1, s2):
    n = g.shape[1]
    half = ROPE_DIM // 2
    return g * c + pltpu.roll(g * s1, half, 1) + pltpu.roll(g * s2, n - half, 1)


LANE_CHUNKS = D_ATTN // 128
HEAD_LANES = 128 // N_HEADS


def _perm_shape(t, d, w, dtype):
    return jax.ShapeDtypeStruct((d, t // d, w), dtype)


def _perm_tile(d, w):
    return pl.BlockSpec((None if d == 1 else d, TM // d, w), lambda i: (0, i, 0))


def _to_planes(ref, scr, d, n_chunks, dtype):
    for r in range(d):
        for cc in range(n_chunks):
            ref[r, :, cc * 128:(cc + 1) * 128] = scr.at[cc][pl.ds(r, TM // d, stride=d), :].astype(dtype)


def _from_planes(ref, scr, d, n_chunks, accumulate=False):
    for r in range(d):
        for cc in range(n_chunks):
            rows = scr.at[cc]
            val = ref[r, :, cc * 128:(cc + 1) * 128].astype(F32)
            if accumulate:
                rows[pl.ds(r, TM // d, stride=d), :] += val
            else:
                rows[pl.ds(r, TM // d, stride=d), :] = val


def _chunks(val):
    return [val[:, cc * 128:(cc + 1) * 128] for cc in range(val.shape[1] // 128)]


def _unchunk(scr, n_chunks, base=0):
    return jnp.concatenate([scr[base + cc] for cc in range(n_chunks)], axis=1)


def _head_expand():
    src = np.arange(128)[:, None]
    dst = np.arange(D_ATTN)[None, :]
    return jnp.asarray((src == (dst // HEAD_DIM) * HEAD_LANES).astype(np.float32))


def _head_reduce():
    src = np.arange(D_ATTN)[:, None]
    dst = np.arange(128)[None, :]
    return jnp.asarray((src // HEAD_DIM == dst // HEAD_LANES).astype(np.float32))


def _dot_exact(a, b):
    return jnp.dot(a, b, precision=lax.Precision.HIGHEST, preferred_element_type=F32)


def _qkvuz(x, w_in, c_tab, s1_tab, s2_tab, ln_z_g, ln_z_b, w_s, b_full):
    t = x.shape[0]
    nchunk = TM // BLK

    def body(x_ref, w_ref, c_ref, s1_ref, s2_ref, g_ref, b_ref, ws_ref, bf_ref,
             qkv1_ref, qkv4_ref, qkv16_ref, hu_ref, hz_ref, mixed_ref, gm_ref, h_scr, wm_scr, p_scr):
        @pl.when(pl.program_id(0) == 0)
        def _():
            row = lax.broadcasted_iota(jnp.int32, (BLK, BLK), 0)
            col = lax.broadcasted_iota(jnp.int32, (BLK, BLK), 1)
            for g in range(N_HEADS):
                wm_scr[g] = jnp.where(col <= row, ws_ref[g], 0.0).astype(MXU)

        xb = x_ref[...].astype(MXU)
        for j in range(N_SHARD):
            h_scr[:, j * W_IN_BLK:(j + 1) * W_IN_BLK] = jnp.dot(xb, w_ref[j], preferred_element_type=F32)
        c, s1, s2 = _tile_heads(c_ref[...]), _tile_heads(s1_ref[...]), _tile_heads(s2_ref[...])
        q = _rope_apply(h_scr[:, 0:D_ATTN], c, s1, s2) * (1.0 / math.sqrt(HEAD_DIM))
        k = _rope_apply(h_scr[:, D_ATTN:2 * D_ATTN], c, s1, s2)
        for part, val in enumerate((q, k, h_scr[:, 2 * D_ATTN:3 * D_ATTN])):
            qkv1_ref[:, part * D_ATTN:(part + 1) * D_ATTN] = val.astype(MXU)
            for cc in range(LANE_CHUNKS):
                p_scr[part * LANE_CHUNKS + cc] = val[:, cc * 128:(cc + 1) * 128]
        _to_planes(qkv4_ref, p_scr, DILATIONS[1], 3 * LANE_CHUNKS, MXU)
        _to_planes(qkv16_ref, p_scr, DILATIONS[2], 3 * LANE_CHUNKS, MXU)
        hu = h_scr[:, 3 * D_ATTN:3 * D_ATTN + D_GMLP]
        hz = h_scr[:, 3 * D_ATTN + D_GMLP:]
        hu_ref[...] = hu
        hz_ref[...] = hz
        zhat, _ = _ln_fwd(_gelu(hz))
        zn = (zhat * g_ref[...] + b_ref[...]).astype(MXU)
        for ch in range(nchunk):
            rows = slice(ch * BLK, (ch + 1) * BLK)
            for g in range(N_HEADS):
                cols = slice(g * HEAD_DIM, (g + 1) * HEAD_DIM)
                mixed_ref[rows, cols] = jnp.dot(wm_scr[g], zn[rows, cols], preferred_element_type=F32) + bf_ref[:, cols]
        gm_ref[...] = (_gelu(hu) * mixed_ref[...]).astype(MXU)

    tok = lambda w: pl.BlockSpec((TM, w), lambda i: (i, 0))
    outs = [_perm_shape(t, d, 3 * D_ATTN, MXU) for d in DILATIONS] + [jax.ShapeDtypeStruct((t, D_GMLP), F32)] * 3 + [
        jax.ShapeDtypeStruct((t, D_GMLP), MXU)]
    return pl.pallas_call(
        body, name="qkvuz", grid=(t // TM,),
        in_specs=[tok(D_MODEL), _full(w_in.shape), tok(128), tok(128), tok(128), _full(ln_z_g.shape), _full(ln_z_b.shape),
                  _full(w_s.shape), _full(b_full.shape)],
        out_specs=[_perm_tile(d, 3 * D_ATTN) for d in DILATIONS] + [tok(D_ATTN)] * 4, out_shape=outs,
        scratch_shapes=[pltpu.VMEM((TM, D_IN), F32), pltpu.VMEM((N_HEADS, BLK, BLK), MXU),
                        pltpu.VMEM((3 * LANE_CHUNKS, TM, 128), F32)],
        compiler_params=_cp(dimension_semantics=("arbitrary",)),
    )(x, w_in, c_tab, s1_tab, s2_tab, ln_z_g, ln_z_b, w_s, b_full)


def _band_valid(n):
    i = lax.broadcasted_iota(jnp.int32, (BLK, 2 * BLK), 0)
    j = lax.broadcasted_iota(jnp.int32, (BLK, 2 * BLK), 1)
    return (j >= i) & (j <= i + BLK) & ((j >= BLK) | (n > 0))


def _attn_fwd(qkv, d):
    _, l_sub, _ = qkv.shape
    nb = l_sub // BLK

    def body(q_ref, kp_ref, kc_ref, vp_ref, vc_ref, o_ref, l_ref):
        valid = _band_valid(pl.program_id(1))
        kcat = jnp.concatenate([kp_ref[...], kc_ref[...]], axis=0)
        vcat = jnp.concatenate([vp_ref[...], vc_ref[...]], axis=0)
        for h in range(N_HEADS):
            cols = slice(h * HEAD_DIM, (h + 1) * HEAD_DIM)
            s = jnp.where(valid, _dot_nt(q_ref[:, cols], kcat[:, cols]), NEG_INF)
            m = jnp.max(s, axis=-1, keepdims=True)
            e = jnp.exp(s - m)
            den = jnp.sum(e, axis=-1, keepdims=True)
            o_ref[:, cols] = _dot(e, vcat[:, cols]) * (1.0 / den)
            l_ref[:, h * HEAD_LANES:(h + 1) * HEAD_LANES] = jnp.broadcast_to(m + jnp.log(den), (BLK, HEAD_LANES))

    def blk(w, col, prev=False):
        return pl.BlockSpec((None, BLK, w), lambda r, n: (r, jnp.maximum(n - 1, 0) if prev else n, col))

    return pl.pallas_call(
        body, name=f"attn_fwd_d{d}", grid=(d, nb),
        in_specs=[blk(D_ATTN, 0), blk(D_ATTN, 1, True), blk(D_ATTN, 1), blk(D_ATTN, 2, True), blk(D_ATTN, 2)],
        out_specs=[blk(D_ATTN, 0), blk(128, 0)],
        out_shape=[jax.ShapeDtypeStruct((d, l_sub, D_ATTN), F32), jax.ShapeDtypeStruct((d, l_sub, 128), F32)],
        compiler_params=_cp(dimension_semantics=("arbitrary", "arbitrary")),
    )(qkv, qkv, qkv, qkv, qkv)


def _attn_bwd(qkv, do, lse, delta, d):
    _, l_sub, _ = qkv.shape
    nb = l_sub // BLK

    def body(q_ref, kp_ref, kc_ref, vp_ref, vc_ref, do_ref, l_ref, dl_ref, dq_ref, dk_ref, dv_ref, ck_scr, cv_scr):
        n = pl.program_id(1)

        @pl.when(n == 0)
        def _():
            ck_scr[...] = jnp.zeros_like(ck_scr)
            cv_scr[...] = jnp.zeros_like(cv_scr)

        @pl.when(n < nb)
        def _():
            valid = _band_valid(n)
            kcat = jnp.concatenate([kp_ref[...], kc_ref[...]], axis=0)
            vcat = jnp.concatenate([vp_ref[...], vc_ref[...]], axis=0)
            for h in range(N_HEADS):
                cols = slice(h * HEAD_DIM, (h + 1) * HEAD_DIM)
                stat = slice(h * HEAD_LANES, h * HEAD_LANES + 1)
                qh, doh = q_ref[:, cols], do_ref[:, cols]
                p = jnp.where(valid, jnp.exp(_dot_nt(qh, kcat[:, cols]) - l_ref[:, stat]), 0.0)
                ds = p * (_dot_nt(doh, vcat[:, cols]) - dl_ref[:, stat])
                dq_ref[:, cols] = _dot(ds, kcat[:, cols])
                dk2 = _dot_tn(ds, qh)
                dv2 = _dot_tn(p, doh)
                dk_ref[:, cols] = ck_scr[:, cols] + dk2[0:BLK]
                dv_ref[:, cols] = cv_scr[:, cols] + dv2[0:BLK]
                ck_scr[:, cols] = dk2[BLK:]
                cv_scr[:, cols] = dv2[BLK:]

        @pl.when(n == nb)
        def _():
            dk_ref[...] = ck_scr[...]
            dv_ref[...] = cv_scr[...]

    def blk(w, col, shift=0):
        return pl.BlockSpec((None, BLK, w), lambda r, n: (r, jnp.clip(n - shift, 0, nb - 1), col))

    return pl.pallas_call(
        body, name=f"attn_bwd_d{d}", grid=(d, nb + 1),
        in_specs=[blk(D_ATTN, 0), blk(D_ATTN, 1, 1), blk(D_ATTN, 1), blk(D_ATTN, 2, 1), blk(D_ATTN, 2),
                  blk(D_ATTN, 0), blk(128, 0), blk(128, 0)],
        out_specs=[blk(D_ATTN, 0), blk(D_ATTN, 0, 1), blk(D_ATTN, 0, 1)],
        out_shape=[jax.ShapeDtypeStruct((d, l_sub, D_ATTN), F32)] * 3,
        scratch_shapes=[pltpu.VMEM((BLK, D_ATTN), F32)] * 2,
        compiler_params=_cp(dimension_semantics=("arbitrary", "arbitrary")),
    )(qkv, qkv, qkv, qkv, qkv, do, lse, delta)


def _mix_ln1(os_, ls_, gm, x, w_o, ln1_g, ln1_b):
    t = x.shape[0]
    expand = _head_expand()

    def body(o1, o4, o16, l1, l4, l16, gm_ref, x_ref, wo_ref, g_ref, b_ref, ex_ref,
             attn_ref, lse1_ref, lse4_ref, lse16_ref, cat_ref, xhat_ref, rstd_ref, x1b_ref, o_scr, l_scr):
        _from_planes(o4, o_scr, DILATIONS[1], LANE_CHUNKS)
        _from_planes(o16, o_scr.at[pl.ds(LANE_CHUNKS, LANE_CHUNKS)], DILATIONS[2], LANE_CHUNKS)
        _from_planes(l4, l_scr, DILATIONS[1], 1)
        _from_planes(l16, l_scr.at[pl.ds(1, 1)], DILATIONS[2], 1)
        la, lb, lc = l1[...], l_scr[0], l_scr[1]
        m = jnp.maximum(jnp.maximum(la, lb), lc)
        ea, eb, ec = jnp.exp(la - m), jnp.exp(lb - m), jnp.exp(lc - m)
        den = ea + eb + ec
        inv = 1.0 / den
        wide = lambda w: _dot_exact(w, ex_ref[...])
        attn = (wide(ea * inv) * o1[...] + wide(eb * inv) * _unchunk(o_scr, LANE_CHUNKS)
                + wide(ec * inv) * _unchunk(o_scr, LANE_CHUNKS, LANE_CHUNKS))
        attn_ref[...] = attn
        lse = m + jnp.log(den)
        lse1_ref[...] = lse
        l_scr[2] = lse
        _to_planes(lse4_ref, l_scr.at[pl.ds(2, 1)], DILATIONS[1], 1, F32)
        _to_planes(lse16_ref, l_scr.at[pl.ds(2, 1)], DILATIONS[2], 1, F32)
        cat_ref[:, 0:D_ATTN] = attn.astype(MXU)
        cat_ref[:, D_ATTN:] = gm_ref[...]
        mix = jnp.dot(cat_ref[...], wo_ref[...], preferred_element_type=F32)
        xhat, rstd = _ln_fwd(ALPHA * x_ref[...] + mix)
        xhat_ref[...] = xhat
        rstd_ref[...] = rstd
        x1b_ref[...] = (xhat * g_ref[...] + b_ref[...]).astype(MXU)

    tok = lambda w: pl.BlockSpec((TM, w), lambda i: (i, 0))
    outs = [jax.ShapeDtypeStruct((t, D_ATTN), F32)] + [_perm_shape(t, d, 128, F32) for d in DILATIONS] + [
        jax.ShapeDtypeStruct((t, D_MODEL), MXU), jax.ShapeDtypeStruct((t, D_MODEL), F32), jax.ShapeDtypeStruct((t, 1), F32),
        jax.ShapeDtypeStruct((t, D_MODEL), MXU)]
    return pl.pallas_call(
        body, name="mix_ln1", grid=(t // TM,),
        in_specs=[_perm_tile(d, D_ATTN) for d in DILATIONS] + [_perm_tile(d, 128) for d in DILATIONS]
        + [tok(D_GMLP), tok(D_MODEL), _full(w_o.shape), _full(ln1_g.shape), _full(ln1_b.shape), _full(expand.shape)],
        out_specs=[tok(D_ATTN)] + [_perm_tile(d, 128) for d in DILATIONS] + [tok(D_MODEL), tok(D_MODEL), tok(1), tok(D_MODEL)],
        out_shape=outs,
        scratch_shapes=[pltpu.VMEM((2 * LANE_CHUNKS, TM, 128), F32), pltpu.VMEM((3, TM, 128), F32)],
        compiler_params=_cp(dimension_semantics=("arbitrary",)),
    )(*os_, *ls_, gm, x, w_o, ln1_g, ln1_b, expand)


def _conv_fwd(a_ext, w_ref, b_ref, rows):
    return (b_ref[...] + w_ref[2:3, :] * a_ext[HALO:HALO + rows] + w_ref[1:2, :] * a_ext[HALO - 1:HALO - 1 + rows]
            + w_ref[0:1, :] * a_ext[HALO - 2:HALO - 2 + rows])


def _ffn_in(x1b, w_a, w_b, conv_w, conv_b):
    t = x1b.shape[0]
    hb = TM // HALO

    def body(x_ref, xh_ref, wa_ref, wb_ref, cw_ref, cb_ref, apre_ref, b_ref, f_ref):
        i = pl.program_id(1)
        a_pre = jnp.dot(x_ref[...], wa_ref[...], preferred_element_type=F32)
        a_halo = jnp.dot(xh_ref[...], wa_ref[...], preferred_element_type=F32)
        a_halo = jnp.where(i > 0, a_halo, 0.0)
        a = _conv_fwd(jnp.concatenate([a_halo, a_pre], axis=0), cw_ref, cb_ref, TM)
        b = jnp.dot(x_ref[...], wb_ref[...], preferred_element_type=F32)
        apre_ref[...] = a_pre
        b_ref[...] = b
        f_ref[...] = (_gelu(a) * b).astype(MXU)

    blk = lambda r, c: pl.BlockSpec((None, r, c), lambda j, i: (j, 0, 0))
    tokj = pl.BlockSpec((None, TM, FF_BLK), lambda j, i: (j, i, 0))
    outs = [jax.ShapeDtypeStruct((N_SHARD, t, FF_BLK), F32)] * 2 + [jax.ShapeDtypeStruct((N_SHARD, t, FF_BLK), MXU)]
    return pl.pallas_call(
        body, name="ffn_in", grid=(N_SHARD, t // TM),
        in_specs=[pl.BlockSpec((TM, D_MODEL), lambda j, i: (i, 0)),
                  pl.BlockSpec((HALO, D_MODEL), lambda j, i: (jnp.maximum(i * hb - 1, 0), 0)),
                  blk(D_MODEL, FF_BLK), blk(D_MODEL, FF_BLK), blk(3, FF_BLK), blk(1, FF_BLK)],
        out_specs=[tokj, tokj, tokj], out_shape=outs,
        compiler_params=_cp(dimension_semantics=("arbitrary", "arbitrary")),
    )(x1b, x1b, w_a, w_b, conv_w, conv_b)


def _ffn_out_ln2(f, w_down, xhat1, ln1_g, ln1_b, ln2_g, ln2_b):
    t = xhat1.shape[0]

    def body(f_ref, wd_ref, xh_ref, g1_ref, b1_ref, g2_ref, b2_ref, xhat_ref, rstd_ref, x2b_ref):
        ff = jnp.dot(f_ref[0], wd_ref[0], preferred_element_type=F32)
        for j in range(1, N_SHARD):
            ff = ff + jnp.dot(f_ref[j], wd_ref[j], preferred_element_type=F32)
        x1 = xh_ref[...] * g1_ref[...] + b1_ref[...]
        xhat, rstd = _ln_fwd(ALPHA * x1 + ff)
        xhat_ref[...] = xhat
        rstd_ref[...] = rstd
        x2b_ref[...] = (xhat * g2_ref[...] + b2_ref[...]).astype(MXU)

    tok = lambda w: pl.BlockSpec((TM, w), lambda i: (i, 0))
    vec = _full((1, D_MODEL))
    outs = [jax.ShapeDtypeStruct((t, D_MODEL), F32), jax.ShapeDtypeStruct((t, 1), F32), jax.ShapeDtypeStruct((t, D_MODEL), MXU)]
    return pl.pallas_call(
        body, name="ffn_out_ln2", grid=(t // TM,),
        in_specs=[pl.BlockSpec((N_SHARD, TM, FF_BLK), lambda i: (0, i, 0)), _full(w_down.shape), tok(D_MODEL), vec, vec, vec, vec],
        out_specs=[tok(D_MODEL), tok(1), tok(D_MODEL)], out_shape=outs,
        compiler_params=_cp(dimension_semantics=("arbitrary",)),
    )(f, w_down, xhat1, ln1_g, ln1_b, ln2_g, ln2_b)


STAT_ROWS = 8


def _ple_loss_bwd(xhat2, rstd2, p, target, ln2_g, ln2_b, w_g, b_g, w_p, ln3_g, ln3_b):
    t = xhat2.shape[0]

    def body(xh2_ref, rs2_ref, p_ref, t_ref, g2_ref, b2_ref, wg_ref, bg_ref, wp_ref, g3_ref, b3_ref,
             dr2_ref, dgp_ref, dpp_ref, stat_ref, pp_scr):
        @pl.when(pl.program_id(0) == 0)
        def _():
            stat_ref[...] = jnp.zeros_like(stat_ref)

        xhat2 = xh2_ref[...]
        x2 = xhat2 * g2_ref[...] + b2_ref[...]
        gate = jax.nn.sigmoid(jnp.dot(x2.astype(MXU), wg_ref[...], preferred_element_type=F32) + bg_ref[...])
        pb = p_ref[...].astype(MXU)
        for j in range(N_SHARD):
            pp_scr[:, j * ROW_BLK:(j + 1) * ROW_BLK] = jnp.dot(pb, wp_ref[j], preferred_element_type=F32)
        pp = pp_scr[...]
        xhat3, rstd3 = _ln_fwd(ALPHA * x2 + gate * pp)
        err = xhat3 * g3_ref[...] + b3_ref[...] - t_ref[...]
        dy = err * (1.0 / D_MODEL)
        dr3 = _ln_bwd(dy, xhat3, rstd3, g3_ref[...])
        dgp = dr3 * pp * gate * (1.0 - gate)
        dgp_ref[...] = dgp.astype(MXU)
        dpp_ref[...] = (dr3 * gate).astype(MXU)
        dx2 = ALPHA * dr3 + _dot_nt(dgp, wg_ref[...])
        dr2_ref[...] = _ln_bwd(dx2, xhat2, rs2_ref[...], g2_ref[...])
        stat_ref[0:1, :] += _colsum(dy * xhat3)
        stat_ref[1:2, :] += _colsum(dy)
        stat_ref[2:3, :] += _colsum(dgp)
        stat_ref[3:4, :] += _colsum(dx2 * xhat2)
        stat_ref[4:5, :] += _colsum(dx2)
        stat_ref[5:6, :] += _colsum(err * err)

    tok = lambda w: pl.BlockSpec((TM, w), lambda i: (i, 0))
    vec = _full((1, D_MODEL))
    outs = [jax.ShapeDtypeStruct((t, D_MODEL), F32), jax.ShapeDtypeStruct((t, D_MODEL), MXU), jax.ShapeDtypeStruct((t, D_MODEL), MXU),
            jax.ShapeDtypeStruct((STAT_ROWS, D_MODEL), F32)]
    return pl.pallas_call(
        body, name="ple_loss_bwd", grid=(t // TM,),
        in_specs=[tok(D_MODEL), tok(1), tok(D_PLE), tok(D_MODEL), vec, vec, _full(w_g.shape), vec, _full(w_p.shape), vec, vec],
        out_specs=[tok(D_MODEL), tok(D_MODEL), tok(D_MODEL), _full((STAT_ROWS, D_MODEL))], out_shape=outs,
        scratch_shapes=[pltpu.VMEM((TM, D_MODEL), F32)],
        compiler_params=_cp(dimension_semantics=("arbitrary",)),
    )(xhat2, rstd2, p, target, ln2_g, ln2_b, w_g, b_g, w_p, ln3_g, ln3_b)


def _ffn_bwd(dr2, a_pre, b, w_down, w_a, w_b, conv_w, conv_b, xhat1, rstd1, ln1_g):
    t = dr2.shape[0]
    nt = t // TM
    hb = TM // HALO
    last_h = t // HALO - 1

    def body(dr_ref, drn_ref, ap_ref, app_ref, apn_ref, b_ref, bn_ref, wd_ref, wa_ref, wb_ref, cw_ref, cb_ref,
             xh_ref, rs_ref, g1_ref, dap_ref, dbb_ref, dr1_ref, cstat_ref, lstat_ref, acc_scr):
        i, j = pl.program_id(0), pl.program_id(1)

        @pl.when((i == 0) & (j == 0))
        def _():
            cstat_ref[...] = jnp.zeros_like(cstat_ref)
            lstat_ref[...] = jnp.zeros_like(lstat_ref)

        ext = TM + HALO
        dr_ext = jnp.concatenate([dr_ref[...], drn_ref[...]], axis=0)
        df = _dot_nt(dr_ext, wd_ref[...])
        a_all = jnp.concatenate([jnp.where(i > 0, app_ref[...], 0.0), ap_ref[...], apn_ref[...]], axis=0)
        a = _conv_fwd(a_all, cw_ref, cb_ref, ext)
        b_ext = jnp.concatenate([b_ref[...], bn_ref[...]], axis=0)
        row = lax.broadcasted_iota(jnp.int32, (ext, 1), 0)
        da = jnp.where((row < TM) | (i < nt - 1), df * b_ext * _gelu_grad(a), 0.0)
        dbb = df[0:TM] * _gelu(a[0:TM])
        da_pre = cw_ref[2:3, :] * da[0:TM] + cw_ref[1:2, :] * da[1:TM + 1] + cw_ref[0:1, :] * da[2:TM + 2]
        dap_ref[...] = da_pre.astype(MXU)
        dbb_ref[...] = dbb.astype(MXU)
        da_m = da[0:TM]
        for kk in range(3):
            cstat_ref[j, kk:kk + 1, :] += _colsum(da_m * a_all[HALO - 2 + kk:HALO - 2 + kk + TM])
        cstat_ref[j, 3:4, :] += _colsum(da_m)
        part = _dot_nt(da_pre, wa_ref[...]) + _dot_nt(dbb, wb_ref[...])

        @pl.when(j == 0)
        def _():
            acc_scr[...] = ALPHA * dr_ref[...] + part

        @pl.when(j > 0)
        def _():
            acc_scr[...] += part

        @pl.when(j == N_SHARD - 1)
        def _():
            dx1 = acc_scr[...]
            xhat1 = xh_ref[...]
            lstat_ref[0:1, :] += _colsum(dx1 * xhat1)
            lstat_ref[1:2, :] += _colsum(dx1)
            dr1_ref[...] = _ln_bwd(dx1, xhat1, rs_ref[...], g1_ref[...])

    tok = lambda w: pl.BlockSpec((TM, w), lambda i, j: (i, 0))
    tokj = pl.BlockSpec((None, TM, FF_BLK), lambda i, j: (j, i, 0))
    prevj = pl.BlockSpec((None, HALO, FF_BLK), lambda i, j: (j, jnp.maximum(i * hb - 1, 0), 0))
    nextj = pl.BlockSpec((None, HALO, FF_BLK), lambda i, j: (j, jnp.minimum((i + 1) * hb, last_h), 0))
    blk = lambda r, c: pl.BlockSpec((None, r, c), lambda i, j: (j, 0, 0))
    outs = [jax.ShapeDtypeStruct((N_SHARD, t, FF_BLK), MXU)] * 2 + [
        jax.ShapeDtypeStruct((t, D_MODEL), F32), jax.ShapeDtypeStruct((N_SHARD, STAT_ROWS, FF_BLK), F32),
        jax.ShapeDtypeStruct((STAT_ROWS, D_MODEL), F32)]
    return pl.pallas_call(
        body, name="ffn_bwd", grid=(nt, N_SHARD),
        in_specs=[tok(D_MODEL), pl.BlockSpec((HALO, D_MODEL), lambda i, j: (jnp.minimum((i + 1) * hb, last_h), 0)),
                  tokj, prevj, nextj, tokj, nextj, blk(FF_BLK, D_MODEL), blk(D_MODEL, FF_BLK), blk(D_MODEL, FF_BLK),
                  blk(3, FF_BLK), blk(1, FF_BLK), tok(D_MODEL), tok(1), _full((1, D_MODEL))],
        out_specs=[tokj, tokj, tok(D_MODEL), _full((N_SHARD, STAT_ROWS, FF_BLK)), _full((STAT_ROWS, D_MODEL))], out_shape=outs,
        scratch_shapes=[pltpu.VMEM((TM, D_MODEL), F32)],
        compiler_params=_cp(dimension_semantics=("arbitrary", "arbitrary")),
    )(dr2, dr2, a_pre, a_pre, a_pre, b, b, w_down, w_a, w_b, conv_w, conv_b, xhat1, rstd1, ln1_g)


def _mix_bwd(dr1, w_o, hu, hz, mixed, attn, ln_z_g, ln_z_b, w_s):
    t = dr1.shape[0]
    nchunk = TM // BLK

    def body(dr_ref, wo_ref, hu_ref, hz_ref, mx_ref, attn_ref, g_ref, b_ref, ws_ref, grp_ref, red_ref,
             do1_ref, do4_ref, do16_ref, dl1_ref, dl4_ref, dl16_ref, duz_ref, dws_ref, dbs_ref, zstat_ref,
             wm_scr, dzn_scr, dbsum_scr, do_scr, dl_scr):
        @pl.when(pl.program_id(0) == 0)
        def _():
            row = lax.broadcasted_iota(jnp.int32, (BLK, BLK), 0)
            col = lax.broadcasted_iota(jnp.int32, (BLK, BLK), 1)
            for g in range(N_HEADS):
                wm_scr[g] = jnp.where(col <= row, ws_ref[g], 0.0).astype(MXU)
            dws_ref[...] = jnp.zeros_like(dws_ref)
            dbsum_scr[...] = jnp.zeros_like(dbsum_scr)
            zstat_ref[...] = jnp.zeros_like(zstat_ref)

        dcat = _dot_nt(dr_ref[...], wo_ref[...])
        dattn = dcat[:, 0:D_ATTN]
        do1_ref[...] = dattn.astype(MXU)
        for cc, val in enumerate(_chunks(dattn)):
            do_scr[cc] = val
        _to_planes(do4_ref, do_scr, DILATIONS[1], LANE_CHUNKS, MXU)
        _to_planes(do16_ref, do_scr, DILATIONS[2], LANE_CHUNKS, MXU)
        delta = _dot_exact(dattn * attn_ref[...], red_ref[...])
        dl1_ref[...] = delta
        dl_scr[0] = delta
        _to_planes(dl4_ref, dl_scr, DILATIONS[1], 1, F32)
        _to_planes(dl16_ref, dl_scr, DILATIONS[2], 1, F32)
        dgm = dcat[:, D_ATTN:]
        hu, hz = hu_ref[...], hz_ref[...]
        u = _gelu(hu)
        duz_ref[:, 0:D_GMLP] = (dgm * mx_ref[...] * _gelu_grad(hu)).astype(MXU)
        dmixed = dgm * u
        dmb = dmixed.astype(MXU)
        zhat, rstd = _ln_fwd(_gelu(hz))
        znb = (zhat * g_ref[...] + b_ref[...]).astype(MXU)
        dbs_acc = jnp.zeros((BLK, D_GMLP), F32)
        for ch in range(nchunk):
            rows = slice(ch * BLK, (ch + 1) * BLK)
            dbs_acc = dbs_acc + dmixed[rows]
            for g in range(N_HEADS):
                cols = slice(g * HEAD_DIM, (g + 1) * HEAD_DIM)
                dzn_scr[rows, cols] = _dot_tn(wm_scr[g], dmb[rows, cols])
                dws_ref[g] += _dot_nt(dmb[rows, cols], znb[rows, cols])
        dbsum_scr[...] += dbs_acc
        dzn = dzn_scr[...]
        zstat_ref[0:1, :] += _colsum(dzn * zhat)
        zstat_ref[1:2, :] += _colsum(dzn)
        duz_ref[:, D_GMLP:] = (_ln_bwd(dzn, zhat, rstd, g_ref[...]) * _gelu_grad(hz)).astype(MXU)

        @pl.when(pl.program_id(0) == nt - 1)
        def _():
            row = lax.broadcasted_iota(jnp.int32, (BLK, BLK), 0)
            col = lax.broadcasted_iota(jnp.int32, (BLK, BLK), 1)
            for g in range(N_HEADS):
                dws_ref[g] = jnp.where(col <= row, dws_ref[g], 0.0)
            dbs_ref[...] = lax.dot_general(grp_ref[...], dbsum_scr[...], (((1,), (1,)), ((), ())),
                                           precision=lax.Precision.HIGHEST, preferred_element_type=F32)

    nt = t // TM
    tok = lambda w: pl.BlockSpec((TM, w), lambda i: (i, 0))
    grp = jnp.asarray((np.arange(D_GMLP)[None, :] // HEAD_DIM == np.arange(N_HEADS)[:, None]).astype(np.float32))
    red = _head_reduce()
    outs = [_perm_shape(t, d, D_ATTN, MXU) for d in DILATIONS] + [_perm_shape(t, d, 128, F32) for d in DILATIONS] + [
        jax.ShapeDtypeStruct((t, 2 * D_GMLP), MXU),
        jax.ShapeDtypeStruct((N_HEADS, BLK, BLK), F32), jax.ShapeDtypeStruct((N_HEADS, BLK), F32),
        jax.ShapeDtypeStruct((STAT_ROWS, D_GMLP), F32)]
    return pl.pallas_call(
        body, name="mix_bwd", grid=(t // TM,),
        in_specs=[tok(D_MODEL), _full(w_o.shape), tok(D_GMLP), tok(D_GMLP), tok(D_GMLP), tok(D_ATTN), _full(ln_z_g.shape),
                  _full(ln_z_b.shape), _full(w_s.shape), _full(grp.shape), _full(red.shape)],
        out_specs=[_perm_tile(d, D_ATTN) for d in DILATIONS] + [_perm_tile(d, 128) for d in DILATIONS]
        + [tok(2 * D_GMLP), _full((N_HEADS, BLK, BLK)), _full((N_HEADS, BLK)), _full((STAT_ROWS, D_GMLP))],
        out_shape=outs,
        scratch_shapes=[pltpu.VMEM((N_HEADS, BLK, BLK), MXU), pltpu.VMEM((TM, D_GMLP), F32), pltpu.VMEM((BLK, D_GMLP), F32),
                        pltpu.VMEM((LANE_CHUNKS, TM, 128), F32), pltpu.VMEM((1, TM, 128), F32)],
        compiler_params=_cp(dimension_semantics=("arbitrary",)),
    )(dr1, w_o, hu, hz, mixed, attn, ln_z_g, ln_z_b, w_s, grp, red)


def _dx_in(dqs, dks, dvs, duz, dr1, w_in, c_tab, s1_tab, s2_tab):
    t = dr1.shape[0]

    def body(dq1, dq4, dq16, dk1, dk4, dk16, dv1, dv4, dv16, duz_ref, dr_ref, w_ref, c_ref, s1_ref, s2_ref,
             dh_ref, dx_ref, acc_scr):
        sums = []
        for part, (g1, g4, g16) in enumerate(((dq1, dq4, dq16), (dk1, dk4, dk16), (dv1, dv4, dv16))):
            acc = acc_scr.at[pl.ds(part * LANE_CHUNKS, LANE_CHUNKS)]
            for cc in range(LANE_CHUNKS):
                acc[cc] = g1[:, cc * 128:(cc + 1) * 128]
            _from_planes(g4, acc, DILATIONS[1], LANE_CHUNKS, accumulate=True)
            _from_planes(g16, acc, DILATIONS[2], LANE_CHUNKS, accumulate=True)
            sums.append(_unchunk(acc_scr, LANE_CHUNKS, part * LANE_CHUNKS))
        c, s1, s2 = _tile_heads(c_ref[...]), _tile_heads(s1_ref[...]), _tile_heads(s2_ref[...])
        dh_ref[:, 0:D_ATTN] = _rope_apply_t(sums[0] * (1.0 / math.sqrt(HEAD_DIM)), c, s1, s2).astype(MXU)
        dh_ref[:, D_ATTN:2 * D_ATTN] = _rope_apply_t(sums[1], c, s1, s2).astype(MXU)
        dh_ref[:, 2 * D_ATTN:3 * D_ATTN] = sums[2].astype(MXU)
        dh_ref[:, 3 * D_ATTN:] = duz_ref[...]
        dx = ALPHA * dr_ref[...]
        for j in range(N_SHARD):
            dx = dx + _dot_nt(dh_ref[:, j * W_IN_BLK:(j + 1) * W_IN_BLK], w_ref[j])
        dx_ref[...] = dx

    tok = lambda w: pl.BlockSpec((TM, w), lambda i: (i, 0))
    outs = [jax.ShapeDtypeStruct((t, D_IN), MXU), jax.ShapeDtypeStruct((t, D_MODEL), F32)]
    return pl.pallas_call(
        body, name="dx_in", grid=(t // TM,),
        in_specs=[_perm_tile(d, D_ATTN) for d in DILATIONS] * 3
        + [tok(2 * D_GMLP), tok(D_MODEL), _full(w_in.shape), tok(128), tok(128), tok(128)],
        out_specs=[tok(D_IN), tok(D_MODEL)], out_shape=outs,
        scratch_shapes=[pltpu.VMEM((3 * LANE_CHUNKS, TM, 128), F32)],
        compiler_params=_cp(dimension_semantics=("arbitrary",)),
    )(*dqs, *dks, *dvs, duz, dr1, w_in, c_tab, s1_tab, s2_tab)


def _wgrad(name, x, dy, x_spec, dy_spec, out_spec, out_shape, grid):
    def body(x_ref, dy_ref, o_ref):
        o_ref[...] = _dot_tn(x_ref[...], dy_ref[...])

    return pl.pallas_call(
        body, name=name, grid=grid, in_specs=[x_spec, dy_spec], out_specs=out_spec,
        out_shape=jax.ShapeDtypeStruct(out_shape, F32),
        compiler_params=_cp(dimension_semantics=("arbitrary",) * len(grid)),
    )(x, dy)


def _local_step(x, p, positions, target, w_in, w_o, w_a, w_b, conv_w, w_down, w_g, w_p,
                ln_z_g, ln_z_b, w_s, b_s, ln1_g, ln1_b, conv_b, ln2_g, ln2_b, b_g, ln3_g, ln3_b):
    t = x.shape[0]
    half = TM
    c_tab, s1_tab, s2_tab = _rope_tables(positions, t)
    b_full = jnp.repeat(jnp.transpose(b_s[0]), HEAD_DIM, axis=1)
    conv_b4 = conv_b.reshape(N_SHARD, 1, FF_BLK)
    *qkvs, hu, hz, mixed, gm = _qkvuz(x, w_in, c_tab, s1_tab, s2_tab, ln_z_g, ln_z_b, w_s[0], b_full)
    branches = [_attn_fwd(qkv, d) for qkv, d in zip(qkvs, DILATIONS)]
    attn, *lses, cat, xhat1, rstd1, x1b = _mix_ln1(
        [o for o, _ in branches], [l for _, l in branches], gm, x, w_o, ln1_g, ln1_b)
    a_pre, b_act, f = _ffn_in(x1b, w_a, w_b, conv_w, conv_b4)
    xhat2, rstd2, x2b = _ffn_out_ln2(f, w_down, xhat1, ln1_g, ln1_b, ln2_g, ln2_b)
    dr2, dgp, dpp, stat3 = _ple_loss_bwd(xhat2, rstd2, p, target, ln2_g, ln2_b, w_g, b_g, w_p, ln3_g, ln3_b)
    da_pre, dbb, dr1, cstat, stat1 = _ffn_bwd(dr2, a_pre, b_act, w_down, w_a, w_b, conv_w, conv_b4, xhat1, rstd1, ln1_g)
    do1, do4, do16, dl1, dl4, dl16, duz, dws, dbs, zstat = _mix_bwd(dr1, w_o, hu, hz, mixed, attn, ln_z_g, ln_z_b, w_s[0])
    dqkv = [_attn_bwd(qkv, do, lse, dl, d)
            for qkv, do, lse, dl, d in zip(qkvs, (do1, do4, do16), lses, (dl1, dl4, dl16), DILATIONS)]
    dh, grad_x = _dx_in([g[0] for g in dqkv], [g[1] for g in dqkv], [g[2] for g in dqkv], duz, dr1, w_in,
                        c_tab, s1_tab, s2_tab)

    full_t = lambda w, im: pl.BlockSpec((t, w), im)
    g_w_in = _wgrad("dw_in", x, dh, full_t(half, lambda j, kk: (0, kk)), full_t(W_IN_BLK, lambda j, kk: (0, j)),
                    pl.BlockSpec((None, half, W_IN_BLK), lambda j, kk: (j, kk, 0)), (N_SHARD, D_MODEL, W_IN_BLK), (N_SHARD, 2))
    g_w_o = _wgrad("dw_o", cat, dr1, full_t(half, lambda kk, n: (0, kk)), full_t(half, lambda kk, n: (0, n)),
                   pl.BlockSpec((half, half), lambda kk, n: (kk, n)), (D_MODEL, D_MODEL), (2, 2))
    ffj = pl.BlockSpec((None, t, FF_BLK), lambda j, kk: (j, 0, 0))
    g_w_a = _wgrad("dw_a", x1b, da_pre, full_t(half, lambda j, kk: (0, kk)), ffj,
                   pl.BlockSpec((None, half, FF_BLK), lambda j, kk: (j, kk, 0)), (N_SHARD, D_MODEL, FF_BLK), (N_SHARD, 2))
    g_w_b = _wgrad("dw_b", x1b, dbb, full_t(half, lambda j, kk: (0, kk)), ffj,
                   pl.BlockSpec((None, half, FF_BLK), lambda j, kk: (j, kk, 0)), (N_SHARD, D_MODEL, FF_BLK), (N_SHARD, 2))
    g_w_down = _wgrad("dw_down", f, dr2, ffj, full_t(half, lambda j, n: (0, n)),
                      pl.BlockSpec((None, FF_BLK, half), lambda j, n: (j, 0, n)), (N_SHARD, FF_BLK, D_MODEL), (N_SHARD, 2))
    g_w_g = _wgrad("dw_g", x2b, dgp, full_t(half, lambda kk, n: (0, kk)), full_t(half, lambda kk, n: (0, n)),
                   pl.BlockSpec((half, half), lambda kk, n: (kk, n)), (D_MODEL, D_MODEL), (2, 2))
    g_w_p = _wgrad("dw_p", p, dpp, full_t(D_PLE, lambda j: (0, 0)), full_t(ROW_BLK, lambda j: (0, j)),
                   pl.BlockSpec((None, D_PLE, ROW_BLK), lambda j: (j, 0, 0)), (N_SHARD, D_PLE, ROW_BLK), (N_SHARD,))

    big = dict(w_in=g_w_in, w_o=g_w_o, w_ff_a=g_w_a, w_ff_b=g_w_b, w_ff_down=g_w_down, w_ple_gate=g_w_g, w_ple_in=g_w_p)
    small = dict(
        ln3_g=stat3[0:1], ln3_b=stat3[1:2], b_ple_gate=stat3[2:3], ln2_g=stat3[3:4], ln2_b=stat3[4:5],
        ln1_g=stat1[0:1], ln1_b=stat1[1:2], ln_z_g=zstat[0:1], ln_z_b=zstat[1:2],
        conv_w=cstat[:, 0:3, :], conv_b=cstat[:, 3, :].reshape(1, D_FF),
        w_s=dws, b_s=dbs)
    return grad_x, big, small, (stat3, stat1, zstat, cstat)


def _rows_tile(r, mult, cap=512):
    return max(d for d in range(mult, min(r, cap) + 1, mult) if r % d == 0)


def _grid_spec(grid, in_specs, out_specs):
    return pltpu.PrefetchScalarGridSpec(num_scalar_prefetch=1, grid=grid, in_specs=in_specs, out_specs=out_specs)


def _place_shard(name, w, chip, dtype):
    r, c = w.shape
    tr = r if r % 16 else _rows_tile(r, 16)

    def body(s_ref, w_ref, o_ref):
        o_ref[...] = w_ref[...].astype(dtype)

    return pl.pallas_call(
        body, name=name,
        grid_spec=_grid_spec((r // tr,), [pl.BlockSpec((tr, c), lambda i, s: (i, 0))],
                             pl.BlockSpec((None, tr, c), lambda i, s: (s[0], i, 0))),
        out_shape=jax.ShapeDtypeStruct((N_SHARD, r, c), dtype), compiler_params=_cp())(chip, w)


def _pair_sum_bf16(name, mine, got, core):
    n, h, c = got.shape
    tr = _rows_tile(h, 16)
    nh = h // tr

    def body(s_ref, a_ref, b_ref, o_ref):
        o_ref[...] = (a_ref[...] + b_ref[...]).astype(BF16)

    spec = pl.BlockSpec((None, tr, c), lambda k, i, s: (k, i, 0))
    return pl.pallas_call(
        body, name=name,
        grid_spec=_grid_spec((n, nh), [pl.BlockSpec((None, tr, c), lambda k, i, s: (k, s[0] * nh + i, 0)), spec], spec),
        out_shape=jax.ShapeDtypeStruct((n, h, c), BF16), compiler_params=_cp())(core, mine, got)


def _chip_sum(name, own, landed, place):
    n, h, c = own.shape
    tr = _rows_tile(h, 16)
    nh = h // tr

    def body(s_ref, a_ref, b_ref, c_ref, d_ref, o_ref):
        o_ref[...] = ((a_ref[...].astype(F32) + b_ref[...].astype(F32)) + c_ref[...].astype(F32)) + d_ref[...].astype(F32)

    def slot(d):
        return pl.BlockSpec((None, tr, c), lambda i, s: ((s[0] + d) % n, i, 0))

    return pl.pallas_call(
        body, name=name,
        grid_spec=_grid_spec((nh,), [slot(0), slot(1), slot(2), slot(3)], pl.BlockSpec((tr, c), lambda i, s: (s[1] * nh + i, 0))),
        out_shape=jax.ShapeDtypeStruct((2 * h, c), F32), compiler_params=_cp())(place, own, landed, landed, landed)


def _adamw_math(w, g, m, v):
    m = ADAM_B1 * m + (1.0 - ADAM_B1) * g
    v = ADAM_B2 * v + (1.0 - ADAM_B2) * (g * g)
    m_hat = m / (1.0 - ADAM_B1 ** ADAM_STEP)
    v_hat = v / (1.0 - ADAM_B2 ** ADAM_STEP)
    delta = -ADAM_LR * (m_hat / (jnp.sqrt(v_hat) + ADAM_EPS) + ADAM_WD * w)
    return delta, m, v


def _adamw_big(name, w, g, m, v):
    _, r, c = w.shape
    tr = _rows_tile(r, 8, cap=256)

    def body(w_ref, g_ref, m_ref, v_ref, d_ref, nm_ref, nv_ref):
        d_ref[...], nm_ref[...], nv_ref[...] = _adamw_math(w_ref[...], g_ref[...], m_ref[...], v_ref[...])

    s3 = pl.BlockSpec((None, tr, c), lambda i: (0, i, 0))
    s2 = pl.BlockSpec((tr, c), lambda i: (i, 0))
    return pl.pallas_call(body, name=name, grid=(r // tr,), in_specs=[s3, s2, s3, s3], out_specs=[s3, s3, s3],
                          out_shape=[jax.ShapeDtypeStruct(w.shape, F32)] * 3, compiler_params=_cp())(w, g, m, v)


MESH = pl.DeviceIdType.MESH
ANY = pl.BlockSpec(memory_space=pl.ANY)


def _place():
    x, y, c = lax.axis_index("x"), lax.axis_index("y"), lax.axis_index("c")
    chips = [(1 - x, y), (x, 1 - y), (1 - x, 1 - y)]
    return x, y, c, 2 * x + y, chips


def _remote(src, dst, send_sem, recv_sem, dev):
    return pltpu.make_async_remote_copy(src_ref=src, dst_ref=dst, send_sem=send_sem, recv_sem=recv_sem,
                                        device_id=dev, device_id_type=MESH)


def _half(ref, hc, rows):
    return ref.at[pl.ds(hc * (rows // 2), rows // 2)]


def _allgather(stacks, split):
    n = len(stacks)

    def body(*refs):
        outs = refs[n:2 * n]
        send, recv, fsend, frecv = refs[2 * n:]
        x, y, c, j, chips = _place()
        rows = [s.shape[1] for s in stacks]

        def piece(a, slot, hc):
            return _half(outs[a].at[slot], hc, rows[a]) if split[a] else outs[a].at[slot]

        def direct(a, t, slot, dev):
            return _remote(piece(a, slot, c), piece(a, slot, c), send.at[a, t], recv.at[a, t], dev)

        sends = [direct(a, t, j, (*chips[t], c)) for a in range(n) for t in range(3)]
        for cp in sends:
            cp.start()
        fwd = []
        for t, (px, py) in enumerate(chips):
            jt = 2 * px + py
            for a in range(n):
                direct(a, t, jt, (px, py, c)).wait_recv()
                if split[a]:
                    cp = _remote(piece(a, jt, c), piece(a, jt, c), fsend.at[a, t], frecv.at[a, t], (x, y, 1 - c))
                    cp.start()
                    fwd.append(cp)
        for t, (px, py) in enumerate(chips):
            jt = 2 * px + py
            for a in range(n):
                if split[a]:
                    _remote(piece(a, jt, 1 - c), piece(a, jt, 1 - c), fsend.at[a, t], frecv.at[a, t], (x, y, 1 - c)).wait_recv()
        for cp in sends + fwd:
            cp.wait_send()

    sem = pltpu.SemaphoreType.DMA
    return pl.pallas_call(
        body, name="allgather_weights", in_specs=[ANY] * n, out_specs=[ANY] * n,
        out_shape=[jax.ShapeDtypeStruct(s.shape, s.dtype) for s in stacks],
        input_output_aliases={a: a for a in range(n)},
        scratch_shapes=[sem((n, 3)), sem((n, 3)), sem((n, 3)), sem((n, 3))],
    )(*stacks)


def _sibling_swap(grads):
    n = len(grads)

    def body(*refs):
        ins, got = refs[:n], refs[n:2 * n]
        send, recv = refs[2 * n:]
        x, y, c, _, _ = _place()
        cps = []
        for a in range(n):
            h = grads[a].shape[1] // 2
            cp = _remote(ins[a].at[:, pl.ds((1 - c) * h, h)], got[a], send.at[a], recv.at[a], (x, y, 1 - c))
            cp.start()
            cps.append(cp)
        for cp in cps:
            cp.wait_recv()
            cp.wait_send()

    sem = pltpu.SemaphoreType.DMA
    halves = [jax.ShapeDtypeStruct((g.shape[0], g.shape[1] // 2, g.shape[2]), g.dtype) for g in grads]
    return pl.pallas_call(body, name="rs_sibling_swap", in_specs=[ANY] * n, out_specs=[ANY] * n, out_shape=halves,
                          scratch_shapes=[sem((n,)), sem((n,))])(*grads)


def _chip_exchange(parts):
    n = len(parts)

    def body(*refs):
        ins, outs = refs[:n], refs[n:2 * n]
        send, recv = refs[2 * n:]
        x, y, c, j, chips = _place()
        sends = []
        for t, (px, py) in enumerate(chips):
            jt = 2 * px + py
            for a in range(n):
                cp = _remote(ins[a].at[jt], outs[a].at[j], send.at[a, t], recv.at[a, t], (px, py, c))
                cp.start()
                sends.append(cp)
        for t, (px, py) in enumerate(chips):
            jt = 2 * px + py
            for a in range(n):
                _remote(ins[a].at[jt], outs[a].at[jt], send.at[a, t], recv.at[a, t], (px, py, c)).wait_recv()
        for cp in sends:
            cp.wait_send()

    sem = pltpu.SemaphoreType.DMA
    return pl.pallas_call(body, name="rs_chip_exchange", in_specs=[ANY] * n, out_specs=[ANY] * n,
                          out_shape=[jax.ShapeDtypeStruct(p.shape, p.dtype) for p in parts],
                          scratch_shapes=[sem((n, 3)), sem((n, 3))])(*parts)


def _sibling_join(blocks):
    n = len(blocks)

    def body(*refs):
        outs = refs[n:2 * n]
        send, recv = refs[2 * n:]
        x, y, c, _, _ = _place()
        cps = []
        for a in range(n):
            h = blocks[a].shape[0] // 2
            mine = outs[a].at[pl.ds(c * h, h)]
            cp = _remote(mine, mine, send.at[a], recv.at[a], (x, y, 1 - c))
            cp.start()
            cps.append(cp)
        for a, cp in enumerate(cps):
            h = blocks[a].shape[0] // 2
            theirs = outs[a].at[pl.ds((1 - c) * h, h)]
            _remote(theirs, theirs, send.at[a], recv.at[a], (x, y, 1 - c)).wait_recv()
            cp.wait_send()

    sem = pltpu.SemaphoreType.DMA
    return pl.pallas_call(body, name="rs_sibling_join", in_specs=[ANY] * n, out_specs=[ANY] * n,
                          out_shape=[jax.ShapeDtypeStruct(b_.shape, b_.dtype) for b_ in blocks],
                          input_output_aliases={a: a for a in range(n)},
                          scratch_shapes=[sem((n,)), sem((n,))])(*blocks)


def _allreduce_small(arrs):
    n = len(arrs)

    def body(*refs):
        ins, outs = refs[:n], refs[n:2 * n]
        sib, chip = refs[2 * n:3 * n], refs[3 * n:4 * n]
        ssend, srecv, csend, crecv = refs[4 * n:]
        x, y, c, j, chips = _place()
        swaps = [_remote(ins[a], sib[a], ssend.at[a], srecv.at[a], (x, y, 1 - c)) for a in range(n)]
        for cp in swaps:
            cp.start()
        sends = []
        for a in range(n):
            swaps[a].wait_recv()
            chip[a][j] = ins[a][...] + sib[a][...]
            for t, (px, py) in enumerate(chips):
                cp = _remote(chip[a].at[j], chip[a].at[j], csend.at[a, t], crecv.at[a, t], (px, py, c))
                cp.start()
                sends.append(cp)
        for a in range(n):
            for t, (px, py) in enumerate(chips):
                jt = 2 * px + py
                _remote(chip[a].at[jt], chip[a].at[jt], csend.at[a, t], crecv.at[a, t], (px, py, c)).wait_recv()
            outs[a][...] = ((chip[a][0] + chip[a][1]) + chip[a][2]) + chip[a][3]
        for cp in swaps + sends:
            cp.wait_send()

    sem = pltpu.SemaphoreType.DMA
    vm = pl.BlockSpec(memory_space=pltpu.VMEM)
    return pl.pallas_call(
        body, name="allreduce_small", in_specs=[vm] * n, out_specs=[vm] * n,
        out_shape=[jax.ShapeDtypeStruct(a.shape, F32) for a in arrs],
        scratch_shapes=[pltpu.VMEM(a.shape, F32) for a in arrs] + [pltpu.VMEM((N_SHARD, *a.shape), F32) for a in arrs]
        + [sem((n,)), sem((n,)), sem((n, 3)), sem((n, 3))],
        compiler_params=_cp(),
    )(*arrs)


def _reduce_scatter(grads, place):
    names = list(grads)
    core = place[1:2]
    got = _sibling_swap([grads[k] for k in names])
    pair = [_pair_sum_bf16(f"rs_pair_{k}", grads[k], g, core) for k, g in zip(names, got)]
    landed = _chip_exchange(pair)
    blocks = [_chip_sum(f"rs_sum_{k}", own, l, place) for k, own, l in zip(names, pair, landed)]
    return dict(zip(names, _sibling_join(blocks)))


SMALL_1024 = ("ln1_g", "ln1_b", "ln2_g", "ln2_b", "b_ple_gate", "ln3_g", "ln3_b")


def _adamw_small(red3, red1, redz, g_conv_w, redc, red_ws, red_bs, params):
    shape2d = {"ln_z_g": (1, D_GMLP), "ln_z_b": (1, D_GMLP), "w_s": (N_HEADS * BLK, BLK), "b_s": (N_HEADS, BLK),
               "conv_w": (3, FF_BLK), "conv_b": (N_SHARD, FF_BLK), **{k: (1, D_MODEL) for k in SMALL_1024}}
    names = list(shape2d)
    flat = [a.reshape(shape2d[k]) for k in names for a in params[k]]

    def body(r3, r1, rz, gcw, rc, rws, rbs, *refs):
        ins, outs = refs[:3 * len(names)], refs[3 * len(names):]

        def grad_of(k):
            if k == "w_s":
                return rws[...]
            if k == "b_s":
                return rbs[...]
            if k == "conv_w":
                return gcw[0:3, :]
            if k == "conv_b":
                return jnp.concatenate([rc[j * STAT_ROWS + 3:j * STAT_ROWS + 4, :] for j in range(N_SHARD)], axis=0)
            src, row = {"ln3_g": (r3, 0), "ln3_b": (r3, 1), "b_ple_gate": (r3, 2), "ln2_g": (r3, 3), "ln2_b": (r3, 4),
                        "ln1_g": (r1, 0), "ln1_b": (r1, 1), "ln_z_g": (rz, 0), "ln_z_b": (rz, 1)}[k]
            return src[row:row + 1, :]

        for i, k in enumerate(names):
            w_ref, m_ref, v_ref = ins[3 * i:3 * i + 3]
            g_ref, d_ref, nm_ref, nv_ref = outs[4 * i:4 * i + 4]
            g = grad_of(k)
            g_ref[...] = g
            d_ref[...], nm_ref[...], nv_ref[...] = _adamw_math(w_ref[...], g, m_ref[...], v_ref[...])

    res = pl.pallas_call(
        body, name="adamw_small",
        out_shape=[jax.ShapeDtypeStruct(shape2d[k], F32) for k in names for _ in range(4)],
        compiler_params=_cp(),
    )(red3, red1, redz, g_conv_w, redc, red_ws, red_bs, *flat)
    return {k: tuple(r.reshape(params[k][0].shape) for r in res[4 * i:4 * i + 4]) for i, k in enumerate(names)}


WEIGHTS = ("w_in", "ln_z_g", "ln_z_b", "w_s", "b_s", "w_o", "ln1_g", "ln1_b", "w_ff_a", "w_ff_b", "conv_w", "conv_b",
           "w_ff_down", "ln2_g", "ln2_b", "w_ple_gate", "b_ple_gate", "w_ple_in", "ln3_g", "ln3_b")
BIG = ("w_in", "w_o", "w_ff_a", "w_ff_b", "w_ff_down", "w_ple_gate", "w_ple_in")


def kernel(x, p, positions, w_in, ln_z_g, ln_z_b, w_s, b_s, w_o, ln1_g, ln1_b, w_ff_a, w_ff_b, conv_w, conv_b, w_ff_down, ln2_g, ln2_b, w_ple_gate, b_ple_gate, w_ple_in, ln3_g, ln3_b, loss_target, m_w_in, m_ln_z_g, m_ln_z_b, m_w_s, m_b_s, m_w_o, m_ln1_g, m_ln1_b, m_w_ff_a, m_w_ff_b, m_conv_w, m_conv_b, m_w_ff_down, m_ln2_g, m_ln2_b, m_w_ple_gate, m_b_ple_gate, m_w_ple_in, m_ln3_g, m_ln3_b, v_w_in, v_ln_z_g, v_ln_z_b, v_w_s, v_b_s, v_w_o, v_ln1_g, v_ln1_b, v_w_ff_a, v_w_ff_b, v_conv_w, v_conv_b, v_w_ff_down, v_ln2_g, v_ln2_b, v_w_ple_gate, v_b_ple_gate, v_w_ple_in, v_ln3_g, v_ln3_b):
    args = locals()
    w = {k: args[k] for k in WEIGHTS}
    m = {k: args["m_" + k] for k in WEIGHTS}
    v = {k: args["v_" + k] for k in WEIGHTS}

    chip = 2 * lax.axis_index("x") + lax.axis_index("y")
    place = jnp.stack([chip, lax.axis_index("c")]).astype(jnp.int32)
    stacks = [_place_shard(f"cast_{k}", w[k][0], place, MXU) for k in BIG] + [_place_shard("place_conv_w", w["conv_w"][0], place, F32)]
    full = _allgather(stacks, [True] * len(BIG) + [False])
    fw = dict(zip(BIG + ("conv_w",), full))

    grad_x, big, small, (stat3, stat1, zstat, cstat) = _local_step(
        x[0], p[0, 0], positions, loss_target[0], fw["w_in"], fw["w_o"].reshape(D_MODEL, D_MODEL), fw["w_ff_a"], fw["w_ff_b"],
        fw["conv_w"], fw["w_ff_down"], fw["w_ple_gate"].reshape(D_MODEL, D_MODEL), fw["w_ple_in"],
        ln_z_g, ln_z_b, w_s, b_s, ln1_g, ln1_b, conv_b, ln2_g, ln2_b, b_ple_gate, ln3_g, ln3_b)

    stacked = {k: big[k].reshape(N_SHARD, *w[k].shape[1:]) for k in BIG}
    red = _reduce_scatter(stacked, place)
    out = {}
    for k in BIG:
        d, nm, nv = _adamw_big(f"adamw_{k}", w[k], red[k], m[k], v[k])
        out[k] = (red[k].reshape(w[k].shape), d, nm, nv)

    red3, red1, redz, redc, red_ws, red_bs = _allreduce_small(
        [stat3, stat1, zstat, cstat.reshape(N_SHARD * STAT_ROWS, FF_BLK), small["w_s"].reshape(N_HEADS * BLK, BLK), small["b_s"]])
    loss = (0.5 / D_MODEL) * jnp.sum(red3[5])
    g_conv_w = lax.dynamic_slice_in_dim(redc, chip * STAT_ROWS, STAT_ROWS, 0)
    names_small = [k for k in WEIGHTS if k not in BIG]
    out.update(_adamw_small(red3, red1, redz, g_conv_w, redc, red_ws, red_bs, {k: (w[k], m[k], v[k]) for k in names_small}))

    return (loss, grad_x[None], *[out[k][0] for k in WEIGHTS], *[out[k][1] for k in WEIGHTS],
            *[out[k][2] for k in WEIGHTS], *[out[k][3] for k in WEIGHTS])
```

```python
import functools
import math

import numpy as np
import jax
import jax.numpy as jnp
from jax import lax
from jax.experimental import pallas as pl
from jax.experimental.pallas import tpu as pltpu

F32 = jnp.float32
BF16 = jnp.bfloat16
MXU = BF16

D_MODEL = 1024
HEAD_DIM = 64
N_HEADS = 8
D_ATTN = 512
D_GMLP = 512
D_IN = 2560
DILATIONS = (1, 4, 16)
BLK = 128
ROPE_THETA = 500000.0
ROPE_DIM = 16
D_FF = 2816
D_PLE = 256
LN_EPS = 1e-5
ALPHA = 2.0 ** 0.25
NEG_INF = -1e30
N_SHARD = 4
W_IN_BLK = D_IN // N_SHARD
FF_BLK = D_FF // N_SHARD
ROW_BLK = D_MODEL // N_SHARD
ADAM_LR, ADAM_B1, ADAM_B2, ADAM_EPS, ADAM_WD, ADAM_STEP = 0.001, 0.9, 0.999, 1e-08, 0.01, 10

TM = 512
HALO = 8
VMEM_LIMIT = 56 * 1024 * 1024


def _cp(**kw):
    return pltpu.CompilerParams(vmem_limit_bytes=VMEM_LIMIT, **kw)


def _full(shape):
    n = len(shape)
    return pl.BlockSpec(shape, lambda *_: (0,) * n)


def _gelu(x):
    return 0.5 * x * (1.0 + lax.erf(x * (1.0 / math.sqrt(2.0))))


def _gelu_grad(x):
    return 0.5 * (1.0 + lax.erf(x * (1.0 / math.sqrt(2.0)))) + x * jnp.exp(-0.5 * x * x) * (1.0 / math.sqrt(2.0 * math.pi))


def _ln_fwd(r):
    mu = jnp.mean(r, axis=-1, keepdims=True)
    xc = r - mu
    var = jnp.mean(xc * xc, axis=-1, keepdims=True)
    rstd = lax.rsqrt(var + LN_EPS)
    return xc * rstd, rstd


def _ln_bwd(dy, xhat, rstd, g):
    dxh = dy * g
    m1 = jnp.mean(dxh, axis=-1, keepdims=True)
    m2 = jnp.mean(dxh * xhat, axis=-1, keepdims=True)
    return rstd * (dxh - m1 - xhat * m2)


def _dot(a, b):
    return jnp.dot(a.astype(MXU), b.astype(MXU), preferred_element_type=F32)


def _dot_nt(a, b):
    return lax.dot_general(a.astype(MXU), b.astype(MXU), (((1,), (1,)), ((), ())), preferred_element_type=F32)


def _dot_tn(a, b):
    return lax.dot_general(a.astype(MXU), b.astype(MXU), (((0,), (0,)), ((), ())), preferred_element_type=F32)


def _colsum(v):
    return jnp.sum(v, axis=0, keepdims=True)


def _rope_tables(positions, t):
    inv = np.float32(ROPE_THETA) ** (-np.arange(0, ROPE_DIM, 2, dtype=np.float32) / np.float32(ROPE_DIM))
    half = ROPE_DIM // 2
    pos_rep = jnp.repeat(positions.reshape(t // 16, 16), half, axis=1)
    inv_row = jnp.asarray(np.tile(inv, 16)[None, :], F32)

    def trig_body(pos_ref, inv_ref, cos_ref, sin_ref):
        ang = pos_ref[...].astype(F32) * inv_ref[...]
        cos_ref[...] = jnp.cos(ang)
        sin_ref[...] = jnp.sin(ang)

    cos8, sin8 = pl.pallas_call(
        trig_body, name="rope_trig",
        out_shape=(jax.ShapeDtypeStruct((t // 16, 128), F32), jax.ShapeDtypeStruct((t // 16, 128), F32)),
    )(pos_rep, inv_row)
    cos8 = cos8.reshape(t, half)
    sin8 = sin8.reshape(t, half)

    lane = np.arange(128) % HEAD_DIM
    sel = (np.arange(half)[:, None] == (lane % half)[None, :])
    e_cos = (sel & (lane < ROPE_DIM)[None, :]).astype(np.float32)
    e_s1 = -(sel & (lane < half)[None, :]).astype(np.float32)
    e_s2 = (sel & ((lane >= half) & (lane < ROPE_DIM))[None, :]).astype(np.float32)
    ones = (lane >= ROPE_DIM).astype(np.float32)[None, :]

    def expand_body(cos_ref, sin_ref, ec_ref, e1_ref, e2_ref, ones_ref, c_ref, s1_ref, s2_ref):
        hp = lax.Precision.HIGHEST
        c_ref[...] = jnp.dot(cos_ref[...], ec_ref[...], precision=hp, preferred_element_type=F32) + ones_ref[...]
        s1_ref[...] = jnp.dot(sin_ref[...], e1_ref[...], precision=hp, preferred_element_type=F32)
        s2_ref[...] = jnp.dot(sin_ref[...], e2_ref[...], precision=hp, preferred_element_type=F32)

    tab = jax.ShapeDtypeStruct((t, 128), F32)
    return pl.pallas_call(expand_body, name="rope_expand", out_shape=(tab, tab, tab), compiler_params=_cp())(
        cos8, sin8, jnp.asarray(e_cos), jnp.asarray(e_s1), jnp.asarray(e_s2), jnp.asarray(ones))


def _tile_heads(tab):
    return jnp.concatenate([tab] * (D_ATTN // 128), axis=1)


def _rope_apply(v, c, s1, s2):
    n = v.shape[1]
    half = ROPE_DIM // 2
    return v * c + pltpu.roll(v, n - half, 1) * s1 + pltpu.roll(v, half, 1) * s2


def _rope_apply_t(g, c, s1, s2):
    n = g.shape[1]
    half = ROPE_DIM // 2
    return g * c + pltpu.roll(g * s1, half, 1) + pltpu.roll(g * s2, n - half, 1)


LANE_CHUNKS = D_ATTN // 128
HEAD_LANES = 128 // N_HEADS


def _perm_shape(t, d, w, dtype):
    return jax.ShapeDtypeStruct((d, t // d, w), dtype)


def _perm_tile(d, w):
    return pl.BlockSpec((None if d == 1 else d, TM // d, w), lambda i: (0, i, 0))


def _to_planes(ref, scr, d, n_chunks, dtype):
    for r in range(d):
        for cc in range(n_chunks):
            ref[r, :, cc * 128:(cc + 1) * 128] = scr.at[cc][pl.ds(r, TM // d, stride=d), :].astype(dtype)


def _from_planes(ref, scr, d, n_chunks, accumulate=False):
    for r in range(d):
        for cc in range(n_chunks):
            rows = scr.at[cc]
            val = ref[r, :, cc * 128:(cc + 1) * 128].astype(F32)
            if accumulate:
                rows[pl.ds(r, TM // d, stride=d), :] += val
            else:
                rows[pl.ds(r, TM // d, stride=d), :] = val


def _chunks(val):
    return [val[:, cc * 128:(cc + 1) * 128] for cc in range(val.shape[1] // 128)]


def _unchunk(scr, n_chunks, base=0):
    return jnp.concatenate([scr[base + cc] for cc in range(n_chunks)], axis=1)


def _head_expand():
    src = np.arange(128)[:, None]
    dst = np.arange(D_ATTN)[None, :]
    return jnp.asarray((src == (dst // HEAD_DIM) * HEAD_LANES).astype(np.float32))


def _head_reduce():
    src = np.arange(D_ATTN)[:, None]
    dst = np.arange(128)[None, :]
    return jnp.asarray((src // HEAD_DIM == dst // HEAD_LANES).astype(np.float32))


def _dot_exact(a, b):
    return jnp.dot(a, b, precision=lax.Precision.HIGHEST, preferred_element_type=F32)


def _qkvuz(x, w_in, c_tab, s1_tab, s2_tab, ln_z_g, ln_z_b, w_s, b_full, dep):
    t = x.shape[0]
    nchunk = TM // BLK

    def body(x_ref, w_ref, c_ref, s1_ref, s2_ref, g_ref, b_ref, ws_ref, bf_ref, dep_ref,
             qkv1_ref, qkv4_ref, qkv16_ref, hu_ref, hz_ref, mixed_ref, gm_ref, h_scr, wm_scr, p_scr):
        @pl.when(pl.program_id(0) == 0)
        def _():
            row = lax.broadcasted_iota(jnp.int32, (BLK, BLK), 0)
            col = lax.broadcasted_iota(jnp.int32, (BLK, BLK), 1)
            for g in range(N_HEADS):
                wm_scr[g] = jnp.where(col <= row, ws_ref[g], 0.0).astype(MXU)

        xb = x_ref[...].astype(MXU)
        for j in range(N_SHARD):
            h_scr[:, j * W_IN_BLK:(j + 1) * W_IN_BLK] = jnp.dot(xb, w_ref[j], preferred_element_type=F32)
        c, s1, s2 = _tile_heads(c_ref[...]), _tile_heads(s1_ref[...]), _tile_heads(s2_ref[...])
        q = _rope_apply(h_scr[:, 0:D_ATTN], c, s1, s2) * (1.0 / math.sqrt(HEAD_DIM))
        k = _rope_apply(h_scr[:, D_ATTN:2 * D_ATTN], c, s1, s2)
        for part, val in enumerate((q, k, h_scr[:, 2 * D_ATTN:3 * D_ATTN])):
            qkv1_ref[:, part * D_ATTN:(part + 1) * D_ATTN] = val.astype(MXU)
            for cc in range(LANE_CHUNKS):
                p_scr[part * LANE_CHUNKS + cc] = val[:, cc * 128:(cc + 1) * 128]
        _to_planes(qkv4_ref, p_scr, DILATIONS[1], 3 * LANE_CHUNKS, MXU)
        _to_planes(qkv16_ref, p_scr, DILATIONS[2], 3 * LANE_CHUNKS, MXU)
        hu = h_scr[:, 3 * D_ATTN:3 * D_ATTN + D_GMLP]
        hz = h_scr[:, 3 * D_ATTN + D_GMLP:]
        hu_ref[...] = hu
        hz_ref[...] = hz
        zhat, _ = _ln_fwd(_gelu(hz))
        zn = (zhat * g_ref[...] + b_ref[...]).astype(MXU)
        for ch in range(nchunk):
            rows = slice(ch * BLK, (ch + 1) * BLK)
            for g in range(N_HEADS):
                cols = slice(g * HEAD_DIM, (g + 1) * HEAD_DIM)
                mixed_ref[rows, cols] = jnp.dot(wm_scr[g], zn[rows, cols], preferred_element_type=F32) + bf_ref[:, cols]
        gm_ref[...] = (_gelu(hu) * mixed_ref[...]).astype(MXU)

    tok = lambda w: pl.BlockSpec((TM, w), lambda i: (i, 0))
    outs = [_perm_shape(t, d, 3 * D_ATTN, MXU) for d in DILATIONS] + [jax.ShapeDtypeStruct((t, D_GMLP), F32)] * 3 + [
        jax.ShapeDtypeStruct((t, D_GMLP), MXU)]
    return pl.pallas_call(
        body, name="qkvuz", grid=(t // TM,),
        in_specs=[tok(D_MODEL), _full(w_in.shape), tok(128), tok(128), tok(128), _full(ln_z_g.shape), _full(ln_z_b.shape),
                  _full(w_s.shape), _full(b_full.shape), pl.BlockSpec(memory_space=pl.ANY)],
        out_specs=[_perm_tile(d, 3 * D_ATTN) for d in DILATIONS] + [tok(D_ATTN)] * 4, out_shape=outs,
        scratch_shapes=[pltpu.VMEM((TM, D_IN), F32), pltpu.VMEM((N_HEADS, BLK, BLK), MXU),
                        pltpu.VMEM((3 * LANE_CHUNKS, TM, 128), F32)],
        compiler_params=_cp(dimension_semantics=("arbitrary",)),
    )(x, w_in, c_tab, s1_tab, s2_tab, ln_z_g, ln_z_b, w_s, b_full, dep)


def _band_valid(n):
    i = lax.broadcasted_iota(jnp.int32, (BLK, 2 * BLK), 0)
    j = lax.broadcasted_iota(jnp.int32, (BLK, 2 * BLK), 1)
    return (j >= i) & (j <= i + BLK) & ((j >= BLK) | (n > 0))


def _attn_fwd(qkv, d):
    _, l_sub, _ = qkv.shape
    nb = l_sub // BLK

    def body(q_ref, kp_ref, kc_ref, vp_ref, vc_ref, o_ref, l_ref):
        valid = _band_valid(pl.program_id(1))
        kcat = jnp.concatenate([kp_ref[...], kc_ref[...]], axis=0)
        vcat = jnp.concatenate([vp_ref[...], vc_ref[...]], axis=0)
        for h in range(N_HEADS):
            cols = slice(h * HEAD_DIM, (h + 1) * HEAD_DIM)
            s = jnp.where(valid, _dot_nt(q_ref[:, cols], kcat[:, cols]), NEG_INF)
            m = jnp.max(s, axis=-1, keepdims=True)
            e = jnp.exp(s - m)
            den = jnp.sum(e, axis=-1, keepdims=True)
            o_ref[:, cols] = _dot(e, vcat[:, cols]) * (1.0 / den)
            l_ref[:, h * HEAD_LANES:(h + 1) * HEAD_LANES] = jnp.broadcast_to(m + jnp.log(den), (BLK, HEAD_LANES))

    def blk(w, col, prev=False):
        return pl.BlockSpec((None, BLK, w), lambda r, n: (r, jnp.maximum(n - 1, 0) if prev else n, col))

    return pl.pallas_call(
        body, name=f"attn_fwd_d{d}", grid=(d, nb),
        in_specs=[blk(D_ATTN, 0), blk(D_ATTN, 1, True), blk(D_ATTN, 1), blk(D_ATTN, 2, True), blk(D_ATTN, 2)],
        out_specs=[blk(D_ATTN, 0), blk(128, 0)],
        out_shape=[jax.ShapeDtypeStruct((d, l_sub, D_ATTN), F32), jax.ShapeDtypeStruct((d, l_sub, 128), F32)],
        compiler_params=_cp(dimension_semantics=("arbitrary", "arbitrary")),
    )(qkv, qkv, qkv, qkv, qkv)


def _attn_bwd(qkv, do, lse, delta, d):
    _, l_sub, _ = qkv.shape
    nb = l_sub // BLK

    def body(q_ref, kp_ref, kc_ref, vp_ref, vc_ref, do_ref, l_ref, dl_ref, dq_ref, dk_ref, dv_ref, ck_scr, cv_scr):
        n = pl.program_id(1)

        @pl.when(n == 0)
        def _():
            ck_scr[...] = jnp.zeros_like(ck_scr)
            cv_scr[...] = jnp.zeros_like(cv_scr)

        @pl.when(n < nb)
        def _():
            valid = _band_valid(n)
            kcat = jnp.concatenate([kp_ref[...], kc_ref[...]], axis=0)
            vcat = jnp.concatenate([vp_ref[...], vc_ref[...]], axis=0)
            for h in range(N_HEADS):
                cols = slice(h * HEAD_DIM, (h + 1) * HEAD_DIM)
                stat = slice(h * HEAD_LANES, h * HEAD_LANES + 1)
                qh, doh = q_ref[:, cols], do_ref[:, cols]
                p = jnp.where(valid, jnp.exp(_dot_nt(qh, kcat[:, cols]) - l_ref[:, stat]), 0.0)
                ds = p * (_dot_nt(doh, vcat[:, cols]) - dl_ref[:, stat])
                dq_ref[:, cols] = _dot(ds, kcat[:, cols])
                dk2 = _dot_tn(ds, qh)
                dv2 = _dot_tn(p, doh)
                dk_ref[:, cols] = ck_scr[:, cols] + dk2[0:BLK]
                dv_ref[:, cols] = cv_scr[:, cols] + dv2[0:BLK]
                ck_scr[:, cols] = dk2[BLK:]
                cv_scr[:, cols] = dv2[BLK:]

        @pl.when(n == nb)
        def _():
            dk_ref[...] = ck_scr[...]
            dv_ref[...] = cv_scr[...]

    def blk(w, col, shift=0):
        return pl.BlockSpec((None, BLK, w), lambda r, n: (r, jnp.clip(n - shift, 0, nb - 1), col))

    return pl.pallas_call(
        body, name=f"attn_bwd_d{d}", grid=(d, nb + 1),
        in_specs=[blk(D_ATTN, 0), blk(D_ATTN, 1, 1), blk(D_ATTN, 1), blk(D_ATTN, 2, 1), blk(D_ATTN, 2),
                  blk(D_ATTN, 0), blk(128, 0), blk(128, 0)],
        out_specs=[blk(D_ATTN, 0), blk(D_ATTN, 0, 1), blk(D_ATTN, 0, 1)],
        out_shape=[jax.ShapeDtypeStruct((d, l_sub, D_ATTN), F32)] * 3,
        scratch_shapes=[pltpu.VMEM((BLK, D_ATTN), F32)] * 2,
        compiler_params=_cp(dimension_semantics=("arbitrary", "arbitrary")),
    )(qkv, qkv, qkv, qkv, qkv, do, lse, delta)


def _mix_ln1(os_, ls_, gm, x, w_o, ln1_g, ln1_b):
    t = x.shape[0]
    expand = _head_expand()

    def body(o1, o4, o16, l1, l4, l16, gm_ref, x_ref, wo_ref, g_ref, b_ref, ex_ref,
             attn_ref, lse1_ref, lse4_ref, lse16_ref, cat_ref, xhat_ref, rstd_ref, x1b_ref, o_scr, l_scr):
        _from_planes(o4, o_scr, DILATIONS[1], LANE_CHUNKS)
        _from_planes(o16, o_scr.at[pl.ds(LANE_CHUNKS, LANE_CHUNKS)], DILATIONS[2], LANE_CHUNKS)
        _from_planes(l4, l_scr, DILATIONS[1], 1)
        _from_planes(l16, l_scr.at[pl.ds(1, 1)], DILATIONS[2], 1)
        la, lb, lc = l1[...], l_scr[0], l_scr[1]
        m = jnp.maximum(jnp.maximum(la, lb), lc)
        ea, eb, ec = jnp.exp(la - m), jnp.exp(lb - m), jnp.exp(lc - m)
        den = ea + eb + ec
        inv = 1.0 / den
        wide = lambda w: _dot_exact(w, ex_ref[...])
        attn = (wide(ea * inv) * o1[...] + wide(eb * inv) * _unchunk(o_scr, LANE_CHUNKS)
                + wide(ec * inv) * _unchunk(o_scr, LANE_CHUNKS, LANE_CHUNKS))
        attn_ref[...] = attn
        lse = m + jnp.log(den)
        lse1_ref[...] = lse
        l_scr[2] = lse
        _to_planes(lse4_ref, l_scr.at[pl.ds(2, 1)], DILATIONS[1], 1, F32)
        _to_planes(lse16_ref, l_scr.at[pl.ds(2, 1)], DILATIONS[2], 1, F32)
        cat_ref[:, 0:D_ATTN] = attn.astype(MXU)
        cat_ref[:, D_ATTN:] = gm_ref[...]
        mix = jnp.dot(cat_ref[...], wo_ref[...], preferred_element_type=F32)
        xhat, rstd = _ln_fwd(ALPHA * x_ref[...] + mix)
        xhat_ref[...] = xhat
        rstd_ref[...] = rstd
        x1b_ref[...] = (xhat * g_ref[...] + b_ref[...]).astype(MXU)

    tok = lambda w: pl.BlockSpec((TM, w), lambda i: (i, 0))
    outs = [jax.ShapeDtypeStruct((t, D_ATTN), F32)] + [_perm_shape(t, d, 128, F32) for d in DILATIONS] + [
        jax.ShapeDtypeStruct((t, D_MODEL), MXU), jax.ShapeDtypeStruct((t, D_MODEL), F32), jax.ShapeDtypeStruct((t, 1), F32),
        jax.ShapeDtypeStruct((t, D_MODEL), MXU)]
    return pl.pallas_call(
        body, name="mix_ln1", grid=(t // TM,),
        in_specs=[_perm_tile(d, D_ATTN) for d in DILATIONS] + [_perm_tile(d, 128) for d in DILATIONS]
        + [tok(D_GMLP), tok(D_MODEL), _full(w_o.shape), _full(ln1_g.shape), _full(ln1_b.shape), _full(expand.shape)],
        out_specs=[tok(D_ATTN)] + [_perm_tile(d, 128) for d in DILATIONS] + [tok(D_MODEL), tok(D_MODEL), tok(1), tok(D_MODEL)],
        out_shape=outs,
        scratch_shapes=[pltpu.VMEM((2 * LANE_CHUNKS, TM, 128), F32), pltpu.VMEM((3, TM, 128), F32)],
        compiler_params=_cp(dimension_semantics=("arbitrary",)),
    )(*os_, *ls_, gm, x, w_o, ln1_g, ln1_b, expand)


def _conv_fwd(a_ext, w_ref, b_ref, rows):
    return (b_ref[...] + w_ref[2:3, :] * a_ext[HALO:HALO + rows] + w_ref[1:2, :] * a_ext[HALO - 1:HALO - 1 + rows]
            + w_ref[0:1, :] * a_ext[HALO - 2:HALO - 2 + rows])


def _ffn_in(x1b, w_a, w_b, conv_w, conv_b):
    t = x1b.shape[0]
    hb = TM // HALO

    def body(x_ref, xh_ref, wa_ref, wb_ref, cw_ref, cb_ref, apre_ref, b_ref, f_ref):
        i = pl.program_id(1)
        a_pre = jnp.dot(x_ref[...], wa_ref[...], preferred_element_type=F32)
        a_halo = jnp.dot(xh_ref[...], wa_ref[...], preferred_element_type=F32)
        a_halo = jnp.where(i > 0, a_halo, 0.0)
        a = _conv_fwd(jnp.concatenate([a_halo, a_pre], axis=0), cw_ref, cb_ref, TM)
        b = jnp.dot(x_ref[...], wb_ref[...], preferred_element_type=F32)
        apre_ref[...] = a_pre
        b_ref[...] = b
        f_ref[...] = (_gelu(a) * b).astype(MXU)

    blk = lambda r, c: pl.BlockSpec((None, r, c), lambda j, i: (j, 0, 0))
    tokj = pl.BlockSpec((None, TM, FF_BLK), lambda j, i: (j, i, 0))
    outs = [jax.ShapeDtypeStruct((N_SHARD, t, FF_BLK), F32)] * 2 + [jax.ShapeDtypeStruct((N_SHARD, t, FF_BLK), MXU)]
    return pl.pallas_call(
        body, name="ffn_in", grid=(N_SHARD, t // TM),
        in_specs=[pl.BlockSpec((TM, D_MODEL), lambda j, i: (i, 0)),
                  pl.BlockSpec((HALO, D_MODEL), lambda j, i: (jnp.maximum(i * hb - 1, 0), 0)),
                  blk(D_MODEL, FF_BLK), blk(D_MODEL, FF_BLK), blk(3, FF_BLK), blk(1, FF_BLK)],
        out_specs=[tokj, tokj, tokj], out_shape=outs,
        compiler_params=_cp(dimension_semantics=("arbitrary", "arbitrary")),
    )(x1b, x1b, w_a, w_b, conv_w, conv_b)


def _ffn_out_ln2(f, w_down, xhat1, ln1_g, ln1_b, ln2_g, ln2_b):
    t = xhat1.shape[0]

    def body(f_ref, wd_ref, xh_ref, g1_ref, b1_ref, g2_ref, b2_ref, xhat_ref, rstd_ref, x2b_ref):
        ff = jnp.dot(f_ref[0], wd_ref[0], preferred_element_type=F32)
        for j in range(1, N_SHARD):
            ff = ff + jnp.dot(f_ref[j], wd_ref[j], preferred_element_type=F32)
        x1 = xh_ref[...] * g1_ref[...] + b1_ref[...]
        xhat, rstd = _ln_fwd(ALPHA * x1 + ff)
        xhat_ref[...] = xhat
        rstd_ref[...] = rstd
        x2b_ref[...] = (xhat * g2_ref[...] + b2_ref[...]).astype(MXU)

    tok = lambda w: pl.BlockSpec((TM, w), lambda i: (i, 0))
    vec = _full((1, D_MODEL))
    outs = [jax.ShapeDtypeStruct((t, D_MODEL), F32), jax.ShapeDtypeStruct((t, 1), F32), jax.ShapeDtypeStruct((t, D_MODEL), MXU)]
    return pl.pallas_call(
        body, name="ffn_out_ln2", grid=(t // TM,),
        in_specs=[pl.BlockSpec((N_SHARD, TM, FF_BLK), lambda i: (0, i, 0)), _full(w_down.shape), tok(D_MODEL), vec, vec, vec, vec],
        out_specs=[tok(D_MODEL), tok(1), tok(D_MODEL)], out_shape=outs,
        compiler_params=_cp(dimension_semantics=("arbitrary",)),
    )(f, w_down, xhat1, ln1_g, ln1_b, ln2_g, ln2_b)


STAT_ROWS = 8


def _ple_loss_bwd(xhat2, rstd2, p, target, ln2_g, ln2_b, w_g, b_g, w_p, ln3_g, ln3_b):
    t = xhat2.shape[0]

    def body(xh2_ref, rs2_ref, p_ref, t_ref, g2_ref, b2_ref, wg_ref, bg_ref, wp_ref, g3_ref, b3_ref,
             dr2_ref, dgp_ref, dpp_ref, stat_ref, pp_scr):
        @pl.when(pl.program_id(0) == 0)
        def _():
            stat_ref[...] = jnp.zeros_like(stat_ref)

        xhat2 = xh2_ref[...]
        x2 = xhat2 * g2_ref[...] + b2_ref[...]
        gate = jax.nn.sigmoid(jnp.dot(x2.astype(MXU), wg_ref[...], preferred_element_type=F32) + bg_ref[...])
        pb = p_ref[...].astype(MXU)
        for j in range(N_SHARD):
            pp_scr[:, j * ROW_BLK:(j + 1) * ROW_BLK] = jnp.dot(pb, wp_ref[j], preferred_element_type=F32)
        pp = pp_scr[...]
        xhat3, rstd3 = _ln_fwd(ALPHA * x2 + gate * pp)
        err = xhat3 * g3_ref[...] + b3_ref[...] - t_ref[...]
        dy = err * (1.0 / D_MODEL)
        dr3 = _ln_bwd(dy, xhat3, rstd3, g3_ref[...])
        dgp = dr3 * pp * gate * (1.0 - gate)
        dgp_ref[...] = dgp.astype(MXU)
        dpp_ref[...] = (dr3 * gate).astype(MXU)
        dx2 = ALPHA * dr3 + _dot_nt(dgp, wg_ref[...])
        dr2_ref[...] = _ln_bwd(dx2, xhat2, rs2_ref[...], g2_ref[...])
        stat_ref[0:1, :] += _colsum(dy * xhat3)
        stat_ref[1:2, :] += _colsum(dy)
        stat_ref[2:3, :] += _colsum(dgp)
        stat_ref[3:4, :] += _colsum(dx2 * xhat2)
        stat_ref[4:5, :] += _colsum(dx2)
        stat_ref[5:6, :] += _colsum(err * err)

    tok = lambda w: pl.BlockSpec((TM, w), lambda i: (i, 0))
    vec = _full((1, D_MODEL))
    outs = [jax.ShapeDtypeStruct((t, D_MODEL), F32), jax.ShapeDtypeStruct((t, D_MODEL), MXU), jax.ShapeDtypeStruct((t, D_MODEL), MXU),
            jax.ShapeDtypeStruct((STAT_ROWS, D_MODEL), F32)]
    return pl.pallas_call(
        body, name="ple_loss_bwd", grid=(t // TM,),
        in_specs=[tok(D_MODEL), tok(1), tok(D_PLE), tok(D_MODEL), vec, vec, _full(w_g.shape), vec, _full(w_p.shape), vec, vec],
        out_specs=[tok(D_MODEL), tok(D_MODEL), tok(D_MODEL), _full((STAT_ROWS, D_MODEL))], out_shape=outs,
        scratch_shapes=[pltpu.VMEM((TM, D_MODEL), F32)],
        compiler_params=_cp(dimension_semantics=("arbitrary",)),
    )(xhat2, rstd2, p, target, ln2_g, ln2_b, w_g, b_g, w_p, ln3_g, ln3_b)


def _ffn_bwd(dr2, a_pre, b, w_down, w_a, w_b, conv_w, conv_b, xhat1, rstd1, ln1_g):
    t = dr2.shape[0]
    nt = t // TM
    hb = TM // HALO
    last_h = t // HALO - 1

    def body(dr_ref, drn_ref, ap_ref, app_ref, apn_ref, b_ref, bn_ref, wd_ref, wa_ref, wb_ref, cw_ref, cb_ref,
             xh_ref, rs_ref, g1_ref, dap_ref, dbb_ref, dr1_ref, cstat_ref, lstat_ref, acc_scr):
        i, j = pl.program_id(0), pl.program_id(1)

        @pl.when((i == 0) & (j == 0))
        def _():
            cstat_ref[...] = jnp.zeros_like(cstat_ref)
            lstat_ref[...] = jnp.zeros_like(lstat_ref)

        ext = TM + HALO
        dr_ext = jnp.concatenate([dr_ref[...], drn_ref[...]], axis=0)
        df = _dot_nt(dr_ext, wd_ref[...])
        a_all = jnp.concatenate([jnp.where(i > 0, app_ref[...], 0.0), ap_ref[...], apn_ref[...]], axis=0)
        a = _conv_fwd(a_all, cw_ref, cb_ref, ext)
        b_ext = jnp.concatenate([b_ref[...], bn_ref[...]], axis=0)
        row = lax.broadcasted_iota(jnp.int32, (ext, 1), 0)
        da = jnp.where((row < TM) | (i < nt - 1), df * b_ext * _gelu_grad(a), 0.0)
        dbb = df[0:TM] * _gelu(a[0:TM])
        da_pre = cw_ref[2:3, :] * da[0:TM] + cw_ref[1:2, :] * da[1:TM + 1] + cw_ref[0:1, :] * da[2:TM + 2]
        dap_ref[...] = da_pre.astype(MXU)
        dbb_ref[...] = dbb.astype(MXU)
        da_m = da[0:TM]
        for kk in range(3):
            cstat_ref[j, kk:kk + 1, :] += _colsum(da_m * a_all[HALO - 2 + kk:HALO - 2 + kk + TM])
        cstat_ref[j, 3:4, :] += _colsum(da_m)
        part = _dot_nt(da_pre, wa_ref[...]) + _dot_nt(dbb, wb_ref[...])

        @pl.when(j == 0)
        def _():
            acc_scr[...] = ALPHA * dr_ref[...] + part

        @pl.when(j > 0)
        def _():
            acc_scr[...] += part

        @pl.when(j == N_SHARD - 1)
        def _():
            dx1 = acc_scr[...]
            xhat1 = xh_ref[...]
            lstat_ref[0:1, :] += _colsum(dx1 * xhat1)
            lstat_ref[1:2, :] += _colsum(dx1)
            dr1_ref[...] = _ln_bwd(dx1, xhat1, rs_ref[...], g1_ref[...])

    tok = lambda w: pl.BlockSpec((TM, w), lambda i, j: (i, 0))
    tokj = pl.BlockSpec((None, TM, FF_BLK), lambda i, j: (j, i, 0))
    prevj = pl.BlockSpec((None, HALO, FF_BLK), lambda i, j: (j, jnp.maximum(i * hb - 1, 0), 0))
    nextj = pl.BlockSpec((None, HALO, FF_BLK), lambda i, j: (j, jnp.minimum((i + 1) * hb, last_h), 0))
    blk = lambda r, c: pl.BlockSpec((None, r, c), lambda i, j: (j, 0, 0))
    outs = [jax.ShapeDtypeStruct((N_SHARD, t, FF_BLK), MXU)] * 2 + [
        jax.ShapeDtypeStruct((t, D_MODEL), F32), jax.ShapeDtypeStruct((N_SHARD, STAT_ROWS, FF_BLK), F32),
        jax.ShapeDtypeStruct((STAT_ROWS, D_MODEL), F32)]
    return pl.pallas_call(
        body, name="ffn_bwd", grid=(nt, N_SHARD),
        in_specs=[tok(D_MODEL), pl.BlockSpec((HALO, D_MODEL), lambda i, j: (jnp.minimum((i + 1) * hb, last_h), 0)),
                  tokj, prevj, nextj, tokj, nextj, blk(FF_BLK, D_MODEL), blk(D_MODEL, FF_BLK), blk(D_MODEL, FF_BLK),
                  blk(3, FF_BLK), blk(1, FF_BLK), tok(D_MODEL), tok(1), _full((1, D_MODEL))],
        out_specs=[tokj, tokj, tok(D_MODEL), _full((N_SHARD, STAT_ROWS, FF_BLK)), _full((STAT_ROWS, D_MODEL))], out_shape=outs,
        scratch_shapes=[pltpu.VMEM((TM, D_MODEL), F32)],
        compiler_params=_cp(dimension_semantics=("arbitrary", "arbitrary")),
    )(dr2, dr2, a_pre, a_pre, a_pre, b, b, w_down, w_a, w_b, conv_w, conv_b, xhat1, rstd1, ln1_g)


def _mix_bwd(dr1, w_o, hu, hz, mixed, attn, ln_z_g, ln_z_b, w_s, dep):
    t = dr1.shape[0]
    nchunk = TM // BLK

    def body(dr_ref, wo_ref, hu_ref, hz_ref, mx_ref, attn_ref, g_ref, b_ref, ws_ref, grp_ref, red_ref, dep_ref,
             do1_ref, do4_ref, do16_ref, dl1_ref, dl4_ref, dl16_ref, duz_ref, dws_ref, dbs_ref, zstat_ref,
             wm_scr, dzn_scr, dbsum_scr, do_scr, dl_scr):
        @pl.when(pl.program_id(0) == 0)
        def _():
            row = lax.broadcasted_iota(jnp.int32, (BLK, BLK), 0)
            col = lax.broadcasted_iota(jnp.int32, (BLK, BLK), 1)
            for g in range(N_HEADS):
                wm_scr[g] = jnp.where(col <= row, ws_ref[g], 0.0).astype(MXU)
            dws_ref[...] = jnp.zeros_like(dws_ref)
            dbsum_scr[...] = jnp.zeros_like(dbsum_scr)
            zstat_ref[...] = jnp.zeros_like(zstat_ref)

        dcat = _dot_nt(dr_ref[...], wo_ref[...])
        dattn = dcat[:, 0:D_ATTN]
        do1_ref[...] = dattn.astype(MXU)
        for cc, val in enumerate(_chunks(dattn)):
            do_scr[cc] = val
        _to_planes(do4_ref, do_scr, DILATIONS[1], LANE_CHUNKS, MXU)
        _to_planes(do16_ref, do_scr, DILATIONS[2], LANE_CHUNKS, MXU)
        delta = _dot_exact(dattn * attn_ref[...], red_ref[...])
        dl1_ref[...] = delta
        dl_scr[0] = delta
        _to_planes(dl4_ref, dl_scr, DILATIONS[1], 1, F32)
        _to_planes(dl16_ref, dl_scr, DILATIONS[2], 1, F32)
        dgm = dcat[:, D_ATTN:]
        hu, hz = hu_ref[...], hz_ref[...]
        u = _gelu(hu)
        duz_ref[:, 0:D_GMLP] = (dgm * mx_ref[...] * _gelu_grad(hu)).astype(MXU)
        dmixed = dgm * u
        dmb = dmixed.astype(MXU)
        zhat, rstd = _ln_fwd(_gelu(hz))
        znb = (zhat * g_ref[...] + b_ref[...]).astype(MXU)
        dbs_acc = jnp.zeros((BLK, D_GMLP), F32)
        for ch in range(nchunk):
            rows = slice(ch * BLK, (ch + 1) * BLK)
            dbs_acc = dbs_acc + dmixed[rows]
            for g in range(N_HEADS):
                cols = slice(g * HEAD_DIM, (g + 1) * HEAD_DIM)
                dzn_scr[rows, cols] = _dot_tn(wm_scr[g], dmb[rows, cols])
                dws_ref[g] += _dot_nt(dmb[rows, cols], znb[rows, cols])
        dbsum_scr[...] += dbs_acc
        dzn = dzn_scr[...]
        zstat_ref[0:1, :] += _colsum(dzn * zhat)
        zstat_ref[1:2, :] += _colsum(dzn)
        duz_ref[:, D_GMLP:] = (_ln_bwd(dzn, zhat, rstd, g_ref[...]) * _gelu_grad(hz)).astype(MXU)

        @pl.when(pl.program_id(0) == nt - 1)
        def _():
            row = lax.broadcasted_iota(jnp.int32, (BLK, BLK), 0)
            col = lax.broadcasted_iota(jnp.int32, (BLK, BLK), 1)
            for g in range(N_HEADS):
                dws_ref[g] = jnp.where(col <= row, dws_ref[g], 0.0)
            dbs_ref[...] = lax.dot_general(grp_ref[...], dbsum_scr[...], (((1,), (1,)), ((), ())),
                                           precision=lax.Precision.HIGHEST, preferred_element_type=F32)

    nt = t // TM
    tok = lambda w: pl.BlockSpec((TM, w), lambda i: (i, 0))
    grp = jnp.asarray((np.arange(D_GMLP)[None, :] // HEAD_DIM == np.arange(N_HEADS)[:, None]).astype(np.float32))
    red = _head_reduce()
    outs = [_perm_shape(t, d, D_ATTN, MXU) for d in DILATIONS] + [_perm_shape(t, d, 128, F32) for d in DILATIONS] + [
        jax.ShapeDtypeStruct((t, 2 * D_GMLP), MXU),
        jax.ShapeDtypeStruct((N_HEADS, BLK, BLK), F32), jax.ShapeDtypeStruct((N_HEADS, BLK), F32),
        jax.ShapeDtypeStruct((STAT_ROWS, D_GMLP), F32)]
    return pl.pallas_call(
        body, name="mix_bwd", grid=(t // TM,),
        in_specs=[tok(D_MODEL), _full(w_o.shape), tok(D_GMLP), tok(D_GMLP), tok(D_GMLP), tok(D_ATTN), _full(ln_z_g.shape),
                  _full(ln_z_b.shape), _full(w_s.shape), _full(grp.shape), _full(red.shape), pl.BlockSpec(memory_space=pl.ANY)],
        out_specs=[_perm_tile(d, D_ATTN) for d in DILATIONS] + [_perm_tile(d, 128) for d in DILATIONS]
        + [tok(2 * D_GMLP), _full((N_HEADS, BLK, BLK)), _full((N_HEADS, BLK)), _full((STAT_ROWS, D_GMLP))],
        out_shape=outs,
        scratch_shapes=[pltpu.VMEM((N_HEADS, BLK, BLK), MXU), pltpu.VMEM((TM, D_GMLP), F32), pltpu.VMEM((BLK, D_GMLP), F32),
                        pltpu.VMEM((LANE_CHUNKS, TM, 128), F32), pltpu.VMEM((1, TM, 128), F32)],
        compiler_params=_cp(dimension_semantics=("arbitrary",)),
    )(dr1, w_o, hu, hz, mixed, attn, ln_z_g, ln_z_b, w_s, grp, red, dep)


def _dx_in(dqs, dks, dvs, duz, dr1, w_in, c_tab, s1_tab, s2_tab):
    t = dr1.shape[0]

    def body(dq1, dq4, dq16, dk1, dk4, dk16, dv1, dv4, dv16, duz_ref, dr_ref, w_ref, c_ref, s1_ref, s2_ref,
             dh_ref, dx_ref, acc_scr):
        sums = []
        for part, (g1, g4, g16) in enumerate(((dq1, dq4, dq16), (dk1, dk4, dk16), (dv1, dv4, dv16))):
            acc = acc_scr.at[pl.ds(part * LANE_CHUNKS, LANE_CHUNKS)]
            for cc in range(LANE_CHUNKS):
                acc[cc] = g1[:, cc * 128:(cc + 1) * 128]
            _from_planes(g4, acc, DILATIONS[1], LANE_CHUNKS, accumulate=True)
            _from_planes(g16, acc, DILATIONS[2], LANE_CHUNKS, accumulate=True)
            sums.append(_unchunk(acc_scr, LANE_CHUNKS, part * LANE_CHUNKS))
        c, s1, s2 = _tile_heads(c_ref[...]), _tile_heads(s1_ref[...]), _tile_heads(s2_ref[...])
        dh_ref[:, 0:D_ATTN] = _rope_apply_t(sums[0] * (1.0 / math.sqrt(HEAD_DIM)), c, s1, s2).astype(MXU)
        dh_ref[:, D_ATTN:2 * D_ATTN] = _rope_apply_t(sums[1], c, s1, s2).astype(MXU)
        dh_ref[:, 2 * D_ATTN:3 * D_ATTN] = sums[2].astype(MXU)
        dh_ref[:, 3 * D_ATTN:] = duz_ref[...]
        dx = ALPHA * dr_ref[...]
        for j in range(N_SHARD):
            dx = dx + _dot_nt(dh_ref[:, j * W_IN_BLK:(j + 1) * W_IN_BLK], w_ref[j])
        dx_ref[...] = dx

    tok = lambda w: pl.BlockSpec((TM, w), lambda i: (i, 0))
    outs = [jax.ShapeDtypeStruct((t, D_IN), MXU), jax.ShapeDtypeStruct((t, D_MODEL), F32)]
    return pl.pallas_call(
        body, name="dx_in", grid=(t // TM,),
        in_specs=[_perm_tile(d, D_ATTN) for d in DILATIONS] * 3
        + [tok(2 * D_GMLP), tok(D_MODEL), _full(w_in.shape), tok(128), tok(128), tok(128)],
        out_specs=[tok(D_IN), tok(D_MODEL)], out_shape=outs,
        scratch_shapes=[pltpu.VMEM((3 * LANE_CHUNKS, TM, 128), F32)],
        compiler_params=_cp(dimension_semantics=("arbitrary",)),
    )(*dqs, *dks, *dvs, duz, dr1, w_in, c_tab, s1_tab, s2_tab)


def _wgrad(name, x, dy, x_spec, dy_spec, out_spec, out_shape, grid):
    def body(x_ref, dy_ref, o_ref):
        o_ref[...] = _dot_tn(x_ref[...], dy_ref[...])

    return pl.pallas_call(
        body, name=name, grid=grid, in_specs=[x_spec, dy_spec], out_specs=out_spec,
        out_shape=jax.ShapeDtypeStruct(out_shape, F32),
        compiler_params=_cp(dimension_semantics=("arbitrary",) * len(grid)),
    )(x, dy)


def _local_step(x, p, positions, target, w_in, w_o, start_dep, late_weights, early_grads,
                ln_z_g, ln_z_b, w_s, b_s, ln1_g, ln1_b, conv_b, ln2_g, ln2_b, b_g, ln3_g, ln3_b):
    t = x.shape[0]
    half = TM
    c_tab, s1_tab, s2_tab = _rope_tables(positions, t)
    b_full = jnp.repeat(jnp.transpose(b_s[0]), HEAD_DIM, axis=1)
    conv_b4 = conv_b.reshape(N_SHARD, 1, FF_BLK)
    *qkvs, hu, hz, mixed, gm = _qkvuz(x, w_in, c_tab, s1_tab, s2_tab, ln_z_g, ln_z_b, w_s[0], b_full, start_dep)
    branches = [_attn_fwd(qkv, d) for qkv, d in zip(qkvs, DILATIONS)]
    attn, *lses, cat, xhat1, rstd1, x1b = _mix_ln1(
        [o for o, _ in branches], [l for _, l in branches], gm, x, w_o, ln1_g, ln1_b)
    w_a, w_b, conv_w, w_down, w_g, w_p = late_weights(branches[-1][1])
    a_pre, b_act, f = _ffn_in(x1b, w_a, w_b, conv_w, conv_b4)
    xhat2, rstd2, x2b = _ffn_out_ln2(f, w_down, xhat1, ln1_g, ln1_b, ln2_g, ln2_b)
    dr2, dgp, dpp, stat3 = _ple_loss_bwd(xhat2, rstd2, p, target, ln2_g, ln2_b, w_g, b_g, w_p, ln3_g, ln3_b)
    da_pre, dbb, dr1, cstat, stat1 = _ffn_bwd(dr2, a_pre, b_act, w_down, w_a, w_b, conv_w, conv_b4, xhat1, rstd1, ln1_g)

    full_t = lambda w, im: pl.BlockSpec((t, w), im)
    ffj = pl.BlockSpec((None, t, FF_BLK), lambda j, kk: (j, 0, 0))
    early = dict(
        w_ple_gate=_wgrad("dw_g", x2b, dgp, full_t(half, lambda kk, n: (0, kk)), full_t(half, lambda kk, n: (0, n)),
                          pl.BlockSpec((half, half), lambda kk, n: (kk, n)), (D_MODEL, D_MODEL), (2, 2)),
        w_ple_in=_wgrad("dw_p", p, dpp, full_t(D_PLE, lambda j: (0, 0)), full_t(ROW_BLK, lambda j: (0, j)),
                        pl.BlockSpec((None, D_PLE, ROW_BLK), lambda j: (j, 0, 0)), (N_SHARD, D_PLE, ROW_BLK), (N_SHARD,)),
        w_ff_down=_wgrad("dw_down", f, dr2, ffj, full_t(half, lambda j, n: (0, n)),
                         pl.BlockSpec((None, FF_BLK, half), lambda j, n: (j, 0, n)), (N_SHARD, FF_BLK, D_MODEL), (N_SHARD, 2)),
        w_ff_a=_wgrad("dw_a", x1b, da_pre, full_t(half, lambda j, kk: (0, kk)), ffj,
                      pl.BlockSpec((None, half, FF_BLK), lambda j, kk: (j, kk, 0)), (N_SHARD, D_MODEL, FF_BLK), (N_SHARD, 2)),
        w_ff_b=_wgrad("dw_b", x1b, dbb, full_t(half, lambda j, kk: (0, kk)), ffj,
                      pl.BlockSpec((None, half, FF_BLK), lambda j, kk: (j, kk, 0)), (N_SHARD, D_MODEL, FF_BLK), (N_SHARD, 2)),
        w_o=_wgrad("dw_o", cat, dr1, full_t(half, lambda kk, n: (0, kk)), full_t(half, lambda kk, n: (0, n)),
                   pl.BlockSpec((half, half), lambda kk, n: (kk, n)), (D_MODEL, D_MODEL), (2, 2)))
    dep = early_grads(early)

    do1, do4, do16, dl1, dl4, dl16, duz, dws, dbs, zstat = _mix_bwd(
        dr1, w_o, hu, hz, mixed, attn, ln_z_g, ln_z_b, w_s[0], dep)
    dqkv = [_attn_bwd(qkv, do, lse, dl, d)
            for qkv, do, lse, dl, d in zip(qkvs, (do1, do4, do16), lses, (dl1, dl4, dl16), DILATIONS)]
    dh, grad_x = _dx_in([g[0] for g in dqkv], [g[1] for g in dqkv], [g[2] for g in dqkv], duz, dr1, w_in,
                        c_tab, s1_tab, s2_tab)
    g_w_in = _wgrad("dw_in", x, dh, full_t(half, lambda j, kk: (0, kk)), full_t(W_IN_BLK, lambda j, kk: (0, j)),
                    pl.BlockSpec((None, half, W_IN_BLK), lambda j, kk: (j, kk, 0)), (N_SHARD, D_MODEL, W_IN_BLK), (N_SHARD, 2))
    return grad_x, g_w_in, dws, dbs, (stat3, stat1, zstat, cstat)


def _rows_tile(r, mult, cap=512):
    return max(d for d in range(mult, min(r, cap) + 1, mult) if r % d == 0)


def _grid_spec(grid, in_specs, out_specs):
    return pltpu.PrefetchScalarGridSpec(num_scalar_prefetch=1, grid=grid, in_specs=in_specs, out_specs=out_specs)


def _place_shard(name, w, chip, dtype):
    r, c = w.shape
    tr = r if r % 16 else _rows_tile(r, 16)

    def body(s_ref, w_ref, o_ref):
        o_ref[...] = w_ref[...].astype(dtype)

    return pl.pallas_call(
        body, name=name,
        grid_spec=_grid_spec((r // tr,), [pl.BlockSpec((tr, c), lambda i, s: (i, 0))],
                             pl.BlockSpec((None, tr, c), lambda i, s: (s[0], i, 0))),
        out_shape=jax.ShapeDtypeStruct((N_SHARD, r, c), dtype), compiler_params=_cp())(chip, w)


def _pair_sum_bf16(name, mine, got, core):
    n, h, c = got.shape
    tr = _rows_tile(h, 16)
    nh = h // tr

    def body(s_ref, a_ref, b_ref, o_ref):
        o_ref[...] = (a_ref[...] + b_ref[...]).astype(BF16)

    spec = pl.BlockSpec((None, tr, c), lambda k, i, s: (k, i, 0))
    return pl.pallas_call(
        body, name=name,
        grid_spec=_grid_spec((n, nh), [pl.BlockSpec((None, tr, c), lambda k, i, s: (k, s[0] * nh + i, 0)), spec], spec),
        out_shape=jax.ShapeDtypeStruct((n, h, c), BF16), compiler_params=_cp())(core, mine, got)


def _chip_sum(name, own, landed, place):
    n, h, c = own.shape
    tr = _rows_tile(h, 16)
    nh = h // tr

    def body(s_ref, a_ref, b_ref, c_ref, d_ref, o_ref):
        o_ref[...] = ((a_ref[...].astype(F32) + b_ref[...].astype(F32)) + c_ref[...].astype(F32)) + d_ref[...].astype(F32)

    def slot(d):
        return pl.BlockSpec((None, tr, c), lambda i, s: ((s[0] + d) % n, i, 0))

    return pl.pallas_call(
        body, name=name,
        grid_spec=_grid_spec((nh,), [slot(0), slot(1), slot(2), slot(3)], pl.BlockSpec((tr, c), lambda i, s: (s[1] * nh + i, 0))),
        out_shape=jax.ShapeDtypeStruct((2 * h, c), F32), compiler_params=_cp())(place, own, landed, landed, landed)


def _adamw_math(w, g, m, v):
    m = ADAM_B1 * m + (1.0 - ADAM_B1) * g
    v = ADAM_B2 * v + (1.0 - ADAM_B2) * (g * g)
    m_hat = m / (1.0 - ADAM_B1 ** ADAM_STEP)
    v_hat = v / (1.0 - ADAM_B2 ** ADAM_STEP)
    delta = -ADAM_LR * (m_hat / (jnp.sqrt(v_hat) + ADAM_EPS) + ADAM_WD * w)
    return delta, m, v


def _adamw_big(name, w, g, m, v):
    _, r, c = w.shape
    tr = _rows_tile(r, 8, cap=256)

    def body(w_ref, g_ref, m_ref, v_ref, d_ref, nm_ref, nv_ref):
        d_ref[...], nm_ref[...], nv_ref[...] = _adamw_math(w_ref[...], g_ref[...], m_ref[...], v_ref[...])

    s3 = pl.BlockSpec((None, tr, c), lambda i: (0, i, 0))
    s2 = pl.BlockSpec((tr, c), lambda i: (i, 0))
    return pl.pallas_call(body, name=name, grid=(r // tr,), in_specs=[s3, s2, s3, s3], out_specs=[s3, s3, s3],
                          out_shape=[jax.ShapeDtypeStruct(w.shape, F32)] * 3, compiler_params=_cp())(w, g, m, v)


MESH = pl.DeviceIdType.MESH
ANY = pl.BlockSpec(memory_space=pl.ANY)


def _place():
    x, y, c = lax.axis_index("x"), lax.axis_index("y"), lax.axis_index("c")
    chips = [(1 - x, y), (x, 1 - y), (1 - x, 1 - y)]
    return x, y, c, 2 * x + y, chips


def _remote(src, dst, send_sem, recv_sem, dev):
    return pltpu.make_async_remote_copy(src_ref=src, dst_ref=dst, send_sem=send_sem, recv_sem=recv_sem,
                                        device_id=dev, device_id_type=MESH)


def _half(ref, hc, rows):
    return ref.at[pl.ds(hc * (rows // 2), rows // 2)]


def _allgather(stacks, split):
    n = len(stacks)

    def body(*refs):
        outs = refs[n:2 * n]
        send, recv, fsend, frecv = refs[2 * n:]
        x, y, c, j, chips = _place()
        rows = [s.shape[1] for s in stacks]

        def piece(a, slot, hc):
            return _half(outs[a].at[slot], hc, rows[a]) if split[a] else outs[a].at[slot]

        def direct(a, t, slot, dev):
            return _remote(piece(a, slot, c), piece(a, slot, c), send.at[a, t], recv.at[a, t], dev)

        sends = [direct(a, t, j, (*chips[t], c)) for a in range(n) for t in range(3)]
        for cp in sends:
            cp.start()
        fwd = []
        for t, (px, py) in enumerate(chips):
            jt = 2 * px + py
            for a in range(n):
                direct(a, t, jt, (px, py, c)).wait_recv()
                if split[a]:
                    cp = _remote(piece(a, jt, c), piece(a, jt, c), fsend.at[a, t], frecv.at[a, t], (x, y, 1 - c))
                    cp.start()
                    fwd.append(cp)
        for t, (px, py) in enumerate(chips):
            jt = 2 * px + py
            for a in range(n):
                if split[a]:
                    _remote(piece(a, jt, 1 - c), piece(a, jt, 1 - c), fsend.at[a, t], frecv.at[a, t], (x, y, 1 - c)).wait_recv()
        for cp in sends + fwd:
            cp.wait_send()

    sem = pltpu.SemaphoreType.DMA
    return pl.pallas_call(
        body, name="allgather_weights", in_specs=[ANY] * n, out_specs=[ANY] * n,
        out_shape=[jax.ShapeDtypeStruct(s.shape, s.dtype) for s in stacks],
        input_output_aliases={a: a for a in range(n)},
        scratch_shapes=[sem((n, 3)), sem((n, 3)), sem((n, 3)), sem((n, 3))],
    )(*stacks)


def _sibling_swap(grads):
    n = len(grads)

    def body(*refs):
        ins, got = refs[:n], refs[n:2 * n]
        send, recv = refs[2 * n:]
        x, y, c, _, _ = _place()
        cps = []
        for a in range(n):
            h = grads[a].shape[1] // 2
            cp = _remote(ins[a].at[:, pl.ds((1 - c) * h, h)], got[a], send.at[a], recv.at[a], (x, y, 1 - c))
            cp.start()
            cps.append(cp)
        for cp in cps:
            cp.wait_recv()
            cp.wait_send()

    sem = pltpu.SemaphoreType.DMA
    halves = [jax.ShapeDtypeStruct((g.shape[0], g.shape[1] // 2, g.shape[2]), g.dtype) for g in grads]
    return pl.pallas_call(body, name="rs_sibling_swap", in_specs=[ANY] * n, out_specs=[ANY] * n, out_shape=halves,
                          scratch_shapes=[sem((n,)), sem((n,))])(*grads)


def _chip_exchange(parts):
    n = len(parts)

    def body(*refs):
        ins, outs = refs[:n], refs[n:2 * n]
        send, recv = refs[2 * n:]
        x, y, c, j, chips = _place()
        sends = []
        for t, (px, py) in enumerate(chips):
            jt = 2 * px + py
            for a in range(n):
                cp = _remote(ins[a].at[jt], outs[a].at[j], send.at[a, t], recv.at[a, t], (px, py, c))
                cp.start()
                sends.append(cp)
        for t, (px, py) in enumerate(chips):
            jt = 2 * px + py
            for a in range(n):
                _remote(ins[a].at[jt], outs[a].at[jt], send.at[a, t], recv.at[a, t], (px, py, c)).wait_recv()
        for cp in sends:
            cp.wait_send()

    sem = pltpu.SemaphoreType.DMA
    return pl.pallas_call(body, name="rs_chip_exchange", in_specs=[ANY] * n, out_specs=[ANY] * n,
                          out_shape=[jax.ShapeDtypeStruct(p.shape, p.dtype) for p in parts],
                          scratch_shapes=[sem((n, 3)), sem((n, 3))])(*parts)


def _sibling_join(blocks):
    n = len(blocks)

    def body(*refs):
        outs = refs[n:2 * n]
        send, recv = refs[2 * n:]
        x, y, c, _, _ = _place()
        cps = []
        for a in range(n):
            h = blocks[a].shape[0] // 2
            mine = outs[a].at[pl.ds(c * h, h)]
            cp = _remote(mine, mine, send.at[a], recv.at[a], (x, y, 1 - c))
            cp.start()
            cps.append(cp)
        for a, cp in enumerate(cps):
            h = blocks[a].shape[0] // 2
            theirs = outs[a].at[pl.ds((1 - c) * h, h)]
            _remote(theirs, theirs, send.at[a], recv.at[a], (x, y, 1 - c)).wait_recv()
            cp.wait_send()

    sem = pltpu.SemaphoreType.DMA
    return pl.pallas_call(body, name="rs_sibling_join", in_specs=[ANY] * n, out_specs=[ANY] * n,
                          out_shape=[jax.ShapeDtypeStruct(b_.shape, b_.dtype) for b_ in blocks],
                          input_output_aliases={a: a for a in range(n)},
                          scratch_shapes=[sem((n,)), sem((n,))])(*blocks)


HBM = pl.BlockSpec(memory_space=pltpu.HBM)
SEM = pl.BlockSpec(memory_space=pltpu.SEMAPHORE)
TOKEN = jax.ShapeDtypeStruct((8, 128), F32)


def _in_flight_params():
    return pltpu.CompilerParams(has_side_effects=pltpu.SideEffectType.DATAFLOW_SIDE_EFFECTING)


def _in_hbm(a):
    return pltpu.with_memory_space_constraint(a, pltpu.HBM)


def _gather_piece(ref, rows, split, slot, hc):
    return _half(ref.at[slot], hc, rows) if split else ref.at[slot]


def _gather_start(stacks, split, after):
    n = len(stacks)

    def body(*refs):
        ins = refs[:n]
        send, recv = refs[n + 1], refs[n + 2]
        token = refs[2 * n + 3]
        _, _, c, j, chips = _place()
        for a in range(n):
            mine = _gather_piece(ins[a], stacks[a].shape[1], split[a], j, c)
            for t in range(3):
                _remote(mine, mine, send.at[3 * a + t], recv.at[3 * a + t], (*chips[t], c)).start()
        token[...] = jnp.zeros_like(token)

    sems = pltpu.SemaphoreType.DMA((3 * n,))
    res = pl.pallas_call(
        body, name="gather_start", in_specs=[HBM] * n + [ANY],
        out_specs=[SEM, SEM] + [HBM] * n + [pl.BlockSpec(memory_space=pltpu.VMEM)],
        out_shape=[sems, sems] + [pltpu.HBM(s.shape, s.dtype) for s in stacks] + [TOKEN],
        input_output_aliases={a: a + 2 for a in range(n)}, compiler_params=_in_flight_params(),
    )(*[_in_hbm(s) for s in stacks], after)
    return res[0], res[1], res[2:2 + n], res[2 + n]


def _gather_wait(send, recv, stacks, split, after):
    n = len(stacks)

    def body(*refs):
        ins = refs[:n]
        send_ref, recv_ref = refs[n], refs[n + 1]
        _, _, c, j, chips = _place()
        for a in range(n):
            rows = stacks[a].shape[1]
            mine = _gather_piece(ins[a], rows, split[a], j, c)
            for t, (px, py) in enumerate(chips):
                theirs = _gather_piece(ins[a], rows, split[a], 2 * px + py, c)
                _remote(mine, mine, send_ref.at[3 * a + t], recv_ref.at[3 * a + t], (px, py, c)).wait_send()
                _remote(theirs, theirs, send_ref.at[3 * a + t], recv_ref.at[3 * a + t], (px, py, c)).wait_recv()

    return pl.pallas_call(
        body, name="gather_wait", in_specs=[HBM] * n + [SEM, SEM, ANY], out_specs=[HBM] * n,
        out_shape=[pltpu.HBM(s.shape, s.dtype) for s in stacks],
        input_output_aliases={a: a for a in range(n)}, compiler_params=_in_flight_params(),
    )(*stacks, send, recv, after)


def _gather_forward(stacks, split):
    idx = [a for a in range(len(stacks)) if split[a]]
    n = len(idx)

    def body(*refs):
        outs = refs[n:2 * n]
        send, recv = refs[2 * n:]
        x, y, c, _, chips = _place()
        sends = []
        for t, (px, py) in enumerate(chips):
            for a in range(n):
                blk = _half(outs[a].at[2 * px + py], c, stacks[idx[a]].shape[1])
                cp = _remote(blk, blk, send.at[a, t], recv.at[a, t], (x, y, 1 - c))
                cp.start()
                sends.append(cp)
        for t, (px, py) in enumerate(chips):
            for a in range(n):
                blk = _half(outs[a].at[2 * px + py], 1 - c, stacks[idx[a]].shape[1])
                _remote(blk, blk, send.at[a, t], recv.at[a, t], (x, y, 1 - c)).wait_recv()
        for cp in sends:
            cp.wait_send()

    sem = pltpu.SemaphoreType.DMA
    res = pl.pallas_call(
        body, name="gather_forward", in_specs=[ANY] * n, out_specs=[ANY] * n,
        out_shape=[jax.ShapeDtypeStruct(stacks[a].shape, stacks[a].dtype) for a in idx],
        input_output_aliases={a: a for a in range(n)}, scratch_shapes=[sem((n, 3)), sem((n, 3))],
    )(*[stacks[a] for a in idx])
    out = list(stacks)
    for a, r in zip(idx, res):
        out[a] = r
    return out


def _exchange_start(parts):
    n = len(parts)

    def body(*refs):
        ins, lands = refs[:n], refs[n:2 * n]
        send, recv = refs[2 * n], refs[2 * n + 1]
        token = refs[4 * n + 2]
        _, _, c, j, chips = _place()
        for t, (px, py) in enumerate(chips):
            for a in range(n):
                _remote(ins[a].at[2 * px + py], lands[a].at[j], send.at[3 * a + t], recv.at[3 * a + t], (px, py, c)).start()
        token[...] = jnp.zeros_like(token)

    sems = pltpu.SemaphoreType.DMA((3 * n,))
    bufs = [pltpu.HBM(p.shape, p.dtype) for p in parts]
    res = pl.pallas_call(
        body, name="exchange_start", in_specs=[HBM] * (2 * n),
        out_specs=[SEM, SEM] + [HBM] * (2 * n) + [pl.BlockSpec(memory_space=pltpu.VMEM)],
        out_shape=[sems, sems] + bufs + bufs + [TOKEN],
        input_output_aliases={a: a + 2 for a in range(2 * n)}, compiler_params=_in_flight_params(),
    )(*[_in_hbm(p) for p in parts], *[_in_hbm(lax.empty(p.shape, p.dtype)) for p in parts])
    return res[0], res[1], res[2:2 + n], res[2 + n:2 + 2 * n], res[2 + 2 * n]


def _exchange_wait(send, recv, parts, lands, after):
    n = len(parts)

    def body(*refs):
        ins, lnd = refs[:n], refs[n:2 * n]
        send_ref, recv_ref = refs[2 * n], refs[2 * n + 1]
        _, _, c, j, chips = _place()
        for t, (px, py) in enumerate(chips):
            jt = 2 * px + py
            for a in range(n):
                _remote(ins[a].at[jt], lnd[a].at[j], send_ref.at[3 * a + t], recv_ref.at[3 * a + t], (px, py, c)).wait_send()
                _remote(ins[a].at[jt], lnd[a].at[jt], send_ref.at[3 * a + t], recv_ref.at[3 * a + t], (px, py, c)).wait_recv()

    bufs = [pltpu.HBM(p.shape, p.dtype) for p in parts]
    res = pl.pallas_call(
        body, name="exchange_wait", in_specs=[HBM] * (2 * n) + [SEM, SEM, ANY], out_specs=[HBM] * (2 * n),
        out_shape=bufs + bufs, input_output_aliases={a: a for a in range(2 * n)}, compiler_params=_in_flight_params(),
    )(*parts, *lands, send, recv, after)
    return res[:n], res[n:]


def _allreduce_small(arrs):
    n = len(arrs)

    def body(*refs):
        ins, outs = refs[:n], refs[n:2 * n]
        sib, chip = refs[2 * n:3 * n], refs[3 * n:4 * n]
        ssend, srecv, csend, crecv = refs[4 * n:]
        x, y, c, j, chips = _place()
        swaps = [_remote(ins[a], sib[a], ssend.at[a], srecv.at[a], (x, y, 1 - c)) for a in range(n)]
        for cp in swaps:
            cp.start()
        sends = []
        for a in range(n):
            swaps[a].wait_recv()
            chip[a][j] = ins[a][...] + sib[a][...]
            for t, (px, py) in enumerate(chips):
                cp = _remote(chip[a].at[j], chip[a].at[j], csend.at[a, t], crecv.at[a, t], (px, py, c))
                cp.start()
                sends.append(cp)
        for a in range(n):
            for t, (px, py) in enumerate(chips):
                jt = 2 * px + py
                _remote(chip[a].at[jt], chip[a].at[jt], csend.at[a, t], crecv.at[a, t], (px, py, c)).wait_recv()
            outs[a][...] = ((chip[a][0] + chip[a][1]) + chip[a][2]) + chip[a][3]
        for cp in swaps + sends:
            cp.wait_send()

    sem = pltpu.SemaphoreType.DMA
    vm = pl.BlockSpec(memory_space=pltpu.VMEM)
    return pl.pallas_call(
        body, name="allreduce_small", in_specs=[vm] * n, out_specs=[vm] * n,
        out_shape=[jax.ShapeDtypeStruct(a.shape, F32) for a in arrs],
        scratch_shapes=[pltpu.VMEM(a.shape, F32) for a in arrs] + [pltpu.VMEM((N_SHARD, *a.shape), F32) for a in arrs]
        + [sem((n,)), sem((n,)), sem((n, 3)), sem((n, 3))],
        compiler_params=_cp(),
    )(*arrs)


SMALL_1024 = ("ln1_g", "ln1_b", "ln2_g", "ln2_b", "b_ple_gate", "ln3_g", "ln3_b")


def _adamw_small(red3, red1, redz, g_conv_w, redc, red_ws, red_bs, params):
    shape2d = {"ln_z_g": (1, D_GMLP), "ln_z_b": (1, D_GMLP), "w_s": (N_HEADS * BLK, BLK), "b_s": (N_HEADS, BLK),
               "conv_w": (3, FF_BLK), "conv_b": (N_SHARD, FF_BLK), **{k: (1, D_MODEL) for k in SMALL_1024}}
    names = list(shape2d)
    flat = [a.reshape(shape2d[k]) for k in names for a in params[k]]

    def body(r3, r1, rz, gcw, rc, rws, rbs, *refs):
        ins, outs = refs[:3 * len(names)], refs[3 * len(names):]

        def grad_of(k):
            if k == "w_s":
                return rws[...]
            if k == "b_s":
                return rbs[...]
            if k == "conv_w":
                return gcw[0:3, :]
            if k == "conv_b":
                return jnp.concatenate([rc[j * STAT_ROWS + 3:j * STAT_ROWS + 4, :] for j in range(N_SHARD)], axis=0)
            src, row = {"ln3_g": (r3, 0), "ln3_b": (r3, 1), "b_ple_gate": (r3, 2), "ln2_g": (r3, 3), "ln2_b": (r3, 4),
                        "ln1_g": (r1, 0), "ln1_b": (r1, 1), "ln_z_g": (rz, 0), "ln_z_b": (rz, 1)}[k]
            return src[row:row + 1, :]

        for i, k in enumerate(names):
            w_ref, m_ref, v_ref = ins[3 * i:3 * i + 3]
            g_ref, d_ref, nm_ref, nv_ref = outs[4 * i:4 * i + 4]
            g = grad_of(k)
            g_ref[...] = g
            d_ref[...], nm_ref[...], nv_ref[...] = _adamw_math(w_ref[...], g, m_ref[...], v_ref[...])

    res = pl.pallas_call(
        body, name="adamw_small",
        out_shape=[jax.ShapeDtypeStruct(shape2d[k], F32) for k in names for _ in range(4)],
        compiler_params=_cp(),
    )(red3, red1, redz, g_conv_w, redc, red_ws, red_bs, *flat)
    return {k: tuple(r.reshape(params[k][0].shape) for r in res[4 * i:4 * i + 4]) for i, k in enumerate(names)}


WEIGHTS = ("w_in", "ln_z_g", "ln_z_b", "w_s", "b_s", "w_o", "ln1_g", "ln1_b", "w_ff_a", "w_ff_b", "conv_w", "conv_b",
           "w_ff_down", "ln2_g", "ln2_b", "w_ple_gate", "b_ple_gate", "w_ple_in", "ln3_g", "ln3_b")
BIG = ("w_in", "w_o", "w_ff_a", "w_ff_b", "w_ff_down", "w_ple_gate", "w_ple_in")
LATE = ("w_ff_a", "w_ff_b", "w_ff_down", "w_ple_gate", "w_ple_in", "conv_w")


def kernel(x, p, positions, w_in, ln_z_g, ln_z_b, w_s, b_s, w_o, ln1_g, ln1_b, w_ff_a, w_ff_b, conv_w, conv_b, w_ff_down, ln2_g, ln2_b, w_ple_gate, b_ple_gate, w_ple_in, ln3_g, ln3_b, loss_target, m_w_in, m_ln_z_g, m_ln_z_b, m_w_s, m_b_s, m_w_o, m_ln1_g, m_ln1_b, m_w_ff_a, m_w_ff_b, m_conv_w, m_conv_b, m_w_ff_down, m_ln2_g, m_ln2_b, m_w_ple_gate, m_b_ple_gate, m_w_ple_in, m_ln3_g, m_ln3_b, v_w_in, v_ln_z_g, v_ln_z_b, v_w_s, v_b_s, v_w_o, v_ln1_g, v_ln1_b, v_w_ff_a, v_w_ff_b, v_conv_w, v_conv_b, v_w_ff_down, v_ln2_g, v_ln2_b, v_w_ple_gate, v_b_ple_gate, v_w_ple_in, v_ln3_g, v_ln3_b):
    args = locals()
    w = {k: args[k] for k in WEIGHTS}
    m = {k: args["m_" + k] for k in WEIGHTS}
    v = {k: args["v_" + k] for k in WEIGHTS}

    chip = 2 * lax.axis_index("x") + lax.axis_index("y")
    place = jnp.stack([chip, lax.axis_index("c")]).astype(jnp.int32)
    stack = {k: _place_shard(f"cast_{k}", w[k][0], place, MXU) for k in BIG}
    stack["conv_w"] = _place_shard("place_conv_w", w["conv_w"][0], place, F32)
    w_in_full, w_o_full = _allgather([stack["w_in"], stack["w_o"]], [True, True])
    split_late = [k != "conv_w" for k in LATE]
    g_send, g_recv, late_flight, start_dep = _gather_start([stack[k] for k in LATE], split_late, w_in_full)

    def late_weights(after):
        landed = _gather_wait(g_send, g_recv, late_flight, split_late, after)
        fw = dict(zip(LATE, _gather_forward(landed, split_late)))
        return (fw["w_ff_a"], fw["w_ff_b"], fw["conv_w"], fw["w_ff_down"], fw["w_ple_gate"].reshape(D_MODEL, D_MODEL),
                fw["w_ple_in"])

    flight = {}

    def early_grads(grads):
        names = list(grads)
        stacked = [grads[k].reshape(N_SHARD, *w[k].shape[1:]) for k in names]
        got = _sibling_swap(stacked)
        pair = [_pair_sum_bf16(f"rs_pair_{k}", s, g, place[1:2]) for k, s, g in zip(names, stacked, got)]
        e_send, e_recv, pair, lands, dep = _exchange_start(pair)
        flight.update(names=names, send=e_send, recv=e_recv, pair=pair, lands=lands)
        return dep

    grad_x, g_w_in, dws, dbs, (stat3, stat1, zstat, cstat) = _local_step(
        x[0], p[0, 0], positions, loss_target[0], w_in_full, w_o_full.reshape(D_MODEL, D_MODEL), start_dep, late_weights,
        early_grads, ln_z_g, ln_z_b, w_s, b_s, ln1_g, ln1_b, conv_b, ln2_g, ln2_b, b_ple_gate, ln3_g, ln3_b)

    pair, landed = _exchange_wait(flight["send"], flight["recv"], flight["pair"], flight["lands"], grad_x)
    blocks = {k: _chip_sum(f"rs_sum_{k}", own, l, place) for k, own, l in zip(flight["names"], pair, landed)}
    got_in, = _sibling_swap([g_w_in])
    pair_in = _pair_sum_bf16("rs_pair_w_in", g_w_in, got_in, place[1:2])
    landed_in, = _chip_exchange([pair_in])
    blocks["w_in"] = _chip_sum("rs_sum_w_in", pair_in, landed_in, place)
    red = dict(zip(BIG, _sibling_join([blocks[k] for k in BIG])))

    out = {}
    for k in BIG:
        d, nm, nv = _adamw_big(f"adamw_{k}", w[k], red[k], m[k], v[k])
        out[k] = (red[k].reshape(w[k].shape), d, nm, nv)

    red3, red1, redz, redc, red_ws, red_bs = _allreduce_small(
        [stat3, stat1, zstat, cstat.reshape(N_SHARD * STAT_ROWS, FF_BLK), dws.reshape(N_HEADS * BLK, BLK), dbs])
    loss = (0.5 / D_MODEL) * jnp.sum(red3[5])
    g_conv_w = lax.dynamic_slice_in_dim(redc, chip * STAT_ROWS, STAT_ROWS, 0)
    names_small = [k for k in WEIGHTS if k not in BIG]
    out.update(_adamw_small(red3, red1, redz, g_conv_w, redc, red_ws, red_bs, {k: (w[k], m[k], v[k]) for k in names_small}))

    return (loss, grad_x[None], *[out[k][0] for k in WEIGHTS], *[out[k][1] for k in WEIGHTS],
            *[out[k][2] for k in WEIGHTS], *[out[k][3] for k in WEIGHTS])
```

```python
import functools
import math

import numpy as np
import jax
import jax.numpy as jnp
from jax import lax
from jax.experimental import pallas as pl
from jax.experimental.pallas import tpu as pltpu

F32 = jnp.float32
BF16 = jnp.bfloat16
MXU = BF16

D_MODEL = 1024
HEAD_DIM = 64
N_HEADS = 8
D_ATTN = 512
D_GMLP = 512
D_IN = 2560
DILATIONS = (1, 4, 16)
BLK = 128
ROPE_THETA = 500000.0
ROPE_DIM = 16
D_FF = 2816
D_PLE = 256
LN_EPS = 1e-5
ALPHA = 2.0 ** 0.25
NEG_INF = -1e30
N_SHARD = 4
W_IN_BLK = D_IN // N_SHARD
FF_BLK = D_FF // N_SHARD
ROW_BLK = D_MODEL // N_SHARD
ADAM_LR, ADAM_B1, ADAM_B2, ADAM_EPS, ADAM_WD, ADAM_STEP = 0.001, 0.9, 0.999, 1e-08, 0.01, 10

TM = 512
HALO = 8
VMEM_LIMIT = 56 * 1024 * 1024


def _cp(**kw):
    return pltpu.CompilerParams(vmem_limit_bytes=VMEM_LIMIT, **kw)


def _full(shape):
    n = len(shape)
    return pl.BlockSpec(shape, lambda *_: (0,) * n)


def _gelu(x):
    return 0.5 * x * (1.0 + lax.erf(x * (1.0 / math.sqrt(2.0))))


def _gelu_grad(x):
    return 0.5 * (1.0 + lax.erf(x * (1.0 / math.sqrt(2.0)))) + x * jnp.exp(-0.5 * x * x) * (1.0 / math.sqrt(2.0 * math.pi))


def _ln_fwd(r):
    mu = jnp.mean(r, axis=-1, keepdims=True)
    xc = r - mu
    var = jnp.mean(xc * xc, axis=-1, keepdims=True)
    rstd = lax.rsqrt(var + LN_EPS)
    return xc * rstd, rstd


def _ln_bwd(dy, xhat, rstd, g):
    dxh = dy * g
    m1 = jnp.mean(dxh, axis=-1, keepdims=True)
    m2 = jnp.mean(dxh * xhat, axis=-1, keepdims=True)
    return rstd * (dxh - m1 - xhat * m2)


def _dot(a, b):
    return jnp.dot(a.astype(MXU), b.astype(MXU), preferred_element_type=F32)


def _dot_nt(a, b):
    return lax.dot_general(a.astype(MXU), b.astype(MXU), (((1,), (1,)), ((), ())), preferred_element_type=F32)


def _dot_tn(a, b):
    return lax.dot_general(a.astype(MXU), b.astype(MXU), (((0,), (0,)), ((), ())), preferred_element_type=F32)


def _colsum(v):
    return jnp.sum(v, axis=0, keepdims=True)


def _rope_tables(positions, t):
    inv = np.float32(ROPE_THETA) ** (-np.arange(0, ROPE_DIM, 2, dtype=np.float32) / np.float32(ROPE_DIM))
    half = ROPE_DIM // 2
    pos_rep = jnp.repeat(positions.reshape(t // 16, 16), half, axis=1)
    inv_row = jnp.asarray(np.tile(inv, 16)[None, :], F32)

    def trig_body(pos_ref, inv_ref, cos_ref, sin_ref):
        ang = pos_ref[...].astype(F32) * inv_ref[...]
        cos_ref[...] = jnp.cos(ang)
        sin_ref[...] = jnp.sin(ang)

    cos8, sin8 = pl.pallas_call(
        trig_body, name="rope_trig",
        out_shape=(jax.ShapeDtypeStruct((t // 16, 128), F32), jax.ShapeDtypeStruct((t // 16, 128), F32)),
    )(pos_rep, inv_row)
    cos8 = cos8.reshape(t, half)
    sin8 = sin8.reshape(t, half)

    lane = np.arange(128) % HEAD_DIM
    sel = (np.arange(half)[:, None] == (lane % half)[None, :])
    e_cos = (sel & (lane < ROPE_DIM)[None, :]).astype(np.float32)
    e_s1 = -(sel & (lane < half)[None, :]).astype(np.float32)
    e_s2 = (sel & ((lane >= half) & (lane < ROPE_DIM))[None, :]).astype(np.float32)
    ones = (lane >= ROPE_DIM).astype(np.float32)[None, :]

    def expand_body(cos_ref, sin_ref, ec_ref, e1_ref, e2_ref, ones_ref, c_ref, s1_ref, s2_ref):
        hp = lax.Precision.HIGHEST
        c_ref[...] = jnp.dot(cos_ref[...], ec_ref[...], precision=hp, preferred_element_type=F32) + ones_ref[...]
        s1_ref[...] = jnp.dot(sin_ref[...], e1_ref[...], precision=hp, preferred_element_type=F32)
        s2_ref[...] = jnp.dot(sin_ref[...], e2_ref[...], precision=hp, preferred_element_type=F32)

    tab = jax.ShapeDtypeStruct((t, 128), F32)
    return pl.pallas_call(expand_body, name="rope_expand", out_shape=(tab, tab, tab), compiler_params=_cp())(
        cos8, sin8, jnp.asarray(e_cos), jnp.asarray(e_s1), jnp.asarray(e_s2), jnp.asarray(ones))


def _tile_heads(tab):
    return jnp.concatenate([tab] * (D_ATTN // 128), axis=1)


def _rope_apply(v, c, s1, s2):
    n = v.shape[1]
    half = ROPE_DIM // 2
    return v * c + pltpu.roll(v, n - half, 1) * s1 + pltpu.roll(v, half, 1) * s2


def _rope_apply_t(g, c, s1, s2):
    n = g.shape[1]
    half = ROPE_DIM // 2
    return g * c + pltpu.roll(g * s1, half, 1) + pltpu.roll(g * s2, n - half, 1)


LANE_CHUNKS = D_ATTN // 128
HEAD_LANES = 128 // N_HEADS


def _perm_shape(t, d, w, dtype):
    return jax.ShapeDtypeStruct((d, t // d, w), dtype)


def _perm_tile(d, w):
    return pl.BlockSpec((None if d == 1 else d, TM // d, w), lambda i: (0, i, 0))


def _to_planes(ref, scr, d, n_chunks, dtype):
    for r in range(d):
        for cc in range(n_chunks):
            ref[r, :, cc * 128:(cc + 1) * 128] = scr.at[cc][pl.ds(r, TM // d, stride=d), :].astype(dtype)


def _from_planes(ref, scr, d, n_chunks, accumulate=False):
    for r in range(d):
        for cc in range(n_chunks):
            rows = scr.at[cc]
            val = ref[r, :, cc * 128:(cc + 1) * 128].astype(F32)
            if accumulate:
                rows[pl.ds(r, TM // d, stride=d), :] += val
            else:
                rows[pl.ds(r, TM // d, stride=d), :] = val


def _chunks(val):
    return [val[:, cc * 128:(cc + 1) * 128] for cc in range(val.shape[1] // 128)]


def _unchunk(scr, n_chunks, base=0):
    return jnp.concatenate([scr[base + cc] for cc in range(n_chunks)], axis=1)


def _head_expand():
    src = np.arange(128)[:, None]
    dst = np.arange(D_ATTN)[None, :]
    return jnp.asarray((src == (dst // HEAD_DIM) * HEAD_LANES).astype(np.float32))


def _head_reduce():
    src = np.arange(D_ATTN)[:, None]
    dst = np.arange(128)[None, :]
    return jnp.asarray((src // HEAD_DIM == dst // HEAD_LANES).astype(np.float32))


def _dot_exact(a, b):
    return jnp.dot(a, b, precision=lax.Precision.HIGHEST, preferred_element_type=F32)


def _qkvuz(x, w_in, c_tab, s1_tab, s2_tab, ln_z_g, ln_z_b, w_s, b_full, dep):
    t = x.shape[0]
    nchunk = TM // BLK

    def body(x_ref, w_ref, c_ref, s1_ref, s2_ref, g_ref, b_ref, ws_ref, bf_ref, dep_ref,
             qkv1_ref, qkv4_ref, qkv16_ref, hu_ref, hz_ref, mixed_ref, gm_ref, h_scr, wm_scr, p_scr):
        @pl.when(pl.program_id(0) == 0)
        def _():
            row = lax.broadcasted_iota(jnp.int32, (BLK, BLK), 0)
            col = lax.broadcasted_iota(jnp.int32, (BLK, BLK), 1)
            for g in range(N_HEADS):
                wm_scr[g] = jnp.where(col <= row, ws_ref[g], 0.0).astype(MXU)

        xb = x_ref[...].astype(MXU)
        for j in range(N_SHARD):
            h_scr[:, j * W_IN_BLK:(j + 1) * W_IN_BLK] = jnp.dot(xb, w_ref[j], preferred_element_type=F32)
        c, s1, s2 = _tile_heads(c_ref[...]), _tile_heads(s1_ref[...]), _tile_heads(s2_ref[...])
        q = _rope_apply(h_scr[:, 0:D_ATTN], c, s1, s2) * (1.0 / math.sqrt(HEAD_DIM))
        k = _rope_apply(h_scr[:, D_ATTN:2 * D_ATTN], c, s1, s2)
        for part, val in enumerate((q, k, h_scr[:, 2 * D_ATTN:3 * D_ATTN])):
            qkv1_ref[:, part * D_ATTN:(part + 1) * D_ATTN] = val.astype(MXU)
            for cc in range(LANE_CHUNKS):
                p_scr[part * LANE_CHUNKS + cc] = val[:, cc * 128:(cc + 1) * 128]
        _to_planes(qkv4_ref, p_scr, DILATIONS[1], 3 * LANE_CHUNKS, MXU)
        _to_planes(qkv16_ref, p_scr, DILATIONS[2], 3 * LANE_CHUNKS, MXU)
        hu = h_scr[:, 3 * D_ATTN:3 * D_ATTN + D_GMLP]
        hz = h_scr[:, 3 * D_ATTN + D_GMLP:]
        hu_ref[...] = hu
        hz_ref[...] = hz
        zhat, _ = _ln_fwd(_gelu(hz))
        zn = (zhat * g_ref[...] + b_ref[...]).astype(MXU)
        for ch in range(nchunk):
            rows = slice(ch * BLK, (ch + 1) * BLK)
            for g in range(N_HEADS):
                cols = slice(g * HEAD_DIM, (g + 1) * HEAD_DIM)
                mixed_ref[rows, cols] = jnp.dot(wm_scr[g], zn[rows, cols], preferred_element_type=F32) + bf_ref[:, cols]
        gm_ref[...] = (_gelu(hu) * mixed_ref[...]).astype(MXU)

    tok = lambda w: pl.BlockSpec((TM, w), lambda i: (i, 0))
    outs = [_perm_shape(t, d, 3 * D_ATTN, MXU) for d in DILATIONS] + [jax.ShapeDtypeStruct((t, D_GMLP), F32)] * 3 + [
        jax.ShapeDtypeStruct((t, D_GMLP), MXU)]
    return pl.pallas_call(
        body, name="qkvuz", grid=(t // TM,),
        in_specs=[tok(D_MODEL), _full(w_in.shape), tok(128), tok(128), tok(128), _full(ln_z_g.shape), _full(ln_z_b.shape),
                  _full(w_s.shape), _full(b_full.shape), pl.BlockSpec(memory_space=pl.ANY)],
        out_specs=[_perm_tile(d, 3 * D_ATTN) for d in DILATIONS] + [tok(D_ATTN)] * 4, out_shape=outs,
        scratch_shapes=[pltpu.VMEM((TM, D_IN), F32), pltpu.VMEM((N_HEADS, BLK, BLK), MXU),
                        pltpu.VMEM((3 * LANE_CHUNKS, TM, 128), F32)],
        compiler_params=_cp(dimension_semantics=("arbitrary",)),
    )(x, w_in, c_tab, s1_tab, s2_tab, ln_z_g, ln_z_b, w_s, b_full, dep)


def _band_valid(n):
    i = lax.broadcasted_iota(jnp.int32, (BLK, 2 * BLK), 0)
    j = lax.broadcasted_iota(jnp.int32, (BLK, 2 * BLK), 1)
    return (j >= i) & (j <= i + BLK) & ((j >= BLK) | (n > 0))


def _attn_fwd(qkv, d):
    _, l_sub, _ = qkv.shape
    nb = l_sub // BLK

    def body(q_ref, kp_ref, kc_ref, vp_ref, vc_ref, o_ref, l_ref):
        valid = _band_valid(pl.program_id(1))
        kcat = jnp.concatenate([kp_ref[...], kc_ref[...]], axis=0)
        vcat = jnp.concatenate([vp_ref[...], vc_ref[...]], axis=0)
        for h in range(N_HEADS):
            cols = slice(h * HEAD_DIM, (h + 1) * HEAD_DIM)
            s = jnp.where(valid, _dot_nt(q_ref[:, cols], kcat[:, cols]), NEG_INF)
            m = jnp.max(s, axis=-1, keepdims=True)
            e = jnp.exp(s - m)
            den = jnp.sum(e, axis=-1, keepdims=True)
            o_ref[:, cols] = _dot(e, vcat[:, cols]) * (1.0 / den)
            l_ref[:, h * HEAD_LANES:(h + 1) * HEAD_LANES] = jnp.broadcast_to(m + jnp.log(den), (BLK, HEAD_LANES))

    def blk(w, col, prev=False):
        return pl.BlockSpec((None, BLK, w), lambda r, n: (r, jnp.maximum(n - 1, 0) if prev else n, col))

    return pl.pallas_call(
        body, name=f"attn_fwd_d{d}", grid=(d, nb),
        in_specs=[blk(D_ATTN, 0), blk(D_ATTN, 1, True), blk(D_ATTN, 1), blk(D_ATTN, 2, True), blk(D_ATTN, 2)],
        out_specs=[blk(D_ATTN, 0), blk(128, 0)],
        out_shape=[jax.ShapeDtypeStruct((d, l_sub, D_ATTN), F32), jax.ShapeDtypeStruct((d, l_sub, 128), F32)],
        compiler_params=_cp(dimension_semantics=("arbitrary", "arbitrary")),
    )(qkv, qkv, qkv, qkv, qkv)


def _attn_bwd(qkv, do, lse, delta, d):
    _, l_sub, _ = qkv.shape
    nb = l_sub // BLK
    whole = l_sub <= 8 * BLK

    def shares(n, q_ref, kp_ref, kc_ref, vp_ref, vc_ref, do_ref, l_ref, dl_ref, dq_ref):
        valid = _band_valid(n)
        kcat = jnp.concatenate([kp_ref[...], kc_ref[...]], axis=0)
        vcat = jnp.concatenate([vp_ref[...], vc_ref[...]], axis=0)
        for h in range(N_HEADS):
            cols = slice(h * HEAD_DIM, (h + 1) * HEAD_DIM)
            stat = slice(h * HEAD_LANES, h * HEAD_LANES + 1)
            qh, doh = q_ref[:, cols], do_ref[:, cols]
            p = jnp.where(valid, jnp.exp(_dot_nt(qh, kcat[:, cols]) - l_ref[:, stat]), 0.0)
            ds = p * (_dot_nt(doh, vcat[:, cols]) - dl_ref[:, stat])
            dq_ref[:, cols] = _dot(ds, kcat[:, cols])
            yield cols, _dot_tn(ds, qh), _dot_tn(p, doh)

    def body_whole(*refs):
        dk_ref, dv_ref = refs[9:]
        n = pl.program_id(1)
        cur = pl.ds(pl.multiple_of(n * BLK, BLK), BLK)
        prev = pl.ds(pl.multiple_of(jnp.maximum(n - 1, 0) * BLK, BLK), BLK)
        for cols, dk2, dv2 in shares(n, *refs[:9]):
            dk_ref[cur, cols] = dk2[BLK:]
            dv_ref[cur, cols] = dv2[BLK:]
            dk_ref[prev, cols] += dk2[0:BLK]
            dv_ref[prev, cols] += dv2[0:BLK]

    def body_carry(*refs):
        dk_ref, dv_ref, ck_scr, cv_scr = refs[9:]
        n = pl.program_id(1)

        @pl.when(n == 0)
        def _():
            ck_scr[...] = jnp.zeros_like(ck_scr)
            cv_scr[...] = jnp.zeros_like(cv_scr)

        @pl.when(n < nb)
        def _():
            for cols, dk2, dv2 in shares(n, *refs[:9]):
                dk_ref[:, cols] = ck_scr[:, cols] + dk2[0:BLK]
                dv_ref[:, cols] = cv_scr[:, cols] + dv2[0:BLK]
                ck_scr[:, cols] = dk2[BLK:]
                cv_scr[:, cols] = dv2[BLK:]

        @pl.when(n == nb)
        def _():
            dk_ref[...] = ck_scr[...]
            dv_ref[...] = cv_scr[...]

    def blk(w, col, shift=0):
        return pl.BlockSpec((None, BLK, w), lambda r, n: (r, jnp.clip(n - shift, 0, nb - 1), col))

    if whole:
        dkv_spec = pl.BlockSpec((None, l_sub, D_ATTN), lambda r, n: (r, 0, 0))
        body, steps, scratch = body_whole, nb, []
    else:
        dkv_spec = blk(D_ATTN, 0, 1)
        body, steps, scratch = body_carry, nb + 1, [pltpu.VMEM((BLK, D_ATTN), F32)] * 2
    return pl.pallas_call(
        body, name=f"attn_bwd_d{d}", grid=(d, steps),
        in_specs=[blk(D_ATTN, 0), blk(D_ATTN, 1, 1), blk(D_ATTN, 1), blk(D_ATTN, 2, 1), blk(D_ATTN, 2),
                  blk(D_ATTN, 0), blk(128, 0), blk(128, 0)],
        out_specs=[blk(D_ATTN, 0), dkv_spec, dkv_spec],
        out_shape=[jax.ShapeDtypeStruct((d, l_sub, D_ATTN), F32)] * 3,
        scratch_shapes=scratch,
        compiler_params=_cp(dimension_semantics=("arbitrary", "arbitrary")),
    )(qkv, qkv, qkv, qkv, qkv, do, lse, delta)


def _mix_ln1(os_, ls_, gm, x, w_o, ln1_g, ln1_b):
    t = x.shape[0]
    expand = _head_expand()

    def body(o1, o4, o16, l1, l4, l16, gm_ref, x_ref, wo_ref, g_ref, b_ref, ex_ref,
             attn_ref, lse1_ref, lse4_ref, lse16_ref, cat_ref, xhat_ref, rstd_ref, x1b_ref, o_scr, l_scr):
        _from_planes(o4, o_scr, DILATIONS[1], LANE_CHUNKS)
        _from_planes(o16, o_scr.at[pl.ds(LANE_CHUNKS, LANE_CHUNKS)], DILATIONS[2], LANE_CHUNKS)
        _from_planes(l4, l_scr, DILATIONS[1], 1)
        _from_planes(l16, l_scr.at[pl.ds(1, 1)], DILATIONS[2], 1)
        la, lb, lc = l1[...], l_scr[0], l_scr[1]
        m = jnp.maximum(jnp.maximum(la, lb), lc)
        ea, eb, ec = jnp.exp(la - m), jnp.exp(lb - m), jnp.exp(lc - m)
        den = ea + eb + ec
        inv = 1.0 / den
        wide = lambda w: _dot_exact(w, ex_ref[...])
        attn = (wide(ea * inv) * o1[...] + wide(eb * inv) * _unchunk(o_scr, LANE_CHUNKS)
                + wide(ec * inv) * _unchunk(o_scr, LANE_CHUNKS, LANE_CHUNKS))
        attn_ref[...] = attn
        lse = m + jnp.log(den)
        lse1_ref[...] = lse
        l_scr[2] = lse
        _to_planes(lse4_ref, l_scr.at[pl.ds(2, 1)], DILATIONS[1], 1, F32)
        _to_planes(lse16_ref, l_scr.at[pl.ds(2, 1)], DILATIONS[2], 1, F32)
        cat_ref[:, 0:D_ATTN] = attn.astype(MXU)
        cat_ref[:, D_ATTN:] = gm_ref[...]
        mix = jnp.dot(cat_ref[...], wo_ref[...], preferred_element_type=F32)
        xhat, rstd = _ln_fwd(ALPHA * x_ref[...] + mix)
        xhat_ref[...] = xhat
        rstd_ref[...] = rstd
        x1b_ref[...] = (xhat * g_ref[...] + b_ref[...]).astype(MXU)

    tok = lambda w: pl.BlockSpec((TM, w), lambda i: (i, 0))
    outs = [jax.ShapeDtypeStruct((t, D_ATTN), F32)] + [_perm_shape(t, d, 128, F32) for d in DILATIONS] + [
        jax.ShapeDtypeStruct((t, D_MODEL), MXU), jax.ShapeDtypeStruct((t, D_MODEL), F32), jax.ShapeDtypeStruct((t, 1), F32),
        jax.ShapeDtypeStruct((t, D_MODEL), MXU)]
    return pl.pallas_call(
        body, name="mix_ln1", grid=(t // TM,),
        in_specs=[_perm_tile(d, D_ATTN) for d in DILATIONS] + [_perm_tile(d, 128) for d in DILATIONS]
        + [tok(D_GMLP), tok(D_MODEL), _full(w_o.shape), _full(ln1_g.shape), _full(ln1_b.shape), _full(expand.shape)],
        out_specs=[tok(D_ATTN)] + [_perm_tile(d, 128) for d in DILATIONS] + [tok(D_MODEL), tok(D_MODEL), tok(1), tok(D_MODEL)],
        out_shape=outs,
        scratch_shapes=[pltpu.VMEM((2 * LANE_CHUNKS, TM, 128), F32), pltpu.VMEM((3, TM, 128), F32)],
        compiler_params=_cp(dimension_semantics=("arbitrary",)),
    )(*os_, *ls_, gm, x, w_o, ln1_g, ln1_b, expand)


def _conv_fwd(a_ext, w_ref, b_ref, rows):
    return (b_ref[...] + w_ref[2:3, :] * a_ext[HALO:HALO + rows] + w_ref[1:2, :] * a_ext[HALO - 1:HALO - 1 + rows]
            + w_ref[0:1, :] * a_ext[HALO - 2:HALO - 2 + rows])


def _ffn_in(x1b, w_a, w_b, conv_w, conv_b):
    t = x1b.shape[0]
    hb = TM // HALO

    def body(x_ref, xh_ref, wa_ref, wb_ref, cw_ref, cb_ref, apre_ref, b_ref, f_ref):
        i = pl.program_id(1)
        a_pre = jnp.dot(x_ref[...], wa_ref[...], preferred_element_type=F32)
        a_halo = jnp.dot(xh_ref[...], wa_ref[...], preferred_element_type=F32)
        a_halo = jnp.where(i > 0, a_halo, 0.0)
        a = _conv_fwd(jnp.concatenate([a_halo, a_pre], axis=0), cw_ref, cb_ref, TM)
        b = jnp.dot(x_ref[...], wb_ref[...], preferred_element_type=F32)
        apre_ref[...] = a_pre
        b_ref[...] = b
        f_ref[...] = (_gelu(a) * b).astype(MXU)

    blk = lambda r, c: pl.BlockSpec((None, r, c), lambda j, i: (j, 0, 0))
    tokj = pl.BlockSpec((None, TM, FF_BLK), lambda j, i: (j, i, 0))
    outs = [jax.ShapeDtypeStruct((N_SHARD, t, FF_BLK), F32)] * 2 + [jax.ShapeDtypeStruct((N_SHARD, t, FF_BLK), MXU)]
    return pl.pallas_call(
        body, name="ffn_in", grid=(N_SHARD, t // TM),
        in_specs=[pl.BlockSpec((TM, D_MODEL), lambda j, i: (i, 0)),
                  pl.BlockSpec((HALO, D_MODEL), lambda j, i: (jnp.maximum(i * hb - 1, 0), 0)),
                  blk(D_MODEL, FF_BLK), blk(D_MODEL, FF_BLK), blk(3, FF_BLK), blk(1, FF_BLK)],
        out_specs=[tokj, tokj, tokj], out_shape=outs,
        compiler_params=_cp(dimension_semantics=("arbitrary", "arbitrary")),
    )(x1b, x1b, w_a, w_b, conv_w, conv_b)


def _ffn_out_ln2(f, w_down, xhat1, ln1_g, ln1_b, ln2_g, ln2_b):
    t = xhat1.shape[0]

    def body(f_ref, wd_ref, xh_ref, g1_ref, b1_ref, g2_ref, b2_ref, xhat_ref, rstd_ref, x2b_ref):
        ff = jnp.dot(f_ref[0], wd_ref[0], preferred_element_type=F32)
        for j in range(1, N_SHARD):
            ff = ff + jnp.dot(f_ref[j], wd_ref[j], preferred_element_type=F32)
        x1 = xh_ref[...] * g1_ref[...] + b1_ref[...]
        xhat, rstd = _ln_fwd(ALPHA * x1 + ff)
        xhat_ref[...] = xhat
        rstd_ref[...] = rstd
        x2b_ref[...] = (xhat * g2_ref[...] + b2_ref[...]).astype(MXU)

    tok = lambda w: pl.BlockSpec((TM, w), lambda i: (i, 0))
    vec = _full((1, D_MODEL))
    outs = [jax.ShapeDtypeStruct((t, D_MODEL), F32), jax.ShapeDtypeStruct((t, 1), F32), jax.ShapeDtypeStruct((t, D_MODEL), MXU)]
    return pl.pallas_call(
        body, name="ffn_out_ln2", grid=(t // TM,),
        in_specs=[pl.BlockSpec((N_SHARD, TM, FF_BLK), lambda i: (0, i, 0)), _full(w_down.shape), tok(D_MODEL), vec, vec, vec, vec],
        out_specs=[tok(D_MODEL), tok(1), tok(D_MODEL)], out_shape=outs,
        compiler_params=_cp(dimension_semantics=("arbitrary",)),
    )(f, w_down, xhat1, ln1_g, ln1_b, ln2_g, ln2_b)


STAT_ROWS = 8


def _ple_loss_bwd(xhat2, rstd2, p, target, ln2_g, ln2_b, w_g, b_g, w_p, ln3_g, ln3_b):
    t = xhat2.shape[0]

    def body(xh2_ref, rs2_ref, p_ref, t_ref, g2_ref, b2_ref, wg_ref, bg_ref, wp_ref, g3_ref, b3_ref,
             dr2_ref, dgp_ref, dpp_ref, stat_ref, pp_scr):
        @pl.when(pl.program_id(0) == 0)
        def _():
            stat_ref[...] = jnp.zeros_like(stat_ref)

        xhat2 = xh2_ref[...]
        x2 = xhat2 * g2_ref[...] + b2_ref[...]
        gate = jax.nn.sigmoid(jnp.dot(x2.astype(MXU), wg_ref[...], preferred_element_type=F32) + bg_ref[...])
        pb = p_ref[...].astype(MXU)
        for j in range(N_SHARD):
            pp_scr[:, j * ROW_BLK:(j + 1) * ROW_BLK] = jnp.dot(pb, wp_ref[j], preferred_element_type=F32)
        pp = pp_scr[...]
        xhat3, rstd3 = _ln_fwd(ALPHA * x2 + gate * pp)
        err = xhat3 * g3_ref[...] + b3_ref[...] - t_ref[...]
        dy = err * (1.0 / D_MODEL)
        dr3 = _ln_bwd(dy, xhat3, rstd3, g3_ref[...])
        dgp = dr3 * pp * gate * (1.0 - gate)
        dgp_ref[...] = dgp.astype(MXU)
        dpp_ref[...] = (dr3 * gate).astype(MXU)
        dx2 = ALPHA * dr3 + _dot_nt(dgp, wg_ref[...])
        dr2_ref[...] = _ln_bwd(dx2, xhat2, rs2_ref[...], g2_ref[...])
        stat_ref[0:1, :] += _colsum(dy * xhat3)
        stat_ref[1:2, :] += _colsum(dy)
        stat_ref[2:3, :] += _colsum(dgp)
        stat_ref[3:4, :] += _colsum(dx2 * xhat2)
        stat_ref[4:5, :] += _colsum(dx2)
        stat_ref[5:6, :] += _colsum(err * err)

    tok = lambda w: pl.BlockSpec((TM, w), lambda i: (i, 0))
    vec = _full((1, D_MODEL))
    outs = [jax.ShapeDtypeStruct((t, D_MODEL), F32), jax.ShapeDtypeStruct((t, D_MODEL), MXU), jax.ShapeDtypeStruct((t, D_MODEL), MXU),
            jax.ShapeDtypeStruct((STAT_ROWS, D_MODEL), F32)]
    return pl.pallas_call(
        body, name="ple_loss_bwd", grid=(t // TM,),
        in_specs=[tok(D_MODEL), tok(1), tok(D_PLE), tok(D_MODEL), vec, vec, _full(w_g.shape), vec, _full(w_p.shape), vec, vec],
        out_specs=[tok(D_MODEL), tok(D_MODEL), tok(D_MODEL), _full((STAT_ROWS, D_MODEL))], out_shape=outs,
        scratch_shapes=[pltpu.VMEM((TM, D_MODEL), F32)],
        compiler_params=_cp(dimension_semantics=("arbitrary",)),
    )(xhat2, rstd2, p, target, ln2_g, ln2_b, w_g, b_g, w_p, ln3_g, ln3_b)


def _ffn_bwd(dr2, a_pre, b, w_down, w_a, w_b, conv_w, conv_b, xhat1, rstd1, ln1_g):
    t = dr2.shape[0]
    nt = t // TM
    hb = TM // HALO
    last_h = t // HALO - 1

    def body(dr_ref, drn_ref, ap_ref, app_ref, apn_ref, b_ref, bn_ref, wd_ref, wa_ref, wb_ref, cw_ref, cb_ref,
             xh_ref, rs_ref, g1_ref, dap_ref, dbb_ref, dr1_ref, cstat_ref, lstat_ref, acc_scr):
        i, j = pl.program_id(0), pl.program_id(1)

        @pl.when((i == 0) & (j == 0))
        def _():
            cstat_ref[...] = jnp.zeros_like(cstat_ref)
            lstat_ref[...] = jnp.zeros_like(lstat_ref)

        ext = TM + HALO
        dr_ext = jnp.concatenate([dr_ref[...], drn_ref[...]], axis=0)
        df = _dot_nt(dr_ext, wd_ref[...])
        a_all = jnp.concatenate([jnp.where(i > 0, app_ref[...], 0.0), ap_ref[...], apn_ref[...]], axis=0)
        a = _conv_fwd(a_all, cw_ref, cb_ref, ext)
        b_ext = jnp.concatenate([b_ref[...], bn_ref[...]], axis=0)
        row = lax.broadcasted_iota(jnp.int32, (ext, 1), 0)
        da = jnp.where((row < TM) | (i < nt - 1), df * b_ext * _gelu_grad(a), 0.0)
        dbb = df[0:TM] * _gelu(a[0:TM])
        da_pre = cw_ref[2:3, :] * da[0:TM] + cw_ref[1:2, :] * da[1:TM + 1] + cw_ref[0:1, :] * da[2:TM + 2]
        dap_ref[...] = da_pre.astype(MXU)
        dbb_ref[...] = dbb.astype(MXU)
        da_m = da[0:TM]
        for kk in range(3):
            cstat_ref[j, kk:kk + 1, :] += _colsum(da_m * a_all[HALO - 2 + kk:HALO - 2 + kk + TM])
        cstat_ref[j, 3:4, :] += _colsum(da_m)
        part = _dot_nt(da_pre, wa_ref[...]) + _dot_nt(dbb, wb_ref[...])

        @pl.when(j == 0)
        def _():
            acc_scr[...] = ALPHA * dr_ref[...] + part

        @pl.when(j > 0)
        def _():
            acc_scr[...] += part

        @pl.when(j == N_SHARD - 1)
        def _():
            dx1 = acc_scr[...]
            xhat1 = xh_ref[...]
            lstat_ref[0:1, :] += _colsum(dx1 * xhat1)
            lstat_ref[1:2, :] += _colsum(dx1)
            dr1_ref[...] = _ln_bwd(dx1, xhat1, rs_ref[...], g1_ref[...])

    tok = lambda w: pl.BlockSpec((TM, w), lambda i, j: (i, 0))
    tokj = pl.BlockSpec((None, TM, FF_BLK), lambda i, j: (j, i, 0))
    prevj = pl.BlockSpec((None, HALO, FF_BLK), lambda i, j: (j, jnp.maximum(i * hb - 1, 0), 0))
    nextj = pl.BlockSpec((None, HALO, FF_BLK), lambda i, j: (j, jnp.minimum((i + 1) * hb, last_h), 0))
    blk = lambda r, c: pl.BlockSpec((None, r, c), lambda i, j: (j, 0, 0))
    outs = [jax.ShapeDtypeStruct((N_SHARD, t, FF_BLK), MXU)] * 2 + [
        jax.ShapeDtypeStruct((t, D_MODEL), F32), jax.ShapeDtypeStruct((N_SHARD, STAT_ROWS, FF_BLK), F32),
        jax.ShapeDtypeStruct((STAT_ROWS, D_MODEL), F32)]
    return pl.pallas_call(
        body, name="ffn_bwd", grid=(nt, N_SHARD),
        in_specs=[tok(D_MODEL), pl.BlockSpec((HALO, D_MODEL), lambda i, j: (jnp.minimum((i + 1) * hb, last_h), 0)),
                  tokj, prevj, nextj, tokj, nextj, blk(FF_BLK, D_MODEL), blk(D_MODEL, FF_BLK), blk(D_MODEL, FF_BLK),
                  blk(3, FF_BLK), blk(1, FF_BLK), tok(D_MODEL), tok(1), _full((1, D_MODEL))],
        out_specs=[tokj, tokj, tok(D_MODEL), _full((N_SHARD, STAT_ROWS, FF_BLK)), _full((STAT_ROWS, D_MODEL))], out_shape=outs,
        scratch_shapes=[pltpu.VMEM((TM, D_MODEL), F32)],
        compiler_params=_cp(dimension_semantics=("arbitrary", "arbitrary")),
    )(dr2, dr2, a_pre, a_pre, a_pre, b, b, w_down, w_a, w_b, conv_w, conv_b, xhat1, rstd1, ln1_g)


def _mix_bwd(dr1, w_o, hu, hz, mixed, attn, ln_z_g, ln_z_b, w_s, dep):
    t = dr1.shape[0]
    nchunk = TM // BLK

    def body(dr_ref, wo_ref, hu_ref, hz_ref, mx_ref, attn_ref, g_ref, b_ref, ws_ref, grp_ref, red_ref, dep_ref,
             do1_ref, do4_ref, do16_ref, dl1_ref, dl4_ref, dl16_ref, duz_ref, dws_ref, dbs_ref, zstat_ref,
             wm_scr, dzn_scr, dbsum_scr, do_scr, dl_scr):
        @pl.when(pl.program_id(0) == 0)
        def _():
            row = lax.broadcasted_iota(jnp.int32, (BLK, BLK), 0)
            col = lax.broadcasted_iota(jnp.int32, (BLK, BLK), 1)
            for g in range(N_HEADS):
                wm_scr[g] = jnp.where(col <= row, ws_ref[g], 0.0).astype(MXU)
            dws_ref[...] = jnp.zeros_like(dws_ref)
            dbsum_scr[...] = jnp.zeros_like(dbsum_scr)
            zstat_ref[...] = jnp.zeros_like(zstat_ref)

        dcat = _dot_nt(dr_ref[...], wo_ref[...])
        dattn = dcat[:, 0:D_ATTN]
        do1_ref[...] = dattn.astype(MXU)
        for cc, val in enumerate(_chunks(dattn)):
            do_scr[cc] = val
        _to_planes(do4_ref, do_scr, DILATIONS[1], LANE_CHUNKS, MXU)
        _to_planes(do16_ref, do_scr, DILATIONS[2], LANE_CHUNKS, MXU)
        delta = _dot_exact(dattn * attn_ref[...], red_ref[...])
        dl1_ref[...] = delta
        dl_scr[0] = delta
        _to_planes(dl4_ref, dl_scr, DILATIONS[1], 1, F32)
        _to_planes(dl16_ref, dl_scr, DILATIONS[2], 1, F32)
        dgm = dcat[:, D_ATTN:]
        hu, hz = hu_ref[...], hz_ref[...]
        u = _gelu(hu)
        duz_ref[:, 0:D_GMLP] = (dgm * mx_ref[...] * _gelu_grad(hu)).astype(MXU)
        dmixed = dgm * u
        dmb = dmixed.astype(MXU)
        zhat, rstd = _ln_fwd(_gelu(hz))
        znb = (zhat * g_ref[...] + b_ref[...]).astype(MXU)
        dbs_acc = jnp.zeros((BLK, D_GMLP), F32)
        for ch in range(nchunk):
            rows = slice(ch * BLK, (ch + 1) * BLK)
            dbs_acc = dbs_acc + dmixed[rows]
            for g in range(N_HEADS):
                cols = slice(g * HEAD_DIM, (g + 1) * HEAD_DIM)
                dzn_scr[rows, cols] = _dot_tn(wm_scr[g], dmb[rows, cols])
                dws_ref[g] += _dot_nt(dmb[rows, cols], znb[rows, cols])
        dbsum_scr[...] += dbs_acc
        dzn = dzn_scr[...]
        zstat_ref[0:1, :] += _colsum(dzn * zhat)
        zstat_ref[1:2, :] += _colsum(dzn)
        duz_ref[:, D_GMLP:] = (_ln_bwd(dzn, zhat, rstd, g_ref[...]) * _gelu_grad(hz)).astype(MXU)

        @pl.when(pl.program_id(0) == nt - 1)
        def _():
            row = lax.broadcasted_iota(jnp.int32, (BLK, BLK), 0)
            col = lax.broadcasted_iota(jnp.int32, (BLK, BLK), 1)
            for g in range(N_HEADS):
                dws_ref[g] = jnp.where(col <= row, dws_ref[g], 0.0)
            dbs_ref[...] = lax.dot_general(grp_ref[...], dbsum_scr[...], (((1,), (1,)), ((), ())),
                                           precision=lax.Precision.HIGHEST, preferred_element_type=F32)

    nt = t // TM
    tok = lambda w: pl.BlockSpec((TM, w), lambda i: (i, 0))
    grp = jnp.asarray((np.arange(D_GMLP)[None, :] // HEAD_DIM == np.arange(N_HEADS)[:, None]).astype(np.float32))
    red = _head_reduce()
    outs = [_perm_shape(t, d, D_ATTN, MXU) for d in DILATIONS] + [_perm_shape(t, d, 128, F32) for d in DILATIONS] + [
        jax.ShapeDtypeStruct((t, 2 * D_GMLP), MXU),
        jax.ShapeDtypeStruct((N_HEADS, BLK, BLK), F32), jax.ShapeDtypeStruct((N_HEADS, BLK), F32),
        jax.ShapeDtypeStruct((STAT_ROWS, D_GMLP), F32)]
    return pl.pallas_call(
        body, name="mix_bwd", grid=(t // TM,),
        in_specs=[tok(D_MODEL), _full(w_o.shape), tok(D_GMLP), tok(D_GMLP), tok(D_GMLP), tok(D_ATTN), _full(ln_z_g.shape),
                  _full(ln_z_b.shape), _full(w_s.shape), _full(grp.shape), _full(red.shape), pl.BlockSpec(memory_space=pl.ANY)],
        out_specs=[_perm_tile(d, D_ATTN) for d in DILATIONS] + [_perm_tile(d, 128) for d in DILATIONS]
        + [tok(2 * D_GMLP), _full((N_HEADS, BLK, BLK)), _full((N_HEADS, BLK)), _full((STAT_ROWS, D_GMLP))],
        out_shape=outs,
        scratch_shapes=[pltpu.VMEM((N_HEADS, BLK, BLK), MXU), pltpu.VMEM((TM, D_GMLP), F32), pltpu.VMEM((BLK, D_GMLP), F32),
                        pltpu.VMEM((LANE_CHUNKS, TM, 128), F32), pltpu.VMEM((1, TM, 128), F32)],
        compiler_params=_cp(dimension_semantics=("arbitrary",)),
    )(dr1, w_o, hu, hz, mixed, attn, ln_z_g, ln_z_b, w_s, grp, red, dep)


def _dx_in(dqs, dks, dvs, duz, dr1, w_in, c_tab, s1_tab, s2_tab):
    t = dr1.shape[0]

    def body(dq1, dq4, dq16, dk1, dk4, dk16, dv1, dv4, dv16, duz_ref, dr_ref, w_ref, c_ref, s1_ref, s2_ref,
             dh_ref, dx_ref, acc_scr):
        sums = []
        for part, (g1, g4, g16) in enumerate(((dq1, dq4, dq16), (dk1, dk4, dk16), (dv1, dv4, dv16))):
            acc = acc_scr.at[pl.ds(part * LANE_CHUNKS, LANE_CHUNKS)]
            for cc in range(LANE_CHUNKS):
                acc[cc] = g1[:, cc * 128:(cc + 1) * 128]
            _from_planes(g4, acc, DILATIONS[1], LANE_CHUNKS, accumulate=True)
            _from_planes(g16, acc, DILATIONS[2], LANE_CHUNKS, accumulate=True)
            sums.append(_unchunk(acc_scr, LANE_CHUNKS, part * LANE_CHUNKS))
        c, s1, s2 = _tile_heads(c_ref[...]), _tile_heads(s1_ref[...]), _tile_heads(s2_ref[...])
        dh_ref[:, 0:D_ATTN] = _rope_apply_t(sums[0] * (1.0 / math.sqrt(HEAD_DIM)), c, s1, s2).astype(MXU)
        dh_ref[:, D_ATTN:2 * D_ATTN] = _rope_apply_t(sums[1], c, s1, s2).astype(MXU)
        dh_ref[:, 2 * D_ATTN:3 * D_ATTN] = sums[2].astype(MXU)
        dh_ref[:, 3 * D_ATTN:] = duz_ref[...]
        dx = ALPHA * dr_ref[...]
        for j in range(N_SHARD):
            dx = dx + _dot_nt(dh_ref[:, j * W_IN_BLK:(j + 1) * W_IN_BLK], w_ref[j])
        dx_ref[...] = dx

    tok = lambda w: pl.BlockSpec((TM, w), lambda i: (i, 0))
    outs = [jax.ShapeDtypeStruct((t, D_IN), MXU), jax.ShapeDtypeStruct((t, D_MODEL), F32)]
    return pl.pallas_call(
        body, name="dx_in", grid=(t // TM,),
        in_specs=[_perm_tile(d, D_ATTN) for d in DILATIONS] * 3
        + [tok(2 * D_GMLP), tok(D_MODEL), _full(w_in.shape), tok(128), tok(128), tok(128)],
        out_specs=[tok(D_IN), tok(D_MODEL)], out_shape=outs,
        scratch_shapes=[pltpu.VMEM((3 * LANE_CHUNKS, TM, 128), F32)],
        compiler_params=_cp(dimension_semantics=("arbitrary",)),
    )(*dqs, *dks, *dvs, duz, dr1, w_in, c_tab, s1_tab, s2_tab)


def _wgrad(name, x, dy, x_spec, dy_spec, out_spec, out_shape, grid):
    def body(x_ref, dy_ref, o_ref):
        o_ref[...] = _dot_tn(x_ref[...], dy_ref[...])

    return pl.pallas_call(
        body, name=name, grid=grid, in_specs=[x_spec, dy_spec], out_specs=out_spec,
        out_shape=jax.ShapeDtypeStruct(out_shape, F32),
        compiler_params=_cp(dimension_semantics=("arbitrary",) * len(grid)),
    )(x, dy)


def _local_step(x, p, positions, target, w_in, start_dep, late_weights, early_grads,
                ln_z_g, ln_z_b, w_s, b_s, ln1_g, ln1_b, conv_b, ln2_g, ln2_b, b_g, ln3_g, ln3_b):
    t = x.shape[0]
    half = TM
    c_tab, s1_tab, s2_tab = _rope_tables(positions, t)
    b_full = jnp.repeat(jnp.transpose(b_s[0]), HEAD_DIM, axis=1)
    conv_b4 = conv_b.reshape(N_SHARD, 1, FF_BLK)
    *qkvs, hu, hz, mixed, gm = _qkvuz(x, w_in, c_tab, s1_tab, s2_tab, ln_z_g, ln_z_b, w_s[0], b_full, start_dep)
    branches = [_attn_fwd(qkv, d) for qkv, d in zip(qkvs, DILATIONS)]
    w_o, w_a, w_b, conv_w, w_down, w_g, w_p = late_weights(branches[-1][1])
    attn, *lses, cat, xhat1, rstd1, x1b = _mix_ln1(
        [o for o, _ in branches], [l for _, l in branches], gm, x, w_o, ln1_g, ln1_b)
    a_pre, b_act, f = _ffn_in(x1b, w_a, w_b, conv_w, conv_b4)
    xhat2, rstd2, x2b = _ffn_out_ln2(f, w_down, xhat1, ln1_g, ln1_b, ln2_g, ln2_b)
    dr2, dgp, dpp, stat3 = _ple_loss_bwd(xhat2, rstd2, p, target, ln2_g, ln2_b, w_g, b_g, w_p, ln3_g, ln3_b)
    da_pre, dbb, dr1, cstat, stat1 = _ffn_bwd(dr2, a_pre, b_act, w_down, w_a, w_b, conv_w, conv_b4, xhat1, rstd1, ln1_g)

    full_t = lambda w, im: pl.BlockSpec((t, w), im)
    ffj = pl.BlockSpec((None, t, FF_BLK), lambda j, kk: (j, 0, 0))
    early = dict(
        w_ple_gate=_wgrad("dw_g", x2b, dgp, full_t(half, lambda kk, n: (0, kk)), full_t(half, lambda kk, n: (0, n)),
                          pl.BlockSpec((half, half), lambda kk, n: (kk, n)), (D_MODEL, D_MODEL), (2, 2)),
        w_ple_in=_wgrad("dw_p", p, dpp, full_t(D_PLE, lambda j: (0, 0)), full_t(ROW_BLK, lambda j: (0, j)),
                        pl.BlockSpec((None, D_PLE, ROW_BLK), lambda j: (j, 0, 0)), (N_SHARD, D_PLE, ROW_BLK), (N_SHARD,)),
        w_ff_down=_wgrad("dw_down", f, dr2, ffj, full_t(half, lambda j, n: (0, n)),
                         pl.BlockSpec((None, FF_BLK, half), lambda j, n: (j, 0, n)), (N_SHARD, FF_BLK, D_MODEL), (N_SHARD, 2)),
        w_ff_a=_wgrad("dw_a", x1b, da_pre, full_t(half, lambda j, kk: (0, kk)), ffj,
                      pl.BlockSpec((None, half, FF_BLK), lambda j, kk: (j, kk, 0)), (N_SHARD, D_MODEL, FF_BLK), (N_SHARD, 2)),
        w_ff_b=_wgrad("dw_b", x1b, dbb, full_t(half, lambda j, kk: (0, kk)), ffj,
                      pl.BlockSpec((None, half, FF_BLK), lambda j, kk: (j, kk, 0)), (N_SHARD, D_MODEL, FF_BLK), (N_SHARD, 2)),
        w_o=_wgrad("dw_o", cat, dr1, full_t(half, lambda kk, n: (0, kk)), full_t(half, lambda kk, n: (0, n)),
                   pl.BlockSpec((half, half), lambda kk, n: (kk, n)), (D_MODEL, D_MODEL), (2, 2)))
    dep = early_grads(early)

    do1, do4, do16, dl1, dl4, dl16, duz, dws, dbs, zstat = _mix_bwd(
        dr1, w_o, hu, hz, mixed, attn, ln_z_g, ln_z_b, w_s[0], dep)
    dqkv = [_attn_bwd(qkv, do, lse, dl, d)
            for qkv, do, lse, dl, d in zip(qkvs, (do1, do4, do16), lses, (dl1, dl4, dl16), DILATIONS)]
    dh, grad_x = _dx_in([g[0] for g in dqkv], [g[1] for g in dqkv], [g[2] for g in dqkv], duz, dr1, w_in,
                        c_tab, s1_tab, s2_tab)
    g_w_in = _wgrad("dw_in", x, dh, full_t(half, lambda j, kk: (0, kk)), full_t(W_IN_BLK, lambda j, kk: (0, j)),
                    pl.BlockSpec((None, half, W_IN_BLK), lambda j, kk: (j, kk, 0)), (N_SHARD, D_MODEL, W_IN_BLK), (N_SHARD, 2))
    return grad_x, g_w_in, dws, dbs, (stat3, stat1, zstat, cstat)


def _rows_tile(r, mult, cap=512):
    return max(d for d in range(mult, min(r, cap) + 1, mult) if r % d == 0)


def _grid_spec(grid, in_specs, out_specs):
    return pltpu.PrefetchScalarGridSpec(num_scalar_prefetch=1, grid=grid, in_specs=in_specs, out_specs=out_specs)


def _place_shard(name, w, chip, dtype):
    r, c = w.shape
    tr = r if r % 16 else _rows_tile(r, 16)

    def body(s_ref, w_ref, o_ref):
        o_ref[...] = w_ref[...].astype(dtype)

    return pl.pallas_call(
        body, name=name,
        grid_spec=_grid_spec((r // tr,), [pl.BlockSpec((tr, c), lambda i, s: (i, 0))],
                             pl.BlockSpec((None, tr, c), lambda i, s: (s[0], i, 0))),
        out_shape=jax.ShapeDtypeStruct((N_SHARD, r, c), dtype), compiler_params=_cp())(chip, w)


def _pair_sum_bf16(name, mine, got, core):
    n, h, c = got.shape
    tr = _rows_tile(h, 16)
    nh = h // tr

    def body(s_ref, a_ref, b_ref, o_ref):
        o_ref[...] = (a_ref[...] + b_ref[...]).astype(BF16)

    spec = pl.BlockSpec((None, tr, c), lambda k, i, s: (k, i, 0))
    return pl.pallas_call(
        body, name=name,
        grid_spec=_grid_spec((n, nh), [pl.BlockSpec((None, tr, c), lambda k, i, s: (k, s[0] * nh + i, 0)), spec], spec),
        out_shape=jax.ShapeDtypeStruct((n, h, c), BF16), compiler_params=_cp())(core, mine, got)


def _chip_sum(name, own, landed, place, dep):
    n, h, c = own.shape
    tr = _rows_tile(h, 16)
    nh = h // tr

    def body(s_ref, a_ref, b_ref, c_ref, d_ref, dep_ref, o_ref):
        o_ref[...] = ((a_ref[...].astype(F32) + b_ref[...].astype(F32)) + c_ref[...].astype(F32)) + d_ref[...].astype(F32)

    def slot(d):
        return pl.BlockSpec((None, tr, c), lambda i, s: ((s[0] + d) % n, i, 0))

    return pl.pallas_call(
        body, name=name,
        grid_spec=_grid_spec((nh,), [slot(0), slot(1), slot(2), slot(3), pl.BlockSpec(memory_space=pl.ANY)],
                             pl.BlockSpec((tr, c), lambda i, s: (s[1] * nh + i, 0))),
        out_shape=jax.ShapeDtypeStruct((2 * h, c), F32), compiler_params=_cp())(place, own, landed, landed, landed, dep)


def _adamw_math(w, g, m, v):
    m = ADAM_B1 * m + (1.0 - ADAM_B1) * g
    v = ADAM_B2 * v + (1.0 - ADAM_B2) * (g * g)
    m_hat = m / (1.0 - ADAM_B1 ** ADAM_STEP)
    v_hat = v / (1.0 - ADAM_B2 ** ADAM_STEP)
    delta = -ADAM_LR * (m_hat / (jnp.sqrt(v_hat) + ADAM_EPS) + ADAM_WD * w)
    return delta, m, v


def _adamw_big(name, w, g, m, v):
    _, r, c = w.shape
    tr = _rows_tile(r, 8, cap=256)

    def body(w_ref, g_ref, m_ref, v_ref, d_ref, nm_ref, nv_ref):
        d_ref[...], nm_ref[...], nv_ref[...] = _adamw_math(w_ref[...], g_ref[...], m_ref[...], v_ref[...])

    s3 = pl.BlockSpec((None, tr, c), lambda i: (0, i, 0))
    s2 = pl.BlockSpec((tr, c), lambda i: (i, 0))
    return pl.pallas_call(body, name=name, grid=(r // tr,), in_specs=[s3, s2, s3, s3], out_specs=[s3, s3, s3],
                          out_shape=[jax.ShapeDtypeStruct(w.shape, F32)] * 3, compiler_params=_cp())(w, g, m, v)


MESH = pl.DeviceIdType.MESH
ANY = pl.BlockSpec(memory_space=pl.ANY)


def _place():
    x, y, c = lax.axis_index("x"), lax.axis_index("y"), lax.axis_index("c")
    chips = [(1 - x, y), (x, 1 - y), (1 - x, 1 - y)]
    return x, y, c, 2 * x + y, chips


def _remote(src, dst, send_sem, recv_sem, dev):
    return pltpu.make_async_remote_copy(src_ref=src, dst_ref=dst, send_sem=send_sem, recv_sem=recv_sem,
                                        device_id=dev, device_id_type=MESH)


def _half(ref, hc, rows):
    return ref.at[pl.ds(hc * (rows // 2), rows // 2)]


def _allgather(stacks, split):
    n = len(stacks)

    def body(*refs):
        outs = refs[n:2 * n]
        send, recv, fsend, frecv = refs[2 * n:]
        x, y, c, j, chips = _place()
        rows = [s.shape[1] for s in stacks]

        def piece(a, slot, hc):
            return _half(outs[a].at[slot], hc, rows[a]) if split[a] else outs[a].at[slot]

        def direct(a, t, slot, dev):
            return _remote(piece(a, slot, c), piece(a, slot, c), send.at[a, t], recv.at[a, t], dev)

        sends = [direct(a, t, j, (*chips[t], c)) for a in range(n) for t in range(3)]
        for cp in sends:
            cp.start()
        fwd = []
        for t, (px, py) in enumerate(chips):
            jt = 2 * px + py
            for a in range(n):
                direct(a, t, jt, (px, py, c)).wait_recv()
                if split[a]:
                    cp = _remote(piece(a, jt, c), piece(a, jt, c), fsend.at[a, t], frecv.at[a, t], (x, y, 1 - c))
                    cp.start()
                    fwd.append(cp)
        for t, (px, py) in enumerate(chips):
            jt = 2 * px + py
            for a in range(n):
                if split[a]:
                    _remote(piece(a, jt, 1 - c), piece(a, jt, 1 - c), fsend.at[a, t], frecv.at[a, t], (x, y, 1 - c)).wait_recv()
        for cp in sends + fwd:
            cp.wait_send()

    sem = pltpu.SemaphoreType.DMA
    return pl.pallas_call(
        body, name="allgather_weights", in_specs=[ANY] * n, out_specs=[ANY] * n,
        out_shape=[jax.ShapeDtypeStruct(s.shape, s.dtype) for s in stacks],
        input_output_aliases={a: a for a in range(n)},
        scratch_shapes=[sem((n, 3)), sem((n, 3)), sem((n, 3)), sem((n, 3))],
    )(*stacks)


def _sibling_swap(grads, tag):
    n = len(grads)

    def body(*refs):
        ins, got = refs[:n], refs[n:2 * n]
        send, recv = refs[2 * n:]
        x, y, c, _, _ = _place()
        cps = []
        for a in range(n):
            h = grads[a].shape[1] // 2
            cp = _remote(ins[a].at[:, pl.ds((1 - c) * h, h)], got[a], send.at[a], recv.at[a], (x, y, 1 - c))
            cp.start()
            cps.append(cp)
        for cp in cps:
            cp.wait_recv()
            cp.wait_send()

    sem = pltpu.SemaphoreType.DMA
    halves = [jax.ShapeDtypeStruct((g.shape[0], g.shape[1] // 2, g.shape[2]), g.dtype) for g in grads]
    return pl.pallas_call(body, name=f"rs_sibling_swap_{tag}", in_specs=[ANY] * n, out_specs=[ANY] * n, out_shape=halves,
                          scratch_shapes=[sem((n,)), sem((n,))])(*grads)


def _sibling_join(blocks, tag):
    n = len(blocks)

    def body(*refs):
        outs = refs[n:2 * n]
        send, recv = refs[2 * n:]
        x, y, c, _, _ = _place()
        cps = []
        for a in range(n):
            h = blocks[a].shape[0] // 2
            mine = outs[a].at[pl.ds(c * h, h)]
            cp = _remote(mine, mine, send.at[a], recv.at[a], (x, y, 1 - c))
            cp.start()
            cps.append(cp)
        for a, cp in enumerate(cps):
            h = blocks[a].shape[0] // 2
            theirs = outs[a].at[pl.ds((1 - c) * h, h)]
            _remote(theirs, theirs, send.at[a], recv.at[a], (x, y, 1 - c)).wait_recv()
            cp.wait_send()

    sem = pltpu.SemaphoreType.DMA
    return pl.pallas_call(body, name=f"rs_sibling_join_{tag}", in_specs=[ANY] * n, out_specs=[ANY] * n,
                          out_shape=[jax.ShapeDtypeStruct(b_.shape, b_.dtype) for b_ in blocks],
                          input_output_aliases={a: a for a in range(n)},
                          scratch_shapes=[sem((n,)), sem((n,))])(*blocks)


HBM = pl.BlockSpec(memory_space=pltpu.HBM)
SEM = pl.BlockSpec(memory_space=pltpu.SEMAPHORE)
TOKEN = jax.ShapeDtypeStruct((8, 128), F32)


def _in_flight_params():
    return pltpu.CompilerParams(has_side_effects=pltpu.SideEffectType.DATAFLOW_SIDE_EFFECTING)


def _in_hbm(a):
    return pltpu.with_memory_space_constraint(a, pltpu.HBM)


def _gather_piece(ref, rows, split, slot, hc):
    return _half(ref.at[slot], hc, rows) if split else ref.at[slot]


def _gather_start(stacks, split, after):
    n = len(stacks)

    def body(*refs):
        ins = refs[:n]
        send, recv = refs[n + 1], refs[n + 2]
        token = refs[2 * n + 3]
        _, _, c, j, chips = _place()
        for a in range(n):
            mine = _gather_piece(ins[a], stacks[a].shape[1], split[a], j, c)
            for t in range(3):
                _remote(mine, mine, send.at[3 * a + t], recv.at[3 * a + t], (*chips[t], c)).start()
        token[...] = jnp.zeros_like(token)

    sems = pltpu.SemaphoreType.DMA((3 * n,))
    res = pl.pallas_call(
        body, name="gather_start", in_specs=[HBM] * n + [ANY],
        out_specs=[SEM, SEM] + [HBM] * n + [pl.BlockSpec(memory_space=pltpu.VMEM)],
        out_shape=[sems, sems] + [pltpu.HBM(s.shape, s.dtype) for s in stacks] + [TOKEN],
        input_output_aliases={a: a + 2 for a in range(n)}, compiler_params=_in_flight_params(),
    )(*[_in_hbm(s) for s in stacks], after)
    return res[0], res[1], res[2:2 + n], res[2 + n]


def _gather_wait(send, recv, stacks, split, after):
    n = len(stacks)

    def body(*refs):
        ins = refs[:n]
        send_ref, recv_ref = refs[n], refs[n + 1]
        _, _, c, j, chips = _place()
        for a in range(n):
            rows = stacks[a].shape[1]
            mine = _gather_piece(ins[a], rows, split[a], j, c)
            for t, (px, py) in enumerate(chips):
                theirs = _gather_piece(ins[a], rows, split[a], 2 * px + py, c)
                _remote(mine, mine, send_ref.at[3 * a + t], recv_ref.at[3 * a + t], (px, py, c)).wait_send()
                _remote(theirs, theirs, send_ref.at[3 * a + t], recv_ref.at[3 * a + t], (px, py, c)).wait_recv()

    return pl.pallas_call(
        body, name="gather_wait", in_specs=[HBM] * n + [SEM, SEM, ANY], out_specs=[HBM] * n,
        out_shape=[pltpu.HBM(s.shape, s.dtype) for s in stacks],
        input_output_aliases={a: a for a in range(n)}, compiler_params=_in_flight_params(),
    )(*stacks, send, recv, after)


def _gather_forward(stacks, split):
    idx = [a for a in range(len(stacks)) if split[a]]
    n = len(idx)

    def body(*refs):
        outs = refs[n:2 * n]
        send, recv = refs[2 * n:]
        x, y, c, _, chips = _place()
        sends = []
        for t, (px, py) in enumerate(chips):
            for a in range(n):
                blk = _half(outs[a].at[2 * px + py], c, stacks[idx[a]].shape[1])
                cp = _remote(blk, blk, send.at[a, t], recv.at[a, t], (x, y, 1 - c))
                cp.start()
                sends.append(cp)
        for t, (px, py) in enumerate(chips):
            for a in range(n):
                blk = _half(outs[a].at[2 * px + py], 1 - c, stacks[idx[a]].shape[1])
                _remote(blk, blk, send.at[a, t], recv.at[a, t], (x, y, 1 - c)).wait_recv()
        for cp in sends:
            cp.wait_send()

    sem = pltpu.SemaphoreType.DMA
    res = pl.pallas_call(
        body, name="gather_forward", in_specs=[ANY] * n, out_specs=[ANY] * n,
        out_shape=[jax.ShapeDtypeStruct(stacks[a].shape, stacks[a].dtype) for a in idx],
        input_output_aliases={a: a for a in range(n)}, scratch_shapes=[sem((n, 3)), sem((n, 3))],
    )(*[stacks[a] for a in idx])
    out = list(stacks)
    for a, r in zip(idx, res):
        out[a] = r
    return out


def _exchange_start(parts, tag):
    n = len(parts)

    def body(*refs):
        ins, lands = refs[:n], refs[n:2 * n]
        send, recv = refs[2 * n], refs[2 * n + 1]
        token = refs[4 * n + 2]
        _, _, c, j, chips = _place()
        for t, (px, py) in enumerate(chips):
            for a in range(n):
                _remote(ins[a].at[2 * px + py], lands[a].at[j], send.at[3 * a + t], recv.at[3 * a + t], (px, py, c)).start()
        token[...] = jnp.zeros_like(token)

    sems = pltpu.SemaphoreType.DMA((3 * n,))
    bufs = [pltpu.HBM(p.shape, p.dtype) for p in parts]
    res = pl.pallas_call(
        body, name=f"exchange_start_{tag}", in_specs=[HBM] * (2 * n),
        out_specs=[SEM, SEM] + [HBM] * (2 * n) + [pl.BlockSpec(memory_space=pltpu.VMEM)],
        out_shape=[sems, sems] + bufs + bufs + [TOKEN],
        input_output_aliases={a: a + 2 for a in range(2 * n)}, compiler_params=_in_flight_params(),
    )(*[_in_hbm(p) for p in parts], *[_in_hbm(lax.empty(p.shape, p.dtype)) for p in parts])
    return res[0], res[1], res[2:2 + n], res[2 + n:2 + 2 * n], res[2 + 2 * n]


def _exchange_wait(send, recv, parts, lands, after, tag):
    n = len(parts)

    def body(*refs):
        ins, lnd = refs[:n], refs[n:2 * n]
        send_ref, recv_ref = refs[2 * n], refs[2 * n + 1]
        _, _, c, j, chips = _place()
        for t, (px, py) in enumerate(chips):
            jt = 2 * px + py
            for a in range(n):
                _remote(ins[a].at[jt], lnd[a].at[j], send_ref.at[3 * a + t], recv_ref.at[3 * a + t], (px, py, c)).wait_send()
                _remote(ins[a].at[jt], lnd[a].at[jt], send_ref.at[3 * a + t], recv_ref.at[3 * a + t], (px, py, c)).wait_recv()

    bufs = [pltpu.HBM(p.shape, p.dtype) for p in parts]
    res = pl.pallas_call(
        body, name=f"exchange_wait_{tag}", in_specs=[HBM] * (2 * n) + [SEM, SEM, ANY], out_specs=[HBM] * (2 * n),
        out_shape=bufs + bufs, input_output_aliases={a: a for a in range(2 * n)}, compiler_params=_in_flight_params(),
    )(*parts, *lands, send, recv, after)
    return res[:n], res[n:]


def _allreduce_small(arrs):
    n = len(arrs)

    def body(*refs):
        ins, outs = refs[:n], refs[n:2 * n]
        sib, chip = refs[2 * n:3 * n], refs[3 * n:4 * n]
        ssend, srecv, csend, crecv = refs[4 * n:]
        x, y, c, j, chips = _place()
        swaps = [_remote(ins[a], sib[a], ssend.at[a], srecv.at[a], (x, y, 1 - c)) for a in range(n)]
        for cp in swaps:
            cp.start()
        sends = []
        for a in range(n):
            swaps[a].wait_recv()
            chip[a][j] = ins[a][...] + sib[a][...]
            for t, (px, py) in enumerate(chips):
                cp = _remote(chip[a].at[j], chip[a].at[j], csend.at[a, t], crecv.at[a, t], (px, py, c))
                cp.start()
                sends.append(cp)
        for a in range(n):
            for t, (px, py) in enumerate(chips):
                jt = 2 * px + py
                _remote(chip[a].at[jt], chip[a].at[jt], csend.at[a, t], crecv.at[a, t], (px, py, c)).wait_recv()
            outs[a][...] = ((chip[a][0] + chip[a][1]) + chip[a][2]) + chip[a][3]
        for cp in swaps + sends:
            cp.wait_send()

    sem = pltpu.SemaphoreType.DMA
    vm = pl.BlockSpec(memory_space=pltpu.VMEM)
    return pl.pallas_call(
        body, name="allreduce_small", in_specs=[vm] * n, out_specs=[vm] * n,
        out_shape=[jax.ShapeDtypeStruct(a.shape, F32) for a in arrs],
        scratch_shapes=[pltpu.VMEM(a.shape, F32) for a in arrs] + [pltpu.VMEM((N_SHARD, *a.shape), F32) for a in arrs]
        + [sem((n,)), sem((n,)), sem((n, 3)), sem((n, 3))],
        compiler_params=_cp(),
    )(*arrs)


SMALL_1024 = ("ln1_g", "ln1_b", "ln2_g", "ln2_b", "b_ple_gate", "ln3_g", "ln3_b")


def _adamw_small(red3, red1, redz, g_conv_w, redc, red_ws, red_bs, params):
    shape2d = {"ln_z_g": (1, D_GMLP), "ln_z_b": (1, D_GMLP), "w_s": (N_HEADS * BLK, BLK), "b_s": (N_HEADS, BLK),
               "conv_w": (3, FF_BLK), "conv_b": (N_SHARD, FF_BLK), **{k: (1, D_MODEL) for k in SMALL_1024}}
    names = list(shape2d)
    flat = [a.reshape(shape2d[k]) for k in names for a in params[k]]

    def body(r3, r1, rz, gcw, rc, rws, rbs, *refs):
        ins, outs = refs[:3 * len(names)], refs[3 * len(names):]

        def grad_of(k):
            if k == "w_s":
                return rws[...]
            if k == "b_s":
                return rbs[...]
            if k == "conv_w":
                return gcw[0:3, :]
            if k == "conv_b":
                return jnp.concatenate([rc[j * STAT_ROWS + 3:j * STAT_ROWS + 4, :] for j in range(N_SHARD)], axis=0)
            src, row = {"ln3_g": (r3, 0), "ln3_b": (r3, 1), "b_ple_gate": (r3, 2), "ln2_g": (r3, 3), "ln2_b": (r3, 4),
                        "ln1_g": (r1, 0), "ln1_b": (r1, 1), "ln_z_g": (rz, 0), "ln_z_b": (rz, 1)}[k]
            return src[row:row + 1, :]

        for i, k in enumerate(names):
            w_ref, m_ref, v_ref = ins[3 * i:3 * i + 3]
            g_ref, d_ref, nm_ref, nv_ref = outs[4 * i:4 * i + 4]
            g = grad_of(k)
            g_ref[...] = g
            d_ref[...], nm_ref[...], nv_ref[...] = _adamw_math(w_ref[...], g, m_ref[...], v_ref[...])

    res = pl.pallas_call(
        body, name="adamw_small",
        out_shape=[jax.ShapeDtypeStruct(shape2d[k], F32) for k in names for _ in range(4)],
        compiler_params=_cp(),
    )(red3, red1, redz, g_conv_w, redc, red_ws, red_bs, *flat)
    return {k: tuple(r.reshape(params[k][0].shape) for r in res[4 * i:4 * i + 4]) for i, k in enumerate(names)}


WEIGHTS = ("w_in", "ln_z_g", "ln_z_b", "w_s", "b_s", "w_o", "ln1_g", "ln1_b", "w_ff_a", "w_ff_b", "conv_w", "conv_b",
           "w_ff_down", "ln2_g", "ln2_b", "w_ple_gate", "b_ple_gate", "w_ple_in", "ln3_g", "ln3_b")
BIG = ("w_in", "w_o", "w_ff_a", "w_ff_b", "w_ff_down", "w_ple_gate", "w_ple_in")
LATE = ("w_o", "w_ff_a", "w_ff_b", "w_ff_down", "w_ple_gate", "w_ple_in", "conv_w")


def kernel(x, p, positions, w_in, ln_z_g, ln_z_b, w_s, b_s, w_o, ln1_g, ln1_b, w_ff_a, w_ff_b, conv_w, conv_b, w_ff_down, ln2_g, ln2_b, w_ple_gate, b_ple_gate, w_ple_in, ln3_g, ln3_b, loss_target, m_w_in, m_ln_z_g, m_ln_z_b, m_w_s, m_b_s, m_w_o, m_ln1_g, m_ln1_b, m_w_ff_a, m_w_ff_b, m_conv_w, m_conv_b, m_w_ff_down, m_ln2_g, m_ln2_b, m_w_ple_gate, m_b_ple_gate, m_w_ple_in, m_ln3_g, m_ln3_b, v_w_in, v_ln_z_g, v_ln_z_b, v_w_s, v_b_s, v_w_o, v_ln1_g, v_ln1_b, v_w_ff_a, v_w_ff_b, v_conv_w, v_conv_b, v_w_ff_down, v_ln2_g, v_ln2_b, v_w_ple_gate, v_b_ple_gate, v_w_ple_in, v_ln3_g, v_ln3_b):
    args = locals()
    w = {k: args[k] for k in WEIGHTS}
    m = {k: args["m_" + k] for k in WEIGHTS}
    v = {k: args["v_" + k] for k in WEIGHTS}

    chip = 2 * lax.axis_index("x") + lax.axis_index("y")
    place = jnp.stack([chip, lax.axis_index("c")]).astype(jnp.int32)
    stack = {k: _place_shard(f"cast_{k}", w[k][0], place, MXU) for k in BIG}
    stack["conv_w"] = _place_shard("place_conv_w", w["conv_w"][0], place, F32)
    w_in_full, = _allgather([stack["w_in"]], [True])
    split_late = [k != "conv_w" for k in LATE]
    g_send, g_recv, late_flight, start_dep = _gather_start([stack[k] for k in LATE], split_late, w_in_full)

    def late_weights(after):
        landed = _gather_wait(g_send, g_recv, late_flight, split_late, after)
        fw = dict(zip(LATE, _gather_forward(landed, split_late)))
        return (fw["w_o"].reshape(D_MODEL, D_MODEL), fw["w_ff_a"], fw["w_ff_b"], fw["conv_w"], fw["w_ff_down"],
                fw["w_ple_gate"].reshape(D_MODEL, D_MODEL), fw["w_ple_in"])

    def partial_sums(names, grads, tag):
        stacked = [g.reshape(N_SHARD, *w[k].shape[1:]) for k, g in zip(names, grads)]
        got = _sibling_swap(stacked, tag)
        pair = [_pair_sum_bf16(f"rs_pair_{k}", s, g, place[1:2]) for k, s, g in zip(names, stacked, got)]
        return (names, tag, *_exchange_start(pair, tag))

    def reduced(trip, after, dep):
        names, tag, send, recv, pair, lands, _ = trip
        pair, landed = _exchange_wait(send, recv, pair, lands, after, tag)
        blocks = [_chip_sum(f"rs_sum_{k}", own, l, place, dep) for k, own, l in zip(names, pair, landed)]
        return dict(zip(names, _sibling_join(blocks, tag)))

    trips = {}

    def early_grads(grads):
        trips["early"] = partial_sums(list(grads), list(grads.values()), "early")
        return trips["early"][-1]

    grad_x, g_w_in, dws, dbs, (stat3, stat1, zstat, cstat) = _local_step(
        x[0], p[0, 0], positions, loss_target[0], w_in_full, start_dep, late_weights, early_grads,
        ln_z_g, ln_z_b, w_s, b_s, ln1_g, ln1_b, conv_b, ln2_g, ln2_b, b_ple_gate, ln3_g, ln3_b)

    trips["w_in"] = partial_sums(["w_in"], [g_w_in], "w_in")
    out = {}

    def adamw(red):
        for k, g in red.items():
            d, nm, nv = _adamw_big(f"adamw_{k}", w[k], g, m[k], v[k])
            out[k] = (g.reshape(w[k].shape), d, nm, nv)

    adamw(reduced(trips["early"], grad_x, trips["w_in"][-1]))
    adamw(reduced(trips["w_in"], out["w_o"][3], start_dep))

    red3, red1, redz, redc, red_ws, red_bs = _allreduce_small(
        [stat3, stat1, zstat, cstat.reshape(N_SHARD * STAT_ROWS, FF_BLK), dws.reshape(N_HEADS * BLK, BLK), dbs])
    loss = (0.5 / D_MODEL) * jnp.sum(red3[5])
    g_conv_w = lax.dynamic_slice_in_dim(redc, chip * STAT_ROWS, STAT_ROWS, 0)
    names_small = [k for k in WEIGHTS if k not in BIG]
    out.update(_adamw_small(red3, red1, redz, g_conv_w, redc, red_ws, red_bs, {k: (w[k], m[k], v[k]) for k in names_small}))

    return (loss, grad_x[None], *[out[k][0] for k in WEIGHTS], *[out[k][1] for k in WEIGHTS],
            *[out[k][2] for k in WEIGHTS], *[out[k][3] for k in WEIGHTS])
```

```python
import functools
import math

import numpy as np
import jax
import jax.numpy as jnp
from jax import lax
from jax.experimental import pallas as pl
from jax.experimental.pallas import tpu as pltpu

F32 = jnp.float32
BF16 = jnp.bfloat16
MXU = BF16

D_MODEL = 1024
HEAD_DIM = 64
N_HEADS = 8
D_ATTN = 512
D_GMLP = 512
D_IN = 2560
DILATIONS = (1, 4, 16)
BLK = 128
ROPE_THETA = 500000.0
ROPE_DIM = 16
D_FF = 2816
D_PLE = 256
LN_EPS = 1e-5
ALPHA = 2.0 ** 0.25
NEG_INF = -1e30
N_SHARD = 4
W_IN_BLK = D_IN // N_SHARD
FF_BLK = D_FF // N_SHARD
ROW_BLK = D_MODEL // N_SHARD
ADAM_LR, ADAM_B1, ADAM_B2, ADAM_EPS, ADAM_WD, ADAM_STEP = 0.001, 0.9, 0.999, 1e-08, 0.01, 10

TM = 512
HALO = 8
VMEM_LIMIT = 56 * 1024 * 1024


def _cp(**kw):
    return pltpu.CompilerParams(vmem_limit_bytes=VMEM_LIMIT, **kw)


def _full(shape):
    n = len(shape)
    return pl.BlockSpec(shape, lambda *_: (0,) * n)


def _gelu(x):
    return 0.5 * x * (1.0 + lax.erf(x * (1.0 / math.sqrt(2.0))))


def _gelu_grad(x):
    return 0.5 * (1.0 + lax.erf(x * (1.0 / math.sqrt(2.0)))) + x * jnp.exp(-0.5 * x * x) * (1.0 / math.sqrt(2.0 * math.pi))


def _ln_fwd(r):
    mu = jnp.mean(r, axis=-1, keepdims=True)
    xc = r - mu
    var = jnp.mean(xc * xc, axis=-1, keepdims=True)
    rstd = lax.rsqrt(var + LN_EPS)
    return xc * rstd, rstd


def _ln_bwd(dy, xhat, rstd, g):
    dxh = dy * g
    m1 = jnp.mean(dxh, axis=-1, keepdims=True)
    m2 = jnp.mean(dxh * xhat, axis=-1, keepdims=True)
    return rstd * (dxh - m1 - xhat * m2)


def _dot(a, b):
    return jnp.dot(a.astype(MXU), b.astype(MXU), preferred_element_type=F32)


def _dot_nt(a, b):
    return lax.dot_general(a.astype(MXU), b.astype(MXU), (((1,), (1,)), ((), ())), preferred_element_type=F32)


def _dot_tn(a, b):
    return lax.dot_general(a.astype(MXU), b.astype(MXU), (((0,), (0,)), ((), ())), preferred_element_type=F32)


def _colsum(v):
    return jnp.sum(v, axis=0, keepdims=True)


def _rope_tables(positions, t):
    inv = np.float32(ROPE_THETA) ** (-np.arange(0, ROPE_DIM, 2, dtype=np.float32) / np.float32(ROPE_DIM))
    half = ROPE_DIM // 2
    pos_rep = jnp.repeat(positions.reshape(t // 16, 16), half, axis=1)
    inv_row = jnp.asarray(np.tile(inv, 16)[None, :], F32)

    def trig_body(pos_ref, inv_ref, cos_ref, sin_ref):
        ang = pos_ref[...].astype(F32) * inv_ref[...]
        cos_ref[...] = jnp.cos(ang)
        sin_ref[...] = jnp.sin(ang)

    cos8, sin8 = pl.pallas_call(
        trig_body, name="rope_trig",
        out_shape=(jax.ShapeDtypeStruct((t // 16, 128), F32), jax.ShapeDtypeStruct((t // 16, 128), F32)),
    )(pos_rep, inv_row)
    cos8 = cos8.reshape(t, half)
    sin8 = sin8.reshape(t, half)

    lane = np.arange(128) % HEAD_DIM
    sel = (np.arange(half)[:, None] == (lane % half)[None, :])
    e_cos = (sel & (lane < ROPE_DIM)[None, :]).astype(np.float32)
    e_s1 = -(sel & (lane < half)[None, :]).astype(np.float32)
    e_s2 = (sel & ((lane >= half) & (lane < ROPE_DIM))[None, :]).astype(np.float32)
    ones = (lane >= ROPE_DIM).astype(np.float32)[None, :]

    def expand_body(cos_ref, sin_ref, ec_ref, e1_ref, e2_ref, ones_ref, c_ref, s1_ref, s2_ref):
        hp = lax.Precision.HIGHEST
        c_ref[...] = jnp.dot(cos_ref[...], ec_ref[...], precision=hp, preferred_element_type=F32) + ones_ref[...]
        s1_ref[...] = jnp.dot(sin_ref[...], e1_ref[...], precision=hp, preferred_element_type=F32)
        s2_ref[...] = jnp.dot(sin_ref[...], e2_ref[...], precision=hp, preferred_element_type=F32)

    tab = jax.ShapeDtypeStruct((t, 128), F32)
    return pl.pallas_call(expand_body, name="rope_expand", out_shape=(tab, tab, tab), compiler_params=_cp())(
        cos8, sin8, jnp.asarray(e_cos), jnp.asarray(e_s1), jnp.asarray(e_s2), jnp.asarray(ones))


def _tile_heads(tab):
    return jnp.concatenate([tab] * (D_ATTN // 128), axis=1)


def _rope_apply(v, c, s1, s2):
    n = v.shape[1]
    half = ROPE_DIM // 2
    return v * c + pltpu.roll(v, n - half, 1) * s1 + pltpu.roll(v, half, 1) * s2


def _rope_apply_t(g, c, s1, s2):
    n = g.shape[1]
    half = ROPE_DIM // 2
    return g * c + pltpu.roll(g * s1, half, 1) + pltpu.roll(g * s2, n - half, 1)


LANE_CHUNKS = D_ATTN // 128
HEAD_LANES = 128 // N_HEADS


def _perm_shape(t, d, w, dtype):
    return jax.ShapeDtypeStruct((d, t // d, w), dtype)


def _perm_tile(d, w):
    return pl.BlockSpec((None if d == 1 else d, TM // d, w), lambda i: (0, i, 0))


def _to_planes(ref, scr, d, n_chunks, dtype):
    for r in range(d):
        for cc in range(n_chunks):
            ref[r, :, cc * 128:(cc + 1) * 128] = scr.at[cc][pl.ds(r, TM // d, stride=d), :].astype(dtype)


def _from_planes(ref, scr, d, n_chunks, accumulate=False):
    for r in range(d):
        for cc in range(n_chunks):
            rows = scr.at[cc]
            val = ref[r, :, cc * 128:(cc + 1) * 128].astype(F32)
            if accumulate:
                rows[pl.ds(r, TM // d, stride=d), :] += val
            else:
                rows[pl.ds(r, TM // d, stride=d), :] = val


def _chunks(val):
    return [val[:, cc * 128:(cc + 1) * 128] for cc in range(val.shape[1] // 128)]


def _unchunk(scr, n_chunks, base=0):
    return jnp.concatenate([scr[base + cc] for cc in range(n_chunks)], axis=1)


def _head_expand():
    src = np.arange(128)[:, None]
    dst = np.arange(D_ATTN)[None, :]
    return jnp.asarray((src == (dst // HEAD_DIM) * HEAD_LANES).astype(np.float32))


def _head_reduce():
    src = np.arange(D_ATTN)[:, None]
    dst = np.arange(128)[None, :]
    return jnp.asarray((src // HEAD_DIM == dst // HEAD_LANES).astype(np.float32))


def _dot_exact(a, b):
    return jnp.dot(a, b, precision=lax.Precision.HIGHEST, preferred_element_type=F32)


def _qkvuz(x, w_in, c_tab, s1_tab, s2_tab, ln_z_g, ln_z_b, w_s, b_full, dep):
    t = x.shape[0]
    nchunk = TM // BLK

    def body(x_ref, w_ref, c_ref, s1_ref, s2_ref, g_ref, b_ref, ws_ref, bf_ref, dep_ref,
             qkv1_ref, qkv4_ref, qkv16_ref, hu_ref, hz_ref, mixed_ref, gm_ref, h_scr, wm_scr, p_scr):
        @pl.when(pl.program_id(0) == 0)
        def _():
            row = lax.broadcasted_iota(jnp.int32, (BLK, BLK), 0)
            col = lax.broadcasted_iota(jnp.int32, (BLK, BLK), 1)
            for g in range(N_HEADS):
                wm_scr[g] = jnp.where(col <= row, ws_ref[g], 0.0).astype(MXU)

        xb = x_ref[...].astype(MXU)
        for j in range(N_SHARD):
            h_scr[:, j * W_IN_BLK:(j + 1) * W_IN_BLK] = jnp.dot(xb, w_ref[j], preferred_element_type=F32)
        c, s1, s2 = _tile_heads(c_ref[...]), _tile_heads(s1_ref[...]), _tile_heads(s2_ref[...])
        q = _rope_apply(h_scr[:, 0:D_ATTN], c, s1, s2) * (1.0 / math.sqrt(HEAD_DIM))
        k = _rope_apply(h_scr[:, D_ATTN:2 * D_ATTN], c, s1, s2)
        for part, val in enumerate((q, k, h_scr[:, 2 * D_ATTN:3 * D_ATTN])):
            qkv1_ref[:, part * D_ATTN:(part + 1) * D_ATTN] = val.astype(MXU)
            for cc in range(LANE_CHUNKS):
                p_scr[part * LANE_CHUNKS + cc] = val[:, cc * 128:(cc + 1) * 128]
        _to_planes(qkv4_ref, p_scr, DILATIONS[1], 3 * LANE_CHUNKS, MXU)
        _to_planes(qkv16_ref, p_scr, DILATIONS[2], 3 * LANE_CHUNKS, MXU)
        hu = h_scr[:, 3 * D_ATTN:3 * D_ATTN + D_GMLP]
        hz = h_scr[:, 3 * D_ATTN + D_GMLP:]
        hu_ref[...] = hu
        hz_ref[...] = hz
        zhat, _ = _ln_fwd(_gelu(hz))
        zn = (zhat * g_ref[...] + b_ref[...]).astype(MXU)
        for ch in range(nchunk):
            rows = slice(ch * BLK, (ch + 1) * BLK)
            for g in range(N_HEADS):
                cols = slice(g * HEAD_DIM, (g + 1) * HEAD_DIM)
                mixed_ref[rows, cols] = jnp.dot(wm_scr[g], zn[rows, cols], preferred_element_type=F32) + bf_ref[:, cols]
        gm_ref[...] = (_gelu(hu) * mixed_ref[...]).astype(MXU)

    tok = lambda w: pl.BlockSpec((TM, w), lambda i: (i, 0))
    outs = [_perm_shape(t, d, 3 * D_ATTN, MXU) for d in DILATIONS] + [jax.ShapeDtypeStruct((t, D_GMLP), F32)] * 3 + [
        jax.ShapeDtypeStruct((t, D_GMLP), MXU)]
    return pl.pallas_call(
        body, name="qkvuz", grid=(t // TM,),
        in_specs=[tok(D_MODEL), _full(w_in.shape), tok(128), tok(128), tok(128), _full(ln_z_g.shape), _full(ln_z_b.shape),
                  _full(w_s.shape), _full(b_full.shape), pl.BlockSpec(memory_space=pl.ANY)],
        out_specs=[_perm_tile(d, 3 * D_ATTN) for d in DILATIONS] + [tok(D_ATTN)] * 4, out_shape=outs,
        scratch_shapes=[pltpu.VMEM((TM, D_IN), F32), pltpu.VMEM((N_HEADS, BLK, BLK), MXU),
                        pltpu.VMEM((3 * LANE_CHUNKS, TM, 128), F32)],
        compiler_params=_cp(dimension_semantics=("arbitrary",)),
    )(x, w_in, c_tab, s1_tab, s2_tab, ln_z_g, ln_z_b, w_s, b_full, dep)


def _band_valid(n):
    i = lax.broadcasted_iota(jnp.int32, (BLK, 2 * BLK), 0)
    j = lax.broadcasted_iota(jnp.int32, (BLK, 2 * BLK), 1)
    return (j >= i) & (j <= i + BLK) & ((j >= BLK) | (n > 0))


def _attn_fwd(qkv, d):
    _, l_sub, _ = qkv.shape
    nb = l_sub // BLK

    def body(q_ref, kp_ref, kc_ref, vp_ref, vc_ref, o_ref, l_ref):
        valid = _band_valid(pl.program_id(1))
        kcat = jnp.concatenate([kp_ref[...], kc_ref[...]], axis=0)
        vcat = jnp.concatenate([vp_ref[...], vc_ref[...]], axis=0)
        for h in range(N_HEADS):
            cols = slice(h * HEAD_DIM, (h + 1) * HEAD_DIM)
            s = jnp.where(valid, _dot_nt(q_ref[:, cols], kcat[:, cols]), NEG_INF)
            m = jnp.max(s, axis=-1, keepdims=True)
            e = jnp.exp(s - m)
            den = jnp.sum(e, axis=-1, keepdims=True)
            o_ref[:, cols] = _dot(e, vcat[:, cols]) * (1.0 / den)
            l_ref[:, h * HEAD_LANES:(h + 1) * HEAD_LANES] = jnp.broadcast_to(m + jnp.log(den), (BLK, HEAD_LANES))

    def blk(w, col, prev=False):
        return pl.BlockSpec((None, BLK, w), lambda r, n: (r, jnp.maximum(n - 1, 0) if prev else n, col))

    return pl.pallas_call(
        body, name=f"attn_fwd_d{d}", grid=(d, nb),
        in_specs=[blk(D_ATTN, 0), blk(D_ATTN, 1, True), blk(D_ATTN, 1), blk(D_ATTN, 2, True), blk(D_ATTN, 2)],
        out_specs=[blk(D_ATTN, 0), blk(128, 0)],
        out_shape=[jax.ShapeDtypeStruct((d, l_sub, D_ATTN), F32), jax.ShapeDtypeStruct((d, l_sub, 128), F32)],
        compiler_params=_cp(dimension_semantics=("arbitrary", "arbitrary")),
    )(qkv, qkv, qkv, qkv, qkv)


def _attn_bwd(qkv, do, lse, delta, d, dep):
    _, l_sub, _ = qkv.shape
    nb = l_sub // BLK
    whole = l_sub <= 8 * BLK

    def shares(n, q_ref, kp_ref, kc_ref, vp_ref, vc_ref, do_ref, l_ref, dl_ref, dq_ref):
        valid = _band_valid(n)
        kcat = jnp.concatenate([kp_ref[...], kc_ref[...]], axis=0)
        vcat = jnp.concatenate([vp_ref[...], vc_ref[...]], axis=0)
        for h in range(N_HEADS):
            cols = slice(h * HEAD_DIM, (h + 1) * HEAD_DIM)
            stat = slice(h * HEAD_LANES, h * HEAD_LANES + 1)
            qh, doh = q_ref[:, cols], do_ref[:, cols]
            p = jnp.where(valid, jnp.exp(_dot_nt(qh, kcat[:, cols]) - l_ref[:, stat]), 0.0)
            ds = p * (_dot_nt(doh, vcat[:, cols]) - dl_ref[:, stat])
            dq_ref[:, cols] = _dot(ds, kcat[:, cols])
            yield cols, _dot_tn(ds, qh), _dot_tn(p, doh)

    def body_whole(*refs):
        dk_ref, dv_ref = refs[10:]
        n = pl.program_id(1)
        cur = pl.ds(pl.multiple_of(n * BLK, BLK), BLK)
        prev = pl.ds(pl.multiple_of(jnp.maximum(n - 1, 0) * BLK, BLK), BLK)
        for cols, dk2, dv2 in shares(n, *refs[:8], refs[9]):
            dk_ref[cur, cols] = dk2[BLK:]
            dv_ref[cur, cols] = dv2[BLK:]
            dk_ref[prev, cols] += dk2[0:BLK]
            dv_ref[prev, cols] += dv2[0:BLK]

    def body_carry(*refs):
        dk_ref, dv_ref, ck_scr, cv_scr = refs[10:]
        n = pl.program_id(1)

        @pl.when(n == 0)
        def _():
            ck_scr[...] = jnp.zeros_like(ck_scr)
            cv_scr[...] = jnp.zeros_like(cv_scr)

        @pl.when(n < nb)
        def _():
            for cols, dk2, dv2 in shares(n, *refs[:8], refs[9]):
                dk_ref[:, cols] = ck_scr[:, cols] + dk2[0:BLK]
                dv_ref[:, cols] = cv_scr[:, cols] + dv2[0:BLK]
                ck_scr[:, cols] = dk2[BLK:]
                cv_scr[:, cols] = dv2[BLK:]

        @pl.when(n == nb)
        def _():
            dk_ref[...] = ck_scr[...]
            dv_ref[...] = cv_scr[...]

    def blk(w, col, shift=0):
        return pl.BlockSpec((None, BLK, w), lambda r, n: (r, jnp.clip(n - shift, 0, nb - 1), col))

    if whole:
        dkv_spec = pl.BlockSpec((None, l_sub, D_ATTN), lambda r, n: (r, 0, 0))
        body, steps, scratch = body_whole, nb, []
    else:
        dkv_spec = blk(D_ATTN, 0, 1)
        body, steps, scratch = body_carry, nb + 1, [pltpu.VMEM((BLK, D_ATTN), F32)] * 2
    return pl.pallas_call(
        body, name=f"attn_bwd_d{d}", grid=(d, steps),
        in_specs=[blk(D_ATTN, 0), blk(D_ATTN, 1, 1), blk(D_ATTN, 1), blk(D_ATTN, 2, 1), blk(D_ATTN, 2),
                  blk(D_ATTN, 0), blk(128, 0), blk(128, 0), pl.BlockSpec(memory_space=pl.ANY)],
        out_specs=[blk(D_ATTN, 0), dkv_spec, dkv_spec],
        out_shape=[jax.ShapeDtypeStruct((d, l_sub, D_ATTN), F32)] * 3,
        scratch_shapes=scratch,
        compiler_params=_cp(dimension_semantics=("arbitrary", "arbitrary")),
    )(qkv, qkv, qkv, qkv, qkv, do, lse, delta, dep)


def _mix_ln1(os_, ls_, gm, x, w_o, ln1_g, ln1_b):
    t = x.shape[0]
    expand = _head_expand()

    def body(o1, o4, o16, l1, l4, l16, gm_ref, x_ref, wo_ref, g_ref, b_ref, ex_ref,
             attn_ref, lse1_ref, lse4_ref, lse16_ref, cat_ref, xhat_ref, rstd_ref, x1b_ref, o_scr, l_scr):
        _from_planes(o4, o_scr, DILATIONS[1], LANE_CHUNKS)
        _from_planes(o16, o_scr.at[pl.ds(LANE_CHUNKS, LANE_CHUNKS)], DILATIONS[2], LANE_CHUNKS)
        _from_planes(l4, l_scr, DILATIONS[1], 1)
        _from_planes(l16, l_scr.at[pl.ds(1, 1)], DILATIONS[2], 1)
        la, lb, lc = l1[...], l_scr[0], l_scr[1]
        m = jnp.maximum(jnp.maximum(la, lb), lc)
        ea, eb, ec = jnp.exp(la - m), jnp.exp(lb - m), jnp.exp(lc - m)
        den = ea + eb + ec
        inv = 1.0 / den
        wide = lambda w: _dot_exact(w, ex_ref[...])
        attn = (wide(ea * inv) * o1[...] + wide(eb * inv) * _unchunk(o_scr, LANE_CHUNKS)
                + wide(ec * inv) * _unchunk(o_scr, LANE_CHUNKS, LANE_CHUNKS))
        attn_ref[...] = attn
        lse = m + jnp.log(den)
        lse1_ref[...] = lse
        l_scr[2] = lse
        _to_planes(lse4_ref, l_scr.at[pl.ds(2, 1)], DILATIONS[1], 1, F32)
        _to_planes(lse16_ref, l_scr.at[pl.ds(2, 1)], DILATIONS[2], 1, F32)
        cat_ref[:, 0:D_ATTN] = attn.astype(MXU)
        cat_ref[:, D_ATTN:] = gm_ref[...]
        mix = jnp.dot(cat_ref[...], wo_ref[...], preferred_element_type=F32)
        xhat, rstd = _ln_fwd(ALPHA * x_ref[...] + mix)
        xhat_ref[...] = xhat
        rstd_ref[...] = rstd
        x1b_ref[...] = (xhat * g_ref[...] + b_ref[...]).astype(MXU)

    tok = lambda w: pl.BlockSpec((TM, w), lambda i: (i, 0))
    outs = [jax.ShapeDtypeStruct((t, D_ATTN), F32)] + [_perm_shape(t, d, 128, F32) for d in DILATIONS] + [
        jax.ShapeDtypeStruct((t, D_MODEL), MXU), jax.ShapeDtypeStruct((t, D_MODEL), F32), jax.ShapeDtypeStruct((t, 1), F32),
        jax.ShapeDtypeStruct((t, D_MODEL), MXU)]
    return pl.pallas_call(
        body, name="mix_ln1", grid=(t // TM,),
        in_specs=[_perm_tile(d, D_ATTN) for d in DILATIONS] + [_perm_tile(d, 128) for d in DILATIONS]
        + [tok(D_GMLP), tok(D_MODEL), _full(w_o.shape), _full(ln1_g.shape), _full(ln1_b.shape), _full(expand.shape)],
        out_specs=[tok(D_ATTN)] + [_perm_tile(d, 128) for d in DILATIONS] + [tok(D_MODEL), tok(D_MODEL), tok(1), tok(D_MODEL)],
        out_shape=outs,
        scratch_shapes=[pltpu.VMEM((2 * LANE_CHUNKS, TM, 128), F32), pltpu.VMEM((3, TM, 128), F32)],
        compiler_params=_cp(dimension_semantics=("arbitrary",)),
    )(*os_, *ls_, gm, x, w_o, ln1_g, ln1_b, expand)


def _conv_fwd(a_ext, w_ref, b_ref, rows):
    return (b_ref[...] + w_ref[2:3, :] * a_ext[HALO:HALO + rows] + w_ref[1:2, :] * a_ext[HALO - 1:HALO - 1 + rows]
            + w_ref[0:1, :] * a_ext[HALO - 2:HALO - 2 + rows])


def _ffn_in(x1b, w_a, w_b, conv_w, conv_b):
    t = x1b.shape[0]
    hb = TM // HALO

    def body(x_ref, xh_ref, wa_ref, wb_ref, cw_ref, cb_ref, apre_ref, b_ref, f_ref):
        i = pl.program_id(1)
        a_pre = _dot_nt(x_ref[...], wa_ref[...])
        a_halo = jnp.where(i > 0, _dot_nt(xh_ref[...], wa_ref[...]), 0.0)
        a = _conv_fwd(jnp.concatenate([a_halo, a_pre], axis=0), cw_ref, cb_ref, TM)
        b = _dot_nt(x_ref[...], wb_ref[...])
        apre_ref[...] = a_pre
        b_ref[...] = b
        f_ref[...] = (_gelu(a) * b).astype(MXU)

    blk = lambda r, c: pl.BlockSpec((None, r, c), lambda j, i: (j, 0, 0))
    tokj = pl.BlockSpec((None, TM, FF_BLK), lambda j, i: (j, i, 0))
    outs = [jax.ShapeDtypeStruct((N_SHARD, t, FF_BLK), F32)] * 2 + [jax.ShapeDtypeStruct((N_SHARD, t, FF_BLK), MXU)]
    return pl.pallas_call(
        body, name="ffn_in", grid=(N_SHARD, t // TM),
        in_specs=[pl.BlockSpec((TM, D_MODEL), lambda j, i: (i, 0)),
                  pl.BlockSpec((HALO, D_MODEL), lambda j, i: (jnp.maximum(i * hb - 1, 0), 0)),
                  blk(FF_BLK, D_MODEL), blk(FF_BLK, D_MODEL), blk(3, FF_BLK), blk(1, FF_BLK)],
        out_specs=[tokj, tokj, tokj], out_shape=outs,
        compiler_params=_cp(dimension_semantics=("arbitrary", "arbitrary")),
    )(x1b, x1b, w_a, w_b, conv_w, conv_b)


def _ffn_out_ln2(f, w_down, xhat1, ln1_g, ln1_b, ln2_g, ln2_b):
    t = xhat1.shape[0]

    def body(f_ref, wd_ref, xh_ref, g1_ref, b1_ref, g2_ref, b2_ref, xhat_ref, rstd_ref, x2b_ref):
        ff = jnp.dot(f_ref[0], wd_ref[0], preferred_element_type=F32)
        for j in range(1, N_SHARD):
            ff = ff + jnp.dot(f_ref[j], wd_ref[j], preferred_element_type=F32)
        x1 = xh_ref[...] * g1_ref[...] + b1_ref[...]
        xhat, rstd = _ln_fwd(ALPHA * x1 + ff)
        xhat_ref[...] = xhat
        rstd_ref[...] = rstd
        x2b_ref[...] = (xhat * g2_ref[...] + b2_ref[...]).astype(MXU)

    tok = lambda w: pl.BlockSpec((TM, w), lambda i: (i, 0))
    vec = _full((1, D_MODEL))
    outs = [jax.ShapeDtypeStruct((t, D_MODEL), F32), jax.ShapeDtypeStruct((t, 1), F32), jax.ShapeDtypeStruct((t, D_MODEL), MXU)]
    return pl.pallas_call(
        body, name="ffn_out_ln2", grid=(t // TM,),
        in_specs=[pl.BlockSpec((N_SHARD, TM, FF_BLK), lambda i: (0, i, 0)), _full(w_down.shape), tok(D_MODEL), vec, vec, vec, vec],
        out_specs=[tok(D_MODEL), tok(1), tok(D_MODEL)], out_shape=outs,
        compiler_params=_cp(dimension_semantics=("arbitrary",)),
    )(f, w_down, xhat1, ln1_g, ln1_b, ln2_g, ln2_b)


STAT_ROWS = 8


def _ple_loss_bwd(xhat2, rstd2, p, target, ln2_g, ln2_b, w_g, b_g, w_p, ln3_g, ln3_b):
    t = xhat2.shape[0]

    def body(xh2_ref, rs2_ref, p_ref, t_ref, g2_ref, b2_ref, wg_ref, bg_ref, wp_ref, g3_ref, b3_ref,
             dr2_ref, dgp_ref, dpp_ref, stat_ref, pp_scr):
        @pl.when(pl.program_id(0) == 0)
        def _():
            stat_ref[...] = jnp.zeros_like(stat_ref)

        xhat2 = xh2_ref[...]
        x2 = xhat2 * g2_ref[...] + b2_ref[...]
        gate = jax.nn.sigmoid(jnp.dot(x2.astype(MXU), wg_ref[...], preferred_element_type=F32) + bg_ref[...])
        pb = p_ref[...].astype(MXU)
        for j in range(N_SHARD):
            pp_scr[:, j * ROW_BLK:(j + 1) * ROW_BLK] = jnp.dot(pb, wp_ref[j], preferred_element_type=F32)
        pp = pp_scr[...]
        xhat3, rstd3 = _ln_fwd(ALPHA * x2 + gate * pp)
        err = xhat3 * g3_ref[...] + b3_ref[...] - t_ref[...]
        dy = err * (1.0 / D_MODEL)
        dr3 = _ln_bwd(dy, xhat3, rstd3, g3_ref[...])
        dgp = dr3 * pp * gate * (1.0 - gate)
        dgp_ref[...] = dgp.astype(MXU)
        dpp_ref[...] = (dr3 * gate).astype(MXU)
        dx2 = ALPHA * dr3 + _dot_nt(dgp, wg_ref[...])
        dr2_ref[...] = _ln_bwd(dx2, xhat2, rs2_ref[...], g2_ref[...])
        stat_ref[0:1, :] += _colsum(dy * xhat3)
        stat_ref[1:2, :] += _colsum(dy)
        stat_ref[2:3, :] += _colsum(dgp)
        stat_ref[3:4, :] += _colsum(dx2 * xhat2)
        stat_ref[4:5, :] += _colsum(dx2)
        stat_ref[5:6, :] += _colsum(err * err)

    tok = lambda w: pl.BlockSpec((TM, w), lambda i: (i, 0))
    vec = _full((1, D_MODEL))
    outs = [jax.ShapeDtypeStruct((t, D_MODEL), F32), jax.ShapeDtypeStruct((t, D_MODEL), MXU), jax.ShapeDtypeStruct((t, D_MODEL), MXU),
            jax.ShapeDtypeStruct((STAT_ROWS, D_MODEL), F32)]
    return pl.pallas_call(
        body, name="ple_loss_bwd", grid=(t // TM,),
        in_specs=[tok(D_MODEL), tok(1), tok(D_PLE), tok(D_MODEL), vec, vec, _full(w_g.shape), vec, _full(w_p.shape), vec, vec],
        out_specs=[tok(D_MODEL), tok(D_MODEL), tok(D_MODEL), _full((STAT_ROWS, D_MODEL))], out_shape=outs,
        scratch_shapes=[pltpu.VMEM((TM, D_MODEL), F32)],
        compiler_params=_cp(dimension_semantics=("arbitrary",)),
    )(xhat2, rstd2, p, target, ln2_g, ln2_b, w_g, b_g, w_p, ln3_g, ln3_b)


def _ffn_bwd(dr2, a_pre, b, w_down, w_a, w_b, conv_w, conv_b, xhat1, rstd1, ln1_g):
    t = dr2.shape[0]
    nt = t // TM
    hb = TM // HALO
    last_h = t // HALO - 1

    def body(dr_ref, drn_ref, ap_ref, app_ref, apn_ref, b_ref, bn_ref, wd_ref, wa_ref, wb_ref, cw_ref, cb_ref,
             xh_ref, rs_ref, g1_ref, dap_ref, dbb_ref, dr1_ref, cstat_ref, lstat_ref, acc_scr):
        i, j = pl.program_id(0), pl.program_id(1)

        @pl.when((i == 0) & (j == 0))
        def _():
            cstat_ref[...] = jnp.zeros_like(cstat_ref)
            lstat_ref[...] = jnp.zeros_like(lstat_ref)

        ext = TM + HALO
        dr_ext = jnp.concatenate([dr_ref[...], drn_ref[...]], axis=0)
        df = _dot_nt(dr_ext, wd_ref[...])
        a_all = jnp.concatenate([jnp.where(i > 0, app_ref[...], 0.0), ap_ref[...], apn_ref[...]], axis=0)
        a = _conv_fwd(a_all, cw_ref, cb_ref, ext)
        b_ext = jnp.concatenate([b_ref[...], bn_ref[...]], axis=0)
        row = lax.broadcasted_iota(jnp.int32, (ext, 1), 0)
        da = jnp.where((row < TM) | (i < nt - 1), df * b_ext * _gelu_grad(a), 0.0)
        dbb = df[0:TM] * _gelu(a[0:TM])
        da_pre = cw_ref[2:3, :] * da[0:TM] + cw_ref[1:2, :] * da[1:TM + 1] + cw_ref[0:1, :] * da[2:TM + 2]
        dap_ref[...] = da_pre.astype(MXU)
        dbb_ref[...] = dbb.astype(MXU)
        da_m = da[0:TM]
        for kk in range(3):
            cstat_ref[j, kk:kk + 1, :] += _colsum(da_m * a_all[HALO - 2 + kk:HALO - 2 + kk + TM])
        cstat_ref[j, 3:4, :] += _colsum(da_m)
        part = _dot(da_pre, wa_ref[...]) + _dot(dbb, wb_ref[...])

        @pl.when(j == 0)
        def _():
            acc_scr[...] = ALPHA * dr_ref[...] + part

        @pl.when(j > 0)
        def _():
            acc_scr[...] += part

        @pl.when(j == N_SHARD - 1)
        def _():
            dx1 = acc_scr[...]
            xhat1 = xh_ref[...]
            lstat_ref[0:1, :] += _colsum(dx1 * xhat1)
            lstat_ref[1:2, :] += _colsum(dx1)
            dr1_ref[...] = _ln_bwd(dx1, xhat1, rs_ref[...], g1_ref[...])

    tok = lambda w: pl.BlockSpec((TM, w), lambda i, j: (i, 0))
    tokj = pl.BlockSpec((None, TM, FF_BLK), lambda i, j: (j, i, 0))
    prevj = pl.BlockSpec((None, HALO, FF_BLK), lambda i, j: (j, jnp.maximum(i * hb - 1, 0), 0))
    nextj = pl.BlockSpec((None, HALO, FF_BLK), lambda i, j: (j, jnp.minimum((i + 1) * hb, last_h), 0))
    blk = lambda r, c: pl.BlockSpec((None, r, c), lambda i, j: (j, 0, 0))
    outs = [jax.ShapeDtypeStruct((N_SHARD, t, FF_BLK), MXU)] * 2 + [
        jax.ShapeDtypeStruct((t, D_MODEL), F32), jax.ShapeDtypeStruct((N_SHARD, STAT_ROWS, FF_BLK), F32),
        jax.ShapeDtypeStruct((STAT_ROWS, D_MODEL), F32)]
    return pl.pallas_call(
        body, name="ffn_bwd", grid=(nt, N_SHARD),
        in_specs=[tok(D_MODEL), pl.BlockSpec((HALO, D_MODEL), lambda i, j: (jnp.minimum((i + 1) * hb, last_h), 0)),
                  tokj, prevj, nextj, tokj, nextj, blk(FF_BLK, D_MODEL), blk(FF_BLK, D_MODEL), blk(FF_BLK, D_MODEL),
                  blk(3, FF_BLK), blk(1, FF_BLK), tok(D_MODEL), tok(1), _full((1, D_MODEL))],
        out_specs=[tokj, tokj, tok(D_MODEL), _full((N_SHARD, STAT_ROWS, FF_BLK)), _full((STAT_ROWS, D_MODEL))], out_shape=outs,
        scratch_shapes=[pltpu.VMEM((TM, D_MODEL), F32)],
        compiler_params=_cp(dimension_semantics=("arbitrary", "arbitrary")),
    )(dr2, dr2, a_pre, a_pre, a_pre, b, b, w_down, w_a, w_b, conv_w, conv_b, xhat1, rstd1, ln1_g)


def _mix_bwd(dr1, w_o, hu, hz, mixed, attn, ln_z_g, ln_z_b, w_s, dep):
    t = dr1.shape[0]
    nchunk = TM // BLK

    def body(dr_ref, wo_ref, hu_ref, hz_ref, mx_ref, attn_ref, g_ref, b_ref, ws_ref, grp_ref, red_ref, dep_ref,
             do1_ref, do4_ref, do16_ref, dl1_ref, dl4_ref, dl16_ref, duz_ref, dws_ref, dbs_ref, zstat_ref,
             wm_scr, dzn_scr, dbsum_scr, do_scr, dl_scr):
        @pl.when(pl.program_id(0) == 0)
        def _():
            row = lax.broadcasted_iota(jnp.int32, (BLK, BLK), 0)
            col = lax.broadcasted_iota(jnp.int32, (BLK, BLK), 1)
            for g in range(N_HEADS):
                wm_scr[g] = jnp.where(col <= row, ws_ref[g], 0.0).astype(MXU)
            dws_ref[...] = jnp.zeros_like(dws_ref)
            dbsum_scr[...] = jnp.zeros_like(dbsum_scr)
            zstat_ref[...] = jnp.zeros_like(zstat_ref)

        dcat = _dot_nt(dr_ref[...], wo_ref[...])
        dattn = dcat[:, 0:D_ATTN]
        do1_ref[...] = dattn.astype(MXU)
        for cc, val in enumerate(_chunks(dattn)):
            do_scr[cc] = val
        _to_planes(do4_ref, do_scr, DILATIONS[1], LANE_CHUNKS, MXU)
        _to_planes(do16_ref, do_scr, DILATIONS[2], LANE_CHUNKS, MXU)
        delta = _dot_exact(dattn * attn_ref[...], red_ref[...])
        dl1_ref[...] = delta
        dl_scr[0] = delta
        _to_planes(dl4_ref, dl_scr, DILATIONS[1], 1, F32)
        _to_planes(dl16_ref, dl_scr, DILATIONS[2], 1, F32)
        dgm = dcat[:, D_ATTN:]
        hu, hz = hu_ref[...], hz_ref[...]
        u = _gelu(hu)
        duz_ref[:, 0:D_GMLP] = (dgm * mx_ref[...] * _gelu_grad(hu)).astype(MXU)
        dmixed = dgm * u
        dmb = dmixed.astype(MXU)
        zhat, rstd = _ln_fwd(_gelu(hz))
        znb = (zhat * g_ref[...] + b_ref[...]).astype(MXU)
        dbs_acc = jnp.zeros((BLK, D_GMLP), F32)
        for ch in range(nchunk):
            rows = slice(ch * BLK, (ch + 1) * BLK)
            dbs_acc = dbs_acc + dmixed[rows]
            for g in range(N_HEADS):
                cols = slice(g * HEAD_DIM, (g + 1) * HEAD_DIM)
                dzn_scr[rows, cols] = _dot_tn(wm_scr[g], dmb[rows, cols])
                dws_ref[g] += _dot_nt(dmb[rows, cols], znb[rows, cols])
        dbsum_scr[...] += dbs_acc
        dzn = dzn_scr[...]
        zstat_ref[0:1, :] += _colsum(dzn * zhat)
        zstat_ref[1:2, :] += _colsum(dzn)
        duz_ref[:, D_GMLP:] = (_ln_bwd(dzn, zhat, rstd, g_ref[...]) * _gelu_grad(hz)).astype(MXU)

        @pl.when(pl.program_id(0) == nt - 1)
        def _():
            row = lax.broadcasted_iota(jnp.int32, (BLK, BLK), 0)
            col = lax.broadcasted_iota(jnp.int32, (BLK, BLK), 1)
            for g in range(N_HEADS):
                dws_ref[g] = jnp.where(col <= row, dws_ref[g], 0.0)
            dbs_ref[...] = lax.dot_general(grp_ref[...], dbsum_scr[...], (((1,), (1,)), ((), ())),
                                           precision=lax.Precision.HIGHEST, preferred_element_type=F32)

    nt = t // TM
    tok = lambda w: pl.BlockSpec((TM, w), lambda i: (i, 0))
    grp = jnp.asarray((np.arange(D_GMLP)[None, :] // HEAD_DIM == np.arange(N_HEADS)[:, None]).astype(np.float32))
    red = _head_reduce()
    outs = [_perm_shape(t, d, D_ATTN, MXU) for d in DILATIONS] + [_perm_shape(t, d, 128, F32) for d in DILATIONS] + [
        jax.ShapeDtypeStruct((t, 2 * D_GMLP), MXU),
        jax.ShapeDtypeStruct((N_HEADS, BLK, BLK), F32), jax.ShapeDtypeStruct((N_HEADS, BLK), F32),
        jax.ShapeDtypeStruct((STAT_ROWS, D_GMLP), F32)]
    return pl.pallas_call(
        body, name="mix_bwd", grid=(t // TM,),
        in_specs=[tok(D_MODEL), _full(w_o.shape), tok(D_GMLP), tok(D_GMLP), tok(D_GMLP), tok(D_ATTN), _full(ln_z_g.shape),
                  _full(ln_z_b.shape), _full(w_s.shape), _full(grp.shape), _full(red.shape), pl.BlockSpec(memory_space=pl.ANY)],
        out_specs=[_perm_tile(d, D_ATTN) for d in DILATIONS] + [_perm_tile(d, 128) for d in DILATIONS]
        + [tok(2 * D_GMLP), _full((N_HEADS, BLK, BLK)), _full((N_HEADS, BLK)), _full((STAT_ROWS, D_GMLP))],
        out_shape=outs,
        scratch_shapes=[pltpu.VMEM((N_HEADS, BLK, BLK), MXU), pltpu.VMEM((TM, D_GMLP), F32), pltpu.VMEM((BLK, D_GMLP), F32),
                        pltpu.VMEM((LANE_CHUNKS, TM, 128), F32), pltpu.VMEM((1, TM, 128), F32)],
        compiler_params=_cp(dimension_semantics=("arbitrary",)),
    )(dr1, w_o, hu, hz, mixed, attn, ln_z_g, ln_z_b, w_s, grp, red, dep)


def _dx_in(dqs, dks, dvs, duz, dr1, w_in, c_tab, s1_tab, s2_tab):
    t = dr1.shape[0]

    def body(dq1, dq4, dq16, dk1, dk4, dk16, dv1, dv4, dv16, duz_ref, dr_ref, w_ref, c_ref, s1_ref, s2_ref,
             dh_ref, dx_ref, acc_scr):
        sums = []
        for part, (g1, g4, g16) in enumerate(((dq1, dq4, dq16), (dk1, dk4, dk16), (dv1, dv4, dv16))):
            acc = acc_scr.at[pl.ds(part * LANE_CHUNKS, LANE_CHUNKS)]
            for cc in range(LANE_CHUNKS):
                acc[cc] = g1[:, cc * 128:(cc + 1) * 128]
            _from_planes(g4, acc, DILATIONS[1], LANE_CHUNKS, accumulate=True)
            _from_planes(g16, acc, DILATIONS[2], LANE_CHUNKS, accumulate=True)
            sums.append(_unchunk(acc_scr, LANE_CHUNKS, part * LANE_CHUNKS))
        c, s1, s2 = _tile_heads(c_ref[...]), _tile_heads(s1_ref[...]), _tile_heads(s2_ref[...])
        dh_ref[:, 0:D_ATTN] = _rope_apply_t(sums[0] * (1.0 / math.sqrt(HEAD_DIM)), c, s1, s2).astype(MXU)
        dh_ref[:, D_ATTN:2 * D_ATTN] = _rope_apply_t(sums[1], c, s1, s2).astype(MXU)
        dh_ref[:, 2 * D_ATTN:3 * D_ATTN] = sums[2].astype(MXU)
        dh_ref[:, 3 * D_ATTN:] = duz_ref[...]
        dx = ALPHA * dr_ref[...]
        for j in range(N_SHARD):
            dx = dx + _dot_nt(dh_ref[:, j * W_IN_BLK:(j + 1) * W_IN_BLK], w_ref[j])
        dx_ref[...] = dx

    tok = lambda w: pl.BlockSpec((TM, w), lambda i: (i, 0))
    outs = [jax.ShapeDtypeStruct((t, D_IN), MXU), jax.ShapeDtypeStruct((t, D_MODEL), F32)]
    return pl.pallas_call(
        body, name="dx_in", grid=(t // TM,),
        in_specs=[_perm_tile(d, D_ATTN) for d in DILATIONS] * 3
        + [tok(2 * D_GMLP), tok(D_MODEL), _full(w_in.shape), tok(128), tok(128), tok(128)],
        out_specs=[tok(D_IN), tok(D_MODEL)], out_shape=outs,
        scratch_shapes=[pltpu.VMEM((3 * LANE_CHUNKS, TM, 128), F32)],
        compiler_params=_cp(dimension_semantics=("arbitrary",)),
    )(*dqs, *dks, *dvs, duz, dr1, w_in, c_tab, s1_tab, s2_tab)


def _wgrad(name, x, dy, x_spec, dy_spec, out_spec, out_shape, grid):
    def body(x_ref, dy_ref, o_ref):
        o_ref[...] = _dot_tn(x_ref[...], dy_ref[...])

    return pl.pallas_call(
        body, name=name, grid=grid, in_specs=[x_spec, dy_spec], out_specs=out_spec,
        out_shape=jax.ShapeDtypeStruct(out_shape, F32),
        compiler_params=_cp(dimension_semantics=("arbitrary",) * len(grid)),
    )(x, dy)


def _local_step(x, p, positions, target, w_in, start_dep, late_weights, early_grads, early_grads_sent,
                ln_z_g, ln_z_b, w_s, b_s, ln1_g, ln1_b, conv_b, ln2_g, ln2_b, b_g, ln3_g, ln3_b):
    t = x.shape[0]
    half = TM
    c_tab, s1_tab, s2_tab = _rope_tables(positions, t)
    b_full = jnp.repeat(jnp.transpose(b_s[0]), HEAD_DIM, axis=1)
    conv_b4 = conv_b.reshape(N_SHARD, 1, FF_BLK)
    *qkvs, hu, hz, mixed, gm = _qkvuz(x, w_in, c_tab, s1_tab, s2_tab, ln_z_g, ln_z_b, w_s[0], b_full, start_dep)
    branches = [_attn_fwd(qkv, d) for qkv, d in zip(qkvs, DILATIONS)]
    w_o, w_a, w_b, conv_w, w_down, w_g, w_p = late_weights(branches[-1][1])
    attn, *lses, cat, xhat1, rstd1, x1b = _mix_ln1(
        [o for o, _ in branches], [l for _, l in branches], gm, x, w_o, ln1_g, ln1_b)
    a_pre, b_act, f = _ffn_in(x1b, w_a, w_b, conv_w, conv_b4)
    xhat2, rstd2, x2b = _ffn_out_ln2(f, w_down, xhat1, ln1_g, ln1_b, ln2_g, ln2_b)
    dr2, dgp, dpp, stat3 = _ple_loss_bwd(xhat2, rstd2, p, target, ln2_g, ln2_b, w_g, b_g, w_p, ln3_g, ln3_b)
    da_pre, dbb, dr1, cstat, stat1 = _ffn_bwd(dr2, a_pre, b_act, w_down, w_a, w_b, conv_w, conv_b4, xhat1, rstd1, ln1_g)

    full_t = lambda w, im: pl.BlockSpec((t, w), im)
    ffj = pl.BlockSpec((None, t, FF_BLK), lambda j, kk: (j, 0, 0))
    early = dict(
        w_ple_gate=_wgrad("dw_g", x2b, dgp, full_t(half, lambda kk, n: (0, kk)), full_t(half, lambda kk, n: (0, n)),
                          pl.BlockSpec((half, half), lambda kk, n: (kk, n)), (D_MODEL, D_MODEL), (2, 2)),
        w_ple_in=_wgrad("dw_p", p, dpp, full_t(D_PLE, lambda j: (0, 0)), full_t(ROW_BLK, lambda j: (0, j)),
                        pl.BlockSpec((None, D_PLE, ROW_BLK), lambda j: (j, 0, 0)), (N_SHARD, D_PLE, ROW_BLK), (N_SHARD,)),
        w_ff_down=_wgrad("dw_down", f, dr2, ffj, full_t(half, lambda j, n: (0, n)),
                         pl.BlockSpec((None, FF_BLK, half), lambda j, n: (j, 0, n)), (N_SHARD, FF_BLK, D_MODEL), (N_SHARD, 2)),
        w_ff_a=_wgrad("dw_a", da_pre, x1b, ffj, full_t(half, lambda j, n: (0, n)),
                      pl.BlockSpec((None, FF_BLK, half), lambda j, n: (j, 0, n)), (N_SHARD, FF_BLK, D_MODEL), (N_SHARD, 2)),
        w_ff_b=_wgrad("dw_b", dbb, x1b, ffj, full_t(half, lambda j, n: (0, n)),
                      pl.BlockSpec((None, FF_BLK, half), lambda j, n: (j, 0, n)), (N_SHARD, FF_BLK, D_MODEL), (N_SHARD, 2)),
        w_o=_wgrad("dw_o", cat, dr1, full_t(half, lambda kk, n: (0, kk)), full_t(half, lambda kk, n: (0, n)),
                   pl.BlockSpec((half, half), lambda kk, n: (kk, n)), (D_MODEL, D_MODEL), (2, 2)))
    dep = early_grads(early)

    do1, do4, do16, dl1, dl4, dl16, duz, dws, dbs, zstat = _mix_bwd(
        dr1, w_o, hu, hz, mixed, attn, ln_z_g, ln_z_b, w_s[0], dep)
    dep = early_grads_sent(duz)
    dqkv = [_attn_bwd(qkv, do, lse, dl, d, dep)
            for qkv, do, lse, dl, d in zip(qkvs, (do1, do4, do16), lses, (dl1, dl4, dl16), DILATIONS)]
    dh, grad_x = _dx_in([g[0] for g in dqkv], [g[1] for g in dqkv], [g[2] for g in dqkv], duz, dr1, w_in,
                        c_tab, s1_tab, s2_tab)
    g_w_in = _wgrad("dw_in", x, dh, full_t(half, lambda j, kk: (0, kk)), full_t(W_IN_BLK, lambda j, kk: (0, j)),
                    pl.BlockSpec((None, half, W_IN_BLK), lambda j, kk: (j, kk, 0)), (N_SHARD, D_MODEL, W_IN_BLK), (N_SHARD, 2))
    return grad_x, g_w_in, dws, dbs, (stat3, stat1, zstat, cstat)


def _rows_tile(r, mult, cap=512):
    return max(d for d in range(mult, min(r, cap) + 1, mult) if r % d == 0)


def _grid_spec(grid, in_specs, out_specs):
    return pltpu.PrefetchScalarGridSpec(num_scalar_prefetch=1, grid=grid, in_specs=in_specs, out_specs=out_specs)


def _place_shard(name, w, chip, dtype, dep):
    r, c = w.shape
    tr = r if r % 16 else _rows_tile(r, 16)

    def body(s_ref, w_ref, dep_ref, o_ref):
        o_ref[...] = w_ref[...].astype(dtype)

    return pl.pallas_call(
        body, name=name,
        grid_spec=_grid_spec((r // tr,), [pl.BlockSpec((tr, c), lambda i, s: (i, 0)), pl.BlockSpec(memory_space=pl.ANY)],
                             pl.BlockSpec((None, tr, c), lambda i, s: (s[0], i, 0))),
        out_shape=jax.ShapeDtypeStruct((N_SHARD, r, c), dtype), compiler_params=_cp())(chip, w, dep)


def _pair_sum_bf16(name, mine, got, core):
    n, h, c = got.shape
    tr = _rows_tile(h, 16)
    nh = h // tr

    def body(s_ref, a_ref, b_ref, o_ref):
        o_ref[...] = (a_ref[...] + b_ref[...]).astype(BF16)

    spec = pl.BlockSpec((None, tr, c), lambda k, i, s: (k, i, 0))
    return pl.pallas_call(
        body, name=name,
        grid_spec=_grid_spec((n, nh), [pl.BlockSpec((None, tr, c), lambda k, i, s: (k, s[0] * nh + i, 0)), spec], spec),
        out_shape=jax.ShapeDtypeStruct((n, h, c), BF16), compiler_params=_cp())(core, mine, got)


def _chip_sum(name, own, landed, place, dep):
    n, h, c = own.shape
    tr = _rows_tile(h, 16)
    nh = h // tr

    def body(s_ref, a_ref, b_ref, c_ref, d_ref, dep_ref, o_ref):
        o_ref[...] = ((a_ref[...].astype(F32) + b_ref[...].astype(F32)) + c_ref[...].astype(F32)) + d_ref[...].astype(F32)

    def slot(d):
        return pl.BlockSpec((None, tr, c), lambda i, s: ((s[0] + d) % n, i, 0))

    return pl.pallas_call(
        body, name=name,
        grid_spec=_grid_spec((nh,), [slot(0), slot(1), slot(2), slot(3), pl.BlockSpec(memory_space=pl.ANY)],
                             pl.BlockSpec((tr, c), lambda i, s: (s[1] * nh + i, 0))),
        out_shape=jax.ShapeDtypeStruct((2 * h, c), F32), compiler_params=_cp())(place, own, landed, landed, landed, dep)


def _adamw_math(w, g, m, v):
    m = ADAM_B1 * m + (1.0 - ADAM_B1) * g
    v = ADAM_B2 * v + (1.0 - ADAM_B2) * (g * g)
    m_hat = m / (1.0 - ADAM_B1 ** ADAM_STEP)
    v_hat = v / (1.0 - ADAM_B2 ** ADAM_STEP)
    delta = -ADAM_LR * (m_hat / (jnp.sqrt(v_hat) + ADAM_EPS) + ADAM_WD * w)
    return delta, m, v


def _adamw_big(name, w, g, m, v):
    _, r, c = w.shape
    tr = _rows_tile(r, 8, cap=256)

    def body(w_ref, g_ref, m_ref, v_ref, d_ref, nm_ref, nv_ref):
        d_ref[...], nm_ref[...], nv_ref[...] = _adamw_math(w_ref[...], g_ref[...], m_ref[...], v_ref[...])

    s3 = pl.BlockSpec((None, tr, c), lambda i: (0, i, 0))
    s2 = pl.BlockSpec((tr, c), lambda i: (i, 0))
    return pl.pallas_call(body, name=name, grid=(r // tr,), in_specs=[s3, s2, s3, s3], out_specs=[s3, s3, s3],
                          out_shape=[jax.ShapeDtypeStruct(w.shape, F32)] * 3, compiler_params=_cp())(w, g, m, v)


MESH = pl.DeviceIdType.MESH
ANY = pl.BlockSpec(memory_space=pl.ANY)


def _place():
    x, y, c = lax.axis_index("x"), lax.axis_index("y"), lax.axis_index("c")
    chips = [(1 - x, y), (x, 1 - y), (1 - x, 1 - y)]
    return x, y, c, 2 * x + y, chips


def _remote(src, dst, send_sem, recv_sem, dev):
    return pltpu.make_async_remote_copy(src_ref=src, dst_ref=dst, send_sem=send_sem, recv_sem=recv_sem,
                                        device_id=dev, device_id_type=MESH)


def _half(ref, hc, rows):
    return ref.at[pl.ds(hc * (rows // 2), rows // 2)]


def _sibling_join(blocks, tag):
    n = len(blocks)

    def body(*refs):
        outs = refs[n:2 * n]
        send, recv = refs[2 * n:]
        x, y, c, _, _ = _place()
        cps = []
        for a in range(n):
            h = blocks[a].shape[0] // 2
            mine = outs[a].at[pl.ds(c * h, h)]
            cp = _remote(mine, mine, send.at[a], recv.at[a], (x, y, 1 - c))
            cp.start()
            cps.append(cp)
        for a, cp in enumerate(cps):
            h = blocks[a].shape[0] // 2
            theirs = outs[a].at[pl.ds((1 - c) * h, h)]
            _remote(theirs, theirs, send.at[a], recv.at[a], (x, y, 1 - c)).wait_recv()
            cp.wait_send()

    sem = pltpu.SemaphoreType.DMA
    return pl.pallas_call(body, name=f"rs_sibling_join_{tag}", in_specs=[ANY] * n, out_specs=[ANY] * n,
                          out_shape=[jax.ShapeDtypeStruct(b_.shape, b_.dtype) for b_ in blocks],
                          input_output_aliases={a: a for a in range(n)},
                          scratch_shapes=[sem((n,)), sem((n,))])(*blocks)


HBM = pl.BlockSpec(memory_space=pltpu.HBM)
SEM = pl.BlockSpec(memory_space=pltpu.SEMAPHORE)
TOKEN = jax.ShapeDtypeStruct((8, 128), F32)


def _in_flight_params():
    return pltpu.CompilerParams(has_side_effects=pltpu.SideEffectType.DATAFLOW_SIDE_EFFECTING)


def _in_hbm(a):
    return pltpu.with_memory_space_constraint(a, pltpu.HBM)


def _gather_piece(ref, rows, split, slot, hc):
    return _half(ref.at[slot], hc, rows) if split else ref.at[slot]


def _gather_start(stacks, split, after, tag):
    n = len(stacks)

    def body(*refs):
        ins = refs[:n]
        send, recv = refs[n + 1], refs[n + 2]
        token = refs[2 * n + 3]
        _, _, c, j, chips = _place()
        for a in range(n):
            mine = _gather_piece(ins[a], stacks[a].shape[1], split[a], j, c)
            for t in range(3):
                _remote(mine, mine, send.at[3 * a + t], recv.at[3 * a + t], (*chips[t], c)).start()
        token[...] = jnp.zeros_like(token)

    sems = pltpu.SemaphoreType.DMA((3 * n,))
    res = pl.pallas_call(
        body, name=f"gather_start_{tag}", in_specs=[HBM] * n + [ANY],
        out_specs=[SEM, SEM] + [HBM] * n + [pl.BlockSpec(memory_space=pltpu.VMEM)],
        out_shape=[sems, sems] + [pltpu.HBM(s.shape, s.dtype) for s in stacks] + [TOKEN],
        input_output_aliases={a: a + 2 for a in range(n)}, compiler_params=_in_flight_params(),
    )(*[_in_hbm(s) for s in stacks], after)
    return res[0], res[1], res[2:2 + n], res[2 + n]


def _gather_wait(send, recv, stacks, split, after, tag):
    n = len(stacks)

    def body(*refs):
        ins = refs[:n]
        send_ref, recv_ref = refs[n], refs[n + 1]
        _, _, c, j, chips = _place()
        for a in range(n):
            rows = stacks[a].shape[1]
            mine = _gather_piece(ins[a], rows, split[a], j, c)
            for t, (px, py) in enumerate(chips):
                theirs = _gather_piece(ins[a], rows, split[a], 2 * px + py, c)
                _remote(mine, mine, send_ref.at[3 * a + t], recv_ref.at[3 * a + t], (px, py, c)).wait_send()
                _remote(theirs, theirs, send_ref.at[3 * a + t], recv_ref.at[3 * a + t], (px, py, c)).wait_recv()

    return pl.pallas_call(
        body, name=f"gather_wait_{tag}", in_specs=[HBM] * n + [SEM, SEM, ANY], out_specs=[HBM] * n,
        out_shape=[pltpu.HBM(s.shape, s.dtype) for s in stacks],
        input_output_aliases={a: a for a in range(n)}, compiler_params=_in_flight_params(),
    )(*stacks, send, recv, after)


def _gather_forward(stacks, split, tag):
    idx = [a for a in range(len(stacks)) if split[a]]
    n = len(idx)

    def body(*refs):
        outs = refs[n:2 * n]
        send, recv = refs[2 * n:]
        x, y, c, _, chips = _place()
        sends = []
        for t, (px, py) in enumerate(chips):
            for a in range(n):
                blk = _half(outs[a].at[2 * px + py], c, stacks[idx[a]].shape[1])
                cp = _remote(blk, blk, send.at[a, t], recv.at[a, t], (x, y, 1 - c))
                cp.start()
                sends.append(cp)
        for t, (px, py) in enumerate(chips):
            for a in range(n):
                blk = _half(outs[a].at[2 * px + py], 1 - c, stacks[idx[a]].shape[1])
                _remote(blk, blk, send.at[a, t], recv.at[a, t], (x, y, 1 - c)).wait_recv()
        for cp in sends:
            cp.wait_send()

    sem = pltpu.SemaphoreType.DMA
    res = pl.pallas_call(
        body, name=f"gather_forward_{tag}", in_specs=[ANY] * n, out_specs=[ANY] * n,
        out_shape=[jax.ShapeDtypeStruct(stacks[a].shape, stacks[a].dtype) for a in idx],
        input_output_aliases={a: a for a in range(n)}, scratch_shapes=[sem((n, 3)), sem((n, 3))],
    )(*[stacks[a] for a in idx])
    out = list(stacks)
    for a, r in zip(idx, res):
        out[a] = r
    return out


def _swap_start(grads, tag):
    n = len(grads)

    def body(*refs):
        ins, gots = refs[:n], refs[n:2 * n]
        send, recv = refs[2 * n], refs[2 * n + 1]
        token = refs[4 * n + 2]
        x, y, c, _, _ = _place()
        for a in range(n):
            h = grads[a].shape[1] // 2
            _remote(ins[a].at[:, pl.ds((1 - c) * h, h)], gots[a], send.at[a], recv.at[a], (x, y, 1 - c)).start()
        token[...] = jnp.zeros_like(token)

    sems = pltpu.SemaphoreType.DMA((n,))
    halves = [(g.shape[0], g.shape[1] // 2, g.shape[2]) for g in grads]
    res = pl.pallas_call(
        body, name=f"swap_start_{tag}", in_specs=[HBM] * (2 * n),
        out_specs=[SEM, SEM] + [HBM] * (2 * n) + [pl.BlockSpec(memory_space=pltpu.VMEM)],
        out_shape=[sems, sems] + [pltpu.HBM(g.shape, g.dtype) for g in grads] + [pltpu.HBM(s, F32) for s in halves] + [TOKEN],
        input_output_aliases={a: a + 2 for a in range(2 * n)}, compiler_params=_in_flight_params(),
    )(*[_in_hbm(g) for g in grads], *[_in_hbm(lax.empty(s, F32)) for s in halves])
    return res[0], res[1], res[2:2 + n], res[2 + n:2 + 2 * n], res[2 + 2 * n]


def _swap_wait(send, recv, grads, gots, after, tag):
    n = len(grads)

    def body(*refs):
        ins, lnd = refs[:n], refs[n:2 * n]
        send_ref, recv_ref = refs[2 * n], refs[2 * n + 1]
        x, y, c, _, _ = _place()
        for a in range(n):
            h = grads[a].shape[1] // 2
            cp = _remote(ins[a].at[:, pl.ds((1 - c) * h, h)], lnd[a], send_ref.at[a], recv_ref.at[a], (x, y, 1 - c))
            cp.wait_send()
            cp.wait_recv()

    bufs = [pltpu.HBM(g.shape, g.dtype) for g in grads] + [pltpu.HBM(g.shape, g.dtype) for g in gots]
    res = pl.pallas_call(
        body, name=f"swap_wait_{tag}", in_specs=[HBM] * (2 * n) + [SEM, SEM, ANY], out_specs=[HBM] * (2 * n),
        out_shape=bufs, input_output_aliases={a: a for a in range(2 * n)}, compiler_params=_in_flight_params(),
    )(*grads, *gots, send, recv, after)
    return res[:n], res[n:]


def _exchange_start(parts, tag):
    n = len(parts)

    def body(*refs):
        ins, lands = refs[:n], refs[n:2 * n]
        send, recv = refs[2 * n], refs[2 * n + 1]
        token = refs[4 * n + 2]
        _, _, c, j, chips = _place()
        for t, (px, py) in enumerate(chips):
            for a in range(n):
                _remote(ins[a].at[2 * px + py], lands[a].at[j], send.at[3 * a + t], recv.at[3 * a + t], (px, py, c)).start()
        token[...] = jnp.zeros_like(token)

    sems = pltpu.SemaphoreType.DMA((3 * n,))
    bufs = [pltpu.HBM(p.shape, p.dtype) for p in parts]
    res = pl.pallas_call(
        body, name=f"exchange_start_{tag}", in_specs=[HBM] * (2 * n),
        out_specs=[SEM, SEM] + [HBM] * (2 * n) + [pl.BlockSpec(memory_space=pltpu.VMEM)],
        out_shape=[sems, sems] + bufs + bufs + [TOKEN],
        input_output_aliases={a: a + 2 for a in range(2 * n)}, compiler_params=_in_flight_params(),
    )(*[_in_hbm(p) for p in parts], *[_in_hbm(lax.empty(p.shape, p.dtype)) for p in parts])
    return res[0], res[1], res[2:2 + n], res[2 + n:2 + 2 * n], res[2 + 2 * n]


def _exchange_wait(send, recv, parts, lands, after, tag):
    n = len(parts)

    def body(*refs):
        ins, lnd = refs[:n], refs[n:2 * n]
        send_ref, recv_ref = refs[2 * n], refs[2 * n + 1]
        _, _, c, j, chips = _place()
        for t, (px, py) in enumerate(chips):
            jt = 2 * px + py
            for a in range(n):
                _remote(ins[a].at[jt], lnd[a].at[j], send_ref.at[3 * a + t], recv_ref.at[3 * a + t], (px, py, c)).wait_send()
                _remote(ins[a].at[jt], lnd[a].at[jt], send_ref.at[3 * a + t], recv_ref.at[3 * a + t], (px, py, c)).wait_recv()

    bufs = [pltpu.HBM(p.shape, p.dtype) for p in parts]
    res = pl.pallas_call(
        body, name=f"exchange_wait_{tag}", in_specs=[HBM] * (2 * n) + [SEM, SEM, ANY], out_specs=[HBM] * (2 * n),
        out_shape=bufs + bufs, input_output_aliases={a: a for a in range(2 * n)}, compiler_params=_in_flight_params(),
    )(*parts, *lands, send, recv, after)
    return res[:n], res[n:]


def _allreduce_small(arrs):
    n = len(arrs)

    def body(*refs):
        ins, outs = refs[:n], refs[n:2 * n]
        sib, chip = refs[2 * n:3 * n], refs[3 * n:4 * n]
        ssend, srecv, csend, crecv = refs[4 * n:]
        x, y, c, j, chips = _place()
        swaps = [_remote(ins[a], sib[a], ssend.at[a], srecv.at[a], (x, y, 1 - c)) for a in range(n)]
        for cp in swaps:
            cp.start()
        sends = []
        for a in range(n):
            swaps[a].wait_recv()
            chip[a][j] = ins[a][...] + sib[a][...]
            for t, (px, py) in enumerate(chips):
                cp = _remote(chip[a].at[j], chip[a].at[j], csend.at[a, t], crecv.at[a, t], (px, py, c))
                cp.start()
                sends.append(cp)
        for a in range(n):
            for t, (px, py) in enumerate(chips):
                jt = 2 * px + py
                _remote(chip[a].at[jt], chip[a].at[jt], csend.at[a, t], crecv.at[a, t], (px, py, c)).wait_recv()
            outs[a][...] = ((chip[a][0] + chip[a][1]) + chip[a][2]) + chip[a][3]
        for cp in swaps + sends:
            cp.wait_send()

    sem = pltpu.SemaphoreType.DMA
    vm = pl.BlockSpec(memory_space=pltpu.VMEM)
    return pl.pallas_call(
        body, name="allreduce_small", in_specs=[vm] * n, out_specs=[vm] * n,
        out_shape=[jax.ShapeDtypeStruct(a.shape, F32) for a in arrs],
        scratch_shapes=[pltpu.VMEM(a.shape, F32) for a in arrs] + [pltpu.VMEM((N_SHARD, *a.shape), F32) for a in arrs]
        + [sem((n,)), sem((n,)), sem((n, 3)), sem((n, 3))],
        compiler_params=_cp(),
    )(*arrs)


SMALL_1024 = ("ln1_g", "ln1_b", "ln2_g", "ln2_b", "b_ple_gate", "ln3_g", "ln3_b")


def _adamw_small(red3, red1, redz, g_conv_w, redc, red_ws, red_bs, params):
    shape2d = {"ln_z_g": (1, D_GMLP), "ln_z_b": (1, D_GMLP), "w_s": (N_HEADS * BLK, BLK), "b_s": (N_HEADS, BLK),
               "conv_w": (3, FF_BLK), "conv_b": (N_SHARD, FF_BLK), **{k: (1, D_MODEL) for k in SMALL_1024}}
    names = list(shape2d)
    flat = [a.reshape(shape2d[k]) for k in names for a in params[k]]

    def body(r3, r1, rz, gcw, rc, rws, rbs, *refs):
        ins, outs = refs[:3 * len(names)], refs[3 * len(names):]

        def grad_of(k):
            if k == "w_s":
                return rws[...]
            if k == "b_s":
                return rbs[...]
            if k == "conv_w":
                return gcw[0:3, :]
            if k == "conv_b":
                return jnp.concatenate([rc[j * STAT_ROWS + 3:j * STAT_ROWS + 4, :] for j in range(N_SHARD)], axis=0)
            src, row = {"ln3_g": (r3, 0), "ln3_b": (r3, 1), "b_ple_gate": (r3, 2), "ln2_g": (r3, 3), "ln2_b": (r3, 4),
                        "ln1_g": (r1, 0), "ln1_b": (r1, 1), "ln_z_g": (rz, 0), "ln_z_b": (rz, 1)}[k]
            return src[row:row + 1, :]

        for i, k in enumerate(names):
            w_ref, m_ref, v_ref = ins[3 * i:3 * i + 3]
            g_ref, d_ref, nm_ref, nv_ref = outs[4 * i:4 * i + 4]
            g = grad_of(k)
            g_ref[...] = g
            d_ref[...], nm_ref[...], nv_ref[...] = _adamw_math(w_ref[...], g, m_ref[...], v_ref[...])

    res = pl.pallas_call(
        body, name="adamw_small",
        out_shape=[jax.ShapeDtypeStruct(shape2d[k], F32) for k in names for _ in range(4)],
        compiler_params=_cp(),
    )(red3, red1, redz, g_conv_w, redc, red_ws, red_bs, *flat)
    return {k: tuple(r.reshape(params[k][0].shape) for r in res[4 * i:4 * i + 4]) for i, k in enumerate(names)}


WEIGHTS = ("w_in", "ln_z_g", "ln_z_b", "w_s", "b_s", "w_o", "ln1_g", "ln1_b", "w_ff_a", "w_ff_b", "conv_w", "conv_b",
           "w_ff_down", "ln2_g", "ln2_b", "w_ple_gate", "b_ple_gate", "w_ple_in", "ln3_g", "ln3_b")
BIG = ("w_in", "w_o", "w_ff_a", "w_ff_b", "w_ff_down", "w_ple_gate", "w_ple_in")
TRANSPOSED = ("w_ff_a", "w_ff_b")
LATE = ("w_o", "w_ff_a", "w_ff_b", "w_ff_down", "w_ple_gate", "w_ple_in", "conv_w")


def kernel(x, p, positions, w_in, ln_z_g, ln_z_b, w_s, b_s, w_o, ln1_g, ln1_b, w_ff_a, w_ff_b, conv_w, conv_b, w_ff_down, ln2_g, ln2_b, w_ple_gate, b_ple_gate, w_ple_in, ln3_g, ln3_b, loss_target, m_w_in, m_ln_z_g, m_ln_z_b, m_w_s, m_b_s, m_w_o, m_ln1_g, m_ln1_b, m_w_ff_a, m_w_ff_b, m_conv_w, m_conv_b, m_w_ff_down, m_ln2_g, m_ln2_b, m_w_ple_gate, m_b_ple_gate, m_w_ple_in, m_ln3_g, m_ln3_b, v_w_in, v_ln_z_g, v_ln_z_b, v_w_s, v_b_s, v_w_o, v_ln1_g, v_ln1_b, v_w_ff_a, v_w_ff_b, v_conv_w, v_conv_b, v_w_ff_down, v_ln2_g, v_ln2_b, v_w_ple_gate, v_b_ple_gate, v_w_ple_in, v_ln3_g, v_ln3_b):
    args = locals()
    w = {k: args[k] for k in WEIGHTS}
    m = {k: args["m_" + k] for k in WEIGHTS}
    v = {k: args["v_" + k] for k in WEIGHTS}

    for k in TRANSPOSED:
        w[k], m[k], v[k] = (jnp.swapaxes(a, 1, 2) for a in (w[k], m[k], v[k]))

    chip = 2 * lax.axis_index("x") + lax.axis_index("y")
    place = jnp.stack([chip, lax.axis_index("c")]).astype(jnp.int32)
    stack = {"w_in": _place_shard("cast_w_in", w["w_in"][0], place, MXU, place)}
    i_send, i_recv, in_flight, dep = _gather_start([stack["w_in"]], [True], place, "w_in")
    for k in LATE:
        stack[k] = _place_shard(f"cast_{k}", w[k][0], place, F32 if k == "conv_w" else MXU, dep)
    landed_in = _gather_wait(i_send, i_recv, in_flight, [True], stack[LATE[-1]], "w_in")
    w_in_full, = _gather_forward(landed_in, [True], "w_in")
    split_late = [k != "conv_w" for k in LATE]
    g_send, g_recv, late_flight, start_dep = _gather_start([stack[k] for k in LATE], split_late, w_in_full, "late")

    def late_weights(after):
        landed = _gather_wait(g_send, g_recv, late_flight, split_late, after, "late")
        fw = dict(zip(LATE, _gather_forward(landed, split_late, "late")))
        return (fw["w_o"].reshape(D_MODEL, D_MODEL), fw["w_ff_a"], fw["w_ff_b"], fw["conv_w"], fw["w_ff_down"],
                fw["w_ple_gate"].reshape(D_MODEL, D_MODEL), fw["w_ple_in"])

    def swap_started(names, grads, tag):
        stacked = [g.reshape(N_SHARD, *w[k].shape[1:]) for k, g in zip(names, grads)]
        return (names, tag, *_swap_start(stacked, tag))

    def partial_sums(swap, after):
        names, tag, send, recv, stacked, gots, _ = swap
        stacked, got = _swap_wait(send, recv, stacked, gots, after, tag)
        pair = [_pair_sum_bf16(f"rs_pair_{k}", s, g, place[1:2]) for k, s, g in zip(names, stacked, got)]
        return (names, tag, *_exchange_start(pair, tag))

    def reduced(trip, after, dep):
        names, tag, send, recv, pair, lands, _ = trip
        pair, landed = _exchange_wait(send, recv, pair, lands, after, tag)
        blocks = [_chip_sum(f"rs_sum_{k}", own, l, place, dep) for k, own, l in zip(names, pair, landed)]
        return dict(zip(names, _sibling_join(blocks, tag)))

    trips = {}

    def early_grads(grads):
        trips["swap"] = swap_started(list(grads), list(grads.values()), "early")
        return trips["swap"][-1]

    def early_grads_sent(after):
        trips["early"] = partial_sums(trips["swap"], after)
        return trips["early"][-1]

    grad_x, g_w_in, dws, dbs, (stat3, stat1, zstat, cstat) = _local_step(
        x[0], p[0, 0], positions, loss_target[0], w_in_full, start_dep, late_weights, early_grads, early_grads_sent,
        ln_z_g, ln_z_b, w_s, b_s, ln1_g, ln1_b, conv_b, ln2_g, ln2_b, b_ple_gate, ln3_g, ln3_b)

    trips["w_in"] = partial_sums(swap_started(["w_in"], [g_w_in], "w_in"), g_w_in)
    out = {}

    def adamw(red):
        for k, g in red.items():
            d, nm, nv = _adamw_big(f"adamw_{k}", w[k], g, m[k], v[k])
            out[k] = (g.reshape(w[k].shape), d, nm, nv)

    adamw(reduced(trips["early"], grad_x, trips["w_in"][-1]))
    adamw(reduced(trips["w_in"], out["w_o"][3], start_dep))
    for k in TRANSPOSED:
        out[k] = tuple(jnp.swapaxes(a, 1, 2) for a in out[k])

    red3, red1, redz, redc, red_ws, red_bs = _allreduce_small(
        [stat3, stat1, zstat, cstat.reshape(N_SHARD * STAT_ROWS, FF_BLK), dws.reshape(N_HEADS * BLK, BLK), dbs])
    loss = (0.5 / D_MODEL) * jnp.sum(red3[5])
    g_conv_w = lax.dynamic_slice_in_dim(redc, chip * STAT_ROWS, STAT_ROWS, 0)
    names_small = [k for k in WEIGHTS if k not in BIG]
    out.update(_adamw_small(red3, red1, redz, g_conv_w, redc, red_ws, red_bs, {k: (w[k], m[k], v[k]) for k in names_small}))

    return (loss, grad_x[None], *[out[k][0] for k in WEIGHTS], *[out[k][1] for k in WEIGHTS],
            *[out[k][2] for k in WEIGHTS], *[out[k][3] for k in WEIGHTS])
```

```python
import functools
import math

import numpy as np
import jax
import jax.numpy as jnp
from jax import lax
from jax.experimental import pallas as pl
from jax.experimental.pallas import tpu as pltpu

F32 = jnp.float32
BF16 = jnp.bfloat16
MXU = BF16

D_MODEL = 1024
HEAD_DIM = 64
N_HEADS = 8
D_ATTN = 512
D_GMLP = 512
D_IN = 2560
DILATIONS = (1, 4, 16)
BLK = 128
ROPE_THETA = 500000.0
ROPE_DIM = 16
D_FF = 2816
D_PLE = 256
LN_EPS = 1e-5
ALPHA = 2.0 ** 0.25
NEG_INF = -1e30
N_SHARD = 4
W_IN_BLK = D_IN // N_SHARD
FF_BLK = D_FF // N_SHARD
ROW_BLK = D_MODEL // N_SHARD
ADAM_LR, ADAM_B1, ADAM_B2, ADAM_EPS, ADAM_WD, ADAM_STEP = 0.001, 0.9, 0.999, 1e-08, 0.01, 10

TM = 512
HALO = 8
VMEM_LIMIT = 56 * 1024 * 1024


def _cp(**kw):
    return pltpu.CompilerParams(vmem_limit_bytes=VMEM_LIMIT, **kw)


def _full(shape):
    n = len(shape)
    return pl.BlockSpec(shape, lambda *_: (0,) * n)


def _gelu(x):
    return 0.5 * x * (1.0 + lax.erf(x * (1.0 / math.sqrt(2.0))))


def _gelu_grad(x):
    return 0.5 * (1.0 + lax.erf(x * (1.0 / math.sqrt(2.0)))) + x * jnp.exp(-0.5 * x * x) * (1.0 / math.sqrt(2.0 * math.pi))


def _ln_fwd(r):
    mu = jnp.mean(r, axis=-1, keepdims=True)
    xc = r - mu
    var = jnp.mean(xc * xc, axis=-1, keepdims=True)
    rstd = lax.rsqrt(var + LN_EPS)
    return xc * rstd, rstd


def _ln_bwd(dy, xhat, rstd, g):
    dxh = dy * g
    m1 = jnp.mean(dxh, axis=-1, keepdims=True)
    m2 = jnp.mean(dxh * xhat, axis=-1, keepdims=True)
    return rstd * (dxh - m1 - xhat * m2)


def _dot(a, b):
    return jnp.dot(a.astype(MXU), b.astype(MXU), preferred_element_type=F32)


def _dot_nt(a, b):
    return lax.dot_general(a.astype(MXU), b.astype(MXU), (((1,), (1,)), ((), ())), preferred_element_type=F32)


def _dot_tn(a, b):
    return lax.dot_general(a.astype(MXU), b.astype(MXU), (((0,), (0,)), ((), ())), preferred_element_type=F32)


def _colsum(v):
    return jnp.sum(v, axis=0, keepdims=True)


def _rope_tables(positions, t, dep):
    inv = np.float32(ROPE_THETA) ** (-np.arange(0, ROPE_DIM, 2, dtype=np.float32) / np.float32(ROPE_DIM))
    half = ROPE_DIM // 2
    pos_rep = jnp.repeat(positions.reshape(t // 16, 16), half, axis=1)
    inv_row = jnp.asarray(np.tile(inv, 16)[None, :], F32)

    def trig_body(pos_ref, inv_ref, dep_ref, cos_ref, sin_ref):
        ang = pos_ref[...].astype(F32) * inv_ref[...]
        cos_ref[...] = jnp.cos(ang)
        sin_ref[...] = jnp.sin(ang)

    vm = pl.BlockSpec(memory_space=pltpu.VMEM)
    cos8, sin8 = pl.pallas_call(
        trig_body, name="rope_trig", in_specs=[vm, vm, pl.BlockSpec(memory_space=pl.ANY)], out_specs=[vm, vm],
        out_shape=(jax.ShapeDtypeStruct((t // 16, 128), F32), jax.ShapeDtypeStruct((t // 16, 128), F32)),
    )(pos_rep, inv_row, dep)
    cos8 = cos8.reshape(t, half)
    sin8 = sin8.reshape(t, half)

    lane = np.arange(128) % HEAD_DIM
    sel = (np.arange(half)[:, None] == (lane % half)[None, :])
    e_cos = (sel & (lane < ROPE_DIM)[None, :]).astype(np.float32)
    e_s1 = -(sel & (lane < half)[None, :]).astype(np.float32)
    e_s2 = (sel & ((lane >= half) & (lane < ROPE_DIM))[None, :]).astype(np.float32)
    ones = (lane >= ROPE_DIM).astype(np.float32)[None, :]

    def expand_body(cos_ref, sin_ref, ec_ref, e1_ref, e2_ref, ones_ref, c_ref, s1_ref, s2_ref):
        hp = lax.Precision.HIGHEST
        c_ref[...] = jnp.dot(cos_ref[...], ec_ref[...], precision=hp, preferred_element_type=F32) + ones_ref[...]
        s1_ref[...] = jnp.dot(sin_ref[...], e1_ref[...], precision=hp, preferred_element_type=F32)
        s2_ref[...] = jnp.dot(sin_ref[...], e2_ref[...], precision=hp, preferred_element_type=F32)

    tab = jax.ShapeDtypeStruct((t, 128), F32)
    return pl.pallas_call(expand_body, name="rope_expand", out_shape=(tab, tab, tab), compiler_params=_cp())(
        cos8, sin8, jnp.asarray(e_cos), jnp.asarray(e_s1), jnp.asarray(e_s2), jnp.asarray(ones))


def _tile_heads(tab):
    return jnp.concatenate([tab] * (D_ATTN // 128), axis=1)


def _rope_apply(v, c, s1, s2):
    n = v.shape[1]
    half = ROPE_DIM // 2
    return v * c + pltpu.roll(v, n - half, 1) * s1 + pltpu.roll(v, half, 1) * s2


def _rope_apply_t(g, c, s1, s2):
    n = g.shape[1]
    half = ROPE_DIM // 2
    return g * c + pltpu.roll(g * s1, half, 1) + pltpu.roll(g * s2, n - half, 1)


LANE_CHUNKS = D_ATTN // 128
HEAD_LANES = 128 // N_HEADS


def _perm_shape(t, d, w, dtype):
    return jax.ShapeDtypeStruct((d, t // d, w), dtype)


def _perm_tile(d, w):
    return pl.BlockSpec((None if d == 1 else d, TM // d, w), lambda i: (0, i, 0))


def _to_planes(ref, scr, d, n_chunks, dtype):
    for r in range(d):
        for cc in range(n_chunks):
            ref[r, :, cc * 128:(cc + 1) * 128] = scr.at[cc][pl.ds(r, TM // d, stride=d), :].astype(dtype)


def _from_planes(ref, scr, d, n_chunks, accumulate=False):
    for r in range(d):
        for cc in range(n_chunks):
            rows = scr.at[cc]
            val = ref[r, :, cc * 128:(cc + 1) * 128].astype(F32)
            if accumulate:
                rows[pl.ds(r, TM // d, stride=d), :] += val
            else:
                rows[pl.ds(r, TM // d, stride=d), :] = val


def _chunks(val):
    return [val[:, cc * 128:(cc + 1) * 128] for cc in range(val.shape[1] // 128)]


def _unchunk(scr, n_chunks, base=0):
    return jnp.concatenate([scr[base + cc] for cc in range(n_chunks)], axis=1)


def _head_expand():
    src = np.arange(128)[:, None]
    dst = np.arange(D_ATTN)[None, :]
    return jnp.asarray((src == (dst // HEAD_DIM) * HEAD_LANES).astype(np.float32))


def _head_reduce():
    src = np.arange(D_ATTN)[:, None]
    dst = np.arange(128)[None, :]
    return jnp.asarray((src // HEAD_DIM == dst // HEAD_LANES).astype(np.float32))


def _dot_exact(a, b):
    return jnp.dot(a, b, precision=lax.Precision.HIGHEST, preferred_element_type=F32)


def _qkvuz(x, w_in, c_tab, s1_tab, s2_tab, ln_z_g, ln_z_b, w_s, b_full, dep):
    t = x.shape[0]
    nchunk = TM // BLK

    def body(x_ref, w_ref, c_ref, s1_ref, s2_ref, g_ref, b_ref, ws_ref, bf_ref, dep_ref,
             qkv1_ref, qkv4_ref, qkv16_ref, hu_ref, hz_ref, mixed_ref, gm_ref, h_scr, wm_scr, p_scr):
        @pl.when(pl.program_id(0) == 0)
        def _():
            row = lax.broadcasted_iota(jnp.int32, (BLK, BLK), 0)
            col = lax.broadcasted_iota(jnp.int32, (BLK, BLK), 1)
            for g in range(N_HEADS):
                wm_scr[g] = jnp.where(col <= row, ws_ref[g], 0.0).astype(MXU)

        xb = x_ref[...].astype(MXU)
        for j in range(N_SHARD):
            h_scr[:, j * W_IN_BLK:(j + 1) * W_IN_BLK] = jnp.dot(xb, w_ref[j], preferred_element_type=F32)
        c, s1, s2 = _tile_heads(c_ref[...]), _tile_heads(s1_ref[...]), _tile_heads(s2_ref[...])
        q = _rope_apply(h_scr[:, 0:D_ATTN], c, s1, s2) * (1.0 / math.sqrt(HEAD_DIM))
        k = _rope_apply(h_scr[:, D_ATTN:2 * D_ATTN], c, s1, s2)
        for part, val in enumerate((q, k, h_scr[:, 2 * D_ATTN:3 * D_ATTN])):
            qkv1_ref[:, part * D_ATTN:(part + 1) * D_ATTN] = val.astype(MXU)
            for cc in range(LANE_CHUNKS):
                p_scr[part * LANE_CHUNKS + cc] = val[:, cc * 128:(cc + 1) * 128]
        _to_planes(qkv4_ref, p_scr, DILATIONS[1], 3 * LANE_CHUNKS, MXU)
        _to_planes(qkv16_ref, p_scr, DILATIONS[2], 3 * LANE_CHUNKS, MXU)
        hu = h_scr[:, 3 * D_ATTN:3 * D_ATTN + D_GMLP]
        hz = h_scr[:, 3 * D_ATTN + D_GMLP:]
        hu_ref[...] = hu
        hz_ref[...] = hz
        zhat, _ = _ln_fwd(_gelu(hz))
        zn = (zhat * g_ref[...] + b_ref[...]).astype(MXU)
        for ch in range(nchunk):
            rows = slice(ch * BLK, (ch + 1) * BLK)
            for g in range(N_HEADS):
                cols = slice(g * HEAD_DIM, (g + 1) * HEAD_DIM)
                mixed_ref[rows, cols] = jnp.dot(wm_scr[g], zn[rows, cols], preferred_element_type=F32) + bf_ref[:, cols]
        gm_ref[...] = (_gelu(hu) * mixed_ref[...]).astype(MXU)

    tok = lambda w: pl.BlockSpec((TM, w), lambda i: (i, 0))
    outs = [_perm_shape(t, d, 3 * D_ATTN, MXU) for d in DILATIONS] + [jax.ShapeDtypeStruct((t, D_GMLP), F32)] * 3 + [
        jax.ShapeDtypeStruct((t, D_GMLP), MXU)]
    return pl.pallas_call(
        body, name="qkvuz", grid=(t // TM,),
        in_specs=[tok(D_MODEL), _full(w_in.shape), tok(128), tok(128), tok(128), _full(ln_z_g.shape), _full(ln_z_b.shape),
                  _full(w_s.shape), _full(b_full.shape), pl.BlockSpec(memory_space=pl.ANY)],
        out_specs=[_perm_tile(d, 3 * D_ATTN) for d in DILATIONS] + [tok(D_ATTN)] * 4, out_shape=outs,
        scratch_shapes=[pltpu.VMEM((TM, D_IN), F32), pltpu.VMEM((N_HEADS, BLK, BLK), MXU),
                        pltpu.VMEM((3 * LANE_CHUNKS, TM, 128), F32)],
        compiler_params=_cp(dimension_semantics=("arbitrary",)),
    )(x, w_in, c_tab, s1_tab, s2_tab, ln_z_g, ln_z_b, w_s, b_full, dep)


def _band_valid(n):
    i = lax.broadcasted_iota(jnp.int32, (BLK, 2 * BLK), 0)
    j = lax.broadcasted_iota(jnp.int32, (BLK, 2 * BLK), 1)
    return (j >= i) & (j <= i + BLK) & ((j >= BLK) | (n > 0))


def _attn_fwd(qkv, d):
    _, l_sub, _ = qkv.shape
    nb = l_sub // BLK

    def body(q_ref, kp_ref, kc_ref, vp_ref, vc_ref, o_ref, l_ref):
        valid = _band_valid(pl.program_id(1))
        kcat = jnp.concatenate([kp_ref[...], kc_ref[...]], axis=0)
        vcat = jnp.concatenate([vp_ref[...], vc_ref[...]], axis=0)
        for h in range(N_HEADS):
            cols = slice(h * HEAD_DIM, (h + 1) * HEAD_DIM)
            s = jnp.where(valid, _dot_nt(q_ref[:, cols], kcat[:, cols]), NEG_INF)
            m = jnp.max(s, axis=-1, keepdims=True)
            e = jnp.exp(s - m)
            den = jnp.sum(e, axis=-1, keepdims=True)
            o_ref[:, cols] = _dot(e, vcat[:, cols]) * (1.0 / den)
            l_ref[:, h * HEAD_LANES:(h + 1) * HEAD_LANES] = jnp.broadcast_to(m + jnp.log(den), (BLK, HEAD_LANES))

    def blk(w, col, prev=False):
        return pl.BlockSpec((None, BLK, w), lambda r, n: (r, jnp.maximum(n - 1, 0) if prev else n, col))

    return pl.pallas_call(
        body, name=f"attn_fwd_d{d}", grid=(d, nb),
        in_specs=[blk(D_ATTN, 0), blk(D_ATTN, 1, True), blk(D_ATTN, 1), blk(D_ATTN, 2, True), blk(D_ATTN, 2)],
        out_specs=[blk(D_ATTN, 0), blk(128, 0)],
        out_shape=[jax.ShapeDtypeStruct((d, l_sub, D_ATTN), F32), jax.ShapeDtypeStruct((d, l_sub, 128), F32)],
        compiler_params=_cp(dimension_semantics=("arbitrary", "arbitrary")),
    )(qkv, qkv, qkv, qkv, qkv)


def _attn_bwd(qkv, do, lse, delta, d, dep):
    _, l_sub, _ = qkv.shape
    nb = l_sub // BLK
    whole = l_sub <= 8 * BLK

    def shares(n, q_ref, kp_ref, kc_ref, vp_ref, vc_ref, do_ref, l_ref, dl_ref, dq_ref):
        valid = _band_valid(n)
        kcat = jnp.concatenate([kp_ref[...], kc_ref[...]], axis=0)
        vcat = jnp.concatenate([vp_ref[...], vc_ref[...]], axis=0)
        for h in range(N_HEADS):
            cols = slice(h * HEAD_DIM, (h + 1) * HEAD_DIM)
            stat = slice(h * HEAD_LANES, h * HEAD_LANES + 1)
            qh, doh = q_ref[:, cols], do_ref[:, cols]
            p = jnp.where(valid, jnp.exp(_dot_nt(qh, kcat[:, cols]) - l_ref[:, stat]), 0.0)
            ds = p * (_dot_nt(doh, vcat[:, cols]) - dl_ref[:, stat])
            dq_ref[:, cols] = _dot(ds, kcat[:, cols])
            yield cols, _dot_tn(ds, qh), _dot_tn(p, doh)

    def body_whole(*refs):
        dk_ref, dv_ref = refs[10:]
        n = pl.program_id(1)
        cur = pl.ds(pl.multiple_of(n * BLK, BLK), BLK)
        prev = pl.ds(pl.multiple_of(jnp.maximum(n - 1, 0) * BLK, BLK), BLK)
        for cols, dk2, dv2 in shares(n, *refs[:8], refs[9]):
            dk_ref[cur, cols] = dk2[BLK:]
            dv_ref[cur, cols] = dv2[BLK:]
            dk_ref[prev, cols] += dk2[0:BLK]
            dv_ref[prev, cols] += dv2[0:BLK]

    def body_carry(*refs):
        dk_ref, dv_ref, ck_scr, cv_scr = refs[10:]
        n = pl.program_id(1)

        @pl.when(n == 0)
        def _():
            ck_scr[...] = jnp.zeros_like(ck_scr)
            cv_scr[...] = jnp.zeros_like(cv_scr)

        @pl.when(n < nb)
        def _():
            for cols, dk2, dv2 in shares(n, *refs[:8], refs[9]):
                dk_ref[:, cols] = ck_scr[:, cols] + dk2[0:BLK]
                dv_ref[:, cols] = cv_scr[:, cols] + dv2[0:BLK]
                ck_scr[:, cols] = dk2[BLK:]
                cv_scr[:, cols] = dv2[BLK:]

        @pl.when(n == nb)
        def _():
            dk_ref[...] = ck_scr[...]
            dv_ref[...] = cv_scr[...]

    def blk(w, col, shift=0):
        return pl.BlockSpec((None, BLK, w), lambda r, n: (r, jnp.clip(n - shift, 0, nb - 1), col))

    if whole:
        dkv_spec = pl.BlockSpec((None, l_sub, D_ATTN), lambda r, n: (r, 0, 0))
        body, steps, scratch = body_whole, nb, []
    else:
        dkv_spec = blk(D_ATTN, 0, 1)
        body, steps, scratch = body_carry, nb + 1, [pltpu.VMEM((BLK, D_ATTN), F32)] * 2
    return pl.pallas_call(
        body, name=f"attn_bwd_d{d}", grid=(d, steps),
        in_specs=[blk(D_ATTN, 0), blk(D_ATTN, 1, 1), blk(D_ATTN, 1), blk(D_ATTN, 2, 1), blk(D_ATTN, 2),
                  blk(D_ATTN, 0), blk(128, 0), blk(128, 0), pl.BlockSpec(memory_space=pl.ANY)],
        out_specs=[blk(D_ATTN, 0), dkv_spec, dkv_spec],
        out_shape=[jax.ShapeDtypeStruct((d, l_sub, D_ATTN), F32)] * 3,
        scratch_shapes=scratch,
        compiler_params=_cp(dimension_semantics=("arbitrary", "arbitrary")),
    )(qkv, qkv, qkv, qkv, qkv, do, lse, delta, dep)


def _mix_ln1(os_, ls_, gm, x, w_o, ln1_g, ln1_b):
    t = x.shape[0]
    expand = _head_expand()

    def body(o1, o4, o16, l1, l4, l16, gm_ref, x_ref, wo_ref, g_ref, b_ref, ex_ref,
             attn_ref, lse1_ref, lse4_ref, lse16_ref, cat_ref, xhat_ref, rstd_ref, x1b_ref, o_scr, l_scr):
        _from_planes(o4, o_scr, DILATIONS[1], LANE_CHUNKS)
        _from_planes(o16, o_scr.at[pl.ds(LANE_CHUNKS, LANE_CHUNKS)], DILATIONS[2], LANE_CHUNKS)
        _from_planes(l4, l_scr, DILATIONS[1], 1)
        _from_planes(l16, l_scr.at[pl.ds(1, 1)], DILATIONS[2], 1)
        la, lb, lc = l1[...], l_scr[0], l_scr[1]
        m = jnp.maximum(jnp.maximum(la, lb), lc)
        ea, eb, ec = jnp.exp(la - m), jnp.exp(lb - m), jnp.exp(lc - m)
        den = ea + eb + ec
        inv = 1.0 / den
        wide = lambda w: _dot_exact(w, ex_ref[...])
        attn = (wide(ea * inv) * o1[...] + wide(eb * inv) * _unchunk(o_scr, LANE_CHUNKS)
                + wide(ec * inv) * _unchunk(o_scr, LANE_CHUNKS, LANE_CHUNKS))
        attn_ref[...] = attn
        lse = m + jnp.log(den)
        lse1_ref[...] = lse
        l_scr[2] = lse
        _to_planes(lse4_ref, l_scr.at[pl.ds(2, 1)], DILATIONS[1], 1, F32)
        _to_planes(lse16_ref, l_scr.at[pl.ds(2, 1)], DILATIONS[2], 1, F32)
        cat_ref[:, 0:D_ATTN] = attn.astype(MXU)
        cat_ref[:, D_ATTN:] = gm_ref[...]
        mix = jnp.dot(cat_ref[...], wo_ref[...], preferred_element_type=F32)
        xhat, rstd = _ln_fwd(ALPHA * x_ref[...] + mix)
        xhat_ref[...] = xhat
        rstd_ref[...] = rstd
        x1b_ref[...] = (xhat * g_ref[...] + b_ref[...]).astype(MXU)

    tok = lambda w: pl.BlockSpec((TM, w), lambda i: (i, 0))
    outs = [jax.ShapeDtypeStruct((t, D_ATTN), F32)] + [_perm_shape(t, d, 128, F32) for d in DILATIONS] + [
        jax.ShapeDtypeStruct((t, D_MODEL), MXU), jax.ShapeDtypeStruct((t, D_MODEL), F32), jax.ShapeDtypeStruct((t, 1), F32),
        jax.ShapeDtypeStruct((t, D_MODEL), MXU)]
    return pl.pallas_call(
        body, name="mix_ln1", grid=(t // TM,),
        in_specs=[_perm_tile(d, D_ATTN) for d in DILATIONS] + [_perm_tile(d, 128) for d in DILATIONS]
        + [tok(D_GMLP), tok(D_MODEL), _full(w_o.shape), _full(ln1_g.shape), _full(ln1_b.shape), _full(expand.shape)],
        out_specs=[tok(D_ATTN)] + [_perm_tile(d, 128) for d in DILATIONS] + [tok(D_MODEL), tok(D_MODEL), tok(1), tok(D_MODEL)],
        out_shape=outs,
        scratch_shapes=[pltpu.VMEM((2 * LANE_CHUNKS, TM, 128), F32), pltpu.VMEM((3, TM, 128), F32)],
        compiler_params=_cp(dimension_semantics=("arbitrary",)),
    )(*os_, *ls_, gm, x, w_o, ln1_g, ln1_b, expand)


def _conv_fwd(a_ext, w_ref, b_ref, rows):
    return (b_ref[...] + w_ref[2:3, :] * a_ext[HALO:HALO + rows] + w_ref[1:2, :] * a_ext[HALO - 1:HALO - 1 + rows]
            + w_ref[0:1, :] * a_ext[HALO - 2:HALO - 2 + rows])


def _ffn_in(x1b, w_a, w_b, conv_w, conv_b):
    t = x1b.shape[0]
    hb = TM // HALO

    def body(x_ref, xh_ref, wa_ref, wb_ref, cw_ref, cb_ref, apre_ref, b_ref, f_ref):
        i = pl.program_id(1)
        a_pre = _dot_nt(x_ref[...], wa_ref[...])
        a_halo = jnp.where(i > 0, _dot_nt(xh_ref[...], wa_ref[...]), 0.0)
        a = _conv_fwd(jnp.concatenate([a_halo, a_pre], axis=0), cw_ref, cb_ref, TM)
        b = _dot_nt(x_ref[...], wb_ref[...])
        apre_ref[...] = a_pre
        b_ref[...] = b
        f_ref[...] = (_gelu(a) * b).astype(MXU)

    blk = lambda r, c: pl.BlockSpec((None, r, c), lambda j, i: (j, 0, 0))
    tokj = pl.BlockSpec((None, TM, FF_BLK), lambda j, i: (j, i, 0))
    outs = [jax.ShapeDtypeStruct((N_SHARD, t, FF_BLK), F32)] * 2 + [jax.ShapeDtypeStruct((N_SHARD, t, FF_BLK), MXU)]
    return pl.pallas_call(
        body, name="ffn_in", grid=(N_SHARD, t // TM),
        in_specs=[pl.BlockSpec((TM, D_MODEL), lambda j, i: (i, 0)),
                  pl.BlockSpec((HALO, D_MODEL), lambda j, i: (jnp.maximum(i * hb - 1, 0), 0)),
                  blk(FF_BLK, D_MODEL), blk(FF_BLK, D_MODEL), blk(3, FF_BLK), blk(1, FF_BLK)],
        out_specs=[tokj, tokj, tokj], out_shape=outs,
        compiler_params=_cp(dimension_semantics=("arbitrary", "arbitrary")),
    )(x1b, x1b, w_a, w_b, conv_w, conv_b)


def _ffn_out_ln2(f, w_down, xhat1, ln1_g, ln1_b, ln2_g, ln2_b):
    t = xhat1.shape[0]

    def body(f_ref, wd_ref, xh_ref, g1_ref, b1_ref, g2_ref, b2_ref, xhat_ref, rstd_ref, x2b_ref):
        ff = jnp.dot(f_ref[0], wd_ref[0], preferred_element_type=F32)
        for j in range(1, N_SHARD):
            ff = ff + jnp.dot(f_ref[j], wd_ref[j], preferred_element_type=F32)
        x1 = xh_ref[...] * g1_ref[...] + b1_ref[...]
        xhat, rstd = _ln_fwd(ALPHA * x1 + ff)
        xhat_ref[...] = xhat
        rstd_ref[...] = rstd
        x2b_ref[...] = (xhat * g2_ref[...] + b2_ref[...]).astype(MXU)

    tok = lambda w: pl.BlockSpec((TM, w), lambda i: (i, 0))
    vec = _full((1, D_MODEL))
    outs = [jax.ShapeDtypeStruct((t, D_MODEL), F32), jax.ShapeDtypeStruct((t, 1), F32), jax.ShapeDtypeStruct((t, D_MODEL), MXU)]
    return pl.pallas_call(
        body, name="ffn_out_ln2", grid=(t // TM,),
        in_specs=[pl.BlockSpec((N_SHARD, TM, FF_BLK), lambda i: (0, i, 0)), _full(w_down.shape), tok(D_MODEL), vec, vec, vec, vec],
        out_specs=[tok(D_MODEL), tok(1), tok(D_MODEL)], out_shape=outs,
        compiler_params=_cp(dimension_semantics=("arbitrary",)),
    )(f, w_down, xhat1, ln1_g, ln1_b, ln2_g, ln2_b)


STAT_ROWS = 8


def _ple_loss_bwd(xhat2, rstd2, p, target, ln2_g, ln2_b, w_g, b_g, w_p, ln3_g, ln3_b):
    t = xhat2.shape[0]

    def body(xh2_ref, rs2_ref, p_ref, t_ref, g2_ref, b2_ref, wg_ref, bg_ref, wp_ref, g3_ref, b3_ref,
             dr2_ref, dgp_ref, dpp_ref, stat_ref, pp_scr):
        @pl.when(pl.program_id(0) == 0)
        def _():
            stat_ref[...] = jnp.zeros_like(stat_ref)

        xhat2 = xh2_ref[...]
        x2 = xhat2 * g2_ref[...] + b2_ref[...]
        gate = jax.nn.sigmoid(jnp.dot(x2.astype(MXU), wg_ref[...], preferred_element_type=F32) + bg_ref[...])
        pb = p_ref[...].astype(MXU)
        for j in range(N_SHARD):
            pp_scr[:, j * ROW_BLK:(j + 1) * ROW_BLK] = jnp.dot(pb, wp_ref[j], preferred_element_type=F32)
        pp = pp_scr[...]
        xhat3, rstd3 = _ln_fwd(ALPHA * x2 + gate * pp)
        err = xhat3 * g3_ref[...] + b3_ref[...] - t_ref[...]
        dy = err * (1.0 / D_MODEL)
        dr3 = _ln_bwd(dy, xhat3, rstd3, g3_ref[...])
        dgp = dr3 * pp * gate * (1.0 - gate)
        dgp_ref[...] = dgp.astype(MXU)
        dpp_ref[...] = (dr3 * gate).astype(MXU)
        dx2 = ALPHA * dr3 + _dot_nt(dgp, wg_ref[...])
        dr2_ref[...] = _ln_bwd(dx2, xhat2, rs2_ref[...], g2_ref[...])
        stat_ref[0:1, :] += _colsum(dy * xhat3)
        stat_ref[1:2, :] += _colsum(dy)
        stat_ref[2:3, :] += _colsum(dgp)
        stat_ref[3:4, :] += _colsum(dx2 * xhat2)
        stat_ref[4:5, :] += _colsum(dx2)
        stat_ref[5:6, :] += _colsum(err * err)

    tok = lambda w: pl.BlockSpec((TM, w), lambda i: (i, 0))
    vec = _full((1, D_MODEL))
    outs = [jax.ShapeDtypeStruct((t, D_MODEL), F32), jax.ShapeDtypeStruct((t, D_MODEL), MXU), jax.ShapeDtypeStruct((t, D_MODEL), MXU),
            jax.ShapeDtypeStruct((STAT_ROWS, D_MODEL), F32)]
    return pl.pallas_call(
        body, name="ple_loss_bwd", grid=(t // TM,),
        in_specs=[tok(D_MODEL), tok(1), tok(D_PLE), tok(D_MODEL), vec, vec, _full(w_g.shape), vec, _full(w_p.shape), vec, vec],
        out_specs=[tok(D_MODEL), tok(D_MODEL), tok(D_MODEL), _full((STAT_ROWS, D_MODEL))], out_shape=outs,
        scratch_shapes=[pltpu.VMEM((TM, D_MODEL), F32)],
        compiler_params=_cp(dimension_semantics=("arbitrary",)),
    )(xhat2, rstd2, p, target, ln2_g, ln2_b, w_g, b_g, w_p, ln3_g, ln3_b)


def _ffn_bwd(dr2, a_pre, b, w_down, w_a, w_b, conv_w, conv_b, xhat1, rstd1, ln1_g):
    t = dr2.shape[0]
    nt = t // TM
    hb = TM // HALO
    last_h = t // HALO - 1

    def body(dr_ref, drn_ref, ap_ref, app_ref, apn_ref, b_ref, bn_ref, wd_ref, wa_ref, wb_ref, cw_ref, cb_ref,
             xh_ref, rs_ref, g1_ref, dap_ref, dbb_ref, dr1_ref, cstat_ref, lstat_ref, acc_scr):
        i, j = pl.program_id(0), pl.program_id(1)

        @pl.when((i == 0) & (j == 0))
        def _():
            cstat_ref[...] = jnp.zeros_like(cstat_ref)
            lstat_ref[...] = jnp.zeros_like(lstat_ref)

        ext = TM + HALO
        dr_ext = jnp.concatenate([dr_ref[...], drn_ref[...]], axis=0)
        df = _dot_nt(dr_ext, wd_ref[...])
        a_all = jnp.concatenate([jnp.where(i > 0, app_ref[...], 0.0), ap_ref[...], apn_ref[...]], axis=0)
        a = _conv_fwd(a_all, cw_ref, cb_ref, ext)
        b_ext = jnp.concatenate([b_ref[...], bn_ref[...]], axis=0)
        row = lax.broadcasted_iota(jnp.int32, (ext, 1), 0)
        da = jnp.where((row < TM) | (i < nt - 1), df * b_ext * _gelu_grad(a), 0.0)
        dbb = df[0:TM] * _gelu(a[0:TM])
        da_pre = cw_ref[2:3, :] * da[0:TM] + cw_ref[1:2, :] * da[1:TM + 1] + cw_ref[0:1, :] * da[2:TM + 2]
        dap_ref[...] = da_pre.astype(MXU)
        dbb_ref[...] = dbb.astype(MXU)
        da_m = da[0:TM]
        for kk in range(3):
            cstat_ref[j, kk:kk + 1, :] += _colsum(da_m * a_all[HALO - 2 + kk:HALO - 2 + kk + TM])
        cstat_ref[j, 3:4, :] += _colsum(da_m)
        part = _dot(da_pre, wa_ref[...]) + _dot(dbb, wb_ref[...])

        @pl.when(j == 0)
        def _():
            acc_scr[...] = ALPHA * dr_ref[...] + part

        @pl.when(j > 0)
        def _():
            acc_scr[...] += part

        @pl.when(j == N_SHARD - 1)
        def _():
            dx1 = acc_scr[...]
            xhat1 = xh_ref[...]
            lstat_ref[0:1, :] += _colsum(dx1 * xhat1)
            lstat_ref[1:2, :] += _colsum(dx1)
            dr1_ref[...] = _ln_bwd(dx1, xhat1, rs_ref[...], g1_ref[...])

    tok = lambda w: pl.BlockSpec((TM, w), lambda i, j: (i, 0))
    tokj = pl.BlockSpec((None, TM, FF_BLK), lambda i, j: (j, i, 0))
    prevj = pl.BlockSpec((None, HALO, FF_BLK), lambda i, j: (j, jnp.maximum(i * hb - 1, 0), 0))
    nextj = pl.BlockSpec((None, HALO, FF_BLK), lambda i, j: (j, jnp.minimum((i + 1) * hb, last_h), 0))
    blk = lambda r, c: pl.BlockSpec((None, r, c), lambda i, j: (j, 0, 0))
    outs = [jax.ShapeDtypeStruct((N_SHARD, t, FF_BLK), MXU)] * 2 + [
        jax.ShapeDtypeStruct((t, D_MODEL), F32), jax.ShapeDtypeStruct((N_SHARD, STAT_ROWS, FF_BLK), F32),
        jax.ShapeDtypeStruct((STAT_ROWS, D_MODEL), F32)]
    return pl.pallas_call(
        body, name="ffn_bwd", grid=(nt, N_SHARD),
        in_specs=[tok(D_MODEL), pl.BlockSpec((HALO, D_MODEL), lambda i, j: (jnp.minimum((i + 1) * hb, last_h), 0)),
                  tokj, prevj, nextj, tokj, nextj, blk(FF_BLK, D_MODEL), blk(FF_BLK, D_MODEL), blk(FF_BLK, D_MODEL),
                  blk(3, FF_BLK), blk(1, FF_BLK), tok(D_MODEL), tok(1), _full((1, D_MODEL))],
        out_specs=[tokj, tokj, tok(D_MODEL), _full((N_SHARD, STAT_ROWS, FF_BLK)), _full((STAT_ROWS, D_MODEL))], out_shape=outs,
        scratch_shapes=[pltpu.VMEM((TM, D_MODEL), F32)],
        compiler_params=_cp(dimension_semantics=("arbitrary", "arbitrary")),
    )(dr2, dr2, a_pre, a_pre, a_pre, b, b, w_down, w_a, w_b, conv_w, conv_b, xhat1, rstd1, ln1_g)


def _mix_bwd(dr1, w_o, hu, hz, mixed, attn, ln_z_g, ln_z_b, w_s, dep):
    t = dr1.shape[0]
    nchunk = TM // BLK

    def body(dr_ref, wo_ref, hu_ref, hz_ref, mx_ref, attn_ref, g_ref, b_ref, ws_ref, grp_ref, red_ref, dep_ref,
             do1_ref, do4_ref, do16_ref, dl1_ref, dl4_ref, dl16_ref, duz_ref, dws_ref, dbs_ref, zstat_ref,
             wm_scr, dzn_scr, dbsum_scr, do_scr, dl_scr):
        @pl.when(pl.program_id(0) == 0)
        def _():
            row = lax.broadcasted_iota(jnp.int32, (BLK, BLK), 0)
            col = lax.broadcasted_iota(jnp.int32, (BLK, BLK), 1)
            for g in range(N_HEADS):
                wm_scr[g] = jnp.where(col <= row, ws_ref[g], 0.0).astype(MXU)
            dws_ref[...] = jnp.zeros_like(dws_ref)
            dbsum_scr[...] = jnp.zeros_like(dbsum_scr)
            zstat_ref[...] = jnp.zeros_like(zstat_ref)

        dcat = _dot_nt(dr_ref[...], wo_ref[...])
        dattn = dcat[:, 0:D_ATTN]
        do1_ref[...] = dattn.astype(MXU)
        for cc, val in enumerate(_chunks(dattn)):
            do_scr[cc] = val
        _to_planes(do4_ref, do_scr, DILATIONS[1], LANE_CHUNKS, MXU)
        _to_planes(do16_ref, do_scr, DILATIONS[2], LANE_CHUNKS, MXU)
        delta = _dot_exact(dattn * attn_ref[...], red_ref[...])
        dl1_ref[...] = delta
        dl_scr[0] = delta
        _to_planes(dl4_ref, dl_scr, DILATIONS[1], 1, F32)
        _to_planes(dl16_ref, dl_scr, DILATIONS[2], 1, F32)
        dgm = dcat[:, D_ATTN:]
        hu, hz = hu_ref[...], hz_ref[...]
        u = _gelu(hu)
        duz_ref[:, 0:D_GMLP] = (dgm * mx_ref[...] * _gelu_grad(hu)).astype(MXU)
        dmixed = dgm * u
        dmb = dmixed.astype(MXU)
        zhat, rstd = _ln_fwd(_gelu(hz))
        znb = (zhat * g_ref[...] + b_ref[...]).astype(MXU)
        dbs_acc = jnp.zeros((BLK, D_GMLP), F32)
        for ch in range(nchunk):
            rows = slice(ch * BLK, (ch + 1) * BLK)
            dbs_acc = dbs_acc + dmixed[rows]
            for g in range(N_HEADS):
                cols = slice(g * HEAD_DIM, (g + 1) * HEAD_DIM)
                dzn_scr[rows, cols] = _dot_tn(wm_scr[g], dmb[rows, cols])
                dws_ref[g] += _dot_nt(dmb[rows, cols], znb[rows, cols])
        dbsum_scr[...] += dbs_acc
        dzn = dzn_scr[...]
        zstat_ref[0:1, :] += _colsum(dzn * zhat)
        zstat_ref[1:2, :] += _colsum(dzn)
        duz_ref[:, D_GMLP:] = (_ln_bwd(dzn, zhat, rstd, g_ref[...]) * _gelu_grad(hz)).astype(MXU)

        @pl.when(pl.program_id(0) == nt - 1)
        def _():
            row = lax.broadcasted_iota(jnp.int32, (BLK, BLK), 0)
            col = lax.broadcasted_iota(jnp.int32, (BLK, BLK), 1)
            for g in range(N_HEADS):
                dws_ref[g] = jnp.where(col <= row, dws_ref[g], 0.0)
            dbs_ref[...] = lax.dot_general(grp_ref[...], dbsum_scr[...], (((1,), (1,)), ((), ())),
                                           precision=lax.Precision.HIGHEST, preferred_element_type=F32)

    nt = t // TM
    tok = lambda w: pl.BlockSpec((TM, w), lambda i: (i, 0))
    grp = jnp.asarray((np.arange(D_GMLP)[None, :] // HEAD_DIM == np.arange(N_HEADS)[:, None]).astype(np.float32))
    red = _head_reduce()
    outs = [_perm_shape(t, d, D_ATTN, MXU) for d in DILATIONS] + [_perm_shape(t, d, 128, F32) for d in DILATIONS] + [
        jax.ShapeDtypeStruct((t, 2 * D_GMLP), MXU),
        jax.ShapeDtypeStruct((N_HEADS, BLK, BLK), F32), jax.ShapeDtypeStruct((N_HEADS, BLK), F32),
        jax.ShapeDtypeStruct((STAT_ROWS, D_GMLP), F32)]
    return pl.pallas_call(
        body, name="mix_bwd", grid=(t // TM,),
        in_specs=[tok(D_MODEL), _full(w_o.shape), tok(D_GMLP), tok(D_GMLP), tok(D_GMLP), tok(D_ATTN), _full(ln_z_g.shape),
                  _full(ln_z_b.shape), _full(w_s.shape), _full(grp.shape), _full(red.shape), pl.BlockSpec(memory_space=pl.ANY)],
        out_specs=[_perm_tile(d, D_ATTN) for d in DILATIONS] + [_perm_tile(d, 128) for d in DILATIONS]
        + [tok(2 * D_GMLP), _full((N_HEADS, BLK, BLK)), _full((N_HEADS, BLK)), _full((STAT_ROWS, D_GMLP))],
        out_shape=outs,
        scratch_shapes=[pltpu.VMEM((N_HEADS, BLK, BLK), MXU), pltpu.VMEM((TM, D_GMLP), F32), pltpu.VMEM((BLK, D_GMLP), F32),
                        pltpu.VMEM((LANE_CHUNKS, TM, 128), F32), pltpu.VMEM((1, TM, 128), F32)],
        compiler_params=_cp(dimension_semantics=("arbitrary",)),
    )(dr1, w_o, hu, hz, mixed, attn, ln_z_g, ln_z_b, w_s, grp, red, dep)


def _dx_in(dqs, dks, dvs, duz, dr1, w_in, c_tab, s1_tab, s2_tab):
    t = dr1.shape[0]

    def body(dq1, dq4, dq16, dk1, dk4, dk16, dv1, dv4, dv16, duz_ref, dr_ref, w_ref, c_ref, s1_ref, s2_ref,
             dh_ref, dx_ref, acc_scr):
        sums = []
        for part, (g1, g4, g16) in enumerate(((dq1, dq4, dq16), (dk1, dk4, dk16), (dv1, dv4, dv16))):
            acc = acc_scr.at[pl.ds(part * LANE_CHUNKS, LANE_CHUNKS)]
            for cc in range(LANE_CHUNKS):
                acc[cc] = g1[:, cc * 128:(cc + 1) * 128]
            _from_planes(g4, acc, DILATIONS[1], LANE_CHUNKS, accumulate=True)
            _from_planes(g16, acc, DILATIONS[2], LANE_CHUNKS, accumulate=True)
            sums.append(_unchunk(acc_scr, LANE_CHUNKS, part * LANE_CHUNKS))
        c, s1, s2 = _tile_heads(c_ref[...]), _tile_heads(s1_ref[...]), _tile_heads(s2_ref[...])
        dh_ref[:, 0:D_ATTN] = _rope_apply_t(sums[0] * (1.0 / math.sqrt(HEAD_DIM)), c, s1, s2).astype(MXU)
        dh_ref[:, D_ATTN:2 * D_ATTN] = _rope_apply_t(sums[1], c, s1, s2).astype(MXU)
        dh_ref[:, 2 * D_ATTN:3 * D_ATTN] = sums[2].astype(MXU)
        dh_ref[:, 3 * D_ATTN:] = duz_ref[...]
        dx = ALPHA * dr_ref[...]
        for j in range(N_SHARD):
            dx = dx + _dot_nt(dh_ref[:, j * W_IN_BLK:(j + 1) * W_IN_BLK], w_ref[j])
        dx_ref[...] = dx

    tok = lambda w: pl.BlockSpec((TM, w), lambda i: (i, 0))
    outs = [jax.ShapeDtypeStruct((t, D_IN), MXU), jax.ShapeDtypeStruct((t, D_MODEL), F32)]
    return pl.pallas_call(
        body, name="dx_in", grid=(t // TM,),
        in_specs=[_perm_tile(d, D_ATTN) for d in DILATIONS] * 3
        + [tok(2 * D_GMLP), tok(D_MODEL), _full(w_in.shape), tok(128), tok(128), tok(128)],
        out_specs=[tok(D_IN), tok(D_MODEL)], out_shape=outs,
        scratch_shapes=[pltpu.VMEM((3 * LANE_CHUNKS, TM, 128), F32)],
        compiler_params=_cp(dimension_semantics=("arbitrary",)),
    )(*dqs, *dks, *dvs, duz, dr1, w_in, c_tab, s1_tab, s2_tab)


def _wgrad(name, x, dy, x_spec, dy_spec, out_spec, out_shape, grid):
    def body(x_ref, dy_ref, o_ref):
        o_ref[...] = _dot_tn(x_ref[...], dy_ref[...])

    return pl.pallas_call(
        body, name=name, grid=grid, in_specs=[x_spec, dy_spec], out_specs=out_spec,
        out_shape=jax.ShapeDtypeStruct(out_shape, F32),
        compiler_params=_cp(dimension_semantics=("arbitrary",) * len(grid)),
    )(x, dy)


def _local_step(x, p, rope, target, w_in, start_dep, late_weights, early_grads, early_grads_sent,
                ln_z_g, ln_z_b, w_s, b_s, ln1_g, ln1_b, conv_b, ln2_g, ln2_b, b_g, ln3_g, ln3_b):
    t = x.shape[0]
    half = TM
    c_tab, s1_tab, s2_tab = rope
    b_full = jnp.repeat(jnp.transpose(b_s[0]), HEAD_DIM, axis=1)
    conv_b4 = conv_b.reshape(N_SHARD, 1, FF_BLK)
    *qkvs, hu, hz, mixed, gm = _qkvuz(x, w_in, c_tab, s1_tab, s2_tab, ln_z_g, ln_z_b, w_s[0], b_full, start_dep)
    branches = [_attn_fwd(qkv, d) for qkv, d in zip(qkvs, DILATIONS)]
    w_o, w_a, w_b, conv_w, w_down, w_g, w_p = late_weights(branches[-1][1])
    attn, *lses, cat, xhat1, rstd1, x1b = _mix_ln1(
        [o for o, _ in branches], [l for _, l in branches], gm, x, w_o, ln1_g, ln1_b)
    a_pre, b_act, f = _ffn_in(x1b, w_a, w_b, conv_w, conv_b4)
    xhat2, rstd2, x2b = _ffn_out_ln2(f, w_down, xhat1, ln1_g, ln1_b, ln2_g, ln2_b)
    dr2, dgp, dpp, stat3 = _ple_loss_bwd(xhat2, rstd2, p, target, ln2_g, ln2_b, w_g, b_g, w_p, ln3_g, ln3_b)
    da_pre, dbb, dr1, cstat, stat1 = _ffn_bwd(dr2, a_pre, b_act, w_down, w_a, w_b, conv_w, conv_b4, xhat1, rstd1, ln1_g)

    full_t = lambda w, im: pl.BlockSpec((t, w), im)
    ffj = pl.BlockSpec((None, t, FF_BLK), lambda j, kk: (j, 0, 0))
    early = dict(
        w_ple_gate=_wgrad("dw_g", x2b, dgp, full_t(half, lambda kk, n: (0, kk)), full_t(half, lambda kk, n: (0, n)),
                          pl.BlockSpec((half, half), lambda kk, n: (kk, n)), (D_MODEL, D_MODEL), (2, 2)),
        w_ple_in=_wgrad("dw_p", p, dpp, full_t(D_PLE, lambda j: (0, 0)), full_t(ROW_BLK, lambda j: (0, j)),
                        pl.BlockSpec((None, D_PLE, ROW_BLK), lambda j: (j, 0, 0)), (N_SHARD, D_PLE, ROW_BLK), (N_SHARD,)),
        w_ff_down=_wgrad("dw_down", f, dr2, ffj, full_t(half, lambda j, n: (0, n)),
                         pl.BlockSpec((None, FF_BLK, half), lambda j, n: (j, 0, n)), (N_SHARD, FF_BLK, D_MODEL), (N_SHARD, 2)),
        w_ff_a=_wgrad("dw_a", da_pre, x1b, ffj, full_t(half, lambda j, n: (0, n)),
                      pl.BlockSpec((None, FF_BLK, half), lambda j, n: (j, 0, n)), (N_SHARD, FF_BLK, D_MODEL), (N_SHARD, 2)),
        w_ff_b=_wgrad("dw_b", dbb, x1b, ffj, full_t(half, lambda j, n: (0, n)),
                      pl.BlockSpec((None, FF_BLK, half), lambda j, n: (j, 0, n)), (N_SHARD, FF_BLK, D_MODEL), (N_SHARD, 2)),
        w_o=_wgrad("dw_o", cat, dr1, full_t(half, lambda kk, n: (0, kk)), full_t(half, lambda kk, n: (0, n)),
                   pl.BlockSpec((half, half), lambda kk, n: (kk, n)), (D_MODEL, D_MODEL), (2, 2)))
    dep = early_grads(early)

    do1, do4, do16, dl1, dl4, dl16, duz, dws, dbs, zstat = _mix_bwd(
        dr1, w_o, hu, hz, mixed, attn, ln_z_g, ln_z_b, w_s[0], dep)
    dep = early_grads_sent(duz)
    dqkv = [_attn_bwd(qkv, do, lse, dl, d, dep)
            for qkv, do, lse, dl, d in zip(qkvs, (do1, do4, do16), lses, (dl1, dl4, dl16), DILATIONS)]
    dh, grad_x = _dx_in([g[0] for g in dqkv], [g[1] for g in dqkv], [g[2] for g in dqkv], duz, dr1, w_in,
                        c_tab, s1_tab, s2_tab)
    g_w_in = _wgrad("dw_in", x, dh, full_t(half, lambda j, kk: (0, kk)), full_t(W_IN_BLK, lambda j, kk: (0, j)),
                    pl.BlockSpec((None, half, W_IN_BLK), lambda j, kk: (j, kk, 0)), (N_SHARD, D_MODEL, W_IN_BLK), (N_SHARD, 2))
    return grad_x, g_w_in, dws, dbs, (stat3, stat1, zstat, cstat)


def _tile_rows(rows, mult, steps):
    if rows % mult:
        return rows
    return next(rows // k for k in range(steps, rows + 1) if rows % k == 0 and (rows // k) % mult == 0)


def _grid_spec(grid, in_specs, out_specs):
    return pltpu.PrefetchScalarGridSpec(num_scalar_prefetch=1, grid=grid, in_specs=in_specs, out_specs=out_specs)


def _on_own_steps(i, count, steps, work):
    if count == steps:
        work()
    else:
        pl.when(i < count)(work)


def _place_shards(name, ws, dtypes, place, dep):
    n = len(ws)
    tiles = [_tile_rows(w.shape[0], 16, 8) for w in ws]
    counts = [w.shape[0] // t for w, t in zip(ws, tiles)]
    steps = max(counts)

    def body(s_ref, *refs):
        i = pl.program_id(0)
        for a in range(n):
            def work(a=a):
                refs[n + 1 + a][...] = refs[a][...].astype(dtypes[a])
            _on_own_steps(i, counts[a], steps, work)

    def tile(a, lead):
        last = counts[a] - 1
        if lead:
            return pl.BlockSpec((None, tiles[a], ws[a].shape[1]), lambda i, s: (s[0], jnp.minimum(i, last), 0))
        return pl.BlockSpec((tiles[a], ws[a].shape[1]), lambda i, s: (jnp.minimum(i, last), 0))

    return pl.pallas_call(
        body, name=name,
        grid_spec=_grid_spec((steps,), [tile(a, False) for a in range(n)] + [pl.BlockSpec(memory_space=pl.ANY)],
                             [tile(a, True) for a in range(n)]),
        out_shape=[jax.ShapeDtypeStruct((N_SHARD, *w.shape), dt) for w, dt in zip(ws, dtypes)],
        compiler_params=_cp())(place, *ws, dep)


def _pair_sums(name, mines, gots, place):
    n = len(mines)
    tiles = [_tile_rows(g.shape[1], 16, 2) for g in gots]
    per_blk = [g.shape[1] // t for g, t in zip(gots, tiles)]
    counts = [N_SHARD * nh for nh in per_blk]
    steps = max(counts)

    def body(s_ref, *refs):
        i = pl.program_id(0)
        for a in range(n):
            def work(a=a):
                refs[2 * n + a][...] = (refs[a][...] + refs[n + a][...]).astype(BF16)
            _on_own_steps(i, counts[a], steps, work)

    def tile(a, mine):
        nh, last = per_blk[a], counts[a] - 1

        def index(i, s):
            g = jnp.minimum(i, last)
            return (g // nh, (s[1] * nh if mine else 0) + g % nh, 0)

        return pl.BlockSpec((None, tiles[a], gots[a].shape[2]), index)

    return pl.pallas_call(
        body, name=name,
        grid_spec=_grid_spec((steps,), [tile(a, True) for a in range(n)] + [tile(a, False) for a in range(n)],
                             [tile(a, False) for a in range(n)]),
        out_shape=[jax.ShapeDtypeStruct(g.shape, BF16) for g in gots], compiler_params=_cp())(place, *mines, *gots)


def _chip_sums(name, owns, landeds, place, dep):
    n = len(owns)
    tiles = [_tile_rows(o.shape[1], 16, 8) for o in owns]
    counts = [o.shape[1] // t for o, t in zip(owns, tiles)]
    steps = max(counts)

    def body(s_ref, *refs):
        i = pl.program_id(0)
        for a in range(n):
            def work(a=a):
                own, l1, l2, l3 = (refs[4 * a + k][...].astype(F32) for k in range(4))
                refs[4 * n + 1 + a][...] = ((own + l1) + l2) + l3
            _on_own_steps(i, counts[a], steps, work)

    def slot(a, d):
        last = counts[a] - 1
        return pl.BlockSpec((None, tiles[a], owns[a].shape[2]), lambda i, s: ((s[0] + d) % N_SHARD, jnp.minimum(i, last), 0))

    def out(a):
        nh, last = counts[a], counts[a] - 1
        return pl.BlockSpec((tiles[a], owns[a].shape[2]), lambda i, s: (s[1] * nh + jnp.minimum(i, last), 0))

    operands = [x for o, l in zip(owns, landeds) for x in (o, l, l, l)]
    return pl.pallas_call(
        body, name=name,
        grid_spec=_grid_spec((steps,), [slot(a, d) for a in range(n) for d in range(4)] + [pl.BlockSpec(memory_space=pl.ANY)],
                             [out(a) for a in range(n)]),
        out_shape=[jax.ShapeDtypeStruct((2 * o.shape[1], o.shape[2]), F32) for o in owns],
        compiler_params=_cp())(place, *operands, dep)


def _adamw_math(w, g, m, v):
    m = ADAM_B1 * m + (1.0 - ADAM_B1) * g
    v = ADAM_B2 * v + (1.0 - ADAM_B2) * (g * g)
    m_hat = m / (1.0 - ADAM_B1 ** ADAM_STEP)
    v_hat = v / (1.0 - ADAM_B2 ** ADAM_STEP)
    delta = -ADAM_LR * (m_hat / (jnp.sqrt(v_hat) + ADAM_EPS) + ADAM_WD * w)
    return delta, m, v


def _adamw_shards(name, ws, gs, ms, vs):
    n = len(ws)
    tiles = [_tile_rows(w.shape[1], 8, 8) for w in ws]
    counts = [w.shape[1] // t for w, t in zip(ws, tiles)]
    steps = max(counts)

    def body(*refs):
        i = pl.program_id(0)
        for a in range(n):
            def work(a=a):
                w_ref, g_ref, m_ref, v_ref = refs[4 * a:4 * a + 4]
                d_ref, nm_ref, nv_ref = refs[4 * n + 3 * a:4 * n + 3 * a + 3]
                d_ref[...], nm_ref[...], nv_ref[...] = _adamw_math(w_ref[...], g_ref[...], m_ref[...], v_ref[...])
            _on_own_steps(i, counts[a], steps, work)

    def tile(a, lead):
        last, c = counts[a] - 1, ws[a].shape[2]
        if lead:
            return pl.BlockSpec((None, tiles[a], c), lambda i: (0, jnp.minimum(i, last), 0))
        return pl.BlockSpec((tiles[a], c), lambda i: (jnp.minimum(i, last), 0))

    res = pl.pallas_call(
        body, name=name, grid=(steps,),
        in_specs=[tile(a, lead) for a in range(n) for lead in (True, False, True, True)],
        out_specs=[tile(a, True) for a in range(n) for _ in range(3)],
        out_shape=[jax.ShapeDtypeStruct(w.shape, F32) for w in ws for _ in range(3)],
        compiler_params=_cp())(*[x for quad in zip(ws, gs, ms, vs) for x in quad])
    return [tuple(res[3 * a:3 * a + 3]) for a in range(n)]


MESH = pl.DeviceIdType.MESH
ANY = pl.BlockSpec(memory_space=pl.ANY)


def _place():
    x, y, c = lax.axis_index("x"), lax.axis_index("y"), lax.axis_index("c")
    chips = [(1 - x, y), (x, 1 - y), (1 - x, 1 - y)]
    return x, y, c, 2 * x + y, chips


def _remote(src, dst, send_sem, recv_sem, dev):
    return pltpu.make_async_remote_copy(src_ref=src, dst_ref=dst, send_sem=send_sem, recv_sem=recv_sem,
                                        device_id=dev, device_id_type=MESH)


def _half(ref, hc, rows):
    return ref.at[pl.ds(hc * (rows // 2), rows // 2)]


def _sibling_join(blocks, tag):
    n = len(blocks)

    def body(*refs):
        outs = refs[n:2 * n]
        send, recv = refs[2 * n:]
        x, y, c, _, _ = _place()
        cps = []
        for a in range(n):
            h = blocks[a].shape[0] // 2
            mine = outs[a].at[pl.ds(c * h, h)]
            cp = _remote(mine, mine, send.at[a], recv.at[a], (x, y, 1 - c))
            cp.start()
            cps.append(cp)
        for a, cp in enumerate(cps):
            h = blocks[a].shape[0] // 2
            theirs = outs[a].at[pl.ds((1 - c) * h, h)]
            _remote(theirs, theirs, send.at[a], recv.at[a], (x, y, 1 - c)).wait_recv()
            cp.wait_send()

    sem = pltpu.SemaphoreType.DMA
    return pl.pallas_call(body, name=f"rs_sibling_join_{tag}", in_specs=[ANY] * n, out_specs=[ANY] * n,
                          out_shape=[jax.ShapeDtypeStruct(b_.shape, b_.dtype) for b_ in blocks],
                          input_output_aliases={a: a for a in range(n)},
                          scratch_shapes=[sem((n,)), sem((n,))])(*blocks)


HBM = pl.BlockSpec(memory_space=pltpu.HBM)
SEM = pl.BlockSpec(memory_space=pltpu.SEMAPHORE)
TOKEN = jax.ShapeDtypeStruct((8, 128), F32)


def _in_flight_params():
    return pltpu.CompilerParams(has_side_effects=pltpu.SideEffectType.DATAFLOW_SIDE_EFFECTING)


def _in_hbm(a):
    return pltpu.with_memory_space_constraint(a, pltpu.HBM)


def _gather_piece(ref, rows, split, slot, hc):
    return _half(ref.at[slot], hc, rows) if split else ref.at[slot]


def _gather_start(stacks, split, after, tag):
    n = len(stacks)

    def body(*refs):
        ins = refs[:n]
        send, recv = refs[n + 1], refs[n + 2]
        token = refs[2 * n + 3]
        _, _, c, j, chips = _place()
        for a in range(n):
            mine = _gather_piece(ins[a], stacks[a].shape[1], split[a], j, c)
            for t in range(3):
                _remote(mine, mine, send.at[3 * a + t], recv.at[3 * a + t], (*chips[t], c)).start()
        token[...] = jnp.zeros_like(token)

    sems = pltpu.SemaphoreType.DMA((3 * n,))
    res = pl.pallas_call(
        body, name=f"gather_start_{tag}", in_specs=[HBM] * n + [ANY],
        out_specs=[SEM, SEM] + [HBM] * n + [pl.BlockSpec(memory_space=pltpu.VMEM)],
        out_shape=[sems, sems] + [pltpu.HBM(s.shape, s.dtype) for s in stacks] + [TOKEN],
        input_output_aliases={a: a + 2 for a in range(n)}, compiler_params=_in_flight_params(),
    )(*[_in_hbm(s) for s in stacks], after)
    return res[0], res[1], res[2:2 + n], res[2 + n]


def _gather_wait(send, recv, stacks, split, after, tag):
    n = len(stacks)

    def body(*refs):
        ins = refs[:n]
        send_ref, recv_ref = refs[n], refs[n + 1]
        _, _, c, j, chips = _place()
        for a in range(n):
            rows = stacks[a].shape[1]
            mine = _gather_piece(ins[a], rows, split[a], j, c)
            for t, (px, py) in enumerate(chips):
                theirs = _gather_piece(ins[a], rows, split[a], 2 * px + py, c)
                _remote(mine, mine, send_ref.at[3 * a + t], recv_ref.at[3 * a + t], (px, py, c)).wait_send()
                _remote(theirs, theirs, send_ref.at[3 * a + t], recv_ref.at[3 * a + t], (px, py, c)).wait_recv()

    return pl.pallas_call(
        body, name=f"gather_wait_{tag}", in_specs=[HBM] * n + [SEM, SEM, ANY], out_specs=[HBM] * n,
        out_shape=[pltpu.HBM(s.shape, s.dtype) for s in stacks],
        input_output_aliases={a: a for a in range(n)}, compiler_params=_in_flight_params(),
    )(*stacks, send, recv, after)


def _gather_forward(stacks, split, tag):
    idx = [a for a in range(len(stacks)) if split[a]]
    n = len(idx)

    def body(*refs):
        outs = refs[n:2 * n]
        send, recv = refs[2 * n:]
        x, y, c, _, chips = _place()
        sends = []
        for t, (px, py) in enumerate(chips):
            for a in range(n):
                blk = _half(outs[a].at[2 * px + py], c, stacks[idx[a]].shape[1])
                cp = _remote(blk, blk, send.at[a, t], recv.at[a, t], (x, y, 1 - c))
                cp.start()
                sends.append(cp)
        for t, (px, py) in enumerate(chips):
            for a in range(n):
                blk = _half(outs[a].at[2 * px + py], 1 - c, stacks[idx[a]].shape[1])
                _remote(blk, blk, send.at[a, t], recv.at[a, t], (x, y, 1 - c)).wait_recv()
        for cp in sends:
            cp.wait_send()

    sem = pltpu.SemaphoreType.DMA
    res = pl.pallas_call(
        body, name=f"gather_forward_{tag}", in_specs=[ANY] * n, out_specs=[ANY] * n,
        out_shape=[jax.ShapeDtypeStruct(stacks[a].shape, stacks[a].dtype) for a in idx],
        input_output_aliases={a: a for a in range(n)}, scratch_shapes=[sem((n, 3)), sem((n, 3))],
    )(*[stacks[a] for a in idx])
    out = list(stacks)
    for a, r in zip(idx, res):
        out[a] = r
    return out


def _swap_start(grads, tag):
    n = len(grads)

    def body(*refs):
        ins, gots = refs[:n], refs[n:2 * n]
        send, recv = refs[2 * n], refs[2 * n + 1]
        token = refs[4 * n + 2]
        x, y, c, _, _ = _place()
        for a in range(n):
            h = grads[a].shape[1] // 2
            _remote(ins[a].at[:, pl.ds((1 - c) * h, h)], gots[a], send.at[a], recv.at[a], (x, y, 1 - c)).start()
        token[...] = jnp.zeros_like(token)

    sems = pltpu.SemaphoreType.DMA((n,))
    halves = [(g.shape[0], g.shape[1] // 2, g.shape[2]) for g in grads]
    res = pl.pallas_call(
        body, name=f"swap_start_{tag}", in_specs=[HBM] * (2 * n),
        out_specs=[SEM, SEM] + [HBM] * (2 * n) + [pl.BlockSpec(memory_space=pltpu.VMEM)],
        out_shape=[sems, sems] + [pltpu.HBM(g.shape, g.dtype) for g in grads] + [pltpu.HBM(s, F32) for s in halves] + [TOKEN],
        input_output_aliases={a: a + 2 for a in range(2 * n)}, compiler_params=_in_flight_params(),
    )(*[_in_hbm(g) for g in grads], *[_in_hbm(lax.empty(s, F32)) for s in halves])
    return res[0], res[1], res[2:2 + n], res[2 + n:2 + 2 * n], res[2 + 2 * n]


def _swap_wait(send, recv, grads, gots, after, tag):
    n = len(grads)

    def body(*refs):
        ins, lnd = refs[:n], refs[n:2 * n]
        send_ref, recv_ref = refs[2 * n], refs[2 * n + 1]
        x, y, c, _, _ = _place()
        for a in range(n):
            h = grads[a].shape[1] // 2
            cp = _remote(ins[a].at[:, pl.ds((1 - c) * h, h)], lnd[a], send_ref.at[a], recv_ref.at[a], (x, y, 1 - c))
            cp.wait_send()
            cp.wait_recv()

    bufs = [pltpu.HBM(g.shape, g.dtype) for g in grads] + [pltpu.HBM(g.shape, g.dtype) for g in gots]
    res = pl.pallas_call(
        body, name=f"swap_wait_{tag}", in_specs=[HBM] * (2 * n) + [SEM, SEM, ANY], out_specs=[HBM] * (2 * n),
        out_shape=bufs, input_output_aliases={a: a for a in range(2 * n)}, compiler_params=_in_flight_params(),
    )(*grads, *gots, send, recv, after)
    return res[:n], res[n:]


def _exchange_start(parts, tag):
    n = len(parts)

    def body(*refs):
        ins, lands = refs[:n], refs[n:2 * n]
        send, recv = refs[2 * n], refs[2 * n + 1]
        token = refs[4 * n + 2]
        _, _, c, j, chips = _place()
        for t, (px, py) in enumerate(chips):
            for a in range(n):
                _remote(ins[a].at[2 * px + py], lands[a].at[j], send.at[3 * a + t], recv.at[3 * a + t], (px, py, c)).start()
        token[...] = jnp.zeros_like(token)

    sems = pltpu.SemaphoreType.DMA((3 * n,))
    bufs = [pltpu.HBM(p.shape, p.dtype) for p in parts]
    res = pl.pallas_call(
        body, name=f"exchange_start_{tag}", in_specs=[HBM] * (2 * n),
        out_specs=[SEM, SEM] + [HBM] * (2 * n) + [pl.BlockSpec(memory_space=pltpu.VMEM)],
        out_shape=[sems, sems] + bufs + bufs + [TOKEN],
        input_output_aliases={a: a + 2 for a in range(2 * n)}, compiler_params=_in_flight_params(),
    )(*[_in_hbm(p) for p in parts], *[_in_hbm(lax.empty(p.shape, p.dtype)) for p in parts])
    return res[0], res[1], res[2:2 + n], res[2 + n:2 + 2 * n], res[2 + 2 * n]


def _exchange_wait(send, recv, parts, lands, after, tag):
    n = len(parts)

    def body(*refs):
        ins, lnd = refs[:n], refs[n:2 * n]
        send_ref, recv_ref = refs[2 * n], refs[2 * n + 1]
        _, _, c, j, chips = _place()
        for t, (px, py) in enumerate(chips):
            jt = 2 * px + py
            for a in range(n):
                _remote(ins[a].at[jt], lnd[a].at[j], send_ref.at[3 * a + t], recv_ref.at[3 * a + t], (px, py, c)).wait_send()
                _remote(ins[a].at[jt], lnd[a].at[jt], send_ref.at[3 * a + t], recv_ref.at[3 * a + t], (px, py, c)).wait_recv()

    bufs = [pltpu.HBM(p.shape, p.dtype) for p in parts]
    res = pl.pallas_call(
        body, name=f"exchange_wait_{tag}", in_specs=[HBM] * (2 * n) + [SEM, SEM, ANY], out_specs=[HBM] * (2 * n),
        out_shape=bufs + bufs, input_output_aliases={a: a for a in range(2 * n)}, compiler_params=_in_flight_params(),
    )(*parts, *lands, send, recv, after)
    return res[:n], res[n:]


def _allreduce_small(arrs):
    n = len(arrs)

    def body(*refs):
        ins, outs = refs[:n], refs[n:2 * n]
        sib, chip = refs[2 * n:3 * n], refs[3 * n:4 * n]
        ssend, srecv, csend, crecv = refs[4 * n:]
        x, y, c, j, chips = _place()
        swaps = [_remote(ins[a], sib[a], ssend.at[a], srecv.at[a], (x, y, 1 - c)) for a in range(n)]
        for cp in swaps:
            cp.start()
        sends = []
        for a in range(n):
            swaps[a].wait_recv()
            chip[a][j] = ins[a][...] + sib[a][...]
            for t, (px, py) in enumerate(chips):
                cp = _remote(chip[a].at[j], chip[a].at[j], csend.at[a, t], crecv.at[a, t], (px, py, c))
                cp.start()
                sends.append(cp)
        for a in range(n):
            for t, (px, py) in enumerate(chips):
                jt = 2 * px + py
                _remote(chip[a].at[jt], chip[a].at[jt], csend.at[a, t], crecv.at[a, t], (px, py, c)).wait_recv()
            outs[a][...] = ((chip[a][0] + chip[a][1]) + chip[a][2]) + chip[a][3]
        for cp in swaps + sends:
            cp.wait_send()

    sem = pltpu.SemaphoreType.DMA
    vm = pl.BlockSpec(memory_space=pltpu.VMEM)
    return pl.pallas_call(
        body, name="allreduce_small", in_specs=[vm] * n, out_specs=[vm] * n,
        out_shape=[jax.ShapeDtypeStruct(a.shape, F32) for a in arrs],
        scratch_shapes=[pltpu.VMEM(a.shape, F32) for a in arrs] + [pltpu.VMEM((N_SHARD, *a.shape), F32) for a in arrs]
        + [sem((n,)), sem((n,)), sem((n, 3)), sem((n, 3))],
        compiler_params=_cp(),
    )(*arrs)


SMALL_1024 = ("ln1_g", "ln1_b", "ln2_g", "ln2_b", "b_ple_gate", "ln3_g", "ln3_b")


def _adamw_small(red3, red1, redz, g_conv_w, redc, red_ws, red_bs, params):
    shape2d = {"ln_z_g": (1, D_GMLP), "ln_z_b": (1, D_GMLP), "w_s": (N_HEADS * BLK, BLK), "b_s": (N_HEADS, BLK),
               "conv_w": (3, FF_BLK), "conv_b": (N_SHARD, FF_BLK), **{k: (1, D_MODEL) for k in SMALL_1024}}
    names = list(shape2d)
    flat = [a.reshape(shape2d[k]) for k in names for a in params[k]]

    def body(r3, r1, rz, gcw, rc, rws, rbs, *refs):
        ins, outs = refs[:3 * len(names)], refs[3 * len(names):]

        def grad_of(k):
            if k == "w_s":
                return rws[...]
            if k == "b_s":
                return rbs[...]
            if k == "conv_w":
                return gcw[0:3, :]
            if k == "conv_b":
                return jnp.concatenate([rc[j * STAT_ROWS + 3:j * STAT_ROWS + 4, :] for j in range(N_SHARD)], axis=0)
            src, row = {"ln3_g": (r3, 0), "ln3_b": (r3, 1), "b_ple_gate": (r3, 2), "ln2_g": (r3, 3), "ln2_b": (r3, 4),
                        "ln1_g": (r1, 0), "ln1_b": (r1, 1), "ln_z_g": (rz, 0), "ln_z_b": (rz, 1)}[k]
            return src[row:row + 1, :]

        for i, k in enumerate(names):
            w_ref, m_ref, v_ref = ins[3 * i:3 * i + 3]
            g_ref, d_ref, nm_ref, nv_ref = outs[4 * i:4 * i + 4]
            g = grad_of(k)
            g_ref[...] = g
            d_ref[...], nm_ref[...], nv_ref[...] = _adamw_math(w_ref[...], g, m_ref[...], v_ref[...])

    res = pl.pallas_call(
        body, name="adamw_small",
        out_shape=[jax.ShapeDtypeStruct(shape2d[k], F32) for k in names for _ in range(4)],
        compiler_params=_cp(),
    )(red3, red1, redz, g_conv_w, redc, red_ws, red_bs, *flat)
    return {k: tuple(r.reshape(params[k][0].shape) for r in res[4 * i:4 * i + 4]) for i, k in enumerate(names)}


WEIGHTS = ("w_in", "ln_z_g", "ln_z_b", "w_s", "b_s", "w_o", "ln1_g", "ln1_b", "w_ff_a", "w_ff_b", "conv_w", "conv_b",
           "w_ff_down", "ln2_g", "ln2_b", "w_ple_gate", "b_ple_gate", "w_ple_in", "ln3_g", "ln3_b")
BIG = ("w_in", "w_o", "w_ff_a", "w_ff_b", "w_ff_down", "w_ple_gate", "w_ple_in")
TRANSPOSED = ("w_ff_a", "w_ff_b")
LATE = ("w_o", "w_ff_a", "w_ff_b", "w_ff_down", "w_ple_gate", "w_ple_in", "conv_w")


def kernel(x, p, positions, w_in, ln_z_g, ln_z_b, w_s, b_s, w_o, ln1_g, ln1_b, w_ff_a, w_ff_b, conv_w, conv_b, w_ff_down, ln2_g, ln2_b, w_ple_gate, b_ple_gate, w_ple_in, ln3_g, ln3_b, loss_target, m_w_in, m_ln_z_g, m_ln_z_b, m_w_s, m_b_s, m_w_o, m_ln1_g, m_ln1_b, m_w_ff_a, m_w_ff_b, m_conv_w, m_conv_b, m_w_ff_down, m_ln2_g, m_ln2_b, m_w_ple_gate, m_b_ple_gate, m_w_ple_in, m_ln3_g, m_ln3_b, v_w_in, v_ln_z_g, v_ln_z_b, v_w_s, v_b_s, v_w_o, v_ln1_g, v_ln1_b, v_w_ff_a, v_w_ff_b, v_conv_w, v_conv_b, v_w_ff_down, v_ln2_g, v_ln2_b, v_w_ple_gate, v_b_ple_gate, v_w_ple_in, v_ln3_g, v_ln3_b):
    args = locals()
    w = {k: args[k] for k in WEIGHTS}
    m = {k: args["m_" + k] for k in WEIGHTS}
    v = {k: args["v_" + k] for k in WEIGHTS}

    for k in TRANSPOSED:
        w[k], m[k], v[k] = (jnp.swapaxes(a, 1, 2) for a in (w[k], m[k], v[k]))

    chip = 2 * lax.axis_index("x") + lax.axis_index("y")
    place = jnp.stack([chip, lax.axis_index("c")]).astype(jnp.int32)
    stack = dict(zip(["w_in"], _place_shards("cast_w_in", [w["w_in"][0]], [MXU], place, place)))
    i_send, i_recv, in_flight, dep = _gather_start([stack["w_in"]], [True], place, "w_in")
    stack.update(zip(LATE, _place_shards("cast_late", [w[k][0] for k in LATE],
                                         [F32 if k == "conv_w" else MXU for k in LATE], place, dep)))
    rope = _rope_tables(positions, x.shape[1], stack[LATE[-1]])
    landed_in = _gather_wait(i_send, i_recv, in_flight, [True], rope[0], "w_in")
    w_in_full, = _gather_forward(landed_in, [True], "w_in")
    split_late = [k != "conv_w" for k in LATE]
    g_send, g_recv, late_flight, start_dep = _gather_start([stack[k] for k in LATE], split_late, w_in_full, "late")

    def late_weights(after):
        landed = _gather_wait(g_send, g_recv, late_flight, split_late, after, "late")
        fw = dict(zip(LATE, _gather_forward(landed, split_late, "late")))
        return (fw["w_o"].reshape(D_MODEL, D_MODEL), fw["w_ff_a"], fw["w_ff_b"], fw["conv_w"], fw["w_ff_down"],
                fw["w_ple_gate"].reshape(D_MODEL, D_MODEL), fw["w_ple_in"])

    def swap_started(names, grads, tag):
        stacked = [g.reshape(N_SHARD, *w[k].shape[1:]) for k, g in zip(names, grads)]
        return (names, tag, *_swap_start(stacked, tag))

    def partial_sums(swap, after):
        names, tag, send, recv, stacked, gots, _ = swap
        stacked, got = _swap_wait(send, recv, stacked, gots, after, tag)
        pair = _pair_sums(f"rs_pair_{tag}", stacked, got, place)
        return (names, tag, *_exchange_start(pair, tag))

    def reduced(trip, after, dep):
        names, tag, send, recv, pair, lands, _ = trip
        pair, landed = _exchange_wait(send, recv, pair, lands, after, tag)
        blocks = _chip_sums(f"rs_sum_{tag}", pair, landed, place, dep)
        return dict(zip(names, _sibling_join(blocks, tag)))

    trips = {}

    def early_grads(grads):
        trips["swap"] = swap_started(list(grads), list(grads.values()), "early")
        return trips["swap"][-1]

    def early_grads_sent(after):
        trips["early"] = partial_sums(trips["swap"], after)
        return trips["early"][-1]

    grad_x, g_w_in, dws, dbs, (stat3, stat1, zstat, cstat) = _local_step(
        x[0], p[0, 0], rope, loss_target[0], w_in_full, start_dep, late_weights, early_grads, early_grads_sent,
        ln_z_g, ln_z_b, w_s, b_s, ln1_g, ln1_b, conv_b, ln2_g, ln2_b, b_ple_gate, ln3_g, ln3_b)

    trips["w_in"] = partial_sums(swap_started(["w_in"], [g_w_in], "w_in"), g_w_in)
    out = {}

    def adamw(red, tag):
        names = list(red)
        steps = _adamw_shards(f"adamw_{tag}", [w[k] for k in names], [red[k] for k in names], [m[k] for k in names],
                              [v[k] for k in names])
        for k, (d, nm, nv) in zip(names, steps):
            out[k] = (red[k].reshape(w[k].shape), d, nm, nv)

    adamw(reduced(trips["early"], grad_x, trips["w_in"][-1]), "early")
    adamw(reduced(trips["w_in"], out["w_o"][3], start_dep), "w_in")
    for k in TRANSPOSED:
        out[k] = tuple(jnp.swapaxes(a, 1, 2) for a in out[k])

    red3, red1, redz, redc, red_ws, red_bs = _allreduce_small(
        [stat3, stat1, zstat, cstat.reshape(N_SHARD * STAT_ROWS, FF_BLK), dws.reshape(N_HEADS * BLK, BLK), dbs])
    loss = (0.5 / D_MODEL) * jnp.sum(red3[5])
    g_conv_w = lax.dynamic_slice_in_dim(redc, chip * STAT_ROWS, STAT_ROWS, 0)
    names_small = [k for k in WEIGHTS if k not in BIG]
    out.update(_adamw_small(red3, red1, redz, g_conv_w, redc, red_ws, red_bs, {k: (w[k], m[k], v[k]) for k in names_small}))

    return (loss, grad_x[None], *[out[k][0] for k in WEIGHTS], *[out[k][1] for k in WEIGHTS],
            *[out[k][2] for k in WEIGHTS], *[out[k][3] for k in WEIGHTS])
```

```python
import functools
import math

import numpy as np
import jax
import jax.numpy as jnp
from jax import lax
from jax.experimental import pallas as pl
from jax.experimental.pallas import tpu as pltpu

F32 = jnp.float32
BF16 = jnp.bfloat16
MXU = BF16

D_MODEL = 1024
HEAD_DIM = 64
N_HEADS = 8
D_ATTN = 512
D_GMLP = 512
D_IN = 2560
DILATIONS = (1, 4, 16)
BLK = 128
ROPE_THETA = 500000.0
ROPE_DIM = 16
D_FF = 2816
D_PLE = 256
LN_EPS = 1e-5
ALPHA = 2.0 ** 0.25
NEG_INF = -1e30
N_SHARD = 4
W_IN_BLK = D_IN // N_SHARD
FF_BLK = D_FF // N_SHARD
ROW_BLK = D_MODEL // N_SHARD
ADAM_LR, ADAM_B1, ADAM_B2, ADAM_EPS, ADAM_WD, ADAM_STEP = 0.001, 0.9, 0.999, 1e-08, 0.01, 10

TM = 512
HALO = 8
VMEM_LIMIT = 56 * 1024 * 1024


def _cp(**kw):
    return pltpu.CompilerParams(vmem_limit_bytes=VMEM_LIMIT, **kw)


def _full(shape):
    n = len(shape)
    return pl.BlockSpec(shape, lambda *_: (0,) * n)


def _gelu(x):
    return 0.5 * x * (1.0 + lax.erf(x * (1.0 / math.sqrt(2.0))))


def _gelu_grad(x):
    return 0.5 * (1.0 + lax.erf(x * (1.0 / math.sqrt(2.0)))) + x * jnp.exp(-0.5 * x * x) * (1.0 / math.sqrt(2.0 * math.pi))


def _ln_fwd(r):
    mu = jnp.mean(r, axis=-1, keepdims=True)
    xc = r - mu
    var = jnp.mean(xc * xc, axis=-1, keepdims=True)
    rstd = lax.rsqrt(var + LN_EPS)
    return xc * rstd, rstd


def _ln_bwd(dy, xhat, rstd, g):
    dxh = dy * g
    m1 = jnp.mean(dxh, axis=-1, keepdims=True)
    m2 = jnp.mean(dxh * xhat, axis=-1, keepdims=True)
    return rstd * (dxh - m1 - xhat * m2)


def _dot(a, b):
    return jnp.dot(a.astype(MXU), b.astype(MXU), preferred_element_type=F32)


def _dot_nt(a, b):
    return lax.dot_general(a.astype(MXU), b.astype(MXU), (((1,), (1,)), ((), ())), preferred_element_type=F32)


def _dot_tn(a, b):
    return lax.dot_general(a.astype(MXU), b.astype(MXU), (((0,), (0,)), ((), ())), preferred_element_type=F32)


def _colsum(v):
    return jnp.sum(v, axis=0, keepdims=True)


def _rope_tables(positions, t, dep):
    inv = np.float32(ROPE_THETA) ** (-np.arange(0, ROPE_DIM, 2, dtype=np.float32) / np.float32(ROPE_DIM))
    half = ROPE_DIM // 2
    pos_rep = jnp.repeat(positions.reshape(t // 16, 16), half, axis=1)
    inv_row = jnp.asarray(np.tile(inv, 16)[None, :], F32)

    def trig_body(pos_ref, inv_ref, dep_ref, cos_ref, sin_ref):
        ang = pos_ref[...].astype(F32) * inv_ref[...]
        cos_ref[...] = jnp.cos(ang)
        sin_ref[...] = jnp.sin(ang)

    vm = pl.BlockSpec(memory_space=pltpu.VMEM)
    cos8, sin8 = pl.pallas_call(
        trig_body, name="rope_trig", in_specs=[vm, vm, pl.BlockSpec(memory_space=pl.ANY)], out_specs=[vm, vm],
        out_shape=(jax.ShapeDtypeStruct((t // 16, 128), F32), jax.ShapeDtypeStruct((t // 16, 128), F32)),
    )(pos_rep, inv_row, dep)
    cos8 = cos8.reshape(t, half)
    sin8 = sin8.reshape(t, half)

    lane = np.arange(128) % HEAD_DIM
    sel = (np.arange(half)[:, None] == (lane % half)[None, :])
    e_cos = (sel & (lane < ROPE_DIM)[None, :]).astype(np.float32)
    e_s1 = -(sel & (lane < half)[None, :]).astype(np.float32)
    e_s2 = (sel & ((lane >= half) & (lane < ROPE_DIM))[None, :]).astype(np.float32)
    ones = (lane >= ROPE_DIM).astype(np.float32)[None, :]

    def expand_body(cos_ref, sin_ref, ec_ref, e1_ref, e2_ref, ones_ref, c_ref, s1_ref, s2_ref):
        hp = lax.Precision.HIGHEST
        c_ref[...] = jnp.dot(cos_ref[...], ec_ref[...], precision=hp, preferred_element_type=F32) + ones_ref[...]
        s1_ref[...] = jnp.dot(sin_ref[...], e1_ref[...], precision=hp, preferred_element_type=F32)
        s2_ref[...] = jnp.dot(sin_ref[...], e2_ref[...], precision=hp, preferred_element_type=F32)

    tab = jax.ShapeDtypeStruct((t, 128), F32)
    return pl.pallas_call(expand_body, name="rope_expand", out_shape=(tab, tab, tab), compiler_params=_cp())(
        cos8, sin8, jnp.asarray(e_cos), jnp.asarray(e_s1), jnp.asarray(e_s2), jnp.asarray(ones))


def _tile_heads(tab):
    return jnp.concatenate([tab] * (D_ATTN // 128), axis=1)


def _rope_apply(v, c, s1, s2):
    n = v.shape[1]
    half = ROPE_DIM // 2
    return v * c + pltpu.roll(v, n - half, 1) * s1 + pltpu.roll(v, half, 1) * s2


def _rope_apply_t(g, c, s1, s2):
    n = g.shape[1]
    half = ROPE_DIM // 2
    return g * c + pltpu.roll(g * s1, half, 1) + pltpu.roll(g * s2, n - half, 1)


LANE_CHUNKS = D_ATTN // 128
HEAD_LANES = 128 // N_HEADS


def _perm_shape(t, d, w, dtype):
    return jax.ShapeDtypeStruct((d, t // d, w), dtype)


def _perm_tile(d, w):
    return pl.BlockSpec((None if d == 1 else d, TM // d, w), lambda i: (0, i, 0))


def _to_planes(ref, scr, d, n_chunks, dtype):
    for r in range(d):
        for cc in range(n_chunks):
            ref[r, :, cc * 128:(cc + 1) * 128] = scr.at[cc][pl.ds(r, TM // d, stride=d), :].astype(dtype)


def _from_planes(ref, scr, d, n_chunks, accumulate=False):
    for r in range(d):
        for cc in range(n_chunks):
            rows = scr.at[cc]
            val = ref[r, :, cc * 128:(cc + 1) * 128].astype(F32)
            if accumulate:
                rows[pl.ds(r, TM // d, stride=d), :] += val
            else:
                rows[pl.ds(r, TM // d, stride=d), :] = val


def _chunks(val):
    return [val[:, cc * 128:(cc + 1) * 128] for cc in range(val.shape[1] // 128)]


def _unchunk(scr, n_chunks, base=0):
    return jnp.concatenate([scr[base + cc] for cc in range(n_chunks)], axis=1)


def _head_expand():
    src = np.arange(128)[:, None]
    dst = np.arange(D_ATTN)[None, :]
    return jnp.asarray((src == (dst // HEAD_DIM) * HEAD_LANES).astype(np.float32))


def _head_reduce():
    src = np.arange(D_ATTN)[:, None]
    dst = np.arange(128)[None, :]
    return jnp.asarray((src // HEAD_DIM == dst // HEAD_LANES).astype(np.float32))


def _dot_select(a, sel):
    hi = a.astype(BF16)
    lo = (a - hi.astype(F32)).astype(BF16)
    sel = sel.astype(BF16)
    return jnp.dot(hi, sel, preferred_element_type=F32) + jnp.dot(lo, sel, preferred_element_type=F32)


def _qkvuz(x, w_in, c_tab, s1_tab, s2_tab, ln_z_g, ln_z_b, w_s, b_full, dep):
    t = x.shape[0]
    nchunk = TM // BLK

    def body(x_ref, w_ref, c_ref, s1_ref, s2_ref, g_ref, b_ref, ws_ref, bf_ref, dep_ref,
             qkv1_ref, qkv4_ref, qkv16_ref, hu_ref, hz_ref, mixed_ref, gm_ref, h_scr, wm_scr, p_scr):
        @pl.when(pl.program_id(0) == 0)
        def _():
            row = lax.broadcasted_iota(jnp.int32, (BLK, BLK), 0)
            col = lax.broadcasted_iota(jnp.int32, (BLK, BLK), 1)
            for g in range(N_HEADS):
                wm_scr[g] = jnp.where(col <= row, ws_ref[g], 0.0).astype(MXU)

        xb = x_ref[...].astype(MXU)
        for j in range(N_SHARD):
            h_scr[:, j * W_IN_BLK:(j + 1) * W_IN_BLK] = jnp.dot(xb, w_ref[j], preferred_element_type=F32)
        c, s1, s2 = _tile_heads(c_ref[...]), _tile_heads(s1_ref[...]), _tile_heads(s2_ref[...])
        q = _rope_apply(h_scr[:, 0:D_ATTN], c, s1, s2) * (1.0 / math.sqrt(HEAD_DIM))
        k = _rope_apply(h_scr[:, D_ATTN:2 * D_ATTN], c, s1, s2)
        for part, val in enumerate((q, k, h_scr[:, 2 * D_ATTN:3 * D_ATTN])):
            qkv1_ref[:, part * D_ATTN:(part + 1) * D_ATTN] = val.astype(MXU)
            for cc in range(LANE_CHUNKS):
                p_scr[part * LANE_CHUNKS + cc] = val[:, cc * 128:(cc + 1) * 128]
        _to_planes(qkv4_ref, p_scr, DILATIONS[1], 3 * LANE_CHUNKS, MXU)
        _to_planes(qkv16_ref, p_scr, DILATIONS[2], 3 * LANE_CHUNKS, MXU)
        hu = h_scr[:, 3 * D_ATTN:3 * D_ATTN + D_GMLP]
        hz = h_scr[:, 3 * D_ATTN + D_GMLP:]
        hu_ref[...] = hu
        hz_ref[...] = hz
        zhat, _ = _ln_fwd(_gelu(hz))
        zn = (zhat * g_ref[...] + b_ref[...]).astype(MXU)
        for ch in range(nchunk):
            rows = slice(ch * BLK, (ch + 1) * BLK)
            for g in range(N_HEADS):
                cols = slice(g * HEAD_DIM, (g + 1) * HEAD_DIM)
                mixed_ref[rows, cols] = jnp.dot(wm_scr[g], zn[rows, cols], preferred_element_type=F32) + bf_ref[:, cols]
        gm_ref[...] = (_gelu(hu) * mixed_ref[...]).astype(MXU)

    tok = lambda w: pl.BlockSpec((TM, w), lambda i: (i, 0))
    outs = [_perm_shape(t, d, 3 * D_ATTN, MXU) for d in DILATIONS] + [jax.ShapeDtypeStruct((t, D_GMLP), F32)] * 3 + [
        jax.ShapeDtypeStruct((t, D_GMLP), MXU)]
    return pl.pallas_call(
        body, name="qkvuz", grid=(t // TM,),
        in_specs=[tok(D_MODEL), _full(w_in.shape), tok(128), tok(128), tok(128), _full(ln_z_g.shape), _full(ln_z_b.shape),
                  _full(w_s.shape), _full(b_full.shape), pl.BlockSpec(memory_space=pl.ANY)],
        out_specs=[_perm_tile(d, 3 * D_ATTN) for d in DILATIONS] + [tok(D_ATTN)] * 4, out_shape=outs,
        scratch_shapes=[pltpu.VMEM((TM, D_IN), F32), pltpu.VMEM((N_HEADS, BLK, BLK), MXU),
                        pltpu.VMEM((3 * LANE_CHUNKS, TM, 128), F32)],
        compiler_params=_cp(dimension_semantics=("arbitrary",)),
    )(x, w_in, c_tab, s1_tab, s2_tab, ln_z_g, ln_z_b, w_s, b_full, dep)


def _band_valid(n):
    i = lax.broadcasted_iota(jnp.int32, (BLK, 2 * BLK), 0)
    j = lax.broadcasted_iota(jnp.int32, (BLK, 2 * BLK), 1)
    return (j >= i) & (j <= i + BLK) & ((j >= BLK) | (n > 0))


def _attn_fwd(qkv, d):
    _, l_sub, _ = qkv.shape
    nb = l_sub // BLK

    def body(q_ref, kp_ref, kc_ref, vp_ref, vc_ref, o_ref, l_ref):
        valid = _band_valid(pl.program_id(1))
        kcat = jnp.concatenate([kp_ref[...], kc_ref[...]], axis=0)
        vcat = jnp.concatenate([vp_ref[...], vc_ref[...]], axis=0)
        for h in range(N_HEADS):
            cols = slice(h * HEAD_DIM, (h + 1) * HEAD_DIM)
            s = jnp.where(valid, _dot_nt(q_ref[:, cols], kcat[:, cols]), NEG_INF)
            m = jnp.max(s, axis=-1, keepdims=True)
            e = jnp.exp(s - m)
            den = jnp.sum(e, axis=-1, keepdims=True)
            o_ref[:, cols] = _dot(e, vcat[:, cols]) * (1.0 / den)
            l_ref[:, h * HEAD_LANES:(h + 1) * HEAD_LANES] = jnp.broadcast_to(m + jnp.log(den), (BLK, HEAD_LANES))

    def blk(w, col, prev=False):
        return pl.BlockSpec((None, BLK, w), lambda r, n: (r, jnp.maximum(n - 1, 0) if prev else n, col))

    return pl.pallas_call(
        body, name=f"attn_fwd_d{d}", grid=(d, nb),
        in_specs=[blk(D_ATTN, 0), blk(D_ATTN, 1, True), blk(D_ATTN, 1), blk(D_ATTN, 2, True), blk(D_ATTN, 2)],
        out_specs=[blk(D_ATTN, 0), blk(128, 0)],
        out_shape=[jax.ShapeDtypeStruct((d, l_sub, D_ATTN), F32), jax.ShapeDtypeStruct((d, l_sub, 128), F32)],
        compiler_params=_cp(dimension_semantics=("arbitrary", "arbitrary")),
    )(qkv, qkv, qkv, qkv, qkv)


def _attn_bwd(qkv, do, lse, delta, d, dep):
    _, l_sub, _ = qkv.shape
    nb = l_sub // BLK
    whole = l_sub <= 8 * BLK

    def shares(n, q_ref, kp_ref, kc_ref, vp_ref, vc_ref, do_ref, l_ref, dl_ref, dq_ref):
        valid = _band_valid(n)
        kcat = jnp.concatenate([kp_ref[...], kc_ref[...]], axis=0)
        vcat = jnp.concatenate([vp_ref[...], vc_ref[...]], axis=0)
        for h in range(N_HEADS):
            cols = slice(h * HEAD_DIM, (h + 1) * HEAD_DIM)
            stat = slice(h * HEAD_LANES, h * HEAD_LANES + 1)
            qh, doh = q_ref[:, cols], do_ref[:, cols]
            p = jnp.where(valid, jnp.exp(_dot_nt(qh, kcat[:, cols]) - l_ref[:, stat]), 0.0)
            ds = p * (_dot_nt(doh, vcat[:, cols]) - dl_ref[:, stat])
            dq_ref[:, cols] = _dot(ds, kcat[:, cols])
            yield cols, _dot_tn(ds, qh), _dot_tn(p, doh)

    def body_whole(*refs):
        dk_ref, dv_ref = refs[10:]
        n = pl.program_id(1)
        cur = pl.ds(pl.multiple_of(n * BLK, BLK), BLK)
        prev = pl.ds(pl.multiple_of(jnp.maximum(n - 1, 0) * BLK, BLK), BLK)
        for cols, dk2, dv2 in shares(n, *refs[:8], refs[9]):
            dk_ref[cur, cols] = dk2[BLK:]
            dv_ref[cur, cols] = dv2[BLK:]
            dk_ref[prev, cols] += dk2[0:BLK]
            dv_ref[prev, cols] += dv2[0:BLK]

    def body_carry(*refs):
        dk_ref, dv_ref, ck_scr, cv_scr = refs[10:]
        n = pl.program_id(1)

        @pl.when(n == 0)
        def _():
            ck_scr[...] = jnp.zeros_like(ck_scr)
            cv_scr[...] = jnp.zeros_like(cv_scr)

        @pl.when(n < nb)
        def _():
            for cols, dk2, dv2 in shares(n, *refs[:8], refs[9]):
                dk_ref[:, cols] = ck_scr[:, cols] + dk2[0:BLK]
                dv_ref[:, cols] = cv_scr[:, cols] + dv2[0:BLK]
                ck_scr[:, cols] = dk2[BLK:]
                cv_scr[:, cols] = dv2[BLK:]

        @pl.when(n == nb)
        def _():
            dk_ref[...] = ck_scr[...]
            dv_ref[...] = cv_scr[...]

    def blk(w, col, shift=0):
        return pl.BlockSpec((None, BLK, w), lambda r, n: (r, jnp.clip(n - shift, 0, nb - 1), col))

    if whole:
        dkv_spec = pl.BlockSpec((None, l_sub, D_ATTN), lambda r, n: (r, 0, 0))
        body, steps, scratch = body_whole, nb, []
    else:
        dkv_spec = blk(D_ATTN, 0, 1)
        body, steps, scratch = body_carry, nb + 1, [pltpu.VMEM((BLK, D_ATTN), F32)] * 2
    return pl.pallas_call(
        body, name=f"attn_bwd_d{d}", grid=(d, steps),
        in_specs=[blk(D_ATTN, 0), blk(D_ATTN, 1, 1), blk(D_ATTN, 1), blk(D_ATTN, 2, 1), blk(D_ATTN, 2),
                  blk(D_ATTN, 0), blk(128, 0), blk(128, 0), pl.BlockSpec(memory_space=pl.ANY)],
        out_specs=[blk(D_ATTN, 0), dkv_spec, dkv_spec],
        out_shape=[jax.ShapeDtypeStruct((d, l_sub, D_ATTN), F32)] * 3,
        scratch_shapes=scratch,
        compiler_params=_cp(dimension_semantics=("arbitrary", "arbitrary")),
    )(qkv, qkv, qkv, qkv, qkv, do, lse, delta, dep)


def _mix_ln1(os_, ls_, gm, x, w_o, ln1_g, ln1_b):
    t = x.shape[0]
    expand = _head_expand()

    def body(o1, o4, o16, l1, l4, l16, gm_ref, x_ref, wo_ref, g_ref, b_ref, ex_ref,
             attn_ref, lse1_ref, lse4_ref, lse16_ref, cat_ref, xhat_ref, rstd_ref, x1b_ref, o_scr, l_scr):
        _from_planes(o4, o_scr, DILATIONS[1], LANE_CHUNKS)
        _from_planes(o16, o_scr.at[pl.ds(LANE_CHUNKS, LANE_CHUNKS)], DILATIONS[2], LANE_CHUNKS)
        _from_planes(l4, l_scr, DILATIONS[1], 1)
        _from_planes(l16, l_scr.at[pl.ds(1, 1)], DILATIONS[2], 1)
        la, lb, lc = l1[...], l_scr[0], l_scr[1]
        m = jnp.maximum(jnp.maximum(la, lb), lc)
        ea, eb, ec = jnp.exp(la - m), jnp.exp(lb - m), jnp.exp(lc - m)
        den = ea + eb + ec
        inv = 1.0 / den
        wide = lambda w: _dot_select(w, ex_ref[...])
        attn = (wide(ea * inv) * o1[...] + wide(eb * inv) * _unchunk(o_scr, LANE_CHUNKS)
                + wide(ec * inv) * _unchunk(o_scr, LANE_CHUNKS, LANE_CHUNKS))
        attn_ref[...] = attn
        lse = m + jnp.log(den)
        lse1_ref[...] = lse
        l_scr[2] = lse
        _to_planes(lse4_ref, l_scr.at[pl.ds(2, 1)], DILATIONS[1], 1, F32)
        _to_planes(lse16_ref, l_scr.at[pl.ds(2, 1)], DILATIONS[2], 1, F32)
        cat_ref[:, 0:D_ATTN] = attn.astype(MXU)
        cat_ref[:, D_ATTN:] = gm_ref[...]
        mix = jnp.dot(cat_ref[...], wo_ref[...], preferred_element_type=F32)
        xhat, rstd = _ln_fwd(ALPHA * x_ref[...] + mix)
        xhat_ref[...] = xhat
        rstd_ref[...] = rstd
        x1b_ref[...] = (xhat * g_ref[...] + b_ref[...]).astype(MXU)

    tok = lambda w: pl.BlockSpec((TM, w), lambda i: (i, 0))
    outs = [jax.ShapeDtypeStruct((t, D_ATTN), F32)] + [_perm_shape(t, d, 128, F32) for d in DILATIONS] + [
        jax.ShapeDtypeStruct((t, D_MODEL), MXU), jax.ShapeDtypeStruct((t, D_MODEL), F32), jax.ShapeDtypeStruct((t, 1), F32),
        jax.ShapeDtypeStruct((t, D_MODEL), MXU)]
    return pl.pallas_call(
        body, name="mix_ln1", grid=(t // TM,),
        in_specs=[_perm_tile(d, D_ATTN) for d in DILATIONS] + [_perm_tile(d, 128) for d in DILATIONS]
        + [tok(D_GMLP), tok(D_MODEL), _full(w_o.shape), _full(ln1_g.shape), _full(ln1_b.shape), _full(expand.shape)],
        out_specs=[tok(D_ATTN)] + [_perm_tile(d, 128) for d in DILATIONS] + [tok(D_MODEL), tok(D_MODEL), tok(1), tok(D_MODEL)],
        out_shape=outs,
        scratch_shapes=[pltpu.VMEM((2 * LANE_CHUNKS, TM, 128), F32), pltpu.VMEM((3, TM, 128), F32)],
        compiler_params=_cp(dimension_semantics=("arbitrary",)),
    )(*os_, *ls_, gm, x, w_o, ln1_g, ln1_b, expand)


def _conv_fwd(a_ext, w_ref, b_ref, rows):
    return (b_ref[...] + w_ref[2:3, :] * a_ext[HALO:HALO + rows] + w_ref[1:2, :] * a_ext[HALO - 1:HALO - 1 + rows]
            + w_ref[0:1, :] * a_ext[HALO - 2:HALO - 2 + rows])


def _ffn_in(x1b, w_a, w_b, conv_w, conv_b):
    t = x1b.shape[0]
    hb = TM // HALO

    def body(x_ref, xh_ref, wa_ref, wb_ref, cw_ref, cb_ref, apre_ref, b_ref, f_ref):
        i = pl.program_id(1)
        a_pre = _dot_nt(x_ref[...], wa_ref[...])
        a_halo = jnp.where(i > 0, _dot_nt(xh_ref[...], wa_ref[...]), 0.0)
        a = _conv_fwd(jnp.concatenate([a_halo, a_pre], axis=0), cw_ref, cb_ref, TM)
        b = _dot_nt(x_ref[...], wb_ref[...])
        apre_ref[...] = a_pre
        b_ref[...] = b
        f_ref[...] = (_gelu(a) * b).astype(MXU)

    blk = lambda r, c: pl.BlockSpec((None, r, c), lambda j, i: (j, 0, 0))
    tokj = pl.BlockSpec((None, TM, FF_BLK), lambda j, i: (j, i, 0))
    outs = [jax.ShapeDtypeStruct((N_SHARD, t, FF_BLK), F32)] * 2 + [jax.ShapeDtypeStruct((N_SHARD, t, FF_BLK), MXU)]
    return pl.pallas_call(
        body, name="ffn_in", grid=(N_SHARD, t // TM),
        in_specs=[pl.BlockSpec((TM, D_MODEL), lambda j, i: (i, 0)),
                  pl.BlockSpec((HALO, D_MODEL), lambda j, i: (jnp.maximum(i * hb - 1, 0), 0)),
                  blk(FF_BLK, D_MODEL), blk(FF_BLK, D_MODEL), blk(3, FF_BLK), blk(1, FF_BLK)],
        out_specs=[tokj, tokj, tokj], out_shape=outs,
        compiler_params=_cp(dimension_semantics=("arbitrary", "arbitrary")),
    )(x1b, x1b, w_a, w_b, conv_w, conv_b)


def _ffn_out_ln2(f, w_down, xhat1, ln1_g, ln1_b, ln2_g, ln2_b):
    t = xhat1.shape[0]

    def body(f_ref, wd_ref, xh_ref, g1_ref, b1_ref, g2_ref, b2_ref, xhat_ref, rstd_ref, x2b_ref):
        ff = jnp.dot(f_ref[0], wd_ref[0], preferred_element_type=F32)
        for j in range(1, N_SHARD):
            ff = ff + jnp.dot(f_ref[j], wd_ref[j], preferred_element_type=F32)
        x1 = xh_ref[...] * g1_ref[...] + b1_ref[...]
        xhat, rstd = _ln_fwd(ALPHA * x1 + ff)
        xhat_ref[...] = xhat
        rstd_ref[...] = rstd
        x2b_ref[...] = (xhat * g2_ref[...] + b2_ref[...]).astype(MXU)

    tok = lambda w: pl.BlockSpec((TM, w), lambda i: (i, 0))
    vec = _full((1, D_MODEL))
    outs = [jax.ShapeDtypeStruct((t, D_MODEL), F32), jax.ShapeDtypeStruct((t, 1), F32), jax.ShapeDtypeStruct((t, D_MODEL), MXU)]
    return pl.pallas_call(
        body, name="ffn_out_ln2", grid=(t // TM,),
        in_specs=[pl.BlockSpec((N_SHARD, TM, FF_BLK), lambda i: (0, i, 0)), _full(w_down.shape), tok(D_MODEL), vec, vec, vec, vec],
        out_specs=[tok(D_MODEL), tok(1), tok(D_MODEL)], out_shape=outs,
        compiler_params=_cp(dimension_semantics=("arbitrary",)),
    )(f, w_down, xhat1, ln1_g, ln1_b, ln2_g, ln2_b)


STAT_ROWS = 8


def _ple_loss_bwd(xhat2, rstd2, p, target, ln2_g, ln2_b, w_g, b_g, w_p, ln3_g, ln3_b):
    t = xhat2.shape[0]

    def body(xh2_ref, rs2_ref, p_ref, t_ref, g2_ref, b2_ref, wg_ref, bg_ref, wp_ref, g3_ref, b3_ref,
             dr2_ref, dgp_ref, dpp_ref, stat_ref, pp_scr):
        @pl.when(pl.program_id(0) == 0)
        def _():
            stat_ref[...] = jnp.zeros_like(stat_ref)

        xhat2 = xh2_ref[...]
        x2 = xhat2 * g2_ref[...] + b2_ref[...]
        gate = jax.nn.sigmoid(jnp.dot(x2.astype(MXU), wg_ref[...], preferred_element_type=F32) + bg_ref[...])
        pb = p_ref[...].astype(MXU)
        for j in range(N_SHARD):
            pp_scr[:, j * ROW_BLK:(j + 1) * ROW_BLK] = jnp.dot(pb, wp_ref[j], preferred_element_type=F32)
        pp = pp_scr[...]
        xhat3, rstd3 = _ln_fwd(ALPHA * x2 + gate * pp)
        err = xhat3 * g3_ref[...] + b3_ref[...] - t_ref[...]
        dy = err * (1.0 / D_MODEL)
        dr3 = _ln_bwd(dy, xhat3, rstd3, g3_ref[...])
        dgp = dr3 * pp * gate * (1.0 - gate)
        dgp_ref[...] = dgp.astype(MXU)
        dpp_ref[...] = (dr3 * gate).astype(MXU)
        dx2 = ALPHA * dr3 + _dot_nt(dgp, wg_ref[...])
        dr2_ref[...] = _ln_bwd(dx2, xhat2, rs2_ref[...], g2_ref[...])
        stat_ref[0:1, :] += _colsum(dy * xhat3)
        stat_ref[1:2, :] += _colsum(dy)
        stat_ref[2:3, :] += _colsum(dgp)
        stat_ref[3:4, :] += _colsum(dx2 * xhat2)
        stat_ref[4:5, :] += _colsum(dx2)
        stat_ref[5:6, :] += _colsum(err * err)

    tok = lambda w: pl.BlockSpec((TM, w), lambda i: (i, 0))
    vec = _full((1, D_MODEL))
    outs = [jax.ShapeDtypeStruct((t, D_MODEL), F32), jax.ShapeDtypeStruct((t, D_MODEL), MXU), jax.ShapeDtypeStruct((t, D_MODEL), MXU),
            jax.ShapeDtypeStruct((STAT_ROWS, D_MODEL), F32)]
    return pl.pallas_call(
        body, name="ple_loss_bwd", grid=(t // TM,),
        in_specs=[tok(D_MODEL), tok(1), tok(D_PLE), tok(D_MODEL), vec, vec, _full(w_g.shape), vec, _full(w_p.shape), vec, vec],
        out_specs=[tok(D_MODEL), tok(D_MODEL), tok(D_MODEL), _full((STAT_ROWS, D_MODEL))], out_shape=outs,
        scratch_shapes=[pltpu.VMEM((TM, D_MODEL), F32)],
        compiler_params=_cp(dimension_semantics=("arbitrary",)),
    )(xhat2, rstd2, p, target, ln2_g, ln2_b, w_g, b_g, w_p, ln3_g, ln3_b)


def _ffn_bwd(dr2, a_pre, b, w_down, w_a, w_b, conv_w, conv_b, xhat1, rstd1, ln1_g):
    t = dr2.shape[0]
    nt = t // TM
    hb = TM // HALO
    last_h = t // HALO - 1

    def body(dr_ref, drn_ref, ap_ref, app_ref, apn_ref, b_ref, bn_ref, wd_ref, wa_ref, wb_ref, cw_ref, cb_ref,
             xh_ref, rs_ref, g1_ref, dap_ref, dbb_ref, dr1_ref, cstat_ref, lstat_ref, acc_scr):
        i, j = pl.program_id(0), pl.program_id(1)

        @pl.when((i == 0) & (j == 0))
        def _():
            cstat_ref[...] = jnp.zeros_like(cstat_ref)
            lstat_ref[...] = jnp.zeros_like(lstat_ref)

        ext = TM + HALO
        dr_ext = jnp.concatenate([dr_ref[...], drn_ref[...]], axis=0)
        df = _dot_nt(dr_ext, wd_ref[...])
        a_all = jnp.concatenate([jnp.where(i > 0, app_ref[...], 0.0), ap_ref[...], apn_ref[...]], axis=0)
        a = _conv_fwd(a_all, cw_ref, cb_ref, ext)
        b_ext = jnp.concatenate([b_ref[...], bn_ref[...]], axis=0)
        row = lax.broadcasted_iota(jnp.int32, (ext, 1), 0)
        da = jnp.where((row < TM) | (i < nt - 1), df * b_ext * _gelu_grad(a), 0.0)
        dbb = df[0:TM] * _gelu(a[0:TM])
        da_pre = cw_ref[2:3, :] * da[0:TM] + cw_ref[1:2, :] * da[1:TM + 1] + cw_ref[0:1, :] * da[2:TM + 2]
        dap_ref[...] = da_pre.astype(MXU)
        dbb_ref[...] = dbb.astype(MXU)
        da_m = da[0:TM]
        for kk in range(3):
            cstat_ref[j, kk:kk + 1, :] += _colsum(da_m * a_all[HALO - 2 + kk:HALO - 2 + kk + TM])
        cstat_ref[j, 3:4, :] += _colsum(da_m)
        part = _dot(da_pre, wa_ref[...]) + _dot(dbb, wb_ref[...])

        @pl.when(j == 0)
        def _():
            acc_scr[...] = ALPHA * dr_ref[...] + part

        @pl.when(j > 0)
        def _():
            acc_scr[...] += part

        @pl.when(j == N_SHARD - 1)
        def _():
            dx1 = acc_scr[...]
            xhat1 = xh_ref[...]
            lstat_ref[0:1, :] += _colsum(dx1 * xhat1)
            lstat_ref[1:2, :] += _colsum(dx1)
            dr1_ref[...] = _ln_bwd(dx1, xhat1, rs_ref[...], g1_ref[...])

    tok = lambda w: pl.BlockSpec((TM, w), lambda i, j: (i, 0))
    tokj = pl.BlockSpec((None, TM, FF_BLK), lambda i, j: (j, i, 0))
    prevj = pl.BlockSpec((None, HALO, FF_BLK), lambda i, j: (j, jnp.maximum(i * hb - 1, 0), 0))
    nextj = pl.BlockSpec((None, HALO, FF_BLK), lambda i, j: (j, jnp.minimum((i + 1) * hb, last_h), 0))
    blk = lambda r, c: pl.BlockSpec((None, r, c), lambda i, j: (j, 0, 0))
    outs = [jax.ShapeDtypeStruct((N_SHARD, t, FF_BLK), MXU)] * 2 + [
        jax.ShapeDtypeStruct((t, D_MODEL), F32), jax.ShapeDtypeStruct((N_SHARD, STAT_ROWS, FF_BLK), F32),
        jax.ShapeDtypeStruct((STAT_ROWS, D_MODEL), F32)]
    return pl.pallas_call(
        body, name="ffn_bwd", grid=(nt, N_SHARD),
        in_specs=[tok(D_MODEL), pl.BlockSpec((HALO, D_MODEL), lambda i, j: (jnp.minimum((i + 1) * hb, last_h), 0)),
                  tokj, prevj, nextj, tokj, nextj, blk(FF_BLK, D_MODEL), blk(FF_BLK, D_MODEL), blk(FF_BLK, D_MODEL),
                  blk(3, FF_BLK), blk(1, FF_BLK), tok(D_MODEL), tok(1), _full((1, D_MODEL))],
        out_specs=[tokj, tokj, tok(D_MODEL), _full((N_SHARD, STAT_ROWS, FF_BLK)), _full((STAT_ROWS, D_MODEL))], out_shape=outs,
        scratch_shapes=[pltpu.VMEM((TM, D_MODEL), F32)],
        compiler_params=_cp(dimension_semantics=("arbitrary", "arbitrary")),
    )(dr2, dr2, a_pre, a_pre, a_pre, b, b, w_down, w_a, w_b, conv_w, conv_b, xhat1, rstd1, ln1_g)


def _mix_bwd(dr1, w_o, hu, hz, mixed, attn, ln_z_g, ln_z_b, w_s, dep):
    t = dr1.shape[0]
    nchunk = TM // BLK

    def body(dr_ref, wo_ref, hu_ref, hz_ref, mx_ref, attn_ref, g_ref, b_ref, ws_ref, grp_ref, red_ref, dep_ref,
             do1_ref, do4_ref, do16_ref, dl1_ref, dl4_ref, dl16_ref, duz_ref, dws_ref, dbs_ref, zstat_ref,
             wm_scr, dzn_scr, dbsum_scr, do_scr, dl_scr):
        @pl.when(pl.program_id(0) == 0)
        def _():
            row = lax.broadcasted_iota(jnp.int32, (BLK, BLK), 0)
            col = lax.broadcasted_iota(jnp.int32, (BLK, BLK), 1)
            for g in range(N_HEADS):
                wm_scr[g] = jnp.where(col <= row, ws_ref[g], 0.0).astype(MXU)
            dws_ref[...] = jnp.zeros_like(dws_ref)
            dbsum_scr[...] = jnp.zeros_like(dbsum_scr)
            zstat_ref[...] = jnp.zeros_like(zstat_ref)

        dcat = _dot_nt(dr_ref[...], wo_ref[...])
        dattn = dcat[:, 0:D_ATTN]
        do1_ref[...] = dattn.astype(MXU)
        for cc, val in enumerate(_chunks(dattn)):
            do_scr[cc] = val
        _to_planes(do4_ref, do_scr, DILATIONS[1], LANE_CHUNKS, MXU)
        _to_planes(do16_ref, do_scr, DILATIONS[2], LANE_CHUNKS, MXU)
        delta = _dot_select(dattn * attn_ref[...], red_ref[...])
        dl1_ref[...] = delta
        dl_scr[0] = delta
        _to_planes(dl4_ref, dl_scr, DILATIONS[1], 1, F32)
        _to_planes(dl16_ref, dl_scr, DILATIONS[2], 1, F32)
        dgm = dcat[:, D_ATTN:]
        hu, hz = hu_ref[...], hz_ref[...]
        u = _gelu(hu)
        duz_ref[:, 0:D_GMLP] = (dgm * mx_ref[...] * _gelu_grad(hu)).astype(MXU)
        dmixed = dgm * u
        dmb = dmixed.astype(MXU)
        zhat, rstd = _ln_fwd(_gelu(hz))
        znb = (zhat * g_ref[...] + b_ref[...]).astype(MXU)
        dbs_acc = jnp.zeros((BLK, D_GMLP), F32)
        for ch in range(nchunk):
            rows = slice(ch * BLK, (ch + 1) * BLK)
            dbs_acc = dbs_acc + dmixed[rows]
            for g in range(N_HEADS):
                cols = slice(g * HEAD_DIM, (g + 1) * HEAD_DIM)
                dzn_scr[rows, cols] = _dot_tn(wm_scr[g], dmb[rows, cols])
                dws_ref[g] += _dot_nt(dmb[rows, cols], znb[rows, cols])
        dbsum_scr[...] += dbs_acc
        dzn = dzn_scr[...]
        zstat_ref[0:1, :] += _colsum(dzn * zhat)
        zstat_ref[1:2, :] += _colsum(dzn)
        duz_ref[:, D_GMLP:] = (_ln_bwd(dzn, zhat, rstd, g_ref[...]) * _gelu_grad(hz)).astype(MXU)

        @pl.when(pl.program_id(0) == nt - 1)
        def _():
            row = lax.broadcasted_iota(jnp.int32, (BLK, BLK), 0)
            col = lax.broadcasted_iota(jnp.int32, (BLK, BLK), 1)
            for g in range(N_HEADS):
                dws_ref[g] = jnp.where(col <= row, dws_ref[g], 0.0)
            dbs_ref[...] = lax.dot_general(grp_ref[...], dbsum_scr[...], (((1,), (1,)), ((), ())),
                                           precision=lax.Precision.HIGHEST, preferred_element_type=F32)

    nt = t // TM
    tok = lambda w: pl.BlockSpec((TM, w), lambda i: (i, 0))
    grp = jnp.asarray((np.arange(D_GMLP)[None, :] // HEAD_DIM == np.arange(N_HEADS)[:, None]).astype(np.float32))
    red = _head_reduce()
    outs = [_perm_shape(t, d, D_ATTN, MXU) for d in DILATIONS] + [_perm_shape(t, d, 128, F32) for d in DILATIONS] + [
        jax.ShapeDtypeStruct((t, 2 * D_GMLP), MXU),
        jax.ShapeDtypeStruct((N_HEADS, BLK, BLK), F32), jax.ShapeDtypeStruct((N_HEADS, BLK), F32),
        jax.ShapeDtypeStruct((STAT_ROWS, D_GMLP), F32)]
    return pl.pallas_call(
        body, name="mix_bwd", grid=(t // TM,),
        in_specs=[tok(D_MODEL), _full(w_o.shape), tok(D_GMLP), tok(D_GMLP), tok(D_GMLP), tok(D_ATTN), _full(ln_z_g.shape),
                  _full(ln_z_b.shape), _full(w_s.shape), _full(grp.shape), _full(red.shape), pl.BlockSpec(memory_space=pl.ANY)],
        out_specs=[_perm_tile(d, D_ATTN) for d in DILATIONS] + [_perm_tile(d, 128) for d in DILATIONS]
        + [tok(2 * D_GMLP), _full((N_HEADS, BLK, BLK)), _full((N_HEADS, BLK)), _full((STAT_ROWS, D_GMLP))],
        out_shape=outs,
        scratch_shapes=[pltpu.VMEM((N_HEADS, BLK, BLK), MXU), pltpu.VMEM((TM, D_GMLP), F32), pltpu.VMEM((BLK, D_GMLP), F32),
                        pltpu.VMEM((LANE_CHUNKS, TM, 128), F32), pltpu.VMEM((1, TM, 128), F32)],
        compiler_params=_cp(dimension_semantics=("arbitrary",)),
    )(dr1, w_o, hu, hz, mixed, attn, ln_z_g, ln_z_b, w_s, grp, red, dep)


def _dx_in(dqs, dks, dvs, duz, dr1, w_in, c_tab, s1_tab, s2_tab):
    t = dr1.shape[0]

    def body(dq1, dq4, dq16, dk1, dk4, dk16, dv1, dv4, dv16, duz_ref, dr_ref, w_ref, c_ref, s1_ref, s2_ref,
             dh_ref, dx_ref, acc_scr):
        sums = []
        for part, (g1, g4, g16) in enumerate(((dq1, dq4, dq16), (dk1, dk4, dk16), (dv1, dv4, dv16))):
            acc = acc_scr.at[pl.ds(part * LANE_CHUNKS, LANE_CHUNKS)]
            for cc in range(LANE_CHUNKS):
                acc[cc] = g1[:, cc * 128:(cc + 1) * 128]
            _from_planes(g4, acc, DILATIONS[1], LANE_CHUNKS, accumulate=True)
            _from_planes(g16, acc, DILATIONS[2], LANE_CHUNKS, accumulate=True)
            sums.append(_unchunk(acc_scr, LANE_CHUNKS, part * LANE_CHUNKS))
        c, s1, s2 = _tile_heads(c_ref[...]), _tile_heads(s1_ref[...]), _tile_heads(s2_ref[...])
        dh_ref[:, 0:D_ATTN] = _rope_apply_t(sums[0] * (1.0 / math.sqrt(HEAD_DIM)), c, s1, s2).astype(MXU)
        dh_ref[:, D_ATTN:2 * D_ATTN] = _rope_apply_t(sums[1], c, s1, s2).astype(MXU)
        dh_ref[:, 2 * D_ATTN:3 * D_ATTN] = sums[2].astype(MXU)
        dh_ref[:, 3 * D_ATTN:] = duz_ref[...]
        dx = ALPHA * dr_ref[...]
        for j in range(N_SHARD):
            dx = dx + _dot_nt(dh_ref[:, j * W_IN_BLK:(j + 1) * W_IN_BLK], w_ref[j])
        dx_ref[...] = dx

    tok = lambda w: pl.BlockSpec((TM, w), lambda i: (i, 0))
    outs = [jax.ShapeDtypeStruct((t, D_IN), MXU), jax.ShapeDtypeStruct((t, D_MODEL), F32)]
    return pl.pallas_call(
        body, name="dx_in", grid=(t // TM,),
        in_specs=[_perm_tile(d, D_ATTN) for d in DILATIONS] * 3
        + [tok(2 * D_GMLP), tok(D_MODEL), _full(w_in.shape), tok(128), tok(128), tok(128)],
        out_specs=[tok(D_IN), tok(D_MODEL)], out_shape=outs,
        scratch_shapes=[pltpu.VMEM((3 * LANE_CHUNKS, TM, 128), F32)],
        compiler_params=_cp(dimension_semantics=("arbitrary",)),
    )(*dqs, *dks, *dvs, duz, dr1, w_in, c_tab, s1_tab, s2_tab)


def _wgrad(name, x, dy, x_spec, dy_spec, out_spec, out_shape, grid):
    def body(x_ref, dy_ref, o_ref):
        o_ref[...] = _dot_tn(x_ref[...], dy_ref[...])

    return pl.pallas_call(
        body, name=name, grid=grid, in_specs=[x_spec, dy_spec], out_specs=out_spec,
        out_shape=jax.ShapeDtypeStruct(out_shape, F32),
        compiler_params=_cp(dimension_semantics=("arbitrary",) * len(grid)),
    )(x, dy)


def _local_step(x, p, rope, target, w_in, start_dep, late_weights, early_grads, early_grads_sent,
                ln_z_g, ln_z_b, w_s, b_s, ln1_g, ln1_b, conv_b, ln2_g, ln2_b, b_g, ln3_g, ln3_b):
    t = x.shape[0]
    half = TM
    c_tab, s1_tab, s2_tab = rope
    b_full = jnp.repeat(jnp.transpose(b_s[0]), HEAD_DIM, axis=1)
    conv_b4 = conv_b.reshape(N_SHARD, 1, FF_BLK)
    *qkvs, hu, hz, mixed, gm = _qkvuz(x, w_in, c_tab, s1_tab, s2_tab, ln_z_g, ln_z_b, w_s[0], b_full, start_dep)
    branches = [_attn_fwd(qkv, d) for qkv, d in zip(qkvs, DILATIONS)]
    w_o, w_a, w_b, conv_w, w_down, w_g, w_p = late_weights(branches[-1][1])
    attn, *lses, cat, xhat1, rstd1, x1b = _mix_ln1(
        [o for o, _ in branches], [l for _, l in branches], gm, x, w_o, ln1_g, ln1_b)
    a_pre, b_act, f = _ffn_in(x1b, w_a, w_b, conv_w, conv_b4)
    xhat2, rstd2, x2b = _ffn_out_ln2(f, w_down, xhat1, ln1_g, ln1_b, ln2_g, ln2_b)
    dr2, dgp, dpp, stat3 = _ple_loss_bwd(xhat2, rstd2, p, target, ln2_g, ln2_b, w_g, b_g, w_p, ln3_g, ln3_b)
    da_pre, dbb, dr1, cstat, stat1 = _ffn_bwd(dr2, a_pre, b_act, w_down, w_a, w_b, conv_w, conv_b4, xhat1, rstd1, ln1_g)

    full_t = lambda w, im: pl.BlockSpec((t, w), im)
    ffj = pl.BlockSpec((None, t, FF_BLK), lambda j, kk: (j, 0, 0))
    early = dict(
        w_ple_gate=_wgrad("dw_g", x2b, dgp, full_t(half, lambda kk, n: (0, kk)), full_t(half, lambda kk, n: (0, n)),
                          pl.BlockSpec((half, half), lambda kk, n: (kk, n)), (D_MODEL, D_MODEL), (2, 2)),
        w_ple_in=_wgrad("dw_p", p, dpp, full_t(D_PLE, lambda j: (0, 0)), full_t(ROW_BLK, lambda j: (0, j)),
                        pl.BlockSpec((None, D_PLE, ROW_BLK), lambda j: (j, 0, 0)), (N_SHARD, D_PLE, ROW_BLK), (N_SHARD,)),
        w_ff_down=_wgrad("dw_down", f, dr2, ffj, full_t(half, lambda j, n: (0, n)),
                         pl.BlockSpec((None, FF_BLK, half), lambda j, n: (j, 0, n)), (N_SHARD, FF_BLK, D_MODEL), (N_SHARD, 2)),
        w_ff_a=_wgrad("dw_a", da_pre, x1b, ffj, full_t(half, lambda j, n: (0, n)),
                      pl.BlockSpec((None, FF_BLK, half), lambda j, n: (j, 0, n)), (N_SHARD, FF_BLK, D_MODEL), (N_SHARD, 2)),
        w_ff_b=_wgrad("dw_b", dbb, x1b, ffj, full_t(half, lambda j, n: (0, n)),
                      pl.BlockSpec((None, FF_BLK, half), lambda j, n: (j, 0, n)), (N_SHARD, FF_BLK, D_MODEL), (N_SHARD, 2)),
        w_o=_wgrad("dw_o", cat, dr1, full_t(half, lambda kk, n: (0, kk)), full_t(half, lambda kk, n: (0, n)),
                   pl.BlockSpec((half, half), lambda kk, n: (kk, n)), (D_MODEL, D_MODEL), (2, 2)))
    dep = early_grads(early)

    do1, do4, do16, dl1, dl4, dl16, duz, dws, dbs, zstat = _mix_bwd(
        dr1, w_o, hu, hz, mixed, attn, ln_z_g, ln_z_b, w_s[0], dep)
    dep = early_grads_sent(duz, (stat3, stat1, zstat, cstat, dws, dbs))
    dqkv = [_attn_bwd(qkv, do, lse, dl, d, dep)
            for qkv, do, lse, dl, d in zip(qkvs, (do1, do4, do16), lses, (dl1, dl4, dl16), DILATIONS)]
    dh, grad_x = _dx_in([g[0] for g in dqkv], [g[1] for g in dqkv], [g[2] for g in dqkv], duz, dr1, w_in,
                        c_tab, s1_tab, s2_tab)
    g_w_in = _wgrad("dw_in", x, dh, full_t(half, lambda j, kk: (0, kk)), full_t(W_IN_BLK, lambda j, kk: (0, j)),
                    pl.BlockSpec((None, half, W_IN_BLK), lambda j, kk: (j, kk, 0)), (N_SHARD, D_MODEL, W_IN_BLK), (N_SHARD, 2))
    return grad_x, g_w_in


def _tile_rows(rows, mult, steps):
    if rows % mult:
        return rows
    return next(rows // k for k in range(steps, rows + 1) if rows % k == 0 and (rows // k) % mult == 0)


def _grid_spec(grid, in_specs, out_specs):
    return pltpu.PrefetchScalarGridSpec(num_scalar_prefetch=1, grid=grid, in_specs=in_specs, out_specs=out_specs)


def _on_own_steps(i, count, steps, work):
    if count == steps:
        work()
    else:
        pl.when(i < count)(work)


def _place_shards(name, ws, dtypes, place, dep):
    n = len(ws)
    tiles = [_tile_rows(w.shape[0], 16, 8) for w in ws]
    counts = [w.shape[0] // t for w, t in zip(ws, tiles)]
    steps = max(counts)

    def body(s_ref, *refs):
        i = pl.program_id(0)
        for a in range(n):
            def work(a=a):
                refs[n + 1 + a][...] = refs[a][...].astype(dtypes[a])
            _on_own_steps(i, counts[a], steps, work)

    def tile(a, lead):
        last = counts[a] - 1
        if lead:
            return pl.BlockSpec((None, tiles[a], ws[a].shape[1]), lambda i, s: (s[0], jnp.minimum(i, last), 0))
        return pl.BlockSpec((tiles[a], ws[a].shape[1]), lambda i, s: (jnp.minimum(i, last), 0))

    return pl.pallas_call(
        body, name=name,
        grid_spec=_grid_spec((steps,), [tile(a, False) for a in range(n)] + [pl.BlockSpec(memory_space=pl.ANY)],
                             [tile(a, True) for a in range(n)]),
        out_shape=[jax.ShapeDtypeStruct((N_SHARD, *w.shape), dt) for w, dt in zip(ws, dtypes)],
        compiler_params=_cp())(place, *ws, dep)


def _pair_sums(name, mines, gots, place):
    n = len(mines)
    tiles = [_tile_rows(g.shape[1], 16, 2) for g in gots]
    per_blk = [g.shape[1] // t for g, t in zip(gots, tiles)]
    counts = [N_SHARD * nh for nh in per_blk]
    steps = max(counts)

    def body(s_ref, *refs):
        i = pl.program_id(0)
        for a in range(n):
            def work(a=a):
                refs[2 * n + a][...] = (refs[a][...] + refs[n + a][...]).astype(BF16)
            _on_own_steps(i, counts[a], steps, work)

    def tile(a, mine):
        nh, last = per_blk[a], counts[a] - 1

        def index(i, s):
            g = jnp.minimum(i, last)
            return (g // nh, (s[1] * nh if mine else 0) + g % nh, 0)

        return pl.BlockSpec((None, tiles[a], gots[a].shape[2]), index)

    return pl.pallas_call(
        body, name=name,
        grid_spec=_grid_spec((steps,), [tile(a, True) for a in range(n)] + [tile(a, False) for a in range(n)],
                             [tile(a, False) for a in range(n)]),
        out_shape=[jax.ShapeDtypeStruct(g.shape, BF16) for g in gots], compiler_params=_cp())(place, *mines, *gots)


def _chip_sums(name, owns, landeds, place, dep):
    n = len(owns)
    tiles = [_tile_rows(o.shape[1], 16, 8) for o in owns]
    counts = [o.shape[1] // t for o, t in zip(owns, tiles)]
    steps = max(counts)

    def body(s_ref, *refs):
        i = pl.program_id(0)
        for a in range(n):
            def work(a=a):
                own, l1, l2, l3 = (refs[4 * a + k][...].astype(F32) for k in range(4))
                refs[4 * n + 1 + a][...] = ((own + l1) + l2) + l3
            _on_own_steps(i, counts[a], steps, work)

    def slot(a, d):
        last = counts[a] - 1
        return pl.BlockSpec((None, tiles[a], owns[a].shape[2]), lambda i, s: ((s[0] + d) % N_SHARD, jnp.minimum(i, last), 0))

    def out(a):
        nh, last = counts[a], counts[a] - 1
        return pl.BlockSpec((tiles[a], owns[a].shape[2]), lambda i, s: (s[1] * nh + jnp.minimum(i, last), 0))

    operands = [x for o, l in zip(owns, landeds) for x in (o, l, l, l)]
    return pl.pallas_call(
        body, name=name,
        grid_spec=_grid_spec((steps,), [slot(a, d) for a in range(n) for d in range(4)] + [pl.BlockSpec(memory_space=pl.ANY)],
                             [out(a) for a in range(n)]),
        out_shape=[jax.ShapeDtypeStruct((2 * o.shape[1], o.shape[2]), F32) for o in owns],
        compiler_params=_cp())(place, *operands, dep)


def _adamw_math(w, g, m, v):
    m = ADAM_B1 * m + (1.0 - ADAM_B1) * g
    v = ADAM_B2 * v + (1.0 - ADAM_B2) * (g * g)
    m_hat = m / (1.0 - ADAM_B1 ** ADAM_STEP)
    v_hat = v / (1.0 - ADAM_B2 ** ADAM_STEP)
    delta = -ADAM_LR * (m_hat / (jnp.sqrt(v_hat) + ADAM_EPS) + ADAM_WD * w)
    return delta, m, v


def _adamw_shards(name, ws, gs, ms, vs):
    n = len(ws)
    tiles = [_tile_rows(w.shape[1], 8, 8) for w in ws]
    counts = [w.shape[1] // t for w, t in zip(ws, tiles)]
    steps = max(counts)

    def body(*refs):
        i = pl.program_id(0)
        for a in range(n):
            def work(a=a):
                w_ref, g_ref, m_ref, v_ref = refs[4 * a:4 * a + 4]
                d_ref, nm_ref, nv_ref = refs[4 * n + 3 * a:4 * n + 3 * a + 3]
                d_ref[...], nm_ref[...], nv_ref[...] = _adamw_math(w_ref[...], g_ref[...], m_ref[...], v_ref[...])
            _on_own_steps(i, counts[a], steps, work)

    def tile(a, lead):
        last, c = counts[a] - 1, ws[a].shape[2]
        if lead:
            return pl.BlockSpec((None, tiles[a], c), lambda i: (0, jnp.minimum(i, last), 0))
        return pl.BlockSpec((tiles[a], c), lambda i: (jnp.minimum(i, last), 0))

    res = pl.pallas_call(
        body, name=name, grid=(steps,),
        in_specs=[tile(a, lead) for a in range(n) for lead in (True, False, True, True)],
        out_specs=[tile(a, True) for a in range(n) for _ in range(3)],
        out_shape=[jax.ShapeDtypeStruct(w.shape, F32) for w in ws for _ in range(3)],
        compiler_params=_cp())(*[x for quad in zip(ws, gs, ms, vs) for x in quad])
    return [tuple(res[3 * a:3 * a + 3]) for a in range(n)]


MESH = pl.DeviceIdType.MESH
ANY = pl.BlockSpec(memory_space=pl.ANY)


def _place():
    x, y, c = lax.axis_index("x"), lax.axis_index("y"), lax.axis_index("c")
    chips = [(1 - x, y), (x, 1 - y), (1 - x, 1 - y)]
    return x, y, c, 2 * x + y, chips


def _remote(src, dst, send_sem, recv_sem, dev):
    return pltpu.make_async_remote_copy(src_ref=src, dst_ref=dst, send_sem=send_sem, recv_sem=recv_sem,
                                        device_id=dev, device_id_type=MESH)


def _half(ref, hc, rows):
    return ref.at[pl.ds(hc * (rows // 2), rows // 2)]


def _sibling_join(blocks, tag):
    n = len(blocks)

    def body(*refs):
        outs = refs[n:2 * n]
        send, recv = refs[2 * n:]
        x, y, c, _, _ = _place()
        cps = []
        for a in range(n):
            h = blocks[a].shape[0] // 2
            mine = outs[a].at[pl.ds(c * h, h)]
            cp = _remote(mine, mine, send.at[a], recv.at[a], (x, y, 1 - c))
            cp.start()
            cps.append(cp)
        for a, cp in enumerate(cps):
            h = blocks[a].shape[0] // 2
            theirs = outs[a].at[pl.ds((1 - c) * h, h)]
            _remote(theirs, theirs, send.at[a], recv.at[a], (x, y, 1 - c)).wait_recv()
            cp.wait_send()

    sem = pltpu.SemaphoreType.DMA
    return pl.pallas_call(body, name=f"rs_sibling_join_{tag}", in_specs=[ANY] * n, out_specs=[ANY] * n,
                          out_shape=[jax.ShapeDtypeStruct(b_.shape, b_.dtype) for b_ in blocks],
                          input_output_aliases={a: a for a in range(n)},
                          scratch_shapes=[sem((n,)), sem((n,))])(*blocks)


HBM = pl.BlockSpec(memory_space=pltpu.HBM)
SEM = pl.BlockSpec(memory_space=pltpu.SEMAPHORE)
TOKEN = jax.ShapeDtypeStruct((8, 128), F32)


def _in_flight_params():
    return pltpu.CompilerParams(has_side_effects=pltpu.SideEffectType.DATAFLOW_SIDE_EFFECTING)


def _in_hbm(a):
    return pltpu.with_memory_space_constraint(a, pltpu.HBM)


def _gather_piece(ref, rows, split, slot, hc):
    return _half(ref.at[slot], hc, rows) if split else ref.at[slot]


def _gather_start(stacks, split, after, tag):
    n = len(stacks)

    def body(*refs):
        ins = refs[:n]
        send, recv = refs[n + 1], refs[n + 2]
        token = refs[2 * n + 3]
        _, _, c, j, chips = _place()
        for a in range(n):
            mine = _gather_piece(ins[a], stacks[a].shape[1], split[a], j, c)
            for t in range(3):
                _remote(mine, mine, send.at[3 * a + t], recv.at[3 * a + t], (*chips[t], c)).start()
        token[...] = jnp.zeros_like(token)

    sems = pltpu.SemaphoreType.DMA((3 * n,))
    res = pl.pallas_call(
        body, name=f"gather_start_{tag}", in_specs=[HBM] * n + [ANY],
        out_specs=[SEM, SEM] + [HBM] * n + [pl.BlockSpec(memory_space=pltpu.VMEM)],
        out_shape=[sems, sems] + [pltpu.HBM(s.shape, s.dtype) for s in stacks] + [TOKEN],
        input_output_aliases={a: a + 2 for a in range(n)}, compiler_params=_in_flight_params(),
    )(*[_in_hbm(s) for s in stacks], after)
    return res[0], res[1], res[2:2 + n], res[2 + n]


def _gather_wait(send, recv, stacks, split, after, tag):
    n = len(stacks)

    def body(*refs):
        ins = refs[:n]
        send_ref, recv_ref = refs[n], refs[n + 1]
        _, _, c, j, chips = _place()
        for a in range(n):
            rows = stacks[a].shape[1]
            mine = _gather_piece(ins[a], rows, split[a], j, c)
            for t, (px, py) in enumerate(chips):
                theirs = _gather_piece(ins[a], rows, split[a], 2 * px + py, c)
                _remote(mine, mine, send_ref.at[3 * a + t], recv_ref.at[3 * a + t], (px, py, c)).wait_send()
                _remote(theirs, theirs, send_ref.at[3 * a + t], recv_ref.at[3 * a + t], (px, py, c)).wait_recv()

    return pl.pallas_call(
        body, name=f"gather_wait_{tag}", in_specs=[HBM] * n + [SEM, SEM, ANY], out_specs=[HBM] * n,
        out_shape=[pltpu.HBM(s.shape, s.dtype) for s in stacks],
        input_output_aliases={a: a for a in range(n)}, compiler_params=_in_flight_params(),
    )(*stacks, send, recv, after)


def _gather_forward(stacks, split, tag):
    idx = [a for a in range(len(stacks)) if split[a]]
    n = len(idx)

    def body(*refs):
        outs = refs[n:2 * n]
        send, recv = refs[2 * n:]
        x, y, c, _, chips = _place()
        sends = []
        for t, (px, py) in enumerate(chips):
            for a in range(n):
                blk = _half(outs[a].at[2 * px + py], c, stacks[idx[a]].shape[1])
                cp = _remote(blk, blk, send.at[a, t], recv.at[a, t], (x, y, 1 - c))
                cp.start()
                sends.append(cp)
        for t, (px, py) in enumerate(chips):
            for a in range(n):
                blk = _half(outs[a].at[2 * px + py], 1 - c, stacks[idx[a]].shape[1])
                _remote(blk, blk, send.at[a, t], recv.at[a, t], (x, y, 1 - c)).wait_recv()
        for cp in sends:
            cp.wait_send()

    sem = pltpu.SemaphoreType.DMA
    res = pl.pallas_call(
        body, name=f"gather_forward_{tag}", in_specs=[ANY] * n, out_specs=[ANY] * n,
        out_shape=[jax.ShapeDtypeStruct(stacks[a].shape, stacks[a].dtype) for a in idx],
        input_output_aliases={a: a for a in range(n)}, scratch_shapes=[sem((n, 3)), sem((n, 3))],
    )(*[stacks[a] for a in idx])
    out = list(stacks)
    for a, r in zip(idx, res):
        out[a] = r
    return out


def _swap_start(grads, tag):
    n = len(grads)

    def body(*refs):
        ins, gots = refs[:n], refs[n:2 * n]
        send, recv = refs[2 * n], refs[2 * n + 1]
        token = refs[4 * n + 2]
        x, y, c, _, _ = _place()
        for a in range(n):
            h = grads[a].shape[1] // 2
            _remote(ins[a].at[:, pl.ds((1 - c) * h, h)], gots[a], send.at[a], recv.at[a], (x, y, 1 - c)).start()
        token[...] = jnp.zeros_like(token)

    sems = pltpu.SemaphoreType.DMA((n,))
    halves = [(g.shape[0], g.shape[1] // 2, g.shape[2]) for g in grads]
    res = pl.pallas_call(
        body, name=f"swap_start_{tag}", in_specs=[HBM] * (2 * n),
        out_specs=[SEM, SEM] + [HBM] * (2 * n) + [pl.BlockSpec(memory_space=pltpu.VMEM)],
        out_shape=[sems, sems] + [pltpu.HBM(g.shape, g.dtype) for g in grads] + [pltpu.HBM(s, F32) for s in halves] + [TOKEN],
        input_output_aliases={a: a + 2 for a in range(2 * n)}, compiler_params=_in_flight_params(),
    )(*[_in_hbm(g) for g in grads], *[_in_hbm(lax.empty(s, F32)) for s in halves])
    return res[0], res[1], res[2:2 + n], res[2 + n:2 + 2 * n], res[2 + 2 * n]


def _swap_wait(send, recv, grads, gots, after, tag):
    n = len(grads)

    def body(*refs):
        ins, lnd = refs[:n], refs[n:2 * n]
        send_ref, recv_ref = refs[2 * n], refs[2 * n + 1]
        x, y, c, _, _ = _place()
        for a in range(n):
            h = grads[a].shape[1] // 2
            cp = _remote(ins[a].at[:, pl.ds((1 - c) * h, h)], lnd[a], send_ref.at[a], recv_ref.at[a], (x, y, 1 - c))
            cp.wait_send()
            cp.wait_recv()

    bufs = [pltpu.HBM(g.shape, g.dtype) for g in grads] + [pltpu.HBM(g.shape, g.dtype) for g in gots]
    res = pl.pallas_call(
        body, name=f"swap_wait_{tag}", in_specs=[HBM] * (2 * n) + [SEM, SEM, ANY], out_specs=[HBM] * (2 * n),
        out_shape=bufs, input_output_aliases={a: a for a in range(2 * n)}, compiler_params=_in_flight_params(),
    )(*grads, *gots, send, recv, after)
    return res[:n], res[n:]


def _exchange_start(parts, tag):
    n = len(parts)

    def body(*refs):
        ins, lands = refs[:n], refs[n:2 * n]
        send, recv = refs[2 * n], refs[2 * n + 1]
        token = refs[4 * n + 2]
        _, _, c, j, chips = _place()
        for t, (px, py) in enumerate(chips):
            for a in range(n):
                _remote(ins[a].at[2 * px + py], lands[a].at[j], send.at[3 * a + t], recv.at[3 * a + t], (px, py, c)).start()
        token[...] = jnp.zeros_like(token)

    sems = pltpu.SemaphoreType.DMA((3 * n,))
    bufs = [pltpu.HBM(p.shape, p.dtype) for p in parts]
    res = pl.pallas_call(
        body, name=f"exchange_start_{tag}", in_specs=[HBM] * (2 * n),
        out_specs=[SEM, SEM] + [HBM] * (2 * n) + [pl.BlockSpec(memory_space=pltpu.VMEM)],
        out_shape=[sems, sems] + bufs + bufs + [TOKEN],
        input_output_aliases={a: a + 2 for a in range(2 * n)}, compiler_params=_in_flight_params(),
    )(*[_in_hbm(p) for p in parts], *[_in_hbm(lax.empty(p.shape, p.dtype)) for p in parts])
    return res[0], res[1], res[2:2 + n], res[2 + n:2 + 2 * n], res[2 + 2 * n]


def _exchange_wait(send, recv, parts, lands, after, tag):
    n = len(parts)

    def body(*refs):
        ins, lnd = refs[:n], refs[n:2 * n]
        send_ref, recv_ref = refs[2 * n], refs[2 * n + 1]
        _, _, c, j, chips = _place()
        for t, (px, py) in enumerate(chips):
            jt = 2 * px + py
            for a in range(n):
                _remote(ins[a].at[jt], lnd[a].at[j], send_ref.at[3 * a + t], recv_ref.at[3 * a + t], (px, py, c)).wait_send()
                _remote(ins[a].at[jt], lnd[a].at[jt], send_ref.at[3 * a + t], recv_ref.at[3 * a + t], (px, py, c)).wait_recv()

    bufs = [pltpu.HBM(p.shape, p.dtype) for p in parts]
    res = pl.pallas_call(
        body, name=f"exchange_wait_{tag}", in_specs=[HBM] * (2 * n) + [SEM, SEM, ANY], out_specs=[HBM] * (2 * n),
        out_shape=bufs + bufs, input_output_aliases={a: a for a in range(2 * n)}, compiler_params=_in_flight_params(),
    )(*parts, *lands, send, recv, after)
    return res[:n], res[n:]


def _small_chip_sums(arrs):
    n = len(arrs)

    def body(*refs):
        ins, outs = refs[:n], refs[n:2 * n]
        sib = refs[2 * n:3 * n]
        send, recv = refs[3 * n:]
        x, y, c, j, _ = _place()
        swaps = [_remote(ins[a], sib[a], send.at[a], recv.at[a], (x, y, 1 - c)) for a in range(n)]
        for cp in swaps:
            cp.start()
        for a in range(n):
            swaps[a].wait_recv()
            outs[a][j] = ins[a][...] + sib[a][...]
        for cp in swaps:
            cp.wait_send()

    sem = pltpu.SemaphoreType.DMA
    vm = pl.BlockSpec(memory_space=pltpu.VMEM)
    return pl.pallas_call(
        body, name="small_chip_sums", in_specs=[vm] * n, out_specs=[vm] * n,
        out_shape=[jax.ShapeDtypeStruct((N_SHARD, *a.shape), F32) for a in arrs],
        scratch_shapes=[pltpu.VMEM(a.shape, F32) for a in arrs] + [sem((n,)), sem((n,))],
        compiler_params=_cp(),
    )(*arrs)


def _small_totals(stacks):
    n = len(stacks)

    def body(*refs):
        for a in range(n):
            refs[n + a][...] = ((refs[a][0] + refs[a][1]) + refs[a][2]) + refs[a][3]

    return pl.pallas_call(body, name="small_totals", out_shape=[jax.ShapeDtypeStruct(s.shape[1:], F32) for s in stacks],
                          compiler_params=_cp())(*stacks)


SMALL_1024 = ("ln1_g", "ln1_b", "ln2_g", "ln2_b", "b_ple_gate", "ln3_g", "ln3_b")


def _adamw_small(red3, red1, redz, g_conv_w, redc, red_ws, red_bs, params):
    shape2d = {"ln_z_g": (1, D_GMLP), "ln_z_b": (1, D_GMLP), "w_s": (N_HEADS * BLK, BLK), "b_s": (N_HEADS, BLK),
               "conv_w": (3, FF_BLK), "conv_b": (N_SHARD, FF_BLK), **{k: (1, D_MODEL) for k in SMALL_1024}}
    names = list(shape2d)
    flat = [a.reshape(shape2d[k]) for k in names for a in params[k]]

    def body(r3, r1, rz, gcw, rc, rws, rbs, *refs):
        ins, outs = refs[:3 * len(names)], refs[3 * len(names):]

        def grad_of(k):
            if k == "w_s":
                return rws[...]
            if k == "b_s":
                return rbs[...]
            if k == "conv_w":
                return gcw[0:3, :]
            if k == "conv_b":
                return jnp.concatenate([rc[j * STAT_ROWS + 3:j * STAT_ROWS + 4, :] for j in range(N_SHARD)], axis=0)
            src, row = {"ln3_g": (r3, 0), "ln3_b": (r3, 1), "b_ple_gate": (r3, 2), "ln2_g": (r3, 3), "ln2_b": (r3, 4),
                        "ln1_g": (r1, 0), "ln1_b": (r1, 1), "ln_z_g": (rz, 0), "ln_z_b": (rz, 1)}[k]
            return src[row:row + 1, :]

        for i, k in enumerate(names):
            w_ref, m_ref, v_ref = ins[3 * i:3 * i + 3]
            g_ref, d_ref, nm_ref, nv_ref = outs[4 * i:4 * i + 4]
            g = grad_of(k)
            g_ref[...] = g
            d_ref[...], nm_ref[...], nv_ref[...] = _adamw_math(w_ref[...], g, m_ref[...], v_ref[...])

    res = pl.pallas_call(
        body, name="adamw_small",
        out_shape=[jax.ShapeDtypeStruct(shape2d[k], F32) for k in names for _ in range(4)],
        compiler_params=_cp(),
    )(red3, red1, redz, g_conv_w, redc, red_ws, red_bs, *flat)
    return {k: tuple(r.reshape(params[k][0].shape) for r in res[4 * i:4 * i + 4]) for i, k in enumerate(names)}


WEIGHTS = ("w_in", "ln_z_g", "ln_z_b", "w_s", "b_s", "w_o", "ln1_g", "ln1_b", "w_ff_a", "w_ff_b", "conv_w", "conv_b",
           "w_ff_down", "ln2_g", "ln2_b", "w_ple_gate", "b_ple_gate", "w_ple_in", "ln3_g", "ln3_b")
BIG = ("w_in", "w_o", "w_ff_a", "w_ff_b", "w_ff_down", "w_ple_gate", "w_ple_in")
TRANSPOSED = ("w_ff_a", "w_ff_b")
LATE = ("w_o", "w_ff_a", "w_ff_b", "w_ff_down", "w_ple_gate", "w_ple_in", "conv_w")


def kernel(x, p, positions, w_in, ln_z_g, ln_z_b, w_s, b_s, w_o, ln1_g, ln1_b, w_ff_a, w_ff_b, conv_w, conv_b, w_ff_down, ln2_g, ln2_b, w_ple_gate, b_ple_gate, w_ple_in, ln3_g, ln3_b, loss_target, m_w_in, m_ln_z_g, m_ln_z_b, m_w_s, m_b_s, m_w_o, m_ln1_g, m_ln1_b, m_w_ff_a, m_w_ff_b, m_conv_w, m_conv_b, m_w_ff_down, m_ln2_g, m_ln2_b, m_w_ple_gate, m_b_ple_gate, m_w_ple_in, m_ln3_g, m_ln3_b, v_w_in, v_ln_z_g, v_ln_z_b, v_w_s, v_b_s, v_w_o, v_ln1_g, v_ln1_b, v_w_ff_a, v_w_ff_b, v_conv_w, v_conv_b, v_w_ff_down, v_ln2_g, v_ln2_b, v_w_ple_gate, v_b_ple_gate, v_w_ple_in, v_ln3_g, v_ln3_b):
    args = locals()
    w = {k: args[k] for k in WEIGHTS}
    m = {k: args["m_" + k] for k in WEIGHTS}
    v = {k: args["v_" + k] for k in WEIGHTS}

    for k in TRANSPOSED:
        w[k], m[k], v[k] = (jnp.swapaxes(a, 1, 2) for a in (w[k], m[k], v[k]))

    chip = 2 * lax.axis_index("x") + lax.axis_index("y")
    place = jnp.stack([chip, lax.axis_index("c")]).astype(jnp.int32)
    stack = dict(zip(["w_in"], _place_shards("cast_w_in", [w["w_in"][0]], [MXU], place, place)))
    i_send, i_recv, in_flight, dep = _gather_start([stack["w_in"]], [True], place, "w_in")
    stack.update(zip(LATE, _place_shards("cast_late", [w[k][0] for k in LATE],
                                         [F32 if k == "conv_w" else MXU for k in LATE], place, dep)))
    rope = _rope_tables(positions, x.shape[1], stack[LATE[-1]])
    landed_in = _gather_wait(i_send, i_recv, in_flight, [True], rope[0], "w_in")
    w_in_full, = _gather_forward(landed_in, [True], "w_in")
    split_late = [k != "conv_w" for k in LATE]
    g_send, g_recv, late_flight, start_dep = _gather_start([stack[k] for k in LATE], split_late, w_in_full, "late")

    def late_weights(after):
        landed = _gather_wait(g_send, g_recv, late_flight, split_late, after, "late")
        fw = dict(zip(LATE, _gather_forward(landed, split_late, "late")))
        return (fw["w_o"].reshape(D_MODEL, D_MODEL), fw["w_ff_a"], fw["w_ff_b"], fw["conv_w"], fw["w_ff_down"],
                fw["w_ple_gate"].reshape(D_MODEL, D_MODEL), fw["w_ple_in"])

    def swap_started(names, grads, tag):
        stacked = [g.reshape(N_SHARD, *w[k].shape[1:]) for k, g in zip(names, grads)]
        return (names, tag, *_swap_start(stacked, tag))

    def partial_sums(swap, after):
        names, tag, send, recv, stacked, gots, _ = swap
        stacked, got = _swap_wait(send, recv, stacked, gots, after, tag)
        pair = _pair_sums(f"rs_pair_{tag}", stacked, got, place)
        return (names, tag, *_exchange_start(pair, tag))

    def reduced(trip, after, dep):
        names, tag, send, recv, pair, lands, _ = trip
        pair, landed = _exchange_wait(send, recv, pair, lands, after, tag)
        blocks = _chip_sums(f"rs_sum_{tag}", pair, landed, place, dep)
        return dict(zip(names, _sibling_join(blocks, tag)))

    trips = {}

    def early_grads(grads):
        trips["swap"] = swap_started(list(grads), list(grads.values()), "early")
        return trips["swap"][-1]

    def early_grads_sent(after, small):
        trips["early"] = partial_sums(trips["swap"], after)
        stat3, stat1, zstat, cstat, dws, dbs = small
        sums = _small_chip_sums([stat3, stat1, zstat, cstat.reshape(N_SHARD * STAT_ROWS, FF_BLK),
                                 dws.reshape(N_HEADS * BLK, BLK), dbs])
        trips["small"] = _gather_start(sums, [False] * len(sums), trips["early"][-1], "small")
        return trips["small"][-1]

    grad_x, g_w_in = _local_step(
        x[0], p[0, 0], rope, loss_target[0], w_in_full, start_dep, late_weights, early_grads, early_grads_sent,
        ln_z_g, ln_z_b, w_s, b_s, ln1_g, ln1_b, conv_b, ln2_g, ln2_b, b_ple_gate, ln3_g, ln3_b)

    trips["w_in"] = partial_sums(swap_started(["w_in"], [g_w_in], "w_in"), g_w_in)
    out = {}

    def adamw(red, tag):
        names = list(red)
        steps = _adamw_shards(f"adamw_{tag}", [w[k] for k in names], [red[k] for k in names], [m[k] for k in names],
                              [v[k] for k in names])
        for k, (d, nm, nv) in zip(names, steps):
            out[k] = (red[k].reshape(w[k].shape), d, nm, nv)

    adamw(reduced(trips["early"], grad_x, trips["w_in"][-1]), "early")
    adamw(reduced(trips["w_in"], out["w_o"][3], start_dep), "w_in")
    for k in TRANSPOSED:
        out[k] = tuple(jnp.swapaxes(a, 1, 2) for a in out[k])

    s_send, s_recv, s_flight, _ = trips["small"]
    red3, red1, redz, redc, red_ws, red_bs = _small_totals(
        _gather_wait(s_send, s_recv, s_flight, [False] * len(s_flight), out["w_in"][3], "small"))
    loss = (0.5 / D_MODEL) * jnp.sum(red3[5])
    g_conv_w = lax.dynamic_slice_in_dim(redc, chip * STAT_ROWS, STAT_ROWS, 0)
    names_small = [k for k in WEIGHTS if k not in BIG]
    out.update(_adamw_small(red3, red1, redz, g_conv_w, redc, red_ws, red_bs, {k: (w[k], m[k], v[k]) for k in names_small}))

    return (loss, grad_x[None], *[out[k][0] for k in WEIGHTS], *[out[k][1] for k in WEIGHTS],
            *[out[k][2] for k in WEIGHTS], *[out[k][3] for k in WEIGHTS])
```

```python
import functools
import math

import numpy as np
import jax
import jax.numpy as jnp
from jax import lax
from jax.experimental import pallas as pl
from jax.experimental.pallas import tpu as pltpu

F32 = jnp.float32
BF16 = jnp.bfloat16
MXU = BF16

D_MODEL = 1024
HEAD_DIM = 64
N_HEADS = 8
D_ATTN = 512
D_GMLP = 512
D_IN = 2560
DILATIONS = (1, 4, 16)
BLK = 128
ROPE_THETA = 500000.0
ROPE_DIM = 16
D_FF = 2816
D_PLE = 256
LN_EPS = 1e-5
ALPHA = 2.0 ** 0.25
NEG_INF = -1e30
N_SHARD = 4
W_IN_BLK = D_IN // N_SHARD
FF_BLK = D_FF // N_SHARD
ROW_BLK = D_MODEL // N_SHARD
ADAM_LR, ADAM_B1, ADAM_B2, ADAM_EPS, ADAM_WD, ADAM_STEP = 0.001, 0.9, 0.999, 1e-08, 0.01, 10

TM = 512
HALO = 8
ROW_GROUPS = 2
VMEM_LIMIT = 56 * 1024 * 1024


def _cp(**kw):
    return pltpu.CompilerParams(vmem_limit_bytes=VMEM_LIMIT, **kw)


def _full(shape):
    n = len(shape)
    return pl.BlockSpec(shape, lambda *_: (0,) * n)


def _gelu(x):
    return 0.5 * x * (1.0 + lax.erf(x * (1.0 / math.sqrt(2.0))))


def _gelu_grad(x):
    return 0.5 * (1.0 + lax.erf(x * (1.0 / math.sqrt(2.0)))) + x * jnp.exp(-0.5 * x * x) * (1.0 / math.sqrt(2.0 * math.pi))


def _ln_fwd(r):
    mu = jnp.mean(r, axis=-1, keepdims=True)
    xc = r - mu
    var = jnp.mean(xc * xc, axis=-1, keepdims=True)
    rstd = lax.rsqrt(var + LN_EPS)
    return xc * rstd, rstd


def _ln_bwd(dy, xhat, rstd, g):
    dxh = dy * g
    m1 = jnp.mean(dxh, axis=-1, keepdims=True)
    m2 = jnp.mean(dxh * xhat, axis=-1, keepdims=True)
    return rstd * (dxh - m1 - xhat * m2)


def _dot(a, b):
    return jnp.dot(a.astype(MXU), b.astype(MXU), preferred_element_type=F32)


def _dot_nt(a, b):
    return lax.dot_general(a.astype(MXU), b.astype(MXU), (((1,), (1,)), ((), ())), preferred_element_type=F32)


def _dot_tn(a, b):
    return lax.dot_general(a.astype(MXU), b.astype(MXU), (((0,), (0,)), ((), ())), preferred_element_type=F32)


def _colsum(v):
    return jnp.sum(v, axis=0, keepdims=True)


def _rope_tables(positions, t, dep):
    inv = np.float32(ROPE_THETA) ** (-np.arange(0, ROPE_DIM, 2, dtype=np.float32) / np.float32(ROPE_DIM))
    half = ROPE_DIM // 2
    pos_rep = jnp.repeat(positions.reshape(t // 16, 16), half, axis=1)
    inv_row = jnp.asarray(np.tile(inv, 16)[None, :], F32)

    def trig_body(pos_ref, inv_ref, dep_ref, cos_ref, sin_ref):
        ang = pos_ref[...].astype(F32) * inv_ref[...]
        cos_ref[...] = jnp.cos(ang)
        sin_ref[...] = jnp.sin(ang)

    vm = pl.BlockSpec(memory_space=pltpu.VMEM)
    cos8, sin8 = pl.pallas_call(
        trig_body, name="rope_trig", in_specs=[vm, vm, pl.BlockSpec(memory_space=pl.ANY)], out_specs=[vm, vm],
        out_shape=(jax.ShapeDtypeStruct((t // 16, 128), F32), jax.ShapeDtypeStruct((t // 16, 128), F32)),
    )(pos_rep, inv_row, dep)
    cos8 = cos8.reshape(t, half)
    sin8 = sin8.reshape(t, half)

    lane = np.arange(128) % HEAD_DIM
    sel = (np.arange(half)[:, None] == (lane % half)[None, :])
    e_cos = (sel & (lane < ROPE_DIM)[None, :]).astype(np.float32)
    e_s1 = -(sel & (lane < half)[None, :]).astype(np.float32)
    e_s2 = (sel & ((lane >= half) & (lane < ROPE_DIM))[None, :]).astype(np.float32)
    ones = (lane >= ROPE_DIM).astype(np.float32)[None, :]

    def expand_body(cos_ref, sin_ref, ec_ref, e1_ref, e2_ref, ones_ref, c_ref, s1_ref, s2_ref):
        hp = lax.Precision.HIGHEST
        c_ref[...] = jnp.dot(cos_ref[...], ec_ref[...], precision=hp, preferred_element_type=F32) + ones_ref[...]
        s1_ref[...] = jnp.dot(sin_ref[...], e1_ref[...], precision=hp, preferred_element_type=F32)
        s2_ref[...] = jnp.dot(sin_ref[...], e2_ref[...], precision=hp, preferred_element_type=F32)

    tab = jax.ShapeDtypeStruct((t, 128), F32)
    return pl.pallas_call(expand_body, name="rope_expand", out_shape=(tab, tab, tab), compiler_params=_cp())(
        cos8, sin8, jnp.asarray(e_cos), jnp.asarray(e_s1), jnp.asarray(e_s2), jnp.asarray(ones))


def _tile_heads(tab):
    return jnp.concatenate([tab] * (D_ATTN // 128), axis=1)


def _rope_apply(v, c, s1, s2):
    n = v.shape[1]
    half = ROPE_DIM // 2
    return v * c + pltpu.roll(v, n - half, 1) * s1 + pltpu.roll(v, half, 1) * s2


def _rope_apply_t(g, c, s1, s2):
    n = g.shape[1]
    half = ROPE_DIM // 2
    return g * c + pltpu.roll(g * s1, half, 1) + pltpu.roll(g * s2, n - half, 1)


LANE_CHUNKS = D_ATTN // 128
HEAD_LANES = 128 // N_HEADS


def _perm_shape(t, d, w, dtype):
    return jax.ShapeDtypeStruct((d, t // d, w), dtype)


def _perm_tile(d, w):
    return pl.BlockSpec((None if d == 1 else d, TM // d, w), lambda i: (0, i, 0))


def _to_planes(ref, scr, d, n_chunks, dtype):
    for r in range(d):
        for cc in range(n_chunks):
            ref[r, :, cc * 128:(cc + 1) * 128] = scr.at[cc][pl.ds(r, TM // d, stride=d), :].astype(dtype)


def _from_planes(ref, scr, d, n_chunks, accumulate=False):
    for r in range(d):
        for cc in range(n_chunks):
            rows = scr.at[cc]
            val = ref[r, :, cc * 128:(cc + 1) * 128].astype(F32)
            if accumulate:
                rows[pl.ds(r, TM // d, stride=d), :] += val
            else:
                rows[pl.ds(r, TM // d, stride=d), :] = val


def _chunks(val):
    return [val[:, cc * 128:(cc + 1) * 128] for cc in range(val.shape[1] // 128)]


def _unchunk(scr, n_chunks, base=0):
    return jnp.concatenate([scr[base + cc] for cc in range(n_chunks)], axis=1)


def _head_expand():
    src = np.arange(128)[:, None]
    dst = np.arange(D_ATTN)[None, :]
    return jnp.asarray((src == (dst // HEAD_DIM) * HEAD_LANES).astype(np.float32))


def _head_reduce():
    src = np.arange(D_ATTN)[:, None]
    dst = np.arange(128)[None, :]
    return jnp.asarray((src // HEAD_DIM == dst // HEAD_LANES).astype(np.float32))


def _dot_select(a, sel):
    hi = a.astype(BF16)
    lo = (a - hi.astype(F32)).astype(BF16)
    sel = sel.astype(BF16)
    return jnp.dot(hi, sel, preferred_element_type=F32) + jnp.dot(lo, sel, preferred_element_type=F32)


def _qkvuz(x, w_in, c_tab, s1_tab, s2_tab, ln_z_g, ln_z_b, w_s, b_full, dep):
    t = x.shape[0]
    nchunk = TM // BLK

    def body(x_ref, w_ref, c_ref, s1_ref, s2_ref, g_ref, b_ref, ws_ref, bf_ref, dep_ref,
             qkv1_ref, qkv4_ref, qkv16_ref, hu_ref, hz_ref, mixed_ref, gm_ref, h_scr, wm_scr, p_scr):
        @pl.when(pl.program_id(0) == 0)
        def _():
            row = lax.broadcasted_iota(jnp.int32, (BLK, BLK), 0)
            col = lax.broadcasted_iota(jnp.int32, (BLK, BLK), 1)
            for g in range(N_HEADS):
                wm_scr[g] = jnp.where(col <= row, ws_ref[g], 0.0).astype(MXU)

        xb = x_ref[...].astype(MXU)
        for j in range(N_SHARD):
            h_scr[:, j * W_IN_BLK:(j + 1) * W_IN_BLK] = jnp.dot(xb, w_ref[j], preferred_element_type=F32)
        c, s1, s2 = _tile_heads(c_ref[...]), _tile_heads(s1_ref[...]), _tile_heads(s2_ref[...])
        q = _rope_apply(h_scr[:, 0:D_ATTN], c, s1, s2) * (1.0 / math.sqrt(HEAD_DIM))
        k = _rope_apply(h_scr[:, D_ATTN:2 * D_ATTN], c, s1, s2)
        for part, val in enumerate((q, k, h_scr[:, 2 * D_ATTN:3 * D_ATTN])):
            qkv1_ref[:, part * D_ATTN:(part + 1) * D_ATTN] = val.astype(MXU)
            for cc in range(LANE_CHUNKS):
                p_scr[part * LANE_CHUNKS + cc] = val[:, cc * 128:(cc + 1) * 128]
        _to_planes(qkv4_ref, p_scr, DILATIONS[1], 3 * LANE_CHUNKS, MXU)
        _to_planes(qkv16_ref, p_scr, DILATIONS[2], 3 * LANE_CHUNKS, MXU)
        hu = h_scr[:, 3 * D_ATTN:3 * D_ATTN + D_GMLP]
        hz = h_scr[:, 3 * D_ATTN + D_GMLP:]
        hu_ref[...] = hu
        hz_ref[...] = hz
        zhat, _ = _ln_fwd(_gelu(hz))
        zn = (zhat * g_ref[...] + b_ref[...]).astype(MXU)
        for ch in range(nchunk):
            rows = slice(ch * BLK, (ch + 1) * BLK)
            for g in range(N_HEADS):
                cols = slice(g * HEAD_DIM, (g + 1) * HEAD_DIM)
                mixed_ref[rows, cols] = jnp.dot(wm_scr[g], zn[rows, cols], preferred_element_type=F32) + bf_ref[:, cols]
        gm_ref[...] = (_gelu(hu) * mixed_ref[...]).astype(MXU)

    tok = lambda w: pl.BlockSpec((TM, w), lambda i: (i, 0))
    outs = [_perm_shape(t, d, 3 * D_ATTN, MXU) for d in DILATIONS] + [jax.ShapeDtypeStruct((t, D_GMLP), F32)] * 3 + [
        jax.ShapeDtypeStruct((t, D_GMLP), MXU)]
    return pl.pallas_call(
        body, name="qkvuz", grid=(t // TM,),
        in_specs=[tok(D_MODEL), _full(w_in.shape), tok(128), tok(128), tok(128), _full(ln_z_g.shape), _full(ln_z_b.shape),
                  _full(w_s.shape), _full(b_full.shape), pl.BlockSpec(memory_space=pl.ANY)],
        out_specs=[_perm_tile(d, 3 * D_ATTN) for d in DILATIONS] + [tok(D_ATTN)] * 4, out_shape=outs,
        scratch_shapes=[pltpu.VMEM((TM, D_IN), F32), pltpu.VMEM((N_HEADS, BLK, BLK), MXU),
                        pltpu.VMEM((3 * LANE_CHUNKS, TM, 128), F32)],
        compiler_params=_cp(dimension_semantics=("arbitrary",)),
    )(x, w_in, c_tab, s1_tab, s2_tab, ln_z_g, ln_z_b, w_s, b_full, dep)


def _band_valid(n):
    i = lax.broadcasted_iota(jnp.int32, (BLK, 2 * BLK), 0)
    j = lax.broadcasted_iota(jnp.int32, (BLK, 2 * BLK), 1)
    return (j >= i) & (j <= i + BLK) & ((j >= BLK) | (n > 0))


def _attn_fwd(qkv, d):
    _, l_sub, _ = qkv.shape
    nb = l_sub // BLK

    def body(q_ref, kp_ref, kc_ref, vp_ref, vc_ref, o_ref, l_ref):
        valid = _band_valid(pl.program_id(1))
        kcat = jnp.concatenate([kp_ref[...], kc_ref[...]], axis=0)
        vcat = jnp.concatenate([vp_ref[...], vc_ref[...]], axis=0)
        for h in range(N_HEADS):
            cols = slice(h * HEAD_DIM, (h + 1) * HEAD_DIM)
            s = jnp.where(valid, _dot_nt(q_ref[:, cols], kcat[:, cols]), NEG_INF)
            m = jnp.max(s, axis=-1, keepdims=True)
            e = jnp.exp(s - m)
            den = jnp.sum(e, axis=-1, keepdims=True)
            o_ref[:, cols] = _dot(e, vcat[:, cols]) * (1.0 / den)
            l_ref[:, h * HEAD_LANES:(h + 1) * HEAD_LANES] = jnp.broadcast_to(m + jnp.log(den), (BLK, HEAD_LANES))

    def blk(w, col, prev=False):
        return pl.BlockSpec((None, BLK, w), lambda r, n: (r, jnp.maximum(n - 1, 0) if prev else n, col))

    return pl.pallas_call(
        body, name=f"attn_fwd_d{d}", grid=(d, nb),
        in_specs=[blk(D_ATTN, 0), blk(D_ATTN, 1, True), blk(D_ATTN, 1), blk(D_ATTN, 2, True), blk(D_ATTN, 2)],
        out_specs=[blk(D_ATTN, 0), blk(128, 0)],
        out_shape=[jax.ShapeDtypeStruct((d, l_sub, D_ATTN), F32), jax.ShapeDtypeStruct((d, l_sub, 128), F32)],
        compiler_params=_cp(dimension_semantics=("arbitrary", "arbitrary")),
    )(qkv, qkv, qkv, qkv, qkv)


def _attn_bwd(qkv, do, lse, delta, d, dep):
    _, l_sub, _ = qkv.shape
    nb = l_sub // BLK
    whole = l_sub <= 8 * BLK

    def shares(n, q_ref, kp_ref, kc_ref, vp_ref, vc_ref, do_ref, l_ref, dl_ref, dq_ref):
        valid = _band_valid(n)
        kcat = jnp.concatenate([kp_ref[...], kc_ref[...]], axis=0)
        vcat = jnp.concatenate([vp_ref[...], vc_ref[...]], axis=0)
        for h in range(N_HEADS):
            cols = slice(h * HEAD_DIM, (h + 1) * HEAD_DIM)
            stat = slice(h * HEAD_LANES, h * HEAD_LANES + 1)
            qh, doh = q_ref[:, cols], do_ref[:, cols]
            p = jnp.where(valid, jnp.exp(_dot_nt(qh, kcat[:, cols]) - l_ref[:, stat]), 0.0)
            ds = p * (_dot_nt(doh, vcat[:, cols]) - dl_ref[:, stat])
            dq_ref[:, cols] = _dot(ds, kcat[:, cols])
            yield cols, _dot_tn(ds, qh), _dot_tn(p, doh)

    def body_whole(*refs):
        dk_ref, dv_ref = refs[10:]
        n = pl.program_id(1)
        cur = pl.ds(pl.multiple_of(n * BLK, BLK), BLK)
        prev = pl.ds(pl.multiple_of(jnp.maximum(n - 1, 0) * BLK, BLK), BLK)
        for cols, dk2, dv2 in shares(n, *refs[:8], refs[9]):
            dk_ref[cur, cols] = dk2[BLK:]
            dv_ref[cur, cols] = dv2[BLK:]
            dk_ref[prev, cols] += dk2[0:BLK]
            dv_ref[prev, cols] += dv2[0:BLK]

    def body_carry(*refs):
        dk_ref, dv_ref, ck_scr, cv_scr = refs[10:]
        n = pl.program_id(1)

        @pl.when(n == 0)
        def _():
            ck_scr[...] = jnp.zeros_like(ck_scr)
            cv_scr[...] = jnp.zeros_like(cv_scr)

        @pl.when(n < nb)
        def _():
            for cols, dk2, dv2 in shares(n, *refs[:8], refs[9]):
                dk_ref[:, cols] = ck_scr[:, cols] + dk2[0:BLK]
                dv_ref[:, cols] = cv_scr[:, cols] + dv2[0:BLK]
                ck_scr[:, cols] = dk2[BLK:]
                cv_scr[:, cols] = dv2[BLK:]

        @pl.when(n == nb)
        def _():
            dk_ref[...] = ck_scr[...]
            dv_ref[...] = cv_scr[...]

    def blk(w, col, shift=0):
        return pl.BlockSpec((None, BLK, w), lambda r, n: (r, jnp.clip(n - shift, 0, nb - 1), col))

    if whole:
        dkv_spec = pl.BlockSpec((None, l_sub, D_ATTN), lambda r, n: (r, 0, 0))
        body, steps, scratch = body_whole, nb, []
    else:
        dkv_spec = blk(D_ATTN, 0, 1)
        body, steps, scratch = body_carry, nb + 1, [pltpu.VMEM((BLK, D_ATTN), F32)] * 2
    return pl.pallas_call(
        body, name=f"attn_bwd_d{d}", grid=(d, steps),
        in_specs=[blk(D_ATTN, 0), blk(D_ATTN, 1, 1), blk(D_ATTN, 1), blk(D_ATTN, 2, 1), blk(D_ATTN, 2),
                  blk(D_ATTN, 0), blk(128, 0), blk(128, 0), pl.BlockSpec(memory_space=pl.ANY)],
        out_specs=[blk(D_ATTN, 0), dkv_spec, dkv_spec],
        out_shape=[jax.ShapeDtypeStruct((d, l_sub, D_ATTN), F32)] * 3,
        scratch_shapes=scratch,
        compiler_params=_cp(dimension_semantics=("arbitrary", "arbitrary")),
    )(qkv, qkv, qkv, qkv, qkv, do, lse, delta, dep)


def _mix_ln1(os_, ls_, gm, x, w_o, ln1_g, ln1_b):
    t = x.shape[0]
    expand = _head_expand()

    def body(o1, o4, o16, l1, l4, l16, gm_ref, x_ref, wo_ref, g_ref, b_ref, ex_ref,
             attn_ref, lse1_ref, lse4_ref, lse16_ref, cat_ref, xhat_ref, rstd_ref, x1b_ref, o_scr, l_scr):
        _from_planes(o4, o_scr, DILATIONS[1], LANE_CHUNKS)
        _from_planes(o16, o_scr.at[pl.ds(LANE_CHUNKS, LANE_CHUNKS)], DILATIONS[2], LANE_CHUNKS)
        _from_planes(l4, l_scr, DILATIONS[1], 1)
        _from_planes(l16, l_scr.at[pl.ds(1, 1)], DILATIONS[2], 1)
        la, lb, lc = l1[...], l_scr[0], l_scr[1]
        m = jnp.maximum(jnp.maximum(la, lb), lc)
        ea, eb, ec = jnp.exp(la - m), jnp.exp(lb - m), jnp.exp(lc - m)
        den = ea + eb + ec
        inv = 1.0 / den
        wide = lambda w: _dot_select(w, ex_ref[...])
        attn = (wide(ea * inv) * o1[...] + wide(eb * inv) * _unchunk(o_scr, LANE_CHUNKS)
                + wide(ec * inv) * _unchunk(o_scr, LANE_CHUNKS, LANE_CHUNKS))
        attn_ref[...] = attn
        lse = m + jnp.log(den)
        lse1_ref[...] = lse
        l_scr[2] = lse
        _to_planes(lse4_ref, l_scr.at[pl.ds(2, 1)], DILATIONS[1], 1, F32)
        _to_planes(lse16_ref, l_scr.at[pl.ds(2, 1)], DILATIONS[2], 1, F32)
        cat_ref[:, 0:D_ATTN] = attn.astype(MXU)
        cat_ref[:, D_ATTN:] = gm_ref[...]
        mix = jnp.dot(cat_ref[...], wo_ref[...], preferred_element_type=F32)
        xhat, rstd = _ln_fwd(ALPHA * x_ref[...] + mix)
        xhat_ref[...] = xhat
        rstd_ref[...] = rstd
        x1b_ref[...] = (xhat * g_ref[...] + b_ref[...]).astype(MXU)

    tok = lambda w: pl.BlockSpec((TM, w), lambda i: (i, 0))
    outs = [jax.ShapeDtypeStruct((t, D_ATTN), F32)] + [_perm_shape(t, d, 128, F32) for d in DILATIONS] + [
        jax.ShapeDtypeStruct((t, D_MODEL), MXU), jax.ShapeDtypeStruct((t, D_MODEL), F32), jax.ShapeDtypeStruct((t, 1), F32),
        jax.ShapeDtypeStruct((t, D_MODEL), MXU)]
    return pl.pallas_call(
        body, name="mix_ln1", grid=(t // TM,),
        in_specs=[_perm_tile(d, D_ATTN) for d in DILATIONS] + [_perm_tile(d, 128) for d in DILATIONS]
        + [tok(D_GMLP), tok(D_MODEL), _full(w_o.shape), _full(ln1_g.shape), _full(ln1_b.shape), _full(expand.shape)],
        out_specs=[tok(D_ATTN)] + [_perm_tile(d, 128) for d in DILATIONS] + [tok(D_MODEL), tok(D_MODEL), tok(1), tok(D_MODEL)],
        out_shape=outs,
        scratch_shapes=[pltpu.VMEM((2 * LANE_CHUNKS, TM, 128), F32), pltpu.VMEM((3, TM, 128), F32)],
        compiler_params=_cp(dimension_semantics=("arbitrary",)),
    )(*os_, *ls_, gm, x, w_o, ln1_g, ln1_b, expand)


def _conv_fwd(a_ext, w_ref, b_ref, rows):
    return (b_ref[...] + w_ref[2:3, :] * a_ext[HALO:HALO + rows] + w_ref[1:2, :] * a_ext[HALO - 1:HALO - 1 + rows]
            + w_ref[0:1, :] * a_ext[HALO - 2:HALO - 2 + rows])


def _ffn_in(x1b, w_a, w_b, conv_w, conv_b):
    t = x1b.shape[0]
    hb = TM // HALO

    def body(x_ref, xh_ref, wa_ref, wb_ref, cw_ref, cb_ref, apre_ref, a_ref, b_ref, f_ref):
        i = pl.program_id(1)
        a_pre = _dot_nt(x_ref[...], wa_ref[...])
        a_halo = jnp.where(i > 0, _dot_nt(xh_ref[...], wa_ref[...]), 0.0)
        a = _conv_fwd(jnp.concatenate([a_halo, a_pre], axis=0), cw_ref, cb_ref, TM)
        b = _dot_nt(x_ref[...], wb_ref[...])
        apre_ref[...] = a_pre
        a_ref[...] = a
        b_ref[...] = b
        f_ref[...] = (_gelu(a) * b).astype(MXU)

    blk = lambda r, c: pl.BlockSpec((None, r, c), lambda j, i: (j, 0, 0))
    tokj = pl.BlockSpec((None, TM, FF_BLK), lambda j, i: (j, i, 0))
    outs = [jax.ShapeDtypeStruct((N_SHARD, t, FF_BLK), F32)] * 3 + [jax.ShapeDtypeStruct((N_SHARD, t, FF_BLK), MXU)]
    return pl.pallas_call(
        body, name="ffn_in", grid=(N_SHARD, t // TM),
        in_specs=[pl.BlockSpec((TM, D_MODEL), lambda j, i: (i, 0)),
                  pl.BlockSpec((HALO, D_MODEL), lambda j, i: (jnp.maximum(i * hb - 1, 0), 0)),
                  blk(FF_BLK, D_MODEL), blk(FF_BLK, D_MODEL), blk(3, FF_BLK), blk(1, FF_BLK)],
        out_specs=[tokj, tokj, tokj, tokj], out_shape=outs,
        compiler_params=_cp(dimension_semantics=("arbitrary", "arbitrary")),
    )(x1b, x1b, w_a, w_b, conv_w, conv_b)


def _ffn_out_ln2(f, w_down, xhat1, ln1_g, ln1_b, ln2_g, ln2_b):
    t = xhat1.shape[0]

    def body(f_ref, wd_ref, xh_ref, g1_ref, b1_ref, g2_ref, b2_ref, xhat_ref, rstd_ref, x2b_ref):
        ff = jnp.dot(f_ref[0], wd_ref[0], preferred_element_type=F32)
        for j in range(1, N_SHARD):
            ff = ff + jnp.dot(f_ref[j], wd_ref[j], preferred_element_type=F32)
        x1 = xh_ref[...] * g1_ref[...] + b1_ref[...]
        xhat, rstd = _ln_fwd(ALPHA * x1 + ff)
        xhat_ref[...] = xhat
        rstd_ref[...] = rstd
        x2b_ref[...] = (xhat * g2_ref[...] + b2_ref[...]).astype(MXU)

    tok = lambda w: pl.BlockSpec((TM, w), lambda i: (i, 0))
    vec = _full((1, D_MODEL))
    outs = [jax.ShapeDtypeStruct((t, D_MODEL), F32), jax.ShapeDtypeStruct((t, 1), F32), jax.ShapeDtypeStruct((t, D_MODEL), MXU)]
    return pl.pallas_call(
        body, name="ffn_out_ln2", grid=(t // TM,),
        in_specs=[pl.BlockSpec((N_SHARD, TM, FF_BLK), lambda i: (0, i, 0)), _full(w_down.shape), tok(D_MODEL), vec, vec, vec, vec],
        out_specs=[tok(D_MODEL), tok(1), tok(D_MODEL)], out_shape=outs,
        compiler_params=_cp(dimension_semantics=("arbitrary",)),
    )(f, w_down, xhat1, ln1_g, ln1_b, ln2_g, ln2_b)


STAT_ROWS = 8


def _ple_loss_bwd(xhat2, rstd2, p, target, ln2_g, ln2_b, w_g, b_g, w_p, ln3_g, ln3_b):
    t = xhat2.shape[0]

    def body(xh2_ref, rs2_ref, p_ref, t_ref, g2_ref, b2_ref, wg_ref, bg_ref, wp_ref, g3_ref, b3_ref,
             dr2_ref, dgp_ref, dpp_ref, stat_ref, pp_scr):
        @pl.when(pl.program_id(0) == 0)
        def _():
            stat_ref[...] = jnp.zeros_like(stat_ref)

        xhat2 = xh2_ref[...]
        x2 = xhat2 * g2_ref[...] + b2_ref[...]
        gate = jax.nn.sigmoid(jnp.dot(x2.astype(MXU), wg_ref[...], preferred_element_type=F32) + bg_ref[...])
        pb = p_ref[...].astype(MXU)
        for j in range(N_SHARD):
            pp_scr[:, j * ROW_BLK:(j + 1) * ROW_BLK] = jnp.dot(pb, wp_ref[j], preferred_element_type=F32)
        pp = pp_scr[...]
        xhat3, rstd3 = _ln_fwd(ALPHA * x2 + gate * pp)
        err = xhat3 * g3_ref[...] + b3_ref[...] - t_ref[...]
        dy = err * (1.0 / D_MODEL)
        dr3 = _ln_bwd(dy, xhat3, rstd3, g3_ref[...])
        dgp = dr3 * pp * gate * (1.0 - gate)
        dgp_ref[...] = dgp.astype(MXU)
        dpp_ref[...] = (dr3 * gate).astype(MXU)
        dx2 = ALPHA * dr3 + _dot_nt(dgp, wg_ref[...])
        dr2_ref[...] = _ln_bwd(dx2, xhat2, rs2_ref[...], g2_ref[...])
        stat_ref[0:1, :] += _colsum(dy * xhat3)
        stat_ref[1:2, :] += _colsum(dy)
        stat_ref[2:3, :] += _colsum(dgp)
        stat_ref[3:4, :] += _colsum(dx2 * xhat2)
        stat_ref[4:5, :] += _colsum(dx2)
        stat_ref[5:6, :] += _colsum(err * err)

    tok = lambda w: pl.BlockSpec((TM, w), lambda i: (i, 0))
    vec = _full((1, D_MODEL))
    outs = [jax.ShapeDtypeStruct((t, D_MODEL), F32), jax.ShapeDtypeStruct((t, D_MODEL), MXU), jax.ShapeDtypeStruct((t, D_MODEL), MXU),
            jax.ShapeDtypeStruct((STAT_ROWS, D_MODEL), F32)]
    return pl.pallas_call(
        body, name="ple_loss_bwd", grid=(t // TM,),
        in_specs=[tok(D_MODEL), tok(1), tok(D_PLE), tok(D_MODEL), vec, vec, _full(w_g.shape), vec, _full(w_p.shape), vec, vec],
        out_specs=[tok(D_MODEL), tok(D_MODEL), tok(D_MODEL), _full((STAT_ROWS, D_MODEL))], out_shape=outs,
        scratch_shapes=[pltpu.VMEM((TM, D_MODEL), F32)],
        compiler_params=_cp(dimension_semantics=("arbitrary",)),
    )(xhat2, rstd2, p, target, ln2_g, ln2_b, w_g, b_g, w_p, ln3_g, ln3_b)


def _ffn_bwd(dr2, a_pre, a, b, w_down, w_a, w_b, conv_w, xhat1, rstd1, ln1_g):
    t = dr2.shape[0]
    nt = t // TM
    hb = TM // HALO
    last_h = t // HALO - 1

    def body(dr_ref, drn_ref, ap_ref, a_ref, an_ref, b_ref, bn_ref, wd_ref, wa_ref, wb_ref, cw_ref,
             xh_ref, rs_ref, g1_ref, dap_ref, dbb_ref, dr1_ref, cstat_ref, lstat_ref, acc_scr):
        i, j = pl.program_id(0), pl.program_id(1)

        @pl.when((i == 0) & (j == 0))
        def _():
            cstat_ref[...] = jnp.zeros_like(cstat_ref)
            lstat_ref[...] = jnp.zeros_like(lstat_ref)

        half = TM // ROW_GROUPS
        parts = []
        for r0 in range(0, TM, half):
            rows = pl.ds(r0, half)
            last = r0 + half == TM

            def ext(ref, nxt):
                return jnp.concatenate([ref[rows], nxt[...]], axis=0) if last else ref[r0:r0 + half + HALO]

            df = _dot_nt(ext(dr_ref, drn_ref), wd_ref[...])
            a_ext, b_ext = ext(a_ref, an_ref), ext(b_ref, bn_ref)
            cdf = 0.5 * (1.0 + lax.erf(a_ext * (1.0 / math.sqrt(2.0))))
            pdf = jnp.exp(-0.5 * a_ext * a_ext) * (1.0 / math.sqrt(2.0 * math.pi))
            da = df * b_ext * (cdf + a_ext * pdf)
            if last:
                da = jnp.concatenate([da[0:half], jnp.where(i < nt - 1, da[half:], 0.0)], axis=0)
            ahead = [da[s:s + half] for s in range(3)]
            da_pre = cw_ref[2:3, :] * ahead[0] + cw_ref[1:2, :] * ahead[1] + cw_ref[0:1, :] * ahead[2]
            dbb = df[0:half] * (a_ext[0:half] * cdf[0:half])
            dap_ref[rows, :] = da_pre.astype(MXU)
            dbb_ref[rows, :] = dbb.astype(MXU)
            for kk in range(3):
                cstat_ref[j, kk:kk + 1, :] += _colsum(ahead[2 - kk] * ap_ref[rows, :])
            cstat_ref[j, 3:4, :] += _colsum(ahead[0])
            parts.append(_dot(da_pre, wa_ref[...]) + _dot(dbb, wb_ref[...]))
        part = jnp.concatenate(parts, axis=0)

        @pl.when(j == 0)
        def _():
            acc_scr[...] = ALPHA * dr_ref[...] + part

        @pl.when(j > 0)
        def _():
            acc_scr[...] += part

        @pl.when(j == N_SHARD - 1)
        def _():
            dx1 = acc_scr[...]
            xhat1 = xh_ref[...]
            lstat_ref[0:1, :] += _colsum(dx1 * xhat1)
            lstat_ref[1:2, :] += _colsum(dx1)
            dr1_ref[...] = _ln_bwd(dx1, xhat1, rs_ref[...], g1_ref[...])

    tok = lambda w: pl.BlockSpec((TM, w), lambda i, j: (i, 0))
    tokj = pl.BlockSpec((None, TM, FF_BLK), lambda i, j: (j, i, 0))
    nextj = pl.BlockSpec((None, HALO, FF_BLK), lambda i, j: (j, jnp.minimum((i + 1) * hb, last_h), 0))
    blk = lambda r, c: pl.BlockSpec((None, r, c), lambda i, j: (j, 0, 0))
    outs = [jax.ShapeDtypeStruct((N_SHARD, t, FF_BLK), MXU)] * 2 + [
        jax.ShapeDtypeStruct((t, D_MODEL), F32), jax.ShapeDtypeStruct((N_SHARD, STAT_ROWS, FF_BLK), F32),
        jax.ShapeDtypeStruct((STAT_ROWS, D_MODEL), F32)]
    return pl.pallas_call(
        body, name="ffn_bwd", grid=(nt, N_SHARD),
        in_specs=[tok(D_MODEL), pl.BlockSpec((HALO, D_MODEL), lambda i, j: (jnp.minimum((i + 1) * hb, last_h), 0)),
                  tokj, tokj, nextj, tokj, nextj, blk(FF_BLK, D_MODEL), blk(FF_BLK, D_MODEL), blk(FF_BLK, D_MODEL),
                  blk(3, FF_BLK), tok(D_MODEL), tok(1), _full((1, D_MODEL))],
        out_specs=[tokj, tokj, tok(D_MODEL), _full((N_SHARD, STAT_ROWS, FF_BLK)), _full((STAT_ROWS, D_MODEL))], out_shape=outs,
        scratch_shapes=[pltpu.VMEM((TM, D_MODEL), F32)],
        compiler_params=_cp(dimension_semantics=("arbitrary", "arbitrary")),
    )(dr2, dr2, a_pre, a, a, b, b, w_down, w_a, w_b, conv_w, xhat1, rstd1, ln1_g)


def _mix_bwd(dr1, w_o, hu, hz, mixed, attn, ln_z_g, ln_z_b, w_s, dep):
    t = dr1.shape[0]
    nchunk = TM // BLK

    def body(dr_ref, wo_ref, hu_ref, hz_ref, mx_ref, attn_ref, g_ref, b_ref, ws_ref, grp_ref, red_ref, dep_ref,
             do1_ref, do4_ref, do16_ref, dl1_ref, dl4_ref, dl16_ref, duz_ref, dws_ref, dbs_ref, zstat_ref,
             wm_scr, dzn_scr, dbsum_scr, do_scr, dl_scr):
        @pl.when(pl.program_id(0) == 0)
        def _():
            row = lax.broadcasted_iota(jnp.int32, (BLK, BLK), 0)
            col = lax.broadcasted_iota(jnp.int32, (BLK, BLK), 1)
            for g in range(N_HEADS):
                wm_scr[g] = jnp.where(col <= row, ws_ref[g], 0.0).astype(MXU)
            dws_ref[...] = jnp.zeros_like(dws_ref)
            dbsum_scr[...] = jnp.zeros_like(dbsum_scr)
            zstat_ref[...] = jnp.zeros_like(zstat_ref)

        dcat = _dot_nt(dr_ref[...], wo_ref[...])
        dattn = dcat[:, 0:D_ATTN]
        do1_ref[...] = dattn.astype(MXU)
        for cc, val in enumerate(_chunks(dattn)):
            do_scr[cc] = val
        _to_planes(do4_ref, do_scr, DILATIONS[1], LANE_CHUNKS, MXU)
        _to_planes(do16_ref, do_scr, DILATIONS[2], LANE_CHUNKS, MXU)
        delta = _dot_select(dattn * attn_ref[...], red_ref[...])
        dl1_ref[...] = delta
        dl_scr[0] = delta
        _to_planes(dl4_ref, dl_scr, DILATIONS[1], 1, F32)
        _to_planes(dl16_ref, dl_scr, DILATIONS[2], 1, F32)
        dgm = dcat[:, D_ATTN:]
        hu, hz = hu_ref[...], hz_ref[...]
        u = _gelu(hu)
        duz_ref[:, 0:D_GMLP] = (dgm * mx_ref[...] * _gelu_grad(hu)).astype(MXU)
        dmixed = dgm * u
        dmb = dmixed.astype(MXU)
        zhat, rstd = _ln_fwd(_gelu(hz))
        znb = (zhat * g_ref[...] + b_ref[...]).astype(MXU)
        dbs_acc = jnp.zeros((BLK, D_GMLP), F32)
        for ch in range(nchunk):
            rows = slice(ch * BLK, (ch + 1) * BLK)
            dbs_acc = dbs_acc + dmixed[rows]
            for g in range(N_HEADS):
                cols = slice(g * HEAD_DIM, (g + 1) * HEAD_DIM)
                dzn_scr[rows, cols] = _dot_tn(wm_scr[g], dmb[rows, cols])
                dws_ref[g] += _dot_nt(dmb[rows, cols], znb[rows, cols])
        dbsum_scr[...] += dbs_acc
        dzn = dzn_scr[...]
        zstat_ref[0:1, :] += _colsum(dzn * zhat)
        zstat_ref[1:2, :] += _colsum(dzn)
        duz_ref[:, D_GMLP:] = (_ln_bwd(dzn, zhat, rstd, g_ref[...]) * _gelu_grad(hz)).astype(MXU)

        @pl.when(pl.program_id(0) == nt - 1)
        def _():
            row = lax.broadcasted_iota(jnp.int32, (BLK, BLK), 0)
            col = lax.broadcasted_iota(jnp.int32, (BLK, BLK), 1)
            for g in range(N_HEADS):
                dws_ref[g] = jnp.where(col <= row, dws_ref[g], 0.0)
            dbs_ref[...] = lax.dot_general(grp_ref[...], dbsum_scr[...], (((1,), (1,)), ((), ())),
                                           precision=lax.Precision.HIGHEST, preferred_element_type=F32)

    nt = t // TM
    tok = lambda w: pl.BlockSpec((TM, w), lambda i: (i, 0))
    grp = jnp.asarray((np.arange(D_GMLP)[None, :] // HEAD_DIM == np.arange(N_HEADS)[:, None]).astype(np.float32))
    red = _head_reduce()
    outs = [_perm_shape(t, d, D_ATTN, MXU) for d in DILATIONS] + [_perm_shape(t, d, 128, F32) for d in DILATIONS] + [
        jax.ShapeDtypeStruct((t, 2 * D_GMLP), MXU),
        jax.ShapeDtypeStruct((N_HEADS, BLK, BLK), F32), jax.ShapeDtypeStruct((N_HEADS, BLK), F32),
        jax.ShapeDtypeStruct((STAT_ROWS, D_GMLP), F32)]
    return pl.pallas_call(
        body, name="mix_bwd", grid=(t // TM,),
        in_specs=[tok(D_MODEL), _full(w_o.shape), tok(D_GMLP), tok(D_GMLP), tok(D_GMLP), tok(D_ATTN), _full(ln_z_g.shape),
                  _full(ln_z_b.shape), _full(w_s.shape), _full(grp.shape), _full(red.shape), pl.BlockSpec(memory_space=pl.ANY)],
        out_specs=[_perm_tile(d, D_ATTN) for d in DILATIONS] + [_perm_tile(d, 128) for d in DILATIONS]
        + [tok(2 * D_GMLP), _full((N_HEADS, BLK, BLK)), _full((N_HEADS, BLK)), _full((STAT_ROWS, D_GMLP))],
        out_shape=outs,
        scratch_shapes=[pltpu.VMEM((N_HEADS, BLK, BLK), MXU), pltpu.VMEM((TM, D_GMLP), F32), pltpu.VMEM((BLK, D_GMLP), F32),
                        pltpu.VMEM((LANE_CHUNKS, TM, 128), F32), pltpu.VMEM((1, TM, 128), F32)],
        compiler_params=_cp(dimension_semantics=("arbitrary",)),
    )(dr1, w_o, hu, hz, mixed, attn, ln_z_g, ln_z_b, w_s, grp, red, dep)


def _dx_in(dqs, dks, dvs, duz, dr1, w_in, c_tab, s1_tab, s2_tab):
    t = dr1.shape[0]

    def body(dq1, dq4, dq16, dk1, dk4, dk16, dv1, dv4, dv16, duz_ref, dr_ref, w_ref, c_ref, s1_ref, s2_ref,
             dh_ref, dx_ref, acc_scr):
        sums = []
        for part, (g1, g4, g16) in enumerate(((dq1, dq4, dq16), (dk1, dk4, dk16), (dv1, dv4, dv16))):
            acc = acc_scr.at[pl.ds(part * LANE_CHUNKS, LANE_CHUNKS)]
            for cc in range(LANE_CHUNKS):
                acc[cc] = g1[:, cc * 128:(cc + 1) * 128]
            _from_planes(g4, acc, DILATIONS[1], LANE_CHUNKS, accumulate=True)
            _from_planes(g16, acc, DILATIONS[2], LANE_CHUNKS, accumulate=True)
            sums.append(_unchunk(acc_scr, LANE_CHUNKS, part * LANE_CHUNKS))
        c, s1, s2 = _tile_heads(c_ref[...]), _tile_heads(s1_ref[...]), _tile_heads(s2_ref[...])
        dh_ref[:, 0:D_ATTN] = _rope_apply_t(sums[0] * (1.0 / math.sqrt(HEAD_DIM)), c, s1, s2).astype(MXU)
        dh_ref[:, D_ATTN:2 * D_ATTN] = _rope_apply_t(sums[1], c, s1, s2).astype(MXU)
        dh_ref[:, 2 * D_ATTN:3 * D_ATTN] = sums[2].astype(MXU)
        dh_ref[:, 3 * D_ATTN:] = duz_ref[...]
        dx = ALPHA * dr_ref[...]
        for j in range(N_SHARD):
            dx = dx + _dot_nt(dh_ref[:, j * W_IN_BLK:(j + 1) * W_IN_BLK], w_ref[j])
        dx_ref[...] = dx

    tok = lambda w: pl.BlockSpec((TM, w), lambda i: (i, 0))
    outs = [jax.ShapeDtypeStruct((t, D_IN), MXU), jax.ShapeDtypeStruct((t, D_MODEL), F32)]
    return pl.pallas_call(
        body, name="dx_in", grid=(t // TM,),
        in_specs=[_perm_tile(d, D_ATTN) for d in DILATIONS] * 3
        + [tok(2 * D_GMLP), tok(D_MODEL), _full(w_in.shape), tok(128), tok(128), tok(128)],
        out_specs=[tok(D_IN), tok(D_MODEL)], out_shape=outs,
        scratch_shapes=[pltpu.VMEM((3 * LANE_CHUNKS, TM, 128), F32)],
        compiler_params=_cp(dimension_semantics=("arbitrary",)),
    )(*dqs, *dks, *dvs, duz, dr1, w_in, c_tab, s1_tab, s2_tab)


def _wgrad(name, x, dy, x_spec, dy_spec, out_spec, out_shape, grid):
    def body(x_ref, dy_ref, o_ref):
        o_ref[...] = _dot_tn(x_ref[...], dy_ref[...])

    return pl.pallas_call(
        body, name=name, grid=grid, in_specs=[x_spec, dy_spec], out_specs=out_spec,
        out_shape=jax.ShapeDtypeStruct(out_shape, F32),
        compiler_params=_cp(dimension_semantics=("arbitrary",) * len(grid)),
    )(x, dy)


def _local_step(x, p, rope, target, w_in, start_dep, late_weights, early_grads, early_grads_sent,
                ln_z_g, ln_z_b, w_s, b_s, ln1_g, ln1_b, conv_b, ln2_g, ln2_b, b_g, ln3_g, ln3_b):
    t = x.shape[0]
    half = TM
    c_tab, s1_tab, s2_tab = rope
    b_full = jnp.repeat(jnp.transpose(b_s[0]), HEAD_DIM, axis=1)
    conv_b4 = conv_b.reshape(N_SHARD, 1, FF_BLK)
    *qkvs, hu, hz, mixed, gm = _qkvuz(x, w_in, c_tab, s1_tab, s2_tab, ln_z_g, ln_z_b, w_s[0], b_full, start_dep)
    branches = [_attn_fwd(qkv, d) for qkv, d in zip(qkvs, DILATIONS)]
    w_o, w_a, w_b, conv_w, w_down, w_g, w_p = late_weights(branches[-1][1])
    attn, *lses, cat, xhat1, rstd1, x1b = _mix_ln1(
        [o for o, _ in branches], [l for _, l in branches], gm, x, w_o, ln1_g, ln1_b)
    a_pre, a_act, b_act, f = _ffn_in(x1b, w_a, w_b, conv_w, conv_b4)
    xhat2, rstd2, x2b = _ffn_out_ln2(f, w_down, xhat1, ln1_g, ln1_b, ln2_g, ln2_b)
    dr2, dgp, dpp, stat3 = _ple_loss_bwd(xhat2, rstd2, p, target, ln2_g, ln2_b, w_g, b_g, w_p, ln3_g, ln3_b)
    da_pre, dbb, dr1, cstat, stat1 = _ffn_bwd(dr2, a_pre, a_act, b_act, w_down, w_a, w_b, conv_w, xhat1, rstd1, ln1_g)

    full_t = lambda w, im: pl.BlockSpec((t, w), im)
    ffj = pl.BlockSpec((None, t, FF_BLK), lambda j, kk: (j, 0, 0))
    early = dict(
        w_ple_gate=_wgrad("dw_g", x2b, dgp, full_t(half, lambda kk, n: (0, kk)), full_t(half, lambda kk, n: (0, n)),
                          pl.BlockSpec((half, half), lambda kk, n: (kk, n)), (D_MODEL, D_MODEL), (2, 2)),
        w_ple_in=_wgrad("dw_p", p, dpp, full_t(D_PLE, lambda j: (0, 0)), full_t(ROW_BLK, lambda j: (0, j)),
                        pl.BlockSpec((None, D_PLE, ROW_BLK), lambda j: (j, 0, 0)), (N_SHARD, D_PLE, ROW_BLK), (N_SHARD,)),
        w_ff_down=_wgrad("dw_down", f, dr2, ffj, full_t(half, lambda j, n: (0, n)),
                         pl.BlockSpec((None, FF_BLK, half), lambda j, n: (j, 0, n)), (N_SHARD, FF_BLK, D_MODEL), (N_SHARD, 2)),
        w_ff_a=_wgrad("dw_a", da_pre, x1b, ffj, full_t(half, lambda j, n: (0, n)),
                      pl.BlockSpec((None, FF_BLK, half), lambda j, n: (j, 0, n)), (N_SHARD, FF_BLK, D_MODEL), (N_SHARD, 2)),
        w_ff_b=_wgrad("dw_b", dbb, x1b, ffj, full_t(half, lambda j, n: (0, n)),
                      pl.BlockSpec((None, FF_BLK, half), lambda j, n: (j, 0, n)), (N_SHARD, FF_BLK, D_MODEL), (N_SHARD, 2)),
        w_o=_wgrad("dw_o", cat, dr1, full_t(half, lambda kk, n: (0, kk)), full_t(half, lambda kk, n: (0, n)),
                   pl.BlockSpec((half, half), lambda kk, n: (kk, n)), (D_MODEL, D_MODEL), (2, 2)))
    dep = early_grads(early)

    do1, do4, do16, dl1, dl4, dl16, duz, dws, dbs, zstat = _mix_bwd(
        dr1, w_o, hu, hz, mixed, attn, ln_z_g, ln_z_b, w_s[0], dep)
    dep = early_grads_sent(duz, (stat3, stat1, zstat, cstat, dws, dbs))
    dqkv = [_attn_bwd(qkv, do, lse, dl, d, dep)
            for qkv, do, lse, dl, d in zip(qkvs, (do1, do4, do16), lses, (dl1, dl4, dl16), DILATIONS)]
    dh, grad_x = _dx_in([g[0] for g in dqkv], [g[1] for g in dqkv], [g[2] for g in dqkv], duz, dr1, w_in,
                        c_tab, s1_tab, s2_tab)
    g_w_in = _wgrad("dw_in", x, dh, full_t(half, lambda j, kk: (0, kk)), full_t(W_IN_BLK, lambda j, kk: (0, j)),
                    pl.BlockSpec((None, half, W_IN_BLK), lambda j, kk: (j, kk, 0)), (N_SHARD, D_MODEL, W_IN_BLK), (N_SHARD, 2))
    return grad_x, g_w_in


def _tile_rows(rows, mult, steps):
    if rows % mult:
        return rows
    return next(rows // k for k in range(steps, rows + 1) if rows % k == 0 and (rows // k) % mult == 0)


def _grid_spec(grid, in_specs, out_specs):
    return pltpu.PrefetchScalarGridSpec(num_scalar_prefetch=1, grid=grid, in_specs=in_specs, out_specs=out_specs)


def _on_own_steps(i, count, steps, work):
    if count == steps:
        work()
    else:
        pl.when(i < count)(work)


def _place_shards(name, ws, dtypes, place, dep):
    n = len(ws)
    tiles = [_tile_rows(w.shape[0], 16, 8) for w in ws]
    counts = [w.shape[0] // t for w, t in zip(ws, tiles)]
    steps = max(counts)

    def body(s_ref, *refs):
        i = pl.program_id(0)
        for a in range(n):
            def work(a=a):
                refs[n + 1 + a][...] = refs[a][...].astype(dtypes[a])
            _on_own_steps(i, counts[a], steps, work)

    def tile(a, lead):
        last = counts[a] - 1
        if lead:
            return pl.BlockSpec((None, tiles[a], ws[a].shape[1]), lambda i, s: (s[0], jnp.minimum(i, last), 0))
        return pl.BlockSpec((tiles[a], ws[a].shape[1]), lambda i, s: (jnp.minimum(i, last), 0))

    return pl.pallas_call(
        body, name=name,
        grid_spec=_grid_spec((steps,), [tile(a, False) for a in range(n)] + [pl.BlockSpec(memory_space=pl.ANY)],
                             [tile(a, True) for a in range(n)]),
        out_shape=[jax.ShapeDtypeStruct((N_SHARD, *w.shape), dt) for w, dt in zip(ws, dtypes)],
        compiler_params=_cp())(place, *ws, dep)


def _pair_sums(name, mines, gots, place):
    n = len(mines)
    tiles = [_tile_rows(g.shape[1], 16, 2) for g in gots]
    per_blk = [g.shape[1] // t for g, t in zip(gots, tiles)]
    counts = [N_SHARD * nh for nh in per_blk]
    steps = max(counts)

    def body(s_ref, *refs):
        i = pl.program_id(0)
        for a in range(n):
            def work(a=a):
                refs[2 * n + a][...] = (refs[a][...] + refs[n + a][...]).astype(BF16)
            _on_own_steps(i, counts[a], steps, work)

    def tile(a, mine):
        nh, last = per_blk[a], counts[a] - 1

        def index(i, s):
            g = jnp.minimum(i, last)
            return (g // nh, (s[1] * nh if mine else 0) + g % nh, 0)

        return pl.BlockSpec((None, tiles[a], gots[a].shape[2]), index)

    return pl.pallas_call(
        body, name=name,
        grid_spec=_grid_spec((steps,), [tile(a, True) for a in range(n)] + [tile(a, False) for a in range(n)],
                             [tile(a, False) for a in range(n)]),
        out_shape=[jax.ShapeDtypeStruct(g.shape, BF16) for g in gots], compiler_params=_cp())(place, *mines, *gots)


def _chip_sums(name, owns, landeds, place, dep):
    n = len(owns)
    tiles = [_tile_rows(o.shape[1], 16, 8) for o in owns]
    counts = [o.shape[1] // t for o, t in zip(owns, tiles)]
    steps = max(counts)

    def body(s_ref, *refs):
        i = pl.program_id(0)
        for a in range(n):
            def work(a=a):
                own, l1, l2, l3 = (refs[4 * a + k][...].astype(F32) for k in range(4))
                refs[4 * n + 1 + a][...] = ((own + l1) + l2) + l3
            _on_own_steps(i, counts[a], steps, work)

    def slot(a, d):
        last = counts[a] - 1
        return pl.BlockSpec((None, tiles[a], owns[a].shape[2]), lambda i, s: ((s[0] + d) % N_SHARD, jnp.minimum(i, last), 0))

    def out(a):
        nh, last = counts[a], counts[a] - 1
        return pl.BlockSpec((tiles[a], owns[a].shape[2]), lambda i, s: (s[1] * nh + jnp.minimum(i, last), 0))

    operands = [x for o, l in zip(owns, landeds) for x in (o, l, l, l)]
    return pl.pallas_call(
        body, name=name,
        grid_spec=_grid_spec((steps,), [slot(a, d) for a in range(n) for d in range(4)] + [pl.BlockSpec(memory_space=pl.ANY)],
                             [out(a) for a in range(n)]),
        out_shape=[jax.ShapeDtypeStruct((2 * o.shape[1], o.shape[2]), F32) for o in owns],
        compiler_params=_cp())(place, *operands, dep)


def _adamw_math(w, g, m, v):
    m = ADAM_B1 * m + (1.0 - ADAM_B1) * g
    v = ADAM_B2 * v + (1.0 - ADAM_B2) * (g * g)
    m_hat = m / (1.0 - ADAM_B1 ** ADAM_STEP)
    v_hat = v / (1.0 - ADAM_B2 ** ADAM_STEP)
    delta = -ADAM_LR * (m_hat / (jnp.sqrt(v_hat) + ADAM_EPS) + ADAM_WD * w)
    return delta, m, v


def _adamw_shards(name, ws, gs, ms, vs):
    n = len(ws)
    tiles = [_tile_rows(w.shape[1], 8, 8) for w in ws]
    counts = [w.shape[1] // t for w, t in zip(ws, tiles)]
    steps = max(counts)

    def body(*refs):
        i = pl.program_id(0)
        for a in range(n):
            def work(a=a):
                w_ref, g_ref, m_ref, v_ref = refs[4 * a:4 * a + 4]
                d_ref, nm_ref, nv_ref = refs[4 * n + 3 * a:4 * n + 3 * a + 3]
                d_ref[...], nm_ref[...], nv_ref[...] = _adamw_math(w_ref[...], g_ref[...], m_ref[...], v_ref[...])
            _on_own_steps(i, counts[a], steps, work)

    def tile(a, lead):
        last, c = counts[a] - 1, ws[a].shape[2]
        if lead:
            return pl.BlockSpec((None, tiles[a], c), lambda i: (0, jnp.minimum(i, last), 0))
        return pl.BlockSpec((tiles[a], c), lambda i: (jnp.minimum(i, last), 0))

    res = pl.pallas_call(
        body, name=name, grid=(steps,),
        in_specs=[tile(a, lead) for a in range(n) for lead in (True, False, True, True)],
        out_specs=[tile(a, True) for a in range(n) for _ in range(3)],
        out_shape=[jax.ShapeDtypeStruct(w.shape, F32) for w in ws for _ in range(3)],
        compiler_params=_cp())(*[x for quad in zip(ws, gs, ms, vs) for x in quad])
    return [tuple(res[3 * a:3 * a + 3]) for a in range(n)]


MESH = pl.DeviceIdType.MESH
ANY = pl.BlockSpec(memory_space=pl.ANY)


def _place():
    x, y, c = lax.axis_index("x"), lax.axis_index("y"), lax.axis_index("c")
    chips = [(1 - x, y), (x, 1 - y), (1 - x, 1 - y)]
    return x, y, c, 2 * x + y, chips


def _remote(src, dst, send_sem, recv_sem, dev):
    return pltpu.make_async_remote_copy(src_ref=src, dst_ref=dst, send_sem=send_sem, recv_sem=recv_sem,
                                        device_id=dev, device_id_type=MESH)


def _half(ref, hc, rows):
    return ref.at[pl.ds(hc * (rows // 2), rows // 2)]


def _sibling_join(blocks, tag):
    n = len(blocks)

    def body(*refs):
        outs = refs[n:2 * n]
        send, recv = refs[2 * n:]
        x, y, c, _, _ = _place()
        cps = []
        for a in range(n):
            h = blocks[a].shape[0] // 2
            mine = outs[a].at[pl.ds(c * h, h)]
            cp = _remote(mine, mine, send.at[a], recv.at[a], (x, y, 1 - c))
            cp.start()
            cps.append(cp)
        for a, cp in enumerate(cps):
            h = blocks[a].shape[0] // 2
            theirs = outs[a].at[pl.ds((1 - c) * h, h)]
            _remote(theirs, theirs, send.at[a], recv.at[a], (x, y, 1 - c)).wait_recv()
            cp.wait_send()

    sem = pltpu.SemaphoreType.DMA
    return pl.pallas_call(body, name=f"rs_sibling_join_{tag}", in_specs=[ANY] * n, out_specs=[ANY] * n,
                          out_shape=[jax.ShapeDtypeStruct(b_.shape, b_.dtype) for b_ in blocks],
                          input_output_aliases={a: a for a in range(n)},
                          scratch_shapes=[sem((n,)), sem((n,))])(*blocks)


HBM = pl.BlockSpec(memory_space=pltpu.HBM)
SEM = pl.BlockSpec(memory_space=pltpu.SEMAPHORE)
TOKEN = jax.ShapeDtypeStruct((8, 128), F32)


def _in_flight_params():
    return pltpu.CompilerParams(has_side_effects=pltpu.SideEffectType.DATAFLOW_SIDE_EFFECTING)


def _in_hbm(a):
    return pltpu.with_memory_space_constraint(a, pltpu.HBM)


def _gather_piece(ref, rows, split, slot, hc):
    return _half(ref.at[slot], hc, rows) if split else ref.at[slot]


def _gather_start(stacks, split, after, tag):
    n = len(stacks)

    def body(*refs):
        ins = refs[:n]
        send, recv = refs[n + 1], refs[n + 2]
        token = refs[2 * n + 3]
        _, _, c, j, chips = _place()
        for a in range(n):
            mine = _gather_piece(ins[a], stacks[a].shape[1], split[a], j, c)
            for t in range(3):
                _remote(mine, mine, send.at[3 * a + t], recv.at[3 * a + t], (*chips[t], c)).start()
        token[...] = jnp.zeros_like(token)

    sems = pltpu.SemaphoreType.DMA((3 * n,))
    res = pl.pallas_call(
        body, name=f"gather_start_{tag}", in_specs=[HBM] * n + [ANY],
        out_specs=[SEM, SEM] + [HBM] * n + [pl.BlockSpec(memory_space=pltpu.VMEM)],
        out_shape=[sems, sems] + [pltpu.HBM(s.shape, s.dtype) for s in stacks] + [TOKEN],
        input_output_aliases={a: a + 2 for a in range(n)}, compiler_params=_in_flight_params(),
    )(*[_in_hbm(s) for s in stacks], after)
    return res[0], res[1], res[2:2 + n], res[2 + n]


def _gather_wait(send, recv, stacks, split, after, tag):
    n = len(stacks)

    def body(*refs):
        ins = refs[:n]
        send_ref, recv_ref = refs[n], refs[n + 1]
        _, _, c, j, chips = _place()
        for a in range(n):
            rows = stacks[a].shape[1]
            mine = _gather_piece(ins[a], rows, split[a], j, c)
            for t, (px, py) in enumerate(chips):
                theirs = _gather_piece(ins[a], rows, split[a], 2 * px + py, c)
                _remote(mine, mine, send_ref.at[3 * a + t], recv_ref.at[3 * a + t], (px, py, c)).wait_send()
                _remote(theirs, theirs, send_ref.at[3 * a + t], recv_ref.at[3 * a + t], (px, py, c)).wait_recv()

    return pl.pallas_call(
        body, name=f"gather_wait_{tag}", in_specs=[HBM] * n + [SEM, SEM, ANY], out_specs=[HBM] * n,
        out_shape=[pltpu.HBM(s.shape, s.dtype) for s in stacks],
        input_output_aliases={a: a for a in range(n)}, compiler_params=_in_flight_params(),
    )(*stacks, send, recv, after)


def _gather_forward(stacks, split, tag):
    idx = [a for a in range(len(stacks)) if split[a]]
    n = len(idx)

    def body(*refs):
        outs = refs[n:2 * n]
        send, recv = refs[2 * n:]
        x, y, c, _, chips = _place()
        sends = []
        for t, (px, py) in enumerate(chips):
            for a in range(n):
                blk = _half(outs[a].at[2 * px + py], c, stacks[idx[a]].shape[1])
                cp = _remote(blk, blk, send.at[a, t], recv.at[a, t], (x, y, 1 - c))
                cp.start()
                sends.append(cp)
        for t, (px, py) in enumerate(chips):
            for a in range(n):
                blk = _half(outs[a].at[2 * px + py], 1 - c, stacks[idx[a]].shape[1])
                _remote(blk, blk, send.at[a, t], recv.at[a, t], (x, y, 1 - c)).wait_recv()
        for cp in sends:
            cp.wait_send()

    sem = pltpu.SemaphoreType.DMA
    res = pl.pallas_call(
        body, name=f"gather_forward_{tag}", in_specs=[ANY] * n, out_specs=[ANY] * n,
        out_shape=[jax.ShapeDtypeStruct(stacks[a].shape, stacks[a].dtype) for a in idx],
        input_output_aliases={a: a for a in range(n)}, scratch_shapes=[sem((n, 3)), sem((n, 3))],
    )(*[stacks[a] for a in idx])
    out = list(stacks)
    for a, r in zip(idx, res):
        out[a] = r
    return out


def _swap_start(grads, tag):
    n = len(grads)

    def body(*refs):
        ins, gots = refs[:n], refs[n:2 * n]
        send, recv = refs[2 * n], refs[2 * n + 1]
        token = refs[4 * n + 2]
        x, y, c, _, _ = _place()
        for a in range(n):
            h = grads[a].shape[1] // 2
            _remote(ins[a].at[:, pl.ds((1 - c) * h, h)], gots[a], send.at[a], recv.at[a], (x, y, 1 - c)).start()
        token[...] = jnp.zeros_like(token)

    sems = pltpu.SemaphoreType.DMA((n,))
    halves = [(g.shape[0], g.shape[1] // 2, g.shape[2]) for g in grads]
    res = pl.pallas_call(
        body, name=f"swap_start_{tag}", in_specs=[HBM] * (2 * n),
        out_specs=[SEM, SEM] + [HBM] * (2 * n) + [pl.BlockSpec(memory_space=pltpu.VMEM)],
        out_shape=[sems, sems] + [pltpu.HBM(g.shape, g.dtype) for g in grads] + [pltpu.HBM(s, F32) for s in halves] + [TOKEN],
        input_output_aliases={a: a + 2 for a in range(2 * n)}, compiler_params=_in_flight_params(),
    )(*[_in_hbm(g) for g in grads], *[_in_hbm(lax.empty(s, F32)) for s in halves])
    return res[0], res[1], res[2:2 + n], res[2 + n:2 + 2 * n], res[2 + 2 * n]


def _swap_wait(send, recv, grads, gots, after, tag):
    n = len(grads)

    def body(*refs):
        ins, lnd = refs[:n], refs[n:2 * n]
        send_ref, recv_ref = refs[2 * n], refs[2 * n + 1]
        x, y, c, _, _ = _place()
        for a in range(n):
            h = grads[a].shape[1] // 2
            cp = _remote(ins[a].at[:, pl.ds((1 - c) * h, h)], lnd[a], send_ref.at[a], recv_ref.at[a], (x, y, 1 - c))
            cp.wait_send()
            cp.wait_recv()

    bufs = [pltpu.HBM(g.shape, g.dtype) for g in grads] + [pltpu.HBM(g.shape, g.dtype) for g in gots]
    res = pl.pallas_call(
        body, name=f"swap_wait_{tag}", in_specs=[HBM] * (2 * n) + [SEM, SEM, ANY], out_specs=[HBM] * (2 * n),
        out_shape=bufs, input_output_aliases={a: a for a in range(2 * n)}, compiler_params=_in_flight_params(),
    )(*grads, *gots, send, recv, after)
    return res[:n], res[n:]


def _exchange_start(parts, tag):
    n = len(parts)

    def body(*refs):
        ins, lands = refs[:n], refs[n:2 * n]
        send, recv = refs[2 * n], refs[2 * n + 1]
        token = refs[4 * n + 2]
        _, _, c, j, chips = _place()
        for t, (px, py) in enumerate(chips):
            for a in range(n):
                _remote(ins[a].at[2 * px + py], lands[a].at[j], send.at[3 * a + t], recv.at[3 * a + t], (px, py, c)).start()
        token[...] = jnp.zeros_like(token)

    sems = pltpu.SemaphoreType.DMA((3 * n,))
    bufs = [pltpu.HBM(p.shape, p.dtype) for p in parts]
    res = pl.pallas_call(
        body, name=f"exchange_start_{tag}", in_specs=[HBM] * (2 * n),
        out_specs=[SEM, SEM] + [HBM] * (2 * n) + [pl.BlockSpec(memory_space=pltpu.VMEM)],
        out_shape=[sems, sems] + bufs + bufs + [TOKEN],
        input_output_aliases={a: a + 2 for a in range(2 * n)}, compiler_params=_in_flight_params(),
    )(*[_in_hbm(p) for p in parts], *[_in_hbm(lax.empty(p.shape, p.dtype)) for p in parts])
    return res[0], res[1], res[2:2 + n], res[2 + n:2 + 2 * n], res[2 + 2 * n]


def _exchange_wait(send, recv, parts, lands, after, tag):
    n = len(parts)

    def body(*refs):
        ins, lnd = refs[:n], refs[n:2 * n]
        send_ref, recv_ref = refs[2 * n], refs[2 * n + 1]
        _, _, c, j, chips = _place()
        for t, (px, py) in enumerate(chips):
            jt = 2 * px + py
            for a in range(n):
                _remote(ins[a].at[jt], lnd[a].at[j], send_ref.at[3 * a + t], recv_ref.at[3 * a + t], (px, py, c)).wait_send()
                _remote(ins[a].at[jt], lnd[a].at[jt], send_ref.at[3 * a + t], recv_ref.at[3 * a + t], (px, py, c)).wait_recv()

    bufs = [pltpu.HBM(p.shape, p.dtype) for p in parts]
    res = pl.pallas_call(
        body, name=f"exchange_wait_{tag}", in_specs=[HBM] * (2 * n) + [SEM, SEM, ANY], out_specs=[HBM] * (2 * n),
        out_shape=bufs + bufs, input_output_aliases={a: a for a in range(2 * n)}, compiler_params=_in_flight_params(),
    )(*parts, *lands, send, recv, after)
    return res[:n], res[n:]


def _small_chip_sums(arrs):
    n = len(arrs)

    def body(*refs):
        ins, outs = refs[:n], refs[n:2 * n]
        sib = refs[2 * n:3 * n]
        send, recv = refs[3 * n:]
        x, y, c, j, _ = _place()
        swaps = [_remote(ins[a], sib[a], send.at[a], recv.at[a], (x, y, 1 - c)) for a in range(n)]
        for cp in swaps:
            cp.start()
        for a in range(n):
            swaps[a].wait_recv()
            outs[a][j] = ins[a][...] + sib[a][...]
        for cp in swaps:
            cp.wait_send()

    sem = pltpu.SemaphoreType.DMA
    vm = pl.BlockSpec(memory_space=pltpu.VMEM)
    return pl.pallas_call(
        body, name="small_chip_sums", in_specs=[vm] * n, out_specs=[vm] * n,
        out_shape=[jax.ShapeDtypeStruct((N_SHARD, *a.shape), F32) for a in arrs],
        scratch_shapes=[pltpu.VMEM(a.shape, F32) for a in arrs] + [sem((n,)), sem((n,))],
        compiler_params=_cp(),
    )(*arrs)


def _small_totals(stacks):
    n = len(stacks)

    def body(*refs):
        for a in range(n):
            refs[n + a][...] = ((refs[a][0] + refs[a][1]) + refs[a][2]) + refs[a][3]

    return pl.pallas_call(body, name="small_totals", out_shape=[jax.ShapeDtypeStruct(s.shape[1:], F32) for s in stacks],
                          compiler_params=_cp())(*stacks)


SMALL_1024 = ("ln1_g", "ln1_b", "ln2_g", "ln2_b", "b_ple_gate", "ln3_g", "ln3_b")


def _adamw_small(red3, red1, redz, g_conv_w, redc, red_ws, red_bs, params):
    shape2d = {"ln_z_g": (1, D_GMLP), "ln_z_b": (1, D_GMLP), "w_s": (N_HEADS * BLK, BLK), "b_s": (N_HEADS, BLK),
               "conv_w": (3, FF_BLK), "conv_b": (N_SHARD, FF_BLK), **{k: (1, D_MODEL) for k in SMALL_1024}}
    names = list(shape2d)
    flat = [a.reshape(shape2d[k]) for k in names for a in params[k]]

    def body(r3, r1, rz, gcw, rc, rws, rbs, *refs):
        ins, outs = refs[:3 * len(names)], refs[3 * len(names):]

        def grad_of(k):
            if k == "w_s":
                return rws[...]
            if k == "b_s":
                return rbs[...]
            if k == "conv_w":
                return gcw[0:3, :]
            if k == "conv_b":
                return jnp.concatenate([rc[j * STAT_ROWS + 3:j * STAT_ROWS + 4, :] for j in range(N_SHARD)], axis=0)
            src, row = {"ln3_g": (r3, 0), "ln3_b": (r3, 1), "b_ple_gate": (r3, 2), "ln2_g": (r3, 3), "ln2_b": (r3, 4),
                        "ln1_g": (r1, 0), "ln1_b": (r1, 1), "ln_z_g": (rz, 0), "ln_z_b": (rz, 1)}[k]
            return src[row:row + 1, :]

        for i, k in enumerate(names):
            w_ref, m_ref, v_ref = ins[3 * i:3 * i + 3]
            g_ref, d_ref, nm_ref, nv_ref = outs[4 * i:4 * i + 4]
            g = grad_of(k)
            g_ref[...] = g
            d_ref[...], nm_ref[...], nv_ref[...] = _adamw_math(w_ref[...], g, m_ref[...], v_ref[...])

    res = pl.pallas_call(
        body, name="adamw_small",
        out_shape=[jax.ShapeDtypeStruct(shape2d[k], F32) for k in names for _ in range(4)],
        compiler_params=_cp(),
    )(red3, red1, redz, g_conv_w, redc, red_ws, red_bs, *flat)
    return {k: tuple(r.reshape(params[k][0].shape) for r in res[4 * i:4 * i + 4]) for i, k in enumerate(names)}


WEIGHTS = ("w_in", "ln_z_g", "ln_z_b", "w_s", "b_s", "w_o", "ln1_g", "ln1_b", "w_ff_a", "w_ff_b", "conv_w", "conv_b",
           "w_ff_down", "ln2_g", "ln2_b", "w_ple_gate", "b_ple_gate", "w_ple_in", "ln3_g", "ln3_b")
BIG = ("w_in", "w_o", "w_ff_a", "w_ff_b", "w_ff_down", "w_ple_gate", "w_ple_in")
TRANSPOSED = ("w_ff_a", "w_ff_b")
LATE = ("w_o", "w_ff_a", "w_ff_b", "w_ff_down", "w_ple_gate", "w_ple_in", "conv_w")


def kernel(x, p, positions, w_in, ln_z_g, ln_z_b, w_s, b_s, w_o, ln1_g, ln1_b, w_ff_a, w_ff_b, conv_w, conv_b, w_ff_down, ln2_g, ln2_b, w_ple_gate, b_ple_gate, w_ple_in, ln3_g, ln3_b, loss_target, m_w_in, m_ln_z_g, m_ln_z_b, m_w_s, m_b_s, m_w_o, m_ln1_g, m_ln1_b, m_w_ff_a, m_w_ff_b, m_conv_w, m_conv_b, m_w_ff_down, m_ln2_g, m_ln2_b, m_w_ple_gate, m_b_ple_gate, m_w_ple_in, m_ln3_g, m_ln3_b, v_w_in, v_ln_z_g, v_ln_z_b, v_w_s, v_b_s, v_w_o, v_ln1_g, v_ln1_b, v_w_ff_a, v_w_ff_b, v_conv_w, v_conv_b, v_w_ff_down, v_ln2_g, v_ln2_b, v_w_ple_gate, v_b_ple_gate, v_w_ple_in, v_ln3_g, v_ln3_b):
    args = locals()
    w = {k: args[k] for k in WEIGHTS}
    m = {k: args["m_" + k] for k in WEIGHTS}
    v = {k: args["v_" + k] for k in WEIGHTS}

    for k in TRANSPOSED:
        w[k], m[k], v[k] = (jnp.swapaxes(a, 1, 2) for a in (w[k], m[k], v[k]))

    chip = 2 * lax.axis_index("x") + lax.axis_index("y")
    place = jnp.stack([chip, lax.axis_index("c")]).astype(jnp.int32)
    stack = dict(zip(["w_in"], _place_shards("cast_w_in", [w["w_in"][0]], [MXU], place, place)))
    i_send, i_recv, in_flight, dep = _gather_start([stack["w_in"]], [True], place, "w_in")
    stack.update(zip(LATE, _place_shards("cast_late", [w[k][0] for k in LATE],
                                         [F32 if k == "conv_w" else MXU for k in LATE], place, dep)))
    rope = _rope_tables(positions, x.shape[1], stack[LATE[-1]])
    landed_in = _gather_wait(i_send, i_recv, in_flight, [True], rope[0], "w_in")
    w_in_full, = _gather_forward(landed_in, [True], "w_in")
    split_late = [k != "conv_w" for k in LATE]
    g_send, g_recv, late_flight, start_dep = _gather_start([stack[k] for k in LATE], split_late, w_in_full, "late")

    def late_weights(after):
        landed = _gather_wait(g_send, g_recv, late_flight, split_late, after, "late")
        fw = dict(zip(LATE, _gather_forward(landed, split_late, "late")))
        return (fw["w_o"].reshape(D_MODEL, D_MODEL), fw["w_ff_a"], fw["w_ff_b"], fw["conv_w"], fw["w_ff_down"],
                fw["w_ple_gate"].reshape(D_MODEL, D_MODEL), fw["w_ple_in"])

    def swap_started(names, grads, tag):
        stacked = [g.reshape(N_SHARD, *w[k].shape[1:]) for k, g in zip(names, grads)]
        return (names, tag, *_swap_start(stacked, tag))

    def partial_sums(swap, after):
        names, tag, send, recv, stacked, gots, _ = swap
        stacked, got = _swap_wait(send, recv, stacked, gots, after, tag)
        pair = _pair_sums(f"rs_pair_{tag}", stacked, got, place)
        return (names, tag, *_exchange_start(pair, tag))

    def reduced(trip, after, dep):
        names, tag, send, recv, pair, lands, _ = trip
        pair, landed = _exchange_wait(send, recv, pair, lands, after, tag)
        blocks = _chip_sums(f"rs_sum_{tag}", pair, landed, place, dep)
        return dict(zip(names, _sibling_join(blocks, tag)))

    trips = {}

    def early_grads(grads):
        trips["swap"] = swap_started(list(grads), list(grads.values()), "early")
        return trips["swap"][-1]

    def early_grads_sent(after, small):
        trips["early"] = partial_sums(trips["swap"], after)
        stat3, stat1, zstat, cstat, dws, dbs = small
        sums = _small_chip_sums([stat3, stat1, zstat, cstat.reshape(N_SHARD * STAT_ROWS, FF_BLK),
                                 dws.reshape(N_HEADS * BLK, BLK), dbs])
        trips["small"] = _gather_start(sums, [False] * len(sums), trips["early"][-1], "small")
        return trips["small"][-1]

    grad_x, g_w_in = _local_step(
        x[0], p[0, 0], rope, loss_target[0], w_in_full, start_dep, late_weights, early_grads, early_grads_sent,
        ln_z_g, ln_z_b, w_s, b_s, ln1_g, ln1_b, conv_b, ln2_g, ln2_b, b_ple_gate, ln3_g, ln3_b)

    trips["w_in"] = partial_sums(swap_started(["w_in"], [g_w_in], "w_in"), g_w_in)
    out = {}

    def adamw(red, tag):
        names = list(red)
        steps = _adamw_shards(f"adamw_{tag}", [w[k] for k in names], [red[k] for k in names], [m[k] for k in names],
                              [v[k] for k in names])
        for k, (d, nm, nv) in zip(names, steps):
            out[k] = (red[k].reshape(w[k].shape), d, nm, nv)

    adamw(reduced(trips["early"], grad_x, trips["w_in"][-1]), "early")
    adamw(reduced(trips["w_in"], out["w_o"][3], start_dep), "w_in")
    for k in TRANSPOSED:
        out[k] = tuple(jnp.swapaxes(a, 1, 2) for a in out[k])

    s_send, s_recv, s_flight, _ = trips["small"]
    red3, red1, redz, redc, red_ws, red_bs = _small_totals(
        _gather_wait(s_send, s_recv, s_flight, [False] * len(s_flight), out["w_in"][3], "small"))
    loss = (0.5 / D_MODEL) * jnp.sum(red3[5])
    g_conv_w = lax.dynamic_slice_in_dim(redc, chip * STAT_ROWS, STAT_ROWS, 0)
    names_small = [k for k in WEIGHTS if k not in BIG]
    out.update(_adamw_small(red3, red1, redz, g_conv_w, redc, red_ws, red_bs, {k: (w[k], m[k], v[k]) for k in names_small}))

    return (loss, grad_x[None], *[out[k][0] for k in WEIGHTS], *[out[k][1] for k in WEIGHTS],
            *[out[k][2] for k in WEIGHTS], *[out[k][3] for k in WEIGHTS])
```

```python
import functools
import math

import numpy as np
import jax
import jax.numpy as jnp
from jax import lax
from jax.experimental import pallas as pl
from jax.experimental.pallas import tpu as pltpu

F32 = jnp.float32
BF16 = jnp.bfloat16
MXU = BF16

D_MODEL = 1024
HEAD_DIM = 64
N_HEADS = 8
D_ATTN = 512
D_GMLP = 512
D_IN = 2560
DILATIONS = (1, 4, 16)
BLK = 128
ROPE_THETA = 500000.0
ROPE_DIM = 16
D_FF = 2816
D_PLE = 256
LN_EPS = 1e-5
ALPHA = 2.0 ** 0.25
NEG_INF = -1e30
N_SHARD = 4
W_IN_BLK = D_IN // N_SHARD
FF_BLK = D_FF // N_SHARD
ROW_BLK = D_MODEL // N_SHARD
ADAM_LR, ADAM_B1, ADAM_B2, ADAM_EPS, ADAM_WD, ADAM_STEP = 0.001, 0.9, 0.999, 1e-08, 0.01, 10

TM = 512
HALO = 8
ROW_GROUPS = 2
VMEM_LIMIT = 56 * 1024 * 1024


def _cp(**kw):
    return pltpu.CompilerParams(vmem_limit_bytes=VMEM_LIMIT, **kw)


def _full(shape):
    n = len(shape)
    return pl.BlockSpec(shape, lambda *_: (0,) * n)


def _gelu(x):
    return 0.5 * x * (1.0 + lax.erf(x * (1.0 / math.sqrt(2.0))))


def _gelu_grad(x):
    return 0.5 * (1.0 + lax.erf(x * (1.0 / math.sqrt(2.0)))) + x * jnp.exp(-0.5 * x * x) * (1.0 / math.sqrt(2.0 * math.pi))


def _ln_fwd(r):
    mu = jnp.mean(r, axis=-1, keepdims=True)
    xc = r - mu
    var = jnp.mean(xc * xc, axis=-1, keepdims=True)
    rstd = lax.rsqrt(var + LN_EPS)
    return xc * rstd, rstd


def _ln_bwd(dy, xhat, rstd, g):
    dxh = dy * g
    m1 = jnp.mean(dxh, axis=-1, keepdims=True)
    m2 = jnp.mean(dxh * xhat, axis=-1, keepdims=True)
    return rstd * (dxh - m1 - xhat * m2)


def _dot(a, b):
    return jnp.dot(a.astype(MXU), b.astype(MXU), preferred_element_type=F32)


def _dot_nt(a, b):
    return lax.dot_general(a.astype(MXU), b.astype(MXU), (((1,), (1,)), ((), ())), preferred_element_type=F32)


def _dot_tn(a, b):
    return lax.dot_general(a.astype(MXU), b.astype(MXU), (((0,), (0,)), ((), ())), preferred_element_type=F32)


def _colsum(v):
    return jnp.sum(v, axis=0, keepdims=True)


def _rope_tables(positions, t, dep):
    inv = np.float32(ROPE_THETA) ** (-np.arange(0, ROPE_DIM, 2, dtype=np.float32) / np.float32(ROPE_DIM))
    half = ROPE_DIM // 2
    pos_rep = jnp.repeat(positions.reshape(t // 16, 16), half, axis=1)
    inv_row = jnp.asarray(np.tile(inv, 16)[None, :], F32)

    def trig_body(pos_ref, inv_ref, dep_ref, cos_ref, sin_ref):
        ang = pos_ref[...].astype(F32) * inv_ref[...]
        cos_ref[...] = jnp.cos(ang)
        sin_ref[...] = jnp.sin(ang)

    vm = pl.BlockSpec(memory_space=pltpu.VMEM)
    cos8, sin8 = pl.pallas_call(
        trig_body, name="rope_trig", in_specs=[vm, vm, pl.BlockSpec(memory_space=pl.ANY)], out_specs=[vm, vm],
        out_shape=(jax.ShapeDtypeStruct((t // 16, 128), F32), jax.ShapeDtypeStruct((t // 16, 128), F32)),
    )(pos_rep, inv_row, dep)
    cos8 = cos8.reshape(t, half)
    sin8 = sin8.reshape(t, half)

    lane = np.arange(128) % HEAD_DIM
    sel = (np.arange(half)[:, None] == (lane % half)[None, :])
    e_cos = (sel & (lane < ROPE_DIM)[None, :]).astype(np.float32)
    e_s1 = -(sel & (lane < half)[None, :]).astype(np.float32)
    e_s2 = (sel & ((lane >= half) & (lane < ROPE_DIM))[None, :]).astype(np.float32)
    ones = (lane >= ROPE_DIM).astype(np.float32)[None, :]

    def expand_body(cos_ref, sin_ref, ec_ref, e1_ref, e2_ref, ones_ref, c_ref, s1_ref, s2_ref):
        hp = lax.Precision.HIGHEST
        c_ref[...] = jnp.dot(cos_ref[...], ec_ref[...], precision=hp, preferred_element_type=F32) + ones_ref[...]
        s1_ref[...] = jnp.dot(sin_ref[...], e1_ref[...], precision=hp, preferred_element_type=F32)
        s2_ref[...] = jnp.dot(sin_ref[...], e2_ref[...], precision=hp, preferred_element_type=F32)

    tab = jax.ShapeDtypeStruct((t, 128), F32)
    return pl.pallas_call(expand_body, name="rope_expand", out_shape=(tab, tab, tab), compiler_params=_cp())(
        cos8, sin8, jnp.asarray(e_cos), jnp.asarray(e_s1), jnp.asarray(e_s2), jnp.asarray(ones))


def _tile_heads(tab):
    return jnp.concatenate([tab] * (D_ATTN // 128), axis=1)


def _rope_apply(v, c, s1, s2):
    n = v.shape[1]
    half = ROPE_DIM // 2
    return v * c + pltpu.roll(v, n - half, 1) * s1 + pltpu.roll(v, half, 1) * s2


def _rope_apply_t(g, c, s1, s2):
    n = g.shape[1]
    half = ROPE_DIM // 2
    return g * c + pltpu.roll(g * s1, half, 1) + pltpu.roll(g * s2, n - half, 1)


LANE_CHUNKS = D_ATTN // 128
HEAD_LANES = 128 // N_HEADS


def _perm_shape(t, d, w, dtype):
    return jax.ShapeDtypeStruct((d, t // d, w), dtype)


def _perm_tile(d, w):
    return pl.BlockSpec((None if d == 1 else d, TM // d, w), lambda i: (0, i, 0))


def _to_planes(ref, scr, d, n_chunks, dtype):
    for r in range(d):
        for cc in range(n_chunks):
            ref[r, :, cc * 128:(cc + 1) * 128] = scr.at[cc][pl.ds(r, TM // d, stride=d), :].astype(dtype)


def _from_planes(ref, scr, d, n_chunks, accumulate=False):
    for r in range(d):
        for cc in range(n_chunks):
            rows = scr.at[cc]
            val = ref[r, :, cc * 128:(cc + 1) * 128].astype(F32)
            if accumulate:
                rows[pl.ds(r, TM // d, stride=d), :] += val
            else:
                rows[pl.ds(r, TM // d, stride=d), :] = val


def _chunks(val):
    return [val[:, cc * 128:(cc + 1) * 128] for cc in range(val.shape[1] // 128)]


def _unchunk(scr, n_chunks, base=0):
    return jnp.concatenate([scr[base + cc] for cc in range(n_chunks)], axis=1)


def _head_expand():
    src = np.arange(128)[:, None]
    dst = np.arange(D_ATTN)[None, :]
    return jnp.asarray((src == (dst // HEAD_DIM) * HEAD_LANES).astype(np.float32))


def _head_reduce():
    src = np.arange(D_ATTN)[:, None]
    dst = np.arange(128)[None, :]
    return jnp.asarray((src // HEAD_DIM == dst // HEAD_LANES).astype(np.float32))


def _dot_select(a, sel):
    hi = a.astype(BF16)
    lo = (a - hi.astype(F32)).astype(BF16)
    sel = sel.astype(BF16)
    return jnp.dot(hi, sel, preferred_element_type=F32) + jnp.dot(lo, sel, preferred_element_type=F32)


def _qkvuz(x, w_in, c_tab, s1_tab, s2_tab, ln_z_g, ln_z_b, w_s, b_full, dep):
    t = x.shape[0]
    nchunk = TM // BLK

    def body(x_ref, w_ref, c_ref, s1_ref, s2_ref, g_ref, b_ref, ws_ref, bf_ref, dep_ref,
             qkv1_ref, qkv4_ref, qkv16_ref, hu_ref, hz_ref, mixed_ref, gm_ref, h_scr, wm_scr, p_scr):
        @pl.when(pl.program_id(0) == 0)
        def _():
            row = lax.broadcasted_iota(jnp.int32, (BLK, BLK), 0)
            col = lax.broadcasted_iota(jnp.int32, (BLK, BLK), 1)
            for g in range(N_HEADS):
                wm_scr[g] = jnp.where(col <= row, ws_ref[g], 0.0).astype(MXU)

        xb = x_ref[...].astype(MXU)
        for j in range(N_SHARD):
            h_scr[:, j * W_IN_BLK:(j + 1) * W_IN_BLK] = jnp.dot(xb, w_ref[j], preferred_element_type=F32)
        c, s1, s2 = _tile_heads(c_ref[...]), _tile_heads(s1_ref[...]), _tile_heads(s2_ref[...])
        q = _rope_apply(h_scr[:, 0:D_ATTN], c, s1, s2) * (1.0 / math.sqrt(HEAD_DIM))
        k = _rope_apply(h_scr[:, D_ATTN:2 * D_ATTN], c, s1, s2)
        for part, val in enumerate((q, k, h_scr[:, 2 * D_ATTN:3 * D_ATTN])):
            qkv1_ref[:, part * D_ATTN:(part + 1) * D_ATTN] = val.astype(MXU)
            for cc in range(LANE_CHUNKS):
                p_scr[part * LANE_CHUNKS + cc] = val[:, cc * 128:(cc + 1) * 128]
        _to_planes(qkv4_ref, p_scr, DILATIONS[1], 3 * LANE_CHUNKS, MXU)
        _to_planes(qkv16_ref, p_scr, DILATIONS[2], 3 * LANE_CHUNKS, MXU)
        hu = h_scr[:, 3 * D_ATTN:3 * D_ATTN + D_GMLP]
        hz = h_scr[:, 3 * D_ATTN + D_GMLP:]
        hu_ref[...] = hu
        hz_ref[...] = hz
        zhat, _ = _ln_fwd(_gelu(hz))
        zn = (zhat * g_ref[...] + b_ref[...]).astype(MXU)
        for ch in range(nchunk):
            rows = slice(ch * BLK, (ch + 1) * BLK)
            for g in range(N_HEADS):
                cols = slice(g * HEAD_DIM, (g + 1) * HEAD_DIM)
                mixed_ref[rows, cols] = jnp.dot(wm_scr[g], zn[rows, cols], preferred_element_type=F32) + bf_ref[:, cols]
        gm_ref[...] = (_gelu(hu) * mixed_ref[...]).astype(MXU)

    tok = lambda w: pl.BlockSpec((TM, w), lambda i: (i, 0))
    outs = [_perm_shape(t, d, 3 * D_ATTN, MXU) for d in DILATIONS] + [jax.ShapeDtypeStruct((t, D_GMLP), F32)] * 3 + [
        jax.ShapeDtypeStruct((t, D_GMLP), MXU)]
    return pl.pallas_call(
        body, name="qkvuz", grid=(t // TM,),
        in_specs=[tok(D_MODEL), _full(w_in.shape), tok(128), tok(128), tok(128), _full(ln_z_g.shape), _full(ln_z_b.shape),
                  _full(w_s.shape), _full(b_full.shape), pl.BlockSpec(memory_space=pl.ANY)],
        out_specs=[_perm_tile(d, 3 * D_ATTN) for d in DILATIONS] + [tok(D_ATTN)] * 4, out_shape=outs,
        scratch_shapes=[pltpu.VMEM((TM, D_IN), F32), pltpu.VMEM((N_HEADS, BLK, BLK), MXU),
                        pltpu.VMEM((3 * LANE_CHUNKS, TM, 128), F32)],
        compiler_params=_cp(dimension_semantics=("arbitrary",)),
    )(x, w_in, c_tab, s1_tab, s2_tab, ln_z_g, ln_z_b, w_s, b_full, dep)


def _band_valid(n):
    i = lax.broadcasted_iota(jnp.int32, (BLK, 2 * BLK), 0)
    j = lax.broadcasted_iota(jnp.int32, (BLK, 2 * BLK), 1)
    return (j >= i) & (j <= i + BLK) & ((j >= BLK) | (n > 0))


def _attn_fwd(qkv, d):
    _, l_sub, _ = qkv.shape
    nb = l_sub // BLK

    def body(q_ref, kp_ref, kc_ref, vp_ref, vc_ref, o_ref, l_ref):
        valid = _band_valid(pl.program_id(1))
        kcat = jnp.concatenate([kp_ref[...], kc_ref[...]], axis=0)
        vcat = jnp.concatenate([vp_ref[...], vc_ref[...]], axis=0)
        for h in range(N_HEADS):
            cols = slice(h * HEAD_DIM, (h + 1) * HEAD_DIM)
            s = jnp.where(valid, _dot_nt(q_ref[:, cols], kcat[:, cols]), NEG_INF)
            m = jnp.max(s, axis=-1, keepdims=True)
            e = jnp.exp(s - m)
            den = jnp.sum(e, axis=-1, keepdims=True)
            o_ref[:, cols] = _dot(e, vcat[:, cols]) * (1.0 / den)
            l_ref[:, h * HEAD_LANES:(h + 1) * HEAD_LANES] = jnp.broadcast_to(m + jnp.log(den), (BLK, HEAD_LANES))

    def blk(w, col, prev=False):
        return pl.BlockSpec((None, BLK, w), lambda r, n: (r, jnp.maximum(n - 1, 0) if prev else n, col))

    return pl.pallas_call(
        body, name=f"attn_fwd_d{d}", grid=(d, nb),
        in_specs=[blk(D_ATTN, 0), blk(D_ATTN, 1, True), blk(D_ATTN, 1), blk(D_ATTN, 2, True), blk(D_ATTN, 2)],
        out_specs=[blk(D_ATTN, 0), blk(128, 0)],
        out_shape=[jax.ShapeDtypeStruct((d, l_sub, D_ATTN), F32), jax.ShapeDtypeStruct((d, l_sub, 128), F32)],
        compiler_params=_cp(dimension_semantics=("arbitrary", "arbitrary")),
    )(qkv, qkv, qkv, qkv, qkv)


def _attn_bwd(qkv, do, lse, delta, d, dep):
    _, l_sub, _ = qkv.shape
    nb = l_sub // BLK
    whole = l_sub <= 8 * BLK

    def shares(n, q_ref, kp_ref, kc_ref, vp_ref, vc_ref, do_ref, l_ref, dl_ref, dq_ref):
        valid = _band_valid(n)
        kcat = jnp.concatenate([kp_ref[...], kc_ref[...]], axis=0)
        vcat = jnp.concatenate([vp_ref[...], vc_ref[...]], axis=0)
        for h in range(N_HEADS):
            cols = slice(h * HEAD_DIM, (h + 1) * HEAD_DIM)
            stat = slice(h * HEAD_LANES, h * HEAD_LANES + 1)
            qh, doh = q_ref[:, cols], do_ref[:, cols]
            p = jnp.where(valid, jnp.exp(_dot_nt(qh, kcat[:, cols]) - l_ref[:, stat]), 0.0)
            ds = p * (_dot_nt(doh, vcat[:, cols]) - dl_ref[:, stat])
            dq_ref[:, cols] = _dot(ds, kcat[:, cols])
            yield cols, _dot_tn(ds, qh), _dot_tn(p, doh)

    def body_whole(*refs):
        dk_ref, dv_ref = refs[10:]
        n = pl.program_id(1)
        cur = pl.ds(pl.multiple_of(n * BLK, BLK), BLK)
        prev = pl.ds(pl.multiple_of(jnp.maximum(n - 1, 0) * BLK, BLK), BLK)
        for cols, dk2, dv2 in shares(n, *refs[:8], refs[9]):
            dk_ref[cur, cols] = dk2[BLK:]
            dv_ref[cur, cols] = dv2[BLK:]
            dk_ref[prev, cols] += dk2[0:BLK]
            dv_ref[prev, cols] += dv2[0:BLK]

    def body_carry(*refs):
        dk_ref, dv_ref, ck_scr, cv_scr = refs[10:]
        n = pl.program_id(1)

        @pl.when(n == 0)
        def _():
            ck_scr[...] = jnp.zeros_like(ck_scr)
            cv_scr[...] = jnp.zeros_like(cv_scr)

        @pl.when(n < nb)
        def _():
            for cols, dk2, dv2 in shares(n, *refs[:8], refs[9]):
                dk_ref[:, cols] = ck_scr[:, cols] + dk2[0:BLK]
                dv_ref[:, cols] = cv_scr[:, cols] + dv2[0:BLK]
                ck_scr[:, cols] = dk2[BLK:]
                cv_scr[:, cols] = dv2[BLK:]

        @pl.when(n == nb)
        def _():
            dk_ref[...] = ck_scr[...]
            dv_ref[...] = cv_scr[...]

    def blk(w, col, shift=0):
        return pl.BlockSpec((None, BLK, w), lambda r, n: (r, jnp.clip(n - shift, 0, nb - 1), col))

    if whole:
        dkv_spec = pl.BlockSpec((None, l_sub, D_ATTN), lambda r, n: (r, 0, 0))
        body, steps, scratch = body_whole, nb, []
    else:
        dkv_spec = blk(D_ATTN, 0, 1)
        body, steps, scratch = body_carry, nb + 1, [pltpu.VMEM((BLK, D_ATTN), F32)] * 2
    return pl.pallas_call(
        body, name=f"attn_bwd_d{d}", grid=(d, steps),
        in_specs=[blk(D_ATTN, 0), blk(D_ATTN, 1, 1), blk(D_ATTN, 1), blk(D_ATTN, 2, 1), blk(D_ATTN, 2),
                  blk(D_ATTN, 0), blk(128, 0), blk(128, 0), pl.BlockSpec(memory_space=pl.ANY)],
        out_specs=[blk(D_ATTN, 0), dkv_spec, dkv_spec],
        out_shape=[jax.ShapeDtypeStruct((d, l_sub, D_ATTN), F32)] * 3,
        scratch_shapes=scratch,
        compiler_params=_cp(dimension_semantics=("arbitrary", "arbitrary")),
    )(qkv, qkv, qkv, qkv, qkv, do, lse, delta, dep)


def _mix_ln1(os_, ls_, gm, x, w_o, ln1_g, ln1_b):
    t = x.shape[0]
    expand = _head_expand()

    def body(o1, o4, o16, l1, l4, l16, gm_ref, x_ref, wo_ref, g_ref, b_ref, ex_ref,
             attn_ref, lse1_ref, lse4_ref, lse16_ref, cat_ref, xhat_ref, rstd_ref, x1b_ref, o_scr, l_scr):
        _from_planes(o4, o_scr, DILATIONS[1], LANE_CHUNKS)
        _from_planes(o16, o_scr.at[pl.ds(LANE_CHUNKS, LANE_CHUNKS)], DILATIONS[2], LANE_CHUNKS)
        _from_planes(l4, l_scr, DILATIONS[1], 1)
        _from_planes(l16, l_scr.at[pl.ds(1, 1)], DILATIONS[2], 1)
        la, lb, lc = l1[...], l_scr[0], l_scr[1]
        m = jnp.maximum(jnp.maximum(la, lb), lc)
        ea, eb, ec = jnp.exp(la - m), jnp.exp(lb - m), jnp.exp(lc - m)
        den = ea + eb + ec
        inv = 1.0 / den
        wide = lambda w: _dot_select(w, ex_ref[...])
        attn = (wide(ea * inv) * o1[...] + wide(eb * inv) * _unchunk(o_scr, LANE_CHUNKS)
                + wide(ec * inv) * _unchunk(o_scr, LANE_CHUNKS, LANE_CHUNKS))
        attn_ref[...] = attn
        lse = m + jnp.log(den)
        lse1_ref[...] = lse
        l_scr[2] = lse
        _to_planes(lse4_ref, l_scr.at[pl.ds(2, 1)], DILATIONS[1], 1, F32)
        _to_planes(lse16_ref, l_scr.at[pl.ds(2, 1)], DILATIONS[2], 1, F32)
        cat_ref[:, 0:D_ATTN] = attn.astype(MXU)
        cat_ref[:, D_ATTN:] = gm_ref[...]
        mix = jnp.dot(cat_ref[...], wo_ref[...], preferred_element_type=F32)
        xhat, rstd = _ln_fwd(ALPHA * x_ref[...] + mix)
        xhat_ref[...] = xhat
        rstd_ref[...] = rstd
        x1b_ref[...] = (xhat * g_ref[...] + b_ref[...]).astype(MXU)

    tok = lambda w: pl.BlockSpec((TM, w), lambda i: (i, 0))
    outs = [jax.ShapeDtypeStruct((t, D_ATTN), F32)] + [_perm_shape(t, d, 128, F32) for d in DILATIONS] + [
        jax.ShapeDtypeStruct((t, D_MODEL), MXU), jax.ShapeDtypeStruct((t, D_MODEL), F32), jax.ShapeDtypeStruct((t, 1), F32),
        jax.ShapeDtypeStruct((t, D_MODEL), MXU)]
    return pl.pallas_call(
        body, name="mix_ln1", grid=(t // TM,),
        in_specs=[_perm_tile(d, D_ATTN) for d in DILATIONS] + [_perm_tile(d, 128) for d in DILATIONS]
        + [tok(D_GMLP), tok(D_MODEL), _full(w_o.shape), _full(ln1_g.shape), _full(ln1_b.shape), _full(expand.shape)],
        out_specs=[tok(D_ATTN)] + [_perm_tile(d, 128) for d in DILATIONS] + [tok(D_MODEL), tok(D_MODEL), tok(1), tok(D_MODEL)],
        out_shape=outs,
        scratch_shapes=[pltpu.VMEM((2 * LANE_CHUNKS, TM, 128), F32), pltpu.VMEM((3, TM, 128), F32)],
        compiler_params=_cp(dimension_semantics=("arbitrary",)),
    )(*os_, *ls_, gm, x, w_o, ln1_g, ln1_b, expand)


def _conv_fwd(a_ext, w_ref, b_ref, rows):
    return (b_ref[...] + w_ref[2:3, :] * a_ext[HALO:HALO + rows] + w_ref[1:2, :] * a_ext[HALO - 1:HALO - 1 + rows]
            + w_ref[0:1, :] * a_ext[HALO - 2:HALO - 2 + rows])


def _ffn_in(x1b, w_a, w_b, conv_w, conv_b):
    t = x1b.shape[0]
    hb = TM // HALO

    def body(x_ref, xh_ref, wa_ref, wb_ref, cw_ref, cb_ref, apre_ref, a_ref, b_ref, f_ref):
        i = pl.program_id(1)
        a_pre = _dot_nt(x_ref[...], wa_ref[...])
        a_halo = jnp.where(i > 0, _dot_nt(xh_ref[...], wa_ref[...]), 0.0)
        a = _conv_fwd(jnp.concatenate([a_halo, a_pre], axis=0), cw_ref, cb_ref, TM)
        b = _dot_nt(x_ref[...], wb_ref[...])
        apre_ref[...] = a_pre
        a_ref[...] = a
        b_ref[...] = b
        f_ref[...] = (_gelu(a) * b).astype(MXU)

    blk = lambda r, c: pl.BlockSpec((None, r, c), lambda j, i: (j, 0, 0))
    tokj = pl.BlockSpec((None, TM, FF_BLK), lambda j, i: (j, i, 0))
    outs = [jax.ShapeDtypeStruct((N_SHARD, t, FF_BLK), F32)] * 3 + [jax.ShapeDtypeStruct((N_SHARD, t, FF_BLK), MXU)]
    return pl.pallas_call(
        body, name="ffn_in", grid=(N_SHARD, t // TM),
        in_specs=[pl.BlockSpec((TM, D_MODEL), lambda j, i: (i, 0)),
                  pl.BlockSpec((HALO, D_MODEL), lambda j, i: (jnp.maximum(i * hb - 1, 0), 0)),
                  blk(FF_BLK, D_MODEL), blk(FF_BLK, D_MODEL), blk(3, FF_BLK), blk(1, FF_BLK)],
        out_specs=[tokj, tokj, tokj, tokj], out_shape=outs,
        compiler_params=_cp(dimension_semantics=("arbitrary", "arbitrary")),
    )(x1b, x1b, w_a, w_b, conv_w, conv_b)


def _ffn_out_ln2(f, w_down, xhat1, ln1_g, ln1_b, ln2_g, ln2_b):
    t = xhat1.shape[0]

    def body(f_ref, wd_ref, xh_ref, g1_ref, b1_ref, g2_ref, b2_ref, xhat_ref, rstd_ref, x2b_ref):
        ff = jnp.dot(f_ref[0], wd_ref[0], preferred_element_type=F32)
        for j in range(1, N_SHARD):
            ff = ff + jnp.dot(f_ref[j], wd_ref[j], preferred_element_type=F32)
        x1 = xh_ref[...] * g1_ref[...] + b1_ref[...]
        xhat, rstd = _ln_fwd(ALPHA * x1 + ff)
        xhat_ref[...] = xhat
        rstd_ref[...] = rstd
        x2b_ref[...] = (xhat * g2_ref[...] + b2_ref[...]).astype(MXU)

    tok = lambda w: pl.BlockSpec((TM, w), lambda i: (i, 0))
    vec = _full((1, D_MODEL))
    outs = [jax.ShapeDtypeStruct((t, D_MODEL), F32), jax.ShapeDtypeStruct((t, 1), F32), jax.ShapeDtypeStruct((t, D_MODEL), MXU)]
    return pl.pallas_call(
        body, name="ffn_out_ln2", grid=(t // TM,),
        in_specs=[pl.BlockSpec((N_SHARD, TM, FF_BLK), lambda i: (0, i, 0)), _full(w_down.shape), tok(D_MODEL), vec, vec, vec, vec],
        out_specs=[tok(D_MODEL), tok(1), tok(D_MODEL)], out_shape=outs,
        compiler_params=_cp(dimension_semantics=("arbitrary",)),
    )(f, w_down, xhat1, ln1_g, ln1_b, ln2_g, ln2_b)


STAT_ROWS = 8


def _ple_loss_bwd(xhat2, rstd2, p, target, ln2_g, ln2_b, w_g, b_g, w_p, ln3_g, ln3_b):
    t = xhat2.shape[0]

    def body(xh2_ref, rs2_ref, p_ref, t_ref, g2_ref, b2_ref, wg_ref, bg_ref, wp_ref, g3_ref, b3_ref,
             dr2_ref, dgp_ref, dpp_ref, stat_ref, pp_scr):
        @pl.when(pl.program_id(0) == 0)
        def _():
            stat_ref[...] = jnp.zeros_like(stat_ref)

        xhat2 = xh2_ref[...]
        x2 = xhat2 * g2_ref[...] + b2_ref[...]
        gate = jax.nn.sigmoid(jnp.dot(x2.astype(MXU), wg_ref[...], preferred_element_type=F32) + bg_ref[...])
        pb = p_ref[...].astype(MXU)
        for j in range(N_SHARD):
            pp_scr[:, j * ROW_BLK:(j + 1) * ROW_BLK] = jnp.dot(pb, wp_ref[j], preferred_element_type=F32)
        pp = pp_scr[...]
        xhat3, rstd3 = _ln_fwd(ALPHA * x2 + gate * pp)
        err = xhat3 * g3_ref[...] + b3_ref[...] - t_ref[...]
        dy = err * (1.0 / D_MODEL)
        dr3 = _ln_bwd(dy, xhat3, rstd3, g3_ref[...])
        dgp = dr3 * pp * gate * (1.0 - gate)
        dgp_ref[...] = dgp.astype(MXU)
        dpp_ref[...] = (dr3 * gate).astype(MXU)
        dx2 = ALPHA * dr3 + _dot_nt(dgp, wg_ref[...])
        dr2_ref[...] = _ln_bwd(dx2, xhat2, rs2_ref[...], g2_ref[...])
        stat_ref[0:1, :] += _colsum(dy * xhat3)
        stat_ref[1:2, :] += _colsum(dy)
        stat_ref[2:3, :] += _colsum(dgp)
        stat_ref[3:4, :] += _colsum(dx2 * xhat2)
        stat_ref[4:5, :] += _colsum(dx2)
        stat_ref[5:6, :] += _colsum(err * err)

    tok = lambda w: pl.BlockSpec((TM, w), lambda i: (i, 0))
    vec = _full((1, D_MODEL))
    outs = [jax.ShapeDtypeStruct((t, D_MODEL), F32), jax.ShapeDtypeStruct((t, D_MODEL), MXU), jax.ShapeDtypeStruct((t, D_MODEL), MXU),
            jax.ShapeDtypeStruct((STAT_ROWS, D_MODEL), F32)]
    return pl.pallas_call(
        body, name="ple_loss_bwd", grid=(t // TM,),
        in_specs=[tok(D_MODEL), tok(1), tok(D_PLE), tok(D_MODEL), vec, vec, _full(w_g.shape), vec, _full(w_p.shape), vec, vec],
        out_specs=[tok(D_MODEL), tok(D_MODEL), tok(D_MODEL), _full((STAT_ROWS, D_MODEL))], out_shape=outs,
        scratch_shapes=[pltpu.VMEM((TM, D_MODEL), F32)],
        compiler_params=_cp(dimension_semantics=("arbitrary",)),
    )(xhat2, rstd2, p, target, ln2_g, ln2_b, w_g, b_g, w_p, ln3_g, ln3_b)


def _ffn_bwd(dr2, a_pre, a, b, w_down, w_a, w_b, conv_w, xhat1, rstd1, ln1_g):
    t = dr2.shape[0]
    nt = t // TM
    hb = TM // HALO
    last_h = t // HALO - 1

    def body(dr_ref, drn_ref, ap_ref, a_ref, an_ref, b_ref, bn_ref, wd_ref, wa_ref, wb_ref, cw_ref,
             xh_ref, rs_ref, g1_ref, dap_ref, dbb_ref, dr1_ref, cstat_ref, lstat_ref, acc_scr):
        i, j = pl.program_id(0), pl.program_id(1)

        @pl.when((i == 0) & (j == 0))
        def _():
            cstat_ref[...] = jnp.zeros_like(cstat_ref)
            lstat_ref[...] = jnp.zeros_like(lstat_ref)

        half = TM // ROW_GROUPS
        parts = []
        for r0 in range(0, TM, half):
            rows = pl.ds(r0, half)
            last = r0 + half == TM

            def ext(ref, nxt):
                return jnp.concatenate([ref[rows], nxt[...]], axis=0) if last else ref[r0:r0 + half + HALO]

            df = _dot_nt(ext(dr_ref, drn_ref), wd_ref[...])
            a_ext, b_ext = ext(a_ref, an_ref), ext(b_ref, bn_ref)
            cdf = 0.5 * (1.0 + lax.erf(a_ext * (1.0 / math.sqrt(2.0))))
            pdf = jnp.exp(-0.5 * a_ext * a_ext) * (1.0 / math.sqrt(2.0 * math.pi))
            da = df * b_ext * (cdf + a_ext * pdf)
            if last:
                da = jnp.concatenate([da[0:half], jnp.where(i < nt - 1, da[half:], 0.0)], axis=0)
            ahead = [da[s:s + half] for s in range(3)]
            da_pre = cw_ref[2:3, :] * ahead[0] + cw_ref[1:2, :] * ahead[1] + cw_ref[0:1, :] * ahead[2]
            dbb = df[0:half] * (a_ext[0:half] * cdf[0:half])
            dap_ref[rows, :] = da_pre.astype(MXU)
            dbb_ref[rows, :] = dbb.astype(MXU)
            for kk in range(3):
                cstat_ref[j, kk:kk + 1, :] += _colsum(ahead[2 - kk] * ap_ref[rows, :])
            cstat_ref[j, 3:4, :] += _colsum(ahead[0])
            parts.append(_dot(da_pre, wa_ref[...]) + _dot(dbb, wb_ref[...]))
        part = jnp.concatenate(parts, axis=0)

        @pl.when(j == 0)
        def _():
            acc_scr[...] = ALPHA * dr_ref[...] + part

        @pl.when(j > 0)
        def _():
            acc_scr[...] += part

        @pl.when(j == N_SHARD - 1)
        def _():
            dx1 = acc_scr[...]
            xhat1 = xh_ref[...]
            lstat_ref[0:1, :] += _colsum(dx1 * xhat1)
            lstat_ref[1:2, :] += _colsum(dx1)
            dr1_ref[...] = _ln_bwd(dx1, xhat1, rs_ref[...], g1_ref[...])

    tok = lambda w: pl.BlockSpec((TM, w), lambda i, j: (i, 0))
    tokj = pl.BlockSpec((None, TM, FF_BLK), lambda i, j: (j, i, 0))
    nextj = pl.BlockSpec((None, HALO, FF_BLK), lambda i, j: (j, jnp.minimum((i + 1) * hb, last_h), 0))
    blk = lambda r, c: pl.BlockSpec((None, r, c), lambda i, j: (j, 0, 0))
    outs = [jax.ShapeDtypeStruct((N_SHARD, t, FF_BLK), MXU)] * 2 + [
        jax.ShapeDtypeStruct((t, D_MODEL), F32), jax.ShapeDtypeStruct((N_SHARD, STAT_ROWS, FF_BLK), F32),
        jax.ShapeDtypeStruct((STAT_ROWS, D_MODEL), F32)]
    return pl.pallas_call(
        body, name="ffn_bwd", grid=(nt, N_SHARD),
        in_specs=[tok(D_MODEL), pl.BlockSpec((HALO, D_MODEL), lambda i, j: (jnp.minimum((i + 1) * hb, last_h), 0)),
                  tokj, tokj, nextj, tokj, nextj, blk(FF_BLK, D_MODEL), blk(FF_BLK, D_MODEL), blk(FF_BLK, D_MODEL),
                  blk(3, FF_BLK), tok(D_MODEL), tok(1), _full((1, D_MODEL))],
        out_specs=[tokj, tokj, tok(D_MODEL), _full((N_SHARD, STAT_ROWS, FF_BLK)), _full((STAT_ROWS, D_MODEL))], out_shape=outs,
        scratch_shapes=[pltpu.VMEM((TM, D_MODEL), F32)],
        compiler_params=_cp(dimension_semantics=("arbitrary", "arbitrary")),
    )(dr2, dr2, a_pre, a, a, b, b, w_down, w_a, w_b, conv_w, xhat1, rstd1, ln1_g)


def _mix_bwd(dr1, w_o, hu, hz, mixed, attn, ln_z_g, ln_z_b, w_s, dep):
    t = dr1.shape[0]
    nchunk = TM // BLK

    def body(dr_ref, wo_ref, hu_ref, hz_ref, mx_ref, attn_ref, g_ref, b_ref, ws_ref, grp_ref, red_ref, dep_ref,
             do1_ref, do4_ref, do16_ref, dl1_ref, dl4_ref, dl16_ref, duz_ref, dws_ref, dbs_ref, zstat_ref,
             wm_scr, dzn_scr, dbsum_scr, do_scr, dl_scr):
        @pl.when(pl.program_id(0) == 0)
        def _():
            row = lax.broadcasted_iota(jnp.int32, (BLK, BLK), 0)
            col = lax.broadcasted_iota(jnp.int32, (BLK, BLK), 1)
            for g in range(N_HEADS):
                wm_scr[g] = jnp.where(col <= row, ws_ref[g], 0.0).astype(MXU)
            dws_ref[...] = jnp.zeros_like(dws_ref)
            dbsum_scr[...] = jnp.zeros_like(dbsum_scr)
            zstat_ref[...] = jnp.zeros_like(zstat_ref)

        dcat = _dot_nt(dr_ref[...], wo_ref[...])
        dattn = dcat[:, 0:D_ATTN]
        do1_ref[...] = dattn.astype(MXU)
        for cc, val in enumerate(_chunks(dattn)):
            do_scr[cc] = val
        _to_planes(do4_ref, do_scr, DILATIONS[1], LANE_CHUNKS, MXU)
        _to_planes(do16_ref, do_scr, DILATIONS[2], LANE_CHUNKS, MXU)
        delta = _dot_select(dattn * attn_ref[...], red_ref[...])
        dl1_ref[...] = delta
        dl_scr[0] = delta
        _to_planes(dl4_ref, dl_scr, DILATIONS[1], 1, F32)
        _to_planes(dl16_ref, dl_scr, DILATIONS[2], 1, F32)
        dgm = dcat[:, D_ATTN:]
        hu, hz = hu_ref[...], hz_ref[...]
        u = _gelu(hu)
        duz_ref[:, 0:D_GMLP] = (dgm * mx_ref[...] * _gelu_grad(hu)).astype(MXU)
        dmixed = dgm * u
        dmb = dmixed.astype(MXU)
        zhat, rstd = _ln_fwd(_gelu(hz))
        znb = (zhat * g_ref[...] + b_ref[...]).astype(MXU)
        dbs_acc = jnp.zeros((BLK, D_GMLP), F32)
        for ch in range(nchunk):
            rows = slice(ch * BLK, (ch + 1) * BLK)
            dbs_acc = dbs_acc + dmixed[rows]
            for g in range(N_HEADS):
                cols = slice(g * HEAD_DIM, (g + 1) * HEAD_DIM)
                dzn_scr[rows, cols] = _dot_tn(wm_scr[g], dmb[rows, cols])
                dws_ref[g] += _dot_nt(dmb[rows, cols], znb[rows, cols])
        dbsum_scr[...] += dbs_acc
        dzn = dzn_scr[...]
        zstat_ref[0:1, :] += _colsum(dzn * zhat)
        zstat_ref[1:2, :] += _colsum(dzn)
        duz_ref[:, D_GMLP:] = (_ln_bwd(dzn, zhat, rstd, g_ref[...]) * _gelu_grad(hz)).astype(MXU)

        @pl.when(pl.program_id(0) == nt - 1)
        def _():
            row = lax.broadcasted_iota(jnp.int32, (BLK, BLK), 0)
            col = lax.broadcasted_iota(jnp.int32, (BLK, BLK), 1)
            for g in range(N_HEADS):
                dws_ref[g] = jnp.where(col <= row, dws_ref[g], 0.0)
            dbs_ref[...] = lax.dot_general(grp_ref[...], dbsum_scr[...], (((1,), (1,)), ((), ())),
                                           precision=lax.Precision.HIGHEST, preferred_element_type=F32)

    nt = t // TM
    tok = lambda w: pl.BlockSpec((TM, w), lambda i: (i, 0))
    grp = jnp.asarray((np.arange(D_GMLP)[None, :] // HEAD_DIM == np.arange(N_HEADS)[:, None]).astype(np.float32))
    red = _head_reduce()
    outs = [_perm_shape(t, d, D_ATTN, MXU) for d in DILATIONS] + [_perm_shape(t, d, 128, F32) for d in DILATIONS] + [
        jax.ShapeDtypeStruct((t, 2 * D_GMLP), MXU),
        jax.ShapeDtypeStruct((N_HEADS, BLK, BLK), F32), jax.ShapeDtypeStruct((N_HEADS, BLK), F32),
        jax.ShapeDtypeStruct((STAT_ROWS, D_GMLP), F32)]
    return pl.pallas_call(
        body, name="mix_bwd", grid=(t // TM,),
        in_specs=[tok(D_MODEL), _full(w_o.shape), tok(D_GMLP), tok(D_GMLP), tok(D_GMLP), tok(D_ATTN), _full(ln_z_g.shape),
                  _full(ln_z_b.shape), _full(w_s.shape), _full(grp.shape), _full(red.shape), pl.BlockSpec(memory_space=pl.ANY)],
        out_specs=[_perm_tile(d, D_ATTN) for d in DILATIONS] + [_perm_tile(d, 128) for d in DILATIONS]
        + [tok(2 * D_GMLP), _full((N_HEADS, BLK, BLK)), _full((N_HEADS, BLK)), _full((STAT_ROWS, D_GMLP))],
        out_shape=outs,
        scratch_shapes=[pltpu.VMEM((N_HEADS, BLK, BLK), MXU), pltpu.VMEM((TM, D_GMLP), F32), pltpu.VMEM((BLK, D_GMLP), F32),
                        pltpu.VMEM((LANE_CHUNKS, TM, 128), F32), pltpu.VMEM((1, TM, 128), F32)],
        compiler_params=_cp(dimension_semantics=("arbitrary",)),
    )(dr1, w_o, hu, hz, mixed, attn, ln_z_g, ln_z_b, w_s, grp, red, dep)


def _dx_in(dqs, dks, dvs, duz, dr1, w_in, c_tab, s1_tab, s2_tab):
    t = dr1.shape[0]

    def body(dq1, dq4, dq16, dk1, dk4, dk16, dv1, dv4, dv16, duz_ref, dr_ref, w_ref, c_ref, s1_ref, s2_ref,
             dh_ref, dx_ref, acc_scr):
        sums = []
        for part, (g1, g4, g16) in enumerate(((dq1, dq4, dq16), (dk1, dk4, dk16), (dv1, dv4, dv16))):
            acc = acc_scr.at[pl.ds(part * LANE_CHUNKS, LANE_CHUNKS)]
            for cc in range(LANE_CHUNKS):
                acc[cc] = g1[:, cc * 128:(cc + 1) * 128]
            _from_planes(g4, acc, DILATIONS[1], LANE_CHUNKS, accumulate=True)
            _from_planes(g16, acc, DILATIONS[2], LANE_CHUNKS, accumulate=True)
            sums.append(_unchunk(acc_scr, LANE_CHUNKS, part * LANE_CHUNKS))
        c, s1, s2 = _tile_heads(c_ref[...]), _tile_heads(s1_ref[...]), _tile_heads(s2_ref[...])
        dh_ref[:, 0:D_ATTN] = _rope_apply_t(sums[0] * (1.0 / math.sqrt(HEAD_DIM)), c, s1, s2).astype(MXU)
        dh_ref[:, D_ATTN:2 * D_ATTN] = _rope_apply_t(sums[1], c, s1, s2).astype(MXU)
        dh_ref[:, 2 * D_ATTN:3 * D_ATTN] = sums[2].astype(MXU)
        dh_ref[:, 3 * D_ATTN:] = duz_ref[...]
        dx = ALPHA * dr_ref[...]
        for j in range(N_SHARD):
            dx = dx + _dot_nt(dh_ref[:, j * W_IN_BLK:(j + 1) * W_IN_BLK], w_ref[j])
        dx_ref[...] = dx

    tok = lambda w: pl.BlockSpec((TM, w), lambda i: (i, 0))
    outs = [jax.ShapeDtypeStruct((t, D_IN), MXU), jax.ShapeDtypeStruct((t, D_MODEL), F32)]
    return pl.pallas_call(
        body, name="dx_in", grid=(t // TM,),
        in_specs=[_perm_tile(d, D_ATTN) for d in DILATIONS] * 3
        + [tok(2 * D_GMLP), tok(D_MODEL), _full(w_in.shape), tok(128), tok(128), tok(128)],
        out_specs=[tok(D_IN), tok(D_MODEL)], out_shape=outs,
        scratch_shapes=[pltpu.VMEM((3 * LANE_CHUNKS, TM, 128), F32)],
        compiler_params=_cp(dimension_semantics=("arbitrary",)),
    )(*dqs, *dks, *dvs, duz, dr1, w_in, c_tab, s1_tab, s2_tab)


def _wgrad(name, x, dy, x_spec, dy_spec, out_spec, out_shape, grid):
    def body(x_ref, dy_ref, o_ref):
        o_ref[...] = _dot_tn(x_ref[...], dy_ref[...])

    return pl.pallas_call(
        body, name=name, grid=grid, in_specs=[x_spec, dy_spec], out_specs=out_spec,
        out_shape=jax.ShapeDtypeStruct(out_shape, F32),
        compiler_params=_cp(dimension_semantics=("arbitrary",) * len(grid)),
    )(x, dy)


def _wgrad_pair(name, xa, xb, dy, x_spec, dy_spec, out_spec, out_shape, grid):
    def body(xa_ref, xb_ref, dy_ref, oa_ref, ob_ref):
        dy = dy_ref[...]
        oa_ref[...] = _dot_tn(xa_ref[...], dy)
        ob_ref[...] = _dot_tn(xb_ref[...], dy)

    return pl.pallas_call(
        body, name=name, grid=grid, in_specs=[x_spec, x_spec, dy_spec], out_specs=[out_spec, out_spec],
        out_shape=[jax.ShapeDtypeStruct(out_shape, F32)] * 2,
        compiler_params=_cp(dimension_semantics=("arbitrary",) * len(grid)),
    )(xa, xb, dy)


def _local_step(x, p, rope, target, w_in, start_dep, late_w_o, late_weights, early_grads, early_grads_sent,
                ln_z_g, ln_z_b, w_s, b_s, ln1_g, ln1_b, conv_b, ln2_g, ln2_b, b_g, ln3_g, ln3_b):
    t = x.shape[0]
    half = TM
    c_tab, s1_tab, s2_tab = rope
    b_full = jnp.repeat(jnp.transpose(b_s[0]), HEAD_DIM, axis=1)
    conv_b4 = conv_b.reshape(N_SHARD, 1, FF_BLK)
    *qkvs, hu, hz, mixed, gm = _qkvuz(x, w_in, c_tab, s1_tab, s2_tab, ln_z_g, ln_z_b, w_s[0], b_full, start_dep)
    branches = [_attn_fwd(qkv, d) for qkv, d in zip(qkvs, DILATIONS)]
    w_o = late_w_o(branches[-1][1])
    attn, *lses, cat, xhat1, rstd1, x1b = _mix_ln1(
        [o for o, _ in branches], [l for _, l in branches], gm, x, w_o, ln1_g, ln1_b)
    w_a, w_b, conv_w, w_down, w_g, w_p = late_weights(x1b)
    a_pre, a_act, b_act, f = _ffn_in(x1b, w_a, w_b, conv_w, conv_b4)
    xhat2, rstd2, x2b = _ffn_out_ln2(f, w_down, xhat1, ln1_g, ln1_b, ln2_g, ln2_b)
    dr2, dgp, dpp, stat3 = _ple_loss_bwd(xhat2, rstd2, p, target, ln2_g, ln2_b, w_g, b_g, w_p, ln3_g, ln3_b)
    da_pre, dbb, dr1, cstat, stat1 = _ffn_bwd(dr2, a_pre, a_act, b_act, w_down, w_a, w_b, conv_w, xhat1, rstd1, ln1_g)

    full_t = lambda w, im: pl.BlockSpec((t, w), im)
    ffj = pl.BlockSpec((None, t, FF_BLK), lambda j, kk: (j, 0, 0))
    early = dict(
        w_ple_gate=_wgrad("dw_g", x2b, dgp, full_t(half, lambda kk, n: (0, kk)), full_t(half, lambda kk, n: (0, n)),
                          pl.BlockSpec((half, half), lambda kk, n: (kk, n)), (D_MODEL, D_MODEL), (2, 2)),
        w_ple_in=_wgrad("dw_p", p, dpp, full_t(D_PLE, lambda j: (0, 0)), full_t(ROW_BLK, lambda j: (0, j)),
                        pl.BlockSpec((None, D_PLE, ROW_BLK), lambda j: (j, 0, 0)), (N_SHARD, D_PLE, ROW_BLK), (N_SHARD,)),
        w_ff_down=_wgrad("dw_down", f, dr2, ffj, full_t(half, lambda j, n: (0, n)),
                         pl.BlockSpec((None, FF_BLK, half), lambda j, n: (j, 0, n)), (N_SHARD, FF_BLK, D_MODEL), (N_SHARD, 2)),
        **dict(zip(("w_ff_a", "w_ff_b"), _wgrad_pair(
            "dw_ab", da_pre, dbb, x1b, ffj, full_t(half, lambda j, n: (0, n)),
            pl.BlockSpec((None, FF_BLK, half), lambda j, n: (j, 0, n)), (N_SHARD, FF_BLK, D_MODEL), (N_SHARD, 2)))),
        w_o=_wgrad("dw_o", cat, dr1, full_t(half, lambda kk, n: (0, kk)), full_t(half, lambda kk, n: (0, n)),
                   pl.BlockSpec((half, half), lambda kk, n: (kk, n)), (D_MODEL, D_MODEL), (2, 2)))
    dep = early_grads(early)

    do1, do4, do16, dl1, dl4, dl16, duz, dws, dbs, zstat = _mix_bwd(
        dr1, w_o, hu, hz, mixed, attn, ln_z_g, ln_z_b, w_s[0], dep)
    dep = early_grads_sent(duz, (stat3, stat1, zstat, cstat, dws, dbs))
    dqkv = [_attn_bwd(qkv, do, lse, dl, d, dep)
            for qkv, do, lse, dl, d in zip(qkvs, (do1, do4, do16), lses, (dl1, dl4, dl16), DILATIONS)]
    dh, grad_x = _dx_in([g[0] for g in dqkv], [g[1] for g in dqkv], [g[2] for g in dqkv], duz, dr1, w_in,
                        c_tab, s1_tab, s2_tab)
    g_w_in = _wgrad("dw_in", x, dh, full_t(half, lambda j, kk: (0, kk)), full_t(W_IN_BLK, lambda j, kk: (0, j)),
                    pl.BlockSpec((None, half, W_IN_BLK), lambda j, kk: (j, kk, 0)), (N_SHARD, D_MODEL, W_IN_BLK), (N_SHARD, 2))
    return grad_x, g_w_in


def _tile_rows(rows, mult, steps):
    if rows % mult:
        return rows
    return next(rows // k for k in range(steps, rows + 1) if rows % k == 0 and (rows // k) % mult == 0)


def _grid_spec(grid, in_specs, out_specs):
    return pltpu.PrefetchScalarGridSpec(num_scalar_prefetch=1, grid=grid, in_specs=in_specs, out_specs=out_specs)


def _on_own_steps(i, count, steps, work):
    if count == steps:
        work()
    else:
        pl.when(i < count)(work)


def _place_shards(name, ws, dtypes, place, dep):
    n = len(ws)
    tiles = [_tile_rows(w.shape[0], 16, 8) for w in ws]
    counts = [w.shape[0] // t for w, t in zip(ws, tiles)]
    steps = max(counts)

    def body(s_ref, *refs):
        i = pl.program_id(0)
        for a in range(n):
            def work(a=a):
                refs[n + 1 + a][...] = refs[a][...].astype(dtypes[a])
            _on_own_steps(i, counts[a], steps, work)

    def tile(a, lead):
        last = counts[a] - 1
        if lead:
            return pl.BlockSpec((None, tiles[a], ws[a].shape[1]), lambda i, s: (s[0], jnp.minimum(i, last), 0))
        return pl.BlockSpec((tiles[a], ws[a].shape[1]), lambda i, s: (jnp.minimum(i, last), 0))

    return pl.pallas_call(
        body, name=name,
        grid_spec=_grid_spec((steps,), [tile(a, False) for a in range(n)] + [pl.BlockSpec(memory_space=pl.ANY)],
                             [tile(a, True) for a in range(n)]),
        out_shape=[jax.ShapeDtypeStruct((N_SHARD, *w.shape), dt) for w, dt in zip(ws, dtypes)],
        compiler_params=_cp())(place, *ws, dep)


def _pair_sums(name, mines, gots, place):
    n = len(mines)
    tiles = [_tile_rows(g.shape[1], 16, 2) for g in gots]
    per_blk = [g.shape[1] // t for g, t in zip(gots, tiles)]
    counts = [N_SHARD * nh for nh in per_blk]
    steps = max(counts)

    def body(s_ref, *refs):
        i = pl.program_id(0)
        for a in range(n):
            def work(a=a):
                refs[2 * n + a][...] = (refs[a][...] + refs[n + a][...]).astype(BF16)
            _on_own_steps(i, counts[a], steps, work)

    def tile(a, mine):
        nh, last = per_blk[a], counts[a] - 1

        def index(i, s):
            g = jnp.minimum(i, last)
            return (g // nh, (s[1] * nh if mine else 0) + g % nh, 0)

        return pl.BlockSpec((None, tiles[a], gots[a].shape[2]), index)

    return pl.pallas_call(
        body, name=name,
        grid_spec=_grid_spec((steps,), [tile(a, True) for a in range(n)] + [tile(a, False) for a in range(n)],
                             [tile(a, False) for a in range(n)]),
        out_shape=[jax.ShapeDtypeStruct(g.shape, BF16) for g in gots], compiler_params=_cp())(place, *mines, *gots)


def _chip_sums(name, owns, landeds, place, dep):
    n = len(owns)
    tiles = [_tile_rows(o.shape[1], 16, 8) for o in owns]
    counts = [o.shape[1] // t for o, t in zip(owns, tiles)]
    steps = max(counts)

    def body(s_ref, *refs):
        i = pl.program_id(0)
        for a in range(n):
            def work(a=a):
                own, l1, l2, l3 = (refs[4 * a + k][...].astype(F32) for k in range(4))
                refs[4 * n + 1 + a][...] = ((own + l1) + l2) + l3
            _on_own_steps(i, counts[a], steps, work)

    def slot(a, d):
        last = counts[a] - 1
        return pl.BlockSpec((None, tiles[a], owns[a].shape[2]), lambda i, s: ((s[0] + d) % N_SHARD, jnp.minimum(i, last), 0))

    def out(a):
        nh, last = counts[a], counts[a] - 1
        return pl.BlockSpec((tiles[a], owns[a].shape[2]), lambda i, s: (s[1] * nh + jnp.minimum(i, last), 0))

    operands = [x for o, l in zip(owns, landeds) for x in (o, l, l, l)]
    return pl.pallas_call(
        body, name=name,
        grid_spec=_grid_spec((steps,), [slot(a, d) for a in range(n) for d in range(4)] + [pl.BlockSpec(memory_space=pl.ANY)],
                             [out(a) for a in range(n)]),
        out_shape=[jax.ShapeDtypeStruct((2 * o.shape[1], o.shape[2]), F32) for o in owns],
        compiler_params=_cp())(place, *operands, dep)


def _adamw_math(w, g, m, v):
    m = ADAM_B1 * m + (1.0 - ADAM_B1) * g
    v = ADAM_B2 * v + (1.0 - ADAM_B2) * (g * g)
    m_hat = m / (1.0 - ADAM_B1 ** ADAM_STEP)
    v_hat = v / (1.0 - ADAM_B2 ** ADAM_STEP)
    delta = -ADAM_LR * (m_hat / (jnp.sqrt(v_hat) + ADAM_EPS) + ADAM_WD * w)
    return delta, m, v


def _adamw_shards(name, ws, gs, ms, vs):
    n = len(ws)
    tiles = [_tile_rows(w.shape[1], 8, 8) for w in ws]
    counts = [w.shape[1] // t for w, t in zip(ws, tiles)]
    steps = max(counts)

    def body(*refs):
        i = pl.program_id(0)
        for a in range(n):
            def work(a=a):
                w_ref, g_ref, m_ref, v_ref = refs[4 * a:4 * a + 4]
                d_ref, nm_ref, nv_ref = refs[4 * n + 3 * a:4 * n + 3 * a + 3]
                d_ref[...], nm_ref[...], nv_ref[...] = _adamw_math(w_ref[...], g_ref[...], m_ref[...], v_ref[...])
            _on_own_steps(i, counts[a], steps, work)

    def tile(a, lead):
        last, c = counts[a] - 1, ws[a].shape[2]
        if lead:
            return pl.BlockSpec((None, tiles[a], c), lambda i: (0, jnp.minimum(i, last), 0))
        return pl.BlockSpec((tiles[a], c), lambda i: (jnp.minimum(i, last), 0))

    res = pl.pallas_call(
        body, name=name, grid=(steps,),
        in_specs=[tile(a, lead) for a in range(n) for lead in (True, False, True, True)],
        out_specs=[tile(a, True) for a in range(n) for _ in range(3)],
        out_shape=[jax.ShapeDtypeStruct(w.shape, F32) for w in ws for _ in range(3)],
        compiler_params=_cp())(*[x for quad in zip(ws, gs, ms, vs) for x in quad])
    return [tuple(res[3 * a:3 * a + 3]) for a in range(n)]


MESH = pl.DeviceIdType.MESH
ANY = pl.BlockSpec(memory_space=pl.ANY)


def _place():
    x, y, c = lax.axis_index("x"), lax.axis_index("y"), lax.axis_index("c")
    chips = [(1 - x, y), (x, 1 - y), (1 - x, 1 - y)]
    return x, y, c, 2 * x + y, chips


def _remote(src, dst, send_sem, recv_sem, dev):
    return pltpu.make_async_remote_copy(src_ref=src, dst_ref=dst, send_sem=send_sem, recv_sem=recv_sem,
                                        device_id=dev, device_id_type=MESH)


def _half(ref, hc, rows):
    return ref.at[pl.ds(hc * (rows // 2), rows // 2)]


def _sibling_join(blocks, tag):
    n = len(blocks)

    def body(*refs):
        outs = refs[n:2 * n]
        send, recv = refs[2 * n:]
        x, y, c, _, _ = _place()
        cps = []
        for a in range(n):
            h = blocks[a].shape[0] // 2
            mine = outs[a].at[pl.ds(c * h, h)]
            cp = _remote(mine, mine, send.at[a], recv.at[a], (x, y, 1 - c))
            cp.start()
            cps.append(cp)
        for a, cp in enumerate(cps):
            h = blocks[a].shape[0] // 2
            theirs = outs[a].at[pl.ds((1 - c) * h, h)]
            _remote(theirs, theirs, send.at[a], recv.at[a], (x, y, 1 - c)).wait_recv()
            cp.wait_send()

    sem = pltpu.SemaphoreType.DMA
    return pl.pallas_call(body, name=f"rs_sibling_join_{tag}", in_specs=[ANY] * n, out_specs=[ANY] * n,
                          out_shape=[jax.ShapeDtypeStruct(b_.shape, b_.dtype) for b_ in blocks],
                          input_output_aliases={a: a for a in range(n)},
                          scratch_shapes=[sem((n,)), sem((n,))])(*blocks)


HBM = pl.BlockSpec(memory_space=pltpu.HBM)
SEM = pl.BlockSpec(memory_space=pltpu.SEMAPHORE)
TOKEN = jax.ShapeDtypeStruct((8, 128), F32)


def _in_flight_params():
    return pltpu.CompilerParams(has_side_effects=pltpu.SideEffectType.DATAFLOW_SIDE_EFFECTING)


def _in_hbm(a):
    return pltpu.with_memory_space_constraint(a, pltpu.HBM)


def _gather_piece(ref, rows, split, slot, hc):
    return _half(ref.at[slot], hc, rows) if split else ref.at[slot]


def _gather_start(stacks, split, after, tag):
    n = len(stacks)

    def body(*refs):
        ins = refs[:n]
        send, recv = refs[n + 1], refs[n + 2]
        token = refs[2 * n + 3]
        _, _, c, j, chips = _place()
        for a in range(n):
            mine = _gather_piece(ins[a], stacks[a].shape[1], split[a], j, c)
            for t in range(3):
                _remote(mine, mine, send.at[3 * a + t], recv.at[3 * a + t], (*chips[t], c)).start()
        token[...] = jnp.zeros_like(token)

    sems = pltpu.SemaphoreType.DMA((3 * n,))
    res = pl.pallas_call(
        body, name=f"gather_start_{tag}", in_specs=[HBM] * n + [ANY],
        out_specs=[SEM, SEM] + [HBM] * n + [pl.BlockSpec(memory_space=pltpu.VMEM)],
        out_shape=[sems, sems] + [pltpu.HBM(s.shape, s.dtype) for s in stacks] + [TOKEN],
        input_output_aliases={a: a + 2 for a in range(n)}, compiler_params=_in_flight_params(),
    )(*[_in_hbm(s) for s in stacks], after)
    return res[0], res[1], res[2:2 + n], res[2 + n]


def _gather_wait(send, recv, stacks, split, after, tag):
    n = len(stacks)

    def body(*refs):
        ins = refs[:n]
        send_ref, recv_ref = refs[n], refs[n + 1]
        _, _, c, j, chips = _place()
        for a in range(n):
            rows = stacks[a].shape[1]
            mine = _gather_piece(ins[a], rows, split[a], j, c)
            for t, (px, py) in enumerate(chips):
                theirs = _gather_piece(ins[a], rows, split[a], 2 * px + py, c)
                _remote(mine, mine, send_ref.at[3 * a + t], recv_ref.at[3 * a + t], (px, py, c)).wait_send()
                _remote(theirs, theirs, send_ref.at[3 * a + t], recv_ref.at[3 * a + t], (px, py, c)).wait_recv()

    return pl.pallas_call(
        body, name=f"gather_wait_{tag}", in_specs=[HBM] * n + [SEM, SEM, ANY], out_specs=[HBM] * n,
        out_shape=[pltpu.HBM(s.shape, s.dtype) for s in stacks],
        input_output_aliases={a: a for a in range(n)}, compiler_params=_in_flight_params(),
    )(*stacks, send, recv, after)


def _gather_forward(stacks, split, tag):
    idx = [a for a in range(len(stacks)) if split[a]]
    n = len(idx)

    def body(*refs):
        outs = refs[n:2 * n]
        send, recv = refs[2 * n:]
        x, y, c, _, chips = _place()
        sends = []
        for t, (px, py) in enumerate(chips):
            for a in range(n):
                blk = _half(outs[a].at[2 * px + py], c, stacks[idx[a]].shape[1])
                cp = _remote(blk, blk, send.at[a, t], recv.at[a, t], (x, y, 1 - c))
                cp.start()
                sends.append(cp)
        for t, (px, py) in enumerate(chips):
            for a in range(n):
                blk = _half(outs[a].at[2 * px + py], 1 - c, stacks[idx[a]].shape[1])
                _remote(blk, blk, send.at[a, t], recv.at[a, t], (x, y, 1 - c)).wait_recv()
        for cp in sends:
            cp.wait_send()

    sem = pltpu.SemaphoreType.DMA
    res = pl.pallas_call(
        body, name=f"gather_forward_{tag}", in_specs=[ANY] * n, out_specs=[ANY] * n,
        out_shape=[jax.ShapeDtypeStruct(stacks[a].shape, stacks[a].dtype) for a in idx],
        input_output_aliases={a: a for a in range(n)}, scratch_shapes=[sem((n, 3)), sem((n, 3))],
    )(*[stacks[a] for a in idx])
    out = list(stacks)
    for a, r in zip(idx, res):
        out[a] = r
    return out


def _swap_start(grads, tag):
    n = len(grads)

    def body(*refs):
        ins, gots = refs[:n], refs[n:2 * n]
        send, recv = refs[2 * n], refs[2 * n + 1]
        token = refs[4 * n + 2]
        x, y, c, _, _ = _place()
        for a in range(n):
            h = grads[a].shape[1] // 2
            _remote(ins[a].at[:, pl.ds((1 - c) * h, h)], gots[a], send.at[a], recv.at[a], (x, y, 1 - c)).start()
        token[...] = jnp.zeros_like(token)

    sems = pltpu.SemaphoreType.DMA((n,))
    halves = [(g.shape[0], g.shape[1] // 2, g.shape[2]) for g in grads]
    res = pl.pallas_call(
        body, name=f"swap_start_{tag}", in_specs=[HBM] * (2 * n),
        out_specs=[SEM, SEM] + [HBM] * (2 * n) + [pl.BlockSpec(memory_space=pltpu.VMEM)],
        out_shape=[sems, sems] + [pltpu.HBM(g.shape, g.dtype) for g in grads] + [pltpu.HBM(s, F32) for s in halves] + [TOKEN],
        input_output_aliases={a: a + 2 for a in range(2 * n)}, compiler_params=_in_flight_params(),
    )(*[_in_hbm(g) for g in grads], *[_in_hbm(lax.empty(s, F32)) for s in halves])
    return res[0], res[1], res[2:2 + n], res[2 + n:2 + 2 * n], res[2 + 2 * n]


def _swap_wait(send, recv, grads, gots, after, tag):
    n = len(grads)

    def body(*refs):
        ins, lnd = refs[:n], refs[n:2 * n]
        send_ref, recv_ref = refs[2 * n], refs[2 * n + 1]
        x, y, c, _, _ = _place()
        for a in range(n):
            h = grads[a].shape[1] // 2
            cp = _remote(ins[a].at[:, pl.ds((1 - c) * h, h)], lnd[a], send_ref.at[a], recv_ref.at[a], (x, y, 1 - c))
            cp.wait_send()
            cp.wait_recv()

    bufs = [pltpu.HBM(g.shape, g.dtype) for g in grads] + [pltpu.HBM(g.shape, g.dtype) for g in gots]
    res = pl.pallas_call(
        body, name=f"swap_wait_{tag}", in_specs=[HBM] * (2 * n) + [SEM, SEM, ANY], out_specs=[HBM] * (2 * n),
        out_shape=bufs, input_output_aliases={a: a for a in range(2 * n)}, compiler_params=_in_flight_params(),
    )(*grads, *gots, send, recv, after)
    return res[:n], res[n:]


def _exchange_start(parts, tag):
    n = len(parts)

    def body(*refs):
        ins, lands = refs[:n], refs[n:2 * n]
        send, recv = refs[2 * n], refs[2 * n + 1]
        token = refs[4 * n + 2]
        _, _, c, j, chips = _place()
        for t, (px, py) in enumerate(chips):
            for a in range(n):
                _remote(ins[a].at[2 * px + py], lands[a].at[j], send.at[3 * a + t], recv.at[3 * a + t], (px, py, c)).start()
        token[...] = jnp.zeros_like(token)

    sems = pltpu.SemaphoreType.DMA((3 * n,))
    bufs = [pltpu.HBM(p.shape, p.dtype) for p in parts]
    res = pl.pallas_call(
        body, name=f"exchange_start_{tag}", in_specs=[HBM] * (2 * n),
        out_specs=[SEM, SEM] + [HBM] * (2 * n) + [pl.BlockSpec(memory_space=pltpu.VMEM)],
        out_shape=[sems, sems] + bufs + bufs + [TOKEN],
        input_output_aliases={a: a + 2 for a in range(2 * n)}, compiler_params=_in_flight_params(),
    )(*[_in_hbm(p) for p in parts], *[_in_hbm(lax.empty(p.shape, p.dtype)) for p in parts])
    return res[0], res[1], res[2:2 + n], res[2 + n:2 + 2 * n], res[2 + 2 * n]


def _exchange_wait(send, recv, parts, lands, after, tag):
    n = len(parts)

    def body(*refs):
        ins, lnd = refs[:n], refs[n:2 * n]
        send_ref, recv_ref = refs[2 * n], refs[2 * n + 1]
        _, _, c, j, chips = _place()
        for t, (px, py) in enumerate(chips):
            jt = 2 * px + py
            for a in range(n):
                _remote(ins[a].at[jt], lnd[a].at[j], send_ref.at[3 * a + t], recv_ref.at[3 * a + t], (px, py, c)).wait_send()
                _remote(ins[a].at[jt], lnd[a].at[jt], send_ref.at[3 * a + t], recv_ref.at[3 * a + t], (px, py, c)).wait_recv()

    bufs = [pltpu.HBM(p.shape, p.dtype) for p in parts]
    res = pl.pallas_call(
        body, name=f"exchange_wait_{tag}", in_specs=[HBM] * (2 * n) + [SEM, SEM, ANY], out_specs=[HBM] * (2 * n),
        out_shape=bufs + bufs, input_output_aliases={a: a for a in range(2 * n)}, compiler_params=_in_flight_params(),
    )(*parts, *lands, send, recv, after)
    return res[:n], res[n:]


def _small_chip_sums(arrs):
    n = len(arrs)

    def body(*refs):
        ins, outs = refs[:n], refs[n:2 * n]
        sib = refs[2 * n:3 * n]
        send, recv = refs[3 * n:]
        x, y, c, j, _ = _place()
        swaps = [_remote(ins[a], sib[a], send.at[a], recv.at[a], (x, y, 1 - c)) for a in range(n)]
        for cp in swaps:
            cp.start()
        for a in range(n):
            swaps[a].wait_recv()
            outs[a][j] = ins[a][...] + sib[a][...]
        for cp in swaps:
            cp.wait_send()

    sem = pltpu.SemaphoreType.DMA
    vm = pl.BlockSpec(memory_space=pltpu.VMEM)
    return pl.pallas_call(
        body, name="small_chip_sums", in_specs=[vm] * n, out_specs=[vm] * n,
        out_shape=[jax.ShapeDtypeStruct((N_SHARD, *a.shape), F32) for a in arrs],
        scratch_shapes=[pltpu.VMEM(a.shape, F32) for a in arrs] + [sem((n,)), sem((n,))],
        compiler_params=_cp(),
    )(*arrs)


def _small_totals(stacks):
    n = len(stacks)

    def body(*refs):
        for a in range(n):
            refs[n + a][...] = ((refs[a][0] + refs[a][1]) + refs[a][2]) + refs[a][3]

    return pl.pallas_call(body, name="small_totals", out_shape=[jax.ShapeDtypeStruct(s.shape[1:], F32) for s in stacks],
                          compiler_params=_cp())(*stacks)


SMALL_1024 = ("ln1_g", "ln1_b", "ln2_g", "ln2_b", "b_ple_gate", "ln3_g", "ln3_b")


def _adamw_small(red3, red1, redz, g_conv_w, redc, red_ws, red_bs, params):
    shape2d = {"ln_z_g": (1, D_GMLP), "ln_z_b": (1, D_GMLP), "w_s": (N_HEADS * BLK, BLK), "b_s": (N_HEADS, BLK),
               "conv_w": (3, FF_BLK), "conv_b": (N_SHARD, FF_BLK), **{k: (1, D_MODEL) for k in SMALL_1024}}
    names = list(shape2d)
    flat = [a.reshape(shape2d[k]) for k in names for a in params[k]]

    def body(r3, r1, rz, gcw, rc, rws, rbs, *refs):
        ins, outs = refs[:3 * len(names)], refs[3 * len(names):]

        def grad_of(k):
            if k == "w_s":
                return rws[...]
            if k == "b_s":
                return rbs[...]
            if k == "conv_w":
                return gcw[0:3, :]
            if k == "conv_b":
                return jnp.concatenate([rc[j * STAT_ROWS + 3:j * STAT_ROWS + 4, :] for j in range(N_SHARD)], axis=0)
            src, row = {"ln3_g": (r3, 0), "ln3_b": (r3, 1), "b_ple_gate": (r3, 2), "ln2_g": (r3, 3), "ln2_b": (r3, 4),
                        "ln1_g": (r1, 0), "ln1_b": (r1, 1), "ln_z_g": (rz, 0), "ln_z_b": (rz, 1)}[k]
            return src[row:row + 1, :]

        for i, k in enumerate(names):
            w_ref, m_ref, v_ref = ins[3 * i:3 * i + 3]
            g_ref, d_ref, nm_ref, nv_ref = outs[4 * i:4 * i + 4]
            g = grad_of(k)
            g_ref[...] = g
            d_ref[...], nm_ref[...], nv_ref[...] = _adamw_math(w_ref[...], g, m_ref[...], v_ref[...])

    res = pl.pallas_call(
        body, name="adamw_small",
        out_shape=[jax.ShapeDtypeStruct(shape2d[k], F32) for k in names for _ in range(4)],
        compiler_params=_cp(),
    )(red3, red1, redz, g_conv_w, redc, red_ws, red_bs, *flat)
    return {k: tuple(r.reshape(params[k][0].shape) for r in res[4 * i:4 * i + 4]) for i, k in enumerate(names)}


WEIGHTS = ("w_in", "ln_z_g", "ln_z_b", "w_s", "b_s", "w_o", "ln1_g", "ln1_b", "w_ff_a", "w_ff_b", "conv_w", "conv_b",
           "w_ff_down", "ln2_g", "ln2_b", "w_ple_gate", "b_ple_gate", "w_ple_in", "ln3_g", "ln3_b")
BIG = ("w_in", "w_o", "w_ff_a", "w_ff_b", "w_ff_down", "w_ple_gate", "w_ple_in")
TRANSPOSED = ("w_ff_a", "w_ff_b")
LATE = ("w_o", "w_ff_a", "w_ff_b", "w_ff_down", "w_ple_gate", "w_ple_in", "conv_w")


def kernel(x, p, positions, w_in, ln_z_g, ln_z_b, w_s, b_s, w_o, ln1_g, ln1_b, w_ff_a, w_ff_b, conv_w, conv_b, w_ff_down, ln2_g, ln2_b, w_ple_gate, b_ple_gate, w_ple_in, ln3_g, ln3_b, loss_target, m_w_in, m_ln_z_g, m_ln_z_b, m_w_s, m_b_s, m_w_o, m_ln1_g, m_ln1_b, m_w_ff_a, m_w_ff_b, m_conv_w, m_conv_b, m_w_ff_down, m_ln2_g, m_ln2_b, m_w_ple_gate, m_b_ple_gate, m_w_ple_in, m_ln3_g, m_ln3_b, v_w_in, v_ln_z_g, v_ln_z_b, v_w_s, v_b_s, v_w_o, v_ln1_g, v_ln1_b, v_w_ff_a, v_w_ff_b, v_conv_w, v_conv_b, v_w_ff_down, v_ln2_g, v_ln2_b, v_w_ple_gate, v_b_ple_gate, v_w_ple_in, v_ln3_g, v_ln3_b):
    args = locals()
    w = {k: args[k] for k in WEIGHTS}
    m = {k: args["m_" + k] for k in WEIGHTS}
    v = {k: args["v_" + k] for k in WEIGHTS}

    for k in TRANSPOSED:
        w[k], m[k], v[k] = (jnp.swapaxes(a, 1, 2) for a in (w[k], m[k], v[k]))

    chip = 2 * lax.axis_index("x") + lax.axis_index("y")
    place = jnp.stack([chip, lax.axis_index("c")]).astype(jnp.int32)
    stack = dict(zip(["w_in"], _place_shards("cast_w_in", [w["w_in"][0]], [MXU], place, place)))
    i_send, i_recv, in_flight, dep = _gather_start([stack["w_in"]], [True], place, "w_in")
    stack.update(zip(LATE, _place_shards("cast_late", [w[k][0] for k in LATE],
                                         [F32 if k == "conv_w" else MXU for k in LATE], place, dep)))
    rope = _rope_tables(positions, x.shape[1], stack[LATE[-1]])
    landed_in = _gather_wait(i_send, i_recv, in_flight, [True], rope[0], "w_in")
    w_in_full, = _gather_forward(landed_in, [True], "w_in")
    o_send, o_recv, o_flight, dep = _gather_start([stack["w_o"]], [True], w_in_full, "w_o")
    ffn_names = [k for k in LATE if k != "w_o"]
    split_ffn = [k != "conv_w" for k in ffn_names]
    g_send, g_recv, late_flight, start_dep = _gather_start([stack[k] for k in ffn_names], split_ffn, dep, "late")

    def late_w_o(after):
        landed = _gather_wait(o_send, o_recv, o_flight, [True], after, "w_o")
        return _gather_forward(landed, [True], "w_o")[0].reshape(D_MODEL, D_MODEL)

    def late_weights(after):
        landed = _gather_wait(g_send, g_recv, late_flight, split_ffn, after, "late")
        fw = dict(zip(ffn_names, _gather_forward(landed, split_ffn, "late")))
        return (fw["w_ff_a"], fw["w_ff_b"], fw["conv_w"], fw["w_ff_down"], fw["w_ple_gate"].reshape(D_MODEL, D_MODEL),
                fw["w_ple_in"])

    def swap_started(names, grads, tag):
        stacked = [g.reshape(N_SHARD, *w[k].shape[1:]) for k, g in zip(names, grads)]
        return (names, tag, *_swap_start(stacked, tag))

    def partial_sums(swap, after):
        names, tag, send, recv, stacked, gots, _ = swap
        stacked, got = _swap_wait(send, recv, stacked, gots, after, tag)
        pair = _pair_sums(f"rs_pair_{tag}", stacked, got, place)
        return (names, tag, *_exchange_start(pair, tag))

    def reduced(trip, after, dep):
        names, tag, send, recv, pair, lands, _ = trip
        pair, landed = _exchange_wait(send, recv, pair, lands, after, tag)
        blocks = _chip_sums(f"rs_sum_{tag}", pair, landed, place, dep)
        return dict(zip(names, _sibling_join(blocks, tag)))

    trips = {}

    def early_grads(grads):
        trips["swap"] = swap_started(list(grads), list(grads.values()), "early")
        return trips["swap"][-1]

    def early_grads_sent(after, small):
        trips["early"] = partial_sums(trips["swap"], after)
        stat3, stat1, zstat, cstat, dws, dbs = small
        sums = _small_chip_sums([stat3, stat1, zstat, cstat.reshape(N_SHARD * STAT_ROWS, FF_BLK),
                                 dws.reshape(N_HEADS * BLK, BLK), dbs])
        trips["small"] = _gather_start(sums, [False] * len(sums), trips["early"][-1], "small")
        return trips["small"][-1]

    grad_x, g_w_in = _local_step(
        x[0], p[0, 0], rope, loss_target[0], w_in_full, start_dep, late_w_o, late_weights, early_grads, early_grads_sent,
        ln_z_g, ln_z_b, w_s, b_s, ln1_g, ln1_b, conv_b, ln2_g, ln2_b, b_ple_gate, ln3_g, ln3_b)

    trips["w_in"] = partial_sums(swap_started(["w_in"], [g_w_in], "w_in"), g_w_in)
    out = {}

    def adamw(red, tag):
        names = list(red)
        steps = _adamw_shards(f"adamw_{tag}", [w[k] for k in names], [red[k] for k in names], [m[k] for k in names],
                              [v[k] for k in names])
        for k, (d, nm, nv) in zip(names, steps):
            out[k] = (red[k].reshape(w[k].shape), d, nm, nv)

    adamw(reduced(trips["early"], grad_x, trips["w_in"][-1]), "early")
    adamw(reduced(trips["w_in"], out["w_o"][3], start_dep), "w_in")
    for k in TRANSPOSED:
        out[k] = tuple(jnp.swapaxes(a, 1, 2) for a in out[k])

    s_send, s_recv, s_flight, _ = trips["small"]
    red3, red1, redz, redc, red_ws, red_bs = _small_totals(
        _gather_wait(s_send, s_recv, s_flight, [False] * len(s_flight), out["w_in"][3], "small"))
    loss = (0.5 / D_MODEL) * jnp.sum(red3[5])
    g_conv_w = lax.dynamic_slice_in_dim(redc, chip * STAT_ROWS, STAT_ROWS, 0)
    names_small = [k for k in WEIGHTS if k not in BIG]
    out.update(_adamw_small(red3, red1, redz, g_conv_w, redc, red_ws, red_bs, {k: (w[k], m[k], v[k]) for k in names_small}))

    return (loss, grad_x[None], *[out[k][0] for k in WEIGHTS], *[out[k][1] for k in WEIGHTS],
            *[out[k][2] for k in WEIGHTS], *[out[k][3] for k in WEIGHTS])
```

```python
import functools
import math

import numpy as np
import jax
import jax.numpy as jnp
from jax import lax
from jax.experimental import pallas as pl
from jax.experimental.pallas import tpu as pltpu

F32 = jnp.float32
BF16 = jnp.bfloat16
MXU = BF16

D_MODEL = 1024
HEAD_DIM = 64
N_HEADS = 8
D_ATTN = 512
D_GMLP = 512
D_IN = 2560
DILATIONS = (1, 4, 16)
BLK = 128
ROPE_THETA = 500000.0
ROPE_DIM = 16
D_FF = 2816
D_PLE = 256
LN_EPS = 1e-5
ALPHA = 2.0 ** 0.25
NEG_INF = -1e30
N_SHARD = 4
W_IN_BLK = D_IN // N_SHARD
FF_BLK = D_FF // N_SHARD
ROW_BLK = D_MODEL // N_SHARD
ADAM_LR, ADAM_B1, ADAM_B2, ADAM_EPS, ADAM_WD, ADAM_STEP = 0.001, 0.9, 0.999, 1e-08, 0.01, 10

TM = 512
HALO = 8
ROW_GROUPS = 2
VMEM_LIMIT = 56 * 1024 * 1024


def _cp(**kw):
    return pltpu.CompilerParams(vmem_limit_bytes=VMEM_LIMIT, **kw)


def _full(shape):
    n = len(shape)
    return pl.BlockSpec(shape, lambda *_: (0,) * n)


def _gelu(x):
    return 0.5 * x * (1.0 + lax.erf(x * (1.0 / math.sqrt(2.0))))


def _gelu_grad(x):
    return 0.5 * (1.0 + lax.erf(x * (1.0 / math.sqrt(2.0)))) + x * jnp.exp(-0.5 * x * x) * (1.0 / math.sqrt(2.0 * math.pi))


def _ln_fwd(r):
    mu = jnp.mean(r, axis=-1, keepdims=True)
    xc = r - mu
    var = jnp.mean(xc * xc, axis=-1, keepdims=True)
    rstd = lax.rsqrt(var + LN_EPS)
    return xc * rstd, rstd


def _ln_bwd(dy, xhat, rstd, g):
    dxh = dy * g
    m1 = jnp.mean(dxh, axis=-1, keepdims=True)
    m2 = jnp.mean(dxh * xhat, axis=-1, keepdims=True)
    return rstd * (dxh - m1 - xhat * m2)


def _dot(a, b):
    return jnp.dot(a.astype(MXU), b.astype(MXU), preferred_element_type=F32)


def _dot_nt(a, b):
    return lax.dot_general(a.astype(MXU), b.astype(MXU), (((1,), (1,)), ((), ())), preferred_element_type=F32)


def _dot_tn(a, b):
    return lax.dot_general(a.astype(MXU), b.astype(MXU), (((0,), (0,)), ((), ())), preferred_element_type=F32)


def _colsum(v):
    return jnp.sum(v, axis=0, keepdims=True)


def _rope_tables(positions, t, dep):
    inv = np.float32(ROPE_THETA) ** (-np.arange(0, ROPE_DIM, 2, dtype=np.float32) / np.float32(ROPE_DIM))
    half = ROPE_DIM // 2
    pos_rep = jnp.repeat(positions.reshape(t // 16, 16), half, axis=1)
    inv_row = jnp.asarray(np.tile(inv, 16)[None, :], F32)

    def trig_body(pos_ref, inv_ref, dep_ref, cos_ref, sin_ref):
        ang = pos_ref[...].astype(F32) * inv_ref[...]
        cos_ref[...] = jnp.cos(ang)
        sin_ref[...] = jnp.sin(ang)

    vm = pl.BlockSpec(memory_space=pltpu.VMEM)
    cos8, sin8 = pl.pallas_call(
        trig_body, name="rope_trig", in_specs=[vm, vm, pl.BlockSpec(memory_space=pl.ANY)], out_specs=[vm, vm],
        out_shape=(jax.ShapeDtypeStruct((t // 16, 128), F32), jax.ShapeDtypeStruct((t // 16, 128), F32)),
    )(pos_rep, inv_row, dep)
    cos8 = cos8.reshape(t, half)
    sin8 = sin8.reshape(t, half)

    lane = np.arange(128) % HEAD_DIM
    sel = (np.arange(half)[:, None] == (lane % half)[None, :])
    e_cos = (sel & (lane < ROPE_DIM)[None, :]).astype(np.float32)
    e_s1 = -(sel & (lane < half)[None, :]).astype(np.float32)
    e_s2 = (sel & ((lane >= half) & (lane < ROPE_DIM))[None, :]).astype(np.float32)
    ones = (lane >= ROPE_DIM).astype(np.float32)[None, :]

    def expand_body(cos_ref, sin_ref, ec_ref, e1_ref, e2_ref, ones_ref, c_ref, s1_ref, s2_ref):
        hp = lax.Precision.HIGHEST
        c_ref[...] = jnp.dot(cos_ref[...], ec_ref[...], precision=hp, preferred_element_type=F32) + ones_ref[...]
        s1_ref[...] = jnp.dot(sin_ref[...], e1_ref[...], precision=hp, preferred_element_type=F32)
        s2_ref[...] = jnp.dot(sin_ref[...], e2_ref[...], precision=hp, preferred_element_type=F32)

    tab = jax.ShapeDtypeStruct((t, 128), F32)
    return pl.pallas_call(expand_body, name="rope_expand", out_shape=(tab, tab, tab), compiler_params=_cp())(
        cos8, sin8, jnp.asarray(e_cos), jnp.asarray(e_s1), jnp.asarray(e_s2), jnp.asarray(ones))


def _tile_heads(tab):
    return jnp.concatenate([tab] * (D_ATTN // 128), axis=1)


def _rope_apply(v, c, s1, s2):
    n = v.shape[1]
    half = ROPE_DIM // 2
    return v * c + pltpu.roll(v, n - half, 1) * s1 + pltpu.roll(v, half, 1) * s2


def _rope_apply_t(g, c, s1, s2):
    n = g.shape[1]
    half = ROPE_DIM // 2
    return g * c + pltpu.roll(g * s1, half, 1) + pltpu.roll(g * s2, n - half, 1)


LANE_CHUNKS = D_ATTN // 128
HEAD_LANES = 128 // N_HEADS


def _perm_shape(t, d, w, dtype):
    return jax.ShapeDtypeStruct((d, t // d, w), dtype)


def _perm_tile(d, w):
    return pl.BlockSpec((None if d == 1 else d, TM // d, w), lambda i: (0, i, 0))


def _to_planes(ref, scr, d, n_chunks, dtype):
    for r in range(d):
        for cc in range(n_chunks):
            ref[r, :, cc * 128:(cc + 1) * 128] = scr.at[cc][pl.ds(r, TM // d, stride=d), :].astype(dtype)


def _from_planes(ref, scr, d, n_chunks, accumulate=False):
    for r in range(d):
        for cc in range(n_chunks):
            rows = scr.at[cc]
            val = ref[r, :, cc * 128:(cc + 1) * 128].astype(F32)
            if accumulate:
                rows[pl.ds(r, TM // d, stride=d), :] += val
            else:
                rows[pl.ds(r, TM // d, stride=d), :] = val


def _chunks(val):
    return [val[:, cc * 128:(cc + 1) * 128] for cc in range(val.shape[1] // 128)]


def _unchunk(scr, n_chunks, base=0):
    return jnp.concatenate([scr[base + cc] for cc in range(n_chunks)], axis=1)


def _head_expand():
    src = np.arange(128)[:, None]
    dst = np.arange(D_ATTN)[None, :]
    return jnp.asarray((src == (dst // HEAD_DIM) * HEAD_LANES).astype(np.float32))


def _head_reduce():
    src = np.arange(D_ATTN)[:, None]
    dst = np.arange(128)[None, :]
    return jnp.asarray((src // HEAD_DIM == dst // HEAD_LANES).astype(np.float32))


def _dot_select(a, sel):
    hi = a.astype(BF16)
    lo = (a - hi.astype(F32)).astype(BF16)
    sel = sel.astype(BF16)
    return jnp.dot(hi, sel, preferred_element_type=F32) + jnp.dot(lo, sel, preferred_element_type=F32)


def _qkvuz(x, w_in, c_tab, s1_tab, s2_tab, ln_z_g, ln_z_b, w_s, b_full, dep):
    t = x.shape[0]
    nchunk = TM // BLK

    def body(x_ref, w_ref, c_ref, s1_ref, s2_ref, g_ref, b_ref, ws_ref, bf_ref, dep_ref,
             qkv1_ref, qkv4_ref, qkv16_ref, hu_ref, hz_ref, mixed_ref, gm_ref, h_scr, wm_scr, p_scr):
        @pl.when(pl.program_id(0) == 0)
        def _():
            row = lax.broadcasted_iota(jnp.int32, (BLK, BLK), 0)
            col = lax.broadcasted_iota(jnp.int32, (BLK, BLK), 1)
            for g in range(N_HEADS):
                wm_scr[g] = jnp.where(col <= row, ws_ref[g], 0.0).astype(MXU)

        xb = x_ref[...].astype(MXU)
        for j in range(N_SHARD):
            h_scr[:, j * W_IN_BLK:(j + 1) * W_IN_BLK] = jnp.dot(xb, w_ref[j], preferred_element_type=F32)
        c, s1, s2 = _tile_heads(c_ref[...]), _tile_heads(s1_ref[...]), _tile_heads(s2_ref[...])
        q = _rope_apply(h_scr[:, 0:D_ATTN], c, s1, s2) * (1.0 / math.sqrt(HEAD_DIM))
        k = _rope_apply(h_scr[:, D_ATTN:2 * D_ATTN], c, s1, s2)
        for part, val in enumerate((q, k, h_scr[:, 2 * D_ATTN:3 * D_ATTN])):
            qkv1_ref[:, part * D_ATTN:(part + 1) * D_ATTN] = val.astype(MXU)
            for cc in range(LANE_CHUNKS):
                p_scr[part * LANE_CHUNKS + cc] = val[:, cc * 128:(cc + 1) * 128]
        _to_planes(qkv4_ref, p_scr, DILATIONS[1], 3 * LANE_CHUNKS, MXU)
        _to_planes(qkv16_ref, p_scr, DILATIONS[2], 3 * LANE_CHUNKS, MXU)
        hu = h_scr[:, 3 * D_ATTN:3 * D_ATTN + D_GMLP]
        hz = h_scr[:, 3 * D_ATTN + D_GMLP:]
        hu_ref[...] = hu
        hz_ref[...] = hz
        zhat, _ = _ln_fwd(_gelu(hz))
        zn = (zhat * g_ref[...] + b_ref[...]).astype(MXU)
        for ch in range(nchunk):
            rows = slice(ch * BLK, (ch + 1) * BLK)
            for g in range(N_HEADS):
                cols = slice(g * HEAD_DIM, (g + 1) * HEAD_DIM)
                mixed_ref[rows, cols] = jnp.dot(wm_scr[g], zn[rows, cols], preferred_element_type=F32) + bf_ref[:, cols]
        gm_ref[...] = (_gelu(hu) * mixed_ref[...]).astype(MXU)

    tok = lambda w: pl.BlockSpec((TM, w), lambda i: (i, 0))
    outs = [_perm_shape(t, d, 3 * D_ATTN, MXU) for d in DILATIONS] + [jax.ShapeDtypeStruct((t, D_GMLP), F32)] * 3 + [
        jax.ShapeDtypeStruct((t, D_GMLP), MXU)]
    return pl.pallas_call(
        body, name="qkvuz", grid=(t // TM,),
        in_specs=[tok(D_MODEL), _full(w_in.shape), tok(128), tok(128), tok(128), _full(ln_z_g.shape), _full(ln_z_b.shape),
                  _full(w_s.shape), _full(b_full.shape), pl.BlockSpec(memory_space=pl.ANY)],
        out_specs=[_perm_tile(d, 3 * D_ATTN) for d in DILATIONS] + [tok(D_ATTN)] * 4, out_shape=outs,
        scratch_shapes=[pltpu.VMEM((TM, D_IN), F32), pltpu.VMEM((N_HEADS, BLK, BLK), MXU),
                        pltpu.VMEM((3 * LANE_CHUNKS, TM, 128), F32)],
        compiler_params=_cp(dimension_semantics=("arbitrary",)),
    )(x, w_in, c_tab, s1_tab, s2_tab, ln_z_g, ln_z_b, w_s, b_full, dep)


def _band_valid(n):
    i = lax.broadcasted_iota(jnp.int32, (BLK, 2 * BLK), 0)
    j = lax.broadcasted_iota(jnp.int32, (BLK, 2 * BLK), 1)
    return (j >= i) & (j <= i + BLK) & ((j >= BLK) | (n > 0))


def _attn_fwd(qkv, d, dep):
    _, l_sub, _ = qkv.shape
    nb = l_sub // BLK

    def body(q_ref, kp_ref, kc_ref, vp_ref, vc_ref, dep_ref, o_ref, l_ref):
        valid = _band_valid(pl.program_id(1))
        kcat = jnp.concatenate([kp_ref[...], kc_ref[...]], axis=0)
        vcat = jnp.concatenate([vp_ref[...], vc_ref[...]], axis=0)
        for h in range(N_HEADS):
            cols = slice(h * HEAD_DIM, (h + 1) * HEAD_DIM)
            s = jnp.where(valid, _dot_nt(q_ref[:, cols], kcat[:, cols]), NEG_INF)
            m = jnp.max(s, axis=-1, keepdims=True)
            e = jnp.exp(s - m)
            den = jnp.sum(e, axis=-1, keepdims=True)
            o_ref[:, cols] = _dot(e, vcat[:, cols]) * (1.0 / den)
            l_ref[:, h * HEAD_LANES:(h + 1) * HEAD_LANES] = jnp.broadcast_to(m + jnp.log(den), (BLK, HEAD_LANES))

    def blk(w, col, prev=False):
        return pl.BlockSpec((None, BLK, w), lambda r, n: (r, jnp.maximum(n - 1, 0) if prev else n, col))

    return pl.pallas_call(
        body, name=f"attn_fwd_d{d}", grid=(d, nb),
        in_specs=[blk(D_ATTN, 0), blk(D_ATTN, 1, True), blk(D_ATTN, 1), blk(D_ATTN, 2, True), blk(D_ATTN, 2),
                  pl.BlockSpec(memory_space=pl.ANY)],
        out_specs=[blk(D_ATTN, 0), blk(128, 0)],
        out_shape=[jax.ShapeDtypeStruct((d, l_sub, D_ATTN), F32), jax.ShapeDtypeStruct((d, l_sub, 128), F32)],
        compiler_params=_cp(dimension_semantics=("arbitrary", "arbitrary")),
    )(qkv, qkv, qkv, qkv, qkv, dep)


def _attn_bwd(qkv, do, lse, delta, d, dep):
    _, l_sub, _ = qkv.shape
    nb = l_sub // BLK
    whole = l_sub <= 8 * BLK

    def shares(n, q_ref, kp_ref, kc_ref, vp_ref, vc_ref, do_ref, l_ref, dl_ref, dq_ref):
        valid = _band_valid(n)
        kcat = jnp.concatenate([kp_ref[...], kc_ref[...]], axis=0)
        vcat = jnp.concatenate([vp_ref[...], vc_ref[...]], axis=0)
        for h in range(N_HEADS):
            cols = slice(h * HEAD_DIM, (h + 1) * HEAD_DIM)
            stat = slice(h * HEAD_LANES, h * HEAD_LANES + 1)
            qh, doh = q_ref[:, cols], do_ref[:, cols]
            p = jnp.where(valid, jnp.exp(_dot_nt(qh, kcat[:, cols]) - l_ref[:, stat]), 0.0)
            ds = p * (_dot_nt(doh, vcat[:, cols]) - dl_ref[:, stat])
            dq_ref[:, cols] = _dot(ds, kcat[:, cols])
            yield cols, _dot_tn(ds, qh), _dot_tn(p, doh)

    def body_whole(*refs):
        dk_ref, dv_ref = refs[10:]
        n = pl.program_id(1)
        cur = pl.ds(pl.multiple_of(n * BLK, BLK), BLK)
        prev = pl.ds(pl.multiple_of(jnp.maximum(n - 1, 0) * BLK, BLK), BLK)
        for cols, dk2, dv2 in shares(n, *refs[:8], refs[9]):
            dk_ref[cur, cols] = dk2[BLK:]
            dv_ref[cur, cols] = dv2[BLK:]
            dk_ref[prev, cols] += dk2[0:BLK]
            dv_ref[prev, cols] += dv2[0:BLK]

    def body_carry(*refs):
        dk_ref, dv_ref, ck_scr, cv_scr = refs[10:]
        n = pl.program_id(1)

        @pl.when(n == 0)
        def _():
            ck_scr[...] = jnp.zeros_like(ck_scr)
            cv_scr[...] = jnp.zeros_like(cv_scr)

        @pl.when(n < nb)
        def _():
            for cols, dk2, dv2 in shares(n, *refs[:8], refs[9]):
                dk_ref[:, cols] = ck_scr[:, cols] + dk2[0:BLK]
                dv_ref[:, cols] = cv_scr[:, cols] + dv2[0:BLK]
                ck_scr[:, cols] = dk2[BLK:]
                cv_scr[:, cols] = dv2[BLK:]

        @pl.when(n == nb)
        def _():
            dk_ref[...] = ck_scr[...]
            dv_ref[...] = cv_scr[...]

    def blk(w, col, shift=0):
        return pl.BlockSpec((None, BLK, w), lambda r, n: (r, jnp.clip(n - shift, 0, nb - 1), col))

    if whole:
        dkv_spec = pl.BlockSpec((None, l_sub, D_ATTN), lambda r, n: (r, 0, 0))
        body, steps, scratch = body_whole, nb, []
    else:
        dkv_spec = blk(D_ATTN, 0, 1)
        body, steps, scratch = body_carry, nb + 1, [pltpu.VMEM((BLK, D_ATTN), F32)] * 2
    return pl.pallas_call(
        body, name=f"attn_bwd_d{d}", grid=(d, steps),
        in_specs=[blk(D_ATTN, 0), blk(D_ATTN, 1, 1), blk(D_ATTN, 1), blk(D_ATTN, 2, 1), blk(D_ATTN, 2),
                  blk(D_ATTN, 0), blk(128, 0), blk(128, 0), pl.BlockSpec(memory_space=pl.ANY)],
        out_specs=[blk(D_ATTN, 0), dkv_spec, dkv_spec],
        out_shape=[jax.ShapeDtypeStruct((d, l_sub, D_ATTN), F32)] * 3,
        scratch_shapes=scratch,
        compiler_params=_cp(dimension_semantics=("arbitrary", "arbitrary")),
    )(qkv, qkv, qkv, qkv, qkv, do, lse, delta, dep)


def _mix_ln1(os_, ls_, gm, x, w_o, ln1_g, ln1_b):
    t = x.shape[0]
    expand = _head_expand()

    def body(o1, o4, o16, l1, l4, l16, gm_ref, x_ref, wo_ref, g_ref, b_ref, ex_ref,
             attn_ref, lse1_ref, lse4_ref, lse16_ref, cat_ref, xhat_ref, rstd_ref, x1b_ref, o_scr, l_scr):
        _from_planes(o4, o_scr, DILATIONS[1], LANE_CHUNKS)
        _from_planes(o16, o_scr.at[pl.ds(LANE_CHUNKS, LANE_CHUNKS)], DILATIONS[2], LANE_CHUNKS)
        _from_planes(l4, l_scr, DILATIONS[1], 1)
        _from_planes(l16, l_scr.at[pl.ds(1, 1)], DILATIONS[2], 1)
        la, lb, lc = l1[...], l_scr[0], l_scr[1]
        m = jnp.maximum(jnp.maximum(la, lb), lc)
        ea, eb, ec = jnp.exp(la - m), jnp.exp(lb - m), jnp.exp(lc - m)
        den = ea + eb + ec
        inv = 1.0 / den
        wide = lambda w: _dot_select(w, ex_ref[...])
        attn = (wide(ea * inv) * o1[...] + wide(eb * inv) * _unchunk(o_scr, LANE_CHUNKS)
                + wide(ec * inv) * _unchunk(o_scr, LANE_CHUNKS, LANE_CHUNKS))
        attn_ref[...] = attn
        lse = m + jnp.log(den)
        lse1_ref[...] = lse
        l_scr[2] = lse
        _to_planes(lse4_ref, l_scr.at[pl.ds(2, 1)], DILATIONS[1], 1, F32)
        _to_planes(lse16_ref, l_scr.at[pl.ds(2, 1)], DILATIONS[2], 1, F32)
        cat_ref[:, 0:D_ATTN] = attn.astype(MXU)
        cat_ref[:, D_ATTN:] = gm_ref[...]
        mix = jnp.dot(cat_ref[...], wo_ref[...], preferred_element_type=F32)
        xhat, rstd = _ln_fwd(ALPHA * x_ref[...] + mix)
        xhat_ref[...] = xhat
        rstd_ref[...] = rstd
        x1b_ref[...] = (xhat * g_ref[...] + b_ref[...]).astype(MXU)

    tok = lambda w: pl.BlockSpec((TM, w), lambda i: (i, 0))
    outs = [jax.ShapeDtypeStruct((t, D_ATTN), F32)] + [_perm_shape(t, d, 128, F32) for d in DILATIONS] + [
        jax.ShapeDtypeStruct((t, D_MODEL), MXU), jax.ShapeDtypeStruct((t, D_MODEL), F32), jax.ShapeDtypeStruct((t, 1), F32),
        jax.ShapeDtypeStruct((t, D_MODEL), MXU)]
    return pl.pallas_call(
        body, name="mix_ln1", grid=(t // TM,),
        in_specs=[_perm_tile(d, D_ATTN) for d in DILATIONS] + [_perm_tile(d, 128) for d in DILATIONS]
        + [tok(D_GMLP), tok(D_MODEL), _full(w_o.shape), _full(ln1_g.shape), _full(ln1_b.shape), _full(expand.shape)],
        out_specs=[tok(D_ATTN)] + [_perm_tile(d, 128) for d in DILATIONS] + [tok(D_MODEL), tok(D_MODEL), tok(1), tok(D_MODEL)],
        out_shape=outs,
        scratch_shapes=[pltpu.VMEM((2 * LANE_CHUNKS, TM, 128), F32), pltpu.VMEM((3, TM, 128), F32)],
        compiler_params=_cp(dimension_semantics=("arbitrary",)),
    )(*os_, *ls_, gm, x, w_o, ln1_g, ln1_b, expand)


def _conv_fwd(a_ext, w_ref, b_ref, rows):
    return (b_ref[...] + w_ref[2:3, :] * a_ext[HALO:HALO + rows] + w_ref[1:2, :] * a_ext[HALO - 1:HALO - 1 + rows]
            + w_ref[0:1, :] * a_ext[HALO - 2:HALO - 2 + rows])


def _ffn_in(x1b, w_a, w_b, conv_w, conv_b):
    t = x1b.shape[0]
    hb = TM // HALO

    def body(x_ref, xh_ref, wa_ref, wb_ref, cw_ref, cb_ref, apre_ref, a_ref, b_ref, f_ref):
        i = pl.program_id(1)
        a_pre = _dot_nt(x_ref[...], wa_ref[...])
        a_halo = jnp.where(i > 0, _dot_nt(xh_ref[...], wa_ref[...]), 0.0)
        a = _conv_fwd(jnp.concatenate([a_halo, a_pre], axis=0), cw_ref, cb_ref, TM)
        b = _dot_nt(x_ref[...], wb_ref[...])
        apre_ref[...] = a_pre
        a_ref[...] = a
        b_ref[...] = b
        f_ref[...] = (_gelu(a) * b).astype(MXU)

    blk = lambda r, c: pl.BlockSpec((None, r, c), lambda j, i: (j, 0, 0))
    tokj = pl.BlockSpec((None, TM, FF_BLK), lambda j, i: (j, i, 0))
    outs = [jax.ShapeDtypeStruct((N_SHARD, t, FF_BLK), F32)] * 3 + [jax.ShapeDtypeStruct((N_SHARD, t, FF_BLK), MXU)]
    return pl.pallas_call(
        body, name="ffn_in", grid=(N_SHARD, t // TM),
        in_specs=[pl.BlockSpec((TM, D_MODEL), lambda j, i: (i, 0)),
                  pl.BlockSpec((HALO, D_MODEL), lambda j, i: (jnp.maximum(i * hb - 1, 0), 0)),
                  blk(FF_BLK, D_MODEL), blk(FF_BLK, D_MODEL), blk(3, FF_BLK), blk(1, FF_BLK)],
        out_specs=[tokj, tokj, tokj, tokj], out_shape=outs,
        compiler_params=_cp(dimension_semantics=("arbitrary", "arbitrary")),
    )(x1b, x1b, w_a, w_b, conv_w, conv_b)


def _ffn_out_ln2(f, w_down, xhat1, ln1_g, ln1_b, ln2_g, ln2_b):
    t = xhat1.shape[0]

    def body(f_ref, wd_ref, xh_ref, g1_ref, b1_ref, g2_ref, b2_ref, xhat_ref, rstd_ref, x2b_ref):
        ff = jnp.dot(f_ref[0], wd_ref[0], preferred_element_type=F32)
        for j in range(1, N_SHARD):
            ff = ff + jnp.dot(f_ref[j], wd_ref[j], preferred_element_type=F32)
        x1 = xh_ref[...] * g1_ref[...] + b1_ref[...]
        xhat, rstd = _ln_fwd(ALPHA * x1 + ff)
        xhat_ref[...] = xhat
        rstd_ref[...] = rstd
        x2b_ref[...] = (xhat * g2_ref[...] + b2_ref[...]).astype(MXU)

    tok = lambda w: pl.BlockSpec((TM, w), lambda i: (i, 0))
    vec = _full((1, D_MODEL))
    outs = [jax.ShapeDtypeStruct((t, D_MODEL), F32), jax.ShapeDtypeStruct((t, 1), F32), jax.ShapeDtypeStruct((t, D_MODEL), MXU)]
    return pl.pallas_call(
        body, name="ffn_out_ln2", grid=(t // TM,),
        in_specs=[pl.BlockSpec((N_SHARD, TM, FF_BLK), lambda i: (0, i, 0)), _full(w_down.shape), tok(D_MODEL), vec, vec, vec, vec],
        out_specs=[tok(D_MODEL), tok(1), tok(D_MODEL)], out_shape=outs,
        compiler_params=_cp(dimension_semantics=("arbitrary",)),
    )(f, w_down, xhat1, ln1_g, ln1_b, ln2_g, ln2_b)


STAT_ROWS = 8


def _ple_loss_bwd(xhat2, rstd2, p, target, ln2_g, ln2_b, w_g, b_g, w_p, ln3_g, ln3_b):
    t = xhat2.shape[0]

    def body(xh2_ref, rs2_ref, p_ref, t_ref, g2_ref, b2_ref, wg_ref, bg_ref, wp_ref, g3_ref, b3_ref,
             dr2_ref, dgp_ref, dpp_ref, stat_ref, pp_scr):
        @pl.when(pl.program_id(0) == 0)
        def _():
            stat_ref[...] = jnp.zeros_like(stat_ref)

        xhat2 = xh2_ref[...]
        x2 = xhat2 * g2_ref[...] + b2_ref[...]
        gate = jax.nn.sigmoid(jnp.dot(x2.astype(MXU), wg_ref[...], preferred_element_type=F32) + bg_ref[...])
        pb = p_ref[...].astype(MXU)
        for j in range(N_SHARD):
            pp_scr[:, j * ROW_BLK:(j + 1) * ROW_BLK] = jnp.dot(pb, wp_ref[j], preferred_element_type=F32)
        pp = pp_scr[...]
        xhat3, rstd3 = _ln_fwd(ALPHA * x2 + gate * pp)
        err = xhat3 * g3_ref[...] + b3_ref[...] - t_ref[...]
        dy = err * (1.0 / D_MODEL)
        dr3 = _ln_bwd(dy, xhat3, rstd3, g3_ref[...])
        dgp = dr3 * pp * gate * (1.0 - gate)
        dgp_ref[...] = dgp.astype(MXU)
        dpp_ref[...] = (dr3 * gate).astype(MXU)
        dx2 = ALPHA * dr3 + _dot_nt(dgp, wg_ref[...])
        dr2_ref[...] = _ln_bwd(dx2, xhat2, rs2_ref[...], g2_ref[...])
        stat_ref[0:1, :] += _colsum(dy * xhat3)
        stat_ref[1:2, :] += _colsum(dy)
        stat_ref[2:3, :] += _colsum(dgp)
        stat_ref[3:4, :] += _colsum(dx2 * xhat2)
        stat_ref[4:5, :] += _colsum(dx2)
        stat_ref[5:6, :] += _colsum(err * err)

    tok = lambda w: pl.BlockSpec((TM, w), lambda i: (i, 0))
    vec = _full((1, D_MODEL))
    outs = [jax.ShapeDtypeStruct((t, D_MODEL), F32), jax.ShapeDtypeStruct((t, D_MODEL), MXU), jax.ShapeDtypeStruct((t, D_MODEL), MXU),
            jax.ShapeDtypeStruct((STAT_ROWS, D_MODEL), F32)]
    return pl.pallas_call(
        body, name="ple_loss_bwd", grid=(t // TM,),
        in_specs=[tok(D_MODEL), tok(1), tok(D_PLE), tok(D_MODEL), vec, vec, _full(w_g.shape), vec, _full(w_p.shape), vec, vec],
        out_specs=[tok(D_MODEL), tok(D_MODEL), tok(D_MODEL), _full((STAT_ROWS, D_MODEL))], out_shape=outs,
        scratch_shapes=[pltpu.VMEM((TM, D_MODEL), F32)],
        compiler_params=_cp(dimension_semantics=("arbitrary",)),
    )(xhat2, rstd2, p, target, ln2_g, ln2_b, w_g, b_g, w_p, ln3_g, ln3_b)


def _ffn_bwd(dr2, a_pre, a, b, w_down, w_a, w_b, conv_w, xhat1, rstd1, ln1_g):
    t = dr2.shape[0]
    nt = t // TM
    hb = TM // HALO
    last_h = t // HALO - 1

    def body(dr_ref, drn_ref, ap_ref, a_ref, an_ref, b_ref, bn_ref, wd_ref, wa_ref, wb_ref, cw_ref,
             xh_ref, rs_ref, g1_ref, dap_ref, dbb_ref, dr1_ref, cstat_ref, lstat_ref, acc_scr):
        i, j = pl.program_id(0), pl.program_id(1)

        @pl.when((i == 0) & (j == 0))
        def _():
            cstat_ref[...] = jnp.zeros_like(cstat_ref)
            lstat_ref[...] = jnp.zeros_like(lstat_ref)

        half = TM // ROW_GROUPS
        parts = []
        for r0 in range(0, TM, half):
            rows = pl.ds(r0, half)
            last = r0 + half == TM

            def ext(ref, nxt):
                return jnp.concatenate([ref[rows], nxt[...]], axis=0) if last else ref[r0:r0 + half + HALO]

            df = _dot_nt(ext(dr_ref, drn_ref), wd_ref[...])
            a_ext, b_ext = ext(a_ref, an_ref), ext(b_ref, bn_ref)
            cdf = 0.5 * (1.0 + lax.erf(a_ext * (1.0 / math.sqrt(2.0))))
            pdf = jnp.exp(-0.5 * a_ext * a_ext) * (1.0 / math.sqrt(2.0 * math.pi))
            da = df * b_ext * (cdf + a_ext * pdf)
            if last:
                da = jnp.concatenate([da[0:half], jnp.where(i < nt - 1, da[half:], 0.0)], axis=0)
            ahead = [da[s:s + half] for s in range(3)]
            da_pre = cw_ref[2:3, :] * ahead[0] + cw_ref[1:2, :] * ahead[1] + cw_ref[0:1, :] * ahead[2]
            dbb = df[0:half] * (a_ext[0:half] * cdf[0:half])
            dap_ref[rows, :] = da_pre.astype(MXU)
            dbb_ref[rows, :] = dbb.astype(MXU)
            for kk in range(3):
                cstat_ref[j, kk:kk + 1, :] += _colsum(ahead[2 - kk] * ap_ref[rows, :])
            cstat_ref[j, 3:4, :] += _colsum(ahead[0])
            parts.append(_dot(da_pre, wa_ref[...]) + _dot(dbb, wb_ref[...]))
        part = jnp.concatenate(parts, axis=0)

        @pl.when(j == 0)
        def _():
            acc_scr[...] = ALPHA * dr_ref[...] + part

        @pl.when(j > 0)
        def _():
            acc_scr[...] += part

        @pl.when(j == N_SHARD - 1)
        def _():
            dx1 = acc_scr[...]
            xhat1 = xh_ref[...]
            lstat_ref[0:1, :] += _colsum(dx1 * xhat1)
            lstat_ref[1:2, :] += _colsum(dx1)
            dr1_ref[...] = _ln_bwd(dx1, xhat1, rs_ref[...], g1_ref[...])

    tok = lambda w: pl.BlockSpec((TM, w), lambda i, j: (i, 0))
    tokj = pl.BlockSpec((None, TM, FF_BLK), lambda i, j: (j, i, 0))
    nextj = pl.BlockSpec((None, HALO, FF_BLK), lambda i, j: (j, jnp.minimum((i + 1) * hb, last_h), 0))
    blk = lambda r, c: pl.BlockSpec((None, r, c), lambda i, j: (j, 0, 0))
    outs = [jax.ShapeDtypeStruct((N_SHARD, t, FF_BLK), MXU)] * 2 + [
        jax.ShapeDtypeStruct((t, D_MODEL), F32), jax.ShapeDtypeStruct((N_SHARD, STAT_ROWS, FF_BLK), F32),
        jax.ShapeDtypeStruct((STAT_ROWS, D_MODEL), F32)]
    return pl.pallas_call(
        body, name="ffn_bwd", grid=(nt, N_SHARD),
        in_specs=[tok(D_MODEL), pl.BlockSpec((HALO, D_MODEL), lambda i, j: (jnp.minimum((i + 1) * hb, last_h), 0)),
                  tokj, tokj, nextj, tokj, nextj, blk(FF_BLK, D_MODEL), blk(FF_BLK, D_MODEL), blk(FF_BLK, D_MODEL),
                  blk(3, FF_BLK), tok(D_MODEL), tok(1), _full((1, D_MODEL))],
        out_specs=[tokj, tokj, tok(D_MODEL), _full((N_SHARD, STAT_ROWS, FF_BLK)), _full((STAT_ROWS, D_MODEL))], out_shape=outs,
        scratch_shapes=[pltpu.VMEM((TM, D_MODEL), F32)],
        compiler_params=_cp(dimension_semantics=("arbitrary", "arbitrary")),
    )(dr2, dr2, a_pre, a, a, b, b, w_down, w_a, w_b, conv_w, xhat1, rstd1, ln1_g)


def _mix_bwd(dr1, w_o, hu, hz, mixed, attn, ln_z_g, ln_z_b, w_s, dep):
    t = dr1.shape[0]
    nchunk = TM // BLK

    def body(dr_ref, wo_ref, hu_ref, hz_ref, mx_ref, attn_ref, g_ref, b_ref, ws_ref, grp_ref, red_ref, dep_ref,
             do1_ref, do4_ref, do16_ref, dl1_ref, dl4_ref, dl16_ref, duz_ref, dws_ref, dbs_ref, zstat_ref,
             wm_scr, dzn_scr, dbsum_scr, do_scr, dl_scr):
        @pl.when(pl.program_id(0) == 0)
        def _():
            row = lax.broadcasted_iota(jnp.int32, (BLK, BLK), 0)
            col = lax.broadcasted_iota(jnp.int32, (BLK, BLK), 1)
            for g in range(N_HEADS):
                wm_scr[g] = jnp.where(col <= row, ws_ref[g], 0.0).astype(MXU)
            dws_ref[...] = jnp.zeros_like(dws_ref)
            dbsum_scr[...] = jnp.zeros_like(dbsum_scr)
            zstat_ref[...] = jnp.zeros_like(zstat_ref)

        dcat = _dot_nt(dr_ref[...], wo_ref[...])
        dattn = dcat[:, 0:D_ATTN]
        do1_ref[...] = dattn.astype(MXU)
        for cc, val in enumerate(_chunks(dattn)):
            do_scr[cc] = val
        _to_planes(do4_ref, do_scr, DILATIONS[1], LANE_CHUNKS, MXU)
        _to_planes(do16_ref, do_scr, DILATIONS[2], LANE_CHUNKS, MXU)
        delta = _dot_select(dattn * attn_ref[...], red_ref[...])
        dl1_ref[...] = delta
        dl_scr[0] = delta
        _to_planes(dl4_ref, dl_scr, DILATIONS[1], 1, F32)
        _to_planes(dl16_ref, dl_scr, DILATIONS[2], 1, F32)
        dgm = dcat[:, D_ATTN:]
        hu, hz = hu_ref[...], hz_ref[...]
        u = _gelu(hu)
        duz_ref[:, 0:D_GMLP] = (dgm * mx_ref[...] * _gelu_grad(hu)).astype(MXU)
        dmixed = dgm * u
        dmb = dmixed.astype(MXU)
        zhat, rstd = _ln_fwd(_gelu(hz))
        znb = (zhat * g_ref[...] + b_ref[...]).astype(MXU)
        dbs_acc = jnp.zeros((BLK, D_GMLP), F32)
        for ch in range(nchunk):
            rows = slice(ch * BLK, (ch + 1) * BLK)
            dbs_acc = dbs_acc + dmixed[rows]
            for g in range(N_HEADS):
                cols = slice(g * HEAD_DIM, (g + 1) * HEAD_DIM)
                dzn_scr[rows, cols] = _dot_tn(wm_scr[g], dmb[rows, cols])
                dws_ref[g] += _dot_nt(dmb[rows, cols], znb[rows, cols])
        dbsum_scr[...] += dbs_acc
        dzn = dzn_scr[...]
        zstat_ref[0:1, :] += _colsum(dzn * zhat)
        zstat_ref[1:2, :] += _colsum(dzn)
        duz_ref[:, D_GMLP:] = (_ln_bwd(dzn, zhat, rstd, g_ref[...]) * _gelu_grad(hz)).astype(MXU)

        @pl.when(pl.program_id(0) == nt - 1)
        def _():
            row = lax.broadcasted_iota(jnp.int32, (BLK, BLK), 0)
            col = lax.broadcasted_iota(jnp.int32, (BLK, BLK), 1)
            for g in range(N_HEADS):
                dws_ref[g] = jnp.where(col <= row, dws_ref[g], 0.0)
            dbs_ref[...] = lax.dot_general(grp_ref[...], dbsum_scr[...], (((1,), (1,)), ((), ())),
                                           precision=lax.Precision.HIGHEST, preferred_element_type=F32)

    nt = t // TM
    tok = lambda w: pl.BlockSpec((TM, w), lambda i: (i, 0))
    grp = jnp.asarray((np.arange(D_GMLP)[None, :] // HEAD_DIM == np.arange(N_HEADS)[:, None]).astype(np.float32))
    red = _head_reduce()
    outs = [_perm_shape(t, d, D_ATTN, MXU) for d in DILATIONS] + [_perm_shape(t, d, 128, F32) for d in DILATIONS] + [
        jax.ShapeDtypeStruct((t, 2 * D_GMLP), MXU),
        jax.ShapeDtypeStruct((N_HEADS, BLK, BLK), F32), jax.ShapeDtypeStruct((N_HEADS, BLK), F32),
        jax.ShapeDtypeStruct((STAT_ROWS, D_GMLP), F32)]
    return pl.pallas_call(
        body, name="mix_bwd", grid=(t // TM,),
        in_specs=[tok(D_MODEL), _full(w_o.shape), tok(D_GMLP), tok(D_GMLP), tok(D_GMLP), tok(D_ATTN), _full(ln_z_g.shape),
                  _full(ln_z_b.shape), _full(w_s.shape), _full(grp.shape), _full(red.shape), pl.BlockSpec(memory_space=pl.ANY)],
        out_specs=[_perm_tile(d, D_ATTN) for d in DILATIONS] + [_perm_tile(d, 128) for d in DILATIONS]
        + [tok(2 * D_GMLP), _full((N_HEADS, BLK, BLK)), _full((N_HEADS, BLK)), _full((STAT_ROWS, D_GMLP))],
        out_shape=outs,
        scratch_shapes=[pltpu.VMEM((N_HEADS, BLK, BLK), MXU), pltpu.VMEM((TM, D_GMLP), F32), pltpu.VMEM((BLK, D_GMLP), F32),
                        pltpu.VMEM((LANE_CHUNKS, TM, 128), F32), pltpu.VMEM((1, TM, 128), F32)],
        compiler_params=_cp(dimension_semantics=("arbitrary",)),
    )(dr1, w_o, hu, hz, mixed, attn, ln_z_g, ln_z_b, w_s, grp, red, dep)


def _dx_in(dqs, dks, dvs, duz, dr1, w_in, c_tab, s1_tab, s2_tab):
    t = dr1.shape[0]

    def body(dq1, dq4, dq16, dk1, dk4, dk16, dv1, dv4, dv16, duz_ref, dr_ref, w_ref, c_ref, s1_ref, s2_ref,
             dh_ref, dx_ref, acc_scr):
        sums = []
        for part, (g1, g4, g16) in enumerate(((dq1, dq4, dq16), (dk1, dk4, dk16), (dv1, dv4, dv16))):
            acc = acc_scr.at[pl.ds(part * LANE_CHUNKS, LANE_CHUNKS)]
            for cc in range(LANE_CHUNKS):
                acc[cc] = g1[:, cc * 128:(cc + 1) * 128]
            _from_planes(g4, acc, DILATIONS[1], LANE_CHUNKS, accumulate=True)
            _from_planes(g16, acc, DILATIONS[2], LANE_CHUNKS, accumulate=True)
            sums.append(_unchunk(acc_scr, LANE_CHUNKS, part * LANE_CHUNKS))
        c, s1, s2 = _tile_heads(c_ref[...]), _tile_heads(s1_ref[...]), _tile_heads(s2_ref[...])
        dh_ref[:, 0:D_ATTN] = _rope_apply_t(sums[0] * (1.0 / math.sqrt(HEAD_DIM)), c, s1, s2).astype(MXU)
        dh_ref[:, D_ATTN:2 * D_ATTN] = _rope_apply_t(sums[1], c, s1, s2).astype(MXU)
        dh_ref[:, 2 * D_ATTN:3 * D_ATTN] = sums[2].astype(MXU)
        dh_ref[:, 3 * D_ATTN:] = duz_ref[...]
        dx = ALPHA * dr_ref[...]
        for j in range(N_SHARD):
            dx = dx + _dot_nt(dh_ref[:, j * W_IN_BLK:(j + 1) * W_IN_BLK], w_ref[j])
        dx_ref[...] = dx

    tok = lambda w: pl.BlockSpec((TM, w), lambda i: (i, 0))
    outs = [jax.ShapeDtypeStruct((t, D_IN), MXU), jax.ShapeDtypeStruct((t, D_MODEL), F32)]
    return pl.pallas_call(
        body, name="dx_in", grid=(t // TM,),
        in_specs=[_perm_tile(d, D_ATTN) for d in DILATIONS] * 3
        + [tok(2 * D_GMLP), tok(D_MODEL), _full(w_in.shape), tok(128), tok(128), tok(128)],
        out_specs=[tok(D_IN), tok(D_MODEL)], out_shape=outs,
        scratch_shapes=[pltpu.VMEM((3 * LANE_CHUNKS, TM, 128), F32)],
        compiler_params=_cp(dimension_semantics=("arbitrary",)),
    )(*dqs, *dks, *dvs, duz, dr1, w_in, c_tab, s1_tab, s2_tab)


def _wgrad(name, x, dy, x_spec, dy_spec, out_spec, out_shape, grid):
    def body(x_ref, dy_ref, o_ref):
        o_ref[...] = _dot_tn(x_ref[...], dy_ref[...])

    return pl.pallas_call(
        body, name=name, grid=grid, in_specs=[x_spec, dy_spec], out_specs=out_spec,
        out_shape=jax.ShapeDtypeStruct(out_shape, F32),
        compiler_params=_cp(dimension_semantics=("arbitrary",) * len(grid)),
    )(x, dy)


def _wgrad_pair(name, xa, xb, dy, x_spec, dy_spec, out_spec, out_shape, grid):
    def body(xa_ref, xb_ref, dy_ref, oa_ref, ob_ref):
        dy = dy_ref[...]
        oa_ref[...] = _dot_tn(xa_ref[...], dy)
        ob_ref[...] = _dot_tn(xb_ref[...], dy)

    return pl.pallas_call(
        body, name=name, grid=grid, in_specs=[x_spec, x_spec, dy_spec], out_specs=[out_spec, out_spec],
        out_shape=[jax.ShapeDtypeStruct(out_shape, F32)] * 2,
        compiler_params=_cp(dimension_semantics=("arbitrary",) * len(grid)),
    )(xa, xb, dy)


def _local_step(x, p, rope, target, w_in, start_dep, late_landed, late_weights, early_grads, early_grads_sent,
                ln_z_g, ln_z_b, w_s, b_s, ln1_g, ln1_b, conv_b, ln2_g, ln2_b, b_g, ln3_g, ln3_b):
    t = x.shape[0]
    half = TM
    c_tab, s1_tab, s2_tab = rope
    b_full = jnp.repeat(jnp.transpose(b_s[0]), HEAD_DIM, axis=1)
    conv_b4 = conv_b.reshape(N_SHARD, 1, FF_BLK)
    *qkvs, hu, hz, mixed, gm = _qkvuz(x, w_in, c_tab, s1_tab, s2_tab, ln_z_g, ln_z_b, w_s[0], b_full, start_dep)
    branches = [_attn_fwd(qkv, d, start_dep) for qkv, d in zip(qkvs[:2], DILATIONS[:2])]
    dep = late_landed(branches[-1][1])
    branches.append(_attn_fwd(qkvs[2], DILATIONS[2], dep))
    w_o, w_a, w_b, conv_w, w_down, w_g, w_p = late_weights(branches[-1][1])
    attn, *lses, cat, xhat1, rstd1, x1b = _mix_ln1(
        [o for o, _ in branches], [l for _, l in branches], gm, x, w_o, ln1_g, ln1_b)
    a_pre, a_act, b_act, f = _ffn_in(x1b, w_a, w_b, conv_w, conv_b4)
    xhat2, rstd2, x2b = _ffn_out_ln2(f, w_down, xhat1, ln1_g, ln1_b, ln2_g, ln2_b)
    dr2, dgp, dpp, stat3 = _ple_loss_bwd(xhat2, rstd2, p, target, ln2_g, ln2_b, w_g, b_g, w_p, ln3_g, ln3_b)
    da_pre, dbb, dr1, cstat, stat1 = _ffn_bwd(dr2, a_pre, a_act, b_act, w_down, w_a, w_b, conv_w, xhat1, rstd1, ln1_g)

    full_t = lambda w, im: pl.BlockSpec((t, w), im)
    ffj = pl.BlockSpec((None, t, FF_BLK), lambda j, kk: (j, 0, 0))
    early = dict(
        w_ple_gate=_wgrad("dw_g", x2b, dgp, full_t(half, lambda kk, n: (0, kk)), full_t(half, lambda kk, n: (0, n)),
                          pl.BlockSpec((half, half), lambda kk, n: (kk, n)), (D_MODEL, D_MODEL), (2, 2)),
        w_ple_in=_wgrad("dw_p", p, dpp, full_t(D_PLE, lambda j: (0, 0)), full_t(ROW_BLK, lambda j: (0, j)),
                        pl.BlockSpec((None, D_PLE, ROW_BLK), lambda j: (j, 0, 0)), (N_SHARD, D_PLE, ROW_BLK), (N_SHARD,)),
        w_ff_down=_wgrad("dw_down", f, dr2, ffj, full_t(half, lambda j, n: (0, n)),
                         pl.BlockSpec((None, FF_BLK, half), lambda j, n: (j, 0, n)), (N_SHARD, FF_BLK, D_MODEL), (N_SHARD, 2)),
        **dict(zip(("w_ff_a", "w_ff_b"), _wgrad_pair(
            "dw_ab", da_pre, dbb, x1b, ffj, full_t(half, lambda j, n: (0, n)),
            pl.BlockSpec((None, FF_BLK, half), lambda j, n: (j, 0, n)), (N_SHARD, FF_BLK, D_MODEL), (N_SHARD, 2)))),
        w_o=_wgrad("dw_o", cat, dr1, full_t(half, lambda kk, n: (0, kk)), full_t(half, lambda kk, n: (0, n)),
                   pl.BlockSpec((half, half), lambda kk, n: (kk, n)), (D_MODEL, D_MODEL), (2, 2)))
    dep = early_grads(early)

    do1, do4, do16, dl1, dl4, dl16, duz, dws, dbs, zstat = _mix_bwd(
        dr1, w_o, hu, hz, mixed, attn, ln_z_g, ln_z_b, w_s[0], dep)
    dep = early_grads_sent(duz, (stat3, stat1, zstat, cstat, dws, dbs))
    dqkv = [_attn_bwd(qkv, do, lse, dl, d, dep)
            for qkv, do, lse, dl, d in zip(qkvs, (do1, do4, do16), lses, (dl1, dl4, dl16), DILATIONS)]
    dh, grad_x = _dx_in([g[0] for g in dqkv], [g[1] for g in dqkv], [g[2] for g in dqkv], duz, dr1, w_in,
                        c_tab, s1_tab, s2_tab)
    g_w_in = _wgrad("dw_in", x, dh, full_t(half, lambda j, kk: (0, kk)), full_t(W_IN_BLK, lambda j, kk: (0, j)),
                    pl.BlockSpec((None, half, W_IN_BLK), lambda j, kk: (j, kk, 0)), (N_SHARD, D_MODEL, W_IN_BLK), (N_SHARD, 2))
    return grad_x, g_w_in


def _tile_rows(rows, mult, steps):
    if rows % mult:
        return rows
    return next(rows // k for k in range(steps, rows + 1) if rows % k == 0 and (rows // k) % mult == 0)


def _grid_spec(grid, in_specs, out_specs):
    return pltpu.PrefetchScalarGridSpec(num_scalar_prefetch=1, grid=grid, in_specs=in_specs, out_specs=out_specs)


def _on_own_steps(i, count, steps, work):
    if count == steps:
        work()
    else:
        pl.when(i < count)(work)


def _place_shards(name, ws, dtypes, place, dep):
    n = len(ws)
    tiles = [_tile_rows(w.shape[0], 16, 8) for w in ws]
    counts = [w.shape[0] // t for w, t in zip(ws, tiles)]
    steps = max(counts)

    def body(s_ref, *refs):
        i = pl.program_id(0)
        for a in range(n):
            def work(a=a):
                refs[n + 1 + a][...] = refs[a][...].astype(dtypes[a])
            _on_own_steps(i, counts[a], steps, work)

    def tile(a, lead):
        last = counts[a] - 1
        if lead:
            return pl.BlockSpec((None, tiles[a], ws[a].shape[1]), lambda i, s: (s[0], jnp.minimum(i, last), 0))
        return pl.BlockSpec((tiles[a], ws[a].shape[1]), lambda i, s: (jnp.minimum(i, last), 0))

    return pl.pallas_call(
        body, name=name,
        grid_spec=_grid_spec((steps,), [tile(a, False) for a in range(n)] + [pl.BlockSpec(memory_space=pl.ANY)],
                             [tile(a, True) for a in range(n)]),
        out_shape=[jax.ShapeDtypeStruct((N_SHARD, *w.shape), dt) for w, dt in zip(ws, dtypes)],
        compiler_params=_cp())(place, *ws, dep)


def _pair_sums(name, mines, gots, place):
    n = len(mines)
    tiles = [_tile_rows(g.shape[1], 16, 2) for g in gots]
    per_blk = [g.shape[1] // t for g, t in zip(gots, tiles)]
    counts = [N_SHARD * nh for nh in per_blk]
    steps = max(counts)

    def body(s_ref, *refs):
        i = pl.program_id(0)
        for a in range(n):
            def work(a=a):
                refs[2 * n + a][...] = (refs[a][...] + refs[n + a][...]).astype(BF16)
            _on_own_steps(i, counts[a], steps, work)

    def tile(a, mine):
        nh, last = per_blk[a], counts[a] - 1

        def index(i, s):
            g = jnp.minimum(i, last)
            return (g // nh, (s[1] * nh if mine else 0) + g % nh, 0)

        return pl.BlockSpec((None, tiles[a], gots[a].shape[2]), index)

    return pl.pallas_call(
        body, name=name,
        grid_spec=_grid_spec((steps,), [tile(a, True) for a in range(n)] + [tile(a, False) for a in range(n)],
                             [tile(a, False) for a in range(n)]),
        out_shape=[jax.ShapeDtypeStruct(g.shape, BF16) for g in gots], compiler_params=_cp())(place, *mines, *gots)


def _chip_sums(name, owns, landeds, place, dep):
    n = len(owns)
    tiles = [_tile_rows(o.shape[1], 16, 8) for o in owns]
    counts = [o.shape[1] // t for o, t in zip(owns, tiles)]
    steps = max(counts)

    def body(s_ref, *refs):
        i = pl.program_id(0)
        for a in range(n):
            def work(a=a):
                own, l1, l2, l3 = (refs[4 * a + k][...].astype(F32) for k in range(4))
                refs[4 * n + 1 + a][...] = ((own + l1) + l2) + l3
            _on_own_steps(i, counts[a], steps, work)

    def slot(a, d):
        last = counts[a] - 1
        return pl.BlockSpec((None, tiles[a], owns[a].shape[2]), lambda i, s: ((s[0] + d) % N_SHARD, jnp.minimum(i, last), 0))

    def out(a):
        nh, last = counts[a], counts[a] - 1
        return pl.BlockSpec((tiles[a], owns[a].shape[2]), lambda i, s: (s[1] * nh + jnp.minimum(i, last), 0))

    operands = [x for o, l in zip(owns, landeds) for x in (o, l, l, l)]
    return pl.pallas_call(
        body, name=name,
        grid_spec=_grid_spec((steps,), [slot(a, d) for a in range(n) for d in range(4)] + [pl.BlockSpec(memory_space=pl.ANY)],
                             [out(a) for a in range(n)]),
        out_shape=[jax.ShapeDtypeStruct((2 * o.shape[1], o.shape[2]), F32) for o in owns],
        compiler_params=_cp())(place, *operands, dep)


def _adamw_math(w, g, m, v):
    m = ADAM_B1 * m + (1.0 - ADAM_B1) * g
    v = ADAM_B2 * v + (1.0 - ADAM_B2) * (g * g)
    m_hat = m / (1.0 - ADAM_B1 ** ADAM_STEP)
    v_hat = v / (1.0 - ADAM_B2 ** ADAM_STEP)
    delta = -ADAM_LR * (m_hat / (jnp.sqrt(v_hat) + ADAM_EPS) + ADAM_WD * w)
    return delta, m, v


def _adamw_shards(name, ws, gs, ms, vs):
    n = len(ws)
    tiles = [_tile_rows(w.shape[1], 8, 8) for w in ws]
    counts = [w.shape[1] // t for w, t in zip(ws, tiles)]
    steps = max(counts)

    def body(*refs):
        i = pl.program_id(0)
        for a in range(n):
            def work(a=a):
                w_ref, g_ref, m_ref, v_ref = refs[4 * a:4 * a + 4]
                d_ref, nm_ref, nv_ref = refs[4 * n + 3 * a:4 * n + 3 * a + 3]
                d_ref[...], nm_ref[...], nv_ref[...] = _adamw_math(w_ref[...], g_ref[...], m_ref[...], v_ref[...])
            _on_own_steps(i, counts[a], steps, work)

    def tile(a, lead):
        last, c = counts[a] - 1, ws[a].shape[2]
        if lead:
            return pl.BlockSpec((None, tiles[a], c), lambda i: (0, jnp.minimum(i, last), 0))
        return pl.BlockSpec((tiles[a], c), lambda i: (jnp.minimum(i, last), 0))

    res = pl.pallas_call(
        body, name=name, grid=(steps,),
        in_specs=[tile(a, lead) for a in range(n) for lead in (True, False, True, True)],
        out_specs=[tile(a, True) for a in range(n) for _ in range(3)],
        out_shape=[jax.ShapeDtypeStruct(w.shape, F32) for w in ws for _ in range(3)],
        compiler_params=_cp())(*[x for quad in zip(ws, gs, ms, vs) for x in quad])
    return [tuple(res[3 * a:3 * a + 3]) for a in range(n)]


MESH = pl.DeviceIdType.MESH
ANY = pl.BlockSpec(memory_space=pl.ANY)


def _place():
    x, y, c = lax.axis_index("x"), lax.axis_index("y"), lax.axis_index("c")
    chips = [(1 - x, y), (x, 1 - y), (1 - x, 1 - y)]
    return x, y, c, 2 * x + y, chips


def _remote(src, dst, send_sem, recv_sem, dev):
    return pltpu.make_async_remote_copy(src_ref=src, dst_ref=dst, send_sem=send_sem, recv_sem=recv_sem,
                                        device_id=dev, device_id_type=MESH)


def _half(ref, hc, rows):
    return ref.at[pl.ds(hc * (rows // 2), rows // 2)]


def _sibling_join(blocks, tag):
    n = len(blocks)

    def body(*refs):
        outs = refs[n:2 * n]
        send, recv = refs[2 * n:]
        x, y, c, _, _ = _place()
        cps = []
        for a in range(n):
            h = blocks[a].shape[0] // 2
            mine = outs[a].at[pl.ds(c * h, h)]
            cp = _remote(mine, mine, send.at[a], recv.at[a], (x, y, 1 - c))
            cp.start()
            cps.append(cp)
        for a, cp in enumerate(cps):
            h = blocks[a].shape[0] // 2
            theirs = outs[a].at[pl.ds((1 - c) * h, h)]
            _remote(theirs, theirs, send.at[a], recv.at[a], (x, y, 1 - c)).wait_recv()
            cp.wait_send()

    sem = pltpu.SemaphoreType.DMA
    return pl.pallas_call(body, name=f"rs_sibling_join_{tag}", in_specs=[ANY] * n, out_specs=[ANY] * n,
                          out_shape=[jax.ShapeDtypeStruct(b_.shape, b_.dtype) for b_ in blocks],
                          input_output_aliases={a: a for a in range(n)},
                          scratch_shapes=[sem((n,)), sem((n,))])(*blocks)


HBM = pl.BlockSpec(memory_space=pltpu.HBM)
SEM = pl.BlockSpec(memory_space=pltpu.SEMAPHORE)
TOKEN = jax.ShapeDtypeStruct((8, 128), F32)


def _in_flight_params():
    return pltpu.CompilerParams(has_side_effects=pltpu.SideEffectType.DATAFLOW_SIDE_EFFECTING)


def _in_hbm(a):
    return pltpu.with_memory_space_constraint(a, pltpu.HBM)


def _gather_piece(ref, rows, split, slot, hc):
    return _half(ref.at[slot], hc, rows) if split else ref.at[slot]


def _gather_start(stacks, split, after, tag):
    n = len(stacks)

    def body(*refs):
        ins = refs[:n]
        send, recv = refs[n + 1], refs[n + 2]
        token = refs[2 * n + 3]
        _, _, c, j, chips = _place()
        for a in range(n):
            mine = _gather_piece(ins[a], stacks[a].shape[1], split[a], j, c)
            for t in range(3):
                _remote(mine, mine, send.at[3 * a + t], recv.at[3 * a + t], (*chips[t], c)).start()
        token[...] = jnp.zeros_like(token)

    sems = pltpu.SemaphoreType.DMA((3 * n,))
    res = pl.pallas_call(
        body, name=f"gather_start_{tag}", in_specs=[HBM] * n + [ANY],
        out_specs=[SEM, SEM] + [HBM] * n + [pl.BlockSpec(memory_space=pltpu.VMEM)],
        out_shape=[sems, sems] + [pltpu.HBM(s.shape, s.dtype) for s in stacks] + [TOKEN],
        input_output_aliases={a: a + 2 for a in range(n)}, compiler_params=_in_flight_params(),
    )(*[_in_hbm(s) for s in stacks], after)
    return res[0], res[1], res[2:2 + n], res[2 + n]


def _gather_wait(send, recv, stacks, split, after, tag):
    n = len(stacks)

    def body(*refs):
        ins = refs[:n]
        send_ref, recv_ref = refs[n], refs[n + 1]
        _, _, c, j, chips = _place()
        for a in range(n):
            rows = stacks[a].shape[1]
            mine = _gather_piece(ins[a], rows, split[a], j, c)
            for t, (px, py) in enumerate(chips):
                theirs = _gather_piece(ins[a], rows, split[a], 2 * px + py, c)
                _remote(mine, mine, send_ref.at[3 * a + t], recv_ref.at[3 * a + t], (px, py, c)).wait_send()
                _remote(theirs, theirs, send_ref.at[3 * a + t], recv_ref.at[3 * a + t], (px, py, c)).wait_recv()

    return pl.pallas_call(
        body, name=f"gather_wait_{tag}", in_specs=[HBM] * n + [SEM, SEM, ANY], out_specs=[HBM] * n,
        out_shape=[pltpu.HBM(s.shape, s.dtype) for s in stacks],
        input_output_aliases={a: a for a in range(n)}, compiler_params=_in_flight_params(),
    )(*stacks, send, recv, after)


def _gather_forward(stacks, split, tag):
    idx = [a for a in range(len(stacks)) if split[a]]
    n = len(idx)

    def body(*refs):
        outs = refs[n:2 * n]
        send, recv = refs[2 * n:]
        x, y, c, _, chips = _place()
        sends = []
        for t, (px, py) in enumerate(chips):
            for a in range(n):
                blk = _half(outs[a].at[2 * px + py], c, stacks[idx[a]].shape[1])
                cp = _remote(blk, blk, send.at[a, t], recv.at[a, t], (x, y, 1 - c))
                cp.start()
                sends.append(cp)
        for t, (px, py) in enumerate(chips):
            for a in range(n):
                blk = _half(outs[a].at[2 * px + py], 1 - c, stacks[idx[a]].shape[1])
                _remote(blk, blk, send.at[a, t], recv.at[a, t], (x, y, 1 - c)).wait_recv()
        for cp in sends:
            cp.wait_send()

    sem = pltpu.SemaphoreType.DMA
    res = pl.pallas_call(
        body, name=f"gather_forward_{tag}", in_specs=[ANY] * n, out_specs=[ANY] * n,
        out_shape=[jax.ShapeDtypeStruct(stacks[a].shape, stacks[a].dtype) for a in idx],
        input_output_aliases={a: a for a in range(n)}, scratch_shapes=[sem((n, 3)), sem((n, 3))],
    )(*[stacks[a] for a in idx])
    out = list(stacks)
    for a, r in zip(idx, res):
        out[a] = r
    return out


def _forward_start(stacks, after, tag):
    n = len(stacks)

    def body(*refs):
        ins = refs[:n]
        send, recv = refs[n + 1], refs[n + 2]
        token = refs[2 * n + 3]
        x, y, c, _, chips = _place()
        for a in range(n):
            for t, (px, py) in enumerate(chips):
                blk = _half(ins[a].at[2 * px + py], c, stacks[a].shape[1])
                _remote(blk, blk, send.at[3 * a + t], recv.at[3 * a + t], (x, y, 1 - c)).start()
        token[...] = jnp.zeros_like(token)

    sems = pltpu.SemaphoreType.DMA((3 * n,))
    res = pl.pallas_call(
        body, name=f"forward_start_{tag}", in_specs=[HBM] * n + [ANY],
        out_specs=[SEM, SEM] + [HBM] * n + [pl.BlockSpec(memory_space=pltpu.VMEM)],
        out_shape=[sems, sems] + [pltpu.HBM(s.shape, s.dtype) for s in stacks] + [TOKEN],
        input_output_aliases={a: a + 2 for a in range(n)}, compiler_params=_in_flight_params(),
    )(*[_in_hbm(s) for s in stacks], after)
    return res[0], res[1], res[2:2 + n], res[2 + n]


def _forward_wait(send, recv, stacks, after, tag):
    n = len(stacks)

    def body(*refs):
        ins = refs[:n]
        send_ref, recv_ref = refs[n], refs[n + 1]
        x, y, c, _, chips = _place()
        for a in range(n):
            for t, (px, py) in enumerate(chips):
                mine = _half(ins[a].at[2 * px + py], c, stacks[a].shape[1])
                theirs = _half(ins[a].at[2 * px + py], 1 - c, stacks[a].shape[1])
                _remote(mine, mine, send_ref.at[3 * a + t], recv_ref.at[3 * a + t], (x, y, 1 - c)).wait_send()
                _remote(theirs, theirs, send_ref.at[3 * a + t], recv_ref.at[3 * a + t], (x, y, 1 - c)).wait_recv()

    return pl.pallas_call(
        body, name=f"forward_wait_{tag}", in_specs=[HBM] * n + [SEM, SEM, ANY], out_specs=[HBM] * n,
        out_shape=[pltpu.HBM(s.shape, s.dtype) for s in stacks],
        input_output_aliases={a: a for a in range(n)}, compiler_params=_in_flight_params(),
    )(*stacks, send, recv, after)


def _swap_start(grads, tag):
    n = len(grads)

    def body(*refs):
        ins, gots = refs[:n], refs[n:2 * n]
        send, recv = refs[2 * n], refs[2 * n + 1]
        token = refs[4 * n + 2]
        x, y, c, _, _ = _place()
        for a in range(n):
            h = grads[a].shape[1] // 2
            _remote(ins[a].at[:, pl.ds((1 - c) * h, h)], gots[a], send.at[a], recv.at[a], (x, y, 1 - c)).start()
        token[...] = jnp.zeros_like(token)

    sems = pltpu.SemaphoreType.DMA((n,))
    halves = [(g.shape[0], g.shape[1] // 2, g.shape[2]) for g in grads]
    res = pl.pallas_call(
        body, name=f"swap_start_{tag}", in_specs=[HBM] * (2 * n),
        out_specs=[SEM, SEM] + [HBM] * (2 * n) + [pl.BlockSpec(memory_space=pltpu.VMEM)],
        out_shape=[sems, sems] + [pltpu.HBM(g.shape, g.dtype) for g in grads] + [pltpu.HBM(s, F32) for s in halves] + [TOKEN],
        input_output_aliases={a: a + 2 for a in range(2 * n)}, compiler_params=_in_flight_params(),
    )(*[_in_hbm(g) for g in grads], *[_in_hbm(lax.empty(s, F32)) for s in halves])
    return res[0], res[1], res[2:2 + n], res[2 + n:2 + 2 * n], res[2 + 2 * n]


def _swap_wait(send, recv, grads, gots, after, tag):
    n = len(grads)

    def body(*refs):
        ins, lnd = refs[:n], refs[n:2 * n]
        send_ref, recv_ref = refs[2 * n], refs[2 * n + 1]
        x, y, c, _, _ = _place()
        for a in range(n):
            h = grads[a].shape[1] // 2
            cp = _remote(ins[a].at[:, pl.ds((1 - c) * h, h)], lnd[a], send_ref.at[a], recv_ref.at[a], (x, y, 1 - c))
            cp.wait_send()
            cp.wait_recv()

    bufs = [pltpu.HBM(g.shape, g.dtype) for g in grads] + [pltpu.HBM(g.shape, g.dtype) for g in gots]
    res = pl.pallas_call(
        body, name=f"swap_wait_{tag}", in_specs=[HBM] * (2 * n) + [SEM, SEM, ANY], out_specs=[HBM] * (2 * n),
        out_shape=bufs, input_output_aliases={a: a for a in range(2 * n)}, compiler_params=_in_flight_params(),
    )(*grads, *gots, send, recv, after)
    return res[:n], res[n:]


def _exchange_start(parts, tag):
    n = len(parts)

    def body(*refs):
        ins, lands = refs[:n], refs[n:2 * n]
        send, recv = refs[2 * n], refs[2 * n + 1]
        token = refs[4 * n + 2]
        _, _, c, j, chips = _place()
        for t, (px, py) in enumerate(chips):
            for a in range(n):
                _remote(ins[a].at[2 * px + py], lands[a].at[j], send.at[3 * a + t], recv.at[3 * a + t], (px, py, c)).start()
        token[...] = jnp.zeros_like(token)

    sems = pltpu.SemaphoreType.DMA((3 * n,))
    bufs = [pltpu.HBM(p.shape, p.dtype) for p in parts]
    res = pl.pallas_call(
        body, name=f"exchange_start_{tag}", in_specs=[HBM] * (2 * n),
        out_specs=[SEM, SEM] + [HBM] * (2 * n) + [pl.BlockSpec(memory_space=pltpu.VMEM)],
        out_shape=[sems, sems] + bufs + bufs + [TOKEN],
        input_output_aliases={a: a + 2 for a in range(2 * n)}, compiler_params=_in_flight_params(),
    )(*[_in_hbm(p) for p in parts], *[_in_hbm(lax.empty(p.shape, p.dtype)) for p in parts])
    return res[0], res[1], res[2:2 + n], res[2 + n:2 + 2 * n], res[2 + 2 * n]


def _exchange_wait(send, recv, parts, lands, after, tag):
    n = len(parts)

    def body(*refs):
        ins, lnd = refs[:n], refs[n:2 * n]
        send_ref, recv_ref = refs[2 * n], refs[2 * n + 1]
        _, _, c, j, chips = _place()
        for t, (px, py) in enumerate(chips):
            jt = 2 * px + py
            for a in range(n):
                _remote(ins[a].at[jt], lnd[a].at[j], send_ref.at[3 * a + t], recv_ref.at[3 * a + t], (px, py, c)).wait_send()
                _remote(ins[a].at[jt], lnd[a].at[jt], send_ref.at[3 * a + t], recv_ref.at[3 * a + t], (px, py, c)).wait_recv()

    bufs = [pltpu.HBM(p.shape, p.dtype) for p in parts]
    res = pl.pallas_call(
        body, name=f"exchange_wait_{tag}", in_specs=[HBM] * (2 * n) + [SEM, SEM, ANY], out_specs=[HBM] * (2 * n),
        out_shape=bufs + bufs, input_output_aliases={a: a for a in range(2 * n)}, compiler_params=_in_flight_params(),
    )(*parts, *lands, send, recv, after)
    return res[:n], res[n:]


def _small_chip_sums(arrs):
    n = len(arrs)

    def body(*refs):
        ins, outs = refs[:n], refs[n:2 * n]
        sib = refs[2 * n:3 * n]
        send, recv = refs[3 * n:]
        x, y, c, j, _ = _place()
        swaps = [_remote(ins[a], sib[a], send.at[a], recv.at[a], (x, y, 1 - c)) for a in range(n)]
        for cp in swaps:
            cp.start()
        for a in range(n):
            swaps[a].wait_recv()
            outs[a][j] = ins[a][...] + sib[a][...]
        for cp in swaps:
            cp.wait_send()

    sem = pltpu.SemaphoreType.DMA
    vm = pl.BlockSpec(memory_space=pltpu.VMEM)
    return pl.pallas_call(
        body, name="small_chip_sums", in_specs=[vm] * n, out_specs=[vm] * n,
        out_shape=[jax.ShapeDtypeStruct((N_SHARD, *a.shape), F32) for a in arrs],
        scratch_shapes=[pltpu.VMEM(a.shape, F32) for a in arrs] + [sem((n,)), sem((n,))],
        compiler_params=_cp(),
    )(*arrs)


def _small_totals(stacks):
    n = len(stacks)

    def body(*refs):
        for a in range(n):
            refs[n + a][...] = ((refs[a][0] + refs[a][1]) + refs[a][2]) + refs[a][3]

    return pl.pallas_call(body, name="small_totals", out_shape=[jax.ShapeDtypeStruct(s.shape[1:], F32) for s in stacks],
                          compiler_params=_cp())(*stacks)


SMALL_1024 = ("ln1_g", "ln1_b", "ln2_g", "ln2_b", "b_ple_gate", "ln3_g", "ln3_b")


def _adamw_small(red3, red1, redz, g_conv_w, redc, red_ws, red_bs, params):
    shape2d = {"ln_z_g": (1, D_GMLP), "ln_z_b": (1, D_GMLP), "w_s": (N_HEADS * BLK, BLK), "b_s": (N_HEADS, BLK),
               "conv_w": (3, FF_BLK), "conv_b": (N_SHARD, FF_BLK), **{k: (1, D_MODEL) for k in SMALL_1024}}
    names = list(shape2d)
    flat = [a.reshape(shape2d[k]) for k in names for a in params[k]]

    def body(r3, r1, rz, gcw, rc, rws, rbs, *refs):
        ins, outs = refs[:3 * len(names)], refs[3 * len(names):]

        def grad_of(k):
            if k == "w_s":
                return rws[...]
            if k == "b_s":
                return rbs[...]
            if k == "conv_w":
                return gcw[0:3, :]
            if k == "conv_b":
                return jnp.concatenate([rc[j * STAT_ROWS + 3:j * STAT_ROWS + 4, :] for j in range(N_SHARD)], axis=0)
            src, row = {"ln3_g": (r3, 0), "ln3_b": (r3, 1), "b_ple_gate": (r3, 2), "ln2_g": (r3, 3), "ln2_b": (r3, 4),
                        "ln1_g": (r1, 0), "ln1_b": (r1, 1), "ln_z_g": (rz, 0), "ln_z_b": (rz, 1)}[k]
            return src[row:row + 1, :]

        for i, k in enumerate(names):
            w_ref, m_ref, v_ref = ins[3 * i:3 * i + 3]
            g_ref, d_ref, nm_ref, nv_ref = outs[4 * i:4 * i + 4]
            g = grad_of(k)
            g_ref[...] = g
            d_ref[...], nm_ref[...], nv_ref[...] = _adamw_math(w_ref[...], g, m_ref[...], v_ref[...])

    res = pl.pallas_call(
        body, name="adamw_small",
        out_shape=[jax.ShapeDtypeStruct(shape2d[k], F32) for k in names for _ in range(4)],
        compiler_params=_cp(),
    )(red3, red1, redz, g_conv_w, redc, red_ws, red_bs, *flat)
    return {k: tuple(r.reshape(params[k][0].shape) for r in res[4 * i:4 * i + 4]) for i, k in enumerate(names)}


WEIGHTS = ("w_in", "ln_z_g", "ln_z_b", "w_s", "b_s", "w_o", "ln1_g", "ln1_b", "w_ff_a", "w_ff_b", "conv_w", "conv_b",
           "w_ff_down", "ln2_g", "ln2_b", "w_ple_gate", "b_ple_gate", "w_ple_in", "ln3_g", "ln3_b")
BIG = ("w_in", "w_o", "w_ff_a", "w_ff_b", "w_ff_down", "w_ple_gate", "w_ple_in")
TRANSPOSED = ("w_ff_a", "w_ff_b")
LATE = ("w_o", "w_ff_a", "w_ff_b", "w_ff_down", "w_ple_gate", "w_ple_in", "conv_w")


def kernel(x, p, positions, w_in, ln_z_g, ln_z_b, w_s, b_s, w_o, ln1_g, ln1_b, w_ff_a, w_ff_b, conv_w, conv_b, w_ff_down, ln2_g, ln2_b, w_ple_gate, b_ple_gate, w_ple_in, ln3_g, ln3_b, loss_target, m_w_in, m_ln_z_g, m_ln_z_b, m_w_s, m_b_s, m_w_o, m_ln1_g, m_ln1_b, m_w_ff_a, m_w_ff_b, m_conv_w, m_conv_b, m_w_ff_down, m_ln2_g, m_ln2_b, m_w_ple_gate, m_b_ple_gate, m_w_ple_in, m_ln3_g, m_ln3_b, v_w_in, v_ln_z_g, v_ln_z_b, v_w_s, v_b_s, v_w_o, v_ln1_g, v_ln1_b, v_w_ff_a, v_w_ff_b, v_conv_w, v_conv_b, v_w_ff_down, v_ln2_g, v_ln2_b, v_w_ple_gate, v_b_ple_gate, v_w_ple_in, v_ln3_g, v_ln3_b):
    args = locals()
    w = {k: args[k] for k in WEIGHTS}
    m = {k: args["m_" + k] for k in WEIGHTS}
    v = {k: args["v_" + k] for k in WEIGHTS}

    for k in TRANSPOSED:
        w[k], m[k], v[k] = (jnp.swapaxes(a, 1, 2) for a in (w[k], m[k], v[k]))

    chip = 2 * lax.axis_index("x") + lax.axis_index("y")
    place = jnp.stack([chip, lax.axis_index("c")]).astype(jnp.int32)
    stack = dict(zip(["w_in"], _place_shards("cast_w_in", [w["w_in"][0]], [MXU], place, place)))
    i_send, i_recv, in_flight, dep = _gather_start([stack["w_in"]], [True], place, "w_in")
    stack.update(zip(LATE, _place_shards("cast_late", [w[k][0] for k in LATE],
                                         [F32 if k == "conv_w" else MXU for k in LATE], place, dep)))
    rope = _rope_tables(positions, x.shape[1], stack[LATE[-1]])
    landed_in = _gather_wait(i_send, i_recv, in_flight, [True], rope[0], "w_in")
    w_in_full, = _gather_forward(landed_in, [True], "w_in")
    split_late = [k != "conv_w" for k in LATE]
    g_send, g_recv, late_flight, start_dep = _gather_start([stack[k] for k in LATE], split_late, w_in_full, "late")
    halves = [k for k, sp in zip(LATE, split_late) if sp]
    trips = {}

    def late_landed(after):
        fw = dict(zip(LATE, _gather_wait(g_send, g_recv, late_flight, split_late, after, "late")))
        trips["late"] = (fw, *_forward_start([fw[k] for k in halves], fw["conv_w"], "late"))
        return trips["late"][-1]

    def late_weights(after):
        fw, send, recv, flight, _ = trips["late"]
        fw.update(zip(halves, _forward_wait(send, recv, flight, after, "late")))
        return (fw["w_o"].reshape(D_MODEL, D_MODEL), fw["w_ff_a"], fw["w_ff_b"], fw["conv_w"], fw["w_ff_down"],
                fw["w_ple_gate"].reshape(D_MODEL, D_MODEL), fw["w_ple_in"])

    def swap_started(names, grads, tag):
        stacked = [g.reshape(N_SHARD, *w[k].shape[1:]) for k, g in zip(names, grads)]
        return (names, tag, *_swap_start(stacked, tag))

    def partial_sums(swap, after):
        names, tag, send, recv, stacked, gots, _ = swap
        stacked, got = _swap_wait(send, recv, stacked, gots, after, tag)
        pair = _pair_sums(f"rs_pair_{tag}", stacked, got, place)
        return (names, tag, *_exchange_start(pair, tag))

    def reduced(trip, after, dep):
        names, tag, send, recv, pair, lands, _ = trip
        pair, landed = _exchange_wait(send, recv, pair, lands, after, tag)
        blocks = _chip_sums(f"rs_sum_{tag}", pair, landed, place, dep)
        return dict(zip(names, _sibling_join(blocks, tag)))

    def early_grads(grads):
        trips["swap"] = swap_started(list(grads), list(grads.values()), "early")
        return trips["swap"][-1]

    def early_grads_sent(after, small):
        trips["early"] = partial_sums(trips["swap"], after)
        stat3, stat1, zstat, cstat, dws, dbs = small
        sums = _small_chip_sums([stat3, stat1, zstat, cstat.reshape(N_SHARD * STAT_ROWS, FF_BLK),
                                 dws.reshape(N_HEADS * BLK, BLK), dbs])
        trips["small"] = _gather_start(sums, [False] * len(sums), trips["early"][-1], "small")
        return trips["small"][-1]

    grad_x, g_w_in = _local_step(
        x[0], p[0, 0], rope, loss_target[0], w_in_full, start_dep, late_landed, late_weights, early_grads, early_grads_sent,
        ln_z_g, ln_z_b, w_s, b_s, ln1_g, ln1_b, conv_b, ln2_g, ln2_b, b_ple_gate, ln3_g, ln3_b)

    trips["w_in"] = partial_sums(swap_started(["w_in"], [g_w_in], "w_in"), g_w_in)
    out = {}

    def adamw(red, tag):
        names = list(red)
        steps = _adamw_shards(f"adamw_{tag}", [w[k] for k in names], [red[k] for k in names], [m[k] for k in names],
                              [v[k] for k in names])
        for k, (d, nm, nv) in zip(names, steps):
            out[k] = (red[k].reshape(w[k].shape), d, nm, nv)

    adamw(reduced(trips["early"], grad_x, trips["w_in"][-1]), "early")
    adamw(reduced(trips["w_in"], out["w_o"][3], start_dep), "w_in")
    for k in TRANSPOSED:
        out[k] = tuple(jnp.swapaxes(a, 1, 2) for a in out[k])

    s_send, s_recv, s_flight, _ = trips["small"]
    red3, red1, redz, redc, red_ws, red_bs = _small_totals(
        _gather_wait(s_send, s_recv, s_flight, [False] * len(s_flight), out["w_in"][3], "small"))
    loss = (0.5 / D_MODEL) * jnp.sum(red3[5])
    g_conv_w = lax.dynamic_slice_in_dim(redc, chip * STAT_ROWS, STAT_ROWS, 0)
    names_small = [k for k in WEIGHTS if k not in BIG]
    out.update(_adamw_small(red3, red1, redz, g_conv_w, redc, red_ws, red_bs, {k: (w[k], m[k], v[k]) for k in names_small}))

    return (loss, grad_x[None], *[out[k][0] for k in WEIGHTS], *[out[k][1] for k in WEIGHTS],
            *[out[k][2] for k in WEIGHTS], *[out[k][3] for k in WEIGHTS])
```

```python
import functools
import math

import numpy as np
import jax
import jax.numpy as jnp
from jax import lax
from jax.experimental import pallas as pl
from jax.experimental.pallas import tpu as pltpu

F32 = jnp.float32
BF16 = jnp.bfloat16
MXU = BF16

D_MODEL = 1024
HEAD_DIM = 64
N_HEADS = 8
D_ATTN = 512
D_GMLP = 512
D_IN = 2560
DILATIONS = (1, 4, 16)
BLK = 128
ROPE_THETA = 500000.0
ROPE_DIM = 16
D_FF = 2816
D_PLE = 256
LN_EPS = 1e-5
ALPHA = 2.0 ** 0.25
NEG_INF = -1e30
N_SHARD = 4
W_IN_BLK = D_IN // N_SHARD
FF_BLK = D_FF // N_SHARD
ROW_BLK = D_MODEL // N_SHARD
ADAM_LR, ADAM_B1, ADAM_B2, ADAM_EPS, ADAM_WD, ADAM_STEP = 0.001, 0.9, 0.999, 1e-08, 0.01, 10

TM = 512
HALO = 8
ROW_GROUPS = 2
VMEM_LIMIT = 56 * 1024 * 1024


def _cp(**kw):
    return pltpu.CompilerParams(vmem_limit_bytes=VMEM_LIMIT, **kw)


def _full(shape):
    n = len(shape)
    return pl.BlockSpec(shape, lambda *_: (0,) * n)


def _gelu(x):
    return 0.5 * x * (1.0 + lax.erf(x * (1.0 / math.sqrt(2.0))))


def _gelu_grad(x):
    return 0.5 * (1.0 + lax.erf(x * (1.0 / math.sqrt(2.0)))) + x * jnp.exp(-0.5 * x * x) * (1.0 / math.sqrt(2.0 * math.pi))


def _ln_fwd(r):
    mu = jnp.mean(r, axis=-1, keepdims=True)
    xc = r - mu
    var = jnp.mean(xc * xc, axis=-1, keepdims=True)
    rstd = lax.rsqrt(var + LN_EPS)
    return xc * rstd, rstd


def _ln_bwd(dy, xhat, rstd, g):
    dxh = dy * g
    m1 = jnp.mean(dxh, axis=-1, keepdims=True)
    m2 = jnp.mean(dxh * xhat, axis=-1, keepdims=True)
    return rstd * (dxh - m1 - xhat * m2)


def _dot(a, b):
    return jnp.dot(a.astype(MXU), b.astype(MXU), preferred_element_type=F32)


def _dot_nt(a, b):
    return lax.dot_general(a.astype(MXU), b.astype(MXU), (((1,), (1,)), ((), ())), preferred_element_type=F32)


def _dot_tn(a, b):
    return lax.dot_general(a.astype(MXU), b.astype(MXU), (((0,), (0,)), ((), ())), preferred_element_type=F32)


def _colsum(v):
    return jnp.sum(v, axis=0, keepdims=True)


def _rope_tables(positions, t, dep):
    inv = np.float32(ROPE_THETA) ** (-np.arange(0, ROPE_DIM, 2, dtype=np.float32) / np.float32(ROPE_DIM))
    half = ROPE_DIM // 2
    pos_rep = jnp.repeat(positions.reshape(t // 16, 16), half, axis=1)
    inv_row = jnp.asarray(np.tile(inv, 16)[None, :], F32)

    def trig_body(pos_ref, inv_ref, dep_ref, cos_ref, sin_ref):
        ang = pos_ref[...].astype(F32) * inv_ref[...]
        cos_ref[...] = jnp.cos(ang)
        sin_ref[...] = jnp.sin(ang)

    vm = pl.BlockSpec(memory_space=pltpu.VMEM)
    cos8, sin8 = pl.pallas_call(
        trig_body, name="rope_trig", in_specs=[vm, vm, pl.BlockSpec(memory_space=pl.ANY)], out_specs=[vm, vm],
        out_shape=(jax.ShapeDtypeStruct((t // 16, 128), F32), jax.ShapeDtypeStruct((t // 16, 128), F32)),
    )(pos_rep, inv_row, dep)
    cos8 = cos8.reshape(t, half)
    sin8 = sin8.reshape(t, half)

    lane = np.arange(128) % HEAD_DIM
    sel = (np.arange(half)[:, None] == (lane % half)[None, :])
    e_cos = (sel & (lane < ROPE_DIM)[None, :]).astype(np.float32)
    e_s1 = -(sel & (lane < half)[None, :]).astype(np.float32)
    e_s2 = (sel & ((lane >= half) & (lane < ROPE_DIM))[None, :]).astype(np.float32)
    ones = (lane >= ROPE_DIM).astype(np.float32)[None, :]

    def expand_body(cos_ref, sin_ref, ec_ref, e1_ref, e2_ref, ones_ref, c_ref, s1_ref, s2_ref):
        hp = lax.Precision.HIGHEST
        c_ref[...] = jnp.dot(cos_ref[...], ec_ref[...], precision=hp, preferred_element_type=F32) + ones_ref[...]
        s1_ref[...] = jnp.dot(sin_ref[...], e1_ref[...], precision=hp, preferred_element_type=F32)
        s2_ref[...] = jnp.dot(sin_ref[...], e2_ref[...], precision=hp, preferred_element_type=F32)

    tab = jax.ShapeDtypeStruct((t, 128), F32)
    return pl.pallas_call(expand_body, name="rope_expand", out_shape=(tab, tab, tab), compiler_params=_cp())(
        cos8, sin8, jnp.asarray(e_cos), jnp.asarray(e_s1), jnp.asarray(e_s2), jnp.asarray(ones))


def _tile_heads(tab):
    return jnp.concatenate([tab] * (D_ATTN // 128), axis=1)


def _rope_apply(v, c, s1, s2):
    n = v.shape[1]
    half = ROPE_DIM // 2
    return v * c + pltpu.roll(v, n - half, 1) * s1 + pltpu.roll(v, half, 1) * s2


def _rope_apply_t(g, c, s1, s2):
    n = g.shape[1]
    half = ROPE_DIM // 2
    return g * c + pltpu.roll(g * s1, half, 1) + pltpu.roll(g * s2, n - half, 1)


LANE_CHUNKS = D_ATTN // 128
HEAD_LANES = 128 // N_HEADS


def _perm_shape(t, d, w, dtype):
    return jax.ShapeDtypeStruct((d, t // d, w), dtype)


def _perm_tile(d, w):
    return pl.BlockSpec((None if d == 1 else d, TM // d, w), lambda i: (0, i, 0))


def _to_planes(ref, scr, d, n_chunks, dtype):
    for r in range(d):
        for cc in range(n_chunks):
            ref[r, :, cc * 128:(cc + 1) * 128] = scr.at[cc][pl.ds(r, TM // d, stride=d), :].astype(dtype)


def _from_planes(ref, scr, d, n_chunks, accumulate=False):
    for r in range(d):
        for cc in range(n_chunks):
            rows = scr.at[cc]
            val = ref[r, :, cc * 128:(cc + 1) * 128].astype(F32)
            if accumulate:
                rows[pl.ds(r, TM // d, stride=d), :] += val
            else:
                rows[pl.ds(r, TM // d, stride=d), :] = val


def _chunks(val):
    return [val[:, cc * 128:(cc + 1) * 128] for cc in range(val.shape[1] // 128)]


def _unchunk(scr, n_chunks, base=0):
    return jnp.concatenate([scr[base + cc] for cc in range(n_chunks)], axis=1)


def _head_expand():
    src = np.arange(128)[:, None]
    dst = np.arange(D_ATTN)[None, :]
    return jnp.asarray((src == (dst // HEAD_DIM) * HEAD_LANES).astype(np.float32))


def _head_reduce():
    src = np.arange(D_ATTN)[:, None]
    dst = np.arange(128)[None, :]
    return jnp.asarray((src // HEAD_DIM == dst // HEAD_LANES).astype(np.float32))


def _dot_select(a, sel):
    hi = a.astype(BF16)
    lo = (a - hi.astype(F32)).astype(BF16)
    sel = sel.astype(BF16)
    return jnp.dot(hi, sel, preferred_element_type=F32) + jnp.dot(lo, sel, preferred_element_type=F32)


def _qkvuz(x, w_in, c_tab, s1_tab, s2_tab, ln_z_g, ln_z_b, w_s, b_full, dep):
    t = x.shape[0]
    nchunk = TM // BLK

    def body(x_ref, w_ref, c_ref, s1_ref, s2_ref, g_ref, b_ref, ws_ref, bf_ref, dep_ref,
             qkv1_ref, qkv4_ref, qkv16_ref, hu_ref, hz_ref, mixed_ref, gm_ref, h_scr, wm_scr, p_scr):
        @pl.when(pl.program_id(0) == 0)
        def _():
            row = lax.broadcasted_iota(jnp.int32, (BLK, BLK), 0)
            col = lax.broadcasted_iota(jnp.int32, (BLK, BLK), 1)
            for g in range(N_HEADS):
                wm_scr[g] = jnp.where(col <= row, ws_ref[g], 0.0).astype(MXU)

        xb = x_ref[...].astype(MXU)
        for j in range(N_SHARD):
            h_scr[:, j * W_IN_BLK:(j + 1) * W_IN_BLK] = jnp.dot(xb, w_ref[j], preferred_element_type=F32)
        c, s1, s2 = _tile_heads(c_ref[...]), _tile_heads(s1_ref[...]), _tile_heads(s2_ref[...])
        q = _rope_apply(h_scr[:, 0:D_ATTN], c, s1, s2) * (1.0 / math.sqrt(HEAD_DIM))
        k = _rope_apply(h_scr[:, D_ATTN:2 * D_ATTN], c, s1, s2)
        for part, val in enumerate((q, k, h_scr[:, 2 * D_ATTN:3 * D_ATTN])):
            qkv1_ref[:, part * D_ATTN:(part + 1) * D_ATTN] = val.astype(MXU)
            for cc in range(LANE_CHUNKS):
                p_scr[part * LANE_CHUNKS + cc] = val[:, cc * 128:(cc + 1) * 128]
        _to_planes(qkv4_ref, p_scr, DILATIONS[1], 3 * LANE_CHUNKS, MXU)
        _to_planes(qkv16_ref, p_scr, DILATIONS[2], 3 * LANE_CHUNKS, MXU)
        hu = h_scr[:, 3 * D_ATTN:3 * D_ATTN + D_GMLP]
        hz = h_scr[:, 3 * D_ATTN + D_GMLP:]
        hu_ref[...] = hu
        hz_ref[...] = hz
        zhat, _ = _ln_fwd(_gelu(hz))
        zn = (zhat * g_ref[...] + b_ref[...]).astype(MXU)
        for ch in range(nchunk):
            rows = slice(ch * BLK, (ch + 1) * BLK)
            for g in range(N_HEADS):
                cols = slice(g * HEAD_DIM, (g + 1) * HEAD_DIM)
                mixed_ref[rows, cols] = jnp.dot(wm_scr[g], zn[rows, cols], preferred_element_type=F32) + bf_ref[:, cols]
        gm_ref[...] = (_gelu(hu) * mixed_ref[...]).astype(MXU)

    tok = lambda w: pl.BlockSpec((TM, w), lambda i: (i, 0))
    outs = [_perm_shape(t, d, 3 * D_ATTN, MXU) for d in DILATIONS] + [jax.ShapeDtypeStruct((t, D_GMLP), F32)] * 3 + [
        jax.ShapeDtypeStruct((t, D_GMLP), MXU)]
    return pl.pallas_call(
        body, name="qkvuz", grid=(t // TM,),
        in_specs=[tok(D_MODEL), _full(w_in.shape), tok(128), tok(128), tok(128), _full(ln_z_g.shape), _full(ln_z_b.shape),
                  _full(w_s.shape), _full(b_full.shape), pl.BlockSpec(memory_space=pl.ANY)],
        out_specs=[_perm_tile(d, 3 * D_ATTN) for d in DILATIONS] + [tok(D_ATTN)] * 4, out_shape=outs,
        scratch_shapes=[pltpu.VMEM((TM, D_IN), F32), pltpu.VMEM((N_HEADS, BLK, BLK), MXU),
                        pltpu.VMEM((3 * LANE_CHUNKS, TM, 128), F32)],
        compiler_params=_cp(dimension_semantics=("arbitrary",)),
    )(x, w_in, c_tab, s1_tab, s2_tab, ln_z_g, ln_z_b, w_s, b_full, dep)


def _band_valid(n):
    i = lax.broadcasted_iota(jnp.int32, (BLK, 2 * BLK), 0)
    j = lax.broadcasted_iota(jnp.int32, (BLK, 2 * BLK), 1)
    return (j >= i) & (j <= i + BLK) & ((j >= BLK) | (n > 0))


def _attn_fwd(qkv, d, dep):
    _, l_sub, _ = qkv.shape
    nb = l_sub // BLK

    def body(q_ref, kp_ref, kc_ref, vp_ref, vc_ref, dep_ref, o_ref, l_ref):
        valid = _band_valid(pl.program_id(1))
        kcat = jnp.concatenate([kp_ref[...], kc_ref[...]], axis=0)
        vcat = jnp.concatenate([vp_ref[...], vc_ref[...]], axis=0)
        for h in range(N_HEADS):
            cols = slice(h * HEAD_DIM, (h + 1) * HEAD_DIM)
            s = jnp.where(valid, _dot_nt(q_ref[:, cols], kcat[:, cols]), NEG_INF)
            m = jnp.max(s, axis=-1, keepdims=True)
            e = jnp.exp(s - m)
            den = jnp.sum(e, axis=-1, keepdims=True)
            o_ref[:, cols] = _dot(e, vcat[:, cols]) * (1.0 / den)
            l_ref[:, h * HEAD_LANES:(h + 1) * HEAD_LANES] = jnp.broadcast_to(m + jnp.log(den), (BLK, HEAD_LANES))

    def blk(w, col, prev=False):
        return pl.BlockSpec((None, BLK, w), lambda r, n: (r, jnp.maximum(n - 1, 0) if prev else n, col))

    return pl.pallas_call(
        body, name=f"attn_fwd_d{d}", grid=(d, nb),
        in_specs=[blk(D_ATTN, 0), blk(D_ATTN, 1, True), blk(D_ATTN, 1), blk(D_ATTN, 2, True), blk(D_ATTN, 2),
                  pl.BlockSpec(memory_space=pl.ANY)],
        out_specs=[blk(D_ATTN, 0), blk(128, 0)],
        out_shape=[jax.ShapeDtypeStruct((d, l_sub, D_ATTN), F32), jax.ShapeDtypeStruct((d, l_sub, 128), F32)],
        compiler_params=_cp(dimension_semantics=("arbitrary", "arbitrary")),
    )(qkv, qkv, qkv, qkv, qkv, dep)


def _attn_bwd(qkv, do, lse, delta, d, dep):
    _, l_sub, _ = qkv.shape
    nb = l_sub // BLK
    whole = l_sub <= 8 * BLK

    def shares(n, q_ref, kp_ref, kc_ref, vp_ref, vc_ref, do_ref, l_ref, dl_ref, dq_ref):
        valid = _band_valid(n)
        kcat = jnp.concatenate([kp_ref[...], kc_ref[...]], axis=0)
        vcat = jnp.concatenate([vp_ref[...], vc_ref[...]], axis=0)
        for h in range(N_HEADS):
            cols = slice(h * HEAD_DIM, (h + 1) * HEAD_DIM)
            stat = slice(h * HEAD_LANES, h * HEAD_LANES + 1)
            qh, doh = q_ref[:, cols], do_ref[:, cols]
            p = jnp.where(valid, jnp.exp(_dot_nt(qh, kcat[:, cols]) - l_ref[:, stat]), 0.0)
            ds = p * (_dot_nt(doh, vcat[:, cols]) - dl_ref[:, stat])
            dq_ref[:, cols] = _dot(ds, kcat[:, cols])
            yield cols, _dot_tn(ds, qh), _dot_tn(p, doh)

    def body_whole(*refs):
        dk_ref, dv_ref = refs[10:]
        n = pl.program_id(1)
        cur = pl.ds(pl.multiple_of(n * BLK, BLK), BLK)
        prev = pl.ds(pl.multiple_of(jnp.maximum(n - 1, 0) * BLK, BLK), BLK)
        for cols, dk2, dv2 in shares(n, *refs[:8], refs[9]):
            dk_ref[cur, cols] = dk2[BLK:]
            dv_ref[cur, cols] = dv2[BLK:]
            dk_ref[prev, cols] += dk2[0:BLK]
            dv_ref[prev, cols] += dv2[0:BLK]

    def body_carry(*refs):
        dk_ref, dv_ref, ck_scr, cv_scr = refs[10:]
        n = pl.program_id(1)

        @pl.when(n == 0)
        def _():
            ck_scr[...] = jnp.zeros_like(ck_scr)
            cv_scr[...] = jnp.zeros_like(cv_scr)

        @pl.when(n < nb)
        def _():
            for cols, dk2, dv2 in shares(n, *refs[:8], refs[9]):
                dk_ref[:, cols] = ck_scr[:, cols] + dk2[0:BLK]
                dv_ref[:, cols] = cv_scr[:, cols] + dv2[0:BLK]
                ck_scr[:, cols] = dk2[BLK:]
                cv_scr[:, cols] = dv2[BLK:]

        @pl.when(n == nb)
        def _():
            dk_ref[...] = ck_scr[...]
            dv_ref[...] = cv_scr[...]

    def blk(w, col, shift=0):
        return pl.BlockSpec((None, BLK, w), lambda r, n: (r, jnp.clip(n - shift, 0, nb - 1), col))

    if whole:
        dkv_spec = pl.BlockSpec((None, l_sub, D_ATTN), lambda r, n: (r, 0, 0))
        body, steps, scratch = body_whole, nb, []
    else:
        dkv_spec = blk(D_ATTN, 0, 1)
        body, steps, scratch = body_carry, nb + 1, [pltpu.VMEM((BLK, D_ATTN), F32)] * 2
    return pl.pallas_call(
        body, name=f"attn_bwd_d{d}", grid=(d, steps),
        in_specs=[blk(D_ATTN, 0), blk(D_ATTN, 1, 1), blk(D_ATTN, 1), blk(D_ATTN, 2, 1), blk(D_ATTN, 2),
                  blk(D_ATTN, 0), blk(128, 0), blk(128, 0), pl.BlockSpec(memory_space=pl.ANY)],
        out_specs=[blk(D_ATTN, 0), dkv_spec, dkv_spec],
        out_shape=[jax.ShapeDtypeStruct((d, l_sub, D_ATTN), F32)] * 3,
        scratch_shapes=scratch,
        compiler_params=_cp(dimension_semantics=("arbitrary", "arbitrary")),
    )(qkv, qkv, qkv, qkv, qkv, do, lse, delta, dep)


def _mix_ln1(os_, ls_, gm, x, w_o, ln1_g, ln1_b, dep):
    t = x.shape[0]
    expand = _head_expand()

    def body(o1, o4, o16, l1, l4, l16, gm_ref, x_ref, wo_ref, g_ref, b_ref, ex_ref, dep_ref,
             attn_ref, lse1_ref, lse4_ref, lse16_ref, cat_ref, xhat_ref, rstd_ref, x1b_ref, o_scr, l_scr):
        _from_planes(o4, o_scr, DILATIONS[1], LANE_CHUNKS)
        _from_planes(o16, o_scr.at[pl.ds(LANE_CHUNKS, LANE_CHUNKS)], DILATIONS[2], LANE_CHUNKS)
        _from_planes(l4, l_scr, DILATIONS[1], 1)
        _from_planes(l16, l_scr.at[pl.ds(1, 1)], DILATIONS[2], 1)
        la, lb, lc = l1[...], l_scr[0], l_scr[1]
        m = jnp.maximum(jnp.maximum(la, lb), lc)
        ea, eb, ec = jnp.exp(la - m), jnp.exp(lb - m), jnp.exp(lc - m)
        den = ea + eb + ec
        inv = 1.0 / den
        wide = lambda w: _dot_select(w, ex_ref[...])
        attn = (wide(ea * inv) * o1[...] + wide(eb * inv) * _unchunk(o_scr, LANE_CHUNKS)
                + wide(ec * inv) * _unchunk(o_scr, LANE_CHUNKS, LANE_CHUNKS))
        attn_ref[...] = attn
        lse = m + jnp.log(den)
        lse1_ref[...] = lse
        l_scr[2] = lse
        _to_planes(lse4_ref, l_scr.at[pl.ds(2, 1)], DILATIONS[1], 1, F32)
        _to_planes(lse16_ref, l_scr.at[pl.ds(2, 1)], DILATIONS[2], 1, F32)
        cat_ref[:, 0:D_ATTN] = attn.astype(MXU)
        cat_ref[:, D_ATTN:] = gm_ref[...]
        mix = jnp.dot(cat_ref[...], wo_ref[...], preferred_element_type=F32)
        xhat, rstd = _ln_fwd(ALPHA * x_ref[...] + mix)
        xhat_ref[...] = xhat
        rstd_ref[...] = rstd
        x1b_ref[...] = (xhat * g_ref[...] + b_ref[...]).astype(MXU)

    tok = lambda w: pl.BlockSpec((TM, w), lambda i: (i, 0))
    outs = [jax.ShapeDtypeStruct((t, D_ATTN), F32)] + [_perm_shape(t, d, 128, F32) for d in DILATIONS] + [
        jax.ShapeDtypeStruct((t, D_MODEL), MXU), jax.ShapeDtypeStruct((t, D_MODEL), F32), jax.ShapeDtypeStruct((t, 1), F32),
        jax.ShapeDtypeStruct((t, D_MODEL), MXU)]
    return pl.pallas_call(
        body, name="mix_ln1", grid=(t // TM,),
        in_specs=[_perm_tile(d, D_ATTN) for d in DILATIONS] + [_perm_tile(d, 128) for d in DILATIONS]
        + [tok(D_GMLP), tok(D_MODEL), _full(w_o.shape), _full(ln1_g.shape), _full(ln1_b.shape), _full(expand.shape),
           pl.BlockSpec(memory_space=pl.ANY)],
        out_specs=[tok(D_ATTN)] + [_perm_tile(d, 128) for d in DILATIONS] + [tok(D_MODEL), tok(D_MODEL), tok(1), tok(D_MODEL)],
        out_shape=outs,
        scratch_shapes=[pltpu.VMEM((2 * LANE_CHUNKS, TM, 128), F32), pltpu.VMEM((3, TM, 128), F32)],
        compiler_params=_cp(dimension_semantics=("arbitrary",)),
    )(*os_, *ls_, gm, x, w_o, ln1_g, ln1_b, expand, dep)


def _conv_fwd(a_ext, w_ref, b_ref, rows):
    return (b_ref[...] + w_ref[2:3, :] * a_ext[HALO:HALO + rows] + w_ref[1:2, :] * a_ext[HALO - 1:HALO - 1 + rows]
            + w_ref[0:1, :] * a_ext[HALO - 2:HALO - 2 + rows])


def _ffn_in(x1b, w_a, w_b, conv_w, conv_b):
    t = x1b.shape[0]
    hb = TM // HALO

    def body(x_ref, xh_ref, wa_ref, wb_ref, cw_ref, cb_ref, apre_ref, a_ref, b_ref, f_ref):
        i = pl.program_id(1)
        a_pre = _dot_nt(x_ref[...], wa_ref[...])
        a_halo = jnp.where(i > 0, _dot_nt(xh_ref[...], wa_ref[...]), 0.0)
        a = _conv_fwd(jnp.concatenate([a_halo, a_pre], axis=0), cw_ref, cb_ref, TM)
        b = _dot_nt(x_ref[...], wb_ref[...])
        apre_ref[...] = a_pre
        a_ref[...] = a
        b_ref[...] = b
        f_ref[...] = (_gelu(a) * b).astype(MXU)

    blk = lambda r, c: pl.BlockSpec((None, r, c), lambda j, i: (j, 0, 0))
    tokj = pl.BlockSpec((None, TM, FF_BLK), lambda j, i: (j, i, 0))
    outs = [jax.ShapeDtypeStruct((N_SHARD, t, FF_BLK), F32)] * 3 + [jax.ShapeDtypeStruct((N_SHARD, t, FF_BLK), MXU)]
    return pl.pallas_call(
        body, name="ffn_in", grid=(N_SHARD, t // TM),
        in_specs=[pl.BlockSpec((TM, D_MODEL), lambda j, i: (i, 0)),
                  pl.BlockSpec((HALO, D_MODEL), lambda j, i: (jnp.maximum(i * hb - 1, 0), 0)),
                  blk(FF_BLK, D_MODEL), blk(FF_BLK, D_MODEL), blk(3, FF_BLK), blk(1, FF_BLK)],
        out_specs=[tokj, tokj, tokj, tokj], out_shape=outs,
        compiler_params=_cp(dimension_semantics=("arbitrary", "arbitrary")),
    )(x1b, x1b, w_a, w_b, conv_w, conv_b)


def _ffn_out_ln2(f, w_down, xhat1, ln1_g, ln1_b, ln2_g, ln2_b):
    t = xhat1.shape[0]

    def body(f_ref, wd_ref, xh_ref, g1_ref, b1_ref, g2_ref, b2_ref, xhat_ref, rstd_ref, x2b_ref):
        ff = jnp.dot(f_ref[0], wd_ref[0], preferred_element_type=F32)
        for j in range(1, N_SHARD):
            ff = ff + jnp.dot(f_ref[j], wd_ref[j], preferred_element_type=F32)
        x1 = xh_ref[...] * g1_ref[...] + b1_ref[...]
        xhat, rstd = _ln_fwd(ALPHA * x1 + ff)
        xhat_ref[...] = xhat
        rstd_ref[...] = rstd
        x2b_ref[...] = (xhat * g2_ref[...] + b2_ref[...]).astype(MXU)

    tok = lambda w: pl.BlockSpec((TM, w), lambda i: (i, 0))
    vec = _full((1, D_MODEL))
    outs = [jax.ShapeDtypeStruct((t, D_MODEL), F32), jax.ShapeDtypeStruct((t, 1), F32), jax.ShapeDtypeStruct((t, D_MODEL), MXU)]
    return pl.pallas_call(
        body, name="ffn_out_ln2", grid=(t // TM,),
        in_specs=[pl.BlockSpec((N_SHARD, TM, FF_BLK), lambda i: (0, i, 0)), _full(w_down.shape), tok(D_MODEL), vec, vec, vec, vec],
        out_specs=[tok(D_MODEL), tok(1), tok(D_MODEL)], out_shape=outs,
        compiler_params=_cp(dimension_semantics=("arbitrary",)),
    )(f, w_down, xhat1, ln1_g, ln1_b, ln2_g, ln2_b)


STAT_ROWS = 8


def _ple_loss_bwd(xhat2, rstd2, p, target, ln2_g, ln2_b, w_g, b_g, w_p, ln3_g, ln3_b):
    t = xhat2.shape[0]

    def body(xh2_ref, rs2_ref, p_ref, t_ref, g2_ref, b2_ref, wg_ref, bg_ref, wp_ref, g3_ref, b3_ref,
             dr2_ref, dgp_ref, dpp_ref, stat_ref, pp_scr):
        @pl.when(pl.program_id(0) == 0)
        def _():
            stat_ref[...] = jnp.zeros_like(stat_ref)

        xhat2 = xh2_ref[...]
        x2 = xhat2 * g2_ref[...] + b2_ref[...]
        gate = jax.nn.sigmoid(jnp.dot(x2.astype(MXU), wg_ref[...], preferred_element_type=F32) + bg_ref[...])
        pb = p_ref[...].astype(MXU)
        for j in range(N_SHARD):
            pp_scr[:, j * ROW_BLK:(j + 1) * ROW_BLK] = jnp.dot(pb, wp_ref[j], preferred_element_type=F32)
        pp = pp_scr[...]
        xhat3, rstd3 = _ln_fwd(ALPHA * x2 + gate * pp)
        err = xhat3 * g3_ref[...] + b3_ref[...] - t_ref[...]
        dy = err * (1.0 / D_MODEL)
        dr3 = _ln_bwd(dy, xhat3, rstd3, g3_ref[...])
        dgp = dr3 * pp * gate * (1.0 - gate)
        dgp_ref[...] = dgp.astype(MXU)
        dpp_ref[...] = (dr3 * gate).astype(MXU)
        dx2 = ALPHA * dr3 + _dot_nt(dgp, wg_ref[...])
        dr2_ref[...] = _ln_bwd(dx2, xhat2, rs2_ref[...], g2_ref[...])
        stat_ref[0:1, :] += _colsum(dy * xhat3)
        stat_ref[1:2, :] += _colsum(dy)
        stat_ref[2:3, :] += _colsum(dgp)
        stat_ref[3:4, :] += _colsum(dx2 * xhat2)
        stat_ref[4:5, :] += _colsum(dx2)
        stat_ref[5:6, :] += _colsum(err * err)

    tok = lambda w: pl.BlockSpec((TM, w), lambda i: (i, 0))
    vec = _full((1, D_MODEL))
    outs = [jax.ShapeDtypeStruct((t, D_MODEL), F32), jax.ShapeDtypeStruct((t, D_MODEL), MXU), jax.ShapeDtypeStruct((t, D_MODEL), MXU),
            jax.ShapeDtypeStruct((STAT_ROWS, D_MODEL), F32)]
    return pl.pallas_call(
        body, name="ple_loss_bwd", grid=(t // TM,),
        in_specs=[tok(D_MODEL), tok(1), tok(D_PLE), tok(D_MODEL), vec, vec, _full(w_g.shape), vec, _full(w_p.shape), vec, vec],
        out_specs=[tok(D_MODEL), tok(D_MODEL), tok(D_MODEL), _full((STAT_ROWS, D_MODEL))], out_shape=outs,
        scratch_shapes=[pltpu.VMEM((TM, D_MODEL), F32)],
        compiler_params=_cp(dimension_semantics=("arbitrary",)),
    )(xhat2, rstd2, p, target, ln2_g, ln2_b, w_g, b_g, w_p, ln3_g, ln3_b)


def _ffn_bwd(dr2, a_pre, a, b, w_down, w_a, w_b, conv_w, xhat1, rstd1, ln1_g):
    t = dr2.shape[0]
    nt = t // TM
    hb = TM // HALO
    last_h = t // HALO - 1

    def body(dr_ref, drn_ref, ap_ref, a_ref, an_ref, b_ref, bn_ref, wd_ref, wa_ref, wb_ref, cw_ref,
             xh_ref, rs_ref, g1_ref, dap_ref, dbb_ref, dr1_ref, cstat_ref, lstat_ref, acc_scr):
        i, j = pl.program_id(0), pl.program_id(1)

        @pl.when((i == 0) & (j == 0))
        def _():
            cstat_ref[...] = jnp.zeros_like(cstat_ref)
            lstat_ref[...] = jnp.zeros_like(lstat_ref)

        half = TM // ROW_GROUPS
        parts = []
        for r0 in range(0, TM, half):
            rows = pl.ds(r0, half)
            last = r0 + half == TM

            def ext(ref, nxt):
                return jnp.concatenate([ref[rows], nxt[...]], axis=0) if last else ref[r0:r0 + half + HALO]

            df = _dot_nt(ext(dr_ref, drn_ref), wd_ref[...])
            a_ext, b_ext = ext(a_ref, an_ref), ext(b_ref, bn_ref)
            cdf = 0.5 * (1.0 + lax.erf(a_ext * (1.0 / math.sqrt(2.0))))
            pdf = jnp.exp(-0.5 * a_ext * a_ext) * (1.0 / math.sqrt(2.0 * math.pi))
            da = df * b_ext * (cdf + a_ext * pdf)
            if last:
                da = jnp.concatenate([da[0:half], jnp.where(i < nt - 1, da[half:], 0.0)], axis=0)
            ahead = [da[s:s + half] for s in range(3)]
            da_pre = cw_ref[2:3, :] * ahead[0] + cw_ref[1:2, :] * ahead[1] + cw_ref[0:1, :] * ahead[2]
            dbb = df[0:half] * (a_ext[0:half] * cdf[0:half])
            dap_ref[rows, :] = da_pre.astype(MXU)
            dbb_ref[rows, :] = dbb.astype(MXU)
            for kk in range(3):
                cstat_ref[j, kk:kk + 1, :] += _colsum(ahead[2 - kk] * ap_ref[rows, :])
            cstat_ref[j, 3:4, :] += _colsum(ahead[0])
            parts.append(_dot(da_pre, wa_ref[...]) + _dot(dbb, wb_ref[...]))
        part = jnp.concatenate(parts, axis=0)

        @pl.when(j == 0)
        def _():
            acc_scr[...] = ALPHA * dr_ref[...] + part

        @pl.when(j > 0)
        def _():
            acc_scr[...] += part

        @pl.when(j == N_SHARD - 1)
        def _():
            dx1 = acc_scr[...]
            xhat1 = xh_ref[...]
            lstat_ref[0:1, :] += _colsum(dx1 * xhat1)
            lstat_ref[1:2, :] += _colsum(dx1)
            dr1_ref[...] = _ln_bwd(dx1, xhat1, rs_ref[...], g1_ref[...])

    tok = lambda w: pl.BlockSpec((TM, w), lambda i, j: (i, 0))
    tokj = pl.BlockSpec((None, TM, FF_BLK), lambda i, j: (j, i, 0))
    nextj = pl.BlockSpec((None, HALO, FF_BLK), lambda i, j: (j, jnp.minimum((i + 1) * hb, last_h), 0))
    blk = lambda r, c: pl.BlockSpec((None, r, c), lambda i, j: (j, 0, 0))
    outs = [jax.ShapeDtypeStruct((N_SHARD, t, FF_BLK), MXU)] * 2 + [
        jax.ShapeDtypeStruct((t, D_MODEL), F32), jax.ShapeDtypeStruct((N_SHARD, STAT_ROWS, FF_BLK), F32),
        jax.ShapeDtypeStruct((STAT_ROWS, D_MODEL), F32)]
    return pl.pallas_call(
        body, name="ffn_bwd", grid=(nt, N_SHARD),
        in_specs=[tok(D_MODEL), pl.BlockSpec((HALO, D_MODEL), lambda i, j: (jnp.minimum((i + 1) * hb, last_h), 0)),
                  tokj, tokj, nextj, tokj, nextj, blk(FF_BLK, D_MODEL), blk(FF_BLK, D_MODEL), blk(FF_BLK, D_MODEL),
                  blk(3, FF_BLK), tok(D_MODEL), tok(1), _full((1, D_MODEL))],
        out_specs=[tokj, tokj, tok(D_MODEL), _full((N_SHARD, STAT_ROWS, FF_BLK)), _full((STAT_ROWS, D_MODEL))], out_shape=outs,
        scratch_shapes=[pltpu.VMEM((TM, D_MODEL), F32)],
        compiler_params=_cp(dimension_semantics=("arbitrary", "arbitrary")),
    )(dr2, dr2, a_pre, a, a, b, b, w_down, w_a, w_b, conv_w, xhat1, rstd1, ln1_g)


def _mix_bwd(dr1, w_o, hu, hz, mixed, attn, ln_z_g, ln_z_b, w_s, dep):
    t = dr1.shape[0]
    nchunk = TM // BLK

    def body(dr_ref, wo_ref, hu_ref, hz_ref, mx_ref, attn_ref, g_ref, b_ref, ws_ref, grp_ref, red_ref, dep_ref,
             do1_ref, do4_ref, do16_ref, dl1_ref, dl4_ref, dl16_ref, duz_ref, dws_ref, dbs_ref, zstat_ref,
             wm_scr, dzn_scr, dbsum_scr, do_scr, dl_scr):
        @pl.when(pl.program_id(0) == 0)
        def _():
            row = lax.broadcasted_iota(jnp.int32, (BLK, BLK), 0)
            col = lax.broadcasted_iota(jnp.int32, (BLK, BLK), 1)
            for g in range(N_HEADS):
                wm_scr[g] = jnp.where(col <= row, ws_ref[g], 0.0).astype(MXU)
            dws_ref[...] = jnp.zeros_like(dws_ref)
            dbsum_scr[...] = jnp.zeros_like(dbsum_scr)
            zstat_ref[...] = jnp.zeros_like(zstat_ref)

        dcat = _dot_nt(dr_ref[...], wo_ref[...])
        dattn = dcat[:, 0:D_ATTN]
        do1_ref[...] = dattn.astype(MXU)
        for cc, val in enumerate(_chunks(dattn)):
            do_scr[cc] = val
        _to_planes(do4_ref, do_scr, DILATIONS[1], LANE_CHUNKS, MXU)
        _to_planes(do16_ref, do_scr, DILATIONS[2], LANE_CHUNKS, MXU)
        delta = _dot_select(dattn * attn_ref[...], red_ref[...])
        dl1_ref[...] = delta
        dl_scr[0] = delta
        _to_planes(dl4_ref, dl_scr, DILATIONS[1], 1, F32)
        _to_planes(dl16_ref, dl_scr, DILATIONS[2], 1, F32)
        dgm = dcat[:, D_ATTN:]
        hu, hz = hu_ref[...], hz_ref[...]
        u = _gelu(hu)
        duz_ref[:, 0:D_GMLP] = (dgm * mx_ref[...] * _gelu_grad(hu)).astype(MXU)
        dmixed = dgm * u
        dmb = dmixed.astype(MXU)
        zhat, rstd = _ln_fwd(_gelu(hz))
        znb = (zhat * g_ref[...] + b_ref[...]).astype(MXU)
        dbs_acc = jnp.zeros((BLK, D_GMLP), F32)
        for ch in range(nchunk):
            rows = slice(ch * BLK, (ch + 1) * BLK)
            dbs_acc = dbs_acc + dmixed[rows]
            for g in range(N_HEADS):
                cols = slice(g * HEAD_DIM, (g + 1) * HEAD_DIM)
                dzn_scr[rows, cols] = _dot_tn(wm_scr[g], dmb[rows, cols])
                dws_ref[g] += _dot_nt(dmb[rows, cols], znb[rows, cols])
        dbsum_scr[...] += dbs_acc
        dzn = dzn_scr[...]
        zstat_ref[0:1, :] += _colsum(dzn * zhat)
        zstat_ref[1:2, :] += _colsum(dzn)
        duz_ref[:, D_GMLP:] = (_ln_bwd(dzn, zhat, rstd, g_ref[...]) * _gelu_grad(hz)).astype(MXU)

        @pl.when(pl.program_id(0) == nt - 1)
        def _():
            row = lax.broadcasted_iota(jnp.int32, (BLK, BLK), 0)
            col = lax.broadcasted_iota(jnp.int32, (BLK, BLK), 1)
            for g in range(N_HEADS):
                dws_ref[g] = jnp.where(col <= row, dws_ref[g], 0.0)
            dbs_ref[...] = lax.dot_general(grp_ref[...], dbsum_scr[...], (((1,), (1,)), ((), ())),
                                           precision=lax.Precision.HIGHEST, preferred_element_type=F32)

    nt = t // TM
    tok = lambda w: pl.BlockSpec((TM, w), lambda i: (i, 0))
    grp = jnp.asarray((np.arange(D_GMLP)[None, :] // HEAD_DIM == np.arange(N_HEADS)[:, None]).astype(np.float32))
    red = _head_reduce()
    outs = [_perm_shape(t, d, D_ATTN, MXU) for d in DILATIONS] + [_perm_shape(t, d, 128, F32) for d in DILATIONS] + [
        jax.ShapeDtypeStruct((t, 2 * D_GMLP), MXU),
        jax.ShapeDtypeStruct((N_HEADS, BLK, BLK), F32), jax.ShapeDtypeStruct((N_HEADS, BLK), F32),
        jax.ShapeDtypeStruct((STAT_ROWS, D_GMLP), F32)]
    return pl.pallas_call(
        body, name="mix_bwd", grid=(t // TM,),
        in_specs=[tok(D_MODEL), _full(w_o.shape), tok(D_GMLP), tok(D_GMLP), tok(D_GMLP), tok(D_ATTN), _full(ln_z_g.shape),
                  _full(ln_z_b.shape), _full(w_s.shape), _full(grp.shape), _full(red.shape), pl.BlockSpec(memory_space=pl.ANY)],
        out_specs=[_perm_tile(d, D_ATTN) for d in DILATIONS] + [_perm_tile(d, 128) for d in DILATIONS]
        + [tok(2 * D_GMLP), _full((N_HEADS, BLK, BLK)), _full((N_HEADS, BLK)), _full((STAT_ROWS, D_GMLP))],
        out_shape=outs,
        scratch_shapes=[pltpu.VMEM((N_HEADS, BLK, BLK), MXU), pltpu.VMEM((TM, D_GMLP), F32), pltpu.VMEM((BLK, D_GMLP), F32),
                        pltpu.VMEM((LANE_CHUNKS, TM, 128), F32), pltpu.VMEM((1, TM, 128), F32)],
        compiler_params=_cp(dimension_semantics=("arbitrary",)),
    )(dr1, w_o, hu, hz, mixed, attn, ln_z_g, ln_z_b, w_s, grp, red, dep)


def _dx_in(dqs, dks, dvs, duz, dr1, w_in, c_tab, s1_tab, s2_tab):
    t = dr1.shape[0]

    def body(dq1, dq4, dq16, dk1, dk4, dk16, dv1, dv4, dv16, duz_ref, dr_ref, w_ref, c_ref, s1_ref, s2_ref,
             dh_ref, dx_ref, acc_scr):
        sums = []
        for part, (g1, g4, g16) in enumerate(((dq1, dq4, dq16), (dk1, dk4, dk16), (dv1, dv4, dv16))):
            acc = acc_scr.at[pl.ds(part * LANE_CHUNKS, LANE_CHUNKS)]
            for cc in range(LANE_CHUNKS):
                acc[cc] = g1[:, cc * 128:(cc + 1) * 128]
            _from_planes(g4, acc, DILATIONS[1], LANE_CHUNKS, accumulate=True)
            _from_planes(g16, acc, DILATIONS[2], LANE_CHUNKS, accumulate=True)
            sums.append(_unchunk(acc_scr, LANE_CHUNKS, part * LANE_CHUNKS))
        c, s1, s2 = _tile_heads(c_ref[...]), _tile_heads(s1_ref[...]), _tile_heads(s2_ref[...])
        dh_ref[:, 0:D_ATTN] = _rope_apply_t(sums[0] * (1.0 / math.sqrt(HEAD_DIM)), c, s1, s2).astype(MXU)
        dh_ref[:, D_ATTN:2 * D_ATTN] = _rope_apply_t(sums[1], c, s1, s2).astype(MXU)
        dh_ref[:, 2 * D_ATTN:3 * D_ATTN] = sums[2].astype(MXU)
        dh_ref[:, 3 * D_ATTN:] = duz_ref[...]
        dx = ALPHA * dr_ref[...]
        for j in range(N_SHARD):
            dx = dx + _dot_nt(dh_ref[:, j * W_IN_BLK:(j + 1) * W_IN_BLK], w_ref[j])
        dx_ref[...] = dx

    tok = lambda w: pl.BlockSpec((TM, w), lambda i: (i, 0))
    outs = [jax.ShapeDtypeStruct((t, D_IN), MXU), jax.ShapeDtypeStruct((t, D_MODEL), F32)]
    return pl.pallas_call(
        body, name="dx_in", grid=(t // TM,),
        in_specs=[_perm_tile(d, D_ATTN) for d in DILATIONS] * 3
        + [tok(2 * D_GMLP), tok(D_MODEL), _full(w_in.shape), tok(128), tok(128), tok(128)],
        out_specs=[tok(D_IN), tok(D_MODEL)], out_shape=outs,
        scratch_shapes=[pltpu.VMEM((3 * LANE_CHUNKS, TM, 128), F32)],
        compiler_params=_cp(dimension_semantics=("arbitrary",)),
    )(*dqs, *dks, *dvs, duz, dr1, w_in, c_tab, s1_tab, s2_tab)


def _wgrad(name, x, dy, x_spec, dy_spec, out_spec, out_shape, grid):
    def body(x_ref, dy_ref, o_ref):
        o_ref[...] = _dot_tn(x_ref[...], dy_ref[...])

    return pl.pallas_call(
        body, name=name, grid=grid, in_specs=[x_spec, dy_spec], out_specs=out_spec,
        out_shape=jax.ShapeDtypeStruct(out_shape, F32),
        compiler_params=_cp(dimension_semantics=("arbitrary",) * len(grid)),
    )(x, dy)


def _wgrad_pair(name, xa, xb, dy, x_spec, dy_spec, out_spec, out_shape, grid):
    def body(xa_ref, xb_ref, dy_ref, oa_ref, ob_ref):
        dy = dy_ref[...]
        oa_ref[...] = _dot_tn(xa_ref[...], dy)
        ob_ref[...] = _dot_tn(xb_ref[...], dy)

    return pl.pallas_call(
        body, name=name, grid=grid, in_specs=[x_spec, x_spec, dy_spec], out_specs=[out_spec, out_spec],
        out_shape=[jax.ShapeDtypeStruct(out_shape, F32)] * 2,
        compiler_params=_cp(dimension_semantics=("arbitrary",) * len(grid)),
    )(xa, xb, dy)


def _local_step(x, p, rope, target, w_in, start_dep, late_landed, late_weights, early_grads, early_grads_sent,
                ln_z_g, ln_z_b, w_s, b_s, ln1_g, ln1_b, conv_b, ln2_g, ln2_b, b_g, ln3_g, ln3_b):
    t = x.shape[0]
    half = TM
    c_tab, s1_tab, s2_tab = rope
    b_full = jnp.repeat(jnp.transpose(b_s[0]), HEAD_DIM, axis=1)
    conv_b4 = conv_b.reshape(N_SHARD, 1, FF_BLK)
    *qkvs, hu, hz, mixed, gm = _qkvuz(x, w_in, c_tab, s1_tab, s2_tab, ln_z_g, ln_z_b, w_s[0], b_full, start_dep)
    branches = [_attn_fwd(qkv, d, start_dep) for qkv, d in zip(qkvs, DILATIONS)]
    w_o, dep = late_landed(branches[-1][1])
    attn, *lses, cat, xhat1, rstd1, x1b = _mix_ln1(
        [o for o, _ in branches], [l for _, l in branches], gm, x, w_o, ln1_g, ln1_b, dep)
    w_a, w_b, conv_w, w_down, w_g, w_p = late_weights(x1b)
    a_pre, a_act, b_act, f = _ffn_in(x1b, w_a, w_b, conv_w, conv_b4)
    xhat2, rstd2, x2b = _ffn_out_ln2(f, w_down, xhat1, ln1_g, ln1_b, ln2_g, ln2_b)
    dr2, dgp, dpp, stat3 = _ple_loss_bwd(xhat2, rstd2, p, target, ln2_g, ln2_b, w_g, b_g, w_p, ln3_g, ln3_b)
    da_pre, dbb, dr1, cstat, stat1 = _ffn_bwd(dr2, a_pre, a_act, b_act, w_down, w_a, w_b, conv_w, xhat1, rstd1, ln1_g)

    full_t = lambda w, im: pl.BlockSpec((t, w), im)
    ffj = pl.BlockSpec((None, t, FF_BLK), lambda j, kk: (j, 0, 0))
    early = dict(
        w_ple_gate=_wgrad("dw_g", x2b, dgp, full_t(half, lambda kk, n: (0, kk)), full_t(half, lambda kk, n: (0, n)),
                          pl.BlockSpec((half, half), lambda kk, n: (kk, n)), (D_MODEL, D_MODEL), (2, 2)),
        w_ple_in=_wgrad("dw_p", p, dpp, full_t(D_PLE, lambda j: (0, 0)), full_t(ROW_BLK, lambda j: (0, j)),
                        pl.BlockSpec((None, D_PLE, ROW_BLK), lambda j: (j, 0, 0)), (N_SHARD, D_PLE, ROW_BLK), (N_SHARD,)),
        w_ff_down=_wgrad("dw_down", f, dr2, ffj, full_t(half, lambda j, n: (0, n)),
                         pl.BlockSpec((None, FF_BLK, half), lambda j, n: (j, 0, n)), (N_SHARD, FF_BLK, D_MODEL), (N_SHARD, 2)),
        **dict(zip(("w_ff_a", "w_ff_b"), _wgrad_pair(
            "dw_ab", da_pre, dbb, x1b, ffj, full_t(half, lambda j, n: (0, n)),
            pl.BlockSpec((None, FF_BLK, half), lambda j, n: (j, 0, n)), (N_SHARD, FF_BLK, D_MODEL), (N_SHARD, 2)))),
        w_o=_wgrad("dw_o", cat, dr1, full_t(half, lambda kk, n: (0, kk)), full_t(half, lambda kk, n: (0, n)),
                   pl.BlockSpec((half, half), lambda kk, n: (kk, n)), (D_MODEL, D_MODEL), (2, 2)))
    dep = early_grads(early)

    do1, do4, do16, dl1, dl4, dl16, duz, dws, dbs, zstat = _mix_bwd(
        dr1, w_o, hu, hz, mixed, attn, ln_z_g, ln_z_b, w_s[0], dep)
    dep = early_grads_sent(duz, (stat3, stat1, zstat, cstat, dws, dbs))
    dqkv = [_attn_bwd(qkv, do, lse, dl, d, dep)
            for qkv, do, lse, dl, d in zip(qkvs, (do1, do4, do16), lses, (dl1, dl4, dl16), DILATIONS)]
    dh, grad_x = _dx_in([g[0] for g in dqkv], [g[1] for g in dqkv], [g[2] for g in dqkv], duz, dr1, w_in,
                        c_tab, s1_tab, s2_tab)
    g_w_in = _wgrad("dw_in", x, dh, full_t(half, lambda j, kk: (0, kk)), full_t(W_IN_BLK, lambda j, kk: (0, j)),
                    pl.BlockSpec((None, half, W_IN_BLK), lambda j, kk: (j, kk, 0)), (N_SHARD, D_MODEL, W_IN_BLK), (N_SHARD, 2))
    return grad_x, g_w_in


def _tile_rows(rows, mult, steps):
    if rows % mult:
        return rows
    return next(rows // k for k in range(steps, rows + 1) if rows % k == 0 and (rows // k) % mult == 0)


def _grid_spec(grid, in_specs, out_specs):
    return pltpu.PrefetchScalarGridSpec(num_scalar_prefetch=1, grid=grid, in_specs=in_specs, out_specs=out_specs)


def _on_own_steps(i, count, steps, work):
    if count == steps:
        work()
    else:
        pl.when(i < count)(work)


def _place_shards(name, ws, dtypes, place, dep):
    n = len(ws)
    tiles = [_tile_rows(w.shape[0], 16, 8) for w in ws]
    counts = [w.shape[0] // t for w, t in zip(ws, tiles)]
    steps = max(counts)

    def body(s_ref, *refs):
        i = pl.program_id(0)
        for a in range(n):
            def work(a=a):
                refs[n + 1 + a][...] = refs[a][...].astype(dtypes[a])
            _on_own_steps(i, counts[a], steps, work)

    def tile(a, lead):
        last = counts[a] - 1
        if lead:
            return pl.BlockSpec((None, tiles[a], ws[a].shape[1]), lambda i, s: (s[0], jnp.minimum(i, last), 0))
        return pl.BlockSpec((tiles[a], ws[a].shape[1]), lambda i, s: (jnp.minimum(i, last), 0))

    return pl.pallas_call(
        body, name=name,
        grid_spec=_grid_spec((steps,), [tile(a, False) for a in range(n)] + [pl.BlockSpec(memory_space=pl.ANY)],
                             [tile(a, True) for a in range(n)]),
        out_shape=[jax.ShapeDtypeStruct((N_SHARD, *w.shape), dt) for w, dt in zip(ws, dtypes)],
        compiler_params=_cp())(place, *ws, dep)


def _pair_sums(name, mines, gots, place):
    n = len(mines)
    tiles = [_tile_rows(g.shape[1], 16, 2) for g in gots]
    per_blk = [g.shape[1] // t for g, t in zip(gots, tiles)]
    counts = [N_SHARD * nh for nh in per_blk]
    steps = max(counts)

    def body(s_ref, *refs):
        i = pl.program_id(0)
        for a in range(n):
            def work(a=a):
                refs[2 * n + a][...] = (refs[a][...] + refs[n + a][...]).astype(BF16)
            _on_own_steps(i, counts[a], steps, work)

    def tile(a, mine):
        nh, last = per_blk[a], counts[a] - 1

        def index(i, s):
            g = jnp.minimum(i, last)
            return (g // nh, (s[1] * nh if mine else 0) + g % nh, 0)

        return pl.BlockSpec((None, tiles[a], gots[a].shape[2]), index)

    return pl.pallas_call(
        body, name=name,
        grid_spec=_grid_spec((steps,), [tile(a, True) for a in range(n)] + [tile(a, False) for a in range(n)],
                             [tile(a, False) for a in range(n)]),
        out_shape=[jax.ShapeDtypeStruct(g.shape, BF16) for g in gots], compiler_params=_cp())(place, *mines, *gots)


def _chip_sums(name, owns, landeds, place, dep):
    n = len(owns)
    tiles = [_tile_rows(o.shape[1], 16, 8) for o in owns]
    counts = [o.shape[1] // t for o, t in zip(owns, tiles)]
    steps = max(counts)

    def body(s_ref, *refs):
        i = pl.program_id(0)
        for a in range(n):
            def work(a=a):
                own, l1, l2, l3 = (refs[4 * a + k][...].astype(F32) for k in range(4))
                refs[4 * n + 1 + a][...] = ((own + l1) + l2) + l3
            _on_own_steps(i, counts[a], steps, work)

    def slot(a, d):
        last = counts[a] - 1
        return pl.BlockSpec((None, tiles[a], owns[a].shape[2]), lambda i, s: ((s[0] + d) % N_SHARD, jnp.minimum(i, last), 0))

    def out(a):
        nh, last = counts[a], counts[a] - 1
        return pl.BlockSpec((tiles[a], owns[a].shape[2]), lambda i, s: (s[1] * nh + jnp.minimum(i, last), 0))

    operands = [x for o, l in zip(owns, landeds) for x in (o, l, l, l)]
    return pl.pallas_call(
        body, name=name,
        grid_spec=_grid_spec((steps,), [slot(a, d) for a in range(n) for d in range(4)] + [pl.BlockSpec(memory_space=pl.ANY)],
                             [out(a) for a in range(n)]),
        out_shape=[jax.ShapeDtypeStruct((2 * o.shape[1], o.shape[2]), F32) for o in owns],
        compiler_params=_cp())(place, *operands, dep)


def _adamw_math(w, g, m, v):
    m = ADAM_B1 * m + (1.0 - ADAM_B1) * g
    v = ADAM_B2 * v + (1.0 - ADAM_B2) * (g * g)
    m_hat = m / (1.0 - ADAM_B1 ** ADAM_STEP)
    v_hat = v / (1.0 - ADAM_B2 ** ADAM_STEP)
    delta = -ADAM_LR * (m_hat / (jnp.sqrt(v_hat) + ADAM_EPS) + ADAM_WD * w)
    return delta, m, v


def _adamw_shards(name, ws, gs, ms, vs):
    n = len(ws)
    tiles = [_tile_rows(w.shape[1], 8, 8) for w in ws]
    counts = [w.shape[1] // t for w, t in zip(ws, tiles)]
    steps = max(counts)

    def body(*refs):
        i = pl.program_id(0)
        for a in range(n):
            def work(a=a):
                w_ref, g_ref, m_ref, v_ref = refs[4 * a:4 * a + 4]
                d_ref, nm_ref, nv_ref = refs[4 * n + 3 * a:4 * n + 3 * a + 3]
                d_ref[...], nm_ref[...], nv_ref[...] = _adamw_math(w_ref[...], g_ref[...], m_ref[...], v_ref[...])
            _on_own_steps(i, counts[a], steps, work)

    def tile(a, lead):
        last, c = counts[a] - 1, ws[a].shape[2]
        if lead:
            return pl.BlockSpec((None, tiles[a], c), lambda i: (0, jnp.minimum(i, last), 0))
        return pl.BlockSpec((tiles[a], c), lambda i: (jnp.minimum(i, last), 0))

    res = pl.pallas_call(
        body, name=name, grid=(steps,),
        in_specs=[tile(a, lead) for a in range(n) for lead in (True, False, True, True)],
        out_specs=[tile(a, True) for a in range(n) for _ in range(3)],
        out_shape=[jax.ShapeDtypeStruct(w.shape, F32) for w in ws for _ in range(3)],
        compiler_params=_cp())(*[x for quad in zip(ws, gs, ms, vs) for x in quad])
    return [tuple(res[3 * a:3 * a + 3]) for a in range(n)]


MESH = pl.DeviceIdType.MESH
ANY = pl.BlockSpec(memory_space=pl.ANY)


def _place():
    x, y, c = lax.axis_index("x"), lax.axis_index("y"), lax.axis_index("c")
    chips = [(1 - x, y), (x, 1 - y), (1 - x, 1 - y)]
    return x, y, c, 2 * x + y, chips


def _remote(src, dst, send_sem, recv_sem, dev):
    return pltpu.make_async_remote_copy(src_ref=src, dst_ref=dst, send_sem=send_sem, recv_sem=recv_sem,
                                        device_id=dev, device_id_type=MESH)


def _half(ref, hc, rows):
    return ref.at[pl.ds(hc * (rows // 2), rows // 2)]


def _sibling_join(blocks, tag):
    n = len(blocks)

    def body(*refs):
        outs = refs[n:2 * n]
        send, recv = refs[2 * n:]
        x, y, c, _, _ = _place()
        cps = []
        for a in range(n):
            h = blocks[a].shape[0] // 2
            mine = outs[a].at[pl.ds(c * h, h)]
            cp = _remote(mine, mine, send.at[a], recv.at[a], (x, y, 1 - c))
            cp.start()
            cps.append(cp)
        for a, cp in enumerate(cps):
            h = blocks[a].shape[0] // 2
            theirs = outs[a].at[pl.ds((1 - c) * h, h)]
            _remote(theirs, theirs, send.at[a], recv.at[a], (x, y, 1 - c)).wait_recv()
            cp.wait_send()

    sem = pltpu.SemaphoreType.DMA
    return pl.pallas_call(body, name=f"rs_sibling_join_{tag}", in_specs=[ANY] * n, out_specs=[ANY] * n,
                          out_shape=[jax.ShapeDtypeStruct(b_.shape, b_.dtype) for b_ in blocks],
                          input_output_aliases={a: a for a in range(n)},
                          scratch_shapes=[sem((n,)), sem((n,))])(*blocks)


HBM = pl.BlockSpec(memory_space=pltpu.HBM)
SEM = pl.BlockSpec(memory_space=pltpu.SEMAPHORE)
TOKEN = jax.ShapeDtypeStruct((8, 128), F32)


def _in_flight_params():
    return pltpu.CompilerParams(has_side_effects=pltpu.SideEffectType.DATAFLOW_SIDE_EFFECTING)


def _in_hbm(a):
    return pltpu.with_memory_space_constraint(a, pltpu.HBM)


def _gather_piece(ref, rows, split, slot, hc):
    return _half(ref.at[slot], hc, rows) if split else ref.at[slot]


def _gather_start(stacks, split, after, tag):
    n = len(stacks)

    def body(*refs):
        ins = refs[:n]
        send, recv = refs[n + 1], refs[n + 2]
        token = refs[2 * n + 3]
        _, _, c, j, chips = _place()
        for a in range(n):
            mine = _gather_piece(ins[a], stacks[a].shape[1], split[a], j, c)
            for t in range(3):
                _remote(mine, mine, send.at[3 * a + t], recv.at[3 * a + t], (*chips[t], c)).start()
        token[...] = jnp.zeros_like(token)

    sems = pltpu.SemaphoreType.DMA((3 * n,))
    res = pl.pallas_call(
        body, name=f"gather_start_{tag}", in_specs=[HBM] * n + [ANY],
        out_specs=[SEM, SEM] + [HBM] * n + [pl.BlockSpec(memory_space=pltpu.VMEM)],
        out_shape=[sems, sems] + [pltpu.HBM(s.shape, s.dtype) for s in stacks] + [TOKEN],
        input_output_aliases={a: a + 2 for a in range(n)}, compiler_params=_in_flight_params(),
    )(*[_in_hbm(s) for s in stacks], after)
    return res[0], res[1], res[2:2 + n], res[2 + n]


def _gather_wait(send, recv, stacks, split, after, tag):
    n = len(stacks)

    def body(*refs):
        ins = refs[:n]
        send_ref, recv_ref = refs[n], refs[n + 1]
        _, _, c, j, chips = _place()
        for a in range(n):
            rows = stacks[a].shape[1]
            mine = _gather_piece(ins[a], rows, split[a], j, c)
            for t, (px, py) in enumerate(chips):
                theirs = _gather_piece(ins[a], rows, split[a], 2 * px + py, c)
                _remote(mine, mine, send_ref.at[3 * a + t], recv_ref.at[3 * a + t], (px, py, c)).wait_send()
                _remote(theirs, theirs, send_ref.at[3 * a + t], recv_ref.at[3 * a + t], (px, py, c)).wait_recv()

    return pl.pallas_call(
        body, name=f"gather_wait_{tag}", in_specs=[HBM] * n + [SEM, SEM, ANY], out_specs=[HBM] * n,
        out_shape=[pltpu.HBM(s.shape, s.dtype) for s in stacks],
        input_output_aliases={a: a for a in range(n)}, compiler_params=_in_flight_params(),
    )(*stacks, send, recv, after)


def _gather_forward(stacks, split, tag):
    idx = [a for a in range(len(stacks)) if split[a]]
    n = len(idx)

    def body(*refs):
        outs = refs[n:2 * n]
        send, recv = refs[2 * n:]
        x, y, c, _, chips = _place()
        sends = []
        for t, (px, py) in enumerate(chips):
            for a in range(n):
                blk = _half(outs[a].at[2 * px + py], c, stacks[idx[a]].shape[1])
                cp = _remote(blk, blk, send.at[a, t], recv.at[a, t], (x, y, 1 - c))
                cp.start()
                sends.append(cp)
        for t, (px, py) in enumerate(chips):
            for a in range(n):
                blk = _half(outs[a].at[2 * px + py], 1 - c, stacks[idx[a]].shape[1])
                _remote(blk, blk, send.at[a, t], recv.at[a, t], (x, y, 1 - c)).wait_recv()
        for cp in sends:
            cp.wait_send()

    sem = pltpu.SemaphoreType.DMA
    res = pl.pallas_call(
        body, name=f"gather_forward_{tag}", in_specs=[ANY] * n, out_specs=[ANY] * n,
        out_shape=[jax.ShapeDtypeStruct(stacks[a].shape, stacks[a].dtype) for a in idx],
        input_output_aliases={a: a for a in range(n)}, scratch_shapes=[sem((n, 3)), sem((n, 3))],
    )(*[stacks[a] for a in idx])
    out = list(stacks)
    for a, r in zip(idx, res):
        out[a] = r
    return out


def _forward_start(stacks, after, tag):
    n = len(stacks)

    def body(*refs):
        ins = refs[:n]
        send, recv = refs[n + 1], refs[n + 2]
        token = refs[2 * n + 3]
        x, y, c, _, chips = _place()
        for a in range(n):
            for t, (px, py) in enumerate(chips):
                blk = _half(ins[a].at[2 * px + py], c, stacks[a].shape[1])
                _remote(blk, blk, send.at[3 * a + t], recv.at[3 * a + t], (x, y, 1 - c)).start()
        token[...] = jnp.zeros_like(token)

    sems = pltpu.SemaphoreType.DMA((3 * n,))
    res = pl.pallas_call(
        body, name=f"forward_start_{tag}", in_specs=[HBM] * n + [ANY],
        out_specs=[SEM, SEM] + [HBM] * n + [pl.BlockSpec(memory_space=pltpu.VMEM)],
        out_shape=[sems, sems] + [pltpu.HBM(s.shape, s.dtype) for s in stacks] + [TOKEN],
        input_output_aliases={a: a + 2 for a in range(n)}, compiler_params=_in_flight_params(),
    )(*[_in_hbm(s) for s in stacks], after)
    return res[0], res[1], res[2:2 + n], res[2 + n]


def _forward_wait(send, recv, stacks, after, tag):
    n = len(stacks)

    def body(*refs):
        ins = refs[:n]
        send_ref, recv_ref = refs[n], refs[n + 1]
        x, y, c, _, chips = _place()
        for a in range(n):
            for t, (px, py) in enumerate(chips):
                mine = _half(ins[a].at[2 * px + py], c, stacks[a].shape[1])
                theirs = _half(ins[a].at[2 * px + py], 1 - c, stacks[a].shape[1])
                _remote(mine, mine, send_ref.at[3 * a + t], recv_ref.at[3 * a + t], (x, y, 1 - c)).wait_send()
                _remote(theirs, theirs, send_ref.at[3 * a + t], recv_ref.at[3 * a + t], (x, y, 1 - c)).wait_recv()

    return pl.pallas_call(
        body, name=f"forward_wait_{tag}", in_specs=[HBM] * n + [SEM, SEM, ANY], out_specs=[HBM] * n,
        out_shape=[pltpu.HBM(s.shape, s.dtype) for s in stacks],
        input_output_aliases={a: a for a in range(n)}, compiler_params=_in_flight_params(),
    )(*stacks, send, recv, after)


def _swap_start(grads, tag):
    n = len(grads)

    def body(*refs):
        ins, gots = refs[:n], refs[n:2 * n]
        send, recv = refs[2 * n], refs[2 * n + 1]
        token = refs[4 * n + 2]
        x, y, c, _, _ = _place()
        for a in range(n):
            h = grads[a].shape[1] // 2
            _remote(ins[a].at[:, pl.ds((1 - c) * h, h)], gots[a], send.at[a], recv.at[a], (x, y, 1 - c)).start()
        token[...] = jnp.zeros_like(token)

    sems = pltpu.SemaphoreType.DMA((n,))
    halves = [(g.shape[0], g.shape[1] // 2, g.shape[2]) for g in grads]
    res = pl.pallas_call(
        body, name=f"swap_start_{tag}", in_specs=[HBM] * (2 * n),
        out_specs=[SEM, SEM] + [HBM] * (2 * n) + [pl.BlockSpec(memory_space=pltpu.VMEM)],
        out_shape=[sems, sems] + [pltpu.HBM(g.shape, g.dtype) for g in grads] + [pltpu.HBM(s, F32) for s in halves] + [TOKEN],
        input_output_aliases={a: a + 2 for a in range(2 * n)}, compiler_params=_in_flight_params(),
    )(*[_in_hbm(g) for g in grads], *[_in_hbm(lax.empty(s, F32)) for s in halves])
    return res[0], res[1], res[2:2 + n], res[2 + n:2 + 2 * n], res[2 + 2 * n]


def _swap_wait(send, recv, grads, gots, after, tag):
    n = len(grads)

    def body(*refs):
        ins, lnd = refs[:n], refs[n:2 * n]
        send_ref, recv_ref = refs[2 * n], refs[2 * n + 1]
        x, y, c, _, _ = _place()
        for a in range(n):
            h = grads[a].shape[1] // 2
            cp = _remote(ins[a].at[:, pl.ds((1 - c) * h, h)], lnd[a], send_ref.at[a], recv_ref.at[a], (x, y, 1 - c))
            cp.wait_send()
            cp.wait_recv()

    bufs = [pltpu.HBM(g.shape, g.dtype) for g in grads] + [pltpu.HBM(g.shape, g.dtype) for g in gots]
    res = pl.pallas_call(
        body, name=f"swap_wait_{tag}", in_specs=[HBM] * (2 * n) + [SEM, SEM, ANY], out_specs=[HBM] * (2 * n),
        out_shape=bufs, input_output_aliases={a: a for a in range(2 * n)}, compiler_params=_in_flight_params(),
    )(*grads, *gots, send, recv, after)
    return res[:n], res[n:]


def _exchange_start(parts, tag):
    n = len(parts)

    def body(*refs):
        ins, lands = refs[:n], refs[n:2 * n]
        send, recv = refs[2 * n], refs[2 * n + 1]
        token = refs[4 * n + 2]
        _, _, c, j, chips = _place()
        for t, (px, py) in enumerate(chips):
            for a in range(n):
                _remote(ins[a].at[2 * px + py], lands[a].at[j], send.at[3 * a + t], recv.at[3 * a + t], (px, py, c)).start()
        token[...] = jnp.zeros_like(token)

    sems = pltpu.SemaphoreType.DMA((3 * n,))
    bufs = [pltpu.HBM(p.shape, p.dtype) for p in parts]
    res = pl.pallas_call(
        body, name=f"exchange_start_{tag}", in_specs=[HBM] * (2 * n),
        out_specs=[SEM, SEM] + [HBM] * (2 * n) + [pl.BlockSpec(memory_space=pltpu.VMEM)],
        out_shape=[sems, sems] + bufs + bufs + [TOKEN],
        input_output_aliases={a: a + 2 for a in range(2 * n)}, compiler_params=_in_flight_params(),
    )(*[_in_hbm(p) for p in parts], *[_in_hbm(lax.empty(p.shape, p.dtype)) for p in parts])
    return res[0], res[1], res[2:2 + n], res[2 + n:2 + 2 * n], res[2 + 2 * n]


def _exchange_wait(send, recv, parts, lands, after, tag):
    n = len(parts)

    def body(*refs):
        ins, lnd = refs[:n], refs[n:2 * n]
        send_ref, recv_ref = refs[2 * n], refs[2 * n + 1]
        _, _, c, j, chips = _place()
        for t, (px, py) in enumerate(chips):
            jt = 2 * px + py
            for a in range(n):
                _remote(ins[a].at[jt], lnd[a].at[j], send_ref.at[3 * a + t], recv_ref.at[3 * a + t], (px, py, c)).wait_send()
                _remote(ins[a].at[jt], lnd[a].at[jt], send_ref.at[3 * a + t], recv_ref.at[3 * a + t], (px, py, c)).wait_recv()

    bufs = [pltpu.HBM(p.shape, p.dtype) for p in parts]
    res = pl.pallas_call(
        body, name=f"exchange_wait_{tag}", in_specs=[HBM] * (2 * n) + [SEM, SEM, ANY], out_specs=[HBM] * (2 * n),
        out_shape=bufs + bufs, input_output_aliases={a: a for a in range(2 * n)}, compiler_params=_in_flight_params(),
    )(*parts, *lands, send, recv, after)
    return res[:n], res[n:]


def _small_chip_sums(arrs):
    n = len(arrs)

    def body(*refs):
        ins, outs = refs[:n], refs[n:2 * n]
        sib = refs[2 * n:3 * n]
        send, recv = refs[3 * n:]
        x, y, c, j, _ = _place()
        swaps = [_remote(ins[a], sib[a], send.at[a], recv.at[a], (x, y, 1 - c)) for a in range(n)]
        for cp in swaps:
            cp.start()
        for a in range(n):
            swaps[a].wait_recv()
            outs[a][j] = ins[a][...] + sib[a][...]
        for cp in swaps:
            cp.wait_send()

    sem = pltpu.SemaphoreType.DMA
    vm = pl.BlockSpec(memory_space=pltpu.VMEM)
    return pl.pallas_call(
        body, name="small_chip_sums", in_specs=[vm] * n, out_specs=[vm] * n,
        out_shape=[jax.ShapeDtypeStruct((N_SHARD, *a.shape), F32) for a in arrs],
        scratch_shapes=[pltpu.VMEM(a.shape, F32) for a in arrs] + [sem((n,)), sem((n,))],
        compiler_params=_cp(),
    )(*arrs)


def _small_totals(stacks):
    n = len(stacks)

    def body(*refs):
        for a in range(n):
            refs[n + a][...] = ((refs[a][0] + refs[a][1]) + refs[a][2]) + refs[a][3]

    return pl.pallas_call(body, name="small_totals", out_shape=[jax.ShapeDtypeStruct(s.shape[1:], F32) for s in stacks],
                          compiler_params=_cp())(*stacks)


SMALL_1024 = ("ln1_g", "ln1_b", "ln2_g", "ln2_b", "b_ple_gate", "ln3_g", "ln3_b")


def _adamw_small(red3, red1, redz, g_conv_w, redc, red_ws, red_bs, params):
    shape2d = {"ln_z_g": (1, D_GMLP), "ln_z_b": (1, D_GMLP), "w_s": (N_HEADS * BLK, BLK), "b_s": (N_HEADS, BLK),
               "conv_w": (3, FF_BLK), "conv_b": (N_SHARD, FF_BLK), **{k: (1, D_MODEL) for k in SMALL_1024}}
    names = list(shape2d)
    flat = [a.reshape(shape2d[k]) for k in names for a in params[k]]

    def body(r3, r1, rz, gcw, rc, rws, rbs, *refs):
        ins, outs = refs[:3 * len(names)], refs[3 * len(names):]

        def grad_of(k):
            if k == "w_s":
                return rws[...]
            if k == "b_s":
                return rbs[...]
            if k == "conv_w":
                return gcw[0:3, :]
            if k == "conv_b":
                return jnp.concatenate([rc[j * STAT_ROWS + 3:j * STAT_ROWS + 4, :] for j in range(N_SHARD)], axis=0)
            src, row = {"ln3_g": (r3, 0), "ln3_b": (r3, 1), "b_ple_gate": (r3, 2), "ln2_g": (r3, 3), "ln2_b": (r3, 4),
                        "ln1_g": (r1, 0), "ln1_b": (r1, 1), "ln_z_g": (rz, 0), "ln_z_b": (rz, 1)}[k]
            return src[row:row + 1, :]

        for i, k in enumerate(names):
            w_ref, m_ref, v_ref = ins[3 * i:3 * i + 3]
            g_ref, d_ref, nm_ref, nv_ref = outs[4 * i:4 * i + 4]
            g = grad_of(k)
            g_ref[...] = g
            d_ref[...], nm_ref[...], nv_ref[...] = _adamw_math(w_ref[...], g, m_ref[...], v_ref[...])

    res = pl.pallas_call(
        body, name="adamw_small",
        out_shape=[jax.ShapeDtypeStruct(shape2d[k], F32) for k in names for _ in range(4)],
        compiler_params=_cp(),
    )(red3, red1, redz, g_conv_w, redc, red_ws, red_bs, *flat)
    return {k: tuple(r.reshape(params[k][0].shape) for r in res[4 * i:4 * i + 4]) for i, k in enumerate(names)}


WEIGHTS = ("w_in", "ln_z_g", "ln_z_b", "w_s", "b_s", "w_o", "ln1_g", "ln1_b", "w_ff_a", "w_ff_b", "conv_w", "conv_b",
           "w_ff_down", "ln2_g", "ln2_b", "w_ple_gate", "b_ple_gate", "w_ple_in", "ln3_g", "ln3_b")
BIG = ("w_in", "w_o", "w_ff_a", "w_ff_b", "w_ff_down", "w_ple_gate", "w_ple_in")
TRANSPOSED = ("w_ff_a", "w_ff_b")
LATE = ("w_o", "w_ff_a", "w_ff_b", "w_ff_down", "w_ple_gate", "w_ple_in", "conv_w")


def kernel(x, p, positions, w_in, ln_z_g, ln_z_b, w_s, b_s, w_o, ln1_g, ln1_b, w_ff_a, w_ff_b, conv_w, conv_b, w_ff_down, ln2_g, ln2_b, w_ple_gate, b_ple_gate, w_ple_in, ln3_g, ln3_b, loss_target, m_w_in, m_ln_z_g, m_ln_z_b, m_w_s, m_b_s, m_w_o, m_ln1_g, m_ln1_b, m_w_ff_a, m_w_ff_b, m_conv_w, m_conv_b, m_w_ff_down, m_ln2_g, m_ln2_b, m_w_ple_gate, m_b_ple_gate, m_w_ple_in, m_ln3_g, m_ln3_b, v_w_in, v_ln_z_g, v_ln_z_b, v_w_s, v_b_s, v_w_o, v_ln1_g, v_ln1_b, v_w_ff_a, v_w_ff_b, v_conv_w, v_conv_b, v_w_ff_down, v_ln2_g, v_ln2_b, v_w_ple_gate, v_b_ple_gate, v_w_ple_in, v_ln3_g, v_ln3_b):
    args = locals()
    w = {k: args[k] for k in WEIGHTS}
    m = {k: args["m_" + k] for k in WEIGHTS}
    v = {k: args["v_" + k] for k in WEIGHTS}

    for k in TRANSPOSED:
        w[k], m[k], v[k] = (jnp.swapaxes(a, 1, 2) for a in (w[k], m[k], v[k]))

    chip = 2 * lax.axis_index("x") + lax.axis_index("y")
    place = jnp.stack([chip, lax.axis_index("c")]).astype(jnp.int32)
    stack = dict(zip(["w_in"], _place_shards("cast_w_in", [w["w_in"][0]], [MXU], place, place)))
    i_send, i_recv, in_flight, dep = _gather_start([stack["w_in"]], [True], place, "w_in")
    stack.update(zip(LATE, _place_shards("cast_late", [w[k][0] for k in LATE],
                                         [F32 if k == "conv_w" else MXU for k in LATE], place, dep)))
    rope = _rope_tables(positions, x.shape[1], stack[LATE[-1]])
    landed_in = _gather_wait(i_send, i_recv, in_flight, [True], rope[0], "w_in")
    w_in_full, = _gather_forward(landed_in, [True], "w_in")
    split_late = [k not in ("w_o", "conv_w") for k in LATE]
    g_send, g_recv, late_flight, start_dep = _gather_start([stack[k] for k in LATE], split_late, w_in_full, "late")
    halves = [k for k, sp in zip(LATE, split_late) if sp]
    trips = {}

    def late_landed(after):
        fw = dict(zip(LATE, _gather_wait(g_send, g_recv, late_flight, split_late, after, "late")))
        trips["late"] = (fw, *_forward_start([fw[k] for k in halves], fw["conv_w"], "late"))
        return fw["w_o"].reshape(D_MODEL, D_MODEL), trips["late"][-1]

    def late_weights(after):
        fw, send, recv, flight, _ = trips["late"]
        fw.update(zip(halves, _forward_wait(send, recv, flight, after, "late")))
        return (fw["w_ff_a"], fw["w_ff_b"], fw["conv_w"], fw["w_ff_down"], fw["w_ple_gate"].reshape(D_MODEL, D_MODEL),
                fw["w_ple_in"])

    def swap_started(names, grads, tag):
        stacked = [g.reshape(N_SHARD, *w[k].shape[1:]) for k, g in zip(names, grads)]
        return (names, tag, *_swap_start(stacked, tag))

    def partial_sums(swap, after):
        names, tag, send, recv, stacked, gots, _ = swap
        stacked, got = _swap_wait(send, recv, stacked, gots, after, tag)
        pair = _pair_sums(f"rs_pair_{tag}", stacked, got, place)
        return (names, tag, *_exchange_start(pair, tag))

    def reduced(trip, after, dep):
        names, tag, send, recv, pair, lands, _ = trip
        pair, landed = _exchange_wait(send, recv, pair, lands, after, tag)
        blocks = _chip_sums(f"rs_sum_{tag}", pair, landed, place, dep)
        return dict(zip(names, _sibling_join(blocks, tag)))

    def early_grads(grads):
        trips["swap"] = swap_started(list(grads), list(grads.values()), "early")
        return trips["swap"][-1]

    def early_grads_sent(after, small):
        trips["early"] = partial_sums(trips["swap"], after)
        stat3, stat1, zstat, cstat, dws, dbs = small
        sums = _small_chip_sums([stat3, stat1, zstat, cstat.reshape(N_SHARD * STAT_ROWS, FF_BLK),
                                 dws.reshape(N_HEADS * BLK, BLK), dbs])
        trips["small"] = _gather_start(sums, [False] * len(sums), trips["early"][-1], "small")
        return trips["small"][-1]

    grad_x, g_w_in = _local_step(
        x[0], p[0, 0], rope, loss_target[0], w_in_full, start_dep, late_landed, late_weights, early_grads, early_grads_sent,
        ln_z_g, ln_z_b, w_s, b_s, ln1_g, ln1_b, conv_b, ln2_g, ln2_b, b_ple_gate, ln3_g, ln3_b)

    trips["w_in"] = partial_sums(swap_started(["w_in"], [g_w_in], "w_in"), g_w_in)
    out = {}

    def adamw(red, tag):
        names = list(red)
        steps = _adamw_shards(f"adamw_{tag}", [w[k] for k in names], [red[k] for k in names], [m[k] for k in names],
                              [v[k] for k in names])
        for k, (d, nm, nv) in zip(names, steps):
            out[k] = (red[k].reshape(w[k].shape), d, nm, nv)

    adamw(reduced(trips["early"], grad_x, trips["w_in"][-1]), "early")
    adamw(reduced(trips["w_in"], out["w_o"][3], start_dep), "w_in")
    for k in TRANSPOSED:
        out[k] = tuple(jnp.swapaxes(a, 1, 2) for a in out[k])

    s_send, s_recv, s_flight, _ = trips["small"]
    red3, red1, redz, redc, red_ws, red_bs = _small_totals(
        _gather_wait(s_send, s_recv, s_flight, [False] * len(s_flight), out["w_in"][3], "small"))
    loss = (0.5 / D_MODEL) * jnp.sum(red3[5])
    g_conv_w = lax.dynamic_slice_in_dim(redc, chip * STAT_ROWS, STAT_ROWS, 0)
    names_small = [k for k in WEIGHTS if k not in BIG]
    out.update(_adamw_small(red3, red1, redz, g_conv_w, redc, red_ws, red_bs, {k: (w[k], m[k], v[k]) for k in names_small}))

    return (loss, grad_x[None], *[out[k][0] for k in WEIGHTS], *[out[k][1] for k in WEIGHTS],
            *[out[k][2] for k in WEIGHTS], *[out[k][3] for k in WEIGHTS])
```

```python
import functools
import math

import numpy as np
import jax
import jax.numpy as jnp
from jax import lax
from jax.experimental import pallas as pl
from jax.experimental.pallas import tpu as pltpu

F32 = jnp.float32
BF16 = jnp.bfloat16
MXU = BF16

D_MODEL = 1024
HEAD_DIM = 64
N_HEADS = 8
D_ATTN = 512
D_GMLP = 512
D_IN = 2560
DILATIONS = (1, 4, 16)
BLK = 128
ROPE_THETA = 500000.0
ROPE_DIM = 16
D_FF = 2816
D_PLE = 256
LN_EPS = 1e-5
ALPHA = 2.0 ** 0.25
NEG_INF = -1e30
N_SHARD = 4
W_IN_BLK = D_IN // N_SHARD
FF_BLK = D_FF // N_SHARD
ROW_BLK = D_MODEL // N_SHARD
ADAM_LR, ADAM_B1, ADAM_B2, ADAM_EPS, ADAM_WD, ADAM_STEP = 0.001, 0.9, 0.999, 1e-08, 0.01, 10

TM = 512
HALO = 8
ROW_GROUPS = 2
VMEM_LIMIT = 56 * 1024 * 1024


def _cp(**kw):
    return pltpu.CompilerParams(vmem_limit_bytes=VMEM_LIMIT, **kw)


def _full(shape):
    n = len(shape)
    return pl.BlockSpec(shape, lambda *_: (0,) * n)


def _gelu(x):
    return 0.5 * x * (1.0 + lax.erf(x * (1.0 / math.sqrt(2.0))))


def _gelu_grad(x):
    return 0.5 * (1.0 + lax.erf(x * (1.0 / math.sqrt(2.0)))) + x * jnp.exp(-0.5 * x * x) * (1.0 / math.sqrt(2.0 * math.pi))


def _ln_fwd(r):
    mu = jnp.mean(r, axis=-1, keepdims=True)
    xc = r - mu
    var = jnp.mean(xc * xc, axis=-1, keepdims=True)
    rstd = lax.rsqrt(var + LN_EPS)
    return xc * rstd, rstd


def _ln_bwd(dy, xhat, rstd, g):
    dxh = dy * g
    m1 = jnp.mean(dxh, axis=-1, keepdims=True)
    m2 = jnp.mean(dxh * xhat, axis=-1, keepdims=True)
    return rstd * (dxh - m1 - xhat * m2)


def _dot(a, b):
    return jnp.dot(a.astype(MXU), b.astype(MXU), preferred_element_type=F32)


def _dot_nt(a, b):
    return lax.dot_general(a.astype(MXU), b.astype(MXU), (((1,), (1,)), ((), ())), preferred_element_type=F32)


def _dot_tn(a, b):
    return lax.dot_general(a.astype(MXU), b.astype(MXU), (((0,), (0,)), ((), ())), preferred_element_type=F32)


def _colsum(v):
    return jnp.sum(v, axis=0, keepdims=True)


def _rope_tables(positions, t, dep):
    inv = np.float32(ROPE_THETA) ** (-np.arange(0, ROPE_DIM, 2, dtype=np.float32) / np.float32(ROPE_DIM))
    half = ROPE_DIM // 2
    pos_rep = jnp.repeat(positions.reshape(t // 16, 16), half, axis=1)
    inv_row = jnp.asarray(np.tile(inv, 16)[None, :], F32)

    def trig_body(pos_ref, inv_ref, dep_ref, cos_ref, sin_ref):
        ang = pos_ref[...].astype(F32) * inv_ref[...]
        cos_ref[...] = jnp.cos(ang)
        sin_ref[...] = jnp.sin(ang)

    vm = pl.BlockSpec(memory_space=pltpu.VMEM)
    cos8, sin8 = pl.pallas_call(
        trig_body, name="rope_trig", in_specs=[vm, vm, pl.BlockSpec(memory_space=pl.ANY)], out_specs=[vm, vm],
        out_shape=(jax.ShapeDtypeStruct((t // 16, 128), F32), jax.ShapeDtypeStruct((t // 16, 128), F32)),
    )(pos_rep, inv_row, dep)
    cos8 = cos8.reshape(t, half)
    sin8 = sin8.reshape(t, half)

    lane = np.arange(128) % HEAD_DIM
    sel = (np.arange(half)[:, None] == (lane % half)[None, :])
    e_cos = (sel & (lane < ROPE_DIM)[None, :]).astype(np.float32)
    e_s1 = -(sel & (lane < half)[None, :]).astype(np.float32)
    e_s2 = (sel & ((lane >= half) & (lane < ROPE_DIM))[None, :]).astype(np.float32)
    ones = (lane >= ROPE_DIM).astype(np.float32)[None, :]

    def expand_body(cos_ref, sin_ref, ec_ref, e1_ref, e2_ref, ones_ref, c_ref, s1_ref, s2_ref):
        hp = lax.Precision.HIGHEST
        c_ref[...] = jnp.dot(cos_ref[...], ec_ref[...], precision=hp, preferred_element_type=F32) + ones_ref[...]
        s1_ref[...] = jnp.dot(sin_ref[...], e1_ref[...], precision=hp, preferred_element_type=F32)
        s2_ref[...] = jnp.dot(sin_ref[...], e2_ref[...], precision=hp, preferred_element_type=F32)

    tab = jax.ShapeDtypeStruct((t, 128), F32)
    return pl.pallas_call(expand_body, name="rope_expand", out_shape=(tab, tab, tab), compiler_params=_cp())(
        cos8, sin8, jnp.asarray(e_cos), jnp.asarray(e_s1), jnp.asarray(e_s2), jnp.asarray(ones))


def _tile_heads(tab):
    return jnp.concatenate([tab] * (D_ATTN // 128), axis=1)


def _rope_apply(v, c, s1, s2):
    n = v.shape[1]
    half = ROPE_DIM // 2
    return v * c + pltpu.roll(v, n - half, 1) * s1 + pltpu.roll(v, half, 1) * s2


def _rope_apply_t(g, c, s1, s2):
    n = g.shape[1]
    half = ROPE_DIM // 2
    return g * c + pltpu.roll(g * s1, half, 1) + pltpu.roll(g * s2, n - half, 1)


LANE_CHUNKS = D_ATTN // 128
HEAD_LANES = 128 // N_HEADS


def _perm_shape(t, d, w, dtype):
    return jax.ShapeDtypeStruct((d, t // d, w), dtype)


def _perm_tile(d, w):
    return pl.BlockSpec((None if d == 1 else d, TM // d, w), lambda i: (0, i, 0))


def _to_planes(ref, scr, d, n_chunks, dtype):
    for r in range(d):
        for cc in range(n_chunks):
            ref[r, :, cc * 128:(cc + 1) * 128] = scr.at[cc][pl.ds(r, TM // d, stride=d), :].astype(dtype)


def _from_planes(ref, scr, d, n_chunks, accumulate=False):
    for r in range(d):
        for cc in range(n_chunks):
            rows = scr.at[cc]
            val = ref[r, :, cc * 128:(cc + 1) * 128].astype(F32)
            if accumulate:
                rows[pl.ds(r, TM // d, stride=d), :] += val
            else:
                rows[pl.ds(r, TM // d, stride=d), :] = val


def _chunks(val):
    return [val[:, cc * 128:(cc + 1) * 128] for cc in range(val.shape[1] // 128)]


def _unchunk(scr, n_chunks, base=0):
    return jnp.concatenate([scr[base + cc] for cc in range(n_chunks)], axis=1)


def _head_expand():
    src = np.arange(128)[:, None]
    dst = np.arange(D_ATTN)[None, :]
    return jnp.asarray((src == (dst // HEAD_DIM) * HEAD_LANES).astype(np.float32))


def _head_reduce():
    src = np.arange(D_ATTN)[:, None]
    dst = np.arange(128)[None, :]
    return jnp.asarray((src // HEAD_DIM == dst // HEAD_LANES).astype(np.float32))


def _dot_select(a, sel):
    hi = a.astype(BF16)
    lo = (a - hi.astype(F32)).astype(BF16)
    sel = sel.astype(BF16)
    return jnp.dot(hi, sel, preferred_element_type=F32) + jnp.dot(lo, sel, preferred_element_type=F32)


def _qkvuz(x, w_in, c_tab, s1_tab, s2_tab, ln_z_g, ln_z_b, w_s, b_full, dep):
    t = x.shape[0]
    nchunk = TM // BLK

    def body(x_ref, w_ref, c_ref, s1_ref, s2_ref, g_ref, b_ref, ws_ref, bf_ref, dep_ref,
             qkv1_ref, qkv4_ref, qkv16_ref, hu_ref, hz_ref, mixed_ref, gm_ref, h_scr, wm_scr, p_scr):
        @pl.when(pl.program_id(0) == 0)
        def _():
            row = lax.broadcasted_iota(jnp.int32, (BLK, BLK), 0)
            col = lax.broadcasted_iota(jnp.int32, (BLK, BLK), 1)
            for g in range(N_HEADS):
                wm_scr[g] = jnp.where(col <= row, ws_ref[g], 0.0).astype(MXU)

        xb = x_ref[...].astype(MXU)
        for j in range(N_SHARD):
            h_scr[:, j * W_IN_BLK:(j + 1) * W_IN_BLK] = jnp.dot(xb, w_ref[j], preferred_element_type=F32)
        c, s1, s2 = _tile_heads(c_ref[...]), _tile_heads(s1_ref[...]), _tile_heads(s2_ref[...])
        q = _rope_apply(h_scr[:, 0:D_ATTN], c, s1, s2) * (1.0 / math.sqrt(HEAD_DIM))
        k = _rope_apply(h_scr[:, D_ATTN:2 * D_ATTN], c, s1, s2)
        for part, val in enumerate((q, k, h_scr[:, 2 * D_ATTN:3 * D_ATTN])):
            qkv1_ref[:, part * D_ATTN:(part + 1) * D_ATTN] = val.astype(MXU)
            for cc in range(LANE_CHUNKS):
                p_scr[part * LANE_CHUNKS + cc] = val[:, cc * 128:(cc + 1) * 128]
        _to_planes(qkv4_ref, p_scr, DILATIONS[1], 3 * LANE_CHUNKS, MXU)
        _to_planes(qkv16_ref, p_scr, DILATIONS[2], 3 * LANE_CHUNKS, MXU)
        hu = h_scr[:, 3 * D_ATTN:3 * D_ATTN + D_GMLP]
        hz = h_scr[:, 3 * D_ATTN + D_GMLP:]
        hu_ref[...] = hu
        hz_ref[...] = hz
        zhat, _ = _ln_fwd(_gelu(hz))
        zn = (zhat * g_ref[...] + b_ref[...]).astype(MXU)
        for ch in range(nchunk):
            rows = slice(ch * BLK, (ch + 1) * BLK)
            for g in range(N_HEADS):
                cols = slice(g * HEAD_DIM, (g + 1) * HEAD_DIM)
                mixed_ref[rows, cols] = jnp.dot(wm_scr[g], zn[rows, cols], preferred_element_type=F32) + bf_ref[:, cols]
        gm_ref[...] = (_gelu(hu) * mixed_ref[...]).astype(MXU)

    tok = lambda w: pl.BlockSpec((TM, w), lambda i: (i, 0))
    outs = [_perm_shape(t, d, 3 * D_ATTN, MXU) for d in DILATIONS] + [jax.ShapeDtypeStruct((t, D_GMLP), F32)] * 3 + [
        jax.ShapeDtypeStruct((t, D_GMLP), MXU)]
    return pl.pallas_call(
        body, name="qkvuz", grid=(t // TM,),
        in_specs=[tok(D_MODEL), _full(w_in.shape), tok(128), tok(128), tok(128), _full(ln_z_g.shape), _full(ln_z_b.shape),
                  _full(w_s.shape), _full(b_full.shape), pl.BlockSpec(memory_space=pl.ANY)],
        out_specs=[_perm_tile(d, 3 * D_ATTN) for d in DILATIONS] + [tok(D_ATTN)] * 4, out_shape=outs,
        scratch_shapes=[pltpu.VMEM((TM, D_IN), F32), pltpu.VMEM((N_HEADS, BLK, BLK), MXU),
                        pltpu.VMEM((3 * LANE_CHUNKS, TM, 128), F32)],
        compiler_params=_cp(dimension_semantics=("arbitrary",)),
    )(x, w_in, c_tab, s1_tab, s2_tab, ln_z_g, ln_z_b, w_s, b_full, dep)


def _band_valid(n):
    i = lax.broadcasted_iota(jnp.int32, (BLK, 2 * BLK), 0)
    j = lax.broadcasted_iota(jnp.int32, (BLK, 2 * BLK), 1)
    return (j >= i) & (j <= i + BLK) & ((j >= BLK) | (n > 0))


def _attn_fwd(qkv, d, dep):
    _, l_sub, _ = qkv.shape
    nb = l_sub // BLK

    def body(q_ref, kp_ref, kc_ref, vp_ref, vc_ref, dep_ref, o_ref, l_ref):
        valid = _band_valid(pl.program_id(1))
        kcat = jnp.concatenate([kp_ref[...], kc_ref[...]], axis=0)
        vcat = jnp.concatenate([vp_ref[...], vc_ref[...]], axis=0)
        for h in range(N_HEADS):
            cols = slice(h * HEAD_DIM, (h + 1) * HEAD_DIM)
            s = jnp.where(valid, _dot_nt(q_ref[:, cols], kcat[:, cols]), NEG_INF)
            m = jnp.max(s, axis=-1, keepdims=True)
            e = jnp.exp(s - m)
            den = jnp.sum(e, axis=-1, keepdims=True)
            o_ref[:, cols] = _dot(e, vcat[:, cols]) * (1.0 / den)
            l_ref[:, h * HEAD_LANES:(h + 1) * HEAD_LANES] = jnp.broadcast_to(m + jnp.log(den), (BLK, HEAD_LANES))

    def blk(w, col, prev=False):
        return pl.BlockSpec((None, BLK, w), lambda r, n: (r, jnp.maximum(n - 1, 0) if prev else n, col))

    return pl.pallas_call(
        body, name=f"attn_fwd_d{d}", grid=(d, nb),
        in_specs=[blk(D_ATTN, 0), blk(D_ATTN, 1, True), blk(D_ATTN, 1), blk(D_ATTN, 2, True), blk(D_ATTN, 2),
                  pl.BlockSpec(memory_space=pl.ANY)],
        out_specs=[blk(D_ATTN, 0), blk(128, 0)],
        out_shape=[jax.ShapeDtypeStruct((d, l_sub, D_ATTN), F32), jax.ShapeDtypeStruct((d, l_sub, 128), F32)],
        compiler_params=_cp(dimension_semantics=("arbitrary", "arbitrary")),
    )(qkv, qkv, qkv, qkv, qkv, dep)


def _attn_bwd(qkv, do, lse, delta, d, dep):
    _, l_sub, _ = qkv.shape
    nb = l_sub // BLK
    whole = l_sub <= 8 * BLK

    def shares(n, q_ref, kp_ref, kc_ref, vp_ref, vc_ref, do_ref, l_ref, dl_ref, dq_ref):
        valid = _band_valid(n)
        kcat = jnp.concatenate([kp_ref[...], kc_ref[...]], axis=0)
        vcat = jnp.concatenate([vp_ref[...], vc_ref[...]], axis=0)
        for h in range(N_HEADS):
            cols = slice(h * HEAD_DIM, (h + 1) * HEAD_DIM)
            stat = slice(h * HEAD_LANES, h * HEAD_LANES + 1)
            qh, doh = q_ref[:, cols], do_ref[:, cols]
            p = jnp.where(valid, jnp.exp(_dot_nt(qh, kcat[:, cols]) - l_ref[:, stat]), 0.0)
            ds = p * (_dot_nt(doh, vcat[:, cols]) - dl_ref[:, stat])
            dq_ref[:, cols] = _dot(ds, kcat[:, cols])
            yield cols, _dot_tn(ds, qh), _dot_tn(p, doh)

    def body_whole(*refs):
        dk_ref, dv_ref = refs[10:]
        n = pl.program_id(1)
        cur = pl.ds(pl.multiple_of(n * BLK, BLK), BLK)
        prev = pl.ds(pl.multiple_of(jnp.maximum(n - 1, 0) * BLK, BLK), BLK)
        for cols, dk2, dv2 in shares(n, *refs[:8], refs[9]):
            dk_ref[cur, cols] = dk2[BLK:]
            dv_ref[cur, cols] = dv2[BLK:]
            dk_ref[prev, cols] += dk2[0:BLK]
            dv_ref[prev, cols] += dv2[0:BLK]

    def body_carry(*refs):
        dk_ref, dv_ref, ck_scr, cv_scr = refs[10:]
        n = pl.program_id(1)

        @pl.when(n == 0)
        def _():
            ck_scr[...] = jnp.zeros_like(ck_scr)
            cv_scr[...] = jnp.zeros_like(cv_scr)

        @pl.when(n < nb)
        def _():
            for cols, dk2, dv2 in shares(n, *refs[:8], refs[9]):
                dk_ref[:, cols] = ck_scr[:, cols] + dk2[0:BLK]
                dv_ref[:, cols] = cv_scr[:, cols] + dv2[0:BLK]
                ck_scr[:, cols] = dk2[BLK:]
                cv_scr[:, cols] = dv2[BLK:]

        @pl.when(n == nb)
        def _():
            dk_ref[...] = ck_scr[...]
            dv_ref[...] = cv_scr[...]

    def blk(w, col, shift=0):
        return pl.BlockSpec((None, BLK, w), lambda r, n: (r, jnp.clip(n - shift, 0, nb - 1), col))

    if whole:
        dkv_spec = pl.BlockSpec((None, l_sub, D_ATTN), lambda r, n: (r, 0, 0))
        body, steps, scratch = body_whole, nb, []
    else:
        dkv_spec = blk(D_ATTN, 0, 1)
        body, steps, scratch = body_carry, nb + 1, [pltpu.VMEM((BLK, D_ATTN), F32)] * 2
    return pl.pallas_call(
        body, name=f"attn_bwd_d{d}", grid=(d, steps),
        in_specs=[blk(D_ATTN, 0), blk(D_ATTN, 1, 1), blk(D_ATTN, 1), blk(D_ATTN, 2, 1), blk(D_ATTN, 2),
                  blk(D_ATTN, 0), blk(128, 0), blk(128, 0), pl.BlockSpec(memory_space=pl.ANY)],
        out_specs=[blk(D_ATTN, 0), dkv_spec, dkv_spec],
        out_shape=[jax.ShapeDtypeStruct((d, l_sub, D_ATTN), F32)] * 3,
        scratch_shapes=scratch,
        compiler_params=_cp(dimension_semantics=("arbitrary", "arbitrary")),
    )(qkv, qkv, qkv, qkv, qkv, do, lse, delta, dep)


def _mix_ln1(os_, ls_, gm, x, w_o, ln1_g, ln1_b, dep):
    t = x.shape[0]
    expand = _head_expand()

    def body(o1, o4, o16, l1, l4, l16, gm_ref, x_ref, wo_ref, g_ref, b_ref, ex_ref, dep_ref,
             attn_ref, lse1_ref, lse4_ref, lse16_ref, cat_ref, xhat_ref, rstd_ref, x1b_ref, o_scr, l_scr):
        _from_planes(o4, o_scr, DILATIONS[1], LANE_CHUNKS)
        _from_planes(o16, o_scr.at[pl.ds(LANE_CHUNKS, LANE_CHUNKS)], DILATIONS[2], LANE_CHUNKS)
        _from_planes(l4, l_scr, DILATIONS[1], 1)
        _from_planes(l16, l_scr.at[pl.ds(1, 1)], DILATIONS[2], 1)
        la, lb, lc = l1[...], l_scr[0], l_scr[1]
        m = jnp.maximum(jnp.maximum(la, lb), lc)
        ea, eb, ec = jnp.exp(la - m), jnp.exp(lb - m), jnp.exp(lc - m)
        den = ea + eb + ec
        inv = 1.0 / den
        wide = lambda w: _dot_select(w, ex_ref[...])
        attn = (wide(ea * inv) * o1[...] + wide(eb * inv) * _unchunk(o_scr, LANE_CHUNKS)
                + wide(ec * inv) * _unchunk(o_scr, LANE_CHUNKS, LANE_CHUNKS))
        attn_ref[...] = attn
        lse = m + jnp.log(den)
        lse1_ref[...] = lse
        l_scr[2] = lse
        _to_planes(lse4_ref, l_scr.at[pl.ds(2, 1)], DILATIONS[1], 1, F32)
        _to_planes(lse16_ref, l_scr.at[pl.ds(2, 1)], DILATIONS[2], 1, F32)
        cat_ref[:, 0:D_ATTN] = attn.astype(MXU)
        cat_ref[:, D_ATTN:] = gm_ref[...]
        mix = jnp.dot(cat_ref[...], wo_ref[...], preferred_element_type=F32)
        xhat, rstd = _ln_fwd(ALPHA * x_ref[...] + mix)
        xhat_ref[...] = xhat
        rstd_ref[...] = rstd
        x1b_ref[...] = (xhat * g_ref[...] + b_ref[...]).astype(MXU)

    tok = lambda w: pl.BlockSpec((TM, w), lambda i: (i, 0))
    outs = [jax.ShapeDtypeStruct((t, D_ATTN), F32)] + [_perm_shape(t, d, 128, F32) for d in DILATIONS] + [
        jax.ShapeDtypeStruct((t, D_MODEL), MXU), jax.ShapeDtypeStruct((t, D_MODEL), F32), jax.ShapeDtypeStruct((t, 1), F32),
        jax.ShapeDtypeStruct((t, D_MODEL), MXU)]
    return pl.pallas_call(
        body, name="mix_ln1", grid=(t // TM,),
        in_specs=[_perm_tile(d, D_ATTN) for d in DILATIONS] + [_perm_tile(d, 128) for d in DILATIONS]
        + [tok(D_GMLP), tok(D_MODEL), _full(w_o.shape), _full(ln1_g.shape), _full(ln1_b.shape), _full(expand.shape),
           pl.BlockSpec(memory_space=pl.ANY)],
        out_specs=[tok(D_ATTN)] + [_perm_tile(d, 128) for d in DILATIONS] + [tok(D_MODEL), tok(D_MODEL), tok(1), tok(D_MODEL)],
        out_shape=outs,
        scratch_shapes=[pltpu.VMEM((2 * LANE_CHUNKS, TM, 128), F32), pltpu.VMEM((3, TM, 128), F32)],
        compiler_params=_cp(dimension_semantics=("arbitrary",)),
    )(*os_, *ls_, gm, x, w_o, ln1_g, ln1_b, expand, dep)


def _conv_fwd(a_ext, w_ref, b_ref, rows):
    return (b_ref[...] + w_ref[2:3, :] * a_ext[HALO:HALO + rows] + w_ref[1:2, :] * a_ext[HALO - 1:HALO - 1 + rows]
            + w_ref[0:1, :] * a_ext[HALO - 2:HALO - 2 + rows])


def _ffn_in(x1b, w_a, w_b, conv_w, conv_b):
    t = x1b.shape[0]
    hb = TM // HALO

    def body(x_ref, xh_ref, wa_ref, wb_ref, cw_ref, cb_ref, apre_ref, a_ref, b_ref, f_ref):
        i = pl.program_id(1)
        a_pre = _dot_nt(x_ref[...], wa_ref[...])
        a_halo = jnp.where(i > 0, _dot_nt(xh_ref[...], wa_ref[...]), 0.0)
        a = _conv_fwd(jnp.concatenate([a_halo, a_pre], axis=0), cw_ref, cb_ref, TM)
        b = _dot_nt(x_ref[...], wb_ref[...])
        apre_ref[...] = a_pre
        a_ref[...] = a
        b_ref[...] = b
        f_ref[...] = (_gelu(a) * b).astype(MXU)

    blk = lambda r, c: pl.BlockSpec((None, r, c), lambda j, i: (j, 0, 0))
    tokj = pl.BlockSpec((None, TM, FF_BLK), lambda j, i: (j, i, 0))
    outs = [jax.ShapeDtypeStruct((N_SHARD, t, FF_BLK), F32)] * 3 + [jax.ShapeDtypeStruct((N_SHARD, t, FF_BLK), MXU)]
    return pl.pallas_call(
        body, name="ffn_in", grid=(N_SHARD, t // TM),
        in_specs=[pl.BlockSpec((TM, D_MODEL), lambda j, i: (i, 0)),
                  pl.BlockSpec((HALO, D_MODEL), lambda j, i: (jnp.maximum(i * hb - 1, 0), 0)),
                  blk(FF_BLK, D_MODEL), blk(FF_BLK, D_MODEL), blk(3, FF_BLK), blk(1, FF_BLK)],
        out_specs=[tokj, tokj, tokj, tokj], out_shape=outs,
        compiler_params=_cp(dimension_semantics=("arbitrary", "arbitrary")),
    )(x1b, x1b, w_a, w_b, conv_w, conv_b)


def _ffn_out_ln2(f, w_down, xhat1, ln1_g, ln1_b, ln2_g, ln2_b):
    t = xhat1.shape[0]

    def body(f_ref, wd_ref, xh_ref, g1_ref, b1_ref, g2_ref, b2_ref, xhat_ref, rstd_ref, x2b_ref):
        ff = jnp.dot(f_ref[0], wd_ref[0], preferred_element_type=F32)
        for j in range(1, N_SHARD):
            ff = ff + jnp.dot(f_ref[j], wd_ref[j], preferred_element_type=F32)
        x1 = xh_ref[...] * g1_ref[...] + b1_ref[...]
        xhat, rstd = _ln_fwd(ALPHA * x1 + ff)
        xhat_ref[...] = xhat
        rstd_ref[...] = rstd
        x2b_ref[...] = (xhat * g2_ref[...] + b2_ref[...]).astype(MXU)

    tok = lambda w: pl.BlockSpec((TM, w), lambda i: (i, 0))
    vec = _full((1, D_MODEL))
    outs = [jax.ShapeDtypeStruct((t, D_MODEL), F32), jax.ShapeDtypeStruct((t, 1), F32), jax.ShapeDtypeStruct((t, D_MODEL), MXU)]
    return pl.pallas_call(
        body, name="ffn_out_ln2", grid=(t // TM,),
        in_specs=[pl.BlockSpec((N_SHARD, TM, FF_BLK), lambda i: (0, i, 0)), _full(w_down.shape), tok(D_MODEL), vec, vec, vec, vec],
        out_specs=[tok(D_MODEL), tok(1), tok(D_MODEL)], out_shape=outs,
        compiler_params=_cp(dimension_semantics=("arbitrary",)),
    )(f, w_down, xhat1, ln1_g, ln1_b, ln2_g, ln2_b)


STAT_ROWS = 8


def _ple_loss_bwd(xhat2, rstd2, p, target, ln2_g, ln2_b, w_g, b_g, w_p, ln3_g, ln3_b):
    t = xhat2.shape[0]

    def body(xh2_ref, rs2_ref, p_ref, t_ref, g2_ref, b2_ref, wg_ref, bg_ref, wp_ref, g3_ref, b3_ref,
             dr2_ref, dgp_ref, dpp_ref, stat_ref, pp_scr):
        @pl.when(pl.program_id(0) == 0)
        def _():
            stat_ref[...] = jnp.zeros_like(stat_ref)

        xhat2 = xh2_ref[...]
        x2 = xhat2 * g2_ref[...] + b2_ref[...]
        gate = jax.nn.sigmoid(jnp.dot(x2.astype(MXU), wg_ref[...], preferred_element_type=F32) + bg_ref[...])
        pb = p_ref[...].astype(MXU)
        for j in range(N_SHARD):
            pp_scr[:, j * ROW_BLK:(j + 1) * ROW_BLK] = jnp.dot(pb, wp_ref[j], preferred_element_type=F32)
        pp = pp_scr[...]
        xhat3, rstd3 = _ln_fwd(ALPHA * x2 + gate * pp)
        err = xhat3 * g3_ref[...] + b3_ref[...] - t_ref[...]
        dy = err * (1.0 / D_MODEL)
        dr3 = _ln_bwd(dy, xhat3, rstd3, g3_ref[...])
        dgp = dr3 * pp * gate * (1.0 - gate)
        dgp_ref[...] = dgp.astype(MXU)
        dpp_ref[...] = (dr3 * gate).astype(MXU)
        dx2 = ALPHA * dr3 + _dot_nt(dgp, wg_ref[...])
        dr2_ref[...] = _ln_bwd(dx2, xhat2, rs2_ref[...], g2_ref[...])
        stat_ref[0:1, :] += _colsum(dy * xhat3)
        stat_ref[1:2, :] += _colsum(dy)
        stat_ref[2:3, :] += _colsum(dgp)
        stat_ref[3:4, :] += _colsum(dx2 * xhat2)
        stat_ref[4:5, :] += _colsum(dx2)
        stat_ref[5:6, :] += _colsum(err * err)

    tok = lambda w: pl.BlockSpec((TM, w), lambda i: (i, 0))
    vec = _full((1, D_MODEL))
    outs = [jax.ShapeDtypeStruct((t, D_MODEL), F32), jax.ShapeDtypeStruct((t, D_MODEL), MXU), jax.ShapeDtypeStruct((t, D_MODEL), MXU),
            jax.ShapeDtypeStruct((STAT_ROWS, D_MODEL), F32)]
    return pl.pallas_call(
        body, name="ple_loss_bwd", grid=(t // TM,),
        in_specs=[tok(D_MODEL), tok(1), tok(D_PLE), tok(D_MODEL), vec, vec, _full(w_g.shape), vec, _full(w_p.shape), vec, vec],
        out_specs=[tok(D_MODEL), tok(D_MODEL), tok(D_MODEL), _full((STAT_ROWS, D_MODEL))], out_shape=outs,
        scratch_shapes=[pltpu.VMEM((TM, D_MODEL), F32)],
        compiler_params=_cp(dimension_semantics=("arbitrary",)),
    )(xhat2, rstd2, p, target, ln2_g, ln2_b, w_g, b_g, w_p, ln3_g, ln3_b)


def _ffn_bwd(dr2, a_pre, a, b, w_down, w_a, w_b, conv_w, xhat1, rstd1, ln1_g):
    t = dr2.shape[0]
    nt = t // TM
    hb = TM // HALO
    last_h = t // HALO - 1

    def body(dr_ref, drn_ref, ap_ref, a_ref, an_ref, b_ref, bn_ref, wd_ref, wa_ref, wb_ref, cw_ref,
             xh_ref, rs_ref, g1_ref, dap_ref, dbb_ref, dr1_ref, cstat_ref, lstat_ref, acc_scr):
        i, j = pl.program_id(0), pl.program_id(1)

        @pl.when((i == 0) & (j == 0))
        def _():
            cstat_ref[...] = jnp.zeros_like(cstat_ref)
            lstat_ref[...] = jnp.zeros_like(lstat_ref)

        half = TM // ROW_GROUPS
        parts = []
        for r0 in range(0, TM, half):
            rows = pl.ds(r0, half)
            last = r0 + half == TM

            def ext(ref, nxt):
                return jnp.concatenate([ref[rows], nxt[...]], axis=0) if last else ref[r0:r0 + half + HALO]

            df = _dot_nt(ext(dr_ref, drn_ref), wd_ref[...])
            a_ext, b_ext = ext(a_ref, an_ref), ext(b_ref, bn_ref)
            cdf = 0.5 * (1.0 + lax.erf(a_ext * (1.0 / math.sqrt(2.0))))
            pdf = jnp.exp(-0.5 * a_ext * a_ext) * (1.0 / math.sqrt(2.0 * math.pi))
            da = df * b_ext * (cdf + a_ext * pdf)
            if last:
                da = jnp.concatenate([da[0:half], jnp.where(i < nt - 1, da[half:], 0.0)], axis=0)
            ahead = [da[s:s + half] for s in range(3)]
            da_pre = cw_ref[2:3, :] * ahead[0] + cw_ref[1:2, :] * ahead[1] + cw_ref[0:1, :] * ahead[2]
            dbb = df[0:half] * (a_ext[0:half] * cdf[0:half])
            dap_ref[rows, :] = da_pre.astype(MXU)
            dbb_ref[rows, :] = dbb.astype(MXU)
            for kk in range(3):
                cstat_ref[j, kk:kk + 1, :] += _colsum(ahead[2 - kk] * ap_ref[rows, :])
            cstat_ref[j, 3:4, :] += _colsum(ahead[0])
            parts.append(_dot(da_pre, wa_ref[...]) + _dot(dbb, wb_ref[...]))
        part = jnp.concatenate(parts, axis=0)

        @pl.when(j == 0)
        def _():
            acc_scr[...] = ALPHA * dr_ref[...] + part

        @pl.when(j > 0)
        def _():
            acc_scr[...] += part

        @pl.when(j == N_SHARD - 1)
        def _():
            dx1 = acc_scr[...]
            xhat1 = xh_ref[...]
            lstat_ref[0:1, :] += _colsum(dx1 * xhat1)
            lstat_ref[1:2, :] += _colsum(dx1)
            dr1_ref[...] = _ln_bwd(dx1, xhat1, rs_ref[...], g1_ref[...])

    tok = lambda w: pl.BlockSpec((TM, w), lambda i, j: (i, 0))
    tokj = pl.BlockSpec((None, TM, FF_BLK), lambda i, j: (j, i, 0))
    nextj = pl.BlockSpec((None, HALO, FF_BLK), lambda i, j: (j, jnp.minimum((i + 1) * hb, last_h), 0))
    blk = lambda r, c: pl.BlockSpec((None, r, c), lambda i, j: (j, 0, 0))
    outs = [jax.ShapeDtypeStruct((N_SHARD, t, FF_BLK), MXU)] * 2 + [
        jax.ShapeDtypeStruct((t, D_MODEL), F32), jax.ShapeDtypeStruct((N_SHARD, STAT_ROWS, FF_BLK), F32),
        jax.ShapeDtypeStruct((STAT_ROWS, D_MODEL), F32)]
    return pl.pallas_call(
        body, name="ffn_bwd", grid=(nt, N_SHARD),
        in_specs=[tok(D_MODEL), pl.BlockSpec((HALO, D_MODEL), lambda i, j: (jnp.minimum((i + 1) * hb, last_h), 0)),
                  tokj, tokj, nextj, tokj, nextj, blk(FF_BLK, D_MODEL), blk(FF_BLK, D_MODEL), blk(FF_BLK, D_MODEL),
                  blk(3, FF_BLK), tok(D_MODEL), tok(1), _full((1, D_MODEL))],
        out_specs=[tokj, tokj, tok(D_MODEL), _full((N_SHARD, STAT_ROWS, FF_BLK)), _full((STAT_ROWS, D_MODEL))], out_shape=outs,
        scratch_shapes=[pltpu.VMEM((TM, D_MODEL), F32)],
        compiler_params=_cp(dimension_semantics=("arbitrary", "arbitrary")),
    )(dr2, dr2, a_pre, a, a, b, b, w_down, w_a, w_b, conv_w, xhat1, rstd1, ln1_g)


def _mix_bwd(dr1, w_o, hu, hz, mixed, attn, ln_z_g, ln_z_b, w_s, dep):
    t = dr1.shape[0]
    nchunk = TM // BLK

    def body(dr_ref, wo_ref, hu_ref, hz_ref, mx_ref, attn_ref, g_ref, b_ref, ws_ref, grp_ref, red_ref, dep_ref,
             do1_ref, do4_ref, do16_ref, dl1_ref, dl4_ref, dl16_ref, duz_ref, dws_ref, dbs_ref, zstat_ref,
             wm_scr, dzn_scr, dbsum_scr, do_scr, dl_scr):
        @pl.when(pl.program_id(0) == 0)
        def _():
            row = lax.broadcasted_iota(jnp.int32, (BLK, BLK), 0)
            col = lax.broadcasted_iota(jnp.int32, (BLK, BLK), 1)
            for g in range(N_HEADS):
                wm_scr[g] = jnp.where(col <= row, ws_ref[g], 0.0).astype(MXU)
            dws_ref[...] = jnp.zeros_like(dws_ref)
            dbsum_scr[...] = jnp.zeros_like(dbsum_scr)
            zstat_ref[...] = jnp.zeros_like(zstat_ref)

        dcat = _dot_nt(dr_ref[...], wo_ref[...])
        dattn = dcat[:, 0:D_ATTN]
        do1_ref[...] = dattn.astype(MXU)
        for cc, val in enumerate(_chunks(dattn)):
            do_scr[cc] = val
        _to_planes(do4_ref, do_scr, DILATIONS[1], LANE_CHUNKS, MXU)
        _to_planes(do16_ref, do_scr, DILATIONS[2], LANE_CHUNKS, MXU)
        delta = _dot_select(dattn * attn_ref[...], red_ref[...])
        dl1_ref[...] = delta
        dl_scr[0] = delta
        _to_planes(dl4_ref, dl_scr, DILATIONS[1], 1, F32)
        _to_planes(dl16_ref, dl_scr, DILATIONS[2], 1, F32)
        dgm = dcat[:, D_ATTN:]
        hu, hz = hu_ref[...], hz_ref[...]
        u = _gelu(hu)
        duz_ref[:, 0:D_GMLP] = (dgm * mx_ref[...] * _gelu_grad(hu)).astype(MXU)
        dmixed = dgm * u
        dmb = dmixed.astype(MXU)
        zhat, rstd = _ln_fwd(_gelu(hz))
        znb = (zhat * g_ref[...] + b_ref[...]).astype(MXU)
        dbs_acc = jnp.zeros((BLK, D_GMLP), F32)
        for ch in range(nchunk):
            rows = slice(ch * BLK, (ch + 1) * BLK)
            dbs_acc = dbs_acc + dmixed[rows]
            for g in range(N_HEADS):
                cols = slice(g * HEAD_DIM, (g + 1) * HEAD_DIM)
                dzn_scr[rows, cols] = _dot_tn(wm_scr[g], dmb[rows, cols])
                dws_ref[g] += _dot_nt(dmb[rows, cols], znb[rows, cols])
        dbsum_scr[...] += dbs_acc
        dzn = dzn_scr[...]
        zstat_ref[0:1, :] += _colsum(dzn * zhat)
        zstat_ref[1:2, :] += _colsum(dzn)
        duz_ref[:, D_GMLP:] = (_ln_bwd(dzn, zhat, rstd, g_ref[...]) * _gelu_grad(hz)).astype(MXU)

        @pl.when(pl.program_id(0) == nt - 1)
        def _():
            row = lax.broadcasted_iota(jnp.int32, (BLK, BLK), 0)
            col = lax.broadcasted_iota(jnp.int32, (BLK, BLK), 1)
            for g in range(N_HEADS):
                dws_ref[g] = jnp.where(col <= row, dws_ref[g], 0.0)
            dbs_ref[...] = lax.dot_general(grp_ref[...], dbsum_scr[...], (((1,), (1,)), ((), ())),
                                           precision=lax.Precision.HIGHEST, preferred_element_type=F32)

    nt = t // TM
    tok = lambda w: pl.BlockSpec((TM, w), lambda i: (i, 0))
    grp = jnp.asarray((np.arange(D_GMLP)[None, :] // HEAD_DIM == np.arange(N_HEADS)[:, None]).astype(np.float32))
    red = _head_reduce()
    outs = [_perm_shape(t, d, D_ATTN, MXU) for d in DILATIONS] + [_perm_shape(t, d, 128, F32) for d in DILATIONS] + [
        jax.ShapeDtypeStruct((t, 2 * D_GMLP), MXU),
        jax.ShapeDtypeStruct((N_HEADS, BLK, BLK), F32), jax.ShapeDtypeStruct((N_HEADS, BLK), F32),
        jax.ShapeDtypeStruct((STAT_ROWS, D_GMLP), F32)]
    return pl.pallas_call(
        body, name="mix_bwd", grid=(t // TM,),
        in_specs=[tok(D_MODEL), _full(w_o.shape), tok(D_GMLP), tok(D_GMLP), tok(D_GMLP), tok(D_ATTN), _full(ln_z_g.shape),
                  _full(ln_z_b.shape), _full(w_s.shape), _full(grp.shape), _full(red.shape), pl.BlockSpec(memory_space=pl.ANY)],
        out_specs=[_perm_tile(d, D_ATTN) for d in DILATIONS] + [_perm_tile(d, 128) for d in DILATIONS]
        + [tok(2 * D_GMLP), _full((N_HEADS, BLK, BLK)), _full((N_HEADS, BLK)), _full((STAT_ROWS, D_GMLP))],
        out_shape=outs,
        scratch_shapes=[pltpu.VMEM((N_HEADS, BLK, BLK), MXU), pltpu.VMEM((TM, D_GMLP), F32), pltpu.VMEM((BLK, D_GMLP), F32),
                        pltpu.VMEM((LANE_CHUNKS, TM, 128), F32), pltpu.VMEM((1, TM, 128), F32)],
        compiler_params=_cp(dimension_semantics=("arbitrary",)),
    )(dr1, w_o, hu, hz, mixed, attn, ln_z_g, ln_z_b, w_s, grp, red, dep)


def _dx_in(dqs, dks, dvs, duz, dr1, w_in, c_tab, s1_tab, s2_tab):
    t = dr1.shape[0]

    def body(dq1, dq4, dq16, dk1, dk4, dk16, dv1, dv4, dv16, duz_ref, dr_ref, w_ref, c_ref, s1_ref, s2_ref,
             dh_ref, dx_ref, acc_scr):
        sums = []
        for part, (g1, g4, g16) in enumerate(((dq1, dq4, dq16), (dk1, dk4, dk16), (dv1, dv4, dv16))):
            acc = acc_scr.at[pl.ds(part * LANE_CHUNKS, LANE_CHUNKS)]
            for cc in range(LANE_CHUNKS):
                acc[cc] = g1[:, cc * 128:(cc + 1) * 128]
            _from_planes(g4, acc, DILATIONS[1], LANE_CHUNKS, accumulate=True)
            _from_planes(g16, acc, DILATIONS[2], LANE_CHUNKS, accumulate=True)
            sums.append(_unchunk(acc_scr, LANE_CHUNKS, part * LANE_CHUNKS))
        c, s1, s2 = _tile_heads(c_ref[...]), _tile_heads(s1_ref[...]), _tile_heads(s2_ref[...])
        dh_ref[:, 0:D_ATTN] = _rope_apply_t(sums[0] * (1.0 / math.sqrt(HEAD_DIM)), c, s1, s2).astype(MXU)
        dh_ref[:, D_ATTN:2 * D_ATTN] = _rope_apply_t(sums[1], c, s1, s2).astype(MXU)
        dh_ref[:, 2 * D_ATTN:3 * D_ATTN] = sums[2].astype(MXU)
        dh_ref[:, 3 * D_ATTN:] = duz_ref[...]
        dx = ALPHA * dr_ref[...]
        for j in range(N_SHARD):
            dx = dx + _dot_nt(dh_ref[:, j * W_IN_BLK:(j + 1) * W_IN_BLK], w_ref[j])
        dx_ref[...] = dx

    tok = lambda w: pl.BlockSpec((TM, w), lambda i: (i, 0))
    outs = [jax.ShapeDtypeStruct((t, D_IN), MXU), jax.ShapeDtypeStruct((t, D_MODEL), F32)]
    return pl.pallas_call(
        body, name="dx_in", grid=(t // TM,),
        in_specs=[_perm_tile(d, D_ATTN) for d in DILATIONS] * 3
        + [tok(2 * D_GMLP), tok(D_MODEL), _full(w_in.shape), tok(128), tok(128), tok(128)],
        out_specs=[tok(D_IN), tok(D_MODEL)], out_shape=outs,
        scratch_shapes=[pltpu.VMEM((3 * LANE_CHUNKS, TM, 128), F32)],
        compiler_params=_cp(dimension_semantics=("arbitrary",)),
    )(*dqs, *dks, *dvs, duz, dr1, w_in, c_tab, s1_tab, s2_tab)


def _wgrad(name, x, dy, x_spec, dy_spec, out_spec, out_shape, grid):
    def body(x_ref, dy_ref, o_ref):
        o_ref[...] = _dot_tn(x_ref[...], dy_ref[...])

    return pl.pallas_call(
        body, name=name, grid=grid, in_specs=[x_spec, dy_spec], out_specs=out_spec,
        out_shape=jax.ShapeDtypeStruct(out_shape, F32),
        compiler_params=_cp(dimension_semantics=("arbitrary",) * len(grid)),
    )(x, dy)


def _wgrad_pair(name, xa, xb, dy, x_spec, dy_spec, out_spec, out_shape, grid):
    def body(xa_ref, xb_ref, dy_ref, oa_ref, ob_ref):
        dy = dy_ref[...]
        oa_ref[...] = _dot_tn(xa_ref[...], dy)
        ob_ref[...] = _dot_tn(xb_ref[...], dy)

    return pl.pallas_call(
        body, name=name, grid=grid, in_specs=[x_spec, x_spec, dy_spec], out_specs=[out_spec, out_spec],
        out_shape=[jax.ShapeDtypeStruct(out_shape, F32)] * 2,
        compiler_params=_cp(dimension_semantics=("arbitrary",) * len(grid)),
    )(xa, xb, dy)


def _local_step(x, p, rope, target, w_in, start_dep, late_landed, late_weights, early_grads, early_grads_sent,
                ln_z_g, ln_z_b, w_s, b_s, ln1_g, ln1_b, conv_b, ln2_g, ln2_b, b_g, ln3_g, ln3_b):
    t = x.shape[0]
    half = TM
    c_tab, s1_tab, s2_tab = rope
    b_full = jnp.repeat(jnp.transpose(b_s[0]), HEAD_DIM, axis=1)
    conv_b4 = conv_b.reshape(N_SHARD, 1, FF_BLK)
    *qkvs, hu, hz, mixed, gm = _qkvuz(x, w_in, c_tab, s1_tab, s2_tab, ln_z_g, ln_z_b, w_s[0], b_full, start_dep)
    branches = [_attn_fwd(qkv, d, start_dep) for qkv, d in zip(qkvs[:2], DILATIONS[:2])]
    dep = late_landed(branches[-1][1])
    branches.append(_attn_fwd(qkvs[2], DILATIONS[2], dep))
    w_o, w_a, w_b, conv_w, w_down, w_g, w_p = late_weights(branches[-1][1])
    attn, *lses, cat, xhat1, rstd1, x1b = _mix_ln1(
        [o for o, _ in branches], [l for _, l in branches], gm, x, w_o, ln1_g, ln1_b, dep)
    a_pre, a_act, b_act, f = _ffn_in(x1b, w_a, w_b, conv_w, conv_b4)
    xhat2, rstd2, x2b = _ffn_out_ln2(f, w_down, xhat1, ln1_g, ln1_b, ln2_g, ln2_b)
    dr2, dgp, dpp, stat3 = _ple_loss_bwd(xhat2, rstd2, p, target, ln2_g, ln2_b, w_g, b_g, w_p, ln3_g, ln3_b)
    da_pre, dbb, dr1, cstat, stat1 = _ffn_bwd(dr2, a_pre, a_act, b_act, w_down, w_a, w_b, conv_w, xhat1, rstd1, ln1_g)

    full_t = lambda w, im: pl.BlockSpec((t, w), im)
    ffj = pl.BlockSpec((None, t, FF_BLK), lambda j, kk: (j, 0, 0))
    early = dict(
        w_ple_gate=_wgrad("dw_g", x2b, dgp, full_t(half, lambda kk, n: (0, kk)), full_t(half, lambda kk, n: (0, n)),
                          pl.BlockSpec((half, half), lambda kk, n: (kk, n)), (D_MODEL, D_MODEL), (2, 2)),
        w_ple_in=_wgrad("dw_p", p, dpp, full_t(D_PLE, lambda j: (0, 0)), full_t(ROW_BLK, lambda j: (0, j)),
                        pl.BlockSpec((None, D_PLE, ROW_BLK), lambda j: (j, 0, 0)), (N_SHARD, D_PLE, ROW_BLK), (N_SHARD,)),
        w_ff_down=_wgrad("dw_down", f, dr2, ffj, full_t(half, lambda j, n: (0, n)),
                         pl.BlockSpec((None, FF_BLK, half), lambda j, n: (j, 0, n)), (N_SHARD, FF_BLK, D_MODEL), (N_SHARD, 2)),
        **dict(zip(("w_ff_a", "w_ff_b"), _wgrad_pair(
            "dw_ab", da_pre, dbb, x1b, ffj, full_t(half, lambda j, n: (0, n)),
            pl.BlockSpec((None, FF_BLK, half), lambda j, n: (j, 0, n)), (N_SHARD, FF_BLK, D_MODEL), (N_SHARD, 2)))),
        w_o=_wgrad("dw_o", cat, dr1, full_t(half, lambda kk, n: (0, kk)), full_t(half, lambda kk, n: (0, n)),
                   pl.BlockSpec((half, half), lambda kk, n: (kk, n)), (D_MODEL, D_MODEL), (2, 2)))
    dep = early_grads(early)

    do1, do4, do16, dl1, dl4, dl16, duz, dws, dbs, zstat = _mix_bwd(
        dr1, w_o, hu, hz, mixed, attn, ln_z_g, ln_z_b, w_s[0], dep)
    dep = early_grads_sent(duz, (stat3, stat1, zstat, cstat, dws, dbs))
    dqkv = [_attn_bwd(qkv, do, lse, dl, d, dep)
            for qkv, do, lse, dl, d in zip(qkvs, (do1, do4, do16), lses, (dl1, dl4, dl16), DILATIONS)]
    dh, grad_x = _dx_in([g[0] for g in dqkv], [g[1] for g in dqkv], [g[2] for g in dqkv], duz, dr1, w_in,
                        c_tab, s1_tab, s2_tab)
    g_w_in = _wgrad("dw_in", x, dh, full_t(half, lambda j, kk: (0, kk)), full_t(W_IN_BLK, lambda j, kk: (0, j)),
                    pl.BlockSpec((None, half, W_IN_BLK), lambda j, kk: (j, kk, 0)), (N_SHARD, D_MODEL, W_IN_BLK), (N_SHARD, 2))
    return grad_x, g_w_in


def _tile_rows(rows, mult, steps):
    if rows % mult:
        return rows
    return next(rows // k for k in range(steps, rows + 1) if rows % k == 0 and (rows // k) % mult == 0)


def _grid_spec(grid, in_specs, out_specs):
    return pltpu.PrefetchScalarGridSpec(num_scalar_prefetch=1, grid=grid, in_specs=in_specs, out_specs=out_specs)


def _on_own_steps(i, count, steps, work):
    if count == steps:
        work()
    else:
        pl.when(i < count)(work)


def _place_shards(name, ws, dtypes, place, dep):
    n = len(ws)
    tiles = [_tile_rows(w.shape[0], 16, 8) for w in ws]
    counts = [w.shape[0] // t for w, t in zip(ws, tiles)]
    steps = max(counts)

    def body(s_ref, *refs):
        i = pl.program_id(0)
        for a in range(n):
            def work(a=a):
                refs[n + 1 + a][...] = refs[a][...].astype(dtypes[a])
            _on_own_steps(i, counts[a], steps, work)

    def tile(a, lead):
        last = counts[a] - 1
        if lead:
            return pl.BlockSpec((None, tiles[a], ws[a].shape[1]), lambda i, s: (s[0], jnp.minimum(i, last), 0))
        return pl.BlockSpec((tiles[a], ws[a].shape[1]), lambda i, s: (jnp.minimum(i, last), 0))

    return pl.pallas_call(
        body, name=name,
        grid_spec=_grid_spec((steps,), [tile(a, False) for a in range(n)] + [pl.BlockSpec(memory_space=pl.ANY)],
                             [tile(a, True) for a in range(n)]),
        out_shape=[jax.ShapeDtypeStruct((N_SHARD, *w.shape), dt) for w, dt in zip(ws, dtypes)],
        compiler_params=_cp())(place, *ws, dep)


def _pair_sums(name, mines, gots, place):
    n = len(mines)
    tiles = [_tile_rows(g.shape[1], 16, 2) for g in gots]
    per_blk = [g.shape[1] // t for g, t in zip(gots, tiles)]
    counts = [N_SHARD * nh for nh in per_blk]
    steps = max(counts)

    def body(s_ref, *refs):
        i = pl.program_id(0)
        for a in range(n):
            def work(a=a):
                refs[2 * n + a][...] = (refs[a][...] + refs[n + a][...]).astype(BF16)
            _on_own_steps(i, counts[a], steps, work)

    def tile(a, mine):
        nh, last = per_blk[a], counts[a] - 1

        def index(i, s):
            g = jnp.minimum(i, last)
            return (g // nh, (s[1] * nh if mine else 0) + g % nh, 0)

        return pl.BlockSpec((None, tiles[a], gots[a].shape[2]), index)

    return pl.pallas_call(
        body, name=name,
        grid_spec=_grid_spec((steps,), [tile(a, True) for a in range(n)] + [tile(a, False) for a in range(n)],
                             [tile(a, False) for a in range(n)]),
        out_shape=[jax.ShapeDtypeStruct(g.shape, BF16) for g in gots], compiler_params=_cp())(place, *mines, *gots)


def _chip_sums(name, owns, landeds, place, dep):
    n = len(owns)
    tiles = [_tile_rows(o.shape[1], 16, 8) for o in owns]
    counts = [o.shape[1] // t for o, t in zip(owns, tiles)]
    steps = max(counts)

    def body(s_ref, *refs):
        i = pl.program_id(0)
        for a in range(n):
            def work(a=a):
                own, l1, l2, l3 = (refs[4 * a + k][...].astype(F32) for k in range(4))
                refs[4 * n + 1 + a][...] = ((own + l1) + l2) + l3
            _on_own_steps(i, counts[a], steps, work)

    def slot(a, d):
        last = counts[a] - 1
        return pl.BlockSpec((None, tiles[a], owns[a].shape[2]), lambda i, s: ((s[0] + d) % N_SHARD, jnp.minimum(i, last), 0))

    def out(a):
        nh, last = counts[a], counts[a] - 1
        return pl.BlockSpec((tiles[a], owns[a].shape[2]), lambda i, s: (s[1] * nh + jnp.minimum(i, last), 0))

    operands = [x for o, l in zip(owns, landeds) for x in (o, l, l, l)]
    return pl.pallas_call(
        body, name=name,
        grid_spec=_grid_spec((steps,), [slot(a, d) for a in range(n) for d in range(4)] + [pl.BlockSpec(memory_space=pl.ANY)],
                             [out(a) for a in range(n)]),
        out_shape=[jax.ShapeDtypeStruct((2 * o.shape[1], o.shape[2]), F32) for o in owns],
        compiler_params=_cp())(place, *operands, dep)


def _adamw_math(w, g, m, v):
    m = ADAM_B1 * m + (1.0 - ADAM_B1) * g
    v = ADAM_B2 * v + (1.0 - ADAM_B2) * (g * g)
    m_hat = m / (1.0 - ADAM_B1 ** ADAM_STEP)
    v_hat = v / (1.0 - ADAM_B2 ** ADAM_STEP)
    delta = -ADAM_LR * (m_hat / (jnp.sqrt(v_hat) + ADAM_EPS) + ADAM_WD * w)
    return delta, m, v


def _adamw_shards(name, ws, gs, ms, vs):
    n = len(ws)
    tiles = [_tile_rows(w.shape[1], 8, 8) for w in ws]
    counts = [w.shape[1] // t for w, t in zip(ws, tiles)]
    steps = max(counts)

    def body(*refs):
        i = pl.program_id(0)
        for a in range(n):
            def work(a=a):
                w_ref, g_ref, m_ref, v_ref = refs[4 * a:4 * a + 4]
                d_ref, nm_ref, nv_ref = refs[4 * n + 3 * a:4 * n + 3 * a + 3]
                d_ref[...], nm_ref[...], nv_ref[...] = _adamw_math(w_ref[...], g_ref[...], m_ref[...], v_ref[...])
            _on_own_steps(i, counts[a], steps, work)

    def tile(a, lead):
        last, c = counts[a] - 1, ws[a].shape[2]
        if lead:
            return pl.BlockSpec((None, tiles[a], c), lambda i: (0, jnp.minimum(i, last), 0))
        return pl.BlockSpec((tiles[a], c), lambda i: (jnp.minimum(i, last), 0))

    res = pl.pallas_call(
        body, name=name, grid=(steps,),
        in_specs=[tile(a, lead) for a in range(n) for lead in (True, False, True, True)],
        out_specs=[tile(a, True) for a in range(n) for _ in range(3)],
        out_shape=[jax.ShapeDtypeStruct(w.shape, F32) for w in ws for _ in range(3)],
        compiler_params=_cp())(*[x for quad in zip(ws, gs, ms, vs) for x in quad])
    return [tuple(res[3 * a:3 * a + 3]) for a in range(n)]


MESH = pl.DeviceIdType.MESH
ANY = pl.BlockSpec(memory_space=pl.ANY)


def _place():
    x, y, c = lax.axis_index("x"), lax.axis_index("y"), lax.axis_index("c")
    chips = [(1 - x, y), (x, 1 - y), (1 - x, 1 - y)]
    return x, y, c, 2 * x + y, chips


def _remote(src, dst, send_sem, recv_sem, dev):
    return pltpu.make_async_remote_copy(src_ref=src, dst_ref=dst, send_sem=send_sem, recv_sem=recv_sem,
                                        device_id=dev, device_id_type=MESH)


def _half(ref, hc, rows):
    return ref.at[pl.ds(hc * (rows // 2), rows // 2)]


def _sibling_join(blocks, tag):
    n = len(blocks)

    def body(*refs):
        outs = refs[n:2 * n]
        send, recv = refs[2 * n:]
        x, y, c, _, _ = _place()
        cps = []
        for a in range(n):
            h = blocks[a].shape[0] // 2
            mine = outs[a].at[pl.ds(c * h, h)]
            cp = _remote(mine, mine, send.at[a], recv.at[a], (x, y, 1 - c))
            cp.start()
            cps.append(cp)
        for a, cp in enumerate(cps):
            h = blocks[a].shape[0] // 2
            theirs = outs[a].at[pl.ds((1 - c) * h, h)]
            _remote(theirs, theirs, send.at[a], recv.at[a], (x, y, 1 - c)).wait_recv()
            cp.wait_send()

    sem = pltpu.SemaphoreType.DMA
    return pl.pallas_call(body, name=f"rs_sibling_join_{tag}", in_specs=[ANY] * n, out_specs=[ANY] * n,
                          out_shape=[jax.ShapeDtypeStruct(b_.shape, b_.dtype) for b_ in blocks],
                          input_output_aliases={a: a for a in range(n)},
                          scratch_shapes=[sem((n,)), sem((n,))])(*blocks)


HBM = pl.BlockSpec(memory_space=pltpu.HBM)
SEM = pl.BlockSpec(memory_space=pltpu.SEMAPHORE)
TOKEN = jax.ShapeDtypeStruct((8, 128), F32)


def _in_flight_params():
    return pltpu.CompilerParams(has_side_effects=pltpu.SideEffectType.DATAFLOW_SIDE_EFFECTING)


def _in_hbm(a):
    return pltpu.with_memory_space_constraint(a, pltpu.HBM)


def _gather_piece(ref, rows, split, slot, hc):
    return _half(ref.at[slot], hc, rows) if split else ref.at[slot]


def _gather_start(stacks, split, after, tag):
    n = len(stacks)

    def body(*refs):
        ins = refs[:n]
        send, recv = refs[n + 1], refs[n + 2]
        token = refs[2 * n + 3]
        _, _, c, j, chips = _place()
        for a in range(n):
            mine = _gather_piece(ins[a], stacks[a].shape[1], split[a], j, c)
            for t in range(3):
                _remote(mine, mine, send.at[3 * a + t], recv.at[3 * a + t], (*chips[t], c)).start()
        token[...] = jnp.zeros_like(token)

    sems = pltpu.SemaphoreType.DMA((3 * n,))
    res = pl.pallas_call(
        body, name=f"gather_start_{tag}", in_specs=[HBM] * n + [ANY],
        out_specs=[SEM, SEM] + [HBM] * n + [pl.BlockSpec(memory_space=pltpu.VMEM)],
        out_shape=[sems, sems] + [pltpu.HBM(s.shape, s.dtype) for s in stacks] + [TOKEN],
        input_output_aliases={a: a + 2 for a in range(n)}, compiler_params=_in_flight_params(),
    )(*[_in_hbm(s) for s in stacks], after)
    return res[0], res[1], res[2:2 + n], res[2 + n]


def _gather_wait(send, recv, stacks, split, after, tag):
    n = len(stacks)

    def body(*refs):
        ins = refs[:n]
        send_ref, recv_ref = refs[n], refs[n + 1]
        _, _, c, j, chips = _place()
        for a in range(n):
            rows = stacks[a].shape[1]
            mine = _gather_piece(ins[a], rows, split[a], j, c)
            for t, (px, py) in enumerate(chips):
                theirs = _gather_piece(ins[a], rows, split[a], 2 * px + py, c)
                _remote(mine, mine, send_ref.at[3 * a + t], recv_ref.at[3 * a + t], (px, py, c)).wait_send()
                _remote(theirs, theirs, send_ref.at[3 * a + t], recv_ref.at[3 * a + t], (px, py, c)).wait_recv()

    return pl.pallas_call(
        body, name=f"gather_wait_{tag}", in_specs=[HBM] * n + [SEM, SEM, ANY], out_specs=[HBM] * n,
        out_shape=[pltpu.HBM(s.shape, s.dtype) for s in stacks],
        input_output_aliases={a: a for a in range(n)}, compiler_params=_in_flight_params(),
    )(*stacks, send, recv, after)


def _gather_forward(stacks, split, tag):
    idx = [a for a in range(len(stacks)) if split[a]]
    n = len(idx)

    def body(*refs):
        outs = refs[n:2 * n]
        send, recv = refs[2 * n:]
        x, y, c, _, chips = _place()
        sends = []
        for t, (px, py) in enumerate(chips):
            for a in range(n):
                blk = _half(outs[a].at[2 * px + py], c, stacks[idx[a]].shape[1])
                cp = _remote(blk, blk, send.at[a, t], recv.at[a, t], (x, y, 1 - c))
                cp.start()
                sends.append(cp)
        for t, (px, py) in enumerate(chips):
            for a in range(n):
                blk = _half(outs[a].at[2 * px + py], 1 - c, stacks[idx[a]].shape[1])
                _remote(blk, blk, send.at[a, t], recv.at[a, t], (x, y, 1 - c)).wait_recv()
        for cp in sends:
            cp.wait_send()

    sem = pltpu.SemaphoreType.DMA
    res = pl.pallas_call(
        body, name=f"gather_forward_{tag}", in_specs=[ANY] * n, out_specs=[ANY] * n,
        out_shape=[jax.ShapeDtypeStruct(stacks[a].shape, stacks[a].dtype) for a in idx],
        input_output_aliases={a: a for a in range(n)}, scratch_shapes=[sem((n, 3)), sem((n, 3))],
    )(*[stacks[a] for a in idx])
    out = list(stacks)
    for a, r in zip(idx, res):
        out[a] = r
    return out


def _forward_start(stacks, after, tag):
    n = len(stacks)

    def body(*refs):
        ins = refs[:n]
        send, recv = refs[n + 1], refs[n + 2]
        token = refs[2 * n + 3]
        x, y, c, _, chips = _place()
        for a in range(n):
            for t, (px, py) in enumerate(chips):
                blk = _half(ins[a].at[2 * px + py], c, stacks[a].shape[1])
                _remote(blk, blk, send.at[3 * a + t], recv.at[3 * a + t], (x, y, 1 - c)).start()
        token[...] = jnp.zeros_like(token)

    sems = pltpu.SemaphoreType.DMA((3 * n,))
    res = pl.pallas_call(
        body, name=f"forward_start_{tag}", in_specs=[HBM] * n + [ANY],
        out_specs=[SEM, SEM] + [HBM] * n + [pl.BlockSpec(memory_space=pltpu.VMEM)],
        out_shape=[sems, sems] + [pltpu.HBM(s.shape, s.dtype) for s in stacks] + [TOKEN],
        input_output_aliases={a: a + 2 for a in range(n)}, compiler_params=_in_flight_params(),
    )(*[_in_hbm(s) for s in stacks], after)
    return res[0], res[1], res[2:2 + n], res[2 + n]


def _forward_wait(send, recv, stacks, after, tag):
    n = len(stacks)

    def body(*refs):
        ins = refs[:n]
        send_ref, recv_ref = refs[n], refs[n + 1]
        x, y, c, _, chips = _place()
        for a in range(n):
            for t, (px, py) in enumerate(chips):
                mine = _half(ins[a].at[2 * px + py], c, stacks[a].shape[1])
                theirs = _half(ins[a].at[2 * px + py], 1 - c, stacks[a].shape[1])
                _remote(mine, mine, send_ref.at[3 * a + t], recv_ref.at[3 * a + t], (x, y, 1 - c)).wait_send()
                _remote(theirs, theirs, send_ref.at[3 * a + t], recv_ref.at[3 * a + t], (x, y, 1 - c)).wait_recv()

    return pl.pallas_call(
        body, name=f"forward_wait_{tag}", in_specs=[HBM] * n + [SEM, SEM, ANY], out_specs=[HBM] * n,
        out_shape=[pltpu.HBM(s.shape, s.dtype) for s in stacks],
        input_output_aliases={a: a for a in range(n)}, compiler_params=_in_flight_params(),
    )(*stacks, send, recv, after)


def _swap_start(grads, tag):
    n = len(grads)

    def body(*refs):
        ins, gots = refs[:n], refs[n:2 * n]
        send, recv = refs[2 * n], refs[2 * n + 1]
        token = refs[4 * n + 2]
        x, y, c, _, _ = _place()
        for a in range(n):
            h = grads[a].shape[1] // 2
            _remote(ins[a].at[:, pl.ds((1 - c) * h, h)], gots[a], send.at[a], recv.at[a], (x, y, 1 - c)).start()
        token[...] = jnp.zeros_like(token)

    sems = pltpu.SemaphoreType.DMA((n,))
    halves = [(g.shape[0], g.shape[1] // 2, g.shape[2]) for g in grads]
    res = pl.pallas_call(
        body, name=f"swap_start_{tag}", in_specs=[HBM] * (2 * n),
        out_specs=[SEM, SEM] + [HBM] * (2 * n) + [pl.BlockSpec(memory_space=pltpu.VMEM)],
        out_shape=[sems, sems] + [pltpu.HBM(g.shape, g.dtype) for g in grads] + [pltpu.HBM(s, F32) for s in halves] + [TOKEN],
        input_output_aliases={a: a + 2 for a in range(2 * n)}, compiler_params=_in_flight_params(),
    )(*[_in_hbm(g) for g in grads], *[_in_hbm(lax.empty(s, F32)) for s in halves])
    return res[0], res[1], res[2:2 + n], res[2 + n:2 + 2 * n], res[2 + 2 * n]


def _swap_wait(send, recv, grads, gots, after, tag):
    n = len(grads)

    def body(*refs):
        ins, lnd = refs[:n], refs[n:2 * n]
        send_ref, recv_ref = refs[2 * n], refs[2 * n + 1]
        x, y, c, _, _ = _place()
        for a in range(n):
            h = grads[a].shape[1] // 2
            cp = _remote(ins[a].at[:, pl.ds((1 - c) * h, h)], lnd[a], send_ref.at[a], recv_ref.at[a], (x, y, 1 - c))
            cp.wait_send()
            cp.wait_recv()

    bufs = [pltpu.HBM(g.shape, g.dtype) for g in grads] + [pltpu.HBM(g.shape, g.dtype) for g in gots]
    res = pl.pallas_call(
        body, name=f"swap_wait_{tag}", in_specs=[HBM] * (2 * n) + [SEM, SEM, ANY], out_specs=[HBM] * (2 * n),
        out_shape=bufs, input_output_aliases={a: a for a in range(2 * n)}, compiler_params=_in_flight_params(),
    )(*grads, *gots, send, recv, after)
    return res[:n], res[n:]


def _exchange_start(parts, tag):
    n = len(parts)

    def body(*refs):
        ins, lands = refs[:n], refs[n:2 * n]
        send, recv = refs[2 * n], refs[2 * n + 1]
        token = refs[4 * n + 2]
        _, _, c, j, chips = _place()
        for t, (px, py) in enumerate(chips):
            for a in range(n):
                _remote(ins[a].at[2 * px + py], lands[a].at[j], send.at[3 * a + t], recv.at[3 * a + t], (px, py, c)).start()
        token[...] = jnp.zeros_like(token)

    sems = pltpu.SemaphoreType.DMA((3 * n,))
    bufs = [pltpu.HBM(p.shape, p.dtype) for p in parts]
    res = pl.pallas_call(
        body, name=f"exchange_start_{tag}", in_specs=[HBM] * (2 * n),
        out_specs=[SEM, SEM] + [HBM] * (2 * n) + [pl.BlockSpec(memory_space=pltpu.VMEM)],
        out_shape=[sems, sems] + bufs + bufs + [TOKEN],
        input_output_aliases={a: a + 2 for a in range(2 * n)}, compiler_params=_in_flight_params(),
    )(*[_in_hbm(p) for p in parts], *[_in_hbm(lax.empty(p.shape, p.dtype)) for p in parts])
    return res[0], res[1], res[2:2 + n], res[2 + n:2 + 2 * n], res[2 + 2 * n]


def _exchange_wait(send, recv, parts, lands, after, tag):
    n = len(parts)

    def body(*refs):
        ins, lnd = refs[:n], refs[n:2 * n]
        send_ref, recv_ref = refs[2 * n], refs[2 * n + 1]
        _, _, c, j, chips = _place()
        for t, (px, py) in enumerate(chips):
            jt = 2 * px + py
            for a in range(n):
                _remote(ins[a].at[jt], lnd[a].at[j], send_ref.at[3 * a + t], recv_ref.at[3 * a + t], (px, py, c)).wait_send()
                _remote(ins[a].at[jt], lnd[a].at[jt], send_ref.at[3 * a + t], recv_ref.at[3 * a + t], (px, py, c)).wait_recv()

    bufs = [pltpu.HBM(p.shape, p.dtype) for p in parts]
    res = pl.pallas_call(
        body, name=f"exchange_wait_{tag}", in_specs=[HBM] * (2 * n) + [SEM, SEM, ANY], out_specs=[HBM] * (2 * n),
        out_shape=bufs + bufs, input_output_aliases={a: a for a in range(2 * n)}, compiler_params=_in_flight_params(),
    )(*parts, *lands, send, recv, after)
    return res[:n], res[n:]


def _small_chip_sums(arrs):
    n = len(arrs)

    def body(*refs):
        ins, outs = refs[:n], refs[n:2 * n]
        sib = refs[2 * n:3 * n]
        send, recv = refs[3 * n:]
        x, y, c, j, _ = _place()
        swaps = [_remote(ins[a], sib[a], send.at[a], recv.at[a], (x, y, 1 - c)) for a in range(n)]
        for cp in swaps:
            cp.start()
        for a in range(n):
            swaps[a].wait_recv()
            outs[a][j] = ins[a][...] + sib[a][...]
        for cp in swaps:
            cp.wait_send()

    sem = pltpu.SemaphoreType.DMA
    vm = pl.BlockSpec(memory_space=pltpu.VMEM)
    return pl.pallas_call(
        body, name="small_chip_sums", in_specs=[vm] * n, out_specs=[vm] * n,
        out_shape=[jax.ShapeDtypeStruct((N_SHARD, *a.shape), F32) for a in arrs],
        scratch_shapes=[pltpu.VMEM(a.shape, F32) for a in arrs] + [sem((n,)), sem((n,))],
        compiler_params=_cp(),
    )(*arrs)


def _small_totals(stacks):
    n = len(stacks)

    def body(*refs):
        for a in range(n):
            refs[n + a][...] = ((refs[a][0] + refs[a][1]) + refs[a][2]) + refs[a][3]

    return pl.pallas_call(body, name="small_totals", out_shape=[jax.ShapeDtypeStruct(s.shape[1:], F32) for s in stacks],
                          compiler_params=_cp())(*stacks)


SMALL_1024 = ("ln1_g", "ln1_b", "ln2_g", "ln2_b", "b_ple_gate", "ln3_g", "ln3_b")


def _adamw_small(red3, red1, redz, g_conv_w, redc, red_ws, red_bs, params):
    shape2d = {"ln_z_g": (1, D_GMLP), "ln_z_b": (1, D_GMLP), "w_s": (N_HEADS * BLK, BLK), "b_s": (N_HEADS, BLK),
               "conv_w": (3, FF_BLK), "conv_b": (N_SHARD, FF_BLK), **{k: (1, D_MODEL) for k in SMALL_1024}}
    names = list(shape2d)
    flat = [a.reshape(shape2d[k]) for k in names for a in params[k]]

    def body(r3, r1, rz, gcw, rc, rws, rbs, *refs):
        ins, outs = refs[:3 * len(names)], refs[3 * len(names):]

        def grad_of(k):
            if k == "w_s":
                return rws[...]
            if k == "b_s":
                return rbs[...]
            if k == "conv_w":
                return gcw[0:3, :]
            if k == "conv_b":
                return jnp.concatenate([rc[j * STAT_ROWS + 3:j * STAT_ROWS + 4, :] for j in range(N_SHARD)], axis=0)
            src, row = {"ln3_g": (r3, 0), "ln3_b": (r3, 1), "b_ple_gate": (r3, 2), "ln2_g": (r3, 3), "ln2_b": (r3, 4),
                        "ln1_g": (r1, 0), "ln1_b": (r1, 1), "ln_z_g": (rz, 0), "ln_z_b": (rz, 1)}[k]
            return src[row:row + 1, :]

        for i, k in enumerate(names):
            w_ref, m_ref, v_ref = ins[3 * i:3 * i + 3]
            g_ref, d_ref, nm_ref, nv_ref = outs[4 * i:4 * i + 4]
            g = grad_of(k)
            g_ref[...] = g
            d_ref[...], nm_ref[...], nv_ref[...] = _adamw_math(w_ref[...], g, m_ref[...], v_ref[...])

    res = pl.pallas_call(
        body, name="adamw_small",
        out_shape=[jax.ShapeDtypeStruct(shape2d[k], F32) for k in names for _ in range(4)],
        compiler_params=_cp(),
    )(red3, red1, redz, g_conv_w, redc, red_ws, red_bs, *flat)
    return {k: tuple(r.reshape(params[k][0].shape) for r in res[4 * i:4 * i + 4]) for i, k in enumerate(names)}


WEIGHTS = ("w_in", "ln_z_g", "ln_z_b", "w_s", "b_s", "w_o", "ln1_g", "ln1_b", "w_ff_a", "w_ff_b", "conv_w", "conv_b",
           "w_ff_down", "ln2_g", "ln2_b", "w_ple_gate", "b_ple_gate", "w_ple_in", "ln3_g", "ln3_b")
BIG = ("w_in", "w_o", "w_ff_a", "w_ff_b", "w_ff_down", "w_ple_gate", "w_ple_in")
TRANSPOSED = ("w_ff_a", "w_ff_b")
LATE = ("w_o", "w_ff_a", "w_ff_b", "w_ff_down", "w_ple_gate", "w_ple_in", "conv_w")


def kernel(x, p, positions, w_in, ln_z_g, ln_z_b, w_s, b_s, w_o, ln1_g, ln1_b, w_ff_a, w_ff_b, conv_w, conv_b, w_ff_down, ln2_g, ln2_b, w_ple_gate, b_ple_gate, w_ple_in, ln3_g, ln3_b, loss_target, m_w_in, m_ln_z_g, m_ln_z_b, m_w_s, m_b_s, m_w_o, m_ln1_g, m_ln1_b, m_w_ff_a, m_w_ff_b, m_conv_w, m_conv_b, m_w_ff_down, m_ln2_g, m_ln2_b, m_w_ple_gate, m_b_ple_gate, m_w_ple_in, m_ln3_g, m_ln3_b, v_w_in, v_ln_z_g, v_ln_z_b, v_w_s, v_b_s, v_w_o, v_ln1_g, v_ln1_b, v_w_ff_a, v_w_ff_b, v_conv_w, v_conv_b, v_w_ff_down, v_ln2_g, v_ln2_b, v_w_ple_gate, v_b_ple_gate, v_w_ple_in, v_ln3_g, v_ln3_b):
    args = locals()
    w = {k: args[k] for k in WEIGHTS}
    m = {k: args["m_" + k] for k in WEIGHTS}
    v = {k: args["v_" + k] for k in WEIGHTS}

    for k in TRANSPOSED:
        w[k], m[k], v[k] = (jnp.swapaxes(a, 1, 2) for a in (w[k], m[k], v[k]))

    chip = 2 * lax.axis_index("x") + lax.axis_index("y")
    place = jnp.stack([chip, lax.axis_index("c")]).astype(jnp.int32)
    stack = dict(zip(["w_in"], _place_shards("cast_w_in", [w["w_in"][0]], [MXU], place, place)))
    i_send, i_recv, in_flight, dep = _gather_start([stack["w_in"]], [True], place, "w_in")
    stack.update(zip(LATE, _place_shards("cast_late", [w[k][0] for k in LATE],
                                         [F32 if k == "conv_w" else MXU for k in LATE], place, dep)))
    split_late = [k != "conv_w" for k in LATE]
    g_send, g_recv, late_flight, start_dep = _gather_start([stack[k] for k in LATE], split_late, place, "late")
    rope = _rope_tables(positions, x.shape[1], start_dep)
    landed_in = _gather_wait(i_send, i_recv, in_flight, [True], rope[0], "w_in")
    w_in_full, = _gather_forward(landed_in, [True], "w_in")
    halves =[k for k, sp in zip(LATE, split_late) if sp]
    trips = {}

    def late_landed(after):
        fw = dict(zip(LATE, _gather_wait(g_send, g_recv, late_flight, split_late, after, "late")))
        trips["late"] = (fw, *_forward_start([fw[k] for k in halves], fw["conv_w"], "late"))
        return trips["late"][-1]

    def late_weights(after):
        fw, send, recv, flight, _ = trips["late"]
        fw.update(zip(halves, _forward_wait(send, recv, flight, after, "late")))
        return (fw["w_o"].reshape(D_MODEL, D_MODEL), fw["w_ff_a"], fw["w_ff_b"], fw["conv_w"], fw["w_ff_down"],
                fw["w_ple_gate"].reshape(D_MODEL, D_MODEL), fw["w_ple_in"])

    def swap_started(names, grads, tag):
        stacked = [g.reshape(N_SHARD, *w[k].shape[1:]) for k, g in zip(names, grads)]
        return (names, tag, *_swap_start(stacked, tag))

    def partial_sums(swap, after):
        names, tag, send, recv, stacked, gots, _ = swap
        stacked, got = _swap_wait(send, recv, stacked, gots, after, tag)
        pair = _pair_sums(f"rs_pair_{tag}", stacked, got, place)
        return (names, tag, *_exchange_start(pair, tag))

    def reduced(trip, after, dep):
        names, tag, send, recv, pair, lands, _ = trip
        pair, landed = _exchange_wait(send, recv, pair, lands, after, tag)
        blocks = _chip_sums(f"rs_sum_{tag}", pair, landed, place, dep)
        return dict(zip(names, _sibling_join(blocks, tag)))

    def early_grads(grads):
        trips["swap"] = swap_started(list(grads), list(grads.values()), "early")
        return trips["swap"][-1]

    def early_grads_sent(after, small):
        trips["early"] = partial_sums(trips["swap"], after)
        stat3, stat1, zstat, cstat, dws, dbs = small
        sums = _small_chip_sums([stat3, stat1, zstat, cstat.reshape(N_SHARD * STAT_ROWS, FF_BLK),
                                 dws.reshape(N_HEADS * BLK, BLK), dbs])
        trips["small"] = _gather_start(sums, [False] * len(sums), trips["early"][-1], "small")
        return trips["small"][-1]

    grad_x, g_w_in = _local_step(
        x[0], p[0, 0], rope, loss_target[0], w_in_full, start_dep, late_landed, late_weights, early_grads, early_grads_sent,
        ln_z_g, ln_z_b, w_s, b_s, ln1_g, ln1_b, conv_b, ln2_g, ln2_b, b_ple_gate, ln3_g, ln3_b)

    trips["w_in"] = partial_sums(swap_started(["w_in"], [g_w_in], "w_in"), g_w_in)
    out = {}

    def adamw(red, tag):
        names = list(red)
        steps = _adamw_shards(f"adamw_{tag}", [w[k] for k in names], [red[k] for k in names], [m[k] for k in names],
                              [v[k] for k in names])
        for k, (d, nm, nv) in zip(names, steps):
            out[k] = (red[k].reshape(w[k].shape), d, nm, nv)

    adamw(reduced(trips["early"], grad_x, trips["w_in"][-1]), "early")
    adamw(reduced(trips["w_in"], out["w_o"][3], start_dep), "w_in")
    for k in TRANSPOSED:
        out[k] = tuple(jnp.swapaxes(a, 1, 2) for a in out[k])

    s_send, s_recv, s_flight, _ = trips["small"]
    red3, red1, redz, redc, red_ws, red_bs = _small_totals(
        _gather_wait(s_send, s_recv, s_flight, [False] * len(s_flight), out["w_in"][3], "small"))
    loss = (0.5 / D_MODEL) * jnp.sum(red3[5])
    g_conv_w = lax.dynamic_slice_in_dim(redc, chip * STAT_ROWS, STAT_ROWS, 0)
    names_small = [k for k in WEIGHTS if k not in BIG]
    out.update(_adamw_small(red3, red1, redz, g_conv_w, redc, red_ws, red_bs, {k: (w[k], m[k], v[k]) for k in names_small}))

    return (loss, grad_x[None], *[out[k][0] for k in WEIGHTS], *[out[k][1] for k in WEIGHTS],
            *[out[k][2] for k in WEIGHTS], *[out[k][3] for k in WEIGHTS])
```

```python
import functools
import math

import numpy as np
import jax
import jax.numpy as jnp
from jax import lax
from jax.experimental import pallas as pl
from jax.experimental.pallas import tpu as pltpu

F32 = jnp.float32
BF16 = jnp.bfloat16
MXU = BF16

D_MODEL = 1024
HEAD_DIM = 64
N_HEADS = 8
D_ATTN = 512
D_GMLP = 512
D_IN = 2560
DILATIONS = (1, 4, 16)
BLK = 128
ROPE_THETA = 500000.0
ROPE_DIM = 16
D_FF = 2816
D_PLE = 256
LN_EPS = 1e-5
ALPHA = 2.0 ** 0.25
NEG_INF = -1e30
N_SHARD = 4
W_IN_BLK = D_IN // N_SHARD
FF_BLK = D_FF // N_SHARD
ROW_BLK = D_MODEL // N_SHARD
ADAM_LR, ADAM_B1, ADAM_B2, ADAM_EPS, ADAM_WD, ADAM_STEP = 0.001, 0.9, 0.999, 1e-08, 0.01, 10

TM = 512
HALO = 8
ROW_GROUPS = 2
VMEM_LIMIT = 56 * 1024 * 1024


def _cp(**kw):
    return pltpu.CompilerParams(vmem_limit_bytes=VMEM_LIMIT, **kw)


def _full(shape):
    n = len(shape)
    return pl.BlockSpec(shape, lambda *_: (0,) * n)


def _gelu(x):
    return 0.5 * x * (1.0 + lax.erf(x * (1.0 / math.sqrt(2.0))))


def _gelu_grad(x):
    return 0.5 * (1.0 + lax.erf(x * (1.0 / math.sqrt(2.0)))) + x * jnp.exp(-0.5 * x * x) * (1.0 / math.sqrt(2.0 * math.pi))


def _ln_fwd(r):
    mu = jnp.mean(r, axis=-1, keepdims=True)
    xc = r - mu
    var = jnp.mean(xc * xc, axis=-1, keepdims=True)
    rstd = lax.rsqrt(var + LN_EPS)
    return xc * rstd, rstd


def _ln_bwd(dy, xhat, rstd, g):
    dxh = dy * g
    m1 = jnp.mean(dxh, axis=-1, keepdims=True)
    m2 = jnp.mean(dxh * xhat, axis=-1, keepdims=True)
    return rstd * (dxh - m1 - xhat * m2)


def _dot(a, b):
    return jnp.dot(a.astype(MXU), b.astype(MXU), preferred_element_type=F32)


def _dot_nt(a, b):
    return lax.dot_general(a.astype(MXU), b.astype(MXU), (((1,), (1,)), ((), ())), preferred_element_type=F32)


def _dot_tn(a, b):
    return lax.dot_general(a.astype(MXU), b.astype(MXU), (((0,), (0,)), ((), ())), preferred_element_type=F32)


def _colsum(v):
    return jnp.sum(v, axis=0, keepdims=True)


def _rope_tables(positions, t, dep):
    inv = np.float32(ROPE_THETA) ** (-np.arange(0, ROPE_DIM, 2, dtype=np.float32) / np.float32(ROPE_DIM))
    half = ROPE_DIM // 2
    pos_rep = jnp.repeat(positions.reshape(t // 16, 16), half, axis=1)
    inv_row = jnp.asarray(np.tile(inv, 16)[None, :], F32)

    def trig_body(pos_ref, inv_ref, dep_ref, cos_ref, sin_ref):
        ang = pos_ref[...].astype(F32) * inv_ref[...]
        cos_ref[...] = jnp.cos(ang)
        sin_ref[...] = jnp.sin(ang)

    vm = pl.BlockSpec(memory_space=pltpu.VMEM)
    cos8, sin8 = pl.pallas_call(
        trig_body, name="rope_trig", in_specs=[vm, vm, pl.BlockSpec(memory_space=pl.ANY)], out_specs=[vm, vm],
        out_shape=(jax.ShapeDtypeStruct((t // 16, 128), F32), jax.ShapeDtypeStruct((t // 16, 128), F32)),
    )(pos_rep, inv_row, dep)
    cos8 = cos8.reshape(t, half)
    sin8 = sin8.reshape(t, half)

    lane = np.arange(128) % HEAD_DIM
    sel = (np.arange(half)[:, None] == (lane % half)[None, :])
    e_cos = (sel & (lane < ROPE_DIM)[None, :]).astype(np.float32)
    e_s1 = -(sel & (lane < half)[None, :]).astype(np.float32)
    e_s2 = (sel & ((lane >= half) & (lane < ROPE_DIM))[None, :]).astype(np.float32)
    ones = (lane >= ROPE_DIM).astype(np.float32)[None, :]

    def expand_body(cos_ref, sin_ref, ec_ref, e1_ref, e2_ref, ones_ref, c_ref, s1_ref, s2_ref):
        hp = lax.Precision.HIGHEST
        c_ref[...] = jnp.dot(cos_ref[...], ec_ref[...], precision=hp, preferred_element_type=F32) + ones_ref[...]
        s1_ref[...] = jnp.dot(sin_ref[...], e1_ref[...], precision=hp, preferred_element_type=F32)
        s2_ref[...] = jnp.dot(sin_ref[...], e2_ref[...], precision=hp, preferred_element_type=F32)

    tab = jax.ShapeDtypeStruct((t, 128), F32)
    return pl.pallas_call(expand_body, name="rope_expand", out_shape=(tab, tab, tab), compiler_params=_cp())(
        cos8, sin8, jnp.asarray(e_cos), jnp.asarray(e_s1), jnp.asarray(e_s2), jnp.asarray(ones))


def _tile_heads(tab):
    return jnp.concatenate([tab] * (D_ATTN // 128), axis=1)


def _rope_apply(v, c, s1, s2):
    n = v.shape[1]
    half = ROPE_DIM // 2
    return v * c + pltpu.roll(v, n - half, 1) * s1 + pltpu.roll(v, half, 1) * s2


def _rope_apply_t(g, c, s1, s2):
    n = g.shape[1]
    half = ROPE_DIM // 2
    return g * c + pltpu.roll(g * s1, half, 1) + pltpu.roll(g * s2, n - half, 1)


LANE_CHUNKS = D_ATTN // 128
HEAD_LANES = 128 // N_HEADS


def _perm_shape(t, d, w, dtype):
    return jax.ShapeDtypeStruct((d, t // d, w), dtype)


def _perm_tile(d, w):
    return pl.BlockSpec((None if d == 1 else d, TM // d, w), lambda i: (0, i, 0))


def _to_planes(ref, scr, d, n_chunks, dtype):
    for r in range(d):
        for cc in range(n_chunks):
            ref[r, :, cc * 128:(cc + 1) * 128] = scr.at[cc][pl.ds(r, TM // d, stride=d), :].astype(dtype)


def _from_planes(ref, scr, d, n_chunks, accumulate=False):
    for r in range(d):
        for cc in range(n_chunks):
            rows = scr.at[cc]
            val = ref[r, :, cc * 128:(cc + 1) * 128].astype(F32)
            if accumulate:
                rows[pl.ds(r, TM // d, stride=d), :] += val
            else:
                rows[pl.ds(r, TM // d, stride=d), :] = val


def _chunks(val):
    return [val[:, cc * 128:(cc + 1) * 128] for cc in range(val.shape[1] // 128)]


def _unchunk(scr, n_chunks, base=0):
    return jnp.concatenate([scr[base + cc] for cc in range(n_chunks)], axis=1)


def _head_expand():
    src = np.arange(128)[:, None]
    dst = np.arange(D_ATTN)[None, :]
    return jnp.asarray((src == (dst // HEAD_DIM) * HEAD_LANES).astype(np.float32))


def _head_reduce():
    src = np.arange(D_ATTN)[:, None]
    dst = np.arange(128)[None, :]
    return jnp.asarray((src // HEAD_DIM == dst // HEAD_LANES).astype(np.float32))


def _dot_select(a, sel):
    hi = a.astype(BF16)
    lo = (a - hi.astype(F32)).astype(BF16)
    sel = sel.astype(BF16)
    return jnp.dot(hi, sel, preferred_element_type=F32) + jnp.dot(lo, sel, preferred_element_type=F32)


def _qkvuz(x, w_in, c_tab, s1_tab, s2_tab, ln_z_g, ln_z_b, w_s, b_full, dep):
    t = x.shape[0]
    nchunk = TM // BLK

    def body(x_ref, w_ref, c_ref, s1_ref, s2_ref, g_ref, b_ref, ws_ref, bf_ref, dep_ref,
             qkv1_ref, qkv4_ref, qkv16_ref, hu_ref, hz_ref, mixed_ref, gm_ref, h_scr, wm_scr, p_scr):
        @pl.when(pl.program_id(0) == 0)
        def _():
            row = lax.broadcasted_iota(jnp.int32, (BLK, BLK), 0)
            col = lax.broadcasted_iota(jnp.int32, (BLK, BLK), 1)
            for g in range(N_HEADS):
                wm_scr[g] = jnp.where(col <= row, ws_ref[g], 0.0).astype(MXU)

        xb = x_ref[...].astype(MXU)
        for j in range(N_SHARD):
            h_scr[:, j * W_IN_BLK:(j + 1) * W_IN_BLK] = jnp.dot(xb, w_ref[j], preferred_element_type=F32)
        c, s1, s2 = _tile_heads(c_ref[...]), _tile_heads(s1_ref[...]), _tile_heads(s2_ref[...])
        q = _rope_apply(h_scr[:, 0:D_ATTN], c, s1, s2) * (1.0 / math.sqrt(HEAD_DIM))
        k = _rope_apply(h_scr[:, D_ATTN:2 * D_ATTN], c, s1, s2)
        for part, val in enumerate((q, k, h_scr[:, 2 * D_ATTN:3 * D_ATTN])):
            qkv1_ref[:, part * D_ATTN:(part + 1) * D_ATTN] = val.astype(MXU)
            for cc in range(LANE_CHUNKS):
                p_scr[part * LANE_CHUNKS + cc] = val[:, cc * 128:(cc + 1) * 128]
        _to_planes(qkv4_ref, p_scr, DILATIONS[1], 3 * LANE_CHUNKS, MXU)
        _to_planes(qkv16_ref, p_scr, DILATIONS[2], 3 * LANE_CHUNKS, MXU)
        hu = h_scr[:, 3 * D_ATTN:3 * D_ATTN + D_GMLP]
        hz = h_scr[:, 3 * D_ATTN + D_GMLP:]
        hu_ref[...] = hu
        hz_ref[...] = hz
        zhat, _ = _ln_fwd(_gelu(hz))
        zn = (zhat * g_ref[...] + b_ref[...]).astype(MXU)
        for ch in range(nchunk):
            rows = slice(ch * BLK, (ch + 1) * BLK)
            for g in range(N_HEADS):
                cols = slice(g * HEAD_DIM, (g + 1) * HEAD_DIM)
                mixed_ref[rows, cols] = jnp.dot(wm_scr[g], zn[rows, cols], preferred_element_type=F32) + bf_ref[:, cols]
        gm_ref[...] = (_gelu(hu) * mixed_ref[...]).astype(MXU)

    tok = lambda w: pl.BlockSpec((TM, w), lambda i: (i, 0))
    outs = [_perm_shape(t, d, 3 * D_ATTN, MXU) for d in DILATIONS] + [jax.ShapeDtypeStruct((t, D_GMLP), F32)] * 3 + [
        jax.ShapeDtypeStruct((t, D_GMLP), MXU)]
    return pl.pallas_call(
        body, name="qkvuz", grid=(t // TM,),
        in_specs=[tok(D_MODEL), _full(w_in.shape), tok(128), tok(128), tok(128), _full(ln_z_g.shape), _full(ln_z_b.shape),
                  _full(w_s.shape), _full(b_full.shape), pl.BlockSpec(memory_space=pl.ANY)],
        out_specs=[_perm_tile(d, 3 * D_ATTN) for d in DILATIONS] + [tok(D_ATTN)] * 4, out_shape=outs,
        scratch_shapes=[pltpu.VMEM((TM, D_IN), F32), pltpu.VMEM((N_HEADS, BLK, BLK), MXU),
                        pltpu.VMEM((3 * LANE_CHUNKS, TM, 128), F32)],
        compiler_params=_cp(dimension_semantics=("arbitrary",)),
    )(x, w_in, c_tab, s1_tab, s2_tab, ln_z_g, ln_z_b, w_s, b_full, dep)


def _band_valid(n):
    i = lax.broadcasted_iota(jnp.int32, (BLK, 2 * BLK), 0)
    j = lax.broadcasted_iota(jnp.int32, (BLK, 2 * BLK), 1)
    return (j >= i) & (j <= i + BLK) & ((j >= BLK) | (n > 0))


def _attn_fwd(qkv, d, dep):
    _, l_sub, _ = qkv.shape
    nb = l_sub // BLK

    def body(q_ref, kp_ref, kc_ref, vp_ref, vc_ref, dep_ref, o_ref, l_ref):
        valid = _band_valid(pl.program_id(1))
        kcat = jnp.concatenate([kp_ref[...], kc_ref[...]], axis=0)
        vcat = jnp.concatenate([vp_ref[...], vc_ref[...]], axis=0)
        for h in range(N_HEADS):
            cols = slice(h * HEAD_DIM, (h + 1) * HEAD_DIM)
            s = jnp.where(valid, _dot_nt(q_ref[:, cols], kcat[:, cols]), NEG_INF)
            m = jnp.max(s, axis=-1, keepdims=True)
            e = jnp.exp(s - m)
            den = jnp.sum(e, axis=-1, keepdims=True)
            o_ref[:, cols] = _dot(e, vcat[:, cols]) * (1.0 / den)
            l_ref[:, h * HEAD_LANES:(h + 1) * HEAD_LANES] = jnp.broadcast_to(m + jnp.log(den), (BLK, HEAD_LANES))

    def blk(w, col, prev=False):
        return pl.BlockSpec((None, BLK, w), lambda r, n: (r, jnp.maximum(n - 1, 0) if prev else n, col))

    return pl.pallas_call(
        body, name=f"attn_fwd_d{d}", grid=(d, nb),
        in_specs=[blk(D_ATTN, 0), blk(D_ATTN, 1, True), blk(D_ATTN, 1), blk(D_ATTN, 2, True), blk(D_ATTN, 2),
                  pl.BlockSpec(memory_space=pl.ANY)],
        out_specs=[blk(D_ATTN, 0), blk(128, 0)],
        out_shape=[jax.ShapeDtypeStruct((d, l_sub, D_ATTN), F32), jax.ShapeDtypeStruct((d, l_sub, 128), F32)],
        compiler_params=_cp(dimension_semantics=("arbitrary", "arbitrary")),
    )(qkv, qkv, qkv, qkv, qkv, dep)


def _attn_bwd(qkv, do, lse, delta, d, dep):
    _, l_sub, _ = qkv.shape
    nb = l_sub // BLK
    whole = l_sub <= 8 * BLK

    def shares(n, q_ref, kp_ref, kc_ref, vp_ref, vc_ref, do_ref, l_ref, dl_ref, dq_ref):
        valid = _band_valid(n)
        kcat = jnp.concatenate([kp_ref[...], kc_ref[...]], axis=0)
        vcat = jnp.concatenate([vp_ref[...], vc_ref[...]], axis=0)
        for h in range(N_HEADS):
            cols = slice(h * HEAD_DIM, (h + 1) * HEAD_DIM)
            stat = slice(h * HEAD_LANES, h * HEAD_LANES + 1)
            qh, doh = q_ref[:, cols], do_ref[:, cols]
            p = jnp.where(valid, jnp.exp(_dot_nt(qh, kcat[:, cols]) - l_ref[:, stat]), 0.0)
            ds = p * (_dot_nt(doh, vcat[:, cols]) - dl_ref[:, stat])
            dq_ref[:, cols] = _dot(ds, kcat[:, cols])
            yield cols, _dot_tn(ds, qh), _dot_tn(p, doh)

    def body_whole(*refs):
        dk_ref, dv_ref = refs[10:]
        n = pl.program_id(1)
        cur = pl.ds(pl.multiple_of(n * BLK, BLK), BLK)
        prev = pl.ds(pl.multiple_of(jnp.maximum(n - 1, 0) * BLK, BLK), BLK)
        for cols, dk2, dv2 in shares(n, *refs[:8], refs[9]):
            dk_ref[cur, cols] = dk2[BLK:]
            dv_ref[cur, cols] = dv2[BLK:]
            dk_ref[prev, cols] += dk2[0:BLK]
            dv_ref[prev, cols] += dv2[0:BLK]

    def body_carry(*refs):
        dk_ref, dv_ref, ck_scr, cv_scr = refs[10:]
        n = pl.program_id(1)

        @pl.when(n == 0)
        def _():
            ck_scr[...] = jnp.zeros_like(ck_scr)
            cv_scr[...] = jnp.zeros_like(cv_scr)

        @pl.when(n < nb)
        def _():
            for cols, dk2, dv2 in shares(n, *refs[:8], refs[9]):
                dk_ref[:, cols] = ck_scr[:, cols] + dk2[0:BLK]
                dv_ref[:, cols] = cv_scr[:, cols] + dv2[0:BLK]
                ck_scr[:, cols] = dk2[BLK:]
                cv_scr[:, cols] = dv2[BLK:]

        @pl.when(n == nb)
        def _():
            dk_ref[...] = ck_scr[...]
            dv_ref[...] = cv_scr[...]

    def blk(w, col, shift=0):
        return pl.BlockSpec((None, BLK, w), lambda r, n: (r, jnp.clip(n - shift, 0, nb - 1), col))

    if whole:
        dkv_spec = pl.BlockSpec((None, l_sub, D_ATTN), lambda r, n: (r, 0, 0))
        body, steps, scratch = body_whole, nb, []
    else:
        dkv_spec = blk(D_ATTN, 0, 1)
        body, steps, scratch = body_carry, nb + 1, [pltpu.VMEM((BLK, D_ATTN), F32)] * 2
    return pl.pallas_call(
        body, name=f"attn_bwd_d{d}", grid=(d, steps),
        in_specs=[blk(D_ATTN, 0), blk(D_ATTN, 1, 1), blk(D_ATTN, 1), blk(D_ATTN, 2, 1), blk(D_ATTN, 2),
                  blk(D_ATTN, 0), blk(128, 0), blk(128, 0), pl.BlockSpec(memory_space=pl.ANY)],
        out_specs=[blk(D_ATTN, 0), dkv_spec, dkv_spec],
        out_shape=[jax.ShapeDtypeStruct((d, l_sub, D_ATTN), F32)] * 3,
        scratch_shapes=scratch,
        compiler_params=_cp(dimension_semantics=("arbitrary", "arbitrary")),
    )(qkv, qkv, qkv, qkv, qkv, do, lse, delta, dep)


def _mix_ln1(os_, ls_, gm, x, w_o, ln1_g, ln1_b, dep):
    t = x.shape[0]
    expand = _head_expand()

    def body(o1, o4, o16, l1, l4, l16, gm_ref, x_ref, wo_ref, g_ref, b_ref, ex_ref, dep_ref,
             attn_ref, lse1_ref, lse4_ref, lse16_ref, cat_ref, xhat_ref, rstd_ref, x1b_ref, o_scr, l_scr):
        _from_planes(o4, o_scr, DILATIONS[1], LANE_CHUNKS)
        _from_planes(o16, o_scr.at[pl.ds(LANE_CHUNKS, LANE_CHUNKS)], DILATIONS[2], LANE_CHUNKS)
        _from_planes(l4, l_scr, DILATIONS[1], 1)
        _from_planes(l16, l_scr.at[pl.ds(1, 1)], DILATIONS[2], 1)
        la, lb, lc = l1[...], l_scr[0], l_scr[1]
        m = jnp.maximum(jnp.maximum(la, lb), lc)
        ea, eb, ec = jnp.exp(la - m), jnp.exp(lb - m), jnp.exp(lc - m)
        den = ea + eb + ec
        inv = 1.0 / den
        wide = lambda w: _dot_select(w, ex_ref[...])
        attn = (wide(ea * inv) * o1[...] + wide(eb * inv) * _unchunk(o_scr, LANE_CHUNKS)
                + wide(ec * inv) * _unchunk(o_scr, LANE_CHUNKS, LANE_CHUNKS))
        attn_ref[...] = attn
        lse = m + jnp.log(den)
        lse1_ref[...] = lse
        l_scr[2] = lse
        _to_planes(lse4_ref, l_scr.at[pl.ds(2, 1)], DILATIONS[1], 1, F32)
        _to_planes(lse16_ref, l_scr.at[pl.ds(2, 1)], DILATIONS[2], 1, F32)
        cat_ref[:, 0:D_ATTN] = attn.astype(MXU)
        cat_ref[:, D_ATTN:] = gm_ref[...]
        mix = jnp.dot(cat_ref[...], wo_ref[...], preferred_element_type=F32)
        xhat, rstd = _ln_fwd(ALPHA * x_ref[...] + mix)
        xhat_ref[...] = xhat
        rstd_ref[...] = rstd
        x1b_ref[...] = (xhat * g_ref[...] + b_ref[...]).astype(MXU)

    tok = lambda w: pl.BlockSpec((TM, w), lambda i: (i, 0))
    outs = [jax.ShapeDtypeStruct((t, D_ATTN), F32)] + [_perm_shape(t, d, 128, F32) for d in DILATIONS] + [
        jax.ShapeDtypeStruct((t, D_MODEL), MXU), jax.ShapeDtypeStruct((t, D_MODEL), F32), jax.ShapeDtypeStruct((t, 1), F32),
        jax.ShapeDtypeStruct((t, D_MODEL), MXU)]
    return pl.pallas_call(
        body, name="mix_ln1", grid=(t // TM,),
        in_specs=[_perm_tile(d, D_ATTN) for d in DILATIONS] + [_perm_tile(d, 128) for d in DILATIONS]
        + [tok(D_GMLP), tok(D_MODEL), _full(w_o.shape), _full(ln1_g.shape), _full(ln1_b.shape), _full(expand.shape),
           pl.BlockSpec(memory_space=pl.ANY)],
        out_specs=[tok(D_ATTN)] + [_perm_tile(d, 128) for d in DILATIONS] + [tok(D_MODEL), tok(D_MODEL), tok(1), tok(D_MODEL)],
        out_shape=outs,
        scratch_shapes=[pltpu.VMEM((2 * LANE_CHUNKS, TM, 128), F32), pltpu.VMEM((3, TM, 128), F32)],
        compiler_params=_cp(dimension_semantics=("arbitrary",)),
    )(*os_, *ls_, gm, x, w_o, ln1_g, ln1_b, expand, dep)


def _conv_fwd(a_ext, w_ref, b_ref, rows):
    return (b_ref[...] + w_ref[2:3, :] * a_ext[HALO:HALO + rows] + w_ref[1:2, :] * a_ext[HALO - 1:HALO - 1 + rows]
            + w_ref[0:1, :] * a_ext[HALO - 2:HALO - 2 + rows])


def _ffn_in(x1b, w_a, w_b, conv_w, conv_b):
    t = x1b.shape[0]
    hb = TM // HALO

    def body(x_ref, xh_ref, wa_ref, wb_ref, cw_ref, cb_ref, apre_ref, a_ref, b_ref, f_ref):
        i = pl.program_id(1)
        a_pre = _dot_nt(x_ref[...], wa_ref[...])
        a_halo = jnp.where(i > 0, _dot_nt(xh_ref[...], wa_ref[...]), 0.0)
        a = _conv_fwd(jnp.concatenate([a_halo, a_pre], axis=0), cw_ref, cb_ref, TM)
        b = _dot_nt(x_ref[...], wb_ref[...])
        apre_ref[...] = a_pre
        a_ref[...] = a
        b_ref[...] = b
        f_ref[...] = (_gelu(a) * b).astype(MXU)

    blk = lambda r, c: pl.BlockSpec((None, r, c), lambda j, i: (j, 0, 0))
    tokj = pl.BlockSpec((None, TM, FF_BLK), lambda j, i: (j, i, 0))
    outs = [jax.ShapeDtypeStruct((N_SHARD, t, FF_BLK), F32)] * 3 + [jax.ShapeDtypeStruct((N_SHARD, t, FF_BLK), MXU)]
    return pl.pallas_call(
        body, name="ffn_in", grid=(N_SHARD, t // TM),
        in_specs=[pl.BlockSpec((TM, D_MODEL), lambda j, i: (i, 0)),
                  pl.BlockSpec((HALO, D_MODEL), lambda j, i: (jnp.maximum(i * hb - 1, 0), 0)),
                  blk(FF_BLK, D_MODEL), blk(FF_BLK, D_MODEL), blk(3, FF_BLK), blk(1, FF_BLK)],
        out_specs=[tokj, tokj, tokj, tokj], out_shape=outs,
        compiler_params=_cp(dimension_semantics=("arbitrary", "arbitrary")),
    )(x1b, x1b, w_a, w_b, conv_w, conv_b)


def _ffn_out_ln2(f, w_down, xhat1, ln1_g, ln1_b, ln2_g, ln2_b):
    t = xhat1.shape[0]

    def body(f_ref, wd_ref, xh_ref, g1_ref, b1_ref, g2_ref, b2_ref, xhat_ref, rstd_ref, x2b_ref):
        ff = jnp.dot(f_ref[0], wd_ref[0], preferred_element_type=F32)
        for j in range(1, N_SHARD):
            ff = ff + jnp.dot(f_ref[j], wd_ref[j], preferred_element_type=F32)
        x1 = xh_ref[...] * g1_ref[...] + b1_ref[...]
        xhat, rstd = _ln_fwd(ALPHA * x1 + ff)
        xhat_ref[...] = xhat
        rstd_ref[...] = rstd
        x2b_ref[...] = (xhat * g2_ref[...] + b2_ref[...]).astype(MXU)

    tok = lambda w: pl.BlockSpec((TM, w), lambda i: (i, 0))
    vec = _full((1, D_MODEL))
    outs = [jax.ShapeDtypeStruct((t, D_MODEL), F32), jax.ShapeDtypeStruct((t, 1), F32), jax.ShapeDtypeStruct((t, D_MODEL), MXU)]
    return pl.pallas_call(
        body, name="ffn_out_ln2", grid=(t // TM,),
        in_specs=[pl.BlockSpec((N_SHARD, TM, FF_BLK), lambda i: (0, i, 0)), _full(w_down.shape), tok(D_MODEL), vec, vec, vec, vec],
        out_specs=[tok(D_MODEL), tok(1), tok(D_MODEL)], out_shape=outs,
        compiler_params=_cp(dimension_semantics=("arbitrary",)),
    )(f, w_down, xhat1, ln1_g, ln1_b, ln2_g, ln2_b)


STAT_ROWS = 8


def _ple_loss_bwd(xhat2, rstd2, p, target, ln2_g, ln2_b, w_g, b_g, w_p, ln3_g, ln3_b):
    t = xhat2.shape[0]

    def body(xh2_ref, rs2_ref, p_ref, t_ref, g2_ref, b2_ref, wg_ref, bg_ref, wp_ref, g3_ref, b3_ref,
             dr2_ref, dgp_ref, dpp_ref, stat_ref, pp_scr):
        @pl.when(pl.program_id(0) == 0)
        def _():
            stat_ref[...] = jnp.zeros_like(stat_ref)

        xhat2 = xh2_ref[...]
        x2 = xhat2 * g2_ref[...] + b2_ref[...]
        gate = jax.nn.sigmoid(jnp.dot(x2.astype(MXU), wg_ref[...], preferred_element_type=F32) + bg_ref[...])
        pb = p_ref[...].astype(MXU)
        for j in range(N_SHARD):
            pp_scr[:, j * ROW_BLK:(j + 1) * ROW_BLK] = jnp.dot(pb, wp_ref[j], preferred_element_type=F32)
        pp = pp_scr[...]
        xhat3, rstd3 = _ln_fwd(ALPHA * x2 + gate * pp)
        err = xhat3 * g3_ref[...] + b3_ref[...] - t_ref[...]
        dy = err * (1.0 / D_MODEL)
        dr3 = _ln_bwd(dy, xhat3, rstd3, g3_ref[...])
        dgp = dr3 * pp * gate * (1.0 - gate)
        dgp_ref[...] = dgp.astype(MXU)
        dpp_ref[...] = (dr3 * gate).astype(MXU)
        dx2 = ALPHA * dr3 + _dot_nt(dgp, wg_ref[...])
        dr2_ref[...] = _ln_bwd(dx2, xhat2, rs2_ref[...], g2_ref[...])
        stat_ref[0:1, :] += _colsum(dy * xhat3)
        stat_ref[1:2, :] += _colsum(dy)
        stat_ref[2:3, :] += _colsum(dgp)
        stat_ref[3:4, :] += _colsum(dx2 * xhat2)
        stat_ref[4:5, :] += _colsum(dx2)
        stat_ref[5:6, :] += _colsum(err * err)

    tok = lambda w: pl.BlockSpec((TM, w), lambda i: (i, 0))
    vec = _full((1, D_MODEL))
    outs = [jax.ShapeDtypeStruct((t, D_MODEL), F32), jax.ShapeDtypeStruct((t, D_MODEL), MXU), jax.ShapeDtypeStruct((t, D_MODEL), MXU),
            jax.ShapeDtypeStruct((STAT_ROWS, D_MODEL), F32)]
    return pl.pallas_call(
        body, name="ple_loss_bwd", grid=(t // TM,),
        in_specs=[tok(D_MODEL), tok(1), tok(D_PLE), tok(D_MODEL), vec, vec, _full(w_g.shape), vec, _full(w_p.shape), vec, vec],
        out_specs=[tok(D_MODEL), tok(D_MODEL), tok(D_MODEL), _full((STAT_ROWS, D_MODEL))], out_shape=outs,
        scratch_shapes=[pltpu.VMEM((TM, D_MODEL), F32)],
        compiler_params=_cp(dimension_semantics=("arbitrary",)),
    )(xhat2, rstd2, p, target, ln2_g, ln2_b, w_g, b_g, w_p, ln3_g, ln3_b)


def _ffn_bwd(dr2, a_pre, a, b, w_down, w_a, w_b, conv_w, xhat1, rstd1, ln1_g):
    t = dr2.shape[0]
    nt = t // TM
    hb = TM // HALO
    last_h = t // HALO - 1

    def body(dr_ref, drn_ref, ap_ref, a_ref, an_ref, b_ref, bn_ref, wd_ref, wa_ref, wb_ref, cw_ref,
             xh_ref, rs_ref, g1_ref, dap_ref, dbb_ref, dr1_ref, cstat_ref, lstat_ref, acc_scr):
        i, j = pl.program_id(0), pl.program_id(1)

        @pl.when((i == 0) & (j == 0))
        def _():
            cstat_ref[...] = jnp.zeros_like(cstat_ref)
            lstat_ref[...] = jnp.zeros_like(lstat_ref)

        half = TM // ROW_GROUPS
        parts = []
        for r0 in range(0, TM, half):
            rows = pl.ds(r0, half)
            last = r0 + half == TM

            def ext(ref, nxt):
                return jnp.concatenate([ref[rows], nxt[...]], axis=0) if last else ref[r0:r0 + half + HALO]

            df = _dot_nt(ext(dr_ref, drn_ref), wd_ref[...])
            a_ext, b_ext = ext(a_ref, an_ref), ext(b_ref, bn_ref)
            cdf = 0.5 * (1.0 + lax.erf(a_ext * (1.0 / math.sqrt(2.0))))
            pdf = jnp.exp(-0.5 * a_ext * a_ext) * (1.0 / math.sqrt(2.0 * math.pi))
            da = df * b_ext * (cdf + a_ext * pdf)
            if last:
                da = jnp.concatenate([da[0:half], jnp.where(i < nt - 1, da[half:], 0.0)], axis=0)
            ahead = [da[s:s + half] for s in range(3)]
            da_pre = cw_ref[2:3, :] * ahead[0] + cw_ref[1:2, :] * ahead[1] + cw_ref[0:1, :] * ahead[2]
            dbb = df[0:half] * (a_ext[0:half] * cdf[0:half])
            dap_ref[rows, :] = da_pre.astype(MXU)
            dbb_ref[rows, :] = dbb.astype(MXU)
            for kk in range(3):
                cstat_ref[j, kk:kk + 1, :] += _colsum(ahead[2 - kk] * ap_ref[rows, :])
            cstat_ref[j, 3:4, :] += _colsum(ahead[0])
            parts.append(_dot(da_pre, wa_ref[...]) + _dot(dbb, wb_ref[...]))
        part = jnp.concatenate(parts, axis=0)

        @pl.when(j == 0)
        def _():
            acc_scr[...] = ALPHA * dr_ref[...] + part

        @pl.when(j > 0)
        def _():
            acc_scr[...] += part

        @pl.when(j == N_SHARD - 1)
        def _():
            dx1 = acc_scr[...]
            xhat1 = xh_ref[...]
            lstat_ref[0:1, :] += _colsum(dx1 * xhat1)
            lstat_ref[1:2, :] += _colsum(dx1)
            dr1_ref[...] = _ln_bwd(dx1, xhat1, rs_ref[...], g1_ref[...])

    tok = lambda w: pl.BlockSpec((TM, w), lambda i, j: (i, 0))
    tokj = pl.BlockSpec((None, TM, FF_BLK), lambda i, j: (j, i, 0))
    nextj = pl.BlockSpec((None, HALO, FF_BLK), lambda i, j: (j, jnp.minimum((i + 1) * hb, last_h), 0))
    blk = lambda r, c: pl.BlockSpec((None, r, c), lambda i, j: (j, 0, 0))
    outs = [jax.ShapeDtypeStruct((N_SHARD, t, FF_BLK), MXU)] * 2 + [
        jax.ShapeDtypeStruct((t, D_MODEL), F32), jax.ShapeDtypeStruct((N_SHARD, STAT_ROWS, FF_BLK), F32),
        jax.ShapeDtypeStruct((STAT_ROWS, D_MODEL), F32)]
    return pl.pallas_call(
        body, name="ffn_bwd", grid=(nt, N_SHARD),
        in_specs=[tok(D_MODEL), pl.BlockSpec((HALO, D_MODEL), lambda i, j: (jnp.minimum((i + 1) * hb, last_h), 0)),
                  tokj, tokj, nextj, tokj, nextj, blk(FF_BLK, D_MODEL), blk(FF_BLK, D_MODEL), blk(FF_BLK, D_MODEL),
                  blk(3, FF_BLK), tok(D_MODEL), tok(1), _full((1, D_MODEL))],
        out_specs=[tokj, tokj, tok(D_MODEL), _full((N_SHARD, STAT_ROWS, FF_BLK)), _full((STAT_ROWS, D_MODEL))], out_shape=outs,
        scratch_shapes=[pltpu.VMEM((TM, D_MODEL), F32)],
        compiler_params=_cp(dimension_semantics=("arbitrary", "arbitrary")),
    )(dr2, dr2, a_pre, a, a, b, b, w_down, w_a, w_b, conv_w, xhat1, rstd1, ln1_g)


def _mix_bwd(dr1, w_o, hu, hz, mixed, attn, ln_z_g, ln_z_b, w_s, dep):
    t = dr1.shape[0]
    nchunk = TM // BLK

    def body(dr_ref, wo_ref, hu_ref, hz_ref, mx_ref, attn_ref, g_ref, b_ref, ws_ref, grp_ref, red_ref, dep_ref,
             do1_ref, do4_ref, do16_ref, dl1_ref, dl4_ref, dl16_ref, duz_ref, dws_ref, dbs_ref, zstat_ref,
             wm_scr, dzn_scr, dbsum_scr, do_scr, dl_scr):
        @pl.when(pl.program_id(0) == 0)
        def _():
            row = lax.broadcasted_iota(jnp.int32, (BLK, BLK), 0)
            col = lax.broadcasted_iota(jnp.int32, (BLK, BLK), 1)
            for g in range(N_HEADS):
                wm_scr[g] = jnp.where(col <= row, ws_ref[g], 0.0).astype(MXU)
            dws_ref[...] = jnp.zeros_like(dws_ref)
            dbsum_scr[...] = jnp.zeros_like(dbsum_scr)
            zstat_ref[...] = jnp.zeros_like(zstat_ref)

        dcat = _dot_nt(dr_ref[...], wo_ref[...])
        dattn = dcat[:, 0:D_ATTN]
        do1_ref[...] = dattn.astype(MXU)
        for cc, val in enumerate(_chunks(dattn)):
            do_scr[cc] = val
        _to_planes(do4_ref, do_scr, DILATIONS[1], LANE_CHUNKS, MXU)
        _to_planes(do16_ref, do_scr, DILATIONS[2], LANE_CHUNKS, MXU)
        delta = _dot_select(dattn * attn_ref[...], red_ref[...])
        dl1_ref[...] = delta
        dl_scr[0] = delta
        _to_planes(dl4_ref, dl_scr, DILATIONS[1], 1, F32)
        _to_planes(dl16_ref, dl_scr, DILATIONS[2], 1, F32)
        dgm = dcat[:, D_ATTN:]
        hu, hz = hu_ref[...], hz_ref[...]
        u = _gelu(hu)
        duz_ref[:, 0:D_GMLP] = (dgm * mx_ref[...] * _gelu_grad(hu)).astype(MXU)
        dmixed = dgm * u
        dmb = dmixed.astype(MXU)
        zhat, rstd = _ln_fwd(_gelu(hz))
        znb = (zhat * g_ref[...] + b_ref[...]).astype(MXU)
        dbs_acc = jnp.zeros((BLK, D_GMLP), F32)
        for ch in range(nchunk):
            rows = slice(ch * BLK, (ch + 1) * BLK)
            dbs_acc = dbs_acc + dmixed[rows]
            for g in range(N_HEADS):
                cols = slice(g * HEAD_DIM, (g + 1) * HEAD_DIM)
                dzn_scr[rows, cols] = _dot_tn(wm_scr[g], dmb[rows, cols])
                dws_ref[g] += _dot_nt(dmb[rows, cols], znb[rows, cols])
        dbsum_scr[...] += dbs_acc
        dzn = dzn_scr[...]
        zstat_ref[0:1, :] += _colsum(dzn * zhat)
        zstat_ref[1:2, :] += _colsum(dzn)
        duz_ref[:, D_GMLP:] = (_ln_bwd(dzn, zhat, rstd, g_ref[...]) * _gelu_grad(hz)).astype(MXU)

        @pl.when(pl.program_id(0) == nt - 1)
        def _():
            row = lax.broadcasted_iota(jnp.int32, (BLK, BLK), 0)
            col = lax.broadcasted_iota(jnp.int32, (BLK, BLK), 1)
            for g in range(N_HEADS):
                dws_ref[g] = jnp.where(col <= row, dws_ref[g], 0.0)
            dbs_ref[...] = lax.dot_general(grp_ref[...], dbsum_scr[...], (((1,), (1,)), ((), ())),
                                           precision=lax.Precision.HIGHEST, preferred_element_type=F32)

    nt = t // TM
    tok = lambda w: pl.BlockSpec((TM, w), lambda i: (i, 0))
    grp = jnp.asarray((np.arange(D_GMLP)[None, :] // HEAD_DIM == np.arange(N_HEADS)[:, None]).astype(np.float32))
    red = _head_reduce()
    outs = [_perm_shape(t, d, D_ATTN, MXU) for d in DILATIONS] + [_perm_shape(t, d, 128, F32) for d in DILATIONS] + [
        jax.ShapeDtypeStruct((t, 2 * D_GMLP), MXU),
        jax.ShapeDtypeStruct((N_HEADS, BLK, BLK), F32), jax.ShapeDtypeStruct((N_HEADS, BLK), F32),
        jax.ShapeDtypeStruct((STAT_ROWS, D_GMLP), F32)]
    return pl.pallas_call(
        body, name="mix_bwd", grid=(t // TM,),
        in_specs=[tok(D_MODEL), _full(w_o.shape), tok(D_GMLP), tok(D_GMLP), tok(D_GMLP), tok(D_ATTN), _full(ln_z_g.shape),
                  _full(ln_z_b.shape), _full(w_s.shape), _full(grp.shape), _full(red.shape), pl.BlockSpec(memory_space=pl.ANY)],
        out_specs=[_perm_tile(d, D_ATTN) for d in DILATIONS] + [_perm_tile(d, 128) for d in DILATIONS]
        + [tok(2 * D_GMLP), _full((N_HEADS, BLK, BLK)), _full((N_HEADS, BLK)), _full((STAT_ROWS, D_GMLP))],
        out_shape=outs,
        scratch_shapes=[pltpu.VMEM((N_HEADS, BLK, BLK), MXU), pltpu.VMEM((TM, D_GMLP), F32), pltpu.VMEM((BLK, D_GMLP), F32),
                        pltpu.VMEM((LANE_CHUNKS, TM, 128), F32), pltpu.VMEM((1, TM, 128), F32)],
        compiler_params=_cp(dimension_semantics=("arbitrary",)),
    )(dr1, w_o, hu, hz, mixed, attn, ln_z_g, ln_z_b, w_s, grp, red, dep)


def _dx_in(dqs, dks, dvs, duz, dr1, w_in, c_tab, s1_tab, s2_tab):
    t = dr1.shape[0]

    def body(dq1, dq4, dq16, dk1, dk4, dk16, dv1, dv4, dv16, duz_ref, dr_ref, w_ref, c_ref, s1_ref, s2_ref,
             dh_ref, dx_ref, acc_scr):
        sums = []
        for part, (g1, g4, g16) in enumerate(((dq1, dq4, dq16), (dk1, dk4, dk16), (dv1, dv4, dv16))):
            acc = acc_scr.at[pl.ds(part * LANE_CHUNKS, LANE_CHUNKS)]
            for cc in range(LANE_CHUNKS):
                acc[cc] = g1[:, cc * 128:(cc + 1) * 128]
            _from_planes(g4, acc, DILATIONS[1], LANE_CHUNKS, accumulate=True)
            _from_planes(g16, acc, DILATIONS[2], LANE_CHUNKS, accumulate=True)
            sums.append(_unchunk(acc_scr, LANE_CHUNKS, part * LANE_CHUNKS))
        c, s1, s2 = _tile_heads(c_ref[...]), _tile_heads(s1_ref[...]), _tile_heads(s2_ref[...])
        dh_ref[:, 0:D_ATTN] = _rope_apply_t(sums[0] * (1.0 / math.sqrt(HEAD_DIM)), c, s1, s2).astype(MXU)
        dh_ref[:, D_ATTN:2 * D_ATTN] = _rope_apply_t(sums[1], c, s1, s2).astype(MXU)
        dh_ref[:, 2 * D_ATTN:3 * D_ATTN] = sums[2].astype(MXU)
        dh_ref[:, 3 * D_ATTN:] = duz_ref[...]
        dx = ALPHA * dr_ref[...]
        for j in range(N_SHARD):
            dx = dx + _dot_nt(dh_ref[:, j * W_IN_BLK:(j + 1) * W_IN_BLK], w_ref[j])
        dx_ref[...] = dx

    tok = lambda w: pl.BlockSpec((TM, w), lambda i: (i, 0))
    outs = [jax.ShapeDtypeStruct((t, D_IN), MXU), jax.ShapeDtypeStruct((t, D_MODEL), F32)]
    return pl.pallas_call(
        body, name="dx_in", grid=(t // TM,),
        in_specs=[_perm_tile(d, D_ATTN) for d in DILATIONS] * 3
        + [tok(2 * D_GMLP), tok(D_MODEL), _full(w_in.shape), tok(128), tok(128), tok(128)],
        out_specs=[tok(D_IN), tok(D_MODEL)], out_shape=outs,
        scratch_shapes=[pltpu.VMEM((3 * LANE_CHUNKS, TM, 128), F32)],
        compiler_params=_cp(dimension_semantics=("arbitrary",)),
    )(*dqs, *dks, *dvs, duz, dr1, w_in, c_tab, s1_tab, s2_tab)


def _wgrad(name, x, dy, x_spec, dy_spec, out_spec, out_shape, grid, dep=None):
    deps = [] if dep is None else [dep]

    def body(x_ref, dy_ref, *rest):
        rest[-1][...] = _dot_tn(x_ref[...], dy_ref[...])

    return pl.pallas_call(
        body, name=name, grid=grid, in_specs=[x_spec, dy_spec] + [pl.BlockSpec(memory_space=pl.ANY)] * len(deps),
        out_specs=out_spec, out_shape=jax.ShapeDtypeStruct(out_shape, F32),
        compiler_params=_cp(dimension_semantics=("arbitrary",) * len(grid)),
    )(x, dy, *deps)


def _wgrad_pair(name, xa, xb, dy, x_spec, dy_spec, out_spec, out_shape, grid):
    def body(xa_ref, xb_ref, dy_ref, oa_ref, ob_ref):
        dy = dy_ref[...]
        oa_ref[...] = _dot_tn(xa_ref[...], dy)
        ob_ref[...] = _dot_tn(xb_ref[...], dy)

    return pl.pallas_call(
        body, name=name, grid=grid, in_specs=[x_spec, x_spec, dy_spec], out_specs=[out_spec, out_spec],
        out_shape=[jax.ShapeDtypeStruct(out_shape, F32)] * 2,
        compiler_params=_cp(dimension_semantics=("arbitrary",) * len(grid)),
    )(xa, xb, dy)


def _local_step(x, p, rope, target, w_in, start_dep, late_landed, late_weights, early_grads, early_grads_sent,
                early_grads_landed,
                ln_z_g, ln_z_b, w_s, b_s, ln1_g, ln1_b, conv_b, ln2_g, ln2_b, b_g, ln3_g, ln3_b):
    t = x.shape[0]
    half = TM
    c_tab, s1_tab, s2_tab = rope
    b_full = jnp.repeat(jnp.transpose(b_s[0]), HEAD_DIM, axis=1)
    conv_b4 = conv_b.reshape(N_SHARD, 1, FF_BLK)
    *qkvs, hu, hz, mixed, gm = _qkvuz(x, w_in, c_tab, s1_tab, s2_tab, ln_z_g, ln_z_b, w_s[0], b_full, start_dep)
    branches = [_attn_fwd(qkv, d, start_dep) for qkv, d in zip(qkvs[:2], DILATIONS[:2])]
    dep = late_landed(branches[-1][1])
    branches.append(_attn_fwd(qkvs[2], DILATIONS[2], dep))
    w_o, w_a, w_b, conv_w, w_down, w_g, w_p = late_weights(branches[-1][1])
    attn, *lses, cat, xhat1, rstd1, x1b = _mix_ln1(
        [o for o, _ in branches], [l for _, l in branches], gm, x, w_o, ln1_g, ln1_b, dep)
    a_pre, a_act, b_act, f = _ffn_in(x1b, w_a, w_b, conv_w, conv_b4)
    xhat2, rstd2, x2b = _ffn_out_ln2(f, w_down, xhat1, ln1_g, ln1_b, ln2_g, ln2_b)
    dr2, dgp, dpp, stat3 = _ple_loss_bwd(xhat2, rstd2, p, target, ln2_g, ln2_b, w_g, b_g, w_p, ln3_g, ln3_b)
    da_pre, dbb, dr1, cstat, stat1 = _ffn_bwd(dr2, a_pre, a_act, b_act, w_down, w_a, w_b, conv_w, xhat1, rstd1, ln1_g)

    full_t = lambda w, im: pl.BlockSpec((t, w), im)
    ffj = pl.BlockSpec((None, t, FF_BLK), lambda j, kk: (j, 0, 0))
    early = dict(
        w_ple_gate=_wgrad("dw_g", x2b, dgp, full_t(half, lambda kk, n: (0, kk)), full_t(half, lambda kk, n: (0, n)),
                          pl.BlockSpec((half, half), lambda kk, n: (kk, n)), (D_MODEL, D_MODEL), (2, 2)),
        w_ple_in=_wgrad("dw_p", p, dpp, full_t(D_PLE, lambda j: (0, 0)), full_t(ROW_BLK, lambda j: (0, j)),
                        pl.BlockSpec((None, D_PLE, ROW_BLK), lambda j: (j, 0, 0)), (N_SHARD, D_PLE, ROW_BLK), (N_SHARD,)),
        w_ff_down=_wgrad("dw_down", f, dr2, ffj, full_t(half, lambda j, n: (0, n)),
                         pl.BlockSpec((None, FF_BLK, half), lambda j, n: (j, 0, n)), (N_SHARD, FF_BLK, D_MODEL), (N_SHARD, 2)),
        **dict(zip(("w_ff_a", "w_ff_b"), _wgrad_pair(
            "dw_ab", da_pre, dbb, x1b, ffj, full_t(half, lambda j, n: (0, n)),
            pl.BlockSpec((None, FF_BLK, half), lambda j, n: (j, 0, n)), (N_SHARD, FF_BLK, D_MODEL), (N_SHARD, 2)))),
        w_o=_wgrad("dw_o", cat, dr1, full_t(half, lambda kk, n: (0, kk)), full_t(half, lambda kk, n: (0, n)),
                   pl.BlockSpec((half, half), lambda kk, n: (kk, n)), (D_MODEL, D_MODEL), (2, 2)))
    dep = early_grads(early)

    do1, do4, do16, dl1, dl4, dl16, duz, dws, dbs, zstat = _mix_bwd(
        dr1, w_o, hu, hz, mixed, attn, ln_z_g, ln_z_b, w_s[0], dep)
    dep = early_grads_sent(duz, (stat3, stat1, zstat, cstat, dws, dbs))
    dqkv = [_attn_bwd(qkv, do, lse, dl, d, dep)
            for qkv, do, lse, dl, d in zip(qkvs, (do1, do4, do16), lses, (dl1, dl4, dl16), DILATIONS)]
    dh, grad_x = _dx_in([g[0] for g in dqkv], [g[1] for g in dqkv], [g[2] for g in dqkv], duz, dr1, w_in,
                        c_tab, s1_tab, s2_tab)
    dep = early_grads_landed(grad_x)
    g_w_in = _wgrad("dw_in", x, dh, full_t(half, lambda j, kk: (0, kk)), full_t(W_IN_BLK, lambda j, kk: (0, j)),
                    pl.BlockSpec((None, half, W_IN_BLK), lambda j, kk: (j, kk, 0)), (N_SHARD, D_MODEL, W_IN_BLK), (N_SHARD, 2),
                    dep)
    return grad_x, g_w_in


def _tile_rows(rows, mult, steps):
    if rows % mult:
        return rows
    return next(rows // k for k in range(steps, rows + 1) if rows % k == 0 and (rows // k) % mult == 0)


def _grid_spec(grid, in_specs, out_specs):
    return pltpu.PrefetchScalarGridSpec(num_scalar_prefetch=1, grid=grid, in_specs=in_specs, out_specs=out_specs)


def _on_own_steps(i, count, steps, work):
    if count == steps:
        work()
    else:
        pl.when(i < count)(work)


def _place_shards(name, ws, dtypes, place, dep):
    n = len(ws)
    tiles = [_tile_rows(w.shape[0], 16, 8) for w in ws]
    counts = [w.shape[0] // t for w, t in zip(ws, tiles)]
    steps = max(counts)

    def body(s_ref, *refs):
        i = pl.program_id(0)
        for a in range(n):
            def work(a=a):
                refs[n + 1 + a][...] = refs[a][...].astype(dtypes[a])
            _on_own_steps(i, counts[a], steps, work)

    def tile(a, lead):
        last = counts[a] - 1
        if lead:
            return pl.BlockSpec((None, tiles[a], ws[a].shape[1]), lambda i, s: (s[0], jnp.minimum(i, last), 0))
        return pl.BlockSpec((tiles[a], ws[a].shape[1]), lambda i, s: (jnp.minimum(i, last), 0))

    return pl.pallas_call(
        body, name=name,
        grid_spec=_grid_spec((steps,), [tile(a, False) for a in range(n)] + [pl.BlockSpec(memory_space=pl.ANY)],
                             [tile(a, True) for a in range(n)]),
        out_shape=[jax.ShapeDtypeStruct((N_SHARD, *w.shape), dt) for w, dt in zip(ws, dtypes)],
        compiler_params=_cp())(place, *ws, dep)


def _pair_sums(name, mines, gots, place):
    n = len(mines)
    tiles = [_tile_rows(g.shape[1], 16, 2) for g in gots]
    per_blk = [g.shape[1] // t for g, t in zip(gots, tiles)]
    counts = [N_SHARD * nh for nh in per_blk]
    steps = max(counts)

    def body(s_ref, *refs):
        i = pl.program_id(0)
        for a in range(n):
            def work(a=a):
                refs[2 * n + a][...] = (refs[a][...] + refs[n + a][...]).astype(BF16)
            _on_own_steps(i, counts[a], steps, work)

    def tile(a, mine):
        nh, last = per_blk[a], counts[a] - 1

        def index(i, s):
            g = jnp.minimum(i, last)
            return (g // nh, (s[1] * nh if mine else 0) + g % nh, 0)

        return pl.BlockSpec((None, tiles[a], gots[a].shape[2]), index)

    return pl.pallas_call(
        body, name=name,
        grid_spec=_grid_spec((steps,), [tile(a, True) for a in range(n)] + [tile(a, False) for a in range(n)],
                             [tile(a, False) for a in range(n)]),
        out_shape=[jax.ShapeDtypeStruct(g.shape, BF16) for g in gots], compiler_params=_cp())(place, *mines, *gots)


def _chip_sums(name, owns, landeds, place, dep):
    n = len(owns)
    tiles = [_tile_rows(o.shape[1], 16, 8) for o in owns]
    counts = [o.shape[1] // t for o, t in zip(owns, tiles)]
    steps = max(counts)

    def body(s_ref, *refs):
        i = pl.program_id(0)
        for a in range(n):
            def work(a=a):
                own, l1, l2, l3 = (refs[4 * a + k][...].astype(F32) for k in range(4))
                refs[4 * n + 1 + a][...] = ((own + l1) + l2) + l3
            _on_own_steps(i, counts[a], steps, work)

    def slot(a, d):
        last = counts[a] - 1
        return pl.BlockSpec((None, tiles[a], owns[a].shape[2]), lambda i, s: ((s[0] + d) % N_SHARD, jnp.minimum(i, last), 0))

    def out(a):
        nh, last = counts[a], counts[a] - 1
        return pl.BlockSpec((tiles[a], owns[a].shape[2]), lambda i, s: (s[1] * nh + jnp.minimum(i, last), 0))

    operands = [x for o, l in zip(owns, landeds) for x in (o, l, l, l)]
    return pl.pallas_call(
        body, name=name,
        grid_spec=_grid_spec((steps,), [slot(a, d) for a in range(n) for d in range(4)] + [pl.BlockSpec(memory_space=pl.ANY)],
                             [out(a) for a in range(n)]),
        out_shape=[jax.ShapeDtypeStruct((2 * o.shape[1], o.shape[2]), F32) for o in owns],
        compiler_params=_cp())(place, *operands, dep)


def _adamw_math(w, g, m, v):
    m = ADAM_B1 * m + (1.0 - ADAM_B1) * g
    v = ADAM_B2 * v + (1.0 - ADAM_B2) * (g * g)
    m_hat = m / (1.0 - ADAM_B1 ** ADAM_STEP)
    v_hat = v / (1.0 - ADAM_B2 ** ADAM_STEP)
    delta = -ADAM_LR * (m_hat / (jnp.sqrt(v_hat) + ADAM_EPS) + ADAM_WD * w)
    return delta, m, v


def _adamw_shards(name, ws, gs, ms, vs):
    n = len(ws)
    tiles = [_tile_rows(w.shape[1], 8, 8) for w in ws]
    counts = [w.shape[1] // t for w, t in zip(ws, tiles)]
    steps = max(counts)

    def body(*refs):
        i = pl.program_id(0)
        for a in range(n):
            def work(a=a):
                w_ref, g_ref, m_ref, v_ref = refs[4 * a:4 * a + 4]
                d_ref, nm_ref, nv_ref = refs[4 * n + 3 * a:4 * n + 3 * a + 3]
                d_ref[...], nm_ref[...], nv_ref[...] = _adamw_math(w_ref[...], g_ref[...], m_ref[...], v_ref[...])
            _on_own_steps(i, counts[a], steps, work)

    def tile(a, lead):
        last, c = counts[a] - 1, ws[a].shape[2]
        if lead:
            return pl.BlockSpec((None, tiles[a], c), lambda i: (0, jnp.minimum(i, last), 0))
        return pl.BlockSpec((tiles[a], c), lambda i: (jnp.minimum(i, last), 0))

    res = pl.pallas_call(
        body, name=name, grid=(steps,),
        in_specs=[tile(a, lead) for a in range(n) for lead in (True, False, True, True)],
        out_specs=[tile(a, True) for a in range(n) for _ in range(3)],
        out_shape=[jax.ShapeDtypeStruct(w.shape, F32) for w in ws for _ in range(3)],
        compiler_params=_cp())(*[x for quad in zip(ws, gs, ms, vs) for x in quad])
    return [tuple(res[3 * a:3 * a + 3]) for a in range(n)]


MESH = pl.DeviceIdType.MESH
ANY = pl.BlockSpec(memory_space=pl.ANY)


def _place():
    x, y, c = lax.axis_index("x"), lax.axis_index("y"), lax.axis_index("c")
    chips = [(1 - x, y), (x, 1 - y), (1 - x, 1 - y)]
    return x, y, c, 2 * x + y, chips


def _remote(src, dst, send_sem, recv_sem, dev):
    return pltpu.make_async_remote_copy(src_ref=src, dst_ref=dst, send_sem=send_sem, recv_sem=recv_sem,
                                        device_id=dev, device_id_type=MESH)


def _half(ref, hc, rows):
    return ref.at[pl.ds(hc * (rows // 2), rows // 2)]


def _sibling_join(blocks, tag):
    n = len(blocks)

    def body(*refs):
        outs = refs[n:2 * n]
        send, recv = refs[2 * n:]
        x, y, c, _, _ = _place()
        cps = []
        for a in range(n):
            h = blocks[a].shape[0] // 2
            mine = outs[a].at[pl.ds(c * h, h)]
            cp = _remote(mine, mine, send.at[a], recv.at[a], (x, y, 1 - c))
            cp.start()
            cps.append(cp)
        for a, cp in enumerate(cps):
            h = blocks[a].shape[0] // 2
            theirs = outs[a].at[pl.ds((1 - c) * h, h)]
            _remote(theirs, theirs, send.at[a], recv.at[a], (x, y, 1 - c)).wait_recv()
            cp.wait_send()

    sem = pltpu.SemaphoreType.DMA
    return pl.pallas_call(body, name=f"rs_sibling_join_{tag}", in_specs=[ANY] * n, out_specs=[ANY] * n,
                          out_shape=[jax.ShapeDtypeStruct(b_.shape, b_.dtype) for b_ in blocks],
                          input_output_aliases={a: a for a in range(n)},
                          scratch_shapes=[sem((n,)), sem((n,))])(*blocks)


def _join_start(blocks, after, tag):
    n = len(blocks)

    def body(*refs):
        ins = refs[:n]
        send, recv = refs[n + 1], refs[n + 2]
        token = refs[2 * n + 3]
        x, y, c, _, _ = _place()
        for a in range(n):
            h = blocks[a].shape[0] // 2
            mine = ins[a].at[pl.ds(c * h, h)]
            _remote(mine, mine, send.at[a], recv.at[a], (x, y, 1 - c)).start()
        token[...] = jnp.zeros_like(token)

    sems = pltpu.SemaphoreType.DMA((n,))
    res = pl.pallas_call(
        body, name=f"join_start_{tag}", in_specs=[HBM] * n + [ANY],
        out_specs=[SEM, SEM] + [HBM] * n + [pl.BlockSpec(memory_space=pltpu.VMEM)],
        out_shape=[sems, sems] + [pltpu.HBM(b_.shape, b_.dtype) for b_ in blocks] + [TOKEN],
        input_output_aliases={a: a + 2 for a in range(n)}, compiler_params=_in_flight_params(),
    )(*[_in_hbm(b_) for b_ in blocks], after)
    return res[0], res[1], res[2:2 + n], res[2 + n]


def _join_wait(send, recv, blocks, after, tag):
    n = len(blocks)

    def body(*refs):
        ins = refs[:n]
        send_ref, recv_ref = refs[n], refs[n + 1]
        x, y, c, _, _ = _place()
        for a in range(n):
            h = blocks[a].shape[0] // 2
            mine, theirs = ins[a].at[pl.ds(c * h, h)], ins[a].at[pl.ds((1 - c) * h, h)]
            _remote(mine, mine, send_ref.at[a], recv_ref.at[a], (x, y, 1 - c)).wait_send()
            _remote(theirs, theirs, send_ref.at[a], recv_ref.at[a], (x, y, 1 - c)).wait_recv()

    return pl.pallas_call(
        body, name=f"join_wait_{tag}", in_specs=[HBM] * n + [SEM, SEM, ANY], out_specs=[HBM] * n,
        out_shape=[pltpu.HBM(b_.shape, b_.dtype) for b_ in blocks],
        input_output_aliases={a: a for a in range(n)}, compiler_params=_in_flight_params(),
    )(*blocks, send, recv, after)


HBM = pl.BlockSpec(memory_space=pltpu.HBM)
SEM = pl.BlockSpec(memory_space=pltpu.SEMAPHORE)
TOKEN = jax.ShapeDtypeStruct((8, 128), F32)


def _in_flight_params():
    return pltpu.CompilerParams(has_side_effects=pltpu.SideEffectType.DATAFLOW_SIDE_EFFECTING)


def _in_hbm(a):
    return pltpu.with_memory_space_constraint(a, pltpu.HBM)


def _gather_piece(ref, rows, split, slot, hc):
    return _half(ref.at[slot], hc, rows) if split else ref.at[slot]


def _gather_start(stacks, split, after, tag):
    n = len(stacks)

    def body(*refs):
        ins = refs[:n]
        send, recv = refs[n + 1], refs[n + 2]
        token = refs[2 * n + 3]
        _, _, c, j, chips = _place()
        for a in range(n):
            mine = _gather_piece(ins[a], stacks[a].shape[1], split[a], j, c)
            for t in range(3):
                _remote(mine, mine, send.at[3 * a + t], recv.at[3 * a + t], (*chips[t], c)).start()
        token[...] = jnp.zeros_like(token)

    sems = pltpu.SemaphoreType.DMA((3 * n,))
    res = pl.pallas_call(
        body, name=f"gather_start_{tag}", in_specs=[HBM] * n + [ANY],
        out_specs=[SEM, SEM] + [HBM] * n + [pl.BlockSpec(memory_space=pltpu.VMEM)],
        out_shape=[sems, sems] + [pltpu.HBM(s.shape, s.dtype) for s in stacks] + [TOKEN],
        input_output_aliases={a: a + 2 for a in range(n)}, compiler_params=_in_flight_params(),
    )(*[_in_hbm(s) for s in stacks], after)
    return res[0], res[1], res[2:2 + n], res[2 + n]


def _gather_wait(send, recv, stacks, split, after, tag):
    n = len(stacks)

    def body(*refs):
        ins = refs[:n]
        send_ref, recv_ref = refs[n], refs[n + 1]
        _, _, c, j, chips = _place()
        for a in range(n):
            rows = stacks[a].shape[1]
            mine = _gather_piece(ins[a], rows, split[a], j, c)
            for t, (px, py) in enumerate(chips):
                theirs = _gather_piece(ins[a], rows, split[a], 2 * px + py, c)
                _remote(mine, mine, send_ref.at[3 * a + t], recv_ref.at[3 * a + t], (px, py, c)).wait_send()
                _remote(theirs, theirs, send_ref.at[3 * a + t], recv_ref.at[3 * a + t], (px, py, c)).wait_recv()

    return pl.pallas_call(
        body, name=f"gather_wait_{tag}", in_specs=[HBM] * n + [SEM, SEM, ANY], out_specs=[HBM] * n,
        out_shape=[pltpu.HBM(s.shape, s.dtype) for s in stacks],
        input_output_aliases={a: a for a in range(n)}, compiler_params=_in_flight_params(),
    )(*stacks, send, recv, after)


def _gather_forward(stacks, split, tag):
    idx = [a for a in range(len(stacks)) if split[a]]
    n = len(idx)

    def body(*refs):
        outs = refs[n:2 * n]
        send, recv = refs[2 * n:]
        x, y, c, _, chips = _place()
        sends = []
        for t, (px, py) in enumerate(chips):
            for a in range(n):
                blk = _half(outs[a].at[2 * px + py], c, stacks[idx[a]].shape[1])
                cp = _remote(blk, blk, send.at[a, t], recv.at[a, t], (x, y, 1 - c))
                cp.start()
                sends.append(cp)
        for t, (px, py) in enumerate(chips):
            for a in range(n):
                blk = _half(outs[a].at[2 * px + py], 1 - c, stacks[idx[a]].shape[1])
                _remote(blk, blk, send.at[a, t], recv.at[a, t], (x, y, 1 - c)).wait_recv()
        for cp in sends:
            cp.wait_send()

    sem = pltpu.SemaphoreType.DMA
    res = pl.pallas_call(
        body, name=f"gather_forward_{tag}", in_specs=[ANY] * n, out_specs=[ANY] * n,
        out_shape=[jax.ShapeDtypeStruct(stacks[a].shape, stacks[a].dtype) for a in idx],
        input_output_aliases={a: a for a in range(n)}, scratch_shapes=[sem((n, 3)), sem((n, 3))],
    )(*[stacks[a] for a in idx])
    out = list(stacks)
    for a, r in zip(idx, res):
        out[a] = r
    return out


def _forward_start(stacks, after, tag):
    n = len(stacks)

    def body(*refs):
        ins = refs[:n]
        send, recv = refs[n + 1], refs[n + 2]
        token = refs[2 * n + 3]
        x, y, c, _, chips = _place()
        for a in range(n):
            for t, (px, py) in enumerate(chips):
                blk = _half(ins[a].at[2 * px + py], c, stacks[a].shape[1])
                _remote(blk, blk, send.at[3 * a + t], recv.at[3 * a + t], (x, y, 1 - c)).start()
        token[...] = jnp.zeros_like(token)

    sems = pltpu.SemaphoreType.DMA((3 * n,))
    res = pl.pallas_call(
        body, name=f"forward_start_{tag}", in_specs=[HBM] * n + [ANY],
        out_specs=[SEM, SEM] + [HBM] * n + [pl.BlockSpec(memory_space=pltpu.VMEM)],
        out_shape=[sems, sems] + [pltpu.HBM(s.shape, s.dtype) for s in stacks] + [TOKEN],
        input_output_aliases={a: a + 2 for a in range(n)}, compiler_params=_in_flight_params(),
    )(*[_in_hbm(s) for s in stacks], after)
    return res[0], res[1], res[2:2 + n], res[2 + n]


def _forward_wait(send, recv, stacks, after, tag):
    n = len(stacks)

    def body(*refs):
        ins = refs[:n]
        send_ref, recv_ref = refs[n], refs[n + 1]
        x, y, c, _, chips = _place()
        for a in range(n):
            for t, (px, py) in enumerate(chips):
                mine = _half(ins[a].at[2 * px + py], c, stacks[a].shape[1])
                theirs = _half(ins[a].at[2 * px + py], 1 - c, stacks[a].shape[1])
                _remote(mine, mine, send_ref.at[3 * a + t], recv_ref.at[3 * a + t], (x, y, 1 - c)).wait_send()
                _remote(theirs, theirs, send_ref.at[3 * a + t], recv_ref.at[3 * a + t], (x, y, 1 - c)).wait_recv()

    return pl.pallas_call(
        body, name=f"forward_wait_{tag}", in_specs=[HBM] * n + [SEM, SEM, ANY], out_specs=[HBM] * n,
        out_shape=[pltpu.HBM(s.shape, s.dtype) for s in stacks],
        input_output_aliases={a: a for a in range(n)}, compiler_params=_in_flight_params(),
    )(*stacks, send, recv, after)


def _swap_start(grads, tag):
    n = len(grads)

    def body(*refs):
        ins, gots = refs[:n], refs[n:2 * n]
        send, recv = refs[2 * n], refs[2 * n + 1]
        token = refs[4 * n + 2]
        x, y, c, _, _ = _place()
        for a in range(n):
            h = grads[a].shape[1] // 2
            _remote(ins[a].at[:, pl.ds((1 - c) * h, h)], gots[a], send.at[a], recv.at[a], (x, y, 1 - c)).start()
        token[...] = jnp.zeros_like(token)

    sems = pltpu.SemaphoreType.DMA((n,))
    halves = [(g.shape[0], g.shape[1] // 2, g.shape[2]) for g in grads]
    res = pl.pallas_call(
        body, name=f"swap_start_{tag}", in_specs=[HBM] * (2 * n),
        out_specs=[SEM, SEM] + [HBM] * (2 * n) + [pl.BlockSpec(memory_space=pltpu.VMEM)],
        out_shape=[sems, sems] + [pltpu.HBM(g.shape, g.dtype) for g in grads] + [pltpu.HBM(s, F32) for s in halves] + [TOKEN],
        input_output_aliases={a: a + 2 for a in range(2 * n)}, compiler_params=_in_flight_params(),
    )(*[_in_hbm(g) for g in grads], *[_in_hbm(lax.empty(s, F32)) for s in halves])
    return res[0], res[1], res[2:2 + n], res[2 + n:2 + 2 * n], res[2 + 2 * n]


def _swap_wait(send, recv, grads, gots, after, tag):
    n = len(grads)

    def body(*refs):
        ins, lnd = refs[:n], refs[n:2 * n]
        send_ref, recv_ref = refs[2 * n], refs[2 * n + 1]
        x, y, c, _, _ = _place()
        for a in range(n):
            h = grads[a].shape[1] // 2
            cp = _remote(ins[a].at[:, pl.ds((1 - c) * h, h)], lnd[a], send_ref.at[a], recv_ref.at[a], (x, y, 1 - c))
            cp.wait_send()
            cp.wait_recv()

    bufs = [pltpu.HBM(g.shape, g.dtype) for g in grads] + [pltpu.HBM(g.shape, g.dtype) for g in gots]
    res = pl.pallas_call(
        body, name=f"swap_wait_{tag}", in_specs=[HBM] * (2 * n) + [SEM, SEM, ANY], out_specs=[HBM] * (2 * n),
        out_shape=bufs, input_output_aliases={a: a for a in range(2 * n)}, compiler_params=_in_flight_params(),
    )(*grads, *gots, send, recv, after)
    return res[:n], res[n:]


def _exchange_start(parts, tag):
    n = len(parts)

    def body(*refs):
        ins, lands = refs[:n], refs[n:2 * n]
        send, recv = refs[2 * n], refs[2 * n + 1]
        token = refs[4 * n + 2]
        _, _, c, j, chips = _place()
        for t, (px, py) in enumerate(chips):
            for a in range(n):
                _remote(ins[a].at[2 * px + py], lands[a].at[j], send.at[3 * a + t], recv.at[3 * a + t], (px, py, c)).start()
        token[...] = jnp.zeros_like(token)

    sems = pltpu.SemaphoreType.DMA((3 * n,))
    bufs = [pltpu.HBM(p.shape, p.dtype) for p in parts]
    res = pl.pallas_call(
        body, name=f"exchange_start_{tag}", in_specs=[HBM] * (2 * n),
        out_specs=[SEM, SEM] + [HBM] * (2 * n) + [pl.BlockSpec(memory_space=pltpu.VMEM)],
        out_shape=[sems, sems] + bufs + bufs + [TOKEN],
        input_output_aliases={a: a + 2 for a in range(2 * n)}, compiler_params=_in_flight_params(),
    )(*[_in_hbm(p) for p in parts], *[_in_hbm(lax.empty(p.shape, p.dtype)) for p in parts])
    return res[0], res[1], res[2:2 + n], res[2 + n:2 + 2 * n], res[2 + 2 * n]


def _exchange_wait(send, recv, parts, lands, after, tag):
    n = len(parts)

    def body(*refs):
        ins, lnd = refs[:n], refs[n:2 * n]
        send_ref, recv_ref = refs[2 * n], refs[2 * n + 1]
        _, _, c, j, chips = _place()
        for t, (px, py) in enumerate(chips):
            jt = 2 * px + py
            for a in range(n):
                _remote(ins[a].at[jt], lnd[a].at[j], send_ref.at[3 * a + t], recv_ref.at[3 * a + t], (px, py, c)).wait_send()
                _remote(ins[a].at[jt], lnd[a].at[jt], send_ref.at[3 * a + t], recv_ref.at[3 * a + t], (px, py, c)).wait_recv()

    bufs = [pltpu.HBM(p.shape, p.dtype) for p in parts]
    res = pl.pallas_call(
        body, name=f"exchange_wait_{tag}", in_specs=[HBM] * (2 * n) + [SEM, SEM, ANY], out_specs=[HBM] * (2 * n),
        out_shape=bufs + bufs, input_output_aliases={a: a for a in range(2 * n)}, compiler_params=_in_flight_params(),
    )(*parts, *lands, send, recv, after)
    return res[:n], res[n:]


def _small_chip_sums(arrs):
    n = len(arrs)

    def body(*refs):
        ins, outs = refs[:n], refs[n:2 * n]
        sib = refs[2 * n:3 * n]
        send, recv = refs[3 * n:]
        x, y, c, j, _ = _place()
        swaps = [_remote(ins[a], sib[a], send.at[a], recv.at[a], (x, y, 1 - c)) for a in range(n)]
        for cp in swaps:
            cp.start()
        for a in range(n):
            swaps[a].wait_recv()
            outs[a][j] = ins[a][...] + sib[a][...]
        for cp in swaps:
            cp.wait_send()

    sem = pltpu.SemaphoreType.DMA
    vm = pl.BlockSpec(memory_space=pltpu.VMEM)
    return pl.pallas_call(
        body, name="small_chip_sums", in_specs=[vm] * n, out_specs=[vm] * n,
        out_shape=[jax.ShapeDtypeStruct((N_SHARD, *a.shape), F32) for a in arrs],
        scratch_shapes=[pltpu.VMEM(a.shape, F32) for a in arrs] + [sem((n,)), sem((n,))],
        compiler_params=_cp(),
    )(*arrs)


def _small_totals(stacks):
    n = len(stacks)

    def body(*refs):
        for a in range(n):
            refs[n + a][...] = ((refs[a][0] + refs[a][1]) + refs[a][2]) + refs[a][3]

    return pl.pallas_call(body, name="small_totals", out_shape=[jax.ShapeDtypeStruct(s.shape[1:], F32) for s in stacks],
                          compiler_params=_cp())(*stacks)


SMALL_1024 = ("ln1_g", "ln1_b", "ln2_g", "ln2_b", "b_ple_gate", "ln3_g", "ln3_b")


def _adamw_small(red3, red1, redz, g_conv_w, redc, red_ws, red_bs, params):
    shape2d = {"ln_z_g": (1, D_GMLP), "ln_z_b": (1, D_GMLP), "w_s": (N_HEADS * BLK, BLK), "b_s": (N_HEADS, BLK),
               "conv_w": (3, FF_BLK), "conv_b": (N_SHARD, FF_BLK), **{k: (1, D_MODEL) for k in SMALL_1024}}
    names = list(shape2d)
    flat = [a.reshape(shape2d[k]) for k in names for a in params[k]]

    def body(r3, r1, rz, gcw, rc, rws, rbs, *refs):
        ins, outs = refs[:3 * len(names)], refs[3 * len(names):]

        def grad_of(k):
            if k == "w_s":
                return rws[...]
            if k == "b_s":
                return rbs[...]
            if k == "conv_w":
                return gcw[0:3, :]
            if k == "conv_b":
                return jnp.concatenate([rc[j * STAT_ROWS + 3:j * STAT_ROWS + 4, :] for j in range(N_SHARD)], axis=0)
            src, row = {"ln3_g": (r3, 0), "ln3_b": (r3, 1), "b_ple_gate": (r3, 2), "ln2_g": (r3, 3), "ln2_b": (r3, 4),
                        "ln1_g": (r1, 0), "ln1_b": (r1, 1), "ln_z_g": (rz, 0), "ln_z_b": (rz, 1)}[k]
            return src[row:row + 1, :]

        for i, k in enumerate(names):
            w_ref, m_ref, v_ref = ins[3 * i:3 * i + 3]
            g_ref, d_ref, nm_ref, nv_ref = outs[4 * i:4 * i + 4]
            g = grad_of(k)
            g_ref[...] = g
            d_ref[...], nm_ref[...], nv_ref[...] = _adamw_math(w_ref[...], g, m_ref[...], v_ref[...])

    res = pl.pallas_call(
        body, name="adamw_small",
        out_shape=[jax.ShapeDtypeStruct(shape2d[k], F32) for k in names for _ in range(4)],
        compiler_params=_cp(),
    )(red3, red1, redz, g_conv_w, redc, red_ws, red_bs, *flat)
    return {k: tuple(r.reshape(params[k][0].shape) for r in res[4 * i:4 * i + 4]) for i, k in enumerate(names)}


WEIGHTS = ("w_in", "ln_z_g", "ln_z_b", "w_s", "b_s", "w_o", "ln1_g", "ln1_b", "w_ff_a", "w_ff_b", "conv_w", "conv_b",
           "w_ff_down", "ln2_g", "ln2_b", "w_ple_gate", "b_ple_gate", "w_ple_in", "ln3_g", "ln3_b")
BIG = ("w_in", "w_o", "w_ff_a", "w_ff_b", "w_ff_down", "w_ple_gate", "w_ple_in")
TRANSPOSED = ("w_ff_a", "w_ff_b")
LATE = ("w_o", "w_ff_a", "w_ff_b", "w_ff_down", "w_ple_gate", "w_ple_in", "conv_w")


def kernel(x, p, positions, w_in, ln_z_g, ln_z_b, w_s, b_s, w_o, ln1_g, ln1_b, w_ff_a, w_ff_b, conv_w, conv_b, w_ff_down, ln2_g, ln2_b, w_ple_gate, b_ple_gate, w_ple_in, ln3_g, ln3_b, loss_target, m_w_in, m_ln_z_g, m_ln_z_b, m_w_s, m_b_s, m_w_o, m_ln1_g, m_ln1_b, m_w_ff_a, m_w_ff_b, m_conv_w, m_conv_b, m_w_ff_down, m_ln2_g, m_ln2_b, m_w_ple_gate, m_b_ple_gate, m_w_ple_in, m_ln3_g, m_ln3_b, v_w_in, v_ln_z_g, v_ln_z_b, v_w_s, v_b_s, v_w_o, v_ln1_g, v_ln1_b, v_w_ff_a, v_w_ff_b, v_conv_w, v_conv_b, v_w_ff_down, v_ln2_g, v_ln2_b, v_w_ple_gate, v_b_ple_gate, v_w_ple_in, v_ln3_g, v_ln3_b):
    args = locals()
    w = {k: args[k] for k in WEIGHTS}
    m = {k: args["m_" + k] for k in WEIGHTS}
    v = {k: args["v_" + k] for k in WEIGHTS}

    for k in TRANSPOSED:
        w[k], m[k], v[k] = (jnp.swapaxes(a, 1, 2) for a in (w[k], m[k], v[k]))

    chip = 2 * lax.axis_index("x") + lax.axis_index("y")
    place = jnp.stack([chip, lax.axis_index("c")]).astype(jnp.int32)
    stack = dict(zip(["w_in"], _place_shards("cast_w_in", [w["w_in"][0]], [MXU], place, place)))
    i_send, i_recv, in_flight, dep = _gather_start([stack["w_in"]], [True], place, "w_in")
    stack.update(zip(LATE, _place_shards("cast_late", [w[k][0] for k in LATE],
                                         [F32 if k == "conv_w" else MXU for k in LATE], place, dep)))
    split_late = [k != "conv_w" for k in LATE]
    g_send, g_recv, late_flight, start_dep = _gather_start([stack[k] for k in LATE], split_late, place, "late")
    rope = _rope_tables(positions, x.shape[1], start_dep)
    landed_in = _gather_wait(i_send, i_recv, in_flight, [True], rope[0], "w_in")
    w_in_full, = _gather_forward(landed_in, [True], "w_in")
    halves =[k for k, sp in zip(LATE, split_late) if sp]
    trips = {}

    def late_landed(after):
        fw = dict(zip(LATE, _gather_wait(g_send, g_recv, late_flight, split_late, after, "late")))
        trips["late"] = (fw, *_forward_start([fw[k] for k in halves], fw["conv_w"], "late"))
        return trips["late"][-1]

    def late_weights(after):
        fw, send, recv, flight, _ = trips["late"]
        fw.update(zip(halves, _forward_wait(send, recv, flight, after, "late")))
        return (fw["w_o"].reshape(D_MODEL, D_MODEL), fw["w_ff_a"], fw["w_ff_b"], fw["conv_w"], fw["w_ff_down"],
                fw["w_ple_gate"].reshape(D_MODEL, D_MODEL), fw["w_ple_in"])

    def swap_started(names, grads, tag):
        stacked = [g.reshape(N_SHARD, *w[k].shape[1:]) for k, g in zip(names, grads)]
        return (names, tag, *_swap_start(stacked, tag))

    def partial_sums(swap, after):
        names, tag, send, recv, stacked, gots, _ = swap
        stacked, got = _swap_wait(send, recv, stacked, gots, after, tag)
        pair = _pair_sums(f"rs_pair_{tag}", stacked, got, place)
        return (names, tag, *_exchange_start(pair, tag))

    def chip_summed(trip, after, dep):
        names, tag, send, recv, pair, lands, _ = trip
        pair, landed = _exchange_wait(send, recv, pair, lands, after, tag)
        return _chip_sums(f"rs_sum_{tag}", pair, landed, place, dep), names, tag

    def reduced(trip, after, dep):
        blocks, names, tag = chip_summed(trip, after, dep)
        return dict(zip(names, _sibling_join(blocks, tag)))

    def early_grads_landed(after):
        blocks, names, tag = chip_summed(trips["early"], after, trips["small"][-1])
        trips["join"] = (names, *_join_start(blocks, after, tag))
        return trips["join"][-1]

    def early_grads(grads):
        trips["swap"] = swap_started(list(grads), list(grads.values()), "early")
        return trips["swap"][-1]

    def early_grads_sent(after, small):
        trips["early"] = partial_sums(trips["swap"], after)
        stat3, stat1, zstat, cstat, dws, dbs = small
        sums = _small_chip_sums([stat3, stat1, zstat, cstat.reshape(N_SHARD * STAT_ROWS, FF_BLK),
                                 dws.reshape(N_HEADS * BLK, BLK), dbs])
        trips["small"] = _gather_start(sums, [False] * len(sums), trips["early"][-1], "small")
        return trips["small"][-1]

    grad_x, g_w_in = _local_step(
        x[0], p[0, 0], rope, loss_target[0], w_in_full, start_dep, late_landed, late_weights, early_grads, early_grads_sent,
        early_grads_landed, ln_z_g, ln_z_b, w_s, b_s, ln1_g, ln1_b, conv_b, ln2_g, ln2_b, b_ple_gate, ln3_g, ln3_b)

    trips["w_in"] = partial_sums(swap_started(["w_in"], [g_w_in], "w_in"), g_w_in)
    out = {}

    def adamw(red, tag):
        names = list(red)
        steps = _adamw_shards(f"adamw_{tag}", [w[k] for k in names], [red[k] for k in names], [m[k] for k in names],
                              [v[k] for k in names])
        for k, (d, nm, nv) in zip(names, steps):
            out[k] = (red[k].reshape(w[k].shape), d, nm, nv)

    names, j_send, j_recv, j_flight, _ = trips["join"]
    adamw(dict(zip(names, _join_wait(j_send, j_recv, j_flight, trips["w_in"][-1], "early"))), "early")
    adamw(reduced(trips["w_in"], out["w_o"][3], start_dep), "w_in")
    for k in TRANSPOSED:
        out[k] = tuple(jnp.swapaxes(a, 1, 2) for a in out[k])

    s_send, s_recv, s_flight, _ = trips["small"]
    red3, red1, redz, redc, red_ws, red_bs = _small_totals(
        _gather_wait(s_send, s_recv, s_flight, [False] * len(s_flight), out["w_in"][3], "small"))
    loss = (0.5 / D_MODEL) * jnp.sum(red3[5])
    g_conv_w = lax.dynamic_slice_in_dim(redc, chip * STAT_ROWS, STAT_ROWS, 0)
    names_small = [k for k in WEIGHTS if k not in BIG]
    out.update(_adamw_small(red3, red1, redz, g_conv_w, redc, red_ws, red_bs, {k: (w[k], m[k], v[k]) for k in names_small}))

    return (loss, grad_x[None], *[out[k][0] for k in WEIGHTS], *[out[k][1] for k in WEIGHTS],
            *[out[k][2] for k in WEIGHTS], *[out[k][3] for k in WEIGHTS])
```

```python
import functools
import math

import numpy as np
import jax
import jax.numpy as jnp
from jax import lax
from jax.experimental import pallas as pl
from jax.experimental.pallas import tpu as pltpu

F32 = jnp.float32
BF16 = jnp.bfloat16
MXU = BF16

D_MODEL = 1024
HEAD_DIM = 64
N_HEADS = 8
D_ATTN = 512
D_GMLP = 512
D_IN = 2560
DILATIONS = (1, 4, 16)
BLK = 128
ROPE_THETA = 500000.0
ROPE_DIM = 16
D_FF = 2816
D_PLE = 256
LN_EPS = 1e-5
ALPHA = 2.0 ** 0.25
NEG_INF = -1e30
N_SHARD = 4
W_IN_BLK = D_IN // N_SHARD
FF_BLK = D_FF // N_SHARD
ROW_BLK = D_MODEL // N_SHARD
ADAM_LR, ADAM_B1, ADAM_B2, ADAM_EPS, ADAM_WD, ADAM_STEP = 0.001, 0.9, 0.999, 1e-08, 0.01, 10

TM = 512
HALO = 8
ROW_GROUPS = 2
VMEM_LIMIT = 56 * 1024 * 1024


def _cp(**kw):
    return pltpu.CompilerParams(vmem_limit_bytes=VMEM_LIMIT, **kw)


def _full(shape):
    n = len(shape)
    return pl.BlockSpec(shape, lambda *_: (0,) * n)


def _gelu(x):
    return 0.5 * x * (1.0 + lax.erf(x * (1.0 / math.sqrt(2.0))))


def _gelu_grad(x):
    return 0.5 * (1.0 + lax.erf(x * (1.0 / math.sqrt(2.0)))) + x * jnp.exp(-0.5 * x * x) * (1.0 / math.sqrt(2.0 * math.pi))


def _ln_fwd(r):
    mu = jnp.mean(r, axis=-1, keepdims=True)
    xc = r - mu
    var = jnp.mean(xc * xc, axis=-1, keepdims=True)
    rstd = lax.rsqrt(var + LN_EPS)
    return xc * rstd, rstd


def _ln_bwd(dy, xhat, rstd, g):
    dxh = dy * g
    m1 = jnp.mean(dxh, axis=-1, keepdims=True)
    m2 = jnp.mean(dxh * xhat, axis=-1, keepdims=True)
    return rstd * (dxh - m1 - xhat * m2)


def _dot(a, b):
    return jnp.dot(a.astype(MXU), b.astype(MXU), preferred_element_type=F32)


def _dot_nt(a, b):
    return lax.dot_general(a.astype(MXU), b.astype(MXU), (((1,), (1,)), ((), ())), preferred_element_type=F32)


def _dot_tn(a, b):
    return lax.dot_general(a.astype(MXU), b.astype(MXU), (((0,), (0,)), ((), ())), preferred_element_type=F32)


def _colsum(v):
    return jnp.sum(v, axis=0, keepdims=True)


def _rope_tables(positions, t, dep):
    inv = np.float32(ROPE_THETA) ** (-np.arange(0, ROPE_DIM, 2, dtype=np.float32) / np.float32(ROPE_DIM))
    half = ROPE_DIM // 2
    pos_rep = jnp.repeat(positions.reshape(t // 16, 16), half, axis=1)
    inv_row = jnp.asarray(np.tile(inv, 16)[None, :], F32)

    def trig_body(pos_ref, inv_ref, dep_ref, cos_ref, sin_ref):
        ang = pos_ref[...].astype(F32) * inv_ref[...]
        cos_ref[...] = jnp.cos(ang)
        sin_ref[...] = jnp.sin(ang)

    vm = pl.BlockSpec(memory_space=pltpu.VMEM)
    cos8, sin8 = pl.pallas_call(
        trig_body, name="rope_trig", in_specs=[vm, vm, pl.BlockSpec(memory_space=pl.ANY)], out_specs=[vm, vm],
        out_shape=(jax.ShapeDtypeStruct((t // 16, 128), F32), jax.ShapeDtypeStruct((t // 16, 128), F32)),
    )(pos_rep, inv_row, dep)
    cos8 = cos8.reshape(t, half)
    sin8 = sin8.reshape(t, half)

    lane = np.arange(128) % HEAD_DIM
    sel = (np.arange(half)[:, None] == (lane % half)[None, :])
    e_cos = (sel & (lane < ROPE_DIM)[None, :]).astype(np.float32)
    e_s1 = -(sel & (lane < half)[None, :]).astype(np.float32)
    e_s2 = (sel & ((lane >= half) & (lane < ROPE_DIM))[None, :]).astype(np.float32)
    ones = (lane >= ROPE_DIM).astype(np.float32)[None, :]

    def expand_body(cos_ref, sin_ref, ec_ref, e1_ref, e2_ref, ones_ref, c_ref, s1_ref, s2_ref):
        hp = lax.Precision.HIGHEST
        c_ref[...] = jnp.dot(cos_ref[...], ec_ref[...], precision=hp, preferred_element_type=F32) + ones_ref[...]
        s1_ref[...] = jnp.dot(sin_ref[...], e1_ref[...], precision=hp, preferred_element_type=F32)
        s2_ref[...] = jnp.dot(sin_ref[...], e2_ref[...], precision=hp, preferred_element_type=F32)

    tab = jax.ShapeDtypeStruct((t, 128), F32)
    return pl.pallas_call(expand_body, name="rope_expand", out_shape=(tab, tab, tab), compiler_params=_cp())(
        cos8, sin8, jnp.asarray(e_cos), jnp.asarray(e_s1), jnp.asarray(e_s2), jnp.asarray(ones))


def _tile_heads(tab):
    return jnp.concatenate([tab] * (D_ATTN // 128), axis=1)


def _rope_apply(v, c, s1, s2):
    n = v.shape[1]
    half = ROPE_DIM // 2
    return v * c + pltpu.roll(v, n - half, 1) * s1 + pltpu.roll(v, half, 1) * s2


def _rope_apply_t(g, c, s1, s2):
    n = g.shape[1]
    half = ROPE_DIM // 2
    return g * c + pltpu.roll(g * s1, half, 1) + pltpu.roll(g * s2, n - half, 1)


LANE_CHUNKS = D_ATTN // 128
HEAD_LANES = 128 // N_HEADS


def _perm_shape(t, d, w, dtype):
    return jax.ShapeDtypeStruct((d, t // d, w), dtype)


def _perm_tile(d, w):
    return pl.BlockSpec((None if d == 1 else d, TM // d, w), lambda i: (0, i, 0))


def _to_planes(ref, scr, d, n_chunks, dtype):
    for r in range(d):
        for cc in range(n_chunks):
            ref[r, :, cc * 128:(cc + 1) * 128] = scr.at[cc][pl.ds(r, TM // d, stride=d), :].astype(dtype)


def _from_planes(ref, scr, d, n_chunks, accumulate=False):
    for r in range(d):
        for cc in range(n_chunks):
            rows = scr.at[cc]
            val = ref[r, :, cc * 128:(cc + 1) * 128].astype(F32)
            if accumulate:
                rows[pl.ds(r, TM // d, stride=d), :] += val
            else:
                rows[pl.ds(r, TM // d, stride=d), :] = val


def _chunks(val):
    return [val[:, cc * 128:(cc + 1) * 128] for cc in range(val.shape[1] // 128)]


def _unchunk(scr, n_chunks, base=0):
    return jnp.concatenate([scr[base + cc] for cc in range(n_chunks)], axis=1)


def _head_expand():
    src = np.arange(128)[:, None]
    dst = np.arange(D_ATTN)[None, :]
    return jnp.asarray((src == (dst // HEAD_DIM) * HEAD_LANES).astype(np.float32))


def _head_reduce():
    src = np.arange(D_ATTN)[:, None]
    dst = np.arange(128)[None, :]
    return jnp.asarray((src // HEAD_DIM == dst // HEAD_LANES).astype(np.float32))


def _dot_select(a, sel):
    hi = a.astype(BF16)
    lo = (a - hi.astype(F32)).astype(BF16)
    sel = sel.astype(BF16)
    return jnp.dot(hi, sel, preferred_element_type=F32) + jnp.dot(lo, sel, preferred_element_type=F32)


def _qkvuz(x, w_in, c_tab, s1_tab, s2_tab, ln_z_g, ln_z_b, w_s, b_full, dep):
    t = x.shape[0]
    nchunk = TM // BLK

    def body(x_ref, w_ref, c_ref, s1_ref, s2_ref, g_ref, b_ref, ws_ref, bf_ref, dep_ref,
             qkv1_ref, qkv4_ref, qkv16_ref, hu_ref, hz_ref, mixed_ref, gm_ref, h_scr, wm_scr, p_scr):
        @pl.when(pl.program_id(0) == 0)
        def _():
            row = lax.broadcasted_iota(jnp.int32, (BLK, BLK), 0)
            col = lax.broadcasted_iota(jnp.int32, (BLK, BLK), 1)
            for g in range(N_HEADS):
                wm_scr[g] = jnp.where(col <= row, ws_ref[g], 0.0).astype(MXU)

        xb = x_ref[...].astype(MXU)
        for j in range(N_SHARD):
            h_scr[:, j * W_IN_BLK:(j + 1) * W_IN_BLK] = jnp.dot(xb, w_ref[j], preferred_element_type=F32)
        c, s1, s2 = _tile_heads(c_ref[...]), _tile_heads(s1_ref[...]), _tile_heads(s2_ref[...])
        q = _rope_apply(h_scr[:, 0:D_ATTN], c, s1, s2) * (1.0 / math.sqrt(HEAD_DIM))
        k = _rope_apply(h_scr[:, D_ATTN:2 * D_ATTN], c, s1, s2)
        for part, val in enumerate((q, k, h_scr[:, 2 * D_ATTN:3 * D_ATTN])):
            qkv1_ref[:, part * D_ATTN:(part + 1) * D_ATTN] = val.astype(MXU)
            for cc in range(LANE_CHUNKS):
                p_scr[part * LANE_CHUNKS + cc] = val[:, cc * 128:(cc + 1) * 128]
        _to_planes(qkv4_ref, p_scr, DILATIONS[1], 3 * LANE_CHUNKS, MXU)
        _to_planes(qkv16_ref, p_scr, DILATIONS[2], 3 * LANE_CHUNKS, MXU)
        hu = h_scr[:, 3 * D_ATTN:3 * D_ATTN + D_GMLP]
        hz = h_scr[:, 3 * D_ATTN + D_GMLP:]
        hu_ref[...] = hu
        hz_ref[...] = hz
        zhat, _ = _ln_fwd(_gelu(hz))
        zn = (zhat * g_ref[...] + b_ref[...]).astype(MXU)
        for ch in range(nchunk):
            rows = slice(ch * BLK, (ch + 1) * BLK)
            for g in range(N_HEADS):
                cols = slice(g * HEAD_DIM, (g + 1) * HEAD_DIM)
                mixed_ref[rows, cols] = jnp.dot(wm_scr[g], zn[rows, cols], preferred_element_type=F32) + bf_ref[:, cols]
        gm_ref[...] = (_gelu(hu) * mixed_ref[...]).astype(MXU)

    tok = lambda w: pl.BlockSpec((TM, w), lambda i: (i, 0))
    outs = [_perm_shape(t, d, 3 * D_ATTN, MXU) for d in DILATIONS] + [jax.ShapeDtypeStruct((t, D_GMLP), F32)] * 3 + [
        jax.ShapeDtypeStruct((t, D_GMLP), MXU)]
    return pl.pallas_call(
        body, name="qkvuz", grid=(t // TM,),
        in_specs=[tok(D_MODEL), _full(w_in.shape), tok(128), tok(128), tok(128), _full(ln_z_g.shape), _full(ln_z_b.shape),
                  _full(w_s.shape), _full(b_full.shape), pl.BlockSpec(memory_space=pl.ANY)],
        out_specs=[_perm_tile(d, 3 * D_ATTN) for d in DILATIONS] + [tok(D_ATTN)] * 4, out_shape=outs,
        scratch_shapes=[pltpu.VMEM((TM, D_IN), F32), pltpu.VMEM((N_HEADS, BLK, BLK), MXU),
                        pltpu.VMEM((3 * LANE_CHUNKS, TM, 128), F32)],
        compiler_params=_cp(dimension_semantics=("arbitrary",)),
    )(x, w_in, c_tab, s1_tab, s2_tab, ln_z_g, ln_z_b, w_s, b_full, dep)


def _band_valid(n):
    i = lax.broadcasted_iota(jnp.int32, (BLK, 2 * BLK), 0)
    j = lax.broadcasted_iota(jnp.int32, (BLK, 2 * BLK), 1)
    return (j >= i) & (j <= i + BLK) & ((j >= BLK) | (n > 0))


def _attn_fwd(qkv, d, dep):
    _, l_sub, _ = qkv.shape
    nb = l_sub // BLK

    def body(q_ref, kp_ref, kc_ref, vp_ref, vc_ref, dep_ref, o_ref, l_ref):
        valid = _band_valid(pl.program_id(1))
        kcat = jnp.concatenate([kp_ref[...], kc_ref[...]], axis=0)
        vcat = jnp.concatenate([vp_ref[...], vc_ref[...]], axis=0)
        for h in range(N_HEADS):
            cols = slice(h * HEAD_DIM, (h + 1) * HEAD_DIM)
            s = jnp.where(valid, _dot_nt(q_ref[:, cols], kcat[:, cols]), NEG_INF)
            m = jnp.max(s, axis=-1, keepdims=True)
            e = jnp.exp(s - m)
            den = jnp.sum(e, axis=-1, keepdims=True)
            o_ref[:, cols] = _dot(e, vcat[:, cols]) * (1.0 / den)
            l_ref[:, h * HEAD_LANES:(h + 1) * HEAD_LANES] = jnp.broadcast_to(m + jnp.log(den), (BLK, HEAD_LANES))

    def blk(w, col, prev=False):
        return pl.BlockSpec((None, BLK, w), lambda r, n: (r, jnp.maximum(n - 1, 0) if prev else n, col))

    return pl.pallas_call(
        body, name=f"attn_fwd_d{d}", grid=(d, nb),
        in_specs=[blk(D_ATTN, 0), blk(D_ATTN, 1, True), blk(D_ATTN, 1), blk(D_ATTN, 2, True), blk(D_ATTN, 2),
                  pl.BlockSpec(memory_space=pl.ANY)],
        out_specs=[blk(D_ATTN, 0), blk(128, 0)],
        out_shape=[jax.ShapeDtypeStruct((d, l_sub, D_ATTN), F32), jax.ShapeDtypeStruct((d, l_sub, 128), F32)],
        compiler_params=_cp(dimension_semantics=("arbitrary", "arbitrary")),
    )(qkv, qkv, qkv, qkv, qkv, dep)


def _attn_bwd(qkv, do, lse, delta, d, dep):
    _, l_sub, _ = qkv.shape
    nb = l_sub // BLK
    whole = l_sub <= 8 * BLK

    def shares(n, q_ref, kp_ref, kc_ref, vp_ref, vc_ref, do_ref, l_ref, dl_ref, dq_ref):
        valid = _band_valid(n)
        kcat = jnp.concatenate([kp_ref[...], kc_ref[...]], axis=0)
        vcat = jnp.concatenate([vp_ref[...], vc_ref[...]], axis=0)
        for h in range(N_HEADS):
            cols = slice(h * HEAD_DIM, (h + 1) * HEAD_DIM)
            stat = slice(h * HEAD_LANES, h * HEAD_LANES + 1)
            qh, doh = q_ref[:, cols], do_ref[:, cols]
            p = jnp.where(valid, jnp.exp(_dot_nt(qh, kcat[:, cols]) - l_ref[:, stat]), 0.0)
            ds = p * (_dot_nt(doh, vcat[:, cols]) - dl_ref[:, stat])
            dq_ref[:, cols] = _dot(ds, kcat[:, cols])
            yield cols, _dot_tn(ds, qh), _dot_tn(p, doh)

    def body_whole(*refs):
        dk_ref, dv_ref = refs[10:]
        n = pl.program_id(1)
        cur = pl.ds(pl.multiple_of(n * BLK, BLK), BLK)
        prev = pl.ds(pl.multiple_of(jnp.maximum(n - 1, 0) * BLK, BLK), BLK)
        for cols, dk2, dv2 in shares(n, *refs[:8], refs[9]):
            dk_ref[cur, cols] = dk2[BLK:]
            dv_ref[cur, cols] = dv2[BLK:]
            dk_ref[prev, cols] += dk2[0:BLK]
            dv_ref[prev, cols] += dv2[0:BLK]

    def body_carry(*refs):
        dk_ref, dv_ref, ck_scr, cv_scr = refs[10:]
        n = pl.program_id(1)

        @pl.when(n == 0)
        def _():
            ck_scr[...] = jnp.zeros_like(ck_scr)
            cv_scr[...] = jnp.zeros_like(cv_scr)

        @pl.when(n < nb)
        def _():
            for cols, dk2, dv2 in shares(n, *refs[:8], refs[9]):
                dk_ref[:, cols] = ck_scr[:, cols] + dk2[0:BLK]
                dv_ref[:, cols] = cv_scr[:, cols] + dv2[0:BLK]
                ck_scr[:, cols] = dk2[BLK:]
                cv_scr[:, cols] = dv2[BLK:]

        @pl.when(n == nb)
        def _():
            dk_ref[...] = ck_scr[...]
            dv_ref[...] = cv_scr[...]

    def blk(w, col, shift=0):
        return pl.BlockSpec((None, BLK, w), lambda r, n: (r, jnp.clip(n - shift, 0, nb - 1), col))

    if whole:
        dkv_spec = pl.BlockSpec((None, l_sub, D_ATTN), lambda r, n: (r, 0, 0))
        body, steps, scratch = body_whole, nb, []
    else:
        dkv_spec = blk(D_ATTN, 0, 1)
        body, steps, scratch = body_carry, nb + 1, [pltpu.VMEM((BLK, D_ATTN), F32)] * 2
    return pl.pallas_call(
        body, name=f"attn_bwd_d{d}", grid=(d, steps),
        in_specs=[blk(D_ATTN, 0), blk(D_ATTN, 1, 1), blk(D_ATTN, 1), blk(D_ATTN, 2, 1), blk(D_ATTN, 2),
                  blk(D_ATTN, 0), blk(128, 0), blk(128, 0), pl.BlockSpec(memory_space=pl.ANY)],
        out_specs=[blk(D_ATTN, 0), dkv_spec, dkv_spec],
        out_shape=[jax.ShapeDtypeStruct((d, l_sub, D_ATTN), F32)] * 3,
        scratch_shapes=scratch,
        compiler_params=_cp(dimension_semantics=("arbitrary", "arbitrary")),
    )(qkv, qkv, qkv, qkv, qkv, do, lse, delta, dep)


def _mix_ln1(os_, ls_, gm, x, w_o, ln1_g, ln1_b, dep):
    t = x.shape[0]
    expand = _head_expand()

    def body(o1, o4, o16, l1, l4, l16, gm_ref, x_ref, wo_ref, g_ref, b_ref, ex_ref, dep_ref,
             attn_ref, lse1_ref, lse4_ref, lse16_ref, cat_ref, xhat_ref, rstd_ref, x1b_ref, o_scr, l_scr):
        _from_planes(o4, o_scr, DILATIONS[1], LANE_CHUNKS)
        _from_planes(o16, o_scr.at[pl.ds(LANE_CHUNKS, LANE_CHUNKS)], DILATIONS[2], LANE_CHUNKS)
        _from_planes(l4, l_scr, DILATIONS[1], 1)
        _from_planes(l16, l_scr.at[pl.ds(1, 1)], DILATIONS[2], 1)
        la, lb, lc = l1[...], l_scr[0], l_scr[1]
        m = jnp.maximum(jnp.maximum(la, lb), lc)
        ea, eb, ec = jnp.exp(la - m), jnp.exp(lb - m), jnp.exp(lc - m)
        den = ea + eb + ec
        inv = 1.0 / den
        wide = lambda w: _dot_select(w, ex_ref[...])
        attn = (wide(ea * inv) * o1[...] + wide(eb * inv) * _unchunk(o_scr, LANE_CHUNKS)
                + wide(ec * inv) * _unchunk(o_scr, LANE_CHUNKS, LANE_CHUNKS))
        attn_ref[...] = attn
        lse = m + jnp.log(den)
        lse1_ref[...] = lse
        l_scr[2] = lse
        _to_planes(lse4_ref, l_scr.at[pl.ds(2, 1)], DILATIONS[1], 1, F32)
        _to_planes(lse16_ref, l_scr.at[pl.ds(2, 1)], DILATIONS[2], 1, F32)
        cat_ref[:, 0:D_ATTN] = attn.astype(MXU)
        cat_ref[:, D_ATTN:] = gm_ref[...]
        mix = jnp.dot(cat_ref[...], wo_ref[...], preferred_element_type=F32)
        xhat, rstd = _ln_fwd(ALPHA * x_ref[...] + mix)
        xhat_ref[...] = xhat
        rstd_ref[...] = rstd
        x1b_ref[...] = (xhat * g_ref[...] + b_ref[...]).astype(MXU)

    tok = lambda w: pl.BlockSpec((TM, w), lambda i: (i, 0))
    outs = [jax.ShapeDtypeStruct((t, D_ATTN), F32)] + [_perm_shape(t, d, 128, F32) for d in DILATIONS] + [
        jax.ShapeDtypeStruct((t, D_MODEL), MXU), jax.ShapeDtypeStruct((t, D_MODEL), F32), jax.ShapeDtypeStruct((t, 1), F32),
        jax.ShapeDtypeStruct((t, D_MODEL), MXU)]
    return pl.pallas_call(
        body, name="mix_ln1", grid=(t // TM,),
        in_specs=[_perm_tile(d, D_ATTN) for d in DILATIONS] + [_perm_tile(d, 128) for d in DILATIONS]
        + [tok(D_GMLP), tok(D_MODEL), _full(w_o.shape), _full(ln1_g.shape), _full(ln1_b.shape), _full(expand.shape),
           pl.BlockSpec(memory_space=pl.ANY)],
        out_specs=[tok(D_ATTN)] + [_perm_tile(d, 128) for d in DILATIONS] + [tok(D_MODEL), tok(D_MODEL), tok(1), tok(D_MODEL)],
        out_shape=outs,
        scratch_shapes=[pltpu.VMEM((2 * LANE_CHUNKS, TM, 128), F32), pltpu.VMEM((3, TM, 128), F32)],
        compiler_params=_cp(dimension_semantics=("arbitrary",)),
    )(*os_, *ls_, gm, x, w_o, ln1_g, ln1_b, expand, dep)


def _conv_fwd(a_ext, w_ref, b_ref, rows):
    return (b_ref[...] + w_ref[2:3, :] * a_ext[HALO:HALO + rows] + w_ref[1:2, :] * a_ext[HALO - 1:HALO - 1 + rows]
            + w_ref[0:1, :] * a_ext[HALO - 2:HALO - 2 + rows])


def _ffn_in(x1b, w_a, w_b, conv_w, conv_b):
    t = x1b.shape[0]
    hb = TM // HALO

    def body(x_ref, xh_ref, wa_ref, wb_ref, cw_ref, cb_ref, apre_ref, act_ref, gate_ref, f_ref):
        i = pl.program_id(1)
        a_pre = _dot_nt(x_ref[...], wa_ref[...])
        a_halo = jnp.where(i > 0, _dot_nt(xh_ref[...], wa_ref[...]), 0.0)
        a = _conv_fwd(jnp.concatenate([a_halo, a_pre], axis=0), cw_ref, cb_ref, TM)
        b = _dot_nt(x_ref[...], wb_ref[...])
        cdf = 0.5 * (1.0 + lax.erf(a * (1.0 / math.sqrt(2.0))))
        pdf = jnp.exp(-0.5 * a * a) * (1.0 / math.sqrt(2.0 * math.pi))
        act = a * cdf
        apre_ref[...] = a_pre
        act_ref[...] = act
        gate_ref[...] = b * (cdf + a * pdf)
        f_ref[...] = (act * b).astype(MXU)

    blk = lambda r, c: pl.BlockSpec((None, r, c), lambda j, i: (j, 0, 0))
    tokj = pl.BlockSpec((None, TM, FF_BLK), lambda j, i: (j, i, 0))
    outs = [jax.ShapeDtypeStruct((N_SHARD, t, FF_BLK), F32)] * 3 + [jax.ShapeDtypeStruct((N_SHARD, t, FF_BLK), MXU)]
    return pl.pallas_call(
        body, name="ffn_in", grid=(N_SHARD, t // TM),
        in_specs=[pl.BlockSpec((TM, D_MODEL), lambda j, i: (i, 0)),
                  pl.BlockSpec((HALO, D_MODEL), lambda j, i: (jnp.maximum(i * hb - 1, 0), 0)),
                  blk(FF_BLK, D_MODEL), blk(FF_BLK, D_MODEL), blk(3, FF_BLK), blk(1, FF_BLK)],
        out_specs=[tokj, tokj, tokj, tokj], out_shape=outs,
        compiler_params=_cp(dimension_semantics=("arbitrary", "arbitrary")),
    )(x1b, x1b, w_a, w_b, conv_w, conv_b)


def _ffn_out_ln2(f, w_down, xhat1, ln1_g, ln1_b, ln2_g, ln2_b):
    t = xhat1.shape[0]

    def body(f_ref, wd_ref, xh_ref, g1_ref, b1_ref, g2_ref, b2_ref, xhat_ref, rstd_ref, x2b_ref):
        ff = jnp.dot(f_ref[0], wd_ref[0], preferred_element_type=F32)
        for j in range(1, N_SHARD):
            ff = ff + jnp.dot(f_ref[j], wd_ref[j], preferred_element_type=F32)
        x1 = xh_ref[...] * g1_ref[...] + b1_ref[...]
        xhat, rstd = _ln_fwd(ALPHA * x1 + ff)
        xhat_ref[...] = xhat
        rstd_ref[...] = rstd
        x2b_ref[...] = (xhat * g2_ref[...] + b2_ref[...]).astype(MXU)

    tok = lambda w: pl.BlockSpec((TM, w), lambda i: (i, 0))
    vec = _full((1, D_MODEL))
    outs = [jax.ShapeDtypeStruct((t, D_MODEL), F32), jax.ShapeDtypeStruct((t, 1), F32), jax.ShapeDtypeStruct((t, D_MODEL), MXU)]
    return pl.pallas_call(
        body, name="ffn_out_ln2", grid=(t // TM,),
        in_specs=[pl.BlockSpec((N_SHARD, TM, FF_BLK), lambda i: (0, i, 0)), _full(w_down.shape), tok(D_MODEL), vec, vec, vec, vec],
        out_specs=[tok(D_MODEL), tok(1), tok(D_MODEL)], out_shape=outs,
        compiler_params=_cp(dimension_semantics=("arbitrary",)),
    )(f, w_down, xhat1, ln1_g, ln1_b, ln2_g, ln2_b)


STAT_ROWS = 8


def _ple_loss_bwd(xhat2, rstd2, p, target, ln2_g, ln2_b, w_g, b_g, w_p, ln3_g, ln3_b):
    t = xhat2.shape[0]

    def body(xh2_ref, rs2_ref, p_ref, t_ref, g2_ref, b2_ref, wg_ref, bg_ref, wp_ref, g3_ref, b3_ref,
             dr2_ref, dgp_ref, dpp_ref, stat_ref, pp_scr):
        @pl.when(pl.program_id(0) == 0)
        def _():
            stat_ref[...] = jnp.zeros_like(stat_ref)

        xhat2 = xh2_ref[...]
        x2 = xhat2 * g2_ref[...] + b2_ref[...]
        gate = jax.nn.sigmoid(jnp.dot(x2.astype(MXU), wg_ref[...], preferred_element_type=F32) + bg_ref[...])
        pb = p_ref[...].astype(MXU)
        for j in range(N_SHARD):
            pp_scr[:, j * ROW_BLK:(j + 1) * ROW_BLK] = jnp.dot(pb, wp_ref[j], preferred_element_type=F32)
        pp = pp_scr[...]
        xhat3, rstd3 = _ln_fwd(ALPHA * x2 + gate * pp)
        err = xhat3 * g3_ref[...] + b3_ref[...] - t_ref[...]
        dy = err * (1.0 / D_MODEL)
        dr3 = _ln_bwd(dy, xhat3, rstd3, g3_ref[...])
        dgp = dr3 * pp * gate * (1.0 - gate)
        dgp_ref[...] = dgp.astype(MXU)
        dpp_ref[...] = (dr3 * gate).astype(MXU)
        dx2 = ALPHA * dr3 + _dot_nt(dgp, wg_ref[...])
        dr2_ref[...] = _ln_bwd(dx2, xhat2, rs2_ref[...], g2_ref[...])
        stat_ref[0:1, :] += _colsum(dy * xhat3)
        stat_ref[1:2, :] += _colsum(dy)
        stat_ref[2:3, :] += _colsum(dgp)
        stat_ref[3:4, :] += _colsum(dx2 * xhat2)
        stat_ref[4:5, :] += _colsum(dx2)
        stat_ref[5:6, :] += _colsum(err * err)

    tok = lambda w: pl.BlockSpec((TM, w), lambda i: (i, 0))
    vec = _full((1, D_MODEL))
    outs = [jax.ShapeDtypeStruct((t, D_MODEL), F32), jax.ShapeDtypeStruct((t, D_MODEL), MXU), jax.ShapeDtypeStruct((t, D_MODEL), MXU),
            jax.ShapeDtypeStruct((STAT_ROWS, D_MODEL), F32)]
    return pl.pallas_call(
        body, name="ple_loss_bwd", grid=(t // TM,),
        in_specs=[tok(D_MODEL), tok(1), tok(D_PLE), tok(D_MODEL), vec, vec, _full(w_g.shape), vec, _full(w_p.shape), vec, vec],
        out_specs=[tok(D_MODEL), tok(D_MODEL), tok(D_MODEL), _full((STAT_ROWS, D_MODEL))], out_shape=outs,
        scratch_shapes=[pltpu.VMEM((TM, D_MODEL), F32)],
        compiler_params=_cp(dimension_semantics=("arbitrary",)),
    )(xhat2, rstd2, p, target, ln2_g, ln2_b, w_g, b_g, w_p, ln3_g, ln3_b)


def _ffn_bwd(dr2, a_pre, act, gate, w_down, w_a, w_b, conv_w, xhat1, rstd1, ln1_g):
    t = dr2.shape[0]
    nt = t // TM
    hb = TM // HALO
    last_h = t // HALO - 1

    def body(dr_ref, drn_ref, ap_ref, act_ref, gate_ref, gaten_ref, wd_ref, wa_ref, wb_ref, cw_ref,
             xh_ref, rs_ref, g1_ref, dap_ref, dbb_ref, dr1_ref, cstat_ref, lstat_ref, acc_scr):
        i, j = pl.program_id(0), pl.program_id(1)

        @pl.when((i == 0) & (j == 0))
        def _():
            cstat_ref[...] = jnp.zeros_like(cstat_ref)
            lstat_ref[...] = jnp.zeros_like(lstat_ref)

        half = TM // ROW_GROUPS
        parts = []
        for r0 in range(0, TM, half):
            rows = pl.ds(r0, half)
            last = r0 + half == TM

            def ext(ref, nxt):
                return jnp.concatenate([ref[rows], nxt[...]], axis=0) if last else ref[r0:r0 + half + HALO]

            df = _dot_nt(ext(dr_ref, drn_ref), wd_ref[...])
            da = df * ext(gate_ref, gaten_ref)
            if last:
                da = jnp.concatenate([da[0:half], jnp.where(i < nt - 1, da[half:], 0.0)], axis=0)
            ahead = [da[s:s + half] for s in range(3)]
            da_pre = cw_ref[2:3, :] * ahead[0] + cw_ref[1:2, :] * ahead[1] + cw_ref[0:1, :] * ahead[2]
            dbb = df[0:half] * act_ref[rows, :]
            dap_ref[rows, :] = da_pre.astype(MXU)
            dbb_ref[rows, :] = dbb.astype(MXU)
            for kk in range(3):
                cstat_ref[j, kk:kk + 1, :] += _colsum(ahead[2 - kk] * ap_ref[rows, :])
            cstat_ref[j, 3:4, :] += _colsum(ahead[0])
            parts.append(_dot(da_pre, wa_ref[...]) + _dot(dbb, wb_ref[...]))
        part = jnp.concatenate(parts, axis=0)

        @pl.when(j == 0)
        def _():
            acc_scr[...] = ALPHA * dr_ref[...] + part

        @pl.when(j > 0)
        def _():
            acc_scr[...] += part

        @pl.when(j == N_SHARD - 1)
        def _():
            dx1 = acc_scr[...]
            xhat1 = xh_ref[...]
            lstat_ref[0:1, :] += _colsum(dx1 * xhat1)
            lstat_ref[1:2, :] += _colsum(dx1)
            dr1_ref[...] = _ln_bwd(dx1, xhat1, rs_ref[...], g1_ref[...])

    tok = lambda w: pl.BlockSpec((TM, w), lambda i, j: (i, 0))
    tokj = pl.BlockSpec((None, TM, FF_BLK), lambda i, j: (j, i, 0))
    nextj = pl.BlockSpec((None, HALO, FF_BLK), lambda i, j: (j, jnp.minimum((i + 1) * hb, last_h), 0))
    blk = lambda r, c: pl.BlockSpec((None, r, c), lambda i, j: (j, 0, 0))
    outs = [jax.ShapeDtypeStruct((N_SHARD, t, FF_BLK), MXU)] * 2 + [
        jax.ShapeDtypeStruct((t, D_MODEL), F32), jax.ShapeDtypeStruct((N_SHARD, STAT_ROWS, FF_BLK), F32),
        jax.ShapeDtypeStruct((STAT_ROWS, D_MODEL), F32)]
    return pl.pallas_call(
        body, name="ffn_bwd", grid=(nt, N_SHARD),
        in_specs=[tok(D_MODEL), pl.BlockSpec((HALO, D_MODEL), lambda i, j: (jnp.minimum((i + 1) * hb, last_h), 0)),
                  tokj, tokj, tokj, nextj, blk(FF_BLK, D_MODEL), blk(FF_BLK, D_MODEL), blk(FF_BLK, D_MODEL),
                  blk(3, FF_BLK), tok(D_MODEL), tok(1), _full((1, D_MODEL))],
        out_specs=[tokj, tokj, tok(D_MODEL), _full((N_SHARD, STAT_ROWS, FF_BLK)), _full((STAT_ROWS, D_MODEL))], out_shape=outs,
        scratch_shapes=[pltpu.VMEM((TM, D_MODEL), F32)],
        compiler_params=_cp(dimension_semantics=("arbitrary", "arbitrary")),
    )(dr2, dr2, a_pre, act, gate, gate, w_down, w_a, w_b, conv_w, xhat1, rstd1, ln1_g)


def _mix_bwd(dr1, w_o, hu, hz, mixed, attn, ln_z_g, ln_z_b, w_s, dep):
    t = dr1.shape[0]
    nchunk = TM // BLK

    def body(dr_ref, wo_ref, hu_ref, hz_ref, mx_ref, attn_ref, g_ref, b_ref, ws_ref, grp_ref, red_ref, dep_ref,
             do1_ref, do4_ref, do16_ref, dl1_ref, dl4_ref, dl16_ref, duz_ref, dws_ref, dbs_ref, zstat_ref,
             wm_scr, dzn_scr, dbsum_scr, do_scr, dl_scr):
        @pl.when(pl.program_id(0) == 0)
        def _():
            row = lax.broadcasted_iota(jnp.int32, (BLK, BLK), 0)
            col = lax.broadcasted_iota(jnp.int32, (BLK, BLK), 1)
            for g in range(N_HEADS):
                wm_scr[g] = jnp.where(col <= row, ws_ref[g], 0.0).astype(MXU)
            dws_ref[...] = jnp.zeros_like(dws_ref)
            dbsum_scr[...] = jnp.zeros_like(dbsum_scr)
            zstat_ref[...] = jnp.zeros_like(zstat_ref)

        dcat = _dot_nt(dr_ref[...], wo_ref[...])
        dattn = dcat[:, 0:D_ATTN]
        do1_ref[...] = dattn.astype(MXU)
        for cc, val in enumerate(_chunks(dattn)):
            do_scr[cc] = val
        _to_planes(do4_ref, do_scr, DILATIONS[1], LANE_CHUNKS, MXU)
        _to_planes(do16_ref, do_scr, DILATIONS[2], LANE_CHUNKS, MXU)
        delta = _dot_select(dattn * attn_ref[...], red_ref[...])
        dl1_ref[...] = delta
        dl_scr[0] = delta
        _to_planes(dl4_ref, dl_scr, DILATIONS[1], 1, F32)
        _to_planes(dl16_ref, dl_scr, DILATIONS[2], 1, F32)
        dgm = dcat[:, D_ATTN:]
        hu, hz = hu_ref[...], hz_ref[...]
        u = _gelu(hu)
        duz_ref[:, 0:D_GMLP] = (dgm * mx_ref[...] * _gelu_grad(hu)).astype(MXU)
        dmixed = dgm * u
        dmb = dmixed.astype(MXU)
        zhat, rstd = _ln_fwd(_gelu(hz))
        znb = (zhat * g_ref[...] + b_ref[...]).astype(MXU)
        dbs_acc = jnp.zeros((BLK, D_GMLP), F32)
        for ch in range(nchunk):
            rows = slice(ch * BLK, (ch + 1) * BLK)
            dbs_acc = dbs_acc + dmixed[rows]
            for g in range(N_HEADS):
                cols = slice(g * HEAD_DIM, (g + 1) * HEAD_DIM)
                dzn_scr[rows, cols] = _dot_tn(wm_scr[g], dmb[rows, cols])
                dws_ref[g] += _dot_nt(dmb[rows, cols], znb[rows, cols])
        dbsum_scr[...] += dbs_acc
        dzn = dzn_scr[...]
        zstat_ref[0:1, :] += _colsum(dzn * zhat)
        zstat_ref[1:2, :] += _colsum(dzn)
        duz_ref[:, D_GMLP:] = (_ln_bwd(dzn, zhat, rstd, g_ref[...]) * _gelu_grad(hz)).astype(MXU)

        @pl.when(pl.program_id(0) == nt - 1)
        def _():
            row = lax.broadcasted_iota(jnp.int32, (BLK, BLK), 0)
            col = lax.broadcasted_iota(jnp.int32, (BLK, BLK), 1)
            for g in range(N_HEADS):
                dws_ref[g] = jnp.where(col <= row, dws_ref[g], 0.0)
            dbs_ref[...] = lax.dot_general(grp_ref[...], dbsum_scr[...], (((1,), (1,)), ((), ())),
                                           precision=lax.Precision.HIGHEST, preferred_element_type=F32)

    nt = t // TM
    tok = lambda w: pl.BlockSpec((TM, w), lambda i: (i, 0))
    grp = jnp.asarray((np.arange(D_GMLP)[None, :] // HEAD_DIM == np.arange(N_HEADS)[:, None]).astype(np.float32))
    red = _head_reduce()
    outs = [_perm_shape(t, d, D_ATTN, MXU) for d in DILATIONS] + [_perm_shape(t, d, 128, F32) for d in DILATIONS] + [
        jax.ShapeDtypeStruct((t, 2 * D_GMLP), MXU),
        jax.ShapeDtypeStruct((N_HEADS, BLK, BLK), F32), jax.ShapeDtypeStruct((N_HEADS, BLK), F32),
        jax.ShapeDtypeStruct((STAT_ROWS, D_GMLP), F32)]
    return pl.pallas_call(
        body, name="mix_bwd", grid=(t // TM,),
        in_specs=[tok(D_MODEL), _full(w_o.shape), tok(D_GMLP), tok(D_GMLP), tok(D_GMLP), tok(D_ATTN), _full(ln_z_g.shape),
                  _full(ln_z_b.shape), _full(w_s.shape), _full(grp.shape), _full(red.shape), pl.BlockSpec(memory_space=pl.ANY)],
        out_specs=[_perm_tile(d, D_ATTN) for d in DILATIONS] + [_perm_tile(d, 128) for d in DILATIONS]
        + [tok(2 * D_GMLP), _full((N_HEADS, BLK, BLK)), _full((N_HEADS, BLK)), _full((STAT_ROWS, D_GMLP))],
        out_shape=outs,
        scratch_shapes=[pltpu.VMEM((N_HEADS, BLK, BLK), MXU), pltpu.VMEM((TM, D_GMLP), F32), pltpu.VMEM((BLK, D_GMLP), F32),
                        pltpu.VMEM((LANE_CHUNKS, TM, 128), F32), pltpu.VMEM((1, TM, 128), F32)],
        compiler_params=_cp(dimension_semantics=("arbitrary",)),
    )(dr1, w_o, hu, hz, mixed, attn, ln_z_g, ln_z_b, w_s, grp, red, dep)


def _dx_in(dqs, dks, dvs, duz, dr1, w_in, c_tab, s1_tab, s2_tab):
    t = dr1.shape[0]

    def body(dq1, dq4, dq16, dk1, dk4, dk16, dv1, dv4, dv16, duz_ref, dr_ref, w_ref, c_ref, s1_ref, s2_ref,
             dh_ref, dx_ref, acc_scr):
        sums = []
        for part, (g1, g4, g16) in enumerate(((dq1, dq4, dq16), (dk1, dk4, dk16), (dv1, dv4, dv16))):
            acc = acc_scr.at[pl.ds(part * LANE_CHUNKS, LANE_CHUNKS)]
            for cc in range(LANE_CHUNKS):
                acc[cc] = g1[:, cc * 128:(cc + 1) * 128]
            _from_planes(g4, acc, DILATIONS[1], LANE_CHUNKS, accumulate=True)
            _from_planes(g16, acc, DILATIONS[2], LANE_CHUNKS, accumulate=True)
            sums.append(_unchunk(acc_scr, LANE_CHUNKS, part * LANE_CHUNKS))
        c, s1, s2 = _tile_heads(c_ref[...]), _tile_heads(s1_ref[...]), _tile_heads(s2_ref[...])
        dh_ref[:, 0:D_ATTN] = _rope_apply_t(sums[0] * (1.0 / math.sqrt(HEAD_DIM)), c, s1, s2).astype(MXU)
        dh_ref[:, D_ATTN:2 * D_ATTN] = _rope_apply_t(sums[1], c, s1, s2).astype(MXU)
        dh_ref[:, 2 * D_ATTN:3 * D_ATTN] = sums[2].astype(MXU)
        dh_ref[:, 3 * D_ATTN:] = duz_ref[...]
        dx = ALPHA * dr_ref[...]
        for j in range(N_SHARD):
            dx = dx + _dot_nt(dh_ref[:, j * W_IN_BLK:(j + 1) * W_IN_BLK], w_ref[j])
        dx_ref[...] = dx

    tok = lambda w: pl.BlockSpec((TM, w), lambda i: (i, 0))
    outs = [jax.ShapeDtypeStruct((t, D_IN), MXU), jax.ShapeDtypeStruct((t, D_MODEL), F32)]
    return pl.pallas_call(
        body, name="dx_in", grid=(t // TM,),
        in_specs=[_perm_tile(d, D_ATTN) for d in DILATIONS] * 3
        + [tok(2 * D_GMLP), tok(D_MODEL), _full(w_in.shape), tok(128), tok(128), tok(128)],
        out_specs=[tok(D_IN), tok(D_MODEL)], out_shape=outs,
        scratch_shapes=[pltpu.VMEM((3 * LANE_CHUNKS, TM, 128), F32)],
        compiler_params=_cp(dimension_semantics=("arbitrary",)),
    )(*dqs, *dks, *dvs, duz, dr1, w_in, c_tab, s1_tab, s2_tab)


def _wgrad(name, x, dy, x_spec, dy_spec, out_spec, out_shape, grid, dep=None):
    deps = [] if dep is None else [dep]

    def body(x_ref, dy_ref, *rest):
        rest[-1][...] = _dot_tn(x_ref[...], dy_ref[...])

    return pl.pallas_call(
        body, name=name, grid=grid, in_specs=[x_spec, dy_spec] + [pl.BlockSpec(memory_space=pl.ANY)] * len(deps),
        out_specs=out_spec, out_shape=jax.ShapeDtypeStruct(out_shape, F32),
        compiler_params=_cp(dimension_semantics=("arbitrary",) * len(grid)),
    )(x, dy, *deps)


def _wgrad_pair(name, xa, xb, dy, x_spec, dy_spec, out_spec, out_shape, grid):
    def body(xa_ref, xb_ref, dy_ref, oa_ref, ob_ref):
        dy = dy_ref[...]
        oa_ref[...] = _dot_tn(xa_ref[...], dy)
        ob_ref[...] = _dot_tn(xb_ref[...], dy)

    return pl.pallas_call(
        body, name=name, grid=grid, in_specs=[x_spec, x_spec, dy_spec], out_specs=[out_spec, out_spec],
        out_shape=[jax.ShapeDtypeStruct(out_shape, F32)] * 2,
        compiler_params=_cp(dimension_semantics=("arbitrary",) * len(grid)),
    )(xa, xb, dy)


def _local_step(x, p, rope, target, w_in, start_dep, late_landed, late_weights, early_grads, early_grads_sent,
                early_grads_landed,
                ln_z_g, ln_z_b, w_s, b_s, ln1_g, ln1_b, conv_b, ln2_g, ln2_b, b_g, ln3_g, ln3_b):
    t = x.shape[0]
    half = TM
    c_tab, s1_tab, s2_tab = rope
    b_full = jnp.repeat(jnp.transpose(b_s[0]), HEAD_DIM, axis=1)
    conv_b4 = conv_b.reshape(N_SHARD, 1, FF_BLK)
    *qkvs, hu, hz, mixed, gm = _qkvuz(x, w_in, c_tab, s1_tab, s2_tab, ln_z_g, ln_z_b, w_s[0], b_full, start_dep)
    branches = [_attn_fwd(qkv, d, start_dep) for qkv, d in zip(qkvs[:2], DILATIONS[:2])]
    dep = late_landed(branches[-1][1])
    branches.append(_attn_fwd(qkvs[2], DILATIONS[2], dep))
    w_o, w_a, w_b, conv_w, w_down, w_g, w_p = late_weights(branches[-1][1])
    attn, *lses, cat, xhat1, rstd1, x1b = _mix_ln1(
        [o for o, _ in branches], [l for _, l in branches], gm, x, w_o, ln1_g, ln1_b, dep)
    a_pre, act, gate, f = _ffn_in(x1b, w_a, w_b, conv_w, conv_b4)
    xhat2, rstd2, x2b = _ffn_out_ln2(f, w_down, xhat1, ln1_g, ln1_b, ln2_g, ln2_b)
    dr2, dgp, dpp, stat3 = _ple_loss_bwd(xhat2, rstd2, p, target, ln2_g, ln2_b, w_g, b_g, w_p, ln3_g, ln3_b)
    da_pre, dbb, dr1, cstat, stat1 = _ffn_bwd(dr2, a_pre, act, gate, w_down, w_a, w_b, conv_w, xhat1, rstd1, ln1_g)

    full_t = lambda w, im: pl.BlockSpec((t, w), im)
    ffj = pl.BlockSpec((None, t, FF_BLK), lambda j, kk: (j, 0, 0))
    early = dict(
        w_ple_gate=_wgrad("dw_g", x2b, dgp, full_t(half, lambda kk, n: (0, kk)), full_t(half, lambda kk, n: (0, n)),
                          pl.BlockSpec((half, half), lambda kk, n: (kk, n)), (D_MODEL, D_MODEL), (2, 2)),
        w_ple_in=_wgrad("dw_p", p, dpp, full_t(D_PLE, lambda j: (0, 0)), full_t(ROW_BLK, lambda j: (0, j)),
                        pl.BlockSpec((None, D_PLE, ROW_BLK), lambda j: (j, 0, 0)), (N_SHARD, D_PLE, ROW_BLK), (N_SHARD,)),
        w_ff_down=_wgrad("dw_down", f, dr2, ffj, full_t(half, lambda j, n: (0, n)),
                         pl.BlockSpec((None, FF_BLK, half), lambda j, n: (j, 0, n)), (N_SHARD, FF_BLK, D_MODEL), (N_SHARD, 2)),
        **dict(zip(("w_ff_a", "w_ff_b"), _wgrad_pair(
            "dw_ab", da_pre, dbb, x1b, ffj, full_t(half, lambda j, n: (0, n)),
            pl.BlockSpec((None, FF_BLK, half), lambda j, n: (j, 0, n)), (N_SHARD, FF_BLK, D_MODEL), (N_SHARD, 2)))),
        w_o=_wgrad("dw_o", cat, dr1, full_t(half, lambda kk, n: (0, kk)), full_t(half, lambda kk, n: (0, n)),
                   pl.BlockSpec((half, half), lambda kk, n: (kk, n)), (D_MODEL, D_MODEL), (2, 2)))
    dep = early_grads(early)

    do1, do4, do16, dl1, dl4, dl16, duz, dws, dbs, zstat = _mix_bwd(
        dr1, w_o, hu, hz, mixed, attn, ln_z_g, ln_z_b, w_s[0], dep)
    dep = early_grads_sent(duz, (stat3, stat1, zstat, cstat, dws, dbs))
    dqkv = [_attn_bwd(qkv, do, lse, dl, d, dep)
            for qkv, do, lse, dl, d in zip(qkvs, (do1, do4, do16), lses, (dl1, dl4, dl16), DILATIONS)]
    dh, grad_x = _dx_in([g[0] for g in dqkv], [g[1] for g in dqkv], [g[2] for g in dqkv], duz, dr1, w_in,
                        c_tab, s1_tab, s2_tab)
    dep = early_grads_landed(grad_x)
    g_w_in = _wgrad("dw_in", x, dh, full_t(half, lambda j, kk: (0, kk)), full_t(W_IN_BLK, lambda j, kk: (0, j)),
                    pl.BlockSpec((None, half, W_IN_BLK), lambda j, kk: (j, kk, 0)), (N_SHARD, D_MODEL, W_IN_BLK), (N_SHARD, 2),
                    dep)
    return grad_x, g_w_in


def _tile_rows(rows, mult, steps):
    if rows % mult:
        return rows
    return next(rows // k for k in range(steps, rows + 1) if rows % k == 0 and (rows // k) % mult == 0)


def _grid_spec(grid, in_specs, out_specs):
    return pltpu.PrefetchScalarGridSpec(num_scalar_prefetch=1, grid=grid, in_specs=in_specs, out_specs=out_specs)


def _on_own_steps(i, count, steps, work):
    if count == steps:
        work()
    else:
        pl.when(i < count)(work)


def _place_shards(name, ws, dtypes, place, dep):
    n = len(ws)
    tiles = [_tile_rows(w.shape[0], 16, 8) for w in ws]
    counts = [w.shape[0] // t for w, t in zip(ws, tiles)]
    steps = max(counts)

    def body(s_ref, *refs):
        i = pl.program_id(0)
        for a in range(n):
            def work(a=a):
                refs[n + 1 + a][...] = refs[a][...].astype(dtypes[a])
            _on_own_steps(i, counts[a], steps, work)

    def tile(a, lead):
        last = counts[a] - 1
        if lead:
            return pl.BlockSpec((None, tiles[a], ws[a].shape[1]), lambda i, s: (s[0], jnp.minimum(i, last), 0))
        return pl.BlockSpec((tiles[a], ws[a].shape[1]), lambda i, s: (jnp.minimum(i, last), 0))

    return pl.pallas_call(
        body, name=name,
        grid_spec=_grid_spec((steps,), [tile(a, False) for a in range(n)] + [pl.BlockSpec(memory_space=pl.ANY)],
                             [tile(a, True) for a in range(n)]),
        out_shape=[jax.ShapeDtypeStruct((N_SHARD, *w.shape), dt) for w, dt in zip(ws, dtypes)],
        compiler_params=_cp())(place, *ws, dep)


def _pair_sums(name, mines, gots, place):
    n = len(mines)
    tiles = [_tile_rows(g.shape[1], 16, 2) for g in gots]
    per_blk = [g.shape[1] // t for g, t in zip(gots, tiles)]
    counts = [N_SHARD * nh for nh in per_blk]
    steps = max(counts)

    def body(s_ref, *refs):
        i = pl.program_id(0)
        for a in range(n):
            def work(a=a):
                refs[2 * n + a][...] = (refs[a][...] + refs[n + a][...]).astype(BF16)
            _on_own_steps(i, counts[a], steps, work)

    def tile(a, mine):
        nh, last = per_blk[a], counts[a] - 1

        def index(i, s):
            g = jnp.minimum(i, last)
            return (g // nh, (s[1] * nh if mine else 0) + g % nh, 0)

        return pl.BlockSpec((None, tiles[a], gots[a].shape[2]), index)

    return pl.pallas_call(
        body, name=name,
        grid_spec=_grid_spec((steps,), [tile(a, True) for a in range(n)] + [tile(a, False) for a in range(n)],
                             [tile(a, False) for a in range(n)]),
        out_shape=[jax.ShapeDtypeStruct(g.shape, BF16) for g in gots], compiler_params=_cp())(place, *mines, *gots)


def _chip_sums(name, owns, landeds, place, dep):
    n = len(owns)
    tiles = [_tile_rows(o.shape[1], 16, 8) for o in owns]
    counts = [o.shape[1] // t for o, t in zip(owns, tiles)]
    steps = max(counts)

    def body(s_ref, *refs):
        i = pl.program_id(0)
        for a in range(n):
            def work(a=a):
                own, l1, l2, l3 = (refs[4 * a + k][...].astype(F32) for k in range(4))
                refs[4 * n + 1 + a][...] = ((own + l1) + l2) + l3
            _on_own_steps(i, counts[a], steps, work)

    def slot(a, d):
        last = counts[a] - 1
        return pl.BlockSpec((None, tiles[a], owns[a].shape[2]), lambda i, s: ((s[0] + d) % N_SHARD, jnp.minimum(i, last), 0))

    def out(a):
        nh, last = counts[a], counts[a] - 1
        return pl.BlockSpec((tiles[a], owns[a].shape[2]), lambda i, s: (s[1] * nh + jnp.minimum(i, last), 0))

    operands = [x for o, l in zip(owns, landeds) for x in (o, l, l, l)]
    return pl.pallas_call(
        body, name=name,
        grid_spec=_grid_spec((steps,), [slot(a, d) for a in range(n) for d in range(4)] + [pl.BlockSpec(memory_space=pl.ANY)],
                             [out(a) for a in range(n)]),
        out_shape=[jax.ShapeDtypeStruct((2 * o.shape[1], o.shape[2]), F32) for o in owns],
        compiler_params=_cp())(place, *operands, dep)


def _adamw_math(w, g, m, v):
    m = ADAM_B1 * m + (1.0 - ADAM_B1) * g
    v = ADAM_B2 * v + (1.0 - ADAM_B2) * (g * g)
    m_hat = m / (1.0 - ADAM_B1 ** ADAM_STEP)
    v_hat = v / (1.0 - ADAM_B2 ** ADAM_STEP)
    delta = -ADAM_LR * (m_hat / (jnp.sqrt(v_hat) + ADAM_EPS) + ADAM_WD * w)
    return delta, m, v


def _adamw_shards(name, ws, gs, ms, vs):
    n = len(ws)
    tiles = [_tile_rows(w.shape[1], 8, 8) for w in ws]
    counts = [w.shape[1] // t for w, t in zip(ws, tiles)]
    steps = max(counts)

    def body(*refs):
        i = pl.program_id(0)
        for a in range(n):
            def work(a=a):
                w_ref, g_ref, m_ref, v_ref = refs[4 * a:4 * a + 4]
                d_ref, nm_ref, nv_ref = refs[4 * n + 3 * a:4 * n + 3 * a + 3]
                d_ref[...], nm_ref[...], nv_ref[...] = _adamw_math(w_ref[...], g_ref[...], m_ref[...], v_ref[...])
            _on_own_steps(i, counts[a], steps, work)

    def tile(a, lead):
        last, c = counts[a] - 1, ws[a].shape[2]
        if lead:
            return pl.BlockSpec((None, tiles[a], c), lambda i: (0, jnp.minimum(i, last), 0))
        return pl.BlockSpec((tiles[a], c), lambda i: (jnp.minimum(i, last), 0))

    res = pl.pallas_call(
        body, name=name, grid=(steps,),
        in_specs=[tile(a, lead) for a in range(n) for lead in (True, False, True, True)],
        out_specs=[tile(a, True) for a in range(n) for _ in range(3)],
        out_shape=[jax.ShapeDtypeStruct(w.shape, F32) for w in ws for _ in range(3)],
        compiler_params=_cp())(*[x for quad in zip(ws, gs, ms, vs) for x in quad])
    return [tuple(res[3 * a:3 * a + 3]) for a in range(n)]


MESH = pl.DeviceIdType.MESH
ANY = pl.BlockSpec(memory_space=pl.ANY)


def _place():
    x, y, c = lax.axis_index("x"), lax.axis_index("y"), lax.axis_index("c")
    chips = [(1 - x, y), (x, 1 - y), (1 - x, 1 - y)]
    return x, y, c, 2 * x + y, chips


def _remote(src, dst, send_sem, recv_sem, dev):
    return pltpu.make_async_remote_copy(src_ref=src, dst_ref=dst, send_sem=send_sem, recv_sem=recv_sem,
                                        device_id=dev, device_id_type=MESH)


def _half(ref, hc, rows):
    return ref.at[pl.ds(hc * (rows // 2), rows // 2)]


def _sibling_join(blocks, tag):
    n = len(blocks)

    def body(*refs):
        outs = refs[n:2 * n]
        send, recv = refs[2 * n:]
        x, y, c, _, _ = _place()
        cps = []
        for a in range(n):
            h = blocks[a].shape[0] // 2
            mine = outs[a].at[pl.ds(c * h, h)]
            cp = _remote(mine, mine, send.at[a], recv.at[a], (x, y, 1 - c))
            cp.start()
            cps.append(cp)
        for a, cp in enumerate(cps):
            h = blocks[a].shape[0] // 2
            theirs = outs[a].at[pl.ds((1 - c) * h, h)]
            _remote(theirs, theirs, send.at[a], recv.at[a], (x, y, 1 - c)).wait_recv()
            cp.wait_send()

    sem = pltpu.SemaphoreType.DMA
    return pl.pallas_call(body, name=f"rs_sibling_join_{tag}", in_specs=[ANY] * n, out_specs=[ANY] * n,
                          out_shape=[jax.ShapeDtypeStruct(b_.shape, b_.dtype) for b_ in blocks],
                          input_output_aliases={a: a for a in range(n)},
                          scratch_shapes=[sem((n,)), sem((n,))])(*blocks)


def _join_start(blocks, after, tag):
    n = len(blocks)

    def body(*refs):
        ins = refs[:n]
        send, recv = refs[n + 1], refs[n + 2]
        token = refs[2 * n + 3]
        x, y, c, _, _ = _place()
        for a in range(n):
            h = blocks[a].shape[0] // 2
            mine = ins[a].at[pl.ds(c * h, h)]
            _remote(mine, mine, send.at[a], recv.at[a], (x, y, 1 - c)).start()
        token[...] = jnp.zeros_like(token)

    sems = pltpu.SemaphoreType.DMA((n,))
    res = pl.pallas_call(
        body, name=f"join_start_{tag}", in_specs=[HBM] * n + [ANY],
        out_specs=[SEM, SEM] + [HBM] * n + [pl.BlockSpec(memory_space=pltpu.VMEM)],
        out_shape=[sems, sems] + [pltpu.HBM(b_.shape, b_.dtype) for b_ in blocks] + [TOKEN],
        input_output_aliases={a: a + 2 for a in range(n)}, compiler_params=_in_flight_params(),
    )(*[_in_hbm(b_) for b_ in blocks], after)
    return res[0], res[1], res[2:2 + n], res[2 + n]


def _join_wait(send, recv, blocks, after, tag):
    n = len(blocks)

    def body(*refs):
        ins = refs[:n]
        send_ref, recv_ref = refs[n], refs[n + 1]
        x, y, c, _, _ = _place()
        for a in range(n):
            h = blocks[a].shape[0] // 2
            mine, theirs = ins[a].at[pl.ds(c * h, h)], ins[a].at[pl.ds((1 - c) * h, h)]
            _remote(mine, mine, send_ref.at[a], recv_ref.at[a], (x, y, 1 - c)).wait_send()
            _remote(theirs, theirs, send_ref.at[a], recv_ref.at[a], (x, y, 1 - c)).wait_recv()

    return pl.pallas_call(
        body, name=f"join_wait_{tag}", in_specs=[HBM] * n + [SEM, SEM, ANY], out_specs=[HBM] * n,
        out_shape=[pltpu.HBM(b_.shape, b_.dtype) for b_ in blocks],
        input_output_aliases={a: a for a in range(n)}, compiler_params=_in_flight_params(),
    )(*blocks, send, recv, after)


HBM = pl.BlockSpec(memory_space=pltpu.HBM)
SEM = pl.BlockSpec(memory_space=pltpu.SEMAPHORE)
TOKEN = jax.ShapeDtypeStruct((8, 128), F32)


def _in_flight_params():
    return pltpu.CompilerParams(has_side_effects=pltpu.SideEffectType.DATAFLOW_SIDE_EFFECTING)


def _in_hbm(a):
    return pltpu.with_memory_space_constraint(a, pltpu.HBM)


def _gather_piece(ref, rows, split, slot, hc):
    return _half(ref.at[slot], hc, rows) if split else ref.at[slot]


def _gather_start(stacks, split, after, tag):
    n = len(stacks)

    def body(*refs):
        ins = refs[:n]
        send, recv = refs[n + 1], refs[n + 2]
        token = refs[2 * n + 3]
        _, _, c, j, chips = _place()
        for a in range(n):
            mine = _gather_piece(ins[a], stacks[a].shape[1], split[a], j, c)
            for t in range(3):
                _remote(mine, mine, send.at[3 * a + t], recv.at[3 * a + t], (*chips[t], c)).start()
        token[...] = jnp.zeros_like(token)

    sems = pltpu.SemaphoreType.DMA((3 * n,))
    res = pl.pallas_call(
        body, name=f"gather_start_{tag}", in_specs=[HBM] * n + [ANY],
        out_specs=[SEM, SEM] + [HBM] * n + [pl.BlockSpec(memory_space=pltpu.VMEM)],
        out_shape=[sems, sems] + [pltpu.HBM(s.shape, s.dtype) for s in stacks] + [TOKEN],
        input_output_aliases={a: a + 2 for a in range(n)}, compiler_params=_in_flight_params(),
    )(*[_in_hbm(s) for s in stacks], after)
    return res[0], res[1], res[2:2 + n], res[2 + n]


def _gather_wait(send, recv, stacks, split, after, tag):
    n = len(stacks)

    def body(*refs):
        ins = refs[:n]
        send_ref, recv_ref = refs[n], refs[n + 1]
        _, _, c, j, chips = _place()
        for a in range(n):
            rows = stacks[a].shape[1]
            mine = _gather_piece(ins[a], rows, split[a], j, c)
            for t, (px, py) in enumerate(chips):
                theirs = _gather_piece(ins[a], rows, split[a], 2 * px + py, c)
                _remote(mine, mine, send_ref.at[3 * a + t], recv_ref.at[3 * a + t], (px, py, c)).wait_send()
                _remote(theirs, theirs, send_ref.at[3 * a + t], recv_ref.at[3 * a + t], (px, py, c)).wait_recv()

    return pl.pallas_call(
        body, name=f"gather_wait_{tag}", in_specs=[HBM] * n + [SEM, SEM, ANY], out_specs=[HBM] * n,
        out_shape=[pltpu.HBM(s.shape, s.dtype) for s in stacks],
        input_output_aliases={a: a for a in range(n)}, compiler_params=_in_flight_params(),
    )(*stacks, send, recv, after)


def _gather_forward(stacks, split, tag):
    idx = [a for a in range(len(stacks)) if split[a]]
    n = len(idx)

    def body(*refs):
        outs = refs[n:2 * n]
        send, recv = refs[2 * n:]
        x, y, c, _, chips = _place()
        sends = []
        for t, (px, py) in enumerate(chips):
            for a in range(n):
                blk = _half(outs[a].at[2 * px + py], c, stacks[idx[a]].shape[1])
                cp = _remote(blk, blk, send.at[a, t], recv.at[a, t], (x, y, 1 - c))
                cp.start()
                sends.append(cp)
        for t, (px, py) in enumerate(chips):
            for a in range(n):
                blk = _half(outs[a].at[2 * px + py], 1 - c, stacks[idx[a]].shape[1])
                _remote(blk, blk, send.at[a, t], recv.at[a, t], (x, y, 1 - c)).wait_recv()
        for cp in sends:
            cp.wait_send()

    sem = pltpu.SemaphoreType.DMA
    res = pl.pallas_call(
        body, name=f"gather_forward_{tag}", in_specs=[ANY] * n, out_specs=[ANY] * n,
        out_shape=[jax.ShapeDtypeStruct(stacks[a].shape, stacks[a].dtype) for a in idx],
        input_output_aliases={a: a for a in range(n)}, scratch_shapes=[sem((n, 3)), sem((n, 3))],
    )(*[stacks[a] for a in idx])
    out = list(stacks)
    for a, r in zip(idx, res):
        out[a] = r
    return out


def _forward_start(stacks, after, tag):
    n = len(stacks)

    def body(*refs):
        ins = refs[:n]
        send, recv = refs[n + 1], refs[n + 2]
        token = refs[2 * n + 3]
        x, y, c, _, chips = _place()
        for a in range(n):
            for t, (px, py) in enumerate(chips):
                blk = _half(ins[a].at[2 * px + py], c, stacks[a].shape[1])
                _remote(blk, blk, send.at[3 * a + t], recv.at[3 * a + t], (x, y, 1 - c)).start()
        token[...] = jnp.zeros_like(token)

    sems = pltpu.SemaphoreType.DMA((3 * n,))
    res = pl.pallas_call(
        body, name=f"forward_start_{tag}", in_specs=[HBM] * n + [ANY],
        out_specs=[SEM, SEM] + [HBM] * n + [pl.BlockSpec(memory_space=pltpu.VMEM)],
        out_shape=[sems, sems] + [pltpu.HBM(s.shape, s.dtype) for s in stacks] + [TOKEN],
        input_output_aliases={a: a + 2 for a in range(n)}, compiler_params=_in_flight_params(),
    )(*[_in_hbm(s) for s in stacks], after)
    return res[0], res[1], res[2:2 + n], res[2 + n]


def _forward_wait(send, recv, stacks, after, tag):
    n = len(stacks)

    def body(*refs):
        ins = refs[:n]
        send_ref, recv_ref = refs[n], refs[n + 1]
        x, y, c, _, chips = _place()
        for a in range(n):
            for t, (px, py) in enumerate(chips):
                mine = _half(ins[a].at[2 * px + py], c, stacks[a].shape[1])
                theirs = _half(ins[a].at[2 * px + py], 1 - c, stacks[a].shape[1])
                _remote(mine, mine, send_ref.at[3 * a + t], recv_ref.at[3 * a + t], (x, y, 1 - c)).wait_send()
                _remote(theirs, theirs, send_ref.at[3 * a + t], recv_ref.at[3 * a + t], (x, y, 1 - c)).wait_recv()

    return pl.pallas_call(
        body, name=f"forward_wait_{tag}", in_specs=[HBM] * n + [SEM, SEM, ANY], out_specs=[HBM] * n,
        out_shape=[pltpu.HBM(s.shape, s.dtype) for s in stacks],
        input_output_aliases={a: a for a in range(n)}, compiler_params=_in_flight_params(),
    )(*stacks, send, recv, after)


def _swap_start(grads, tag):
    n = len(grads)

    def body(*refs):
        ins, gots = refs[:n], refs[n:2 * n]
        send, recv = refs[2 * n], refs[2 * n + 1]
        token = refs[4 * n + 2]
        x, y, c, _, _ = _place()
        for a in range(n):
            h = grads[a].shape[1] // 2
            _remote(ins[a].at[:, pl.ds((1 - c) * h, h)], gots[a], send.at[a], recv.at[a], (x, y, 1 - c)).start()
        token[...] = jnp.zeros_like(token)

    sems = pltpu.SemaphoreType.DMA((n,))
    halves = [(g.shape[0], g.shape[1] // 2, g.shape[2]) for g in grads]
    res = pl.pallas_call(
        body, name=f"swap_start_{tag}", in_specs=[HBM] * (2 * n),
        out_specs=[SEM, SEM] + [HBM] * (2 * n) + [pl.BlockSpec(memory_space=pltpu.VMEM)],
        out_shape=[sems, sems] + [pltpu.HBM(g.shape, g.dtype) for g in grads] + [pltpu.HBM(s, F32) for s in halves] + [TOKEN],
        input_output_aliases={a: a + 2 for a in range(2 * n)}, compiler_params=_in_flight_params(),
    )(*[_in_hbm(g) for g in grads], *[_in_hbm(lax.empty(s, F32)) for s in halves])
    return res[0], res[1], res[2:2 + n], res[2 + n:2 + 2 * n], res[2 + 2 * n]


def _swap_wait(send, recv, grads, gots, after, tag):
    n = len(grads)

    def body(*refs):
        ins, lnd = refs[:n], refs[n:2 * n]
        send_ref, recv_ref = refs[2 * n], refs[2 * n + 1]
        x, y, c, _, _ = _place()
        for a in range(n):
            h = grads[a].shape[1] // 2
            cp = _remote(ins[a].at[:, pl.ds((1 - c) * h, h)], lnd[a], send_ref.at[a], recv_ref.at[a], (x, y, 1 - c))
            cp.wait_send()
            cp.wait_recv()

    bufs = [pltpu.HBM(g.shape, g.dtype) for g in grads] + [pltpu.HBM(g.shape, g.dtype) for g in gots]
    res = pl.pallas_call(
        body, name=f"swap_wait_{tag}", in_specs=[HBM] * (2 * n) + [SEM, SEM, ANY], out_specs=[HBM] * (2 * n),
        out_shape=bufs, input_output_aliases={a: a for a in range(2 * n)}, compiler_params=_in_flight_params(),
    )(*grads, *gots, send, recv, after)
    return res[:n], res[n:]


def _exchange_start(parts, tag):
    n = len(parts)

    def body(*refs):
        ins, lands = refs[:n], refs[n:2 * n]
        send, recv = refs[2 * n], refs[2 * n + 1]
        token = refs[4 * n + 2]
        _, _, c, j, chips = _place()
        for t, (px, py) in enumerate(chips):
            for a in range(n):
                _remote(ins[a].at[2 * px + py], lands[a].at[j], send.at[3 * a + t], recv.at[3 * a + t], (px, py, c)).start()
        token[...] = jnp.zeros_like(token)

    sems = pltpu.SemaphoreType.DMA((3 * n,))
    bufs = [pltpu.HBM(p.shape, p.dtype) for p in parts]
    res = pl.pallas_call(
        body, name=f"exchange_start_{tag}", in_specs=[HBM] * (2 * n),
        out_specs=[SEM, SEM] + [HBM] * (2 * n) + [pl.BlockSpec(memory_space=pltpu.VMEM)],
        out_shape=[sems, sems] + bufs + bufs + [TOKEN],
        input_output_aliases={a: a + 2 for a in range(2 * n)}, compiler_params=_in_flight_params(),
    )(*[_in_hbm(p) for p in parts], *[_in_hbm(lax.empty(p.shape, p.dtype)) for p in parts])
    return res[0], res[1], res[2:2 + n], res[2 + n:2 + 2 * n], res[2 + 2 * n]


def _exchange_wait(send, recv, parts, lands, after, tag):
    n = len(parts)

    def body(*refs):
        ins, lnd = refs[:n], refs[n:2 * n]
        send_ref, recv_ref = refs[2 * n], refs[2 * n + 1]
        _, _, c, j, chips = _place()
        for t, (px, py) in enumerate(chips):
            jt = 2 * px + py
            for a in range(n):
                _remote(ins[a].at[jt], lnd[a].at[j], send_ref.at[3 * a + t], recv_ref.at[3 * a + t], (px, py, c)).wait_send()
                _remote(ins[a].at[jt], lnd[a].at[jt], send_ref.at[3 * a + t], recv_ref.at[3 * a + t], (px, py, c)).wait_recv()

    bufs = [pltpu.HBM(p.shape, p.dtype) for p in parts]
    res = pl.pallas_call(
        body, name=f"exchange_wait_{tag}", in_specs=[HBM] * (2 * n) + [SEM, SEM, ANY], out_specs=[HBM] * (2 * n),
        out_shape=bufs + bufs, input_output_aliases={a: a for a in range(2 * n)}, compiler_params=_in_flight_params(),
    )(*parts, *lands, send, recv, after)
    return res[:n], res[n:]


def _small_chip_sums(arrs):
    n = len(arrs)

    def body(*refs):
        ins, outs = refs[:n], refs[n:2 * n]
        sib = refs[2 * n:3 * n]
        send, recv = refs[3 * n:]
        x, y, c, j, _ = _place()
        swaps = [_remote(ins[a], sib[a], send.at[a], recv.at[a], (x, y, 1 - c)) for a in range(n)]
        for cp in swaps:
            cp.start()
        for a in range(n):
            swaps[a].wait_recv()
            outs[a][j] = ins[a][...] + sib[a][...]
        for cp in swaps:
            cp.wait_send()

    sem = pltpu.SemaphoreType.DMA
    vm = pl.BlockSpec(memory_space=pltpu.VMEM)
    return pl.pallas_call(
        body, name="small_chip_sums", in_specs=[vm] * n, out_specs=[vm] * n,
        out_shape=[jax.ShapeDtypeStruct((N_SHARD, *a.shape), F32) for a in arrs],
        scratch_shapes=[pltpu.VMEM(a.shape, F32) for a in arrs] + [sem((n,)), sem((n,))],
        compiler_params=_cp(),
    )(*arrs)


def _small_totals(stacks):
    n = len(stacks)

    def body(*refs):
        for a in range(n):
            refs[n + a][...] = ((refs[a][0] + refs[a][1]) + refs[a][2]) + refs[a][3]

    return pl.pallas_call(body, name="small_totals", out_shape=[jax.ShapeDtypeStruct(s.shape[1:], F32) for s in stacks],
                          compiler_params=_cp())(*stacks)


SMALL_1024 = ("ln1_g", "ln1_b", "ln2_g", "ln2_b", "b_ple_gate", "ln3_g", "ln3_b")


def _adamw_small(red3, red1, redz, g_conv_w, redc, red_ws, red_bs, params):
    shape2d = {"ln_z_g": (1, D_GMLP), "ln_z_b": (1, D_GMLP), "w_s": (N_HEADS * BLK, BLK), "b_s": (N_HEADS, BLK),
               "conv_w": (3, FF_BLK), "conv_b": (N_SHARD, FF_BLK), **{k: (1, D_MODEL) for k in SMALL_1024}}
    names = list(shape2d)
    flat = [a.reshape(shape2d[k]) for k in names for a in params[k]]

    def body(r3, r1, rz, gcw, rc, rws, rbs, *refs):
        ins, outs = refs[:3 * len(names)], refs[3 * len(names):]

        def grad_of(k):
            if k == "w_s":
                return rws[...]
            if k == "b_s":
                return rbs[...]
            if k == "conv_w":
                return gcw[0:3, :]
            if k == "conv_b":
                return jnp.concatenate([rc[j * STAT_ROWS + 3:j * STAT_ROWS + 4, :] for j in range(N_SHARD)], axis=0)
            src, row = {"ln3_g": (r3, 0), "ln3_b": (r3, 1), "b_ple_gate": (r3, 2), "ln2_g": (r3, 3), "ln2_b": (r3, 4),
                        "ln1_g": (r1, 0), "ln1_b": (r1, 1), "ln_z_g": (rz, 0), "ln_z_b": (rz, 1)}[k]
            return src[row:row + 1, :]

        for i, k in enumerate(names):
            w_ref, m_ref, v_ref = ins[3 * i:3 * i + 3]
            g_ref, d_ref, nm_ref, nv_ref = outs[4 * i:4 * i + 4]
            g = grad_of(k)
            g_ref[...] = g
            d_ref[...], nm_ref[...], nv_ref[...] = _adamw_math(w_ref[...], g, m_ref[...], v_ref[...])

    res = pl.pallas_call(
        body, name="adamw_small",
        out_shape=[jax.ShapeDtypeStruct(shape2d[k], F32) for k in names for _ in range(4)],
        compiler_params=_cp(),
    )(red3, red1, redz, g_conv_w, redc, red_ws, red_bs, *flat)
    return {k: tuple(r.reshape(params[k][0].shape) for r in res[4 * i:4 * i + 4]) for i, k in enumerate(names)}


WEIGHTS = ("w_in", "ln_z_g", "ln_z_b", "w_s", "b_s", "w_o", "ln1_g", "ln1_b", "w_ff_a", "w_ff_b", "conv_w", "conv_b",
           "w_ff_down", "ln2_g", "ln2_b", "w_ple_gate", "b_ple_gate", "w_ple_in", "ln3_g", "ln3_b")
BIG = ("w_in", "w_o", "w_ff_a", "w_ff_b", "w_ff_down", "w_ple_gate", "w_ple_in")
TRANSPOSED = ("w_ff_a", "w_ff_b")
LATE = ("w_o", "w_ff_a", "w_ff_b", "w_ff_down", "w_ple_gate", "w_ple_in", "conv_w")


def kernel(x, p, positions, w_in, ln_z_g, ln_z_b, w_s, b_s, w_o, ln1_g, ln1_b, w_ff_a, w_ff_b, conv_w, conv_b, w_ff_down, ln2_g, ln2_b, w_ple_gate, b_ple_gate, w_ple_in, ln3_g, ln3_b, loss_target, m_w_in, m_ln_z_g, m_ln_z_b, m_w_s, m_b_s, m_w_o, m_ln1_g, m_ln1_b, m_w_ff_a, m_w_ff_b, m_conv_w, m_conv_b, m_w_ff_down, m_ln2_g, m_ln2_b, m_w_ple_gate, m_b_ple_gate, m_w_ple_in, m_ln3_g, m_ln3_b, v_w_in, v_ln_z_g, v_ln_z_b, v_w_s, v_b_s, v_w_o, v_ln1_g, v_ln1_b, v_w_ff_a, v_w_ff_b, v_conv_w, v_conv_b, v_w_ff_down, v_ln2_g, v_ln2_b, v_w_ple_gate, v_b_ple_gate, v_w_ple_in, v_ln3_g, v_ln3_b):
    args = locals()
    w = {k: args[k] for k in WEIGHTS}
    m = {k: args["m_" + k] for k in WEIGHTS}
    v = {k: args["v_" + k] for k in WEIGHTS}

    for k in TRANSPOSED:
        w[k], m[k], v[k] = (jnp.swapaxes(a, 1, 2) for a in (w[k], m[k], v[k]))

    chip = 2 * lax.axis_index("x") + lax.axis_index("y")
    place = jnp.stack([chip, lax.axis_index("c")]).astype(jnp.int32)
    stack = dict(zip(["w_in"], _place_shards("cast_w_in", [w["w_in"][0]], [MXU], place, place)))
    i_send, i_recv, in_flight, dep = _gather_start([stack["w_in"]], [True], place, "w_in")
    stack.update(zip(LATE, _place_shards("cast_late", [w[k][0] for k in LATE],
                                         [F32 if k == "conv_w" else MXU for k in LATE], place, dep)))
    split_late = [k != "conv_w" for k in LATE]
    g_send, g_recv, late_flight, start_dep = _gather_start([stack[k] for k in LATE], split_late, place, "late")
    rope = _rope_tables(positions, x.shape[1], start_dep)
    landed_in = _gather_wait(i_send, i_recv, in_flight, [True], rope[0], "w_in")
    w_in_full, = _gather_forward(landed_in, [True], "w_in")
    halves =[k for k, sp in zip(LATE, split_late) if sp]
    trips = {}

    def late_landed(after):
        fw = dict(zip(LATE, _gather_wait(g_send, g_recv, late_flight, split_late, after, "late")))
        trips["late"] = (fw, *_forward_start([fw[k] for k in halves], fw["conv_w"], "late"))
        return trips["late"][-1]

    def late_weights(after):
        fw, send, recv, flight, _ = trips["late"]
        fw.update(zip(halves, _forward_wait(send, recv, flight, after, "late")))
        return (fw["w_o"].reshape(D_MODEL, D_MODEL), fw["w_ff_a"], fw["w_ff_b"], fw["conv_w"], fw["w_ff_down"],
                fw["w_ple_gate"].reshape(D_MODEL, D_MODEL), fw["w_ple_in"])

    def swap_started(names, grads, tag):
        stacked = [g.reshape(N_SHARD, *w[k].shape[1:]) for k, g in zip(names, grads)]
        return (names, tag, *_swap_start(stacked, tag))

    def partial_sums(swap, after):
        names, tag, send, recv, stacked, gots, _ = swap
        stacked, got = _swap_wait(send, recv, stacked, gots, after, tag)
        pair = _pair_sums(f"rs_pair_{tag}", stacked, got, place)
        return (names, tag, *_exchange_start(pair, tag))

    def chip_summed(trip, after, dep):
        names, tag, send, recv, pair, lands, _ = trip
        pair, landed = _exchange_wait(send, recv, pair, lands, after, tag)
        return _chip_sums(f"rs_sum_{tag}", pair, landed, place, dep), names, tag

    def reduced(trip, after, dep):
        blocks, names, tag = chip_summed(trip, after, dep)
        return dict(zip(names, _sibling_join(blocks, tag)))

    def early_grads_landed(after):
        blocks, names, tag = chip_summed(trips["early"], after, trips["small"][-1])
        trips["join"] = (names, *_join_start(blocks, after, tag))
        return trips["join"][-1]

    def early_grads(grads):
        trips["swap"] = swap_started(list(grads), list(grads.values()), "early")
        return trips["swap"][-1]

    def early_grads_sent(after, small):
        trips["early"] = partial_sums(trips["swap"], after)
        stat3, stat1, zstat, cstat, dws, dbs = small
        sums = _small_chip_sums([stat3, stat1, zstat, cstat.reshape(N_SHARD * STAT_ROWS, FF_BLK),
                                 dws.reshape(N_HEADS * BLK, BLK), dbs])
        trips["small"] = _gather_start(sums, [False] * len(sums), trips["early"][-1], "small")
        return trips["small"][-1]

    grad_x, g_w_in = _local_step(
        x[0], p[0, 0], rope, loss_target[0], w_in_full, start_dep, late_landed, late_weights, early_grads, early_grads_sent,
        early_grads_landed, ln_z_g, ln_z_b, w_s, b_s, ln1_g, ln1_b, conv_b, ln2_g, ln2_b, b_ple_gate, ln3_g, ln3_b)

    trips["w_in"] = partial_sums(swap_started(["w_in"], [g_w_in], "w_in"), g_w_in)
    out = {}

    def adamw(red, tag):
        names = list(red)
        steps = _adamw_shards(f"adamw_{tag}", [w[k] for k in names], [red[k] for k in names], [m[k] for k in names],
                              [v[k] for k in names])
        for k, (d, nm, nv) in zip(names, steps):
            out[k] = (red[k].reshape(w[k].shape), d, nm, nv)

    names, j_send, j_recv, j_flight, _ = trips["join"]
    adamw(dict(zip(names, _join_wait(j_send, j_recv, j_flight, trips["w_in"][-1], "early"))), "early")
    adamw(reduced(trips["w_in"], out["w_o"][3], start_dep), "w_in")
    for k in TRANSPOSED:
        out[k] = tuple(jnp.swapaxes(a, 1, 2) for a in out[k])

    s_send, s_recv, s_flight, _ = trips["small"]
    red3, red1, redz, redc, red_ws, red_bs = _small_totals(
        _gather_wait(s_send, s_recv, s_flight, [False] * len(s_flight), out["w_in"][3], "small"))
    loss = (0.5 / D_MODEL) * jnp.sum(red3[5])
    g_conv_w = lax.dynamic_slice_in_dim(redc, chip * STAT_ROWS, STAT_ROWS, 0)
    names_small = [k for k in WEIGHTS if k not in BIG]
    out.update(_adamw_small(red3, red1, redz, g_conv_w, redc, red_ws, red_bs, {k: (w[k], m[k], v[k]) for k in names_small}))

    return (loss, grad_x[None], *[out[k][0] for k in WEIGHTS], *[out[k][1] for k in WEIGHTS],
            *[out[k][2] for k in WEIGHTS], *[out[k][3] for k in WEIGHTS])
```

```python
import functools
import math

import numpy as np
import jax
import jax.numpy as jnp
from jax import lax
from jax.experimental import pallas as pl
from jax.experimental.pallas import tpu as pltpu

F32 = jnp.float32
BF16 = jnp.bfloat16
MXU = BF16

D_MODEL = 1024
HEAD_DIM = 64
N_HEADS = 8
D_ATTN = 512
D_GMLP = 512
D_IN = 2560
DILATIONS = (1, 4, 16)
BLK = 128
ROPE_THETA = 500000.0
ROPE_DIM = 16
D_FF = 2816
D_PLE = 256
LN_EPS = 1e-5
ALPHA = 2.0 ** 0.25
NEG_INF = -1e30
N_SHARD = 4
W_IN_BLK = D_IN // N_SHARD
FF_BLK = D_FF // N_SHARD
ROW_BLK = D_MODEL // N_SHARD
ADAM_LR, ADAM_B1, ADAM_B2, ADAM_EPS, ADAM_WD, ADAM_STEP = 0.001, 0.9, 0.999, 1e-08, 0.01, 10

TM = 512
HALO = 8
ROW_GROUPS = 2
VMEM_LIMIT = 56 * 1024 * 1024


def _cp(**kw):
    return pltpu.CompilerParams(vmem_limit_bytes=VMEM_LIMIT, **kw)


def _full(shape):
    n = len(shape)
    return pl.BlockSpec(shape, lambda *_: (0,) * n)


def _gelu(x):
    return 0.5 * x * (1.0 + lax.erf(x * (1.0 / math.sqrt(2.0))))


def _gelu_grad(x):
    return 0.5 * (1.0 + lax.erf(x * (1.0 / math.sqrt(2.0)))) + x * jnp.exp(-0.5 * x * x) * (1.0 / math.sqrt(2.0 * math.pi))


def _ln_fwd(r):
    mu = jnp.mean(r, axis=-1, keepdims=True)
    xc = r - mu
    var = jnp.mean(xc * xc, axis=-1, keepdims=True)
    rstd = lax.rsqrt(var + LN_EPS)
    return xc * rstd, rstd


def _ln_bwd(dy, xhat, rstd, g):
    dxh = dy * g
    m1 = jnp.mean(dxh, axis=-1, keepdims=True)
    m2 = jnp.mean(dxh * xhat, axis=-1, keepdims=True)
    return rstd * (dxh - m1 - xhat * m2)


def _dot(a, b):
    return jnp.dot(a.astype(MXU), b.astype(MXU), preferred_element_type=F32)


def _dot_nt(a, b):
    return lax.dot_general(a.astype(MXU), b.astype(MXU), (((1,), (1,)), ((), ())), preferred_element_type=F32)


def _dot_tn(a, b):
    return lax.dot_general(a.astype(MXU), b.astype(MXU), (((0,), (0,)), ((), ())), preferred_element_type=F32)


def _colsum(v):
    return jnp.sum(v, axis=0, keepdims=True)


def _rope_tables(positions, t, dep):
    inv = np.float32(ROPE_THETA) ** (-np.arange(0, ROPE_DIM, 2, dtype=np.float32) / np.float32(ROPE_DIM))
    half = ROPE_DIM // 2
    pos_rep = jnp.repeat(positions.reshape(t // 16, 16), half, axis=1)
    inv_row = jnp.asarray(np.tile(inv, 16)[None, :], F32)

    def trig_body(pos_ref, inv_ref, dep_ref, cos_ref, sin_ref):
        ang = pos_ref[...].astype(F32) * inv_ref[...]
        cos_ref[...] = jnp.cos(ang)
        sin_ref[...] = jnp.sin(ang)

    vm = pl.BlockSpec(memory_space=pltpu.VMEM)
    cos8, sin8 = pl.pallas_call(
        trig_body, name="rope_trig", in_specs=[vm, vm, pl.BlockSpec(memory_space=pl.ANY)], out_specs=[vm, vm],
        out_shape=(jax.ShapeDtypeStruct((t // 16, 128), F32), jax.ShapeDtypeStruct((t // 16, 128), F32)),
    )(pos_rep, inv_row, dep)
    cos8 = cos8.reshape(t, half)
    sin8 = sin8.reshape(t, half)

    lane = np.arange(128) % HEAD_DIM
    sel = (np.arange(half)[:, None] == (lane % half)[None, :])
    e_cos = (sel & (lane < ROPE_DIM)[None, :]).astype(np.float32)
    e_s1 = -(sel & (lane < half)[None, :]).astype(np.float32)
    e_s2 = (sel & ((lane >= half) & (lane < ROPE_DIM))[None, :]).astype(np.float32)
    ones = (lane >= ROPE_DIM).astype(np.float32)[None, :]

    def expand_body(cos_ref, sin_ref, ec_ref, e1_ref, e2_ref, ones_ref, c_ref, s1_ref, s2_ref):
        hp = lax.Precision.HIGHEST
        c_ref[...] = jnp.dot(cos_ref[...], ec_ref[...], precision=hp, preferred_element_type=F32) + ones_ref[...]
        s1_ref[...] = jnp.dot(sin_ref[...], e1_ref[...], precision=hp, preferred_element_type=F32)
        s2_ref[...] = jnp.dot(sin_ref[...], e2_ref[...], precision=hp, preferred_element_type=F32)

    tab = jax.ShapeDtypeStruct((t, 128), F32)
    return pl.pallas_call(expand_body, name="rope_expand", out_shape=(tab, tab, tab), compiler_params=_cp())(
        cos8, sin8, jnp.asarray(e_cos), jnp.asarray(e_s1), jnp.asarray(e_s2), jnp.asarray(ones))


def _tile_heads(tab):
    return jnp.concatenate([tab] * (D_ATTN // 128), axis=1)


def _rope_apply(v, c, s1, s2):
    n = v.shape[1]
    half = ROPE_DIM // 2
    return v * c + pltpu.roll(v, n - half, 1) * s1 + pltpu.roll(v, half, 1) * s2


def _rope_apply_t(g, c, s1, s2):
    n = g.shape[1]
    half = ROPE_DIM // 2
    return g * c + pltpu.roll(g * s1, half, 1) + pltpu.roll(g * s2, n - half, 1)


LANE_CHUNKS = D_ATTN // 128
HEAD_LANES = 128 // N_HEADS


def _perm_shape(t, d, w, dtype):
    return jax.ShapeDtypeStruct((d, t // d, w), dtype)


def _perm_tile(d, w):
    return pl.BlockSpec((None if d == 1 else d, TM // d, w), lambda i: (0, i, 0))


def _to_planes(ref, scr, d, n_chunks, dtype):
    for r in range(d):
        for cc in range(n_chunks):
            ref[r, :, cc * 128:(cc + 1) * 128] = scr.at[cc][pl.ds(r, TM // d, stride=d), :].astype(dtype)


def _from_planes(ref, scr, d, n_chunks, accumulate=False):
    for r in range(d):
        for cc in range(n_chunks):
            rows = scr.at[cc]
            val = ref[r, :, cc * 128:(cc + 1) * 128].astype(F32)
            if accumulate:
                rows[pl.ds(r, TM // d, stride=d), :] += val
            else:
                rows[pl.ds(r, TM // d, stride=d), :] = val


def _chunks(val):
    return [val[:, cc * 128:(cc + 1) * 128] for cc in range(val.shape[1] // 128)]


def _unchunk(scr, n_chunks, base=0):
    return jnp.concatenate([scr[base + cc] for cc in range(n_chunks)], axis=1)


def _head_expand():
    src = np.arange(128)[:, None]
    dst = np.arange(D_ATTN)[None, :]
    return jnp.asarray((src == (dst // HEAD_DIM) * HEAD_LANES).astype(np.float32))


def _head_reduce():
    src = np.arange(D_ATTN)[:, None]
    dst = np.arange(128)[None, :]
    return jnp.asarray((src // HEAD_DIM == dst // HEAD_LANES).astype(np.float32))


def _dot_select(a, sel):
    hi = a.astype(BF16)
    lo = (a - hi.astype(F32)).astype(BF16)
    sel = sel.astype(BF16)
    return jnp.dot(hi, sel, preferred_element_type=F32) + jnp.dot(lo, sel, preferred_element_type=F32)


def _qkvuz(x, w_in, c_tab, s1_tab, s2_tab, ln_z_g, ln_z_b, w_s, b_full, dep):
    t = x.shape[0]
    nchunk = TM // BLK

    def body(x_ref, w_ref, c_ref, s1_ref, s2_ref, g_ref, b_ref, ws_ref, bf_ref, dep_ref,
             qkv1_ref, qkv4_ref, qkv16_ref, hu_ref, hz_ref, mixed_ref, gm_ref, xb_ref, h_scr, wm_scr, p_scr):
        @pl.when(pl.program_id(0) == 0)
        def _():
            row = lax.broadcasted_iota(jnp.int32, (BLK, BLK), 0)
            col = lax.broadcasted_iota(jnp.int32, (BLK, BLK), 1)
            for g in range(N_HEADS):
                wm_scr[g] = jnp.where(col <= row, ws_ref[g], 0.0).astype(MXU)

        xb = x_ref[...].astype(MXU)
        xb_ref[...] = xb
        for j in range(N_SHARD):
            h_scr[:, j * W_IN_BLK:(j + 1) * W_IN_BLK] = jnp.dot(xb, w_ref[j], preferred_element_type=F32)
        c, s1, s2 = _tile_heads(c_ref[...]), _tile_heads(s1_ref[...]), _tile_heads(s2_ref[...])
        q = _rope_apply(h_scr[:, 0:D_ATTN], c, s1, s2) * (1.0 / math.sqrt(HEAD_DIM))
        k = _rope_apply(h_scr[:, D_ATTN:2 * D_ATTN], c, s1, s2)
        for part, val in enumerate((q, k, h_scr[:, 2 * D_ATTN:3 * D_ATTN])):
            qkv1_ref[:, part * D_ATTN:(part + 1) * D_ATTN] = val.astype(MXU)
            for cc in range(LANE_CHUNKS):
                p_scr[part * LANE_CHUNKS + cc] = val[:, cc * 128:(cc + 1) * 128]
        _to_planes(qkv4_ref, p_scr, DILATIONS[1], 3 * LANE_CHUNKS, MXU)
        _to_planes(qkv16_ref, p_scr, DILATIONS[2], 3 * LANE_CHUNKS, MXU)
        hu = h_scr[:, 3 * D_ATTN:3 * D_ATTN + D_GMLP]
        hz = h_scr[:, 3 * D_ATTN + D_GMLP:]
        hu_ref[...] = hu
        hz_ref[...] = hz
        zhat, _ = _ln_fwd(_gelu(hz))
        zn = (zhat * g_ref[...] + b_ref[...]).astype(MXU)
        for ch in range(nchunk):
            rows = slice(ch * BLK, (ch + 1) * BLK)
            for g in range(N_HEADS):
                cols = slice(g * HEAD_DIM, (g + 1) * HEAD_DIM)
                mixed_ref[rows, cols] = jnp.dot(wm_scr[g], zn[rows, cols], preferred_element_type=F32) + bf_ref[:, cols]
        gm_ref[...] = (_gelu(hu) * mixed_ref[...]).astype(MXU)

    tok = lambda w: pl.BlockSpec((TM, w), lambda i: (i, 0))
    outs = [_perm_shape(t, d, 3 * D_ATTN, MXU) for d in DILATIONS] + [jax.ShapeDtypeStruct((t, D_GMLP), F32)] * 3 + [
        jax.ShapeDtypeStruct((t, D_GMLP), MXU), jax.ShapeDtypeStruct((t, D_MODEL), MXU)]
    return pl.pallas_call(
        body, name="qkvuz", grid=(t // TM,),
        in_specs=[tok(D_MODEL), _full(w_in.shape), tok(128), tok(128), tok(128), _full(ln_z_g.shape), _full(ln_z_b.shape),
                  _full(w_s.shape), _full(b_full.shape), pl.BlockSpec(memory_space=pl.ANY)],
        out_specs=[_perm_tile(d, 3 * D_ATTN) for d in DILATIONS] + [tok(D_ATTN)] * 4 + [tok(D_MODEL)], out_shape=outs,
        scratch_shapes=[pltpu.VMEM((TM, D_IN), F32), pltpu.VMEM((N_HEADS, BLK, BLK), MXU),
                        pltpu.VMEM((3 * LANE_CHUNKS, TM, 128), F32)],
        compiler_params=_cp(dimension_semantics=("arbitrary",)),
    )(x, w_in, c_tab, s1_tab, s2_tab, ln_z_g, ln_z_b, w_s, b_full, dep)


def _band_valid(n):
    i = lax.broadcasted_iota(jnp.int32, (BLK, 2 * BLK), 0)
    j = lax.broadcasted_iota(jnp.int32, (BLK, 2 * BLK), 1)
    return (j >= i) & (j <= i + BLK) & ((j >= BLK) | (n > 0))


def _attn_fwd(qkv, d, dep):
    _, l_sub, _ = qkv.shape
    nb = l_sub // BLK

    def body(q_ref, kp_ref, kc_ref, vp_ref, vc_ref, dep_ref, o_ref, l_ref):
        valid = _band_valid(pl.program_id(1))
        kcat = jnp.concatenate([kp_ref[...], kc_ref[...]], axis=0)
        vcat = jnp.concatenate([vp_ref[...], vc_ref[...]], axis=0)
        for h in range(N_HEADS):
            cols = slice(h * HEAD_DIM, (h + 1) * HEAD_DIM)
            s = jnp.where(valid, _dot_nt(q_ref[:, cols], kcat[:, cols]), NEG_INF)
            m = jnp.max(s, axis=-1, keepdims=True)
            e = jnp.exp(s - m)
            den = jnp.sum(e, axis=-1, keepdims=True)
            o_ref[:, cols] = _dot(e, vcat[:, cols]) * (1.0 / den)
            l_ref[:, h * HEAD_LANES:(h + 1) * HEAD_LANES] = jnp.broadcast_to(m + jnp.log(den), (BLK, HEAD_LANES))

    def blk(w, col, prev=False):
        return pl.BlockSpec((None, BLK, w), lambda r, n: (r, jnp.maximum(n - 1, 0) if prev else n, col))

    return pl.pallas_call(
        body, name=f"attn_fwd_d{d}", grid=(d, nb),
        in_specs=[blk(D_ATTN, 0), blk(D_ATTN, 1, True), blk(D_ATTN, 1), blk(D_ATTN, 2, True), blk(D_ATTN, 2),
                  pl.BlockSpec(memory_space=pl.ANY)],
        out_specs=[blk(D_ATTN, 0), blk(128, 0)],
        out_shape=[jax.ShapeDtypeStruct((d, l_sub, D_ATTN), F32), jax.ShapeDtypeStruct((d, l_sub, 128), F32)],
        compiler_params=_cp(dimension_semantics=("arbitrary", "arbitrary")),
    )(qkv, qkv, qkv, qkv, qkv, dep)


def _attn_bwd(qkv, do, lse, delta, d, dep):
    _, l_sub, _ = qkv.shape
    nb = l_sub // BLK
    whole = l_sub <= 8 * BLK

    def shares(n, q_ref, kp_ref, kc_ref, vp_ref, vc_ref, do_ref, l_ref, dl_ref, dq_ref):
        valid = _band_valid(n)
        kcat = jnp.concatenate([kp_ref[...], kc_ref[...]], axis=0)
        vcat = jnp.concatenate([vp_ref[...], vc_ref[...]], axis=0)
        for h in range(N_HEADS):
            cols = slice(h * HEAD_DIM, (h + 1) * HEAD_DIM)
            stat = slice(h * HEAD_LANES, h * HEAD_LANES + 1)
            qh, doh = q_ref[:, cols], do_ref[:, cols]
            p = jnp.where(valid, jnp.exp(_dot_nt(qh, kcat[:, cols]) - l_ref[:, stat]), 0.0)
            ds = p * (_dot_nt(doh, vcat[:, cols]) - dl_ref[:, stat])
            dq_ref[:, cols] = _dot(ds, kcat[:, cols])
            yield cols, _dot_tn(ds, qh), _dot_tn(p, doh)

    def body_whole(*refs):
        dk_ref, dv_ref = refs[10:]
        n = pl.program_id(1)
        cur = pl.ds(pl.multiple_of(n * BLK, BLK), BLK)
        prev = pl.ds(pl.multiple_of(jnp.maximum(n - 1, 0) * BLK, BLK), BLK)
        for cols, dk2, dv2 in shares(n, *refs[:8], refs[9]):
            dk_ref[cur, cols] = dk2[BLK:]
            dv_ref[cur, cols] = dv2[BLK:]
            dk_ref[prev, cols] += dk2[0:BLK]
            dv_ref[prev, cols] += dv2[0:BLK]

    def body_carry(*refs):
        dk_ref, dv_ref, ck_scr, cv_scr = refs[10:]
        n = pl.program_id(1)

        @pl.when(n == 0)
        def _():
            ck_scr[...] = jnp.zeros_like(ck_scr)
            cv_scr[...] = jnp.zeros_like(cv_scr)

        @pl.when(n < nb)
        def _():
            for cols, dk2, dv2 in shares(n, *refs[:8], refs[9]):
                dk_ref[:, cols] = ck_scr[:, cols] + dk2[0:BLK]
                dv_ref[:, cols] = cv_scr[:, cols] + dv2[0:BLK]
                ck_scr[:, cols] = dk2[BLK:]
                cv_scr[:, cols] = dv2[BLK:]

        @pl.when(n == nb)
        def _():
            dk_ref[...] = ck_scr[...]
            dv_ref[...] = cv_scr[...]

    def blk(w, col, shift=0):
        return pl.BlockSpec((None, BLK, w), lambda r, n: (r, jnp.clip(n - shift, 0, nb - 1), col))

    if whole:
        dkv_spec = pl.BlockSpec((None, l_sub, D_ATTN), lambda r, n: (r, 0, 0))
        body, steps, scratch = body_whole, nb, []
    else:
        dkv_spec = blk(D_ATTN, 0, 1)
        body, steps, scratch = body_carry, nb + 1, [pltpu.VMEM((BLK, D_ATTN), F32)] * 2
    return pl.pallas_call(
        body, name=f"attn_bwd_d{d}", grid=(d, steps),
        in_specs=[blk(D_ATTN, 0), blk(D_ATTN, 1, 1), blk(D_ATTN, 1), blk(D_ATTN, 2, 1), blk(D_ATTN, 2),
                  blk(D_ATTN, 0), blk(128, 0), blk(128, 0), pl.BlockSpec(memory_space=pl.ANY)],
        out_specs=[blk(D_ATTN, 0), dkv_spec, dkv_spec],
        out_shape=[jax.ShapeDtypeStruct((d, l_sub, D_ATTN), F32)] * 3,
        scratch_shapes=scratch,
        compiler_params=_cp(dimension_semantics=("arbitrary", "arbitrary")),
    )(qkv, qkv, qkv, qkv, qkv, do, lse, delta, dep)


def _mix_ln1(os_, ls_, gm, x, w_o, ln1_g, ln1_b, dep):
    t = x.shape[0]
    expand = _head_expand()

    def body(o1, o4, o16, l1, l4, l16, gm_ref, x_ref, wo_ref, g_ref, b_ref, ex_ref, dep_ref,
             attn_ref, lse1_ref, lse4_ref, lse16_ref, cat_ref, xhat_ref, rstd_ref, x1b_ref, o_scr, l_scr):
        _from_planes(o4, o_scr, DILATIONS[1], LANE_CHUNKS)
        _from_planes(o16, o_scr.at[pl.ds(LANE_CHUNKS, LANE_CHUNKS)], DILATIONS[2], LANE_CHUNKS)
        _from_planes(l4, l_scr, DILATIONS[1], 1)
        _from_planes(l16, l_scr.at[pl.ds(1, 1)], DILATIONS[2], 1)
        la, lb, lc = l1[...], l_scr[0], l_scr[1]
        m = jnp.maximum(jnp.maximum(la, lb), lc)
        ea, eb, ec = jnp.exp(la - m), jnp.exp(lb - m), jnp.exp(lc - m)
        den = ea + eb + ec
        inv = 1.0 / den
        wide = lambda w: _dot_select(w, ex_ref[...])
        attn = (wide(ea * inv) * o1[...] + wide(eb * inv) * _unchunk(o_scr, LANE_CHUNKS)
                + wide(ec * inv) * _unchunk(o_scr, LANE_CHUNKS, LANE_CHUNKS))
        attn_ref[...] = attn
        lse = m + jnp.log(den)
        lse1_ref[...] = lse
        l_scr[2] = lse
        _to_planes(lse4_ref, l_scr.at[pl.ds(2, 1)], DILATIONS[1], 1, F32)
        _to_planes(lse16_ref, l_scr.at[pl.ds(2, 1)], DILATIONS[2], 1, F32)
        cat_ref[:, 0:D_ATTN] = attn.astype(MXU)
        cat_ref[:, D_ATTN:] = gm_ref[...]
        mix = jnp.dot(cat_ref[...], wo_ref[...], preferred_element_type=F32)
        xhat, rstd = _ln_fwd(ALPHA * x_ref[...] + mix)
        xhat_ref[...] = xhat
        rstd_ref[...] = rstd
        x1b_ref[...] = (xhat * g_ref[...] + b_ref[...]).astype(MXU)

    tok = lambda w: pl.BlockSpec((TM, w), lambda i: (i, 0))
    outs = [jax.ShapeDtypeStruct((t, D_ATTN), F32)] + [_perm_shape(t, d, 128, F32) for d in DILATIONS] + [
        jax.ShapeDtypeStruct((t, D_MODEL), MXU), jax.ShapeDtypeStruct((t, D_MODEL), F32), jax.ShapeDtypeStruct((t, 1), F32),
        jax.ShapeDtypeStruct((t, D_MODEL), MXU)]
    return pl.pallas_call(
        body, name="mix_ln1", grid=(t // TM,),
        in_specs=[_perm_tile(d, D_ATTN) for d in DILATIONS] + [_perm_tile(d, 128) for d in DILATIONS]
        + [tok(D_GMLP), tok(D_MODEL), _full(w_o.shape), _full(ln1_g.shape), _full(ln1_b.shape), _full(expand.shape),
           pl.BlockSpec(memory_space=pl.ANY)],
        out_specs=[tok(D_ATTN)] + [_perm_tile(d, 128) for d in DILATIONS] + [tok(D_MODEL), tok(D_MODEL), tok(1), tok(D_MODEL)],
        out_shape=outs,
        scratch_shapes=[pltpu.VMEM((2 * LANE_CHUNKS, TM, 128), F32), pltpu.VMEM((3, TM, 128), F32)],
        compiler_params=_cp(dimension_semantics=("arbitrary",)),
    )(*os_, *ls_, gm, x, w_o, ln1_g, ln1_b, expand, dep)


def _conv_fwd(a_ext, w_ref, b_ref, rows):
    back = [pltpu.roll(a_ext, s, 0)[HALO:HALO + rows] for s in (1, 2)]
    return b_ref[...] + w_ref[2:3, :] * a_ext[HALO:HALO + rows] + w_ref[1:2, :] * back[0] + w_ref[0:1, :] * back[1]


def _ffn_in(x1b, w_a, w_b, conv_w, conv_b):
    t = x1b.shape[0]
    hb = TM // HALO

    def body(x_ref, xh_ref, wa_ref, wb_ref, cw_ref, cb_ref, apre_ref, act_ref, gate_ref, f_ref):
        i = pl.program_id(1)
        a_pre = _dot_nt(x_ref[...], wa_ref[...])
        a_halo = jnp.where(i > 0, _dot_nt(xh_ref[...], wa_ref[...]), 0.0)
        a = _conv_fwd(jnp.concatenate([a_halo, a_pre], axis=0), cw_ref, cb_ref, TM)
        b = _dot_nt(x_ref[...], wb_ref[...])
        cdf = 0.5 * (1.0 + lax.erf(a * (1.0 / math.sqrt(2.0))))
        pdf = jnp.exp(-0.5 * a * a) * (1.0 / math.sqrt(2.0 * math.pi))
        act = a * cdf
        apre_ref[...] = a_pre
        act_ref[...] = act
        gate_ref[...] = b * (cdf + a * pdf)
        f_ref[...] = (act * b).astype(MXU)

    blk = lambda r, c: pl.BlockSpec((None, r, c), lambda j, i: (j, 0, 0))
    tokj = pl.BlockSpec((None, TM, FF_BLK), lambda j, i: (j, i, 0))
    outs = [jax.ShapeDtypeStruct((N_SHARD, t, FF_BLK), F32)] * 3 + [jax.ShapeDtypeStruct((N_SHARD, t, FF_BLK), MXU)]
    return pl.pallas_call(
        body, name="ffn_in", grid=(N_SHARD, t // TM),
        in_specs=[pl.BlockSpec((TM, D_MODEL), lambda j, i: (i, 0)),
                  pl.BlockSpec((HALO, D_MODEL), lambda j, i: (jnp.maximum(i * hb - 1, 0), 0)),
                  blk(FF_BLK, D_MODEL), blk(FF_BLK, D_MODEL), blk(3, FF_BLK), blk(1, FF_BLK)],
        out_specs=[tokj, tokj, tokj, tokj], out_shape=outs,
        compiler_params=_cp(dimension_semantics=("arbitrary", "arbitrary")),
    )(x1b, x1b, w_a, w_b, conv_w, conv_b)


def _ffn_out_ln2(f, w_down, xhat1, ln1_g, ln1_b, ln2_g, ln2_b):
    t = xhat1.shape[0]

    def body(f_ref, wd_ref, xh_ref, g1_ref, b1_ref, g2_ref, b2_ref, xhat_ref, rstd_ref, x2b_ref):
        ff = jnp.dot(f_ref[0], wd_ref[0], preferred_element_type=F32)
        for j in range(1, N_SHARD):
            ff = ff + jnp.dot(f_ref[j], wd_ref[j], preferred_element_type=F32)
        x1 = xh_ref[...] * g1_ref[...] + b1_ref[...]
        xhat, rstd = _ln_fwd(ALPHA * x1 + ff)
        xhat_ref[...] = xhat
        rstd_ref[...] = rstd
        x2b_ref[...] = (xhat * g2_ref[...] + b2_ref[...]).astype(MXU)

    tok = lambda w: pl.BlockSpec((TM, w), lambda i: (i, 0))
    vec = _full((1, D_MODEL))
    outs = [jax.ShapeDtypeStruct((t, D_MODEL), F32), jax.ShapeDtypeStruct((t, 1), F32), jax.ShapeDtypeStruct((t, D_MODEL), MXU)]
    return pl.pallas_call(
        body, name="ffn_out_ln2", grid=(t // TM,),
        in_specs=[pl.BlockSpec((N_SHARD, TM, FF_BLK), lambda i: (0, i, 0)), _full(w_down.shape), tok(D_MODEL), vec, vec, vec, vec],
        out_specs=[tok(D_MODEL), tok(1), tok(D_MODEL)], out_shape=outs,
        compiler_params=_cp(dimension_semantics=("arbitrary",)),
    )(f, w_down, xhat1, ln1_g, ln1_b, ln2_g, ln2_b)


STAT_ROWS = 8


def _ple_loss_bwd(xhat2, rstd2, p, target, ln2_g, ln2_b, w_g, b_g, w_p, ln3_g, ln3_b):
    t = xhat2.shape[0]

    def body(xh2_ref, rs2_ref, p_ref, t_ref, g2_ref, b2_ref, wg_ref, bg_ref, wp_ref, g3_ref, b3_ref,
             dr2_ref, dr2b_ref, dgp_ref, dpp_ref, stat_ref, pp_scr):
        @pl.when(pl.program_id(0) == 0)
        def _():
            stat_ref[...] = jnp.zeros_like(stat_ref)

        xhat2 = xh2_ref[...]
        x2 = xhat2 * g2_ref[...] + b2_ref[...]
        gate = jax.nn.sigmoid(jnp.dot(x2.astype(MXU), wg_ref[...], preferred_element_type=F32) + bg_ref[...])
        pb = p_ref[...].astype(MXU)
        for j in range(N_SHARD):
            pp_scr[:, j * ROW_BLK:(j + 1) * ROW_BLK] = jnp.dot(pb, wp_ref[j], preferred_element_type=F32)
        pp = pp_scr[...]
        xhat3, rstd3 = _ln_fwd(ALPHA * x2 + gate * pp)
        err = xhat3 * g3_ref[...] + b3_ref[...] - t_ref[...]
        dy = err * (1.0 / D_MODEL)
        dr3 = _ln_bwd(dy, xhat3, rstd3, g3_ref[...])
        dgp = dr3 * pp * gate * (1.0 - gate)
        dgp_ref[...] = dgp.astype(MXU)
        dpp_ref[...] = (dr3 * gate).astype(MXU)
        dx2 = ALPHA * dr3 + _dot_nt(dgp, wg_ref[...])
        dr2 = _ln_bwd(dx2, xhat2, rs2_ref[...], g2_ref[...])
        dr2_ref[...] = dr2
        dr2b_ref[...] = dr2.astype(MXU)
        stat_ref[0:1, :] += _colsum(dy * xhat3)
        stat_ref[1:2, :] += _colsum(dy)
        stat_ref[2:3, :] += _colsum(dgp)
        stat_ref[3:4, :] += _colsum(dx2 * xhat2)
        stat_ref[4:5, :] += _colsum(dx2)
        stat_ref[5:6, :] += _colsum(err * err)

    tok = lambda w: pl.BlockSpec((TM, w), lambda i: (i, 0))
    vec = _full((1, D_MODEL))
    outs = [jax.ShapeDtypeStruct((t, D_MODEL), F32)] + [jax.ShapeDtypeStruct((t, D_MODEL), MXU)] * 3 + [
        jax.ShapeDtypeStruct((STAT_ROWS, D_MODEL), F32)]
    return pl.pallas_call(
        body, name="ple_loss_bwd", grid=(t // TM,),
        in_specs=[tok(D_MODEL), tok(1), tok(D_PLE), tok(D_MODEL), vec, vec, _full(w_g.shape), vec, _full(w_p.shape), vec, vec],
        out_specs=[tok(D_MODEL)] * 4 + [_full((STAT_ROWS, D_MODEL))], out_shape=outs,
        scratch_shapes=[pltpu.VMEM((TM, D_MODEL), F32)],
        compiler_params=_cp(dimension_semantics=("arbitrary",)),
    )(xhat2, rstd2, p, target, ln2_g, ln2_b, w_g, b_g, w_p, ln3_g, ln3_b)


def _ffn_bwd(dr2, dr2b, a_pre, act, gate, w_down, w_a, w_b, conv_w, xhat1, rstd1, ln1_g):
    t = dr2.shape[0]
    nt = t // TM
    hb = TM // HALO
    last_h = t // HALO - 1
    halo2 = 2 * HALO

    def body(dr_ref, drb_ref, drbn_ref, ap_ref, act_ref, gate_ref, gaten_ref, wd_ref, wa_ref, wb_ref, cw_ref,
             xh_ref, rs_ref, g1_ref, dap_ref, dbb_ref, dr1_ref, cstat_ref, lstat_ref, acc_scr):
        i, j = pl.program_id(0), pl.program_id(1)

        @pl.when((i == 0) & (j == 0))
        def _():
            cstat_ref[...] = jnp.zeros_like(cstat_ref)
            lstat_ref[...] = jnp.zeros_like(lstat_ref)

        half = TM // ROW_GROUPS
        parts = []
        for r0 in range(0, TM, half):
            rows = pl.ds(r0, half)
            last = r0 + half == TM

            def ext(ref, nxt):
                return jnp.concatenate([ref[rows], nxt[...]], axis=0) if last else ref[r0:r0 + half + HALO]

            drb = jnp.concatenate([drb_ref[rows, :], drbn_ref[...]], axis=0) if last else drb_ref[r0:r0 + half + halo2, :]
            df = _dot_nt(drb, wd_ref[...])[0:half + HALO]
            da = df * ext(gate_ref, gaten_ref)
            if last:
                da = jnp.concatenate([da[0:half], jnp.where(i < nt - 1, da[half:], 0.0)], axis=0)
            ahead = [da[0:half]] + [pltpu.roll(da, half + HALO - s, 0)[0:half] for s in (1, 2)]
            da_pre = cw_ref[2:3, :] * ahead[0] + cw_ref[1:2, :] * ahead[1] + cw_ref[0:1, :] * ahead[2]
            dbb = df[0:half] * act_ref[rows, :]
            dap_ref[rows, :] = da_pre.astype(MXU)
            dbb_ref[rows, :] = dbb.astype(MXU)
            for kk in range(3):
                cstat_ref[j, kk:kk + 1, :] += _colsum(ahead[2 - kk] * ap_ref[rows, :])
            cstat_ref[j, 3:4, :] += _colsum(ahead[0])
            parts.append(_dot(da_pre, wa_ref[...]) + _dot(dbb, wb_ref[...]))
        part = jnp.concatenate(parts, axis=0)

        @pl.when(j == 0)
        def _():
            acc_scr[...] = ALPHA * dr_ref[...] + part

        @pl.when(j > 0)
        def _():
            acc_scr[...] += part

        @pl.when(j == N_SHARD - 1)
        def _():
            dx1 = acc_scr[...]
            xhat1 = xh_ref[...]
            lstat_ref[0:1, :] += _colsum(dx1 * xhat1)
            lstat_ref[1:2, :] += _colsum(dx1)
            dr1_ref[...] = _ln_bwd(dx1, xhat1, rs_ref[...], g1_ref[...])

    tok = lambda w: pl.BlockSpec((TM, w), lambda i, j: (i, 0))
    tokj = pl.BlockSpec((None, TM, FF_BLK), lambda i, j: (j, i, 0))
    nextj = pl.BlockSpec((None, HALO, FF_BLK), lambda i, j: (j, jnp.minimum((i + 1) * hb, last_h), 0))
    blk = lambda r, c: pl.BlockSpec((None, r, c), lambda i, j: (j, 0, 0))
    outs = [jax.ShapeDtypeStruct((N_SHARD, t, FF_BLK), MXU)] * 2 + [
        jax.ShapeDtypeStruct((t, D_MODEL), F32), jax.ShapeDtypeStruct((N_SHARD, STAT_ROWS, FF_BLK), F32),
        jax.ShapeDtypeStruct((STAT_ROWS, D_MODEL), F32)]
    return pl.pallas_call(
        body, name="ffn_bwd", grid=(nt, N_SHARD),
        in_specs=[tok(D_MODEL), tok(D_MODEL),
                  pl.BlockSpec((halo2, D_MODEL), lambda i, j: (jnp.minimum((i + 1) * (hb // 2), last_h // 2), 0)),
                  tokj, tokj, tokj, nextj, blk(FF_BLK, D_MODEL), blk(FF_BLK, D_MODEL), blk(FF_BLK, D_MODEL),
                  blk(3, FF_BLK), tok(D_MODEL), tok(1), _full((1, D_MODEL))],
        out_specs=[tokj, tokj, tok(D_MODEL), _full((N_SHARD, STAT_ROWS, FF_BLK)), _full((STAT_ROWS, D_MODEL))], out_shape=outs,
        scratch_shapes=[pltpu.VMEM((TM, D_MODEL), F32)],
        compiler_params=_cp(dimension_semantics=("arbitrary", "arbitrary")),
    )(dr2, dr2b, dr2b, a_pre, act, gate, gate, w_down, w_a, w_b, conv_w, xhat1, rstd1, ln1_g)


def _mix_bwd(dr1, w_o, hu, hz, mixed, attn, ln_z_g, ln_z_b, w_s, dep):
    t = dr1.shape[0]
    nchunk = TM // BLK

    def body(dr_ref, wo_ref, hu_ref, hz_ref, mx_ref, attn_ref, g_ref, b_ref, ws_ref, grp_ref, red_ref, dep_ref,
             do1_ref, do4_ref, do16_ref, dl1_ref, dl4_ref, dl16_ref, duz_ref, dws_ref, dbs_ref, zstat_ref,
             wm_scr, dzn_scr, dbsum_scr, do_scr, dl_scr):
        @pl.when(pl.program_id(0) == 0)
        def _():
            row = lax.broadcasted_iota(jnp.int32, (BLK, BLK), 0)
            col = lax.broadcasted_iota(jnp.int32, (BLK, BLK), 1)
            for g in range(N_HEADS):
                wm_scr[g] = jnp.where(col <= row, ws_ref[g], 0.0).astype(MXU)
            dws_ref[...] = jnp.zeros_like(dws_ref)
            dbsum_scr[...] = jnp.zeros_like(dbsum_scr)
            zstat_ref[...] = jnp.zeros_like(zstat_ref)

        dcat = _dot_nt(dr_ref[...], wo_ref[...])
        dattn = dcat[:, 0:D_ATTN]
        do1_ref[...] = dattn.astype(MXU)
        for cc, val in enumerate(_chunks(dattn)):
            do_scr[cc] = val
        _to_planes(do4_ref, do_scr, DILATIONS[1], LANE_CHUNKS, MXU)
        _to_planes(do16_ref, do_scr, DILATIONS[2], LANE_CHUNKS, MXU)
        delta = _dot_select(dattn * attn_ref[...], red_ref[...])
        dl1_ref[...] = delta
        dl_scr[0] = delta
        _to_planes(dl4_ref, dl_scr, DILATIONS[1], 1, F32)
        _to_planes(dl16_ref, dl_scr, DILATIONS[2], 1, F32)
        dgm = dcat[:, D_ATTN:]
        hu, hz = hu_ref[...], hz_ref[...]
        u = _gelu(hu)
        duz_ref[:, 0:D_GMLP] = (dgm * mx_ref[...] * _gelu_grad(hu)).astype(MXU)
        dmixed = dgm * u
        dmb = dmixed.astype(MXU)
        zhat, rstd = _ln_fwd(_gelu(hz))
        znb = (zhat * g_ref[...] + b_ref[...]).astype(MXU)
        dbs_acc = jnp.zeros((BLK, D_GMLP), F32)
        for ch in range(nchunk):
            rows = slice(ch * BLK, (ch + 1) * BLK)
            dbs_acc = dbs_acc + dmixed[rows]
            for g in range(N_HEADS):
                cols = slice(g * HEAD_DIM, (g + 1) * HEAD_DIM)
                dzn_scr[rows, cols] = _dot_tn(wm_scr[g], dmb[rows, cols])
                dws_ref[g] += _dot_nt(dmb[rows, cols], znb[rows, cols])
        dbsum_scr[...] += dbs_acc
        dzn = dzn_scr[...]
        zstat_ref[0:1, :] += _colsum(dzn * zhat)
        zstat_ref[1:2, :] += _colsum(dzn)
        duz_ref[:, D_GMLP:] = (_ln_bwd(dzn, zhat, rstd, g_ref[...]) * _gelu_grad(hz)).astype(MXU)

        @pl.when(pl.program_id(0) == nt - 1)
        def _():
            row = lax.broadcasted_iota(jnp.int32, (BLK, BLK), 0)
            col = lax.broadcasted_iota(jnp.int32, (BLK, BLK), 1)
            for g in range(N_HEADS):
                dws_ref[g] = jnp.where(col <= row, dws_ref[g], 0.0)
            dbs_ref[...] = lax.dot_general(grp_ref[...], dbsum_scr[...], (((1,), (1,)), ((), ())),
                                           precision=lax.Precision.HIGHEST, preferred_element_type=F32)

    nt = t // TM
    tok = lambda w: pl.BlockSpec((TM, w), lambda i: (i, 0))
    grp = jnp.asarray((np.arange(D_GMLP)[None, :] // HEAD_DIM == np.arange(N_HEADS)[:, None]).astype(np.float32))
    red = _head_reduce()
    outs = [_perm_shape(t, d, D_ATTN, MXU) for d in DILATIONS] + [_perm_shape(t, d, 128, F32) for d in DILATIONS] + [
        jax.ShapeDtypeStruct((t, 2 * D_GMLP), MXU),
        jax.ShapeDtypeStruct((N_HEADS, BLK, BLK), F32), jax.ShapeDtypeStruct((N_HEADS, BLK), F32),
        jax.ShapeDtypeStruct((STAT_ROWS, D_GMLP), F32)]
    return pl.pallas_call(
        body, name="mix_bwd", grid=(t // TM,),
        in_specs=[tok(D_MODEL), _full(w_o.shape), tok(D_GMLP), tok(D_GMLP), tok(D_GMLP), tok(D_ATTN), _full(ln_z_g.shape),
                  _full(ln_z_b.shape), _full(w_s.shape), _full(grp.shape), _full(red.shape), pl.BlockSpec(memory_space=pl.ANY)],
        out_specs=[_perm_tile(d, D_ATTN) for d in DILATIONS] + [_perm_tile(d, 128) for d in DILATIONS]
        + [tok(2 * D_GMLP), _full((N_HEADS, BLK, BLK)), _full((N_HEADS, BLK)), _full((STAT_ROWS, D_GMLP))],
        out_shape=outs,
        scratch_shapes=[pltpu.VMEM((N_HEADS, BLK, BLK), MXU), pltpu.VMEM((TM, D_GMLP), F32), pltpu.VMEM((BLK, D_GMLP), F32),
                        pltpu.VMEM((LANE_CHUNKS, TM, 128), F32), pltpu.VMEM((1, TM, 128), F32)],
        compiler_params=_cp(dimension_semantics=("arbitrary",)),
    )(dr1, w_o, hu, hz, mixed, attn, ln_z_g, ln_z_b, w_s, grp, red, dep)


def _dx_in(dqs, dks, dvs, duz, dr1, w_in, c_tab, s1_tab, s2_tab):
    t = dr1.shape[0]

    def body(dq1, dq4, dq16, dk1, dk4, dk16, dv1, dv4, dv16, duz_ref, dr_ref, w_ref, c_ref, s1_ref, s2_ref,
             dh_ref, dx_ref, acc_scr):
        sums = []
        for part, (g1, g4, g16) in enumerate(((dq1, dq4, dq16), (dk1, dk4, dk16), (dv1, dv4, dv16))):
            acc = acc_scr.at[pl.ds(part * LANE_CHUNKS, LANE_CHUNKS)]
            for cc in range(LANE_CHUNKS):
                acc[cc] = g1[:, cc * 128:(cc + 1) * 128]
            _from_planes(g4, acc, DILATIONS[1], LANE_CHUNKS, accumulate=True)
            _from_planes(g16, acc, DILATIONS[2], LANE_CHUNKS, accumulate=True)
            sums.append(_unchunk(acc_scr, LANE_CHUNKS, part * LANE_CHUNKS))
        c, s1, s2 = _tile_heads(c_ref[...]), _tile_heads(s1_ref[...]), _tile_heads(s2_ref[...])
        dh_ref[:, 0:D_ATTN] = _rope_apply_t(sums[0] * (1.0 / math.sqrt(HEAD_DIM)), c, s1, s2).astype(MXU)
        dh_ref[:, D_ATTN:2 * D_ATTN] = _rope_apply_t(sums[1], c, s1, s2).astype(MXU)
        dh_ref[:, 2 * D_ATTN:3 * D_ATTN] = sums[2].astype(MXU)
        dh_ref[:, 3 * D_ATTN:] = duz_ref[...]
        dx = ALPHA * dr_ref[...]
        for j in range(N_SHARD):
            dx = dx + _dot_nt(dh_ref[:, j * W_IN_BLK:(j + 1) * W_IN_BLK], w_ref[j])
        dx_ref[...] = dx

    tok = lambda w: pl.BlockSpec((TM, w), lambda i: (i, 0))
    outs = [jax.ShapeDtypeStruct((t, D_IN), MXU), jax.ShapeDtypeStruct((t, D_MODEL), F32)]
    return pl.pallas_call(
        body, name="dx_in", grid=(t // TM,),
        in_specs=[_perm_tile(d, D_ATTN) for d in DILATIONS] * 3
        + [tok(2 * D_GMLP), tok(D_MODEL), _full(w_in.shape), tok(128), tok(128), tok(128)],
        out_specs=[tok(D_IN), tok(D_MODEL)], out_shape=outs,
        scratch_shapes=[pltpu.VMEM((3 * LANE_CHUNKS, TM, 128), F32)],
        compiler_params=_cp(dimension_semantics=("arbitrary",)),
    )(*dqs, *dks, *dvs, duz, dr1, w_in, c_tab, s1_tab, s2_tab)


def _wgrad(name, x, dy, x_spec, dy_spec, out_spec, out_shape, grid, dep=None):
    deps = [] if dep is None else [dep]

    def body(x_ref, dy_ref, *rest):
        rest[-1][...] = _dot_tn(x_ref[...], dy_ref[...])

    return pl.pallas_call(
        body, name=name, grid=grid, in_specs=[x_spec, dy_spec] + [pl.BlockSpec(memory_space=pl.ANY)] * len(deps),
        out_specs=out_spec, out_shape=jax.ShapeDtypeStruct(out_shape, F32),
        compiler_params=_cp(dimension_semantics=("arbitrary",) * len(grid)),
    )(x, dy, *deps)


def _wgrad_pair(name, xa, xb, dy, x_spec, dy_spec, out_spec, out_shape, grid):
    def body(xa_ref, xb_ref, dy_ref, oa_ref, ob_ref):
        dy = dy_ref[...]
        oa_ref[...] = _dot_tn(xa_ref[...], dy)
        ob_ref[...] = _dot_tn(xb_ref[...], dy)

    return pl.pallas_call(
        body, name=name, grid=grid, in_specs=[x_spec, x_spec, dy_spec], out_specs=[out_spec, out_spec],
        out_shape=[jax.ShapeDtypeStruct(out_shape, F32)] * 2,
        compiler_params=_cp(dimension_semantics=("arbitrary",) * len(grid)),
    )(xa, xb, dy)


def _local_step(x, p, rope, target, w_in, start_dep, late_landed, late_weights, early_grads, early_grads_sent,
                early_grads_landed,
                ln_z_g, ln_z_b, w_s, b_s, ln1_g, ln1_b, conv_b, ln2_g, ln2_b, b_g, ln3_g, ln3_b):
    t = x.shape[0]
    half = TM
    c_tab, s1_tab, s2_tab = rope
    b_full = jnp.repeat(jnp.transpose(b_s[0]), HEAD_DIM, axis=1)
    conv_b4 = conv_b.reshape(N_SHARD, 1, FF_BLK)
    *qkvs, hu, hz, mixed, gm, xb = _qkvuz(x, w_in, c_tab, s1_tab, s2_tab, ln_z_g, ln_z_b, w_s[0], b_full, start_dep)
    branches = [_attn_fwd(qkv, d, start_dep) for qkv, d in zip(qkvs[:2], DILATIONS[:2])]
    dep = late_landed(branches[-1][1])
    branches.append(_attn_fwd(qkvs[2], DILATIONS[2], dep))
    w_o, w_a, w_b, conv_w, w_down, w_g, w_p = late_weights(branches[-1][1])
    attn, *lses, cat, xhat1, rstd1, x1b = _mix_ln1(
        [o for o, _ in branches], [l for _, l in branches], gm, x, w_o, ln1_g, ln1_b, dep)
    a_pre, act, gate, f = _ffn_in(x1b, w_a, w_b, conv_w, conv_b4)
    xhat2, rstd2, x2b = _ffn_out_ln2(f, w_down, xhat1, ln1_g, ln1_b, ln2_g, ln2_b)
    dr2, dr2b, dgp, dpp, stat3 = _ple_loss_bwd(xhat2, rstd2, p, target, ln2_g, ln2_b, w_g, b_g, w_p, ln3_g, ln3_b)
    da_pre, dbb, dr1, cstat, stat1 = _ffn_bwd(dr2, dr2b, a_pre, act, gate, w_down, w_a, w_b, conv_w, xhat1, rstd1, ln1_g)

    full_t = lambda w, im: pl.BlockSpec((t, w), im)
    ffj = pl.BlockSpec((None, t, FF_BLK), lambda j, kk: (j, 0, 0))
    early = dict(
        w_ple_gate=_wgrad("dw_g", x2b, dgp, full_t(half, lambda kk, n: (0, kk)), full_t(half, lambda kk, n: (0, n)),
                          pl.BlockSpec((half, half), lambda kk, n: (kk, n)), (D_MODEL, D_MODEL), (2, 2)),
        w_ple_in=_wgrad("dw_p", p, dpp, full_t(D_PLE, lambda j: (0, 0)), full_t(ROW_BLK, lambda j: (0, j)),
                        pl.BlockSpec((None, D_PLE, ROW_BLK), lambda j: (j, 0, 0)), (N_SHARD, D_PLE, ROW_BLK), (N_SHARD,)),
        w_ff_down=_wgrad("dw_down", f, dr2b, ffj, full_t(half, lambda j, n: (0, n)),
                         pl.BlockSpec((None, FF_BLK, half), lambda j, n: (j, 0, n)), (N_SHARD, FF_BLK, D_MODEL), (N_SHARD, 2)),
        **dict(zip(("w_ff_a", "w_ff_b"), _wgrad_pair(
            "dw_ab", da_pre, dbb, x1b, ffj, full_t(half, lambda j, n: (0, n)),
            pl.BlockSpec((None, FF_BLK, half), lambda j, n: (j, 0, n)), (N_SHARD, FF_BLK, D_MODEL), (N_SHARD, 2)))),
        w_o=_wgrad("dw_o", cat, dr1, full_t(half, lambda kk, n: (0, kk)), full_t(half, lambda kk, n: (0, n)),
                   pl.BlockSpec((half, half), lambda kk, n: (kk, n)), (D_MODEL, D_MODEL), (2, 2)))
    dep = early_grads(early)

    do1, do4, do16, dl1, dl4, dl16, duz, dws, dbs, zstat = _mix_bwd(
        dr1, w_o, hu, hz, mixed, attn, ln_z_g, ln_z_b, w_s[0], dep)
    dep = early_grads_sent(duz, (stat3, stat1, zstat, cstat, dws, dbs))
    dqkv = [_attn_bwd(qkv, do, lse, dl, d, dep)
            for qkv, do, lse, dl, d in zip(qkvs, (do1, do4, do16), lses, (dl1, dl4, dl16), DILATIONS)]
    dh, grad_x = _dx_in([g[0] for g in dqkv], [g[1] for g in dqkv], [g[2] for g in dqkv], duz, dr1, w_in,
                        c_tab, s1_tab, s2_tab)
    dep = early_grads_landed(grad_x)
    g_w_in = _wgrad("dw_in", xb, dh, full_t(half, lambda j, kk: (0, kk)), full_t(W_IN_BLK, lambda j, kk: (0, j)),
                    pl.BlockSpec((None, half, W_IN_BLK), lambda j, kk: (j, kk, 0)), (N_SHARD, D_MODEL, W_IN_BLK), (N_SHARD, 2),
                    dep)
    return grad_x, g_w_in


def _tile_rows(rows, mult, steps):
    if rows % mult:
        return rows
    return next(rows // k for k in range(steps, rows + 1) if rows % k == 0 and (rows // k) % mult == 0)


def _grid_spec(grid, in_specs, out_specs):
    return pltpu.PrefetchScalarGridSpec(num_scalar_prefetch=1, grid=grid, in_specs=in_specs, out_specs=out_specs)


def _on_own_steps(i, count, steps, work):
    if count == steps:
        work()
    else:
        pl.when(i < count)(work)


def _place_shards(name, ws, dtypes, place, dep):
    n = len(ws)
    tiles = [_tile_rows(w.shape[0], 16, 8) for w in ws]
    counts = [w.shape[0] // t for w, t in zip(ws, tiles)]
    steps = max(counts)

    def body(s_ref, *refs):
        i = pl.program_id(0)
        for a in range(n):
            def work(a=a):
                refs[n + 1 + a][...] = refs[a][...].astype(dtypes[a])
            _on_own_steps(i, counts[a], steps, work)

    def tile(a, lead):
        last = counts[a] - 1
        if lead:
            return pl.BlockSpec((None, tiles[a], ws[a].shape[1]), lambda i, s: (s[0], jnp.minimum(i, last), 0))
        return pl.BlockSpec((tiles[a], ws[a].shape[1]), lambda i, s: (jnp.minimum(i, last), 0))

    return pl.pallas_call(
        body, name=name,
        grid_spec=_grid_spec((steps,), [tile(a, False) for a in range(n)] + [pl.BlockSpec(memory_space=pl.ANY)],
                             [tile(a, True) for a in range(n)]),
        out_shape=[jax.ShapeDtypeStruct((N_SHARD, *w.shape), dt) for w, dt in zip(ws, dtypes)],
        compiler_params=_cp())(place, *ws, dep)


def _pair_sums(name, mines, gots, place):
    n = len(mines)
    tiles = [_tile_rows(g.shape[1], 16, 2) for g in gots]
    per_blk = [g.shape[1] // t for g, t in zip(gots, tiles)]
    counts = [N_SHARD * nh for nh in per_blk]
    steps = max(counts)

    def body(s_ref, *refs):
        i = pl.program_id(0)
        for a in range(n):
            def work(a=a):
                refs[2 * n + a][...] = (refs[a][...] + refs[n + a][...]).astype(BF16)
            _on_own_steps(i, counts[a], steps, work)

    def tile(a, mine):
        nh, last = per_blk[a], counts[a] - 1

        def index(i, s):
            g = jnp.minimum(i, last)
            return (g // nh, (s[1] * nh if mine else 0) + g % nh, 0)

        return pl.BlockSpec((None, tiles[a], gots[a].shape[2]), index)

    return pl.pallas_call(
        body, name=name,
        grid_spec=_grid_spec((steps,), [tile(a, True) for a in range(n)] + [tile(a, False) for a in range(n)],
                             [tile(a, False) for a in range(n)]),
        out_shape=[jax.ShapeDtypeStruct(g.shape, BF16) for g in gots], compiler_params=_cp())(place, *mines, *gots)


def _chip_sums(name, owns, landeds, place, dep):
    n = len(owns)
    tiles = [_tile_rows(o.shape[1], 16, 8) for o in owns]
    counts = [o.shape[1] // t for o, t in zip(owns, tiles)]
    steps = max(counts)

    def body(s_ref, *refs):
        i = pl.program_id(0)
        for a in range(n):
            def work(a=a):
                own, l1, l2, l3 = (refs[4 * a + k][...].astype(F32) for k in range(4))
                refs[4 * n + 1 + a][...] = ((own + l1) + l2) + l3
            _on_own_steps(i, counts[a], steps, work)

    def slot(a, d):
        last = counts[a] - 1
        return pl.BlockSpec((None, tiles[a], owns[a].shape[2]), lambda i, s: ((s[0] + d) % N_SHARD, jnp.minimum(i, last), 0))

    def out(a):
        nh, last = counts[a], counts[a] - 1
        return pl.BlockSpec((tiles[a], owns[a].shape[2]), lambda i, s: (s[1] * nh + jnp.minimum(i, last), 0))

    operands = [x for o, l in zip(owns, landeds) for x in (o, l, l, l)]
    return pl.pallas_call(
        body, name=name,
        grid_spec=_grid_spec((steps,), [slot(a, d) for a in range(n) for d in range(4)] + [pl.BlockSpec(memory_space=pl.ANY)],
                             [out(a) for a in range(n)]),
        out_shape=[jax.ShapeDtypeStruct((2 * o.shape[1], o.shape[2]), F32) for o in owns],
        compiler_params=_cp())(place, *operands, dep)


def _adamw_math(w, g, m, v):
    m = ADAM_B1 * m + (1.0 - ADAM_B1) * g
    v = ADAM_B2 * v + (1.0 - ADAM_B2) * (g * g)
    m_hat = m / (1.0 - ADAM_B1 ** ADAM_STEP)
    v_hat = v / (1.0 - ADAM_B2 ** ADAM_STEP)
    delta = -ADAM_LR * (m_hat / (jnp.sqrt(v_hat) + ADAM_EPS) + ADAM_WD * w)
    return delta, m, v


def _adamw_shards(name, ws, gs, ms, vs):
    n = len(ws)
    tiles = [_tile_rows(w.shape[1], 8, 8) for w in ws]
    counts = [w.shape[1] // t for w, t in zip(ws, tiles)]
    steps = max(counts)

    def body(*refs):
        i = pl.program_id(0)
        for a in range(n):
            def work(a=a):
                w_ref, g_ref, m_ref, v_ref = refs[4 * a:4 * a + 4]
                d_ref, nm_ref, nv_ref = refs[4 * n + 3 * a:4 * n + 3 * a + 3]
                d_ref[...], nm_ref[...], nv_ref[...] = _adamw_math(w_ref[...], g_ref[...], m_ref[...], v_ref[...])
            _on_own_steps(i, counts[a], steps, work)

    def tile(a, lead):
        last, c = counts[a] - 1, ws[a].shape[2]
        if lead:
            return pl.BlockSpec((None, tiles[a], c), lambda i: (0, jnp.minimum(i, last), 0))
        return pl.BlockSpec((tiles[a], c), lambda i: (jnp.minimum(i, last), 0))

    res = pl.pallas_call(
        body, name=name, grid=(steps,),
        in_specs=[tile(a, lead) for a in range(n) for lead in (True, False, True, True)],
        out_specs=[tile(a, True) for a in range(n) for _ in range(3)],
        out_shape=[jax.ShapeDtypeStruct(w.shape, F32) for w in ws for _ in range(3)],
        compiler_params=_cp())(*[x for quad in zip(ws, gs, ms, vs) for x in quad])
    return [tuple(res[3 * a:3 * a + 3]) for a in range(n)]


MESH = pl.DeviceIdType.MESH
ANY = pl.BlockSpec(memory_space=pl.ANY)


def _place():
    x, y, c = lax.axis_index("x"), lax.axis_index("y"), lax.axis_index("c")
    chips = [(1 - x, y), (x, 1 - y), (1 - x, 1 - y)]
    return x, y, c, 2 * x + y, chips


def _remote(src, dst, send_sem, recv_sem, dev):
    return pltpu.make_async_remote_copy(src_ref=src, dst_ref=dst, send_sem=send_sem, recv_sem=recv_sem,
                                        device_id=dev, device_id_type=MESH)


def _half(ref, hc, rows):
    return ref.at[pl.ds(hc * (rows // 2), rows // 2)]


def _sibling_join(blocks, tag):
    n = len(blocks)

    def body(*refs):
        outs = refs[n:2 * n]
        send, recv = refs[2 * n:]
        x, y, c, _, _ = _place()
        cps = []
        for a in range(n):
            h = blocks[a].shape[0] // 2
            mine = outs[a].at[pl.ds(c * h, h)]
            cp = _remote(mine, mine, send.at[a], recv.at[a], (x, y, 1 - c))
            cp.start()
            cps.append(cp)
        for a, cp in enumerate(cps):
            h = blocks[a].shape[0] // 2
            theirs = outs[a].at[pl.ds((1 - c) * h, h)]
            _remote(theirs, theirs, send.at[a], recv.at[a], (x, y, 1 - c)).wait_recv()
            cp.wait_send()

    sem = pltpu.SemaphoreType.DMA
    return pl.pallas_call(body, name=f"rs_sibling_join_{tag}", in_specs=[ANY] * n, out_specs=[ANY] * n,
                          out_shape=[jax.ShapeDtypeStruct(b_.shape, b_.dtype) for b_ in blocks],
                          input_output_aliases={a: a for a in range(n)},
                          scratch_shapes=[sem((n,)), sem((n,))])(*blocks)


def _join_start(blocks, after, tag):
    n = len(blocks)

    def body(*refs):
        ins = refs[:n]
        send, recv = refs[n + 1], refs[n + 2]
        token = refs[2 * n + 3]
        x, y, c, _, _ = _place()
        for a in range(n):
            h = blocks[a].shape[0] // 2
            mine = ins[a].at[pl.ds(c * h, h)]
            _remote(mine, mine, send.at[a], recv.at[a], (x, y, 1 - c)).start()
        token[...] = jnp.zeros_like(token)

    sems = pltpu.SemaphoreType.DMA((n,))
    res = pl.pallas_call(
        body, name=f"join_start_{tag}", in_specs=[HBM] * n + [ANY],
        out_specs=[SEM, SEM] + [HBM] * n + [pl.BlockSpec(memory_space=pltpu.VMEM)],
        out_shape=[sems, sems] + [pltpu.HBM(b_.shape, b_.dtype) for b_ in blocks] + [TOKEN],
        input_output_aliases={a: a + 2 for a in range(n)}, compiler_params=_in_flight_params(),
    )(*[_in_hbm(b_) for b_ in blocks], after)
    return res[0], res[1], res[2:2 + n], res[2 + n]


def _join_wait(send, recv, blocks, after, tag):
    n = len(blocks)

    def body(*refs):
        ins = refs[:n]
        send_ref, recv_ref = refs[n], refs[n + 1]
        x, y, c, _, _ = _place()
        for a in range(n):
            h = blocks[a].shape[0] // 2
            mine, theirs = ins[a].at[pl.ds(c * h, h)], ins[a].at[pl.ds((1 - c) * h, h)]
            _remote(mine, mine, send_ref.at[a], recv_ref.at[a], (x, y, 1 - c)).wait_send()
            _remote(theirs, theirs, send_ref.at[a], recv_ref.at[a], (x, y, 1 - c)).wait_recv()

    return pl.pallas_call(
        body, name=f"join_wait_{tag}", in_specs=[HBM] * n + [SEM, SEM, ANY], out_specs=[HBM] * n,
        out_shape=[pltpu.HBM(b_.shape, b_.dtype) for b_ in blocks],
        input_output_aliases={a: a for a in range(n)}, compiler_params=_in_flight_params(),
    )(*blocks, send, recv, after)


HBM = pl.BlockSpec(memory_space=pltpu.HBM)
SEM = pl.BlockSpec(memory_space=pltpu.SEMAPHORE)
TOKEN = jax.ShapeDtypeStruct((8, 128), F32)


def _in_flight_params():
    return pltpu.CompilerParams(has_side_effects=pltpu.SideEffectType.DATAFLOW_SIDE_EFFECTING)


def _in_hbm(a):
    return pltpu.with_memory_space_constraint(a, pltpu.HBM)


def _gather_piece(ref, rows, split, slot, hc):
    return _half(ref.at[slot], hc, rows) if split else ref.at[slot]


def _gather_start(stacks, split, after, tag):
    n = len(stacks)

    def body(*refs):
        ins = refs[:n]
        send, recv = refs[n + 1], refs[n + 2]
        token = refs[2 * n + 3]
        _, _, c, j, chips = _place()
        for a in range(n):
            mine = _gather_piece(ins[a], stacks[a].shape[1], split[a], j, c)
            for t in range(3):
                _remote(mine, mine, send.at[3 * a + t], recv.at[3 * a + t], (*chips[t], c)).start()
        token[...] = jnp.zeros_like(token)

    sems = pltpu.SemaphoreType.DMA((3 * n,))
    res = pl.pallas_call(
        body, name=f"gather_start_{tag}", in_specs=[HBM] * n + [ANY],
        out_specs=[SEM, SEM] + [HBM] * n + [pl.BlockSpec(memory_space=pltpu.VMEM)],
        out_shape=[sems, sems] + [pltpu.HBM(s.shape, s.dtype) for s in stacks] + [TOKEN],
        input_output_aliases={a: a + 2 for a in range(n)}, compiler_params=_in_flight_params(),
    )(*[_in_hbm(s) for s in stacks], after)
    return res[0], res[1], res[2:2 + n], res[2 + n]


def _gather_wait(send, recv, stacks, split, after, tag):
    n = len(stacks)

    def body(*refs):
        ins = refs[:n]
        send_ref, recv_ref = refs[n], refs[n + 1]
        _, _, c, j, chips = _place()
        for a in range(n):
            rows = stacks[a].shape[1]
            mine = _gather_piece(ins[a], rows, split[a], j, c)
            for t, (px, py) in enumerate(chips):
                theirs = _gather_piece(ins[a], rows, split[a], 2 * px + py, c)
                _remote(mine, mine, send_ref.at[3 * a + t], recv_ref.at[3 * a + t], (px, py, c)).wait_send()
                _remote(theirs, theirs, send_ref.at[3 * a + t], recv_ref.at[3 * a + t], (px, py, c)).wait_recv()

    return pl.pallas_call(
        body, name=f"gather_wait_{tag}", in_specs=[HBM] * n + [SEM, SEM, ANY], out_specs=[HBM] * n,
        out_shape=[pltpu.HBM(s.shape, s.dtype) for s in stacks],
        input_output_aliases={a: a for a in range(n)}, compiler_params=_in_flight_params(),
    )(*stacks, send, recv, after)


def _gather_forward(stacks, split, tag):
    idx = [a for a in range(len(stacks)) if split[a]]
    n = len(idx)

    def body(*refs):
        outs = refs[n:2 * n]
        send, recv = refs[2 * n:]
        x, y, c, _, chips = _place()
        sends = []
        for t, (px, py) in enumerate(chips):
            for a in range(n):
                blk = _half(outs[a].at[2 * px + py], c, stacks[idx[a]].shape[1])
                cp = _remote(blk, blk, send.at[a, t], recv.at[a, t], (x, y, 1 - c))
                cp.start()
                sends.append(cp)
        for t, (px, py) in enumerate(chips):
            for a in range(n):
                blk = _half(outs[a].at[2 * px + py], 1 - c, stacks[idx[a]].shape[1])
                _remote(blk, blk, send.at[a, t], recv.at[a, t], (x, y, 1 - c)).wait_recv()
        for cp in sends:
            cp.wait_send()

    sem = pltpu.SemaphoreType.DMA
    res = pl.pallas_call(
        body, name=f"gather_forward_{tag}", in_specs=[ANY] * n, out_specs=[ANY] * n,
        out_shape=[jax.ShapeDtypeStruct(stacks[a].shape, stacks[a].dtype) for a in idx],
        input_output_aliases={a: a for a in range(n)}, scratch_shapes=[sem((n, 3)), sem((n, 3))],
    )(*[stacks[a] for a in idx])
    out = list(stacks)
    for a, r in zip(idx, res):
        out[a] = r
    return out


def _forward_start(stacks, after, tag):
    n = len(stacks)

    def body(*refs):
        ins = refs[:n]
        send, recv = refs[n + 1], refs[n + 2]
        token = refs[2 * n + 3]
        x, y, c, _, chips = _place()
        for a in range(n):
            for t, (px, py) in enumerate(chips):
                blk = _half(ins[a].at[2 * px + py], c, stacks[a].shape[1])
                _remote(blk, blk, send.at[3 * a + t], recv.at[3 * a + t], (x, y, 1 - c)).start()
        token[...] = jnp.zeros_like(token)

    sems = pltpu.SemaphoreType.DMA((3 * n,))
    res = pl.pallas_call(
        body, name=f"forward_start_{tag}", in_specs=[HBM] * n + [ANY],
        out_specs=[SEM, SEM] + [HBM] * n + [pl.BlockSpec(memory_space=pltpu.VMEM)],
        out_shape=[sems, sems] + [pltpu.HBM(s.shape, s.dtype) for s in stacks] + [TOKEN],
        input_output_aliases={a: a + 2 for a in range(n)}, compiler_params=_in_flight_params(),
    )(*[_in_hbm(s) for s in stacks], after)
    return res[0], res[1], res[2:2 + n], res[2 + n]


def _forward_wait(send, recv, stacks, after, tag):
    n = len(stacks)

    def body(*refs):
        ins = refs[:n]
        send_ref, recv_ref = refs[n], refs[n + 1]
        x, y, c, _, chips = _place()
        for a in range(n):
            for t, (px, py) in enumerate(chips):
                mine = _half(ins[a].at[2 * px + py], c, stacks[a].shape[1])
                theirs = _half(ins[a].at[2 * px + py], 1 - c, stacks[a].shape[1])
                _remote(mine, mine, send_ref.at[3 * a + t], recv_ref.at[3 * a + t], (x, y, 1 - c)).wait_send()
                _remote(theirs, theirs, send_ref.at[3 * a + t], recv_ref.at[3 * a + t], (x, y, 1 - c)).wait_recv()

    return pl.pallas_call(
        body, name=f"forward_wait_{tag}", in_specs=[HBM] * n + [SEM, SEM, ANY], out_specs=[HBM] * n,
        out_shape=[pltpu.HBM(s.shape, s.dtype) for s in stacks],
        input_output_aliases={a: a for a in range(n)}, compiler_params=_in_flight_params(),
    )(*stacks, send, recv, after)


def _swap_start(grads, tag):
    n = len(grads)

    def body(*refs):
        ins, gots = refs[:n], refs[n:2 * n]
        send, recv = refs[2 * n], refs[2 * n + 1]
        token = refs[4 * n + 2]
        x, y, c, _, _ = _place()
        for a in range(n):
            h = grads[a].shape[1] // 2
            _remote(ins[a].at[:, pl.ds((1 - c) * h, h)], gots[a], send.at[a], recv.at[a], (x, y, 1 - c)).start()
        token[...] = jnp.zeros_like(token)

    sems = pltpu.SemaphoreType.DMA((n,))
    halves = [(g.shape[0], g.shape[1] // 2, g.shape[2]) for g in grads]
    res = pl.pallas_call(
        body, name=f"swap_start_{tag}", in_specs=[HBM] * (2 * n),
        out_specs=[SEM, SEM] + [HBM] * (2 * n) + [pl.BlockSpec(memory_space=pltpu.VMEM)],
        out_shape=[sems, sems] + [pltpu.HBM(g.shape, g.dtype) for g in grads] + [pltpu.HBM(s, F32) for s in halves] + [TOKEN],
        input_output_aliases={a: a + 2 for a in range(2 * n)}, compiler_params=_in_flight_params(),
    )(*[_in_hbm(g) for g in grads], *[_in_hbm(lax.empty(s, F32)) for s in halves])
    return res[0], res[1], res[2:2 + n], res[2 + n:2 + 2 * n], res[2 + 2 * n]


def _swap_wait(send, recv, grads, gots, after, tag):
    n = len(grads)

    def body(*refs):
        ins, lnd = refs[:n], refs[n:2 * n]
        send_ref, recv_ref = refs[2 * n], refs[2 * n + 1]
        x, y, c, _, _ = _place()
        for a in range(n):
            h = grads[a].shape[1] // 2
            cp = _remote(ins[a].at[:, pl.ds((1 - c) * h, h)], lnd[a], send_ref.at[a], recv_ref.at[a], (x, y, 1 - c))
            cp.wait_send()
            cp.wait_recv()

    bufs = [pltpu.HBM(g.shape, g.dtype) for g in grads] + [pltpu.HBM(g.shape, g.dtype) for g in gots]
    res = pl.pallas_call(
        body, name=f"swap_wait_{tag}", in_specs=[HBM] * (2 * n) + [SEM, SEM, ANY], out_specs=[HBM] * (2 * n),
        out_shape=bufs, input_output_aliases={a: a for a in range(2 * n)}, compiler_params=_in_flight_params(),
    )(*grads, *gots, send, recv, after)
    return res[:n], res[n:]


def _exchange_start(parts, tag):
    n = len(parts)

    def body(*refs):
        ins, lands = refs[:n], refs[n:2 * n]
        send, recv = refs[2 * n], refs[2 * n + 1]
        token = refs[4 * n + 2]
        _, _, c, j, chips = _place()
        for t, (px, py) in enumerate(chips):
            for a in range(n):
                _remote(ins[a].at[2 * px + py], lands[a].at[j], send.at[3 * a + t], recv.at[3 * a + t], (px, py, c)).start()
        token[...] = jnp.zeros_like(token)

    sems = pltpu.SemaphoreType.DMA((3 * n,))
    bufs = [pltpu.HBM(p.shape, p.dtype) for p in parts]
    res = pl.pallas_call(
        body, name=f"exchange_start_{tag}", in_specs=[HBM] * (2 * n),
        out_specs=[SEM, SEM] + [HBM] * (2 * n) + [pl.BlockSpec(memory_space=pltpu.VMEM)],
        out_shape=[sems, sems] + bufs + bufs + [TOKEN],
        input_output_aliases={a: a + 2 for a in range(2 * n)}, compiler_params=_in_flight_params(),
    )(*[_in_hbm(p) for p in parts], *[_in_hbm(lax.empty(p.shape, p.dtype)) for p in parts])
    return res[0], res[1], res[2:2 + n], res[2 + n:2 + 2 * n], res[2 + 2 * n]


def _exchange_wait(send, recv, parts, lands, after, tag):
    n = len(parts)

    def body(*refs):
        ins, lnd = refs[:n], refs[n:2 * n]
        send_ref, recv_ref = refs[2 * n], refs[2 * n + 1]
        _, _, c, j, chips = _place()
        for t, (px, py) in enumerate(chips):
            jt = 2 * px + py
            for a in range(n):
                _remote(ins[a].at[jt], lnd[a].at[j], send_ref.at[3 * a + t], recv_ref.at[3 * a + t], (px, py, c)).wait_send()
                _remote(ins[a].at[jt], lnd[a].at[jt], send_ref.at[3 * a + t], recv_ref.at[3 * a + t], (px, py, c)).wait_recv()

    bufs = [pltpu.HBM(p.shape, p.dtype) for p in parts]
    res = pl.pallas_call(
        body, name=f"exchange_wait_{tag}", in_specs=[HBM] * (2 * n) + [SEM, SEM, ANY], out_specs=[HBM] * (2 * n),
        out_shape=bufs + bufs, input_output_aliases={a: a for a in range(2 * n)}, compiler_params=_in_flight_params(),
    )(*parts, *lands, send, recv, after)
    return res[:n], res[n:]


def _small_chip_sums(arrs):
    n = len(arrs)

    def body(*refs):
        ins, outs = refs[:n], refs[n:2 * n]
        sib = refs[2 * n:3 * n]
        send, recv = refs[3 * n:]
        x, y, c, j, _ = _place()
        swaps = [_remote(ins[a], sib[a], send.at[a], recv.at[a], (x, y, 1 - c)) for a in range(n)]
        for cp in swaps:
            cp.start()
        for a in range(n):
            swaps[a].wait_recv()
            outs[a][j] = ins[a][...] + sib[a][...]
        for cp in swaps:
            cp.wait_send()

    sem = pltpu.SemaphoreType.DMA
    vm = pl.BlockSpec(memory_space=pltpu.VMEM)
    return pl.pallas_call(
        body, name="small_chip_sums", in_specs=[vm] * n, out_specs=[vm] * n,
        out_shape=[jax.ShapeDtypeStruct((N_SHARD, *a.shape), F32) for a in arrs],
        scratch_shapes=[pltpu.VMEM(a.shape, F32) for a in arrs] + [sem((n,)), sem((n,))],
        compiler_params=_cp(),
    )(*arrs)


def _small_totals(stacks):
    n = len(stacks)

    def body(*refs):
        for a in range(n):
            refs[n + a][...] = ((refs[a][0] + refs[a][1]) + refs[a][2]) + refs[a][3]

    return pl.pallas_call(body, name="small_totals", out_shape=[jax.ShapeDtypeStruct(s.shape[1:], F32) for s in stacks],
                          compiler_params=_cp())(*stacks)


SMALL_1024 = ("ln1_g", "ln1_b", "ln2_g", "ln2_b", "b_ple_gate", "ln3_g", "ln3_b")


def _adamw_small(red3, red1, redz, g_conv_w, redc, red_ws, red_bs, params):
    shape2d = {"ln_z_g": (1, D_GMLP), "ln_z_b": (1, D_GMLP), "w_s": (N_HEADS * BLK, BLK), "b_s": (N_HEADS, BLK),
               "conv_w": (3, FF_BLK), "conv_b": (N_SHARD, FF_BLK), **{k: (1, D_MODEL) for k in SMALL_1024}}
    names = list(shape2d)
    flat = [a.reshape(shape2d[k]) for k in names for a in params[k]]

    def body(r3, r1, rz, gcw, rc, rws, rbs, *refs):
        ins, outs = refs[:3 * len(names)], refs[3 * len(names):]

        def grad_of(k):
            if k == "w_s":
                return rws[...]
            if k == "b_s":
                return rbs[...]
            if k == "conv_w":
                return gcw[0:3, :]
            if k == "conv_b":
                return jnp.concatenate([rc[j * STAT_ROWS + 3:j * STAT_ROWS + 4, :] for j in range(N_SHARD)], axis=0)
            src, row = {"ln3_g": (r3, 0), "ln3_b": (r3, 1), "b_ple_gate": (r3, 2), "ln2_g": (r3, 3), "ln2_b": (r3, 4),
                        "ln1_g": (r1, 0), "ln1_b": (r1, 1), "ln_z_g": (rz, 0), "ln_z_b": (rz, 1)}[k]
            return src[row:row + 1, :]

        for i, k in enumerate(names):
            w_ref, m_ref, v_ref = ins[3 * i:3 * i + 3]
            g_ref, d_ref, nm_ref, nv_ref = outs[4 * i:4 * i + 4]
            g = grad_of(k)
            g_ref[...] = g
            d_ref[...], nm_ref[...], nv_ref[...] = _adamw_math(w_ref[...], g, m_ref[...], v_ref[...])

    res = pl.pallas_call(
        body, name="adamw_small",
        out_shape=[jax.ShapeDtypeStruct(shape2d[k], F32) for k in names for _ in range(4)],
        compiler_params=_cp(),
    )(red3, red1, redz, g_conv_w, redc, red_ws, red_bs, *flat)
    return {k: tuple(r.reshape(params[k][0].shape) for r in res[4 * i:4 * i + 4]) for i, k in enumerate(names)}


WEIGHTS = ("w_in", "ln_z_g", "ln_z_b", "w_s", "b_s", "w_o", "ln1_g", "ln1_b", "w_ff_a", "w_ff_b", "conv_w", "conv_b",
           "w_ff_down", "ln2_g", "ln2_b", "w_ple_gate", "b_ple_gate", "w_ple_in", "ln3_g", "ln3_b")
BIG = ("w_in", "w_o", "w_ff_a", "w_ff_b", "w_ff_down", "w_ple_gate", "w_ple_in")
TRANSPOSED = ("w_ff_a", "w_ff_b")
LATE = ("w_o", "w_ff_a", "w_ff_b", "w_ff_down", "w_ple_gate", "w_ple_in", "conv_w")


def kernel(x, p, positions, w_in, ln_z_g, ln_z_b, w_s, b_s, w_o, ln1_g, ln1_b, w_ff_a, w_ff_b, conv_w, conv_b, w_ff_down, ln2_g, ln2_b, w_ple_gate, b_ple_gate, w_ple_in, ln3_g, ln3_b, loss_target, m_w_in, m_ln_z_g, m_ln_z_b, m_w_s, m_b_s, m_w_o, m_ln1_g, m_ln1_b, m_w_ff_a, m_w_ff_b, m_conv_w, m_conv_b, m_w_ff_down, m_ln2_g, m_ln2_b, m_w_ple_gate, m_b_ple_gate, m_w_ple_in, m_ln3_g, m_ln3_b, v_w_in, v_ln_z_g, v_ln_z_b, v_w_s, v_b_s, v_w_o, v_ln1_g, v_ln1_b, v_w_ff_a, v_w_ff_b, v_conv_w, v_conv_b, v_w_ff_down, v_ln2_g, v_ln2_b, v_w_ple_gate, v_b_ple_gate, v_w_ple_in, v_ln3_g, v_ln3_b):
    args = locals()
    w = {k: args[k] for k in WEIGHTS}
    m = {k: args["m_" + k] for k in WEIGHTS}
    v = {k: args["v_" + k] for k in WEIGHTS}

    for k in TRANSPOSED:
        w[k], m[k], v[k] = (jnp.swapaxes(a, 1, 2) for a in (w[k], m[k], v[k]))

    chip = 2 * lax.axis_index("x") + lax.axis_index("y")
    place = jnp.stack([chip, lax.axis_index("c")]).astype(jnp.int32)
    stack = dict(zip(["w_in"], _place_shards("cast_w_in", [w["w_in"][0]], [MXU], place, place)))
    i_send, i_recv, in_flight, dep = _gather_start([stack["w_in"]], [True], place, "w_in")
    stack.update(zip(LATE, _place_shards("cast_late", [w[k][0] for k in LATE],
                                         [F32 if k == "conv_w" else MXU for k in LATE], place, dep)))
    split_late = [k != "conv_w" for k in LATE]
    g_send, g_recv, late_flight, start_dep = _gather_start([stack[k] for k in LATE], split_late, place, "late")
    rope = _rope_tables(positions, x.shape[1], start_dep)
    landed_in = _gather_wait(i_send, i_recv, in_flight, [True], rope[0], "w_in")
    w_in_full, = _gather_forward(landed_in, [True], "w_in")
    halves =[k for k, sp in zip(LATE, split_late) if sp]
    trips = {}

    def late_landed(after):
        fw = dict(zip(LATE, _gather_wait(g_send, g_recv, late_flight, split_late, after, "late")))
        trips["late"] = (fw, *_forward_start([fw[k] for k in halves], fw["conv_w"], "late"))
        return trips["late"][-1]

    def late_weights(after):
        fw, send, recv, flight, _ = trips["late"]
        fw.update(zip(halves, _forward_wait(send, recv, flight, after, "late")))
        return (fw["w_o"].reshape(D_MODEL, D_MODEL), fw["w_ff_a"], fw["w_ff_b"], fw["conv_w"], fw["w_ff_down"],
                fw["w_ple_gate"].reshape(D_MODEL, D_MODEL), fw["w_ple_in"])

    def swap_started(names, grads, tag):
        stacked = [g.reshape(N_SHARD, *w[k].shape[1:]) for k, g in zip(names, grads)]
        return (names, tag, *_swap_start(stacked, tag))

    def partial_sums(swap, after):
        names, tag, send, recv, stacked, gots, _ = swap
        stacked, got = _swap_wait(send, recv, stacked, gots, after, tag)
        pair = _pair_sums(f"rs_pair_{tag}", stacked, got, place)
        return (names, tag, *_exchange_start(pair, tag))

    def chip_summed(trip, after, dep):
        names, tag, send, recv, pair, lands, _ = trip
        pair, landed = _exchange_wait(send, recv, pair, lands, after, tag)
        return _chip_sums(f"rs_sum_{tag}", pair, landed, place, dep), names, tag

    def reduced(trip, after, dep):
        blocks, names, tag = chip_summed(trip, after, dep)
        return dict(zip(names, _sibling_join(blocks, tag)))

    def early_grads_landed(after):
        blocks, names, tag = chip_summed(trips["early"], after, trips["small"][-1])
        trips["join"] = (names, *_join_start(blocks, after, tag))
        return trips["join"][-1]

    def early_grads(grads):
        trips["swap"] = swap_started(list(grads), list(grads.values()), "early")
        return trips["swap"][-1]

    def early_grads_sent(after, small):
        trips["early"] = partial_sums(trips["swap"], after)
        stat3, stat1, zstat, cstat, dws, dbs = small
        sums = _small_chip_sums([stat3, stat1, zstat, cstat.reshape(N_SHARD * STAT_ROWS, FF_BLK),
                                 dws.reshape(N_HEADS * BLK, BLK), dbs])
        trips["small"] = _gather_start(sums, [False] * len(sums), trips["early"][-1], "small")
        return trips["small"][-1]

    grad_x, g_w_in = _local_step(
        x[0], p[0, 0], rope, loss_target[0], w_in_full, start_dep, late_landed, late_weights, early_grads, early_grads_sent,
        early_grads_landed, ln_z_g, ln_z_b, w_s, b_s, ln1_g, ln1_b, conv_b, ln2_g, ln2_b, b_ple_gate, ln3_g, ln3_b)

    trips["w_in"] = partial_sums(swap_started(["w_in"], [g_w_in], "w_in"), g_w_in)
    out = {}

    def adamw(red, tag):
        names = list(red)
        steps = _adamw_shards(f"adamw_{tag}", [w[k] for k in names], [red[k] for k in names], [m[k] for k in names],
                              [v[k] for k in names])
        for k, (d, nm, nv) in zip(names, steps):
            out[k] = (red[k].reshape(w[k].shape), d, nm, nv)

    names, j_send, j_recv, j_flight, _ = trips["join"]
    adamw(dict(zip(names, _join_wait(j_send, j_recv, j_flight, trips["w_in"][-1], "early"))), "early")
    adamw(reduced(trips["w_in"], out["w_o"][3], start_dep), "w_in")
    for k in TRANSPOSED:
        out[k] = tuple(jnp.swapaxes(a, 1, 2) for a in out[k])

    s_send, s_recv, s_flight, _ = trips["small"]
    red3, red1, redz, redc, red_ws, red_bs = _small_totals(
        _gather_wait(s_send, s_recv, s_flight, [False] * len(s_flight), out["w_in"][3], "small"))
    loss = (0.5 / D_MODEL) * jnp.sum(red3[5])
    g_conv_w = lax.dynamic_slice_in_dim(redc, chip * STAT_ROWS, STAT_ROWS, 0)
    names_small = [k for k in WEIGHTS if k not in BIG]
    out.update(_adamw_small(red3, red1, redz, g_conv_w, redc, red_ws, red_bs, {k: (w[k], m[k], v[k]) for k in names_small}))

    return (loss, grad_x[None], *[out[k][0] for k in WEIGHTS], *[out[k][1] for k in WEIGHTS],
            *[out[k][2] for k in WEIGHTS], *[out[k][3] for k in WEIGHTS])
```

```python
import functools
import math

import numpy as np
import jax
import jax.numpy as jnp
from jax import lax
from jax.experimental import pallas as pl
from jax.experimental.pallas import tpu as pltpu

F32 = jnp.float32
BF16 = jnp.bfloat16
MXU = BF16

D_MODEL = 1024
HEAD_DIM = 64
N_HEADS = 8
D_ATTN = 512
D_GMLP = 512
D_IN = 2560
DILATIONS = (1, 4, 16)
BLK = 128
ROPE_THETA = 500000.0
ROPE_DIM = 16
D_FF = 2816
D_PLE = 256
LN_EPS = 1e-5
ALPHA = 2.0 ** 0.25
NEG_INF = -1e30
N_SHARD = 4
W_IN_BLK = D_IN // N_SHARD
FF_BLK = D_FF // N_SHARD
ROW_BLK = D_MODEL // N_SHARD
ADAM_LR, ADAM_B1, ADAM_B2, ADAM_EPS, ADAM_WD, ADAM_STEP = 0.001, 0.9, 0.999, 1e-08, 0.01, 10

TM = 512
HALO = 8
ROW_GROUPS = 2
VMEM_LIMIT = 56 * 1024 * 1024


def _cp(**kw):
    return pltpu.CompilerParams(vmem_limit_bytes=VMEM_LIMIT, **kw)


def _full(shape):
    n = len(shape)
    return pl.BlockSpec(shape, lambda *_: (0,) * n)


def _gelu(x):
    return 0.5 * x * (1.0 + lax.erf(x * (1.0 / math.sqrt(2.0))))


def _gelu_grad(x):
    return 0.5 * (1.0 + lax.erf(x * (1.0 / math.sqrt(2.0)))) + x * jnp.exp(-0.5 * x * x) * (1.0 / math.sqrt(2.0 * math.pi))


def _ln_fwd(r):
    mu = jnp.mean(r, axis=-1, keepdims=True)
    xc = r - mu
    var = jnp.mean(xc * xc, axis=-1, keepdims=True)
    rstd = lax.rsqrt(var + LN_EPS)
    return xc * rstd, rstd


def _ln_bwd(dy, xhat, rstd, g):
    dxh = dy * g
    m1 = jnp.mean(dxh, axis=-1, keepdims=True)
    m2 = jnp.mean(dxh * xhat, axis=-1, keepdims=True)
    return rstd * (dxh - m1 - xhat * m2)


def _dot(a, b):
    return jnp.dot(a.astype(MXU), b.astype(MXU), preferred_element_type=F32)


def _dot_nt(a, b):
    return lax.dot_general(a.astype(MXU), b.astype(MXU), (((1,), (1,)), ((), ())), preferred_element_type=F32)


def _dot_tn(a, b):
    return lax.dot_general(a.astype(MXU), b.astype(MXU), (((0,), (0,)), ((), ())), preferred_element_type=F32)


def _colsum(v):
    return jnp.sum(v, axis=0, keepdims=True)


def _rope_tables(positions, t, dep):
    inv = np.float32(ROPE_THETA) ** (-np.arange(0, ROPE_DIM, 2, dtype=np.float32) / np.float32(ROPE_DIM))
    half = ROPE_DIM // 2
    pos_rep = jnp.repeat(positions.reshape(t // 16, 16), half, axis=1)
    inv_row = jnp.asarray(np.tile(inv, 16)[None, :], F32)

    def trig_body(pos_ref, inv_ref, dep_ref, cos_ref, sin_ref):
        ang = pos_ref[...].astype(F32) * inv_ref[...]
        cos_ref[...] = jnp.cos(ang)
        sin_ref[...] = jnp.sin(ang)

    vm = pl.BlockSpec(memory_space=pltpu.VMEM)
    cos8, sin8 = pl.pallas_call(
        trig_body, name="rope_trig", in_specs=[vm, vm, pl.BlockSpec(memory_space=pl.ANY)], out_specs=[vm, vm],
        out_shape=(jax.ShapeDtypeStruct((t // 16, 128), F32), jax.ShapeDtypeStruct((t // 16, 128), F32)),
    )(pos_rep, inv_row, dep)
    cos8 = cos8.reshape(t, half)
    sin8 = sin8.reshape(t, half)

    lane = np.arange(128) % HEAD_DIM
    sel = (np.arange(half)[:, None] == (lane % half)[None, :])
    e_cos = (sel & (lane < ROPE_DIM)[None, :]).astype(np.float32)
    e_s1 = -(sel & (lane < half)[None, :]).astype(np.float32)
    e_s2 = (sel & ((lane >= half) & (lane < ROPE_DIM))[None, :]).astype(np.float32)
    ones = (lane >= ROPE_DIM).astype(np.float32)[None, :]

    def expand_body(cos_ref, sin_ref, ec_ref, e1_ref, e2_ref, ones_ref, c_ref, s1_ref, s2_ref):
        hp = lax.Precision.HIGHEST
        c_ref[...] = jnp.dot(cos_ref[...], ec_ref[...], precision=hp, preferred_element_type=F32) + ones_ref[...]
        s1_ref[...] = jnp.dot(sin_ref[...], e1_ref[...], precision=hp, preferred_element_type=F32)
        s2_ref[...] = jnp.dot(sin_ref[...], e2_ref[...], precision=hp, preferred_element_type=F32)

    tab = jax.ShapeDtypeStruct((t, 128), F32)
    return pl.pallas_call(expand_body, name="rope_expand", out_shape=(tab, tab, tab), compiler_params=_cp())(
        cos8, sin8, jnp.asarray(e_cos), jnp.asarray(e_s1), jnp.asarray(e_s2), jnp.asarray(ones))


def _tile_heads(tab):
    return jnp.concatenate([tab] * (D_ATTN // 128), axis=1)


def _rope_apply(v, c, s1, s2):
    n = v.shape[1]
    half = ROPE_DIM // 2
    return v * c + pltpu.roll(v, n - half, 1) * s1 + pltpu.roll(v, half, 1) * s2


def _rope_apply_t(g, c, s1, s2):
    n = g.shape[1]
    half = ROPE_DIM // 2
    return g * c + pltpu.roll(g * s1, half, 1) + pltpu.roll(g * s2, n - half, 1)


LANE_CHUNKS = D_ATTN // 128
HEAD_LANES = 128 // N_HEADS


def _perm_shape(t, d, w, dtype):
    return jax.ShapeDtypeStruct((d, t // d, w), dtype)


def _perm_tile(d, w):
    return pl.BlockSpec((None if d == 1 else d, TM // d, w), lambda i: (0, i, 0))


def _to_planes(ref, scr, d, n_chunks, dtype):
    for r in range(d):
        for cc in range(n_chunks):
            ref[r, :, cc * 128:(cc + 1) * 128] = scr.at[cc][pl.ds(r, TM // d, stride=d), :].astype(dtype)


def _from_planes(ref, scr, d, n_chunks, accumulate=False):
    for r in range(d):
        for cc in range(n_chunks):
            rows = scr.at[cc]
            val = ref[r, :, cc * 128:(cc + 1) * 128].astype(F32)
            if accumulate:
                rows[pl.ds(r, TM // d, stride=d), :] += val
            else:
                rows[pl.ds(r, TM // d, stride=d), :] = val


def _chunks(val):
    return [val[:, cc * 128:(cc + 1) * 128] for cc in range(val.shape[1] // 128)]


def _unchunk(scr, n_chunks, base=0):
    return jnp.concatenate([scr[base + cc] for cc in range(n_chunks)], axis=1)


def _head_expand():
    src = np.arange(128)[:, None]
    dst = np.arange(D_ATTN)[None, :]
    return jnp.asarray((src == (dst // HEAD_DIM) * HEAD_LANES).astype(np.float32))


def _head_reduce():
    src = np.arange(D_ATTN)[:, None]
    dst = np.arange(128)[None, :]
    return jnp.asarray((src // HEAD_DIM == dst // HEAD_LANES).astype(np.float32))


def _dot_select(a, sel):
    hi = a.astype(BF16)
    lo = (a - hi.astype(F32)).astype(BF16)
    sel = sel.astype(BF16)
    return jnp.dot(hi, sel, preferred_element_type=F32) + jnp.dot(lo, sel, preferred_element_type=F32)


def _qkvuz(x, w_in, c_tab, s1_tab, s2_tab, ln_z_g, ln_z_b, w_s, b_full, dep):
    t = x.shape[0]
    nchunk = TM // BLK

    def body(x_ref, w_ref, c_ref, s1_ref, s2_ref, g_ref, b_ref, ws_ref, bf_ref, dep_ref,
             qkv1_ref, qkv4_ref, qkv16_ref, hu_ref, hz_ref, mixed_ref, gm_ref, xb_ref, h_scr, wm_scr, p_scr):
        @pl.when(pl.program_id(0) == 0)
        def _():
            row = lax.broadcasted_iota(jnp.int32, (BLK, BLK), 0)
            col = lax.broadcasted_iota(jnp.int32, (BLK, BLK), 1)
            for g in range(N_HEADS):
                wm_scr[g] = jnp.where(col <= row, ws_ref[g], 0.0).astype(MXU)

        xb = x_ref[...].astype(MXU)
        xb_ref[...] = xb
        for j in range(N_SHARD):
            h_scr[:, j * W_IN_BLK:(j + 1) * W_IN_BLK] = jnp.dot(xb, w_ref[j], preferred_element_type=F32)
        c, s1, s2 = _tile_heads(c_ref[...]), _tile_heads(s1_ref[...]), _tile_heads(s2_ref[...])
        q = _rope_apply(h_scr[:, 0:D_ATTN], c, s1, s2) * (1.0 / math.sqrt(HEAD_DIM))
        k = _rope_apply(h_scr[:, D_ATTN:2 * D_ATTN], c, s1, s2)
        for part, val in enumerate((q, k, h_scr[:, 2 * D_ATTN:3 * D_ATTN])):
            qkv1_ref[:, part * D_ATTN:(part + 1) * D_ATTN] = val.astype(MXU)
            for cc in range(LANE_CHUNKS):
                p_scr[part * LANE_CHUNKS + cc] = val[:, cc * 128:(cc + 1) * 128]
        _to_planes(qkv4_ref, p_scr, DILATIONS[1], 3 * LANE_CHUNKS, MXU)
        _to_planes(qkv16_ref, p_scr, DILATIONS[2], 3 * LANE_CHUNKS, MXU)
        hu = h_scr[:, 3 * D_ATTN:3 * D_ATTN + D_GMLP]
        hz = h_scr[:, 3 * D_ATTN + D_GMLP:]
        hu_ref[...] = hu
        hz_ref[...] = hz
        zhat, _ = _ln_fwd(_gelu(hz))
        zn = (zhat * g_ref[...] + b_ref[...]).astype(MXU)
        for ch in range(nchunk):
            rows = slice(ch * BLK, (ch + 1) * BLK)
            for g in range(N_HEADS):
                cols = slice(g * HEAD_DIM, (g + 1) * HEAD_DIM)
                mixed_ref[rows, cols] = jnp.dot(wm_scr[g], zn[rows, cols], preferred_element_type=F32) + bf_ref[:, cols]
        gm_ref[...] = (_gelu(hu) * mixed_ref[...]).astype(MXU)

    tok = lambda w: pl.BlockSpec((TM, w), lambda i: (i, 0))
    outs = [_perm_shape(t, d, 3 * D_ATTN, MXU) for d in DILATIONS] + [jax.ShapeDtypeStruct((t, D_GMLP), F32)] * 3 + [
        jax.ShapeDtypeStruct((t, D_GMLP), MXU), jax.ShapeDtypeStruct((t, D_MODEL), MXU)]
    return pl.pallas_call(
        body, name="qkvuz", grid=(t // TM,),
        in_specs=[tok(D_MODEL), _full(w_in.shape), tok(128), tok(128), tok(128), _full(ln_z_g.shape), _full(ln_z_b.shape),
                  _full(w_s.shape), _full(b_full.shape), pl.BlockSpec(memory_space=pl.ANY)],
        out_specs=[_perm_tile(d, 3 * D_ATTN) for d in DILATIONS] + [tok(D_ATTN)] * 4 + [tok(D_MODEL)], out_shape=outs,
        scratch_shapes=[pltpu.VMEM((TM, D_IN), F32), pltpu.VMEM((N_HEADS, BLK, BLK), MXU),
                        pltpu.VMEM((3 * LANE_CHUNKS, TM, 128), F32)],
        compiler_params=_cp(dimension_semantics=("arbitrary",)),
    )(x, w_in, c_tab, s1_tab, s2_tab, ln_z_g, ln_z_b, w_s, b_full, dep)


def _band_valid(n):
    i = lax.broadcasted_iota(jnp.int32, (BLK, 2 * BLK), 0)
    j = lax.broadcasted_iota(jnp.int32, (BLK, 2 * BLK), 1)
    return (j >= i) & (j <= i + BLK) & ((j >= BLK) | (n > 0))


def _attn_fwd(qkv, d, dep):
    _, l_sub, _ = qkv.shape
    nb = l_sub // BLK

    def body(q_ref, kp_ref, kc_ref, vp_ref, vc_ref, dep_ref, o_ref, l_ref):
        valid = _band_valid(pl.program_id(1))
        kcat = jnp.concatenate([kp_ref[...], kc_ref[...]], axis=0)
        vcat = jnp.concatenate([vp_ref[...], vc_ref[...]], axis=0)
        for h in range(N_HEADS):
            cols = slice(h * HEAD_DIM, (h + 1) * HEAD_DIM)
            s = jnp.where(valid, _dot_nt(q_ref[:, cols], kcat[:, cols]), NEG_INF)
            m = jnp.max(s, axis=-1, keepdims=True)
            e = jnp.exp(s - m)
            den = jnp.sum(e, axis=-1, keepdims=True)
            o_ref[:, cols] = _dot(e, vcat[:, cols]) * (1.0 / den)
            l_ref[:, h * HEAD_LANES:(h + 1) * HEAD_LANES] = jnp.broadcast_to(m + jnp.log(den), (BLK, HEAD_LANES))

    def blk(w, col, prev=False):
        return pl.BlockSpec((None, BLK, w), lambda r, n: (r, jnp.maximum(n - 1, 0) if prev else n, col))

    return pl.pallas_call(
        body, name=f"attn_fwd_d{d}", grid=(d, nb),
        in_specs=[blk(D_ATTN, 0), blk(D_ATTN, 1, True), blk(D_ATTN, 1), blk(D_ATTN, 2, True), blk(D_ATTN, 2),
                  pl.BlockSpec(memory_space=pl.ANY)],
        out_specs=[blk(D_ATTN, 0), blk(128, 0)],
        out_shape=[jax.ShapeDtypeStruct((d, l_sub, D_ATTN), F32), jax.ShapeDtypeStruct((d, l_sub, 128), F32)],
        compiler_params=_cp(dimension_semantics=("arbitrary", "arbitrary")),
    )(qkv, qkv, qkv, qkv, qkv, dep)


def _attn_bwd(qkv, do, lse, delta, d, dep):
    _, l_sub, _ = qkv.shape
    nb = l_sub // BLK
    whole = l_sub <= 8 * BLK

    def shares(n, q_ref, kp_ref, kc_ref, vp_ref, vc_ref, do_ref, l_ref, dl_ref, dq_ref):
        valid = _band_valid(n)
        kcat = jnp.concatenate([kp_ref[...], kc_ref[...]], axis=0)
        vcat = jnp.concatenate([vp_ref[...], vc_ref[...]], axis=0)
        for h in range(N_HEADS):
            cols = slice(h * HEAD_DIM, (h + 1) * HEAD_DIM)
            stat = slice(h * HEAD_LANES, h * HEAD_LANES + 1)
            qh, doh = q_ref[:, cols], do_ref[:, cols]
            p = jnp.where(valid, jnp.exp(_dot_nt(qh, kcat[:, cols]) - l_ref[:, stat]), 0.0)
            ds = p * (_dot_nt(doh, vcat[:, cols]) - dl_ref[:, stat])
            dq_ref[:, cols] = _dot(ds, kcat[:, cols])
            yield cols, _dot_tn(ds, qh), _dot_tn(p, doh)

    def body_whole(*refs):
        dk_ref, dv_ref = refs[10:]
        n = pl.program_id(1)
        cur = pl.ds(pl.multiple_of(n * BLK, BLK), BLK)
        prev = pl.ds(pl.multiple_of(jnp.maximum(n - 1, 0) * BLK, BLK), BLK)
        for cols, dk2, dv2 in shares(n, *refs[:8], refs[9]):
            dk_ref[cur, cols] = dk2[BLK:]
            dv_ref[cur, cols] = dv2[BLK:]
            dk_ref[prev, cols] += dk2[0:BLK]
            dv_ref[prev, cols] += dv2[0:BLK]

    def body_carry(*refs):
        dk_ref, dv_ref, ck_scr, cv_scr = refs[10:]
        n = pl.program_id(1)

        @pl.when(n == 0)
        def _():
            ck_scr[...] = jnp.zeros_like(ck_scr)
            cv_scr[...] = jnp.zeros_like(cv_scr)

        @pl.when(n < nb)
        def _():
            for cols, dk2, dv2 in shares(n, *refs[:8], refs[9]):
                dk_ref[:, cols] = ck_scr[:, cols] + dk2[0:BLK]
                dv_ref[:, cols] = cv_scr[:, cols] + dv2[0:BLK]
                ck_scr[:, cols] = dk2[BLK:]
                cv_scr[:, cols] = dv2[BLK:]

        @pl.when(n == nb)
        def _():
            dk_ref[...] = ck_scr[...]
            dv_ref[...] = cv_scr[...]

    def blk(w, col, shift=0):
        return pl.BlockSpec((None, BLK, w), lambda r, n: (r, jnp.clip(n - shift, 0, nb - 1), col))

    if whole:
        dkv_spec = pl.BlockSpec((None, l_sub, D_ATTN), lambda r, n: (r, 0, 0))
        body, steps, scratch = body_whole, nb, []
    else:
        dkv_spec = blk(D_ATTN, 0, 1)
        body, steps, scratch = body_carry, nb + 1, [pltpu.VMEM((BLK, D_ATTN), F32)] * 2
    return pl.pallas_call(
        body, name=f"attn_bwd_d{d}", grid=(d, steps),
        in_specs=[blk(D_ATTN, 0), blk(D_ATTN, 1, 1), blk(D_ATTN, 1), blk(D_ATTN, 2, 1), blk(D_ATTN, 2),
                  blk(D_ATTN, 0), blk(128, 0), blk(128, 0), pl.BlockSpec(memory_space=pl.ANY)],
        out_specs=[blk(D_ATTN, 0), dkv_spec, dkv_spec],
        out_shape=[jax.ShapeDtypeStruct((d, l_sub, D_ATTN), F32)] * 3,
        scratch_shapes=scratch,
        compiler_params=_cp(dimension_semantics=("arbitrary", "arbitrary")),
    )(qkv, qkv, qkv, qkv, qkv, do, lse, delta, dep)


def _mix_ln1(os_, ls_, gm, x, w_o, ln1_g, ln1_b, dep):
    t = x.shape[0]
    expand = _head_expand()

    def body(o1, o4, o16, l1, l4, l16, gm_ref, x_ref, wo_ref, g_ref, b_ref, ex_ref, dep_ref,
             attn_ref, lse1_ref, lse4_ref, lse16_ref, cat_ref, xhat_ref, rstd_ref, x1b_ref, o_scr, l_scr):
        _from_planes(o4, o_scr, DILATIONS[1], LANE_CHUNKS)
        _from_planes(o16, o_scr.at[pl.ds(LANE_CHUNKS, LANE_CHUNKS)], DILATIONS[2], LANE_CHUNKS)
        _from_planes(l4, l_scr, DILATIONS[1], 1)
        _from_planes(l16, l_scr.at[pl.ds(1, 1)], DILATIONS[2], 1)
        la, lb, lc = l1[...], l_scr[0], l_scr[1]
        m = jnp.maximum(jnp.maximum(la, lb), lc)
        ea, eb, ec = jnp.exp(la - m), jnp.exp(lb - m), jnp.exp(lc - m)
        den = ea + eb + ec
        inv = 1.0 / den
        wide = lambda w: _dot_select(w, ex_ref[...])
        attn = (wide(ea * inv) * o1[...] + wide(eb * inv) * _unchunk(o_scr, LANE_CHUNKS)
                + wide(ec * inv) * _unchunk(o_scr, LANE_CHUNKS, LANE_CHUNKS))
        attn_ref[...] = attn
        lse = m + jnp.log(den)
        lse1_ref[...] = lse
        l_scr[2] = lse
        _to_planes(lse4_ref, l_scr.at[pl.ds(2, 1)], DILATIONS[1], 1, F32)
        _to_planes(lse16_ref, l_scr.at[pl.ds(2, 1)], DILATIONS[2], 1, F32)
        cat_ref[:, 0:D_ATTN] = attn.astype(MXU)
        cat_ref[:, D_ATTN:] = gm_ref[...]
        mix = jnp.dot(cat_ref[...], wo_ref[...], preferred_element_type=F32)
        xhat, rstd = _ln_fwd(ALPHA * x_ref[...] + mix)
        xhat_ref[...] = xhat
        rstd_ref[...] = rstd
        x1b_ref[...] = (xhat * g_ref[...] + b_ref[...]).astype(MXU)

    tok = lambda w: pl.BlockSpec((TM, w), lambda i: (i, 0))
    outs = [jax.ShapeDtypeStruct((t, D_ATTN), F32)] + [_perm_shape(t, d, 128, F32) for d in DILATIONS] + [
        jax.ShapeDtypeStruct((t, D_MODEL), MXU), jax.ShapeDtypeStruct((t, D_MODEL), F32), jax.ShapeDtypeStruct((t, 1), F32),
        jax.ShapeDtypeStruct((t, D_MODEL), MXU)]
    return pl.pallas_call(
        body, name="mix_ln1", grid=(t // TM,),
        in_specs=[_perm_tile(d, D_ATTN) for d in DILATIONS] + [_perm_tile(d, 128) for d in DILATIONS]
        + [tok(D_GMLP), tok(D_MODEL), _full(w_o.shape), _full(ln1_g.shape), _full(ln1_b.shape), _full(expand.shape),
           pl.BlockSpec(memory_space=pl.ANY)],
        out_specs=[tok(D_ATTN)] + [_perm_tile(d, 128) for d in DILATIONS] + [tok(D_MODEL), tok(D_MODEL), tok(1), tok(D_MODEL)],
        out_shape=outs,
        scratch_shapes=[pltpu.VMEM((2 * LANE_CHUNKS, TM, 128), F32), pltpu.VMEM((3, TM, 128), F32)],
        compiler_params=_cp(dimension_semantics=("arbitrary",)),
    )(*os_, *ls_, gm, x, w_o, ln1_g, ln1_b, expand, dep)


def _conv_fwd(a_ext, w_ref, b_ref, rows):
    back = [pltpu.roll(a_ext, s, 0)[HALO:HALO + rows] for s in (1, 2)]
    return b_ref[...] + w_ref[2:3, :] * a_ext[HALO:HALO + rows] + w_ref[1:2, :] * back[0] + w_ref[0:1, :] * back[1]


def _ffn_in(x1b, w_a, w_b, conv_w, conv_b):
    t = x1b.shape[0]
    hb = TM // HALO

    def body(x_ref, xh_ref, wa_ref, wb_ref, cw_ref, cb_ref, apre_ref, act_ref, gate_ref, f_ref):
        i = pl.program_id(1)
        a_pre = _dot_nt(x_ref[...], wa_ref[...])
        a_halo = jnp.where(i > 0, _dot_nt(xh_ref[...], wa_ref[...]), 0.0)
        a = _conv_fwd(jnp.concatenate([a_halo, a_pre], axis=0), cw_ref, cb_ref, TM)
        b = _dot_nt(x_ref[...], wb_ref[...])
        cdf = 0.5 * (1.0 + lax.erf(a * (1.0 / math.sqrt(2.0))))
        pdf = jnp.exp(-0.5 * a * a) * (1.0 / math.sqrt(2.0 * math.pi))
        act = a * cdf
        apre_ref[...] = a_pre
        act_ref[...] = act
        gate_ref[...] = b * (cdf + a * pdf)
        f_ref[...] = (act * b).astype(MXU)

    blk = lambda r, c: pl.BlockSpec((None, r, c), lambda j, i: (j, 0, 0))
    tokj = pl.BlockSpec((None, TM, FF_BLK), lambda j, i: (j, i, 0))
    outs = [jax.ShapeDtypeStruct((N_SHARD, t, FF_BLK), F32)] * 3 + [jax.ShapeDtypeStruct((N_SHARD, t, FF_BLK), MXU)]
    return pl.pallas_call(
        body, name="ffn_in", grid=(N_SHARD, t // TM),
        in_specs=[pl.BlockSpec((TM, D_MODEL), lambda j, i: (i, 0)),
                  pl.BlockSpec((HALO, D_MODEL), lambda j, i: (jnp.maximum(i * hb - 1, 0), 0)),
                  blk(FF_BLK, D_MODEL), blk(FF_BLK, D_MODEL), blk(3, FF_BLK), blk(1, FF_BLK)],
        out_specs=[tokj, tokj, tokj, tokj], out_shape=outs,
        compiler_params=_cp(dimension_semantics=("arbitrary", "arbitrary")),
    )(x1b, x1b, w_a, w_b, conv_w, conv_b)


def _ffn_out_ln2(f, w_down, xhat1, ln1_g, ln1_b, ln2_g, ln2_b):
    t = xhat1.shape[0]

    def body(f_ref, wd_ref, xh_ref, g1_ref, b1_ref, g2_ref, b2_ref, xhat_ref, rstd_ref, x2b_ref):
        half = TM // ROW_GROUPS
        for r0 in range(0, TM, half):
            rows = pl.ds(r0, half)
            ff = jnp.dot(f_ref[0, rows, :], wd_ref[0], preferred_element_type=F32)
            for j in range(1, N_SHARD):
                ff = ff + jnp.dot(f_ref[j, rows, :], wd_ref[j], preferred_element_type=F32)
            x1 = xh_ref[rows, :] * g1_ref[...] + b1_ref[...]
            xhat, rstd = _ln_fwd(ALPHA * x1 + ff)
            xhat_ref[rows, :] = xhat
            rstd_ref[rows, :] = rstd
            x2b_ref[rows, :] = (xhat * g2_ref[...] + b2_ref[...]).astype(MXU)

    tok = lambda w: pl.BlockSpec((TM, w), lambda i: (i, 0))
    vec = _full((1, D_MODEL))
    outs = [jax.ShapeDtypeStruct((t, D_MODEL), F32), jax.ShapeDtypeStruct((t, 1), F32), jax.ShapeDtypeStruct((t, D_MODEL), MXU)]
    return pl.pallas_call(
        body, name="ffn_out_ln2", grid=(t // TM,),
        in_specs=[pl.BlockSpec((N_SHARD, TM, FF_BLK), lambda i: (0, i, 0)), _full(w_down.shape), tok(D_MODEL), vec, vec, vec, vec],
        out_specs=[tok(D_MODEL), tok(1), tok(D_MODEL)], out_shape=outs,
        compiler_params=_cp(dimension_semantics=("arbitrary",)),
    )(f, w_down, xhat1, ln1_g, ln1_b, ln2_g, ln2_b)


STAT_ROWS = 8


def _ple_loss_bwd(xhat2, rstd2, p, target, ln2_g, ln2_b, w_g, b_g, w_p, ln3_g, ln3_b):
    t = xhat2.shape[0]

    def body(xh2_ref, rs2_ref, p_ref, t_ref, g2_ref, b2_ref, wg_ref, bg_ref, wp_ref, g3_ref, b3_ref,
             dr2_ref, dr2b_ref, dgp_ref, dpp_ref, stat_ref, pp_scr):
        @pl.when(pl.program_id(0) == 0)
        def _():
            stat_ref[...] = jnp.zeros_like(stat_ref)

        xhat2 = xh2_ref[...]
        x2 = xhat2 * g2_ref[...] + b2_ref[...]
        gate = jax.nn.sigmoid(jnp.dot(x2.astype(MXU), wg_ref[...], preferred_element_type=F32) + bg_ref[...])
        pb = p_ref[...].astype(MXU)
        for j in range(N_SHARD):
            pp_scr[:, j * ROW_BLK:(j + 1) * ROW_BLK] = jnp.dot(pb, wp_ref[j], preferred_element_type=F32)
        pp = pp_scr[...]
        xhat3, rstd3 = _ln_fwd(ALPHA * x2 + gate * pp)
        err = xhat3 * g3_ref[...] + b3_ref[...] - t_ref[...]
        dy = err * (1.0 / D_MODEL)
        dr3 = _ln_bwd(dy, xhat3, rstd3, g3_ref[...])
        dgp = dr3 * pp * gate * (1.0 - gate)
        dgp_ref[...] = dgp.astype(MXU)
        dpp_ref[...] = (dr3 * gate).astype(MXU)
        dx2 = ALPHA * dr3 + _dot_nt(dgp, wg_ref[...])
        dr2 = _ln_bwd(dx2, xhat2, rs2_ref[...], g2_ref[...])
        dr2_ref[...] = dr2
        dr2b_ref[...] = dr2.astype(MXU)
        stat_ref[0:1, :] += _colsum(dy * xhat3)
        stat_ref[1:2, :] += _colsum(dy)
        stat_ref[2:3, :] += _colsum(dgp)
        stat_ref[3:4, :] += _colsum(dx2 * xhat2)
        stat_ref[4:5, :] += _colsum(dx2)
        stat_ref[5:6, :] += _colsum(err * err)

    tok = lambda w: pl.BlockSpec((TM, w), lambda i: (i, 0))
    vec = _full((1, D_MODEL))
    outs = [jax.ShapeDtypeStruct((t, D_MODEL), F32)] + [jax.ShapeDtypeStruct((t, D_MODEL), MXU)] * 3 + [
        jax.ShapeDtypeStruct((STAT_ROWS, D_MODEL), F32)]
    return pl.pallas_call(
        body, name="ple_loss_bwd", grid=(t // TM,),
        in_specs=[tok(D_MODEL), tok(1), tok(D_PLE), tok(D_MODEL), vec, vec, _full(w_g.shape), vec, _full(w_p.shape), vec, vec],
        out_specs=[tok(D_MODEL)] * 4 + [_full((STAT_ROWS, D_MODEL))], out_shape=outs,
        scratch_shapes=[pltpu.VMEM((TM, D_MODEL), F32)],
        compiler_params=_cp(dimension_semantics=("arbitrary",)),
    )(xhat2, rstd2, p, target, ln2_g, ln2_b, w_g, b_g, w_p, ln3_g, ln3_b)


def _ffn_bwd(dr2, dr2b, a_pre, act, gate, w_down, w_a, w_b, conv_w, xhat1, rstd1, ln1_g):
    t = dr2.shape[0]
    nt = t // TM
    hb = TM // HALO
    last_h = t // HALO - 1
    halo2 = 2 * HALO

    def body(dr_ref, drb_ref, drbn_ref, ap_ref, act_ref, gate_ref, gaten_ref, wd_ref, wa_ref, wb_ref, cw_ref,
             xh_ref, rs_ref, g1_ref, dap_ref, dbb_ref, dr1_ref, cstat_ref, lstat_ref, acc_scr):
        i, j = pl.program_id(0), pl.program_id(1)

        @pl.when((i == 0) & (j == 0))
        def _():
            cstat_ref[...] = jnp.zeros_like(cstat_ref)
            lstat_ref[...] = jnp.zeros_like(lstat_ref)

        half = TM // ROW_GROUPS
        parts = []
        for r0 in range(0, TM, half):
            rows = pl.ds(r0, half)
            last = r0 + half == TM

            def ext(ref, nxt):
                return jnp.concatenate([ref[rows], nxt[...]], axis=0) if last else ref[r0:r0 + half + HALO]

            drb = jnp.concatenate([drb_ref[rows, :], drbn_ref[...]], axis=0) if last else drb_ref[r0:r0 + half + halo2, :]
            df = _dot_nt(drb, wd_ref[...])[0:half + HALO]
            da = df * ext(gate_ref, gaten_ref)
            if last:
                da = jnp.concatenate([da[0:half], jnp.where(i < nt - 1, da[half:], 0.0)], axis=0)
            ahead = [da[0:half]] + [pltpu.roll(da, half + HALO - s, 0)[0:half] for s in (1, 2)]
            da_pre = cw_ref[2:3, :] * ahead[0] + cw_ref[1:2, :] * ahead[1] + cw_ref[0:1, :] * ahead[2]
            dbb = df[0:half] * act_ref[rows, :]
            dap_ref[rows, :] = da_pre.astype(MXU)
            dbb_ref[rows, :] = dbb.astype(MXU)
            for kk in range(3):
                cstat_ref[j, kk:kk + 1, :] += _colsum(ahead[2 - kk] * ap_ref[rows, :])
            cstat_ref[j, 3:4, :] += _colsum(ahead[0])
            parts.append(_dot(da_pre, wa_ref[...]) + _dot(dbb, wb_ref[...]))
        part = jnp.concatenate(parts, axis=0)

        @pl.when(j == 0)
        def _():
            acc_scr[...] = ALPHA * dr_ref[...] + part

        @pl.when(j > 0)
        def _():
            acc_scr[...] += part

        @pl.when(j == N_SHARD - 1)
        def _():
            dx1 = acc_scr[...]
            xhat1 = xh_ref[...]
            lstat_ref[0:1, :] += _colsum(dx1 * xhat1)
            lstat_ref[1:2, :] += _colsum(dx1)
            dr1_ref[...] = _ln_bwd(dx1, xhat1, rs_ref[...], g1_ref[...])

    tok = lambda w: pl.BlockSpec((TM, w), lambda i, j: (i, 0))
    tokj = pl.BlockSpec((None, TM, FF_BLK), lambda i, j: (j, i, 0))
    nextj = pl.BlockSpec((None, HALO, FF_BLK), lambda i, j: (j, jnp.minimum((i + 1) * hb, last_h), 0))
    blk = lambda r, c: pl.BlockSpec((None, r, c), lambda i, j: (j, 0, 0))
    outs = [jax.ShapeDtypeStruct((N_SHARD, t, FF_BLK), MXU)] * 2 + [
        jax.ShapeDtypeStruct((t, D_MODEL), F32), jax.ShapeDtypeStruct((N_SHARD, STAT_ROWS, FF_BLK), F32),
        jax.ShapeDtypeStruct((STAT_ROWS, D_MODEL), F32)]
    return pl.pallas_call(
        body, name="ffn_bwd", grid=(nt, N_SHARD),
        in_specs=[tok(D_MODEL), tok(D_MODEL),
                  pl.BlockSpec((halo2, D_MODEL), lambda i, j: (jnp.minimum((i + 1) * (hb // 2), last_h // 2), 0)),
                  tokj, tokj, tokj, nextj, blk(FF_BLK, D_MODEL), blk(FF_BLK, D_MODEL), blk(FF_BLK, D_MODEL),
                  blk(3, FF_BLK), tok(D_MODEL), tok(1), _full((1, D_MODEL))],
        out_specs=[tokj, tokj, tok(D_MODEL), _full((N_SHARD, STAT_ROWS, FF_BLK)), _full((STAT_ROWS, D_MODEL))], out_shape=outs,
        scratch_shapes=[pltpu.VMEM((TM, D_MODEL), F32)],
        compiler_params=_cp(dimension_semantics=("arbitrary", "arbitrary")),
    )(dr2, dr2b, dr2b, a_pre, act, gate, gate, w_down, w_a, w_b, conv_w, xhat1, rstd1, ln1_g)


def _mix_bwd(dr1, w_o, hu, hz, mixed, attn, ln_z_g, ln_z_b, w_s, dep):
    t = dr1.shape[0]
    nchunk = TM // BLK

    def body(dr_ref, wo_ref, hu_ref, hz_ref, mx_ref, attn_ref, g_ref, b_ref, ws_ref, grp_ref, red_ref, dep_ref,
             do1_ref, do4_ref, do16_ref, dl1_ref, dl4_ref, dl16_ref, duz_ref, dws_ref, dbs_ref, zstat_ref,
             wm_scr, dzn_scr, dbsum_scr, do_scr, dl_scr):
        @pl.when(pl.program_id(0) == 0)
        def _():
            row = lax.broadcasted_iota(jnp.int32, (BLK, BLK), 0)
            col = lax.broadcasted_iota(jnp.int32, (BLK, BLK), 1)
            for g in range(N_HEADS):
                wm_scr[g] = jnp.where(col <= row, ws_ref[g], 0.0).astype(MXU)
            dws_ref[...] = jnp.zeros_like(dws_ref)
            dbsum_scr[...] = jnp.zeros_like(dbsum_scr)
            zstat_ref[...] = jnp.zeros_like(zstat_ref)

        dcat = _dot_nt(dr_ref[...], wo_ref[...])
        dattn = dcat[:, 0:D_ATTN]
        do1_ref[...] = dattn.astype(MXU)
        for cc, val in enumerate(_chunks(dattn)):
            do_scr[cc] = val
        _to_planes(do4_ref, do_scr, DILATIONS[1], LANE_CHUNKS, MXU)
        _to_planes(do16_ref, do_scr, DILATIONS[2], LANE_CHUNKS, MXU)
        delta = _dot_select(dattn * attn_ref[...], red_ref[...])
        dl1_ref[...] = delta
        dl_scr[0] = delta
        _to_planes(dl4_ref, dl_scr, DILATIONS[1], 1, F32)
        _to_planes(dl16_ref, dl_scr, DILATIONS[2], 1, F32)
        dgm = dcat[:, D_ATTN:]
        hu, hz = hu_ref[...], hz_ref[...]
        u = _gelu(hu)
        duz_ref[:, 0:D_GMLP] = (dgm * mx_ref[...] * _gelu_grad(hu)).astype(MXU)
        dmixed = dgm * u
        dmb = dmixed.astype(MXU)
        zhat, rstd = _ln_fwd(_gelu(hz))
        znb = (zhat * g_ref[...] + b_ref[...]).astype(MXU)
        dbs_acc = jnp.zeros((BLK, D_GMLP), F32)
        for ch in range(nchunk):
            rows = slice(ch * BLK, (ch + 1) * BLK)
            dbs_acc = dbs_acc + dmixed[rows]
            for g in range(N_HEADS):
                cols = slice(g * HEAD_DIM, (g + 1) * HEAD_DIM)
                dzn_scr[rows, cols] = _dot_tn(wm_scr[g], dmb[rows, cols])
                dws_ref[g] += _dot_nt(dmb[rows, cols], znb[rows, cols])
        dbsum_scr[...] += dbs_acc
        dzn = dzn_scr[...]
        zstat_ref[0:1, :] += _colsum(dzn * zhat)
        zstat_ref[1:2, :] += _colsum(dzn)
        duz_ref[:, D_GMLP:] = (_ln_bwd(dzn, zhat, rstd, g_ref[...]) * _gelu_grad(hz)).astype(MXU)

        @pl.when(pl.program_id(0) == nt - 1)
        def _():
            row = lax.broadcasted_iota(jnp.int32, (BLK, BLK), 0)
            col = lax.broadcasted_iota(jnp.int32, (BLK, BLK), 1)
            for g in range(N_HEADS):
                dws_ref[g] = jnp.where(col <= row, dws_ref[g], 0.0)
            dbs_ref[...] = lax.dot_general(grp_ref[...], dbsum_scr[...], (((1,), (1,)), ((), ())),
                                           precision=lax.Precision.HIGHEST, preferred_element_type=F32)

    nt = t // TM
    tok = lambda w: pl.BlockSpec((TM, w), lambda i: (i, 0))
    grp = jnp.asarray((np.arange(D_GMLP)[None, :] // HEAD_DIM == np.arange(N_HEADS)[:, None]).astype(np.float32))
    red = _head_reduce()
    outs = [_perm_shape(t, d, D_ATTN, MXU) for d in DILATIONS] + [_perm_shape(t, d, 128, F32) for d in DILATIONS] + [
        jax.ShapeDtypeStruct((t, 2 * D_GMLP), MXU),
        jax.ShapeDtypeStruct((N_HEADS, BLK, BLK), F32), jax.ShapeDtypeStruct((N_HEADS, BLK), F32),
        jax.ShapeDtypeStruct((STAT_ROWS, D_GMLP), F32)]
    return pl.pallas_call(
        body, name="mix_bwd", grid=(t // TM,),
        in_specs=[tok(D_MODEL), _full(w_o.shape), tok(D_GMLP), tok(D_GMLP), tok(D_GMLP), tok(D_ATTN), _full(ln_z_g.shape),
                  _full(ln_z_b.shape), _full(w_s.shape), _full(grp.shape), _full(red.shape), pl.BlockSpec(memory_space=pl.ANY)],
        out_specs=[_perm_tile(d, D_ATTN) for d in DILATIONS] + [_perm_tile(d, 128) for d in DILATIONS]
        + [tok(2 * D_GMLP), _full((N_HEADS, BLK, BLK)), _full((N_HEADS, BLK)), _full((STAT_ROWS, D_GMLP))],
        out_shape=outs,
        scratch_shapes=[pltpu.VMEM((N_HEADS, BLK, BLK), MXU), pltpu.VMEM((TM, D_GMLP), F32), pltpu.VMEM((BLK, D_GMLP), F32),
                        pltpu.VMEM((LANE_CHUNKS, TM, 128), F32), pltpu.VMEM((1, TM, 128), F32)],
        compiler_params=_cp(dimension_semantics=("arbitrary",)),
    )(dr1, w_o, hu, hz, mixed, attn, ln_z_g, ln_z_b, w_s, grp, red, dep)


def _dx_in(dqs, dks, dvs, duz, dr1, w_in, c_tab, s1_tab, s2_tab):
    t = dr1.shape[0]

    def body(dq1, dq4, dq16, dk1, dk4, dk16, dv1, dv4, dv16, duz_ref, dr_ref, w_ref, c_ref, s1_ref, s2_ref,
             dh_ref, dx_ref, acc_scr):
        sums = []
        for part, (g1, g4, g16) in enumerate(((dq1, dq4, dq16), (dk1, dk4, dk16), (dv1, dv4, dv16))):
            acc = acc_scr.at[pl.ds(part * LANE_CHUNKS, LANE_CHUNKS)]
            for cc in range(LANE_CHUNKS):
                acc[cc] = g1[:, cc * 128:(cc + 1) * 128]
            _from_planes(g4, acc, DILATIONS[1], LANE_CHUNKS, accumulate=True)
            _from_planes(g16, acc, DILATIONS[2], LANE_CHUNKS, accumulate=True)
            sums.append(_unchunk(acc_scr, LANE_CHUNKS, part * LANE_CHUNKS))
        c, s1, s2 = _tile_heads(c_ref[...]), _tile_heads(s1_ref[...]), _tile_heads(s2_ref[...])
        dh_ref[:, 0:D_ATTN] = _rope_apply_t(sums[0] * (1.0 / math.sqrt(HEAD_DIM)), c, s1, s2).astype(MXU)
        dh_ref[:, D_ATTN:2 * D_ATTN] = _rope_apply_t(sums[1], c, s1, s2).astype(MXU)
        dh_ref[:, 2 * D_ATTN:3 * D_ATTN] = sums[2].astype(MXU)
        dh_ref[:, 3 * D_ATTN:] = duz_ref[...]
        dx = ALPHA * dr_ref[...]
        for j in range(N_SHARD):
            dx = dx + _dot_nt(dh_ref[:, j * W_IN_BLK:(j + 1) * W_IN_BLK], w_ref[j])
        dx_ref[...] = dx

    tok = lambda w: pl.BlockSpec((TM, w), lambda i: (i, 0))
    outs = [jax.ShapeDtypeStruct((t, D_IN), MXU), jax.ShapeDtypeStruct((t, D_MODEL), F32)]
    return pl.pallas_call(
        body, name="dx_in", grid=(t // TM,),
        in_specs=[_perm_tile(d, D_ATTN) for d in DILATIONS] * 3
        + [tok(2 * D_GMLP), tok(D_MODEL), _full(w_in.shape), tok(128), tok(128), tok(128)],
        out_specs=[tok(D_IN), tok(D_MODEL)], out_shape=outs,
        scratch_shapes=[pltpu.VMEM((3 * LANE_CHUNKS, TM, 128), F32)],
        compiler_params=_cp(dimension_semantics=("arbitrary",)),
    )(*dqs, *dks, *dvs, duz, dr1, w_in, c_tab, s1_tab, s2_tab)


def _wgrad(name, x, dy, x_spec, dy_spec, out_spec, out_shape, grid, dep=None):
    deps = [] if dep is None else [dep]

    def body(x_ref, dy_ref, *rest):
        rest[-1][...] = _dot_tn(x_ref[...], dy_ref[...])

    return pl.pallas_call(
        body, name=name, grid=grid, in_specs=[x_spec, dy_spec] + [pl.BlockSpec(memory_space=pl.ANY)] * len(deps),
        out_specs=out_spec, out_shape=jax.ShapeDtypeStruct(out_shape, F32),
        compiler_params=_cp(dimension_semantics=("arbitrary",) * len(grid)),
    )(x, dy, *deps)


def _wgrad_pair(name, xa, xb, dy, x_spec, dy_spec, out_spec, out_shape, grid):
    def body(xa_ref, xb_ref, dy_ref, oa_ref, ob_ref):
        dy = dy_ref[...]
        oa_ref[...] = _dot_tn(xa_ref[...], dy)
        ob_ref[...] = _dot_tn(xb_ref[...], dy)

    return pl.pallas_call(
        body, name=name, grid=grid, in_specs=[x_spec, x_spec, dy_spec], out_specs=[out_spec, out_spec],
        out_shape=[jax.ShapeDtypeStruct(out_shape, F32)] * 2,
        compiler_params=_cp(dimension_semantics=("arbitrary",) * len(grid)),
    )(xa, xb, dy)


def _local_step(x, p, rope, target, w_in, start_dep, late_landed, late_weights, early_grads, early_grads_sent,
                early_grads_landed,
                ln_z_g, ln_z_b, w_s, b_s, ln1_g, ln1_b, conv_b, ln2_g, ln2_b, b_g, ln3_g, ln3_b):
    t = x.shape[0]
    half = TM
    c_tab, s1_tab, s2_tab = rope
    b_full = jnp.repeat(jnp.transpose(b_s[0]), HEAD_DIM, axis=1)
    conv_b4 = conv_b.reshape(N_SHARD, 1, FF_BLK)
    *qkvs, hu, hz, mixed, gm, xb = _qkvuz(x, w_in, c_tab, s1_tab, s2_tab, ln_z_g, ln_z_b, w_s[0], b_full, start_dep)
    branches = [_attn_fwd(qkv, d, start_dep) for qkv, d in zip(qkvs[:2], DILATIONS[:2])]
    dep = late_landed(branches[-1][1])
    branches.append(_attn_fwd(qkvs[2], DILATIONS[2], dep))
    w_o, w_a, w_b, conv_w, w_down, w_g, w_p = late_weights(branches[-1][1])
    attn, *lses, cat, xhat1, rstd1, x1b = _mix_ln1(
        [o for o, _ in branches], [l for _, l in branches], gm, x, w_o, ln1_g, ln1_b, dep)
    a_pre, act, gate, f = _ffn_in(x1b, w_a, w_b, conv_w, conv_b4)
    xhat2, rstd2, x2b = _ffn_out_ln2(f, w_down, xhat1, ln1_g, ln1_b, ln2_g, ln2_b)
    dr2, dr2b, dgp, dpp, stat3 = _ple_loss_bwd(xhat2, rstd2, p, target, ln2_g, ln2_b, w_g, b_g, w_p, ln3_g, ln3_b)
    da_pre, dbb, dr1, cstat, stat1 = _ffn_bwd(dr2, dr2b, a_pre, act, gate, w_down, w_a, w_b, conv_w, xhat1, rstd1, ln1_g)

    full_t = lambda w, im: pl.BlockSpec((t, w), im)
    ffj = pl.BlockSpec((None, t, FF_BLK), lambda j, kk: (j, 0, 0))
    early = dict(
        w_ple_gate=_wgrad("dw_g", x2b, dgp, full_t(half, lambda kk, n: (0, kk)), full_t(half, lambda kk, n: (0, n)),
                          pl.BlockSpec((half, half), lambda kk, n: (kk, n)), (D_MODEL, D_MODEL), (2, 2)),
        w_ple_in=_wgrad("dw_p", p, dpp, full_t(D_PLE, lambda j: (0, 0)), full_t(ROW_BLK, lambda j: (0, j)),
                        pl.BlockSpec((None, D_PLE, ROW_BLK), lambda j: (j, 0, 0)), (N_SHARD, D_PLE, ROW_BLK), (N_SHARD,)),
        w_ff_down=_wgrad("dw_down", f, dr2b, ffj, full_t(half, lambda j, n: (0, n)),
                         pl.BlockSpec((None, FF_BLK, half), lambda j, n: (j, 0, n)), (N_SHARD, FF_BLK, D_MODEL), (N_SHARD, 2)),
        **dict(zip(("w_ff_a", "w_ff_b"), _wgrad_pair(
            "dw_ab", da_pre, dbb, x1b, ffj, full_t(half, lambda j, n: (0, n)),
            pl.BlockSpec((None, FF_BLK, half), lambda j, n: (j, 0, n)), (N_SHARD, FF_BLK, D_MODEL), (N_SHARD, 2)))),
        w_o=_wgrad("dw_o", cat, dr1, full_t(half, lambda kk, n: (0, kk)), full_t(half, lambda kk, n: (0, n)),
                   pl.BlockSpec((half, half), lambda kk, n: (kk, n)), (D_MODEL, D_MODEL), (2, 2)))
    dep = early_grads(early)

    do1, do4, do16, dl1, dl4, dl16, duz, dws, dbs, zstat = _mix_bwd(
        dr1, w_o, hu, hz, mixed, attn, ln_z_g, ln_z_b, w_s[0], dep)
    dep = early_grads_sent(duz, (stat3, stat1, zstat, cstat, dws, dbs))
    dqkv = [_attn_bwd(qkv, do, lse, dl, d, dep)
            for qkv, do, lse, dl, d in zip(qkvs, (do1, do4, do16), lses, (dl1, dl4, dl16), DILATIONS)]
    dh, grad_x = _dx_in([g[0] for g in dqkv], [g[1] for g in dqkv], [g[2] for g in dqkv], duz, dr1, w_in,
                        c_tab, s1_tab, s2_tab)
    dep = early_grads_landed(grad_x)
    g_w_in = _wgrad("dw_in", xb, dh, full_t(half, lambda j, kk: (0, kk)), full_t(W_IN_BLK, lambda j, kk: (0, j)),
                    pl.BlockSpec((None, half, W_IN_BLK), lambda j, kk: (j, kk, 0)), (N_SHARD, D_MODEL, W_IN_BLK), (N_SHARD, 2),
                    dep)
    return grad_x, g_w_in


def _tile_rows(rows, mult, steps):
    if rows % mult:
        return rows
    return next(rows // k for k in range(steps, rows + 1) if rows % k == 0 and (rows // k) % mult == 0)


def _grid_spec(grid, in_specs, out_specs):
    return pltpu.PrefetchScalarGridSpec(num_scalar_prefetch=1, grid=grid, in_specs=in_specs, out_specs=out_specs)


def _on_own_steps(i, count, steps, work):
    if count == steps:
        work()
    else:
        pl.when(i < count)(work)


def _place_shards(name, ws, dtypes, place, dep):
    n = len(ws)
    tiles = [_tile_rows(w.shape[0], 16, 8) for w in ws]
    counts = [w.shape[0] // t for w, t in zip(ws, tiles)]
    steps = max(counts)

    def body(s_ref, *refs):
        i = pl.program_id(0)
        for a in range(n):
            def work(a=a):
                refs[n + 1 + a][...] = refs[a][...].astype(dtypes[a])
            _on_own_steps(i, counts[a], steps, work)

    def tile(a, lead):
        last = counts[a] - 1
        if lead:
            return pl.BlockSpec((None, tiles[a], ws[a].shape[1]), lambda i, s: (s[0], jnp.minimum(i, last), 0))
        return pl.BlockSpec((tiles[a], ws[a].shape[1]), lambda i, s: (jnp.minimum(i, last), 0))

    return pl.pallas_call(
        body, name=name,
        grid_spec=_grid_spec((steps,), [tile(a, False) for a in range(n)] + [pl.BlockSpec(memory_space=pl.ANY)],
                             [tile(a, True) for a in range(n)]),
        out_shape=[jax.ShapeDtypeStruct((N_SHARD, *w.shape), dt) for w, dt in zip(ws, dtypes)],
        compiler_params=_cp())(place, *ws, dep)


def _pair_sums(name, mines, gots, place):
    n = len(mines)
    tiles = [_tile_rows(g.shape[1], 16, 2) for g in gots]
    per_blk = [g.shape[1] // t for g, t in zip(gots, tiles)]
    counts = [N_SHARD * nh for nh in per_blk]
    steps = max(counts)

    def body(s_ref, *refs):
        i = pl.program_id(0)
        for a in range(n):
            def work(a=a):
                refs[2 * n + a][...] = (refs[a][...] + refs[n + a][...]).astype(BF16)
            _on_own_steps(i, counts[a], steps, work)

    def tile(a, mine):
        nh, last = per_blk[a], counts[a] - 1

        def index(i, s):
            g = jnp.minimum(i, last)
            return (g // nh, (s[1] * nh if mine else 0) + g % nh, 0)

        return pl.BlockSpec((None, tiles[a], gots[a].shape[2]), index)

    return pl.pallas_call(
        body, name=name,
        grid_spec=_grid_spec((steps,), [tile(a, True) for a in range(n)] + [tile(a, False) for a in range(n)],
                             [tile(a, False) for a in range(n)]),
        out_shape=[jax.ShapeDtypeStruct(g.shape, BF16) for g in gots], compiler_params=_cp())(place, *mines, *gots)


def _chip_sums(name, owns, landeds, place, dep):
    n = len(owns)
    tiles = [_tile_rows(o.shape[1], 16, 8) for o in owns]
    counts = [o.shape[1] // t for o, t in zip(owns, tiles)]
    steps = max(counts)

    def body(s_ref, *refs):
        i = pl.program_id(0)
        for a in range(n):
            def work(a=a):
                own, l1, l2, l3 = (refs[4 * a + k][...].astype(F32) for k in range(4))
                refs[4 * n + 1 + a][...] = ((own + l1) + l2) + l3
            _on_own_steps(i, counts[a], steps, work)

    def slot(a, d):
        last = counts[a] - 1
        return pl.BlockSpec((None, tiles[a], owns[a].shape[2]), lambda i, s: ((s[0] + d) % N_SHARD, jnp.minimum(i, last), 0))

    def out(a):
        nh, last = counts[a], counts[a] - 1
        return pl.BlockSpec((tiles[a], owns[a].shape[2]), lambda i, s: (s[1] * nh + jnp.minimum(i, last), 0))

    operands = [x for o, l in zip(owns, landeds) for x in (o, l, l, l)]
    return pl.pallas_call(
        body, name=name,
        grid_spec=_grid_spec((steps,), [slot(a, d) for a in range(n) for d in range(4)] + [pl.BlockSpec(memory_space=pl.ANY)],
                             [out(a) for a in range(n)]),
        out_shape=[jax.ShapeDtypeStruct((2 * o.shape[1], o.shape[2]), F32) for o in owns],
        compiler_params=_cp())(place, *operands, dep)


def _adamw_math(w, g, m, v):
    m = ADAM_B1 * m + (1.0 - ADAM_B1) * g
    v = ADAM_B2 * v + (1.0 - ADAM_B2) * (g * g)
    m_hat = m / (1.0 - ADAM_B1 ** ADAM_STEP)
    v_hat = v / (1.0 - ADAM_B2 ** ADAM_STEP)
    delta = -ADAM_LR * (m_hat / (jnp.sqrt(v_hat) + ADAM_EPS) + ADAM_WD * w)
    return delta, m, v


def _adamw_shards(name, ws, gs, ms, vs):
    n = len(ws)
    tiles = [_tile_rows(w.shape[1], 8, 8) for w in ws]
    counts = [w.shape[1] // t for w, t in zip(ws, tiles)]
    steps = max(counts)

    def body(*refs):
        i = pl.program_id(0)
        for a in range(n):
            def work(a=a):
                w_ref, g_ref, m_ref, v_ref = refs[4 * a:4 * a + 4]
                d_ref, nm_ref, nv_ref = refs[4 * n + 3 * a:4 * n + 3 * a + 3]
                d_ref[...], nm_ref[...], nv_ref[...] = _adamw_math(w_ref[...], g_ref[...], m_ref[...], v_ref[...])
            _on_own_steps(i, counts[a], steps, work)

    def tile(a, lead):
        last, c = counts[a] - 1, ws[a].shape[2]
        if lead:
            return pl.BlockSpec((None, tiles[a], c), lambda i: (0, jnp.minimum(i, last), 0))
        return pl.BlockSpec((tiles[a], c), lambda i: (jnp.minimum(i, last), 0))

    res = pl.pallas_call(
        body, name=name, grid=(steps,),
        in_specs=[tile(a, lead) for a in range(n) for lead in (True, False, True, True)],
        out_specs=[tile(a, True) for a in range(n) for _ in range(3)],
        out_shape=[jax.ShapeDtypeStruct(w.shape, F32) for w in ws for _ in range(3)],
        compiler_params=_cp())(*[x for quad in zip(ws, gs, ms, vs) for x in quad])
    return [tuple(res[3 * a:3 * a + 3]) for a in range(n)]


MESH = pl.DeviceIdType.MESH
ANY = pl.BlockSpec(memory_space=pl.ANY)


def _place():
    x, y, c = lax.axis_index("x"), lax.axis_index("y"), lax.axis_index("c")
    chips = [(1 - x, y), (x, 1 - y), (1 - x, 1 - y)]
    return x, y, c, 2 * x + y, chips


def _remote(src, dst, send_sem, recv_sem, dev):
    return pltpu.make_async_remote_copy(src_ref=src, dst_ref=dst, send_sem=send_sem, recv_sem=recv_sem,
                                        device_id=dev, device_id_type=MESH)


def _half(ref, hc, rows):
    return ref.at[pl.ds(hc * (rows // 2), rows // 2)]


def _sibling_join(blocks, tag):
    n = len(blocks)

    def body(*refs):
        outs = refs[n:2 * n]
        send, recv = refs[2 * n:]
        x, y, c, _, _ = _place()
        cps = []
        for a in range(n):
            h = blocks[a].shape[0] // 2
            mine = outs[a].at[pl.ds(c * h, h)]
            cp = _remote(mine, mine, send.at[a], recv.at[a], (x, y, 1 - c))
            cp.start()
            cps.append(cp)
        for a, cp in enumerate(cps):
            h = blocks[a].shape[0] // 2
            theirs = outs[a].at[pl.ds((1 - c) * h, h)]
            _remote(theirs, theirs, send.at[a], recv.at[a], (x, y, 1 - c)).wait_recv()
            cp.wait_send()

    sem = pltpu.SemaphoreType.DMA
    return pl.pallas_call(body, name=f"rs_sibling_join_{tag}", in_specs=[ANY] * n, out_specs=[ANY] * n,
                          out_shape=[jax.ShapeDtypeStruct(b_.shape, b_.dtype) for b_ in blocks],
                          input_output_aliases={a: a for a in range(n)},
                          scratch_shapes=[sem((n,)), sem((n,))])(*blocks)


def _join_start(blocks, after, tag):
    n = len(blocks)

    def body(*refs):
        ins = refs[:n]
        send, recv = refs[n + 1], refs[n + 2]
        token = refs[2 * n + 3]
        x, y, c, _, _ = _place()
        for a in range(n):
            h = blocks[a].shape[0] // 2
            mine = ins[a].at[pl.ds(c * h, h)]
            _remote(mine, mine, send.at[a], recv.at[a], (x, y, 1 - c)).start()
        token[...] = jnp.zeros_like(token)

    sems = pltpu.SemaphoreType.DMA((n,))
    res = pl.pallas_call(
        body, name=f"join_start_{tag}", in_specs=[HBM] * n + [ANY],
        out_specs=[SEM, SEM] + [HBM] * n + [pl.BlockSpec(memory_space=pltpu.VMEM)],
        out_shape=[sems, sems] + [pltpu.HBM(b_.shape, b_.dtype) for b_ in blocks] + [TOKEN],
        input_output_aliases={a: a + 2 for a in range(n)}, compiler_params=_in_flight_params(),
    )(*[_in_hbm(b_) for b_ in blocks], after)
    return res[0], res[1], res[2:2 + n], res[2 + n]


def _join_wait(send, recv, blocks, after, tag):
    n = len(blocks)

    def body(*refs):
        ins = refs[:n]
        send_ref, recv_ref = refs[n], refs[n + 1]
        x, y, c, _, _ = _place()
        for a in range(n):
            h = blocks[a].shape[0] // 2
            mine, theirs = ins[a].at[pl.ds(c * h, h)], ins[a].at[pl.ds((1 - c) * h, h)]
            _remote(mine, mine, send_ref.at[a], recv_ref.at[a], (x, y, 1 - c)).wait_send()
            _remote(theirs, theirs, send_ref.at[a], recv_ref.at[a], (x, y, 1 - c)).wait_recv()

    return pl.pallas_call(
        body, name=f"join_wait_{tag}", in_specs=[HBM] * n + [SEM, SEM, ANY], out_specs=[HBM] * n,
        out_shape=[pltpu.HBM(b_.shape, b_.dtype) for b_ in blocks],
        input_output_aliases={a: a for a in range(n)}, compiler_params=_in_flight_params(),
    )(*blocks, send, recv, after)


HBM = pl.BlockSpec(memory_space=pltpu.HBM)
SEM = pl.BlockSpec(memory_space=pltpu.SEMAPHORE)
TOKEN = jax.ShapeDtypeStruct((8, 128), F32)


def _in_flight_params():
    return pltpu.CompilerParams(has_side_effects=pltpu.SideEffectType.DATAFLOW_SIDE_EFFECTING)


def _in_hbm(a):
    return pltpu.with_memory_space_constraint(a, pltpu.HBM)


def _gather_piece(ref, rows, split, slot, hc):
    return _half(ref.at[slot], hc, rows) if split else ref.at[slot]


def _gather_start(stacks, split, after, tag):
    n = len(stacks)

    def body(*refs):
        ins = refs[:n]
        send, recv = refs[n + 1], refs[n + 2]
        token = refs[2 * n + 3]
        _, _, c, j, chips = _place()
        for a in range(n):
            mine = _gather_piece(ins[a], stacks[a].shape[1], split[a], j, c)
            for t in range(3):
                _remote(mine, mine, send.at[3 * a + t], recv.at[3 * a + t], (*chips[t], c)).start()
        token[...] = jnp.zeros_like(token)

    sems = pltpu.SemaphoreType.DMA((3 * n,))
    res = pl.pallas_call(
        body, name=f"gather_start_{tag}", in_specs=[HBM] * n + [ANY],
        out_specs=[SEM, SEM] + [HBM] * n + [pl.BlockSpec(memory_space=pltpu.VMEM)],
        out_shape=[sems, sems] + [pltpu.HBM(s.shape, s.dtype) for s in stacks] + [TOKEN],
        input_output_aliases={a: a + 2 for a in range(n)}, compiler_params=_in_flight_params(),
    )(*[_in_hbm(s) for s in stacks], after)
    return res[0], res[1], res[2:2 + n], res[2 + n]


def _gather_wait(send, recv, stacks, split, after, tag):
    n = len(stacks)

    def body(*refs):
        ins = refs[:n]
        send_ref, recv_ref = refs[n], refs[n + 1]
        _, _, c, j, chips = _place()
        for a in range(n):
            rows = stacks[a].shape[1]
            mine = _gather_piece(ins[a], rows, split[a], j, c)
            for t, (px, py) in enumerate(chips):
                theirs = _gather_piece(ins[a], rows, split[a], 2 * px + py, c)
                _remote(mine, mine, send_ref.at[3 * a + t], recv_ref.at[3 * a + t], (px, py, c)).wait_send()
                _remote(theirs, theirs, send_ref.at[3 * a + t], recv_ref.at[3 * a + t], (px, py, c)).wait_recv()

    return pl.pallas_call(
        body, name=f"gather_wait_{tag}", in_specs=[HBM] * n + [SEM, SEM, ANY], out_specs=[HBM] * n,
        out_shape=[pltpu.HBM(s.shape, s.dtype) for s in stacks],
        input_output_aliases={a: a for a in range(n)}, compiler_params=_in_flight_params(),
    )(*stacks, send, recv, after)


def _gather_forward(stacks, split, tag):
    idx = [a for a in range(len(stacks)) if split[a]]
    n = len(idx)

    def body(*refs):
        outs = refs[n:2 * n]
        send, recv = refs[2 * n:]
        x, y, c, _, chips = _place()
        sends = []
        for t, (px, py) in enumerate(chips):
            for a in range(n):
                blk = _half(outs[a].at[2 * px + py], c, stacks[idx[a]].shape[1])
                cp = _remote(blk, blk, send.at[a, t], recv.at[a, t], (x, y, 1 - c))
                cp.start()
                sends.append(cp)
        for t, (px, py) in enumerate(chips):
            for a in range(n):
                blk = _half(outs[a].at[2 * px + py], 1 - c, stacks[idx[a]].shape[1])
                _remote(blk, blk, send.at[a, t], recv.at[a, t], (x, y, 1 - c)).wait_recv()
        for cp in sends:
            cp.wait_send()

    sem = pltpu.SemaphoreType.DMA
    res = pl.pallas_call(
        body, name=f"gather_forward_{tag}", in_specs=[ANY] * n, out_specs=[ANY] * n,
        out_shape=[jax.ShapeDtypeStruct(stacks[a].shape, stacks[a].dtype) for a in idx],
        input_output_aliases={a: a for a in range(n)}, scratch_shapes=[sem((n, 3)), sem((n, 3))],
    )(*[stacks[a] for a in idx])
    out = list(stacks)
    for a, r in zip(idx, res):
        out[a] = r
    return out


def _forward_start(stacks, after, tag):
    n = len(stacks)

    def body(*refs):
        ins = refs[:n]
        send, recv = refs[n + 1], refs[n + 2]
        token = refs[2 * n + 3]
        x, y, c, _, chips = _place()
        for a in range(n):
            for t, (px, py) in enumerate(chips):
                blk = _half(ins[a].at[2 * px + py], c, stacks[a].shape[1])
                _remote(blk, blk, send.at[3 * a + t], recv.at[3 * a + t], (x, y, 1 - c)).start()
        token[...] = jnp.zeros_like(token)

    sems = pltpu.SemaphoreType.DMA((3 * n,))
    res = pl.pallas_call(
        body, name=f"forward_start_{tag}", in_specs=[HBM] * n + [ANY],
        out_specs=[SEM, SEM] + [HBM] * n + [pl.BlockSpec(memory_space=pltpu.VMEM)],
        out_shape=[sems, sems] + [pltpu.HBM(s.shape, s.dtype) for s in stacks] + [TOKEN],
        input_output_aliases={a: a + 2 for a in range(n)}, compiler_params=_in_flight_params(),
    )(*[_in_hbm(s) for s in stacks], after)
    return res[0], res[1], res[2:2 + n], res[2 + n]


def _forward_wait(send, recv, stacks, after, tag):
    n = len(stacks)

    def body(*refs):
        ins = refs[:n]
        send_ref, recv_ref = refs[n], refs[n + 1]
        x, y, c, _, chips = _place()
        for a in range(n):
            for t, (px, py) in enumerate(chips):
                mine = _half(ins[a].at[2 * px + py], c, stacks[a].shape[1])
                theirs = _half(ins[a].at[2 * px + py], 1 - c, stacks[a].shape[1])
                _remote(mine, mine, send_ref.at[3 * a + t], recv_ref.at[3 * a + t], (x, y, 1 - c)).wait_send()
                _remote(theirs, theirs, send_ref.at[3 * a + t], recv_ref.at[3 * a + t], (x, y, 1 - c)).wait_recv()

    return pl.pallas_call(
        body, name=f"forward_wait_{tag}", in_specs=[HBM] * n + [SEM, SEM, ANY], out_specs=[HBM] * n,
        out_shape=[pltpu.HBM(s.shape, s.dtype) for s in stacks],
        input_output_aliases={a: a for a in range(n)}, compiler_params=_in_flight_params(),
    )(*stacks, send, recv, after)


def _swap_start(grads, tag):
    n = len(grads)

    def body(*refs):
        ins, gots = refs[:n], refs[n:2 * n]
        send, recv = refs[2 * n], refs[2 * n + 1]
        token = refs[4 * n + 2]
        x, y, c, _, _ = _place()
        for a in range(n):
            h = grads[a].shape[1] // 2
            _remote(ins[a].at[:, pl.ds((1 - c) * h, h)], gots[a], send.at[a], recv.at[a], (x, y, 1 - c)).start()
        token[...] = jnp.zeros_like(token)

    sems = pltpu.SemaphoreType.DMA((n,))
    halves = [(g.shape[0], g.shape[1] // 2, g.shape[2]) for g in grads]
    res = pl.pallas_call(
        body, name=f"swap_start_{tag}", in_specs=[HBM] * (2 * n),
        out_specs=[SEM, SEM] + [HBM] * (2 * n) + [pl.BlockSpec(memory_space=pltpu.VMEM)],
        out_shape=[sems, sems] + [pltpu.HBM(g.shape, g.dtype) for g in grads] + [pltpu.HBM(s, F32) for s in halves] + [TOKEN],
        input_output_aliases={a: a + 2 for a in range(2 * n)}, compiler_params=_in_flight_params(),
    )(*[_in_hbm(g) for g in grads], *[_in_hbm(lax.empty(s, F32)) for s in halves])
    return res[0], res[1], res[2:2 + n], res[2 + n:2 + 2 * n], res[2 + 2 * n]


def _swap_wait(send, recv, grads, gots, after, tag):
    n = len(grads)

    def body(*refs):
        ins, lnd = refs[:n], refs[n:2 * n]
        send_ref, recv_ref = refs[2 * n], refs[2 * n + 1]
        x, y, c, _, _ = _place()
        for a in range(n):
            h = grads[a].shape[1] // 2
            cp = _remote(ins[a].at[:, pl.ds((1 - c) * h, h)], lnd[a], send_ref.at[a], recv_ref.at[a], (x, y, 1 - c))
            cp.wait_send()
            cp.wait_recv()

    bufs = [pltpu.HBM(g.shape, g.dtype) for g in grads] + [pltpu.HBM(g.shape, g.dtype) for g in gots]
    res = pl.pallas_call(
        body, name=f"swap_wait_{tag}", in_specs=[HBM] * (2 * n) + [SEM, SEM, ANY], out_specs=[HBM] * (2 * n),
        out_shape=bufs, input_output_aliases={a: a for a in range(2 * n)}, compiler_params=_in_flight_params(),
    )(*grads, *gots, send, recv, after)
    return res[:n], res[n:]


def _exchange_start(parts, tag):
    n = len(parts)

    def body(*refs):
        ins, lands = refs[:n], refs[n:2 * n]
        send, recv = refs[2 * n], refs[2 * n + 1]
        token = refs[4 * n + 2]
        _, _, c, j, chips = _place()
        for t, (px, py) in enumerate(chips):
            for a in range(n):
                _remote(ins[a].at[2 * px + py], lands[a].at[j], send.at[3 * a + t], recv.at[3 * a + t], (px, py, c)).start()
        token[...] = jnp.zeros_like(token)

    sems = pltpu.SemaphoreType.DMA((3 * n,))
    bufs = [pltpu.HBM(p.shape, p.dtype) for p in parts]
    res = pl.pallas_call(
        body, name=f"exchange_start_{tag}", in_specs=[HBM] * (2 * n),
        out_specs=[SEM, SEM] + [HBM] * (2 * n) + [pl.BlockSpec(memory_space=pltpu.VMEM)],
        out_shape=[sems, sems] + bufs + bufs + [TOKEN],
        input_output_aliases={a: a + 2 for a in range(2 * n)}, compiler_params=_in_flight_params(),
    )(*[_in_hbm(p) for p in parts], *[_in_hbm(lax.empty(p.shape, p.dtype)) for p in parts])
    return res[0], res[1], res[2:2 + n], res[2 + n:2 + 2 * n], res[2 + 2 * n]


def _exchange_wait(send, recv, parts, lands, after, tag):
    n = len(parts)

    def body(*refs):
        ins, lnd = refs[:n], refs[n:2 * n]
        send_ref, recv_ref = refs[2 * n], refs[2 * n + 1]
        _, _, c, j, chips = _place()
        for t, (px, py) in enumerate(chips):
            jt = 2 * px + py
            for a in range(n):
                _remote(ins[a].at[jt], lnd[a].at[j], send_ref.at[3 * a + t], recv_ref.at[3 * a + t], (px, py, c)).wait_send()
                _remote(ins[a].at[jt], lnd[a].at[jt], send_ref.at[3 * a + t], recv_ref.at[3 * a + t], (px, py, c)).wait_recv()

    bufs = [pltpu.HBM(p.shape, p.dtype) for p in parts]
    res = pl.pallas_call(
        body, name=f"exchange_wait_{tag}", in_specs=[HBM] * (2 * n) + [SEM, SEM, ANY], out_specs=[HBM] * (2 * n),
        out_shape=bufs + bufs, input_output_aliases={a: a for a in range(2 * n)}, compiler_params=_in_flight_params(),
    )(*parts, *lands, send, recv, after)
    return res[:n], res[n:]


def _small_chip_sums(arrs):
    n = len(arrs)

    def body(*refs):
        ins, outs = refs[:n], refs[n:2 * n]
        sib = refs[2 * n:3 * n]
        send, recv = refs[3 * n:]
        x, y, c, j, _ = _place()
        swaps = [_remote(ins[a], sib[a], send.at[a], recv.at[a], (x, y, 1 - c)) for a in range(n)]
        for cp in swaps:
            cp.start()
        for a in range(n):
            swaps[a].wait_recv()
            outs[a][j] = ins[a][...] + sib[a][...]
        for cp in swaps:
            cp.wait_send()

    sem = pltpu.SemaphoreType.DMA
    vm = pl.BlockSpec(memory_space=pltpu.VMEM)
    return pl.pallas_call(
        body, name="small_chip_sums", in_specs=[vm] * n, out_specs=[vm] * n,
        out_shape=[jax.ShapeDtypeStruct((N_SHARD, *a.shape), F32) for a in arrs],
        scratch_shapes=[pltpu.VMEM(a.shape, F32) for a in arrs] + [sem((n,)), sem((n,))],
        compiler_params=_cp(),
    )(*arrs)


def _small_totals(stacks):
    n = len(stacks)

    def body(*refs):
        for a in range(n):
            refs[n + a][...] = ((refs[a][0] + refs[a][1]) + refs[a][2]) + refs[a][3]

    return pl.pallas_call(body, name="small_totals", out_shape=[jax.ShapeDtypeStruct(s.shape[1:], F32) for s in stacks],
                          compiler_params=_cp())(*stacks)


SMALL_1024 = ("ln1_g", "ln1_b", "ln2_g", "ln2_b", "b_ple_gate", "ln3_g", "ln3_b")


def _adamw_small(red3, red1, redz, g_conv_w, redc, red_ws, red_bs, params):
    shape2d = {"ln_z_g": (1, D_GMLP), "ln_z_b": (1, D_GMLP), "w_s": (N_HEADS * BLK, BLK), "b_s": (N_HEADS, BLK),
               "conv_w": (3, FF_BLK), "conv_b": (N_SHARD, FF_BLK), **{k: (1, D_MODEL) for k in SMALL_1024}}
    names = list(shape2d)
    flat = [a.reshape(shape2d[k]) for k in names for a in params[k]]

    def body(r3, r1, rz, gcw, rc, rws, rbs, *refs):
        ins, outs = refs[:3 * len(names)], refs[3 * len(names):]

        def grad_of(k):
            if k == "w_s":
                return rws[...]
            if k == "b_s":
                return rbs[...]
            if k == "conv_w":
                return gcw[0:3, :]
            if k == "conv_b":
                return jnp.concatenate([rc[j * STAT_ROWS + 3:j * STAT_ROWS + 4, :] for j in range(N_SHARD)], axis=0)
            src, row = {"ln3_g": (r3, 0), "ln3_b": (r3, 1), "b_ple_gate": (r3, 2), "ln2_g": (r3, 3), "ln2_b": (r3, 4),
                        "ln1_g": (r1, 0), "ln1_b": (r1, 1), "ln_z_g": (rz, 0), "ln_z_b": (rz, 1)}[k]
            return src[row:row + 1, :]

        for i, k in enumerate(names):
            w_ref, m_ref, v_ref = ins[3 * i:3 * i + 3]
            g_ref, d_ref, nm_ref, nv_ref = outs[4 * i:4 * i + 4]
            g = grad_of(k)
            g_ref[...] = g
            d_ref[...], nm_ref[...], nv_ref[...] = _adamw_math(w_ref[...], g, m_ref[...], v_ref[...])

    res = pl.pallas_call(
        body, name="adamw_small",
        out_shape=[jax.ShapeDtypeStruct(shape2d[k], F32) for k in names for _ in range(4)],
        compiler_params=_cp(),
    )(red3, red1, redz, g_conv_w, redc, red_ws, red_bs, *flat)
    return {k: tuple(r.reshape(params[k][0].shape) for r in res[4 * i:4 * i + 4]) for i, k in enumerate(names)}


WEIGHTS = ("w_in", "ln_z_g", "ln_z_b", "w_s", "b_s", "w_o", "ln1_g", "ln1_b", "w_ff_a", "w_ff_b", "conv_w", "conv_b",
           "w_ff_down", "ln2_g", "ln2_b", "w_ple_gate", "b_ple_gate", "w_ple_in", "ln3_g", "ln3_b")
BIG = ("w_in", "w_o", "w_ff_a", "w_ff_b", "w_ff_down", "w_ple_gate", "w_ple_in")
TRANSPOSED = ("w_ff_a", "w_ff_b")
LATE = ("w_o", "w_ff_a", "w_ff_b", "w_ff_down", "w_ple_gate", "w_ple_in", "conv_w")


def kernel(x, p, positions, w_in, ln_z_g, ln_z_b, w_s, b_s, w_o, ln1_g, ln1_b, w_ff_a, w_ff_b, conv_w, conv_b, w_ff_down, ln2_g, ln2_b, w_ple_gate, b_ple_gate, w_ple_in, ln3_g, ln3_b, loss_target, m_w_in, m_ln_z_g, m_ln_z_b, m_w_s, m_b_s, m_w_o, m_ln1_g, m_ln1_b, m_w_ff_a, m_w_ff_b, m_conv_w, m_conv_b, m_w_ff_down, m_ln2_g, m_ln2_b, m_w_ple_gate, m_b_ple_gate, m_w_ple_in, m_ln3_g, m_ln3_b, v_w_in, v_ln_z_g, v_ln_z_b, v_w_s, v_b_s, v_w_o, v_ln1_g, v_ln1_b, v_w_ff_a, v_w_ff_b, v_conv_w, v_conv_b, v_w_ff_down, v_ln2_g, v_ln2_b, v_w_ple_gate, v_b_ple_gate, v_w_ple_in, v_ln3_g, v_ln3_b):
    args = locals()
    w = {k: args[k] for k in WEIGHTS}
    m = {k: args["m_" + k] for k in WEIGHTS}
    v = {k: args["v_" + k] for k in WEIGHTS}

    for k in TRANSPOSED:
        w[k], m[k], v[k] = (jnp.swapaxes(a, 1, 2) for a in (w[k], m[k], v[k]))

    chip = 2 * lax.axis_index("x") + lax.axis_index("y")
    place = jnp.stack([chip, lax.axis_index("c")]).astype(jnp.int32)
    stack = dict(zip(["w_in"], _place_shards("cast_w_in", [w["w_in"][0]], [MXU], place, place)))
    i_send, i_recv, in_flight, dep = _gather_start([stack["w_in"]], [True], place, "w_in")
    stack.update(zip(LATE, _place_shards("cast_late", [w[k][0] for k in LATE],
                                         [F32 if k == "conv_w" else MXU for k in LATE], place, dep)))
    split_late = [k != "conv_w" for k in LATE]
    g_send, g_recv, late_flight, start_dep = _gather_start([stack[k] for k in LATE], split_late, place, "late")
    rope = _rope_tables(positions, x.shape[1], start_dep)
    landed_in = _gather_wait(i_send, i_recv, in_flight, [True], rope[0], "w_in")
    w_in_full, = _gather_forward(landed_in, [True], "w_in")
    halves =[k for k, sp in zip(LATE, split_late) if sp]
    trips = {}

    def late_landed(after):
        fw = dict(zip(LATE, _gather_wait(g_send, g_recv, late_flight, split_late, after, "late")))
        trips["late"] = (fw, *_forward_start([fw[k] for k in halves], fw["conv_w"], "late"))
        return trips["late"][-1]

    def late_weights(after):
        fw, send, recv, flight, _ = trips["late"]
        fw.update(zip(halves, _forward_wait(send, recv, flight, after, "late")))
        return (fw["w_o"].reshape(D_MODEL, D_MODEL), fw["w_ff_a"], fw["w_ff_b"], fw["conv_w"], fw["w_ff_down"],
                fw["w_ple_gate"].reshape(D_MODEL, D_MODEL), fw["w_ple_in"])

    def swap_started(names, grads, tag):
        stacked = [g.reshape(N_SHARD, *w[k].shape[1:]) for k, g in zip(names, grads)]
        return (names, tag, *_swap_start(stacked, tag))

    def partial_sums(swap, after):
        names, tag, send, recv, stacked, gots, _ = swap
        stacked, got = _swap_wait(send, recv, stacked, gots, after, tag)
        pair = _pair_sums(f"rs_pair_{tag}", stacked, got, place)
        return (names, tag, *_exchange_start(pair, tag))

    def chip_summed(trip, after, dep):
        names, tag, send, recv, pair, lands, _ = trip
        pair, landed = _exchange_wait(send, recv, pair, lands, after, tag)
        return _chip_sums(f"rs_sum_{tag}", pair, landed, place, dep), names, tag

    def reduced(trip, after, dep):
        blocks, names, tag = chip_summed(trip, after, dep)
        return dict(zip(names, _sibling_join(blocks, tag)))

    def early_grads_landed(after):
        blocks, names, tag = chip_summed(trips["early"], after, trips["small"][-1])
        trips["join"] = (names, *_join_start(blocks, after, tag))
        return trips["join"][-1]

    def early_grads(grads):
        trips["swap"] = swap_started(list(grads), list(grads.values()), "early")
        return trips["swap"][-1]

    def early_grads_sent(after, small):
        trips["early"] = partial_sums(trips["swap"], after)
        stat3, stat1, zstat, cstat, dws, dbs = small
        sums = _small_chip_sums([stat3, stat1, zstat, cstat.reshape(N_SHARD * STAT_ROWS, FF_BLK),
                                 dws.reshape(N_HEADS * BLK, BLK), dbs])
        trips["small"] = _gather_start(sums, [False] * len(sums), trips["early"][-1], "small")
        return trips["small"][-1]

    grad_x, g_w_in = _local_step(
        x[0], p[0, 0], rope, loss_target[0], w_in_full, start_dep, late_landed, late_weights, early_grads, early_grads_sent,
        early_grads_landed, ln_z_g, ln_z_b, w_s, b_s, ln1_g, ln1_b, conv_b, ln2_g, ln2_b, b_ple_gate, ln3_g, ln3_b)

    trips["w_in"] = partial_sums(swap_started(["w_in"], [g_w_in], "w_in"), g_w_in)
    out = {}

    def adamw(red, tag):
        names = list(red)
        steps = _adamw_shards(f"adamw_{tag}", [w[k] for k in names], [red[k] for k in names], [m[k] for k in names],
                              [v[k] for k in names])
        for k, (d, nm, nv) in zip(names, steps):
            out[k] = (red[k].reshape(w[k].shape), d, nm, nv)

    names, j_send, j_recv, j_flight, _ = trips["join"]
    adamw(dict(zip(names, _join_wait(j_send, j_recv, j_flight, trips["w_in"][-1], "early"))), "early")
    adamw(reduced(trips["w_in"], out["w_o"][3], start_dep), "w_in")
    for k in TRANSPOSED:
        out[k] = tuple(jnp.swapaxes(a, 1, 2) for a in out[k])

    s_send, s_recv, s_flight, _ = trips["small"]
    red3, red1, redz, redc, red_ws, red_bs = _small_totals(
        _gather_wait(s_send, s_recv, s_flight, [False] * len(s_flight), out["w_in"][3], "small"))
    loss = (0.5 / D_MODEL) * jnp.sum(red3[5])
    g_conv_w = lax.dynamic_slice_in_dim(redc, chip * STAT_ROWS, STAT_ROWS, 0)
    names_small = [k for k in WEIGHTS if k not in BIG]
    out.update(_adamw_small(red3, red1, redz, g_conv_w, redc, red_ws, red_bs, {k: (w[k], m[k], v[k]) for k in names_small}))

    return (loss, grad_x[None], *[out[k][0] for k in WEIGHTS], *[out[k][1] for k in WEIGHTS],
            *[out[k][2] for k in WEIGHTS], *[out[k][3] for k in WEIGHTS])
```

```python
import math

import numpy as np
import jax
import jax.numpy as jnp
from jax import lax
from jax.experimental import pallas as pl
from jax.experimental.pallas import tpu as pltpu

F32 = jnp.float32
BF16 = jnp.bfloat16
MXU = BF16

D_MODEL = 1024
HEAD_DIM = 64
N_HEADS = 8
D_ATTN = 512
D_GMLP = 512
D_IN = 2560
DILATIONS = (1, 4, 16)
BLK = 128
ROPE_THETA = 500000.0
ROPE_DIM = 16
D_FF = 2816
D_PLE = 256
LN_EPS = 1e-5
ALPHA = 2.0 ** 0.25
NEG_INF = -1e30
N_SHARD = 4
W_IN_BLK = D_IN // N_SHARD
FF_BLK = D_FF // N_SHARD
ROW_BLK = D_MODEL // N_SHARD
ADAM_LR, ADAM_B1, ADAM_B2, ADAM_EPS, ADAM_WD, ADAM_STEP = 0.001, 0.9, 0.999, 1e-08, 0.01, 10

TM = 512
HALO = 8
ROW_GROUPS = 2
VMEM_LIMIT = 56 * 1024 * 1024


def _cp(**kw):
    return pltpu.CompilerParams(vmem_limit_bytes=VMEM_LIMIT, **kw)


def _full(shape):
    n = len(shape)
    return pl.BlockSpec(shape, lambda *_: (0,) * n)


def _gelu(x):
    return 0.5 * x * (1.0 + lax.erf(x * (1.0 / math.sqrt(2.0))))


def _gelu_grad(x):
    return 0.5 * (1.0 + lax.erf(x * (1.0 / math.sqrt(2.0)))) + x * jnp.exp(-0.5 * x * x) * (1.0 / math.sqrt(2.0 * math.pi))


def _ln_fwd(r):
    mu = jnp.mean(r, axis=-1, keepdims=True)
    xc = r - mu
    var = jnp.mean(xc * xc, axis=-1, keepdims=True)
    rstd = lax.rsqrt(var + LN_EPS)
    return xc * rstd, rstd


def _ln_bwd(dy, xhat, rstd, g):
    dxh = dy * g
    m1 = jnp.mean(dxh, axis=-1, keepdims=True)
    m2 = jnp.mean(dxh * xhat, axis=-1, keepdims=True)
    return rstd * (dxh - m1 - xhat * m2)


def _dot(a, b):
    return jnp.dot(a.astype(MXU), b.astype(MXU), preferred_element_type=F32)


def _dot_nt(a, b):
    return lax.dot_general(a.astype(MXU), b.astype(MXU), (((1,), (1,)), ((), ())), preferred_element_type=F32)


def _dot_tn(a, b):
    return lax.dot_general(a.astype(MXU), b.astype(MXU), (((0,), (0,)), ((), ())), preferred_element_type=F32)


def _colsum(v):
    return jnp.sum(v, axis=0, keepdims=True)


def _rope_tables(positions, t, dep):
    inv = np.float32(ROPE_THETA) ** (-np.arange(0, ROPE_DIM, 2, dtype=np.float32) / np.float32(ROPE_DIM))
    half = ROPE_DIM // 2
    pos_rep = jnp.repeat(positions.reshape(t // 16, 16), half, axis=1)
    inv_row = jnp.asarray(np.tile(inv, 16)[None, :], F32)

    def trig_body(pos_ref, inv_ref, dep_ref, cos_ref, sin_ref):
        ang = pos_ref[...].astype(F32) * inv_ref[...]
        cos_ref[...] = jnp.cos(ang)
        sin_ref[...] = jnp.sin(ang)

    vm = pl.BlockSpec(memory_space=pltpu.VMEM)
    cos8, sin8 = pl.pallas_call(
        trig_body, name="rope_trig", in_specs=[vm, vm, pl.BlockSpec(memory_space=pl.ANY)], out_specs=[vm, vm],
        out_shape=(jax.ShapeDtypeStruct((t // 16, 128), F32), jax.ShapeDtypeStruct((t // 16, 128), F32)),
    )(pos_rep, inv_row, dep)
    cos8 = cos8.reshape(t, half)
    sin8 = sin8.reshape(t, half)

    lane = np.arange(128) % HEAD_DIM
    sel = (np.arange(half)[:, None] == (lane % half)[None, :])
    e_cos = (sel & (lane < ROPE_DIM)[None, :]).astype(np.float32)
    e_s1 = -(sel & (lane < half)[None, :]).astype(np.float32)
    e_s2 = (sel & ((lane >= half) & (lane < ROPE_DIM))[None, :]).astype(np.float32)
    ones = (lane >= ROPE_DIM).astype(np.float32)[None, :]

    def expand_body(cos_ref, sin_ref, ec_ref, e1_ref, e2_ref, ones_ref, c_ref, s1_ref, s2_ref):
        hp = lax.Precision.HIGHEST
        c_ref[...] = jnp.dot(cos_ref[...], ec_ref[...], precision=hp, preferred_element_type=F32) + ones_ref[...]
        s1_ref[...] = jnp.dot(sin_ref[...], e1_ref[...], precision=hp, preferred_element_type=F32)
        s2_ref[...] = jnp.dot(sin_ref[...], e2_ref[...], precision=hp, preferred_element_type=F32)

    tab = jax.ShapeDtypeStruct((t, 128), F32)
    return pl.pallas_call(expand_body, name="rope_expand", out_shape=(tab, tab, tab), compiler_params=_cp())(
        cos8, sin8, jnp.asarray(e_cos), jnp.asarray(e_s1), jnp.asarray(e_s2), jnp.asarray(ones))


def _tile_heads(tab):
    return jnp.concatenate([tab] * (D_ATTN // 128), axis=1)


def _rope_apply(v, c, s1, s2):
    n = v.shape[1]
    half = ROPE_DIM // 2
    return v * c + pltpu.roll(v, n - half, 1) * s1 + pltpu.roll(v, half, 1) * s2


def _rope_apply_t(g, c, s1, s2):
    n = g.shape[1]
    half = ROPE_DIM // 2
    return g * c + pltpu.roll(g * s1, half, 1) + pltpu.roll(g * s2, n - half, 1)


LANE_CHUNKS = D_ATTN // 128
HEAD_LANES = 128 // N_HEADS


def _perm_shape(t, d, w, dtype):
    return jax.ShapeDtypeStruct((d, t // d, w), dtype)


def _perm_tile(d, w):
    return pl.BlockSpec((None if d == 1 else d, TM // d, w), lambda i: (0, i, 0))


def _to_planes(ref, scr, d, n_chunks, dtype):
    for r in range(d):
        for cc in range(n_chunks):
            ref[r, :, cc * 128:(cc + 1) * 128] = scr.at[cc][pl.ds(r, TM // d, stride=d), :].astype(dtype)


def _from_planes(ref, scr, d, n_chunks, accumulate=False):
    for r in range(d):
        for cc in range(n_chunks):
            rows = scr.at[cc]
            val = ref[r, :, cc * 128:(cc + 1) * 128].astype(F32)
            if accumulate:
                rows[pl.ds(r, TM // d, stride=d), :] += val
            else:
                rows[pl.ds(r, TM // d, stride=d), :] = val


def _chunks(val):
    return [val[:, cc * 128:(cc + 1) * 128] for cc in range(val.shape[1] // 128)]


def _unchunk(scr, n_chunks, base=0):
    return jnp.concatenate([scr[base + cc] for cc in range(n_chunks)], axis=1)


def _head_expand():
    src = np.arange(128)[:, None]
    dst = np.arange(D_ATTN)[None, :]
    return jnp.asarray((src == (dst // HEAD_DIM) * HEAD_LANES).astype(np.float32))


def _head_reduce():
    src = np.arange(D_ATTN)[:, None]
    dst = np.arange(128)[None, :]
    return jnp.asarray((src // HEAD_DIM == dst // HEAD_LANES).astype(np.float32))


def _dot_select(a, sel):
    hi = a.astype(BF16)
    lo = (a - hi.astype(F32)).astype(BF16)
    sel = sel.astype(BF16)
    return jnp.dot(hi, sel, preferred_element_type=F32) + jnp.dot(lo, sel, preferred_element_type=F32)


def _qkvuz(x, w_in, c_tab, s1_tab, s2_tab, ln_z_g, ln_z_b, w_s, b_full, dep):
    t = x.shape[0]
    nchunk = TM // BLK

    def body(x_ref, w_ref, c_ref, s1_ref, s2_ref, g_ref, b_ref, ws_ref, bf_ref, dep_ref,
             qkv1_ref, qkv4_ref, qkv16_ref, hu_ref, hz_ref, mixed_ref, gm_ref, xb_ref, h_scr, wm_scr, p_scr):
        @pl.when(pl.program_id(0) == 0)
        def _():
            row = lax.broadcasted_iota(jnp.int32, (BLK, BLK), 0)
            col = lax.broadcasted_iota(jnp.int32, (BLK, BLK), 1)
            for g in range(N_HEADS):
                wm_scr[g] = jnp.where(col <= row, ws_ref[g], 0.0).astype(MXU)

        xb = x_ref[...].astype(MXU)
        xb_ref[...] = xb
        for j in range(N_SHARD):
            h_scr[:, j * W_IN_BLK:(j + 1) * W_IN_BLK] = jnp.dot(xb, w_ref[j], preferred_element_type=F32)
        c, s1, s2 = _tile_heads(c_ref[...]), _tile_heads(s1_ref[...]), _tile_heads(s2_ref[...])
        q = _rope_apply(h_scr[:, 0:D_ATTN], c, s1, s2) * (1.0 / math.sqrt(HEAD_DIM))
        k = _rope_apply(h_scr[:, D_ATTN:2 * D_ATTN], c, s1, s2)
        for part, val in enumerate((q, k, h_scr[:, 2 * D_ATTN:3 * D_ATTN])):
            qkv1_ref[:, part * D_ATTN:(part + 1) * D_ATTN] = val.astype(MXU)
            for cc in range(LANE_CHUNKS):
                p_scr[part * LANE_CHUNKS + cc] = val[:, cc * 128:(cc + 1) * 128]
        _to_planes(qkv4_ref, p_scr, DILATIONS[1], 3 * LANE_CHUNKS, MXU)
        _to_planes(qkv16_ref, p_scr, DILATIONS[2], 3 * LANE_CHUNKS, MXU)
        hu = h_scr[:, 3 * D_ATTN:3 * D_ATTN + D_GMLP]
        hz = h_scr[:, 3 * D_ATTN + D_GMLP:]
        hu_ref[...] = hu
        hz_ref[...] = hz
        zhat, _ = _ln_fwd(_gelu(hz))
        zn = (zhat * g_ref[...] + b_ref[...]).astype(MXU)
        for ch in range(nchunk):
            rows = slice(ch * BLK, (ch + 1) * BLK)
            for g in range(N_HEADS):
                cols = slice(g * HEAD_DIM, (g + 1) * HEAD_DIM)
                mixed_ref[rows, cols] = jnp.dot(wm_scr[g], zn[rows, cols], preferred_element_type=F32) + bf_ref[:, cols]
        gm_ref[...] = (_gelu(hu) * mixed_ref[...]).astype(MXU)

    tok = lambda w: pl.BlockSpec((TM, w), lambda i: (i, 0))
    outs = [_perm_shape(t, d, 3 * D_ATTN, MXU) for d in DILATIONS] + [jax.ShapeDtypeStruct((t, D_GMLP), F32)] * 3 + [
        jax.ShapeDtypeStruct((t, D_GMLP), MXU), jax.ShapeDtypeStruct((t, D_MODEL), MXU)]
    return pl.pallas_call(
        body, name="qkvuz", grid=(t // TM,),
        in_specs=[tok(D_MODEL), _full(w_in.shape), tok(128), tok(128), tok(128), _full(ln_z_g.shape), _full(ln_z_b.shape),
                  _full(w_s.shape), _full(b_full.shape), pl.BlockSpec(memory_space=pl.ANY)],
        out_specs=[_perm_tile(d, 3 * D_ATTN) for d in DILATIONS] + [tok(D_ATTN)] * 4 + [tok(D_MODEL)], out_shape=outs,
        scratch_shapes=[pltpu.VMEM((TM, D_IN), F32), pltpu.VMEM((N_HEADS, BLK, BLK), MXU),
                        pltpu.VMEM((3 * LANE_CHUNKS, TM, 128), F32)],
        compiler_params=_cp(dimension_semantics=("arbitrary",)),
    )(x, w_in, c_tab, s1_tab, s2_tab, ln_z_g, ln_z_b, w_s, b_full, dep)


def _band_valid(n):
    i = lax.broadcasted_iota(jnp.int32, (BLK, 2 * BLK), 0)
    j = lax.broadcasted_iota(jnp.int32, (BLK, 2 * BLK), 1)
    return (j >= i) & (j <= i + BLK) & ((j >= BLK) | (n > 0))


def _attn_fwd(qkv, d, dep):
    _, l_sub, _ = qkv.shape
    nb = l_sub // BLK

    def body(q_ref, kp_ref, kc_ref, vp_ref, vc_ref, dep_ref, o_ref, l_ref):
        valid = _band_valid(pl.program_id(1))
        kcat = jnp.concatenate([kp_ref[...], kc_ref[...]], axis=0)
        vcat = jnp.concatenate([vp_ref[...], vc_ref[...]], axis=0)
        for h in range(N_HEADS):
            cols = slice(h * HEAD_DIM, (h + 1) * HEAD_DIM)
            s = jnp.where(valid, _dot_nt(q_ref[:, cols], kcat[:, cols]), NEG_INF)
            m = jnp.max(s, axis=-1, keepdims=True)
            e = jnp.exp(s - m)
            den = jnp.sum(e, axis=-1, keepdims=True)
            o_ref[:, cols] = _dot(e, vcat[:, cols]) * (1.0 / den)
            l_ref[:, h * HEAD_LANES:(h + 1) * HEAD_LANES] = jnp.broadcast_to(m + jnp.log(den), (BLK, HEAD_LANES))

    def blk(w, col, prev=False):
        return pl.BlockSpec((None, BLK, w), lambda r, n: (r, jnp.maximum(n - 1, 0) if prev else n, col))

    return pl.pallas_call(
        body, name=f"attn_fwd_d{d}", grid=(d, nb),
        in_specs=[blk(D_ATTN, 0), blk(D_ATTN, 1, True), blk(D_ATTN, 1), blk(D_ATTN, 2, True), blk(D_ATTN, 2),
                  pl.BlockSpec(memory_space=pl.ANY)],
        out_specs=[blk(D_ATTN, 0), blk(128, 0)],
        out_shape=[jax.ShapeDtypeStruct((d, l_sub, D_ATTN), F32), jax.ShapeDtypeStruct((d, l_sub, 128), F32)],
        compiler_params=_cp(dimension_semantics=("arbitrary", "arbitrary")),
    )(qkv, qkv, qkv, qkv, qkv, dep)


def _attn_bwd(qkv, do, lse, delta, d, dep):
    _, l_sub, _ = qkv.shape
    nb = l_sub // BLK
    whole = l_sub <= 8 * BLK

    def shares(n, q_ref, kp_ref, kc_ref, vp_ref, vc_ref, do_ref, l_ref, dl_ref, dq_ref):
        valid = _band_valid(n)
        kcat = jnp.concatenate([kp_ref[...], kc_ref[...]], axis=0)
        vcat = jnp.concatenate([vp_ref[...], vc_ref[...]], axis=0)
        for h in range(N_HEADS):
            cols = slice(h * HEAD_DIM, (h + 1) * HEAD_DIM)
            stat = slice(h * HEAD_LANES, h * HEAD_LANES + 1)
            qh, doh = q_ref[:, cols], do_ref[:, cols]
            p = jnp.where(valid, jnp.exp(_dot_nt(qh, kcat[:, cols]) - l_ref[:, stat]), 0.0)
            ds = p * (_dot_nt(doh, vcat[:, cols]) - dl_ref[:, stat])
            dq_ref[:, cols] = _dot(ds, kcat[:, cols])
            yield cols, _dot_tn(ds, qh), _dot_tn(p, doh)

    def body_whole(*refs):
        dk_ref, dv_ref = refs[10:]
        n = pl.program_id(1)
        cur = pl.ds(pl.multiple_of(n * BLK, BLK), BLK)
        prev = pl.ds(pl.multiple_of(jnp.maximum(n - 1, 0) * BLK, BLK), BLK)
        for cols, dk2, dv2 in shares(n, *refs[:8], refs[9]):
            dk_ref[cur, cols] = dk2[BLK:]
            dv_ref[cur, cols] = dv2[BLK:]
            dk_ref[prev, cols] += dk2[0:BLK]
            dv_ref[prev, cols] += dv2[0:BLK]

    def body_carry(*refs):
        dk_ref, dv_ref, ck_scr, cv_scr = refs[10:]
        n = pl.program_id(1)

        @pl.when(n == 0)
        def _():
            ck_scr[...] = jnp.zeros_like(ck_scr)
            cv_scr[...] = jnp.zeros_like(cv_scr)

        @pl.when(n < nb)
        def _():
            for cols, dk2, dv2 in shares(n, *refs[:8], refs[9]):
                dk_ref[:, cols] = ck_scr[:, cols] + dk2[0:BLK]
                dv_ref[:, cols] = cv_scr[:, cols] + dv2[0:BLK]
                ck_scr[:, cols] = dk2[BLK:]
                cv_scr[:, cols] = dv2[BLK:]

        @pl.when(n == nb)
        def _():
            dk_ref[...] = ck_scr[...]
            dv_ref[...] = cv_scr[...]

    def blk(w, col, shift=0):
        return pl.BlockSpec((None, BLK, w), lambda r, n: (r, jnp.clip(n - shift, 0, nb - 1), col))

    if whole:
        dkv_spec = pl.BlockSpec((None, l_sub, D_ATTN), lambda r, n: (r, 0, 0))
        body, steps, scratch = body_whole, nb, []
    else:
        dkv_spec = blk(D_ATTN, 0, 1)
        body, steps, scratch = body_carry, nb + 1, [pltpu.VMEM((BLK, D_ATTN), F32)] * 2
    return pl.pallas_call(
        body, name=f"attn_bwd_d{d}", grid=(d, steps),
        in_specs=[blk(D_ATTN, 0), blk(D_ATTN, 1, 1), blk(D_ATTN, 1), blk(D_ATTN, 2, 1), blk(D_ATTN, 2),
                  blk(D_ATTN, 0), blk(128, 0), blk(128, 0), pl.BlockSpec(memory_space=pl.ANY)],
        out_specs=[blk(D_ATTN, 0), dkv_spec, dkv_spec],
        out_shape=[jax.ShapeDtypeStruct((d, l_sub, D_ATTN), F32)] * 3,
        scratch_shapes=scratch,
        compiler_params=_cp(dimension_semantics=("arbitrary", "arbitrary")),
    )(qkv, qkv, qkv, qkv, qkv, do, lse, delta, dep)


def _mix_ln1(os_, ls_, gm, x, w_o, ln1_g, ln1_b, dep):
    t = x.shape[0]
    expand = _head_expand()

    def body(o1, o4, o16, l1, l4, l16, gm_ref, x_ref, wo_ref, g_ref, b_ref, ex_ref, dep_ref,
             attn_ref, lse1_ref, lse4_ref, lse16_ref, cat_ref, xhat_ref, rstd_ref, x1b_ref, o_scr, l_scr):
        _from_planes(o4, o_scr, DILATIONS[1], LANE_CHUNKS)
        _from_planes(o16, o_scr.at[pl.ds(LANE_CHUNKS, LANE_CHUNKS)], DILATIONS[2], LANE_CHUNKS)
        _from_planes(l4, l_scr, DILATIONS[1], 1)
        _from_planes(l16, l_scr.at[pl.ds(1, 1)], DILATIONS[2], 1)
        la, lb, lc = l1[...], l_scr[0], l_scr[1]
        m = jnp.maximum(jnp.maximum(la, lb), lc)
        ea, eb, ec = jnp.exp(la - m), jnp.exp(lb - m), jnp.exp(lc - m)
        den = ea + eb + ec
        inv = 1.0 / den
        wide = lambda w: _dot_select(w, ex_ref[...])
        attn = (wide(ea * inv) * o1[...] + wide(eb * inv) * _unchunk(o_scr, LANE_CHUNKS)
                + wide(ec * inv) * _unchunk(o_scr, LANE_CHUNKS, LANE_CHUNKS))
        attn_ref[...] = attn
        lse = m + jnp.log(den)
        lse1_ref[...] = lse
        l_scr[2] = lse
        _to_planes(lse4_ref, l_scr.at[pl.ds(2, 1)], DILATIONS[1], 1, F32)
        _to_planes(lse16_ref, l_scr.at[pl.ds(2, 1)], DILATIONS[2], 1, F32)
        cat_ref[:, 0:D_ATTN] = attn.astype(MXU)
        cat_ref[:, D_ATTN:] = gm_ref[...]
        mix = jnp.dot(cat_ref[...], wo_ref[...], preferred_element_type=F32)
        xhat, rstd = _ln_fwd(ALPHA * x_ref[...] + mix)
        xhat_ref[...] = xhat
        rstd_ref[...] = rstd
        x1b_ref[...] = (xhat * g_ref[...] + b_ref[...]).astype(MXU)

    tok = lambda w: pl.BlockSpec((TM, w), lambda i: (i, 0))
    outs = [jax.ShapeDtypeStruct((t, D_ATTN), F32)] + [_perm_shape(t, d, 128, F32) for d in DILATIONS] + [
        jax.ShapeDtypeStruct((t, D_MODEL), MXU), jax.ShapeDtypeStruct((t, D_MODEL), F32), jax.ShapeDtypeStruct((t, 1), F32),
        jax.ShapeDtypeStruct((t, D_MODEL), MXU)]
    return pl.pallas_call(
        body, name="mix_ln1", grid=(t // TM,),
        in_specs=[_perm_tile(d, D_ATTN) for d in DILATIONS] + [_perm_tile(d, 128) for d in DILATIONS]
        + [tok(D_GMLP), tok(D_MODEL), _full(w_o.shape), _full(ln1_g.shape), _full(ln1_b.shape), _full(expand.shape),
           pl.BlockSpec(memory_space=pl.ANY)],
        out_specs=[tok(D_ATTN)] + [_perm_tile(d, 128) for d in DILATIONS] + [tok(D_MODEL), tok(D_MODEL), tok(1), tok(D_MODEL)],
        out_shape=outs,
        scratch_shapes=[pltpu.VMEM((2 * LANE_CHUNKS, TM, 128), F32), pltpu.VMEM((3, TM, 128), F32)],
        compiler_params=_cp(dimension_semantics=("arbitrary",)),
    )(*os_, *ls_, gm, x, w_o, ln1_g, ln1_b, expand, dep)


def _conv_fwd(a_ext, w_ref, b_ref, rows):
    back = [pltpu.roll(a_ext, s, 0)[HALO:HALO + rows] for s in (1, 2)]
    return b_ref[...] + w_ref[2:3, :] * a_ext[HALO:HALO + rows] + w_ref[1:2, :] * back[0] + w_ref[0:1, :] * back[1]


def _ffn_in(x1b, w_a, w_b, conv_w, conv_b):
    t = x1b.shape[0]
    hb = TM // HALO

    def body(x_ref, xh_ref, wa_ref, wb_ref, cw_ref, cb_ref, apre_ref, act_ref, gate_ref, f_ref):
        i = pl.program_id(1)
        a_pre = _dot_nt(x_ref[...], wa_ref[...])
        a_halo = jnp.where(i > 0, _dot_nt(xh_ref[...], wa_ref[...]), 0.0)
        a = _conv_fwd(jnp.concatenate([a_halo, a_pre], axis=0), cw_ref, cb_ref, TM)
        b = _dot_nt(x_ref[...], wb_ref[...])
        cdf = 0.5 * (1.0 + lax.erf(a * (1.0 / math.sqrt(2.0))))
        pdf = jnp.exp(-0.5 * a * a) * (1.0 / math.sqrt(2.0 * math.pi))
        act = a * cdf
        apre_ref[...] = a_pre
        act_ref[...] = act
        gate_ref[...] = b * (cdf + a * pdf)
        f_ref[...] = (act * b).astype(MXU)

    blk = lambda r, c: pl.BlockSpec((None, r, c), lambda j, i: (j, 0, 0))
    tokj = pl.BlockSpec((None, TM, FF_BLK), lambda j, i: (j, i, 0))
    outs = [jax.ShapeDtypeStruct((N_SHARD, t, FF_BLK), F32)] * 3 + [jax.ShapeDtypeStruct((N_SHARD, t, FF_BLK), MXU)]
    return pl.pallas_call(
        body, name="ffn_in", grid=(N_SHARD, t // TM),
        in_specs=[pl.BlockSpec((TM, D_MODEL), lambda j, i: (i, 0)),
                  pl.BlockSpec((HALO, D_MODEL), lambda j, i: (jnp.maximum(i * hb - 1, 0), 0)),
                  blk(FF_BLK, D_MODEL), blk(FF_BLK, D_MODEL), blk(3, FF_BLK), blk(1, FF_BLK)],
        out_specs=[tokj, tokj, tokj, tokj], out_shape=outs,
        compiler_params=_cp(dimension_semantics=("arbitrary", "arbitrary")),
    )(x1b, x1b, w_a, w_b, conv_w, conv_b)


def _ffn_out_ln2(f, w_down, xhat1, ln1_g, ln1_b):
    t = xhat1.shape[0]

    def body(f_ref, wd_ref, xh_ref, g1_ref, b1_ref, xhat_ref, rstd_ref):
        half = TM // ROW_GROUPS
        for r0 in range(0, TM, half):
            rows = pl.ds(r0, half)
            ff = jnp.dot(f_ref[0, rows, :], wd_ref[0], preferred_element_type=F32)
            for j in range(1, N_SHARD):
                ff = ff + jnp.dot(f_ref[j, rows, :], wd_ref[j], preferred_element_type=F32)
            x1 = xh_ref[rows, :] * g1_ref[...] + b1_ref[...]
            xhat, rstd = _ln_fwd(ALPHA * x1 + ff)
            xhat_ref[rows, :] = xhat
            rstd_ref[rows, :] = rstd

    tok = lambda w: pl.BlockSpec((TM, w), lambda i: (i, 0))
    vec = _full((1, D_MODEL))
    outs = [jax.ShapeDtypeStruct((t, D_MODEL), F32), jax.ShapeDtypeStruct((t, 1), F32)]
    return pl.pallas_call(
        body, name="ffn_out_ln2", grid=(t // TM,),
        in_specs=[pl.BlockSpec((N_SHARD, TM, FF_BLK), lambda i: (0, i, 0)), _full(w_down.shape), tok(D_MODEL), vec, vec],
        out_specs=[tok(D_MODEL), tok(1)], out_shape=outs,
        compiler_params=_cp(dimension_semantics=("arbitrary",)),
    )(f, w_down, xhat1, ln1_g, ln1_b)


STAT_ROWS = 8


def _ple_loss_bwd(xhat2, rstd2, p, target, ln2_g, ln2_b, w_g, b_g, w_p, ln3_g, ln3_b):
    t = xhat2.shape[0]

    def body(xh2_ref, rs2_ref, p_ref, t_ref, g2_ref, b2_ref, wg_ref, bg_ref, wp_ref, g3_ref, b3_ref,
             dr2_ref, dr2b_ref, stat_ref, dwg_ref, dwp_ref, pp_scr, dwp_scr):
        @pl.when(pl.program_id(0) == 0)
        def _():
            stat_ref[...] = jnp.zeros_like(stat_ref)
            dwg_ref[...] = jnp.zeros_like(dwg_ref)
            dwp_scr[...] = jnp.zeros_like(dwp_scr)

        xhat2 = xh2_ref[...]
        x2 = xhat2 * g2_ref[...] + b2_ref[...]
        x2b = x2.astype(MXU)
        gate = jax.nn.sigmoid(jnp.dot(x2b, wg_ref[...], preferred_element_type=F32) + bg_ref[...])
        pb = p_ref[...].astype(MXU)
        for j in range(N_SHARD):
            pp_scr[:, j * ROW_BLK:(j + 1) * ROW_BLK] = jnp.dot(pb, wp_ref[j], preferred_element_type=F32)
        pp = pp_scr[...]
        xhat3, rstd3 = _ln_fwd(ALPHA * x2 + gate * pp)
        err = xhat3 * g3_ref[...] + b3_ref[...] - t_ref[...]
        dy = err * (1.0 / D_MODEL)
        dr3 = _ln_bwd(dy, xhat3, rstd3, g3_ref[...])
        dgp = dr3 * pp * gate * (1.0 - gate)
        dgp_b = dgp.astype(MXU)
        dwg_ref[...] += _dot_tn(x2b, dgp_b)
        dwp_scr[...] += _dot_tn(pb, dr3 * gate)
        dx2 = ALPHA * dr3 + _dot_nt(dgp_b, wg_ref[...])
        dr2 = _ln_bwd(dx2, xhat2, rs2_ref[...], g2_ref[...])
        dr2_ref[...] = dr2
        dr2b_ref[...] = dr2.astype(MXU)
        stat_ref[0:1, :] += _colsum(dy * xhat3)
        stat_ref[1:2, :] += _colsum(dy)
        stat_ref[2:3, :] += _colsum(dgp)
        stat_ref[3:4, :] += _colsum(dx2 * xhat2)
        stat_ref[4:5, :] += _colsum(dx2)
        stat_ref[5:6, :] += _colsum(err * err)

        @pl.when(pl.program_id(0) == t // TM - 1)
        def _():
            for j in range(N_SHARD):
                dwp_ref[j] = dwp_scr[:, j * ROW_BLK:(j + 1) * ROW_BLK]

    tok = lambda w: pl.BlockSpec((TM, w), lambda i: (i, 0))
    vec = _full((1, D_MODEL))
    outs = [jax.ShapeDtypeStruct((t, D_MODEL), F32), jax.ShapeDtypeStruct((t, D_MODEL), MXU),
            jax.ShapeDtypeStruct((STAT_ROWS, D_MODEL), F32), jax.ShapeDtypeStruct((D_MODEL, D_MODEL), F32),
            jax.ShapeDtypeStruct((N_SHARD, D_PLE, ROW_BLK), F32)]
    return pl.pallas_call(
        body, name="ple_loss_bwd", grid=(t // TM,),
        in_specs=[tok(D_MODEL), tok(1), tok(D_PLE), tok(D_MODEL), vec, vec, _full(w_g.shape), vec, _full(w_p.shape), vec, vec],
        out_specs=[tok(D_MODEL), tok(D_MODEL), _full((STAT_ROWS, D_MODEL)), _full((D_MODEL, D_MODEL)),
                   _full((N_SHARD, D_PLE, ROW_BLK))], out_shape=outs,
        scratch_shapes=[pltpu.VMEM((TM, D_MODEL), F32), pltpu.VMEM((D_PLE, D_MODEL), F32)],
        compiler_params=_cp(dimension_semantics=("arbitrary",)),
    )(xhat2, rstd2, p, target, ln2_g, ln2_b, w_g, b_g, w_p, ln3_g, ln3_b)


def _ffn_bwd(dr2, dr2b, a_pre, act, gate, w_down, w_a, w_b, conv_w, xhat1, rstd1, ln1_g):
    t = dr2.shape[0]
    nt = t // TM
    hb = TM // HALO
    last_h = t // HALO - 1
    halo2 = 2 * HALO

    def body(dr_ref, drb_ref, drbn_ref, ap_ref, act_ref, gate_ref, gaten_ref, wd_ref, wa_ref, wb_ref, cw_ref,
             xh_ref, rs_ref, g1_ref, dap_ref, dbb_ref, dr1_ref, cstat_ref, lstat_ref, acc_scr):
        i, j = pl.program_id(0), pl.program_id(1)

        @pl.when((i == 0) & (j == 0))
        def _():
            cstat_ref[...] = jnp.zeros_like(cstat_ref)
            lstat_ref[...] = jnp.zeros_like(lstat_ref)

        half = TM // ROW_GROUPS
        parts = []
        for r0 in range(0, TM, half):
            rows = pl.ds(r0, half)
            last = r0 + half == TM

            def ext(ref, nxt):
                return jnp.concatenate([ref[rows], nxt[...]], axis=0) if last else ref[r0:r0 + half + HALO]

            drb = jnp.concatenate([drb_ref[rows, :], drbn_ref[...]], axis=0) if last else drb_ref[r0:r0 + half + halo2, :]
            df = _dot_nt(drb, wd_ref[...])[0:half + HALO]
            da = df * ext(gate_ref, gaten_ref)
            if last:
                da = jnp.concatenate([da[0:half], jnp.where(i < nt - 1, da[half:], 0.0)], axis=0)
            ahead = [da[0:half]] + [pltpu.roll(da, half + HALO - s, 0)[0:half] for s in (1, 2)]
            da_pre = cw_ref[2:3, :] * ahead[0] + cw_ref[1:2, :] * ahead[1] + cw_ref[0:1, :] * ahead[2]
            dbb = df[0:half] * act_ref[rows, :]
            dap_ref[rows, :] = da_pre.astype(MXU)
            dbb_ref[rows, :] = dbb.astype(MXU)
            for kk in range(3):
                cstat_ref[j, kk:kk + 1, :] += _colsum(ahead[2 - kk] * ap_ref[rows, :])
            cstat_ref[j, 3:4, :] += _colsum(ahead[0])
            parts.append(_dot(da_pre, wa_ref[...]) + _dot(dbb, wb_ref[...]))
        part = jnp.concatenate(parts, axis=0)

        @pl.when(j == 0)
        def _():
            acc_scr[...] = ALPHA * dr_ref[...] + part

        @pl.when(j > 0)
        def _():
            acc_scr[...] += part

        @pl.when(j == N_SHARD - 1)
        def _():
            dx1 = acc_scr[...]
            xhat1 = xh_ref[...]
            lstat_ref[0:1, :] += _colsum(dx1 * xhat1)
            lstat_ref[1:2, :] += _colsum(dx1)
            dr1_ref[...] = _ln_bwd(dx1, xhat1, rs_ref[...], g1_ref[...])

    tok = lambda w: pl.BlockSpec((TM, w), lambda i, j: (i, 0))
    tokj = pl.BlockSpec((None, TM, FF_BLK), lambda i, j: (j, i, 0))
    nextj = pl.BlockSpec((None, HALO, FF_BLK), lambda i, j: (j, jnp.minimum((i + 1) * hb, last_h), 0))
    blk = lambda r, c: pl.BlockSpec((None, r, c), lambda i, j: (j, 0, 0))
    outs = [jax.ShapeDtypeStruct((N_SHARD, t, FF_BLK), MXU)] * 2 + [
        jax.ShapeDtypeStruct((t, D_MODEL), F32), jax.ShapeDtypeStruct((N_SHARD, STAT_ROWS, FF_BLK), F32),
        jax.ShapeDtypeStruct((STAT_ROWS, D_MODEL), F32)]
    return pl.pallas_call(
        body, name="ffn_bwd", grid=(nt, N_SHARD),
        in_specs=[tok(D_MODEL), tok(D_MODEL),
                  pl.BlockSpec((halo2, D_MODEL), lambda i, j: (jnp.minimum((i + 1) * (hb // 2), last_h // 2), 0)),
                  tokj, tokj, tokj, nextj, blk(FF_BLK, D_MODEL), blk(FF_BLK, D_MODEL), blk(FF_BLK, D_MODEL),
                  blk(3, FF_BLK), tok(D_MODEL), tok(1), _full((1, D_MODEL))],
        out_specs=[tokj, tokj, tok(D_MODEL), _full((N_SHARD, STAT_ROWS, FF_BLK)), _full((STAT_ROWS, D_MODEL))], out_shape=outs,
        scratch_shapes=[pltpu.VMEM((TM, D_MODEL), F32)],
        compiler_params=_cp(dimension_semantics=("arbitrary", "arbitrary")),
    )(dr2, dr2b, dr2b, a_pre, act, gate, gate, w_down, w_a, w_b, conv_w, xhat1, rstd1, ln1_g)


def _mix_bwd(dr1, w_o, hu, hz, mixed, attn, ln_z_g, ln_z_b, w_s, dep):
    t = dr1.shape[0]
    nchunk = TM // BLK

    def body(dr_ref, wo_ref, hu_ref, hz_ref, mx_ref, attn_ref, g_ref, b_ref, ws_ref, grp_ref, red_ref, dep_ref,
             do1_ref, do4_ref, do16_ref, dl1_ref, dl4_ref, dl16_ref, duz_ref, dws_ref, dbs_ref, zstat_ref,
             wm_scr, dzn_scr, dbsum_scr, do_scr, dl_scr):
        @pl.when(pl.program_id(0) == 0)
        def _():
            row = lax.broadcasted_iota(jnp.int32, (BLK, BLK), 0)
            col = lax.broadcasted_iota(jnp.int32, (BLK, BLK), 1)
            for g in range(N_HEADS):
                wm_scr[g] = jnp.where(col <= row, ws_ref[g], 0.0).astype(MXU)
            dws_ref[...] = jnp.zeros_like(dws_ref)
            dbsum_scr[...] = jnp.zeros_like(dbsum_scr)
            zstat_ref[...] = jnp.zeros_like(zstat_ref)

        dcat = _dot_nt(dr_ref[...], wo_ref[...])
        dattn = dcat[:, 0:D_ATTN]
        do1_ref[...] = dattn.astype(MXU)
        for cc, val in enumerate(_chunks(dattn)):
            do_scr[cc] = val
        _to_planes(do4_ref, do_scr, DILATIONS[1], LANE_CHUNKS, MXU)
        _to_planes(do16_ref, do_scr, DILATIONS[2], LANE_CHUNKS, MXU)
        delta = _dot_select(dattn * attn_ref[...], red_ref[...])
        dl1_ref[...] = delta
        dl_scr[0] = delta
        _to_planes(dl4_ref, dl_scr, DILATIONS[1], 1, F32)
        _to_planes(dl16_ref, dl_scr, DILATIONS[2], 1, F32)
        dgm = dcat[:, D_ATTN:]
        hu, hz = hu_ref[...], hz_ref[...]
        u = _gelu(hu)
        duz_ref[:, 0:D_GMLP] = (dgm * mx_ref[...] * _gelu_grad(hu)).astype(MXU)
        dmixed = dgm * u
        dmb = dmixed.astype(MXU)
        zhat, rstd = _ln_fwd(_gelu(hz))
        znb = (zhat * g_ref[...] + b_ref[...]).astype(MXU)
        dbs_acc = jnp.zeros((BLK, D_GMLP), F32)
        for ch in range(nchunk):
            rows = slice(ch * BLK, (ch + 1) * BLK)
            dbs_acc = dbs_acc + dmixed[rows]
            for g in range(N_HEADS):
                cols = slice(g * HEAD_DIM, (g + 1) * HEAD_DIM)
                dzn_scr[rows, cols] = _dot_tn(wm_scr[g], dmb[rows, cols])
                dws_ref[g] += _dot_nt(dmb[rows, cols], znb[rows, cols])
        dbsum_scr[...] += dbs_acc
        dzn = dzn_scr[...]
        zstat_ref[0:1, :] += _colsum(dzn * zhat)
        zstat_ref[1:2, :] += _colsum(dzn)
        duz_ref[:, D_GMLP:] = (_ln_bwd(dzn, zhat, rstd, g_ref[...]) * _gelu_grad(hz)).astype(MXU)

        @pl.when(pl.program_id(0) == nt - 1)
        def _():
            row = lax.broadcasted_iota(jnp.int32, (BLK, BLK), 0)
            col = lax.broadcasted_iota(jnp.int32, (BLK, BLK), 1)
            for g in range(N_HEADS):
                dws_ref[g] = jnp.where(col <= row, dws_ref[g], 0.0)
            dbs_ref[...] = lax.dot_general(grp_ref[...], dbsum_scr[...], (((1,), (1,)), ((), ())),
                                           precision=lax.Precision.HIGHEST, preferred_element_type=F32)

    nt = t // TM
    tok = lambda w: pl.BlockSpec((TM, w), lambda i: (i, 0))
    grp = jnp.asarray((np.arange(D_GMLP)[None, :] // HEAD_DIM == np.arange(N_HEADS)[:, None]).astype(np.float32))
    red = _head_reduce()
    outs = [_perm_shape(t, d, D_ATTN, MXU) for d in DILATIONS] + [_perm_shape(t, d, 128, F32) for d in DILATIONS] + [
        jax.ShapeDtypeStruct((t, 2 * D_GMLP), MXU),
        jax.ShapeDtypeStruct((N_HEADS, BLK, BLK), F32), jax.ShapeDtypeStruct((N_HEADS, BLK), F32),
        jax.ShapeDtypeStruct((STAT_ROWS, D_GMLP), F32)]
    return pl.pallas_call(
        body, name="mix_bwd", grid=(t // TM,),
        in_specs=[tok(D_MODEL), _full(w_o.shape), tok(D_GMLP), tok(D_GMLP), tok(D_GMLP), tok(D_ATTN), _full(ln_z_g.shape),
                  _full(ln_z_b.shape), _full(w_s.shape), _full(grp.shape), _full(red.shape), pl.BlockSpec(memory_space=pl.ANY)],
        out_specs=[_perm_tile(d, D_ATTN) for d in DILATIONS] + [_perm_tile(d, 128) for d in DILATIONS]
        + [tok(2 * D_GMLP), _full((N_HEADS, BLK, BLK)), _full((N_HEADS, BLK)), _full((STAT_ROWS, D_GMLP))],
        out_shape=outs,
        scratch_shapes=[pltpu.VMEM((N_HEADS, BLK, BLK), MXU), pltpu.VMEM((TM, D_GMLP), F32), pltpu.VMEM((BLK, D_GMLP), F32),
                        pltpu.VMEM((LANE_CHUNKS, TM, 128), F32), pltpu.VMEM((1, TM, 128), F32)],
        compiler_params=_cp(dimension_semantics=("arbitrary",)),
    )(dr1, w_o, hu, hz, mixed, attn, ln_z_g, ln_z_b, w_s, grp, red, dep)


def _dx_in(dqs, dks, dvs, duz, dr1, w_in, c_tab, s1_tab, s2_tab):
    t = dr1.shape[0]

    def body(dq1, dq4, dq16, dk1, dk4, dk16, dv1, dv4, dv16, duz_ref, dr_ref, w_ref, c_ref, s1_ref, s2_ref,
             dh_ref, dx_ref, acc_scr):
        sums = []
        for part, (g1, g4, g16) in enumerate(((dq1, dq4, dq16), (dk1, dk4, dk16), (dv1, dv4, dv16))):
            acc = acc_scr.at[pl.ds(part * LANE_CHUNKS, LANE_CHUNKS)]
            for cc in range(LANE_CHUNKS):
                acc[cc] = g1[:, cc * 128:(cc + 1) * 128]
            _from_planes(g4, acc, DILATIONS[1], LANE_CHUNKS, accumulate=True)
            _from_planes(g16, acc, DILATIONS[2], LANE_CHUNKS, accumulate=True)
            sums.append(_unchunk(acc_scr, LANE_CHUNKS, part * LANE_CHUNKS))
        c, s1, s2 = _tile_heads(c_ref[...]), _tile_heads(s1_ref[...]), _tile_heads(s2_ref[...])
        dh_ref[:, 0:D_ATTN] = _rope_apply_t(sums[0] * (1.0 / math.sqrt(HEAD_DIM)), c, s1, s2).astype(MXU)
        dh_ref[:, D_ATTN:2 * D_ATTN] = _rope_apply_t(sums[1], c, s1, s2).astype(MXU)
        dh_ref[:, 2 * D_ATTN:3 * D_ATTN] = sums[2].astype(MXU)
        dh_ref[:, 3 * D_ATTN:] = duz_ref[...]
        dx = ALPHA * dr_ref[...]
        for j in range(N_SHARD):
            dx = dx + _dot_nt(dh_ref[:, j * W_IN_BLK:(j + 1) * W_IN_BLK], w_ref[j])
        dx_ref[...] = dx

    tok = lambda w: pl.BlockSpec((TM, w), lambda i: (i, 0))
    outs = [jax.ShapeDtypeStruct((t, D_IN), MXU), jax.ShapeDtypeStruct((t, D_MODEL), F32)]
    return pl.pallas_call(
        body, name="dx_in", grid=(t // TM,),
        in_specs=[_perm_tile(d, D_ATTN) for d in DILATIONS] * 3
        + [tok(2 * D_GMLP), tok(D_MODEL), _full(w_in.shape), tok(128), tok(128), tok(128)],
        out_specs=[tok(D_IN), tok(D_MODEL)], out_shape=outs,
        scratch_shapes=[pltpu.VMEM((3 * LANE_CHUNKS, TM, 128), F32)],
        compiler_params=_cp(dimension_semantics=("arbitrary",)),
    )(*dqs, *dks, *dvs, duz, dr1, w_in, c_tab, s1_tab, s2_tab)


def _wgrad(name, x, dy, x_spec, dy_spec, out_spec, out_shape, grid, dep=None):
    deps = [] if dep is None else [dep]

    def body(x_ref, dy_ref, *rest):
        rest[-1][...] = _dot_tn(x_ref[...], dy_ref[...])

    return pl.pallas_call(
        body, name=name, grid=grid, in_specs=[x_spec, dy_spec] + [pl.BlockSpec(memory_space=pl.ANY)] * len(deps),
        out_specs=out_spec, out_shape=jax.ShapeDtypeStruct(out_shape, F32),
        compiler_params=_cp(dimension_semantics=("arbitrary",) * len(grid)),
    )(x, dy, *deps)


def _wgrad_pair(name, xa, xb, dy, x_spec, dy_spec, out_spec, out_shape, grid):
    def body(xa_ref, xb_ref, dy_ref, oa_ref, ob_ref):
        dy = dy_ref[...]
        oa_ref[...] = _dot_tn(xa_ref[...], dy)
        ob_ref[...] = _dot_tn(xb_ref[...], dy)

    return pl.pallas_call(
        body, name=name, grid=grid, in_specs=[x_spec, x_spec, dy_spec], out_specs=[out_spec, out_spec],
        out_shape=[jax.ShapeDtypeStruct(out_shape, F32)] * 2,
        compiler_params=_cp(dimension_semantics=("arbitrary",) * len(grid)),
    )(xa, xb, dy)


def _local_step(x, p, rope, target, w_in, start_dep, late_landed, late_weights, early_grads, early_grads_sent,
                early_grads_landed,
                ln_z_g, ln_z_b, w_s, b_s, ln1_g, ln1_b, conv_b, ln2_g, ln2_b, b_g, ln3_g, ln3_b):
    t = x.shape[0]
    half = TM
    c_tab, s1_tab, s2_tab = rope
    b_full = jnp.repeat(jnp.transpose(b_s[0]), HEAD_DIM, axis=1)
    conv_b4 = conv_b.reshape(N_SHARD, 1, FF_BLK)
    *qkvs, hu, hz, mixed, gm, xb = _qkvuz(x, w_in, c_tab, s1_tab, s2_tab, ln_z_g, ln_z_b, w_s[0], b_full, start_dep)
    branches = [_attn_fwd(qkv, d, start_dep) for qkv, d in zip(qkvs[:2], DILATIONS[:2])]
    dep = late_landed(branches[-1][1])
    branches.append(_attn_fwd(qkvs[2], DILATIONS[2], dep))
    w_o, w_a, w_b, conv_w, w_down, w_g, w_p = late_weights(branches[-1][1])
    attn, *lses, cat, xhat1, rstd1, x1b = _mix_ln1(
        [o for o, _ in branches], [l for _, l in branches], gm, x, w_o, ln1_g, ln1_b, dep)
    a_pre, act, gate, f = _ffn_in(x1b, w_a, w_b, conv_w, conv_b4)
    xhat2, rstd2 = _ffn_out_ln2(f, w_down, xhat1, ln1_g, ln1_b)
    dr2, dr2b, stat3, g_w_g, g_w_p = _ple_loss_bwd(xhat2, rstd2, p, target, ln2_g, ln2_b, w_g, b_g, w_p, ln3_g, ln3_b)
    da_pre, dbb, dr1, cstat, stat1 = _ffn_bwd(dr2, dr2b, a_pre, act, gate, w_down, w_a, w_b, conv_w, xhat1, rstd1, ln1_g)

    full_t = lambda w, im: pl.BlockSpec((t, w), im)
    ffj = pl.BlockSpec((None, t, FF_BLK), lambda j, kk: (j, 0, 0))
    early = dict(
        w_ple_gate=g_w_g, w_ple_in=g_w_p,
        w_ff_down=_wgrad("dw_down", f, dr2b, ffj, full_t(half, lambda j, n: (0, n)),
                         pl.BlockSpec((None, FF_BLK, half), lambda j, n: (j, 0, n)), (N_SHARD, FF_BLK, D_MODEL), (N_SHARD, 2)),
        **dict(zip(("w_ff_a", "w_ff_b"), _wgrad_pair(
            "dw_ab", da_pre, dbb, x1b, ffj, full_t(half, lambda j, n: (0, n)),
            pl.BlockSpec((None, FF_BLK, half), lambda j, n: (j, 0, n)), (N_SHARD, FF_BLK, D_MODEL), (N_SHARD, 2)))),
        w_o=_wgrad("dw_o", cat, dr1, full_t(half, lambda kk, n: (0, kk)), full_t(half, lambda kk, n: (0, n)),
                   pl.BlockSpec((half, half), lambda kk, n: (kk, n)), (D_MODEL, D_MODEL), (2, 2)))
    dep = early_grads(early)

    do1, do4, do16, dl1, dl4, dl16, duz, dws, dbs, zstat = _mix_bwd(
        dr1, w_o, hu, hz, mixed, attn, ln_z_g, ln_z_b, w_s[0], dep)
    dep = early_grads_sent(duz, (stat3, stat1, zstat, cstat, dws, dbs))
    dqkv = [_attn_bwd(qkv, do, lse, dl, d, dep)
            for qkv, do, lse, dl, d in zip(qkvs, (do1, do4, do16), lses, (dl1, dl4, dl16), DILATIONS)]
    dh, grad_x = _dx_in([g[0] for g in dqkv], [g[1] for g in dqkv], [g[2] for g in dqkv], duz, dr1, w_in,
                        c_tab, s1_tab, s2_tab)
    dep = early_grads_landed(grad_x)
    g_w_in = _wgrad("dw_in", xb, dh, full_t(half, lambda j, kk: (0, kk)), full_t(W_IN_BLK, lambda j, kk: (0, j)),
                    pl.BlockSpec((None, half, W_IN_BLK), lambda j, kk: (j, kk, 0)), (N_SHARD, D_MODEL, W_IN_BLK), (N_SHARD, 2),
                    dep)
    return grad_x, g_w_in


def _tile_rows(rows, mult, steps):
    if rows % mult:
        return rows
    return next(rows // k for k in range(steps, rows + 1) if rows % k == 0 and (rows // k) % mult == 0)


def _grid_spec(grid, in_specs, out_specs):
    return pltpu.PrefetchScalarGridSpec(num_scalar_prefetch=1, grid=grid, in_specs=in_specs, out_specs=out_specs)


def _on_own_steps(i, count, steps, work):
    if count == steps:
        work()
    else:
        pl.when(i < count)(work)


def _place_shards(name, ws, dtypes, place, dep):
    n = len(ws)
    tiles = [_tile_rows(w.shape[0], 16, 8) for w in ws]
    counts = [w.shape[0] // t for w, t in zip(ws, tiles)]
    steps = max(counts)

    def body(s_ref, *refs):
        i = pl.program_id(0)
        for a in range(n):
            def work(a=a):
                refs[n + 1 + a][...] = refs[a][...].astype(dtypes[a])
            _on_own_steps(i, counts[a], steps, work)

    def tile(a, lead):
        last = counts[a] - 1
        if lead:
            return pl.BlockSpec((None, tiles[a], ws[a].shape[1]), lambda i, s: (s[0], jnp.minimum(i, last), 0))
        return pl.BlockSpec((tiles[a], ws[a].shape[1]), lambda i, s: (jnp.minimum(i, last), 0))

    return pl.pallas_call(
        body, name=name,
        grid_spec=_grid_spec((steps,), [tile(a, False) for a in range(n)] + [pl.BlockSpec(memory_space=pl.ANY)],
                             [tile(a, True) for a in range(n)]),
        out_shape=[jax.ShapeDtypeStruct((N_SHARD, *w.shape), dt) for w, dt in zip(ws, dtypes)],
        compiler_params=_cp())(place, *ws, dep)


def _pair_sums(name, mines, gots, place):
    n = len(mines)
    tiles = [_tile_rows(g.shape[1], 16, 2) for g in gots]
    per_blk = [g.shape[1] // t for g, t in zip(gots, tiles)]
    counts = [N_SHARD * nh for nh in per_blk]
    steps = max(counts)

    def body(s_ref, *refs):
        i = pl.program_id(0)
        for a in range(n):
            def work(a=a):
                refs[2 * n + a][...] = (refs[a][...] + refs[n + a][...]).astype(BF16)
            _on_own_steps(i, counts[a], steps, work)

    def tile(a, mine):
        nh, last = per_blk[a], counts[a] - 1

        def index(i, s):
            g = jnp.minimum(i, last)
            return (g // nh, (s[1] * nh if mine else 0) + g % nh, 0)

        return pl.BlockSpec((None, tiles[a], gots[a].shape[2]), index)

    return pl.pallas_call(
        body, name=name,
        grid_spec=_grid_spec((steps,), [tile(a, True) for a in range(n)] + [tile(a, False) for a in range(n)],
                             [tile(a, False) for a in range(n)]),
        out_shape=[jax.ShapeDtypeStruct(g.shape, BF16) for g in gots], compiler_params=_cp())(place, *mines, *gots)


def _chip_sums(name, owns, landeds, place, dep):
    n = len(owns)
    tiles = [_tile_rows(o.shape[1], 16, 8) for o in owns]
    counts = [o.shape[1] // t for o, t in zip(owns, tiles)]
    steps = max(counts)

    def body(s_ref, *refs):
        i = pl.program_id(0)
        for a in range(n):
            def work(a=a):
                own, l1, l2, l3 = (refs[4 * a + k][...].astype(F32) for k in range(4))
                refs[4 * n + 1 + a][...] = ((own + l1) + l2) + l3
            _on_own_steps(i, counts[a], steps, work)

    def slot(a, d):
        last = counts[a] - 1
        return pl.BlockSpec((None, tiles[a], owns[a].shape[2]), lambda i, s: ((s[0] + d) % N_SHARD, jnp.minimum(i, last), 0))

    def out(a):
        nh, last = counts[a], counts[a] - 1
        return pl.BlockSpec((tiles[a], owns[a].shape[2]), lambda i, s: (s[1] * nh + jnp.minimum(i, last), 0))

    operands = [x for o, l in zip(owns, landeds) for x in (o, l, l, l)]
    return pl.pallas_call(
        body, name=name,
        grid_spec=_grid_spec((steps,), [slot(a, d) for a in range(n) for d in range(4)] + [pl.BlockSpec(memory_space=pl.ANY)],
                             [out(a) for a in range(n)]),
        out_shape=[jax.ShapeDtypeStruct((2 * o.shape[1], o.shape[2]), F32) for o in owns],
        compiler_params=_cp())(place, *operands, dep)


def _adamw_math(w, g, m, v):
    m = ADAM_B1 * m + (1.0 - ADAM_B1) * g
    v = ADAM_B2 * v + (1.0 - ADAM_B2) * (g * g)
    m_hat = m / (1.0 - ADAM_B1 ** ADAM_STEP)
    v_hat = v / (1.0 - ADAM_B2 ** ADAM_STEP)
    delta = -ADAM_LR * (m_hat / (jnp.sqrt(v_hat) + ADAM_EPS) + ADAM_WD * w)
    return delta, m, v


def _adamw_shards(name, ws, gs, ms, vs):
    n = len(ws)
    tiles = [_tile_rows(w.shape[1], 8, 8) for w in ws]
    counts = [w.shape[1] // t for w, t in zip(ws, tiles)]
    steps = max(counts)

    def body(*refs):
        i = pl.program_id(0)
        for a in range(n):
            def work(a=a):
                w_ref, g_ref, m_ref, v_ref = refs[4 * a:4 * a + 4]
                d_ref, nm_ref, nv_ref = refs[4 * n + 3 * a:4 * n + 3 * a + 3]
                d_ref[...], nm_ref[...], nv_ref[...] = _adamw_math(w_ref[...], g_ref[...], m_ref[...], v_ref[...])
            _on_own_steps(i, counts[a], steps, work)

    def tile(a, lead):
        last, c = counts[a] - 1, ws[a].shape[2]
        if lead:
            return pl.BlockSpec((None, tiles[a], c), lambda i: (0, jnp.minimum(i, last), 0))
        return pl.BlockSpec((tiles[a], c), lambda i: (jnp.minimum(i, last), 0))

    res = pl.pallas_call(
        body, name=name, grid=(steps,),
        in_specs=[tile(a, lead) for a in range(n) for lead in (True, False, True, True)],
        out_specs=[tile(a, True) for a in range(n) for _ in range(3)],
        out_shape=[jax.ShapeDtypeStruct(w.shape, F32) for w in ws for _ in range(3)],
        compiler_params=_cp())(*[x for quad in zip(ws, gs, ms, vs) for x in quad])
    return [tuple(res[3 * a:3 * a + 3]) for a in range(n)]


MESH = pl.DeviceIdType.MESH
ANY = pl.BlockSpec(memory_space=pl.ANY)


def _place():
    x, y, c = lax.axis_index("x"), lax.axis_index("y"), lax.axis_index("c")
    chips = [(1 - x, y), (x, 1 - y), (1 - x, 1 - y)]
    return x, y, c, 2 * x + y, chips


def _remote(src, dst, send_sem, recv_sem, dev):
    return pltpu.make_async_remote_copy(src_ref=src, dst_ref=dst, send_sem=send_sem, recv_sem=recv_sem,
                                        device_id=dev, device_id_type=MESH)


def _half(ref, hc, rows):
    return ref.at[pl.ds(hc * (rows // 2), rows // 2)]


def _sibling_join(blocks, tag):
    n = len(blocks)

    def body(*refs):
        outs = refs[n:2 * n]
        send, recv = refs[2 * n:]
        x, y, c, _, _ = _place()
        cps = []
        for a in range(n):
            h = blocks[a].shape[0] // 2
            mine = outs[a].at[pl.ds(c * h, h)]
            cp = _remote(mine, mine, send.at[a], recv.at[a], (x, y, 1 - c))
            cp.start()
            cps.append(cp)
        for a, cp in enumerate(cps):
            h = blocks[a].shape[0] // 2
            theirs = outs[a].at[pl.ds((1 - c) * h, h)]
            _remote(theirs, theirs, send.at[a], recv.at[a], (x, y, 1 - c)).wait_recv()
            cp.wait_send()

    sem = pltpu.SemaphoreType.DMA
    return pl.pallas_call(body, name=f"rs_sibling_join_{tag}", in_specs=[ANY] * n, out_specs=[ANY] * n,
                          out_shape=[jax.ShapeDtypeStruct(b_.shape, b_.dtype) for b_ in blocks],
                          input_output_aliases={a: a for a in range(n)},
                          scratch_shapes=[sem((n,)), sem((n,))])(*blocks)


def _join_start(blocks, after, tag):
    n = len(blocks)

    def body(*refs):
        ins = refs[:n]
        send, recv = refs[n + 1], refs[n + 2]
        token = refs[2 * n + 3]
        x, y, c, _, _ = _place()
        for a in range(n):
            h = blocks[a].shape[0] // 2
            mine = ins[a].at[pl.ds(c * h, h)]
            _remote(mine, mine, send.at[a], recv.at[a], (x, y, 1 - c)).start()
        token[...] = jnp.zeros_like(token)

    sems = pltpu.SemaphoreType.DMA((n,))
    res = pl.pallas_call(
        body, name=f"join_start_{tag}", in_specs=[HBM] * n + [ANY],
        out_specs=[SEM, SEM] + [HBM] * n + [pl.BlockSpec(memory_space=pltpu.VMEM)],
        out_shape=[sems, sems] + [pltpu.HBM(b_.shape, b_.dtype) for b_ in blocks] + [TOKEN],
        input_output_aliases={a: a + 2 for a in range(n)}, compiler_params=_in_flight_params(),
    )(*[_in_hbm(b_) for b_ in blocks], after)
    return res[0], res[1], res[2:2 + n], res[2 + n]


def _join_wait(send, recv, blocks, after, tag):
    n = len(blocks)

    def body(*refs):
        ins = refs[:n]
        send_ref, recv_ref = refs[n], refs[n + 1]
        x, y, c, _, _ = _place()
        for a in range(n):
            h = blocks[a].shape[0] // 2
            mine, theirs = ins[a].at[pl.ds(c * h, h)], ins[a].at[pl.ds((1 - c) * h, h)]
            _remote(mine, mine, send_ref.at[a], recv_ref.at[a], (x, y, 1 - c)).wait_send()
            _remote(theirs, theirs, send_ref.at[a], recv_ref.at[a], (x, y, 1 - c)).wait_recv()

    return pl.pallas_call(
        body, name=f"join_wait_{tag}", in_specs=[HBM] * n + [SEM, SEM, ANY], out_specs=[HBM] * n,
        out_shape=[pltpu.HBM(b_.shape, b_.dtype) for b_ in blocks],
        input_output_aliases={a: a for a in range(n)}, compiler_params=_in_flight_params(),
    )(*blocks, send, recv, after)


HBM = pl.BlockSpec(memory_space=pltpu.HBM)
SEM = pl.BlockSpec(memory_space=pltpu.SEMAPHORE)
TOKEN = jax.ShapeDtypeStruct((8, 128), F32)


def _in_flight_params():
    return pltpu.CompilerParams(has_side_effects=pltpu.SideEffectType.DATAFLOW_SIDE_EFFECTING)


def _in_hbm(a):
    return pltpu.with_memory_space_constraint(a, pltpu.HBM)


def _gather_piece(ref, rows, split, slot, hc):
    return _half(ref.at[slot], hc, rows) if split else ref.at[slot]


def _gather_start(stacks, split, after, tag):
    n = len(stacks)

    def body(*refs):
        ins = refs[:n]
        send, recv = refs[n + 1], refs[n + 2]
        token = refs[2 * n + 3]
        _, _, c, j, chips = _place()
        for a in range(n):
            mine = _gather_piece(ins[a], stacks[a].shape[1], split[a], j, c)
            for t in range(3):
                _remote(mine, mine, send.at[3 * a + t], recv.at[3 * a + t], (*chips[t], c)).start()
        token[...] = jnp.zeros_like(token)

    sems = pltpu.SemaphoreType.DMA((3 * n,))
    res = pl.pallas_call(
        body, name=f"gather_start_{tag}", in_specs=[HBM] * n + [ANY],
        out_specs=[SEM, SEM] + [HBM] * n + [pl.BlockSpec(memory_space=pltpu.VMEM)],
        out_shape=[sems, sems] + [pltpu.HBM(s.shape, s.dtype) for s in stacks] + [TOKEN],
        input_output_aliases={a: a + 2 for a in range(n)}, compiler_params=_in_flight_params(),
    )(*[_in_hbm(s) for s in stacks], after)
    return res[0], res[1], res[2:2 + n], res[2 + n]


def _gather_wait(send, recv, stacks, split, after, tag):
    n = len(stacks)

    def body(*refs):
        ins = refs[:n]
        send_ref, recv_ref = refs[n], refs[n + 1]
        _, _, c, j, chips = _place()
        for a in range(n):
            rows = stacks[a].shape[1]
            mine = _gather_piece(ins[a], rows, split[a], j, c)
            for t, (px, py) in enumerate(chips):
                theirs = _gather_piece(ins[a], rows, split[a], 2 * px + py, c)
                _remote(mine, mine, send_ref.at[3 * a + t], recv_ref.at[3 * a + t], (px, py, c)).wait_send()
                _remote(theirs, theirs, send_ref.at[3 * a + t], recv_ref.at[3 * a + t], (px, py, c)).wait_recv()

    return pl.pallas_call(
        body, name=f"gather_wait_{tag}", in_specs=[HBM] * n + [SEM, SEM, ANY], out_specs=[HBM] * n,
        out_shape=[pltpu.HBM(s.shape, s.dtype) for s in stacks],
        input_output_aliases={a: a for a in range(n)}, compiler_params=_in_flight_params(),
    )(*stacks, send, recv, after)


def _gather_forward(stacks, split, tag):
    idx = [a for a in range(len(stacks)) if split[a]]
    n = len(idx)

    def body(*refs):
        outs = refs[n:2 * n]
        send, recv = refs[2 * n:]
        x, y, c, _, chips = _place()
        sends = []
        for t, (px, py) in enumerate(chips):
            for a in range(n):
                blk = _half(outs[a].at[2 * px + py], c, stacks[idx[a]].shape[1])
                cp = _remote(blk, blk, send.at[a, t], recv.at[a, t], (x, y, 1 - c))
                cp.start()
                sends.append(cp)
        for t, (px, py) in enumerate(chips):
            for a in range(n):
                blk = _half(outs[a].at[2 * px + py], 1 - c, stacks[idx[a]].shape[1])
                _remote(blk, blk, send.at[a, t], recv.at[a, t], (x, y, 1 - c)).wait_recv()
        for cp in sends:
            cp.wait_send()

    sem = pltpu.SemaphoreType.DMA
    res = pl.pallas_call(
        body, name=f"gather_forward_{tag}", in_specs=[ANY] * n, out_specs=[ANY] * n,
        out_shape=[jax.ShapeDtypeStruct(stacks[a].shape, stacks[a].dtype) for a in idx],
        input_output_aliases={a: a for a in range(n)}, scratch_shapes=[sem((n, 3)), sem((n, 3))],
    )(*[stacks[a] for a in idx])
    out = list(stacks)
    for a, r in zip(idx, res):
        out[a] = r
    return out


def _forward_start(stacks, after, tag):
    n = len(stacks)

    def body(*refs):
        ins = refs[:n]
        send, recv = refs[n + 1], refs[n + 2]
        token = refs[2 * n + 3]
        x, y, c, _, chips = _place()
        for a in range(n):
            for t, (px, py) in enumerate(chips):
                blk = _half(ins[a].at[2 * px + py], c, stacks[a].shape[1])
                _remote(blk, blk, send.at[3 * a + t], recv.at[3 * a + t], (x, y, 1 - c)).start()
        token[...] = jnp.zeros_like(token)

    sems = pltpu.SemaphoreType.DMA((3 * n,))
    res = pl.pallas_call(
        body, name=f"forward_start_{tag}", in_specs=[HBM] * n + [ANY],
        out_specs=[SEM, SEM] + [HBM] * n + [pl.BlockSpec(memory_space=pltpu.VMEM)],
        out_shape=[sems, sems] + [pltpu.HBM(s.shape, s.dtype) for s in stacks] + [TOKEN],
        input_output_aliases={a: a + 2 for a in range(n)}, compiler_params=_in_flight_params(),
    )(*[_in_hbm(s) for s in stacks], after)
    return res[0], res[1], res[2:2 + n], res[2 + n]


def _forward_wait(send, recv, stacks, after, tag):
    n = len(stacks)

    def body(*refs):
        ins = refs[:n]
        send_ref, recv_ref = refs[n], refs[n + 1]
        x, y, c, _, chips = _place()
        for a in range(n):
            for t, (px, py) in enumerate(chips):
                mine = _half(ins[a].at[2 * px + py], c, stacks[a].shape[1])
                theirs = _half(ins[a].at[2 * px + py], 1 - c, stacks[a].shape[1])
                _remote(mine, mine, send_ref.at[3 * a + t], recv_ref.at[3 * a + t], (x, y, 1 - c)).wait_send()
                _remote(theirs, theirs, send_ref.at[3 * a + t], recv_ref.at[3 * a + t], (x, y, 1 - c)).wait_recv()

    return pl.pallas_call(
        body, name=f"forward_wait_{tag}", in_specs=[HBM] * n + [SEM, SEM, ANY], out_specs=[HBM] * n,
        out_shape=[pltpu.HBM(s.shape, s.dtype) for s in stacks],
        input_output_aliases={a: a for a in range(n)}, compiler_params=_in_flight_params(),
    )(*stacks, send, recv, after)


def _swap_start(grads, tag):
    n = len(grads)

    def body(*refs):
        ins, gots = refs[:n], refs[n:2 * n]
        send, recv = refs[2 * n], refs[2 * n + 1]
        token = refs[4 * n + 2]
        x, y, c, _, _ = _place()
        for a in range(n):
            h = grads[a].shape[1] // 2
            _remote(ins[a].at[:, pl.ds((1 - c) * h, h)], gots[a], send.at[a], recv.at[a], (x, y, 1 - c)).start()
        token[...] = jnp.zeros_like(token)

    sems = pltpu.SemaphoreType.DMA((n,))
    halves = [(g.shape[0], g.shape[1] // 2, g.shape[2]) for g in grads]
    res = pl.pallas_call(
        body, name=f"swap_start_{tag}", in_specs=[HBM] * (2 * n),
        out_specs=[SEM, SEM] + [HBM] * (2 * n) + [pl.BlockSpec(memory_space=pltpu.VMEM)],
        out_shape=[sems, sems] + [pltpu.HBM(g.shape, g.dtype) for g in grads] + [pltpu.HBM(s, F32) for s in halves] + [TOKEN],
        input_output_aliases={a: a + 2 for a in range(2 * n)}, compiler_params=_in_flight_params(),
    )(*[_in_hbm(g) for g in grads], *[_in_hbm(lax.empty(s, F32)) for s in halves])
    return res[0], res[1], res[2:2 + n], res[2 + n:2 + 2 * n], res[2 + 2 * n]


def _swap_wait(send, recv, grads, gots, after, tag):
    n = len(grads)

    def body(*refs):
        ins, lnd = refs[:n], refs[n:2 * n]
        send_ref, recv_ref = refs[2 * n], refs[2 * n + 1]
        x, y, c, _, _ = _place()
        for a in range(n):
            h = grads[a].shape[1] // 2
            cp = _remote(ins[a].at[:, pl.ds((1 - c) * h, h)], lnd[a], send_ref.at[a], recv_ref.at[a], (x, y, 1 - c))
            cp.wait_send()
            cp.wait_recv()

    bufs = [pltpu.HBM(g.shape, g.dtype) for g in grads] + [pltpu.HBM(g.shape, g.dtype) for g in gots]
    res = pl.pallas_call(
        body, name=f"swap_wait_{tag}", in_specs=[HBM] * (2 * n) + [SEM, SEM, ANY], out_specs=[HBM] * (2 * n),
        out_shape=bufs, input_output_aliases={a: a for a in range(2 * n)}, compiler_params=_in_flight_params(),
    )(*grads, *gots, send, recv, after)
    return res[:n], res[n:]


def _exchange_start(parts, tag):
    n = len(parts)

    def body(*refs):
        ins, lands = refs[:n], refs[n:2 * n]
        send, recv = refs[2 * n], refs[2 * n + 1]
        token = refs[4 * n + 2]
        _, _, c, j, chips = _place()
        for t, (px, py) in enumerate(chips):
            for a in range(n):
                _remote(ins[a].at[2 * px + py], lands[a].at[j], send.at[3 * a + t], recv.at[3 * a + t], (px, py, c)).start()
        token[...] = jnp.zeros_like(token)

    sems = pltpu.SemaphoreType.DMA((3 * n,))
    bufs = [pltpu.HBM(p.shape, p.dtype) for p in parts]
    res = pl.pallas_call(
        body, name=f"exchange_start_{tag}", in_specs=[HBM] * (2 * n),
        out_specs=[SEM, SEM] + [HBM] * (2 * n) + [pl.BlockSpec(memory_space=pltpu.VMEM)],
        out_shape=[sems, sems] + bufs + bufs + [TOKEN],
        input_output_aliases={a: a + 2 for a in range(2 * n)}, compiler_params=_in_flight_params(),
    )(*[_in_hbm(p) for p in parts], *[_in_hbm(lax.empty(p.shape, p.dtype)) for p in parts])
    return res[0], res[1], res[2:2 + n], res[2 + n:2 + 2 * n], res[2 + 2 * n]


def _exchange_wait(send, recv, parts, lands, after, tag):
    n = len(parts)

    def body(*refs):
        ins, lnd = refs[:n], refs[n:2 * n]
        send_ref, recv_ref = refs[2 * n], refs[2 * n + 1]
        _, _, c, j, chips = _place()
        for t, (px, py) in enumerate(chips):
            jt = 2 * px + py
            for a in range(n):
                _remote(ins[a].at[jt], lnd[a].at[j], send_ref.at[3 * a + t], recv_ref.at[3 * a + t], (px, py, c)).wait_send()
                _remote(ins[a].at[jt], lnd[a].at[jt], send_ref.at[3 * a + t], recv_ref.at[3 * a + t], (px, py, c)).wait_recv()

    bufs = [pltpu.HBM(p.shape, p.dtype) for p in parts]
    res = pl.pallas_call(
        body, name=f"exchange_wait_{tag}", in_specs=[HBM] * (2 * n) + [SEM, SEM, ANY], out_specs=[HBM] * (2 * n),
        out_shape=bufs + bufs, input_output_aliases={a: a for a in range(2 * n)}, compiler_params=_in_flight_params(),
    )(*parts, *lands, send, recv, after)
    return res[:n], res[n:]


def _small_chip_sums(arrs):
    n = len(arrs)

    def body(*refs):
        ins, outs = refs[:n], refs[n:2 * n]
        sib = refs[2 * n:3 * n]
        send, recv = refs[3 * n:]
        x, y, c, j, _ = _place()
        swaps = [_remote(ins[a], sib[a], send.at[a], recv.at[a], (x, y, 1 - c)) for a in range(n)]
        for cp in swaps:
            cp.start()
        for a in range(n):
            swaps[a].wait_recv()
            outs[a][j] = ins[a][...] + sib[a][...]
        for cp in swaps:
            cp.wait_send()

    sem = pltpu.SemaphoreType.DMA
    vm = pl.BlockSpec(memory_space=pltpu.VMEM)
    return pl.pallas_call(
        body, name="small_chip_sums", in_specs=[vm] * n, out_specs=[vm] * n,
        out_shape=[jax.ShapeDtypeStruct((N_SHARD, *a.shape), F32) for a in arrs],
        scratch_shapes=[pltpu.VMEM(a.shape, F32) for a in arrs] + [sem((n,)), sem((n,))],
        compiler_params=_cp(),
    )(*arrs)


def _small_totals(stacks):
    n = len(stacks)

    def body(*refs):
        for a in range(n):
            refs[n + a][...] = ((refs[a][0] + refs[a][1]) + refs[a][2]) + refs[a][3]

    return pl.pallas_call(body, name="small_totals", out_shape=[jax.ShapeDtypeStruct(s.shape[1:], F32) for s in stacks],
                          compiler_params=_cp())(*stacks)


SMALL_1024 = ("ln1_g", "ln1_b", "ln2_g", "ln2_b", "b_ple_gate", "ln3_g", "ln3_b")


def _adamw_small(red3, red1, redz, g_conv_w, redc, red_ws, red_bs, params):
    shape2d = {"ln_z_g": (1, D_GMLP), "ln_z_b": (1, D_GMLP), "w_s": (N_HEADS * BLK, BLK), "b_s": (N_HEADS, BLK),
               "conv_w": (3, FF_BLK), "conv_b": (N_SHARD, FF_BLK), **{k: (1, D_MODEL) for k in SMALL_1024}}
    names = list(shape2d)
    flat = [a.reshape(shape2d[k]) for k in names for a in params[k]]

    def body(r3, r1, rz, gcw, rc, rws, rbs, *refs):
        ins, outs = refs[:3 * len(names)], refs[3 * len(names):]

        def grad_of(k):
            if k == "w_s":
                return rws[...]
            if k == "b_s":
                return rbs[...]
            if k == "conv_w":
                return gcw[0:3, :]
            if k == "conv_b":
                return jnp.concatenate([rc[j * STAT_ROWS + 3:j * STAT_ROWS + 4, :] for j in range(N_SHARD)], axis=0)
            src, row = {"ln3_g": (r3, 0), "ln3_b": (r3, 1), "b_ple_gate": (r3, 2), "ln2_g": (r3, 3), "ln2_b": (r3, 4),
                        "ln1_g": (r1, 0), "ln1_b": (r1, 1), "ln_z_g": (rz, 0), "ln_z_b": (rz, 1)}[k]
            return src[row:row + 1, :]

        for i, k in enumerate(names):
            w_ref, m_ref, v_ref = ins[3 * i:3 * i + 3]
            g_ref, d_ref, nm_ref, nv_ref = outs[4 * i:4 * i + 4]
            g = grad_of(k)
            g_ref[...] = g
            d_ref[...], nm_ref[...], nv_ref[...] = _adamw_math(w_ref[...], g, m_ref[...], v_ref[...])

    res = pl.pallas_call(
        body, name="adamw_small",
        out_shape=[jax.ShapeDtypeStruct(shape2d[k], F32) for k in names for _ in range(4)],
        compiler_params=_cp(),
    )(red3, red1, redz, g_conv_w, redc, red_ws, red_bs, *flat)
    return {k: tuple(r.reshape(params[k][0].shape) for r in res[4 * i:4 * i + 4]) for i, k in enumerate(names)}


WEIGHTS = ("w_in", "ln_z_g", "ln_z_b", "w_s", "b_s", "w_o", "ln1_g", "ln1_b", "w_ff_a", "w_ff_b", "conv_w", "conv_b",
           "w_ff_down", "ln2_g", "ln2_b", "w_ple_gate", "b_ple_gate", "w_ple_in", "ln3_g", "ln3_b")
BIG = ("w_in", "w_o", "w_ff_a", "w_ff_b", "w_ff_down", "w_ple_gate", "w_ple_in")
TRANSPOSED = ("w_ff_a", "w_ff_b")
LATE = ("w_o", "w_ff_a", "w_ff_b", "w_ff_down", "w_ple_gate", "w_ple_in", "conv_w")


def kernel(x, p, positions, w_in, ln_z_g, ln_z_b, w_s, b_s, w_o, ln1_g, ln1_b, w_ff_a, w_ff_b, conv_w, conv_b, w_ff_down, ln2_g, ln2_b, w_ple_gate, b_ple_gate, w_ple_in, ln3_g, ln3_b, loss_target, m_w_in, m_ln_z_g, m_ln_z_b, m_w_s, m_b_s, m_w_o, m_ln1_g, m_ln1_b, m_w_ff_a, m_w_ff_b, m_conv_w, m_conv_b, m_w_ff_down, m_ln2_g, m_ln2_b, m_w_ple_gate, m_b_ple_gate, m_w_ple_in, m_ln3_g, m_ln3_b, v_w_in, v_ln_z_g, v_ln_z_b, v_w_s, v_b_s, v_w_o, v_ln1_g, v_ln1_b, v_w_ff_a, v_w_ff_b, v_conv_w, v_conv_b, v_w_ff_down, v_ln2_g, v_ln2_b, v_w_ple_gate, v_b_ple_gate, v_w_ple_in, v_ln3_g, v_ln3_b):
    args = locals()
    w = {k: args[k] for k in WEIGHTS}
    m = {k: args["m_" + k] for k in WEIGHTS}
    v = {k: args["v_" + k] for k in WEIGHTS}

    for k in TRANSPOSED:
        w[k], m[k], v[k] = (jnp.swapaxes(a, 1, 2) for a in (w[k], m[k], v[k]))

    chip = 2 * lax.axis_index("x") + lax.axis_index("y")
    place = jnp.stack([chip, lax.axis_index("c")]).astype(jnp.int32)
    stack = dict(zip(["w_in"], _place_shards("cast_w_in", [w["w_in"][0]], [MXU], place, place)))
    i_send, i_recv, in_flight, dep = _gather_start([stack["w_in"]], [True], place, "w_in")
    stack.update(zip(LATE, _place_shards("cast_late", [w[k][0] for k in LATE],
                                         [F32 if k == "conv_w" else MXU for k in LATE], place, dep)))
    split_late = [k != "conv_w" for k in LATE]
    g_send, g_recv, late_flight, start_dep = _gather_start([stack[k] for k in LATE], split_late, place, "late")
    rope = _rope_tables(positions, x.shape[1], start_dep)
    landed_in = _gather_wait(i_send, i_recv, in_flight, [True], rope[0], "w_in")
    w_in_full, = _gather_forward(landed_in, [True], "w_in")
    halves =[k for k, sp in zip(LATE, split_late) if sp]
    trips = {}

    def late_landed(after):
        fw = dict(zip(LATE, _gather_wait(g_send, g_recv, late_flight, split_late, after, "late")))
        trips["late"] = (fw, *_forward_start([fw[k] for k in halves], fw["conv_w"], "late"))
        return trips["late"][-1]

    def late_weights(after):
        fw, send, recv, flight, _ = trips["late"]
        fw.update(zip(halves, _forward_wait(send, recv, flight, after, "late")))
        return (fw["w_o"].reshape(D_MODEL, D_MODEL), fw["w_ff_a"], fw["w_ff_b"], fw["conv_w"], fw["w_ff_down"],
                fw["w_ple_gate"].reshape(D_MODEL, D_MODEL), fw["w_ple_in"])

    def swap_started(names, grads, tag):
        stacked = [g.reshape(N_SHARD, *w[k].shape[1:]) for k, g in zip(names, grads)]
        return (names, tag, *_swap_start(stacked, tag))

    def partial_sums(swap, after):
        names, tag, send, recv, stacked, gots, _ = swap
        stacked, got = _swap_wait(send, recv, stacked, gots, after, tag)
        pair = _pair_sums(f"rs_pair_{tag}", stacked, got, place)
        return (names, tag, *_exchange_start(pair, tag))

    def chip_summed(trip, after, dep):
        names, tag, send, recv, pair, lands, _ = trip
        pair, landed = _exchange_wait(send, recv, pair, lands, after, tag)
        return _chip_sums(f"rs_sum_{tag}", pair, landed, place, dep), names, tag

    def reduced(trip, after, dep):
        blocks, names, tag = chip_summed(trip, after, dep)
        return dict(zip(names, _sibling_join(blocks, tag)))

    def early_grads_landed(after):
        blocks, names, tag = chip_summed(trips["early"], after, trips["small"][-1])
        trips["join"] = (names, *_join_start(blocks, after, tag))
        return trips["join"][-1]

    def early_grads(grads):
        trips["swap"] = swap_started(list(grads), list(grads.values()), "early")
        return trips["swap"][-1]

    def early_grads_sent(after, small):
        trips["early"] = partial_sums(trips["swap"], after)
        stat3, stat1, zstat, cstat, dws, dbs = small
        sums = _small_chip_sums([stat3, stat1, zstat, cstat.reshape(N_SHARD * STAT_ROWS, FF_BLK),
                                 dws.reshape(N_HEADS * BLK, BLK), dbs])
        trips["small"] = _gather_start(sums, [False] * len(sums), trips["early"][-1], "small")
        return trips["small"][-1]

    grad_x, g_w_in = _local_step(
        x[0], p[0, 0], rope, loss_target[0], w_in_full, start_dep, late_landed, late_weights, early_grads, early_grads_sent,
        early_grads_landed, ln_z_g, ln_z_b, w_s, b_s, ln1_g, ln1_b, conv_b, ln2_g, ln2_b, b_ple_gate, ln3_g, ln3_b)

    trips["w_in"] = partial_sums(swap_started(["w_in"], [g_w_in], "w_in"), g_w_in)
    out = {}

    def adamw(red, tag):
        names = list(red)
        steps = _adamw_shards(f"adamw_{tag}", [w[k] for k in names], [red[k] for k in names], [m[k] for k in names],
                              [v[k] for k in names])
        for k, (d, nm, nv) in zip(names, steps):
            out[k] = (red[k].reshape(w[k].shape), d, nm, nv)

    names, j_send, j_recv, j_flight, _ = trips["join"]
    adamw(dict(zip(names, _join_wait(j_send, j_recv, j_flight, trips["w_in"][-1], "early"))), "early")
    adamw(reduced(trips["w_in"], out["w_o"][3], start_dep), "w_in")
    for k in TRANSPOSED:
        out[k] = tuple(jnp.swapaxes(a, 1, 2) for a in out[k])

    s_send, s_recv, s_flight, _ = trips["small"]
    red3, red1, redz, redc, red_ws, red_bs = _small_totals(
        _gather_wait(s_send, s_recv, s_flight, [False] * len(s_flight), out["w_in"][3], "small"))
    loss = (0.5 / D_MODEL) * jnp.sum(red3[5])
    g_conv_w = lax.dynamic_slice_in_dim(redc, chip * STAT_ROWS, STAT_ROWS, 0)
    names_small = [k for k in WEIGHTS if k not in BIG]
    out.update(_adamw_small(red3, red1, redz, g_conv_w, redc, red_ws, red_bs, {k: (w[k], m[k], v[k]) for k in names_small}))

    return (loss, grad_x[None], *[out[k][0] for k in WEIGHTS], *[out[k][1] for k in WEIGHTS],
            *[out[k][2] for k in WEIGHTS], *[out[k][3] for k in WEIGHTS])
```

```python
import math

import numpy as np
import jax
import jax.numpy as jnp
from jax import lax
from jax.experimental import pallas as pl
from jax.experimental.pallas import tpu as pltpu

F32 = jnp.float32
BF16 = jnp.bfloat16
MXU = BF16

D_MODEL = 1024
HEAD_DIM = 64
N_HEADS = 8
D_ATTN = 512
D_GMLP = 512
D_IN = 2560
DILATIONS = (1, 4, 16)
BLK = 128
ROPE_THETA = 500000.0
ROPE_DIM = 16
D_FF = 2816
D_PLE = 256
LN_EPS = 1e-5
ALPHA = 2.0 ** 0.25
NEG_INF = -1e30
N_SHARD = 4
W_IN_BLK = D_IN // N_SHARD
FF_BLK = D_FF // N_SHARD
ROW_BLK = D_MODEL // N_SHARD
ADAM_LR, ADAM_B1, ADAM_B2, ADAM_EPS, ADAM_WD, ADAM_STEP = 0.001, 0.9, 0.999, 1e-08, 0.01, 10

TM = 512
HALO = 8
ROW_GROUPS = 2
VMEM_LIMIT = 56 * 1024 * 1024


def _cp(**kw):
    return pltpu.CompilerParams(vmem_limit_bytes=VMEM_LIMIT, **kw)


def _full(shape):
    n = len(shape)
    return pl.BlockSpec(shape, lambda *_: (0,) * n)


def _gelu(x):
    return 0.5 * x * (1.0 + lax.erf(x * (1.0 / math.sqrt(2.0))))


def _gelu_grad(x):
    return 0.5 * (1.0 + lax.erf(x * (1.0 / math.sqrt(2.0)))) + x * jnp.exp(-0.5 * x * x) * (1.0 / math.sqrt(2.0 * math.pi))


def _ln_fwd(r):
    mu = jnp.mean(r, axis=-1, keepdims=True)
    xc = r - mu
    var = jnp.mean(xc * xc, axis=-1, keepdims=True)
    rstd = lax.rsqrt(var + LN_EPS)
    return xc * rstd, rstd


def _ln_bwd(dy, xhat, rstd, g):
    dxh = dy * g
    m1 = jnp.mean(dxh, axis=-1, keepdims=True)
    m2 = jnp.mean(dxh * xhat, axis=-1, keepdims=True)
    return rstd * (dxh - m1 - xhat * m2)


def _dot(a, b):
    return jnp.dot(a.astype(MXU), b.astype(MXU), preferred_element_type=F32)


def _dot_nt(a, b):
    return lax.dot_general(a.astype(MXU), b.astype(MXU), (((1,), (1,)), ((), ())), preferred_element_type=F32)


def _dot_tn(a, b):
    return lax.dot_general(a.astype(MXU), b.astype(MXU), (((0,), (0,)), ((), ())), preferred_element_type=F32)


def _colsum(v):
    return jnp.sum(v, axis=0, keepdims=True)


def _rope_tables(positions, t, dep):
    inv = np.float32(ROPE_THETA) ** (-np.arange(0, ROPE_DIM, 2, dtype=np.float32) / np.float32(ROPE_DIM))
    half = ROPE_DIM // 2
    pos_rep = jnp.repeat(positions.reshape(t // 16, 16), half, axis=1)
    inv_row = jnp.asarray(np.tile(inv, 16)[None, :], F32)

    def trig_body(pos_ref, inv_ref, dep_ref, cos_ref, sin_ref):
        ang = pos_ref[...].astype(F32) * inv_ref[...]
        cos_ref[...] = jnp.cos(ang)
        sin_ref[...] = jnp.sin(ang)

    vm = pl.BlockSpec(memory_space=pltpu.VMEM)
    cos8, sin8 = pl.pallas_call(
        trig_body, name="rope_trig", in_specs=[vm, vm, pl.BlockSpec(memory_space=pl.ANY)], out_specs=[vm, vm],
        out_shape=(jax.ShapeDtypeStruct((t // 16, 128), F32), jax.ShapeDtypeStruct((t // 16, 128), F32)),
    )(pos_rep, inv_row, dep)
    cos8 = cos8.reshape(t, half)
    sin8 = sin8.reshape(t, half)

    lane = np.arange(128) % HEAD_DIM
    sel = (np.arange(half)[:, None] == (lane % half)[None, :])
    e_cos = (sel & (lane < ROPE_DIM)[None, :]).astype(np.float32)
    e_s1 = -(sel & (lane < half)[None, :]).astype(np.float32)
    e_s2 = (sel & ((lane >= half) & (lane < ROPE_DIM))[None, :]).astype(np.float32)
    ones = (lane >= ROPE_DIM).astype(np.float32)[None, :]

    def expand_body(cos_ref, sin_ref, ec_ref, e1_ref, e2_ref, ones_ref, c_ref, s1_ref, s2_ref):
        hp = lax.Precision.HIGHEST
        c_ref[...] = jnp.dot(cos_ref[...], ec_ref[...], precision=hp, preferred_element_type=F32) + ones_ref[...]
        s1_ref[...] = jnp.dot(sin_ref[...], e1_ref[...], precision=hp, preferred_element_type=F32)
        s2_ref[...] = jnp.dot(sin_ref[...], e2_ref[...], precision=hp, preferred_element_type=F32)

    tab = jax.ShapeDtypeStruct((t, 128), F32)
    return pl.pallas_call(expand_body, name="rope_expand", out_shape=(tab, tab, tab), compiler_params=_cp())(
        cos8, sin8, jnp.asarray(e_cos), jnp.asarray(e_s1), jnp.asarray(e_s2), jnp.asarray(ones))


def _tile_heads(tab):
    return jnp.concatenate([tab] * (D_ATTN // 128), axis=1)


def _rope_apply(v, c, s1, s2):
    n = v.shape[1]
    half = ROPE_DIM // 2
    return v * c + pltpu.roll(v, n - half, 1) * s1 + pltpu.roll(v, half, 1) * s2


def _rope_apply_t(g, c, s1, s2):
    n = g.shape[1]
    half = ROPE_DIM // 2
    return g * c + pltpu.roll(g * s1, half, 1) + pltpu.roll(g * s2, n - half, 1)


LANE_CHUNKS = D_ATTN // 128
HEAD_LANES = 128 // N_HEADS


def _perm_shape(t, d, w, dtype):
    return jax.ShapeDtypeStruct((d, t // d, w), dtype)


def _perm_tile(d, w):
    return pl.BlockSpec((None if d == 1 else d, TM // d, w), lambda i: (0, i, 0))


def _to_planes(ref, scr, d, n_chunks, dtype):
    for r in range(d):
        for cc in range(n_chunks):
            ref[r, :, cc * 128:(cc + 1) * 128] = scr.at[cc][pl.ds(r, TM // d, stride=d), :].astype(dtype)


def _from_planes(ref, scr, d, n_chunks, accumulate=False):
    for r in range(d):
        for cc in range(n_chunks):
            rows = scr.at[cc]
            val = ref[r, :, cc * 128:(cc + 1) * 128].astype(F32)
            if accumulate:
                rows[pl.ds(r, TM // d, stride=d), :] += val
            else:
                rows[pl.ds(r, TM // d, stride=d), :] = val


def _chunks(val):
    return [val[:, cc * 128:(cc + 1) * 128] for cc in range(val.shape[1] // 128)]


def _unchunk(scr, n_chunks, base=0):
    return jnp.concatenate([scr[base + cc] for cc in range(n_chunks)], axis=1)


def _head_expand():
    src = np.arange(128)[:, None]
    dst = np.arange(D_ATTN)[None, :]
    return jnp.asarray((src == (dst // HEAD_DIM) * HEAD_LANES).astype(np.float32))


def _head_reduce():
    src = np.arange(D_ATTN)[:, None]
    dst = np.arange(128)[None, :]
    return jnp.asarray((src // HEAD_DIM == dst // HEAD_LANES).astype(np.float32))


def _dot_select(a, sel):
    hi = a.astype(BF16)
    lo = (a - hi.astype(F32)).astype(BF16)
    sel = sel.astype(BF16)
    return jnp.dot(hi, sel, preferred_element_type=F32) + jnp.dot(lo, sel, preferred_element_type=F32)


def _qkvuz(x, w_in, c_tab, s1_tab, s2_tab, ln_z_g, ln_z_b, w_s, b_full, dep):
    t = x.shape[0]
    nchunk = TM // BLK

    def body(x_ref, w_ref, c_ref, s1_ref, s2_ref, g_ref, b_ref, ws_ref, bf_ref, dep_ref,
             qkv1_ref, qkv4_ref, qkv16_ref, hu_ref, hz_ref, mixed_ref, gm_ref, xb_ref, h_scr, wm_scr, p_scr):
        @pl.when(pl.program_id(0) == 0)
        def _():
            row = lax.broadcasted_iota(jnp.int32, (BLK, BLK), 0)
            col = lax.broadcasted_iota(jnp.int32, (BLK, BLK), 1)
            for g in range(N_HEADS):
                wm_scr[g] = jnp.where(col <= row, ws_ref[g], 0.0).astype(MXU)

        xb = x_ref[...].astype(MXU)
        xb_ref[...] = xb
        for j in range(N_SHARD):
            h_scr[:, j * W_IN_BLK:(j + 1) * W_IN_BLK] = jnp.dot(xb, w_ref[j], preferred_element_type=F32)
        c, s1, s2 = _tile_heads(c_ref[...]), _tile_heads(s1_ref[...]), _tile_heads(s2_ref[...])
        q = _rope_apply(h_scr[:, 0:D_ATTN], c, s1, s2) * (1.0 / math.sqrt(HEAD_DIM))
        k = _rope_apply(h_scr[:, D_ATTN:2 * D_ATTN], c, s1, s2)
        for part, val in enumerate((q, k, h_scr[:, 2 * D_ATTN:3 * D_ATTN])):
            qkv1_ref[:, part * D_ATTN:(part + 1) * D_ATTN] = val.astype(MXU)
            for cc in range(LANE_CHUNKS):
                p_scr[part * LANE_CHUNKS + cc] = val[:, cc * 128:(cc + 1) * 128]
        _to_planes(qkv4_ref, p_scr, DILATIONS[1], 3 * LANE_CHUNKS, MXU)
        _to_planes(qkv16_ref, p_scr, DILATIONS[2], 3 * LANE_CHUNKS, MXU)
        hu = h_scr[:, 3 * D_ATTN:3 * D_ATTN + D_GMLP]
        hz = h_scr[:, 3 * D_ATTN + D_GMLP:]
        hu_ref[...] = hu
        hz_ref[...] = hz
        zhat, _ = _ln_fwd(_gelu(hz))
        zn = (zhat * g_ref[...] + b_ref[...]).astype(MXU)
        for ch in range(nchunk):
            rows = slice(ch * BLK, (ch + 1) * BLK)
            for g in range(N_HEADS):
                cols = slice(g * HEAD_DIM, (g + 1) * HEAD_DIM)
                mixed_ref[rows, cols] = jnp.dot(wm_scr[g], zn[rows, cols], preferred_element_type=F32) + bf_ref[:, cols]
        gm_ref[...] = (_gelu(hu) * mixed_ref[...]).astype(MXU)

    tok = lambda w: pl.BlockSpec((TM, w), lambda i: (i, 0))
    outs = [_perm_shape(t, d, 3 * D_ATTN, MXU) for d in DILATIONS] + [jax.ShapeDtypeStruct((t, D_GMLP), F32)] * 3 + [
        jax.ShapeDtypeStruct((t, D_GMLP), MXU), jax.ShapeDtypeStruct((t, D_MODEL), MXU)]
    return pl.pallas_call(
        body, name="qkvuz", grid=(t // TM,),
        in_specs=[tok(D_MODEL), _full(w_in.shape), tok(128), tok(128), tok(128), _full(ln_z_g.shape), _full(ln_z_b.shape),
                  _full(w_s.shape), _full(b_full.shape), pl.BlockSpec(memory_space=pl.ANY)],
        out_specs=[_perm_tile(d, 3 * D_ATTN) for d in DILATIONS] + [tok(D_ATTN)] * 4 + [tok(D_MODEL)], out_shape=outs,
        scratch_shapes=[pltpu.VMEM((TM, D_IN), F32), pltpu.VMEM((N_HEADS, BLK, BLK), MXU),
                        pltpu.VMEM((3 * LANE_CHUNKS, TM, 128), F32)],
        compiler_params=_cp(dimension_semantics=("arbitrary",)),
    )(x, w_in, c_tab, s1_tab, s2_tab, ln_z_g, ln_z_b, w_s, b_full, dep)


def _band_valid(n):
    i = lax.broadcasted_iota(jnp.int32, (BLK, 2 * BLK), 0)
    j = lax.broadcasted_iota(jnp.int32, (BLK, 2 * BLK), 1)
    return (j >= i) & (j <= i + BLK) & ((j >= BLK) | (n > 0))


def _attn_fwd(qkv, d, dep):
    _, l_sub, _ = qkv.shape
    nb = l_sub // BLK

    def body(q_ref, kp_ref, kc_ref, vp_ref, vc_ref, dep_ref, o_ref, l_ref):
        valid = _band_valid(pl.program_id(1))
        kcat = jnp.concatenate([kp_ref[...], kc_ref[...]], axis=0)
        vcat = jnp.concatenate([vp_ref[...], vc_ref[...]], axis=0)
        for h in range(N_HEADS):
            cols = slice(h * HEAD_DIM, (h + 1) * HEAD_DIM)
            s = jnp.where(valid, _dot_nt(q_ref[:, cols], kcat[:, cols]), NEG_INF)
            m = jnp.max(s, axis=-1, keepdims=True)
            e = jnp.exp(s - m)
            den = jnp.sum(e, axis=-1, keepdims=True)
            o_ref[:, cols] = _dot(e, vcat[:, cols]) * (1.0 / den)
            l_ref[:, h * HEAD_LANES:(h + 1) * HEAD_LANES] = jnp.broadcast_to(m + jnp.log(den), (BLK, HEAD_LANES))

    def blk(w, col, prev=False):
        return pl.BlockSpec((None, BLK, w), lambda r, n: (r, jnp.maximum(n - 1, 0) if prev else n, col))

    return pl.pallas_call(
        body, name=f"attn_fwd_d{d}", grid=(d, nb),
        in_specs=[blk(D_ATTN, 0), blk(D_ATTN, 1, True), blk(D_ATTN, 1), blk(D_ATTN, 2, True), blk(D_ATTN, 2),
                  pl.BlockSpec(memory_space=pl.ANY)],
        out_specs=[blk(D_ATTN, 0), blk(128, 0)],
        out_shape=[jax.ShapeDtypeStruct((d, l_sub, D_ATTN), F32), jax.ShapeDtypeStruct((d, l_sub, 128), F32)],
        compiler_params=_cp(dimension_semantics=("arbitrary", "arbitrary")),
    )(qkv, qkv, qkv, qkv, qkv, dep)


def _attn_bwd(qkv, do, lse, delta, d, dep):
    _, l_sub, _ = qkv.shape
    nb = l_sub // BLK
    whole = l_sub <= 8 * BLK

    def shares(n, q_ref, kp_ref, kc_ref, vp_ref, vc_ref, do_ref, l_ref, dl_ref, dq_ref):
        valid = _band_valid(n)
        kcat = jnp.concatenate([kp_ref[...], kc_ref[...]], axis=0)
        vcat = jnp.concatenate([vp_ref[...], vc_ref[...]], axis=0)
        for h in range(N_HEADS):
            cols = slice(h * HEAD_DIM, (h + 1) * HEAD_DIM)
            stat = slice(h * HEAD_LANES, h * HEAD_LANES + 1)
            qh, doh = q_ref[:, cols], do_ref[:, cols]
            p = jnp.where(valid, jnp.exp(_dot_nt(qh, kcat[:, cols]) - l_ref[:, stat]), 0.0)
            ds = p * (_dot_nt(doh, vcat[:, cols]) - dl_ref[:, stat])
            dq_ref[:, cols] = _dot(ds, kcat[:, cols]).astype(MXU)
            yield cols, _dot_tn(ds, qh), _dot_tn(p, doh)

    def body_whole(*refs):
        dko_ref, dvo_ref, dk_ref, dv_ref = refs[10:]
        n = pl.program_id(1)
        cur = pl.ds(pl.multiple_of(n * BLK, BLK), BLK)
        prev = pl.ds(pl.multiple_of(jnp.maximum(n - 1, 0) * BLK, BLK), BLK)
        for cols, dk2, dv2 in shares(n, *refs[:8], refs[9]):
            dk_ref[cur, cols] = dk2[BLK:]
            dv_ref[cur, cols] = dv2[BLK:]
            dk_ref[prev, cols] += dk2[0:BLK]
            dv_ref[prev, cols] += dv2[0:BLK]

        @pl.when(n == nb - 1)
        def _():
            dko_ref[...] = dk_ref[...].astype(MXU)
            dvo_ref[...] = dv_ref[...].astype(MXU)

    def body_carry(*refs):
        dk_ref, dv_ref, ck_scr, cv_scr = refs[10:]
        n = pl.program_id(1)

        @pl.when(n == 0)
        def _():
            ck_scr[...] = jnp.zeros_like(ck_scr)
            cv_scr[...] = jnp.zeros_like(cv_scr)

        @pl.when(n < nb)
        def _():
            for cols, dk2, dv2 in shares(n, *refs[:8], refs[9]):
                dk_ref[:, cols] = (ck_scr[:, cols] + dk2[0:BLK]).astype(MXU)
                dv_ref[:, cols] = (cv_scr[:, cols] + dv2[0:BLK]).astype(MXU)
                ck_scr[:, cols] = dk2[BLK:]
                cv_scr[:, cols] = dv2[BLK:]

        @pl.when(n == nb)
        def _():
            dk_ref[...] = ck_scr[...].astype(MXU)
            dv_ref[...] = cv_scr[...].astype(MXU)

    def blk(w, col, shift=0):
        return pl.BlockSpec((None, BLK, w), lambda r, n: (r, jnp.clip(n - shift, 0, nb - 1), col))

    if whole:
        dkv_spec = pl.BlockSpec((None, l_sub, D_ATTN), lambda r, n: (r, 0, 0))
        body, steps, scratch = body_whole, nb, [pltpu.VMEM((l_sub, D_ATTN), F32)] * 2
    else:
        dkv_spec = blk(D_ATTN, 0, 1)
        body, steps, scratch = body_carry, nb + 1, [pltpu.VMEM((BLK, D_ATTN), F32)] * 2
    return pl.pallas_call(
        body, name=f"attn_bwd_d{d}", grid=(d, steps),
        in_specs=[blk(D_ATTN, 0), blk(D_ATTN, 1, 1), blk(D_ATTN, 1), blk(D_ATTN, 2, 1), blk(D_ATTN, 2),
                  blk(D_ATTN, 0), blk(128, 0), blk(128, 0), pl.BlockSpec(memory_space=pl.ANY)],
        out_specs=[blk(D_ATTN, 0), dkv_spec, dkv_spec],
        out_shape=[jax.ShapeDtypeStruct((d, l_sub, D_ATTN), MXU)] * 3,
        scratch_shapes=scratch,
        compiler_params=_cp(dimension_semantics=("arbitrary", "arbitrary")),
    )(qkv, qkv, qkv, qkv, qkv, do, lse, delta, dep)


def _mix_ln1(os_, ls_, gm, x, w_o, ln1_g, ln1_b, dep):
    t = x.shape[0]
    expand = _head_expand()

    def body(o1, o4, o16, l1, l4, l16, gm_ref, x_ref, wo_ref, g_ref, b_ref, ex_ref, dep_ref,
             attn_ref, lse1_ref, lse4_ref, lse16_ref, cat_ref, xhat_ref, rstd_ref, x1b_ref, o_scr, l_scr):
        _from_planes(o4, o_scr, DILATIONS[1], LANE_CHUNKS)
        _from_planes(o16, o_scr.at[pl.ds(LANE_CHUNKS, LANE_CHUNKS)], DILATIONS[2], LANE_CHUNKS)
        _from_planes(l4, l_scr, DILATIONS[1], 1)
        _from_planes(l16, l_scr.at[pl.ds(1, 1)], DILATIONS[2], 1)
        la, lb, lc = l1[...], l_scr[0], l_scr[1]
        m = jnp.maximum(jnp.maximum(la, lb), lc)
        ea, eb, ec = jnp.exp(la - m), jnp.exp(lb - m), jnp.exp(lc - m)
        den = ea + eb + ec
        inv = 1.0 / den
        wide = lambda w: _dot_select(w, ex_ref[...])
        attn = (wide(ea * inv) * o1[...] + wide(eb * inv) * _unchunk(o_scr, LANE_CHUNKS)
                + wide(ec * inv) * _unchunk(o_scr, LANE_CHUNKS, LANE_CHUNKS))
        attn_ref[...] = attn
        lse = m + jnp.log(den)
        lse1_ref[...] = lse
        l_scr[2] = lse
        _to_planes(lse4_ref, l_scr.at[pl.ds(2, 1)], DILATIONS[1], 1, F32)
        _to_planes(lse16_ref, l_scr.at[pl.ds(2, 1)], DILATIONS[2], 1, F32)
        cat_ref[:, 0:D_ATTN] = attn.astype(MXU)
        cat_ref[:, D_ATTN:] = gm_ref[...]
        mix = jnp.dot(cat_ref[...], wo_ref[...], preferred_element_type=F32)
        xhat, rstd = _ln_fwd(ALPHA * x_ref[...] + mix)
        xhat_ref[...] = xhat
        rstd_ref[...] = rstd
        x1b_ref[...] = (xhat * g_ref[...] + b_ref[...]).astype(MXU)

    tok = lambda w: pl.BlockSpec((TM, w), lambda i: (i, 0))
    outs = [jax.ShapeDtypeStruct((t, D_ATTN), F32)] + [_perm_shape(t, d, 128, F32) for d in DILATIONS] + [
        jax.ShapeDtypeStruct((t, D_MODEL), MXU), jax.ShapeDtypeStruct((t, D_MODEL), F32), jax.ShapeDtypeStruct((t, 1), F32),
        jax.ShapeDtypeStruct((t, D_MODEL), MXU)]
    return pl.pallas_call(
        body, name="mix_ln1", grid=(t // TM,),
        in_specs=[_perm_tile(d, D_ATTN) for d in DILATIONS] + [_perm_tile(d, 128) for d in DILATIONS]
        + [tok(D_GMLP), tok(D_MODEL), _full(w_o.shape), _full(ln1_g.shape), _full(ln1_b.shape), _full(expand.shape),
           pl.BlockSpec(memory_space=pl.ANY)],
        out_specs=[tok(D_ATTN)] + [_perm_tile(d, 128) for d in DILATIONS] + [tok(D_MODEL), tok(D_MODEL), tok(1), tok(D_MODEL)],
        out_shape=outs,
        scratch_shapes=[pltpu.VMEM((2 * LANE_CHUNKS, TM, 128), F32), pltpu.VMEM((3, TM, 128), F32)],
        compiler_params=_cp(dimension_semantics=("arbitrary",)),
    )(*os_, *ls_, gm, x, w_o, ln1_g, ln1_b, expand, dep)


def _conv_fwd(a_ext, w_ref, b_ref, rows):
    back = [pltpu.roll(a_ext, s, 0)[HALO:HALO + rows] for s in (1, 2)]
    return b_ref[...] + w_ref[2:3, :] * a_ext[HALO:HALO + rows] + w_ref[1:2, :] * back[0] + w_ref[0:1, :] * back[1]


def _ffn_in(x1b, w_a, w_b, conv_w, conv_b):
    t = x1b.shape[0]
    hb = TM // HALO

    def body(x_ref, xh_ref, wa_ref, wb_ref, cw_ref, cb_ref, apre_ref, act_ref, gate_ref, f_ref):
        i = pl.program_id(1)
        a_pre = _dot_nt(x_ref[...], wa_ref[...])
        a_halo = jnp.where(i > 0, _dot_nt(xh_ref[...], wa_ref[...]), 0.0)
        a = _conv_fwd(jnp.concatenate([a_halo, a_pre], axis=0), cw_ref, cb_ref, TM)
        b = _dot_nt(x_ref[...], wb_ref[...])
        cdf = 0.5 * (1.0 + lax.erf(a * (1.0 / math.sqrt(2.0))))
        pdf = jnp.exp(-0.5 * a * a) * (1.0 / math.sqrt(2.0 * math.pi))
        act = a * cdf
        apre_ref[...] = a_pre
        act_ref[...] = act
        gate_ref[...] = b * (cdf + a * pdf)
        f_ref[...] = (act * b).astype(MXU)

    blk = lambda r, c: pl.BlockSpec((None, r, c), lambda j, i: (j, 0, 0))
    tokj = pl.BlockSpec((None, TM, FF_BLK), lambda j, i: (j, i, 0))
    outs = [jax.ShapeDtypeStruct((N_SHARD, t, FF_BLK), F32)] * 3 + [jax.ShapeDtypeStruct((N_SHARD, t, FF_BLK), MXU)]
    return pl.pallas_call(
        body, name="ffn_in", grid=(N_SHARD, t // TM),
        in_specs=[pl.BlockSpec((TM, D_MODEL), lambda j, i: (i, 0)),
                  pl.BlockSpec((HALO, D_MODEL), lambda j, i: (jnp.maximum(i * hb - 1, 0), 0)),
                  blk(FF_BLK, D_MODEL), blk(FF_BLK, D_MODEL), blk(3, FF_BLK), blk(1, FF_BLK)],
        out_specs=[tokj, tokj, tokj, tokj], out_shape=outs,
        compiler_params=_cp(dimension_semantics=("arbitrary", "arbitrary")),
    )(x1b, x1b, w_a, w_b, conv_w, conv_b)


def _ffn_out_ln2(f, w_down, xhat1, ln1_g, ln1_b):
    t = xhat1.shape[0]

    def body(f_ref, wd_ref, xh_ref, g1_ref, b1_ref, xhat_ref, rstd_ref):
        half = TM // ROW_GROUPS
        for r0 in range(0, TM, half):
            rows = pl.ds(r0, half)
            ff = jnp.dot(f_ref[0, rows, :], wd_ref[0], preferred_element_type=F32)
            for j in range(1, N_SHARD):
                ff = ff + jnp.dot(f_ref[j, rows, :], wd_ref[j], preferred_element_type=F32)
            x1 = xh_ref[rows, :] * g1_ref[...] + b1_ref[...]
            xhat, rstd = _ln_fwd(ALPHA * x1 + ff)
            xhat_ref[rows, :] = xhat
            rstd_ref[rows, :] = rstd

    tok = lambda w: pl.BlockSpec((TM, w), lambda i: (i, 0))
    vec = _full((1, D_MODEL))
    outs = [jax.ShapeDtypeStruct((t, D_MODEL), F32), jax.ShapeDtypeStruct((t, 1), F32)]
    return pl.pallas_call(
        body, name="ffn_out_ln2", grid=(t // TM,),
        in_specs=[pl.BlockSpec((N_SHARD, TM, FF_BLK), lambda i: (0, i, 0)), _full(w_down.shape), tok(D_MODEL), vec, vec],
        out_specs=[tok(D_MODEL), tok(1)], out_shape=outs,
        compiler_params=_cp(dimension_semantics=("arbitrary",)),
    )(f, w_down, xhat1, ln1_g, ln1_b)


STAT_ROWS = 8


def _ple_loss_bwd(xhat2, rstd2, p, target, ln2_g, ln2_b, w_g, b_g, w_p, ln3_g, ln3_b):
    t = xhat2.shape[0]

    def body(xh2_ref, rs2_ref, p_ref, t_ref, g2_ref, b2_ref, wg_ref, bg_ref, wp_ref, g3_ref, b3_ref,
             dr2_ref, dr2b_ref, stat_ref, dwg_ref, dwp_ref, pp_scr, dwp_scr):
        @pl.when(pl.program_id(0) == 0)
        def _():
            stat_ref[...] = jnp.zeros_like(stat_ref)
            dwg_ref[...] = jnp.zeros_like(dwg_ref)
            dwp_scr[...] = jnp.zeros_like(dwp_scr)

        xhat2 = xh2_ref[...]
        x2 = xhat2 * g2_ref[...] + b2_ref[...]
        x2b = x2.astype(MXU)
        gate = jax.nn.sigmoid(jnp.dot(x2b, wg_ref[...], preferred_element_type=F32) + bg_ref[...])
        pb = p_ref[...].astype(MXU)
        for j in range(N_SHARD):
            pp_scr[:, j * ROW_BLK:(j + 1) * ROW_BLK] = jnp.dot(pb, wp_ref[j], preferred_element_type=F32)
        pp = pp_scr[...]
        xhat3, rstd3 = _ln_fwd(ALPHA * x2 + gate * pp)
        err = xhat3 * g3_ref[...] + b3_ref[...] - t_ref[...]
        dy = err * (1.0 / D_MODEL)
        dr3 = _ln_bwd(dy, xhat3, rstd3, g3_ref[...])
        dgp = dr3 * pp * gate * (1.0 - gate)
        dgp_b = dgp.astype(MXU)
        dwg_ref[...] += _dot_tn(x2b, dgp_b)
        dwp_scr[...] += _dot_tn(pb, dr3 * gate)
        dx2 = ALPHA * dr3 + _dot_nt(dgp_b, wg_ref[...])
        dr2 = _ln_bwd(dx2, xhat2, rs2_ref[...], g2_ref[...])
        dr2_ref[...] = dr2
        dr2b_ref[...] = dr2.astype(MXU)
        stat_ref[0:1, :] += _colsum(dy * xhat3)
        stat_ref[1:2, :] += _colsum(dy)
        stat_ref[2:3, :] += _colsum(dgp)
        stat_ref[3:4, :] += _colsum(dx2 * xhat2)
        stat_ref[4:5, :] += _colsum(dx2)
        stat_ref[5:6, :] += _colsum(err * err)

        @pl.when(pl.program_id(0) == t // TM - 1)
        def _():
            for j in range(N_SHARD):
                dwp_ref[j] = dwp_scr[:, j * ROW_BLK:(j + 1) * ROW_BLK]

    tok = lambda w: pl.BlockSpec((TM, w), lambda i: (i, 0))
    vec = _full((1, D_MODEL))
    outs = [jax.ShapeDtypeStruct((t, D_MODEL), F32), jax.ShapeDtypeStruct((t, D_MODEL), MXU),
            jax.ShapeDtypeStruct((STAT_ROWS, D_MODEL), F32), jax.ShapeDtypeStruct((D_MODEL, D_MODEL), F32),
            jax.ShapeDtypeStruct((N_SHARD, D_PLE, ROW_BLK), F32)]
    return pl.pallas_call(
        body, name="ple_loss_bwd", grid=(t // TM,),
        in_specs=[tok(D_MODEL), tok(1), tok(D_PLE), tok(D_MODEL), vec, vec, _full(w_g.shape), vec, _full(w_p.shape), vec, vec],
        out_specs=[tok(D_MODEL), tok(D_MODEL), _full((STAT_ROWS, D_MODEL)), _full((D_MODEL, D_MODEL)),
                   _full((N_SHARD, D_PLE, ROW_BLK))], out_shape=outs,
        scratch_shapes=[pltpu.VMEM((TM, D_MODEL), F32), pltpu.VMEM((D_PLE, D_MODEL), F32)],
        compiler_params=_cp(dimension_semantics=("arbitrary",)),
    )(xhat2, rstd2, p, target, ln2_g, ln2_b, w_g, b_g, w_p, ln3_g, ln3_b)


def _ffn_bwd(dr2, dr2b, a_pre, act, gate, w_down, w_a, w_b, conv_w, xhat1, rstd1, ln1_g):
    t = dr2.shape[0]
    nt = t // TM
    hb = TM // HALO
    last_h = t // HALO - 1
    halo2 = 2 * HALO

    def body(dr_ref, drb_ref, drbn_ref, ap_ref, act_ref, gate_ref, gaten_ref, wd_ref, wa_ref, wb_ref, cw_ref,
             xh_ref, rs_ref, g1_ref, dap_ref, dbb_ref, dr1_ref, cstat_ref, lstat_ref, acc_scr):
        i, j = pl.program_id(0), pl.program_id(1)

        @pl.when((i == 0) & (j == 0))
        def _():
            cstat_ref[...] = jnp.zeros_like(cstat_ref)
            lstat_ref[...] = jnp.zeros_like(lstat_ref)

        half = TM // ROW_GROUPS
        parts = []
        for r0 in range(0, TM, half):
            rows = pl.ds(r0, half)
            last = r0 + half == TM

            def ext(ref, nxt):
                return jnp.concatenate([ref[rows], nxt[...]], axis=0) if last else ref[r0:r0 + half + HALO]

            drb = jnp.concatenate([drb_ref[rows, :], drbn_ref[...]], axis=0) if last else drb_ref[r0:r0 + half + halo2, :]
            df = _dot_nt(drb, wd_ref[...])[0:half + HALO]
            da = df * ext(gate_ref, gaten_ref)
            if last:
                da = jnp.concatenate([da[0:half], jnp.where(i < nt - 1, da[half:], 0.0)], axis=0)
            ahead = [da[0:half]] + [pltpu.roll(da, half + HALO - s, 0)[0:half] for s in (1, 2)]
            da_pre = cw_ref[2:3, :] * ahead[0] + cw_ref[1:2, :] * ahead[1] + cw_ref[0:1, :] * ahead[2]
            dbb = df[0:half] * act_ref[rows, :]
            dap_ref[rows, :] = da_pre.astype(MXU)
            dbb_ref[rows, :] = dbb.astype(MXU)
            for kk in range(3):
                cstat_ref[j, kk:kk + 1, :] += _colsum(ahead[2 - kk] * ap_ref[rows, :])
            cstat_ref[j, 3:4, :] += _colsum(ahead[0])
            parts.append(_dot(da_pre, wa_ref[...]) + _dot(dbb, wb_ref[...]))
        part = jnp.concatenate(parts, axis=0)

        @pl.when(j == 0)
        def _():
            acc_scr[...] = ALPHA * dr_ref[...] + part

        @pl.when(j > 0)
        def _():
            acc_scr[...] += part

        @pl.when(j == N_SHARD - 1)
        def _():
            dx1 = acc_scr[...]
            xhat1 = xh_ref[...]
            lstat_ref[0:1, :] += _colsum(dx1 * xhat1)
            lstat_ref[1:2, :] += _colsum(dx1)
            dr1_ref[...] = _ln_bwd(dx1, xhat1, rs_ref[...], g1_ref[...])

    tok = lambda w: pl.BlockSpec((TM, w), lambda i, j: (i, 0))
    tokj = pl.BlockSpec((None, TM, FF_BLK), lambda i, j: (j, i, 0))
    nextj = pl.BlockSpec((None, HALO, FF_BLK), lambda i, j: (j, jnp.minimum((i + 1) * hb, last_h), 0))
    blk = lambda r, c: pl.BlockSpec((None, r, c), lambda i, j: (j, 0, 0))
    outs = [jax.ShapeDtypeStruct((N_SHARD, t, FF_BLK), MXU)] * 2 + [
        jax.ShapeDtypeStruct((t, D_MODEL), F32), jax.ShapeDtypeStruct((N_SHARD, STAT_ROWS, FF_BLK), F32),
        jax.ShapeDtypeStruct((STAT_ROWS, D_MODEL), F32)]
    return pl.pallas_call(
        body, name="ffn_bwd", grid=(nt, N_SHARD),
        in_specs=[tok(D_MODEL), tok(D_MODEL),
                  pl.BlockSpec((halo2, D_MODEL), lambda i, j: (jnp.minimum((i + 1) * (hb // 2), last_h // 2), 0)),
                  tokj, tokj, tokj, nextj, blk(FF_BLK, D_MODEL), blk(FF_BLK, D_MODEL), blk(FF_BLK, D_MODEL),
                  blk(3, FF_BLK), tok(D_MODEL), tok(1), _full((1, D_MODEL))],
        out_specs=[tokj, tokj, tok(D_MODEL), _full((N_SHARD, STAT_ROWS, FF_BLK)), _full((STAT_ROWS, D_MODEL))], out_shape=outs,
        scratch_shapes=[pltpu.VMEM((TM, D_MODEL), F32)],
        compiler_params=_cp(dimension_semantics=("arbitrary", "arbitrary")),
    )(dr2, dr2b, dr2b, a_pre, act, gate, gate, w_down, w_a, w_b, conv_w, xhat1, rstd1, ln1_g)


def _mix_bwd(dr1, w_o, hu, hz, mixed, attn, ln_z_g, ln_z_b, w_s, dep):
    t = dr1.shape[0]
    nchunk = TM // BLK

    def body(dr_ref, wo_ref, hu_ref, hz_ref, mx_ref, attn_ref, g_ref, b_ref, ws_ref, grp_ref, red_ref, dep_ref,
             do1_ref, do4_ref, do16_ref, dl1_ref, dl4_ref, dl16_ref, duz_ref, dws_ref, dbs_ref, zstat_ref,
             wm_scr, dzn_scr, dbsum_scr, do_scr, dl_scr):
        @pl.when(pl.program_id(0) == 0)
        def _():
            row = lax.broadcasted_iota(jnp.int32, (BLK, BLK), 0)
            col = lax.broadcasted_iota(jnp.int32, (BLK, BLK), 1)
            for g in range(N_HEADS):
                wm_scr[g] = jnp.where(col <= row, ws_ref[g], 0.0).astype(MXU)
            dws_ref[...] = jnp.zeros_like(dws_ref)
            dbsum_scr[...] = jnp.zeros_like(dbsum_scr)
            zstat_ref[...] = jnp.zeros_like(zstat_ref)

        dcat = _dot_nt(dr_ref[...], wo_ref[...])
        dattn = dcat[:, 0:D_ATTN]
        do1_ref[...] = dattn.astype(MXU)
        for cc, val in enumerate(_chunks(dattn)):
            do_scr[cc] = val
        _to_planes(do4_ref, do_scr, DILATIONS[1], LANE_CHUNKS, MXU)
        _to_planes(do16_ref, do_scr, DILATIONS[2], LANE_CHUNKS, MXU)
        delta = _dot_select(dattn * attn_ref[...], red_ref[...])
        dl1_ref[...] = delta
        dl_scr[0] = delta
        _to_planes(dl4_ref, dl_scr, DILATIONS[1], 1, F32)
        _to_planes(dl16_ref, dl_scr, DILATIONS[2], 1, F32)
        dgm = dcat[:, D_ATTN:]
        hu, hz = hu_ref[...], hz_ref[...]
        u = _gelu(hu)
        duz_ref[:, 0:D_GMLP] = (dgm * mx_ref[...] * _gelu_grad(hu)).astype(MXU)
        dmixed = dgm * u
        dmb = dmixed.astype(MXU)
        zhat, rstd = _ln_fwd(_gelu(hz))
        znb = (zhat * g_ref[...] + b_ref[...]).astype(MXU)
        dbs_acc = jnp.zeros((BLK, D_GMLP), F32)
        for ch in range(nchunk):
            rows = slice(ch * BLK, (ch + 1) * BLK)
            dbs_acc = dbs_acc + dmixed[rows]
            for g in range(N_HEADS):
                cols = slice(g * HEAD_DIM, (g + 1) * HEAD_DIM)
                dzn_scr[rows, cols] = _dot_tn(wm_scr[g], dmb[rows, cols])
                dws_ref[g] += _dot_nt(dmb[rows, cols], znb[rows, cols])
        dbsum_scr[...] += dbs_acc
        dzn = dzn_scr[...]
        zstat_ref[0:1, :] += _colsum(dzn * zhat)
        zstat_ref[1:2, :] += _colsum(dzn)
        duz_ref[:, D_GMLP:] = (_ln_bwd(dzn, zhat, rstd, g_ref[...]) * _gelu_grad(hz)).astype(MXU)

        @pl.when(pl.program_id(0) == nt - 1)
        def _():
            row = lax.broadcasted_iota(jnp.int32, (BLK, BLK), 0)
            col = lax.broadcasted_iota(jnp.int32, (BLK, BLK), 1)
            for g in range(N_HEADS):
                dws_ref[g] = jnp.where(col <= row, dws_ref[g], 0.0)
            dbs_ref[...] = lax.dot_general(grp_ref[...], dbsum_scr[...], (((1,), (1,)), ((), ())),
                                           precision=lax.Precision.HIGHEST, preferred_element_type=F32)

    nt = t // TM
    tok = lambda w: pl.BlockSpec((TM, w), lambda i: (i, 0))
    grp = jnp.asarray((np.arange(D_GMLP)[None, :] // HEAD_DIM == np.arange(N_HEADS)[:, None]).astype(np.float32))
    red = _head_reduce()
    outs = [_perm_shape(t, d, D_ATTN, MXU) for d in DILATIONS] + [_perm_shape(t, d, 128, F32) for d in DILATIONS] + [
        jax.ShapeDtypeStruct((t, 2 * D_GMLP), MXU),
        jax.ShapeDtypeStruct((N_HEADS, BLK, BLK), F32), jax.ShapeDtypeStruct((N_HEADS, BLK), F32),
        jax.ShapeDtypeStruct((STAT_ROWS, D_GMLP), F32)]
    return pl.pallas_call(
        body, name="mix_bwd", grid=(t // TM,),
        in_specs=[tok(D_MODEL), _full(w_o.shape), tok(D_GMLP), tok(D_GMLP), tok(D_GMLP), tok(D_ATTN), _full(ln_z_g.shape),
                  _full(ln_z_b.shape), _full(w_s.shape), _full(grp.shape), _full(red.shape), pl.BlockSpec(memory_space=pl.ANY)],
        out_specs=[_perm_tile(d, D_ATTN) for d in DILATIONS] + [_perm_tile(d, 128) for d in DILATIONS]
        + [tok(2 * D_GMLP), _full((N_HEADS, BLK, BLK)), _full((N_HEADS, BLK)), _full((STAT_ROWS, D_GMLP))],
        out_shape=outs,
        scratch_shapes=[pltpu.VMEM((N_HEADS, BLK, BLK), MXU), pltpu.VMEM((TM, D_GMLP), F32), pltpu.VMEM((BLK, D_GMLP), F32),
                        pltpu.VMEM((LANE_CHUNKS, TM, 128), F32), pltpu.VMEM((1, TM, 128), F32)],
        compiler_params=_cp(dimension_semantics=("arbitrary",)),
    )(dr1, w_o, hu, hz, mixed, attn, ln_z_g, ln_z_b, w_s, grp, red, dep)


def _dx_in(dqs, dks, dvs, duz, dr1, w_in, c_tab, s1_tab, s2_tab):
    t = dr1.shape[0]

    def body(dq1, dq4, dq16, dk1, dk4, dk16, dv1, dv4, dv16, duz_ref, dr_ref, w_ref, c_ref, s1_ref, s2_ref,
             dh_ref, dx_ref, acc_scr):
        sums = []
        for part, (g1, g4, g16) in enumerate(((dq1, dq4, dq16), (dk1, dk4, dk16), (dv1, dv4, dv16))):
            acc = acc_scr.at[pl.ds(part * LANE_CHUNKS, LANE_CHUNKS)]
            for cc in range(LANE_CHUNKS):
                acc[cc] = g1[:, cc * 128:(cc + 1) * 128].astype(F32)
            _from_planes(g4, acc, DILATIONS[1], LANE_CHUNKS, accumulate=True)
            _from_planes(g16, acc, DILATIONS[2], LANE_CHUNKS, accumulate=True)
            sums.append(_unchunk(acc_scr, LANE_CHUNKS, part * LANE_CHUNKS))
        c, s1, s2 = _tile_heads(c_ref[...]), _tile_heads(s1_ref[...]), _tile_heads(s2_ref[...])
        dh_ref[:, 0:D_ATTN] = _rope_apply_t(sums[0] * (1.0 / math.sqrt(HEAD_DIM)), c, s1, s2).astype(MXU)
        dh_ref[:, D_ATTN:2 * D_ATTN] = _rope_apply_t(sums[1], c, s1, s2).astype(MXU)
        dh_ref[:, 2 * D_ATTN:3 * D_ATTN] = sums[2].astype(MXU)
        dh_ref[:, 3 * D_ATTN:] = duz_ref[...]
        dx = ALPHA * dr_ref[...]
        for j in range(N_SHARD):
            dx = dx + _dot_nt(dh_ref[:, j * W_IN_BLK:(j + 1) * W_IN_BLK], w_ref[j])
        dx_ref[...] = dx

    tok = lambda w: pl.BlockSpec((TM, w), lambda i: (i, 0))
    outs = [jax.ShapeDtypeStruct((t, D_IN), MXU), jax.ShapeDtypeStruct((t, D_MODEL), F32)]
    return pl.pallas_call(
        body, name="dx_in", grid=(t // TM,),
        in_specs=[_perm_tile(d, D_ATTN) for d in DILATIONS] * 3
        + [tok(2 * D_GMLP), tok(D_MODEL), _full(w_in.shape), tok(128), tok(128), tok(128)],
        out_specs=[tok(D_IN), tok(D_MODEL)], out_shape=outs,
        scratch_shapes=[pltpu.VMEM((3 * LANE_CHUNKS, TM, 128), F32)],
        compiler_params=_cp(dimension_semantics=("arbitrary",)),
    )(*dqs, *dks, *dvs, duz, dr1, w_in, c_tab, s1_tab, s2_tab)


def _wgrad(name, x, dy, x_spec, dy_spec, out_spec, out_shape, grid, dep=None):
    deps = [] if dep is None else [dep]

    def body(x_ref, dy_ref, *rest):
        rest[-1][...] = _dot_tn(x_ref[...], dy_ref[...])

    return pl.pallas_call(
        body, name=name, grid=grid, in_specs=[x_spec, dy_spec] + [pl.BlockSpec(memory_space=pl.ANY)] * len(deps),
        out_specs=out_spec, out_shape=jax.ShapeDtypeStruct(out_shape, F32),
        compiler_params=_cp(dimension_semantics=("arbitrary",) * len(grid)),
    )(x, dy, *deps)


def _wgrad_pair(name, xa, xb, dy, x_spec, dy_spec, out_spec, out_shape, grid):
    def body(xa_ref, xb_ref, dy_ref, oa_ref, ob_ref):
        dy = dy_ref[...]
        oa_ref[...] = _dot_tn(xa_ref[...], dy)
        ob_ref[...] = _dot_tn(xb_ref[...], dy)

    return pl.pallas_call(
        body, name=name, grid=grid, in_specs=[x_spec, x_spec, dy_spec], out_specs=[out_spec, out_spec],
        out_shape=[jax.ShapeDtypeStruct(out_shape, F32)] * 2,
        compiler_params=_cp(dimension_semantics=("arbitrary",) * len(grid)),
    )(xa, xb, dy)


def _local_step(x, p, rope, target, w_in, start_dep, late_landed, late_weights, early_grads, early_grads_sent,
                early_grads_landed,
                ln_z_g, ln_z_b, w_s, b_s, ln1_g, ln1_b, conv_b, ln2_g, ln2_b, b_g, ln3_g, ln3_b):
    t = x.shape[0]
    half = TM
    c_tab, s1_tab, s2_tab = rope
    b_full = jnp.repeat(jnp.transpose(b_s[0]), HEAD_DIM, axis=1)
    conv_b4 = conv_b.reshape(N_SHARD, 1, FF_BLK)
    *qkvs, hu, hz, mixed, gm, xb = _qkvuz(x, w_in, c_tab, s1_tab, s2_tab, ln_z_g, ln_z_b, w_s[0], b_full, start_dep)
    branches = [_attn_fwd(qkv, d, start_dep) for qkv, d in zip(qkvs[:2], DILATIONS[:2])]
    dep = late_landed(branches[-1][1])
    branches.append(_attn_fwd(qkvs[2], DILATIONS[2], dep))
    w_o, w_a, w_b, conv_w, w_down, w_g, w_p = late_weights(branches[-1][1])
    attn, *lses, cat, xhat1, rstd1, x1b = _mix_ln1(
        [o for o, _ in branches], [l for _, l in branches], gm, x, w_o, ln1_g, ln1_b, dep)
    a_pre, act, gate, f = _ffn_in(x1b, w_a, w_b, conv_w, conv_b4)
    xhat2, rstd2 = _ffn_out_ln2(f, w_down, xhat1, ln1_g, ln1_b)
    dr2, dr2b, stat3, g_w_g, g_w_p = _ple_loss_bwd(xhat2, rstd2, p, target, ln2_g, ln2_b, w_g, b_g, w_p, ln3_g, ln3_b)
    da_pre, dbb, dr1, cstat, stat1 = _ffn_bwd(dr2, dr2b, a_pre, act, gate, w_down, w_a, w_b, conv_w, xhat1, rstd1, ln1_g)

    full_t = lambda w, im: pl.BlockSpec((t, w), im)
    ffj = pl.BlockSpec((None, t, FF_BLK), lambda j, kk: (j, 0, 0))
    early = dict(
        w_ple_gate=g_w_g, w_ple_in=g_w_p,
        w_ff_down=_wgrad("dw_down", f, dr2b, ffj, full_t(half, lambda j, n: (0, n)),
                         pl.BlockSpec((None, FF_BLK, half), lambda j, n: (j, 0, n)), (N_SHARD, FF_BLK, D_MODEL), (N_SHARD, 2)),
        **dict(zip(("w_ff_a", "w_ff_b"), _wgrad_pair(
            "dw_ab", da_pre, dbb, x1b, ffj, full_t(half, lambda j, n: (0, n)),
            pl.BlockSpec((None, FF_BLK, half), lambda j, n: (j, 0, n)), (N_SHARD, FF_BLK, D_MODEL), (N_SHARD, 2)))),
        w_o=_wgrad("dw_o", cat, dr1, full_t(half, lambda kk, n: (0, kk)), full_t(half, lambda kk, n: (0, n)),
                   pl.BlockSpec((half, half), lambda kk, n: (kk, n)), (D_MODEL, D_MODEL), (2, 2)))
    dep = early_grads(early)

    do1, do4, do16, dl1, dl4, dl16, duz, dws, dbs, zstat = _mix_bwd(
        dr1, w_o, hu, hz, mixed, attn, ln_z_g, ln_z_b, w_s[0], dep)
    dep = early_grads_sent(duz, (stat3, stat1, zstat, cstat, dws, dbs))
    dqkv = [_attn_bwd(qkv, do, lse, dl, d, dep)
            for qkv, do, lse, dl, d in zip(qkvs, (do1, do4, do16), lses, (dl1, dl4, dl16), DILATIONS)]
    dh, grad_x = _dx_in([g[0] for g in dqkv], [g[1] for g in dqkv], [g[2] for g in dqkv], duz, dr1, w_in,
                        c_tab, s1_tab, s2_tab)
    dep = early_grads_landed(grad_x)
    g_w_in = _wgrad("dw_in", xb, dh, full_t(half, lambda j, kk: (0, kk)), full_t(W_IN_BLK, lambda j, kk: (0, j)),
                    pl.BlockSpec((None, half, W_IN_BLK), lambda j, kk: (j, kk, 0)), (N_SHARD, D_MODEL, W_IN_BLK), (N_SHARD, 2),
                    dep)
    return grad_x, g_w_in


def _tile_rows(rows, mult, steps):
    if rows % mult:
        return rows
    return next(rows // k for k in range(steps, rows + 1) if rows % k == 0 and (rows // k) % mult == 0)


def _grid_spec(grid, in_specs, out_specs):
    return pltpu.PrefetchScalarGridSpec(num_scalar_prefetch=1, grid=grid, in_specs=in_specs, out_specs=out_specs)


def _on_own_steps(i, count, steps, work):
    if count == steps:
        work()
    else:
        pl.when(i < count)(work)


def _place_shards(name, ws, dtypes, place, dep):
    n = len(ws)
    tiles = [_tile_rows(w.shape[0], 16, 8) for w in ws]
    counts = [w.shape[0] // t for w, t in zip(ws, tiles)]
    steps = max(counts)

    def body(s_ref, *refs):
        i = pl.program_id(0)
        for a in range(n):
            def work(a=a):
                refs[n + 1 + a][...] = refs[a][...].astype(dtypes[a])
            _on_own_steps(i, counts[a], steps, work)

    def tile(a, lead):
        last = counts[a] - 1
        if lead:
            return pl.BlockSpec((None, tiles[a], ws[a].shape[1]), lambda i, s: (s[0], jnp.minimum(i, last), 0))
        return pl.BlockSpec((tiles[a], ws[a].shape[1]), lambda i, s: (jnp.minimum(i, last), 0))

    return pl.pallas_call(
        body, name=name,
        grid_spec=_grid_spec((steps,), [tile(a, False) for a in range(n)] + [pl.BlockSpec(memory_space=pl.ANY)],
                             [tile(a, True) for a in range(n)]),
        out_shape=[jax.ShapeDtypeStruct((N_SHARD, *w.shape), dt) for w, dt in zip(ws, dtypes)],
        compiler_params=_cp())(place, *ws, dep)


def _pair_sums(name, mines, gots, place):
    n = len(mines)
    tiles = [_tile_rows(g.shape[1], 16, 2) for g in gots]
    per_blk = [g.shape[1] // t for g, t in zip(gots, tiles)]
    counts = [N_SHARD * nh for nh in per_blk]
    steps = max(counts)

    def body(s_ref, *refs):
        i = pl.program_id(0)
        for a in range(n):
            def work(a=a):
                refs[2 * n + a][...] = (refs[a][...] + refs[n + a][...]).astype(BF16)
            _on_own_steps(i, counts[a], steps, work)

    def tile(a, mine):
        nh, last = per_blk[a], counts[a] - 1

        def index(i, s):
            g = jnp.minimum(i, last)
            return (g // nh, (s[1] * nh if mine else 0) + g % nh, 0)

        return pl.BlockSpec((None, tiles[a], gots[a].shape[2]), index)

    return pl.pallas_call(
        body, name=name,
        grid_spec=_grid_spec((steps,), [tile(a, True) for a in range(n)] + [tile(a, False) for a in range(n)],
                             [tile(a, False) for a in range(n)]),
        out_shape=[jax.ShapeDtypeStruct(g.shape, BF16) for g in gots], compiler_params=_cp())(place, *mines, *gots)


def _chip_sums(name, owns, landeds, place, dep):
    n = len(owns)
    tiles = [_tile_rows(o.shape[1], 16, 8) for o in owns]
    counts = [o.shape[1] // t for o, t in zip(owns, tiles)]
    steps = max(counts)

    def body(s_ref, *refs):
        i = pl.program_id(0)
        for a in range(n):
            def work(a=a):
                own, l1, l2, l3 = (refs[4 * a + k][...].astype(F32) for k in range(4))
                refs[4 * n + 1 + a][...] = ((own + l1) + l2) + l3
            _on_own_steps(i, counts[a], steps, work)

    def slot(a, d):
        last = counts[a] - 1
        return pl.BlockSpec((None, tiles[a], owns[a].shape[2]), lambda i, s: ((s[0] + d) % N_SHARD, jnp.minimum(i, last), 0))

    def out(a):
        nh, last = counts[a], counts[a] - 1
        return pl.BlockSpec((tiles[a], owns[a].shape[2]), lambda i, s: (s[1] * nh + jnp.minimum(i, last), 0))

    operands = [x for o, l in zip(owns, landeds) for x in (o, l, l, l)]
    return pl.pallas_call(
        body, name=name,
        grid_spec=_grid_spec((steps,), [slot(a, d) for a in range(n) for d in range(4)] + [pl.BlockSpec(memory_space=pl.ANY)],
                             [out(a) for a in range(n)]),
        out_shape=[jax.ShapeDtypeStruct((2 * o.shape[1], o.shape[2]), F32) for o in owns],
        compiler_params=_cp())(place, *operands, dep)


def _adamw_math(w, g, m, v):
    m = ADAM_B1 * m + (1.0 - ADAM_B1) * g
    v = ADAM_B2 * v + (1.0 - ADAM_B2) * (g * g)
    m_hat = m / (1.0 - ADAM_B1 ** ADAM_STEP)
    v_hat = v / (1.0 - ADAM_B2 ** ADAM_STEP)
    delta = -ADAM_LR * (m_hat / (jnp.sqrt(v_hat) + ADAM_EPS) + ADAM_WD * w)
    return delta, m, v


def _adamw_shards(name, ws, gs, ms, vs):
    n = len(ws)
    tiles = [_tile_rows(w.shape[1], 8, 8) for w in ws]
    counts = [w.shape[1] // t for w, t in zip(ws, tiles)]
    steps = max(counts)

    def body(*refs):
        i = pl.program_id(0)
        for a in range(n):
            def work(a=a):
                w_ref, g_ref, m_ref, v_ref = refs[4 * a:4 * a + 4]
                d_ref, nm_ref, nv_ref = refs[4 * n + 3 * a:4 * n + 3 * a + 3]
                d_ref[...], nm_ref[...], nv_ref[...] = _adamw_math(w_ref[...], g_ref[...], m_ref[...], v_ref[...])
            _on_own_steps(i, counts[a], steps, work)

    def tile(a, lead):
        last, c = counts[a] - 1, ws[a].shape[2]
        if lead:
            return pl.BlockSpec((None, tiles[a], c), lambda i: (0, jnp.minimum(i, last), 0))
        return pl.BlockSpec((tiles[a], c), lambda i: (jnp.minimum(i, last), 0))

    res = pl.pallas_call(
        body, name=name, grid=(steps,),
        in_specs=[tile(a, lead) for a in range(n) for lead in (True, False, True, True)],
        out_specs=[tile(a, True) for a in range(n) for _ in range(3)],
        out_shape=[jax.ShapeDtypeStruct(w.shape, F32) for w in ws for _ in range(3)],
        compiler_params=_cp())(*[x for quad in zip(ws, gs, ms, vs) for x in quad])
    return [tuple(res[3 * a:3 * a + 3]) for a in range(n)]


MESH = pl.DeviceIdType.MESH
ANY = pl.BlockSpec(memory_space=pl.ANY)


def _place():
    x, y, c = lax.axis_index("x"), lax.axis_index("y"), lax.axis_index("c")
    chips = [(1 - x, y), (x, 1 - y), (1 - x, 1 - y)]
    return x, y, c, 2 * x + y, chips


def _remote(src, dst, send_sem, recv_sem, dev):
    return pltpu.make_async_remote_copy(src_ref=src, dst_ref=dst, send_sem=send_sem, recv_sem=recv_sem,
                                        device_id=dev, device_id_type=MESH)


def _half(ref, hc, rows):
    return ref.at[pl.ds(hc * (rows // 2), rows // 2)]


def _sibling_join(blocks, tag):
    n = len(blocks)

    def body(*refs):
        outs = refs[n:2 * n]
        send, recv = refs[2 * n:]
        x, y, c, _, _ = _place()
        cps = []
        for a in range(n):
            h = blocks[a].shape[0] // 2
            mine = outs[a].at[pl.ds(c * h, h)]
            cp = _remote(mine, mine, send.at[a], recv.at[a], (x, y, 1 - c))
            cp.start()
            cps.append(cp)
        for a, cp in enumerate(cps):
            h = blocks[a].shape[0] // 2
            theirs = outs[a].at[pl.ds((1 - c) * h, h)]
            _remote(theirs, theirs, send.at[a], recv.at[a], (x, y, 1 - c)).wait_recv()
            cp.wait_send()

    sem = pltpu.SemaphoreType.DMA
    return pl.pallas_call(body, name=f"rs_sibling_join_{tag}", in_specs=[ANY] * n, out_specs=[ANY] * n,
                          out_shape=[jax.ShapeDtypeStruct(b_.shape, b_.dtype) for b_ in blocks],
                          input_output_aliases={a: a for a in range(n)},
                          scratch_shapes=[sem((n,)), sem((n,))])(*blocks)


def _join_start(blocks, after, tag):
    n = len(blocks)

    def body(*refs):
        ins = refs[:n]
        send, recv = refs[n + 1], refs[n + 2]
        token = refs[2 * n + 3]
        x, y, c, _, _ = _place()
        for a in range(n):
            h = blocks[a].shape[0] // 2
            mine = ins[a].at[pl.ds(c * h, h)]
            _remote(mine, mine, send.at[a], recv.at[a], (x, y, 1 - c)).start()
        token[...] = jnp.zeros_like(token)

    sems = pltpu.SemaphoreType.DMA((n,))
    res = pl.pallas_call(
        body, name=f"join_start_{tag}", in_specs=[HBM] * n + [ANY],
        out_specs=[SEM, SEM] + [HBM] * n + [pl.BlockSpec(memory_space=pltpu.VMEM)],
        out_shape=[sems, sems] + [pltpu.HBM(b_.shape, b_.dtype) for b_ in blocks] + [TOKEN],
        input_output_aliases={a: a + 2 for a in range(n)}, compiler_params=_in_flight_params(),
    )(*[_in_hbm(b_) for b_ in blocks], after)
    return res[0], res[1], res[2:2 + n], res[2 + n]


def _join_wait(send, recv, blocks, after, tag):
    n = len(blocks)

    def body(*refs):
        ins = refs[:n]
        send_ref, recv_ref = refs[n], refs[n + 1]
        x, y, c, _, _ = _place()
        for a in range(n):
            h = blocks[a].shape[0] // 2
            mine, theirs = ins[a].at[pl.ds(c * h, h)], ins[a].at[pl.ds((1 - c) * h, h)]
            _remote(mine, mine, send_ref.at[a], recv_ref.at[a], (x, y, 1 - c)).wait_send()
            _remote(theirs, theirs, send_ref.at[a], recv_ref.at[a], (x, y, 1 - c)).wait_recv()

    return pl.pallas_call(
        body, name=f"join_wait_{tag}", in_specs=[HBM] * n + [SEM, SEM, ANY], out_specs=[HBM] * n,
        out_shape=[pltpu.HBM(b_.shape, b_.dtype) for b_ in blocks],
        input_output_aliases={a: a for a in range(n)}, compiler_params=_in_flight_params(),
    )(*blocks, send, recv, after)


HBM = pl.BlockSpec(memory_space=pltpu.HBM)
SEM = pl.BlockSpec(memory_space=pltpu.SEMAPHORE)
TOKEN = jax.ShapeDtypeStruct((8, 128), F32)


def _in_flight_params():
    return pltpu.CompilerParams(has_side_effects=pltpu.SideEffectType.DATAFLOW_SIDE_EFFECTING)


def _in_hbm(a):
    return pltpu.with_memory_space_constraint(a, pltpu.HBM)


def _gather_piece(ref, rows, split, slot, hc):
    return _half(ref.at[slot], hc, rows) if split else ref.at[slot]


def _gather_start(stacks, split, after, tag):
    n = len(stacks)

    def body(*refs):
        ins = refs[:n]
        send, recv = refs[n + 1], refs[n + 2]
        token = refs[2 * n + 3]
        _, _, c, j, chips = _place()
        for a in range(n):
            mine = _gather_piece(ins[a], stacks[a].shape[1], split[a], j, c)
            for t in range(3):
                _remote(mine, mine, send.at[3 * a + t], recv.at[3 * a + t], (*chips[t], c)).start()
        token[...] = jnp.zeros_like(token)

    sems = pltpu.SemaphoreType.DMA((3 * n,))
    res = pl.pallas_call(
        body, name=f"gather_start_{tag}", in_specs=[HBM] * n + [ANY],
        out_specs=[SEM, SEM] + [HBM] * n + [pl.BlockSpec(memory_space=pltpu.VMEM)],
        out_shape=[sems, sems] + [pltpu.HBM(s.shape, s.dtype) for s in stacks] + [TOKEN],
        input_output_aliases={a: a + 2 for a in range(n)}, compiler_params=_in_flight_params(),
    )(*[_in_hbm(s) for s in stacks], after)
    return res[0], res[1], res[2:2 + n], res[2 + n]


def _gather_wait(send, recv, stacks, split, after, tag):
    n = len(stacks)

    def body(*refs):
        ins = refs[:n]
        send_ref, recv_ref = refs[n], refs[n + 1]
        _, _, c, j, chips = _place()
        for a in range(n):
            rows = stacks[a].shape[1]
            mine = _gather_piece(ins[a], rows, split[a], j, c)
            for t, (px, py) in enumerate(chips):
                theirs = _gather_piece(ins[a], rows, split[a], 2 * px + py, c)
                _remote(mine, mine, send_ref.at[3 * a + t], recv_ref.at[3 * a + t], (px, py, c)).wait_send()
                _remote(theirs, theirs, send_ref.at[3 * a + t], recv_ref.at[3 * a + t], (px, py, c)).wait_recv()

    return pl.pallas_call(
        body, name=f"gather_wait_{tag}", in_specs=[HBM] * n + [SEM, SEM, ANY], out_specs=[HBM] * n,
        out_shape=[pltpu.HBM(s.shape, s.dtype) for s in stacks],
        input_output_aliases={a: a for a in range(n)}, compiler_params=_in_flight_params(),
    )(*stacks, send, recv, after)


def _gather_forward(stacks, split, tag):
    idx = [a for a in range(len(stacks)) if split[a]]
    n = len(idx)

    def body(*refs):
        outs = refs[n:2 * n]
        send, recv = refs[2 * n:]
        x, y, c, _, chips = _place()
        sends = []
        for t, (px, py) in enumerate(chips):
            for a in range(n):
                blk = _half(outs[a].at[2 * px + py], c, stacks[idx[a]].shape[1])
                cp = _remote(blk, blk, send.at[a, t], recv.at[a, t], (x, y, 1 - c))
                cp.start()
                sends.append(cp)
        for t, (px, py) in enumerate(chips):
            for a in range(n):
                blk = _half(outs[a].at[2 * px + py], 1 - c, stacks[idx[a]].shape[1])
                _remote(blk, blk, send.at[a, t], recv.at[a, t], (x, y, 1 - c)).wait_recv()
        for cp in sends:
            cp.wait_send()

    sem = pltpu.SemaphoreType.DMA
    res = pl.pallas_call(
        body, name=f"gather_forward_{tag}", in_specs=[ANY] * n, out_specs=[ANY] * n,
        out_shape=[jax.ShapeDtypeStruct(stacks[a].shape, stacks[a].dtype) for a in idx],
        input_output_aliases={a: a for a in range(n)}, scratch_shapes=[sem((n, 3)), sem((n, 3))],
    )(*[stacks[a] for a in idx])
    out = list(stacks)
    for a, r in zip(idx, res):
        out[a] = r
    return out


def _forward_start(stacks, after, tag):
    n = len(stacks)

    def body(*refs):
        ins = refs[:n]
        send, recv = refs[n + 1], refs[n + 2]
        token = refs[2 * n + 3]
        x, y, c, _, chips = _place()
        for a in range(n):
            for t, (px, py) in enumerate(chips):
                blk = _half(ins[a].at[2 * px + py], c, stacks[a].shape[1])
                _remote(blk, blk, send.at[3 * a + t], recv.at[3 * a + t], (x, y, 1 - c)).start()
        token[...] = jnp.zeros_like(token)

    sems = pltpu.SemaphoreType.DMA((3 * n,))
    res = pl.pallas_call(
        body, name=f"forward_start_{tag}", in_specs=[HBM] * n + [ANY],
        out_specs=[SEM, SEM] + [HBM] * n + [pl.BlockSpec(memory_space=pltpu.VMEM)],
        out_shape=[sems, sems] + [pltpu.HBM(s.shape, s.dtype) for s in stacks] + [TOKEN],
        input_output_aliases={a: a + 2 for a in range(n)}, compiler_params=_in_flight_params(),
    )(*[_in_hbm(s) for s in stacks], after)
    return res[0], res[1], res[2:2 + n], res[2 + n]


def _forward_wait(send, recv, stacks, after, tag):
    n = len(stacks)

    def body(*refs):
        ins = refs[:n]
        send_ref, recv_ref = refs[n], refs[n + 1]
        x, y, c, _, chips = _place()
        for a in range(n):
            for t, (px, py) in enumerate(chips):
                mine = _half(ins[a].at[2 * px + py], c, stacks[a].shape[1])
                theirs = _half(ins[a].at[2 * px + py], 1 - c, stacks[a].shape[1])
                _remote(mine, mine, send_ref.at[3 * a + t], recv_ref.at[3 * a + t], (x, y, 1 - c)).wait_send()
                _remote(theirs, theirs, send_ref.at[3 * a + t], recv_ref.at[3 * a + t], (x, y, 1 - c)).wait_recv()

    return pl.pallas_call(
        body, name=f"forward_wait_{tag}", in_specs=[HBM] * n + [SEM, SEM, ANY], out_specs=[HBM] * n,
        out_shape=[pltpu.HBM(s.shape, s.dtype) for s in stacks],
        input_output_aliases={a: a for a in range(n)}, compiler_params=_in_flight_params(),
    )(*stacks, send, recv, after)


def _swap_start(grads, tag):
    n = len(grads)

    def body(*refs):
        ins, gots = refs[:n], refs[n:2 * n]
        send, recv = refs[2 * n], refs[2 * n + 1]
        token = refs[4 * n + 2]
        x, y, c, _, _ = _place()
        for a in range(n):
            h = grads[a].shape[1] // 2
            _remote(ins[a].at[:, pl.ds((1 - c) * h, h)], gots[a], send.at[a], recv.at[a], (x, y, 1 - c)).start()
        token[...] = jnp.zeros_like(token)

    sems = pltpu.SemaphoreType.DMA((n,))
    halves = [(g.shape[0], g.shape[1] // 2, g.shape[2]) for g in grads]
    res = pl.pallas_call(
        body, name=f"swap_start_{tag}", in_specs=[HBM] * (2 * n),
        out_specs=[SEM, SEM] + [HBM] * (2 * n) + [pl.BlockSpec(memory_space=pltpu.VMEM)],
        out_shape=[sems, sems] + [pltpu.HBM(g.shape, g.dtype) for g in grads] + [pltpu.HBM(s, F32) for s in halves] + [TOKEN],
        input_output_aliases={a: a + 2 for a in range(2 * n)}, compiler_params=_in_flight_params(),
    )(*[_in_hbm(g) for g in grads], *[_in_hbm(lax.empty(s, F32)) for s in halves])
    return res[0], res[1], res[2:2 + n], res[2 + n:2 + 2 * n], res[2 + 2 * n]


def _swap_wait(send, recv, grads, gots, after, tag):
    n = len(grads)

    def body(*refs):
        ins, lnd = refs[:n], refs[n:2 * n]
        send_ref, recv_ref = refs[2 * n], refs[2 * n + 1]
        x, y, c, _, _ = _place()
        for a in range(n):
            h = grads[a].shape[1] // 2
            cp = _remote(ins[a].at[:, pl.ds((1 - c) * h, h)], lnd[a], send_ref.at[a], recv_ref.at[a], (x, y, 1 - c))
            cp.wait_send()
            cp.wait_recv()

    bufs = [pltpu.HBM(g.shape, g.dtype) for g in grads] + [pltpu.HBM(g.shape, g.dtype) for g in gots]
    res = pl.pallas_call(
        body, name=f"swap_wait_{tag}", in_specs=[HBM] * (2 * n) + [SEM, SEM, ANY], out_specs=[HBM] * (2 * n),
        out_shape=bufs, input_output_aliases={a: a for a in range(2 * n)}, compiler_params=_in_flight_params(),
    )(*grads, *gots, send, recv, after)
    return res[:n], res[n:]


def _exchange_start(parts, tag):
    n = len(parts)

    def body(*refs):
        ins, lands = refs[:n], refs[n:2 * n]
        send, recv = refs[2 * n], refs[2 * n + 1]
        token = refs[4 * n + 2]
        _, _, c, j, chips = _place()
        for t, (px, py) in enumerate(chips):
            for a in range(n):
                _remote(ins[a].at[2 * px + py], lands[a].at[j], send.at[3 * a + t], recv.at[3 * a + t], (px, py, c)).start()
        token[...] = jnp.zeros_like(token)

    sems = pltpu.SemaphoreType.DMA((3 * n,))
    bufs = [pltpu.HBM(p.shape, p.dtype) for p in parts]
    res = pl.pallas_call(
        body, name=f"exchange_start_{tag}", in_specs=[HBM] * (2 * n),
        out_specs=[SEM, SEM] + [HBM] * (2 * n) + [pl.BlockSpec(memory_space=pltpu.VMEM)],
        out_shape=[sems, sems] + bufs + bufs + [TOKEN],
        input_output_aliases={a: a + 2 for a in range(2 * n)}, compiler_params=_in_flight_params(),
    )(*[_in_hbm(p) for p in parts], *[_in_hbm(lax.empty(p.shape, p.dtype)) for p in parts])
    return res[0], res[1], res[2:2 + n], res[2 + n:2 + 2 * n], res[2 + 2 * n]


def _exchange_wait(send, recv, parts, lands, after, tag):
    n = len(parts)

    def body(*refs):
        ins, lnd = refs[:n], refs[n:2 * n]
        send_ref, recv_ref = refs[2 * n], refs[2 * n + 1]
        _, _, c, j, chips = _place()
        for t, (px, py) in enumerate(chips):
            jt = 2 * px + py
            for a in range(n):
                _remote(ins[a].at[jt], lnd[a].at[j], send_ref.at[3 * a + t], recv_ref.at[3 * a + t], (px, py, c)).wait_send()
                _remote(ins[a].at[jt], lnd[a].at[jt], send_ref.at[3 * a + t], recv_ref.at[3 * a + t], (px, py, c)).wait_recv()

    bufs = [pltpu.HBM(p.shape, p.dtype) for p in parts]
    res = pl.pallas_call(
        body, name=f"exchange_wait_{tag}", in_specs=[HBM] * (2 * n) + [SEM, SEM, ANY], out_specs=[HBM] * (2 * n),
        out_shape=bufs + bufs, input_output_aliases={a: a for a in range(2 * n)}, compiler_params=_in_flight_params(),
    )(*parts, *lands, send, recv, after)
    return res[:n], res[n:]


def _small_chip_sums(arrs):
    n = len(arrs)

    def body(*refs):
        ins, outs = refs[:n], refs[n:2 * n]
        sib = refs[2 * n:3 * n]
        send, recv = refs[3 * n:]
        x, y, c, j, _ = _place()
        swaps = [_remote(ins[a], sib[a], send.at[a], recv.at[a], (x, y, 1 - c)) for a in range(n)]
        for cp in swaps:
            cp.start()
        for a in range(n):
            swaps[a].wait_recv()
            outs[a][j] = ins[a][...] + sib[a][...]
        for cp in swaps:
            cp.wait_send()

    sem = pltpu.SemaphoreType.DMA
    vm = pl.BlockSpec(memory_space=pltpu.VMEM)
    return pl.pallas_call(
        body, name="small_chip_sums", in_specs=[vm] * n, out_specs=[vm] * n,
        out_shape=[jax.ShapeDtypeStruct((N_SHARD, *a.shape), F32) for a in arrs],
        scratch_shapes=[pltpu.VMEM(a.shape, F32) for a in arrs] + [sem((n,)), sem((n,))],
        compiler_params=_cp(),
    )(*arrs)


def _small_totals(stacks):
    n = len(stacks)

    def body(*refs):
        for a in range(n):
            refs[n + a][...] = ((refs[a][0] + refs[a][1]) + refs[a][2]) + refs[a][3]

    return pl.pallas_call(body, name="small_totals", out_shape=[jax.ShapeDtypeStruct(s.shape[1:], F32) for s in stacks],
                          compiler_params=_cp())(*stacks)


SMALL_1024 = ("ln1_g", "ln1_b", "ln2_g", "ln2_b", "b_ple_gate", "ln3_g", "ln3_b")


def _adamw_small(red3, red1, redz, g_conv_w, redc, red_ws, red_bs, params):
    shape2d = {"ln_z_g": (1, D_GMLP), "ln_z_b": (1, D_GMLP), "w_s": (N_HEADS * BLK, BLK), "b_s": (N_HEADS, BLK),
               "conv_w": (3, FF_BLK), "conv_b": (N_SHARD, FF_BLK), **{k: (1, D_MODEL) for k in SMALL_1024}}
    names = list(shape2d)
    flat = [a.reshape(shape2d[k]) for k in names for a in params[k]]

    def body(r3, r1, rz, gcw, rc, rws, rbs, *refs):
        ins, outs = refs[:3 * len(names)], refs[3 * len(names):]

        def grad_of(k):
            if k == "w_s":
                return rws[...]
            if k == "b_s":
                return rbs[...]
            if k == "conv_w":
                return gcw[0:3, :]
            if k == "conv_b":
                return jnp.concatenate([rc[j * STAT_ROWS + 3:j * STAT_ROWS + 4, :] for j in range(N_SHARD)], axis=0)
            src, row = {"ln3_g": (r3, 0), "ln3_b": (r3, 1), "b_ple_gate": (r3, 2), "ln2_g": (r3, 3), "ln2_b": (r3, 4),
                        "ln1_g": (r1, 0), "ln1_b": (r1, 1), "ln_z_g": (rz, 0), "ln_z_b": (rz, 1)}[k]
            return src[row:row + 1, :]

        for i, k in enumerate(names):
            w_ref, m_ref, v_ref = ins[3 * i:3 * i + 3]
            g_ref, d_ref, nm_ref, nv_ref = outs[4 * i:4 * i + 4]
            g = grad_of(k)
            g_ref[...] = g
            d_ref[...], nm_ref[...], nv_ref[...] = _adamw_math(w_ref[...], g, m_ref[...], v_ref[...])

    res = pl.pallas_call(
        body, name="adamw_small",
        out_shape=[jax.ShapeDtypeStruct(shape2d[k], F32) for k in names for _ in range(4)],
        compiler_params=_cp(),
    )(red3, red1, redz, g_conv_w, redc, red_ws, red_bs, *flat)
    return {k: tuple(r.reshape(params[k][0].shape) for r in res[4 * i:4 * i + 4]) for i, k in enumerate(names)}


WEIGHTS = ("w_in", "ln_z_g", "ln_z_b", "w_s", "b_s", "w_o", "ln1_g", "ln1_b", "w_ff_a", "w_ff_b", "conv_w", "conv_b",
           "w_ff_down", "ln2_g", "ln2_b", "w_ple_gate", "b_ple_gate", "w_ple_in", "ln3_g", "ln3_b")
BIG = ("w_in", "w_o", "w_ff_a", "w_ff_b", "w_ff_down", "w_ple_gate", "w_ple_in")
TRANSPOSED = ("w_ff_a", "w_ff_b")
LATE = ("w_o", "w_ff_a", "w_ff_b", "w_ff_down", "w_ple_gate", "w_ple_in", "conv_w")


def kernel(x, p, positions, w_in, ln_z_g, ln_z_b, w_s, b_s, w_o, ln1_g, ln1_b, w_ff_a, w_ff_b, conv_w, conv_b, w_ff_down, ln2_g, ln2_b, w_ple_gate, b_ple_gate, w_ple_in, ln3_g, ln3_b, loss_target, m_w_in, m_ln_z_g, m_ln_z_b, m_w_s, m_b_s, m_w_o, m_ln1_g, m_ln1_b, m_w_ff_a, m_w_ff_b, m_conv_w, m_conv_b, m_w_ff_down, m_ln2_g, m_ln2_b, m_w_ple_gate, m_b_ple_gate, m_w_ple_in, m_ln3_g, m_ln3_b, v_w_in, v_ln_z_g, v_ln_z_b, v_w_s, v_b_s, v_w_o, v_ln1_g, v_ln1_b, v_w_ff_a, v_w_ff_b, v_conv_w, v_conv_b, v_w_ff_down, v_ln2_g, v_ln2_b, v_w_ple_gate, v_b_ple_gate, v_w_ple_in, v_ln3_g, v_ln3_b):
    args = locals()
    w = {k: args[k] for k in WEIGHTS}
    m = {k: args["m_" + k] for k in WEIGHTS}
    v = {k: args["v_" + k] for k in WEIGHTS}

    for k in TRANSPOSED:
        w[k], m[k], v[k] = (jnp.swapaxes(a, 1, 2) for a in (w[k], m[k], v[k]))

    chip = 2 * lax.axis_index("x") + lax.axis_index("y")
    place = jnp.stack([chip, lax.axis_index("c")]).astype(jnp.int32)
    stack = dict(zip(["w_in"], _place_shards("cast_w_in", [w["w_in"][0]], [MXU], place, place)))
    i_send, i_recv, in_flight, dep = _gather_start([stack["w_in"]], [True], place, "w_in")
    stack.update(zip(LATE, _place_shards("cast_late", [w[k][0] for k in LATE],
                                         [F32 if k == "conv_w" else MXU for k in LATE], place, dep)))
    split_late = [k != "conv_w" for k in LATE]
    g_send, g_recv, late_flight, start_dep = _gather_start([stack[k] for k in LATE], split_late, place, "late")
    rope = _rope_tables(positions, x.shape[1], start_dep)
    landed_in = _gather_wait(i_send, i_recv, in_flight, [True], rope[0], "w_in")
    w_in_full, = _gather_forward(landed_in, [True], "w_in")
    halves =[k for k, sp in zip(LATE, split_late) if sp]
    trips = {}

    def late_landed(after):
        fw = dict(zip(LATE, _gather_wait(g_send, g_recv, late_flight, split_late, after, "late")))
        trips["late"] = (fw, *_forward_start([fw[k] for k in halves], fw["conv_w"], "late"))
        return trips["late"][-1]

    def late_weights(after):
        fw, send, recv, flight, _ = trips["late"]
        fw.update(zip(halves, _forward_wait(send, recv, flight, after, "late")))
        return (fw["w_o"].reshape(D_MODEL, D_MODEL), fw["w_ff_a"], fw["w_ff_b"], fw["conv_w"], fw["w_ff_down"],
                fw["w_ple_gate"].reshape(D_MODEL, D_MODEL), fw["w_ple_in"])

    def swap_started(names, grads, tag):
        stacked = [g.reshape(N_SHARD, *w[k].shape[1:]) for k, g in zip(names, grads)]
        return (names, tag, *_swap_start(stacked, tag))

    def partial_sums(swap, after):
        names, tag, send, recv, stacked, gots, _ = swap
        stacked, got = _swap_wait(send, recv, stacked, gots, after, tag)
        pair = _pair_sums(f"rs_pair_{tag}", stacked, got, place)
        return (names, tag, *_exchange_start(pair, tag))

    def chip_summed(trip, after, dep):
        names, tag, send, recv, pair, lands, _ = trip
        pair, landed = _exchange_wait(send, recv, pair, lands, after, tag)
        return _chip_sums(f"rs_sum_{tag}", pair, landed, place, dep), names, tag

    def reduced(trip, after, dep):
        blocks, names, tag = chip_summed(trip, after, dep)
        return dict(zip(names, _sibling_join(blocks, tag)))

    def early_grads_landed(after):
        blocks, names, tag = chip_summed(trips["early"], after, trips["small"][-1])
        trips["join"] = (names, *_join_start(blocks, after, tag))
        return trips["join"][-1]

    def early_grads(grads):
        trips["swap"] = swap_started(list(grads), list(grads.values()), "early")
        return trips["swap"][-1]

    def early_grads_sent(after, small):
        trips["early"] = partial_sums(trips["swap"], after)
        stat3, stat1, zstat, cstat, dws, dbs = small
        sums = _small_chip_sums([stat3, stat1, zstat, cstat.reshape(N_SHARD * STAT_ROWS, FF_BLK),
                                 dws.reshape(N_HEADS * BLK, BLK), dbs])
        trips["small"] = _gather_start(sums, [False] * len(sums), trips["early"][-1], "small")
        return trips["small"][-1]

    grad_x, g_w_in = _local_step(
        x[0], p[0, 0], rope, loss_target[0], w_in_full, start_dep, late_landed, late_weights, early_grads, early_grads_sent,
        early_grads_landed, ln_z_g, ln_z_b, w_s, b_s, ln1_g, ln1_b, conv_b, ln2_g, ln2_b, b_ple_gate, ln3_g, ln3_b)

    trips["w_in"] = partial_sums(swap_started(["w_in"], [g_w_in], "w_in"), g_w_in)
    out = {}

    def adamw(red, tag):
        names = list(red)
        steps = _adamw_shards(f"adamw_{tag}", [w[k] for k in names], [red[k] for k in names], [m[k] for k in names],
                              [v[k] for k in names])
        for k, (d, nm, nv) in zip(names, steps):
            out[k] = (red[k].reshape(w[k].shape), d, nm, nv)

    names, j_send, j_recv, j_flight, _ = trips["join"]
    adamw(dict(zip(names, _join_wait(j_send, j_recv, j_flight, trips["w_in"][-1], "early"))), "early")
    adamw(reduced(trips["w_in"], out["w_o"][3], start_dep), "w_in")
    for k in TRANSPOSED:
        out[k] = tuple(jnp.swapaxes(a, 1, 2) for a in out[k])

    s_send, s_recv, s_flight, _ = trips["small"]
    red3, red1, redz, redc, red_ws, red_bs = _small_totals(
        _gather_wait(s_send, s_recv, s_flight, [False] * len(s_flight), out["w_in"][3], "small"))
    loss = (0.5 / D_MODEL) * jnp.sum(red3[5])
    g_conv_w = lax.dynamic_slice_in_dim(redc, chip * STAT_ROWS, STAT_ROWS, 0)
    names_small = [k for k in WEIGHTS if k not in BIG]
    out.update(_adamw_small(red3, red1, redz, g_conv_w, redc, red_ws, red_bs, {k: (w[k], m[k], v[k]) for k in names_small}))

    return (loss, grad_x[None], *[out[k][0] for k in WEIGHTS], *[out[k][1] for k in WEIGHTS],
            *[out[k][2] for k in WEIGHTS], *[out[k][3] for k in WEIGHTS])
```

```python
import math

import numpy as np
import jax
import jax.numpy as jnp
from jax import lax
from jax.experimental import pallas as pl
from jax.experimental.pallas import tpu as pltpu

F32 = jnp.float32
BF16 = jnp.bfloat16
MXU = BF16

D_MODEL = 1024
HEAD_DIM = 64
N_HEADS = 8
D_ATTN = 512
D_GMLP = 512
D_IN = 2560
DILATIONS = (1, 4, 16)
BLK = 128
ROPE_THETA = 500000.0
ROPE_DIM = 16
D_FF = 2816
D_PLE = 256
LN_EPS = 1e-5
ALPHA = 2.0 ** 0.25
NEG_INF = -1e30
N_SHARD = 4
W_IN_BLK = D_IN // N_SHARD
FF_BLK = D_FF // N_SHARD
ROW_BLK = D_MODEL // N_SHARD
ADAM_LR, ADAM_B1, ADAM_B2, ADAM_EPS, ADAM_WD, ADAM_STEP = 0.001, 0.9, 0.999, 1e-08, 0.01, 10

TM = 512
HALO = 8
ROW_GROUPS = 2
VMEM_LIMIT = 56 * 1024 * 1024


def _cp(**kw):
    return pltpu.CompilerParams(vmem_limit_bytes=VMEM_LIMIT, **kw)


def _full(shape):
    n = len(shape)
    return pl.BlockSpec(shape, lambda *_: (0,) * n)


def _gelu(x):
    return 0.5 * x * (1.0 + lax.erf(x * (1.0 / math.sqrt(2.0))))


def _gelu_grad(x):
    return 0.5 * (1.0 + lax.erf(x * (1.0 / math.sqrt(2.0)))) + x * jnp.exp(-0.5 * x * x) * (1.0 / math.sqrt(2.0 * math.pi))


def _ln_fwd(r):
    mu = jnp.mean(r, axis=-1, keepdims=True)
    xc = r - mu
    var = jnp.mean(xc * xc, axis=-1, keepdims=True)
    rstd = lax.rsqrt(var + LN_EPS)
    return xc * rstd, rstd


def _ln_bwd(dy, xhat, rstd, g):
    dxh = dy * g
    m1 = jnp.mean(dxh, axis=-1, keepdims=True)
    m2 = jnp.mean(dxh * xhat, axis=-1, keepdims=True)
    return rstd * (dxh - m1 - xhat * m2)


def _dot(a, b):
    return jnp.dot(a.astype(MXU), b.astype(MXU), preferred_element_type=F32)


def _dot_nt(a, b):
    return lax.dot_general(a.astype(MXU), b.astype(MXU), (((1,), (1,)), ((), ())), preferred_element_type=F32)


def _dot_tn(a, b):
    return lax.dot_general(a.astype(MXU), b.astype(MXU), (((0,), (0,)), ((), ())), preferred_element_type=F32)


def _colsum(v):
    return jnp.sum(v, axis=0, keepdims=True)


def _rope_tables(positions, t, dep):
    inv = np.float32(ROPE_THETA) ** (-np.arange(0, ROPE_DIM, 2, dtype=np.float32) / np.float32(ROPE_DIM))
    half = ROPE_DIM // 2
    pos_rep = jnp.repeat(positions.reshape(t // 16, 16), half, axis=1)
    inv_row = jnp.asarray(np.tile(inv, 16)[None, :], F32)

    def trig_body(pos_ref, inv_ref, dep_ref, cos_ref, sin_ref):
        ang = pos_ref[...].astype(F32) * inv_ref[...]
        cos_ref[...] = jnp.cos(ang)
        sin_ref[...] = jnp.sin(ang)

    vm = pl.BlockSpec(memory_space=pltpu.VMEM)
    cos8, sin8 = pl.pallas_call(
        trig_body, name="rope_trig", in_specs=[vm, vm, pl.BlockSpec(memory_space=pl.ANY)], out_specs=[vm, vm],
        out_shape=(jax.ShapeDtypeStruct((t // 16, 128), F32), jax.ShapeDtypeStruct((t // 16, 128), F32)),
    )(pos_rep, inv_row, dep)
    cos8 = cos8.reshape(t, half)
    sin8 = sin8.reshape(t, half)

    lane = np.arange(128) % HEAD_DIM
    sel = (np.arange(half)[:, None] == (lane % half)[None, :])
    e_cos = (sel & (lane < ROPE_DIM)[None, :]).astype(np.float32)
    e_s1 = -(sel & (lane < half)[None, :]).astype(np.float32)
    e_s2 = (sel & ((lane >= half) & (lane < ROPE_DIM))[None, :]).astype(np.float32)
    ones = (lane >= ROPE_DIM).astype(np.float32)[None, :]

    def expand_body(cos_ref, sin_ref, ec_ref, e1_ref, e2_ref, ones_ref, c_ref, s1_ref, s2_ref):
        hp = lax.Precision.HIGHEST
        c_ref[...] = jnp.dot(cos_ref[...], ec_ref[...], precision=hp, preferred_element_type=F32) + ones_ref[...]
        s1_ref[...] = jnp.dot(sin_ref[...], e1_ref[...], precision=hp, preferred_element_type=F32)
        s2_ref[...] = jnp.dot(sin_ref[...], e2_ref[...], precision=hp, preferred_element_type=F32)

    tab = jax.ShapeDtypeStruct((t, 128), F32)
    return pl.pallas_call(expand_body, name="rope_expand", out_shape=(tab, tab, tab), compiler_params=_cp())(
        cos8, sin8, jnp.asarray(e_cos), jnp.asarray(e_s1), jnp.asarray(e_s2), jnp.asarray(ones))


def _tile_heads(tab):
    return jnp.concatenate([tab] * (D_ATTN // 128), axis=1)


def _rope_apply(v, c, s1, s2):
    n = v.shape[1]
    half = ROPE_DIM // 2
    return v * c + pltpu.roll(v, n - half, 1) * s1 + pltpu.roll(v, half, 1) * s2


def _rope_apply_t(g, c, s1, s2):
    n = g.shape[1]
    half = ROPE_DIM // 2
    return g * c + pltpu.roll(g * s1, half, 1) + pltpu.roll(g * s2, n - half, 1)


LANE_CHUNKS = D_ATTN // 128
HEAD_LANES = 128 // N_HEADS


def _perm_shape(t, d, w, dtype):
    return jax.ShapeDtypeStruct((d, t // d, w), dtype)


def _perm_tile(d, w):
    return pl.BlockSpec((None if d == 1 else d, TM // d, w), lambda i: (0, i, 0))


def _to_planes(ref, scr, d, n_chunks, dtype):
    for r in range(d):
        for cc in range(n_chunks):
            ref[r, :, cc * 128:(cc + 1) * 128] = scr.at[cc][pl.ds(r, TM // d, stride=d), :].astype(dtype)


def _from_planes(ref, scr, d, n_chunks, accumulate=False):
    for r in range(d):
        for cc in range(n_chunks):
            rows = scr.at[cc]
            val = ref[r, :, cc * 128:(cc + 1) * 128].astype(F32)
            if accumulate:
                rows[pl.ds(r, TM // d, stride=d), :] += val
            else:
                rows[pl.ds(r, TM // d, stride=d), :] = val


def _chunks(val):
    return [val[:, cc * 128:(cc + 1) * 128] for cc in range(val.shape[1] // 128)]


def _unchunk(scr, n_chunks, base=0):
    return jnp.concatenate([scr[base + cc] for cc in range(n_chunks)], axis=1)


def _head_expand():
    src = np.arange(128)[:, None]
    dst = np.arange(D_ATTN)[None, :]
    return jnp.asarray((src == (dst // HEAD_DIM) * HEAD_LANES).astype(np.float32))


def _head_reduce():
    src = np.arange(D_ATTN)[:, None]
    dst = np.arange(128)[None, :]
    return jnp.asarray((src // HEAD_DIM == dst // HEAD_LANES).astype(np.float32))


def _dot_select(a, sel):
    hi = a.astype(BF16)
    lo = (a - hi.astype(F32)).astype(BF16)
    sel = sel.astype(BF16)
    return jnp.dot(hi, sel, preferred_element_type=F32) + jnp.dot(lo, sel, preferred_element_type=F32)


def _qkvuz(x, w_in, c_tab, s1_tab, s2_tab, ln_z_g, ln_z_b, w_s, b_full, dep):
    t = x.shape[0]
    nchunk = TM // BLK

    def body(x_ref, w_ref, c_ref, s1_ref, s2_ref, g_ref, b_ref, ws_ref, bf_ref, dep_ref,
             qkv1_ref, qkv4_ref, qkv16_ref, hu_ref, hz_ref, mixed_ref, gm_ref, xb_ref, h_scr, wm_scr, p_scr):
        @pl.when(pl.program_id(0) == 0)
        def _():
            row = lax.broadcasted_iota(jnp.int32, (BLK, BLK), 0)
            col = lax.broadcasted_iota(jnp.int32, (BLK, BLK), 1)
            for g in range(N_HEADS):
                wm_scr[g] = jnp.where(col <= row, ws_ref[g], 0.0).astype(MXU)

        xb = x_ref[...].astype(MXU)
        xb_ref[...] = xb
        for j in range(N_SHARD):
            h_scr[:, j * W_IN_BLK:(j + 1) * W_IN_BLK] = jnp.dot(xb, w_ref[j], preferred_element_type=F32)
        c, s1, s2 = _tile_heads(c_ref[...]), _tile_heads(s1_ref[...]), _tile_heads(s2_ref[...])
        q = _rope_apply(h_scr[:, 0:D_ATTN], c, s1, s2) * (1.0 / math.sqrt(HEAD_DIM))
        k = _rope_apply(h_scr[:, D_ATTN:2 * D_ATTN], c, s1, s2)
        for part, val in enumerate((q, k, h_scr[:, 2 * D_ATTN:3 * D_ATTN])):
            qkv1_ref[:, part * D_ATTN:(part + 1) * D_ATTN] = val.astype(MXU)
            for cc in range(LANE_CHUNKS):
                p_scr[part * LANE_CHUNKS + cc] = val[:, cc * 128:(cc + 1) * 128]
        _to_planes(qkv4_ref, p_scr, DILATIONS[1], 3 * LANE_CHUNKS, MXU)
        _to_planes(qkv16_ref, p_scr, DILATIONS[2], 3 * LANE_CHUNKS, MXU)
        hu = h_scr[:, 3 * D_ATTN:3 * D_ATTN + D_GMLP]
        hz = h_scr[:, 3 * D_ATTN + D_GMLP:]
        hu_ref[...] = hu
        hz_ref[...] = hz
        zhat, _ = _ln_fwd(_gelu(hz))
        zn = (zhat * g_ref[...] + b_ref[...]).astype(MXU)
        for ch in range(nchunk):
            rows = slice(ch * BLK, (ch + 1) * BLK)
            for g in range(N_HEADS):
                cols = slice(g * HEAD_DIM, (g + 1) * HEAD_DIM)
                mixed_ref[rows, cols] = jnp.dot(wm_scr[g], zn[rows, cols], preferred_element_type=F32) + bf_ref[:, cols]
        gm_ref[...] = (_gelu(hu) * mixed_ref[...]).astype(MXU)

    tok = lambda w: pl.BlockSpec((TM, w), lambda i: (i, 0))
    outs = [_perm_shape(t, d, 3 * D_ATTN, MXU) for d in DILATIONS] + [jax.ShapeDtypeStruct((t, D_GMLP), F32)] * 3 + [
        jax.ShapeDtypeStruct((t, D_GMLP), MXU), jax.ShapeDtypeStruct((t, D_MODEL), MXU)]
    return pl.pallas_call(
        body, name="qkvuz", grid=(t // TM,),
        in_specs=[tok(D_MODEL), _full(w_in.shape), tok(128), tok(128), tok(128), _full(ln_z_g.shape), _full(ln_z_b.shape),
                  _full(w_s.shape), _full(b_full.shape), pl.BlockSpec(memory_space=pl.ANY)],
        out_specs=[_perm_tile(d, 3 * D_ATTN) for d in DILATIONS] + [tok(D_ATTN)] * 4 + [tok(D_MODEL)], out_shape=outs,
        scratch_shapes=[pltpu.VMEM((TM, D_IN), F32), pltpu.VMEM((N_HEADS, BLK, BLK), MXU),
                        pltpu.VMEM((3 * LANE_CHUNKS, TM, 128), F32)],
        compiler_params=_cp(dimension_semantics=("arbitrary",)),
    )(x, w_in, c_tab, s1_tab, s2_tab, ln_z_g, ln_z_b, w_s, b_full, dep)


def _band_valid(n):
    i = lax.broadcasted_iota(jnp.int32, (BLK, 2 * BLK), 0)
    j = lax.broadcasted_iota(jnp.int32, (BLK, 2 * BLK), 1)
    return (j >= i) & (j <= i + BLK) & ((j >= BLK) | (n > 0))


def _attn_fwd(qkv, d, dep):
    _, l_sub, _ = qkv.shape
    nb = l_sub // BLK

    def body(q_ref, kp_ref, kc_ref, vp_ref, vc_ref, dep_ref, o_ref, l_ref):
        valid = _band_valid(pl.program_id(1))
        kcat = jnp.concatenate([kp_ref[...], kc_ref[...]], axis=0)
        vcat = jnp.concatenate([vp_ref[...], vc_ref[...]], axis=0)
        for h in range(N_HEADS):
            cols = slice(h * HEAD_DIM, (h + 1) * HEAD_DIM)
            s = jnp.where(valid, _dot_nt(q_ref[:, cols], kcat[:, cols]), NEG_INF)
            m = jnp.max(s, axis=-1, keepdims=True)
            e = jnp.exp(s - m)
            den = jnp.sum(e, axis=-1, keepdims=True)
            o_ref[:, cols] = _dot(e, vcat[:, cols]) * (1.0 / den)
            l_ref[:, h * HEAD_LANES:(h + 1) * HEAD_LANES] = jnp.broadcast_to(m + jnp.log(den), (BLK, HEAD_LANES))

    def blk(w, col, prev=False):
        return pl.BlockSpec((None, BLK, w), lambda r, n: (r, jnp.maximum(n - 1, 0) if prev else n, col))

    return pl.pallas_call(
        body, name=f"attn_fwd_d{d}", grid=(d, nb),
        in_specs=[blk(D_ATTN, 0), blk(D_ATTN, 1, True), blk(D_ATTN, 1), blk(D_ATTN, 2, True), blk(D_ATTN, 2),
                  pl.BlockSpec(memory_space=pl.ANY)],
        out_specs=[blk(D_ATTN, 0), blk(128, 0)],
        out_shape=[jax.ShapeDtypeStruct((d, l_sub, D_ATTN), F32), jax.ShapeDtypeStruct((d, l_sub, 128), F32)],
        compiler_params=_cp(dimension_semantics=("arbitrary", "arbitrary")),
    )(qkv, qkv, qkv, qkv, qkv, dep)


def _attn_bwd(qkv, do, lse, delta, d, dep):
    _, l_sub, _ = qkv.shape
    nb = l_sub // BLK
    whole = l_sub <= 8 * BLK

    def shares(n, q_ref, kp_ref, kc_ref, vp_ref, vc_ref, do_ref, l_ref, dl_ref, dq_ref):
        valid = _band_valid(n)
        kcat = jnp.concatenate([kp_ref[...], kc_ref[...]], axis=0)
        vcat = jnp.concatenate([vp_ref[...], vc_ref[...]], axis=0)
        for h in range(N_HEADS):
            cols = slice(h * HEAD_DIM, (h + 1) * HEAD_DIM)
            stat = slice(h * HEAD_LANES, h * HEAD_LANES + 1)
            qh, doh = q_ref[:, cols], do_ref[:, cols]
            p = jnp.where(valid, jnp.exp(_dot_nt(qh, kcat[:, cols]) - l_ref[:, stat]), 0.0)
            ds = p * (_dot_nt(doh, vcat[:, cols]) - dl_ref[:, stat])
            dq_ref[:, cols] = _dot(ds, kcat[:, cols])
            yield cols, _dot_tn(ds, qh), _dot_tn(p, doh)

    def body_whole(*refs):
        dk_ref, dv_ref = refs[10:]
        n = pl.program_id(1)
        cur = pl.ds(pl.multiple_of(n * BLK, BLK), BLK)
        prev = pl.ds(pl.multiple_of(jnp.maximum(n - 1, 0) * BLK, BLK), BLK)
        for cols, dk2, dv2 in shares(n, *refs[:8], refs[9]):
            dk_ref[cur, cols] = dk2[BLK:]
            dv_ref[cur, cols] = dv2[BLK:]
            dk_ref[prev, cols] += dk2[0:BLK]
            dv_ref[prev, cols] += dv2[0:BLK]

    def body_carry(*refs):
        dk_ref, dv_ref, ck_scr, cv_scr = refs[10:]
        n = pl.program_id(1)

        @pl.when(n == 0)
        def _():
            ck_scr[...] = jnp.zeros_like(ck_scr)
            cv_scr[...] = jnp.zeros_like(cv_scr)

        @pl.when(n < nb)
        def _():
            for cols, dk2, dv2 in shares(n, *refs[:8], refs[9]):
                dk_ref[:, cols] = ck_scr[:, cols] + dk2[0:BLK]
                dv_ref[:, cols] = cv_scr[:, cols] + dv2[0:BLK]
                ck_scr[:, cols] = dk2[BLK:]
                cv_scr[:, cols] = dv2[BLK:]

        @pl.when(n == nb)
        def _():
            dk_ref[...] = ck_scr[...]
            dv_ref[...] = cv_scr[...]

    def blk(w, col, shift=0):
        return pl.BlockSpec((None, BLK, w), lambda r, n: (r, jnp.clip(n - shift, 0, nb - 1), col))

    if whole:
        dkv_spec = pl.BlockSpec((None, l_sub, D_ATTN), lambda r, n: (r, 0, 0))
        body, steps, scratch = body_whole, nb, []
    else:
        dkv_spec = blk(D_ATTN, 0, 1)
        body, steps, scratch = body_carry, nb + 1, [pltpu.VMEM((BLK, D_ATTN), F32)] * 2
    return pl.pallas_call(
        body, name=f"attn_bwd_d{d}", grid=(d, steps),
        in_specs=[blk(D_ATTN, 0), blk(D_ATTN, 1, 1), blk(D_ATTN, 1), blk(D_ATTN, 2, 1), blk(D_ATTN, 2),
                  blk(D_ATTN, 0), blk(128, 0), blk(128, 0), pl.BlockSpec(memory_space=pl.ANY)],
        out_specs=[blk(D_ATTN, 0), dkv_spec, dkv_spec],
        out_shape=[jax.ShapeDtypeStruct((d, l_sub, D_ATTN), F32)] * 3,
        scratch_shapes=scratch,
        compiler_params=_cp(dimension_semantics=("arbitrary", "arbitrary")),
    )(qkv, qkv, qkv, qkv, qkv, do, lse, delta, dep)


def _mix_ln1(os_, ls_, gm, x, w_o, ln1_g, ln1_b, dep):
    t = x.shape[0]
    expand = _head_expand()

    def body(o1, o4, o16, l1, l4, l16, gm_ref, x_ref, wo_ref, g_ref, b_ref, ex_ref, dep_ref,
             attn_ref, lse1_ref, lse4_ref, lse16_ref, cat_ref, xhat_ref, rstd_ref, x1b_ref, o_scr, l_scr):
        _from_planes(o4, o_scr, DILATIONS[1], LANE_CHUNKS)
        _from_planes(o16, o_scr.at[pl.ds(LANE_CHUNKS, LANE_CHUNKS)], DILATIONS[2], LANE_CHUNKS)
        _from_planes(l4, l_scr, DILATIONS[1], 1)
        _from_planes(l16, l_scr.at[pl.ds(1, 1)], DILATIONS[2], 1)
        la, lb, lc = l1[...], l_scr[0], l_scr[1]
        m = jnp.maximum(jnp.maximum(la, lb), lc)
        ea, eb, ec = jnp.exp(la - m), jnp.exp(lb - m), jnp.exp(lc - m)
        den = ea + eb + ec
        inv = 1.0 / den
        wide = lambda w: _dot_select(w, ex_ref[...])
        attn = (wide(ea * inv) * o1[...] + wide(eb * inv) * _unchunk(o_scr, LANE_CHUNKS)
                + wide(ec * inv) * _unchunk(o_scr, LANE_CHUNKS, LANE_CHUNKS))
        attn_ref[...] = attn
        lse = m + jnp.log(den)
        lse1_ref[...] = lse
        l_scr[2] = lse
        _to_planes(lse4_ref, l_scr.at[pl.ds(2, 1)], DILATIONS[1], 1, F32)
        _to_planes(lse16_ref, l_scr.at[pl.ds(2, 1)], DILATIONS[2], 1, F32)
        cat_ref[:, 0:D_ATTN] = attn.astype(MXU)
        cat_ref[:, D_ATTN:] = gm_ref[...]
        mix = jnp.dot(cat_ref[...], wo_ref[...], preferred_element_type=F32)
        xhat, rstd = _ln_fwd(ALPHA * x_ref[...] + mix)
        xhat_ref[...] = xhat
        rstd_ref[...] = rstd
        x1b_ref[...] = (xhat * g_ref[...] + b_ref[...]).astype(MXU)

    tok = lambda w: pl.BlockSpec((TM, w), lambda i: (i, 0))
    outs = [jax.ShapeDtypeStruct((t, D_ATTN), F32)] + [_perm_shape(t, d, 128, F32) for d in DILATIONS] + [
        jax.ShapeDtypeStruct((t, D_MODEL), MXU), jax.ShapeDtypeStruct((t, D_MODEL), F32), jax.ShapeDtypeStruct((t, 1), F32),
        jax.ShapeDtypeStruct((t, D_MODEL), MXU)]
    return pl.pallas_call(
        body, name="mix_ln1", grid=(t // TM,),
        in_specs=[_perm_tile(d, D_ATTN) for d in DILATIONS] + [_perm_tile(d, 128) for d in DILATIONS]
        + [tok(D_GMLP), tok(D_MODEL), _full(w_o.shape), _full(ln1_g.shape), _full(ln1_b.shape), _full(expand.shape),
           pl.BlockSpec(memory_space=pl.ANY)],
        out_specs=[tok(D_ATTN)] + [_perm_tile(d, 128) for d in DILATIONS] + [tok(D_MODEL), tok(D_MODEL), tok(1), tok(D_MODEL)],
        out_shape=outs,
        scratch_shapes=[pltpu.VMEM((2 * LANE_CHUNKS, TM, 128), F32), pltpu.VMEM((3, TM, 128), F32)],
        compiler_params=_cp(dimension_semantics=("arbitrary",)),
    )(*os_, *ls_, gm, x, w_o, ln1_g, ln1_b, expand, dep)


def _conv_fwd(a_ext, w_ref, b_ref, rows):
    back = [pltpu.roll(a_ext, s, 0)[HALO:HALO + rows] for s in (1, 2)]
    return b_ref[...] + w_ref[2:3, :] * a_ext[HALO:HALO + rows] + w_ref[1:2, :] * back[0] + w_ref[0:1, :] * back[1]


def _ffn_in(x1b, w_a, w_b, conv_w, conv_b):
    t = x1b.shape[0]
    hb = TM // HALO

    def body(x_ref, xh_ref, wa_ref, wb_ref, cw_ref, cb_ref, apre_ref, act_ref, gate_ref, f_ref):
        i = pl.program_id(1)
        a_pre = _dot_nt(x_ref[...], wa_ref[...])
        a_halo = jnp.where(i > 0, _dot_nt(xh_ref[...], wa_ref[...]), 0.0)
        a = _conv_fwd(jnp.concatenate([a_halo, a_pre], axis=0), cw_ref, cb_ref, TM)
        b = _dot_nt(x_ref[...], wb_ref[...])
        cdf = 0.5 * (1.0 + lax.erf(a * (1.0 / math.sqrt(2.0))))
        pdf = jnp.exp(-0.5 * a * a) * (1.0 / math.sqrt(2.0 * math.pi))
        act = a * cdf
        apre_ref[...] = a_pre
        act_ref[...] = act
        gate_ref[...] = b * (cdf + a * pdf)
        f_ref[...] = (act * b).astype(MXU)

    blk = lambda r, c: pl.BlockSpec((None, r, c), lambda j, i: (j, 0, 0))
    tokj = pl.BlockSpec((None, TM, FF_BLK), lambda j, i: (j, i, 0))
    outs = [jax.ShapeDtypeStruct((N_SHARD, t, FF_BLK), F32)] * 3 + [jax.ShapeDtypeStruct((N_SHARD, t, FF_BLK), MXU)]
    return pl.pallas_call(
        body, name="ffn_in", grid=(N_SHARD, t // TM),
        in_specs=[pl.BlockSpec((TM, D_MODEL), lambda j, i: (i, 0)),
                  pl.BlockSpec((HALO, D_MODEL), lambda j, i: (jnp.maximum(i * hb - 1, 0), 0)),
                  blk(FF_BLK, D_MODEL), blk(FF_BLK, D_MODEL), blk(3, FF_BLK), blk(1, FF_BLK)],
        out_specs=[tokj, tokj, tokj, tokj], out_shape=outs,
        compiler_params=_cp(dimension_semantics=("arbitrary", "arbitrary")),
    )(x1b, x1b, w_a, w_b, conv_w, conv_b)


def _ffn_out_ln2(f, w_down, xhat1, ln1_g, ln1_b):
    t = xhat1.shape[0]

    def body(f_ref, wd_ref, xh_ref, g1_ref, b1_ref, xhat_ref, rstd_ref):
        half = TM // ROW_GROUPS
        for r0 in range(0, TM, half):
            rows = pl.ds(r0, half)
            ff = jnp.dot(f_ref[0, rows, :], wd_ref[0], preferred_element_type=F32)
            for j in range(1, N_SHARD):
                ff = ff + jnp.dot(f_ref[j, rows, :], wd_ref[j], preferred_element_type=F32)
            x1 = xh_ref[rows, :] * g1_ref[...] + b1_ref[...]
            xhat, rstd = _ln_fwd(ALPHA * x1 + ff)
            xhat_ref[rows, :] = xhat
            rstd_ref[rows, :] = rstd

    tok = lambda w: pl.BlockSpec((TM, w), lambda i: (i, 0))
    vec = _full((1, D_MODEL))
    outs = [jax.ShapeDtypeStruct((t, D_MODEL), F32), jax.ShapeDtypeStruct((t, 1), F32)]
    return pl.pallas_call(
        body, name="ffn_out_ln2", grid=(t // TM,),
        in_specs=[pl.BlockSpec((N_SHARD, TM, FF_BLK), lambda i: (0, i, 0)), _full(w_down.shape), tok(D_MODEL), vec, vec],
        out_specs=[tok(D_MODEL), tok(1)], out_shape=outs,
        compiler_params=_cp(dimension_semantics=("arbitrary",)),
    )(f, w_down, xhat1, ln1_g, ln1_b)


STAT_ROWS = 8


def _ple_loss_bwd(xhat2, rstd2, p, target, ln2_g, ln2_b, w_g, b_g, w_p, ln3_g, ln3_b):
    t = xhat2.shape[0]

    def body(xh2_ref, rs2_ref, p_ref, t_ref, g2_ref, b2_ref, wg_ref, bg_ref, wp_ref, g3_ref, b3_ref,
             dr2_ref, dr2b_ref, stat_ref, dwg_ref, dwp_ref, pp_scr, dwp_scr):
        @pl.when(pl.program_id(0) == 0)
        def _():
            stat_ref[...] = jnp.zeros_like(stat_ref)
            dwg_ref[...] = jnp.zeros_like(dwg_ref)
            dwp_scr[...] = jnp.zeros_like(dwp_scr)

        xhat2 = xh2_ref[...]
        x2 = xhat2 * g2_ref[...] + b2_ref[...]
        x2b = x2.astype(MXU)
        gate = jax.nn.sigmoid(jnp.dot(x2b, wg_ref[...], preferred_element_type=F32) + bg_ref[...])
        pb = p_ref[...].astype(MXU)
        for j in range(N_SHARD):
            pp_scr[:, j * ROW_BLK:(j + 1) * ROW_BLK] = jnp.dot(pb, wp_ref[j], preferred_element_type=F32)
        pp = pp_scr[...]
        xhat3, rstd3 = _ln_fwd(ALPHA * x2 + gate * pp)
        err = xhat3 * g3_ref[...] + b3_ref[...] - t_ref[...]
        dy = err * (1.0 / D_MODEL)
        dr3 = _ln_bwd(dy, xhat3, rstd3, g3_ref[...])
        dgp = dr3 * pp * gate * (1.0 - gate)
        dgp_b = dgp.astype(MXU)
        dwg_ref[...] += _dot_tn(x2b, dgp_b)
        dwp_scr[...] += _dot_tn(pb, dr3 * gate)
        dx2 = ALPHA * dr3 + _dot_nt(dgp_b, wg_ref[...])
        dr2 = _ln_bwd(dx2, xhat2, rs2_ref[...], g2_ref[...])
        dr2_ref[...] = dr2
        dr2b_ref[...] = dr2.astype(MXU)
        stat_ref[0:1, :] += _colsum(dy * xhat3)
        stat_ref[1:2, :] += _colsum(dy)
        stat_ref[2:3, :] += _colsum(dgp)
        stat_ref[3:4, :] += _colsum(dx2 * xhat2)
        stat_ref[4:5, :] += _colsum(dx2)
        stat_ref[5:6, :] += _colsum(err * err)

        @pl.when(pl.program_id(0) == t // TM - 1)
        def _():
            for j in range(N_SHARD):
                dwp_ref[j] = dwp_scr[:, j * ROW_BLK:(j + 1) * ROW_BLK]

    tok = lambda w: pl.BlockSpec((TM, w), lambda i: (i, 0))
    vec = _full((1, D_MODEL))
    outs = [jax.ShapeDtypeStruct((t, D_MODEL), F32), jax.ShapeDtypeStruct((t, D_MODEL), MXU),
            jax.ShapeDtypeStruct((STAT_ROWS, D_MODEL), F32), jax.ShapeDtypeStruct((D_MODEL, D_MODEL), F32),
            jax.ShapeDtypeStruct((N_SHARD, D_PLE, ROW_BLK), F32)]
    return pl.pallas_call(
        body, name="ple_loss_bwd", grid=(t // TM,),
        in_specs=[tok(D_MODEL), tok(1), tok(D_PLE), tok(D_MODEL), vec, vec, _full(w_g.shape), vec, _full(w_p.shape), vec, vec],
        out_specs=[tok(D_MODEL), tok(D_MODEL), _full((STAT_ROWS, D_MODEL)), _full((D_MODEL, D_MODEL)),
                   _full((N_SHARD, D_PLE, ROW_BLK))], out_shape=outs,
        scratch_shapes=[pltpu.VMEM((TM, D_MODEL), F32), pltpu.VMEM((D_PLE, D_MODEL), F32)],
        compiler_params=_cp(dimension_semantics=("arbitrary",)),
    )(xhat2, rstd2, p, target, ln2_g, ln2_b, w_g, b_g, w_p, ln3_g, ln3_b)


def _ffn_bwd(dr2, dr2b, a_pre, act, gate, w_down, w_a, w_b, conv_w, xhat1, rstd1, ln1_g, cat):
    t = dr2.shape[0]
    nt = t // TM
    hb = TM // HALO
    last_h = t // HALO - 1
    halo2 = 2 * HALO

    def body(dr_ref, drb_ref, drbn_ref, ap_ref, act_ref, gate_ref, gaten_ref, wd_ref, wa_ref, wb_ref, cw_ref,
             xh_ref, rs_ref, g1_ref, cat_ref, dap_ref, dbb_ref, dr1_ref, cstat_ref, lstat_ref, dwo_ref, acc_scr):
        i, j = pl.program_id(0), pl.program_id(1)

        @pl.when((i == 0) & (j == 0))
        def _():
            cstat_ref[...] = jnp.zeros_like(cstat_ref)
            lstat_ref[...] = jnp.zeros_like(lstat_ref)
            dwo_ref[...] = jnp.zeros_like(dwo_ref)

        half = TM // ROW_GROUPS
        parts = []
        for r0 in range(0, TM, half):
            rows = pl.ds(r0, half)
            last = r0 + half == TM

            def ext(ref, nxt):
                return jnp.concatenate([ref[rows], nxt[...]], axis=0) if last else ref[r0:r0 + half + HALO]

            drb = jnp.concatenate([drb_ref[rows, :], drbn_ref[...]], axis=0) if last else drb_ref[r0:r0 + half + halo2, :]
            df = _dot_nt(drb, wd_ref[...])[0:half + HALO]
            da = df * ext(gate_ref, gaten_ref)
            if last:
                da = jnp.concatenate([da[0:half], jnp.where(i < nt - 1, da[half:], 0.0)], axis=0)
            ahead = [da[0:half]] + [pltpu.roll(da, half + HALO - s, 0)[0:half] for s in (1, 2)]
            da_pre = cw_ref[2:3, :] * ahead[0] + cw_ref[1:2, :] * ahead[1] + cw_ref[0:1, :] * ahead[2]
            dbb = df[0:half] * act_ref[rows, :]
            dap_ref[rows, :] = da_pre.astype(MXU)
            dbb_ref[rows, :] = dbb.astype(MXU)
            for kk in range(3):
                cstat_ref[j, kk:kk + 1, :] += _colsum(ahead[2 - kk] * ap_ref[rows, :])
            cstat_ref[j, 3:4, :] += _colsum(ahead[0])
            parts.append(_dot(da_pre, wa_ref[...]) + _dot(dbb, wb_ref[...]))
        part = jnp.concatenate(parts, axis=0)

        @pl.when(j == 0)
        def _():
            acc_scr[...] = ALPHA * dr_ref[...] + part

        @pl.when(j > 0)
        def _():
            acc_scr[...] += part

        @pl.when(j == N_SHARD - 1)
        def _():
            dx1 = acc_scr[...]
            xhat1 = xh_ref[...]
            lstat_ref[0:1, :] += _colsum(dx1 * xhat1)
            lstat_ref[1:2, :] += _colsum(dx1)
            dr1 = _ln_bwd(dx1, xhat1, rs_ref[...], g1_ref[...])
            dr1_ref[...] = dr1
            dwo_ref[...] += _dot_tn(cat_ref[...], dr1)

    tok = lambda w: pl.BlockSpec((TM, w), lambda i, j: (i, 0))
    tokj = pl.BlockSpec((None, TM, FF_BLK), lambda i, j: (j, i, 0))
    nextj = pl.BlockSpec((None, HALO, FF_BLK), lambda i, j: (j, jnp.minimum((i + 1) * hb, last_h), 0))
    blk = lambda r, c: pl.BlockSpec((None, r, c), lambda i, j: (j, 0, 0))
    outs = [jax.ShapeDtypeStruct((N_SHARD, t, FF_BLK), MXU)] * 2 + [
        jax.ShapeDtypeStruct((t, D_MODEL), F32), jax.ShapeDtypeStruct((N_SHARD, STAT_ROWS, FF_BLK), F32),
        jax.ShapeDtypeStruct((STAT_ROWS, D_MODEL), F32), jax.ShapeDtypeStruct((D_MODEL, D_MODEL), F32)]
    return pl.pallas_call(
        body, name="ffn_bwd", grid=(nt, N_SHARD),
        in_specs=[tok(D_MODEL), tok(D_MODEL),
                  pl.BlockSpec((halo2, D_MODEL), lambda i, j: (jnp.minimum((i + 1) * (hb // 2), last_h // 2), 0)),
                  tokj, tokj, tokj, nextj, blk(FF_BLK, D_MODEL), blk(FF_BLK, D_MODEL), blk(FF_BLK, D_MODEL),
                  blk(3, FF_BLK), tok(D_MODEL), tok(1), _full((1, D_MODEL)), tok(D_MODEL)],
        out_specs=[tokj, tokj, tok(D_MODEL), _full((N_SHARD, STAT_ROWS, FF_BLK)), _full((STAT_ROWS, D_MODEL)),
                   _full((D_MODEL, D_MODEL))], out_shape=outs,
        scratch_shapes=[pltpu.VMEM((TM, D_MODEL), F32)],
        compiler_params=_cp(dimension_semantics=("arbitrary", "arbitrary")),
    )(dr2, dr2b, dr2b, a_pre, act, gate, gate, w_down, w_a, w_b, conv_w, xhat1, rstd1, ln1_g, cat)


def _mix_bwd(dr1, w_o, hu, hz, mixed, attn, ln_z_g, ln_z_b, w_s, dep):
    t = dr1.shape[0]
    nchunk = TM // BLK

    def body(dr_ref, wo_ref, hu_ref, hz_ref, mx_ref, attn_ref, g_ref, b_ref, ws_ref, grp_ref, red_ref, dep_ref,
             do1_ref, do4_ref, do16_ref, dl1_ref, dl4_ref, dl16_ref, duz_ref, dws_ref, dbs_ref, zstat_ref,
             wm_scr, dzn_scr, dbsum_scr, do_scr, dl_scr):
        @pl.when(pl.program_id(0) == 0)
        def _():
            row = lax.broadcasted_iota(jnp.int32, (BLK, BLK), 0)
            col = lax.broadcasted_iota(jnp.int32, (BLK, BLK), 1)
            for g in range(N_HEADS):
                wm_scr[g] = jnp.where(col <= row, ws_ref[g], 0.0).astype(MXU)
            dws_ref[...] = jnp.zeros_like(dws_ref)
            dbsum_scr[...] = jnp.zeros_like(dbsum_scr)
            zstat_ref[...] = jnp.zeros_like(zstat_ref)

        dcat = _dot_nt(dr_ref[...], wo_ref[...])
        dattn = dcat[:, 0:D_ATTN]
        do1_ref[...] = dattn.astype(MXU)
        for cc, val in enumerate(_chunks(dattn)):
            do_scr[cc] = val
        _to_planes(do4_ref, do_scr, DILATIONS[1], LANE_CHUNKS, MXU)
        _to_planes(do16_ref, do_scr, DILATIONS[2], LANE_CHUNKS, MXU)
        delta = _dot_select(dattn * attn_ref[...], red_ref[...])
        dl1_ref[...] = delta
        dl_scr[0] = delta
        _to_planes(dl4_ref, dl_scr, DILATIONS[1], 1, F32)
        _to_planes(dl16_ref, dl_scr, DILATIONS[2], 1, F32)
        dgm = dcat[:, D_ATTN:]
        hu, hz = hu_ref[...], hz_ref[...]
        u = _gelu(hu)
        duz_ref[:, 0:D_GMLP] = (dgm * mx_ref[...] * _gelu_grad(hu)).astype(MXU)
        dmixed = dgm * u
        dmb = dmixed.astype(MXU)
        zhat, rstd = _ln_fwd(_gelu(hz))
        znb = (zhat * g_ref[...] + b_ref[...]).astype(MXU)
        dbs_acc = jnp.zeros((BLK, D_GMLP), F32)
        for ch in range(nchunk):
            rows = slice(ch * BLK, (ch + 1) * BLK)
            dbs_acc = dbs_acc + dmixed[rows]
            for g in range(N_HEADS):
                cols = slice(g * HEAD_DIM, (g + 1) * HEAD_DIM)
                dzn_scr[rows, cols] = _dot_tn(wm_scr[g], dmb[rows, cols])
                dws_ref[g] += _dot_nt(dmb[rows, cols], znb[rows, cols])
        dbsum_scr[...] += dbs_acc
        dzn = dzn_scr[...]
        zstat_ref[0:1, :] += _colsum(dzn * zhat)
        zstat_ref[1:2, :] += _colsum(dzn)
        duz_ref[:, D_GMLP:] = (_ln_bwd(dzn, zhat, rstd, g_ref[...]) * _gelu_grad(hz)).astype(MXU)

        @pl.when(pl.program_id(0) == nt - 1)
        def _():
            row = lax.broadcasted_iota(jnp.int32, (BLK, BLK), 0)
            col = lax.broadcasted_iota(jnp.int32, (BLK, BLK), 1)
            for g in range(N_HEADS):
                dws_ref[g] = jnp.where(col <= row, dws_ref[g], 0.0)
            dbs_ref[...] = lax.dot_general(grp_ref[...], dbsum_scr[...], (((1,), (1,)), ((), ())),
                                           precision=lax.Precision.HIGHEST, preferred_element_type=F32)

    nt = t // TM
    tok = lambda w: pl.BlockSpec((TM, w), lambda i: (i, 0))
    grp = jnp.asarray((np.arange(D_GMLP)[None, :] // HEAD_DIM == np.arange(N_HEADS)[:, None]).astype(np.float32))
    red = _head_reduce()
    outs = [_perm_shape(t, d, D_ATTN, MXU) for d in DILATIONS] + [_perm_shape(t, d, 128, F32) for d in DILATIONS] + [
        jax.ShapeDtypeStruct((t, 2 * D_GMLP), MXU),
        jax.ShapeDtypeStruct((N_HEADS, BLK, BLK), F32), jax.ShapeDtypeStruct((N_HEADS, BLK), F32),
        jax.ShapeDtypeStruct((STAT_ROWS, D_GMLP), F32)]
    return pl.pallas_call(
        body, name="mix_bwd", grid=(t // TM,),
        in_specs=[tok(D_MODEL), _full(w_o.shape), tok(D_GMLP), tok(D_GMLP), tok(D_GMLP), tok(D_ATTN), _full(ln_z_g.shape),
                  _full(ln_z_b.shape), _full(w_s.shape), _full(grp.shape), _full(red.shape), pl.BlockSpec(memory_space=pl.ANY)],
        out_specs=[_perm_tile(d, D_ATTN) for d in DILATIONS] + [_perm_tile(d, 128) for d in DILATIONS]
        + [tok(2 * D_GMLP), _full((N_HEADS, BLK, BLK)), _full((N_HEADS, BLK)), _full((STAT_ROWS, D_GMLP))],
        out_shape=outs,
        scratch_shapes=[pltpu.VMEM((N_HEADS, BLK, BLK), MXU), pltpu.VMEM((TM, D_GMLP), F32), pltpu.VMEM((BLK, D_GMLP), F32),
                        pltpu.VMEM((LANE_CHUNKS, TM, 128), F32), pltpu.VMEM((1, TM, 128), F32)],
        compiler_params=_cp(dimension_semantics=("arbitrary",)),
    )(dr1, w_o, hu, hz, mixed, attn, ln_z_g, ln_z_b, w_s, grp, red, dep)


def _dx_in(dqs, dks, dvs, duz, dr1, w_in, c_tab, s1_tab, s2_tab):
    t = dr1.shape[0]

    def body(dq1, dq4, dq16, dk1, dk4, dk16, dv1, dv4, dv16, duz_ref, dr_ref, w_ref, c_ref, s1_ref, s2_ref,
             dh_ref, dx_ref, acc_scr):
        sums = []
        for part, (g1, g4, g16) in enumerate(((dq1, dq4, dq16), (dk1, dk4, dk16), (dv1, dv4, dv16))):
            acc = acc_scr.at[pl.ds(part * LANE_CHUNKS, LANE_CHUNKS)]
            for cc in range(LANE_CHUNKS):
                acc[cc] = g1[:, cc * 128:(cc + 1) * 128]
            _from_planes(g4, acc, DILATIONS[1], LANE_CHUNKS, accumulate=True)
            _from_planes(g16, acc, DILATIONS[2], LANE_CHUNKS, accumulate=True)
            sums.append(_unchunk(acc_scr, LANE_CHUNKS, part * LANE_CHUNKS))
        c, s1, s2 = _tile_heads(c_ref[...]), _tile_heads(s1_ref[...]), _tile_heads(s2_ref[...])
        dh_ref[:, 0:D_ATTN] = _rope_apply_t(sums[0] * (1.0 / math.sqrt(HEAD_DIM)), c, s1, s2).astype(MXU)
        dh_ref[:, D_ATTN:2 * D_ATTN] = _rope_apply_t(sums[1], c, s1, s2).astype(MXU)
        dh_ref[:, 2 * D_ATTN:3 * D_ATTN] = sums[2].astype(MXU)
        dh_ref[:, 3 * D_ATTN:] = duz_ref[...]
        dx = ALPHA * dr_ref[...]
        for j in range(N_SHARD):
            dx = dx + _dot_nt(dh_ref[:, j * W_IN_BLK:(j + 1) * W_IN_BLK], w_ref[j])
        dx_ref[...] = dx

    tok = lambda w: pl.BlockSpec((TM, w), lambda i: (i, 0))
    outs = [jax.ShapeDtypeStruct((t, D_IN), MXU), jax.ShapeDtypeStruct((t, D_MODEL), F32)]
    return pl.pallas_call(
        body, name="dx_in", grid=(t // TM,),
        in_specs=[_perm_tile(d, D_ATTN) for d in DILATIONS] * 3
        + [tok(2 * D_GMLP), tok(D_MODEL), _full(w_in.shape), tok(128), tok(128), tok(128)],
        out_specs=[tok(D_IN), tok(D_MODEL)], out_shape=outs,
        scratch_shapes=[pltpu.VMEM((3 * LANE_CHUNKS, TM, 128), F32)],
        compiler_params=_cp(dimension_semantics=("arbitrary",)),
    )(*dqs, *dks, *dvs, duz, dr1, w_in, c_tab, s1_tab, s2_tab)


def _wgrad(name, x, dy, x_spec, dy_spec, out_spec, out_shape, grid, dep=None):
    deps = [] if dep is None else [dep]

    def body(x_ref, dy_ref, *rest):
        rest[-1][...] = _dot_tn(x_ref[...], dy_ref[...])

    return pl.pallas_call(
        body, name=name, grid=grid, in_specs=[x_spec, dy_spec] + [pl.BlockSpec(memory_space=pl.ANY)] * len(deps),
        out_specs=out_spec, out_shape=jax.ShapeDtypeStruct(out_shape, F32),
        compiler_params=_cp(dimension_semantics=("arbitrary",) * len(grid)),
    )(x, dy, *deps)


def _wgrad_pair(name, xa, xb, dy, x_spec, dy_spec, out_spec, out_shape, grid):
    def body(xa_ref, xb_ref, dy_ref, oa_ref, ob_ref):
        dy = dy_ref[...]
        oa_ref[...] = _dot_tn(xa_ref[...], dy)
        ob_ref[...] = _dot_tn(xb_ref[...], dy)

    return pl.pallas_call(
        body, name=name, grid=grid, in_specs=[x_spec, x_spec, dy_spec], out_specs=[out_spec, out_spec],
        out_shape=[jax.ShapeDtypeStruct(out_shape, F32)] * 2,
        compiler_params=_cp(dimension_semantics=("arbitrary",) * len(grid)),
    )(xa, xb, dy)


def _local_step(x, p, rope, target, w_in, start_dep, late_landed, late_weights, early_grads, early_grads_sent,
                early_grads_landed,
                ln_z_g, ln_z_b, w_s, b_s, ln1_g, ln1_b, conv_b, ln2_g, ln2_b, b_g, ln3_g, ln3_b):
    t = x.shape[0]
    half = TM
    c_tab, s1_tab, s2_tab = rope
    b_full = jnp.repeat(jnp.transpose(b_s[0]), HEAD_DIM, axis=1)
    conv_b4 = conv_b.reshape(N_SHARD, 1, FF_BLK)
    *qkvs, hu, hz, mixed, gm, xb = _qkvuz(x, w_in, c_tab, s1_tab, s2_tab, ln_z_g, ln_z_b, w_s[0], b_full, start_dep)
    branches = [_attn_fwd(qkv, d, start_dep) for qkv, d in zip(qkvs[:2], DILATIONS[:2])]
    dep = late_landed(branches[-1][1])
    branches.append(_attn_fwd(qkvs[2], DILATIONS[2], dep))
    w_o, w_a, w_b, conv_w, w_down, w_g, w_p = late_weights(branches[-1][1])
    attn, *lses, cat, xhat1, rstd1, x1b = _mix_ln1(
        [o for o, _ in branches], [l for _, l in branches], gm, x, w_o, ln1_g, ln1_b, dep)
    a_pre, act, gate, f = _ffn_in(x1b, w_a, w_b, conv_w, conv_b4)
    xhat2, rstd2 = _ffn_out_ln2(f, w_down, xhat1, ln1_g, ln1_b)
    dr2, dr2b, stat3, g_w_g, g_w_p = _ple_loss_bwd(xhat2, rstd2, p, target, ln2_g, ln2_b, w_g, b_g, w_p, ln3_g, ln3_b)
    da_pre, dbb, dr1, cstat, stat1, g_w_o = _ffn_bwd(dr2, dr2b, a_pre, act, gate, w_down, w_a, w_b, conv_w, xhat1, rstd1,
                                                    ln1_g, cat)

    full_t = lambda w, im: pl.BlockSpec((t, w), im)
    ffj = pl.BlockSpec((None, t, FF_BLK), lambda j, kk: (j, 0, 0))
    early = dict(
        w_ple_gate=g_w_g, w_ple_in=g_w_p,
        w_ff_down=_wgrad("dw_down", f, dr2b, ffj, full_t(half, lambda j, n: (0, n)),
                         pl.BlockSpec((None, FF_BLK, half), lambda j, n: (j, 0, n)), (N_SHARD, FF_BLK, D_MODEL), (N_SHARD, 2)),
        **dict(zip(("w_ff_a", "w_ff_b"), _wgrad_pair(
            "dw_ab", da_pre, dbb, x1b, ffj, full_t(half, lambda j, n: (0, n)),
            pl.BlockSpec((None, FF_BLK, half), lambda j, n: (j, 0, n)), (N_SHARD, FF_BLK, D_MODEL), (N_SHARD, 2)))),
        w_o=g_w_o)
    dep = early_grads(early)

    do1, do4, do16, dl1, dl4, dl16, duz, dws, dbs, zstat = _mix_bwd(
        dr1, w_o, hu, hz, mixed, attn, ln_z_g, ln_z_b, w_s[0], dep)
    dep = early_grads_sent(duz, (stat3, stat1, zstat, cstat, dws, dbs))
    dqkv = [_attn_bwd(qkv, do, lse, dl, d, dep)
            for qkv, do, lse, dl, d in zip(qkvs, (do1, do4, do16), lses, (dl1, dl4, dl16), DILATIONS)]
    dh, grad_x = _dx_in([g[0] for g in dqkv], [g[1] for g in dqkv], [g[2] for g in dqkv], duz, dr1, w_in,
                        c_tab, s1_tab, s2_tab)
    dep = early_grads_landed(grad_x)
    g_w_in = _wgrad("dw_in", xb, dh, full_t(half, lambda j, kk: (0, kk)), full_t(W_IN_BLK, lambda j, kk: (0, j)),
                    pl.BlockSpec((None, half, W_IN_BLK), lambda j, kk: (j, kk, 0)), (N_SHARD, D_MODEL, W_IN_BLK), (N_SHARD, 2),
                    dep)
    return grad_x, g_w_in


def _tile_rows(rows, mult, steps):
    if rows % mult:
        return rows
    return next(rows // k for k in range(steps, rows + 1) if rows % k == 0 and (rows // k) % mult == 0)


def _grid_spec(grid, in_specs, out_specs):
    return pltpu.PrefetchScalarGridSpec(num_scalar_prefetch=1, grid=grid, in_specs=in_specs, out_specs=out_specs)


def _on_own_steps(i, count, steps, work):
    if count == steps:
        work()
    else:
        pl.when(i < count)(work)


def _place_shards(name, ws, dtypes, place, dep):
    n = len(ws)
    tiles = [_tile_rows(w.shape[0], 16, 8) for w in ws]
    counts = [w.shape[0] // t for w, t in zip(ws, tiles)]
    steps = max(counts)

    def body(s_ref, *refs):
        i = pl.program_id(0)
        for a in range(n):
            def work(a=a):
                refs[n + 1 + a][...] = refs[a][...].astype(dtypes[a])
            _on_own_steps(i, counts[a], steps, work)

    def tile(a, lead):
        last = counts[a] - 1
        if lead:
            return pl.BlockSpec((None, tiles[a], ws[a].shape[1]), lambda i, s: (s[0], jnp.minimum(i, last), 0))
        return pl.BlockSpec((tiles[a], ws[a].shape[1]), lambda i, s: (jnp.minimum(i, last), 0))

    return pl.pallas_call(
        body, name=name,
        grid_spec=_grid_spec((steps,), [tile(a, False) for a in range(n)] + [pl.BlockSpec(memory_space=pl.ANY)],
                             [tile(a, True) for a in range(n)]),
        out_shape=[jax.ShapeDtypeStruct((N_SHARD, *w.shape), dt) for w, dt in zip(ws, dtypes)],
        compiler_params=_cp())(place, *ws, dep)


def _pair_sums(name, mines, gots, place):
    n = len(mines)
    tiles = [_tile_rows(g.shape[1], 16, 2) for g in gots]
    per_blk = [g.shape[1] // t for g, t in zip(gots, tiles)]
    counts = [N_SHARD * nh for nh in per_blk]
    steps = max(counts)

    def body(s_ref, *refs):
        i = pl.program_id(0)
        for a in range(n):
            def work(a=a):
                refs[2 * n + a][...] = (refs[a][...] + refs[n + a][...]).astype(BF16)
            _on_own_steps(i, counts[a], steps, work)

    def tile(a, mine):
        nh, last = per_blk[a], counts[a] - 1

        def index(i, s):
            g = jnp.minimum(i, last)
            return (g // nh, (s[1] * nh if mine else 0) + g % nh, 0)

        return pl.BlockSpec((None, tiles[a], gots[a].shape[2]), index)

    return pl.pallas_call(
        body, name=name,
        grid_spec=_grid_spec((steps,), [tile(a, True) for a in range(n)] + [tile(a, False) for a in range(n)],
                             [tile(a, False) for a in range(n)]),
        out_shape=[jax.ShapeDtypeStruct(g.shape, BF16) for g in gots], compiler_params=_cp())(place, *mines, *gots)


def _chip_sums(name, owns, landeds, place, dep):
    n = len(owns)
    tiles = [_tile_rows(o.shape[1], 16, 8) for o in owns]
    counts = [o.shape[1] // t for o, t in zip(owns, tiles)]
    steps = max(counts)

    def body(s_ref, *refs):
        i = pl.program_id(0)
        for a in range(n):
            def work(a=a):
                own, l1, l2, l3 = (refs[4 * a + k][...].astype(F32) for k in range(4))
                refs[4 * n + 1 + a][...] = ((own + l1) + l2) + l3
            _on_own_steps(i, counts[a], steps, work)

    def slot(a, d):
        last = counts[a] - 1
        return pl.BlockSpec((None, tiles[a], owns[a].shape[2]), lambda i, s: ((s[0] + d) % N_SHARD, jnp.minimum(i, last), 0))

    def out(a):
        nh, last = counts[a], counts[a] - 1
        return pl.BlockSpec((tiles[a], owns[a].shape[2]), lambda i, s: (s[1] * nh + jnp.minimum(i, last), 0))

    operands = [x for o, l in zip(owns, landeds) for x in (o, l, l, l)]
    return pl.pallas_call(
        body, name=name,
        grid_spec=_grid_spec((steps,), [slot(a, d) for a in range(n) for d in range(4)] + [pl.BlockSpec(memory_space=pl.ANY)],
                             [out(a) for a in range(n)]),
        out_shape=[jax.ShapeDtypeStruct((2 * o.shape[1], o.shape[2]), F32) for o in owns],
        compiler_params=_cp())(place, *operands, dep)


def _adamw_math(w, g, m, v):
    m = ADAM_B1 * m + (1.0 - ADAM_B1) * g
    v = ADAM_B2 * v + (1.0 - ADAM_B2) * (g * g)
    m_hat = m / (1.0 - ADAM_B1 ** ADAM_STEP)
    v_hat = v / (1.0 - ADAM_B2 ** ADAM_STEP)
    delta = -ADAM_LR * (m_hat / (jnp.sqrt(v_hat) + ADAM_EPS) + ADAM_WD * w)
    return delta, m, v


def _adamw_shards(name, ws, gs, ms, vs):
    n = len(ws)
    tiles = [_tile_rows(w.shape[1], 8, 8) for w in ws]
    counts = [w.shape[1] // t for w, t in zip(ws, tiles)]
    steps = max(counts)

    def body(*refs):
        i = pl.program_id(0)
        for a in range(n):
            def work(a=a):
                w_ref, g_ref, m_ref, v_ref = refs[4 * a:4 * a + 4]
                d_ref, nm_ref, nv_ref = refs[4 * n + 3 * a:4 * n + 3 * a + 3]
                d_ref[...], nm_ref[...], nv_ref[...] = _adamw_math(w_ref[...], g_ref[...], m_ref[...], v_ref[...])
            _on_own_steps(i, counts[a], steps, work)

    def tile(a, lead):
        last, c = counts[a] - 1, ws[a].shape[2]
        if lead:
            return pl.BlockSpec((None, tiles[a], c), lambda i: (0, jnp.minimum(i, last), 0))
        return pl.BlockSpec((tiles[a], c), lambda i: (jnp.minimum(i, last), 0))

    res = pl.pallas_call(
        body, name=name, grid=(steps,),
        in_specs=[tile(a, lead) for a in range(n) for lead in (True, False, True, True)],
        out_specs=[tile(a, True) for a in range(n) for _ in range(3)],
        out_shape=[jax.ShapeDtypeStruct(w.shape, F32) for w in ws for _ in range(3)],
        compiler_params=_cp())(*[x for quad in zip(ws, gs, ms, vs) for x in quad])
    return [tuple(res[3 * a:3 * a + 3]) for a in range(n)]


MESH = pl.DeviceIdType.MESH
ANY = pl.BlockSpec(memory_space=pl.ANY)


def _place():
    x, y, c = lax.axis_index("x"), lax.axis_index("y"), lax.axis_index("c")
    chips = [(1 - x, y), (x, 1 - y), (1 - x, 1 - y)]
    return x, y, c, 2 * x + y, chips


def _remote(src, dst, send_sem, recv_sem, dev):
    return pltpu.make_async_remote_copy(src_ref=src, dst_ref=dst, send_sem=send_sem, recv_sem=recv_sem,
                                        device_id=dev, device_id_type=MESH)


def _half(ref, hc, rows):
    return ref.at[pl.ds(hc * (rows // 2), rows // 2)]


def _sibling_join(blocks, tag):
    n = len(blocks)

    def body(*refs):
        outs = refs[n:2 * n]
        send, recv = refs[2 * n:]
        x, y, c, _, _ = _place()
        cps = []
        for a in range(n):
            h = blocks[a].shape[0] // 2
            mine = outs[a].at[pl.ds(c * h, h)]
            cp = _remote(mine, mine, send.at[a], recv.at[a], (x, y, 1 - c))
            cp.start()
            cps.append(cp)
        for a, cp in enumerate(cps):
            h = blocks[a].shape[0] // 2
            theirs = outs[a].at[pl.ds((1 - c) * h, h)]
            _remote(theirs, theirs, send.at[a], recv.at[a], (x, y, 1 - c)).wait_recv()
            cp.wait_send()

    sem = pltpu.SemaphoreType.DMA
    return pl.pallas_call(body, name=f"rs_sibling_join_{tag}", in_specs=[ANY] * n, out_specs=[ANY] * n,
                          out_shape=[jax.ShapeDtypeStruct(b_.shape, b_.dtype) for b_ in blocks],
                          input_output_aliases={a: a for a in range(n)},
                          scratch_shapes=[sem((n,)), sem((n,))])(*blocks)


def _join_start(blocks, after, tag):
    n = len(blocks)

    def body(*refs):
        ins = refs[:n]
        send, recv = refs[n + 1], refs[n + 2]
        token = refs[2 * n + 3]
        x, y, c, _, _ = _place()
        for a in range(n):
            h = blocks[a].shape[0] // 2
            mine = ins[a].at[pl.ds(c * h, h)]
            _remote(mine, mine, send.at[a], recv.at[a], (x, y, 1 - c)).start()
        token[...] = jnp.zeros_like(token)

    sems = pltpu.SemaphoreType.DMA((n,))
    res = pl.pallas_call(
        body, name=f"join_start_{tag}", in_specs=[HBM] * n + [ANY],
        out_specs=[SEM, SEM] + [HBM] * n + [pl.BlockSpec(memory_space=pltpu.VMEM)],
        out_shape=[sems, sems] + [pltpu.HBM(b_.shape, b_.dtype) for b_ in blocks] + [TOKEN],
        input_output_aliases={a: a + 2 for a in range(n)}, compiler_params=_in_flight_params(),
    )(*[_in_hbm(b_) for b_ in blocks], after)
    return res[0], res[1], res[2:2 + n], res[2 + n]


def _join_wait(send, recv, blocks, after, tag):
    n = len(blocks)

    def body(*refs):
        ins = refs[:n]
        send_ref, recv_ref = refs[n], refs[n + 1]
        x, y, c, _, _ = _place()
        for a in range(n):
            h = blocks[a].shape[0] // 2
            mine, theirs = ins[a].at[pl.ds(c * h, h)], ins[a].at[pl.ds((1 - c) * h, h)]
            _remote(mine, mine, send_ref.at[a], recv_ref.at[a], (x, y, 1 - c)).wait_send()
            _remote(theirs, theirs, send_ref.at[a], recv_ref.at[a], (x, y, 1 - c)).wait_recv()

    return pl.pallas_call(
        body, name=f"join_wait_{tag}", in_specs=[HBM] * n + [SEM, SEM, ANY], out_specs=[HBM] * n,
        out_shape=[pltpu.HBM(b_.shape, b_.dtype) for b_ in blocks],
        input_output_aliases={a: a for a in range(n)}, compiler_params=_in_flight_params(),
    )(*blocks, send, recv, after)


HBM = pl.BlockSpec(memory_space=pltpu.HBM)
SEM = pl.BlockSpec(memory_space=pltpu.SEMAPHORE)
TOKEN = jax.ShapeDtypeStruct((8, 128), F32)


def _in_flight_params():
    return pltpu.CompilerParams(has_side_effects=pltpu.SideEffectType.DATAFLOW_SIDE_EFFECTING)


def _in_hbm(a):
    return pltpu.with_memory_space_constraint(a, pltpu.HBM)


def _gather_piece(ref, rows, split, slot, hc):
    return _half(ref.at[slot], hc, rows) if split else ref.at[slot]


def _gather_start(stacks, split, after, tag):
    n = len(stacks)

    def body(*refs):
        ins = refs[:n]
        send, recv = refs[n + 1], refs[n + 2]
        token = refs[2 * n + 3]
        _, _, c, j, chips = _place()
        for a in range(n):
            mine = _gather_piece(ins[a], stacks[a].shape[1], split[a], j, c)
            for t in range(3):
                _remote(mine, mine, send.at[3 * a + t], recv.at[3 * a + t], (*chips[t], c)).start()
        token[...] = jnp.zeros_like(token)

    sems = pltpu.SemaphoreType.DMA((3 * n,))
    res = pl.pallas_call(
        body, name=f"gather_start_{tag}", in_specs=[HBM] * n + [ANY],
        out_specs=[SEM, SEM] + [HBM] * n + [pl.BlockSpec(memory_space=pltpu.VMEM)],
        out_shape=[sems, sems] + [pltpu.HBM(s.shape, s.dtype) for s in stacks] + [TOKEN],
        input_output_aliases={a: a + 2 for a in range(n)}, compiler_params=_in_flight_params(),
    )(*[_in_hbm(s) for s in stacks], after)
    return res[0], res[1], res[2:2 + n], res[2 + n]


def _gather_wait(send, recv, stacks, split, after, tag):
    n = len(stacks)

    def body(*refs):
        ins = refs[:n]
        send_ref, recv_ref = refs[n], refs[n + 1]
        _, _, c, j, chips = _place()
        for a in range(n):
            rows = stacks[a].shape[1]
            mine = _gather_piece(ins[a], rows, split[a], j, c)
            for t, (px, py) in enumerate(chips):
                theirs = _gather_piece(ins[a], rows, split[a], 2 * px + py, c)
                _remote(mine, mine, send_ref.at[3 * a + t], recv_ref.at[3 * a + t], (px, py, c)).wait_send()
                _remote(theirs, theirs, send_ref.at[3 * a + t], recv_ref.at[3 * a + t], (px, py, c)).wait_recv()

    return pl.pallas_call(
        body, name=f"gather_wait_{tag}", in_specs=[HBM] * n + [SEM, SEM, ANY], out_specs=[HBM] * n,
        out_shape=[pltpu.HBM(s.shape, s.dtype) for s in stacks],
        input_output_aliases={a: a for a in range(n)}, compiler_params=_in_flight_params(),
    )(*stacks, send, recv, after)


def _gather_forward(stacks, split, tag):
    idx = [a for a in range(len(stacks)) if split[a]]
    n = len(idx)

    def body(*refs):
        outs = refs[n:2 * n]
        send, recv = refs[2 * n:]
        x, y, c, _, chips = _place()
        sends = []
        for t, (px, py) in enumerate(chips):
            for a in range(n):
                blk = _half(outs[a].at[2 * px + py], c, stacks[idx[a]].shape[1])
                cp = _remote(blk, blk, send.at[a, t], recv.at[a, t], (x, y, 1 - c))
                cp.start()
                sends.append(cp)
        for t, (px, py) in enumerate(chips):
            for a in range(n):
                blk = _half(outs[a].at[2 * px + py], 1 - c, stacks[idx[a]].shape[1])
                _remote(blk, blk, send.at[a, t], recv.at[a, t], (x, y, 1 - c)).wait_recv()
        for cp in sends:
            cp.wait_send()

    sem = pltpu.SemaphoreType.DMA
    res = pl.pallas_call(
        body, name=f"gather_forward_{tag}", in_specs=[ANY] * n, out_specs=[ANY] * n,
        out_shape=[jax.ShapeDtypeStruct(stacks[a].shape, stacks[a].dtype) for a in idx],
        input_output_aliases={a: a for a in range(n)}, scratch_shapes=[sem((n, 3)), sem((n, 3))],
    )(*[stacks[a] for a in idx])
    out = list(stacks)
    for a, r in zip(idx, res):
        out[a] = r
    return out


def _forward_start(stacks, after, tag):
    n = len(stacks)

    def body(*refs):
        ins = refs[:n]
        send, recv = refs[n + 1], refs[n + 2]
        token = refs[2 * n + 3]
        x, y, c, _, chips = _place()
        for a in range(n):
            for t, (px, py) in enumerate(chips):
                blk = _half(ins[a].at[2 * px + py], c, stacks[a].shape[1])
                _remote(blk, blk, send.at[3 * a + t], recv.at[3 * a + t], (x, y, 1 - c)).start()
        token[...] = jnp.zeros_like(token)

    sems = pltpu.SemaphoreType.DMA((3 * n,))
    res = pl.pallas_call(
        body, name=f"forward_start_{tag}", in_specs=[HBM] * n + [ANY],
        out_specs=[SEM, SEM] + [HBM] * n + [pl.BlockSpec(memory_space=pltpu.VMEM)],
        out_shape=[sems, sems] + [pltpu.HBM(s.shape, s.dtype) for s in stacks] + [TOKEN],
        input_output_aliases={a: a + 2 for a in range(n)}, compiler_params=_in_flight_params(),
    )(*[_in_hbm(s) for s in stacks], after)
    return res[0], res[1], res[2:2 + n], res[2 + n]


def _forward_wait(send, recv, stacks, after, tag):
    n = len(stacks)

    def body(*refs):
        ins = refs[:n]
        send_ref, recv_ref = refs[n], refs[n + 1]
        x, y, c, _, chips = _place()
        for a in range(n):
            for t, (px, py) in enumerate(chips):
                mine = _half(ins[a].at[2 * px + py], c, stacks[a].shape[1])
                theirs = _half(ins[a].at[2 * px + py], 1 - c, stacks[a].shape[1])
                _remote(mine, mine, send_ref.at[3 * a + t], recv_ref.at[3 * a + t], (x, y, 1 - c)).wait_send()
                _remote(theirs, theirs, send_ref.at[3 * a + t], recv_ref.at[3 * a + t], (x, y, 1 - c)).wait_recv()

    return pl.pallas_call(
        body, name=f"forward_wait_{tag}", in_specs=[HBM] * n + [SEM, SEM, ANY], out_specs=[HBM] * n,
        out_shape=[pltpu.HBM(s.shape, s.dtype) for s in stacks],
        input_output_aliases={a: a for a in range(n)}, compiler_params=_in_flight_params(),
    )(*stacks, send, recv, after)


def _swap_start(grads, tag):
    n = len(grads)

    def body(*refs):
        ins, gots = refs[:n], refs[n:2 * n]
        send, recv = refs[2 * n], refs[2 * n + 1]
        token = refs[4 * n + 2]
        x, y, c, _, _ = _place()
        for a in range(n):
            h = grads[a].shape[1] // 2
            _remote(ins[a].at[:, pl.ds((1 - c) * h, h)], gots[a], send.at[a], recv.at[a], (x, y, 1 - c)).start()
        token[...] = jnp.zeros_like(token)

    sems = pltpu.SemaphoreType.DMA((n,))
    halves = [(g.shape[0], g.shape[1] // 2, g.shape[2]) for g in grads]
    res = pl.pallas_call(
        body, name=f"swap_start_{tag}", in_specs=[HBM] * (2 * n),
        out_specs=[SEM, SEM] + [HBM] * (2 * n) + [pl.BlockSpec(memory_space=pltpu.VMEM)],
        out_shape=[sems, sems] + [pltpu.HBM(g.shape, g.dtype) for g in grads] + [pltpu.HBM(s, F32) for s in halves] + [TOKEN],
        input_output_aliases={a: a + 2 for a in range(2 * n)}, compiler_params=_in_flight_params(),
    )(*[_in_hbm(g) for g in grads], *[_in_hbm(lax.empty(s, F32)) for s in halves])
    return res[0], res[1], res[2:2 + n], res[2 + n:2 + 2 * n], res[2 + 2 * n]


def _swap_wait(send, recv, grads, gots, after, tag):
    n = len(grads)

    def body(*refs):
        ins, lnd = refs[:n], refs[n:2 * n]
        send_ref, recv_ref = refs[2 * n], refs[2 * n + 1]
        x, y, c, _, _ = _place()
        for a in range(n):
            h = grads[a].shape[1] // 2
            cp = _remote(ins[a].at[:, pl.ds((1 - c) * h, h)], lnd[a], send_ref.at[a], recv_ref.at[a], (x, y, 1 - c))
            cp.wait_send()
            cp.wait_recv()

    bufs = [pltpu.HBM(g.shape, g.dtype) for g in grads] + [pltpu.HBM(g.shape, g.dtype) for g in gots]
    res = pl.pallas_call(
        body, name=f"swap_wait_{tag}", in_specs=[HBM] * (2 * n) + [SEM, SEM, ANY], out_specs=[HBM] * (2 * n),
        out_shape=bufs, input_output_aliases={a: a for a in range(2 * n)}, compiler_params=_in_flight_params(),
    )(*grads, *gots, send, recv, after)
    return res[:n], res[n:]


def _exchange_start(parts, tag):
    n = len(parts)

    def body(*refs):
        ins, lands = refs[:n], refs[n:2 * n]
        send, recv = refs[2 * n], refs[2 * n + 1]
        token = refs[4 * n + 2]
        _, _, c, j, chips = _place()
        for t, (px, py) in enumerate(chips):
            for a in range(n):
                _remote(ins[a].at[2 * px + py], lands[a].at[j], send.at[3 * a + t], recv.at[3 * a + t], (px, py, c)).start()
        token[...] = jnp.zeros_like(token)

    sems = pltpu.SemaphoreType.DMA((3 * n,))
    bufs = [pltpu.HBM(p.shape, p.dtype) for p in parts]
    res = pl.pallas_call(
        body, name=f"exchange_start_{tag}", in_specs=[HBM] * (2 * n),
        out_specs=[SEM, SEM] + [HBM] * (2 * n) + [pl.BlockSpec(memory_space=pltpu.VMEM)],
        out_shape=[sems, sems] + bufs + bufs + [TOKEN],
        input_output_aliases={a: a + 2 for a in range(2 * n)}, compiler_params=_in_flight_params(),
    )(*[_in_hbm(p) for p in parts], *[_in_hbm(lax.empty(p.shape, p.dtype)) for p in parts])
    return res[0], res[1], res[2:2 + n], res[2 + n:2 + 2 * n], res[2 + 2 * n]


def _exchange_wait(send, recv, parts, lands, after, tag):
    n = len(parts)

    def body(*refs):
        ins, lnd = refs[:n], refs[n:2 * n]
        send_ref, recv_ref = refs[2 * n], refs[2 * n + 1]
        _, _, c, j, chips = _place()
        for t, (px, py) in enumerate(chips):
            jt = 2 * px + py
            for a in range(n):
                _remote(ins[a].at[jt], lnd[a].at[j], send_ref.at[3 * a + t], recv_ref.at[3 * a + t], (px, py, c)).wait_send()
                _remote(ins[a].at[jt], lnd[a].at[jt], send_ref.at[3 * a + t], recv_ref.at[3 * a + t], (px, py, c)).wait_recv()

    bufs = [pltpu.HBM(p.shape, p.dtype) for p in parts]
    res = pl.pallas_call(
        body, name=f"exchange_wait_{tag}", in_specs=[HBM] * (2 * n) + [SEM, SEM, ANY], out_specs=[HBM] * (2 * n),
        out_shape=bufs + bufs, input_output_aliases={a: a for a in range(2 * n)}, compiler_params=_in_flight_params(),
    )(*parts, *lands, send, recv, after)
    return res[:n], res[n:]


def _small_chip_sums(arrs):
    n = len(arrs)

    def body(*refs):
        ins, outs = refs[:n], refs[n:2 * n]
        sib = refs[2 * n:3 * n]
        send, recv = refs[3 * n:]
        x, y, c, j, _ = _place()
        swaps = [_remote(ins[a], sib[a], send.at[a], recv.at[a], (x, y, 1 - c)) for a in range(n)]
        for cp in swaps:
            cp.start()
        for a in range(n):
            swaps[a].wait_recv()
            outs[a][j] = ins[a][...] + sib[a][...]
        for cp in swaps:
            cp.wait_send()

    sem = pltpu.SemaphoreType.DMA
    vm = pl.BlockSpec(memory_space=pltpu.VMEM)
    return pl.pallas_call(
        body, name="small_chip_sums", in_specs=[vm] * n, out_specs=[vm] * n,
        out_shape=[jax.ShapeDtypeStruct((N_SHARD, *a.shape), F32) for a in arrs],
        scratch_shapes=[pltpu.VMEM(a.shape, F32) for a in arrs] + [sem((n,)), sem((n,))],
        compiler_params=_cp(),
    )(*arrs)


def _small_totals(stacks):
    n = len(stacks)

    def body(*refs):
        for a in range(n):
            refs[n + a][...] = ((refs[a][0] + refs[a][1]) + refs[a][2]) + refs[a][3]

    return pl.pallas_call(body, name="small_totals", out_shape=[jax.ShapeDtypeStruct(s.shape[1:], F32) for s in stacks],
                          compiler_params=_cp())(*stacks)


SMALL_1024 = ("ln1_g", "ln1_b", "ln2_g", "ln2_b", "b_ple_gate", "ln3_g", "ln3_b")


def _adamw_small(red3, red1, redz, g_conv_w, redc, red_ws, red_bs, params):
    shape2d = {"ln_z_g": (1, D_GMLP), "ln_z_b": (1, D_GMLP), "w_s": (N_HEADS * BLK, BLK), "b_s": (N_HEADS, BLK),
               "conv_w": (3, FF_BLK), "conv_b": (N_SHARD, FF_BLK), **{k: (1, D_MODEL) for k in SMALL_1024}}
    names = list(shape2d)
    flat = [a.reshape(shape2d[k]) for k in names for a in params[k]]

    def body(r3, r1, rz, gcw, rc, rws, rbs, *refs):
        ins, outs = refs[:3 * len(names)], refs[3 * len(names):]

        def grad_of(k):
            if k == "w_s":
                return rws[...]
            if k == "b_s":
                return rbs[...]
            if k == "conv_w":
                return gcw[0:3, :]
            if k == "conv_b":
                return jnp.concatenate([rc[j * STAT_ROWS + 3:j * STAT_ROWS + 4, :] for j in range(N_SHARD)], axis=0)
            src, row = {"ln3_g": (r3, 0), "ln3_b": (r3, 1), "b_ple_gate": (r3, 2), "ln2_g": (r3, 3), "ln2_b": (r3, 4),
                        "ln1_g": (r1, 0), "ln1_b": (r1, 1), "ln_z_g": (rz, 0), "ln_z_b": (rz, 1)}[k]
            return src[row:row + 1, :]

        for i, k in enumerate(names):
            w_ref, m_ref, v_ref = ins[3 * i:3 * i + 3]
            g_ref, d_ref, nm_ref, nv_ref = outs[4 * i:4 * i + 4]
            g = grad_of(k)
            g_ref[...] = g
            d_ref[...], nm_ref[...], nv_ref[...] = _adamw_math(w_ref[...], g, m_ref[...], v_ref[...])

    res = pl.pallas_call(
        body, name="adamw_small",
        out_shape=[jax.ShapeDtypeStruct(shape2d[k], F32) for k in names for _ in range(4)],
        compiler_params=_cp(),
    )(red3, red1, redz, g_conv_w, redc, red_ws, red_bs, *flat)
    return {k: tuple(r.reshape(params[k][0].shape) for r in res[4 * i:4 * i + 4]) for i, k in enumerate(names)}


WEIGHTS = ("w_in", "ln_z_g", "ln_z_b", "w_s", "b_s", "w_o", "ln1_g", "ln1_b", "w_ff_a", "w_ff_b", "conv_w", "conv_b",
           "w_ff_down", "ln2_g", "ln2_b", "w_ple_gate", "b_ple_gate", "w_ple_in", "ln3_g", "ln3_b")
BIG = ("w_in", "w_o", "w_ff_a", "w_ff_b", "w_ff_down", "w_ple_gate", "w_ple_in")
TRANSPOSED = ("w_ff_a", "w_ff_b")
LATE = ("w_o", "w_ff_a", "w_ff_b", "w_ff_down", "w_ple_gate", "w_ple_in", "conv_w")


def kernel(x, p, positions, w_in, ln_z_g, ln_z_b, w_s, b_s, w_o, ln1_g, ln1_b, w_ff_a, w_ff_b, conv_w, conv_b, w_ff_down, ln2_g, ln2_b, w_ple_gate, b_ple_gate, w_ple_in, ln3_g, ln3_b, loss_target, m_w_in, m_ln_z_g, m_ln_z_b, m_w_s, m_b_s, m_w_o, m_ln1_g, m_ln1_b, m_w_ff_a, m_w_ff_b, m_conv_w, m_conv_b, m_w_ff_down, m_ln2_g, m_ln2_b, m_w_ple_gate, m_b_ple_gate, m_w_ple_in, m_ln3_g, m_ln3_b, v_w_in, v_ln_z_g, v_ln_z_b, v_w_s, v_b_s, v_w_o, v_ln1_g, v_ln1_b, v_w_ff_a, v_w_ff_b, v_conv_w, v_conv_b, v_w_ff_down, v_ln2_g, v_ln2_b, v_w_ple_gate, v_b_ple_gate, v_w_ple_in, v_ln3_g, v_ln3_b):
    args = locals()
    w = {k: args[k] for k in WEIGHTS}
    m = {k: args["m_" + k] for k in WEIGHTS}
    v = {k: args["v_" + k] for k in WEIGHTS}

    for k in TRANSPOSED:
        w[k], m[k], v[k] = (jnp.swapaxes(a, 1, 2) for a in (w[k], m[k], v[k]))

    chip = 2 * lax.axis_index("x") + lax.axis_index("y")
    place = jnp.stack([chip, lax.axis_index("c")]).astype(jnp.int32)
    stack = dict(zip(["w_in"], _place_shards("cast_w_in", [w["w_in"][0]], [MXU], place, place)))
    i_send, i_recv, in_flight, dep = _gather_start([stack["w_in"]], [True], place, "w_in")
    stack.update(zip(LATE, _place_shards("cast_late", [w[k][0] for k in LATE],
                                         [F32 if k == "conv_w" else MXU for k in LATE], place, dep)))
    split_late = [k != "conv_w" for k in LATE]
    g_send, g_recv, late_flight, start_dep = _gather_start([stack[k] for k in LATE], split_late, place, "late")
    rope = _rope_tables(positions, x.shape[1], start_dep)
    landed_in = _gather_wait(i_send, i_recv, in_flight, [True], rope[0], "w_in")
    w_in_full, = _gather_forward(landed_in, [True], "w_in")
    halves =[k for k, sp in zip(LATE, split_late) if sp]
    trips = {}

    def late_landed(after):
        fw = dict(zip(LATE, _gather_wait(g_send, g_recv, late_flight, split_late, after, "late")))
        trips["late"] = (fw, *_forward_start([fw[k] for k in halves], fw["conv_w"], "late"))
        return trips["late"][-1]

    def late_weights(after):
        fw, send, recv, flight, _ = trips["late"]
        fw.update(zip(halves, _forward_wait(send, recv, flight, after, "late")))
        return (fw["w_o"].reshape(D_MODEL, D_MODEL), fw["w_ff_a"], fw["w_ff_b"], fw["conv_w"], fw["w_ff_down"],
                fw["w_ple_gate"].reshape(D_MODEL, D_MODEL), fw["w_ple_in"])

    def swap_started(names, grads, tag):
        stacked = [g.reshape(N_SHARD, *w[k].shape[1:]) for k, g in zip(names, grads)]
        return (names, tag, *_swap_start(stacked, tag))

    def partial_sums(swap, after):
        names, tag, send, recv, stacked, gots, _ = swap
        stacked, got = _swap_wait(send, recv, stacked, gots, after, tag)
        pair = _pair_sums(f"rs_pair_{tag}", stacked, got, place)
        return (names, tag, *_exchange_start(pair, tag))

    def chip_summed(trip, after, dep):
        names, tag, send, recv, pair, lands, _ = trip
        pair, landed = _exchange_wait(send, recv, pair, lands, after, tag)
        return _chip_sums(f"rs_sum_{tag}", pair, landed, place, dep), names, tag

    def reduced(trip, after, dep):
        blocks, names, tag = chip_summed(trip, after, dep)
        return dict(zip(names, _sibling_join(blocks, tag)))

    def early_grads_landed(after):
        blocks, names, tag = chip_summed(trips["early"], after, trips["small"][-1])
        trips["join"] = (names, *_join_start(blocks, after, tag))
        return trips["join"][-1]

    def early_grads(grads):
        trips["swap"] = swap_started(list(grads), list(grads.values()), "early")
        return trips["swap"][-1]

    def early_grads_sent(after, small):
        trips["early"] = partial_sums(trips["swap"], after)
        stat3, stat1, zstat, cstat, dws, dbs = small
        sums = _small_chip_sums([stat3, stat1, zstat, cstat.reshape(N_SHARD * STAT_ROWS, FF_BLK),
                                 dws.reshape(N_HEADS * BLK, BLK), dbs])
        trips["small"] = _gather_start(sums, [False] * len(sums), trips["early"][-1], "small")
        return trips["small"][-1]

    grad_x, g_w_in = _local_step(
        x[0], p[0, 0], rope, loss_target[0], w_in_full, start_dep, late_landed, late_weights, early_grads, early_grads_sent,
        early_grads_landed, ln_z_g, ln_z_b, w_s, b_s, ln1_g, ln1_b, conv_b, ln2_g, ln2_b, b_ple_gate, ln3_g, ln3_b)

    trips["w_in"] = partial_sums(swap_started(["w_in"], [g_w_in], "w_in"), g_w_in)
    out = {}

    def adamw(red, tag):
        names = list(red)
        steps = _adamw_shards(f"adamw_{tag}", [w[k] for k in names], [red[k] for k in names], [m[k] for k in names],
                              [v[k] for k in names])
        for k, (d, nm, nv) in zip(names, steps):
            out[k] = (red[k].reshape(w[k].shape), d, nm, nv)

    names, j_send, j_recv, j_flight, _ = trips["join"]
    adamw(dict(zip(names, _join_wait(j_send, j_recv, j_flight, trips["w_in"][-1], "early"))), "early")
    adamw(reduced(trips["w_in"], out["w_o"][3], start_dep), "w_in")
    for k in TRANSPOSED:
        out[k] = tuple(jnp.swapaxes(a, 1, 2) for a in out[k])

    s_send, s_recv, s_flight, _ = trips["small"]
    red3, red1, redz, redc, red_ws, red_bs = _small_totals(
        _gather_wait(s_send, s_recv, s_flight, [False] * len(s_flight), out["w_in"][3], "small"))
    loss = (0.5 / D_MODEL) * jnp.sum(red3[5])
    g_conv_w = lax.dynamic_slice_in_dim(redc, chip * STAT_ROWS, STAT_ROWS, 0)
    names_small = [k for k in WEIGHTS if k not in BIG]
    out.update(_adamw_small(red3, red1, redz, g_conv_w, redc, red_ws, red_bs, {k: (w[k], m[k], v[k]) for k in names_small}))

    return (loss, grad_x[None], *[out[k][0] for k in WEIGHTS], *[out[k][1] for k in WEIGHTS],
            *[out[k][2] for k in WEIGHTS], *[out[k][3] for k in WEIGHTS])
```

```python
import math

import numpy as np
import jax
import jax.numpy as jnp
from jax import lax
from jax.experimental import pallas as pl
from jax.experimental.pallas import tpu as pltpu

F32 = jnp.float32
BF16 = jnp.bfloat16
MXU = BF16

D_MODEL = 1024
HEAD_DIM = 64
N_HEADS = 8
D_ATTN = 512
D_GMLP = 512
D_IN = 2560
DILATIONS = (1, 4, 16)
BLK = 128
ROPE_THETA = 500000.0
ROPE_DIM = 16
D_FF = 2816
D_PLE = 256
LN_EPS = 1e-5
ALPHA = 2.0 ** 0.25
NEG_INF = -1e30
N_SHARD = 4
W_IN_BLK = D_IN // N_SHARD
FF_BLK = D_FF // N_SHARD
ROW_BLK = D_MODEL // N_SHARD
ADAM_LR, ADAM_B1, ADAM_B2, ADAM_EPS, ADAM_WD, ADAM_STEP = 0.001, 0.9, 0.999, 1e-08, 0.01, 10

TM = 512
HALO = 8
ROW_GROUPS = 2
VMEM_LIMIT = 56 * 1024 * 1024


def _cp(**kw):
    return pltpu.CompilerParams(vmem_limit_bytes=VMEM_LIMIT, **kw)


def _full(shape):
    n = len(shape)
    return pl.BlockSpec(shape, lambda *_: (0,) * n)


def _gelu(x):
    return 0.5 * x * (1.0 + lax.erf(x * (1.0 / math.sqrt(2.0))))


def _gelu_grad(x):
    return 0.5 * (1.0 + lax.erf(x * (1.0 / math.sqrt(2.0)))) + x * jnp.exp(-0.5 * x * x) * (1.0 / math.sqrt(2.0 * math.pi))


def _ln_fwd(r):
    mu = jnp.mean(r, axis=-1, keepdims=True)
    xc = r - mu
    var = jnp.mean(xc * xc, axis=-1, keepdims=True)
    rstd = lax.rsqrt(var + LN_EPS)
    return xc * rstd, rstd


def _ln_bwd(dy, xhat, rstd, g):
    dxh = dy * g
    m1 = jnp.mean(dxh, axis=-1, keepdims=True)
    m2 = jnp.mean(dxh * xhat, axis=-1, keepdims=True)
    return rstd * (dxh - m1 - xhat * m2)


def _dot(a, b):
    return jnp.dot(a.astype(MXU), b.astype(MXU), preferred_element_type=F32)


def _dot_nt(a, b):
    return lax.dot_general(a.astype(MXU), b.astype(MXU), (((1,), (1,)), ((), ())), preferred_element_type=F32)


def _dot_tn(a, b):
    return lax.dot_general(a.astype(MXU), b.astype(MXU), (((0,), (0,)), ((), ())), preferred_element_type=F32)


def _colsum(v):
    return jnp.sum(v, axis=0, keepdims=True)


def _rope_tables(positions, t, dep):
    inv = np.float32(ROPE_THETA) ** (-np.arange(0, ROPE_DIM, 2, dtype=np.float32) / np.float32(ROPE_DIM))
    half = ROPE_DIM // 2
    pos_rep = jnp.repeat(positions.reshape(t // 16, 16), half, axis=1)
    inv_row = jnp.asarray(np.tile(inv, 16)[None, :], F32)

    def trig_body(pos_ref, inv_ref, dep_ref, cos_ref, sin_ref):
        ang = pos_ref[...].astype(F32) * inv_ref[...]
        cos_ref[...] = jnp.cos(ang)
        sin_ref[...] = jnp.sin(ang)

    vm = pl.BlockSpec(memory_space=pltpu.VMEM)
    cos8, sin8 = pl.pallas_call(
        trig_body, name="rope_trig", in_specs=[vm, vm, pl.BlockSpec(memory_space=pl.ANY)], out_specs=[vm, vm],
        out_shape=(jax.ShapeDtypeStruct((t // 16, 128), F32), jax.ShapeDtypeStruct((t // 16, 128), F32)),
    )(pos_rep, inv_row, dep)
    cos8 = cos8.reshape(t, half)
    sin8 = sin8.reshape(t, half)

    lane = np.arange(128) % HEAD_DIM
    sel = (np.arange(half)[:, None] == (lane % half)[None, :])
    e_cos = (sel & (lane < ROPE_DIM)[None, :]).astype(np.float32)
    e_s1 = -(sel & (lane < half)[None, :]).astype(np.float32)
    e_s2 = (sel & ((lane >= half) & (lane < ROPE_DIM))[None, :]).astype(np.float32)
    ones = (lane >= ROPE_DIM).astype(np.float32)[None, :]

    def expand_body(cos_ref, sin_ref, ec_ref, e1_ref, e2_ref, ones_ref, c_ref, s1_ref, s2_ref):
        hp = lax.Precision.HIGHEST
        c_ref[...] = jnp.dot(cos_ref[...], ec_ref[...], precision=hp, preferred_element_type=F32) + ones_ref[...]
        s1_ref[...] = jnp.dot(sin_ref[...], e1_ref[...], precision=hp, preferred_element_type=F32)
        s2_ref[...] = jnp.dot(sin_ref[...], e2_ref[...], precision=hp, preferred_element_type=F32)

    tab = jax.ShapeDtypeStruct((t, 128), F32)
    return pl.pallas_call(expand_body, name="rope_expand", out_shape=(tab, tab, tab), compiler_params=_cp())(
        cos8, sin8, jnp.asarray(e_cos), jnp.asarray(e_s1), jnp.asarray(e_s2), jnp.asarray(ones))


def _tile_heads(tab):
    return jnp.concatenate([tab] * (D_ATTN // 128), axis=1)


def _rope_apply(v, c, s1, s2):
    n = v.shape[1]
    half = ROPE_DIM // 2
    return v * c + pltpu.roll(v, n - half, 1) * s1 + pltpu.roll(v, half, 1) * s2


def _rope_apply_t(g, c, s1, s2):
    n = g.shape[1]
    half = ROPE_DIM // 2
    return g * c + pltpu.roll(g * s1, half, 1) + pltpu.roll(g * s2, n - half, 1)


LANE_CHUNKS = D_ATTN // 128
HEAD_LANES = 128 // N_HEADS


def _perm_shape(t, d, w, dtype):
    return jax.ShapeDtypeStruct((d, t // d, w), dtype)


def _perm_tile(d, w):
    return pl.BlockSpec((None if d == 1 else d, TM // d, w), lambda i: (0, i, 0))


def _to_planes(ref, scr, d, n_chunks, dtype):
    for r in range(d):
        for cc in range(n_chunks):
            ref[r, :, cc * 128:(cc + 1) * 128] = scr.at[cc][pl.ds(r, TM // d, stride=d), :].astype(dtype)


def _from_planes(ref, scr, d, n_chunks, accumulate=False):
    for r in range(d):
        for cc in range(n_chunks):
            rows = scr.at[cc]
            val = ref[r, :, cc * 128:(cc + 1) * 128].astype(F32)
            if accumulate:
                rows[pl.ds(r, TM // d, stride=d), :] += val
            else:
                rows[pl.ds(r, TM // d, stride=d), :] = val


def _chunks(val):
    return [val[:, cc * 128:(cc + 1) * 128] for cc in range(val.shape[1] // 128)]


def _unchunk(scr, n_chunks, base=0):
    return jnp.concatenate([scr[base + cc] for cc in range(n_chunks)], axis=1)


def _head_expand():
    src = np.arange(128)[:, None]
    dst = np.arange(D_ATTN)[None, :]
    return jnp.asarray((src == (dst // HEAD_DIM) * HEAD_LANES).astype(np.float32))


def _head_reduce():
    src = np.arange(D_ATTN)[:, None]
    dst = np.arange(128)[None, :]
    return jnp.asarray((src // HEAD_DIM == dst // HEAD_LANES).astype(np.float32))


def _dot_select(a, sel):
    hi = a.astype(BF16)
    lo = (a - hi.astype(F32)).astype(BF16)
    sel = sel.astype(BF16)
    return jnp.dot(hi, sel, preferred_element_type=F32) + jnp.dot(lo, sel, preferred_element_type=F32)


def _qkvuz(x, w_in, c_tab, s1_tab, s2_tab, ln_z_g, ln_z_b, w_s, b_full, dep):
    t = x.shape[0]
    nchunk = TM // BLK

    def body(x_ref, w_ref, c_ref, s1_ref, s2_ref, g_ref, b_ref, ws_ref, bf_ref, dep_ref,
             qkv1_ref, qkv4_ref, qkv16_ref, hu_ref, hz_ref, mixed_ref, gm_ref, xb_ref, h_scr, wm_scr, p_scr):
        @pl.when(pl.program_id(0) == 0)
        def _():
            row = lax.broadcasted_iota(jnp.int32, (BLK, BLK), 0)
            col = lax.broadcasted_iota(jnp.int32, (BLK, BLK), 1)
            for g in range(N_HEADS):
                wm_scr[g] = jnp.where(col <= row, ws_ref[g], 0.0).astype(MXU)

        xb = x_ref[...].astype(MXU)
        xb_ref[...] = xb
        for j in range(N_SHARD):
            h_scr[:, j * W_IN_BLK:(j + 1) * W_IN_BLK] = jnp.dot(xb, w_ref[j], preferred_element_type=F32)
        c, s1, s2 = _tile_heads(c_ref[...]), _tile_heads(s1_ref[...]), _tile_heads(s2_ref[...])
        q = _rope_apply(h_scr[:, 0:D_ATTN], c, s1, s2) * (1.0 / math.sqrt(HEAD_DIM))
        k = _rope_apply(h_scr[:, D_ATTN:2 * D_ATTN], c, s1, s2)
        for part, val in enumerate((q, k, h_scr[:, 2 * D_ATTN:3 * D_ATTN])):
            qkv1_ref[:, part * D_ATTN:(part + 1) * D_ATTN] = val.astype(MXU)
            for cc in range(LANE_CHUNKS):
                p_scr[part * LANE_CHUNKS + cc] = val[:, cc * 128:(cc + 1) * 128]
        _to_planes(qkv4_ref, p_scr, DILATIONS[1], 3 * LANE_CHUNKS, MXU)
        _to_planes(qkv16_ref, p_scr, DILATIONS[2], 3 * LANE_CHUNKS, MXU)
        hu = h_scr[:, 3 * D_ATTN:3 * D_ATTN + D_GMLP]
        hz = h_scr[:, 3 * D_ATTN + D_GMLP:]
        hu_ref[...] = hu
        hz_ref[...] = hz
        zhat, _ = _ln_fwd(_gelu(hz))
        zn = (zhat * g_ref[...] + b_ref[...]).astype(MXU)
        for ch in range(nchunk):
            rows = slice(ch * BLK, (ch + 1) * BLK)
            for g in range(N_HEADS):
                cols = slice(g * HEAD_DIM, (g + 1) * HEAD_DIM)
                mixed_ref[rows, cols] = jnp.dot(wm_scr[g], zn[rows, cols], preferred_element_type=F32) + bf_ref[:, cols]
        gm_ref[...] = (_gelu(hu) * mixed_ref[...]).astype(MXU)

    tok = lambda w: pl.BlockSpec((TM, w), lambda i: (i, 0))
    outs = [_perm_shape(t, d, 3 * D_ATTN, MXU) for d in DILATIONS] + [jax.ShapeDtypeStruct((t, D_GMLP), F32)] * 3 + [
        jax.ShapeDtypeStruct((t, D_GMLP), MXU), jax.ShapeDtypeStruct((t, D_MODEL), MXU)]
    return pl.pallas_call(
        body, name="qkvuz", grid=(t // TM,),
        in_specs=[tok(D_MODEL), _full(w_in.shape), tok(128), tok(128), tok(128), _full(ln_z_g.shape), _full(ln_z_b.shape),
                  _full(w_s.shape), _full(b_full.shape), pl.BlockSpec(memory_space=pl.ANY)],
        out_specs=[_perm_tile(d, 3 * D_ATTN) for d in DILATIONS] + [tok(D_ATTN)] * 4 + [tok(D_MODEL)], out_shape=outs,
        scratch_shapes=[pltpu.VMEM((TM, D_IN), F32), pltpu.VMEM((N_HEADS, BLK, BLK), MXU),
                        pltpu.VMEM((3 * LANE_CHUNKS, TM, 128), F32)],
        compiler_params=_cp(dimension_semantics=("arbitrary",)),
    )(x, w_in, c_tab, s1_tab, s2_tab, ln_z_g, ln_z_b, w_s, b_full, dep)


def _band_valid(n):
    i = lax.broadcasted_iota(jnp.int32, (BLK, 2 * BLK), 0)
    j = lax.broadcasted_iota(jnp.int32, (BLK, 2 * BLK), 1)
    return (j >= i) & (j <= i + BLK) & ((j >= BLK) | (n > 0))


def _attn_fwd(qkv, d, dep):
    _, l_sub, _ = qkv.shape
    nb = l_sub // BLK

    def body(q_ref, kp_ref, kc_ref, vp_ref, vc_ref, dep_ref, o_ref, l_ref):
        valid = _band_valid(pl.program_id(1))
        kcat = jnp.concatenate([kp_ref[...], kc_ref[...]], axis=0)
        vcat = jnp.concatenate([vp_ref[...], vc_ref[...]], axis=0)
        for h in range(N_HEADS):
            cols = slice(h * HEAD_DIM, (h + 1) * HEAD_DIM)
            s = jnp.where(valid, _dot_nt(q_ref[:, cols], kcat[:, cols]), NEG_INF)
            m = jnp.max(s, axis=-1, keepdims=True)
            e = jnp.exp(s - m)
            den = jnp.sum(e, axis=-1, keepdims=True)
            o_ref[:, cols] = _dot(e, vcat[:, cols]) * (1.0 / den)
            l_ref[:, h * HEAD_LANES:(h + 1) * HEAD_LANES] = jnp.broadcast_to(m + jnp.log(den), (BLK, HEAD_LANES))

    def blk(w, col, prev=False):
        return pl.BlockSpec((None, BLK, w), lambda r, n: (r, jnp.maximum(n - 1, 0) if prev else n, col))

    return pl.pallas_call(
        body, name=f"attn_fwd_d{d}", grid=(d, nb),
        in_specs=[blk(D_ATTN, 0), blk(D_ATTN, 1, True), blk(D_ATTN, 1), blk(D_ATTN, 2, True), blk(D_ATTN, 2),
                  pl.BlockSpec(memory_space=pl.ANY)],
        out_specs=[blk(D_ATTN, 0), blk(128, 0)],
        out_shape=[jax.ShapeDtypeStruct((d, l_sub, D_ATTN), F32), jax.ShapeDtypeStruct((d, l_sub, 128), F32)],
        compiler_params=_cp(dimension_semantics=("arbitrary", "arbitrary")),
    )(qkv, qkv, qkv, qkv, qkv, dep)


def _attn_bwd(qkv, do, lse, delta, d, dep):
    _, l_sub, _ = qkv.shape
    nb = l_sub // BLK
    whole = l_sub <= 8 * BLK

    def shares(n, q_ref, kp_ref, kc_ref, vp_ref, vc_ref, do_ref, l_ref, dl_ref, dq_ref):
        valid = _band_valid(n)
        kcat = jnp.concatenate([kp_ref[...], kc_ref[...]], axis=0)
        vcat = jnp.concatenate([vp_ref[...], vc_ref[...]], axis=0)
        for h in range(N_HEADS):
            cols = slice(h * HEAD_DIM, (h + 1) * HEAD_DIM)
            stat = slice(h * HEAD_LANES, h * HEAD_LANES + 1)
            qh, doh = q_ref[:, cols], do_ref[:, cols]
            p = jnp.where(valid, jnp.exp(_dot_nt(qh, kcat[:, cols]) - l_ref[:, stat]), 0.0)
            ds = p * (_dot_nt(doh, vcat[:, cols]) - dl_ref[:, stat])
            dq_ref[:, cols] = _dot(ds, kcat[:, cols])
            yield cols, _dot_tn(ds, qh), _dot_tn(p, doh)

    def body_whole(*refs):
        dk_ref, dv_ref = refs[10:]
        n = pl.program_id(1)
        cur = pl.ds(pl.multiple_of(n * BLK, BLK), BLK)
        prev = pl.ds(pl.multiple_of(jnp.maximum(n - 1, 0) * BLK, BLK), BLK)
        for cols, dk2, dv2 in shares(n, *refs[:8], refs[9]):
            dk_ref[cur, cols] = dk2[BLK:]
            dv_ref[cur, cols] = dv2[BLK:]
            dk_ref[prev, cols] += dk2[0:BLK]
            dv_ref[prev, cols] += dv2[0:BLK]

    def body_carry(*refs):
        dk_ref, dv_ref, ck_scr, cv_scr = refs[10:]
        n = pl.program_id(1)

        @pl.when(n == 0)
        def _():
            ck_scr[...] = jnp.zeros_like(ck_scr)
            cv_scr[...] = jnp.zeros_like(cv_scr)

        @pl.when(n < nb)
        def _():
            for cols, dk2, dv2 in shares(n, *refs[:8], refs[9]):
                dk_ref[:, cols] = ck_scr[:, cols] + dk2[0:BLK]
                dv_ref[:, cols] = cv_scr[:, cols] + dv2[0:BLK]
                ck_scr[:, cols] = dk2[BLK:]
                cv_scr[:, cols] = dv2[BLK:]

        @pl.when(n == nb)
        def _():
            dk_ref[...] = ck_scr[...]
            dv_ref[...] = cv_scr[...]

    def blk(w, col, shift=0):
        return pl.BlockSpec((None, BLK, w), lambda r, n: (r, jnp.clip(n - shift, 0, nb - 1), col))

    if whole:
        dkv_spec = pl.BlockSpec((None, l_sub, D_ATTN), lambda r, n: (r, 0, 0))
        body, steps, scratch = body_whole, nb, []
    else:
        dkv_spec = blk(D_ATTN, 0, 1)
        body, steps, scratch = body_carry, nb + 1, [pltpu.VMEM((BLK, D_ATTN), F32)] * 2
    return pl.pallas_call(
        body, name=f"attn_bwd_d{d}", grid=(d, steps),
        in_specs=[blk(D_ATTN, 0), blk(D_ATTN, 1, 1), blk(D_ATTN, 1), blk(D_ATTN, 2, 1), blk(D_ATTN, 2),
                  blk(D_ATTN, 0), blk(128, 0), blk(128, 0), pl.BlockSpec(memory_space=pl.ANY)],
        out_specs=[blk(D_ATTN, 0), dkv_spec, dkv_spec],
        out_shape=[jax.ShapeDtypeStruct((d, l_sub, D_ATTN), F32)] * 3,
        scratch_shapes=scratch,
        compiler_params=_cp(dimension_semantics=("arbitrary", "arbitrary")),
    )(qkv, qkv, qkv, qkv, qkv, do, lse, delta, dep)


def _mix_ln1(os_, ls_, gm, x, w_o, ln1_g, ln1_b, dep):
    t = x.shape[0]
    expand = _head_expand()

    def body(o1, o4, o16, l1, l4, l16, gm_ref, x_ref, wo_ref, g_ref, b_ref, ex_ref, dep_ref,
             attn_ref, lse1_ref, lse4_ref, lse16_ref, cat_ref, xhat_ref, rstd_ref, x1b_ref, o_scr, l_scr):
        _from_planes(o4, o_scr, DILATIONS[1], LANE_CHUNKS)
        _from_planes(o16, o_scr.at[pl.ds(LANE_CHUNKS, LANE_CHUNKS)], DILATIONS[2], LANE_CHUNKS)
        _from_planes(l4, l_scr, DILATIONS[1], 1)
        _from_planes(l16, l_scr.at[pl.ds(1, 1)], DILATIONS[2], 1)
        la, lb, lc = l1[...], l_scr[0], l_scr[1]
        m = jnp.maximum(jnp.maximum(la, lb), lc)
        ea, eb, ec = jnp.exp(la - m), jnp.exp(lb - m), jnp.exp(lc - m)
        den = ea + eb + ec
        inv = 1.0 / den
        wide = lambda w: _dot_select(w, ex_ref[...])
        attn = (wide(ea * inv) * o1[...] + wide(eb * inv) * _unchunk(o_scr, LANE_CHUNKS)
                + wide(ec * inv) * _unchunk(o_scr, LANE_CHUNKS, LANE_CHUNKS))
        attn_ref[...] = attn
        lse = m + jnp.log(den)
        lse1_ref[...] = lse
        l_scr[2] = lse
        _to_planes(lse4_ref, l_scr.at[pl.ds(2, 1)], DILATIONS[1], 1, F32)
        _to_planes(lse16_ref, l_scr.at[pl.ds(2, 1)], DILATIONS[2], 1, F32)
        cat_ref[:, 0:D_ATTN] = attn.astype(MXU)
        cat_ref[:, D_ATTN:] = gm_ref[...]
        mix = jnp.dot(cat_ref[...], wo_ref[...], preferred_element_type=F32)
        xhat, rstd = _ln_fwd(ALPHA * x_ref[...] + mix)
        xhat_ref[...] = xhat
        rstd_ref[...] = rstd
        x1b_ref[...] = (xhat * g_ref[...] + b_ref[...]).astype(MXU)

    tok = lambda w: pl.BlockSpec((TM, w), lambda i: (i, 0))
    outs = [jax.ShapeDtypeStruct((t, D_ATTN), F32)] + [_perm_shape(t, d, 128, F32) for d in DILATIONS] + [
        jax.ShapeDtypeStruct((t, D_MODEL), MXU), jax.ShapeDtypeStruct((t, D_MODEL), F32), jax.ShapeDtypeStruct((t, 1), F32),
        jax.ShapeDtypeStruct((t, D_MODEL), MXU)]
    return pl.pallas_call(
        body, name="mix_ln1", grid=(t // TM,),
        in_specs=[_perm_tile(d, D_ATTN) for d in DILATIONS] + [_perm_tile(d, 128) for d in DILATIONS]
        + [tok(D_GMLP), tok(D_MODEL), _full(w_o.shape), _full(ln1_g.shape), _full(ln1_b.shape), _full(expand.shape),
           pl.BlockSpec(memory_space=pl.ANY)],
        out_specs=[tok(D_ATTN)] + [_perm_tile(d, 128) for d in DILATIONS] + [tok(D_MODEL), tok(D_MODEL), tok(1), tok(D_MODEL)],
        out_shape=outs,
        scratch_shapes=[pltpu.VMEM((2 * LANE_CHUNKS, TM, 128), F32), pltpu.VMEM((3, TM, 128), F32)],
        compiler_params=_cp(dimension_semantics=("arbitrary",)),
    )(*os_, *ls_, gm, x, w_o, ln1_g, ln1_b, expand, dep)


def _conv_fwd(a_ext, w_ref, b_ref, rows):
    back = [pltpu.roll(a_ext, s, 0)[HALO:HALO + rows] for s in (1, 2)]
    return b_ref[...] + w_ref[2:3, :] * a_ext[HALO:HALO + rows] + w_ref[1:2, :] * back[0] + w_ref[0:1, :] * back[1]


def _ffn_in(x1b, w_a, w_b, conv_w, conv_b):
    t = x1b.shape[0]
    hb = TM // HALO

    def body(x_ref, xh_ref, wa_ref, wb_ref, cw_ref, cb_ref, apre_ref, act_ref, gate_ref, f_ref):
        i = pl.program_id(1)
        a_pre = _dot_nt(x_ref[...], wa_ref[...])
        a_halo = jnp.where(i > 0, _dot_nt(xh_ref[...], wa_ref[...]), 0.0)
        a = _conv_fwd(jnp.concatenate([a_halo, a_pre], axis=0), cw_ref, cb_ref, TM)
        b = _dot_nt(x_ref[...], wb_ref[...])
        cdf = 0.5 * (1.0 + lax.erf(a * (1.0 / math.sqrt(2.0))))
        pdf = jnp.exp(-0.5 * a * a) * (1.0 / math.sqrt(2.0 * math.pi))
        act = a * cdf
        apre_ref[...] = a_pre
        act_ref[...] = act
        gate_ref[...] = b * (cdf + a * pdf)
        f_ref[...] = (act * b).astype(MXU)

    blk = lambda r, c: pl.BlockSpec((None, r, c), lambda j, i: (j, 0, 0))
    tokj = pl.BlockSpec((None, TM, FF_BLK), lambda j, i: (j, i, 0))
    outs = [jax.ShapeDtypeStruct((N_SHARD, t, FF_BLK), F32)] * 3 + [jax.ShapeDtypeStruct((N_SHARD, t, FF_BLK), MXU)]
    return pl.pallas_call(
        body, name="ffn_in", grid=(N_SHARD, t // TM),
        in_specs=[pl.BlockSpec((TM, D_MODEL), lambda j, i: (i, 0)),
                  pl.BlockSpec((HALO, D_MODEL), lambda j, i: (jnp.maximum(i * hb - 1, 0), 0)),
                  blk(FF_BLK, D_MODEL), blk(FF_BLK, D_MODEL), blk(3, FF_BLK), blk(1, FF_BLK)],
        out_specs=[tokj, tokj, tokj, tokj], out_shape=outs,
        compiler_params=_cp(dimension_semantics=("arbitrary", "arbitrary")),
    )(x1b, x1b, w_a, w_b, conv_w, conv_b)


def _ffn_out_ln2(f, w_down, xhat1, ln1_g, ln1_b):
    t = xhat1.shape[0]

    def body(f_ref, wd_ref, xh_ref, g1_ref, b1_ref, xhat_ref, rstd_ref):
        half = TM // ROW_GROUPS
        for r0 in range(0, TM, half):
            rows = pl.ds(r0, half)
            ff = jnp.dot(f_ref[0, rows, :], wd_ref[0], preferred_element_type=F32)
            for j in range(1, N_SHARD):
                ff = ff + jnp.dot(f_ref[j, rows, :], wd_ref[j], preferred_element_type=F32)
            x1 = xh_ref[rows, :] * g1_ref[...] + b1_ref[...]
            xhat, rstd = _ln_fwd(ALPHA * x1 + ff)
            xhat_ref[rows, :] = xhat
            rstd_ref[rows, :] = rstd

    tok = lambda w: pl.BlockSpec((TM, w), lambda i: (i, 0))
    vec = _full((1, D_MODEL))
    outs = [jax.ShapeDtypeStruct((t, D_MODEL), F32), jax.ShapeDtypeStruct((t, 1), F32)]
    return pl.pallas_call(
        body, name="ffn_out_ln2", grid=(t // TM,),
        in_specs=[pl.BlockSpec((N_SHARD, TM, FF_BLK), lambda i: (0, i, 0)), _full(w_down.shape), tok(D_MODEL), vec, vec],
        out_specs=[tok(D_MODEL), tok(1)], out_shape=outs,
        compiler_params=_cp(dimension_semantics=("arbitrary",)),
    )(f, w_down, xhat1, ln1_g, ln1_b)


STAT_ROWS = 8


def _ple_loss_bwd(xhat2, rstd2, p, target, ln2_g, ln2_b, w_g, b_g, w_p, ln3_g, ln3_b):
    t = xhat2.shape[0]

    def body(xh2_ref, rs2_ref, p_ref, t_ref, g2_ref, b2_ref, wg_ref, bg_ref, wp_ref, g3_ref, b3_ref,
             dr2_ref, dr2b_ref, stat_ref, dwg_ref, dwp_ref, pp_scr, dwp_scr):
        @pl.when(pl.program_id(0) == 0)
        def _():
            stat_ref[...] = jnp.zeros_like(stat_ref)
            dwg_ref[...] = jnp.zeros_like(dwg_ref)
            dwp_scr[...] = jnp.zeros_like(dwp_scr)

        xhat2 = xh2_ref[...]
        x2 = xhat2 * g2_ref[...] + b2_ref[...]
        x2b = x2.astype(MXU)
        gate = jax.nn.sigmoid(jnp.dot(x2b, wg_ref[...], preferred_element_type=F32) + bg_ref[...])
        pb = p_ref[...].astype(MXU)
        for j in range(N_SHARD):
            pp_scr[:, j * ROW_BLK:(j + 1) * ROW_BLK] = jnp.dot(pb, wp_ref[j], preferred_element_type=F32)
        pp = pp_scr[...]
        xhat3, rstd3 = _ln_fwd(ALPHA * x2 + gate * pp)
        err = xhat3 * g3_ref[...] + b3_ref[...] - t_ref[...]
        dy = err * (1.0 / D_MODEL)
        dr3 = _ln_bwd(dy, xhat3, rstd3, g3_ref[...])
        dgp = dr3 * pp * gate * (1.0 - gate)
        dgp_b = dgp.astype(MXU)
        dwg_ref[...] += _dot_tn(x2b, dgp_b)
        dwp_scr[...] += _dot_tn(pb, dr3 * gate)
        dx2 = ALPHA * dr3 + _dot_nt(dgp_b, wg_ref[...])
        dr2 = _ln_bwd(dx2, xhat2, rs2_ref[...], g2_ref[...])
        dr2_ref[...] = dr2
        dr2b_ref[...] = dr2.astype(MXU)
        stat_ref[0:1, :] += _colsum(dy * xhat3)
        stat_ref[1:2, :] += _colsum(dy)
        stat_ref[2:3, :] += _colsum(dgp)
        stat_ref[3:4, :] += _colsum(dx2 * xhat2)
        stat_ref[4:5, :] += _colsum(dx2)
        stat_ref[5:6, :] += _colsum(err * err)

        @pl.when(pl.program_id(0) == t // TM - 1)
        def _():
            for j in range(N_SHARD):
                dwp_ref[j] = dwp_scr[:, j * ROW_BLK:(j + 1) * ROW_BLK]

    tok = lambda w: pl.BlockSpec((TM, w), lambda i: (i, 0))
    vec = _full((1, D_MODEL))
    outs = [jax.ShapeDtypeStruct((t, D_MODEL), F32), jax.ShapeDtypeStruct((t, D_MODEL), MXU),
            jax.ShapeDtypeStruct((STAT_ROWS, D_MODEL), F32), jax.ShapeDtypeStruct((D_MODEL, D_MODEL), F32),
            jax.ShapeDtypeStruct((N_SHARD, D_PLE, ROW_BLK), F32)]
    return pl.pallas_call(
        body, name="ple_loss_bwd", grid=(t // TM,),
        in_specs=[tok(D_MODEL), tok(1), tok(D_PLE), tok(D_MODEL), vec, vec, _full(w_g.shape), vec, _full(w_p.shape), vec, vec],
        out_specs=[tok(D_MODEL), tok(D_MODEL), _full((STAT_ROWS, D_MODEL)), _full((D_MODEL, D_MODEL)),
                   _full((N_SHARD, D_PLE, ROW_BLK))], out_shape=outs,
        scratch_shapes=[pltpu.VMEM((TM, D_MODEL), F32), pltpu.VMEM((D_PLE, D_MODEL), F32)],
        compiler_params=_cp(dimension_semantics=("arbitrary",)),
    )(xhat2, rstd2, p, target, ln2_g, ln2_b, w_g, b_g, w_p, ln3_g, ln3_b)


def _ffn_bwd(dr2, dr2b, a_pre, act, gate, w_down, w_a, w_b, conv_w, xhat1, rstd1, ln1_g, cat):
    t = dr2.shape[0]
    nt = t // TM
    hb = TM // HALO
    last_h = t // HALO - 1
    halo2 = 2 * HALO

    def body(dr_ref, drb_ref, drbn_ref, ap_ref, act_ref, gate_ref, gaten_ref, wd_ref, wa_ref, wb_ref, cw_ref,
             xh_ref, rs_ref, g1_ref, cat_ref, dap_ref, dbb_ref, dr1_ref, cstat_ref, lstat_ref, dwo_ref, acc_scr):
        i, j = pl.program_id(0), pl.program_id(1)

        @pl.when((i == 0) & (j == 0))
        def _():
            cstat_ref[...] = jnp.zeros_like(cstat_ref)
            lstat_ref[...] = jnp.zeros_like(lstat_ref)
            dwo_ref[...] = jnp.zeros_like(dwo_ref)

        half = TM // ROW_GROUPS
        parts = []
        for r0 in range(0, TM, half):
            rows = pl.ds(r0, half)
            last = r0 + half == TM

            def ext(ref, nxt):
                return jnp.concatenate([ref[rows], nxt[...]], axis=0) if last else ref[r0:r0 + half + HALO]

            drb = jnp.concatenate([drb_ref[rows, :], drbn_ref[...]], axis=0) if last else drb_ref[r0:r0 + half + halo2, :]
            df = _dot_nt(drb, wd_ref[...])[0:half + HALO]
            da = df * ext(gate_ref, gaten_ref)
            if last:
                da = jnp.concatenate([da[0:half], jnp.where(i < nt - 1, da[half:], 0.0)], axis=0)
            ahead = [da[0:half]] + [pltpu.roll(da, half + HALO - s, 0)[0:half] for s in (1, 2)]
            da_pre = cw_ref[2:3, :] * ahead[0] + cw_ref[1:2, :] * ahead[1] + cw_ref[0:1, :] * ahead[2]
            dbb = df[0:half] * act_ref[rows, :]
            dap_ref[rows, :] = da_pre.astype(MXU)
            dbb_ref[rows, :] = dbb.astype(MXU)
            for kk in range(3):
                cstat_ref[j, kk:kk + 1, :] += _colsum(ahead[2 - kk] * ap_ref[rows, :])
            cstat_ref[j, 3:4, :] += _colsum(ahead[0])
            parts.append(_dot(da_pre, wa_ref[...]) + _dot(dbb, wb_ref[...]))
        part = jnp.concatenate(parts, axis=0)

        @pl.when(j == 0)
        def _():
            acc_scr[...] = ALPHA * dr_ref[...] + part

        @pl.when(j > 0)
        def _():
            acc_scr[...] += part

        @pl.when(j == N_SHARD - 1)
        def _():
            dx1 = acc_scr[...]
            xhat1 = xh_ref[...]
            lstat_ref[0:1, :] += _colsum(dx1 * xhat1)
            lstat_ref[1:2, :] += _colsum(dx1)
            dr1 = _ln_bwd(dx1, xhat1, rs_ref[...], g1_ref[...])
            dr1_ref[...] = dr1
            dwo_ref[...] += _dot_tn(cat_ref[...], dr1)

    tok = lambda w: pl.BlockSpec((TM, w), lambda i, j: (i, 0))
    tokj = pl.BlockSpec((None, TM, FF_BLK), lambda i, j: (j, i, 0))
    nextj = pl.BlockSpec((None, HALO, FF_BLK), lambda i, j: (j, jnp.minimum((i + 1) * hb, last_h), 0))
    blk = lambda r, c: pl.BlockSpec((None, r, c), lambda i, j: (j, 0, 0))
    outs = [jax.ShapeDtypeStruct((N_SHARD, t, FF_BLK), MXU)] * 2 + [
        jax.ShapeDtypeStruct((t, D_MODEL), F32), jax.ShapeDtypeStruct((N_SHARD, STAT_ROWS, FF_BLK), F32),
        jax.ShapeDtypeStruct((STAT_ROWS, D_MODEL), F32), jax.ShapeDtypeStruct((D_MODEL, D_MODEL), F32)]
    return pl.pallas_call(
        body, name="ffn_bwd", grid=(nt, N_SHARD),
        in_specs=[tok(D_MODEL), tok(D_MODEL),
                  pl.BlockSpec((halo2, D_MODEL), lambda i, j: (jnp.minimum((i + 1) * (hb // 2), last_h // 2), 0)),
                  tokj, tokj, tokj, nextj, blk(FF_BLK, D_MODEL), blk(FF_BLK, D_MODEL), blk(FF_BLK, D_MODEL),
                  blk(3, FF_BLK), tok(D_MODEL), tok(1), _full((1, D_MODEL)), tok(D_MODEL)],
        out_specs=[tokj, tokj, tok(D_MODEL), _full((N_SHARD, STAT_ROWS, FF_BLK)), _full((STAT_ROWS, D_MODEL)),
                   _full((D_MODEL, D_MODEL))], out_shape=outs,
        scratch_shapes=[pltpu.VMEM((TM, D_MODEL), F32)],
        compiler_params=_cp(dimension_semantics=("arbitrary", "arbitrary")),
    )(dr2, dr2b, dr2b, a_pre, act, gate, gate, w_down, w_a, w_b, conv_w, xhat1, rstd1, ln1_g, cat)


def _mix_bwd(dr1, w_o, hu, hz, mixed, attn, ln_z_g, ln_z_b, w_s, dep):
    t = dr1.shape[0]
    nchunk = TM // BLK

    def body(dr_ref, wo_ref, hu_ref, hz_ref, mx_ref, attn_ref, g_ref, b_ref, ws_ref, grp_ref, red_ref, dep_ref,
             do1_ref, do4_ref, do16_ref, dl1_ref, dl4_ref, dl16_ref, duz_ref, dws_ref, dbs_ref, zstat_ref,
             wm_scr, dzn_scr, dbsum_scr, do_scr, dl_scr):
        @pl.when(pl.program_id(0) == 0)
        def _():
            row = lax.broadcasted_iota(jnp.int32, (BLK, BLK), 0)
            col = lax.broadcasted_iota(jnp.int32, (BLK, BLK), 1)
            for g in range(N_HEADS):
                wm_scr[g] = jnp.where(col <= row, ws_ref[g], 0.0).astype(MXU)
            dws_ref[...] = jnp.zeros_like(dws_ref)
            dbsum_scr[...] = jnp.zeros_like(dbsum_scr)
            zstat_ref[...] = jnp.zeros_like(zstat_ref)

        dcat = _dot_nt(dr_ref[...], wo_ref[...])
        dattn = dcat[:, 0:D_ATTN]
        do1_ref[...] = dattn.astype(MXU)
        for cc, val in enumerate(_chunks(dattn)):
            do_scr[cc] = val
        _to_planes(do4_ref, do_scr, DILATIONS[1], LANE_CHUNKS, MXU)
        _to_planes(do16_ref, do_scr, DILATIONS[2], LANE_CHUNKS, MXU)
        delta = _dot_select(dattn * attn_ref[...], red_ref[...])
        dl1_ref[...] = delta
        dl_scr[0] = delta
        _to_planes(dl4_ref, dl_scr, DILATIONS[1], 1, F32)
        _to_planes(dl16_ref, dl_scr, DILATIONS[2], 1, F32)
        dgm = dcat[:, D_ATTN:]
        hu, hz = hu_ref[...], hz_ref[...]
        u = _gelu(hu)
        duz_ref[:, 0:D_GMLP] = (dgm * mx_ref[...] * _gelu_grad(hu)).astype(MXU)
        dmixed = dgm * u
        dmb = dmixed.astype(MXU)
        zhat, rstd = _ln_fwd(_gelu(hz))
        znb = (zhat * g_ref[...] + b_ref[...]).astype(MXU)
        dbs_acc = jnp.zeros((BLK, D_GMLP), F32)
        for ch in range(nchunk):
            rows = slice(ch * BLK, (ch + 1) * BLK)
            dbs_acc = dbs_acc + dmixed[rows]
            for g in range(N_HEADS):
                cols = slice(g * HEAD_DIM, (g + 1) * HEAD_DIM)
                dzn_scr[rows, cols] = _dot_tn(wm_scr[g], dmb[rows, cols])
                dws_ref[g] += _dot_nt(dmb[rows, cols], znb[rows, cols])
        dbsum_scr[...] += dbs_acc
        dzn = dzn_scr[...]
        zstat_ref[0:1, :] += _colsum(dzn * zhat)
        zstat_ref[1:2, :] += _colsum(dzn)
        duz_ref[:, D_GMLP:] = (_ln_bwd(dzn, zhat, rstd, g_ref[...]) * _gelu_grad(hz)).astype(MXU)

        @pl.when(pl.program_id(0) == nt - 1)
        def _():
            row = lax.broadcasted_iota(jnp.int32, (BLK, BLK), 0)
            col = lax.broadcasted_iota(jnp.int32, (BLK, BLK), 1)
            for g in range(N_HEADS):
                dws_ref[g] = jnp.where(col <= row, dws_ref[g], 0.0)
            dbs_ref[...] = lax.dot_general(grp_ref[...], dbsum_scr[...], (((1,), (1,)), ((), ())),
                                           precision=lax.Precision.HIGHEST, preferred_element_type=F32)

    nt = t // TM
    tok = lambda w: pl.BlockSpec((TM, w), lambda i: (i, 0))
    grp = jnp.asarray((np.arange(D_GMLP)[None, :] // HEAD_DIM == np.arange(N_HEADS)[:, None]).astype(np.float32))
    red = _head_reduce()
    outs = [_perm_shape(t, d, D_ATTN, MXU) for d in DILATIONS] + [_perm_shape(t, d, 128, F32) for d in DILATIONS] + [
        jax.ShapeDtypeStruct((t, 2 * D_GMLP), MXU),
        jax.ShapeDtypeStruct((N_HEADS, BLK, BLK), F32), jax.ShapeDtypeStruct((N_HEADS, BLK), F32),
        jax.ShapeDtypeStruct((STAT_ROWS, D_GMLP), F32)]
    return pl.pallas_call(
        body, name="mix_bwd", grid=(t // TM,),
        in_specs=[tok(D_MODEL), _full(w_o.shape), tok(D_GMLP), tok(D_GMLP), tok(D_GMLP), tok(D_ATTN), _full(ln_z_g.shape),
                  _full(ln_z_b.shape), _full(w_s.shape), _full(grp.shape), _full(red.shape), pl.BlockSpec(memory_space=pl.ANY)],
        out_specs=[_perm_tile(d, D_ATTN) for d in DILATIONS] + [_perm_tile(d, 128) for d in DILATIONS]
        + [tok(2 * D_GMLP), _full((N_HEADS, BLK, BLK)), _full((N_HEADS, BLK)), _full((STAT_ROWS, D_GMLP))],
        out_shape=outs,
        scratch_shapes=[pltpu.VMEM((N_HEADS, BLK, BLK), MXU), pltpu.VMEM((TM, D_GMLP), F32), pltpu.VMEM((BLK, D_GMLP), F32),
                        pltpu.VMEM((LANE_CHUNKS, TM, 128), F32), pltpu.VMEM((1, TM, 128), F32)],
        compiler_params=_cp(dimension_semantics=("arbitrary",)),
    )(dr1, w_o, hu, hz, mixed, attn, ln_z_g, ln_z_b, w_s, grp, red, dep)


def _dx_in(dqs, dks, dvs, duz, dr1, w_in, c_tab, s1_tab, s2_tab):
    t = dr1.shape[0]

    def body(dq1, dq4, dq16, dk1, dk4, dk16, dv1, dv4, dv16, duz_ref, dr_ref, w_ref, c_ref, s1_ref, s2_ref,
             dh_ref, dx_ref, acc_scr):
        sums = []
        for part, (g1, g4, g16) in enumerate(((dq1, dq4, dq16), (dk1, dk4, dk16), (dv1, dv4, dv16))):
            acc = acc_scr.at[pl.ds(part * LANE_CHUNKS, LANE_CHUNKS)]
            for cc in range(LANE_CHUNKS):
                acc[cc] = g1[:, cc * 128:(cc + 1) * 128]
            _from_planes(g4, acc, DILATIONS[1], LANE_CHUNKS, accumulate=True)
            _from_planes(g16, acc, DILATIONS[2], LANE_CHUNKS, accumulate=True)
            sums.append(_unchunk(acc_scr, LANE_CHUNKS, part * LANE_CHUNKS))
        c, s1, s2 = _tile_heads(c_ref[...]), _tile_heads(s1_ref[...]), _tile_heads(s2_ref[...])
        dh_ref[:, 0:D_ATTN] = _rope_apply_t(sums[0] * (1.0 / math.sqrt(HEAD_DIM)), c, s1, s2).astype(MXU)
        dh_ref[:, D_ATTN:2 * D_ATTN] = _rope_apply_t(sums[1], c, s1, s2).astype(MXU)
        dh_ref[:, 2 * D_ATTN:3 * D_ATTN] = sums[2].astype(MXU)
        dh_ref[:, 3 * D_ATTN:] = duz_ref[...]
        dx = ALPHA * dr_ref[...]
        for j in range(N_SHARD):
            dx = dx + _dot_nt(dh_ref[:, j * W_IN_BLK:(j + 1) * W_IN_BLK], w_ref[j])
        dx_ref[...] = dx

    tok = lambda w: pl.BlockSpec((TM, w), lambda i: (i, 0))
    outs = [jax.ShapeDtypeStruct((t, D_IN), MXU), jax.ShapeDtypeStruct((t, D_MODEL), F32)]
    return pl.pallas_call(
        body, name="dx_in", grid=(t // TM,),
        in_specs=[_perm_tile(d, D_ATTN) for d in DILATIONS] * 3
        + [tok(2 * D_GMLP), tok(D_MODEL), _full(w_in.shape), tok(128), tok(128), tok(128)],
        out_specs=[tok(D_IN), tok(D_MODEL)], out_shape=outs,
        scratch_shapes=[pltpu.VMEM((3 * LANE_CHUNKS, TM, 128), F32)],
        compiler_params=_cp(dimension_semantics=("arbitrary",)),
    )(*dqs, *dks, *dvs, duz, dr1, w_in, c_tab, s1_tab, s2_tab)


def _wgrad(name, x, dy, x_spec, dy_spec, out_spec, out_shape, grid, dep=None):
    deps = [] if dep is None else [dep]

    def body(x_ref, dy_ref, *rest):
        rest[-1][...] = _dot_tn(x_ref[...], dy_ref[...])

    return pl.pallas_call(
        body, name=name, grid=grid, in_specs=[x_spec, dy_spec] + [pl.BlockSpec(memory_space=pl.ANY)] * len(deps),
        out_specs=out_spec, out_shape=jax.ShapeDtypeStruct(out_shape, F32),
        compiler_params=_cp(dimension_semantics=("arbitrary",) * len(grid)),
    )(x, dy, *deps)


def _wgrad_pair(name, xa, xb, dy, x_spec, dy_spec, out_spec, out_shape, grid):
    def body(xa_ref, xb_ref, dy_ref, oa_ref, ob_ref):
        dy = dy_ref[...]
        oa_ref[...] = _dot_tn(xa_ref[...], dy)
        ob_ref[...] = _dot_tn(xb_ref[...], dy)

    return pl.pallas_call(
        body, name=name, grid=grid, in_specs=[x_spec, x_spec, dy_spec], out_specs=[out_spec, out_spec],
        out_shape=[jax.ShapeDtypeStruct(out_shape, F32)] * 2,
        compiler_params=_cp(dimension_semantics=("arbitrary",) * len(grid)),
    )(xa, xb, dy)


def _local_step(x, p, rope, target, w_in, start_dep, late_landed, late_weights, early_grads, early_grads_sent,
                early_grads_landed,
                ln_z_g, ln_z_b, w_s, b_s, ln1_g, ln1_b, conv_b, ln2_g, ln2_b, b_g, ln3_g, ln3_b):
    t = x.shape[0]
    half = TM
    c_tab, s1_tab, s2_tab = rope
    b_full = jnp.repeat(jnp.transpose(b_s[0]), HEAD_DIM, axis=1)
    conv_b4 = conv_b.reshape(N_SHARD, 1, FF_BLK)
    *qkvs, hu, hz, mixed, gm, xb = _qkvuz(x, w_in, c_tab, s1_tab, s2_tab, ln_z_g, ln_z_b, w_s[0], b_full, start_dep)
    branches = [_attn_fwd(qkv, d, start_dep) for qkv, d in zip(qkvs[:2], DILATIONS[:2])]
    dep = late_landed(branches[-1][1])
    branches.append(_attn_fwd(qkvs[2], DILATIONS[2], dep))
    w_o, w_a, w_b, conv_w, w_down, w_g, w_p = late_weights(branches[-1][1])
    attn, *lses, cat, xhat1, rstd1, x1b = _mix_ln1(
        [o for o, _ in branches], [l for _, l in branches], gm, x, w_o, ln1_g, ln1_b, dep)
    a_pre, act, gate, f = _ffn_in(x1b, w_a, w_b, conv_w, conv_b4)
    xhat2, rstd2 = _ffn_out_ln2(f, w_down, xhat1, ln1_g, ln1_b)
    dr2, dr2b, stat3, g_w_g, g_w_p = _ple_loss_bwd(xhat2, rstd2, p, target, ln2_g, ln2_b, w_g, b_g, w_p, ln3_g, ln3_b)
    da_pre, dbb, dr1, cstat, stat1, g_w_o = _ffn_bwd(dr2, dr2b, a_pre, act, gate, w_down, w_a, w_b, conv_w, xhat1, rstd1,
                                                    ln1_g, cat)

    full_t = lambda w, im: pl.BlockSpec((t, w), im)
    ffj = pl.BlockSpec((None, t, FF_BLK), lambda j, kk: (j, 0, 0))
    early = dict(
        w_ple_gate=g_w_g, w_ple_in=g_w_p,
        w_ff_down=_wgrad("dw_down", f, dr2b, ffj, full_t(half, lambda j, n: (0, n)),
                         pl.BlockSpec((None, FF_BLK, half), lambda j, n: (j, 0, n)), (N_SHARD, FF_BLK, D_MODEL), (N_SHARD, 2)),
        **dict(zip(("w_ff_a", "w_ff_b"), _wgrad_pair(
            "dw_ab", da_pre, dbb, x1b, ffj, full_t(half, lambda j, n: (0, n)),
            pl.BlockSpec((None, FF_BLK, half), lambda j, n: (j, 0, n)), (N_SHARD, FF_BLK, D_MODEL), (N_SHARD, 2)))),
        w_o=g_w_o)
    dep = early_grads(early)

    do1, do4, do16, dl1, dl4, dl16, duz, dws, dbs, zstat = _mix_bwd(
        dr1, w_o, hu, hz, mixed, attn, ln_z_g, ln_z_b, w_s[0], dep)
    dep = early_grads_sent(duz, (stat3, stat1, zstat, cstat, dws, dbs))
    dqkv = [_attn_bwd(qkv, do, lse, dl, d, dep)
            for qkv, do, lse, dl, d in zip(qkvs, (do1, do4, do16), lses, (dl1, dl4, dl16), DILATIONS)]
    dh, grad_x = _dx_in([g[0] for g in dqkv], [g[1] for g in dqkv], [g[2] for g in dqkv], duz, dr1, w_in,
                        c_tab, s1_tab, s2_tab)
    dep = early_grads_landed(grad_x)
    g_w_in = _wgrad("dw_in", xb, dh, full_t(half, lambda j, kk: (0, kk)), full_t(W_IN_BLK, lambda j, kk: (0, j)),
                    pl.BlockSpec((None, half, W_IN_BLK), lambda j, kk: (j, kk, 0)), (N_SHARD, D_MODEL, W_IN_BLK), (N_SHARD, 2),
                    dep)
    return grad_x, g_w_in


def _tile_rows(rows, mult, steps):
    if rows % mult:
        return rows
    return next(rows // k for k in range(steps, rows + 1) if rows % k == 0 and (rows // k) % mult == 0)


def _grid_spec(grid, in_specs, out_specs):
    return pltpu.PrefetchScalarGridSpec(num_scalar_prefetch=1, grid=grid, in_specs=in_specs, out_specs=out_specs)


def _on_own_steps(i, count, steps, work):
    if count == steps:
        work()
    else:
        pl.when(i < count)(work)


def _place_shards(name, ws, dtypes, place, dep):
    n = len(ws)
    tiles = [_tile_rows(w.shape[0], 16, 2) for w in ws]
    counts = [w.shape[0] // t for w, t in zip(ws, tiles)]
    steps = max(counts)

    def body(s_ref, *refs):
        i = pl.program_id(0)
        for a in range(n):
            def work(a=a):
                refs[n + 1 + a][...] = refs[a][...].astype(dtypes[a])
            _on_own_steps(i, counts[a], steps, work)

    def tile(a, lead):
        last = counts[a] - 1
        if lead:
            return pl.BlockSpec((None, tiles[a], ws[a].shape[1]), lambda i, s: (s[0], jnp.minimum(i, last), 0))
        return pl.BlockSpec((tiles[a], ws[a].shape[1]), lambda i, s: (jnp.minimum(i, last), 0))

    return pl.pallas_call(
        body, name=name,
        grid_spec=_grid_spec((steps,), [tile(a, False) for a in range(n)] + [pl.BlockSpec(memory_space=pl.ANY)],
                             [tile(a, True) for a in range(n)]),
        out_shape=[jax.ShapeDtypeStruct((N_SHARD, *w.shape), dt) for w, dt in zip(ws, dtypes)],
        compiler_params=_cp())(place, *ws, dep)


def _pair_sums(name, mines, gots, place):
    n = len(mines)
    tiles = [_tile_rows(g.shape[1], 16, 2) for g in gots]
    per_blk = [g.shape[1] // t for g, t in zip(gots, tiles)]
    counts = [N_SHARD * nh for nh in per_blk]
    steps = max(counts)

    def body(s_ref, *refs):
        i = pl.program_id(0)
        for a in range(n):
            def work(a=a):
                refs[2 * n + a][...] = (refs[a][...] + refs[n + a][...]).astype(BF16)
            _on_own_steps(i, counts[a], steps, work)

    def tile(a, mine):
        nh, last = per_blk[a], counts[a] - 1

        def index(i, s):
            g = jnp.minimum(i, last)
            return (g // nh, (s[1] * nh if mine else 0) + g % nh, 0)

        return pl.BlockSpec((None, tiles[a], gots[a].shape[2]), index)

    return pl.pallas_call(
        body, name=name,
        grid_spec=_grid_spec((steps,), [tile(a, True) for a in range(n)] + [tile(a, False) for a in range(n)],
                             [tile(a, False) for a in range(n)]),
        out_shape=[jax.ShapeDtypeStruct(g.shape, BF16) for g in gots], compiler_params=_cp())(place, *mines, *gots)


def _chip_sums(name, owns, landeds, place, dep):
    n = len(owns)
    tiles = [_tile_rows(o.shape[1], 16, 4) for o in owns]
    counts = [o.shape[1] // t for o, t in zip(owns, tiles)]
    steps = max(counts)

    def body(s_ref, *refs):
        i = pl.program_id(0)
        for a in range(n):
            def work(a=a):
                own, l1, l2, l3 = (refs[4 * a + k][...].astype(F32) for k in range(4))
                refs[4 * n + 1 + a][...] = ((own + l1) + l2) + l3
            _on_own_steps(i, counts[a], steps, work)

    def slot(a, d):
        last = counts[a] - 1
        return pl.BlockSpec((None, tiles[a], owns[a].shape[2]), lambda i, s: ((s[0] + d) % N_SHARD, jnp.minimum(i, last), 0))

    def out(a):
        nh, last = counts[a], counts[a] - 1
        return pl.BlockSpec((tiles[a], owns[a].shape[2]), lambda i, s: (s[1] * nh + jnp.minimum(i, last), 0))

    operands = [x for o, l in zip(owns, landeds) for x in (o, l, l, l)]
    return pl.pallas_call(
        body, name=name,
        grid_spec=_grid_spec((steps,), [slot(a, d) for a in range(n) for d in range(4)] + [pl.BlockSpec(memory_space=pl.ANY)],
                             [out(a) for a in range(n)]),
        out_shape=[jax.ShapeDtypeStruct((2 * o.shape[1], o.shape[2]), F32) for o in owns],
        compiler_params=_cp())(place, *operands, dep)


def _adamw_math(w, g, m, v):
    m = ADAM_B1 * m + (1.0 - ADAM_B1) * g
    v = ADAM_B2 * v + (1.0 - ADAM_B2) * (g * g)
    m_hat = m / (1.0 - ADAM_B1 ** ADAM_STEP)
    v_hat = v / (1.0 - ADAM_B2 ** ADAM_STEP)
    delta = -ADAM_LR * (m_hat / (jnp.sqrt(v_hat) + ADAM_EPS) + ADAM_WD * w)
    return delta, m, v


def _adamw_shards(name, ws, gs, ms, vs):
    n = len(ws)
    tiles = [_tile_rows(w.shape[1], 8, 8) for w in ws]
    counts = [w.shape[1] // t for w, t in zip(ws, tiles)]
    steps = max(counts)

    def body(*refs):
        i = pl.program_id(0)
        for a in range(n):
            def work(a=a):
                w_ref, g_ref, m_ref, v_ref = refs[4 * a:4 * a + 4]
                d_ref, nm_ref, nv_ref = refs[4 * n + 3 * a:4 * n + 3 * a + 3]
                d_ref[...], nm_ref[...], nv_ref[...] = _adamw_math(w_ref[...], g_ref[...], m_ref[...], v_ref[...])
            _on_own_steps(i, counts[a], steps, work)

    def tile(a, lead):
        last, c = counts[a] - 1, ws[a].shape[2]
        if lead:
            return pl.BlockSpec((None, tiles[a], c), lambda i: (0, jnp.minimum(i, last), 0))
        return pl.BlockSpec((tiles[a], c), lambda i: (jnp.minimum(i, last), 0))

    res = pl.pallas_call(
        body, name=name, grid=(steps,),
        in_specs=[tile(a, lead) for a in range(n) for lead in (True, False, True, True)],
        out_specs=[tile(a, True) for a in range(n) for _ in range(3)],
        out_shape=[jax.ShapeDtypeStruct(w.shape, F32) for w in ws for _ in range(3)],
        compiler_params=_cp())(*[x for quad in zip(ws, gs, ms, vs) for x in quad])
    return [tuple(res[3 * a:3 * a + 3]) for a in range(n)]


MESH = pl.DeviceIdType.MESH
ANY = pl.BlockSpec(memory_space=pl.ANY)


def _place():
    x, y, c = lax.axis_index("x"), lax.axis_index("y"), lax.axis_index("c")
    chips = [(1 - x, y), (x, 1 - y), (1 - x, 1 - y)]
    return x, y, c, 2 * x + y, chips


def _remote(src, dst, send_sem, recv_sem, dev):
    return pltpu.make_async_remote_copy(src_ref=src, dst_ref=dst, send_sem=send_sem, recv_sem=recv_sem,
                                        device_id=dev, device_id_type=MESH)


def _half(ref, hc, rows):
    return ref.at[pl.ds(hc * (rows // 2), rows // 2)]


def _sibling_join(blocks, tag):
    n = len(blocks)

    def body(*refs):
        outs = refs[n:2 * n]
        send, recv = refs[2 * n:]
        x, y, c, _, _ = _place()
        cps = []
        for a in range(n):
            h = blocks[a].shape[0] // 2
            mine = outs[a].at[pl.ds(c * h, h)]
            cp = _remote(mine, mine, send.at[a], recv.at[a], (x, y, 1 - c))
            cp.start()
            cps.append(cp)
        for a, cp in enumerate(cps):
            h = blocks[a].shape[0] // 2
            theirs = outs[a].at[pl.ds((1 - c) * h, h)]
            _remote(theirs, theirs, send.at[a], recv.at[a], (x, y, 1 - c)).wait_recv()
            cp.wait_send()

    sem = pltpu.SemaphoreType.DMA
    return pl.pallas_call(body, name=f"rs_sibling_join_{tag}", in_specs=[ANY] * n, out_specs=[ANY] * n,
                          out_shape=[jax.ShapeDtypeStruct(b_.shape, b_.dtype) for b_ in blocks],
                          input_output_aliases={a: a for a in range(n)},
                          scratch_shapes=[sem((n,)), sem((n,))])(*blocks)


def _join_start(blocks, after, tag):
    n = len(blocks)

    def body(*refs):
        ins = refs[:n]
        send, recv = refs[n + 1], refs[n + 2]
        token = refs[2 * n + 3]
        x, y, c, _, _ = _place()
        for a in range(n):
            h = blocks[a].shape[0] // 2
            mine = ins[a].at[pl.ds(c * h, h)]
            _remote(mine, mine, send.at[a], recv.at[a], (x, y, 1 - c)).start()
        token[...] = jnp.zeros_like(token)

    sems = pltpu.SemaphoreType.DMA((n,))
    res = pl.pallas_call(
        body, name=f"join_start_{tag}", in_specs=[HBM] * n + [ANY],
        out_specs=[SEM, SEM] + [HBM] * n + [pl.BlockSpec(memory_space=pltpu.VMEM)],
        out_shape=[sems, sems] + [pltpu.HBM(b_.shape, b_.dtype) for b_ in blocks] + [TOKEN],
        input_output_aliases={a: a + 2 for a in range(n)}, compiler_params=_in_flight_params(),
    )(*[_in_hbm(b_) for b_ in blocks], after)
    return res[0], res[1], res[2:2 + n], res[2 + n]


def _join_wait(send, recv, blocks, after, tag):
    n = len(blocks)

    def body(*refs):
        ins = refs[:n]
        send_ref, recv_ref = refs[n], refs[n + 1]
        x, y, c, _, _ = _place()
        for a in range(n):
            h = blocks[a].shape[0] // 2
            mine, theirs = ins[a].at[pl.ds(c * h, h)], ins[a].at[pl.ds((1 - c) * h, h)]
            _remote(mine, mine, send_ref.at[a], recv_ref.at[a], (x, y, 1 - c)).wait_send()
            _remote(theirs, theirs, send_ref.at[a], recv_ref.at[a], (x, y, 1 - c)).wait_recv()

    return pl.pallas_call(
        body, name=f"join_wait_{tag}", in_specs=[HBM] * n + [SEM, SEM, ANY], out_specs=[HBM] * n,
        out_shape=[pltpu.HBM(b_.shape, b_.dtype) for b_ in blocks],
        input_output_aliases={a: a for a in range(n)}, compiler_params=_in_flight_params(),
    )(*blocks, send, recv, after)


HBM = pl.BlockSpec(memory_space=pltpu.HBM)
SEM = pl.BlockSpec(memory_space=pltpu.SEMAPHORE)
TOKEN = jax.ShapeDtypeStruct((8, 128), F32)


def _in_flight_params():
    return pltpu.CompilerParams(has_side_effects=pltpu.SideEffectType.DATAFLOW_SIDE_EFFECTING)


def _in_hbm(a):
    return pltpu.with_memory_space_constraint(a, pltpu.HBM)


def _gather_piece(ref, rows, split, slot, hc):
    return _half(ref.at[slot], hc, rows) if split else ref.at[slot]


def _gather_start(stacks, split, after, tag):
    n = len(stacks)

    def body(*refs):
        ins = refs[:n]
        send, recv = refs[n + 1], refs[n + 2]
        token = refs[2 * n + 3]
        _, _, c, j, chips = _place()
        for a in range(n):
            mine = _gather_piece(ins[a], stacks[a].shape[1], split[a], j, c)
            for t in range(3):
                _remote(mine, mine, send.at[3 * a + t], recv.at[3 * a + t], (*chips[t], c)).start()
        token[...] = jnp.zeros_like(token)

    sems = pltpu.SemaphoreType.DMA((3 * n,))
    res = pl.pallas_call(
        body, name=f"gather_start_{tag}", in_specs=[HBM] * n + [ANY],
        out_specs=[SEM, SEM] + [HBM] * n + [pl.BlockSpec(memory_space=pltpu.VMEM)],
        out_shape=[sems, sems] + [pltpu.HBM(s.shape, s.dtype) for s in stacks] + [TOKEN],
        input_output_aliases={a: a + 2 for a in range(n)}, compiler_params=_in_flight_params(),
    )(*[_in_hbm(s) for s in stacks], after)
    return res[0], res[1], res[2:2 + n], res[2 + n]


def _gather_wait(send, recv, stacks, split, after, tag):
    n = len(stacks)

    def body(*refs):
        ins = refs[:n]
        send_ref, recv_ref = refs[n], refs[n + 1]
        _, _, c, j, chips = _place()
        for a in range(n):
            rows = stacks[a].shape[1]
            mine = _gather_piece(ins[a], rows, split[a], j, c)
            for t, (px, py) in enumerate(chips):
                theirs = _gather_piece(ins[a], rows, split[a], 2 * px + py, c)
                _remote(mine, mine, send_ref.at[3 * a + t], recv_ref.at[3 * a + t], (px, py, c)).wait_send()
                _remote(theirs, theirs, send_ref.at[3 * a + t], recv_ref.at[3 * a + t], (px, py, c)).wait_recv()

    return pl.pallas_call(
        body, name=f"gather_wait_{tag}", in_specs=[HBM] * n + [SEM, SEM, ANY], out_specs=[HBM] * n,
        out_shape=[pltpu.HBM(s.shape, s.dtype) for s in stacks],
        input_output_aliases={a: a for a in range(n)}, compiler_params=_in_flight_params(),
    )(*stacks, send, recv, after)


def _gather_forward(stacks, split, tag):
    idx = [a for a in range(len(stacks)) if split[a]]
    n = len(idx)

    def body(*refs):
        outs = refs[n:2 * n]
        send, recv = refs[2 * n:]
        x, y, c, _, chips = _place()
        sends = []
        for t, (px, py) in enumerate(chips):
            for a in range(n):
                blk = _half(outs[a].at[2 * px + py], c, stacks[idx[a]].shape[1])
                cp = _remote(blk, blk, send.at[a, t], recv.at[a, t], (x, y, 1 - c))
                cp.start()
                sends.append(cp)
        for t, (px, py) in enumerate(chips):
            for a in range(n):
                blk = _half(outs[a].at[2 * px + py], 1 - c, stacks[idx[a]].shape[1])
                _remote(blk, blk, send.at[a, t], recv.at[a, t], (x, y, 1 - c)).wait_recv()
        for cp in sends:
            cp.wait_send()

    sem = pltpu.SemaphoreType.DMA
    res = pl.pallas_call(
        body, name=f"gather_forward_{tag}", in_specs=[ANY] * n, out_specs=[ANY] * n,
        out_shape=[jax.ShapeDtypeStruct(stacks[a].shape, stacks[a].dtype) for a in idx],
        input_output_aliases={a: a for a in range(n)}, scratch_shapes=[sem((n, 3)), sem((n, 3))],
    )(*[stacks[a] for a in idx])
    out = list(stacks)
    for a, r in zip(idx, res):
        out[a] = r
    return out


def _forward_start(stacks, after, tag):
    n = len(stacks)

    def body(*refs):
        ins = refs[:n]
        send, recv = refs[n + 1], refs[n + 2]
        token = refs[2 * n + 3]
        x, y, c, _, chips = _place()
        for a in range(n):
            for t, (px, py) in enumerate(chips):
                blk = _half(ins[a].at[2 * px + py], c, stacks[a].shape[1])
                _remote(blk, blk, send.at[3 * a + t], recv.at[3 * a + t], (x, y, 1 - c)).start()
        token[...] = jnp.zeros_like(token)

    sems = pltpu.SemaphoreType.DMA((3 * n,))
    res = pl.pallas_call(
        body, name=f"forward_start_{tag}", in_specs=[HBM] * n + [ANY],
        out_specs=[SEM, SEM] + [HBM] * n + [pl.BlockSpec(memory_space=pltpu.VMEM)],
        out_shape=[sems, sems] + [pltpu.HBM(s.shape, s.dtype) for s in stacks] + [TOKEN],
        input_output_aliases={a: a + 2 for a in range(n)}, compiler_params=_in_flight_params(),
    )(*[_in_hbm(s) for s in stacks], after)
    return res[0], res[1], res[2:2 + n], res[2 + n]


def _forward_wait(send, recv, stacks, after, tag):
    n = len(stacks)

    def body(*refs):
        ins = refs[:n]
        send_ref, recv_ref = refs[n], refs[n + 1]
        x, y, c, _, chips = _place()
        for a in range(n):
            for t, (px, py) in enumerate(chips):
                mine = _half(ins[a].at[2 * px + py], c, stacks[a].shape[1])
                theirs = _half(ins[a].at[2 * px + py], 1 - c, stacks[a].shape[1])
                _remote(mine, mine, send_ref.at[3 * a + t], recv_ref.at[3 * a + t], (x, y, 1 - c)).wait_send()
                _remote(theirs, theirs, send_ref.at[3 * a + t], recv_ref.at[3 * a + t], (x, y, 1 - c)).wait_recv()

    return pl.pallas_call(
        body, name=f"forward_wait_{tag}", in_specs=[HBM] * n + [SEM, SEM, ANY], out_specs=[HBM] * n,
        out_shape=[pltpu.HBM(s.shape, s.dtype) for s in stacks],
        input_output_aliases={a: a for a in range(n)}, compiler_params=_in_flight_params(),
    )(*stacks, send, recv, after)


def _swap_start(grads, tag):
    n = len(grads)

    def body(*refs):
        ins, gots = refs[:n], refs[n:2 * n]
        send, recv = refs[2 * n], refs[2 * n + 1]
        token = refs[4 * n + 2]
        x, y, c, _, _ = _place()
        for a in range(n):
            h = grads[a].shape[1] // 2
            _remote(ins[a].at[:, pl.ds((1 - c) * h, h)], gots[a], send.at[a], recv.at[a], (x, y, 1 - c)).start()
        token[...] = jnp.zeros_like(token)

    sems = pltpu.SemaphoreType.DMA((n,))
    halves = [(g.shape[0], g.shape[1] // 2, g.shape[2]) for g in grads]
    res = pl.pallas_call(
        body, name=f"swap_start_{tag}", in_specs=[HBM] * (2 * n),
        out_specs=[SEM, SEM] + [HBM] * (2 * n) + [pl.BlockSpec(memory_space=pltpu.VMEM)],
        out_shape=[sems, sems] + [pltpu.HBM(g.shape, g.dtype) for g in grads] + [pltpu.HBM(s, F32) for s in halves] + [TOKEN],
        input_output_aliases={a: a + 2 for a in range(2 * n)}, compiler_params=_in_flight_params(),
    )(*[_in_hbm(g) for g in grads], *[_in_hbm(lax.empty(s, F32)) for s in halves])
    return res[0], res[1], res[2:2 + n], res[2 + n:2 + 2 * n], res[2 + 2 * n]


def _swap_wait(send, recv, grads, gots, after, tag):
    n = len(grads)

    def body(*refs):
        ins, lnd = refs[:n], refs[n:2 * n]
        send_ref, recv_ref = refs[2 * n], refs[2 * n + 1]
        x, y, c, _, _ = _place()
        for a in range(n):
            h = grads[a].shape[1] // 2
            cp = _remote(ins[a].at[:, pl.ds((1 - c) * h, h)], lnd[a], send_ref.at[a], recv_ref.at[a], (x, y, 1 - c))
            cp.wait_send()
            cp.wait_recv()

    bufs = [pltpu.HBM(g.shape, g.dtype) for g in grads] + [pltpu.HBM(g.shape, g.dtype) for g in gots]
    res = pl.pallas_call(
        body, name=f"swap_wait_{tag}", in_specs=[HBM] * (2 * n) + [SEM, SEM, ANY], out_specs=[HBM] * (2 * n),
        out_shape=bufs, input_output_aliases={a: a for a in range(2 * n)}, compiler_params=_in_flight_params(),
    )(*grads, *gots, send, recv, after)
    return res[:n], res[n:]


def _exchange_start(parts, tag):
    n = len(parts)

    def body(*refs):
        ins, lands = refs[:n], refs[n:2 * n]
        send, recv = refs[2 * n], refs[2 * n + 1]
        token = refs[4 * n + 2]
        _, _, c, j, chips = _place()
        for t, (px, py) in enumerate(chips):
            for a in range(n):
                _remote(ins[a].at[2 * px + py], lands[a].at[j], send.at[3 * a + t], recv.at[3 * a + t], (px, py, c)).start()
        token[...] = jnp.zeros_like(token)

    sems = pltpu.SemaphoreType.DMA((3 * n,))
    bufs = [pltpu.HBM(p.shape, p.dtype) for p in parts]
    res = pl.pallas_call(
        body, name=f"exchange_start_{tag}", in_specs=[HBM] * (2 * n),
        out_specs=[SEM, SEM] + [HBM] * (2 * n) + [pl.BlockSpec(memory_space=pltpu.VMEM)],
        out_shape=[sems, sems] + bufs + bufs + [TOKEN],
        input_output_aliases={a: a + 2 for a in range(2 * n)}, compiler_params=_in_flight_params(),
    )(*[_in_hbm(p) for p in parts], *[_in_hbm(lax.empty(p.shape, p.dtype)) for p in parts])
    return res[0], res[1], res[2:2 + n], res[2 + n:2 + 2 * n], res[2 + 2 * n]


def _exchange_wait(send, recv, parts, lands, after, tag):
    n = len(parts)

    def body(*refs):
        ins, lnd = refs[:n], refs[n:2 * n]
        send_ref, recv_ref = refs[2 * n], refs[2 * n + 1]
        _, _, c, j, chips = _place()
        for t, (px, py) in enumerate(chips):
            jt = 2 * px + py
            for a in range(n):
                _remote(ins[a].at[jt], lnd[a].at[j], send_ref.at[3 * a + t], recv_ref.at[3 * a + t], (px, py, c)).wait_send()
                _remote(ins[a].at[jt], lnd[a].at[jt], send_ref.at[3 * a + t], recv_ref.at[3 * a + t], (px, py, c)).wait_recv()

    bufs = [pltpu.HBM(p.shape, p.dtype) for p in parts]
    res = pl.pallas_call(
        body, name=f"exchange_wait_{tag}", in_specs=[HBM] * (2 * n) + [SEM, SEM, ANY], out_specs=[HBM] * (2 * n),
        out_shape=bufs + bufs, input_output_aliases={a: a for a in range(2 * n)}, compiler_params=_in_flight_params(),
    )(*parts, *lands, send, recv, after)
    return res[:n], res[n:]


def _small_chip_sums(arrs):
    n = len(arrs)

    def body(*refs):
        ins, outs = refs[:n], refs[n:2 * n]
        sib = refs[2 * n:3 * n]
        send, recv = refs[3 * n:]
        x, y, c, j, _ = _place()
        swaps = [_remote(ins[a], sib[a], send.at[a], recv.at[a], (x, y, 1 - c)) for a in range(n)]
        for cp in swaps:
            cp.start()
        for a in range(n):
            swaps[a].wait_recv()
            outs[a][j] = ins[a][...] + sib[a][...]
        for cp in swaps:
            cp.wait_send()

    sem = pltpu.SemaphoreType.DMA
    vm = pl.BlockSpec(memory_space=pltpu.VMEM)
    return pl.pallas_call(
        body, name="small_chip_sums", in_specs=[vm] * n, out_specs=[vm] * n,
        out_shape=[jax.ShapeDtypeStruct((N_SHARD, *a.shape), F32) for a in arrs],
        scratch_shapes=[pltpu.VMEM(a.shape, F32) for a in arrs] + [sem((n,)), sem((n,))],
        compiler_params=_cp(),
    )(*arrs)


def _small_totals(stacks):
    n = len(stacks)

    def body(*refs):
        for a in range(n):
            refs[n + a][...] = ((refs[a][0] + refs[a][1]) + refs[a][2]) + refs[a][3]

    return pl.pallas_call(body, name="small_totals", out_shape=[jax.ShapeDtypeStruct(s.shape[1:], F32) for s in stacks],
                          compiler_params=_cp())(*stacks)


SMALL_1024 = ("ln1_g", "ln1_b", "ln2_g", "ln2_b", "b_ple_gate", "ln3_g", "ln3_b")


def _adamw_small(red3, red1, redz, g_conv_w, redc, red_ws, red_bs, params):
    shape2d = {"ln_z_g": (1, D_GMLP), "ln_z_b": (1, D_GMLP), "w_s": (N_HEADS * BLK, BLK), "b_s": (N_HEADS, BLK),
               "conv_w": (3, FF_BLK), "conv_b": (N_SHARD, FF_BLK), **{k: (1, D_MODEL) for k in SMALL_1024}}
    names = list(shape2d)
    flat = [a.reshape(shape2d[k]) for k in names for a in params[k]]

    def body(r3, r1, rz, gcw, rc, rws, rbs, *refs):
        ins, outs = refs[:3 * len(names)], refs[3 * len(names):]

        def grad_of(k):
            if k == "w_s":
                return rws[...]
            if k == "b_s":
                return rbs[...]
            if k == "conv_w":
                return gcw[0:3, :]
            if k == "conv_b":
                return jnp.concatenate([rc[j * STAT_ROWS + 3:j * STAT_ROWS + 4, :] for j in range(N_SHARD)], axis=0)
            src, row = {"ln3_g": (r3, 0), "ln3_b": (r3, 1), "b_ple_gate": (r3, 2), "ln2_g": (r3, 3), "ln2_b": (r3, 4),
                        "ln1_g": (r1, 0), "ln1_b": (r1, 1), "ln_z_g": (rz, 0), "ln_z_b": (rz, 1)}[k]
            return src[row:row + 1, :]

        for i, k in enumerate(names):
            w_ref, m_ref, v_ref = ins[3 * i:3 * i + 3]
            g_ref, d_ref, nm_ref, nv_ref = outs[4 * i:4 * i + 4]
            g = grad_of(k)
            g_ref[...] = g
            d_ref[...], nm_ref[...], nv_ref[...] = _adamw_math(w_ref[...], g, m_ref[...], v_ref[...])

    res = pl.pallas_call(
        body, name="adamw_small",
        out_shape=[jax.ShapeDtypeStruct(shape2d[k], F32) for k in names for _ in range(4)],
        compiler_params=_cp(),
    )(red3, red1, redz, g_conv_w, redc, red_ws, red_bs, *flat)
    return {k: tuple(r.reshape(params[k][0].shape) for r in res[4 * i:4 * i + 4]) for i, k in enumerate(names)}


WEIGHTS = ("w_in", "ln_z_g", "ln_z_b", "w_s", "b_s", "w_o", "ln1_g", "ln1_b", "w_ff_a", "w_ff_b", "conv_w", "conv_b",
           "w_ff_down", "ln2_g", "ln2_b", "w_ple_gate", "b_ple_gate", "w_ple_in", "ln3_g", "ln3_b")
BIG = ("w_in", "w_o", "w_ff_a", "w_ff_b", "w_ff_down", "w_ple_gate", "w_ple_in")
TRANSPOSED = ("w_ff_a", "w_ff_b")
LATE = ("w_o", "w_ff_a", "w_ff_b", "w_ff_down", "w_ple_gate", "w_ple_in", "conv_w")


def kernel(x, p, positions, w_in, ln_z_g, ln_z_b, w_s, b_s, w_o, ln1_g, ln1_b, w_ff_a, w_ff_b, conv_w, conv_b, w_ff_down, ln2_g, ln2_b, w_ple_gate, b_ple_gate, w_ple_in, ln3_g, ln3_b, loss_target, m_w_in, m_ln_z_g, m_ln_z_b, m_w_s, m_b_s, m_w_o, m_ln1_g, m_ln1_b, m_w_ff_a, m_w_ff_b, m_conv_w, m_conv_b, m_w_ff_down, m_ln2_g, m_ln2_b, m_w_ple_gate, m_b_ple_gate, m_w_ple_in, m_ln3_g, m_ln3_b, v_w_in, v_ln_z_g, v_ln_z_b, v_w_s, v_b_s, v_w_o, v_ln1_g, v_ln1_b, v_w_ff_a, v_w_ff_b, v_conv_w, v_conv_b, v_w_ff_down, v_ln2_g, v_ln2_b, v_w_ple_gate, v_b_ple_gate, v_w_ple_in, v_ln3_g, v_ln3_b):
    args = locals()
    w = {k: args[k] for k in WEIGHTS}
    m = {k: args["m_" + k] for k in WEIGHTS}
    v = {k: args["v_" + k] for k in WEIGHTS}

    for k in TRANSPOSED:
        w[k], m[k], v[k] = (jnp.swapaxes(a, 1, 2) for a in (w[k], m[k], v[k]))

    chip = 2 * lax.axis_index("x") + lax.axis_index("y")
    place = jnp.stack([chip, lax.axis_index("c")]).astype(jnp.int32)
    stack = dict(zip(["w_in"], _place_shards("cast_w_in", [w["w_in"][0]], [MXU], place, place)))
    i_send, i_recv, in_flight, dep = _gather_start([stack["w_in"]], [True], place, "w_in")
    stack.update(zip(LATE, _place_shards("cast_late", [w[k][0] for k in LATE],
                                         [F32 if k == "conv_w" else MXU for k in LATE], place, dep)))
    split_late = [k != "conv_w" for k in LATE]
    g_send, g_recv, late_flight, start_dep = _gather_start([stack[k] for k in LATE], split_late, place, "late")
    rope = _rope_tables(positions, x.shape[1], start_dep)
    landed_in = _gather_wait(i_send, i_recv, in_flight, [True], rope[0], "w_in")
    w_in_full, = _gather_forward(landed_in, [True], "w_in")
    halves =[k for k, sp in zip(LATE, split_late) if sp]
    trips = {}

    def late_landed(after):
        fw = dict(zip(LATE, _gather_wait(g_send, g_recv, late_flight, split_late, after, "late")))
        trips["late"] = (fw, *_forward_start([fw[k] for k in halves], fw["conv_w"], "late"))
        return trips["late"][-1]

    def late_weights(after):
        fw, send, recv, flight, _ = trips["late"]
        fw.update(zip(halves, _forward_wait(send, recv, flight, after, "late")))
        return (fw["w_o"].reshape(D_MODEL, D_MODEL), fw["w_ff_a"], fw["w_ff_b"], fw["conv_w"], fw["w_ff_down"],
                fw["w_ple_gate"].reshape(D_MODEL, D_MODEL), fw["w_ple_in"])

    def swap_started(names, grads, tag):
        stacked = [g.reshape(N_SHARD, *w[k].shape[1:]) for k, g in zip(names, grads)]
        return (names, tag, *_swap_start(stacked, tag))

    def partial_sums(swap, after):
        names, tag, send, recv, stacked, gots, _ = swap
        stacked, got = _swap_wait(send, recv, stacked, gots, after, tag)
        pair = _pair_sums(f"rs_pair_{tag}", stacked, got, place)
        return (names, tag, *_exchange_start(pair, tag))

    def chip_summed(trip, after, dep):
        names, tag, send, recv, pair, lands, _ = trip
        pair, landed = _exchange_wait(send, recv, pair, lands, after, tag)
        return _chip_sums(f"rs_sum_{tag}", pair, landed, place, dep), names, tag

    def reduced(trip, after, dep):
        blocks, names, tag = chip_summed(trip, after, dep)
        return dict(zip(names, _sibling_join(blocks, tag)))

    def early_grads_landed(after):
        blocks, names, tag = chip_summed(trips["early"], after, trips["small"][-1])
        trips["join"] = (names, *_join_start(blocks, after, tag))
        return trips["join"][-1]

    def early_grads(grads):
        trips["swap"] = swap_started(list(grads), list(grads.values()), "early")
        return trips["swap"][-1]

    def early_grads_sent(after, small):
        trips["early"] = partial_sums(trips["swap"], after)
        stat3, stat1, zstat, cstat, dws, dbs = small
        sums = _small_chip_sums([stat3, stat1, zstat, cstat.reshape(N_SHARD * STAT_ROWS, FF_BLK),
                                 dws.reshape(N_HEADS * BLK, BLK), dbs])
        trips["small"] = _gather_start(sums, [False] * len(sums), trips["early"][-1], "small")
        return trips["small"][-1]

    grad_x, g_w_in = _local_step(
        x[0], p[0, 0], rope, loss_target[0], w_in_full, start_dep, late_landed, late_weights, early_grads, early_grads_sent,
        early_grads_landed, ln_z_g, ln_z_b, w_s, b_s, ln1_g, ln1_b, conv_b, ln2_g, ln2_b, b_ple_gate, ln3_g, ln3_b)

    trips["w_in"] = partial_sums(swap_started(["w_in"], [g_w_in], "w_in"), g_w_in)
    out = {}

    def adamw(red, tag):
        names = list(red)
        steps = _adamw_shards(f"adamw_{tag}", [w[k] for k in names], [red[k] for k in names], [m[k] for k in names],
                              [v[k] for k in names])
        for k, (d, nm, nv) in zip(names, steps):
            out[k] = (red[k].reshape(w[k].shape), d, nm, nv)

    names, j_send, j_recv, j_flight, _ = trips["join"]
    adamw(dict(zip(names, _join_wait(j_send, j_recv, j_flight, trips["w_in"][-1], "early"))), "early")
    adamw(reduced(trips["w_in"], out["w_o"][3], start_dep), "w_in")
    for k in TRANSPOSED:
        out[k] = tuple(jnp.swapaxes(a, 1, 2) for a in out[k])

    s_send, s_recv, s_flight, _ = trips["small"]
    red3, red1, redz, redc, red_ws, red_bs = _small_totals(
        _gather_wait(s_send, s_recv, s_flight, [False] * len(s_flight), out["w_in"][3], "small"))
    loss = (0.5 / D_MODEL) * jnp.sum(red3[5])
    g_conv_w = lax.dynamic_slice_in_dim(redc, chip * STAT_ROWS, STAT_ROWS, 0)
    names_small = [k for k in WEIGHTS if k not in BIG]
    out.update(_adamw_small(red3, red1, redz, g_conv_w, redc, red_ws, red_bs, {k: (w[k], m[k], v[k]) for k in names_small}))

    return (loss, grad_x[None], *[out[k][0] for k in WEIGHTS], *[out[k][1] for k in WEIGHTS],
            *[out[k][2] for k in WEIGHTS], *[out[k][3] for k in WEIGHTS])
```

```python
import math

import numpy as np
import jax
import jax.numpy as jnp
from jax import lax
from jax.experimental import pallas as pl
from jax.experimental.pallas import tpu as pltpu

F32 = jnp.float32
BF16 = jnp.bfloat16
MXU = BF16

D_MODEL = 1024
HEAD_DIM = 64
N_HEADS = 8
D_ATTN = 512
D_GMLP = 512
D_IN = 2560
DILATIONS = (1, 4, 16)
BLK = 128
ROPE_THETA = 500000.0
ROPE_DIM = 16
D_FF = 2816
D_PLE = 256
LN_EPS = 1e-5
ALPHA = 2.0 ** 0.25
NEG_INF = -1e30
N_SHARD = 4
W_IN_BLK = D_IN // N_SHARD
FF_BLK = D_FF // N_SHARD
ROW_BLK = D_MODEL // N_SHARD
ADAM_LR, ADAM_B1, ADAM_B2, ADAM_EPS, ADAM_WD, ADAM_STEP = 0.001, 0.9, 0.999, 1e-08, 0.01, 10

TM = 512
HALO = 8
ROW_GROUPS = 2
VMEM_LIMIT = 56 * 1024 * 1024


def _cp(**kw):
    return pltpu.CompilerParams(vmem_limit_bytes=VMEM_LIMIT, **kw)


def _full(shape):
    n = len(shape)
    return pl.BlockSpec(shape, lambda *_: (0,) * n)


def _gelu(x):
    return 0.5 * x * (1.0 + lax.erf(x * (1.0 / math.sqrt(2.0))))


def _gelu_grad(x):
    return 0.5 * (1.0 + lax.erf(x * (1.0 / math.sqrt(2.0)))) + x * jnp.exp(-0.5 * x * x) * (1.0 / math.sqrt(2.0 * math.pi))


def _ln_fwd(r):
    mu = jnp.mean(r, axis=-1, keepdims=True)
    xc = r - mu
    var = jnp.mean(xc * xc, axis=-1, keepdims=True)
    rstd = lax.rsqrt(var + LN_EPS)
    return xc * rstd, rstd


def _ln_bwd(dy, xhat, rstd, g):
    dxh = dy * g
    m1 = jnp.mean(dxh, axis=-1, keepdims=True)
    m2 = jnp.mean(dxh * xhat, axis=-1, keepdims=True)
    return rstd * (dxh - m1 - xhat * m2)


def _dot(a, b):
    return jnp.dot(a.astype(MXU), b.astype(MXU), preferred_element_type=F32)


def _dot_nt(a, b):
    return lax.dot_general(a.astype(MXU), b.astype(MXU), (((1,), (1,)), ((), ())), preferred_element_type=F32)


def _dot_tn(a, b):
    return lax.dot_general(a.astype(MXU), b.astype(MXU), (((0,), (0,)), ((), ())), preferred_element_type=F32)


def _colsum(v):
    return jnp.sum(v, axis=0, keepdims=True)


def _rope_tables(positions, t, dep):
    inv = np.float32(ROPE_THETA) ** (-np.arange(0, ROPE_DIM, 2, dtype=np.float32) / np.float32(ROPE_DIM))
    half = ROPE_DIM // 2
    pos_rep = jnp.repeat(positions.reshape(t // 16, 16), half, axis=1)
    inv_row = jnp.asarray(np.tile(inv, 16)[None, :], F32)

    def trig_body(pos_ref, inv_ref, dep_ref, cos_ref, sin_ref):
        ang = pos_ref[...].astype(F32) * inv_ref[...]
        cos_ref[...] = jnp.cos(ang)
        sin_ref[...] = jnp.sin(ang)

    vm = pl.BlockSpec(memory_space=pltpu.VMEM)
    cos8, sin8 = pl.pallas_call(
        trig_body, name="rope_trig", in_specs=[vm, vm, pl.BlockSpec(memory_space=pl.ANY)], out_specs=[vm, vm],
        out_shape=(jax.ShapeDtypeStruct((t // 16, 128), F32), jax.ShapeDtypeStruct((t // 16, 128), F32)),
    )(pos_rep, inv_row, dep)
    cos8 = cos8.reshape(t, half)
    sin8 = sin8.reshape(t, half)

    lane = np.arange(128) % HEAD_DIM
    sel = (np.arange(half)[:, None] == (lane % half)[None, :])
    e_cos = (sel & (lane < ROPE_DIM)[None, :]).astype(np.float32)
    e_s1 = -(sel & (lane < half)[None, :]).astype(np.float32)
    e_s2 = (sel & ((lane >= half) & (lane < ROPE_DIM))[None, :]).astype(np.float32)
    ones = (lane >= ROPE_DIM).astype(np.float32)[None, :]

    def expand_body(cos_ref, sin_ref, ec_ref, e1_ref, e2_ref, ones_ref, c_ref, s1_ref, s2_ref):
        hp = lax.Precision.HIGHEST
        c_ref[...] = jnp.dot(cos_ref[...], ec_ref[...], precision=hp, preferred_element_type=F32) + ones_ref[...]
        s1_ref[...] = jnp.dot(sin_ref[...], e1_ref[...], precision=hp, preferred_element_type=F32)
        s2_ref[...] = jnp.dot(sin_ref[...], e2_ref[...], precision=hp, preferred_element_type=F32)

    tab = jax.ShapeDtypeStruct((t, 128), F32)
    return pl.pallas_call(expand_body, name="rope_expand", out_shape=(tab, tab, tab), compiler_params=_cp())(
        cos8, sin8, jnp.asarray(e_cos), jnp.asarray(e_s1), jnp.asarray(e_s2), jnp.asarray(ones))


def _tile_heads(tab):
    return jnp.concatenate([tab] * (D_ATTN // 128), axis=1)


def _rope_apply(v, c, s1, s2):
    n = v.shape[1]
    half = ROPE_DIM // 2
    return v * c + pltpu.roll(v, n - half, 1) * s1 + pltpu.roll(v, half, 1) * s2


def _rope_apply_t(g, c, s1, s2):
    n = g.shape[1]
    half = ROPE_DIM // 2
    return g * c + pltpu.roll(g * s1, half, 1) + pltpu.roll(g * s2, n - half, 1)


LANE_CHUNKS = D_ATTN // 128
HEAD_LANES = 128 // N_HEADS


def _perm_shape(t, d, w, dtype):
    return jax.ShapeDtypeStruct((d, t // d, w), dtype)


def _perm_tile(d, w):
    return pl.BlockSpec((None if d == 1 else d, TM // d, w), lambda i: (0, i, 0))


def _to_planes(ref, scr, d, n_chunks, dtype):
    for r in range(d):
        for cc in range(n_chunks):
            ref[r, :, cc * 128:(cc + 1) * 128] = scr.at[cc][pl.ds(r, TM // d, stride=d), :].astype(dtype)


def _from_planes(ref, scr, d, n_chunks, accumulate=False):
    for r in range(d):
        for cc in range(n_chunks):
            rows = scr.at[cc]
            val = ref[r, :, cc * 128:(cc + 1) * 128].astype(F32)
            if accumulate:
                rows[pl.ds(r, TM // d, stride=d), :] += val
            else:
                rows[pl.ds(r, TM // d, stride=d), :] = val


def _chunks(val):
    return [val[:, cc * 128:(cc + 1) * 128] for cc in range(val.shape[1] // 128)]


def _unchunk(scr, n_chunks, base=0):
    return jnp.concatenate([scr[base + cc] for cc in range(n_chunks)], axis=1)


def _head_expand():
    src = np.arange(128)[:, None]
    dst = np.arange(D_ATTN)[None, :]
    return jnp.asarray((src == (dst // HEAD_DIM) * HEAD_LANES).astype(np.float32))


def _head_reduce():
    src = np.arange(D_ATTN)[:, None]
    dst = np.arange(128)[None, :]
    return jnp.asarray((src // HEAD_DIM == dst // HEAD_LANES).astype(np.float32))


def _dot_select(a, sel):
    hi = a.astype(BF16)
    lo = (a - hi.astype(F32)).astype(BF16)
    sel = sel.astype(BF16)
    return jnp.dot(hi, sel, preferred_element_type=F32) + jnp.dot(lo, sel, preferred_element_type=F32)


def _qkvuz(x, w_in, c_tab, s1_tab, s2_tab, ln_z_g, ln_z_b, w_s, b_full, dep):
    t = x.shape[0]
    nchunk = TM // BLK

    def body(x_ref, w_ref, c_ref, s1_ref, s2_ref, g_ref, b_ref, ws_ref, bf_ref, dep_ref,
             qkv1_ref, qkv4_ref, qkv16_ref, hu_ref, hz_ref, mixed_ref, gm_ref, xb_ref, h_scr, wm_scr, p_scr):
        @pl.when(pl.program_id(0) == 0)
        def _():
            row = lax.broadcasted_iota(jnp.int32, (BLK, BLK), 0)
            col = lax.broadcasted_iota(jnp.int32, (BLK, BLK), 1)
            for g in range(N_HEADS):
                wm_scr[g] = jnp.where(col <= row, ws_ref[g], 0.0).astype(MXU)

        xb = x_ref[...].astype(MXU)
        xb_ref[...] = xb
        for j in range(N_SHARD):
            h_scr[:, j * W_IN_BLK:(j + 1) * W_IN_BLK] = jnp.dot(xb, w_ref[j], preferred_element_type=F32)
        c, s1, s2 = _tile_heads(c_ref[...]), _tile_heads(s1_ref[...]), _tile_heads(s2_ref[...])
        q = _rope_apply(h_scr[:, 0:D_ATTN], c, s1, s2) * (1.0 / math.sqrt(HEAD_DIM))
        k = _rope_apply(h_scr[:, D_ATTN:2 * D_ATTN], c, s1, s2)
        for part, val in enumerate((q, k, h_scr[:, 2 * D_ATTN:3 * D_ATTN])):
            qkv1_ref[:, part * D_ATTN:(part + 1) * D_ATTN] = val.astype(MXU)
            for cc in range(LANE_CHUNKS):
                p_scr[part * LANE_CHUNKS + cc] = val[:, cc * 128:(cc + 1) * 128]
        _to_planes(qkv4_ref, p_scr, DILATIONS[1], 3 * LANE_CHUNKS, MXU)
        _to_planes(qkv16_ref, p_scr, DILATIONS[2], 3 * LANE_CHUNKS, MXU)
        hu = h_scr[:, 3 * D_ATTN:3 * D_ATTN + D_GMLP]
        hz = h_scr[:, 3 * D_ATTN + D_GMLP:]
        hu_ref[...] = hu
        hz_ref[...] = hz
        zhat, _ = _ln_fwd(_gelu(hz))
        zn = (zhat * g_ref[...] + b_ref[...]).astype(MXU)
        for ch in range(nchunk):
            rows = slice(ch * BLK, (ch + 1) * BLK)
            for g in range(N_HEADS):
                cols = slice(g * HEAD_DIM, (g + 1) * HEAD_DIM)
                mixed_ref[rows, cols] = jnp.dot(wm_scr[g], zn[rows, cols], preferred_element_type=F32) + bf_ref[:, cols]
        gm_ref[...] = (_gelu(hu) * mixed_ref[...]).astype(MXU)

    tok = lambda w: pl.BlockSpec((TM, w), lambda i: (i, 0))
    outs = [_perm_shape(t, d, 3 * D_ATTN, MXU) for d in DILATIONS] + [jax.ShapeDtypeStruct((t, D_GMLP), F32)] * 3 + [
        jax.ShapeDtypeStruct((t, D_GMLP), MXU), jax.ShapeDtypeStruct((t, D_MODEL), MXU)]
    return pl.pallas_call(
        body, name="qkvuz", grid=(t // TM,),
        in_specs=[tok(D_MODEL), _full(w_in.shape), tok(128), tok(128), tok(128), _full(ln_z_g.shape), _full(ln_z_b.shape),
                  _full(w_s.shape), _full(b_full.shape), pl.BlockSpec(memory_space=pl.ANY)],
        out_specs=[_perm_tile(d, 3 * D_ATTN) for d in DILATIONS] + [tok(D_ATTN)] * 4 + [tok(D_MODEL)], out_shape=outs,
        scratch_shapes=[pltpu.VMEM((TM, D_IN), F32), pltpu.VMEM((N_HEADS, BLK, BLK), MXU),
                        pltpu.VMEM((3 * LANE_CHUNKS, TM, 128), F32)],
        compiler_params=_cp(dimension_semantics=("arbitrary",)),
    )(x, w_in, c_tab, s1_tab, s2_tab, ln_z_g, ln_z_b, w_s, b_full, dep)


def _band_valid(n):
    i = lax.broadcasted_iota(jnp.int32, (BLK, 2 * BLK), 0)
    j = lax.broadcasted_iota(jnp.int32, (BLK, 2 * BLK), 1)
    return (j >= i) & (j <= i + BLK) & ((j >= BLK) | (n > 0))


def _attn_fwd(qkv, d, dep):
    _, l_sub, _ = qkv.shape
    nb = l_sub // BLK

    def body(q_ref, kp_ref, kc_ref, vp_ref, vc_ref, dep_ref, o_ref, l_ref):
        valid = _band_valid(pl.program_id(1))
        kcat = jnp.concatenate([kp_ref[...], kc_ref[...]], axis=0)
        vcat = jnp.concatenate([vp_ref[...], vc_ref[...]], axis=0)
        for h in range(N_HEADS):
            cols = slice(h * HEAD_DIM, (h + 1) * HEAD_DIM)
            s = jnp.where(valid, _dot_nt(q_ref[:, cols], kcat[:, cols]), NEG_INF)
            m = jnp.max(s, axis=-1, keepdims=True)
            e = jnp.exp(s - m)
            den = jnp.sum(e, axis=-1, keepdims=True)
            o_ref[:, cols] = _dot(e, vcat[:, cols]) * (1.0 / den)
            l_ref[:, h * HEAD_LANES:(h + 1) * HEAD_LANES] = jnp.broadcast_to(m + jnp.log(den), (BLK, HEAD_LANES))

    def blk(w, col, prev=False):
        return pl.BlockSpec((None, BLK, w), lambda r, n: (r, jnp.maximum(n - 1, 0) if prev else n, col))

    return pl.pallas_call(
        body, name=f"attn_fwd_d{d}", grid=(d, nb),
        in_specs=[blk(D_ATTN, 0), blk(D_ATTN, 1, True), blk(D_ATTN, 1), blk(D_ATTN, 2, True), blk(D_ATTN, 2),
                  pl.BlockSpec(memory_space=pl.ANY)],
        out_specs=[blk(D_ATTN, 0), blk(128, 0)],
        out_shape=[jax.ShapeDtypeStruct((d, l_sub, D_ATTN), F32), jax.ShapeDtypeStruct((d, l_sub, 128), F32)],
        compiler_params=_cp(dimension_semantics=("arbitrary", "arbitrary")),
    )(qkv, qkv, qkv, qkv, qkv, dep)


def _attn_bwd(qkv, do, lse, delta, d, dep):
    _, l_sub, _ = qkv.shape
    nb = l_sub // BLK
    whole = l_sub <= 8 * BLK

    def shares(n, q_ref, kp_ref, kc_ref, vp_ref, vc_ref, do_ref, l_ref, dl_ref, dq_ref):
        valid = _band_valid(n)
        kcat = jnp.concatenate([kp_ref[...], kc_ref[...]], axis=0)
        vcat = jnp.concatenate([vp_ref[...], vc_ref[...]], axis=0)
        for h in range(N_HEADS):
            cols = slice(h * HEAD_DIM, (h + 1) * HEAD_DIM)
            stat = slice(h * HEAD_LANES, h * HEAD_LANES + 1)
            qh, doh = q_ref[:, cols], do_ref[:, cols]
            p = jnp.where(valid, jnp.exp(_dot_nt(qh, kcat[:, cols]) - l_ref[:, stat]), 0.0)
            ds = p * (_dot_nt(doh, vcat[:, cols]) - dl_ref[:, stat])
            dq_ref[:, cols] = _dot(ds, kcat[:, cols])
            yield cols, _dot_tn(ds, qh), _dot_tn(p, doh)

    def body_whole(*refs):
        dk_ref, dv_ref = refs[10:]
        n = pl.program_id(1)
        cur = pl.ds(pl.multiple_of(n * BLK, BLK), BLK)
        prev = pl.ds(pl.multiple_of(jnp.maximum(n - 1, 0) * BLK, BLK), BLK)
        for cols, dk2, dv2 in shares(n, *refs[:8], refs[9]):
            dk_ref[cur, cols] = dk2[BLK:]
            dv_ref[cur, cols] = dv2[BLK:]
            dk_ref[prev, cols] += dk2[0:BLK]
            dv_ref[prev, cols] += dv2[0:BLK]

    def body_carry(*refs):
        dk_ref, dv_ref, ck_scr, cv_scr = refs[10:]
        n = pl.program_id(1)

        @pl.when(n == 0)
        def _():
            ck_scr[...] = jnp.zeros_like(ck_scr)
            cv_scr[...] = jnp.zeros_like(cv_scr)

        @pl.when(n < nb)
        def _():
            for cols, dk2, dv2 in shares(n, *refs[:8], refs[9]):
                dk_ref[:, cols] = ck_scr[:, cols] + dk2[0:BLK]
                dv_ref[:, cols] = cv_scr[:, cols] + dv2[0:BLK]
                ck_scr[:, cols] = dk2[BLK:]
                cv_scr[:, cols] = dv2[BLK:]

        @pl.when(n == nb)
        def _():
            dk_ref[...] = ck_scr[...]
            dv_ref[...] = cv_scr[...]

    def blk(w, col, shift=0):
        return pl.BlockSpec((None, BLK, w), lambda r, n: (r, jnp.clip(n - shift, 0, nb - 1), col))

    if whole:
        dkv_spec = pl.BlockSpec((None, l_sub, D_ATTN), lambda r, n: (r, 0, 0))
        body, steps, scratch = body_whole, nb, []
    else:
        dkv_spec = blk(D_ATTN, 0, 1)
        body, steps, scratch = body_carry, nb + 1, [pltpu.VMEM((BLK, D_ATTN), F32)] * 2
    return pl.pallas_call(
        body, name=f"attn_bwd_d{d}", grid=(d, steps),
        in_specs=[blk(D_ATTN, 0), blk(D_ATTN, 1, 1), blk(D_ATTN, 1), blk(D_ATTN, 2, 1), blk(D_ATTN, 2),
                  blk(D_ATTN, 0), blk(128, 0), blk(128, 0), pl.BlockSpec(memory_space=pl.ANY)],
        out_specs=[blk(D_ATTN, 0), dkv_spec, dkv_spec],
        out_shape=[jax.ShapeDtypeStruct((d, l_sub, D_ATTN), F32)] * 3,
        scratch_shapes=scratch,
        compiler_params=_cp(dimension_semantics=("arbitrary", "arbitrary")),
    )(qkv, qkv, qkv, qkv, qkv, do, lse, delta, dep)


def _mix_ln1(os_, ls_, gm, x, w_o, ln1_g, ln1_b, dep):
    t = x.shape[0]
    expand = _head_expand()

    def body(o1, o4, o16, l1, l4, l16, gm_ref, x_ref, wo_ref, g_ref, b_ref, ex_ref, dep_ref,
             attn_ref, lse1_ref, lse4_ref, lse16_ref, cat_ref, xhat_ref, rstd_ref, x1b_ref, o_scr, l_scr):
        _from_planes(o4, o_scr, DILATIONS[1], LANE_CHUNKS)
        _from_planes(o16, o_scr.at[pl.ds(LANE_CHUNKS, LANE_CHUNKS)], DILATIONS[2], LANE_CHUNKS)
        _from_planes(l4, l_scr, DILATIONS[1], 1)
        _from_planes(l16, l_scr.at[pl.ds(1, 1)], DILATIONS[2], 1)
        la, lb, lc = l1[...], l_scr[0], l_scr[1]
        m = jnp.maximum(jnp.maximum(la, lb), lc)
        ea, eb, ec = jnp.exp(la - m), jnp.exp(lb - m), jnp.exp(lc - m)
        den = ea + eb + ec
        inv = 1.0 / den
        wide = lambda w: _dot_select(w, ex_ref[...])
        attn = (wide(ea * inv) * o1[...] + wide(eb * inv) * _unchunk(o_scr, LANE_CHUNKS)
                + wide(ec * inv) * _unchunk(o_scr, LANE_CHUNKS, LANE_CHUNKS))
        attn_ref[...] = attn
        lse = m + jnp.log(den)
        lse1_ref[...] = lse
        l_scr[2] = lse
        _to_planes(lse4_ref, l_scr.at[pl.ds(2, 1)], DILATIONS[1], 1, F32)
        _to_planes(lse16_ref, l_scr.at[pl.ds(2, 1)], DILATIONS[2], 1, F32)
        cat_ref[:, 0:D_ATTN] = attn.astype(MXU)
        cat_ref[:, D_ATTN:] = gm_ref[...]
        mix = jnp.dot(cat_ref[...], wo_ref[...], preferred_element_type=F32)
        xhat, rstd = _ln_fwd(ALPHA * x_ref[...] + mix)
        xhat_ref[...] = xhat
        rstd_ref[...] = rstd
        x1b_ref[...] = (xhat * g_ref[...] + b_ref[...]).astype(MXU)

    tok = lambda w: pl.BlockSpec((TM, w), lambda i: (i, 0))
    outs = [jax.ShapeDtypeStruct((t, D_ATTN), F32)] + [_perm_shape(t, d, 128, F32) for d in DILATIONS] + [
        jax.ShapeDtypeStruct((t, D_MODEL), MXU), jax.ShapeDtypeStruct((t, D_MODEL), F32), jax.ShapeDtypeStruct((t, 1), F32),
        jax.ShapeDtypeStruct((t, D_MODEL), MXU)]
    return pl.pallas_call(
        body, name="mix_ln1", grid=(t // TM,),
        in_specs=[_perm_tile(d, D_ATTN) for d in DILATIONS] + [_perm_tile(d, 128) for d in DILATIONS]
        + [tok(D_GMLP), tok(D_MODEL), _full(w_o.shape), _full(ln1_g.shape), _full(ln1_b.shape), _full(expand.shape),
           pl.BlockSpec(memory_space=pl.ANY)],
        out_specs=[tok(D_ATTN)] + [_perm_tile(d, 128) for d in DILATIONS] + [tok(D_MODEL), tok(D_MODEL), tok(1), tok(D_MODEL)],
        out_shape=outs,
        scratch_shapes=[pltpu.VMEM((2 * LANE_CHUNKS, TM, 128), F32), pltpu.VMEM((3, TM, 128), F32)],
        compiler_params=_cp(dimension_semantics=("arbitrary",)),
    )(*os_, *ls_, gm, x, w_o, ln1_g, ln1_b, expand, dep)


def _conv_fwd(a_ext, w_ref, b_ref, rows):
    back = [pltpu.roll(a_ext, s, 0)[HALO:HALO + rows] for s in (1, 2)]
    return b_ref[...] + w_ref[2:3, :] * a_ext[HALO:HALO + rows] + w_ref[1:2, :] * back[0] + w_ref[0:1, :] * back[1]


def _ffn_in(x1b, w_a, w_b, conv_w, conv_b):
    t = x1b.shape[0]
    hb = TM // HALO

    def body(x_ref, xh_ref, wa_ref, wb_ref, cw_ref, cb_ref, apre_ref, act_ref, gate_ref, f_ref):
        i = pl.program_id(1)
        a_pre = _dot_nt(x_ref[...], wa_ref[...])
        a_halo = jnp.where(i > 0, _dot_nt(xh_ref[...], wa_ref[...]), 0.0)
        a = _conv_fwd(jnp.concatenate([a_halo, a_pre], axis=0), cw_ref, cb_ref, TM)
        b = _dot_nt(x_ref[...], wb_ref[...])
        cdf = 0.5 * (1.0 + lax.erf(a * (1.0 / math.sqrt(2.0))))
        pdf = jnp.exp(-0.5 * a * a) * (1.0 / math.sqrt(2.0 * math.pi))
        act = a * cdf
        apre_ref[...] = a_pre
        act_ref[...] = act
        gate_ref[...] = b * (cdf + a * pdf)
        f_ref[...] = (act * b).astype(MXU)

    blk = lambda r, c: pl.BlockSpec((None, r, c), lambda j, i: (j, 0, 0))
    tokj = pl.BlockSpec((None, TM, FF_BLK), lambda j, i: (j, i, 0))
    outs = [jax.ShapeDtypeStruct((N_SHARD, t, FF_BLK), F32)] * 3 + [jax.ShapeDtypeStruct((N_SHARD, t, FF_BLK), MXU)]
    return pl.pallas_call(
        body, name="ffn_in", grid=(N_SHARD, t // TM),
        in_specs=[pl.BlockSpec((TM, D_MODEL), lambda j, i: (i, 0)),
                  pl.BlockSpec((HALO, D_MODEL), lambda j, i: (jnp.maximum(i * hb - 1, 0), 0)),
                  blk(FF_BLK, D_MODEL), blk(FF_BLK, D_MODEL), blk(3, FF_BLK), blk(1, FF_BLK)],
        out_specs=[tokj, tokj, tokj, tokj], out_shape=outs,
        compiler_params=_cp(dimension_semantics=("arbitrary", "arbitrary")),
    )(x1b, x1b, w_a, w_b, conv_w, conv_b)


def _ffn_out_ln2(f, w_down, xhat1, ln1_g, ln1_b):
    t = xhat1.shape[0]

    def body(f_ref, wd_ref, xh_ref, g1_ref, b1_ref, xhat_ref, rstd_ref):
        half = TM // ROW_GROUPS
        for r0 in range(0, TM, half):
            rows = pl.ds(r0, half)
            ff = jnp.dot(f_ref[0, rows, :], wd_ref[0], preferred_element_type=F32)
            for j in range(1, N_SHARD):
                ff = ff + jnp.dot(f_ref[j, rows, :], wd_ref[j], preferred_element_type=F32)
            x1 = xh_ref[rows, :] * g1_ref[...] + b1_ref[...]
            xhat, rstd = _ln_fwd(ALPHA * x1 + ff)
            xhat_ref[rows, :] = xhat
            rstd_ref[rows, :] = rstd

    tok = lambda w: pl.BlockSpec((TM, w), lambda i: (i, 0))
    vec = _full((1, D_MODEL))
    outs = [jax.ShapeDtypeStruct((t, D_MODEL), F32), jax.ShapeDtypeStruct((t, 1), F32)]
    return pl.pallas_call(
        body, name="ffn_out_ln2", grid=(t // TM,),
        in_specs=[pl.BlockSpec((N_SHARD, TM, FF_BLK), lambda i: (0, i, 0)), _full(w_down.shape), tok(D_MODEL), vec, vec],
        out_specs=[tok(D_MODEL), tok(1)], out_shape=outs,
        compiler_params=_cp(dimension_semantics=("arbitrary",)),
    )(f, w_down, xhat1, ln1_g, ln1_b)


STAT_ROWS = 8


def _ple_loss_bwd(xhat2, rstd2, p, target, ln2_g, ln2_b, w_g, b_g, w_p, ln3_g, ln3_b):
    t = xhat2.shape[0]

    def body(xh2_ref, rs2_ref, p_ref, t_ref, g2_ref, b2_ref, wg_ref, bg_ref, wp_ref, g3_ref, b3_ref,
             dr2_ref, dr2b_ref, stat_ref, dwg_ref, dwp_ref, pp_scr, dwp_scr):
        @pl.when(pl.program_id(0) == 0)
        def _():
            stat_ref[...] = jnp.zeros_like(stat_ref)
            dwg_ref[...] = jnp.zeros_like(dwg_ref)
            dwp_scr[...] = jnp.zeros_like(dwp_scr)

        xhat2 = xh2_ref[...]
        x2 = xhat2 * g2_ref[...] + b2_ref[...]
        x2b = x2.astype(MXU)
        gate = jax.nn.sigmoid(jnp.dot(x2b, wg_ref[...], preferred_element_type=F32) + bg_ref[...])
        pb = p_ref[...].astype(MXU)
        for j in range(N_SHARD):
            pp_scr[:, j * ROW_BLK:(j + 1) * ROW_BLK] = jnp.dot(pb, wp_ref[j], preferred_element_type=F32)
        pp = pp_scr[...]
        xhat3, rstd3 = _ln_fwd(ALPHA * x2 + gate * pp)
        err = xhat3 * g3_ref[...] + b3_ref[...] - t_ref[...]
        dy = err * (1.0 / D_MODEL)
        dr3 = _ln_bwd(dy, xhat3, rstd3, g3_ref[...])
        dgp = dr3 * pp * gate * (1.0 - gate)
        dgp_b = dgp.astype(MXU)
        dwg_ref[...] += _dot_tn(x2b, dgp_b)
        dwp_scr[...] += _dot_tn(pb, dr3 * gate)
        dx2 = ALPHA * dr3 + _dot_nt(dgp_b, wg_ref[...])
        dr2 = _ln_bwd(dx2, xhat2, rs2_ref[...], g2_ref[...])
        dr2_ref[...] = dr2
        dr2b_ref[...] = dr2.astype(MXU)
        stat_ref[0:1, :] += _colsum(dy * xhat3)
        stat_ref[1:2, :] += _colsum(dy)
        stat_ref[2:3, :] += _colsum(dgp)
        stat_ref[3:4, :] += _colsum(dx2 * xhat2)
        stat_ref[4:5, :] += _colsum(dx2)
        stat_ref[5:6, :] += _colsum(err * err)

        @pl.when(pl.program_id(0) == t // TM - 1)
        def _():
            for j in range(N_SHARD):
                dwp_ref[j] = dwp_scr[:, j * ROW_BLK:(j + 1) * ROW_BLK]

    tok = lambda w: pl.BlockSpec((TM, w), lambda i: (i, 0))
    vec = _full((1, D_MODEL))
    outs = [jax.ShapeDtypeStruct((t, D_MODEL), F32), jax.ShapeDtypeStruct((t, D_MODEL), MXU),
            jax.ShapeDtypeStruct((STAT_ROWS, D_MODEL), F32), jax.ShapeDtypeStruct((D_MODEL, D_MODEL), F32),
            jax.ShapeDtypeStruct((N_SHARD, D_PLE, ROW_BLK), F32)]
    return pl.pallas_call(
        body, name="ple_loss_bwd", grid=(t // TM,),
        in_specs=[tok(D_MODEL), tok(1), tok(D_PLE), tok(D_MODEL), vec, vec, _full(w_g.shape), vec, _full(w_p.shape), vec, vec],
        out_specs=[tok(D_MODEL), tok(D_MODEL), _full((STAT_ROWS, D_MODEL)), _full((D_MODEL, D_MODEL)),
                   _full((N_SHARD, D_PLE, ROW_BLK))], out_shape=outs,
        scratch_shapes=[pltpu.VMEM((TM, D_MODEL), F32), pltpu.VMEM((D_PLE, D_MODEL), F32)],
        compiler_params=_cp(dimension_semantics=("arbitrary",)),
    )(xhat2, rstd2, p, target, ln2_g, ln2_b, w_g, b_g, w_p, ln3_g, ln3_b)


def _ffn_bwd(dr2, dr2b, a_pre, act, gate, w_down, w_a, w_b, conv_w, xhat1, rstd1, ln1_g, cat):
    t = dr2.shape[0]
    nt = t // TM
    hb = TM // HALO
    last_h = t // HALO - 1
    halo2 = 2 * HALO

    def body(dr_ref, drb_ref, drbn_ref, ap_ref, act_ref, gate_ref, gaten_ref, wd_ref, wa_ref, wb_ref, cw_ref,
             xh_ref, rs_ref, g1_ref, cat_ref, dap_ref, dbb_ref, dr1_ref, cstat_ref, lstat_ref, dwo_ref, acc_scr):
        i, j = pl.program_id(0), pl.program_id(1)

        @pl.when((i == 0) & (j == 0))
        def _():
            cstat_ref[...] = jnp.zeros_like(cstat_ref)
            lstat_ref[...] = jnp.zeros_like(lstat_ref)
            dwo_ref[...] = jnp.zeros_like(dwo_ref)

        half = TM // ROW_GROUPS
        parts = []
        for r0 in range(0, TM, half):
            rows = pl.ds(r0, half)
            last = r0 + half == TM

            def ext(ref, nxt):
                return jnp.concatenate([ref[rows], nxt[...]], axis=0) if last else ref[r0:r0 + half + HALO]

            drb = jnp.concatenate([drb_ref[rows, :], drbn_ref[...]], axis=0) if last else drb_ref[r0:r0 + half + halo2, :]
            df = _dot_nt(drb, wd_ref[...])[0:half + HALO]
            da = df * ext(gate_ref, gaten_ref)
            if last:
                da = jnp.concatenate([da[0:half], jnp.where(i < nt - 1, da[half:], 0.0)], axis=0)
            ahead = [da[0:half]] + [pltpu.roll(da, half + HALO - s, 0)[0:half] for s in (1, 2)]
            da_pre = cw_ref[2:3, :] * ahead[0] + cw_ref[1:2, :] * ahead[1] + cw_ref[0:1, :] * ahead[2]
            dbb = df[0:half] * act_ref[rows, :]
            dap_ref[rows, :] = da_pre.astype(MXU)
            dbb_ref[rows, :] = dbb.astype(MXU)
            for kk in range(3):
                cstat_ref[j, kk:kk + 1, :] += _colsum(ahead[2 - kk] * ap_ref[rows, :])
            cstat_ref[j, 3:4, :] += _colsum(ahead[0])
            parts.append(_dot(da_pre, wa_ref[...]) + _dot(dbb, wb_ref[...]))
        part = jnp.concatenate(parts, axis=0)

        @pl.when(j == 0)
        def _():
            acc_scr[...] = ALPHA * dr_ref[...] + part

        @pl.when(j > 0)
        def _():
            acc_scr[...] += part

        @pl.when(j == N_SHARD - 1)
        def _():
            dx1 = acc_scr[...]
            xhat1 = xh_ref[...]
            lstat_ref[0:1, :] += _colsum(dx1 * xhat1)
            lstat_ref[1:2, :] += _colsum(dx1)
            dr1 = _ln_bwd(dx1, xhat1, rs_ref[...], g1_ref[...])
            dr1_ref[...] = dr1
            dwo_ref[...] += _dot_tn(cat_ref[...], dr1)

    tok = lambda w: pl.BlockSpec((TM, w), lambda i, j: (i, 0))
    tokj = pl.BlockSpec((None, TM, FF_BLK), lambda i, j: (j, i, 0))
    nextj = pl.BlockSpec((None, HALO, FF_BLK), lambda i, j: (j, jnp.minimum((i + 1) * hb, last_h), 0))
    blk = lambda r, c: pl.BlockSpec((None, r, c), lambda i, j: (j, 0, 0))
    outs = [jax.ShapeDtypeStruct((N_SHARD, t, FF_BLK), MXU)] * 2 + [
        jax.ShapeDtypeStruct((t, D_MODEL), F32), jax.ShapeDtypeStruct((N_SHARD, STAT_ROWS, FF_BLK), F32),
        jax.ShapeDtypeStruct((STAT_ROWS, D_MODEL), F32), jax.ShapeDtypeStruct((D_MODEL, D_MODEL), F32)]
    return pl.pallas_call(
        body, name="ffn_bwd", grid=(nt, N_SHARD),
        in_specs=[tok(D_MODEL), tok(D_MODEL),
                  pl.BlockSpec((halo2, D_MODEL), lambda i, j: (jnp.minimum((i + 1) * (hb // 2), last_h // 2), 0)),
                  tokj, tokj, tokj, nextj, blk(FF_BLK, D_MODEL), blk(FF_BLK, D_MODEL), blk(FF_BLK, D_MODEL),
                  blk(3, FF_BLK), tok(D_MODEL), tok(1), _full((1, D_MODEL)), tok(D_MODEL)],
        out_specs=[tokj, tokj, tok(D_MODEL), _full((N_SHARD, STAT_ROWS, FF_BLK)), _full((STAT_ROWS, D_MODEL)),
                   _full((D_MODEL, D_MODEL))], out_shape=outs,
        scratch_shapes=[pltpu.VMEM((TM, D_MODEL), F32)],
        compiler_params=_cp(dimension_semantics=("arbitrary", "arbitrary")),
    )(dr2, dr2b, dr2b, a_pre, act, gate, gate, w_down, w_a, w_b, conv_w, xhat1, rstd1, ln1_g, cat)


def _mix_bwd(dr1, w_o, hu, hz, mixed, attn, ln_z_g, ln_z_b, w_s, dep):
    t = dr1.shape[0]
    nchunk = TM // BLK

    def body(dr_ref, wo_ref, hu_ref, hz_ref, mx_ref, attn_ref, g_ref, b_ref, ws_ref, grp_ref, red_ref, dep_ref,
             do1_ref, do4_ref, do16_ref, dl1_ref, dl4_ref, dl16_ref, duz_ref, dws_ref, dbs_ref, zstat_ref,
             wm_scr, dzn_scr, dbsum_scr, do_scr, dl_scr):
        @pl.when(pl.program_id(0) == 0)
        def _():
            row = lax.broadcasted_iota(jnp.int32, (BLK, BLK), 0)
            col = lax.broadcasted_iota(jnp.int32, (BLK, BLK), 1)
            for g in range(N_HEADS):
                wm_scr[g] = jnp.where(col <= row, ws_ref[g], 0.0).astype(MXU)
            dws_ref[...] = jnp.zeros_like(dws_ref)
            dbsum_scr[...] = jnp.zeros_like(dbsum_scr)
            zstat_ref[...] = jnp.zeros_like(zstat_ref)

        dcat = _dot_nt(dr_ref[...], wo_ref[...])
        dattn = dcat[:, 0:D_ATTN]
        do1_ref[...] = dattn.astype(MXU)
        for cc, val in enumerate(_chunks(dattn)):
            do_scr[cc] = val
        _to_planes(do4_ref, do_scr, DILATIONS[1], LANE_CHUNKS, MXU)
        _to_planes(do16_ref, do_scr, DILATIONS[2], LANE_CHUNKS, MXU)
        delta = _dot_select(dattn * attn_ref[...], red_ref[...])
        dl1_ref[...] = delta
        dl_scr[0] = delta
        _to_planes(dl4_ref, dl_scr, DILATIONS[1], 1, F32)
        _to_planes(dl16_ref, dl_scr, DILATIONS[2], 1, F32)
        dgm = dcat[:, D_ATTN:]
        hu, hz = hu_ref[...], hz_ref[...]
        u = _gelu(hu)
        duz_ref[:, 0:D_GMLP] = (dgm * mx_ref[...] * _gelu_grad(hu)).astype(MXU)
        dmixed = dgm * u
        dmb = dmixed.astype(MXU)
        zhat, rstd = _ln_fwd(_gelu(hz))
        znb = (zhat * g_ref[...] + b_ref[...]).astype(MXU)
        dbs_acc = jnp.zeros((BLK, D_GMLP), F32)
        for ch in range(nchunk):
            rows = slice(ch * BLK, (ch + 1) * BLK)
            dbs_acc = dbs_acc + dmixed[rows]
            for g in range(N_HEADS):
                cols = slice(g * HEAD_DIM, (g + 1) * HEAD_DIM)
                dzn_scr[rows, cols] = _dot_tn(wm_scr[g], dmb[rows, cols])
                dws_ref[g] += _dot_nt(dmb[rows, cols], znb[rows, cols])
        dbsum_scr[...] += dbs_acc
        dzn = dzn_scr[...]
        zstat_ref[0:1, :] += _colsum(dzn * zhat)
        zstat_ref[1:2, :] += _colsum(dzn)
        duz_ref[:, D_GMLP:] = (_ln_bwd(dzn, zhat, rstd, g_ref[...]) * _gelu_grad(hz)).astype(MXU)

        @pl.when(pl.program_id(0) == nt - 1)
        def _():
            row = lax.broadcasted_iota(jnp.int32, (BLK, BLK), 0)
            col = lax.broadcasted_iota(jnp.int32, (BLK, BLK), 1)
            for g in range(N_HEADS):
                dws_ref[g] = jnp.where(col <= row, dws_ref[g], 0.0)
            dbs_ref[...] = lax.dot_general(grp_ref[...], dbsum_scr[...], (((1,), (1,)), ((), ())),
                                           precision=lax.Precision.HIGHEST, preferred_element_type=F32)

    nt = t // TM
    tok = lambda w: pl.BlockSpec((TM, w), lambda i: (i, 0))
    grp = jnp.asarray((np.arange(D_GMLP)[None, :] // HEAD_DIM == np.arange(N_HEADS)[:, None]).astype(np.float32))
    red = _head_reduce()
    outs = [_perm_shape(t, d, D_ATTN, MXU) for d in DILATIONS] + [_perm_shape(t, d, 128, F32) for d in DILATIONS] + [
        jax.ShapeDtypeStruct((t, 2 * D_GMLP), MXU),
        jax.ShapeDtypeStruct((N_HEADS, BLK, BLK), F32), jax.ShapeDtypeStruct((N_HEADS, BLK), F32),
        jax.ShapeDtypeStruct((STAT_ROWS, D_GMLP), F32)]
    return pl.pallas_call(
        body, name="mix_bwd", grid=(t // TM,),
        in_specs=[tok(D_MODEL), _full(w_o.shape), tok(D_GMLP), tok(D_GMLP), tok(D_GMLP), tok(D_ATTN), _full(ln_z_g.shape),
                  _full(ln_z_b.shape), _full(w_s.shape), _full(grp.shape), _full(red.shape), pl.BlockSpec(memory_space=pl.ANY)],
        out_specs=[_perm_tile(d, D_ATTN) for d in DILATIONS] + [_perm_tile(d, 128) for d in DILATIONS]
        + [tok(2 * D_GMLP), _full((N_HEADS, BLK, BLK)), _full((N_HEADS, BLK)), _full((STAT_ROWS, D_GMLP))],
        out_shape=outs,
        scratch_shapes=[pltpu.VMEM((N_HEADS, BLK, BLK), MXU), pltpu.VMEM((TM, D_GMLP), F32), pltpu.VMEM((BLK, D_GMLP), F32),
                        pltpu.VMEM((LANE_CHUNKS, TM, 128), F32), pltpu.VMEM((1, TM, 128), F32)],
        compiler_params=_cp(dimension_semantics=("arbitrary",)),
    )(dr1, w_o, hu, hz, mixed, attn, ln_z_g, ln_z_b, w_s, grp, red, dep)


def _dx_in(dqs, dks, dvs, duz, dr1, w_in, c_tab, s1_tab, s2_tab):
    t = dr1.shape[0]

    def body(dq1, dq4, dq16, dk1, dk4, dk16, dv1, dv4, dv16, duz_ref, dr_ref, w_ref, c_ref, s1_ref, s2_ref,
             dh_ref, dx_ref, acc_scr):
        sums = []
        for part, (g1, g4, g16) in enumerate(((dq1, dq4, dq16), (dk1, dk4, dk16), (dv1, dv4, dv16))):
            acc = acc_scr.at[pl.ds(part * LANE_CHUNKS, LANE_CHUNKS)]
            for cc in range(LANE_CHUNKS):
                acc[cc] = g1[:, cc * 128:(cc + 1) * 128]
            _from_planes(g4, acc, DILATIONS[1], LANE_CHUNKS, accumulate=True)
            _from_planes(g16, acc, DILATIONS[2], LANE_CHUNKS, accumulate=True)
            sums.append(_unchunk(acc_scr, LANE_CHUNKS, part * LANE_CHUNKS))
        c, s1, s2 = _tile_heads(c_ref[...]), _tile_heads(s1_ref[...]), _tile_heads(s2_ref[...])
        dh_ref[:, 0:D_ATTN] = _rope_apply_t(sums[0] * (1.0 / math.sqrt(HEAD_DIM)), c, s1, s2).astype(MXU)
        dh_ref[:, D_ATTN:2 * D_ATTN] = _rope_apply_t(sums[1], c, s1, s2).astype(MXU)
        dh_ref[:, 2 * D_ATTN:3 * D_ATTN] = sums[2].astype(MXU)
        dh_ref[:, 3 * D_ATTN:] = duz_ref[...]
        dx = ALPHA * dr_ref[...]
        for j in range(N_SHARD):
            dx = dx + _dot_nt(dh_ref[:, j * W_IN_BLK:(j + 1) * W_IN_BLK], w_ref[j])
        dx_ref[...] = dx

    tok = lambda w: pl.BlockSpec((TM, w), lambda i: (i, 0))
    outs = [jax.ShapeDtypeStruct((t, D_IN), MXU), jax.ShapeDtypeStruct((t, D_MODEL), F32)]
    return pl.pallas_call(
        body, name="dx_in", grid=(t // TM,),
        in_specs=[_perm_tile(d, D_ATTN) for d in DILATIONS] * 3
        + [tok(2 * D_GMLP), tok(D_MODEL), _full(w_in.shape), tok(128), tok(128), tok(128)],
        out_specs=[tok(D_IN), tok(D_MODEL)], out_shape=outs,
        scratch_shapes=[pltpu.VMEM((3 * LANE_CHUNKS, TM, 128), F32)],
        compiler_params=_cp(dimension_semantics=("arbitrary",)),
    )(*dqs, *dks, *dvs, duz, dr1, w_in, c_tab, s1_tab, s2_tab)


def _wgrad(name, x, dy, x_spec, dy_spec, out_spec, out_shape, grid, dep=None):
    deps = [] if dep is None else [dep]

    def body(x_ref, dy_ref, *rest):
        rest[-1][...] = _dot_tn(x_ref[...], dy_ref[...])

    return pl.pallas_call(
        body, name=name, grid=grid, in_specs=[x_spec, dy_spec] + [pl.BlockSpec(memory_space=pl.ANY)] * len(deps),
        out_specs=out_spec, out_shape=jax.ShapeDtypeStruct(out_shape, F32),
        compiler_params=_cp(dimension_semantics=("arbitrary",) * len(grid)),
    )(x, dy, *deps)


def _wgrad_pair(name, xa, xb, dy, x_spec, dy_spec, out_spec, out_shape, grid):
    def body(xa_ref, xb_ref, dy_ref, oa_ref, ob_ref):
        dy = dy_ref[...]
        oa_ref[...] = _dot_tn(xa_ref[...], dy)
        ob_ref[...] = _dot_tn(xb_ref[...], dy)

    return pl.pallas_call(
        body, name=name, grid=grid, in_specs=[x_spec, x_spec, dy_spec], out_specs=[out_spec, out_spec],
        out_shape=[jax.ShapeDtypeStruct(out_shape, F32)] * 2,
        compiler_params=_cp(dimension_semantics=("arbitrary",) * len(grid)),
    )(xa, xb, dy)


def _local_step(x, p, rope, target, w_in, start_dep, late_landed, late_weights, early_grads, early_grads_sent,
                early_grads_landed,
                ln_z_g, ln_z_b, w_s, b_s, ln1_g, ln1_b, conv_b, ln2_g, ln2_b, b_g, ln3_g, ln3_b):
    t = x.shape[0]
    half = TM
    c_tab, s1_tab, s2_tab = rope
    b_full = jnp.repeat(jnp.transpose(b_s[0]), HEAD_DIM, axis=1)
    conv_b4 = conv_b.reshape(N_SHARD, 1, FF_BLK)
    *qkvs, hu, hz, mixed, gm, xb = _qkvuz(x, w_in, c_tab, s1_tab, s2_tab, ln_z_g, ln_z_b, w_s[0], b_full, start_dep)
    branches = [_attn_fwd(qkv, d, start_dep) for qkv, d in zip(qkvs[:2], DILATIONS[:2])]
    dep = late_landed(branches[-1][1])
    branches.append(_attn_fwd(qkvs[2], DILATIONS[2], dep))
    w_o, w_a, w_b, conv_w, w_down, w_g, w_p = late_weights(branches[-1][1])
    attn, *lses, cat, xhat1, rstd1, x1b = _mix_ln1(
        [o for o, _ in branches], [l for _, l in branches], gm, x, w_o, ln1_g, ln1_b, dep)
    a_pre, act, gate, f = _ffn_in(x1b, w_a, w_b, conv_w, conv_b4)
    xhat2, rstd2 = _ffn_out_ln2(f, w_down, xhat1, ln1_g, ln1_b)
    dr2, dr2b, stat3, g_w_g, g_w_p = _ple_loss_bwd(xhat2, rstd2, p, target, ln2_g, ln2_b, w_g, b_g, w_p, ln3_g, ln3_b)
    da_pre, dbb, dr1, cstat, stat1, g_w_o = _ffn_bwd(dr2, dr2b, a_pre, act, gate, w_down, w_a, w_b, conv_w, xhat1, rstd1,
                                                    ln1_g, cat)

    full_t = lambda w, im: pl.BlockSpec((t, w), im)
    ffj = pl.BlockSpec((None, t, FF_BLK), lambda j, kk: (j, 0, 0))
    early = dict(
        w_ple_gate=g_w_g, w_ple_in=g_w_p,
        w_ff_down=_wgrad("dw_down", f, dr2b, ffj, full_t(half, lambda j, n: (0, n)),
                         pl.BlockSpec((None, FF_BLK, half), lambda j, n: (j, 0, n)), (N_SHARD, FF_BLK, D_MODEL), (N_SHARD, 2)),
        **dict(zip(("w_ff_a", "w_ff_b"), _wgrad_pair(
            "dw_ab", da_pre, dbb, x1b, ffj, full_t(half, lambda j, n: (0, n)),
            pl.BlockSpec((None, FF_BLK, half), lambda j, n: (j, 0, n)), (N_SHARD, FF_BLK, D_MODEL), (N_SHARD, 2)))),
        w_o=g_w_o)
    dep = early_grads(early)

    do1, do4, do16, dl1, dl4, dl16, duz, dws, dbs, zstat = _mix_bwd(
        dr1, w_o, hu, hz, mixed, attn, ln_z_g, ln_z_b, w_s[0], dep)
    dep = early_grads_sent(duz, (stat3, stat1, zstat, cstat, dws, dbs))
    dqkv = [_attn_bwd(qkv, do, lse, dl, d, dep)
            for qkv, do, lse, dl, d in zip(qkvs, (do1, do4, do16), lses, (dl1, dl4, dl16), DILATIONS)]
    dh, grad_x = _dx_in([g[0] for g in dqkv], [g[1] for g in dqkv], [g[2] for g in dqkv], duz, dr1, w_in,
                        c_tab, s1_tab, s2_tab)
    dep = early_grads_landed(grad_x)
    g_w_in = _wgrad("dw_in", xb, dh, full_t(half, lambda j, kk: (0, kk)), full_t(W_IN_BLK, lambda j, kk: (0, j)),
                    pl.BlockSpec((None, half, W_IN_BLK), lambda j, kk: (j, kk, 0)), (N_SHARD, D_MODEL, W_IN_BLK), (N_SHARD, 2),
                    dep)
    return grad_x, g_w_in


def _tile_rows(rows, mult, steps):
    if rows % mult:
        return rows
    return next(rows // k for k in range(steps, rows + 1) if rows % k == 0 and (rows // k) % mult == 0)


def _grid_spec(grid, in_specs, out_specs):
    return pltpu.PrefetchScalarGridSpec(num_scalar_prefetch=1, grid=grid, in_specs=in_specs, out_specs=out_specs)


def _on_own_steps(i, count, steps, work):
    if count == steps:
        work()
    else:
        pl.when(i < count)(work)


def _place_shards(name, ws, dtypes, place, dep):
    n = len(ws)
    tiles = [_tile_rows(w.shape[0], 16, 2) for w in ws]
    counts = [w.shape[0] // t for w, t in zip(ws, tiles)]
    steps = max(counts)

    def body(s_ref, *refs):
        i = pl.program_id(0)
        for a in range(n):
            def work(a=a):
                refs[n + 1 + a][...] = refs[a][...].astype(dtypes[a])
            _on_own_steps(i, counts[a], steps, work)

    def tile(a, lead):
        last = counts[a] - 1
        if lead:
            return pl.BlockSpec((None, tiles[a], ws[a].shape[1]), lambda i, s: (s[0], jnp.minimum(i, last), 0))
        return pl.BlockSpec((tiles[a], ws[a].shape[1]), lambda i, s: (jnp.minimum(i, last), 0))

    return pl.pallas_call(
        body, name=name,
        grid_spec=_grid_spec((steps,), [tile(a, False) for a in range(n)] + [pl.BlockSpec(memory_space=pl.ANY)],
                             [tile(a, True) for a in range(n)]),
        out_shape=[jax.ShapeDtypeStruct((N_SHARD, *w.shape), dt) for w, dt in zip(ws, dtypes)],
        compiler_params=_cp())(place, *ws, dep)


def _pair_sums(name, mines, gots, place):
    n = len(mines)
    tiles = [_tile_rows(g.shape[1], 16, 1) for g in gots]
    per_blk = [g.shape[1] // t for g, t in zip(gots, tiles)]
    counts = [N_SHARD * nh for nh in per_blk]
    steps = max(counts)

    def body(s_ref, *refs):
        i = pl.program_id(0)
        for a in range(n):
            def work(a=a):
                refs[2 * n + a][...] = (refs[a][...] + refs[n + a][...]).astype(BF16)
            _on_own_steps(i, counts[a], steps, work)

    def tile(a, mine):
        nh, last = per_blk[a], counts[a] - 1

        def index(i, s):
            g = jnp.minimum(i, last)
            return (g // nh, (s[1] * nh if mine else 0) + g % nh, 0)

        return pl.BlockSpec((None, tiles[a], gots[a].shape[2]), index)

    return pl.pallas_call(
        body, name=name,
        grid_spec=_grid_spec((steps,), [tile(a, True) for a in range(n)] + [tile(a, False) for a in range(n)],
                             [tile(a, False) for a in range(n)]),
        out_shape=[jax.ShapeDtypeStruct(g.shape, BF16) for g in gots], compiler_params=_cp())(place, *mines, *gots)


def _chip_sums(name, owns, landeds, place, dep):
    n = len(owns)
    tiles = [_tile_rows(o.shape[1], 16, 4) for o in owns]
    counts = [o.shape[1] // t for o, t in zip(owns, tiles)]
    steps = max(counts)

    def body(s_ref, *refs):
        i = pl.program_id(0)
        for a in range(n):
            def work(a=a):
                own, l1, l2, l3 = (refs[4 * a + k][...].astype(F32) for k in range(4))
                refs[4 * n + 1 + a][...] = ((own + l1) + l2) + l3
            _on_own_steps(i, counts[a], steps, work)

    def slot(a, d):
        last = counts[a] - 1
        return pl.BlockSpec((None, tiles[a], owns[a].shape[2]), lambda i, s: ((s[0] + d) % N_SHARD, jnp.minimum(i, last), 0))

    def out(a):
        nh, last = counts[a], counts[a] - 1
        return pl.BlockSpec((tiles[a], owns[a].shape[2]), lambda i, s: (s[1] * nh + jnp.minimum(i, last), 0))

    operands = [x for o, l in zip(owns, landeds) for x in (o, l, l, l)]
    return pl.pallas_call(
        body, name=name,
        grid_spec=_grid_spec((steps,), [slot(a, d) for a in range(n) for d in range(4)] + [pl.BlockSpec(memory_space=pl.ANY)],
                             [out(a) for a in range(n)]),
        out_shape=[jax.ShapeDtypeStruct((2 * o.shape[1], o.shape[2]), F32) for o in owns],
        compiler_params=_cp())(place, *operands, dep)


def _adamw_math(w, g, m, v):
    m = ADAM_B1 * m + (1.0 - ADAM_B1) * g
    v = ADAM_B2 * v + (1.0 - ADAM_B2) * (g * g)
    m_hat = m / (1.0 - ADAM_B1 ** ADAM_STEP)
    v_hat = v / (1.0 - ADAM_B2 ** ADAM_STEP)
    delta = -ADAM_LR * (m_hat / (jnp.sqrt(v_hat) + ADAM_EPS) + ADAM_WD * w)
    return delta, m, v


def _adamw_shards(name, ws, gs, ms, vs):
    n = len(ws)
    tiles = [_tile_rows(w.shape[1], 8, 8 if n > 1 else 2) for w in ws]
    counts = [w.shape[1] // t for w, t in zip(ws, tiles)]
    steps = max(counts)

    def body(*refs):
        i = pl.program_id(0)
        for a in range(n):
            def work(a=a):
                w_ref, g_ref, m_ref, v_ref = refs[4 * a:4 * a + 4]
                d_ref, nm_ref, nv_ref = refs[4 * n + 3 * a:4 * n + 3 * a + 3]
                d_ref[...], nm_ref[...], nv_ref[...] = _adamw_math(w_ref[...], g_ref[...], m_ref[...], v_ref[...])
            _on_own_steps(i, counts[a], steps, work)

    def tile(a, lead):
        last, c = counts[a] - 1, ws[a].shape[2]
        if lead:
            return pl.BlockSpec((None, tiles[a], c), lambda i: (0, jnp.minimum(i, last), 0))
        return pl.BlockSpec((tiles[a], c), lambda i: (jnp.minimum(i, last), 0))

    res = pl.pallas_call(
        body, name=name, grid=(steps,),
        in_specs=[tile(a, lead) for a in range(n) for lead in (True, False, True, True)],
        out_specs=[tile(a, True) for a in range(n) for _ in range(3)],
        out_shape=[jax.ShapeDtypeStruct(w.shape, F32) for w in ws for _ in range(3)],
        compiler_params=_cp())(*[x for quad in zip(ws, gs, ms, vs) for x in quad])
    return [tuple(res[3 * a:3 * a + 3]) for a in range(n)]


MESH = pl.DeviceIdType.MESH
ANY = pl.BlockSpec(memory_space=pl.ANY)


def _place():
    x, y, c = lax.axis_index("x"), lax.axis_index("y"), lax.axis_index("c")
    chips = [(1 - x, y), (x, 1 - y), (1 - x, 1 - y)]
    return x, y, c, 2 * x + y, chips


def _remote(src, dst, send_sem, recv_sem, dev):
    return pltpu.make_async_remote_copy(src_ref=src, dst_ref=dst, send_sem=send_sem, recv_sem=recv_sem,
                                        device_id=dev, device_id_type=MESH)


def _half(ref, hc, rows):
    return ref.at[pl.ds(hc * (rows // 2), rows // 2)]


def _sibling_join(blocks, tag):
    n = len(blocks)

    def body(*refs):
        outs = refs[n:2 * n]
        send, recv = refs[2 * n:]
        x, y, c, _, _ = _place()
        cps = []
        for a in range(n):
            h = blocks[a].shape[0] // 2
            mine = outs[a].at[pl.ds(c * h, h)]
            cp = _remote(mine, mine, send.at[a], recv.at[a], (x, y, 1 - c))
            cp.start()
            cps.append(cp)
        for a, cp in enumerate(cps):
            h = blocks[a].shape[0] // 2
            theirs = outs[a].at[pl.ds((1 - c) * h, h)]
            _remote(theirs, theirs, send.at[a], recv.at[a], (x, y, 1 - c)).wait_recv()
            cp.wait_send()

    sem = pltpu.SemaphoreType.DMA
    return pl.pallas_call(body, name=f"rs_sibling_join_{tag}", in_specs=[ANY] * n, out_specs=[ANY] * n,
                          out_shape=[jax.ShapeDtypeStruct(b_.shape, b_.dtype) for b_ in blocks],
                          input_output_aliases={a: a for a in range(n)},
                          scratch_shapes=[sem((n,)), sem((n,))])(*blocks)


def _join_start(blocks, after, tag):
    n = len(blocks)

    def body(*refs):
        ins = refs[:n]
        send, recv = refs[n + 1], refs[n + 2]
        token = refs[2 * n + 3]
        x, y, c, _, _ = _place()
        for a in range(n):
            h = blocks[a].shape[0] // 2
            mine = ins[a].at[pl.ds(c * h, h)]
            _remote(mine, mine, send.at[a], recv.at[a], (x, y, 1 - c)).start()
        token[...] = jnp.zeros_like(token)

    sems = pltpu.SemaphoreType.DMA((n,))
    res = pl.pallas_call(
        body, name=f"join_start_{tag}", in_specs=[HBM] * n + [ANY],
        out_specs=[SEM, SEM] + [HBM] * n + [pl.BlockSpec(memory_space=pltpu.VMEM)],
        out_shape=[sems, sems] + [pltpu.HBM(b_.shape, b_.dtype) for b_ in blocks] + [TOKEN],
        input_output_aliases={a: a + 2 for a in range(n)}, compiler_params=_in_flight_params(),
    )(*[_in_hbm(b_) for b_ in blocks], after)
    return res[0], res[1], res[2:2 + n], res[2 + n]


def _join_wait(send, recv, blocks, after, tag):
    n = len(blocks)

    def body(*refs):
        ins = refs[:n]
        send_ref, recv_ref = refs[n], refs[n + 1]
        x, y, c, _, _ = _place()
        for a in range(n):
            h = blocks[a].shape[0] // 2
            mine, theirs = ins[a].at[pl.ds(c * h, h)], ins[a].at[pl.ds((1 - c) * h, h)]
            _remote(mine, mine, send_ref.at[a], recv_ref.at[a], (x, y, 1 - c)).wait_send()
            _remote(theirs, theirs, send_ref.at[a], recv_ref.at[a], (x, y, 1 - c)).wait_recv()

    return pl.pallas_call(
        body, name=f"join_wait_{tag}", in_specs=[HBM] * n + [SEM, SEM, ANY], out_specs=[HBM] * n,
        out_shape=[pltpu.HBM(b_.shape, b_.dtype) for b_ in blocks],
        input_output_aliases={a: a for a in range(n)}, compiler_params=_in_flight_params(),
    )(*blocks, send, recv, after)


HBM = pl.BlockSpec(memory_space=pltpu.HBM)
SEM = pl.BlockSpec(memory_space=pltpu.SEMAPHORE)
TOKEN = jax.ShapeDtypeStruct((8, 128), F32)


def _in_flight_params():
    return pltpu.CompilerParams(has_side_effects=pltpu.SideEffectType.DATAFLOW_SIDE_EFFECTING)


def _in_hbm(a):
    return pltpu.with_memory_space_constraint(a, pltpu.HBM)


def _gather_piece(ref, rows, split, slot, hc):
    return _half(ref.at[slot], hc, rows) if split else ref.at[slot]


def _gather_start(stacks, split, after, tag):
    n = len(stacks)

    def body(*refs):
        ins = refs[:n]
        send, recv = refs[n + 1], refs[n + 2]
        token = refs[2 * n + 3]
        _, _, c, j, chips = _place()
        for a in range(n):
            mine = _gather_piece(ins[a], stacks[a].shape[1], split[a], j, c)
            for t in range(3):
                _remote(mine, mine, send.at[3 * a + t], recv.at[3 * a + t], (*chips[t], c)).start()
        token[...] = jnp.zeros_like(token)

    sems = pltpu.SemaphoreType.DMA((3 * n,))
    res = pl.pallas_call(
        body, name=f"gather_start_{tag}", in_specs=[HBM] * n + [ANY],
        out_specs=[SEM, SEM] + [HBM] * n + [pl.BlockSpec(memory_space=pltpu.VMEM)],
        out_shape=[sems, sems] + [pltpu.HBM(s.shape, s.dtype) for s in stacks] + [TOKEN],
        input_output_aliases={a: a + 2 for a in range(n)}, compiler_params=_in_flight_params(),
    )(*[_in_hbm(s) for s in stacks], after)
    return res[0], res[1], res[2:2 + n], res[2 + n]


def _gather_wait(send, recv, stacks, split, after, tag):
    n = len(stacks)

    def body(*refs):
        ins = refs[:n]
        send_ref, recv_ref = refs[n], refs[n + 1]
        _, _, c, j, chips = _place()
        for a in range(n):
            rows = stacks[a].shape[1]
            mine = _gather_piece(ins[a], rows, split[a], j, c)
            for t, (px, py) in enumerate(chips):
                theirs = _gather_piece(ins[a], rows, split[a], 2 * px + py, c)
                _remote(mine, mine, send_ref.at[3 * a + t], recv_ref.at[3 * a + t], (px, py, c)).wait_send()
                _remote(theirs, theirs, send_ref.at[3 * a + t], recv_ref.at[3 * a + t], (px, py, c)).wait_recv()

    return pl.pallas_call(
        body, name=f"gather_wait_{tag}", in_specs=[HBM] * n + [SEM, SEM, ANY], out_specs=[HBM] * n,
        out_shape=[pltpu.HBM(s.shape, s.dtype) for s in stacks],
        input_output_aliases={a: a for a in range(n)}, compiler_params=_in_flight_params(),
    )(*stacks, send, recv, after)


def _gather_forward(stacks, split, tag):
    idx = [a for a in range(len(stacks)) if split[a]]
    n = len(idx)

    def body(*refs):
        outs = refs[n:2 * n]
        send, recv = refs[2 * n:]
        x, y, c, _, chips = _place()
        sends = []
        for t, (px, py) in enumerate(chips):
            for a in range(n):
                blk = _half(outs[a].at[2 * px + py], c, stacks[idx[a]].shape[1])
                cp = _remote(blk, blk, send.at[a, t], recv.at[a, t], (x, y, 1 - c))
                cp.start()
                sends.append(cp)
        for t, (px, py) in enumerate(chips):
            for a in range(n):
                blk = _half(outs[a].at[2 * px + py], 1 - c, stacks[idx[a]].shape[1])
                _remote(blk, blk, send.at[a, t], recv.at[a, t], (x, y, 1 - c)).wait_recv()
        for cp in sends:
            cp.wait_send()

    sem = pltpu.SemaphoreType.DMA
    res = pl.pallas_call(
        body, name=f"gather_forward_{tag}", in_specs=[ANY] * n, out_specs=[ANY] * n,
        out_shape=[jax.ShapeDtypeStruct(stacks[a].shape, stacks[a].dtype) for a in idx],
        input_output_aliases={a: a for a in range(n)}, scratch_shapes=[sem((n, 3)), sem((n, 3))],
    )(*[stacks[a] for a in idx])
    out = list(stacks)
    for a, r in zip(idx, res):
        out[a] = r
    return out


def _forward_start(stacks, after, tag):
    n = len(stacks)

    def body(*refs):
        ins = refs[:n]
        send, recv = refs[n + 1], refs[n + 2]
        token = refs[2 * n + 3]
        x, y, c, _, chips = _place()
        for a in range(n):
            for t, (px, py) in enumerate(chips):
                blk = _half(ins[a].at[2 * px + py], c, stacks[a].shape[1])
                _remote(blk, blk, send.at[3 * a + t], recv.at[3 * a + t], (x, y, 1 - c)).start()
        token[...] = jnp.zeros_like(token)

    sems = pltpu.SemaphoreType.DMA((3 * n,))
    res = pl.pallas_call(
        body, name=f"forward_start_{tag}", in_specs=[HBM] * n + [ANY],
        out_specs=[SEM, SEM] + [HBM] * n + [pl.BlockSpec(memory_space=pltpu.VMEM)],
        out_shape=[sems, sems] + [pltpu.HBM(s.shape, s.dtype) for s in stacks] + [TOKEN],
        input_output_aliases={a: a + 2 for a in range(n)}, compiler_params=_in_flight_params(),
    )(*[_in_hbm(s) for s in stacks], after)
    return res[0], res[1], res[2:2 + n], res[2 + n]


def _forward_wait(send, recv, stacks, after, tag):
    n = len(stacks)

    def body(*refs):
        ins = refs[:n]
        send_ref, recv_ref = refs[n], refs[n + 1]
        x, y, c, _, chips = _place()
        for a in range(n):
            for t, (px, py) in enumerate(chips):
                mine = _half(ins[a].at[2 * px + py], c, stacks[a].shape[1])
                theirs = _half(ins[a].at[2 * px + py], 1 - c, stacks[a].shape[1])
                _remote(mine, mine, send_ref.at[3 * a + t], recv_ref.at[3 * a + t], (x, y, 1 - c)).wait_send()
                _remote(theirs, theirs, send_ref.at[3 * a + t], recv_ref.at[3 * a + t], (x, y, 1 - c)).wait_recv()

    return pl.pallas_call(
        body, name=f"forward_wait_{tag}", in_specs=[HBM] * n + [SEM, SEM, ANY], out_specs=[HBM] * n,
        out_shape=[pltpu.HBM(s.shape, s.dtype) for s in stacks],
        input_output_aliases={a: a for a in range(n)}, compiler_params=_in_flight_params(),
    )(*stacks, send, recv, after)


def _swap_start(grads, tag):
    n = len(grads)

    def body(*refs):
        ins, gots = refs[:n], refs[n:2 * n]
        send, recv = refs[2 * n], refs[2 * n + 1]
        token = refs[4 * n + 2]
        x, y, c, _, _ = _place()
        for a in range(n):
            h = grads[a].shape[1] // 2
            _remote(ins[a].at[:, pl.ds((1 - c) * h, h)], gots[a], send.at[a], recv.at[a], (x, y, 1 - c)).start()
        token[...] = jnp.zeros_like(token)

    sems = pltpu.SemaphoreType.DMA((n,))
    halves = [(g.shape[0], g.shape[1] // 2, g.shape[2]) for g in grads]
    res = pl.pallas_call(
        body, name=f"swap_start_{tag}", in_specs=[HBM] * (2 * n),
        out_specs=[SEM, SEM] + [HBM] * (2 * n) + [pl.BlockSpec(memory_space=pltpu.VMEM)],
        out_shape=[sems, sems] + [pltpu.HBM(g.shape, g.dtype) for g in grads] + [pltpu.HBM(s, F32) for s in halves] + [TOKEN],
        input_output_aliases={a: a + 2 for a in range(2 * n)}, compiler_params=_in_flight_params(),
    )(*[_in_hbm(g) for g in grads], *[_in_hbm(lax.empty(s, F32)) for s in halves])
    return res[0], res[1], res[2:2 + n], res[2 + n:2 + 2 * n], res[2 + 2 * n]


def _swap_wait(send, recv, grads, gots, after, tag):
    n = len(grads)

    def body(*refs):
        ins, lnd = refs[:n], refs[n:2 * n]
        send_ref, recv_ref = refs[2 * n], refs[2 * n + 1]
        x, y, c, _, _ = _place()
        for a in range(n):
            h = grads[a].shape[1] // 2
            cp = _remote(ins[a].at[:, pl.ds((1 - c) * h, h)], lnd[a], send_ref.at[a], recv_ref.at[a], (x, y, 1 - c))
            cp.wait_send()
            cp.wait_recv()

    bufs = [pltpu.HBM(g.shape, g.dtype) for g in grads] + [pltpu.HBM(g.shape, g.dtype) for g in gots]
    res = pl.pallas_call(
        body, name=f"swap_wait_{tag}", in_specs=[HBM] * (2 * n) + [SEM, SEM, ANY], out_specs=[HBM] * (2 * n),
        out_shape=bufs, input_output_aliases={a: a for a in range(2 * n)}, compiler_params=_in_flight_params(),
    )(*grads, *gots, send, recv, after)
    return res[:n], res[n:]


def _exchange_start(parts, tag):
    n = len(parts)

    def body(*refs):
        ins, lands = refs[:n], refs[n:2 * n]
        send, recv = refs[2 * n], refs[2 * n + 1]
        token = refs[4 * n + 2]
        _, _, c, j, chips = _place()
        for t, (px, py) in enumerate(chips):
            for a in range(n):
                _remote(ins[a].at[2 * px + py], lands[a].at[j], send.at[3 * a + t], recv.at[3 * a + t], (px, py, c)).start()
        token[...] = jnp.zeros_like(token)

    sems = pltpu.SemaphoreType.DMA((3 * n,))
    bufs = [pltpu.HBM(p.shape, p.dtype) for p in parts]
    res = pl.pallas_call(
        body, name=f"exchange_start_{tag}", in_specs=[HBM] * (2 * n),
        out_specs=[SEM, SEM] + [HBM] * (2 * n) + [pl.BlockSpec(memory_space=pltpu.VMEM)],
        out_shape=[sems, sems] + bufs + bufs + [TOKEN],
        input_output_aliases={a: a + 2 for a in range(2 * n)}, compiler_params=_in_flight_params(),
    )(*[_in_hbm(p) for p in parts], *[_in_hbm(lax.empty(p.shape, p.dtype)) for p in parts])
    return res[0], res[1], res[2:2 + n], res[2 + n:2 + 2 * n], res[2 + 2 * n]


def _exchange_wait(send, recv, parts, lands, after, tag):
    n = len(parts)

    def body(*refs):
        ins, lnd = refs[:n], refs[n:2 * n]
        send_ref, recv_ref = refs[2 * n], refs[2 * n + 1]
        _, _, c, j, chips = _place()
        for t, (px, py) in enumerate(chips):
            jt = 2 * px + py
            for a in range(n):
                _remote(ins[a].at[jt], lnd[a].at[j], send_ref.at[3 * a + t], recv_ref.at[3 * a + t], (px, py, c)).wait_send()
                _remote(ins[a].at[jt], lnd[a].at[jt], send_ref.at[3 * a + t], recv_ref.at[3 * a + t], (px, py, c)).wait_recv()

    bufs = [pltpu.HBM(p.shape, p.dtype) for p in parts]
    res = pl.pallas_call(
        body, name=f"exchange_wait_{tag}", in_specs=[HBM] * (2 * n) + [SEM, SEM, ANY], out_specs=[HBM] * (2 * n),
        out_shape=bufs + bufs, input_output_aliases={a: a for a in range(2 * n)}, compiler_params=_in_flight_params(),
    )(*parts, *lands, send, recv, after)
    return res[:n], res[n:]


def _small_chip_sums(arrs):
    n = len(arrs)

    def body(*refs):
        ins, outs = refs[:n], refs[n:2 * n]
        sib = refs[2 * n:3 * n]
        send, recv = refs[3 * n:]
        x, y, c, j, _ = _place()
        swaps = [_remote(ins[a], sib[a], send.at[a], recv.at[a], (x, y, 1 - c)) for a in range(n)]
        for cp in swaps:
            cp.start()
        for a in range(n):
            swaps[a].wait_recv()
            outs[a][j] = ins[a][...] + sib[a][...]
        for cp in swaps:
            cp.wait_send()

    sem = pltpu.SemaphoreType.DMA
    vm = pl.BlockSpec(memory_space=pltpu.VMEM)
    return pl.pallas_call(
        body, name="small_chip_sums", in_specs=[vm] * n, out_specs=[vm] * n,
        out_shape=[jax.ShapeDtypeStruct((N_SHARD, *a.shape), F32) for a in arrs],
        scratch_shapes=[pltpu.VMEM(a.shape, F32) for a in arrs] + [sem((n,)), sem((n,))],
        compiler_params=_cp(),
    )(*arrs)


def _small_totals(stacks):
    n = len(stacks)

    def body(*refs):
        for a in range(n):
            refs[n + a][...] = ((refs[a][0] + refs[a][1]) + refs[a][2]) + refs[a][3]

    return pl.pallas_call(body, name="small_totals", out_shape=[jax.ShapeDtypeStruct(s.shape[1:], F32) for s in stacks],
                          compiler_params=_cp())(*stacks)


SMALL_1024 = ("ln1_g", "ln1_b", "ln2_g", "ln2_b", "b_ple_gate", "ln3_g", "ln3_b")


def _adamw_small(red3, red1, redz, g_conv_w, redc, red_ws, red_bs, params):
    shape2d = {"ln_z_g": (1, D_GMLP), "ln_z_b": (1, D_GMLP), "w_s": (N_HEADS * BLK, BLK), "b_s": (N_HEADS, BLK),
               "conv_w": (3, FF_BLK), "conv_b": (N_SHARD, FF_BLK), **{k: (1, D_MODEL) for k in SMALL_1024}}
    names = list(shape2d)
    flat = [a.reshape(shape2d[k]) for k in names for a in params[k]]

    def body(r3, r1, rz, gcw, rc, rws, rbs, *refs):
        ins, outs = refs[:3 * len(names)], refs[3 * len(names):]

        def grad_of(k):
            if k == "w_s":
                return rws[...]
            if k == "b_s":
                return rbs[...]
            if k == "conv_w":
                return gcw[0:3, :]
            if k == "conv_b":
                return jnp.concatenate([rc[j * STAT_ROWS + 3:j * STAT_ROWS + 4, :] for j in range(N_SHARD)], axis=0)
            src, row = {"ln3_g": (r3, 0), "ln3_b": (r3, 1), "b_ple_gate": (r3, 2), "ln2_g": (r3, 3), "ln2_b": (r3, 4),
                        "ln1_g": (r1, 0), "ln1_b": (r1, 1), "ln_z_g": (rz, 0), "ln_z_b": (rz, 1)}[k]
            return src[row:row + 1, :]

        for i, k in enumerate(names):
            w_ref, m_ref, v_ref = ins[3 * i:3 * i + 3]
            g_ref, d_ref, nm_ref, nv_ref = outs[4 * i:4 * i + 4]
            g = grad_of(k)
            g_ref[...] = g
            d_ref[...], nm_ref[...], nv_ref[...] = _adamw_math(w_ref[...], g, m_ref[...], v_ref[...])

    res = pl.pallas_call(
        body, name="adamw_small",
        out_shape=[jax.ShapeDtypeStruct(shape2d[k], F32) for k in names for _ in range(4)],
        compiler_params=_cp(),
    )(red3, red1, redz, g_conv_w, redc, red_ws, red_bs, *flat)
    return {k: tuple(r.reshape(params[k][0].shape) for r in res[4 * i:4 * i + 4]) for i, k in enumerate(names)}


WEIGHTS = ("w_in", "ln_z_g", "ln_z_b", "w_s", "b_s", "w_o", "ln1_g", "ln1_b", "w_ff_a", "w_ff_b", "conv_w", "conv_b",
           "w_ff_down", "ln2_g", "ln2_b", "w_ple_gate", "b_ple_gate", "w_ple_in", "ln3_g", "ln3_b")
BIG = ("w_in", "w_o", "w_ff_a", "w_ff_b", "w_ff_down", "w_ple_gate", "w_ple_in")
TRANSPOSED = ("w_ff_a", "w_ff_b")
LATE = ("w_o", "w_ff_a", "w_ff_b", "w_ff_down", "w_ple_gate", "w_ple_in", "conv_w")


def kernel(x, p, positions, w_in, ln_z_g, ln_z_b, w_s, b_s, w_o, ln1_g, ln1_b, w_ff_a, w_ff_b, conv_w, conv_b, w_ff_down, ln2_g, ln2_b, w_ple_gate, b_ple_gate, w_ple_in, ln3_g, ln3_b, loss_target, m_w_in, m_ln_z_g, m_ln_z_b, m_w_s, m_b_s, m_w_o, m_ln1_g, m_ln1_b, m_w_ff_a, m_w_ff_b, m_conv_w, m_conv_b, m_w_ff_down, m_ln2_g, m_ln2_b, m_w_ple_gate, m_b_ple_gate, m_w_ple_in, m_ln3_g, m_ln3_b, v_w_in, v_ln_z_g, v_ln_z_b, v_w_s, v_b_s, v_w_o, v_ln1_g, v_ln1_b, v_w_ff_a, v_w_ff_b, v_conv_w, v_conv_b, v_w_ff_down, v_ln2_g, v_ln2_b, v_w_ple_gate, v_b_ple_gate, v_w_ple_in, v_ln3_g, v_ln3_b):
    args = locals()
    w = {k: args[k] for k in WEIGHTS}
    m = {k: args["m_" + k] for k in WEIGHTS}
    v = {k: args["v_" + k] for k in WEIGHTS}

    for k in TRANSPOSED:
        w[k], m[k], v[k] = (jnp.swapaxes(a, 1, 2) for a in (w[k], m[k], v[k]))

    chip = 2 * lax.axis_index("x") + lax.axis_index("y")
    place = jnp.stack([chip, lax.axis_index("c")]).astype(jnp.int32)
    stack = dict(zip(["w_in"], _place_shards("cast_w_in", [w["w_in"][0]], [MXU], place, place)))
    i_send, i_recv, in_flight, dep = _gather_start([stack["w_in"]], [True], place, "w_in")
    stack.update(zip(LATE, _place_shards("cast_late", [w[k][0] for k in LATE],
                                         [F32 if k == "conv_w" else MXU for k in LATE], place, dep)))
    split_late = [k != "conv_w" for k in LATE]
    g_send, g_recv, late_flight, start_dep = _gather_start([stack[k] for k in LATE], split_late, place, "late")
    rope = _rope_tables(positions, x.shape[1], start_dep)
    landed_in = _gather_wait(i_send, i_recv, in_flight, [True], rope[0], "w_in")
    w_in_full, = _gather_forward(landed_in, [True], "w_in")
    halves =[k for k, sp in zip(LATE, split_late) if sp]
    trips = {}

    def late_landed(after):
        fw = dict(zip(LATE, _gather_wait(g_send, g_recv, late_flight, split_late, after, "late")))
        trips["late"] = (fw, *_forward_start([fw[k] for k in halves], fw["conv_w"], "late"))
        return trips["late"][-1]

    def late_weights(after):
        fw, send, recv, flight, _ = trips["late"]
        fw.update(zip(halves, _forward_wait(send, recv, flight, after, "late")))
        return (fw["w_o"].reshape(D_MODEL, D_MODEL), fw["w_ff_a"], fw["w_ff_b"], fw["conv_w"], fw["w_ff_down"],
                fw["w_ple_gate"].reshape(D_MODEL, D_MODEL), fw["w_ple_in"])

    def swap_started(names, grads, tag):
        stacked = [g.reshape(N_SHARD, *w[k].shape[1:]) for k, g in zip(names, grads)]
        return (names, tag, *_swap_start(stacked, tag))

    def partial_sums(swap, after):
        names, tag, send, recv, stacked, gots, _ = swap
        stacked, got = _swap_wait(send, recv, stacked, gots, after, tag)
        pair = _pair_sums(f"rs_pair_{tag}", stacked, got, place)
        return (names, tag, *_exchange_start(pair, tag))

    def chip_summed(trip, after, dep):
        names, tag, send, recv, pair, lands, _ = trip
        pair, landed = _exchange_wait(send, recv, pair, lands, after, tag)
        return _chip_sums(f"rs_sum_{tag}", pair, landed, place, dep), names, tag

    def reduced(trip, after, dep):
        blocks, names, tag = chip_summed(trip, after, dep)
        return dict(zip(names, _sibling_join(blocks, tag)))

    def early_grads_landed(after):
        blocks, names, tag = chip_summed(trips["early"], after, trips["small"][-1])
        trips["join"] = (names, *_join_start(blocks, after, tag))
        return trips["join"][-1]

    def early_grads(grads):
        trips["swap"] = swap_started(list(grads), list(grads.values()), "early")
        return trips["swap"][-1]

    def early_grads_sent(after, small):
        trips["early"] = partial_sums(trips["swap"], after)
        stat3, stat1, zstat, cstat, dws, dbs = small
        sums = _small_chip_sums([stat3, stat1, zstat, cstat.reshape(N_SHARD * STAT_ROWS, FF_BLK),
                                 dws.reshape(N_HEADS * BLK, BLK), dbs])
        trips["small"] = _gather_start(sums, [False] * len(sums), trips["early"][-1], "small")
        return trips["small"][-1]

    grad_x, g_w_in = _local_step(
        x[0], p[0, 0], rope, loss_target[0], w_in_full, start_dep, late_landed, late_weights, early_grads, early_grads_sent,
        early_grads_landed, ln_z_g, ln_z_b, w_s, b_s, ln1_g, ln1_b, conv_b, ln2_g, ln2_b, b_ple_gate, ln3_g, ln3_b)

    trips["w_in"] = partial_sums(swap_started(["w_in"], [g_w_in], "w_in"), g_w_in)
    out = {}

    def adamw(red, tag):
        names = list(red)
        steps = _adamw_shards(f"adamw_{tag}", [w[k] for k in names], [red[k] for k in names], [m[k] for k in names],
                              [v[k] for k in names])
        for k, (d, nm, nv) in zip(names, steps):
            out[k] = (red[k].reshape(w[k].shape), d, nm, nv)

    names, j_send, j_recv, j_flight, _ = trips["join"]
    adamw(dict(zip(names, _join_wait(j_send, j_recv, j_flight, trips["w_in"][-1], "early"))), "early")
    adamw(reduced(trips["w_in"], out["w_o"][3], start_dep), "w_in")
    for k in TRANSPOSED:
        out[k] = tuple(jnp.swapaxes(a, 1, 2) for a in out[k])

    s_send, s_recv, s_flight, _ = trips["small"]
    red3, red1, redz, redc, red_ws, red_bs = _small_totals(
        _gather_wait(s_send, s_recv, s_flight, [False] * len(s_flight), out["w_in"][3], "small"))
    loss = (0.5 / D_MODEL) * jnp.sum(red3[5])
    g_conv_w = lax.dynamic_slice_in_dim(redc, chip * STAT_ROWS, STAT_ROWS, 0)
    names_small = [k for k in WEIGHTS if k not in BIG]
    out.update(_adamw_small(red3, red1, redz, g_conv_w, redc, red_ws, red_bs, {k: (w[k], m[k], v[k]) for k in names_small}))

    return (loss, grad_x[None], *[out[k][0] for k in WEIGHTS], *[out[k][1] for k in WEIGHTS],
            *[out[k][2] for k in WEIGHTS], *[out[k][3] for k in WEIGHTS])
```

```python
import math

import numpy as np
import jax
import jax.numpy as jnp
from jax import lax
from jax.experimental import pallas as pl
from jax.experimental.pallas import tpu as pltpu

F32 = jnp.float32
BF16 = jnp.bfloat16
MXU = BF16

D_MODEL = 1024
HEAD_DIM = 64
N_HEADS = 8
D_ATTN = 512
D_GMLP = 512
D_IN = 2560
DILATIONS = (1, 4, 16)
BLK = 128
ROPE_THETA = 500000.0
ROPE_DIM = 16
D_FF = 2816
D_PLE = 256
LN_EPS = 1e-5
ALPHA = 2.0 ** 0.25
NEG_INF = -1e30
N_SHARD = 4
W_IN_BLK = D_IN // N_SHARD
FF_BLK = D_FF // N_SHARD
ROW_BLK = D_MODEL // N_SHARD
ADAM_LR, ADAM_B1, ADAM_B2, ADAM_EPS, ADAM_WD, ADAM_STEP = 0.001, 0.9, 0.999, 1e-08, 0.01, 10

TM = 512
HALO = 8
ROW_GROUPS = 2
VMEM_LIMIT = 56 * 1024 * 1024


def _cp(**kw):
    return pltpu.CompilerParams(vmem_limit_bytes=VMEM_LIMIT, **kw)


def _full(shape):
    n = len(shape)
    return pl.BlockSpec(shape, lambda *_: (0,) * n)


def _gelu(x):
    return 0.5 * x * (1.0 + lax.erf(x * (1.0 / math.sqrt(2.0))))


def _gelu_grad(x):
    return 0.5 * (1.0 + lax.erf(x * (1.0 / math.sqrt(2.0)))) + x * jnp.exp(-0.5 * x * x) * (1.0 / math.sqrt(2.0 * math.pi))


def _ln_fwd(r):
    mu = jnp.mean(r, axis=-1, keepdims=True)
    xc = r - mu
    var = jnp.mean(xc * xc, axis=-1, keepdims=True)
    rstd = lax.rsqrt(var + LN_EPS)
    return xc * rstd, rstd


def _ln_bwd(dy, xhat, rstd, g):
    dxh = dy * g
    m1 = jnp.mean(dxh, axis=-1, keepdims=True)
    m2 = jnp.mean(dxh * xhat, axis=-1, keepdims=True)
    return rstd * (dxh - m1 - xhat * m2)


def _dot(a, b):
    return jnp.dot(a.astype(MXU), b.astype(MXU), preferred_element_type=F32)


def _dot_nt(a, b):
    return lax.dot_general(a.astype(MXU), b.astype(MXU), (((1,), (1,)), ((), ())), preferred_element_type=F32)


def _dot_tn(a, b):
    return lax.dot_general(a.astype(MXU), b.astype(MXU), (((0,), (0,)), ((), ())), preferred_element_type=F32)


def _colsum(v):
    return jnp.sum(v, axis=0, keepdims=True)


def _rope_tables(positions, t, dep):
    inv = np.float32(ROPE_THETA) ** (-np.arange(0, ROPE_DIM, 2, dtype=np.float32) / np.float32(ROPE_DIM))
    half = ROPE_DIM // 2
    pos_rep = jnp.repeat(positions.reshape(t // 16, 16), half, axis=1)
    inv_row = jnp.asarray(np.tile(inv, 16)[None, :], F32)

    def trig_body(pos_ref, inv_ref, dep_ref, cos_ref, sin_ref):
        ang = pos_ref[...].astype(F32) * inv_ref[...]
        cos_ref[...] = jnp.cos(ang)
        sin_ref[...] = jnp.sin(ang)

    vm = pl.BlockSpec(memory_space=pltpu.VMEM)
    cos8, sin8 = pl.pallas_call(
        trig_body, name="rope_trig", in_specs=[vm, vm, pl.BlockSpec(memory_space=pl.ANY)], out_specs=[vm, vm],
        out_shape=(jax.ShapeDtypeStruct((t // 16, 128), F32), jax.ShapeDtypeStruct((t // 16, 128), F32)),
    )(pos_rep, inv_row, dep)
    cos8 = cos8.reshape(t, half)
    sin8 = sin8.reshape(t, half)

    lane = np.arange(128) % HEAD_DIM
    sel = (np.arange(half)[:, None] == (lane % half)[None, :])
    e_cos = (sel & (lane < ROPE_DIM)[None, :]).astype(np.float32)
    e_s1 = -(sel & (lane < half)[None, :]).astype(np.float32)
    e_s2 = (sel & ((lane >= half) & (lane < ROPE_DIM))[None, :]).astype(np.float32)
    ones = (lane >= ROPE_DIM).astype(np.float32)[None, :]

    def expand_body(cos_ref, sin_ref, ec_ref, e1_ref, e2_ref, ones_ref, c_ref, s1_ref, s2_ref):
        hp = lax.Precision.HIGHEST
        c_ref[...] = jnp.dot(cos_ref[...], ec_ref[...], precision=hp, preferred_element_type=F32) + ones_ref[...]
        s1_ref[...] = jnp.dot(sin_ref[...], e1_ref[...], precision=hp, preferred_element_type=F32)
        s2_ref[...] = jnp.dot(sin_ref[...], e2_ref[...], precision=hp, preferred_element_type=F32)

    tab = jax.ShapeDtypeStruct((t, 128), F32)
    return pl.pallas_call(expand_body, name="rope_expand", out_shape=(tab, tab, tab), compiler_params=_cp())(
        cos8, sin8, jnp.asarray(e_cos), jnp.asarray(e_s1), jnp.asarray(e_s2), jnp.asarray(ones))


def _tile_heads(tab):
    return jnp.concatenate([tab] * (D_ATTN // 128), axis=1)


def _rope_apply(v, c, s1, s2):
    n = v.shape[1]
    half = ROPE_DIM // 2
    return v * c + pltpu.roll(v, n - half, 1) * s1 + pltpu.roll(v, half, 1) * s2


def _rope_apply_t(g, c, s1, s2):
    n = g.shape[1]
    half = ROPE_DIM // 2
    return g * c + pltpu.roll(g * s1, half, 1) + pltpu.roll(g * s2, n - half, 1)


LANE_CHUNKS = D_ATTN // 128
HEAD_LANES = 128 // N_HEADS


def _perm_shape(t, d, w, dtype):
    return jax.ShapeDtypeStruct((d, t // d, w), dtype)


def _perm_tile(d, w):
    return pl.BlockSpec((None if d == 1 else d, TM // d, w), lambda i: (0, i, 0))


def _to_planes(ref, scr, d, n_chunks, dtype):
    for r in range(d):
        for cc in range(n_chunks):
            ref[r, :, cc * 128:(cc + 1) * 128] = scr.at[cc][pl.ds(r, TM // d, stride=d), :].astype(dtype)


def _from_planes(ref, scr, d, n_chunks, accumulate=False):
    for r in range(d):
        for cc in range(n_chunks):
            rows = scr.at[cc]
            val = ref[r, :, cc * 128:(cc + 1) * 128].astype(F32)
            if accumulate:
                rows[pl.ds(r, TM // d, stride=d), :] += val
            else:
                rows[pl.ds(r, TM // d, stride=d), :] = val


def _chunks(val):
    return [val[:, cc * 128:(cc + 1) * 128] for cc in range(val.shape[1] // 128)]


def _unchunk(scr, n_chunks, base=0):
    return jnp.concatenate([scr[base + cc] for cc in range(n_chunks)], axis=1)


def _head_expand():
    src = np.arange(128)[:, None]
    dst = np.arange(D_ATTN)[None, :]
    return jnp.asarray((src == (dst // HEAD_DIM) * HEAD_LANES).astype(np.float32))


def _head_reduce():
    src = np.arange(D_ATTN)[:, None]
    dst = np.arange(128)[None, :]
    return jnp.asarray((src // HEAD_DIM == dst // HEAD_LANES).astype(np.float32))


def _dot_select(a, sel):
    hi = a.astype(BF16)
    lo = (a - hi.astype(F32)).astype(BF16)
    sel = sel.astype(BF16)
    return jnp.dot(hi, sel, preferred_element_type=F32) + jnp.dot(lo, sel, preferred_element_type=F32)


def _qkvuz(x, w_in, c_tab, s1_tab, s2_tab, ln_z_g, ln_z_b, w_s, b_full, dep):
    t = x.shape[0]
    nchunk = TM // BLK

    def body(x_ref, w_ref, c_ref, s1_ref, s2_ref, g_ref, b_ref, ws_ref, bf_ref, dep_ref,
             qkv1_ref, qkv4_ref, qkv16_ref, hu_ref, hz_ref, mixed_ref, gm_ref, xb_ref, h_scr, wm_scr, p_scr):
        @pl.when(pl.program_id(0) == 0)
        def _():
            row = lax.broadcasted_iota(jnp.int32, (BLK, BLK), 0)
            col = lax.broadcasted_iota(jnp.int32, (BLK, BLK), 1)
            for g in range(N_HEADS):
                wm_scr[g] = jnp.where(col <= row, ws_ref[g], 0.0).astype(MXU)

        xb = x_ref[...].astype(MXU)
        xb_ref[...] = xb
        for j in range(N_SHARD):
            h_scr[:, j * W_IN_BLK:(j + 1) * W_IN_BLK] = jnp.dot(xb, w_ref[j], preferred_element_type=F32)
        c, s1, s2 = _tile_heads(c_ref[...]), _tile_heads(s1_ref[...]), _tile_heads(s2_ref[...])
        q = _rope_apply(h_scr[:, 0:D_ATTN], c, s1, s2) * (1.0 / math.sqrt(HEAD_DIM))
        k = _rope_apply(h_scr[:, D_ATTN:2 * D_ATTN], c, s1, s2)
        for part, val in enumerate((q, k, h_scr[:, 2 * D_ATTN:3 * D_ATTN])):
            qkv1_ref[:, part * D_ATTN:(part + 1) * D_ATTN] = val.astype(MXU)
            for cc in range(LANE_CHUNKS):
                p_scr[part * LANE_CHUNKS + cc] = val[:, cc * 128:(cc + 1) * 128]
        _to_planes(qkv4_ref, p_scr, DILATIONS[1], 3 * LANE_CHUNKS, MXU)
        _to_planes(qkv16_ref, p_scr, DILATIONS[2], 3 * LANE_CHUNKS, MXU)
        hu = h_scr[:, 3 * D_ATTN:3 * D_ATTN + D_GMLP]
        hz = h_scr[:, 3 * D_ATTN + D_GMLP:]
        hu_ref[...] = hu
        hz_ref[...] = hz
        zhat, _ = _ln_fwd(_gelu(hz))
        zn = (zhat * g_ref[...] + b_ref[...]).astype(MXU)
        for ch in range(nchunk):
            rows = slice(ch * BLK, (ch + 1) * BLK)
            for g in range(N_HEADS):
                cols = slice(g * HEAD_DIM, (g + 1) * HEAD_DIM)
                mixed_ref[rows, cols] = jnp.dot(wm_scr[g], zn[rows, cols], preferred_element_type=F32) + bf_ref[:, cols]
        gm_ref[...] = (_gelu(hu) * mixed_ref[...]).astype(MXU)

    tok = lambda w: pl.BlockSpec((TM, w), lambda i: (i, 0))
    outs = [_perm_shape(t, d, 3 * D_ATTN, MXU) for d in DILATIONS] + [jax.ShapeDtypeStruct((t, D_GMLP), F32)] * 3 + [
        jax.ShapeDtypeStruct((t, D_GMLP), MXU), jax.ShapeDtypeStruct((t, D_MODEL), MXU)]
    return pl.pallas_call(
        body, name="qkvuz", grid=(t // TM,),
        in_specs=[tok(D_MODEL), _full(w_in.shape), tok(128), tok(128), tok(128), _full(ln_z_g.shape), _full(ln_z_b.shape),
                  _full(w_s.shape), _full(b_full.shape), pl.BlockSpec(memory_space=pl.ANY)],
        out_specs=[_perm_tile(d, 3 * D_ATTN) for d in DILATIONS] + [tok(D_ATTN)] * 4 + [tok(D_MODEL)], out_shape=outs,
        scratch_shapes=[pltpu.VMEM((TM, D_IN), F32), pltpu.VMEM((N_HEADS, BLK, BLK), MXU),
                        pltpu.VMEM((3 * LANE_CHUNKS, TM, 128), F32)],
        compiler_params=_cp(dimension_semantics=("arbitrary",)),
    )(x, w_in, c_tab, s1_tab, s2_tab, ln_z_g, ln_z_b, w_s, b_full, dep)


def _band_valid(n):
    i = lax.broadcasted_iota(jnp.int32, (BLK, 2 * BLK), 0)
    j = lax.broadcasted_iota(jnp.int32, (BLK, 2 * BLK), 1)
    return (j >= i) & (j <= i + BLK) & ((j >= BLK) | (n > 0))


def _attn_fwd(qkv, d, dep):
    _, l_sub, _ = qkv.shape
    nb = l_sub // BLK

    def body(q_ref, kp_ref, kc_ref, vp_ref, vc_ref, dep_ref, o_ref, l_ref):
        valid = _band_valid(pl.program_id(1))
        kcat = jnp.concatenate([kp_ref[...], kc_ref[...]], axis=0)
        vcat = jnp.concatenate([vp_ref[...], vc_ref[...]], axis=0)
        for h in range(N_HEADS):
            cols = slice(h * HEAD_DIM, (h + 1) * HEAD_DIM)
            s = jnp.where(valid, _dot_nt(q_ref[:, cols], kcat[:, cols]), NEG_INF)
            m = jnp.max(s, axis=-1, keepdims=True)
            e = jnp.exp(s - m)
            den = jnp.sum(e, axis=-1, keepdims=True)
            o_ref[:, cols] = _dot(e, vcat[:, cols]) * (1.0 / den)
            l_ref[:, h * HEAD_LANES:(h + 1) * HEAD_LANES] = jnp.broadcast_to(m + jnp.log(den), (BLK, HEAD_LANES))

    def blk(w, col, prev=False):
        return pl.BlockSpec((None, BLK, w), lambda r, n: (r, jnp.maximum(n - 1, 0) if prev else n, col))

    return pl.pallas_call(
        body, name=f"attn_fwd_d{d}", grid=(d, nb),
        in_specs=[blk(D_ATTN, 0), blk(D_ATTN, 1, True), blk(D_ATTN, 1), blk(D_ATTN, 2, True), blk(D_ATTN, 2),
                  pl.BlockSpec(memory_space=pl.ANY)],
        out_specs=[blk(D_ATTN, 0), blk(128, 0)],
        out_shape=[jax.ShapeDtypeStruct((d, l_sub, D_ATTN), F32), jax.ShapeDtypeStruct((d, l_sub, 128), F32)],
        compiler_params=_cp(dimension_semantics=("arbitrary", "arbitrary")),
    )(qkv, qkv, qkv, qkv, qkv, dep)


def _attn_bwd(qkv, do, lse, delta, d, dep):
    _, l_sub, _ = qkv.shape
    nb = l_sub // BLK
    whole = l_sub <= 8 * BLK

    def shares(n, q_ref, kp_ref, kc_ref, vp_ref, vc_ref, do_ref, l_ref, dl_ref, dq_ref):
        valid = _band_valid(n)
        kcat = jnp.concatenate([kp_ref[...], kc_ref[...]], axis=0)
        vcat = jnp.concatenate([vp_ref[...], vc_ref[...]], axis=0)
        for h in range(N_HEADS):
            cols = slice(h * HEAD_DIM, (h + 1) * HEAD_DIM)
            stat = slice(h * HEAD_LANES, h * HEAD_LANES + 1)
            qh, doh = q_ref[:, cols], do_ref[:, cols]
            p = jnp.where(valid, jnp.exp(_dot_nt(qh, kcat[:, cols]) - l_ref[:, stat]), 0.0)
            ds = p * (_dot_nt(doh, vcat[:, cols]) - dl_ref[:, stat])
            dq_ref[:, cols] = _dot(ds, kcat[:, cols])
            yield cols, _dot_tn(ds, qh), _dot_tn(p, doh)

    def body_whole(*refs):
        dk_ref, dv_ref = refs[10:]
        n = pl.program_id(1)
        cur = pl.ds(pl.multiple_of(n * BLK, BLK), BLK)
        prev = pl.ds(pl.multiple_of(jnp.maximum(n - 1, 0) * BLK, BLK), BLK)
        for cols, dk2, dv2 in shares(n, *refs[:8], refs[9]):
            dk_ref[cur, cols] = dk2[BLK:]
            dv_ref[cur, cols] = dv2[BLK:]
            dk_ref[prev, cols] += dk2[0:BLK]
            dv_ref[prev, cols] += dv2[0:BLK]

    def body_carry(*refs):
        dk_ref, dv_ref, ck_scr, cv_scr = refs[10:]
        n = pl.program_id(1)

        @pl.when(n == 0)
        def _():
            ck_scr[...] = jnp.zeros_like(ck_scr)
            cv_scr[...] = jnp.zeros_like(cv_scr)

        @pl.when(n < nb)
        def _():
            for cols, dk2, dv2 in shares(n, *refs[:8], refs[9]):
                dk_ref[:, cols] = ck_scr[:, cols] + dk2[0:BLK]
                dv_ref[:, cols] = cv_scr[:, cols] + dv2[0:BLK]
                ck_scr[:, cols] = dk2[BLK:]
                cv_scr[:, cols] = dv2[BLK:]

        @pl.when(n == nb)
        def _():
            dk_ref[...] = ck_scr[...]
            dv_ref[...] = cv_scr[...]

    def blk(w, col, shift=0):
        return pl.BlockSpec((None, BLK, w), lambda r, n: (r, jnp.clip(n - shift, 0, nb - 1), col))

    if whole:
        dkv_spec = pl.BlockSpec((None, l_sub, D_ATTN), lambda r, n: (r, 0, 0))
        body, steps, scratch = body_whole, nb, []
    else:
        dkv_spec = blk(D_ATTN, 0, 1)
        body, steps, scratch = body_carry, nb + 1, [pltpu.VMEM((BLK, D_ATTN), F32)] * 2
    return pl.pallas_call(
        body, name=f"attn_bwd_d{d}", grid=(d, steps),
        in_specs=[blk(D_ATTN, 0), blk(D_ATTN, 1, 1), blk(D_ATTN, 1), blk(D_ATTN, 2, 1), blk(D_ATTN, 2),
                  blk(D_ATTN, 0), blk(128, 0), blk(128, 0), pl.BlockSpec(memory_space=pl.ANY)],
        out_specs=[blk(D_ATTN, 0), dkv_spec, dkv_spec],
        out_shape=[jax.ShapeDtypeStruct((d, l_sub, D_ATTN), F32)] * 3,
        scratch_shapes=scratch,
        compiler_params=_cp(dimension_semantics=("arbitrary", "arbitrary")),
    )(qkv, qkv, qkv, qkv, qkv, do, lse, delta, dep)


def _mix_ln1(os_, ls_, gm, x, w_o, ln1_g, ln1_b, dep):
    t = x.shape[0]
    expand = _head_expand()

    def body(o1, o4, o16, l1, l4, l16, gm_ref, x_ref, wo_ref, g_ref, b_ref, ex_ref, dep_ref,
             attn_ref, lse1_ref, lse4_ref, lse16_ref, cat_ref, xhat_ref, rstd_ref, x1b_ref, o_scr, l_scr):
        _from_planes(o4, o_scr, DILATIONS[1], LANE_CHUNKS)
        _from_planes(o16, o_scr.at[pl.ds(LANE_CHUNKS, LANE_CHUNKS)], DILATIONS[2], LANE_CHUNKS)
        _from_planes(l4, l_scr, DILATIONS[1], 1)
        _from_planes(l16, l_scr.at[pl.ds(1, 1)], DILATIONS[2], 1)
        la, lb, lc = l1[...], l_scr[0], l_scr[1]
        m = jnp.maximum(jnp.maximum(la, lb), lc)
        ea, eb, ec = jnp.exp(la - m), jnp.exp(lb - m), jnp.exp(lc - m)
        den = ea + eb + ec
        inv = 1.0 / den
        wide = lambda w: _dot_select(w, ex_ref[...])
        attn = (wide(ea * inv) * o1[...] + wide(eb * inv) * _unchunk(o_scr, LANE_CHUNKS)
                + wide(ec * inv) * _unchunk(o_scr, LANE_CHUNKS, LANE_CHUNKS))
        attn_ref[...] = attn
        lse = m + jnp.log(den)
        lse1_ref[...] = lse
        l_scr[2] = lse
        _to_planes(lse4_ref, l_scr.at[pl.ds(2, 1)], DILATIONS[1], 1, F32)
        _to_planes(lse16_ref, l_scr.at[pl.ds(2, 1)], DILATIONS[2], 1, F32)
        cat_ref[:, 0:D_ATTN] = attn.astype(MXU)
        cat_ref[:, D_ATTN:] = gm_ref[...]
        mix = jnp.dot(cat_ref[...], wo_ref[...], preferred_element_type=F32)
        xhat, rstd = _ln_fwd(ALPHA * x_ref[...] + mix)
        xhat_ref[...] = xhat
        rstd_ref[...] = rstd
        x1b_ref[...] = (xhat * g_ref[...] + b_ref[...]).astype(MXU)

    tok = lambda w: pl.BlockSpec((TM, w), lambda i: (i, 0))
    outs = [jax.ShapeDtypeStruct((t, D_ATTN), F32)] + [_perm_shape(t, d, 128, F32) for d in DILATIONS] + [
        jax.ShapeDtypeStruct((t, D_MODEL), MXU), jax.ShapeDtypeStruct((t, D_MODEL), F32), jax.ShapeDtypeStruct((t, 1), F32),
        jax.ShapeDtypeStruct((t, D_MODEL), MXU)]
    return pl.pallas_call(
        body, name="mix_ln1", grid=(t // TM,),
        in_specs=[_perm_tile(d, D_ATTN) for d in DILATIONS] + [_perm_tile(d, 128) for d in DILATIONS]
        + [tok(D_GMLP), tok(D_MODEL), _full(w_o.shape), _full(ln1_g.shape), _full(ln1_b.shape), _full(expand.shape),
           pl.BlockSpec(memory_space=pl.ANY)],
        out_specs=[tok(D_ATTN)] + [_perm_tile(d, 128) for d in DILATIONS] + [tok(D_MODEL), tok(D_MODEL), tok(1), tok(D_MODEL)],
        out_shape=outs,
        scratch_shapes=[pltpu.VMEM((2 * LANE_CHUNKS, TM, 128), F32), pltpu.VMEM((3, TM, 128), F32)],
        compiler_params=_cp(dimension_semantics=("arbitrary",)),
    )(*os_, *ls_, gm, x, w_o, ln1_g, ln1_b, expand, dep)


def _conv_fwd(a_ext, w_ref, b_ref, rows):
    back = [pltpu.roll(a_ext, s, 0)[HALO:HALO + rows] for s in (1, 2)]
    return b_ref[...] + w_ref[2:3, :] * a_ext[HALO:HALO + rows] + w_ref[1:2, :] * back[0] + w_ref[0:1, :] * back[1]


def _ffn_in(x1b, w_a, w_b, conv_w, conv_b):
    t = x1b.shape[0]
    hb = TM // HALO

    def body(x_ref, xh_ref, wa_ref, wb_ref, cw_ref, cb_ref, apre_ref, act_ref, gate_ref, f_ref):
        i = pl.program_id(1)
        a_pre = _dot_nt(x_ref[...], wa_ref[...])
        a_halo = jnp.where(i > 0, _dot_nt(xh_ref[...], wa_ref[...]), 0.0)
        a = _conv_fwd(jnp.concatenate([a_halo, a_pre], axis=0), cw_ref, cb_ref, TM)
        b = _dot_nt(x_ref[...], wb_ref[...])
        cdf = 0.5 * (1.0 + lax.erf(a * (1.0 / math.sqrt(2.0))))
        pdf = jnp.exp(-0.5 * a * a) * (1.0 / math.sqrt(2.0 * math.pi))
        act = a * cdf
        apre_ref[...] = a_pre
        act_ref[...] = act
        gate_ref[...] = b * (cdf + a * pdf)
        f_ref[...] = (act * b).astype(MXU)

    blk = lambda r, c: pl.BlockSpec((None, r, c), lambda j, i: (j, 0, 0))
    tokj = pl.BlockSpec((None, TM, FF_BLK), lambda j, i: (j, i, 0))
    outs = [jax.ShapeDtypeStruct((N_SHARD, t, FF_BLK), F32)] * 3 + [jax.ShapeDtypeStruct((N_SHARD, t, FF_BLK), MXU)]
    return pl.pallas_call(
        body, name="ffn_in", grid=(N_SHARD, t // TM),
        in_specs=[pl.BlockSpec((TM, D_MODEL), lambda j, i: (i, 0)),
                  pl.BlockSpec((HALO, D_MODEL), lambda j, i: (jnp.maximum(i * hb - 1, 0), 0)),
                  blk(FF_BLK, D_MODEL), blk(FF_BLK, D_MODEL), blk(3, FF_BLK), blk(1, FF_BLK)],
        out_specs=[tokj, tokj, tokj, tokj], out_shape=outs,
        compiler_params=_cp(dimension_semantics=("arbitrary", "arbitrary")),
    )(x1b, x1b, w_a, w_b, conv_w, conv_b)


def _ffn_out_ln2(f, w_down, xhat1, ln1_g, ln1_b):
    t = xhat1.shape[0]

    def body(f_ref, wd_ref, xh_ref, g1_ref, b1_ref, xhat_ref, rstd_ref):
        half = TM // ROW_GROUPS
        for r0 in range(0, TM, half):
            rows = pl.ds(r0, half)
            ff = jnp.dot(f_ref[0, rows, :], wd_ref[0], preferred_element_type=F32)
            for j in range(1, N_SHARD):
                ff = ff + jnp.dot(f_ref[j, rows, :], wd_ref[j], preferred_element_type=F32)
            x1 = xh_ref[rows, :] * g1_ref[...] + b1_ref[...]
            xhat, rstd = _ln_fwd(ALPHA * x1 + ff)
            xhat_ref[rows, :] = xhat
            rstd_ref[rows, :] = rstd

    tok = lambda w: pl.BlockSpec((TM, w), lambda i: (i, 0))
    vec = _full((1, D_MODEL))
    outs = [jax.ShapeDtypeStruct((t, D_MODEL), F32), jax.ShapeDtypeStruct((t, 1), F32)]
    return pl.pallas_call(
        body, name="ffn_out_ln2", grid=(t // TM,),
        in_specs=[pl.BlockSpec((N_SHARD, TM, FF_BLK), lambda i: (0, i, 0)), _full(w_down.shape), tok(D_MODEL), vec, vec],
        out_specs=[tok(D_MODEL), tok(1)], out_shape=outs,
        compiler_params=_cp(dimension_semantics=("arbitrary",)),
    )(f, w_down, xhat1, ln1_g, ln1_b)


STAT_ROWS = 8


def _ple_loss_bwd(xhat2, rstd2, p, target, ln2_g, ln2_b, w_g, b_g, w_p, ln3_g, ln3_b):
    t = xhat2.shape[0]

    def body(xh2_ref, rs2_ref, p_ref, t_ref, g2_ref, b2_ref, wg_ref, bg_ref, wp_ref, g3_ref, b3_ref,
             dr2_ref, dr2b_ref, stat_ref, dwg_ref, dwp_ref, pp_scr, dwp_scr):
        @pl.when(pl.program_id(0) == 0)
        def _():
            stat_ref[...] = jnp.zeros_like(stat_ref)
            dwg_ref[...] = jnp.zeros_like(dwg_ref)
            dwp_scr[...] = jnp.zeros_like(dwp_scr)

        xhat2 = xh2_ref[...]
        x2 = xhat2 * g2_ref[...] + b2_ref[...]
        x2b = x2.astype(MXU)
        gate = jax.nn.sigmoid(jnp.dot(x2b, wg_ref[...], preferred_element_type=F32) + bg_ref[...])
        pb = p_ref[...].astype(MXU)
        for j in range(N_SHARD):
            pp_scr[:, j * ROW_BLK:(j + 1) * ROW_BLK] = jnp.dot(pb, wp_ref[j], preferred_element_type=F32)
        pp = pp_scr[...]
        xhat3, rstd3 = _ln_fwd(ALPHA * x2 + gate * pp)
        err = xhat3 * g3_ref[...] + b3_ref[...] - t_ref[...]
        dy = err * (1.0 / D_MODEL)
        dr3 = _ln_bwd(dy, xhat3, rstd3, g3_ref[...])
        dgp = dr3 * pp * gate * (1.0 - gate)
        dgp_b = dgp.astype(MXU)
        dwg_ref[...] += _dot_tn(x2b, dgp_b)
        dwp_scr[...] += _dot_tn(pb, dr3 * gate)
        dx2 = ALPHA * dr3 + _dot_nt(dgp_b, wg_ref[...])
        dr2 = _ln_bwd(dx2, xhat2, rs2_ref[...], g2_ref[...])
        dr2_ref[...] = dr2
        dr2b_ref[...] = dr2.astype(MXU)
        stat_ref[0:1, :] += _colsum(dy * xhat3)
        stat_ref[1:2, :] += _colsum(dy)
        stat_ref[2:3, :] += _colsum(dgp)
        stat_ref[3:4, :] += _colsum(dx2 * xhat2)
        stat_ref[4:5, :] += _colsum(dx2)
        stat_ref[5:6, :] += _colsum(err * err)

        @pl.when(pl.program_id(0) == t // TM - 1)
        def _():
            for j in range(N_SHARD):
                dwp_ref[j] = dwp_scr[:, j * ROW_BLK:(j + 1) * ROW_BLK]

    tok = lambda w: pl.BlockSpec((TM, w), lambda i: (i, 0))
    vec = _full((1, D_MODEL))
    outs = [jax.ShapeDtypeStruct((t, D_MODEL), F32), jax.ShapeDtypeStruct((t, D_MODEL), MXU),
            jax.ShapeDtypeStruct((STAT_ROWS, D_MODEL), F32), jax.ShapeDtypeStruct((D_MODEL, D_MODEL), F32),
            jax.ShapeDtypeStruct((N_SHARD, D_PLE, ROW_BLK), F32)]
    return pl.pallas_call(
        body, name="ple_loss_bwd", grid=(t // TM,),
        in_specs=[tok(D_MODEL), tok(1), tok(D_PLE), tok(D_MODEL), vec, vec, _full(w_g.shape), vec, _full(w_p.shape), vec, vec],
        out_specs=[tok(D_MODEL), tok(D_MODEL), _full((STAT_ROWS, D_MODEL)), _full((D_MODEL, D_MODEL)),
                   _full((N_SHARD, D_PLE, ROW_BLK))], out_shape=outs,
        scratch_shapes=[pltpu.VMEM((TM, D_MODEL), F32), pltpu.VMEM((D_PLE, D_MODEL), F32)],
        compiler_params=_cp(dimension_semantics=("arbitrary",)),
    )(xhat2, rstd2, p, target, ln2_g, ln2_b, w_g, b_g, w_p, ln3_g, ln3_b)


def _ffn_bwd(dr2, dr2b, a_pre, act, gate, w_down, w_a, w_b, conv_w, xhat1, rstd1, ln1_g, cat):
    t = dr2.shape[0]
    nt = t // TM
    hb = TM // HALO
    last_h = t // HALO - 1
    halo2 = 2 * HALO

    def body(dr_ref, drb_ref, drbn_ref, ap_ref, act_ref, gate_ref, gaten_ref, wd_ref, wa_ref, wb_ref, cw_ref,
             xh_ref, rs_ref, g1_ref, cat_ref, dap_ref, dbb_ref, dr1_ref, cstat_ref, lstat_ref, dwo_ref, acc_scr):
        i, j = pl.program_id(0), pl.program_id(1)

        @pl.when((i == 0) & (j == 0))
        def _():
            cstat_ref[...] = jnp.zeros_like(cstat_ref)
            lstat_ref[...] = jnp.zeros_like(lstat_ref)
            dwo_ref[...] = jnp.zeros_like(dwo_ref)

        half = TM // ROW_GROUPS
        parts = []
        for r0 in range(0, TM, half):
            rows = pl.ds(r0, half)
            last = r0 + half == TM

            def ext(ref, nxt):
                return jnp.concatenate([ref[rows], nxt[...]], axis=0) if last else ref[r0:r0 + half + HALO]

            drb = jnp.concatenate([drb_ref[rows, :], drbn_ref[...]], axis=0) if last else drb_ref[r0:r0 + half + halo2, :]
            df = _dot_nt(drb, wd_ref[...])[0:half + HALO]
            da = df * ext(gate_ref, gaten_ref)
            if last:
                da = jnp.concatenate([da[0:half], jnp.where(i < nt - 1, da[half:], 0.0)], axis=0)
            ahead = [da[0:half]] + [pltpu.roll(da, half + HALO - s, 0)[0:half] for s in (1, 2)]
            da_pre = cw_ref[2:3, :] * ahead[0] + cw_ref[1:2, :] * ahead[1] + cw_ref[0:1, :] * ahead[2]
            dbb = df[0:half] * act_ref[rows, :]
            dap_ref[rows, :] = da_pre.astype(MXU)
            dbb_ref[rows, :] = dbb.astype(MXU)
            for kk in range(3):
                cstat_ref[j, kk:kk + 1, :] += _colsum(ahead[2 - kk] * ap_ref[rows, :])
            cstat_ref[j, 3:4, :] += _colsum(ahead[0])
            parts.append(_dot(da_pre, wa_ref[...]) + _dot(dbb, wb_ref[...]))
        part = jnp.concatenate(parts, axis=0)

        @pl.when(j == 0)
        def _():
            acc_scr[...] = ALPHA * dr_ref[...] + part

        @pl.when(j > 0)
        def _():
            acc_scr[...] += part

        @pl.when(j == N_SHARD - 1)
        def _():
            dx1 = acc_scr[...]
            xhat1 = xh_ref[...]
            lstat_ref[0:1, :] += _colsum(dx1 * xhat1)
            lstat_ref[1:2, :] += _colsum(dx1)
            dr1 = _ln_bwd(dx1, xhat1, rs_ref[...], g1_ref[...])
            dr1_ref[...] = dr1
            dwo_ref[...] += _dot_tn(cat_ref[...], dr1)

    tok = lambda w: pl.BlockSpec((TM, w), lambda i, j: (i, 0))
    tokj = pl.BlockSpec((None, TM, FF_BLK), lambda i, j: (j, i, 0))
    nextj = pl.BlockSpec((None, HALO, FF_BLK), lambda i, j: (j, jnp.minimum((i + 1) * hb, last_h), 0))
    blk = lambda r, c: pl.BlockSpec((None, r, c), lambda i, j: (j, 0, 0))
    outs = [jax.ShapeDtypeStruct((N_SHARD, t, FF_BLK), MXU)] * 2 + [
        jax.ShapeDtypeStruct((t, D_MODEL), F32), jax.ShapeDtypeStruct((N_SHARD, STAT_ROWS, FF_BLK), F32),
        jax.ShapeDtypeStruct((STAT_ROWS, D_MODEL), F32), jax.ShapeDtypeStruct((D_MODEL, D_MODEL), F32)]
    return pl.pallas_call(
        body, name="ffn_bwd", grid=(nt, N_SHARD),
        in_specs=[tok(D_MODEL), tok(D_MODEL),
                  pl.BlockSpec((halo2, D_MODEL), lambda i, j: (jnp.minimum((i + 1) * (hb // 2), last_h // 2), 0)),
                  tokj, tokj, tokj, nextj, blk(FF_BLK, D_MODEL), blk(FF_BLK, D_MODEL), blk(FF_BLK, D_MODEL),
                  blk(3, FF_BLK), tok(D_MODEL), tok(1), _full((1, D_MODEL)), tok(D_MODEL)],
        out_specs=[tokj, tokj, tok(D_MODEL), _full((N_SHARD, STAT_ROWS, FF_BLK)), _full((STAT_ROWS, D_MODEL)),
                   _full((D_MODEL, D_MODEL))], out_shape=outs,
        scratch_shapes=[pltpu.VMEM((TM, D_MODEL), F32)],
        compiler_params=_cp(dimension_semantics=("arbitrary", "arbitrary")),
    )(dr2, dr2b, dr2b, a_pre, act, gate, gate, w_down, w_a, w_b, conv_w, xhat1, rstd1, ln1_g, cat)


def _mix_bwd(dr1, w_o, hu, hz, mixed, attn, ln_z_g, ln_z_b, w_s, dep):
    t = dr1.shape[0]
    nchunk = TM // BLK

    def body(dr_ref, wo_ref, hu_ref, hz_ref, mx_ref, attn_ref, g_ref, b_ref, ws_ref, grp_ref, red_ref, dep_ref,
             do1_ref, do4_ref, do16_ref, dl1_ref, dl4_ref, dl16_ref, duz_ref, dws_ref, dbs_ref, zstat_ref,
             wm_scr, dzn_scr, dbsum_scr, do_scr, dl_scr):
        @pl.when(pl.program_id(0) == 0)
        def _():
            row = lax.broadcasted_iota(jnp.int32, (BLK, BLK), 0)
            col = lax.broadcasted_iota(jnp.int32, (BLK, BLK), 1)
            for g in range(N_HEADS):
                wm_scr[g] = jnp.where(col <= row, ws_ref[g], 0.0).astype(MXU)
            dws_ref[...] = jnp.zeros_like(dws_ref)
            dbsum_scr[...] = jnp.zeros_like(dbsum_scr)
            zstat_ref[...] = jnp.zeros_like(zstat_ref)

        dcat = _dot_nt(dr_ref[...], wo_ref[...])
        dattn = dcat[:, 0:D_ATTN]
        do1_ref[...] = dattn.astype(MXU)
        for cc, val in enumerate(_chunks(dattn)):
            do_scr[cc] = val
        _to_planes(do4_ref, do_scr, DILATIONS[1], LANE_CHUNKS, MXU)
        _to_planes(do16_ref, do_scr, DILATIONS[2], LANE_CHUNKS, MXU)
        delta = _dot_select(dattn * attn_ref[...], red_ref[...])
        dl1_ref[...] = delta
        dl_scr[0] = delta
        _to_planes(dl4_ref, dl_scr, DILATIONS[1], 1, F32)
        _to_planes(dl16_ref, dl_scr, DILATIONS[2], 1, F32)
        dgm = dcat[:, D_ATTN:]
        hu, hz = hu_ref[...], hz_ref[...]
        u = _gelu(hu)
        duz_ref[:, 0:D_GMLP] = (dgm * mx_ref[...] * _gelu_grad(hu)).astype(MXU)
        dmixed = dgm * u
        dmb = dmixed.astype(MXU)
        zhat, rstd = _ln_fwd(_gelu(hz))
        znb = (zhat * g_ref[...] + b_ref[...]).astype(MXU)
        dbs_acc = jnp.zeros((BLK, D_GMLP), F32)
        for ch in range(nchunk):
            rows = slice(ch * BLK, (ch + 1) * BLK)
            dbs_acc = dbs_acc + dmixed[rows]
            for g in range(N_HEADS):
                cols = slice(g * HEAD_DIM, (g + 1) * HEAD_DIM)
                dzn_scr[rows, cols] = _dot_tn(wm_scr[g], dmb[rows, cols])
                dws_ref[g] += _dot_nt(dmb[rows, cols], znb[rows, cols])
        dbsum_scr[...] += dbs_acc
        dzn = dzn_scr[...]
        zstat_ref[0:1, :] += _colsum(dzn * zhat)
        zstat_ref[1:2, :] += _colsum(dzn)
        duz_ref[:, D_GMLP:] = (_ln_bwd(dzn, zhat, rstd, g_ref[...]) * _gelu_grad(hz)).astype(MXU)

        @pl.when(pl.program_id(0) == nt - 1)
        def _():
            row = lax.broadcasted_iota(jnp.int32, (BLK, BLK), 0)
            col = lax.broadcasted_iota(jnp.int32, (BLK, BLK), 1)
            for g in range(N_HEADS):
                dws_ref[g] = jnp.where(col <= row, dws_ref[g], 0.0)
            dbs_ref[...] = lax.dot_general(grp_ref[...], dbsum_scr[...], (((1,), (1,)), ((), ())),
                                           precision=lax.Precision.HIGHEST, preferred_element_type=F32)

    nt = t // TM
    tok = lambda w: pl.BlockSpec((TM, w), lambda i: (i, 0))
    grp = jnp.asarray((np.arange(D_GMLP)[None, :] // HEAD_DIM == np.arange(N_HEADS)[:, None]).astype(np.float32))
    red = _head_reduce()
    outs = [_perm_shape(t, d, D_ATTN, MXU) for d in DILATIONS] + [_perm_shape(t, d, 128, F32) for d in DILATIONS] + [
        jax.ShapeDtypeStruct((t, 2 * D_GMLP), MXU),
        jax.ShapeDtypeStruct((N_HEADS, BLK, BLK), F32), jax.ShapeDtypeStruct((N_HEADS, BLK), F32),
        jax.ShapeDtypeStruct((STAT_ROWS, D_GMLP), F32)]
    return pl.pallas_call(
        body, name="mix_bwd", grid=(t // TM,),
        in_specs=[tok(D_MODEL), _full(w_o.shape), tok(D_GMLP), tok(D_GMLP), tok(D_GMLP), tok(D_ATTN), _full(ln_z_g.shape),
                  _full(ln_z_b.shape), _full(w_s.shape), _full(grp.shape), _full(red.shape), pl.BlockSpec(memory_space=pl.ANY)],
        out_specs=[_perm_tile(d, D_ATTN) for d in DILATIONS] + [_perm_tile(d, 128) for d in DILATIONS]
        + [tok(2 * D_GMLP), _full((N_HEADS, BLK, BLK)), _full((N_HEADS, BLK)), _full((STAT_ROWS, D_GMLP))],
        out_shape=outs,
        scratch_shapes=[pltpu.VMEM((N_HEADS, BLK, BLK), MXU), pltpu.VMEM((TM, D_GMLP), F32), pltpu.VMEM((BLK, D_GMLP), F32),
                        pltpu.VMEM((LANE_CHUNKS, TM, 128), F32), pltpu.VMEM((1, TM, 128), F32)],
        compiler_params=_cp(dimension_semantics=("arbitrary",)),
    )(dr1, w_o, hu, hz, mixed, attn, ln_z_g, ln_z_b, w_s, grp, red, dep)


def _dx_in(dqs, dks, dvs, duz, dr1, w_in, c_tab, s1_tab, s2_tab):
    t = dr1.shape[0]

    def body(dq1, dq4, dq16, dk1, dk4, dk16, dv1, dv4, dv16, duz_ref, dr_ref, w_ref, c_ref, s1_ref, s2_ref,
             dh_ref, dx_ref, acc_scr):
        sums = []
        for part, (g1, g4, g16) in enumerate(((dq1, dq4, dq16), (dk1, dk4, dk16), (dv1, dv4, dv16))):
            acc = acc_scr.at[pl.ds(part * LANE_CHUNKS, LANE_CHUNKS)]
            for cc in range(LANE_CHUNKS):
                acc[cc] = g1[:, cc * 128:(cc + 1) * 128]
            _from_planes(g4, acc, DILATIONS[1], LANE_CHUNKS, accumulate=True)
            _from_planes(g16, acc, DILATIONS[2], LANE_CHUNKS, accumulate=True)
            sums.append(_unchunk(acc_scr, LANE_CHUNKS, part * LANE_CHUNKS))
        c, s1, s2 = _tile_heads(c_ref[...]), _tile_heads(s1_ref[...]), _tile_heads(s2_ref[...])
        dh_ref[:, 0:D_ATTN] = _rope_apply_t(sums[0] * (1.0 / math.sqrt(HEAD_DIM)), c, s1, s2).astype(MXU)
        dh_ref[:, D_ATTN:2 * D_ATTN] = _rope_apply_t(sums[1], c, s1, s2).astype(MXU)
        dh_ref[:, 2 * D_ATTN:3 * D_ATTN] = sums[2].astype(MXU)
        dh_ref[:, 3 * D_ATTN:] = duz_ref[...]
        dx = ALPHA * dr_ref[...]
        for j in range(N_SHARD):
            dx = dx + _dot_nt(dh_ref[:, j * W_IN_BLK:(j + 1) * W_IN_BLK], w_ref[j])
        dx_ref[...] = dx

    tok = lambda w: pl.BlockSpec((TM, w), lambda i: (i, 0))
    outs = [jax.ShapeDtypeStruct((t, D_IN), MXU), jax.ShapeDtypeStruct((t, D_MODEL), F32)]
    return pl.pallas_call(
        body, name="dx_in", grid=(t // TM,),
        in_specs=[_perm_tile(d, D_ATTN) for d in DILATIONS] * 3
        + [tok(2 * D_GMLP), tok(D_MODEL), _full(w_in.shape), tok(128), tok(128), tok(128)],
        out_specs=[tok(D_IN), tok(D_MODEL)], out_shape=outs,
        scratch_shapes=[pltpu.VMEM((3 * LANE_CHUNKS, TM, 128), F32)],
        compiler_params=_cp(dimension_semantics=("arbitrary",)),
    )(*dqs, *dks, *dvs, duz, dr1, w_in, c_tab, s1_tab, s2_tab)


def _wgrad(name, x, dy, x_spec, dy_spec, out_spec, out_shape, grid, dep=None):
    deps = [] if dep is None else [dep]

    def body(x_ref, dy_ref, *rest):
        rest[-1][...] = _dot_tn(x_ref[...], dy_ref[...])

    return pl.pallas_call(
        body, name=name, grid=grid, in_specs=[x_spec, dy_spec] + [pl.BlockSpec(memory_space=pl.ANY)] * len(deps),
        out_specs=out_spec, out_shape=jax.ShapeDtypeStruct(out_shape, F32),
        compiler_params=_cp(dimension_semantics=("arbitrary",) * len(grid)),
    )(x, dy, *deps)


def _wgrad_pair(name, xa, xb, dy, x_spec, dy_spec, out_spec, out_shape, grid):
    def body(xa_ref, xb_ref, dy_ref, oa_ref, ob_ref):
        dy = dy_ref[...]
        oa_ref[...] = _dot_tn(xa_ref[...], dy)
        ob_ref[...] = _dot_tn(xb_ref[...], dy)

    return pl.pallas_call(
        body, name=name, grid=grid, in_specs=[x_spec, x_spec, dy_spec], out_specs=[out_spec, out_spec],
        out_shape=[jax.ShapeDtypeStruct(out_shape, F32)] * 2,
        compiler_params=_cp(dimension_semantics=("arbitrary",) * len(grid)),
    )(xa, xb, dy)


def _local_step(x, p, rope, target, w_in, start_dep, late_landed, late_weights, early_grads, early_grads_sent,
                early_grads_landed,
                ln_z_g, ln_z_b, w_s, b_s, ln1_g, ln1_b, conv_b, ln2_g, ln2_b, b_g, ln3_g, ln3_b):
    t = x.shape[0]
    half = TM
    c_tab, s1_tab, s2_tab = rope
    b_full = jnp.repeat(jnp.transpose(b_s[0]), HEAD_DIM, axis=1)
    conv_b4 = conv_b.reshape(N_SHARD, 1, FF_BLK)
    *qkvs, hu, hz, mixed, gm, xb = _qkvuz(x, w_in, c_tab, s1_tab, s2_tab, ln_z_g, ln_z_b, w_s[0], b_full, start_dep)
    branches = [_attn_fwd(qkv, d, start_dep) for qkv, d in zip(qkvs[:2], DILATIONS[:2])]
    dep = late_landed(branches[-1][1])
    branches.append(_attn_fwd(qkvs[2], DILATIONS[2], dep))
    w_o, w_a, w_b, conv_w, w_down, w_g, w_p = late_weights(branches[-1][1])
    attn, *lses, cat, xhat1, rstd1, x1b = _mix_ln1(
        [o for o, _ in branches], [l for _, l in branches], gm, x, w_o, ln1_g, ln1_b, dep)
    a_pre, act, gate, f = _ffn_in(x1b, w_a, w_b, conv_w, conv_b4)
    xhat2, rstd2 = _ffn_out_ln2(f, w_down, xhat1, ln1_g, ln1_b)
    dr2, dr2b, stat3, g_w_g, g_w_p = _ple_loss_bwd(xhat2, rstd2, p, target, ln2_g, ln2_b, w_g, b_g, w_p, ln3_g, ln3_b)
    da_pre, dbb, dr1, cstat, stat1, g_w_o = _ffn_bwd(dr2, dr2b, a_pre, act, gate, w_down, w_a, w_b, conv_w, xhat1, rstd1,
                                                    ln1_g, cat)

    full_t = lambda w, im: pl.BlockSpec((t, w), im)
    ffj = pl.BlockSpec((None, t, FF_BLK), lambda j, kk: (j, 0, 0))
    early = dict(
        w_ple_gate=g_w_g, w_ple_in=g_w_p,
        w_ff_down=_wgrad("dw_down", f, dr2b, ffj, full_t(half, lambda j, n: (0, n)),
                         pl.BlockSpec((None, FF_BLK, half), lambda j, n: (j, 0, n)), (N_SHARD, FF_BLK, D_MODEL), (N_SHARD, 2)),
        **dict(zip(("w_ff_a", "w_ff_b"), _wgrad_pair(
            "dw_ab", da_pre, dbb, x1b, ffj, full_t(half, lambda j, n: (0, n)),
            pl.BlockSpec((None, FF_BLK, half), lambda j, n: (j, 0, n)), (N_SHARD, FF_BLK, D_MODEL), (N_SHARD, 2)))),
        w_o=g_w_o)
    dep = early_grads(early)

    do1, do4, do16, dl1, dl4, dl16, duz, dws, dbs, zstat = _mix_bwd(
        dr1, w_o, hu, hz, mixed, attn, ln_z_g, ln_z_b, w_s[0], dep)
    dep = early_grads_sent(duz, (stat3, stat1, zstat, cstat, dws, dbs))
    dqkv = [_attn_bwd(qkv, do, lse, dl, d, dep)
            for qkv, do, lse, dl, d in zip(qkvs, (do1, do4, do16), lses, (dl1, dl4, dl16), DILATIONS)]
    dh, grad_x = _dx_in([g[0] for g in dqkv], [g[1] for g in dqkv], [g[2] for g in dqkv], duz, dr1, w_in,
                        c_tab, s1_tab, s2_tab)
    dep = early_grads_landed(grad_x)
    g_w_in = _wgrad("dw_in", xb, dh, full_t(half, lambda j, kk: (0, kk)), full_t(W_IN_BLK, lambda j, kk: (0, j)),
                    pl.BlockSpec((None, half, W_IN_BLK), lambda j, kk: (j, kk, 0)), (N_SHARD, D_MODEL, W_IN_BLK), (N_SHARD, 2),
                    dep)
    return grad_x, g_w_in


def _tile_rows(rows, mult, steps):
    if rows % mult:
        return rows
    return next(rows // k for k in range(steps, rows + 1) if rows % k == 0 and (rows // k) % mult == 0)


def _grid_spec(grid, in_specs, out_specs):
    return pltpu.PrefetchScalarGridSpec(num_scalar_prefetch=1, grid=grid, in_specs=in_specs, out_specs=out_specs)


def _on_own_steps(i, count, steps, work):
    if count == steps:
        work()
    else:
        pl.when(i < count)(work)


def _place_shards(name, ws, dtypes, place, dep):
    n = len(ws)
    tiles = [_tile_rows(w.shape[0], 16, 2) for w in ws]
    counts = [w.shape[0] // t for w, t in zip(ws, tiles)]
    steps = max(counts)

    def body(s_ref, *refs):
        i = pl.program_id(0)
        for a in range(n):
            def work(a=a):
                refs[n + 1 + a][...] = refs[a][...].astype(dtypes[a])
            _on_own_steps(i, counts[a], steps, work)

    def tile(a, lead):
        last = counts[a] - 1
        if lead:
            return pl.BlockSpec((None, tiles[a], ws[a].shape[1]), lambda i, s: (s[0], jnp.minimum(i, last), 0))
        return pl.BlockSpec((tiles[a], ws[a].shape[1]), lambda i, s: (jnp.minimum(i, last), 0))

    return pl.pallas_call(
        body, name=name,
        grid_spec=_grid_spec((steps,), [tile(a, False) for a in range(n)] + [pl.BlockSpec(memory_space=pl.ANY)],
                             [tile(a, True) for a in range(n)]),
        out_shape=[jax.ShapeDtypeStruct((N_SHARD, *w.shape), dt) for w, dt in zip(ws, dtypes)],
        compiler_params=_cp())(place, *ws, dep)


def _pair_sums(name, mines, gots, place):
    n = len(mines)
    tiles = [_tile_rows(g.shape[1], 16, 1) for g in gots]
    per_blk = [g.shape[1] // t for g, t in zip(gots, tiles)]
    counts = [N_SHARD * nh for nh in per_blk]
    steps = max(counts)

    def body(s_ref, *refs):
        i = pl.program_id(0)
        for a in range(n):
            def work(a=a):
                refs[2 * n + a][...] = (refs[a][...] + refs[n + a][...]).astype(BF16)
            _on_own_steps(i, counts[a], steps, work)

    def tile(a, mine):
        nh, last = per_blk[a], counts[a] - 1

        def index(i, s):
            g = jnp.minimum(i, last)
            return (g // nh, (s[1] * nh if mine else 0) + g % nh, 0)

        return pl.BlockSpec((None, tiles[a], gots[a].shape[2]), index)

    return pl.pallas_call(
        body, name=name,
        grid_spec=_grid_spec((steps,), [tile(a, True) for a in range(n)] + [tile(a, False) for a in range(n)],
                             [tile(a, False) for a in range(n)]),
        out_shape=[jax.ShapeDtypeStruct(g.shape, BF16) for g in gots], compiler_params=_cp())(place, *mines, *gots)


def _chip_sums(name, owns, landeds, place, dep):
    n = len(owns)
    tiles = [_tile_rows(o.shape[1], 16, 4) for o in owns]
    counts = [o.shape[1] // t for o, t in zip(owns, tiles)]
    steps = max(counts)

    def body(s_ref, *refs):
        i = pl.program_id(0)
        for a in range(n):
            def work(a=a):
                own, l1, l2, l3 = (refs[4 * a + k][...].astype(F32) for k in range(4))
                refs[4 * n + 1 + a][...] = ((own + l1) + l2) + l3
            _on_own_steps(i, counts[a], steps, work)

    def slot(a, d):
        last = counts[a] - 1
        return pl.BlockSpec((None, tiles[a], owns[a].shape[2]), lambda i, s: ((s[0] + d) % N_SHARD, jnp.minimum(i, last), 0))

    def out(a):
        nh, last = counts[a], counts[a] - 1
        return pl.BlockSpec((tiles[a], owns[a].shape[2]), lambda i, s: (s[1] * nh + jnp.minimum(i, last), 0))

    operands = [x for o, l in zip(owns, landeds) for x in (o, l, l, l)]
    return pl.pallas_call(
        body, name=name,
        grid_spec=_grid_spec((steps,), [slot(a, d) for a in range(n) for d in range(4)] + [pl.BlockSpec(memory_space=pl.ANY)],
                             [out(a) for a in range(n)]),
        out_shape=[jax.ShapeDtypeStruct((2 * o.shape[1], o.shape[2]), F32) for o in owns],
        compiler_params=_cp())(place, *operands, dep)


def _adamw_math(w, g, m, v):
    m = ADAM_B1 * m + (1.0 - ADAM_B1) * g
    v = ADAM_B2 * v + (1.0 - ADAM_B2) * (g * g)
    m_hat = m / (1.0 - ADAM_B1 ** ADAM_STEP)
    v_hat = v / (1.0 - ADAM_B2 ** ADAM_STEP)
    delta = -ADAM_LR * (m_hat / (jnp.sqrt(v_hat) + ADAM_EPS) + ADAM_WD * w)
    return delta, m, v


def _adamw_shards(name, ws, gs, ms, vs):
    n = len(ws)
    tiles = [_tile_rows(w.shape[1], 8, 8 if n > 1 else 2) for w in ws]
    counts = [w.shape[1] // t for w, t in zip(ws, tiles)]
    steps = max(counts)

    def body(*refs):
        i = pl.program_id(0)
        for a in range(n):
            def work(a=a):
                w_ref, g_ref, m_ref, v_ref = refs[4 * a:4 * a + 4]
                d_ref, nm_ref, nv_ref = refs[4 * n + 3 * a:4 * n + 3 * a + 3]
                d_ref[...], nm_ref[...], nv_ref[...] = _adamw_math(w_ref[...], g_ref[...], m_ref[...], v_ref[...])
            _on_own_steps(i, counts[a], steps, work)

    def tile(a, lead):
        last, c = counts[a] - 1, ws[a].shape[2]
        if lead:
            return pl.BlockSpec((None, tiles[a], c), lambda i: (0, jnp.minimum(i, last), 0))
        return pl.BlockSpec((tiles[a], c), lambda i: (jnp.minimum(i, last), 0))

    res = pl.pallas_call(
        body, name=name, grid=(steps,),
        in_specs=[tile(a, lead) for a in range(n) for lead in (True, False, True, True)],
        out_specs=[tile(a, True) for a in range(n) for _ in range(3)],
        out_shape=[jax.ShapeDtypeStruct(w.shape, F32) for w in ws for _ in range(3)],
        compiler_params=_cp())(*[x for quad in zip(ws, gs, ms, vs) for x in quad])
    return [tuple(res[3 * a:3 * a + 3]) for a in range(n)]


MESH = pl.DeviceIdType.MESH
ANY = pl.BlockSpec(memory_space=pl.ANY)


def _place():
    x, y, c = lax.axis_index("x"), lax.axis_index("y"), lax.axis_index("c")
    chips = [(1 - x, y), (x, 1 - y), (1 - x, 1 - y)]
    return x, y, c, 2 * x + y, chips


def _remote(src, dst, send_sem, recv_sem, dev):
    return pltpu.make_async_remote_copy(src_ref=src, dst_ref=dst, send_sem=send_sem, recv_sem=recv_sem,
                                        device_id=dev, device_id_type=MESH)


def _half(ref, hc, rows):
    return ref.at[pl.ds(hc * (rows // 2), rows // 2)]


def _sibling_join(blocks, tag):
    n = len(blocks)

    def body(*refs):
        outs = refs[n:2 * n]
        send, recv = refs[2 * n:]
        x, y, c, _, _ = _place()
        cps = []
        for a in range(n):
            h = blocks[a].shape[0] // 2
            mine = outs[a].at[pl.ds(c * h, h)]
            cp = _remote(mine, mine, send.at[a], recv.at[a], (x, y, 1 - c))
            cp.start()
            cps.append(cp)
        for a, cp in enumerate(cps):
            h = blocks[a].shape[0] // 2
            theirs = outs[a].at[pl.ds((1 - c) * h, h)]
            _remote(theirs, theirs, send.at[a], recv.at[a], (x, y, 1 - c)).wait_recv()
            cp.wait_send()

    sem = pltpu.SemaphoreType.DMA
    return pl.pallas_call(body, name=f"rs_sibling_join_{tag}", in_specs=[ANY] * n, out_specs=[ANY] * n,
                          out_shape=[jax.ShapeDtypeStruct(b_.shape, b_.dtype) for b_ in blocks],
                          input_output_aliases={a: a for a in range(n)},
                          scratch_shapes=[sem((n,)), sem((n,))])(*blocks)


def _join_start(blocks, after, tag):
    n = len(blocks)

    def body(*refs):
        ins = refs[:n]
        send, recv = refs[n + 1], refs[n + 2]
        token = refs[2 * n + 3]
        x, y, c, _, _ = _place()
        for a in range(n):
            h = blocks[a].shape[0] // 2
            mine = ins[a].at[pl.ds(c * h, h)]
            _remote(mine, mine, send.at[a], recv.at[a], (x, y, 1 - c)).start()
        token[...] = jnp.zeros_like(token)

    sems = pltpu.SemaphoreType.DMA((n,))
    res = pl.pallas_call(
        body, name=f"join_start_{tag}", in_specs=[HBM] * n + [ANY],
        out_specs=[SEM, SEM] + [HBM] * n + [pl.BlockSpec(memory_space=pltpu.VMEM)],
        out_shape=[sems, sems] + [pltpu.HBM(b_.shape, b_.dtype) for b_ in blocks] + [TOKEN],
        input_output_aliases={a: a + 2 for a in range(n)}, compiler_params=_in_flight_params(),
    )(*[_in_hbm(b_) for b_ in blocks], after)
    return res[0], res[1], res[2:2 + n], res[2 + n]


def _join_wait(send, recv, blocks, after, tag):
    n = len(blocks)

    def body(*refs):
        ins = refs[:n]
        send_ref, recv_ref = refs[n], refs[n + 1]
        x, y, c, _, _ = _place()
        for a in range(n):
            h = blocks[a].shape[0] // 2
            mine, theirs = ins[a].at[pl.ds(c * h, h)], ins[a].at[pl.ds((1 - c) * h, h)]
            _remote(mine, mine, send_ref.at[a], recv_ref.at[a], (x, y, 1 - c)).wait_send()
            _remote(theirs, theirs, send_ref.at[a], recv_ref.at[a], (x, y, 1 - c)).wait_recv()

    return pl.pallas_call(
        body, name=f"join_wait_{tag}", in_specs=[HBM] * n + [SEM, SEM, ANY], out_specs=[HBM] * n,
        out_shape=[pltpu.HBM(b_.shape, b_.dtype) for b_ in blocks],
        input_output_aliases={a: a for a in range(n)}, compiler_params=_in_flight_params(),
    )(*blocks, send, recv, after)


HBM = pl.BlockSpec(memory_space=pltpu.HBM)
SEM = pl.BlockSpec(memory_space=pltpu.SEMAPHORE)
TOKEN = jax.ShapeDtypeStruct((8, 128), F32)


def _in_flight_params():
    return pltpu.CompilerParams(has_side_effects=pltpu.SideEffectType.DATAFLOW_SIDE_EFFECTING)


def _in_hbm(a):
    return pltpu.with_memory_space_constraint(a, pltpu.HBM)


def _gather_piece(ref, rows, split, slot, hc):
    return _half(ref.at[slot], hc, rows) if split else ref.at[slot]


def _gather_start(stacks, split, after, tag):
    n = len(stacks)

    def body(*refs):
        ins = refs[:n]
        send, recv = refs[n + 1], refs[n + 2]
        token = refs[2 * n + 3]
        _, _, c, j, chips = _place()
        for a in range(n):
            mine = _gather_piece(ins[a], stacks[a].shape[1], split[a], j, c)
            for t in range(3):
                _remote(mine, mine, send.at[3 * a + t], recv.at[3 * a + t], (*chips[t], c)).start()
        token[...] = jnp.zeros_like(token)

    sems = pltpu.SemaphoreType.DMA((3 * n,))
    res = pl.pallas_call(
        body, name=f"gather_start_{tag}", in_specs=[HBM] * n + [ANY],
        out_specs=[SEM, SEM] + [HBM] * n + [pl.BlockSpec(memory_space=pltpu.VMEM)],
        out_shape=[sems, sems] + [pltpu.HBM(s.shape, s.dtype) for s in stacks] + [TOKEN],
        input_output_aliases={a: a + 2 for a in range(n)}, compiler_params=_in_flight_params(),
    )(*[_in_hbm(s) for s in stacks], after)
    return res[0], res[1], res[2:2 + n], res[2 + n]


def _gather_wait(send, recv, stacks, split, after, tag):
    n = len(stacks)

    def body(*refs):
        ins = refs[:n]
        send_ref, recv_ref = refs[n], refs[n + 1]
        _, _, c, j, chips = _place()
        for a in range(n):
            rows = stacks[a].shape[1]
            mine = _gather_piece(ins[a], rows, split[a], j, c)
            for t, (px, py) in enumerate(chips):
                theirs = _gather_piece(ins[a], rows, split[a], 2 * px + py, c)
                _remote(mine, mine, send_ref.at[3 * a + t], recv_ref.at[3 * a + t], (px, py, c)).wait_send()
                _remote(theirs, theirs, send_ref.at[3 * a + t], recv_ref.at[3 * a + t], (px, py, c)).wait_recv()

    return pl.pallas_call(
        body, name=f"gather_wait_{tag}", in_specs=[HBM] * n + [SEM, SEM, ANY], out_specs=[HBM] * n,
        out_shape=[pltpu.HBM(s.shape, s.dtype) for s in stacks],
        input_output_aliases={a: a for a in range(n)}, compiler_params=_in_flight_params(),
    )(*stacks, send, recv, after)


def _gather_forward(stacks, split, tag):
    idx = [a for a in range(len(stacks)) if split[a]]
    n = len(idx)

    def body(*refs):
        outs = refs[n:2 * n]
        send, recv = refs[2 * n:]
        x, y, c, _, chips = _place()
        sends = []
        for t, (px, py) in enumerate(chips):
            for a in range(n):
                blk = _half(outs[a].at[2 * px + py], c, stacks[idx[a]].shape[1])
                cp = _remote(blk, blk, send.at[a, t], recv.at[a, t], (x, y, 1 - c))
                cp.start()
                sends.append(cp)
        for t, (px, py) in enumerate(chips):
            for a in range(n):
                blk = _half(outs[a].at[2 * px + py], 1 - c, stacks[idx[a]].shape[1])
                _remote(blk, blk, send.at[a, t], recv.at[a, t], (x, y, 1 - c)).wait_recv()
        for cp in sends:
            cp.wait_send()

    sem = pltpu.SemaphoreType.DMA
    res = pl.pallas_call(
        body, name=f"gather_forward_{tag}", in_specs=[ANY] * n, out_specs=[ANY] * n,
        out_shape=[jax.ShapeDtypeStruct(stacks[a].shape, stacks[a].dtype) for a in idx],
        input_output_aliases={a: a for a in range(n)}, scratch_shapes=[sem((n, 3)), sem((n, 3))],
    )(*[stacks[a] for a in idx])
    out = list(stacks)
    for a, r in zip(idx, res):
        out[a] = r
    return out


def _forward_start(stacks, after, tag):
    n = len(stacks)

    def body(*refs):
        ins = refs[:n]
        send, recv = refs[n + 1], refs[n + 2]
        token = refs[2 * n + 3]
        x, y, c, _, chips = _place()
        for a in range(n):
            for t, (px, py) in enumerate(chips):
                blk = _half(ins[a].at[2 * px + py], c, stacks[a].shape[1])
                _remote(blk, blk, send.at[3 * a + t], recv.at[3 * a + t], (x, y, 1 - c)).start()
        token[...] = jnp.zeros_like(token)

    sems = pltpu.SemaphoreType.DMA((3 * n,))
    res = pl.pallas_call(
        body, name=f"forward_start_{tag}", in_specs=[HBM] * n + [ANY],
        out_specs=[SEM, SEM] + [HBM] * n + [pl.BlockSpec(memory_space=pltpu.VMEM)],
        out_shape=[sems, sems] + [pltpu.HBM(s.shape, s.dtype) for s in stacks] + [TOKEN],
        input_output_aliases={a: a + 2 for a in range(n)}, compiler_params=_in_flight_params(),
    )(*[_in_hbm(s) for s in stacks], after)
    return res[0], res[1], res[2:2 + n], res[2 + n]


def _forward_wait(send, recv, stacks, after, tag):
    n = len(stacks)

    def body(*refs):
        ins = refs[:n]
        send_ref, recv_ref = refs[n], refs[n + 1]
        x, y, c, _, chips = _place()
        for a in range(n):
            for t, (px, py) in enumerate(chips):
                mine = _half(ins[a].at[2 * px + py], c, stacks[a].shape[1])
                theirs = _half(ins[a].at[2 * px + py], 1 - c, stacks[a].shape[1])
                _remote(mine, mine, send_ref.at[3 * a + t], recv_ref.at[3 * a + t], (x, y, 1 - c)).wait_send()
                _remote(theirs, theirs, send_ref.at[3 * a + t], recv_ref.at[3 * a + t], (x, y, 1 - c)).wait_recv()

    return pl.pallas_call(
        body, name=f"forward_wait_{tag}", in_specs=[HBM] * n + [SEM, SEM, ANY], out_specs=[HBM] * n,
        out_shape=[pltpu.HBM(s.shape, s.dtype) for s in stacks],
        input_output_aliases={a: a for a in range(n)}, compiler_params=_in_flight_params(),
    )(*stacks, send, recv, after)


def _swap_start(grads, tag):
    n = len(grads)

    def body(*refs):
        ins, gots = refs[:n], refs[n:2 * n]
        send, recv = refs[2 * n], refs[2 * n + 1]
        token = refs[4 * n + 2]
        x, y, c, _, _ = _place()
        for a in range(n):
            h = grads[a].shape[1] // 2
            _remote(ins[a].at[:, pl.ds((1 - c) * h, h)], gots[a], send.at[a], recv.at[a], (x, y, 1 - c)).start()
        token[...] = jnp.zeros_like(token)

    sems = pltpu.SemaphoreType.DMA((n,))
    halves = [(g.shape[0], g.shape[1] // 2, g.shape[2]) for g in grads]
    res = pl.pallas_call(
        body, name=f"swap_start_{tag}", in_specs=[HBM] * (2 * n),
        out_specs=[SEM, SEM] + [HBM] * (2 * n) + [pl.BlockSpec(memory_space=pltpu.VMEM)],
        out_shape=[sems, sems] + [pltpu.HBM(g.shape, g.dtype) for g in grads] + [pltpu.HBM(s, F32) for s in halves] + [TOKEN],
        input_output_aliases={a: a + 2 for a in range(2 * n)}, compiler_params=_in_flight_params(),
    )(*[_in_hbm(g) for g in grads], *[_in_hbm(lax.empty(s, F32)) for s in halves])
    return res[0], res[1], res[2:2 + n], res[2 + n:2 + 2 * n], res[2 + 2 * n]


def _swap_wait(send, recv, grads, gots, after, tag):
    n = len(grads)

    def body(*refs):
        ins, lnd = refs[:n], refs[n:2 * n]
        send_ref, recv_ref = refs[2 * n], refs[2 * n + 1]
        x, y, c, _, _ = _place()
        for a in range(n):
            h = grads[a].shape[1] // 2
            cp = _remote(ins[a].at[:, pl.ds((1 - c) * h, h)], lnd[a], send_ref.at[a], recv_ref.at[a], (x, y, 1 - c))
            cp.wait_send()
            cp.wait_recv()

    bufs = [pltpu.HBM(g.shape, g.dtype) for g in grads] + [pltpu.HBM(g.shape, g.dtype) for g in gots]
    res = pl.pallas_call(
        body, name=f"swap_wait_{tag}", in_specs=[HBM] * (2 * n) + [SEM, SEM, ANY], out_specs=[HBM] * (2 * n),
        out_shape=bufs, input_output_aliases={a: a for a in range(2 * n)}, compiler_params=_in_flight_params(),
    )(*grads, *gots, send, recv, after)
    return res[:n], res[n:]


def _exchange_start(parts, tag):
    n = len(parts)

    def body(*refs):
        ins, lands = refs[:n], refs[n:2 * n]
        send, recv = refs[2 * n], refs[2 * n + 1]
        token = refs[4 * n + 2]
        _, _, c, j, chips = _place()
        for t, (px, py) in enumerate(chips):
            for a in range(n):
                _remote(ins[a].at[2 * px + py], lands[a].at[j], send.at[3 * a + t], recv.at[3 * a + t], (px, py, c)).start()
        token[...] = jnp.zeros_like(token)

    sems = pltpu.SemaphoreType.DMA((3 * n,))
    bufs = [pltpu.HBM(p.shape, p.dtype) for p in parts]
    res = pl.pallas_call(
        body, name=f"exchange_start_{tag}", in_specs=[HBM] * (2 * n),
        out_specs=[SEM, SEM] + [HBM] * (2 * n) + [pl.BlockSpec(memory_space=pltpu.VMEM)],
        out_shape=[sems, sems] + bufs + bufs + [TOKEN],
        input_output_aliases={a: a + 2 for a in range(2 * n)}, compiler_params=_in_flight_params(),
    )(*[_in_hbm(p) for p in parts], *[_in_hbm(lax.empty(p.shape, p.dtype)) for p in parts])
    return res[0], res[1], res[2:2 + n], res[2 + n:2 + 2 * n], res[2 + 2 * n]


def _exchange_wait(send, recv, parts, lands, after, tag):
    n = len(parts)

    def body(*refs):
        ins, lnd = refs[:n], refs[n:2 * n]
        send_ref, recv_ref = refs[2 * n], refs[2 * n + 1]
        _, _, c, j, chips = _place()
        for t, (px, py) in enumerate(chips):
            jt = 2 * px + py
            for a in range(n):
                _remote(ins[a].at[jt], lnd[a].at[j], send_ref.at[3 * a + t], recv_ref.at[3 * a + t], (px, py, c)).wait_send()
                _remote(ins[a].at[jt], lnd[a].at[jt], send_ref.at[3 * a + t], recv_ref.at[3 * a + t], (px, py, c)).wait_recv()

    bufs = [pltpu.HBM(p.shape, p.dtype) for p in parts]
    res = pl.pallas_call(
        body, name=f"exchange_wait_{tag}", in_specs=[HBM] * (2 * n) + [SEM, SEM, ANY], out_specs=[HBM] * (2 * n),
        out_shape=bufs + bufs, input_output_aliases={a: a for a in range(2 * n)}, compiler_params=_in_flight_params(),
    )(*parts, *lands, send, recv, after)
    return res[:n], res[n:]


def _small_chip_sums(arrs):
    n = len(arrs)

    def body(*refs):
        ins, outs = refs[:n], refs[n:2 * n]
        sib = refs[2 * n:3 * n]
        send, recv = refs[3 * n:]
        x, y, c, j, _ = _place()
        swaps = [_remote(ins[a], sib[a], send.at[a], recv.at[a], (x, y, 1 - c)) for a in range(n)]
        for cp in swaps:
            cp.start()
        for a in range(n):
            swaps[a].wait_recv()
            outs[a][j] = ins[a][...] + sib[a][...]
        for cp in swaps:
            cp.wait_send()

    sem = pltpu.SemaphoreType.DMA
    vm = pl.BlockSpec(memory_space=pltpu.VMEM)
    return pl.pallas_call(
        body, name="small_chip_sums", in_specs=[vm] * n, out_specs=[vm] * n,
        out_shape=[jax.ShapeDtypeStruct((N_SHARD, *a.shape), F32) for a in arrs],
        scratch_shapes=[pltpu.VMEM(a.shape, F32) for a in arrs] + [sem((n,)), sem((n,))],
        compiler_params=_cp(),
    )(*arrs)


def _small_totals(stacks):
    n = len(stacks)

    def body(*refs):
        for a in range(n):
            refs[n + a][...] = ((refs[a][0] + refs[a][1]) + refs[a][2]) + refs[a][3]

    return pl.pallas_call(body, name="small_totals", out_shape=[jax.ShapeDtypeStruct(s.shape[1:], F32) for s in stacks],
                          compiler_params=_cp())(*stacks)


SMALL_1024 = ("ln1_g", "ln1_b", "ln2_g", "ln2_b", "b_ple_gate", "ln3_g", "ln3_b")


def _adamw_small(red3, red1, redz, g_conv_w, redc, red_ws, red_bs, params):
    shape2d = {"ln_z_g": (1, D_GMLP), "ln_z_b": (1, D_GMLP), "w_s": (N_HEADS * BLK, BLK), "b_s": (N_HEADS, BLK),
               "conv_w": (3, FF_BLK), "conv_b": (N_SHARD, FF_BLK), **{k: (1, D_MODEL) for k in SMALL_1024}}
    names = list(shape2d)
    flat = [a.reshape(shape2d[k]) for k in names for a in params[k]]

    def body(r3, r1, rz, gcw, rc, rws, rbs, *refs):
        ins, outs = refs[:3 * len(names)], refs[3 * len(names):]

        def grad_of(k):
            if k == "w_s":
                return rws[...]
            if k == "b_s":
                return rbs[...]
            if k == "conv_w":
                return gcw[0:3, :]
            if k == "conv_b":
                return jnp.concatenate([rc[j * STAT_ROWS + 3:j * STAT_ROWS + 4, :] for j in range(N_SHARD)], axis=0)
            src, row = {"ln3_g": (r3, 0), "ln3_b": (r3, 1), "b_ple_gate": (r3, 2), "ln2_g": (r3, 3), "ln2_b": (r3, 4),
                        "ln1_g": (r1, 0), "ln1_b": (r1, 1), "ln_z_g": (rz, 0), "ln_z_b": (rz, 1)}[k]
            return src[row:row + 1, :]

        for i, k in enumerate(names):
            w_ref, m_ref, v_ref = ins[3 * i:3 * i + 3]
            g_ref, d_ref, nm_ref, nv_ref = outs[4 * i:4 * i + 4]
            g = grad_of(k)
            g_ref[...] = g
            d_ref[...], nm_ref[...], nv_ref[...] = _adamw_math(w_ref[...], g, m_ref[...], v_ref[...])

    res = pl.pallas_call(
        body, name="adamw_small",
        out_shape=[jax.ShapeDtypeStruct(shape2d[k], F32) for k in names for _ in range(4)],
        compiler_params=_cp(),
    )(red3, red1, redz, g_conv_w, redc, red_ws, red_bs, *flat)
    return {k: tuple(r.reshape(params[k][0].shape) for r in res[4 * i:4 * i + 4]) for i, k in enumerate(names)}


WEIGHTS = ("w_in", "ln_z_g", "ln_z_b", "w_s", "b_s", "w_o", "ln1_g", "ln1_b", "w_ff_a", "w_ff_b", "conv_w", "conv_b",
           "w_ff_down", "ln2_g", "ln2_b", "w_ple_gate", "b_ple_gate", "w_ple_in", "ln3_g", "ln3_b")
BIG = ("w_in", "w_o", "w_ff_a", "w_ff_b", "w_ff_down", "w_ple_gate", "w_ple_in")
TRANSPOSED = ("w_ff_a", "w_ff_b")
LATE = ("w_o", "w_ff_a", "w_ff_b", "w_ff_down", "w_ple_gate", "w_ple_in", "conv_w")


def kernel(x, p, positions, w_in, ln_z_g, ln_z_b, w_s, b_s, w_o, ln1_g, ln1_b, w_ff_a, w_ff_b, conv_w, conv_b, w_ff_down, ln2_g, ln2_b, w_ple_gate, b_ple_gate, w_ple_in, ln3_g, ln3_b, loss_target, m_w_in, m_ln_z_g, m_ln_z_b, m_w_s, m_b_s, m_w_o, m_ln1_g, m_ln1_b, m_w_ff_a, m_w_ff_b, m_conv_w, m_conv_b, m_w_ff_down, m_ln2_g, m_ln2_b, m_w_ple_gate, m_b_ple_gate, m_w_ple_in, m_ln3_g, m_ln3_b, v_w_in, v_ln_z_g, v_ln_z_b, v_w_s, v_b_s, v_w_o, v_ln1_g, v_ln1_b, v_w_ff_a, v_w_ff_b, v_conv_w, v_conv_b, v_w_ff_down, v_ln2_g, v_ln2_b, v_w_ple_gate, v_b_ple_gate, v_w_ple_in, v_ln3_g, v_ln3_b):
    args = locals()
    w = {k: args[k] for k in WEIGHTS}
    m = {k: args["m_" + k] for k in WEIGHTS}
    v = {k: args["v_" + k] for k in WEIGHTS}

    for k in TRANSPOSED:
        w[k], m[k], v[k] = (jnp.swapaxes(a, 1, 2) for a in (w[k], m[k], v[k]))

    chip = 2 * lax.axis_index("x") + lax.axis_index("y")
    place = jnp.stack([chip, lax.axis_index("c")]).astype(jnp.int32)
    stack = dict(zip(["w_in"], _place_shards("cast_w_in", [w["w_in"][0]], [MXU], place, place)))
    i_send, i_recv, in_flight, dep = _gather_start([stack["w_in"]], [True], place, "w_in")
    stack.update(zip(LATE, _place_shards("cast_late", [w[k][0] for k in LATE],
                                         [F32 if k == "conv_w" else MXU for k in LATE], place, dep)))
    split_late = [k != "conv_w" for k in LATE]
    g_send, g_recv, late_flight, start_dep = _gather_start([stack[k] for k in LATE], split_late, place, "late")
    rope = _rope_tables(positions, x.shape[1], start_dep)
    landed_in = _gather_wait(i_send, i_recv, in_flight, [True], rope[0], "w_in")
    w_in_full, = _gather_forward(landed_in, [True], "w_in")
    halves =[k for k, sp in zip(LATE, split_late) if sp]
    trips = {}

    def late_landed(after):
        fw = dict(zip(LATE, _gather_wait(g_send, g_recv, late_flight, split_late, after, "late")))
        trips["late"] = (fw, *_forward_start([fw[k] for k in halves], fw["conv_w"], "late"))
        return trips["late"][-1]

    def late_weights(after):
        fw, send, recv, flight, _ = trips["late"]
        fw.update(zip(halves, _forward_wait(send, recv, flight, after, "late")))
        return (fw["w_o"].reshape(D_MODEL, D_MODEL), fw["w_ff_a"], fw["w_ff_b"], fw["conv_w"], fw["w_ff_down"],
                fw["w_ple_gate"].reshape(D_MODEL, D_MODEL), fw["w_ple_in"])

    def swap_started(names, grads, tag):
        stacked = [g.reshape(N_SHARD, *w[k].shape[1:]) for k, g in zip(names, grads)]
        return (names, tag, *_swap_start(stacked, tag))

    def partial_sums(swap, after):
        names, tag, send, recv, stacked, gots, _ = swap
        stacked, got = _swap_wait(send, recv, stacked, gots, after, tag)
        pair = _pair_sums(f"rs_pair_{tag}", stacked, got, place)
        return (names, tag, *_exchange_start(pair, tag))

    def chip_summed(trip, after, dep):
        names, tag, send, recv, pair, lands, _ = trip
        pair, landed = _exchange_wait(send, recv, pair, lands, after, tag)
        return _chip_sums(f"rs_sum_{tag}", pair, landed, place, dep), names, tag

    def reduced(trip, after, dep):
        blocks, names, tag = chip_summed(trip, after, dep)
        return dict(zip(names, _sibling_join(blocks, tag)))

    def early_grads_landed(after):
        blocks, names, tag = chip_summed(trips["early"], after, trips["small"][-1])
        trips["join"] = (names, *_join_start(blocks, after, tag))
        return trips["join"][-1]

    def early_grads(grads):
        trips["swap"] = swap_started(list(grads), list(grads.values()), "early")
        return trips["swap"][-1]

    def early_grads_sent(after, small):
        trips["early"] = partial_sums(trips["swap"], after)
        stat3, stat1, zstat, cstat, dws, dbs = small
        sums = _small_chip_sums([stat3, stat1, zstat, cstat.reshape(N_SHARD * STAT_ROWS, FF_BLK),
                                 dws.reshape(N_HEADS * BLK, BLK), dbs])
        trips["small"] = _gather_start(sums, [False] * len(sums), trips["early"][-1], "small")
        return trips["small"][-1]

    grad_x, g_w_in = _local_step(
        x[0], p[0, 0], rope, loss_target[0], w_in_full, start_dep, late_landed, late_weights, early_grads, early_grads_sent,
        early_grads_landed, ln_z_g, ln_z_b, w_s, b_s, ln1_g, ln1_b, conv_b, ln2_g, ln2_b, b_ple_gate, ln3_g, ln3_b)

    swap_in = swap_started(["w_in"], [g_w_in], "w_in")
    trips["w_in"] = partial_sums(swap_in, swap_in[-1])
    out = {}

    def adamw(red, tag):
        names = list(red)
        steps = _adamw_shards(f"adamw_{tag}", [w[k] for k in names], [red[k] for k in names], [m[k] for k in names],
                              [v[k] for k in names])
        for k, (d, nm, nv) in zip(names, steps):
            out[k] = (red[k].reshape(w[k].shape), d, nm, nv)

    names, j_send, j_recv, j_flight, _ = trips["join"]
    adamw(dict(zip(names, _join_wait(j_send, j_recv, j_flight, trips["w_in"][-1], "early"))), "early")
    adamw(reduced(trips["w_in"], out["w_o"][3], start_dep), "w_in")
    for k in TRANSPOSED:
        out[k] = tuple(jnp.swapaxes(a, 1, 2) for a in out[k])

    s_send, s_recv, s_flight, _ = trips["small"]
    red3, red1, redz, redc, red_ws, red_bs = _small_totals(
        _gather_wait(s_send, s_recv, s_flight, [False] * len(s_flight), out["w_in"][3], "small"))
    loss = (0.5 / D_MODEL) * jnp.sum(red3[5])
    g_conv_w = lax.dynamic_slice_in_dim(redc, chip * STAT_ROWS, STAT_ROWS, 0)
    names_small = [k for k in WEIGHTS if k not in BIG]
    out.update(_adamw_small(red3, red1, redz, g_conv_w, redc, red_ws, red_bs, {k: (w[k], m[k], v[k]) for k in names_small}))

    return (loss, grad_x[None], *[out[k][0] for k in WEIGHTS], *[out[k][1] for k in WEIGHTS],
            *[out[k][2] for k in WEIGHTS], *[out[k][3] for k in WEIGHTS])
```

```python
import math

import numpy as np
import jax
import jax.numpy as jnp
from jax import lax
from jax.experimental import pallas as pl
from jax.experimental.pallas import tpu as pltpu

F32 = jnp.float32
BF16 = jnp.bfloat16
MXU = BF16

D_MODEL = 1024
HEAD_DIM = 64
N_HEADS = 8
D_ATTN = 512
D_GMLP = 512
D_IN = 2560
DILATIONS = (1, 4, 16)
BLK = 128
ROPE_THETA = 500000.0
ROPE_DIM = 16
D_FF = 2816
D_PLE = 256
LN_EPS = 1e-5
ALPHA = 2.0 ** 0.25
NEG_INF = -1e30
N_SHARD = 4
W_IN_BLK = D_IN // N_SHARD
FF_BLK = D_FF // N_SHARD
ROW_BLK = D_MODEL // N_SHARD
ADAM_LR, ADAM_B1, ADAM_B2, ADAM_EPS, ADAM_WD, ADAM_STEP = 0.001, 0.9, 0.999, 1e-08, 0.01, 10

TM = 512
HALO = 8
ROW_GROUPS = 2
VMEM_LIMIT = 56 * 1024 * 1024


def _cp(**kw):
    return pltpu.CompilerParams(vmem_limit_bytes=VMEM_LIMIT, **kw)


def _full(shape):
    n = len(shape)
    return pl.BlockSpec(shape, lambda *_: (0,) * n)


def _gelu(x):
    return 0.5 * x * (1.0 + lax.erf(x * (1.0 / math.sqrt(2.0))))


def _gelu_grad(x):
    return 0.5 * (1.0 + lax.erf(x * (1.0 / math.sqrt(2.0)))) + x * jnp.exp(-0.5 * x * x) * (1.0 / math.sqrt(2.0 * math.pi))


def _ln_fwd(r):
    mu = jnp.mean(r, axis=-1, keepdims=True)
    xc = r - mu
    var = jnp.mean(xc * xc, axis=-1, keepdims=True)
    rstd = lax.rsqrt(var + LN_EPS)
    return xc * rstd, rstd


def _ln_bwd(dy, xhat, rstd, g):
    dxh = dy * g
    m1 = jnp.mean(dxh, axis=-1, keepdims=True)
    m2 = jnp.mean(dxh * xhat, axis=-1, keepdims=True)
    return rstd * (dxh - m1 - xhat * m2)


def _dot(a, b):
    return jnp.dot(a.astype(MXU), b.astype(MXU), preferred_element_type=F32)


def _dot_nt(a, b):
    return lax.dot_general(a.astype(MXU), b.astype(MXU), (((1,), (1,)), ((), ())), preferred_element_type=F32)


def _dot_tn(a, b):
    return lax.dot_general(a.astype(MXU), b.astype(MXU), (((0,), (0,)), ((), ())), preferred_element_type=F32)


def _colsum(v):
    return jnp.sum(v, axis=0, keepdims=True)


def _rope_tables(positions, t, dep):
    inv = np.float32(ROPE_THETA) ** (-np.arange(0, ROPE_DIM, 2, dtype=np.float32) / np.float32(ROPE_DIM))
    half = ROPE_DIM // 2
    pos_rep = jnp.repeat(positions.reshape(t // 16, 16), half, axis=1)
    inv_row = jnp.asarray(np.tile(inv, 16)[None, :], F32)

    def trig_body(pos_ref, inv_ref, dep_ref, cos_ref, sin_ref):
        ang = pos_ref[...].astype(F32) * inv_ref[...]
        cos_ref[...] = jnp.cos(ang)
        sin_ref[...] = jnp.sin(ang)

    vm = pl.BlockSpec(memory_space=pltpu.VMEM)
    cos8, sin8 = pl.pallas_call(
        trig_body, name="rope_trig", in_specs=[vm, vm, pl.BlockSpec(memory_space=pl.ANY)], out_specs=[vm, vm],
        out_shape=(jax.ShapeDtypeStruct((t // 16, 128), F32), jax.ShapeDtypeStruct((t // 16, 128), F32)),
    )(pos_rep, inv_row, dep)
    cos8 = cos8.reshape(t, half)
    sin8 = sin8.reshape(t, half)

    lane = np.arange(128) % HEAD_DIM
    sel = (np.arange(half)[:, None] == (lane % half)[None, :])
    e_cos = (sel & (lane < ROPE_DIM)[None, :]).astype(np.float32)
    e_s1 = -(sel & (lane < half)[None, :]).astype(np.float32)
    e_s2 = (sel & ((lane >= half) & (lane < ROPE_DIM))[None, :]).astype(np.float32)
    ones = (lane >= ROPE_DIM).astype(np.float32)[None, :]

    def expand_body(cos_ref, sin_ref, ec_ref, e1_ref, e2_ref, ones_ref, c_ref, s1_ref, s2_ref):
        hp = lax.Precision.HIGHEST
        c_ref[...] = jnp.dot(cos_ref[...], ec_ref[...], precision=hp, preferred_element_type=F32) + ones_ref[...]
        s1_ref[...] = jnp.dot(sin_ref[...], e1_ref[...], precision=hp, preferred_element_type=F32)
        s2_ref[...] = jnp.dot(sin_ref[...], e2_ref[...], precision=hp, preferred_element_type=F32)

    tab = jax.ShapeDtypeStruct((t, 128), F32)
    return pl.pallas_call(expand_body, name="rope_expand", out_shape=(tab, tab, tab), compiler_params=_cp())(
        cos8, sin8, jnp.asarray(e_cos), jnp.asarray(e_s1), jnp.asarray(e_s2), jnp.asarray(ones))


def _tile_heads(tab):
    return jnp.concatenate([tab] * (D_ATTN // 128), axis=1)


def _rope_apply(v, c, s1, s2):
    n = v.shape[1]
    half = ROPE_DIM // 2
    return v * c + pltpu.roll(v, n - half, 1) * s1 + pltpu.roll(v, half, 1) * s2


def _rope_apply_t(g, c, s1, s2):
    n = g.shape[1]
    half = ROPE_DIM // 2
    return g * c + pltpu.roll(g * s1, half, 1) + pltpu.roll(g * s2, n - half, 1)


LANE_CHUNKS = D_ATTN // 128
HEAD_LANES = 128 // N_HEADS


def _perm_shape(t, d, w, dtype):
    return jax.ShapeDtypeStruct((d, t // d, w), dtype)


def _perm_tile(d, w):
    return pl.BlockSpec((None if d == 1 else d, TM // d, w), lambda i: (0, i, 0))


def _to_planes(ref, scr, d, n_chunks, dtype):
    for r in range(d):
        for cc in range(n_chunks):
            ref[r, :, cc * 128:(cc + 1) * 128] = scr.at[cc][pl.ds(r, TM // d, stride=d), :].astype(dtype)


def _from_planes(ref, scr, d, n_chunks, accumulate=False):
    for r in range(d):
        for cc in range(n_chunks):
            rows = scr.at[cc]
            val = ref[r, :, cc * 128:(cc + 1) * 128].astype(F32)
            if accumulate:
                rows[pl.ds(r, TM // d, stride=d), :] += val
            else:
                rows[pl.ds(r, TM // d, stride=d), :] = val


def _chunks(val):
    return [val[:, cc * 128:(cc + 1) * 128] for cc in range(val.shape[1] // 128)]


def _unchunk(scr, n_chunks, base=0):
    return jnp.concatenate([scr[base + cc] for cc in range(n_chunks)], axis=1)


def _head_expand():
    src = np.arange(128)[:, None]
    dst = np.arange(D_ATTN)[None, :]
    return jnp.asarray((src == (dst // HEAD_DIM) * HEAD_LANES).astype(np.float32))


def _head_reduce():
    src = np.arange(D_ATTN)[:, None]
    dst = np.arange(128)[None, :]
    return jnp.asarray((src // HEAD_DIM == dst // HEAD_LANES).astype(np.float32))


def _dot_select(a, sel):
    hi = a.astype(BF16)
    lo = (a - hi.astype(F32)).astype(BF16)
    sel = sel.astype(BF16)
    return jnp.dot(hi, sel, preferred_element_type=F32) + jnp.dot(lo, sel, preferred_element_type=F32)


def _qkvuz(x, w_in, c_tab, s1_tab, s2_tab, ln_z_g, ln_z_b, w_s, b_full, dep):
    t = x.shape[0]
    nchunk = TM // BLK

    def body(x_ref, w_ref, c_ref, s1_ref, s2_ref, g_ref, b_ref, ws_ref, bf_ref, dep_ref,
             qkv1_ref, qkv4_ref, qkv16_ref, hu_ref, hz_ref, mixed_ref, gm_ref, xb_ref, h_scr, wm_scr, p_scr):
        @pl.when(pl.program_id(0) == 0)
        def _():
            row = lax.broadcasted_iota(jnp.int32, (BLK, BLK), 0)
            col = lax.broadcasted_iota(jnp.int32, (BLK, BLK), 1)
            for g in range(N_HEADS):
                wm_scr[g] = jnp.where(col <= row, ws_ref[g], 0.0).astype(MXU)

        xb = x_ref[...].astype(MXU)
        xb_ref[...] = xb
        for j in range(N_SHARD):
            h_scr[:, j * W_IN_BLK:(j + 1) * W_IN_BLK] = jnp.dot(xb, w_ref[j], preferred_element_type=F32)
        c, s1, s2 = _tile_heads(c_ref[...]), _tile_heads(s1_ref[...]), _tile_heads(s2_ref[...])
        q = _rope_apply(h_scr[:, 0:D_ATTN], c, s1, s2) * (1.0 / math.sqrt(HEAD_DIM))
        k = _rope_apply(h_scr[:, D_ATTN:2 * D_ATTN], c, s1, s2)
        for part, val in enumerate((q, k, h_scr[:, 2 * D_ATTN:3 * D_ATTN])):
            qkv1_ref[:, part * D_ATTN:(part + 1) * D_ATTN] = val.astype(MXU)
            for cc in range(LANE_CHUNKS):
                p_scr[part * LANE_CHUNKS + cc] = val[:, cc * 128:(cc + 1) * 128]
        _to_planes(qkv4_ref, p_scr, DILATIONS[1], 3 * LANE_CHUNKS, MXU)
        _to_planes(qkv16_ref, p_scr, DILATIONS[2], 3 * LANE_CHUNKS, MXU)
        hu = h_scr[:, 3 * D_ATTN:3 * D_ATTN + D_GMLP]
        hz = h_scr[:, 3 * D_ATTN + D_GMLP:]
        hu_ref[...] = hu
        hz_ref[...] = hz
        zhat, _ = _ln_fwd(_gelu(hz))
        zn = (zhat * g_ref[...] + b_ref[...]).astype(MXU)
        for ch in range(nchunk):
            rows = slice(ch * BLK, (ch + 1) * BLK)
            for g in range(N_HEADS):
                cols = slice(g * HEAD_DIM, (g + 1) * HEAD_DIM)
                mixed_ref[rows, cols] = jnp.dot(wm_scr[g], zn[rows, cols], preferred_element_type=F32) + bf_ref[:, cols]
        gm_ref[...] = (_gelu(hu) * mixed_ref[...]).astype(MXU)

    tok = lambda w: pl.BlockSpec((TM, w), lambda i: (i, 0))
    outs = [_perm_shape(t, d, 3 * D_ATTN, MXU) for d in DILATIONS] + [jax.ShapeDtypeStruct((t, D_GMLP), F32)] * 3 + [
        jax.ShapeDtypeStruct((t, D_GMLP), MXU), jax.ShapeDtypeStruct((t, D_MODEL), MXU)]
    return pl.pallas_call(
        body, name="qkvuz", grid=(t // TM,),
        in_specs=[tok(D_MODEL), _full(w_in.shape), tok(128), tok(128), tok(128), _full(ln_z_g.shape), _full(ln_z_b.shape),
                  _full(w_s.shape), _full(b_full.shape), pl.BlockSpec(memory_space=pl.ANY)],
        out_specs=[_perm_tile(d, 3 * D_ATTN) for d in DILATIONS] + [tok(D_ATTN)] * 4 + [tok(D_MODEL)], out_shape=outs,
        scratch_shapes=[pltpu.VMEM((TM, D_IN), F32), pltpu.VMEM((N_HEADS, BLK, BLK), MXU),
                        pltpu.VMEM((3 * LANE_CHUNKS, TM, 128), F32)],
        compiler_params=_cp(dimension_semantics=("arbitrary",)),
    )(x, w_in, c_tab, s1_tab, s2_tab, ln_z_g, ln_z_b, w_s, b_full, dep)


def _band_valid(n):
    i = lax.broadcasted_iota(jnp.int32, (BLK, 2 * BLK), 0)
    j = lax.broadcasted_iota(jnp.int32, (BLK, 2 * BLK), 1)
    return (j >= i) & (j <= i + BLK) & ((j >= BLK) | (n > 0))


def _attn_fwd(qkv, d, dep):
    _, l_sub, _ = qkv.shape
    nb = l_sub // BLK

    def body(q_ref, kp_ref, kc_ref, vp_ref, vc_ref, dep_ref, o_ref, l_ref):
        valid = _band_valid(pl.program_id(1))
        kcat = jnp.concatenate([kp_ref[...], kc_ref[...]], axis=0)
        vcat = jnp.concatenate([vp_ref[...], vc_ref[...]], axis=0)
        for h in range(N_HEADS):
            cols = slice(h * HEAD_DIM, (h + 1) * HEAD_DIM)
            s = jnp.where(valid, _dot_nt(q_ref[:, cols], kcat[:, cols]), NEG_INF)
            m = jnp.max(s, axis=-1, keepdims=True)
            e = jnp.exp(s - m)
            den = jnp.sum(e, axis=-1, keepdims=True)
            o_ref[:, cols] = _dot(e, vcat[:, cols]) * (1.0 / den)
            l_ref[:, h * HEAD_LANES:(h + 1) * HEAD_LANES] = jnp.broadcast_to(m + jnp.log(den), (BLK, HEAD_LANES))

    def blk(w, col, prev=False):
        return pl.BlockSpec((None, BLK, w), lambda r, n: (r, jnp.maximum(n - 1, 0) if prev else n, col))

    return pl.pallas_call(
        body, name=f"attn_fwd_d{d}", grid=(d, nb),
        in_specs=[blk(D_ATTN, 0), blk(D_ATTN, 1, True), blk(D_ATTN, 1), blk(D_ATTN, 2, True), blk(D_ATTN, 2),
                  pl.BlockSpec(memory_space=pl.ANY)],
        out_specs=[blk(D_ATTN, 0), blk(128, 0)],
        out_shape=[jax.ShapeDtypeStruct((d, l_sub, D_ATTN), F32), jax.ShapeDtypeStruct((d, l_sub, 128), F32)],
        compiler_params=_cp(dimension_semantics=("arbitrary", "arbitrary")),
    )(qkv, qkv, qkv, qkv, qkv, dep)


def _attn_bwd(qkv, do, lse, delta, d, dep):
    _, l_sub, _ = qkv.shape
    nb = l_sub // BLK
    whole = l_sub <= 8 * BLK

    def shares(n, q_ref, kp_ref, kc_ref, vp_ref, vc_ref, do_ref, l_ref, dl_ref, dq_ref):
        valid = _band_valid(n)
        kcat = jnp.concatenate([kp_ref[...], kc_ref[...]], axis=0)
        vcat = jnp.concatenate([vp_ref[...], vc_ref[...]], axis=0)
        for h in range(N_HEADS):
            cols = slice(h * HEAD_DIM, (h + 1) * HEAD_DIM)
            stat = slice(h * HEAD_LANES, h * HEAD_LANES + 1)
            qh, doh = q_ref[:, cols], do_ref[:, cols]
            p = jnp.where(valid, jnp.exp(_dot_nt(qh, kcat[:, cols]) - l_ref[:, stat]), 0.0)
            ds = p * (_dot_nt(doh, vcat[:, cols]) - dl_ref[:, stat])
            dq_ref[:, cols] = _dot(ds, kcat[:, cols])
            yield cols, _dot_tn(ds, qh), _dot_tn(p, doh)

    def body_whole(*refs):
        dk_ref, dv_ref = refs[10:]
        n = pl.program_id(1)
        cur = pl.ds(pl.multiple_of(n * BLK, BLK), BLK)
        prev = pl.ds(pl.multiple_of(jnp.maximum(n - 1, 0) * BLK, BLK), BLK)
        for cols, dk2, dv2 in shares(n, *refs[:8], refs[9]):
            dk_ref[cur, cols] = dk2[BLK:]
            dv_ref[cur, cols] = dv2[BLK:]
            dk_ref[prev, cols] += dk2[0:BLK]
            dv_ref[prev, cols] += dv2[0:BLK]

    def body_carry(*refs):
        dk_ref, dv_ref, ck_scr, cv_scr = refs[10:]
        n = pl.program_id(1)

        @pl.when(n == 0)
        def _():
            ck_scr[...] = jnp.zeros_like(ck_scr)
            cv_scr[...] = jnp.zeros_like(cv_scr)

        @pl.when(n < nb)
        def _():
            for cols, dk2, dv2 in shares(n, *refs[:8], refs[9]):
                dk_ref[:, cols] = ck_scr[:, cols] + dk2[0:BLK]
                dv_ref[:, cols] = cv_scr[:, cols] + dv2[0:BLK]
                ck_scr[:, cols] = dk2[BLK:]
                cv_scr[:, cols] = dv2[BLK:]

        @pl.when(n == nb)
        def _():
            dk_ref[...] = ck_scr[...]
            dv_ref[...] = cv_scr[...]

    def blk(w, col, shift=0):
        return pl.BlockSpec((None, BLK, w), lambda r, n: (r, jnp.clip(n - shift, 0, nb - 1), col))

    if whole:
        dkv_spec = pl.BlockSpec((None, l_sub, D_ATTN), lambda r, n: (r, 0, 0))
        body, steps, scratch = body_whole, nb, []
    else:
        dkv_spec = blk(D_ATTN, 0, 1)
        body, steps, scratch = body_carry, nb + 1, [pltpu.VMEM((BLK, D_ATTN), F32)] * 2
    return pl.pallas_call(
        body, name=f"attn_bwd_d{d}", grid=(d, steps),
        in_specs=[blk(D_ATTN, 0), blk(D_ATTN, 1, 1), blk(D_ATTN, 1), blk(D_ATTN, 2, 1), blk(D_ATTN, 2),
                  blk(D_ATTN, 0), blk(128, 0), blk(128, 0), pl.BlockSpec(memory_space=pl.ANY)],
        out_specs=[blk(D_ATTN, 0), dkv_spec, dkv_spec],
        out_shape=[jax.ShapeDtypeStruct((d, l_sub, D_ATTN), F32)] * 3,
        scratch_shapes=scratch,
        compiler_params=_cp(dimension_semantics=("arbitrary", "arbitrary")),
    )(qkv, qkv, qkv, qkv, qkv, do, lse, delta, dep)


def _mix_ln1(os_, ls_, gm, x, w_o, ln1_g, ln1_b, dep):
    t = x.shape[0]
    expand = _head_expand()

    def body(o1, o4, o16, l1, l4, l16, gm_ref, x_ref, wo_ref, g_ref, b_ref, ex_ref, dep_ref,
             attn_ref, lse1_ref, lse4_ref, lse16_ref, cat_ref, xhat_ref, rstd_ref, x1b_ref, o_scr, l_scr):
        _from_planes(o4, o_scr, DILATIONS[1], LANE_CHUNKS)
        _from_planes(o16, o_scr.at[pl.ds(LANE_CHUNKS, LANE_CHUNKS)], DILATIONS[2], LANE_CHUNKS)
        _from_planes(l4, l_scr, DILATIONS[1], 1)
        _from_planes(l16, l_scr.at[pl.ds(1, 1)], DILATIONS[2], 1)
        la, lb, lc = l1[...], l_scr[0], l_scr[1]
        m = jnp.maximum(jnp.maximum(la, lb), lc)
        ea, eb, ec = jnp.exp(la - m), jnp.exp(lb - m), jnp.exp(lc - m)
        den = ea + eb + ec
        inv = 1.0 / den
        wide = lambda w: _dot_select(w, ex_ref[...])
        attn = (wide(ea * inv) * o1[...] + wide(eb * inv) * _unchunk(o_scr, LANE_CHUNKS)
                + wide(ec * inv) * _unchunk(o_scr, LANE_CHUNKS, LANE_CHUNKS))
        attn_ref[...] = attn
        lse = m + jnp.log(den)
        lse1_ref[...] = lse
        l_scr[2] = lse
        _to_planes(lse4_ref, l_scr.at[pl.ds(2, 1)], DILATIONS[1], 1, F32)
        _to_planes(lse16_ref, l_scr.at[pl.ds(2, 1)], DILATIONS[2], 1, F32)
        cat_ref[:, 0:D_ATTN] = attn.astype(MXU)
        cat_ref[:, D_ATTN:] = gm_ref[...]
        mix = jnp.dot(cat_ref[...], wo_ref[...], preferred_element_type=F32)
        xhat, rstd = _ln_fwd(ALPHA * x_ref[...] + mix)
        xhat_ref[...] = xhat
        rstd_ref[...] = rstd
        x1b_ref[...] = (xhat * g_ref[...] + b_ref[...]).astype(MXU)

    tok = lambda w: pl.BlockSpec((TM, w), lambda i: (i, 0))
    outs = [jax.ShapeDtypeStruct((t, D_ATTN), F32)] + [_perm_shape(t, d, 128, F32) for d in DILATIONS] + [
        jax.ShapeDtypeStruct((t, D_MODEL), MXU), jax.ShapeDtypeStruct((t, D_MODEL), F32), jax.ShapeDtypeStruct((t, 1), F32),
        jax.ShapeDtypeStruct((t, D_MODEL), MXU)]
    return pl.pallas_call(
        body, name="mix_ln1", grid=(t // TM,),
        in_specs=[_perm_tile(d, D_ATTN) for d in DILATIONS] + [_perm_tile(d, 128) for d in DILATIONS]
        + [tok(D_GMLP), tok(D_MODEL), _full(w_o.shape), _full(ln1_g.shape), _full(ln1_b.shape), _full(expand.shape),
           pl.BlockSpec(memory_space=pl.ANY)],
        out_specs=[tok(D_ATTN)] + [_perm_tile(d, 128) for d in DILATIONS] + [tok(D_MODEL), tok(D_MODEL), tok(1), tok(D_MODEL)],
        out_shape=outs,
        scratch_shapes=[pltpu.VMEM((2 * LANE_CHUNKS, TM, 128), F32), pltpu.VMEM((3, TM, 128), F32)],
        compiler_params=_cp(dimension_semantics=("arbitrary",)),
    )(*os_, *ls_, gm, x, w_o, ln1_g, ln1_b, expand, dep)


def _conv_fwd(a_ext, w_ref, b_ref, rows):
    back = [pltpu.roll(a_ext, s, 0)[HALO:HALO + rows] for s in (1, 2)]
    return b_ref[...] + w_ref[2:3, :] * a_ext[HALO:HALO + rows] + w_ref[1:2, :] * back[0] + w_ref[0:1, :] * back[1]


def _ffn_in(x1b, w_a, w_b, conv_w, conv_b):
    t = x1b.shape[0]
    hb = TM // HALO

    def body(x_ref, xh_ref, wa_ref, wb_ref, cw_ref, cb_ref, apre_ref, act_ref, gate_ref, f_ref):
        i = pl.program_id(1)
        a_pre = _dot_nt(x_ref[...], wa_ref[...])
        a_halo = jnp.where(i > 0, _dot_nt(xh_ref[...], wa_ref[...]), 0.0)
        a = _conv_fwd(jnp.concatenate([a_halo, a_pre], axis=0), cw_ref, cb_ref, TM)
        b = _dot_nt(x_ref[...], wb_ref[...])
        cdf = 0.5 * (1.0 + lax.erf(a * (1.0 / math.sqrt(2.0))))
        pdf = jnp.exp(-0.5 * a * a) * (1.0 / math.sqrt(2.0 * math.pi))
        act = a * cdf
        apre_ref[...] = a_pre
        act_ref[...] = act
        gate_ref[...] = b * (cdf + a * pdf)
        f_ref[...] = (act * b).astype(MXU)

    blk = lambda r, c: pl.BlockSpec((None, r, c), lambda j, i: (j, 0, 0))
    tokj = pl.BlockSpec((None, TM, FF_BLK), lambda j, i: (j, i, 0))
    outs = [jax.ShapeDtypeStruct((N_SHARD, t, FF_BLK), F32)] * 3 + [jax.ShapeDtypeStruct((N_SHARD, t, FF_BLK), MXU)]
    return pl.pallas_call(
        body, name="ffn_in", grid=(N_SHARD, t // TM),
        in_specs=[pl.BlockSpec((TM, D_MODEL), lambda j, i: (i, 0)),
                  pl.BlockSpec((HALO, D_MODEL), lambda j, i: (jnp.maximum(i * hb - 1, 0), 0)),
                  blk(FF_BLK, D_MODEL), blk(FF_BLK, D_MODEL), blk(3, FF_BLK), blk(1, FF_BLK)],
        out_specs=[tokj, tokj, tokj, tokj], out_shape=outs,
        compiler_params=_cp(dimension_semantics=("arbitrary", "arbitrary")),
    )(x1b, x1b, w_a, w_b, conv_w, conv_b)


def _ffn_out_ln2(f, w_down, xhat1, ln1_g, ln1_b):
    t = xhat1.shape[0]

    def body(f_ref, wd_ref, xh_ref, g1_ref, b1_ref, xhat_ref, rstd_ref):
        half = TM // ROW_GROUPS
        for r0 in range(0, TM, half):
            rows = pl.ds(r0, half)
            ff = jnp.dot(f_ref[0, rows, :], wd_ref[0], preferred_element_type=F32)
            for j in range(1, N_SHARD):
                ff = ff + jnp.dot(f_ref[j, rows, :], wd_ref[j], preferred_element_type=F32)
            x1 = xh_ref[rows, :] * g1_ref[...] + b1_ref[...]
            xhat, rstd = _ln_fwd(ALPHA * x1 + ff)
            xhat_ref[rows, :] = xhat
            rstd_ref[rows, :] = rstd

    tok = lambda w: pl.BlockSpec((TM, w), lambda i: (i, 0))
    vec = _full((1, D_MODEL))
    outs = [jax.ShapeDtypeStruct((t, D_MODEL), F32), jax.ShapeDtypeStruct((t, 1), F32)]
    return pl.pallas_call(
        body, name="ffn_out_ln2", grid=(t // TM,),
        in_specs=[pl.BlockSpec((N_SHARD, TM, FF_BLK), lambda i: (0, i, 0)), _full(w_down.shape), tok(D_MODEL), vec, vec],
        out_specs=[tok(D_MODEL), tok(1)], out_shape=outs,
        compiler_params=_cp(dimension_semantics=("arbitrary",)),
    )(f, w_down, xhat1, ln1_g, ln1_b)


STAT_ROWS = 8


def _ple_loss_bwd(xhat2, rstd2, p, target, ln2_g, ln2_b, w_g, b_g, w_p, ln3_g, ln3_b):
    t = xhat2.shape[0]

    def body(xh2_ref, rs2_ref, p_ref, t_ref, g2_ref, b2_ref, wg_ref, bg_ref, wp_ref, g3_ref, b3_ref,
             dr2_ref, dr2b_ref, stat_ref, dwg_ref, dwp_ref, pp_scr, dwp_scr):
        @pl.when(pl.program_id(0) == 0)
        def _():
            stat_ref[...] = jnp.zeros_like(stat_ref)
            dwg_ref[...] = jnp.zeros_like(dwg_ref)
            dwp_scr[...] = jnp.zeros_like(dwp_scr)

        xhat2 = xh2_ref[...]
        x2 = xhat2 * g2_ref[...] + b2_ref[...]
        x2b = x2.astype(MXU)
        gate = jax.nn.sigmoid(jnp.dot(x2b, wg_ref[...], preferred_element_type=F32) + bg_ref[...])
        pb = p_ref[...].astype(MXU)
        for j in range(N_SHARD):
            pp_scr[:, j * ROW_BLK:(j + 1) * ROW_BLK] = jnp.dot(pb, wp_ref[j], preferred_element_type=F32)
        pp = pp_scr[...]
        xhat3, rstd3 = _ln_fwd(ALPHA * x2 + gate * pp)
        err = xhat3 * g3_ref[...] + b3_ref[...] - t_ref[...]
        dy = err * (1.0 / D_MODEL)
        dr3 = _ln_bwd(dy, xhat3, rstd3, g3_ref[...])
        dgp = dr3 * pp * gate * (1.0 - gate)
        dgp_b = dgp.astype(MXU)
        dwg_ref[...] += _dot_tn(x2b, dgp_b)
        dwp_scr[...] += _dot_tn(pb, dr3 * gate)
        dx2 = ALPHA * dr3 + _dot_nt(dgp_b, wg_ref[...])
        dr2 = _ln_bwd(dx2, xhat2, rs2_ref[...], g2_ref[...])
        dr2_ref[...] = dr2
        dr2b_ref[...] = dr2.astype(MXU)
        stat_ref[0:1, :] += _colsum(dy * xhat3)
        stat_ref[1:2, :] += _colsum(dy)
        stat_ref[2:3, :] += _colsum(dgp)
        stat_ref[3:4, :] += _colsum(dx2 * xhat2)
        stat_ref[4:5, :] += _colsum(dx2)
        stat_ref[5:6, :] += _colsum(err * err)

        @pl.when(pl.program_id(0) == t // TM - 1)
        def _():
            for j in range(N_SHARD):
                dwp_ref[j] = dwp_scr[:, j * ROW_BLK:(j + 1) * ROW_BLK]

    tok = lambda w: pl.BlockSpec((TM, w), lambda i: (i, 0))
    vec = _full((1, D_MODEL))
    outs = [jax.ShapeDtypeStruct((t, D_MODEL), F32), jax.ShapeDtypeStruct((t, D_MODEL), MXU),
            jax.ShapeDtypeStruct((STAT_ROWS, D_MODEL), F32), jax.ShapeDtypeStruct((D_MODEL, D_MODEL), F32),
            jax.ShapeDtypeStruct((N_SHARD, D_PLE, ROW_BLK), F32)]
    return pl.pallas_call(
        body, name="ple_loss_bwd", grid=(t // TM,),
        in_specs=[tok(D_MODEL), tok(1), tok(D_PLE), tok(D_MODEL), vec, vec, _full(w_g.shape), vec, _full(w_p.shape), vec, vec],
        out_specs=[tok(D_MODEL), tok(D_MODEL), _full((STAT_ROWS, D_MODEL)), _full((D_MODEL, D_MODEL)),
                   _full((N_SHARD, D_PLE, ROW_BLK))], out_shape=outs,
        scratch_shapes=[pltpu.VMEM((TM, D_MODEL), F32), pltpu.VMEM((D_PLE, D_MODEL), F32)],
        compiler_params=_cp(dimension_semantics=("arbitrary",)),
    )(xhat2, rstd2, p, target, ln2_g, ln2_b, w_g, b_g, w_p, ln3_g, ln3_b)


def _ffn_bwd(dr2, dr2b, a_pre, act, gate, w_down, w_a, w_b, conv_w, xhat1, rstd1, ln1_g, cat):
    t = dr2.shape[0]
    nt = t // TM
    hb = TM // HALO
    last_h = t // HALO - 1
    halo2 = 2 * HALO

    def body(dr_ref, drb_ref, drbn_ref, ap_ref, act_ref, gate_ref, gaten_ref, wd_ref, wa_ref, wb_ref, cw_ref,
             xh_ref, rs_ref, g1_ref, cat_ref, dap_ref, dbb_ref, dr1_ref, cstat_ref, lstat_ref, dwo_ref, acc_scr):
        i, j = pl.program_id(0), pl.program_id(1)

        @pl.when((i == 0) & (j == 0))
        def _():
            cstat_ref[...] = jnp.zeros_like(cstat_ref)
            lstat_ref[...] = jnp.zeros_like(lstat_ref)
            dwo_ref[...] = jnp.zeros_like(dwo_ref)

        half = TM // ROW_GROUPS
        parts = []
        for r0 in range(0, TM, half):
            rows = pl.ds(r0, half)
            last = r0 + half == TM

            def ext(ref, nxt):
                return jnp.concatenate([ref[rows], nxt[...]], axis=0) if last else ref[r0:r0 + half + HALO]

            drb = jnp.concatenate([drb_ref[rows, :], drbn_ref[...]], axis=0) if last else drb_ref[r0:r0 + half + halo2, :]
            df = _dot_nt(drb, wd_ref[...])[0:half + HALO]
            da = df * ext(gate_ref, gaten_ref)
            if last:
                da = jnp.concatenate([da[0:half], jnp.where(i < nt - 1, da[half:], 0.0)], axis=0)
            ahead = [da[0:half]] + [pltpu.roll(da, half + HALO - s, 0)[0:half] for s in (1, 2)]
            da_pre = cw_ref[2:3, :] * ahead[0] + cw_ref[1:2, :] * ahead[1] + cw_ref[0:1, :] * ahead[2]
            dbb = df[0:half] * act_ref[rows, :]
            dap_ref[rows, :] = da_pre.astype(MXU)
            dbb_ref[rows, :] = dbb.astype(MXU)
            for kk in range(3):
                cstat_ref[j, kk:kk + 1, :] += _colsum(ahead[2 - kk] * ap_ref[rows, :])
            cstat_ref[j, 3:4, :] += _colsum(ahead[0])
            parts.append(_dot(da_pre, wa_ref[...]) + _dot(dbb, wb_ref[...]))
        part = jnp.concatenate(parts, axis=0)

        @pl.when(j == 0)
        def _():
            acc_scr[...] = ALPHA * dr_ref[...] + part

        @pl.when(j > 0)
        def _():
            acc_scr[...] += part

        @pl.when(j == N_SHARD - 1)
        def _():
            dx1 = acc_scr[...]
            xhat1 = xh_ref[...]
            lstat_ref[0:1, :] += _colsum(dx1 * xhat1)
            lstat_ref[1:2, :] += _colsum(dx1)
            dr1 = _ln_bwd(dx1, xhat1, rs_ref[...], g1_ref[...])
            dr1_ref[...] = dr1
            dwo_ref[...] += _dot_tn(cat_ref[...], dr1)

    tok = lambda w: pl.BlockSpec((TM, w), lambda i, j: (i, 0))
    tokj = pl.BlockSpec((None, TM, FF_BLK), lambda i, j: (j, i, 0))
    nextj = pl.BlockSpec((None, HALO, FF_BLK), lambda i, j: (j, jnp.minimum((i + 1) * hb, last_h), 0))
    blk = lambda r, c: pl.BlockSpec((None, r, c), lambda i, j: (j, 0, 0))
    outs = [jax.ShapeDtypeStruct((N_SHARD, t, FF_BLK), MXU)] * 2 + [
        jax.ShapeDtypeStruct((t, D_MODEL), F32), jax.ShapeDtypeStruct((N_SHARD, STAT_ROWS, FF_BLK), F32),
        jax.ShapeDtypeStruct((STAT_ROWS, D_MODEL), F32), jax.ShapeDtypeStruct((D_MODEL, D_MODEL), F32)]
    return pl.pallas_call(
        body, name="ffn_bwd", grid=(nt, N_SHARD),
        in_specs=[tok(D_MODEL), tok(D_MODEL),
                  pl.BlockSpec((halo2, D_MODEL), lambda i, j: (jnp.minimum((i + 1) * (hb // 2), last_h // 2), 0)),
                  tokj, tokj, tokj, nextj, blk(FF_BLK, D_MODEL), blk(FF_BLK, D_MODEL), blk(FF_BLK, D_MODEL),
                  blk(3, FF_BLK), tok(D_MODEL), tok(1), _full((1, D_MODEL)), tok(D_MODEL)],
        out_specs=[tokj, tokj, tok(D_MODEL), _full((N_SHARD, STAT_ROWS, FF_BLK)), _full((STAT_ROWS, D_MODEL)),
                   _full((D_MODEL, D_MODEL))], out_shape=outs,
        scratch_shapes=[pltpu.VMEM((TM, D_MODEL), F32)],
        compiler_params=_cp(dimension_semantics=("arbitrary", "arbitrary")),
    )(dr2, dr2b, dr2b, a_pre, act, gate, gate, w_down, w_a, w_b, conv_w, xhat1, rstd1, ln1_g, cat)


def _mix_bwd(dr1, w_o, hu, hz, mixed, attn, ln_z_g, ln_z_b, w_s, dep):
    t = dr1.shape[0]
    nchunk = TM // BLK

    def body(dr_ref, wo_ref, hu_ref, hz_ref, mx_ref, attn_ref, g_ref, b_ref, ws_ref, grp_ref, red_ref, dep_ref,
             do1_ref, do4_ref, do16_ref, dl1_ref, dl4_ref, dl16_ref, duz_ref, dws_ref, dbs_ref, zstat_ref,
             wm_scr, dzn_scr, dbsum_scr, do_scr, dl_scr):
        @pl.when(pl.program_id(0) == 0)
        def _():
            row = lax.broadcasted_iota(jnp.int32, (BLK, BLK), 0)
            col = lax.broadcasted_iota(jnp.int32, (BLK, BLK), 1)
            for g in range(N_HEADS):
                wm_scr[g] = jnp.where(col <= row, ws_ref[g], 0.0).astype(MXU)
            dws_ref[...] = jnp.zeros_like(dws_ref)
            dbsum_scr[...] = jnp.zeros_like(dbsum_scr)
            zstat_ref[...] = jnp.zeros_like(zstat_ref)

        dcat = _dot_nt(dr_ref[...], wo_ref[...])
        dattn = dcat[:, 0:D_ATTN]
        do1_ref[...] = dattn.astype(MXU)
        for cc, val in enumerate(_chunks(dattn)):
            do_scr[cc] = val
        _to_planes(do4_ref, do_scr, DILATIONS[1], LANE_CHUNKS, MXU)
        _to_planes(do16_ref, do_scr, DILATIONS[2], LANE_CHUNKS, MXU)
        delta = _dot_select(dattn * attn_ref[...], red_ref[...])
        dl1_ref[...] = delta
        dl_scr[0] = delta
        _to_planes(dl4_ref, dl_scr, DILATIONS[1], 1, F32)
        _to_planes(dl16_ref, dl_scr, DILATIONS[2], 1, F32)
        dgm = dcat[:, D_ATTN:]
        hu, hz = hu_ref[...], hz_ref[...]
        u = _gelu(hu)
        duz_ref[:, 0:D_GMLP] = (dgm * mx_ref[...] * _gelu_grad(hu)).astype(MXU)
        dmixed = dgm * u
        dmb = dmixed.astype(MXU)
        zhat, rstd = _ln_fwd(_gelu(hz))
        znb = (zhat * g_ref[...] + b_ref[...]).astype(MXU)
        dbs_acc = jnp.zeros((BLK, D_GMLP), F32)
        for ch in range(nchunk):
            rows = slice(ch * BLK, (ch + 1) * BLK)
            dbs_acc = dbs_acc + dmixed[rows]
            for g in range(N_HEADS):
                cols = slice(g * HEAD_DIM, (g + 1) * HEAD_DIM)
                dzn_scr[rows, cols] = _dot_tn(wm_scr[g], dmb[rows, cols])
                dws_ref[g] += _dot_nt(dmb[rows, cols], znb[rows, cols])
        dbsum_scr[...] += dbs_acc
        dzn = dzn_scr[...]
        zstat_ref[0:1, :] += _colsum(dzn * zhat)
        zstat_ref[1:2, :] += _colsum(dzn)
        duz_ref[:, D_GMLP:] = (_ln_bwd(dzn, zhat, rstd, g_ref[...]) * _gelu_grad(hz)).astype(MXU)

        @pl.when(pl.program_id(0) == nt - 1)
        def _():
            row = lax.broadcasted_iota(jnp.int32, (BLK, BLK), 0)
            col = lax.broadcasted_iota(jnp.int32, (BLK, BLK), 1)
            for g in range(N_HEADS):
                dws_ref[g] = jnp.where(col <= row, dws_ref[g], 0.0)
            dbs_ref[...] = lax.dot_general(grp_ref[...], dbsum_scr[...], (((1,), (1,)), ((), ())),
                                           precision=lax.Precision.HIGHEST, preferred_element_type=F32)

    nt = t // TM
    tok = lambda w: pl.BlockSpec((TM, w), lambda i: (i, 0))
    grp = jnp.asarray((np.arange(D_GMLP)[None, :] // HEAD_DIM == np.arange(N_HEADS)[:, None]).astype(np.float32))
    red = _head_reduce()
    outs = [_perm_shape(t, d, D_ATTN, MXU) for d in DILATIONS] + [_perm_shape(t, d, 128, F32) for d in DILATIONS] + [
        jax.ShapeDtypeStruct((t, 2 * D_GMLP), MXU),
        jax.ShapeDtypeStruct((N_HEADS, BLK, BLK), F32), jax.ShapeDtypeStruct((N_HEADS, BLK), F32),
        jax.ShapeDtypeStruct((STAT_ROWS, D_GMLP), F32)]
    return pl.pallas_call(
        body, name="mix_bwd", grid=(t // TM,),
        in_specs=[tok(D_MODEL), _full(w_o.shape), tok(D_GMLP), tok(D_GMLP), tok(D_GMLP), tok(D_ATTN), _full(ln_z_g.shape),
                  _full(ln_z_b.shape), _full(w_s.shape), _full(grp.shape), _full(red.shape), pl.BlockSpec(memory_space=pl.ANY)],
        out_specs=[_perm_tile(d, D_ATTN) for d in DILATIONS] + [_perm_tile(d, 128) for d in DILATIONS]
        + [tok(2 * D_GMLP), _full((N_HEADS, BLK, BLK)), _full((N_HEADS, BLK)), _full((STAT_ROWS, D_GMLP))],
        out_shape=outs,
        scratch_shapes=[pltpu.VMEM((N_HEADS, BLK, BLK), MXU), pltpu.VMEM((TM, D_GMLP), F32), pltpu.VMEM((BLK, D_GMLP), F32),
                        pltpu.VMEM((LANE_CHUNKS, TM, 128), F32), pltpu.VMEM((1, TM, 128), F32)],
        compiler_params=_cp(dimension_semantics=("arbitrary",)),
    )(dr1, w_o, hu, hz, mixed, attn, ln_z_g, ln_z_b, w_s, grp, red, dep)


def _dx_in(dqs, dks, dvs, duz, dr1, w_in, c_tab, s1_tab, s2_tab):
    t = dr1.shape[0]

    def body(dq1, dq4, dq16, dk1, dk4, dk16, dv1, dv4, dv16, duz_ref, dr_ref, w_ref, c_ref, s1_ref, s2_ref,
             dh_ref, dx_ref, acc_scr):
        sums = []
        for part, (g1, g4, g16) in enumerate(((dq1, dq4, dq16), (dk1, dk4, dk16), (dv1, dv4, dv16))):
            acc = acc_scr.at[pl.ds(part * LANE_CHUNKS, LANE_CHUNKS)]
            for cc in range(LANE_CHUNKS):
                acc[cc] = g1[:, cc * 128:(cc + 1) * 128]
            _from_planes(g4, acc, DILATIONS[1], LANE_CHUNKS, accumulate=True)
            _from_planes(g16, acc, DILATIONS[2], LANE_CHUNKS, accumulate=True)
            sums.append(_unchunk(acc_scr, LANE_CHUNKS, part * LANE_CHUNKS))
        c, s1, s2 = _tile_heads(c_ref[...]), _tile_heads(s1_ref[...]), _tile_heads(s2_ref[...])
        dh_ref[:, 0:D_ATTN] = _rope_apply_t(sums[0] * (1.0 / math.sqrt(HEAD_DIM)), c, s1, s2).astype(MXU)
        dh_ref[:, D_ATTN:2 * D_ATTN] = _rope_apply_t(sums[1], c, s1, s2).astype(MXU)
        dh_ref[:, 2 * D_ATTN:3 * D_ATTN] = sums[2].astype(MXU)
        dh_ref[:, 3 * D_ATTN:] = duz_ref[...]
        dx = ALPHA * dr_ref[...]
        for j in range(N_SHARD):
            dx = dx + _dot_nt(dh_ref[:, j * W_IN_BLK:(j + 1) * W_IN_BLK], w_ref[j])
        dx_ref[...] = dx

    tok = lambda w: pl.BlockSpec((TM, w), lambda i: (i, 0))
    outs = [jax.ShapeDtypeStruct((t, D_IN), MXU), jax.ShapeDtypeStruct((t, D_MODEL), F32)]
    return pl.pallas_call(
        body, name="dx_in", grid=(t // TM,),
        in_specs=[_perm_tile(d, D_ATTN) for d in DILATIONS] * 3
        + [tok(2 * D_GMLP), tok(D_MODEL), _full(w_in.shape), tok(128), tok(128), tok(128)],
        out_specs=[tok(D_IN), tok(D_MODEL)], out_shape=outs,
        scratch_shapes=[pltpu.VMEM((3 * LANE_CHUNKS, TM, 128), F32)],
        compiler_params=_cp(dimension_semantics=("arbitrary",)),
    )(*dqs, *dks, *dvs, duz, dr1, w_in, c_tab, s1_tab, s2_tab)


def _wgrad(name, x, dy, x_spec, dy_spec, out_spec, out_shape, grid, dep=None):
    deps = [] if dep is None else [dep]

    def body(x_ref, dy_ref, *rest):
        rest[-1][...] = _dot_tn(x_ref[...], dy_ref[...])

    return pl.pallas_call(
        body, name=name, grid=grid, in_specs=[x_spec, dy_spec] + [pl.BlockSpec(memory_space=pl.ANY)] * len(deps),
        out_specs=out_spec, out_shape=jax.ShapeDtypeStruct(out_shape, F32),
        compiler_params=_cp(dimension_semantics=("arbitrary",) * len(grid)),
    )(x, dy, *deps)


def _wgrad_pair(name, xa, xb, dy, x_spec, dy_spec, out_spec, out_shape, grid):
    def body(xa_ref, xb_ref, dy_ref, oa_ref, ob_ref):
        dy = dy_ref[...]
        oa_ref[...] = _dot_tn(xa_ref[...], dy)
        ob_ref[...] = _dot_tn(xb_ref[...], dy)

    return pl.pallas_call(
        body, name=name, grid=grid, in_specs=[x_spec, x_spec, dy_spec], out_specs=[out_spec, out_spec],
        out_shape=[jax.ShapeDtypeStruct(out_shape, F32)] * 2,
        compiler_params=_cp(dimension_semantics=("arbitrary",) * len(grid)),
    )(xa, xb, dy)


def _local_step(x, p, rope, target, w_in, start_dep, late_landed, late_weights, early_grads, early_grads_sent,
                early_grads_landed,
                ln_z_g, ln_z_b, w_s, b_s, ln1_g, ln1_b, conv_b, ln2_g, ln2_b, b_g, ln3_g, ln3_b):
    t = x.shape[0]
    half = TM
    c_tab, s1_tab, s2_tab = rope
    b_full = jnp.repeat(jnp.transpose(b_s[0]), HEAD_DIM, axis=1)
    conv_b4 = conv_b.reshape(N_SHARD, 1, FF_BLK)
    *qkvs, hu, hz, mixed, gm, xb = _qkvuz(x, w_in, c_tab, s1_tab, s2_tab, ln_z_g, ln_z_b, w_s[0], b_full, start_dep)
    branches = [_attn_fwd(qkv, d, start_dep) for qkv, d in zip(qkvs[:2], DILATIONS[:2])]
    dep = late_landed(branches[-1][1])
    branches.append(_attn_fwd(qkvs[2], DILATIONS[2], dep))
    w_o, w_a, w_b, conv_w, w_down, w_g, w_p = late_weights(branches[-1][1])
    attn, *lses, cat, xhat1, rstd1, x1b = _mix_ln1(
        [o for o, _ in branches], [l for _, l in branches], gm, x, w_o, ln1_g, ln1_b, dep)
    a_pre, act, gate, f = _ffn_in(x1b, w_a, w_b, conv_w, conv_b4)
    xhat2, rstd2 = _ffn_out_ln2(f, w_down, xhat1, ln1_g, ln1_b)
    dr2, dr2b, stat3, g_w_g, g_w_p = _ple_loss_bwd(xhat2, rstd2, p, target, ln2_g, ln2_b, w_g, b_g, w_p, ln3_g, ln3_b)
    da_pre, dbb, dr1, cstat, stat1, g_w_o = _ffn_bwd(dr2, dr2b, a_pre, act, gate, w_down, w_a, w_b, conv_w, xhat1, rstd1,
                                                    ln1_g, cat)

    full_t = lambda w, im: pl.BlockSpec((t, w), im)
    ffj = pl.BlockSpec((None, t, FF_BLK), lambda j, kk: (j, 0, 0))
    early = dict(
        w_ple_gate=g_w_g, w_ple_in=g_w_p,
        w_ff_down=_wgrad("dw_down", f, dr2b, ffj, full_t(half, lambda j, n: (0, n)),
                         pl.BlockSpec((None, FF_BLK, half), lambda j, n: (j, 0, n)), (N_SHARD, FF_BLK, D_MODEL), (N_SHARD, 2)),
        **dict(zip(("w_ff_a", "w_ff_b"), _wgrad_pair(
            "dw_ab", da_pre, dbb, x1b, ffj, full_t(half, lambda j, n: (0, n)),
            pl.BlockSpec((None, FF_BLK, half), lambda j, n: (j, 0, n)), (N_SHARD, FF_BLK, D_MODEL), (N_SHARD, 2)))),
        w_o=g_w_o)
    dep = early_grads(early)

    do1, do4, do16, dl1, dl4, dl16, duz, dws, dbs, zstat = _mix_bwd(
        dr1, w_o, hu, hz, mixed, attn, ln_z_g, ln_z_b, w_s[0], dep)
    dep = early_grads_sent(duz, (stat3, stat1, zstat, cstat, dws, dbs))
    dqkv = [_attn_bwd(qkv, do, lse, dl, d, dep)
            for qkv, do, lse, dl, d in zip(qkvs, (do1, do4, do16), lses, (dl1, dl4, dl16), DILATIONS)]
    dh, grad_x = _dx_in([g[0] for g in dqkv], [g[1] for g in dqkv], [g[2] for g in dqkv], duz, dr1, w_in,
                        c_tab, s1_tab, s2_tab)
    dep = early_grads_landed(grad_x)
    g_w_in = _wgrad("dw_in", xb, dh, full_t(half, lambda j, kk: (0, kk)), full_t(W_IN_BLK, lambda j, kk: (0, j)),
                    pl.BlockSpec((None, half, W_IN_BLK), lambda j, kk: (j, kk, 0)), (N_SHARD, D_MODEL, W_IN_BLK), (N_SHARD, 2),
                    dep)
    return grad_x, g_w_in


def _tile_rows(rows, mult, steps):
    if rows % mult:
        return rows
    return next(rows // k for k in range(steps, rows + 1) if rows % k == 0 and (rows // k) % mult == 0)


def _grid_spec(grid, in_specs, out_specs):
    return pltpu.PrefetchScalarGridSpec(num_scalar_prefetch=1, grid=grid, in_specs=in_specs, out_specs=out_specs)


def _on_own_steps(i, count, steps, work):
    if count == steps:
        work()
    else:
        pl.when(i < count)(work)


def _place_shards(name, ws, dtypes, place, dep):
    n = len(ws)
    tiles = [_tile_rows(w.shape[0], 16, 2) for w in ws]
    counts = [w.shape[0] // t for w, t in zip(ws, tiles)]
    steps = max(counts)

    def body(s_ref, *refs):
        i = pl.program_id(0)
        for a in range(n):
            def work(a=a):
                refs[n + 1 + a][...] = refs[a][...].astype(dtypes[a])
            _on_own_steps(i, counts[a], steps, work)

    def tile(a, lead):
        last = counts[a] - 1
        if lead:
            return pl.BlockSpec((None, tiles[a], ws[a].shape[1]), lambda i, s: (s[0], jnp.minimum(i, last), 0))
        return pl.BlockSpec((tiles[a], ws[a].shape[1]), lambda i, s: (jnp.minimum(i, last), 0))

    return pl.pallas_call(
        body, name=name,
        grid_spec=_grid_spec((steps,), [tile(a, False) for a in range(n)] + [pl.BlockSpec(memory_space=pl.ANY)],
                             [tile(a, True) for a in range(n)]),
        out_shape=[jax.ShapeDtypeStruct((N_SHARD, *w.shape), dt) for w, dt in zip(ws, dtypes)],
        compiler_params=_cp())(place, *ws, dep)


def _pair_sums(name, mines, gots, place):
    n = len(mines)
    tiles = [_tile_rows(g.shape[1], 16, 1) for g in gots]
    per_blk = [g.shape[1] // t for g, t in zip(gots, tiles)]
    counts = [N_SHARD * nh for nh in per_blk]
    steps = max(counts)

    def body(s_ref, *refs):
        i = pl.program_id(0)
        for a in range(n):
            def work(a=a):
                refs[2 * n + a][...] = (refs[a][...] + refs[n + a][...]).astype(BF16)
            _on_own_steps(i, counts[a], steps, work)

    def tile(a, mine):
        nh, last = per_blk[a], counts[a] - 1

        def index(i, s):
            g = jnp.minimum(i, last)
            return (g // nh, (s[1] * nh if mine else 0) + g % nh, 0)

        return pl.BlockSpec((None, tiles[a], gots[a].shape[2]), index)

    return pl.pallas_call(
        body, name=name,
        grid_spec=_grid_spec((steps,), [tile(a, True) for a in range(n)] + [tile(a, False) for a in range(n)],
                             [tile(a, False) for a in range(n)]),
        out_shape=[jax.ShapeDtypeStruct(g.shape, BF16) for g in gots], compiler_params=_cp())(place, *mines, *gots)


def _chip_sums(name, owns, landeds, place, dep):
    n = len(owns)
    tiles = [_tile_rows(o.shape[1], 16, 4) for o in owns]
    counts = [o.shape[1] // t for o, t in zip(owns, tiles)]
    steps = max(counts)

    def body(s_ref, *refs):
        i = pl.program_id(0)
        for a in range(n):
            def work(a=a):
                own, l1, l2, l3 = (refs[4 * a + k][...].astype(F32) for k in range(4))
                refs[4 * n + 1 + a][...] = ((own + l1) + l2) + l3
            _on_own_steps(i, counts[a], steps, work)

    def slot(a, d):
        last = counts[a] - 1
        return pl.BlockSpec((None, tiles[a], owns[a].shape[2]), lambda i, s: ((s[0] + d) % N_SHARD, jnp.minimum(i, last), 0))

    def out(a):
        nh, last = counts[a], counts[a] - 1
        return pl.BlockSpec((tiles[a], owns[a].shape[2]), lambda i, s: (s[1] * nh + jnp.minimum(i, last), 0))

    operands = [x for o, l in zip(owns, landeds) for x in (o, l, l, l)]
    return pl.pallas_call(
        body, name=name,
        grid_spec=_grid_spec((steps,), [slot(a, d) for a in range(n) for d in range(4)] + [pl.BlockSpec(memory_space=pl.ANY)],
                             [out(a) for a in range(n)]),
        out_shape=[jax.ShapeDtypeStruct((2 * o.shape[1], o.shape[2]), F32) for o in owns],
        compiler_params=_cp())(place, *operands, dep)


def _adamw_math(w, g, m, v):
    m = ADAM_B1 * m + (1.0 - ADAM_B1) * g
    v = ADAM_B2 * v + (1.0 - ADAM_B2) * (g * g)
    m_hat = m / (1.0 - ADAM_B1 ** ADAM_STEP)
    v_hat = v / (1.0 - ADAM_B2 ** ADAM_STEP)
    delta = -ADAM_LR * (m_hat / (jnp.sqrt(v_hat) + ADAM_EPS) + ADAM_WD * w)
    return delta, m, v


def _adamw_shards(name, ws, gs, ms, vs):
    n = len(ws)
    tiles = [_tile_rows(w.shape[1], 8, 8 if n > 1 else 2) for w in ws]
    counts = [w.shape[1] // t for w, t in zip(ws, tiles)]
    steps = max(counts)

    def body(*refs):
        i = pl.program_id(0)
        for a in range(n):
            def work(a=a):
                w_ref, g_ref, m_ref, v_ref = refs[4 * a:4 * a + 4]
                go_ref, d_ref, nm_ref, nv_ref = refs[4 * n + 4 * a:4 * n + 4 * a + 4]
                g = g_ref[...]
                go_ref[...] = g
                d_ref[...], nm_ref[...], nv_ref[...] = _adamw_math(w_ref[...], g, m_ref[...], v_ref[...])
            _on_own_steps(i, counts[a], steps, work)

    def tile(a, lead):
        last, c = counts[a] - 1, ws[a].shape[2]
        if lead:
            return pl.BlockSpec((None, tiles[a], c), lambda i: (0, jnp.minimum(i, last), 0))
        return pl.BlockSpec((tiles[a], c), lambda i: (jnp.minimum(i, last), 0))

    res = pl.pallas_call(
        body, name=name, grid=(steps,),
        in_specs=[tile(a, lead) for a in range(n) for lead in (True, False, True, True)],
        out_specs=[tile(a, True) for a in range(n) for _ in range(4)],
        out_shape=[jax.ShapeDtypeStruct(w.shape, F32) for w in ws for _ in range(4)],
        compiler_params=_cp())(*[x for quad in zip(ws, gs, ms, vs) for x in quad])
    return [tuple(res[4 * a:4 * a + 4]) for a in range(n)]


MESH = pl.DeviceIdType.MESH
ANY = pl.BlockSpec(memory_space=pl.ANY)


def _place():
    x, y, c = lax.axis_index("x"), lax.axis_index("y"), lax.axis_index("c")
    chips = [(1 - x, y), (x, 1 - y), (1 - x, 1 - y)]
    return x, y, c, 2 * x + y, chips


def _remote(src, dst, send_sem, recv_sem, dev):
    return pltpu.make_async_remote_copy(src_ref=src, dst_ref=dst, send_sem=send_sem, recv_sem=recv_sem,
                                        device_id=dev, device_id_type=MESH)


def _half(ref, hc, rows):
    return ref.at[pl.ds(hc * (rows // 2), rows // 2)]


def _sibling_join(blocks, tag):
    n = len(blocks)

    def body(*refs):
        outs = refs[n:2 * n]
        send, recv = refs[2 * n:]
        x, y, c, _, _ = _place()
        cps = []
        for a in range(n):
            h = blocks[a].shape[0] // 2
            mine = outs[a].at[pl.ds(c * h, h)]
            cp = _remote(mine, mine, send.at[a], recv.at[a], (x, y, 1 - c))
            cp.start()
            cps.append(cp)
        for a, cp in enumerate(cps):
            h = blocks[a].shape[0] // 2
            theirs = outs[a].at[pl.ds((1 - c) * h, h)]
            _remote(theirs, theirs, send.at[a], recv.at[a], (x, y, 1 - c)).wait_recv()
            cp.wait_send()

    sem = pltpu.SemaphoreType.DMA
    return pl.pallas_call(body, name=f"rs_sibling_join_{tag}", in_specs=[ANY] * n, out_specs=[ANY] * n,
                          out_shape=[jax.ShapeDtypeStruct(b_.shape, b_.dtype) for b_ in blocks],
                          input_output_aliases={a: a for a in range(n)},
                          scratch_shapes=[sem((n,)), sem((n,))])(*blocks)


def _join_start(blocks, after, tag):
    n = len(blocks)

    def body(*refs):
        ins = refs[:n]
        send, recv = refs[n + 1], refs[n + 2]
        token = refs[2 * n + 3]
        x, y, c, _, _ = _place()
        for a in range(n):
            h = blocks[a].shape[0] // 2
            mine = ins[a].at[pl.ds(c * h, h)]
            _remote(mine, mine, send.at[a], recv.at[a], (x, y, 1 - c)).start()
        token[...] = jnp.zeros_like(token)

    sems = pltpu.SemaphoreType.DMA((n,))
    res = pl.pallas_call(
        body, name=f"join_start_{tag}", in_specs=[HBM] * n + [ANY],
        out_specs=[SEM, SEM] + [HBM] * n + [pl.BlockSpec(memory_space=pltpu.VMEM)],
        out_shape=[sems, sems] + [pltpu.HBM(b_.shape, b_.dtype) for b_ in blocks] + [TOKEN],
        input_output_aliases={a: a + 2 for a in range(n)}, compiler_params=_in_flight_params(),
    )(*[_in_hbm(b_) for b_ in blocks], after)
    return res[0], res[1], res[2:2 + n], res[2 + n]


def _join_wait(send, recv, blocks, after, tag):
    n = len(blocks)

    def body(*refs):
        ins = refs[:n]
        send_ref, recv_ref = refs[n], refs[n + 1]
        x, y, c, _, _ = _place()
        for a in range(n):
            h = blocks[a].shape[0] // 2
            mine, theirs = ins[a].at[pl.ds(c * h, h)], ins[a].at[pl.ds((1 - c) * h, h)]
            _remote(mine, mine, send_ref.at[a], recv_ref.at[a], (x, y, 1 - c)).wait_send()
            _remote(theirs, theirs, send_ref.at[a], recv_ref.at[a], (x, y, 1 - c)).wait_recv()

    return pl.pallas_call(
        body, name=f"join_wait_{tag}", in_specs=[HBM] * n + [SEM, SEM, ANY], out_specs=[HBM] * n,
        out_shape=[pltpu.HBM(b_.shape, b_.dtype) for b_ in blocks],
        input_output_aliases={a: a for a in range(n)}, compiler_params=_in_flight_params(),
    )(*blocks, send, recv, after)


HBM = pl.BlockSpec(memory_space=pltpu.HBM)
SEM = pl.BlockSpec(memory_space=pltpu.SEMAPHORE)
TOKEN = jax.ShapeDtypeStruct((8, 128), F32)


def _in_flight_params():
    return pltpu.CompilerParams(has_side_effects=pltpu.SideEffectType.DATAFLOW_SIDE_EFFECTING)


def _in_hbm(a):
    return pltpu.with_memory_space_constraint(a, pltpu.HBM)


def _gather_piece(ref, rows, split, slot, hc):
    return _half(ref.at[slot], hc, rows) if split else ref.at[slot]


def _gather_start(stacks, split, after, tag):
    n = len(stacks)

    def body(*refs):
        ins = refs[:n]
        send, recv = refs[n + 1], refs[n + 2]
        token = refs[2 * n + 3]
        _, _, c, j, chips = _place()
        for a in range(n):
            mine = _gather_piece(ins[a], stacks[a].shape[1], split[a], j, c)
            for t in range(3):
                _remote(mine, mine, send.at[3 * a + t], recv.at[3 * a + t], (*chips[t], c)).start()
        token[...] = jnp.zeros_like(token)

    sems = pltpu.SemaphoreType.DMA((3 * n,))
    res = pl.pallas_call(
        body, name=f"gather_start_{tag}", in_specs=[HBM] * n + [ANY],
        out_specs=[SEM, SEM] + [HBM] * n + [pl.BlockSpec(memory_space=pltpu.VMEM)],
        out_shape=[sems, sems] + [pltpu.HBM(s.shape, s.dtype) for s in stacks] + [TOKEN],
        input_output_aliases={a: a + 2 for a in range(n)}, compiler_params=_in_flight_params(),
    )(*[_in_hbm(s) for s in stacks], after)
    return res[0], res[1], res[2:2 + n], res[2 + n]


def _gather_wait(send, recv, stacks, split, after, tag):
    n = len(stacks)

    def body(*refs):
        ins = refs[:n]
        send_ref, recv_ref = refs[n], refs[n + 1]
        _, _, c, j, chips = _place()
        for a in range(n):
            rows = stacks[a].shape[1]
            mine = _gather_piece(ins[a], rows, split[a], j, c)
            for t, (px, py) in enumerate(chips):
                theirs = _gather_piece(ins[a], rows, split[a], 2 * px + py, c)
                _remote(mine, mine, send_ref.at[3 * a + t], recv_ref.at[3 * a + t], (px, py, c)).wait_send()
                _remote(theirs, theirs, send_ref.at[3 * a + t], recv_ref.at[3 * a + t], (px, py, c)).wait_recv()

    return pl.pallas_call(
        body, name=f"gather_wait_{tag}", in_specs=[HBM] * n + [SEM, SEM, ANY], out_specs=[HBM] * n,
        out_shape=[pltpu.HBM(s.shape, s.dtype) for s in stacks],
        input_output_aliases={a: a for a in range(n)}, compiler_params=_in_flight_params(),
    )(*stacks, send, recv, after)


def _gather_forward(stacks, split, tag):
    idx = [a for a in range(len(stacks)) if split[a]]
    n = len(idx)

    def body(*refs):
        outs = refs[n:2 * n]
        send, recv = refs[2 * n:]
        x, y, c, _, chips = _place()
        sends = []
        for t, (px, py) in enumerate(chips):
            for a in range(n):
                blk = _half(outs[a].at[2 * px + py], c, stacks[idx[a]].shape[1])
                cp = _remote(blk, blk, send.at[a, t], recv.at[a, t], (x, y, 1 - c))
                cp.start()
                sends.append(cp)
        for t, (px, py) in enumerate(chips):
            for a in range(n):
                blk = _half(outs[a].at[2 * px + py], 1 - c, stacks[idx[a]].shape[1])
                _remote(blk, blk, send.at[a, t], recv.at[a, t], (x, y, 1 - c)).wait_recv()
        for cp in sends:
            cp.wait_send()

    sem = pltpu.SemaphoreType.DMA
    res = pl.pallas_call(
        body, name=f"gather_forward_{tag}", in_specs=[ANY] * n, out_specs=[ANY] * n,
        out_shape=[jax.ShapeDtypeStruct(stacks[a].shape, stacks[a].dtype) for a in idx],
        input_output_aliases={a: a for a in range(n)}, scratch_shapes=[sem((n, 3)), sem((n, 3))],
    )(*[stacks[a] for a in idx])
    out = list(stacks)
    for a, r in zip(idx, res):
        out[a] = r
    return out


def _forward_start(stacks, after, tag):
    n = len(stacks)

    def body(*refs):
        ins = refs[:n]
        send, recv = refs[n + 1], refs[n + 2]
        token = refs[2 * n + 3]
        x, y, c, _, chips = _place()
        for a in range(n):
            for t, (px, py) in enumerate(chips):
                blk = _half(ins[a].at[2 * px + py], c, stacks[a].shape[1])
                _remote(blk, blk, send.at[3 * a + t], recv.at[3 * a + t], (x, y, 1 - c)).start()
        token[...] = jnp.zeros_like(token)

    sems = pltpu.SemaphoreType.DMA((3 * n,))
    res = pl.pallas_call(
        body, name=f"forward_start_{tag}", in_specs=[HBM] * n + [ANY],
        out_specs=[SEM, SEM] + [HBM] * n + [pl.BlockSpec(memory_space=pltpu.VMEM)],
        out_shape=[sems, sems] + [pltpu.HBM(s.shape, s.dtype) for s in stacks] + [TOKEN],
        input_output_aliases={a: a + 2 for a in range(n)}, compiler_params=_in_flight_params(),
    )(*[_in_hbm(s) for s in stacks], after)
    return res[0], res[1], res[2:2 + n], res[2 + n]


def _forward_wait(send, recv, stacks, after, tag):
    n = len(stacks)

    def body(*refs):
        ins = refs[:n]
        send_ref, recv_ref = refs[n], refs[n + 1]
        x, y, c, _, chips = _place()
        for a in range(n):
            for t, (px, py) in enumerate(chips):
                mine = _half(ins[a].at[2 * px + py], c, stacks[a].shape[1])
                theirs = _half(ins[a].at[2 * px + py], 1 - c, stacks[a].shape[1])
                _remote(mine, mine, send_ref.at[3 * a + t], recv_ref.at[3 * a + t], (x, y, 1 - c)).wait_send()
                _remote(theirs, theirs, send_ref.at[3 * a + t], recv_ref.at[3 * a + t], (x, y, 1 - c)).wait_recv()

    return pl.pallas_call(
        body, name=f"forward_wait_{tag}", in_specs=[HBM] * n + [SEM, SEM, ANY], out_specs=[HBM] * n,
        out_shape=[pltpu.HBM(s.shape, s.dtype) for s in stacks],
        input_output_aliases={a: a for a in range(n)}, compiler_params=_in_flight_params(),
    )(*stacks, send, recv, after)


def _swap_start(grads, tag):
    n = len(grads)

    def body(*refs):
        ins, gots = refs[:n], refs[n:2 * n]
        send, recv = refs[2 * n], refs[2 * n + 1]
        token = refs[4 * n + 2]
        x, y, c, _, _ = _place()
        for a in range(n):
            h = grads[a].shape[1] // 2
            _remote(ins[a].at[:, pl.ds((1 - c) * h, h)], gots[a], send.at[a], recv.at[a], (x, y, 1 - c)).start()
        token[...] = jnp.zeros_like(token)

    sems = pltpu.SemaphoreType.DMA((n,))
    halves = [(g.shape[0], g.shape[1] // 2, g.shape[2]) for g in grads]
    res = pl.pallas_call(
        body, name=f"swap_start_{tag}", in_specs=[HBM] * (2 * n),
        out_specs=[SEM, SEM] + [HBM] * (2 * n) + [pl.BlockSpec(memory_space=pltpu.VMEM)],
        out_shape=[sems, sems] + [pltpu.HBM(g.shape, g.dtype) for g in grads] + [pltpu.HBM(s, F32) for s in halves] + [TOKEN],
        input_output_aliases={a: a + 2 for a in range(2 * n)}, compiler_params=_in_flight_params(),
    )(*[_in_hbm(g) for g in grads], *[_in_hbm(lax.empty(s, F32)) for s in halves])
    return res[0], res[1], res[2:2 + n], res[2 + n:2 + 2 * n], res[2 + 2 * n]


def _swap_wait(send, recv, grads, gots, after, tag):
    n = len(grads)

    def body(*refs):
        ins, lnd = refs[:n], refs[n:2 * n]
        send_ref, recv_ref = refs[2 * n], refs[2 * n + 1]
        x, y, c, _, _ = _place()
        for a in range(n):
            h = grads[a].shape[1] // 2
            cp = _remote(ins[a].at[:, pl.ds((1 - c) * h, h)], lnd[a], send_ref.at[a], recv_ref.at[a], (x, y, 1 - c))
            cp.wait_send()
            cp.wait_recv()

    bufs = [pltpu.HBM(g.shape, g.dtype) for g in grads] + [pltpu.HBM(g.shape, g.dtype) for g in gots]
    res = pl.pallas_call(
        body, name=f"swap_wait_{tag}", in_specs=[HBM] * (2 * n) + [SEM, SEM, ANY], out_specs=[HBM] * (2 * n),
        out_shape=bufs, input_output_aliases={a: a for a in range(2 * n)}, compiler_params=_in_flight_params(),
    )(*grads, *gots, send, recv, after)
    return res[:n], res[n:]


def _exchange_start(parts, tag):
    n = len(parts)

    def body(*refs):
        ins, lands = refs[:n], refs[n:2 * n]
        send, recv = refs[2 * n], refs[2 * n + 1]
        token = refs[4 * n + 2]
        _, _, c, j, chips = _place()
        for t, (px, py) in enumerate(chips):
            for a in range(n):
                _remote(ins[a].at[2 * px + py], lands[a].at[j], send.at[3 * a + t], recv.at[3 * a + t], (px, py, c)).start()
        token[...] = jnp.zeros_like(token)

    sems = pltpu.SemaphoreType.DMA((3 * n,))
    bufs = [pltpu.HBM(p.shape, p.dtype) for p in parts]
    res = pl.pallas_call(
        body, name=f"exchange_start_{tag}", in_specs=[HBM] * (2 * n),
        out_specs=[SEM, SEM] + [HBM] * (2 * n) + [pl.BlockSpec(memory_space=pltpu.VMEM)],
        out_shape=[sems, sems] + bufs + bufs + [TOKEN],
        input_output_aliases={a: a + 2 for a in range(2 * n)}, compiler_params=_in_flight_params(),
    )(*[_in_hbm(p) for p in parts], *[_in_hbm(lax.empty(p.shape, p.dtype)) for p in parts])
    return res[0], res[1], res[2:2 + n], res[2 + n:2 + 2 * n], res[2 + 2 * n]


def _exchange_wait(send, recv, parts, lands, after, tag):
    n = len(parts)

    def body(*refs):
        ins, lnd = refs[:n], refs[n:2 * n]
        send_ref, recv_ref = refs[2 * n], refs[2 * n + 1]
        _, _, c, j, chips = _place()
        for t, (px, py) in enumerate(chips):
            jt = 2 * px + py
            for a in range(n):
                _remote(ins[a].at[jt], lnd[a].at[j], send_ref.at[3 * a + t], recv_ref.at[3 * a + t], (px, py, c)).wait_send()
                _remote(ins[a].at[jt], lnd[a].at[jt], send_ref.at[3 * a + t], recv_ref.at[3 * a + t], (px, py, c)).wait_recv()

    bufs = [pltpu.HBM(p.shape, p.dtype) for p in parts]
    res = pl.pallas_call(
        body, name=f"exchange_wait_{tag}", in_specs=[HBM] * (2 * n) + [SEM, SEM, ANY], out_specs=[HBM] * (2 * n),
        out_shape=bufs + bufs, input_output_aliases={a: a for a in range(2 * n)}, compiler_params=_in_flight_params(),
    )(*parts, *lands, send, recv, after)
    return res[:n], res[n:]


def _small_chip_sums(arrs):
    n = len(arrs)

    def body(*refs):
        ins, outs = refs[:n], refs[n:2 * n]
        sib = refs[2 * n:3 * n]
        send, recv = refs[3 * n:]
        x, y, c, j, _ = _place()
        swaps = [_remote(ins[a], sib[a], send.at[a], recv.at[a], (x, y, 1 - c)) for a in range(n)]
        for cp in swaps:
            cp.start()
        for a in range(n):
            swaps[a].wait_recv()
            outs[a][j] = ins[a][...] + sib[a][...]
        for cp in swaps:
            cp.wait_send()

    sem = pltpu.SemaphoreType.DMA
    vm = pl.BlockSpec(memory_space=pltpu.VMEM)
    return pl.pallas_call(
        body, name="small_chip_sums", in_specs=[vm] * n, out_specs=[vm] * n,
        out_shape=[jax.ShapeDtypeStruct((N_SHARD, *a.shape), F32) for a in arrs],
        scratch_shapes=[pltpu.VMEM(a.shape, F32) for a in arrs] + [sem((n,)), sem((n,))],
        compiler_params=_cp(),
    )(*arrs)


def _small_totals(stacks):
    n = len(stacks)

    def body(*refs):
        for a in range(n):
            refs[n + a][...] = ((refs[a][0] + refs[a][1]) + refs[a][2]) + refs[a][3]

    return pl.pallas_call(body, name="small_totals", out_shape=[jax.ShapeDtypeStruct(s.shape[1:], F32) for s in stacks],
                          compiler_params=_cp())(*stacks)


SMALL_1024 = ("ln1_g", "ln1_b", "ln2_g", "ln2_b", "b_ple_gate", "ln3_g", "ln3_b")


def _adamw_small(red3, red1, redz, g_conv_w, redc, red_ws, red_bs, params):
    shape2d = {"ln_z_g": (1, D_GMLP), "ln_z_b": (1, D_GMLP), "w_s": (N_HEADS * BLK, BLK), "b_s": (N_HEADS, BLK),
               "conv_w": (3, FF_BLK), "conv_b": (1, D_FF), **{k: (1, D_MODEL) for k in SMALL_1024}}
    names = list(shape2d)
    flat = [a.reshape(shape2d[k]) for k in names for a in params[k]]

    def body(r3, r1, rz, gcw, rc, rws, rbs, *refs):
        ins, outs = refs[:3 * len(names)], refs[3 * len(names):]

        def grad_of(k):
            if k == "w_s":
                return rws[...]
            if k == "b_s":
                return rbs[...]
            if k == "conv_w":
                return gcw[0:3, :]
            src, row = {"ln3_g": (r3, 0), "ln3_b": (r3, 1), "b_ple_gate": (r3, 2), "ln2_g": (r3, 3), "ln2_b": (r3, 4),
                        "ln1_g": (r1, 0), "ln1_b": (r1, 1), "ln_z_g": (rz, 0), "ln_z_b": (rz, 1)}[k]
            return src[row:row + 1, :]

        for i, k in enumerate(names):
            w_ref, m_ref, v_ref = ins[3 * i:3 * i + 3]
            g_ref, d_ref, nm_ref, nv_ref = outs[4 * i:4 * i + 4]
            if k == "conv_b":
                for j in range(N_SHARD):
                    cols = slice(j * FF_BLK, (j + 1) * FF_BLK)
                    g = rc[j * STAT_ROWS + 3:j * STAT_ROWS + 4, :]
                    g_ref[:, cols] = g
                    d_ref[:, cols], nm_ref[:, cols], nv_ref[:, cols] = _adamw_math(w_ref[:, cols], g, m_ref[:, cols], v_ref[:, cols])
                continue
            g = grad_of(k)
            g_ref[...] = g
            d_ref[...], nm_ref[...], nv_ref[...] = _adamw_math(w_ref[...], g, m_ref[...], v_ref[...])

    res = pl.pallas_call(
        body, name="adamw_small",
        out_shape=[jax.ShapeDtypeStruct(shape2d[k], F32) for k in names for _ in range(4)],
        compiler_params=_cp(),
    )(red3, red1, redz, g_conv_w, redc, red_ws, red_bs, *flat)
    return {k: tuple(r.reshape(params[k][0].shape) for r in res[4 * i:4 * i + 4]) for i, k in enumerate(names)}


WEIGHTS = ("w_in", "ln_z_g", "ln_z_b", "w_s", "b_s", "w_o", "ln1_g", "ln1_b", "w_ff_a", "w_ff_b", "conv_w", "conv_b",
           "w_ff_down", "ln2_g", "ln2_b", "w_ple_gate", "b_ple_gate", "w_ple_in", "ln3_g", "ln3_b")
BIG = ("w_in", "w_o", "w_ff_a", "w_ff_b", "w_ff_down", "w_ple_gate", "w_ple_in")
TRANSPOSED = ("w_ff_a", "w_ff_b")
LATE = ("w_o", "w_ff_a", "w_ff_b", "w_ff_down", "w_ple_gate", "w_ple_in", "conv_w")


def kernel(x, p, positions, w_in, ln_z_g, ln_z_b, w_s, b_s, w_o, ln1_g, ln1_b, w_ff_a, w_ff_b, conv_w, conv_b, w_ff_down, ln2_g, ln2_b, w_ple_gate, b_ple_gate, w_ple_in, ln3_g, ln3_b, loss_target, m_w_in, m_ln_z_g, m_ln_z_b, m_w_s, m_b_s, m_w_o, m_ln1_g, m_ln1_b, m_w_ff_a, m_w_ff_b, m_conv_w, m_conv_b, m_w_ff_down, m_ln2_g, m_ln2_b, m_w_ple_gate, m_b_ple_gate, m_w_ple_in, m_ln3_g, m_ln3_b, v_w_in, v_ln_z_g, v_ln_z_b, v_w_s, v_b_s, v_w_o, v_ln1_g, v_ln1_b, v_w_ff_a, v_w_ff_b, v_conv_w, v_conv_b, v_w_ff_down, v_ln2_g, v_ln2_b, v_w_ple_gate, v_b_ple_gate, v_w_ple_in, v_ln3_g, v_ln3_b):
    args = locals()
    w = {k: args[k] for k in WEIGHTS}
    m = {k: args["m_" + k] for k in WEIGHTS}
    v = {k: args["v_" + k] for k in WEIGHTS}

    for k in TRANSPOSED:
        w[k], m[k], v[k] = (jnp.swapaxes(a, 1, 2) for a in (w[k], m[k], v[k]))

    chip = 2 * lax.axis_index("x") + lax.axis_index("y")
    place = jnp.stack([chip, lax.axis_index("c")]).astype(jnp.int32)
    stack = dict(zip(["w_in"], _place_shards("cast_w_in", [w["w_in"][0]], [MXU], place, place)))
    i_send, i_recv, in_flight, dep = _gather_start([stack["w_in"]], [True], place, "w_in")
    stack.update(zip(LATE, _place_shards("cast_late", [w[k][0] for k in LATE],
                                         [F32 if k == "conv_w" else MXU for k in LATE], place, dep)))
    split_late = [k != "conv_w" for k in LATE]
    g_send, g_recv, late_flight, start_dep = _gather_start([stack[k] for k in LATE], split_late, place, "late")
    rope = _rope_tables(positions, x.shape[1], start_dep)
    landed_in = _gather_wait(i_send, i_recv, in_flight, [True], rope[0], "w_in")
    w_in_full, = _gather_forward(landed_in, [True], "w_in")
    halves =[k for k, sp in zip(LATE, split_late) if sp]
    trips = {}

    def late_landed(after):
        fw = dict(zip(LATE, _gather_wait(g_send, g_recv, late_flight, split_late, after, "late")))
        trips["late"] = (fw, *_forward_start([fw[k] for k in halves], fw["conv_w"], "late"))
        return trips["late"][-1]

    def late_weights(after):
        fw, send, recv, flight, _ = trips["late"]
        fw.update(zip(halves, _forward_wait(send, recv, flight, after, "late")))
        return (fw["w_o"].reshape(D_MODEL, D_MODEL), fw["w_ff_a"], fw["w_ff_b"], fw["conv_w"], fw["w_ff_down"],
                fw["w_ple_gate"].reshape(D_MODEL, D_MODEL), fw["w_ple_in"])

    def swap_started(names, grads, tag):
        stacked = [g.reshape(N_SHARD, *w[k].shape[1:]) for k, g in zip(names, grads)]
        return (names, tag, *_swap_start(stacked, tag))

    def partial_sums(swap, after):
        names, tag, send, recv, stacked, gots, _ = swap
        stacked, got = _swap_wait(send, recv, stacked, gots, after, tag)
        pair = _pair_sums(f"rs_pair_{tag}", stacked, got, place)
        return (names, tag, *_exchange_start(pair, tag))

    def chip_summed(trip, after, dep):
        names, tag, send, recv, pair, lands, _ = trip
        pair, landed = _exchange_wait(send, recv, pair, lands, after, tag)
        return _chip_sums(f"rs_sum_{tag}", pair, landed, place, dep), names, tag

    def reduced(trip, after, dep):
        blocks, names, tag = chip_summed(trip, after, dep)
        return dict(zip(names, _sibling_join(blocks, tag)))

    def early_grads_landed(after):
        blocks, names, tag = chip_summed(trips["early"], after, trips["small"][-1])
        trips["join"] = (names, *_join_start(blocks, after, tag))
        return trips["join"][-1]

    def early_grads(grads):
        trips["swap"] = swap_started(list(grads), list(grads.values()), "early")
        return trips["swap"][-1]

    def early_grads_sent(after, small):
        trips["early"] = partial_sums(trips["swap"], after)
        stat3, stat1, zstat, cstat, dws, dbs = small
        sums = _small_chip_sums([stat3, stat1, zstat, cstat.reshape(N_SHARD * STAT_ROWS, FF_BLK),
                                 dws.reshape(N_HEADS * BLK, BLK), dbs])
        trips["small"] = _gather_start(sums, [False] * len(sums), trips["early"][-1], "small")
        return trips["small"][-1]

    grad_x, g_w_in = _local_step(
        x[0], p[0, 0], rope, loss_target[0], w_in_full, start_dep, late_landed, late_weights, early_grads, early_grads_sent,
        early_grads_landed, ln_z_g, ln_z_b, w_s, b_s, ln1_g, ln1_b, conv_b, ln2_g, ln2_b, b_ple_gate, ln3_g, ln3_b)

    swap_in = swap_started(["w_in"], [g_w_in], "w_in")
    trips["w_in"] = partial_sums(swap_in, swap_in[-1])
    out = {}

    def adamw(red, tag):
        names = list(red)
        steps = _adamw_shards(f"adamw_{tag}", [w[k] for k in names], [red[k] for k in names], [m[k] for k in names],
                              [v[k] for k in names])
        out.update(zip(names, steps))

    names, j_send, j_recv, j_flight, _ = trips["join"]
    adamw(dict(zip(names, _join_wait(j_send, j_recv, j_flight, trips["w_in"][-1], "early"))), "early")
    adamw(reduced(trips["w_in"], out["w_o"][3], start_dep), "w_in")
    for k in TRANSPOSED:
        out[k] = tuple(jnp.swapaxes(a, 1, 2) for a in out[k])

    s_send, s_recv, s_flight, _ = trips["small"]
    red3, red1, redz, redc, red_ws, red_bs = _small_totals(
        _gather_wait(s_send, s_recv, s_flight, [False] * len(s_flight), out["w_in"][3], "small"))
    loss = (0.5 / D_MODEL) * jnp.sum(red3[5])
    g_conv_w = lax.dynamic_slice_in_dim(redc, chip * STAT_ROWS, STAT_ROWS, 0)
    names_small = [k for k in WEIGHTS if k not in BIG]
    out.update(_adamw_small(red3, red1, redz, g_conv_w, redc, red_ws, red_bs, {k: (w[k], m[k], v[k]) for k in names_small}))

    return (loss, grad_x[None], *[out[k][0] for k in WEIGHTS], *[out[k][1] for k in WEIGHTS],
            *[out[k][2] for k in WEIGHTS], *[out[k][3] for k in WEIGHTS])
```

```python
import math

import numpy as np
import jax
import jax.numpy as jnp
from jax import lax
from jax.experimental import pallas as pl
from jax.experimental.pallas import tpu as pltpu

F32 = jnp.float32
BF16 = jnp.bfloat16
MXU = BF16

D_MODEL = 1024
HEAD_DIM = 64
N_HEADS = 8
D_ATTN = 512
D_GMLP = 512
D_IN = 2560
DILATIONS = (1, 4, 16)
BLK = 128
ROPE_THETA = 500000.0
ROPE_DIM = 16
D_FF = 2816
D_PLE = 256
LN_EPS = 1e-5
ALPHA = 2.0 ** 0.25
NEG_INF = -1e30
N_SHARD = 4
W_IN_BLK = D_IN // N_SHARD
FF_BLK = D_FF // N_SHARD
ROW_BLK = D_MODEL // N_SHARD
ADAM_LR, ADAM_B1, ADAM_B2, ADAM_EPS, ADAM_WD, ADAM_STEP = 0.001, 0.9, 0.999, 1e-08, 0.01, 10

TM = 512
HALO = 8
ROW_GROUPS = 2
VMEM_LIMIT = 56 * 1024 * 1024


def _cp(**kw):
    return pltpu.CompilerParams(vmem_limit_bytes=VMEM_LIMIT, **kw)


def _full(shape):
    n = len(shape)
    return pl.BlockSpec(shape, lambda *_: (0,) * n)


def _gelu(x):
    return 0.5 * x * (1.0 + lax.erf(x * (1.0 / math.sqrt(2.0))))


def _gelu_grad(x):
    return 0.5 * (1.0 + lax.erf(x * (1.0 / math.sqrt(2.0)))) + x * jnp.exp(-0.5 * x * x) * (1.0 / math.sqrt(2.0 * math.pi))


def _ln_fwd(r):
    mu = jnp.mean(r, axis=-1, keepdims=True)
    xc = r - mu
    var = jnp.mean(xc * xc, axis=-1, keepdims=True)
    rstd = lax.rsqrt(var + LN_EPS)
    return xc * rstd, rstd


def _ln_bwd(dy, xhat, rstd, g):
    dxh = dy * g
    m1 = jnp.mean(dxh, axis=-1, keepdims=True)
    m2 = jnp.mean(dxh * xhat, axis=-1, keepdims=True)
    return rstd * (dxh - m1 - xhat * m2)


def _dot(a, b):
    return jnp.dot(a.astype(MXU), b.astype(MXU), preferred_element_type=F32)


def _dot_nt(a, b):
    return lax.dot_general(a.astype(MXU), b.astype(MXU), (((1,), (1,)), ((), ())), preferred_element_type=F32)


def _dot_tn(a, b):
    return lax.dot_general(a.astype(MXU), b.astype(MXU), (((0,), (0,)), ((), ())), preferred_element_type=F32)


def _colsum(v):
    return jnp.sum(v, axis=0, keepdims=True)


def _rope_tables(positions, t, dep):
    inv = np.float32(ROPE_THETA) ** (-np.arange(0, ROPE_DIM, 2, dtype=np.float32) / np.float32(ROPE_DIM))
    half = ROPE_DIM // 2
    pos_rep = jnp.repeat(positions.reshape(t // 16, 16), half, axis=1)
    inv_row = jnp.asarray(np.tile(inv, 16)[None, :], F32)

    def trig_body(pos_ref, inv_ref, dep_ref, cos_ref, sin_ref):
        ang = pos_ref[...].astype(F32) * inv_ref[...]
        cos_ref[...] = jnp.cos(ang)
        sin_ref[...] = jnp.sin(ang)

    vm = pl.BlockSpec(memory_space=pltpu.VMEM)
    cos8, sin8 = pl.pallas_call(
        trig_body, name="rope_trig", in_specs=[vm, vm, pl.BlockSpec(memory_space=pl.ANY)], out_specs=[vm, vm],
        out_shape=(jax.ShapeDtypeStruct((t // 16, 128), F32), jax.ShapeDtypeStruct((t // 16, 128), F32)),
    )(pos_rep, inv_row, dep)
    cos8 = cos8.reshape(t, half)
    sin8 = sin8.reshape(t, half)

    lane = np.arange(128) % HEAD_DIM
    sel = (np.arange(half)[:, None] == (lane % half)[None, :])
    e_cos = (sel & (lane < ROPE_DIM)[None, :]).astype(np.float32)
    e_s1 = -(sel & (lane < half)[None, :]).astype(np.float32)
    e_s2 = (sel & ((lane >= half) & (lane < ROPE_DIM))[None, :]).astype(np.float32)
    ones = (lane >= ROPE_DIM).astype(np.float32)[None, :]

    def expand_body(cos_ref, sin_ref, ec_ref, e1_ref, e2_ref, ones_ref, c_ref, s1_ref, s2_ref):
        c_ref[...] = _dot_select(cos_ref[...], ec_ref[...], terms=3) + ones_ref[...]
        s1_ref[...] = _dot_select(sin_ref[...], e1_ref[...], terms=3)
        s2_ref[...] = _dot_select(sin_ref[...], e2_ref[...], terms=3)

    tab = jax.ShapeDtypeStruct((t, 128), F32)
    return pl.pallas_call(expand_body, name="rope_expand", out_shape=(tab, tab, tab), compiler_params=_cp())(
        cos8, sin8, jnp.asarray(e_cos), jnp.asarray(e_s1), jnp.asarray(e_s2), jnp.asarray(ones))


def _tile_heads(tab):
    return jnp.concatenate([tab] * (D_ATTN // 128), axis=1)


def _rope_apply(v, c, s1, s2):
    n = v.shape[1]
    half = ROPE_DIM // 2
    return v * c + pltpu.roll(v, n - half, 1) * s1 + pltpu.roll(v, half, 1) * s2


def _rope_apply_t(g, c, s1, s2):
    n = g.shape[1]
    half = ROPE_DIM // 2
    return g * c + pltpu.roll(g * s1, half, 1) + pltpu.roll(g * s2, n - half, 1)


LANE_CHUNKS = D_ATTN // 128
HEAD_LANES = 128 // N_HEADS


def _perm_shape(t, d, w, dtype):
    return jax.ShapeDtypeStruct((d, t // d, w), dtype)


def _perm_tile(d, w):
    return pl.BlockSpec((None if d == 1 else d, TM // d, w), lambda i: (0, i, 0))


def _to_planes(ref, scr, d, n_chunks, dtype):
    for r in range(d):
        for cc in range(n_chunks):
            ref[r, :, cc * 128:(cc + 1) * 128] = scr.at[cc][pl.ds(r, TM // d, stride=d), :].astype(dtype)


def _from_planes(ref, scr, d, n_chunks, accumulate=False):
    for r in range(d):
        for cc in range(n_chunks):
            rows = scr.at[cc]
            val = ref[r, :, cc * 128:(cc + 1) * 128].astype(F32)
            if accumulate:
                rows[pl.ds(r, TM // d, stride=d), :] += val
            else:
                rows[pl.ds(r, TM // d, stride=d), :] = val


def _chunks(val):
    return [val[:, cc * 128:(cc + 1) * 128] for cc in range(val.shape[1] // 128)]


def _unchunk(scr, n_chunks, base=0):
    return jnp.concatenate([scr[base + cc] for cc in range(n_chunks)], axis=1)


def _head_expand():
    src = np.arange(128)[:, None]
    dst = np.arange(D_ATTN)[None, :]
    return jnp.asarray((src == (dst // HEAD_DIM) * HEAD_LANES).astype(np.float32))


def _head_reduce():
    src = np.arange(D_ATTN)[:, None]
    dst = np.arange(128)[None, :]
    return jnp.asarray((src // HEAD_DIM == dst // HEAD_LANES).astype(np.float32))


def _dot_select(a, sel, terms=2):
    sel = sel.astype(BF16)
    out, rest = None, a
    for _ in range(terms):
        part = rest.astype(BF16)
        rest = rest - part.astype(F32)
        prod = jnp.dot(part, sel, preferred_element_type=F32)
        out = prod if out is None else out + prod
    return out


def _qkvuz(x, w_in, c_tab, s1_tab, s2_tab, ln_z_g, ln_z_b, w_s, b_full, dep):
    t = x.shape[0]
    nchunk = TM // BLK

    def body(x_ref, w_ref, c_ref, s1_ref, s2_ref, g_ref, b_ref, ws_ref, bf_ref, dep_ref,
             qkv1_ref, qkv4_ref, qkv16_ref, hu_ref, hz_ref, mixed_ref, gm_ref, xb_ref, h_scr, wm_scr, p_scr):
        @pl.when(pl.program_id(0) == 0)
        def _():
            row = lax.broadcasted_iota(jnp.int32, (BLK, BLK), 0)
            col = lax.broadcasted_iota(jnp.int32, (BLK, BLK), 1)
            for g in range(N_HEADS):
                wm_scr[g] = jnp.where(col <= row, ws_ref[g], 0.0).astype(MXU)

        xb = x_ref[...].astype(MXU)
        xb_ref[...] = xb
        for j in range(N_SHARD):
            h_scr[:, j * W_IN_BLK:(j + 1) * W_IN_BLK] = jnp.dot(xb, w_ref[j], preferred_element_type=F32)
        c, s1, s2 = _tile_heads(c_ref[...]), _tile_heads(s1_ref[...]), _tile_heads(s2_ref[...])
        q = _rope_apply(h_scr[:, 0:D_ATTN], c, s1, s2) * (1.0 / math.sqrt(HEAD_DIM))
        k = _rope_apply(h_scr[:, D_ATTN:2 * D_ATTN], c, s1, s2)
        for part, val in enumerate((q, k, h_scr[:, 2 * D_ATTN:3 * D_ATTN])):
            qkv1_ref[:, part * D_ATTN:(part + 1) * D_ATTN] = val.astype(MXU)
            for cc in range(LANE_CHUNKS):
                p_scr[part * LANE_CHUNKS + cc] = val[:, cc * 128:(cc + 1) * 128]
        _to_planes(qkv4_ref, p_scr, DILATIONS[1], 3 * LANE_CHUNKS, MXU)
        _to_planes(qkv16_ref, p_scr, DILATIONS[2], 3 * LANE_CHUNKS, MXU)
        hu = h_scr[:, 3 * D_ATTN:3 * D_ATTN + D_GMLP]
        hz = h_scr[:, 3 * D_ATTN + D_GMLP:]
        hu_ref[...] = hu
        hz_ref[...] = hz
        zhat, _ = _ln_fwd(_gelu(hz))
        zn = (zhat * g_ref[...] + b_ref[...]).astype(MXU)
        for ch in range(nchunk):
            rows = slice(ch * BLK, (ch + 1) * BLK)
            for g in range(N_HEADS):
                cols = slice(g * HEAD_DIM, (g + 1) * HEAD_DIM)
                mixed_ref[rows, cols] = jnp.dot(wm_scr[g], zn[rows, cols], preferred_element_type=F32) + bf_ref[:, cols]
        gm_ref[...] = (_gelu(hu) * mixed_ref[...]).astype(MXU)

    tok = lambda w: pl.BlockSpec((TM, w), lambda i: (i, 0))
    outs = [_perm_shape(t, d, 3 * D_ATTN, MXU) for d in DILATIONS] + [jax.ShapeDtypeStruct((t, D_GMLP), F32)] * 3 + [
        jax.ShapeDtypeStruct((t, D_GMLP), MXU), jax.ShapeDtypeStruct((t, D_MODEL), MXU)]
    return pl.pallas_call(
        body, name="qkvuz", grid=(t // TM,),
        in_specs=[tok(D_MODEL), _full(w_in.shape), tok(128), tok(128), tok(128), _full(ln_z_g.shape), _full(ln_z_b.shape),
                  _full(w_s.shape), _full(b_full.shape), pl.BlockSpec(memory_space=pl.ANY)],
        out_specs=[_perm_tile(d, 3 * D_ATTN) for d in DILATIONS] + [tok(D_ATTN)] * 4 + [tok(D_MODEL)], out_shape=outs,
        scratch_shapes=[pltpu.VMEM((TM, D_IN), F32), pltpu.VMEM((N_HEADS, BLK, BLK), MXU),
                        pltpu.VMEM((3 * LANE_CHUNKS, TM, 128), F32)],
        compiler_params=_cp(dimension_semantics=("arbitrary",)),
    )(x, w_in, c_tab, s1_tab, s2_tab, ln_z_g, ln_z_b, w_s, b_full, dep)


def _band_valid(n):
    i = lax.broadcasted_iota(jnp.int32, (BLK, 2 * BLK), 0)
    j = lax.broadcasted_iota(jnp.int32, (BLK, 2 * BLK), 1)
    return (j >= i) & (j <= i + BLK) & ((j >= BLK) | (n > 0))


def _attn_fwd(qkv, d, dep):
    _, l_sub, _ = qkv.shape
    nb = l_sub // BLK

    def body(q_ref, kp_ref, kc_ref, vp_ref, vc_ref, dep_ref, o_ref, l_ref):
        valid = _band_valid(pl.program_id(1))
        kcat = jnp.concatenate([kp_ref[...], kc_ref[...]], axis=0)
        vcat = jnp.concatenate([vp_ref[...], vc_ref[...]], axis=0)
        for h in range(N_HEADS):
            cols = slice(h * HEAD_DIM, (h + 1) * HEAD_DIM)
            s = jnp.where(valid, _dot_nt(q_ref[:, cols], kcat[:, cols]), NEG_INF)
            m = jnp.max(s, axis=-1, keepdims=True)
            e = jnp.exp(s - m)
            den = jnp.sum(e, axis=-1, keepdims=True)
            o_ref[:, cols] = _dot(e, vcat[:, cols]) * (1.0 / den)
            l_ref[:, h * HEAD_LANES:(h + 1) * HEAD_LANES] = jnp.broadcast_to(m + jnp.log(den), (BLK, HEAD_LANES))

    def blk(w, col, prev=False):
        return pl.BlockSpec((None, BLK, w), lambda r, n: (r, jnp.maximum(n - 1, 0) if prev else n, col))

    return pl.pallas_call(
        body, name=f"attn_fwd_d{d}", grid=(d, nb),
        in_specs=[blk(D_ATTN, 0), blk(D_ATTN, 1, True), blk(D_ATTN, 1), blk(D_ATTN, 2, True), blk(D_ATTN, 2),
                  pl.BlockSpec(memory_space=pl.ANY)],
        out_specs=[blk(D_ATTN, 0), blk(128, 0)],
        out_shape=[jax.ShapeDtypeStruct((d, l_sub, D_ATTN), F32), jax.ShapeDtypeStruct((d, l_sub, 128), F32)],
        compiler_params=_cp(dimension_semantics=("arbitrary", "arbitrary")),
    )(qkv, qkv, qkv, qkv, qkv, dep)


def _attn_bwd(qkv, do, lse, delta, d, dep):
    _, l_sub, _ = qkv.shape
    nb = l_sub // BLK
    whole = l_sub <= 8 * BLK

    def shares(n, q_ref, kp_ref, kc_ref, vp_ref, vc_ref, do_ref, l_ref, dl_ref, dq_ref):
        valid = _band_valid(n)
        kcat = jnp.concatenate([kp_ref[...], kc_ref[...]], axis=0)
        vcat = jnp.concatenate([vp_ref[...], vc_ref[...]], axis=0)
        for h in range(N_HEADS):
            cols = slice(h * HEAD_DIM, (h + 1) * HEAD_DIM)
            stat = slice(h * HEAD_LANES, h * HEAD_LANES + 1)
            qh, doh = q_ref[:, cols], do_ref[:, cols]
            p = jnp.where(valid, jnp.exp(_dot_nt(qh, kcat[:, cols]) - l_ref[:, stat]), 0.0)
            ds = p * (_dot_nt(doh, vcat[:, cols]) - dl_ref[:, stat])
            dq_ref[:, cols] = _dot(ds, kcat[:, cols])
            yield cols, _dot_tn(ds, qh), _dot_tn(p, doh)

    def body_whole(*refs):
        dk_ref, dv_ref = refs[10:]
        n = pl.program_id(1)
        cur = pl.ds(pl.multiple_of(n * BLK, BLK), BLK)
        prev = pl.ds(pl.multiple_of(jnp.maximum(n - 1, 0) * BLK, BLK), BLK)
        for cols, dk2, dv2 in shares(n, *refs[:8], refs[9]):
            dk_ref[cur, cols] = dk2[BLK:]
            dv_ref[cur, cols] = dv2[BLK:]
            dk_ref[prev, cols] += dk2[0:BLK]
            dv_ref[prev, cols] += dv2[0:BLK]

    def body_carry(*refs):
        dk_ref, dv_ref, ck_scr, cv_scr = refs[10:]
        n = pl.program_id(1)

        @pl.when(n == 0)
        def _():
            ck_scr[...] = jnp.zeros_like(ck_scr)
            cv_scr[...] = jnp.zeros_like(cv_scr)

        @pl.when(n < nb)
        def _():
            for cols, dk2, dv2 in shares(n, *refs[:8], refs[9]):
                dk_ref[:, cols] = ck_scr[:, cols] + dk2[0:BLK]
                dv_ref[:, cols] = cv_scr[:, cols] + dv2[0:BLK]
                ck_scr[:, cols] = dk2[BLK:]
                cv_scr[:, cols] = dv2[BLK:]

        @pl.when(n == nb)
        def _():
            dk_ref[...] = ck_scr[...]
            dv_ref[...] = cv_scr[...]

    def blk(w, col, shift=0):
        return pl.BlockSpec((None, BLK, w), lambda r, n: (r, jnp.clip(n - shift, 0, nb - 1), col))

    if whole:
        dkv_spec = pl.BlockSpec((None, l_sub, D_ATTN), lambda r, n: (r, 0, 0))
        body, steps, scratch = body_whole, nb, []
    else:
        dkv_spec = blk(D_ATTN, 0, 1)
        body, steps, scratch = body_carry, nb + 1, [pltpu.VMEM((BLK, D_ATTN), F32)] * 2
    return pl.pallas_call(
        body, name=f"attn_bwd_d{d}", grid=(d, steps),
        in_specs=[blk(D_ATTN, 0), blk(D_ATTN, 1, 1), blk(D_ATTN, 1), blk(D_ATTN, 2, 1), blk(D_ATTN, 2),
                  blk(D_ATTN, 0), blk(128, 0), blk(128, 0), pl.BlockSpec(memory_space=pl.ANY)],
        out_specs=[blk(D_ATTN, 0), dkv_spec, dkv_spec],
        out_shape=[jax.ShapeDtypeStruct((d, l_sub, D_ATTN), F32)] * 3,
        scratch_shapes=scratch,
        compiler_params=_cp(dimension_semantics=("arbitrary", "arbitrary")),
    )(qkv, qkv, qkv, qkv, qkv, do, lse, delta, dep)


def _mix_ln1(os_, ls_, gm, x, w_o, ln1_g, ln1_b, dep):
    t = x.shape[0]
    expand = _head_expand()

    def body(o1, o4, o16, l1, l4, l16, gm_ref, x_ref, wo_ref, g_ref, b_ref, ex_ref, dep_ref,
             attn_ref, lse1_ref, lse4_ref, lse16_ref, cat_ref, xhat_ref, rstd_ref, x1b_ref, o_scr, l_scr):
        _from_planes(o4, o_scr, DILATIONS[1], LANE_CHUNKS)
        _from_planes(o16, o_scr.at[pl.ds(LANE_CHUNKS, LANE_CHUNKS)], DILATIONS[2], LANE_CHUNKS)
        _from_planes(l4, l_scr, DILATIONS[1], 1)
        _from_planes(l16, l_scr.at[pl.ds(1, 1)], DILATIONS[2], 1)
        la, lb, lc = l1[...], l_scr[0], l_scr[1]
        m = jnp.maximum(jnp.maximum(la, lb), lc)
        ea, eb, ec = jnp.exp(la - m), jnp.exp(lb - m), jnp.exp(lc - m)
        den = ea + eb + ec
        inv = 1.0 / den
        wide = lambda w: _dot_select(w, ex_ref[...])
        attn = (wide(ea * inv) * o1[...] + wide(eb * inv) * _unchunk(o_scr, LANE_CHUNKS)
                + wide(ec * inv) * _unchunk(o_scr, LANE_CHUNKS, LANE_CHUNKS))
        attn_ref[...] = attn
        lse = m + jnp.log(den)
        lse1_ref[...] = lse
        l_scr[2] = lse
        _to_planes(lse4_ref, l_scr.at[pl.ds(2, 1)], DILATIONS[1], 1, F32)
        _to_planes(lse16_ref, l_scr.at[pl.ds(2, 1)], DILATIONS[2], 1, F32)
        cat_ref[:, 0:D_ATTN] = attn.astype(MXU)
        cat_ref[:, D_ATTN:] = gm_ref[...]
        mix = jnp.dot(cat_ref[...], wo_ref[...], preferred_element_type=F32)
        xhat, rstd = _ln_fwd(ALPHA * x_ref[...] + mix)
        xhat_ref[...] = xhat
        rstd_ref[...] = rstd
        x1b_ref[...] = (xhat * g_ref[...] + b_ref[...]).astype(MXU)

    tok = lambda w: pl.BlockSpec((TM, w), lambda i: (i, 0))
    outs = [jax.ShapeDtypeStruct((t, D_ATTN), F32)] + [_perm_shape(t, d, 128, F32) for d in DILATIONS] + [
        jax.ShapeDtypeStruct((t, D_MODEL), MXU), jax.ShapeDtypeStruct((t, D_MODEL), F32), jax.ShapeDtypeStruct((t, 1), F32),
        jax.ShapeDtypeStruct((t, D_MODEL), MXU)]
    return pl.pallas_call(
        body, name="mix_ln1", grid=(t // TM,),
        in_specs=[_perm_tile(d, D_ATTN) for d in DILATIONS] + [_perm_tile(d, 128) for d in DILATIONS]
        + [tok(D_GMLP), tok(D_MODEL), _full(w_o.shape), _full(ln1_g.shape), _full(ln1_b.shape), _full(expand.shape),
           pl.BlockSpec(memory_space=pl.ANY)],
        out_specs=[tok(D_ATTN)] + [_perm_tile(d, 128) for d in DILATIONS] + [tok(D_MODEL), tok(D_MODEL), tok(1), tok(D_MODEL)],
        out_shape=outs,
        scratch_shapes=[pltpu.VMEM((2 * LANE_CHUNKS, TM, 128), F32), pltpu.VMEM((3, TM, 128), F32)],
        compiler_params=_cp(dimension_semantics=("arbitrary",)),
    )(*os_, *ls_, gm, x, w_o, ln1_g, ln1_b, expand, dep)


def _conv_fwd(a_ext, w_ref, b_ref, rows):
    back = [pltpu.roll(a_ext, s, 0)[HALO:HALO + rows] for s in (1, 2)]
    return b_ref[...] + w_ref[2:3, :] * a_ext[HALO:HALO + rows] + w_ref[1:2, :] * back[0] + w_ref[0:1, :] * back[1]


def _ffn_in(x1b, w_a, w_b, conv_w, conv_b):
    t = x1b.shape[0]
    hb = TM // HALO

    def body(x_ref, xh_ref, wa_ref, wb_ref, cw_ref, cb_ref, apre_ref, act_ref, gate_ref, f_ref):
        i = pl.program_id(1)
        a_pre = _dot_nt(x_ref[...], wa_ref[...])
        a_halo = jnp.where(i > 0, _dot_nt(xh_ref[...], wa_ref[...]), 0.0)
        a = _conv_fwd(jnp.concatenate([a_halo, a_pre], axis=0), cw_ref, cb_ref, TM)
        b = _dot_nt(x_ref[...], wb_ref[...])
        cdf = 0.5 * (1.0 + lax.erf(a * (1.0 / math.sqrt(2.0))))
        pdf = jnp.exp(-0.5 * a * a) * (1.0 / math.sqrt(2.0 * math.pi))
        act = a * cdf
        apre_ref[...] = a_pre
        act_ref[...] = act
        gate_ref[...] = b * (cdf + a * pdf)
        f_ref[...] = (act * b).astype(MXU)

    blk = lambda r, c: pl.BlockSpec((None, r, c), lambda j, i: (j, 0, 0))
    tokj = pl.BlockSpec((None, TM, FF_BLK), lambda j, i: (j, i, 0))
    outs = [jax.ShapeDtypeStruct((N_SHARD, t, FF_BLK), F32)] * 3 + [jax.ShapeDtypeStruct((N_SHARD, t, FF_BLK), MXU)]
    return pl.pallas_call(
        body, name="ffn_in", grid=(N_SHARD, t // TM),
        in_specs=[pl.BlockSpec((TM, D_MODEL), lambda j, i: (i, 0)),
                  pl.BlockSpec((HALO, D_MODEL), lambda j, i: (jnp.maximum(i * hb - 1, 0), 0)),
                  blk(FF_BLK, D_MODEL), blk(FF_BLK, D_MODEL), blk(3, FF_BLK), blk(1, FF_BLK)],
        out_specs=[tokj, tokj, tokj, tokj], out_shape=outs,
        compiler_params=_cp(dimension_semantics=("arbitrary", "arbitrary")),
    )(x1b, x1b, w_a, w_b, conv_w, conv_b)


def _ffn_out_ln2(f, w_down, xhat1, ln1_g, ln1_b):
    t = xhat1.shape[0]

    def body(f_ref, wd_ref, xh_ref, g1_ref, b1_ref, xhat_ref, rstd_ref):
        half = TM // ROW_GROUPS
        for r0 in range(0, TM, half):
            rows = pl.ds(r0, half)
            ff = jnp.dot(f_ref[0, rows, :], wd_ref[0], preferred_element_type=F32)
            for j in range(1, N_SHARD):
                ff = ff + jnp.dot(f_ref[j, rows, :], wd_ref[j], preferred_element_type=F32)
            x1 = xh_ref[rows, :] * g1_ref[...] + b1_ref[...]
            xhat, rstd = _ln_fwd(ALPHA * x1 + ff)
            xhat_ref[rows, :] = xhat
            rstd_ref[rows, :] = rstd

    tok = lambda w: pl.BlockSpec((TM, w), lambda i: (i, 0))
    vec = _full((1, D_MODEL))
    outs = [jax.ShapeDtypeStruct((t, D_MODEL), F32), jax.ShapeDtypeStruct((t, 1), F32)]
    return pl.pallas_call(
        body, name="ffn_out_ln2", grid=(t // TM,),
        in_specs=[pl.BlockSpec((N_SHARD, TM, FF_BLK), lambda i: (0, i, 0)), _full(w_down.shape), tok(D_MODEL), vec, vec],
        out_specs=[tok(D_MODEL), tok(1)], out_shape=outs,
        compiler_params=_cp(dimension_semantics=("arbitrary",)),
    )(f, w_down, xhat1, ln1_g, ln1_b)


STAT_ROWS = 8


def _ple_loss_bwd(xhat2, rstd2, p, target, ln2_g, ln2_b, w_g, b_g, w_p, ln3_g, ln3_b):
    t = xhat2.shape[0]

    def body(xh2_ref, rs2_ref, p_ref, t_ref, g2_ref, b2_ref, wg_ref, bg_ref, wp_ref, g3_ref, b3_ref,
             dr2_ref, dr2b_ref, stat_ref, dwg_ref, dwp_ref, pp_scr, dwp_scr):
        @pl.when(pl.program_id(0) == 0)
        def _():
            stat_ref[...] = jnp.zeros_like(stat_ref)
            dwg_ref[...] = jnp.zeros_like(dwg_ref)
            dwp_scr[...] = jnp.zeros_like(dwp_scr)

        xhat2 = xh2_ref[...]
        x2 = xhat2 * g2_ref[...] + b2_ref[...]
        x2b = x2.astype(MXU)
        gate = jax.nn.sigmoid(jnp.dot(x2b, wg_ref[...], preferred_element_type=F32) + bg_ref[...])
        pb = p_ref[...].astype(MXU)
        for j in range(N_SHARD):
            pp_scr[:, j * ROW_BLK:(j + 1) * ROW_BLK] = jnp.dot(pb, wp_ref[j], preferred_element_type=F32)
        pp = pp_scr[...]
        xhat3, rstd3 = _ln_fwd(ALPHA * x2 + gate * pp)
        err = xhat3 * g3_ref[...] + b3_ref[...] - t_ref[...]
        dy = err * (1.0 / D_MODEL)
        dr3 = _ln_bwd(dy, xhat3, rstd3, g3_ref[...])
        dgp = dr3 * pp * gate * (1.0 - gate)
        dgp_b = dgp.astype(MXU)
        dwg_ref[...] += _dot_tn(x2b, dgp_b)
        dwp_scr[...] += _dot_tn(pb, dr3 * gate)
        dx2 = ALPHA * dr3 + _dot_nt(dgp_b, wg_ref[...])
        dr2 = _ln_bwd(dx2, xhat2, rs2_ref[...], g2_ref[...])
        dr2_ref[...] = dr2
        dr2b_ref[...] = dr2.astype(MXU)
        stat_ref[0:1, :] += _colsum(dy * xhat3)
        stat_ref[1:2, :] += _colsum(dy)
        stat_ref[2:3, :] += _colsum(dgp)
        stat_ref[3:4, :] += _colsum(dx2 * xhat2)
        stat_ref[4:5, :] += _colsum(dx2)
        stat_ref[5:6, :] += _colsum(err * err)

        @pl.when(pl.program_id(0) == t // TM - 1)
        def _():
            for j in range(N_SHARD):
                dwp_ref[j] = dwp_scr[:, j * ROW_BLK:(j + 1) * ROW_BLK]

    tok = lambda w: pl.BlockSpec((TM, w), lambda i: (i, 0))
    vec = _full((1, D_MODEL))
    outs = [jax.ShapeDtypeStruct((t, D_MODEL), F32), jax.ShapeDtypeStruct((t, D_MODEL), MXU),
            jax.ShapeDtypeStruct((STAT_ROWS, D_MODEL), F32), jax.ShapeDtypeStruct((D_MODEL, D_MODEL), F32),
            jax.ShapeDtypeStruct((N_SHARD, D_PLE, ROW_BLK), F32)]
    return pl.pallas_call(
        body, name="ple_loss_bwd", grid=(t // TM,),
        in_specs=[tok(D_MODEL), tok(1), tok(D_PLE), tok(D_MODEL), vec, vec, _full(w_g.shape), vec, _full(w_p.shape), vec, vec],
        out_specs=[tok(D_MODEL), tok(D_MODEL), _full((STAT_ROWS, D_MODEL)), _full((D_MODEL, D_MODEL)),
                   _full((N_SHARD, D_PLE, ROW_BLK))], out_shape=outs,
        scratch_shapes=[pltpu.VMEM((TM, D_MODEL), F32), pltpu.VMEM((D_PLE, D_MODEL), F32)],
        compiler_params=_cp(dimension_semantics=("arbitrary",)),
    )(xhat2, rstd2, p, target, ln2_g, ln2_b, w_g, b_g, w_p, ln3_g, ln3_b)


def _ffn_bwd(dr2, dr2b, a_pre, act, gate, w_down, w_a, w_b, conv_w, xhat1, rstd1, ln1_g, cat):
    t = dr2.shape[0]
    nt = t // TM
    hb = TM // HALO
    last_h = t // HALO - 1
    halo2 = 2 * HALO

    def body(dr_ref, drb_ref, drbn_ref, ap_ref, act_ref, gate_ref, gaten_ref, wd_ref, wa_ref, wb_ref, cw_ref,
             xh_ref, rs_ref, g1_ref, cat_ref, dap_ref, dbb_ref, dr1_ref, cstat_ref, lstat_ref, dwo_ref, acc_scr):
        i, j = pl.program_id(0), pl.program_id(1)

        @pl.when((i == 0) & (j == 0))
        def _():
            cstat_ref[...] = jnp.zeros_like(cstat_ref)
            lstat_ref[...] = jnp.zeros_like(lstat_ref)
            dwo_ref[...] = jnp.zeros_like(dwo_ref)

        half = TM // ROW_GROUPS
        parts = []
        for r0 in range(0, TM, half):
            rows = pl.ds(r0, half)
            last = r0 + half == TM

            def ext(ref, nxt):
                return jnp.concatenate([ref[rows], nxt[...]], axis=0) if last else ref[r0:r0 + half + HALO]

            drb = jnp.concatenate([drb_ref[rows, :], drbn_ref[...]], axis=0) if last else drb_ref[r0:r0 + half + halo2, :]
            df = _dot_nt(drb, wd_ref[...])[0:half + HALO]
            da = df * ext(gate_ref, gaten_ref)
            if last:
                da = jnp.concatenate([da[0:half], jnp.where(i < nt - 1, da[half:], 0.0)], axis=0)
            ahead = [da[0:half]] + [pltpu.roll(da, half + HALO - s, 0)[0:half] for s in (1, 2)]
            da_pre = cw_ref[2:3, :] * ahead[0] + cw_ref[1:2, :] * ahead[1] + cw_ref[0:1, :] * ahead[2]
            dbb = df[0:half] * act_ref[rows, :]
            dap_ref[rows, :] = da_pre.astype(MXU)
            dbb_ref[rows, :] = dbb.astype(MXU)
            for kk in range(3):
                cstat_ref[j, kk:kk + 1, :] += _colsum(ahead[2 - kk] * ap_ref[rows, :])
            cstat_ref[j, 3:4, :] += _colsum(ahead[0])
            parts.append(_dot(da_pre, wa_ref[...]) + _dot(dbb, wb_ref[...]))
        part = jnp.concatenate(parts, axis=0)

        @pl.when(j == 0)
        def _():
            acc_scr[...] = ALPHA * dr_ref[...] + part

        @pl.when(j > 0)
        def _():
            acc_scr[...] += part

        @pl.when(j == N_SHARD - 1)
        def _():
            dx1 = acc_scr[...]
            xhat1 = xh_ref[...]
            lstat_ref[0:1, :] += _colsum(dx1 * xhat1)
            lstat_ref[1:2, :] += _colsum(dx1)
            dr1 = _ln_bwd(dx1, xhat1, rs_ref[...], g1_ref[...])
            dr1_ref[...] = dr1
            dwo_ref[...] += _dot_tn(cat_ref[...], dr1)

    tok = lambda w: pl.BlockSpec((TM, w), lambda i, j: (i, 0))
    tokj = pl.BlockSpec((None, TM, FF_BLK), lambda i, j: (j, i, 0))
    nextj = pl.BlockSpec((None, HALO, FF_BLK), lambda i, j: (j, jnp.minimum((i + 1) * hb, last_h), 0))
    blk = lambda r, c: pl.BlockSpec((None, r, c), lambda i, j: (j, 0, 0))
    outs = [jax.ShapeDtypeStruct((N_SHARD, t, FF_BLK), MXU)] * 2 + [
        jax.ShapeDtypeStruct((t, D_MODEL), F32), jax.ShapeDtypeStruct((N_SHARD, STAT_ROWS, FF_BLK), F32),
        jax.ShapeDtypeStruct((STAT_ROWS, D_MODEL), F32), jax.ShapeDtypeStruct((D_MODEL, D_MODEL), F32)]
    return pl.pallas_call(
        body, name="ffn_bwd", grid=(nt, N_SHARD),
        in_specs=[tok(D_MODEL), tok(D_MODEL),
                  pl.BlockSpec((halo2, D_MODEL), lambda i, j: (jnp.minimum((i + 1) * (hb // 2), last_h // 2), 0)),
                  tokj, tokj, tokj, nextj, blk(FF_BLK, D_MODEL), blk(FF_BLK, D_MODEL), blk(FF_BLK, D_MODEL),
                  blk(3, FF_BLK), tok(D_MODEL), tok(1), _full((1, D_MODEL)), tok(D_MODEL)],
        out_specs=[tokj, tokj, tok(D_MODEL), _full((N_SHARD, STAT_ROWS, FF_BLK)), _full((STAT_ROWS, D_MODEL)),
                   _full((D_MODEL, D_MODEL))], out_shape=outs,
        scratch_shapes=[pltpu.VMEM((TM, D_MODEL), F32)],
        compiler_params=_cp(dimension_semantics=("arbitrary", "arbitrary")),
    )(dr2, dr2b, dr2b, a_pre, act, gate, gate, w_down, w_a, w_b, conv_w, xhat1, rstd1, ln1_g, cat)


def _mix_bwd(dr1, w_o, hu, hz, mixed, attn, ln_z_g, ln_z_b, w_s, dep):
    t = dr1.shape[0]
    nchunk = TM // BLK

    def body(dr_ref, wo_ref, hu_ref, hz_ref, mx_ref, attn_ref, g_ref, b_ref, ws_ref, grp_ref, red_ref, dep_ref,
             do1_ref, do4_ref, do16_ref, dl1_ref, dl4_ref, dl16_ref, duz_ref, dws_ref, dbs_ref, zstat_ref,
             wm_scr, dzn_scr, dbsum_scr, do_scr, dl_scr):
        @pl.when(pl.program_id(0) == 0)
        def _():
            row = lax.broadcasted_iota(jnp.int32, (BLK, BLK), 0)
            col = lax.broadcasted_iota(jnp.int32, (BLK, BLK), 1)
            for g in range(N_HEADS):
                wm_scr[g] = jnp.where(col <= row, ws_ref[g], 0.0).astype(MXU)
            dws_ref[...] = jnp.zeros_like(dws_ref)
            dbsum_scr[...] = jnp.zeros_like(dbsum_scr)
            zstat_ref[...] = jnp.zeros_like(zstat_ref)

        dcat = _dot_nt(dr_ref[...], wo_ref[...])
        dattn = dcat[:, 0:D_ATTN]
        do1_ref[...] = dattn.astype(MXU)
        for cc, val in enumerate(_chunks(dattn)):
            do_scr[cc] = val
        _to_planes(do4_ref, do_scr, DILATIONS[1], LANE_CHUNKS, MXU)
        _to_planes(do16_ref, do_scr, DILATIONS[2], LANE_CHUNKS, MXU)
        delta = _dot_select(dattn * attn_ref[...], red_ref[...])
        dl1_ref[...] = delta
        dl_scr[0] = delta
        _to_planes(dl4_ref, dl_scr, DILATIONS[1], 1, F32)
        _to_planes(dl16_ref, dl_scr, DILATIONS[2], 1, F32)
        dgm = dcat[:, D_ATTN:]
        hu, hz = hu_ref[...], hz_ref[...]
        u = _gelu(hu)
        duz_ref[:, 0:D_GMLP] = (dgm * mx_ref[...] * _gelu_grad(hu)).astype(MXU)
        dmixed = dgm * u
        dmb = dmixed.astype(MXU)
        zhat, rstd = _ln_fwd(_gelu(hz))
        znb = (zhat * g_ref[...] + b_ref[...]).astype(MXU)
        dbs_acc = jnp.zeros((BLK, D_GMLP), F32)
        for ch in range(nchunk):
            rows = slice(ch * BLK, (ch + 1) * BLK)
            dbs_acc = dbs_acc + dmixed[rows]
            for g in range(N_HEADS):
                cols = slice(g * HEAD_DIM, (g + 1) * HEAD_DIM)
                dzn_scr[rows, cols] = _dot_tn(wm_scr[g], dmb[rows, cols])
                dws_ref[g] += _dot_nt(dmb[rows, cols], znb[rows, cols])
        dbsum_scr[...] += dbs_acc
        dzn = dzn_scr[...]
        zstat_ref[0:1, :] += _colsum(dzn * zhat)
        zstat_ref[1:2, :] += _colsum(dzn)
        duz_ref[:, D_GMLP:] = (_ln_bwd(dzn, zhat, rstd, g_ref[...]) * _gelu_grad(hz)).astype(MXU)

        @pl.when(pl.program_id(0) == nt - 1)
        def _():
            row = lax.broadcasted_iota(jnp.int32, (BLK, BLK), 0)
            col = lax.broadcasted_iota(jnp.int32, (BLK, BLK), 1)
            for g in range(N_HEADS):
                dws_ref[g] = jnp.where(col <= row, dws_ref[g], 0.0)
            dbs_ref[...] = lax.dot_general(grp_ref[...], dbsum_scr[...], (((1,), (1,)), ((), ())),
                                           precision=lax.Precision.HIGHEST, preferred_element_type=F32)

    nt = t // TM
    tok = lambda w: pl.BlockSpec((TM, w), lambda i: (i, 0))
    grp = jnp.asarray((np.arange(D_GMLP)[None, :] // HEAD_DIM == np.arange(N_HEADS)[:, None]).astype(np.float32))
    red = _head_reduce()
    outs = [_perm_shape(t, d, D_ATTN, MXU) for d in DILATIONS] + [_perm_shape(t, d, 128, F32) for d in DILATIONS] + [
        jax.ShapeDtypeStruct((t, 2 * D_GMLP), MXU),
        jax.ShapeDtypeStruct((N_HEADS, BLK, BLK), F32), jax.ShapeDtypeStruct((N_HEADS, BLK), F32),
        jax.ShapeDtypeStruct((STAT_ROWS, D_GMLP), F32)]
    return pl.pallas_call(
        body, name="mix_bwd", grid=(t // TM,),
        in_specs=[tok(D_MODEL), _full(w_o.shape), tok(D_GMLP), tok(D_GMLP), tok(D_GMLP), tok(D_ATTN), _full(ln_z_g.shape),
                  _full(ln_z_b.shape), _full(w_s.shape), _full(grp.shape), _full(red.shape), pl.BlockSpec(memory_space=pl.ANY)],
        out_specs=[_perm_tile(d, D_ATTN) for d in DILATIONS] + [_perm_tile(d, 128) for d in DILATIONS]
        + [tok(2 * D_GMLP), _full((N_HEADS, BLK, BLK)), _full((N_HEADS, BLK)), _full((STAT_ROWS, D_GMLP))],
        out_shape=outs,
        scratch_shapes=[pltpu.VMEM((N_HEADS, BLK, BLK), MXU), pltpu.VMEM((TM, D_GMLP), F32), pltpu.VMEM((BLK, D_GMLP), F32),
                        pltpu.VMEM((LANE_CHUNKS, TM, 128), F32), pltpu.VMEM((1, TM, 128), F32)],
        compiler_params=_cp(dimension_semantics=("arbitrary",)),
    )(dr1, w_o, hu, hz, mixed, attn, ln_z_g, ln_z_b, w_s, grp, red, dep)


def _dx_in(dqs, dks, dvs, duz, dr1, w_in, c_tab, s1_tab, s2_tab):
    t = dr1.shape[0]

    def body(dq1, dq4, dq16, dk1, dk4, dk16, dv1, dv4, dv16, duz_ref, dr_ref, w_ref, c_ref, s1_ref, s2_ref,
             dh_ref, dx_ref, acc_scr):
        sums = []
        for part, (g1, g4, g16) in enumerate(((dq1, dq4, dq16), (dk1, dk4, dk16), (dv1, dv4, dv16))):
            acc = acc_scr.at[pl.ds(part * LANE_CHUNKS, LANE_CHUNKS)]
            for cc in range(LANE_CHUNKS):
                acc[cc] = g1[:, cc * 128:(cc + 1) * 128]
            _from_planes(g4, acc, DILATIONS[1], LANE_CHUNKS, accumulate=True)
            _from_planes(g16, acc, DILATIONS[2], LANE_CHUNKS, accumulate=True)
            sums.append(_unchunk(acc_scr, LANE_CHUNKS, part * LANE_CHUNKS))
        c, s1, s2 = _tile_heads(c_ref[...]), _tile_heads(s1_ref[...]), _tile_heads(s2_ref[...])
        dh_ref[:, 0:D_ATTN] = _rope_apply_t(sums[0] * (1.0 / math.sqrt(HEAD_DIM)), c, s1, s2).astype(MXU)
        dh_ref[:, D_ATTN:2 * D_ATTN] = _rope_apply_t(sums[1], c, s1, s2).astype(MXU)
        dh_ref[:, 2 * D_ATTN:3 * D_ATTN] = sums[2].astype(MXU)
        dh_ref[:, 3 * D_ATTN:] = duz_ref[...]
        dx = ALPHA * dr_ref[...]
        for j in range(N_SHARD):
            dx = dx + _dot_nt(dh_ref[:, j * W_IN_BLK:(j + 1) * W_IN_BLK], w_ref[j])
        dx_ref[...] = dx

    tok = lambda w: pl.BlockSpec((TM, w), lambda i: (i, 0))
    outs = [jax.ShapeDtypeStruct((t, D_IN), MXU), jax.ShapeDtypeStruct((t, D_MODEL), F32)]
    return pl.pallas_call(
        body, name="dx_in", grid=(t // TM,),
        in_specs=[_perm_tile(d, D_ATTN) for d in DILATIONS] * 3
        + [tok(2 * D_GMLP), tok(D_MODEL), _full(w_in.shape), tok(128), tok(128), tok(128)],
        out_specs=[tok(D_IN), tok(D_MODEL)], out_shape=outs,
        scratch_shapes=[pltpu.VMEM((3 * LANE_CHUNKS, TM, 128), F32)],
        compiler_params=_cp(dimension_semantics=("arbitrary",)),
    )(*dqs, *dks, *dvs, duz, dr1, w_in, c_tab, s1_tab, s2_tab)


def _wgrad(name, x, dy, x_spec, dy_spec, out_spec, out_shape, grid, dep=None):
    deps = [] if dep is None else [dep]

    def body(x_ref, dy_ref, *rest):
        rest[-1][...] = _dot_tn(x_ref[...], dy_ref[...])

    return pl.pallas_call(
        body, name=name, grid=grid, in_specs=[x_spec, dy_spec] + [pl.BlockSpec(memory_space=pl.ANY)] * len(deps),
        out_specs=out_spec, out_shape=jax.ShapeDtypeStruct(out_shape, F32),
        compiler_params=_cp(dimension_semantics=("arbitrary",) * len(grid)),
    )(x, dy, *deps)


def _wgrad_pair(name, xa, xb, dy, x_spec, dy_spec, out_spec, out_shape, grid):
    def body(xa_ref, xb_ref, dy_ref, oa_ref, ob_ref):
        dy = dy_ref[...]
        oa_ref[...] = _dot_tn(xa_ref[...], dy)
        ob_ref[...] = _dot_tn(xb_ref[...], dy)

    return pl.pallas_call(
        body, name=name, grid=grid, in_specs=[x_spec, x_spec, dy_spec], out_specs=[out_spec, out_spec],
        out_shape=[jax.ShapeDtypeStruct(out_shape, F32)] * 2,
        compiler_params=_cp(dimension_semantics=("arbitrary",) * len(grid)),
    )(xa, xb, dy)


def _local_step(x, p, rope, target, w_in, start_dep, late_landed, late_weights, early_grads, early_grads_sent,
                early_grads_landed,
                ln_z_g, ln_z_b, w_s, b_s, ln1_g, ln1_b, conv_b, ln2_g, ln2_b, b_g, ln3_g, ln3_b):
    t = x.shape[0]
    half = TM
    c_tab, s1_tab, s2_tab = rope
    b_full = jnp.repeat(jnp.transpose(b_s[0]), HEAD_DIM, axis=1)
    conv_b4 = conv_b.reshape(N_SHARD, 1, FF_BLK)
    *qkvs, hu, hz, mixed, gm, xb = _qkvuz(x, w_in, c_tab, s1_tab, s2_tab, ln_z_g, ln_z_b, w_s[0], b_full, start_dep)
    branches = [_attn_fwd(qkv, d, start_dep) for qkv, d in zip(qkvs[:2], DILATIONS[:2])]
    dep = late_landed(branches[-1][1])
    branches.append(_attn_fwd(qkvs[2], DILATIONS[2], dep))
    w_o, w_a, w_b, conv_w, w_down, w_g, w_p = late_weights(branches[-1][1])
    attn, *lses, cat, xhat1, rstd1, x1b = _mix_ln1(
        [o for o, _ in branches], [l for _, l in branches], gm, x, w_o, ln1_g, ln1_b, dep)
    a_pre, act, gate, f = _ffn_in(x1b, w_a, w_b, conv_w, conv_b4)
    xhat2, rstd2 = _ffn_out_ln2(f, w_down, xhat1, ln1_g, ln1_b)
    dr2, dr2b, stat3, g_w_g, g_w_p = _ple_loss_bwd(xhat2, rstd2, p, target, ln2_g, ln2_b, w_g, b_g, w_p, ln3_g, ln3_b)
    da_pre, dbb, dr1, cstat, stat1, g_w_o = _ffn_bwd(dr2, dr2b, a_pre, act, gate, w_down, w_a, w_b, conv_w, xhat1, rstd1,
                                                    ln1_g, cat)

    full_t = lambda w, im: pl.BlockSpec((t, w), im)
    ffj = pl.BlockSpec((None, t, FF_BLK), lambda j, kk: (j, 0, 0))
    early = dict(
        w_ple_gate=g_w_g, w_ple_in=g_w_p,
        w_ff_down=_wgrad("dw_down", f, dr2b, ffj, full_t(half, lambda j, n: (0, n)),
                         pl.BlockSpec((None, FF_BLK, half), lambda j, n: (j, 0, n)), (N_SHARD, FF_BLK, D_MODEL), (N_SHARD, 2)),
        **dict(zip(("w_ff_a", "w_ff_b"), _wgrad_pair(
            "dw_ab", da_pre, dbb, x1b, ffj, full_t(half, lambda j, n: (0, n)),
            pl.BlockSpec((None, FF_BLK, half), lambda j, n: (j, 0, n)), (N_SHARD, FF_BLK, D_MODEL), (N_SHARD, 2)))),
        w_o=g_w_o)
    dep = early_grads(early)

    do1, do4, do16, dl1, dl4, dl16, duz, dws, dbs, zstat = _mix_bwd(
        dr1, w_o, hu, hz, mixed, attn, ln_z_g, ln_z_b, w_s[0], dep)
    dep = early_grads_sent(duz, (stat3, stat1, zstat, cstat, dws, dbs))
    dqkv = [_attn_bwd(qkv, do, lse, dl, d, dep)
            for qkv, do, lse, dl, d in zip(qkvs, (do1, do4, do16), lses, (dl1, dl4, dl16), DILATIONS)]
    dh, grad_x = _dx_in([g[0] for g in dqkv], [g[1] for g in dqkv], [g[2] for g in dqkv], duz, dr1, w_in,
                        c_tab, s1_tab, s2_tab)
    dep = early_grads_landed(grad_x)
    g_w_in = _wgrad("dw_in", xb, dh, full_t(half, lambda j, kk: (0, kk)), full_t(W_IN_BLK, lambda j, kk: (0, j)),
                    pl.BlockSpec((None, half, W_IN_BLK), lambda j, kk: (j, kk, 0)), (N_SHARD, D_MODEL, W_IN_BLK), (N_SHARD, 2),
                    dep)
    return grad_x, g_w_in


def _tile_rows(rows, mult, steps):
    if rows % mult:
        return rows
    return next(rows // k for k in range(steps, rows + 1) if rows % k == 0 and (rows // k) % mult == 0)


def _grid_spec(grid, in_specs, out_specs):
    return pltpu.PrefetchScalarGridSpec(num_scalar_prefetch=1, grid=grid, in_specs=in_specs, out_specs=out_specs)


def _on_own_steps(i, count, steps, work):
    if count == steps:
        work()
    else:
        pl.when(i < count)(work)


def _place_shards(name, ws, dtypes, place, dep):
    n = len(ws)
    tiles = [_tile_rows(w.shape[0], 16, 2) for w in ws]
    counts = [w.shape[0] // t for w, t in zip(ws, tiles)]
    steps = max(counts)

    def body(s_ref, *refs):
        i = pl.program_id(0)
        for a in range(n):
            def work(a=a):
                refs[n + 1 + a][...] = refs[a][...].astype(dtypes[a])
            _on_own_steps(i, counts[a], steps, work)

    def tile(a, lead):
        last = counts[a] - 1
        if lead:
            return pl.BlockSpec((None, tiles[a], ws[a].shape[1]), lambda i, s: (s[0], jnp.minimum(i, last), 0))
        return pl.BlockSpec((tiles[a], ws[a].shape[1]), lambda i, s: (jnp.minimum(i, last), 0))

    return pl.pallas_call(
        body, name=name,
        grid_spec=_grid_spec((steps,), [tile(a, False) for a in range(n)] + [pl.BlockSpec(memory_space=pl.ANY)],
                             [tile(a, True) for a in range(n)]),
        out_shape=[jax.ShapeDtypeStruct((N_SHARD, *w.shape), dt) for w, dt in zip(ws, dtypes)],
        compiler_params=_cp())(place, *ws, dep)


def _pair_sums(name, mines, gots, place):
    n = len(mines)
    tiles = [_tile_rows(g.shape[1], 16, 1) for g in gots]
    per_blk = [g.shape[1] // t for g, t in zip(gots, tiles)]
    counts = [N_SHARD * nh for nh in per_blk]
    steps = max(counts)

    def body(s_ref, *refs):
        i = pl.program_id(0)
        for a in range(n):
            def work(a=a):
                refs[2 * n + a][...] = (refs[a][...] + refs[n + a][...]).astype(BF16)
            _on_own_steps(i, counts[a], steps, work)

    def tile(a, mine):
        nh, last = per_blk[a], counts[a] - 1

        def index(i, s):
            g = jnp.minimum(i, last)
            return (g // nh, (s[1] * nh if mine else 0) + g % nh, 0)

        return pl.BlockSpec((None, tiles[a], gots[a].shape[2]), index)

    return pl.pallas_call(
        body, name=name,
        grid_spec=_grid_spec((steps,), [tile(a, True) for a in range(n)] + [tile(a, False) for a in range(n)],
                             [tile(a, False) for a in range(n)]),
        out_shape=[jax.ShapeDtypeStruct(g.shape, BF16) for g in gots], compiler_params=_cp())(place, *mines, *gots)


def _chip_sums(name, owns, landeds, place, dep):
    n = len(owns)
    tiles = [_tile_rows(o.shape[1], 16, 4) for o in owns]
    counts = [o.shape[1] // t for o, t in zip(owns, tiles)]
    steps = max(counts)

    def body(s_ref, *refs):
        i = pl.program_id(0)
        for a in range(n):
            def work(a=a):
                own, l1, l2, l3 = (refs[4 * a + k][...].astype(F32) for k in range(4))
                refs[4 * n + 1 + a][...] = ((own + l1) + l2) + l3
            _on_own_steps(i, counts[a], steps, work)

    def slot(a, d):
        last = counts[a] - 1
        return pl.BlockSpec((None, tiles[a], owns[a].shape[2]), lambda i, s: ((s[0] + d) % N_SHARD, jnp.minimum(i, last), 0))

    def out(a):
        nh, last = counts[a], counts[a] - 1
        return pl.BlockSpec((tiles[a], owns[a].shape[2]), lambda i, s: (s[1] * nh + jnp.minimum(i, last), 0))

    operands = [x for o, l in zip(owns, landeds) for x in (o, l, l, l)]
    return pl.pallas_call(
        body, name=name,
        grid_spec=_grid_spec((steps,), [slot(a, d) for a in range(n) for d in range(4)] + [pl.BlockSpec(memory_space=pl.ANY)],
                             [out(a) for a in range(n)]),
        out_shape=[jax.ShapeDtypeStruct((2 * o.shape[1], o.shape[2]), F32) for o in owns],
        compiler_params=_cp())(place, *operands, dep)


def _adamw_math(w, g, m, v):
    m = ADAM_B1 * m + (1.0 - ADAM_B1) * g
    v = ADAM_B2 * v + (1.0 - ADAM_B2) * (g * g)
    m_hat = m / (1.0 - ADAM_B1 ** ADAM_STEP)
    v_hat = v / (1.0 - ADAM_B2 ** ADAM_STEP)
    delta = -ADAM_LR * (m_hat / (jnp.sqrt(v_hat) + ADAM_EPS) + ADAM_WD * w)
    return delta, m, v


def _adamw_shards(name, ws, gs, ms, vs):
    n = len(ws)
    tiles = [_tile_rows(w.shape[1], 8, 8 if n > 1 else 2) for w in ws]
    counts = [w.shape[1] // t for w, t in zip(ws, tiles)]
    steps = max(counts)

    def body(*refs):
        i = pl.program_id(0)
        for a in range(n):
            def work(a=a):
                w_ref, g_ref, m_ref, v_ref = refs[4 * a:4 * a + 4]
                go_ref, d_ref, nm_ref, nv_ref = refs[4 * n + 4 * a:4 * n + 4 * a + 4]
                g = g_ref[...]
                go_ref[...] = g
                d_ref[...], nm_ref[...], nv_ref[...] = _adamw_math(w_ref[...], g, m_ref[...], v_ref[...])
            _on_own_steps(i, counts[a], steps, work)

    def tile(a, lead):
        last, c = counts[a] - 1, ws[a].shape[2]
        if lead:
            return pl.BlockSpec((None, tiles[a], c), lambda i: (0, jnp.minimum(i, last), 0))
        return pl.BlockSpec((tiles[a], c), lambda i: (jnp.minimum(i, last), 0))

    res = pl.pallas_call(
        body, name=name, grid=(steps,),
        in_specs=[tile(a, lead) for a in range(n) for lead in (True, False, True, True)],
        out_specs=[tile(a, True) for a in range(n) for _ in range(4)],
        out_shape=[jax.ShapeDtypeStruct(w.shape, F32) for w in ws for _ in range(4)],
        compiler_params=_cp())(*[x for quad in zip(ws, gs, ms, vs) for x in quad])
    return [tuple(res[4 * a:4 * a + 4]) for a in range(n)]


MESH = pl.DeviceIdType.MESH
ANY = pl.BlockSpec(memory_space=pl.ANY)


def _place():
    x, y, c = lax.axis_index("x"), lax.axis_index("y"), lax.axis_index("c")
    chips = [(1 - x, y), (x, 1 - y), (1 - x, 1 - y)]
    return x, y, c, 2 * x + y, chips


def _remote(src, dst, send_sem, recv_sem, dev):
    return pltpu.make_async_remote_copy(src_ref=src, dst_ref=dst, send_sem=send_sem, recv_sem=recv_sem,
                                        device_id=dev, device_id_type=MESH)


def _half(ref, hc, rows):
    return ref.at[pl.ds(hc * (rows // 2), rows // 2)]


def _sibling_join(blocks, tag):
    n = len(blocks)

    def body(*refs):
        outs = refs[n:2 * n]
        send, recv = refs[2 * n:]
        x, y, c, _, _ = _place()
        cps = []
        for a in range(n):
            h = blocks[a].shape[0] // 2
            mine = outs[a].at[pl.ds(c * h, h)]
            cp = _remote(mine, mine, send.at[a], recv.at[a], (x, y, 1 - c))
            cp.start()
            cps.append(cp)
        for a, cp in enumerate(cps):
            h = blocks[a].shape[0] // 2
            theirs = outs[a].at[pl.ds((1 - c) * h, h)]
            _remote(theirs, theirs, send.at[a], recv.at[a], (x, y, 1 - c)).wait_recv()
            cp.wait_send()

    sem = pltpu.SemaphoreType.DMA
    return pl.pallas_call(body, name=f"rs_sibling_join_{tag}", in_specs=[ANY] * n, out_specs=[ANY] * n,
                          out_shape=[jax.ShapeDtypeStruct(b_.shape, b_.dtype) for b_ in blocks],
                          input_output_aliases={a: a for a in range(n)},
                          scratch_shapes=[sem((n,)), sem((n,))])(*blocks)


def _join_start(blocks, after, tag):
    n = len(blocks)

    def body(*refs):
        ins = refs[:n]
        send, recv = refs[n + 1], refs[n + 2]
        token = refs[2 * n + 3]
        x, y, c, _, _ = _place()
        for a in range(n):
            h = blocks[a].shape[0] // 2
            mine = ins[a].at[pl.ds(c * h, h)]
            _remote(mine, mine, send.at[a], recv.at[a], (x, y, 1 - c)).start()
        token[...] = jnp.zeros_like(token)

    sems = pltpu.SemaphoreType.DMA((n,))
    res = pl.pallas_call(
        body, name=f"join_start_{tag}", in_specs=[HBM] * n + [ANY],
        out_specs=[SEM, SEM] + [HBM] * n + [pl.BlockSpec(memory_space=pltpu.VMEM)],
        out_shape=[sems, sems] + [pltpu.HBM(b_.shape, b_.dtype) for b_ in blocks] + [TOKEN],
        input_output_aliases={a: a + 2 for a in range(n)}, compiler_params=_in_flight_params(),
    )(*[_in_hbm(b_) for b_ in blocks], after)
    return res[0], res[1], res[2:2 + n], res[2 + n]


def _join_wait(send, recv, blocks, after, tag):
    n = len(blocks)

    def body(*refs):
        ins = refs[:n]
        send_ref, recv_ref = refs[n], refs[n + 1]
        x, y, c, _, _ = _place()
        for a in range(n):
            h = blocks[a].shape[0] // 2
            mine, theirs = ins[a].at[pl.ds(c * h, h)], ins[a].at[pl.ds((1 - c) * h, h)]
            _remote(mine, mine, send_ref.at[a], recv_ref.at[a], (x, y, 1 - c)).wait_send()
            _remote(theirs, theirs, send_ref.at[a], recv_ref.at[a], (x, y, 1 - c)).wait_recv()

    return pl.pallas_call(
        body, name=f"join_wait_{tag}", in_specs=[HBM] * n + [SEM, SEM, ANY], out_specs=[HBM] * n,
        out_shape=[pltpu.HBM(b_.shape, b_.dtype) for b_ in blocks],
        input_output_aliases={a: a for a in range(n)}, compiler_params=_in_flight_params(),
    )(*blocks, send, recv, after)


HBM = pl.BlockSpec(memory_space=pltpu.HBM)
SEM = pl.BlockSpec(memory_space=pltpu.SEMAPHORE)
TOKEN = jax.ShapeDtypeStruct((8, 128), F32)


def _in_flight_params():
    return pltpu.CompilerParams(has_side_effects=pltpu.SideEffectType.DATAFLOW_SIDE_EFFECTING)


def _in_hbm(a):
    return pltpu.with_memory_space_constraint(a, pltpu.HBM)


def _gather_piece(ref, rows, split, slot, hc):
    return _half(ref.at[slot], hc, rows) if split else ref.at[slot]


def _gather_start(stacks, split, after, tag):
    n = len(stacks)

    def body(*refs):
        ins = refs[:n]
        send, recv = refs[n + 1], refs[n + 2]
        token = refs[2 * n + 3]
        _, _, c, j, chips = _place()
        for a in range(n):
            mine = _gather_piece(ins[a], stacks[a].shape[1], split[a], j, c)
            for t in range(3):
                _remote(mine, mine, send.at[3 * a + t], recv.at[3 * a + t], (*chips[t], c)).start()
        token[...] = jnp.zeros_like(token)

    sems = pltpu.SemaphoreType.DMA((3 * n,))
    res = pl.pallas_call(
        body, name=f"gather_start_{tag}", in_specs=[HBM] * n + [ANY],
        out_specs=[SEM, SEM] + [HBM] * n + [pl.BlockSpec(memory_space=pltpu.VMEM)],
        out_shape=[sems, sems] + [pltpu.HBM(s.shape, s.dtype) for s in stacks] + [TOKEN],
        input_output_aliases={a: a + 2 for a in range(n)}, compiler_params=_in_flight_params(),
    )(*[_in_hbm(s) for s in stacks], after)
    return res[0], res[1], res[2:2 + n], res[2 + n]


def _gather_wait(send, recv, stacks, split, after, tag):
    n = len(stacks)

    def body(*refs):
        ins = refs[:n]
        send_ref, recv_ref = refs[n], refs[n + 1]
        _, _, c, j, chips = _place()
        for a in range(n):
            rows = stacks[a].shape[1]
            mine = _gather_piece(ins[a], rows, split[a], j, c)
            for t, (px, py) in enumerate(chips):
                theirs = _gather_piece(ins[a], rows, split[a], 2 * px + py, c)
                _remote(mine, mine, send_ref.at[3 * a + t], recv_ref.at[3 * a + t], (px, py, c)).wait_send()
                _remote(theirs, theirs, send_ref.at[3 * a + t], recv_ref.at[3 * a + t], (px, py, c)).wait_recv()

    return pl.pallas_call(
        body, name=f"gather_wait_{tag}", in_specs=[HBM] * n + [SEM, SEM, ANY], out_specs=[HBM] * n,
        out_shape=[pltpu.HBM(s.shape, s.dtype) for s in stacks],
        input_output_aliases={a: a for a in range(n)}, compiler_params=_in_flight_params(),
    )(*stacks, send, recv, after)


def _gather_forward(stacks, split, tag):
    idx = [a for a in range(len(stacks)) if split[a]]
    n = len(idx)

    def body(*refs):
        outs = refs[n:2 * n]
        send, recv = refs[2 * n:]
        x, y, c, _, chips = _place()
        sends = []
        for t, (px, py) in enumerate(chips):
            for a in range(n):
                blk = _half(outs[a].at[2 * px + py], c, stacks[idx[a]].shape[1])
                cp = _remote(blk, blk, send.at[a, t], recv.at[a, t], (x, y, 1 - c))
                cp.start()
                sends.append(cp)
        for t, (px, py) in enumerate(chips):
            for a in range(n):
                blk = _half(outs[a].at[2 * px + py], 1 - c, stacks[idx[a]].shape[1])
                _remote(blk, blk, send.at[a, t], recv.at[a, t], (x, y, 1 - c)).wait_recv()
        for cp in sends:
            cp.wait_send()

    sem = pltpu.SemaphoreType.DMA
    res = pl.pallas_call(
        body, name=f"gather_forward_{tag}", in_specs=[ANY] * n, out_specs=[ANY] * n,
        out_shape=[jax.ShapeDtypeStruct(stacks[a].shape, stacks[a].dtype) for a in idx],
        input_output_aliases={a: a for a in range(n)}, scratch_shapes=[sem((n, 3)), sem((n, 3))],
    )(*[stacks[a] for a in idx])
    out = list(stacks)
    for a, r in zip(idx, res):
        out[a] = r
    return out


def _forward_start(stacks, after, tag):
    n = len(stacks)

    def body(*refs):
        ins = refs[:n]
        send, recv = refs[n + 1], refs[n + 2]
        token = refs[2 * n + 3]
        x, y, c, _, chips = _place()
        for a in range(n):
            for t, (px, py) in enumerate(chips):
                blk = _half(ins[a].at[2 * px + py], c, stacks[a].shape[1])
                _remote(blk, blk, send.at[3 * a + t], recv.at[3 * a + t], (x, y, 1 - c)).start()
        token[...] = jnp.zeros_like(token)

    sems = pltpu.SemaphoreType.DMA((3 * n,))
    res = pl.pallas_call(
        body, name=f"forward_start_{tag}", in_specs=[HBM] * n + [ANY],
        out_specs=[SEM, SEM] + [HBM] * n + [pl.BlockSpec(memory_space=pltpu.VMEM)],
        out_shape=[sems, sems] + [pltpu.HBM(s.shape, s.dtype) for s in stacks] + [TOKEN],
        input_output_aliases={a: a + 2 for a in range(n)}, compiler_params=_in_flight_params(),
    )(*[_in_hbm(s) for s in stacks], after)
    return res[0], res[1], res[2:2 + n], res[2 + n]


def _forward_wait(send, recv, stacks, after, tag):
    n = len(stacks)

    def body(*refs):
        ins = refs[:n]
        send_ref, recv_ref = refs[n], refs[n + 1]
        x, y, c, _, chips = _place()
        for a in range(n):
            for t, (px, py) in enumerate(chips):
                mine = _half(ins[a].at[2 * px + py], c, stacks[a].shape[1])
                theirs = _half(ins[a].at[2 * px + py], 1 - c, stacks[a].shape[1])
                _remote(mine, mine, send_ref.at[3 * a + t], recv_ref.at[3 * a + t], (x, y, 1 - c)).wait_send()
                _remote(theirs, theirs, send_ref.at[3 * a + t], recv_ref.at[3 * a + t], (x, y, 1 - c)).wait_recv()

    return pl.pallas_call(
        body, name=f"forward_wait_{tag}", in_specs=[HBM] * n + [SEM, SEM, ANY], out_specs=[HBM] * n,
        out_shape=[pltpu.HBM(s.shape, s.dtype) for s in stacks],
        input_output_aliases={a: a for a in range(n)}, compiler_params=_in_flight_params(),
    )(*stacks, send, recv, after)


def _swap_start(grads, tag):
    n = len(grads)

    def body(*refs):
        ins, gots = refs[:n], refs[n:2 * n]
        send, recv = refs[2 * n], refs[2 * n + 1]
        token = refs[4 * n + 2]
        x, y, c, _, _ = _place()
        for a in range(n):
            h = grads[a].shape[1] // 2
            _remote(ins[a].at[:, pl.ds((1 - c) * h, h)], gots[a], send.at[a], recv.at[a], (x, y, 1 - c)).start()
        token[...] = jnp.zeros_like(token)

    sems = pltpu.SemaphoreType.DMA((n,))
    halves = [(g.shape[0], g.shape[1] // 2, g.shape[2]) for g in grads]
    res = pl.pallas_call(
        body, name=f"swap_start_{tag}", in_specs=[HBM] * (2 * n),
        out_specs=[SEM, SEM] + [HBM] * (2 * n) + [pl.BlockSpec(memory_space=pltpu.VMEM)],
        out_shape=[sems, sems] + [pltpu.HBM(g.shape, g.dtype) for g in grads] + [pltpu.HBM(s, F32) for s in halves] + [TOKEN],
        input_output_aliases={a: a + 2 for a in range(2 * n)}, compiler_params=_in_flight_params(),
    )(*[_in_hbm(g) for g in grads], *[_in_hbm(lax.empty(s, F32)) for s in halves])
    return res[0], res[1], res[2:2 + n], res[2 + n:2 + 2 * n], res[2 + 2 * n]


def _swap_wait(send, recv, grads, gots, after, tag):
    n = len(grads)

    def body(*refs):
        ins, lnd = refs[:n], refs[n:2 * n]
        send_ref, recv_ref = refs[2 * n], refs[2 * n + 1]
        x, y, c, _, _ = _place()
        for a in range(n):
            h = grads[a].shape[1] // 2
            cp = _remote(ins[a].at[:, pl.ds((1 - c) * h, h)], lnd[a], send_ref.at[a], recv_ref.at[a], (x, y, 1 - c))
            cp.wait_send()
            cp.wait_recv()

    bufs = [pltpu.HBM(g.shape, g.dtype) for g in grads] + [pltpu.HBM(g.shape, g.dtype) for g in gots]
    res = pl.pallas_call(
        body, name=f"swap_wait_{tag}", in_specs=[HBM] * (2 * n) + [SEM, SEM, ANY], out_specs=[HBM] * (2 * n),
        out_shape=bufs, input_output_aliases={a: a for a in range(2 * n)}, compiler_params=_in_flight_params(),
    )(*grads, *gots, send, recv, after)
    return res[:n], res[n:]


def _exchange_start(parts, tag):
    n = len(parts)

    def body(*refs):
        ins, lands = refs[:n], refs[n:2 * n]
        send, recv = refs[2 * n], refs[2 * n + 1]
        token = refs[4 * n + 2]
        _, _, c, j, chips = _place()
        for t, (px, py) in enumerate(chips):
            for a in range(n):
                _remote(ins[a].at[2 * px + py], lands[a].at[j], send.at[3 * a + t], recv.at[3 * a + t], (px, py, c)).start()
        token[...] = jnp.zeros_like(token)

    sems = pltpu.SemaphoreType.DMA((3 * n,))
    bufs = [pltpu.HBM(p.shape, p.dtype) for p in parts]
    res = pl.pallas_call(
        body, name=f"exchange_start_{tag}", in_specs=[HBM] * (2 * n),
        out_specs=[SEM, SEM] + [HBM] * (2 * n) + [pl.BlockSpec(memory_space=pltpu.VMEM)],
        out_shape=[sems, sems] + bufs + bufs + [TOKEN],
        input_output_aliases={a: a + 2 for a in range(2 * n)}, compiler_params=_in_flight_params(),
    )(*[_in_hbm(p) for p in parts], *[_in_hbm(lax.empty(p.shape, p.dtype)) for p in parts])
    return res[0], res[1], res[2:2 + n], res[2 + n:2 + 2 * n], res[2 + 2 * n]


def _exchange_wait(send, recv, parts, lands, after, tag):
    n = len(parts)

    def body(*refs):
        ins, lnd = refs[:n], refs[n:2 * n]
        send_ref, recv_ref = refs[2 * n], refs[2 * n + 1]
        _, _, c, j, chips = _place()
        for t, (px, py) in enumerate(chips):
            jt = 2 * px + py
            for a in range(n):
                _remote(ins[a].at[jt], lnd[a].at[j], send_ref.at[3 * a + t], recv_ref.at[3 * a + t], (px, py, c)).wait_send()
                _remote(ins[a].at[jt], lnd[a].at[jt], send_ref.at[3 * a + t], recv_ref.at[3 * a + t], (px, py, c)).wait_recv()

    bufs = [pltpu.HBM(p.shape, p.dtype) for p in parts]
    res = pl.pallas_call(
        body, name=f"exchange_wait_{tag}", in_specs=[HBM] * (2 * n) + [SEM, SEM, ANY], out_specs=[HBM] * (2 * n),
        out_shape=bufs + bufs, input_output_aliases={a: a for a in range(2 * n)}, compiler_params=_in_flight_params(),
    )(*parts, *lands, send, recv, after)
    return res[:n], res[n:]


def _small_chip_sums(arrs):
    n = len(arrs)

    def body(*refs):
        ins, outs = refs[:n], refs[n:2 * n]
        sib = refs[2 * n:3 * n]
        send, recv = refs[3 * n:]
        x, y, c, j, _ = _place()
        swaps = [_remote(ins[a], sib[a], send.at[a], recv.at[a], (x, y, 1 - c)) for a in range(n)]
        for cp in swaps:
            cp.start()
        for a in range(n):
            swaps[a].wait_recv()
            outs[a][j] = ins[a][...] + sib[a][...]
        for cp in swaps:
            cp.wait_send()

    sem = pltpu.SemaphoreType.DMA
    vm = pl.BlockSpec(memory_space=pltpu.VMEM)
    return pl.pallas_call(
        body, name="small_chip_sums", in_specs=[vm] * n, out_specs=[vm] * n,
        out_shape=[jax.ShapeDtypeStruct((N_SHARD, *a.shape), F32) for a in arrs],
        scratch_shapes=[pltpu.VMEM(a.shape, F32) for a in arrs] + [sem((n,)), sem((n,))],
        compiler_params=_cp(),
    )(*arrs)


def _small_totals(stacks):
    n = len(stacks)

    def body(*refs):
        for a in range(n):
            refs[n + a][...] = ((refs[a][0] + refs[a][1]) + refs[a][2]) + refs[a][3]

    return pl.pallas_call(body, name="small_totals", out_shape=[jax.ShapeDtypeStruct(s.shape[1:], F32) for s in stacks],
                          compiler_params=_cp())(*stacks)


SMALL_1024 = ("ln1_g", "ln1_b", "ln2_g", "ln2_b", "b_ple_gate", "ln3_g", "ln3_b")


def _adamw_small(red3, red1, redz, g_conv_w, redc, red_ws, red_bs, params):
    shape2d = {"ln_z_g": (1, D_GMLP), "ln_z_b": (1, D_GMLP), "w_s": (N_HEADS * BLK, BLK), "b_s": (N_HEADS, BLK),
               "conv_w": (3, FF_BLK), "conv_b": (1, D_FF), **{k: (1, D_MODEL) for k in SMALL_1024}}
    names = list(shape2d)
    flat = [a.reshape(shape2d[k]) for k in names for a in params[k]]

    def body(r3, r1, rz, gcw, rc, rws, rbs, *refs):
        ins, outs = refs[:3 * len(names)], refs[3 * len(names):]

        def grad_of(k):
            if k == "w_s":
                return rws[...]
            if k == "b_s":
                return rbs[...]
            if k == "conv_w":
                return gcw[0:3, :]
            src, row = {"ln3_g": (r3, 0), "ln3_b": (r3, 1), "b_ple_gate": (r3, 2), "ln2_g": (r3, 3), "ln2_b": (r3, 4),
                        "ln1_g": (r1, 0), "ln1_b": (r1, 1), "ln_z_g": (rz, 0), "ln_z_b": (rz, 1)}[k]
            return src[row:row + 1, :]

        for i, k in enumerate(names):
            w_ref, m_ref, v_ref = ins[3 * i:3 * i + 3]
            g_ref, d_ref, nm_ref, nv_ref = outs[4 * i:4 * i + 4]
            if k == "conv_b":
                for j in range(N_SHARD):
                    cols = slice(j * FF_BLK, (j + 1) * FF_BLK)
                    g = rc[j * STAT_ROWS + 3:j * STAT_ROWS + 4, :]
                    g_ref[:, cols] = g
                    d_ref[:, cols], nm_ref[:, cols], nv_ref[:, cols] = _adamw_math(w_ref[:, cols], g, m_ref[:, cols], v_ref[:, cols])
                continue
            g = grad_of(k)
            g_ref[...] = g
            d_ref[...], nm_ref[...], nv_ref[...] = _adamw_math(w_ref[...], g, m_ref[...], v_ref[...])

    res = pl.pallas_call(
        body, name="adamw_small",
        out_shape=[jax.ShapeDtypeStruct(shape2d[k], F32) for k in names for _ in range(4)],
        compiler_params=_cp(),
    )(red3, red1, redz, g_conv_w, redc, red_ws, red_bs, *flat)
    return {k: tuple(r.reshape(params[k][0].shape) for r in res[4 * i:4 * i + 4]) for i, k in enumerate(names)}


WEIGHTS = ("w_in", "ln_z_g", "ln_z_b", "w_s", "b_s", "w_o", "ln1_g", "ln1_b", "w_ff_a", "w_ff_b", "conv_w", "conv_b",
           "w_ff_down", "ln2_g", "ln2_b", "w_ple_gate", "b_ple_gate", "w_ple_in", "ln3_g", "ln3_b")
BIG = ("w_in", "w_o", "w_ff_a", "w_ff_b", "w_ff_down", "w_ple_gate", "w_ple_in")
TRANSPOSED = ("w_ff_a", "w_ff_b")
LATE = ("w_o", "w_ff_a", "w_ff_b", "w_ff_down", "w_ple_gate", "w_ple_in", "conv_w")


def kernel(x, p, positions, w_in, ln_z_g, ln_z_b, w_s, b_s, w_o, ln1_g, ln1_b, w_ff_a, w_ff_b, conv_w, conv_b, w_ff_down, ln2_g, ln2_b, w_ple_gate, b_ple_gate, w_ple_in, ln3_g, ln3_b, loss_target, m_w_in, m_ln_z_g, m_ln_z_b, m_w_s, m_b_s, m_w_o, m_ln1_g, m_ln1_b, m_w_ff_a, m_w_ff_b, m_conv_w, m_conv_b, m_w_ff_down, m_ln2_g, m_ln2_b, m_w_ple_gate, m_b_ple_gate, m_w_ple_in, m_ln3_g, m_ln3_b, v_w_in, v_ln_z_g, v_ln_z_b, v_w_s, v_b_s, v_w_o, v_ln1_g, v_ln1_b, v_w_ff_a, v_w_ff_b, v_conv_w, v_conv_b, v_w_ff_down, v_ln2_g, v_ln2_b, v_w_ple_gate, v_b_ple_gate, v_w_ple_in, v_ln3_g, v_ln3_b):
    args = locals()
    w = {k: args[k] for k in WEIGHTS}
    m = {k: args["m_" + k] for k in WEIGHTS}
    v = {k: args["v_" + k] for k in WEIGHTS}

    for k in TRANSPOSED:
        w[k], m[k], v[k] = (jnp.swapaxes(a, 1, 2) for a in (w[k], m[k], v[k]))

    chip = 2 * lax.axis_index("x") + lax.axis_index("y")
    place = jnp.stack([chip, lax.axis_index("c")]).astype(jnp.int32)
    stack = dict(zip(["w_in"], _place_shards("cast_w_in", [w["w_in"][0]], [MXU], place, place)))
    i_send, i_recv, in_flight, dep = _gather_start([stack["w_in"]], [True], place, "w_in")
    stack.update(zip(LATE, _place_shards("cast_late", [w[k][0] for k in LATE],
                                         [F32 if k == "conv_w" else MXU for k in LATE], place, dep)))
    split_late = [k != "conv_w" for k in LATE]
    g_send, g_recv, late_flight, start_dep = _gather_start([stack[k] for k in LATE], split_late, place, "late")
    rope = _rope_tables(positions, x.shape[1], start_dep)
    landed_in = _gather_wait(i_send, i_recv, in_flight, [True], rope[0], "w_in")
    w_in_full, = _gather_forward(landed_in, [True], "w_in")
    halves =[k for k, sp in zip(LATE, split_late) if sp]
    trips = {}

    def late_landed(after):
        fw = dict(zip(LATE, _gather_wait(g_send, g_recv, late_flight, split_late, after, "late")))
        trips["late"] = (fw, *_forward_start([fw[k] for k in halves], fw["conv_w"], "late"))
        return trips["late"][-1]

    def late_weights(after):
        fw, send, recv, flight, _ = trips["late"]
        fw.update(zip(halves, _forward_wait(send, recv, flight, after, "late")))
        return (fw["w_o"].reshape(D_MODEL, D_MODEL), fw["w_ff_a"], fw["w_ff_b"], fw["conv_w"], fw["w_ff_down"],
                fw["w_ple_gate"].reshape(D_MODEL, D_MODEL), fw["w_ple_in"])

    def swap_started(names, grads, tag):
        stacked = [g.reshape(N_SHARD, *w[k].shape[1:]) for k, g in zip(names, grads)]
        return (names, tag, *_swap_start(stacked, tag))

    def partial_sums(swap, after):
        names, tag, send, recv, stacked, gots, _ = swap
        stacked, got = _swap_wait(send, recv, stacked, gots, after, tag)
        pair = _pair_sums(f"rs_pair_{tag}", stacked, got, place)
        return (names, tag, *_exchange_start(pair, tag))

    def chip_summed(trip, after, dep):
        names, tag, send, recv, pair, lands, _ = trip
        pair, landed = _exchange_wait(send, recv, pair, lands, after, tag)
        return _chip_sums(f"rs_sum_{tag}", pair, landed, place, dep), names, tag

    def reduced(trip, after, dep):
        blocks, names, tag = chip_summed(trip, after, dep)
        return dict(zip(names, _sibling_join(blocks, tag)))

    def early_grads_landed(after):
        blocks, names, tag = chip_summed(trips["early"], after, trips["small"][-1])
        trips["join"] = (names, *_join_start(blocks, after, tag))
        return trips["join"][-1]

    def early_grads(grads):
        trips["swap"] = swap_started(list(grads), list(grads.values()), "early")
        return trips["swap"][-1]

    def early_grads_sent(after, small):
        trips["early"] = partial_sums(trips["swap"], after)
        stat3, stat1, zstat, cstat, dws, dbs = small
        sums = _small_chip_sums([stat3, stat1, zstat, cstat.reshape(N_SHARD * STAT_ROWS, FF_BLK),
                                 dws.reshape(N_HEADS * BLK, BLK), dbs])
        trips["small"] = _gather_start(sums, [False] * len(sums), trips["early"][-1], "small")
        return trips["small"][-1]

    grad_x, g_w_in = _local_step(
        x[0], p[0, 0], rope, loss_target[0], w_in_full, start_dep, late_landed, late_weights, early_grads, early_grads_sent,
        early_grads_landed, ln_z_g, ln_z_b, w_s, b_s, ln1_g, ln1_b, conv_b, ln2_g, ln2_b, b_ple_gate, ln3_g, ln3_b)

    swap_in = swap_started(["w_in"], [g_w_in], "w_in")
    trips["w_in"] = partial_sums(swap_in, swap_in[-1])
    out = {}

    def adamw(red, tag):
        names = list(red)
        steps = _adamw_shards(f"adamw_{tag}", [w[k] for k in names], [red[k] for k in names], [m[k] for k in names],
                              [v[k] for k in names])
        out.update(zip(names, steps))

    names, j_send, j_recv, j_flight, _ = trips["join"]
    adamw(dict(zip(names, _join_wait(j_send, j_recv, j_flight, trips["w_in"][-1], "early"))), "early")
    adamw(reduced(trips["w_in"], out["w_o"][3], start_dep), "w_in")
    for k in TRANSPOSED:
        out[k] = tuple(jnp.swapaxes(a, 1, 2) for a in out[k])

    s_send, s_recv, s_flight, _ = trips["small"]
    red3, red1, redz, redc, red_ws, red_bs = _small_totals(
        _gather_wait(s_send, s_recv, s_flight, [False] * len(s_flight), out["w_in"][3], "small"))
    loss = (0.5 / D_MODEL) * jnp.sum(red3[5])
    g_conv_w = lax.dynamic_slice_in_dim(redc, chip * STAT_ROWS, STAT_ROWS, 0)
    names_small = [k for k in WEIGHTS if k not in BIG]
    out.update(_adamw_small(red3, red1, redz, g_conv_w, redc, red_ws, red_bs, {k: (w[k], m[k], v[k]) for k in names_small}))

    return (loss, grad_x[None], *[out[k][0] for k in WEIGHTS], *[out[k][1] for k in WEIGHTS],
            *[out[k][2] for k in WEIGHTS], *[out[k][3] for k in WEIGHTS])
```

```python
import math

import numpy as np
import jax
import jax.numpy as jnp
from jax import lax
from jax.experimental import pallas as pl
from jax.experimental.pallas import tpu as pltpu

F32 = jnp.float32
BF16 = jnp.bfloat16
MXU = BF16

D_MODEL = 1024
HEAD_DIM = 64
N_HEADS = 8
D_ATTN = 512
D_GMLP = 512
D_IN = 2560
DILATIONS = (1, 4, 16)
BLK = 128
ROPE_THETA = 500000.0
ROPE_DIM = 16
D_FF = 2816
D_PLE = 256
LN_EPS = 1e-5
ALPHA = 2.0 ** 0.25
NEG_INF = -1e30
N_SHARD = 4
W_IN_BLK = D_IN // N_SHARD
FF_BLK = D_FF // N_SHARD
ROW_BLK = D_MODEL // N_SHARD
ADAM_LR, ADAM_B1, ADAM_B2, ADAM_EPS, ADAM_WD, ADAM_STEP = 0.001, 0.9, 0.999, 1e-08, 0.01, 10

TM = 512
HALO = 8
ROW_GROUPS = 2
VMEM_LIMIT = 56 * 1024 * 1024


def _cp(**kw):
    return pltpu.CompilerParams(vmem_limit_bytes=VMEM_LIMIT, **kw)


def _full(shape):
    n = len(shape)
    return pl.BlockSpec(shape, lambda *_: (0,) * n)


def _gelu(x):
    return 0.5 * x * (1.0 + lax.erf(x * (1.0 / math.sqrt(2.0))))


def _gelu_grad(x):
    return 0.5 * (1.0 + lax.erf(x * (1.0 / math.sqrt(2.0)))) + x * jnp.exp(-0.5 * x * x) * (1.0 / math.sqrt(2.0 * math.pi))


def _ln_fwd(r):
    mu = jnp.mean(r, axis=-1, keepdims=True)
    xc = r - mu
    var = jnp.mean(xc * xc, axis=-1, keepdims=True)
    rstd = lax.rsqrt(var + LN_EPS)
    return xc * rstd, rstd


def _ln_bwd(dy, xhat, rstd, g):
    dxh = dy * g
    m1 = jnp.mean(dxh, axis=-1, keepdims=True)
    m2 = jnp.mean(dxh * xhat, axis=-1, keepdims=True)
    return rstd * (dxh - m1 - xhat * m2)


def _dot(a, b):
    return jnp.dot(a.astype(MXU), b.astype(MXU), preferred_element_type=F32)


def _dot_nt(a, b):
    return lax.dot_general(a.astype(MXU), b.astype(MXU), (((1,), (1,)), ((), ())), preferred_element_type=F32)


def _dot_tn(a, b):
    return lax.dot_general(a.astype(MXU), b.astype(MXU), (((0,), (0,)), ((), ())), preferred_element_type=F32)


def _colsum(v):
    return jnp.sum(v, axis=0, keepdims=True)


def _rope_tables(positions, t, dep):
    inv = np.float32(ROPE_THETA) ** (-np.arange(0, ROPE_DIM, 2, dtype=np.float32) / np.float32(ROPE_DIM))
    half = ROPE_DIM // 2
    pos_rep = jnp.repeat(positions.reshape(t // 16, 16), half, axis=1)
    inv_row = jnp.asarray(np.tile(inv, 16)[None, :], F32)

    def trig_body(pos_ref, inv_ref, dep_ref, cos_ref, sin_ref):
        ang = pos_ref[...].astype(F32) * inv_ref[...]
        cos_ref[...] = jnp.cos(ang)
        sin_ref[...] = jnp.sin(ang)

    vm = pl.BlockSpec(memory_space=pltpu.VMEM)
    cos8, sin8 = pl.pallas_call(
        trig_body, name="rope_trig", in_specs=[vm, vm, pl.BlockSpec(memory_space=pl.ANY)], out_specs=[vm, vm],
        out_shape=(jax.ShapeDtypeStruct((t // 16, 128), F32), jax.ShapeDtypeStruct((t // 16, 128), F32)),
    )(pos_rep, inv_row, dep)
    cos8 = cos8.reshape(t, half)
    sin8 = sin8.reshape(t, half)

    lane = np.arange(128) % HEAD_DIM
    sel = (np.arange(half)[:, None] == (lane % half)[None, :])
    e_cos = (sel & (lane < ROPE_DIM)[None, :]).astype(np.float32)
    e_s1 = -(sel & (lane < half)[None, :]).astype(np.float32)
    e_s2 = (sel & ((lane >= half) & (lane < ROPE_DIM))[None, :]).astype(np.float32)
    ones = (lane >= ROPE_DIM).astype(np.float32)[None, :]

    def expand_body(cos_ref, sin_ref, ec_ref, e1_ref, e2_ref, ones_ref, c_ref, s1_ref, s2_ref):
        c_ref[...] = _dot_select(cos_ref[...], ec_ref[...], terms=3) + ones_ref[...]
        s1_ref[...] = _dot_select(sin_ref[...], e1_ref[...], terms=3)
        s2_ref[...] = _dot_select(sin_ref[...], e2_ref[...], terms=3)

    tab = jax.ShapeDtypeStruct((t, 128), F32)
    return pl.pallas_call(expand_body, name="rope_expand", out_shape=(tab, tab, tab), compiler_params=_cp())(
        cos8, sin8, jnp.asarray(e_cos), jnp.asarray(e_s1), jnp.asarray(e_s2), jnp.asarray(ones))


def _tile_heads(tab):
    return jnp.concatenate([tab] * (D_ATTN // 128), axis=1)


def _rope_apply(v, c, s1, s2):
    n = v.shape[1]
    half = ROPE_DIM // 2
    return v * c + pltpu.roll(v, n - half, 1) * s1 + pltpu.roll(v, half, 1) * s2


def _rope_apply_t(g, c, s1, s2):
    n = g.shape[1]
    half = ROPE_DIM // 2
    return g * c + pltpu.roll(g * s1, half, 1) + pltpu.roll(g * s2, n - half, 1)


LANE_CHUNKS = D_ATTN // 128
HEAD_LANES = 128 // N_HEADS


def _perm_shape(t, d, w, dtype):
    return jax.ShapeDtypeStruct((d, t // d, w), dtype)


def _perm_tile(d, w):
    return pl.BlockSpec((None if d == 1 else d, TM // d, w), lambda i: (0, i, 0))


def _to_planes(ref, scr, d, n_chunks, dtype):
    for r in range(d):
        for cc in range(n_chunks):
            ref[r, :, cc * 128:(cc + 1) * 128] = scr.at[cc][pl.ds(r, TM // d, stride=d), :].astype(dtype)


def _from_planes(ref, scr, d, n_chunks, accumulate=False):
    for r in range(d):
        for cc in range(n_chunks):
            rows = scr.at[cc]
            val = ref[r, :, cc * 128:(cc + 1) * 128].astype(F32)
            if accumulate:
                rows[pl.ds(r, TM // d, stride=d), :] += val
            else:
                rows[pl.ds(r, TM // d, stride=d), :] = val


def _chunks(val):
    return [val[:, cc * 128:(cc + 1) * 128] for cc in range(val.shape[1] // 128)]


def _unchunk(scr, n_chunks, base=0):
    return jnp.concatenate([scr[base + cc] for cc in range(n_chunks)], axis=1)


def _head_expand():
    src = np.arange(128)[:, None]
    dst = np.arange(D_ATTN)[None, :]
    return jnp.asarray((src == (dst // HEAD_DIM) * HEAD_LANES).astype(np.float32))


def _head_reduce():
    src = np.arange(D_ATTN)[:, None]
    dst = np.arange(128)[None, :]
    return jnp.asarray((src // HEAD_DIM == dst // HEAD_LANES).astype(np.float32))


def _dot_select(a, sel, terms=2):
    sel = sel.astype(BF16)
    out, rest = None, a
    for _ in range(terms):
        part = rest.astype(BF16)
        rest = rest - part.astype(F32)
        prod = jnp.dot(part, sel, preferred_element_type=F32)
        out = prod if out is None else out + prod
    return out


def _qkvuz(x, w_in, c_tab, s1_tab, s2_tab, ln_z_g, ln_z_b, w_s, b_full, dep):
    t = x.shape[0]
    nchunk = TM // BLK

    def body(x_ref, w_ref, c_ref, s1_ref, s2_ref, g_ref, b_ref, ws_ref, bf_ref, dep_ref,
             qkv1_ref, qkv4_ref, qkv16_ref, hu_ref, hz_ref, mixed_ref, gm_ref, xb_ref, h_scr, wm_scr, p_scr):
        @pl.when(pl.program_id(0) == 0)
        def _():
            row = lax.broadcasted_iota(jnp.int32, (BLK, BLK), 0)
            col = lax.broadcasted_iota(jnp.int32, (BLK, BLK), 1)
            for g in range(N_HEADS):
                wm_scr[g] = jnp.where(col <= row, ws_ref[g], 0.0).astype(MXU)

        xb = x_ref[...].astype(MXU)
        xb_ref[...] = xb
        for j in range(N_SHARD):
            h_scr[:, j * W_IN_BLK:(j + 1) * W_IN_BLK] = jnp.dot(xb, w_ref[j], preferred_element_type=F32)
        c, s1, s2 = _tile_heads(c_ref[...]), _tile_heads(s1_ref[...]), _tile_heads(s2_ref[...])
        q = _rope_apply(h_scr[:, 0:D_ATTN], c, s1, s2) * (1.0 / math.sqrt(HEAD_DIM))
        k = _rope_apply(h_scr[:, D_ATTN:2 * D_ATTN], c, s1, s2)
        for part, val in enumerate((q, k, h_scr[:, 2 * D_ATTN:3 * D_ATTN])):
            qkv1_ref[:, part * D_ATTN:(part + 1) * D_ATTN] = val.astype(MXU)
            for cc in range(LANE_CHUNKS):
                p_scr[part * LANE_CHUNKS + cc] = val[:, cc * 128:(cc + 1) * 128]
        _to_planes(qkv4_ref, p_scr, DILATIONS[1], 3 * LANE_CHUNKS, MXU)
        _to_planes(qkv16_ref, p_scr, DILATIONS[2], 3 * LANE_CHUNKS, MXU)
        hu = h_scr[:, 3 * D_ATTN:3 * D_ATTN + D_GMLP]
        hz = h_scr[:, 3 * D_ATTN + D_GMLP:]
        hu_ref[...] = hu
        hz_ref[...] = hz
        zhat, _ = _ln_fwd(_gelu(hz))
        zn = (zhat * g_ref[...] + b_ref[...]).astype(MXU)
        for ch in range(nchunk):
            rows = slice(ch * BLK, (ch + 1) * BLK)
            for g in range(N_HEADS):
                cols = slice(g * HEAD_DIM, (g + 1) * HEAD_DIM)
                mixed_ref[rows, cols] = jnp.dot(wm_scr[g], zn[rows, cols], preferred_element_type=F32) + bf_ref[:, cols]
        gm_ref[...] = (_gelu(hu) * mixed_ref[...]).astype(MXU)

    tok = lambda w: pl.BlockSpec((TM, w), lambda i: (i, 0))
    outs = [_perm_shape(t, d, 3 * D_ATTN, MXU) for d in DILATIONS] + [jax.ShapeDtypeStruct((t, D_GMLP), F32)] * 3 + [
        jax.ShapeDtypeStruct((t, D_GMLP), MXU), jax.ShapeDtypeStruct((t, D_MODEL), MXU)]
    return pl.pallas_call(
        body, name="qkvuz", grid=(t // TM,),
        in_specs=[tok(D_MODEL), _full(w_in.shape), tok(128), tok(128), tok(128), _full(ln_z_g.shape), _full(ln_z_b.shape),
                  _full(w_s.shape), _full(b_full.shape), pl.BlockSpec(memory_space=pl.ANY)],
        out_specs=[_perm_tile(d, 3 * D_ATTN) for d in DILATIONS] + [tok(D_ATTN)] * 4 + [tok(D_MODEL)], out_shape=outs,
        scratch_shapes=[pltpu.VMEM((TM, D_IN), F32), pltpu.VMEM((N_HEADS, BLK, BLK), MXU),
                        pltpu.VMEM((3 * LANE_CHUNKS, TM, 128), F32)],
        compiler_params=_cp(dimension_semantics=("arbitrary",)),
    )(x, w_in, c_tab, s1_tab, s2_tab, ln_z_g, ln_z_b, w_s, b_full, dep)


def _band_valid(n):
    i = lax.broadcasted_iota(jnp.int32, (BLK, 2 * BLK), 0)
    j = lax.broadcasted_iota(jnp.int32, (BLK, 2 * BLK), 1)
    return (j >= i) & (j <= i + BLK) & ((j >= BLK) | (n > 0))


def _attn_fwd(qkv, d, dep):
    _, l_sub, _ = qkv.shape
    nb = l_sub // BLK

    def body(q_ref, kp_ref, kc_ref, vp_ref, vc_ref, dep_ref, o_ref, l_ref):
        valid = _band_valid(pl.program_id(1))
        kcat = jnp.concatenate([kp_ref[...], kc_ref[...]], axis=0)
        vcat = jnp.concatenate([vp_ref[...], vc_ref[...]], axis=0)
        for h in range(N_HEADS):
            cols = slice(h * HEAD_DIM, (h + 1) * HEAD_DIM)
            s = jnp.where(valid, _dot_nt(q_ref[:, cols], kcat[:, cols]), NEG_INF)
            m = jnp.max(s, axis=-1, keepdims=True)
            e = jnp.exp(s - m)
            den = jnp.sum(e, axis=-1, keepdims=True)
            o_ref[:, cols] = _dot(e, vcat[:, cols]) * (1.0 / den)
            l_ref[:, h * HEAD_LANES:(h + 1) * HEAD_LANES] = jnp.broadcast_to(m + jnp.log(den), (BLK, HEAD_LANES))

    def blk(w, col, prev=False):
        return pl.BlockSpec((None, BLK, w), lambda r, n: (r, jnp.maximum(n - 1, 0) if prev else n, col))

    return pl.pallas_call(
        body, name=f"attn_fwd_d{d}", grid=(d, nb),
        in_specs=[blk(D_ATTN, 0), blk(D_ATTN, 1, True), blk(D_ATTN, 1), blk(D_ATTN, 2, True), blk(D_ATTN, 2),
                  pl.BlockSpec(memory_space=pl.ANY)],
        out_specs=[blk(D_ATTN, 0), blk(128, 0)],
        out_shape=[jax.ShapeDtypeStruct((d, l_sub, D_ATTN), F32), jax.ShapeDtypeStruct((d, l_sub, 128), F32)],
        compiler_params=_cp(dimension_semantics=("arbitrary", "arbitrary")),
    )(qkv, qkv, qkv, qkv, qkv, dep)


def _attn_bwd(qkv, do, lse, delta, d, dep):
    _, l_sub, _ = qkv.shape
    nb = l_sub // BLK
    whole = l_sub <= 8 * BLK

    def shares(n, q_ref, kp_ref, kc_ref, vp_ref, vc_ref, do_ref, l_ref, dl_ref, dq_ref):
        valid = _band_valid(n)
        kcat = jnp.concatenate([kp_ref[...], kc_ref[...]], axis=0)
        vcat = jnp.concatenate([vp_ref[...], vc_ref[...]], axis=0)
        for h in range(N_HEADS):
            cols = slice(h * HEAD_DIM, (h + 1) * HEAD_DIM)
            stat = slice(h * HEAD_LANES, h * HEAD_LANES + 1)
            qh, doh = q_ref[:, cols], do_ref[:, cols]
            p = jnp.where(valid, jnp.exp(_dot_nt(qh, kcat[:, cols]) - l_ref[:, stat]), 0.0)
            ds = p * (_dot_nt(doh, vcat[:, cols]) - dl_ref[:, stat])
            dq_ref[:, cols] = _dot(ds, kcat[:, cols])
            yield cols, _dot_tn(ds, qh), _dot_tn(p, doh)

    def body_whole(*refs):
        dk_ref, dv_ref = refs[10:]
        n = pl.program_id(1)
        cur = pl.ds(pl.multiple_of(n * BLK, BLK), BLK)
        prev = pl.ds(pl.multiple_of(jnp.maximum(n - 1, 0) * BLK, BLK), BLK)
        for cols, dk2, dv2 in shares(n, *refs[:8], refs[9]):
            dk_ref[cur, cols] = dk2[BLK:]
            dv_ref[cur, cols] = dv2[BLK:]
            dk_ref[prev, cols] += dk2[0:BLK]
            dv_ref[prev, cols] += dv2[0:BLK]

    def body_carry(*refs):
        dk_ref, dv_ref, ck_scr, cv_scr = refs[10:]
        n = pl.program_id(1)

        @pl.when(n == 0)
        def _():
            ck_scr[...] = jnp.zeros_like(ck_scr)
            cv_scr[...] = jnp.zeros_like(cv_scr)

        @pl.when(n < nb)
        def _():
            for cols, dk2, dv2 in shares(n, *refs[:8], refs[9]):
                dk_ref[:, cols] = ck_scr[:, cols] + dk2[0:BLK]
                dv_ref[:, cols] = cv_scr[:, cols] + dv2[0:BLK]
                ck_scr[:, cols] = dk2[BLK:]
                cv_scr[:, cols] = dv2[BLK:]

        @pl.when(n == nb)
        def _():
            dk_ref[...] = ck_scr[...]
            dv_ref[...] = cv_scr[...]

    def blk(w, col, shift=0):
        return pl.BlockSpec((None, BLK, w), lambda r, n: (r, jnp.clip(n - shift, 0, nb - 1), col))

    if whole:
        dkv_spec = pl.BlockSpec((None, l_sub, D_ATTN), lambda r, n: (r, 0, 0))
        body, steps, scratch = body_whole, nb, []
    else:
        dkv_spec = blk(D_ATTN, 0, 1)
        body, steps, scratch = body_carry, nb + 1, [pltpu.VMEM((BLK, D_ATTN), F32)] * 2
    return pl.pallas_call(
        body, name=f"attn_bwd_d{d}", grid=(d, steps),
        in_specs=[blk(D_ATTN, 0), blk(D_ATTN, 1, 1), blk(D_ATTN, 1), blk(D_ATTN, 2, 1), blk(D_ATTN, 2),
                  blk(D_ATTN, 0), blk(128, 0), blk(128, 0), pl.BlockSpec(memory_space=pl.ANY)],
        out_specs=[blk(D_ATTN, 0), dkv_spec, dkv_spec],
        out_shape=[jax.ShapeDtypeStruct((d, l_sub, D_ATTN), F32)] * 3,
        scratch_shapes=scratch,
        compiler_params=_cp(dimension_semantics=("arbitrary", "arbitrary")),
    )(qkv, qkv, qkv, qkv, qkv, do, lse, delta, dep)


def _mix_ln1(os_, ls_, gm, x, w_o, ln1_g, ln1_b, dep):
    t = x.shape[0]
    expand = _head_expand()

    def body(o1, o4, o16, l1, l4, l16, gm_ref, x_ref, wo_ref, g_ref, b_ref, ex_ref, dep_ref,
             attn_ref, lse1_ref, lse4_ref, lse16_ref, cat_ref, xhat_ref, rstd_ref, x1b_ref, o_scr, l_scr):
        _from_planes(o4, o_scr, DILATIONS[1], LANE_CHUNKS)
        _from_planes(o16, o_scr.at[pl.ds(LANE_CHUNKS, LANE_CHUNKS)], DILATIONS[2], LANE_CHUNKS)
        _from_planes(l4, l_scr, DILATIONS[1], 1)
        _from_planes(l16, l_scr.at[pl.ds(1, 1)], DILATIONS[2], 1)
        la, lb, lc = l1[...], l_scr[0], l_scr[1]
        m = jnp.maximum(jnp.maximum(la, lb), lc)
        ea, eb, ec = jnp.exp(la - m), jnp.exp(lb - m), jnp.exp(lc - m)
        den = ea + eb + ec
        inv = 1.0 / den
        wide = lambda w: _dot_select(w, ex_ref[...])
        attn = (wide(ea * inv) * o1[...] + wide(eb * inv) * _unchunk(o_scr, LANE_CHUNKS)
                + wide(ec * inv) * _unchunk(o_scr, LANE_CHUNKS, LANE_CHUNKS))
        attn_ref[...] = attn
        lse = m + jnp.log(den)
        lse1_ref[...] = lse
        l_scr[2] = lse
        _to_planes(lse4_ref, l_scr.at[pl.ds(2, 1)], DILATIONS[1], 1, F32)
        _to_planes(lse16_ref, l_scr.at[pl.ds(2, 1)], DILATIONS[2], 1, F32)
        cat_ref[:, 0:D_ATTN] = attn.astype(MXU)
        cat_ref[:, D_ATTN:] = gm_ref[...]
        mix = jnp.dot(cat_ref[...], wo_ref[...], preferred_element_type=F32)
        xhat, rstd = _ln_fwd(ALPHA * x_ref[...] + mix)
        xhat_ref[...] = xhat
        rstd_ref[...] = rstd
        x1b_ref[...] = (xhat * g_ref[...] + b_ref[...]).astype(MXU)

    tok = lambda w: pl.BlockSpec((TM, w), lambda i: (i, 0))
    outs = [jax.ShapeDtypeStruct((t, D_ATTN), F32)] + [_perm_shape(t, d, 128, F32) for d in DILATIONS] + [
        jax.ShapeDtypeStruct((t, D_MODEL), MXU), jax.ShapeDtypeStruct((t, D_MODEL), F32), jax.ShapeDtypeStruct((t, 1), F32),
        jax.ShapeDtypeStruct((t, D_MODEL), MXU)]
    return pl.pallas_call(
        body, name="mix_ln1", grid=(t // TM,),
        in_specs=[_perm_tile(d, D_ATTN) for d in DILATIONS] + [_perm_tile(d, 128) for d in DILATIONS]
        + [tok(D_GMLP), tok(D_MODEL), _full(w_o.shape), _full(ln1_g.shape), _full(ln1_b.shape), _full(expand.shape),
           pl.BlockSpec(memory_space=pl.ANY)],
        out_specs=[tok(D_ATTN)] + [_perm_tile(d, 128) for d in DILATIONS] + [tok(D_MODEL), tok(D_MODEL), tok(1), tok(D_MODEL)],
        out_shape=outs,
        scratch_shapes=[pltpu.VMEM((2 * LANE_CHUNKS, TM, 128), F32), pltpu.VMEM((3, TM, 128), F32)],
        compiler_params=_cp(dimension_semantics=("arbitrary",)),
    )(*os_, *ls_, gm, x, w_o, ln1_g, ln1_b, expand, dep)


def _conv_fwd(a_ext, w_ref, b_ref, rows):
    back = [pltpu.roll(a_ext, s, 0)[HALO:HALO + rows] for s in (1, 2)]
    return b_ref[...] + w_ref[2:3, :] * a_ext[HALO:HALO + rows] + w_ref[1:2, :] * back[0] + w_ref[0:1, :] * back[1]


def _ffn_in(x1b, w_a, w_b, conv_w, conv_b):
    t = x1b.shape[0]
    hb = TM // HALO

    def body(x_ref, xh_ref, wa_ref, wb_ref, cw_ref, cb_ref, apre_ref, act_ref, gate_ref, f_ref):
        i = pl.program_id(1)
        a_pre = _dot_nt(x_ref[...], wa_ref[...])
        a_halo = jnp.where(i > 0, _dot_nt(xh_ref[...], wa_ref[...]), 0.0)
        a = _conv_fwd(jnp.concatenate([a_halo, a_pre], axis=0), cw_ref, cb_ref, TM)
        b = _dot_nt(x_ref[...], wb_ref[...])
        cdf = 0.5 * (1.0 + lax.erf(a * (1.0 / math.sqrt(2.0))))
        pdf = jnp.exp(-0.5 * a * a) * (1.0 / math.sqrt(2.0 * math.pi))
        act = a * cdf
        apre_ref[...] = a_pre
        act_ref[...] = act
        gate_ref[...] = b * (cdf + a * pdf)
        f_ref[...] = (act * b).astype(MXU)

    blk = lambda r, c: pl.BlockSpec((None, r, c), lambda j, i: (j, 0, 0))
    tokj = pl.BlockSpec((None, TM, FF_BLK), lambda j, i: (j, i, 0))
    outs = [jax.ShapeDtypeStruct((N_SHARD, t, FF_BLK), F32)] * 3 + [jax.ShapeDtypeStruct((N_SHARD, t, FF_BLK), MXU)]
    return pl.pallas_call(
        body, name="ffn_in", grid=(N_SHARD, t // TM),
        in_specs=[pl.BlockSpec((TM, D_MODEL), lambda j, i: (i, 0)),
                  pl.BlockSpec((HALO, D_MODEL), lambda j, i: (jnp.maximum(i * hb - 1, 0), 0)),
                  blk(FF_BLK, D_MODEL), blk(FF_BLK, D_MODEL), blk(3, FF_BLK), blk(1, FF_BLK)],
        out_specs=[tokj, tokj, tokj, tokj], out_shape=outs,
        compiler_params=_cp(dimension_semantics=("arbitrary", "arbitrary")),
    )(x1b, x1b, w_a, w_b, conv_w, conv_b)


def _ffn_out_ln2(f, w_down, xhat1, ln1_g, ln1_b):
    t = xhat1.shape[0]

    def body(f_ref, wd_ref, xh_ref, g1_ref, b1_ref, xhat_ref, rstd_ref):
        half = TM // ROW_GROUPS
        for r0 in range(0, TM, half):
            rows = pl.ds(r0, half)
            ff = jnp.dot(f_ref[0, rows, :], wd_ref[0], preferred_element_type=F32)
            for j in range(1, N_SHARD):
                ff = ff + jnp.dot(f_ref[j, rows, :], wd_ref[j], preferred_element_type=F32)
            x1 = xh_ref[rows, :] * g1_ref[...] + b1_ref[...]
            xhat, rstd = _ln_fwd(ALPHA * x1 + ff)
            xhat_ref[rows, :] = xhat
            rstd_ref[rows, :] = rstd

    tok = lambda w: pl.BlockSpec((TM, w), lambda i: (i, 0))
    vec = _full((1, D_MODEL))
    outs = [jax.ShapeDtypeStruct((t, D_MODEL), F32), jax.ShapeDtypeStruct((t, 1), F32)]
    return pl.pallas_call(
        body, name="ffn_out_ln2", grid=(t // TM,),
        in_specs=[pl.BlockSpec((N_SHARD, TM, FF_BLK), lambda i: (0, i, 0)), _full(w_down.shape), tok(D_MODEL), vec, vec],
        out_specs=[tok(D_MODEL), tok(1)], out_shape=outs,
        compiler_params=_cp(dimension_semantics=("arbitrary",)),
    )(f, w_down, xhat1, ln1_g, ln1_b)


STAT_ROWS = 8


def _ple_loss_bwd(xhat2, rstd2, p, target, ln2_g, ln2_b, w_g, b_g, w_p, ln3_g, ln3_b):
    t = xhat2.shape[0]

    def body(xh2_ref, rs2_ref, p_ref, t_ref, g2_ref, b2_ref, wg_ref, bg_ref, wp_ref, g3_ref, b3_ref,
             dr2_ref, dr2b_ref, stat_ref, dwg_ref, dwp_ref, pp_scr, dwp_scr):
        @pl.when(pl.program_id(0) == 0)
        def _():
            stat_ref[...] = jnp.zeros_like(stat_ref)
            dwg_ref[...] = jnp.zeros_like(dwg_ref)
            dwp_scr[...] = jnp.zeros_like(dwp_scr)

        xhat2 = xh2_ref[...]
        x2 = xhat2 * g2_ref[...] + b2_ref[...]
        x2b = x2.astype(MXU)
        gate = jax.nn.sigmoid(jnp.dot(x2b, wg_ref[...], preferred_element_type=F32) + bg_ref[...])
        pb = p_ref[...].astype(MXU)
        for j in range(N_SHARD):
            pp_scr[:, j * ROW_BLK:(j + 1) * ROW_BLK] = jnp.dot(pb, wp_ref[j], preferred_element_type=F32)
        pp = pp_scr[...]
        xhat3, rstd3 = _ln_fwd(ALPHA * x2 + gate * pp)
        err = xhat3 * g3_ref[...] + b3_ref[...] - t_ref[...]
        dy = err * (1.0 / D_MODEL)
        dr3 = _ln_bwd(dy, xhat3, rstd3, g3_ref[...])
        dgp = dr3 * pp * gate * (1.0 - gate)
        dgp_b = dgp.astype(MXU)
        dwg_ref[...] += _dot_tn(x2b, dgp_b)
        dwp_scr[...] += _dot_tn(pb, dr3 * gate)
        dx2 = ALPHA * dr3 + _dot_nt(dgp_b, wg_ref[...])
        dr2 = _ln_bwd(dx2, xhat2, rs2_ref[...], g2_ref[...])
        dr2_ref[...] = dr2
        dr2b_ref[...] = dr2.astype(MXU)
        stat_ref[0:1, :] += _colsum(dy * xhat3)
        stat_ref[1:2, :] += _colsum(dy)
        stat_ref[2:3, :] += _colsum(dgp)
        stat_ref[3:4, :] += _colsum(dx2 * xhat2)
        stat_ref[4:5, :] += _colsum(dx2)
        stat_ref[5:6, :] += _colsum(err * err)

        @pl.when(pl.program_id(0) == t // TM - 1)
        def _():
            for j in range(N_SHARD):
                dwp_ref[j] = dwp_scr[:, j * ROW_BLK:(j + 1) * ROW_BLK]

    tok = lambda w: pl.BlockSpec((TM, w), lambda i: (i, 0))
    vec = _full((1, D_MODEL))
    outs = [jax.ShapeDtypeStruct((t, D_MODEL), F32), jax.ShapeDtypeStruct((t, D_MODEL), MXU),
            jax.ShapeDtypeStruct((STAT_ROWS, D_MODEL), F32), jax.ShapeDtypeStruct((D_MODEL, D_MODEL), F32),
            jax.ShapeDtypeStruct((N_SHARD, D_PLE, ROW_BLK), F32)]
    return pl.pallas_call(
        body, name="ple_loss_bwd", grid=(t // TM,),
        in_specs=[tok(D_MODEL), tok(1), tok(D_PLE), tok(D_MODEL), vec, vec, _full(w_g.shape), vec, _full(w_p.shape), vec, vec],
        out_specs=[tok(D_MODEL), tok(D_MODEL), _full((STAT_ROWS, D_MODEL)), _full((D_MODEL, D_MODEL)),
                   _full((N_SHARD, D_PLE, ROW_BLK))], out_shape=outs,
        scratch_shapes=[pltpu.VMEM((TM, D_MODEL), F32), pltpu.VMEM((D_PLE, D_MODEL), F32)],
        compiler_params=_cp(dimension_semantics=("arbitrary",)),
    )(xhat2, rstd2, p, target, ln2_g, ln2_b, w_g, b_g, w_p, ln3_g, ln3_b)


def _ffn_bwd(dr2, dr2b, a_pre, act, gate, w_down, w_a, w_b, conv_w, xhat1, rstd1, ln1_g, cat):
    t = dr2.shape[0]
    nt = t // TM
    hb = TM // HALO
    last_h = t // HALO - 1
    halo2 = 2 * HALO

    def body(dr_ref, drb_ref, drbn_ref, ap_ref, act_ref, gate_ref, gaten_ref, wd_ref, wa_ref, wb_ref, cw_ref,
             xh_ref, rs_ref, g1_ref, cat_ref, dap_ref, dbb_ref, dr1_ref, cstat_ref, lstat_ref, dwo_ref, acc_scr):
        i, j = pl.program_id(0), pl.program_id(1)

        @pl.when((i == 0) & (j == 0))
        def _():
            cstat_ref[...] = jnp.zeros_like(cstat_ref)
            lstat_ref[...] = jnp.zeros_like(lstat_ref)
            dwo_ref[...] = jnp.zeros_like(dwo_ref)

        half = TM // ROW_GROUPS
        parts = []
        for r0 in range(0, TM, half):
            rows = pl.ds(r0, half)
            last = r0 + half == TM

            def ext(ref, nxt):
                return jnp.concatenate([ref[rows], nxt[...]], axis=0) if last else ref[r0:r0 + half + HALO]

            drb = jnp.concatenate([drb_ref[rows, :], drbn_ref[...]], axis=0) if last else drb_ref[r0:r0 + half + halo2, :]
            df = _dot_nt(drb, wd_ref[...])[0:half + HALO]
            da = df * ext(gate_ref, gaten_ref)
            if last:
                da = jnp.concatenate([da[0:half], jnp.where(i < nt - 1, da[half:], 0.0)], axis=0)
            ahead = [da[0:half]] + [pltpu.roll(da, half + HALO - s, 0)[0:half] for s in (1, 2)]
            da_pre = cw_ref[2:3, :] * ahead[0] + cw_ref[1:2, :] * ahead[1] + cw_ref[0:1, :] * ahead[2]
            dbb = df[0:half] * act_ref[rows, :]
            dap_ref[rows, :] = da_pre.astype(MXU)
            dbb_ref[rows, :] = dbb.astype(MXU)
            for kk in range(3):
                cstat_ref[j, kk:kk + 1, :] += _colsum(ahead[2 - kk] * ap_ref[rows, :])
            cstat_ref[j, 3:4, :] += _colsum(ahead[0])
            parts.append(_dot(da_pre, wa_ref[...]) + _dot(dbb, wb_ref[...]))
        part = jnp.concatenate(parts, axis=0)

        @pl.when(j == 0)
        def _():
            acc_scr[...] = ALPHA * dr_ref[...] + part

        @pl.when(j > 0)
        def _():
            acc_scr[...] += part

        @pl.when(j == N_SHARD - 1)
        def _():
            dx1 = acc_scr[...]
            xhat1 = xh_ref[...]
            lstat_ref[0:1, :] += _colsum(dx1 * xhat1)
            lstat_ref[1:2, :] += _colsum(dx1)
            dr1 = _ln_bwd(dx1, xhat1, rs_ref[...], g1_ref[...])
            dr1_ref[...] = dr1
            dwo_ref[...] += _dot_tn(cat_ref[...], dr1)

    tok = lambda w: pl.BlockSpec((TM, w), lambda i, j: (i, 0))
    tokj = pl.BlockSpec((None, TM, FF_BLK), lambda i, j: (j, i, 0))
    nextj = pl.BlockSpec((None, HALO, FF_BLK), lambda i, j: (j, jnp.minimum((i + 1) * hb, last_h), 0))
    blk = lambda r, c: pl.BlockSpec((None, r, c), lambda i, j: (j, 0, 0))
    outs = [jax.ShapeDtypeStruct((N_SHARD, t, FF_BLK), MXU)] * 2 + [
        jax.ShapeDtypeStruct((t, D_MODEL), F32), jax.ShapeDtypeStruct((N_SHARD, STAT_ROWS, FF_BLK), F32),
        jax.ShapeDtypeStruct((STAT_ROWS, D_MODEL), F32), jax.ShapeDtypeStruct((D_MODEL, D_MODEL), F32)]
    return pl.pallas_call(
        body, name="ffn_bwd", grid=(nt, N_SHARD),
        in_specs=[tok(D_MODEL), tok(D_MODEL),
                  pl.BlockSpec((halo2, D_MODEL), lambda i, j: (jnp.minimum((i + 1) * (hb // 2), last_h // 2), 0)),
                  tokj, tokj, tokj, nextj, blk(FF_BLK, D_MODEL), blk(FF_BLK, D_MODEL), blk(FF_BLK, D_MODEL),
                  blk(3, FF_BLK), tok(D_MODEL), tok(1), _full((1, D_MODEL)), tok(D_MODEL)],
        out_specs=[tokj, tokj, tok(D_MODEL), _full((N_SHARD, STAT_ROWS, FF_BLK)), _full((STAT_ROWS, D_MODEL)),
                   _full((D_MODEL, D_MODEL))], out_shape=outs,
        scratch_shapes=[pltpu.VMEM((TM, D_MODEL), F32)],
        compiler_params=_cp(dimension_semantics=("arbitrary", "arbitrary")),
    )(dr2, dr2b, dr2b, a_pre, act, gate, gate, w_down, w_a, w_b, conv_w, xhat1, rstd1, ln1_g, cat)


def _mix_bwd(dr1, w_o, hu, hz, mixed, attn, ln_z_g, ln_z_b, w_s, dep):
    t = dr1.shape[0]
    nchunk = TM // BLK

    def body(dr_ref, wo_ref, hu_ref, hz_ref, mx_ref, attn_ref, g_ref, b_ref, ws_ref, grp_ref, red_ref, dep_ref,
             do1_ref, do4_ref, do16_ref, dl1_ref, dl4_ref, dl16_ref, duz_ref, dws_ref, dbs_ref, zstat_ref,
             wm_scr, dzn_scr, dbsum_scr, do_scr, dl_scr):
        @pl.when(pl.program_id(0) == 0)
        def _():
            row = lax.broadcasted_iota(jnp.int32, (BLK, BLK), 0)
            col = lax.broadcasted_iota(jnp.int32, (BLK, BLK), 1)
            for g in range(N_HEADS):
                wm_scr[g] = jnp.where(col <= row, ws_ref[g], 0.0).astype(MXU)
            dws_ref[...] = jnp.zeros_like(dws_ref)
            dbsum_scr[...] = jnp.zeros_like(dbsum_scr)
            zstat_ref[...] = jnp.zeros_like(zstat_ref)

        dcat = _dot_nt(dr_ref[...], wo_ref[...])
        dattn = dcat[:, 0:D_ATTN]
        do1_ref[...] = dattn.astype(MXU)
        for cc, val in enumerate(_chunks(dattn)):
            do_scr[cc] = val
        _to_planes(do4_ref, do_scr, DILATIONS[1], LANE_CHUNKS, MXU)
        _to_planes(do16_ref, do_scr, DILATIONS[2], LANE_CHUNKS, MXU)
        delta = _dot_select(dattn * attn_ref[...], red_ref[...])
        dl1_ref[...] = delta
        dl_scr[0] = delta
        _to_planes(dl4_ref, dl_scr, DILATIONS[1], 1, F32)
        _to_planes(dl16_ref, dl_scr, DILATIONS[2], 1, F32)
        dgm = dcat[:, D_ATTN:]
        hu, hz = hu_ref[...], hz_ref[...]
        u = _gelu(hu)
        duz_ref[:, 0:D_GMLP] = (dgm * mx_ref[...] * _gelu_grad(hu)).astype(MXU)
        dmixed = dgm * u
        dmb = dmixed.astype(MXU)
        zhat, rstd = _ln_fwd(_gelu(hz))
        znb = (zhat * g_ref[...] + b_ref[...]).astype(MXU)
        dbs_acc = jnp.zeros((BLK, D_GMLP), F32)
        for ch in range(nchunk):
            rows = slice(ch * BLK, (ch + 1) * BLK)
            dbs_acc = dbs_acc + dmixed[rows]
            for g in range(N_HEADS):
                cols = slice(g * HEAD_DIM, (g + 1) * HEAD_DIM)
                dzn_scr[rows, cols] = _dot_tn(wm_scr[g], dmb[rows, cols])
                dws_ref[g] += _dot_nt(dmb[rows, cols], znb[rows, cols])
        dbsum_scr[...] += dbs_acc
        dzn = dzn_scr[...]
        zstat_ref[0:1, :] += _colsum(dzn * zhat)
        zstat_ref[1:2, :] += _colsum(dzn)
        duz_ref[:, D_GMLP:] = (_ln_bwd(dzn, zhat, rstd, g_ref[...]) * _gelu_grad(hz)).astype(MXU)

        @pl.when(pl.program_id(0) == nt - 1)
        def _():
            row = lax.broadcasted_iota(jnp.int32, (BLK, BLK), 0)
            col = lax.broadcasted_iota(jnp.int32, (BLK, BLK), 1)
            for g in range(N_HEADS):
                dws_ref[g] = jnp.where(col <= row, dws_ref[g], 0.0)
            dbs_ref[...] = lax.dot_general(grp_ref[...], dbsum_scr[...], (((1,), (1,)), ((), ())),
                                           precision=lax.Precision.HIGHEST, preferred_element_type=F32)

    nt = t // TM
    tok = lambda w: pl.BlockSpec((TM, w), lambda i: (i, 0))
    grp = jnp.asarray((np.arange(D_GMLP)[None, :] // HEAD_DIM == np.arange(N_HEADS)[:, None]).astype(np.float32))
    red = _head_reduce()
    outs = [_perm_shape(t, d, D_ATTN, MXU) for d in DILATIONS] + [_perm_shape(t, d, 128, F32) for d in DILATIONS] + [
        jax.ShapeDtypeStruct((t, 2 * D_GMLP), MXU),
        jax.ShapeDtypeStruct((N_HEADS, BLK, BLK), F32), jax.ShapeDtypeStruct((N_HEADS, BLK), F32),
        jax.ShapeDtypeStruct((STAT_ROWS, D_GMLP), F32)]
    return pl.pallas_call(
        body, name="mix_bwd", grid=(t // TM,),
        in_specs=[tok(D_MODEL), _full(w_o.shape), tok(D_GMLP), tok(D_GMLP), tok(D_GMLP), tok(D_ATTN), _full(ln_z_g.shape),
                  _full(ln_z_b.shape), _full(w_s.shape), _full(grp.shape), _full(red.shape), pl.BlockSpec(memory_space=pl.ANY)],
        out_specs=[_perm_tile(d, D_ATTN) for d in DILATIONS] + [_perm_tile(d, 128) for d in DILATIONS]
        + [tok(2 * D_GMLP), _full((N_HEADS, BLK, BLK)), _full((N_HEADS, BLK)), _full((STAT_ROWS, D_GMLP))],
        out_shape=outs,
        scratch_shapes=[pltpu.VMEM((N_HEADS, BLK, BLK), MXU), pltpu.VMEM((TM, D_GMLP), F32), pltpu.VMEM((BLK, D_GMLP), F32),
                        pltpu.VMEM((LANE_CHUNKS, TM, 128), F32), pltpu.VMEM((1, TM, 128), F32)],
        compiler_params=_cp(dimension_semantics=("arbitrary",)),
    )(dr1, w_o, hu, hz, mixed, attn, ln_z_g, ln_z_b, w_s, grp, red, dep)


def _dx_in(dqs, dks, dvs, duz, dr1, w_in, c_tab, s1_tab, s2_tab):
    t = dr1.shape[0]

    def body(dq1, dq4, dq16, dk1, dk4, dk16, dv1, dv4, dv16, duz_ref, dr_ref, w_ref, c_ref, s1_ref, s2_ref,
             dh_ref, dx_ref, acc_scr):
        sums = []
        for part, (g1, g4, g16) in enumerate(((dq1, dq4, dq16), (dk1, dk4, dk16), (dv1, dv4, dv16))):
            acc = acc_scr.at[pl.ds(part * LANE_CHUNKS, LANE_CHUNKS)]
            for cc in range(LANE_CHUNKS):
                acc[cc] = g1[:, cc * 128:(cc + 1) * 128]
            _from_planes(g4, acc, DILATIONS[1], LANE_CHUNKS, accumulate=True)
            _from_planes(g16, acc, DILATIONS[2], LANE_CHUNKS, accumulate=True)
            sums.append(_unchunk(acc_scr, LANE_CHUNKS, part * LANE_CHUNKS))
        c, s1, s2 = _tile_heads(c_ref[...]), _tile_heads(s1_ref[...]), _tile_heads(s2_ref[...])
        dh_ref[:, 0:D_ATTN] = _rope_apply_t(sums[0] * (1.0 / math.sqrt(HEAD_DIM)), c, s1, s2).astype(MXU)
        dh_ref[:, D_ATTN:2 * D_ATTN] = _rope_apply_t(sums[1], c, s1, s2).astype(MXU)
        dh_ref[:, 2 * D_ATTN:3 * D_ATTN] = sums[2].astype(MXU)
        dh_ref[:, 3 * D_ATTN:] = duz_ref[...]
        dx = ALPHA * dr_ref[...]
        for j in range(N_SHARD):
            dx = dx + _dot_nt(dh_ref[:, j * W_IN_BLK:(j + 1) * W_IN_BLK], w_ref[j])
        dx_ref[...] = dx

    tok = lambda w: pl.BlockSpec((TM, w), lambda i: (i, 0))
    outs = [jax.ShapeDtypeStruct((t, D_IN), MXU), jax.ShapeDtypeStruct((t, D_MODEL), F32)]
    return pl.pallas_call(
        body, name="dx_in", grid=(t // TM,),
        in_specs=[_perm_tile(d, D_ATTN) for d in DILATIONS] * 3
        + [tok(2 * D_GMLP), tok(D_MODEL), _full(w_in.shape), tok(128), tok(128), tok(128)],
        out_specs=[tok(D_IN), tok(D_MODEL)], out_shape=outs,
        scratch_shapes=[pltpu.VMEM((3 * LANE_CHUNKS, TM, 128), F32)],
        compiler_params=_cp(dimension_semantics=("arbitrary",)),
    )(*dqs, *dks, *dvs, duz, dr1, w_in, c_tab, s1_tab, s2_tab)


def _wgrad(name, x, dy, x_spec, dy_spec, out_spec, out_shape, grid, dep=None):
    deps = [] if dep is None else [dep]

    def body(x_ref, dy_ref, *rest):
        rest[-1][...] = _dot_tn(x_ref[...], dy_ref[...])

    return pl.pallas_call(
        body, name=name, grid=grid, in_specs=[x_spec, dy_spec] + [pl.BlockSpec(memory_space=pl.ANY)] * len(deps),
        out_specs=out_spec, out_shape=jax.ShapeDtypeStruct(out_shape, F32),
        compiler_params=_cp(dimension_semantics=("arbitrary",) * len(grid)),
    )(x, dy, *deps)


def _wgrad_pair(name, xa, xb, dy, x_spec, dy_spec, out_spec, out_shape, grid):
    def body(xa_ref, xb_ref, dy_ref, oa_ref, ob_ref):
        dy = dy_ref[...]
        oa_ref[...] = _dot_tn(xa_ref[...], dy)
        ob_ref[...] = _dot_tn(xb_ref[...], dy)

    return pl.pallas_call(
        body, name=name, grid=grid, in_specs=[x_spec, x_spec, dy_spec], out_specs=[out_spec, out_spec],
        out_shape=[jax.ShapeDtypeStruct(out_shape, F32)] * 2,
        compiler_params=_cp(dimension_semantics=("arbitrary",) * len(grid)),
    )(xa, xb, dy)


def _local_step(x, p, rope, target, w_in, start_dep, late_landed, late_weights, early_grads, early_grads_sent,
                early_grads_landed,
                ln_z_g, ln_z_b, w_s, b_s, ln1_g, ln1_b, conv_b, ln2_g, ln2_b, b_g, ln3_g, ln3_b):
    t = x.shape[0]
    half = TM
    c_tab, s1_tab, s2_tab = rope
    b_full = jnp.repeat(jnp.transpose(b_s[0]), HEAD_DIM, axis=1)
    conv_b4 = conv_b.reshape(N_SHARD, 1, FF_BLK)
    *qkvs, hu, hz, mixed, gm, xb = _qkvuz(x, w_in, c_tab, s1_tab, s2_tab, ln_z_g, ln_z_b, w_s[0], b_full, start_dep)
    branches = [_attn_fwd(qkv, d, start_dep) for qkv, d in zip(qkvs[:2], DILATIONS[:2])]
    dep = late_landed(branches[-1][1])
    branches.append(_attn_fwd(qkvs[2], DILATIONS[2], dep))
    w_o, w_a, w_b, conv_w, w_down, w_g, w_p = late_weights(branches[-1][1])
    attn, *lses, cat, xhat1, rstd1, x1b = _mix_ln1(
        [o for o, _ in branches], [l for _, l in branches], gm, x, w_o, ln1_g, ln1_b, dep)
    a_pre, act, gate, f = _ffn_in(x1b, w_a, w_b, conv_w, conv_b4)
    xhat2, rstd2 = _ffn_out_ln2(f, w_down, xhat1, ln1_g, ln1_b)
    dr2, dr2b, stat3, g_w_g, g_w_p = _ple_loss_bwd(xhat2, rstd2, p, target, ln2_g, ln2_b, w_g, b_g, w_p, ln3_g, ln3_b)
    da_pre, dbb, dr1, cstat, stat1, g_w_o = _ffn_bwd(dr2, dr2b, a_pre, act, gate, w_down, w_a, w_b, conv_w, xhat1, rstd1,
                                                    ln1_g, cat)

    full_t = lambda w, im: pl.BlockSpec((t, w), im)
    ffj = pl.BlockSpec((None, t, FF_BLK), lambda j, kk: (j, 0, 0))
    early = dict(
        w_ple_gate=g_w_g, w_ple_in=g_w_p,
        w_ff_down=_wgrad("dw_down", f, dr2b, ffj, full_t(half, lambda j, n: (0, n)),
                         pl.BlockSpec((None, FF_BLK, half), lambda j, n: (j, 0, n)), (N_SHARD, FF_BLK, D_MODEL), (N_SHARD, 2)),
        **dict(zip(("w_ff_a", "w_ff_b"), _wgrad_pair(
            "dw_ab", da_pre, dbb, x1b, ffj, full_t(half, lambda j, n: (0, n)),
            pl.BlockSpec((None, FF_BLK, half), lambda j, n: (j, 0, n)), (N_SHARD, FF_BLK, D_MODEL), (N_SHARD, 2)))),
        w_o=g_w_o)
    dep = early_grads(early)

    do1, do4, do16, dl1, dl4, dl16, duz, dws, dbs, zstat = _mix_bwd(
        dr1, w_o, hu, hz, mixed, attn, ln_z_g, ln_z_b, w_s[0], dep)
    dep = early_grads_sent(duz, (stat3, stat1, zstat, cstat, dws, dbs))
    dqkv = [_attn_bwd(qkv, do, lse, dl, d, dep)
            for qkv, do, lse, dl, d in zip(qkvs, (do1, do4, do16), lses, (dl1, dl4, dl16), DILATIONS)]
    dh, grad_x = _dx_in([g[0] for g in dqkv], [g[1] for g in dqkv], [g[2] for g in dqkv], duz, dr1, w_in,
                        c_tab, s1_tab, s2_tab)
    dep = early_grads_landed(grad_x)
    g_w_in = _wgrad("dw_in", xb, dh, full_t(half, lambda j, kk: (0, kk)), full_t(W_IN_BLK, lambda j, kk: (0, j)),
                    pl.BlockSpec((None, half, W_IN_BLK), lambda j, kk: (j, kk, 0)), (N_SHARD, D_MODEL, W_IN_BLK), (N_SHARD, 2),
                    dep)
    return grad_x, g_w_in


def _tile_rows(rows, mult, steps):
    if rows % mult:
        return rows
    return next(rows // k for k in range(steps, rows + 1) if rows % k == 0 and (rows // k) % mult == 0)


def _grid_spec(grid, in_specs, out_specs):
    return pltpu.PrefetchScalarGridSpec(num_scalar_prefetch=1, grid=grid, in_specs=in_specs, out_specs=out_specs)


def _on_own_steps(i, count, steps, work):
    if count == steps:
        work()
    else:
        pl.when(i < count)(work)


def _place_shards(name, ws, dtypes, place, dep):
    n = len(ws)
    tiles = [_tile_rows(w.shape[0], 16, 2) for w in ws]
    counts = [w.shape[0] // t for w, t in zip(ws, tiles)]
    steps = max(counts)

    def body(s_ref, *refs):
        i = pl.program_id(0)
        for a in range(n):
            def work(a=a):
                refs[n + 1 + a][...] = refs[a][...].astype(dtypes[a])
            _on_own_steps(i, counts[a], steps, work)

    def tile(a, lead):
        last = counts[a] - 1
        if lead:
            return pl.BlockSpec((None, tiles[a], ws[a].shape[1]), lambda i, s: (s[0], jnp.minimum(i, last), 0))
        return pl.BlockSpec((tiles[a], ws[a].shape[1]), lambda i, s: (jnp.minimum(i, last), 0))

    return pl.pallas_call(
        body, name=name,
        grid_spec=_grid_spec((steps,), [tile(a, False) for a in range(n)] + [pl.BlockSpec(memory_space=pl.ANY)],
                             [tile(a, True) for a in range(n)]),
        out_shape=[jax.ShapeDtypeStruct((N_SHARD, *w.shape), dt) for w, dt in zip(ws, dtypes)],
        compiler_params=_cp())(place, *ws, dep)


def _pair_sums(name, mines, gots, place):
    n = len(mines)
    tiles = [_tile_rows(g.shape[1], 16, 1) for g in gots]
    per_blk = [g.shape[1] // t for g, t in zip(gots, tiles)]
    counts = [N_SHARD * nh for nh in per_blk]
    steps = max(counts)

    def body(s_ref, *refs):
        i = pl.program_id(0)
        for a in range(n):
            def work(a=a):
                refs[2 * n + a][...] = (refs[a][...] + refs[n + a][...]).astype(BF16)
            _on_own_steps(i, counts[a], steps, work)

    def tile(a, mine):
        nh, last = per_blk[a], counts[a] - 1

        def index(i, s):
            g = jnp.minimum(i, last)
            return (g // nh, (s[1] * nh if mine else 0) + g % nh, 0)

        return pl.BlockSpec((None, tiles[a], gots[a].shape[2]), index)

    return pl.pallas_call(
        body, name=name,
        grid_spec=_grid_spec((steps,), [tile(a, True) for a in range(n)] + [tile(a, False) for a in range(n)],
                             [tile(a, False) for a in range(n)]),
        out_shape=[jax.ShapeDtypeStruct(g.shape, BF16) for g in gots], compiler_params=_cp())(place, *mines, *gots)


def _chip_sums(name, owns, landeds, place, dep):
    n = len(owns)
    tiles = [_tile_rows(o.shape[1], 16, 4) for o in owns]
    counts = [o.shape[1] // t for o, t in zip(owns, tiles)]
    steps = max(counts)

    def body(s_ref, *refs):
        i = pl.program_id(0)
        for a in range(n):
            def work(a=a):
                own, l1, l2, l3 = (refs[4 * a + k][...].astype(F32) for k in range(4))
                refs[4 * n + 1 + a][...] = ((own + l1) + l2) + l3
            _on_own_steps(i, counts[a], steps, work)

    def slot(a, d):
        last = counts[a] - 1
        return pl.BlockSpec((None, tiles[a], owns[a].shape[2]), lambda i, s: ((s[0] + d) % N_SHARD, jnp.minimum(i, last), 0))

    def out(a):
        nh, last = counts[a], counts[a] - 1
        return pl.BlockSpec((tiles[a], owns[a].shape[2]), lambda i, s: (s[1] * nh + jnp.minimum(i, last), 0))

    operands = [x for o, l in zip(owns, landeds) for x in (o, l, l, l)]
    return pl.pallas_call(
        body, name=name,
        grid_spec=_grid_spec((steps,), [slot(a, d) for a in range(n) for d in range(4)] + [pl.BlockSpec(memory_space=pl.ANY)],
                             [out(a) for a in range(n)]),
        out_shape=[jax.ShapeDtypeStruct((2 * o.shape[1], o.shape[2]), F32) for o in owns],
        compiler_params=_cp())(place, *operands, dep)


def _adamw_math(w, g, m, v):
    m = ADAM_B1 * m + (1.0 - ADAM_B1) * g
    v = ADAM_B2 * v + (1.0 - ADAM_B2) * (g * g)
    m_hat = m / (1.0 - ADAM_B1 ** ADAM_STEP)
    v_hat = v / (1.0 - ADAM_B2 ** ADAM_STEP)
    delta = -ADAM_LR * (m_hat / (jnp.sqrt(v_hat) + ADAM_EPS) + ADAM_WD * w)
    return delta, m, v


def _adamw_shards(name, ws, gs, ms, vs):
    n = len(ws)
    tiles = [_tile_rows(w.shape[1], 8, 8 if n > 1 else 2) for w in ws]
    counts = [w.shape[1] // t for w, t in zip(ws, tiles)]
    steps = max(counts)

    def body(*refs):
        i = pl.program_id(0)
        for a in range(n):
            def work(a=a):
                w_ref, g_ref, m_ref, v_ref = refs[4 * a:4 * a + 4]
                go_ref, d_ref, nm_ref, nv_ref = refs[4 * n + 4 * a:4 * n + 4 * a + 4]
                g = g_ref[...]
                go_ref[...] = g
                d_ref[...], nm_ref[...], nv_ref[...] = _adamw_math(w_ref[...], g, m_ref[...], v_ref[...])
            _on_own_steps(i, counts[a], steps, work)

    def tile(a, lead):
        last, c = counts[a] - 1, ws[a].shape[2]
        if lead:
            return pl.BlockSpec((None, tiles[a], c), lambda i: (0, jnp.minimum(i, last), 0))
        return pl.BlockSpec((tiles[a], c), lambda i: (jnp.minimum(i, last), 0))

    res = pl.pallas_call(
        body, name=name, grid=(steps,),
        in_specs=[tile(a, lead) for a in range(n) for lead in (True, False, True, True)],
        out_specs=[tile(a, True) for a in range(n) for _ in range(4)],
        out_shape=[jax.ShapeDtypeStruct(w.shape, F32) for w in ws for _ in range(4)],
        compiler_params=_cp())(*[x for quad in zip(ws, gs, ms, vs) for x in quad])
    return [tuple(res[4 * a:4 * a + 4]) for a in range(n)]


MESH = pl.DeviceIdType.MESH
ANY = pl.BlockSpec(memory_space=pl.ANY)


def _place():
    x, y, c = lax.axis_index("x"), lax.axis_index("y"), lax.axis_index("c")
    chips = [(1 - x, y), (x, 1 - y), (1 - x, 1 - y)]
    return x, y, c, 2 * x + y, chips


def _remote(src, dst, send_sem, recv_sem, dev):
    return pltpu.make_async_remote_copy(src_ref=src, dst_ref=dst, send_sem=send_sem, recv_sem=recv_sem,
                                        device_id=dev, device_id_type=MESH)


def _half(ref, hc, rows):
    return ref.at[pl.ds(hc * (rows // 2), rows // 2)]


def _sibling_join(blocks, tag):
    n = len(blocks)

    def body(*refs):
        outs = refs[n:2 * n]
        send, recv = refs[2 * n:]
        x, y, c, _, _ = _place()
        cps = []
        for a in range(n):
            h = blocks[a].shape[0] // 2
            mine = outs[a].at[pl.ds(c * h, h)]
            cp = _remote(mine, mine, send.at[a], recv.at[a], (x, y, 1 - c))
            cp.start()
            cps.append(cp)
        for a, cp in enumerate(cps):
            h = blocks[a].shape[0] // 2
            theirs = outs[a].at[pl.ds((1 - c) * h, h)]
            _remote(theirs, theirs, send.at[a], recv.at[a], (x, y, 1 - c)).wait_recv()
            cp.wait_send()

    sem = pltpu.SemaphoreType.DMA
    return pl.pallas_call(body, name=f"rs_sibling_join_{tag}", in_specs=[ANY] * n, out_specs=[ANY] * n,
                          out_shape=[jax.ShapeDtypeStruct(b_.shape, b_.dtype) for b_ in blocks],
                          input_output_aliases={a: a for a in range(n)},
                          scratch_shapes=[sem((n,)), sem((n,))])(*blocks)


def _join_start(blocks, after, tag):
    n = len(blocks)

    def body(*refs):
        ins = refs[:n]
        send, recv = refs[n + 1], refs[n + 2]
        token = refs[2 * n + 3]
        x, y, c, _, _ = _place()
        for a in range(n):
            h = blocks[a].shape[0] // 2
            mine = ins[a].at[pl.ds(c * h, h)]
            _remote(mine, mine, send.at[a], recv.at[a], (x, y, 1 - c)).start()
        token[...] = jnp.zeros_like(token)

    sems = pltpu.SemaphoreType.DMA((n,))
    res = pl.pallas_call(
        body, name=f"join_start_{tag}", in_specs=[HBM] * n + [ANY],
        out_specs=[SEM, SEM] + [HBM] * n + [pl.BlockSpec(memory_space=pltpu.VMEM)],
        out_shape=[sems, sems] + [pltpu.HBM(b_.shape, b_.dtype) for b_ in blocks] + [TOKEN],
        input_output_aliases={a: a + 2 for a in range(n)}, compiler_params=_in_flight_params(),
    )(*[_in_hbm(b_) for b_ in blocks], after)
    return res[0], res[1], res[2:2 + n], res[2 + n]


def _join_wait(send, recv, blocks, after, tag):
    n = len(blocks)

    def body(*refs):
        ins = refs[:n]
        send_ref, recv_ref = refs[n], refs[n + 1]
        x, y, c, _, _ = _place()
        for a in range(n):
            h = blocks[a].shape[0] // 2
            mine, theirs = ins[a].at[pl.ds(c * h, h)], ins[a].at[pl.ds((1 - c) * h, h)]
            _remote(mine, mine, send_ref.at[a], recv_ref.at[a], (x, y, 1 - c)).wait_send()
            _remote(theirs, theirs, send_ref.at[a], recv_ref.at[a], (x, y, 1 - c)).wait_recv()

    return pl.pallas_call(
        body, name=f"join_wait_{tag}", in_specs=[HBM] * n + [SEM, SEM, ANY], out_specs=[HBM] * n,
        out_shape=[pltpu.HBM(b_.shape, b_.dtype) for b_ in blocks],
        input_output_aliases={a: a for a in range(n)}, compiler_params=_in_flight_params(),
    )(*blocks, send, recv, after)


HBM = pl.BlockSpec(memory_space=pltpu.HBM)
SEM = pl.BlockSpec(memory_space=pltpu.SEMAPHORE)
TOKEN = jax.ShapeDtypeStruct((8, 128), F32)


def _in_flight_params():
    return pltpu.CompilerParams(has_side_effects=pltpu.SideEffectType.DATAFLOW_SIDE_EFFECTING)


def _in_hbm(a):
    return pltpu.with_memory_space_constraint(a, pltpu.HBM)


def _gather_piece(ref, rows, split, slot, hc):
    return _half(ref.at[slot], hc, rows) if split else ref.at[slot]


def _gather_start(stacks, split, after, tag):
    n = len(stacks)

    def body(*refs):
        ins = refs[:n]
        send, recv = refs[n + 1], refs[n + 2]
        token = refs[2 * n + 3]
        _, _, c, j, chips = _place()
        for a in range(n):
            mine = _gather_piece(ins[a], stacks[a].shape[1], split[a], j, c)
            for t in range(3):
                _remote(mine, mine, send.at[3 * a + t], recv.at[3 * a + t], (*chips[t], c)).start()
        token[...] = jnp.zeros_like(token)

    sems = pltpu.SemaphoreType.DMA((3 * n,))
    res = pl.pallas_call(
        body, name=f"gather_start_{tag}", in_specs=[HBM] * n + [ANY],
        out_specs=[SEM, SEM] + [HBM] * n + [pl.BlockSpec(memory_space=pltpu.VMEM)],
        out_shape=[sems, sems] + [pltpu.HBM(s.shape, s.dtype) for s in stacks] + [TOKEN],
        input_output_aliases={a: a + 2 for a in range(n)}, compiler_params=_in_flight_params(),
    )(*[_in_hbm(s) for s in stacks], after)
    return res[0], res[1], res[2:2 + n], res[2 + n]


def _gather_wait(send, recv, stacks, split, after, tag):
    n = len(stacks)

    def body(*refs):
        ins = refs[:n]
        send_ref, recv_ref = refs[n], refs[n + 1]
        _, _, c, j, chips = _place()
        for a in range(n):
            rows = stacks[a].shape[1]
            mine = _gather_piece(ins[a], rows, split[a], j, c)
            for t, (px, py) in enumerate(chips):
                theirs = _gather_piece(ins[a], rows, split[a], 2 * px + py, c)
                _remote(mine, mine, send_ref.at[3 * a + t], recv_ref.at[3 * a + t], (px, py, c)).wait_send()
                _remote(theirs, theirs, send_ref.at[3 * a + t], recv_ref.at[3 * a + t], (px, py, c)).wait_recv()

    return pl.pallas_call(
        body, name=f"gather_wait_{tag}", in_specs=[HBM] * n + [SEM, SEM, ANY], out_specs=[HBM] * n,
        out_shape=[pltpu.HBM(s.shape, s.dtype) for s in stacks],
        input_output_aliases={a: a for a in range(n)}, compiler_params=_in_flight_params(),
    )(*stacks, send, recv, after)


def _gather_forward(stacks, split, tag):
    idx = [a for a in range(len(stacks)) if split[a]]
    n = len(idx)

    def body(*refs):
        outs = refs[n:2 * n]
        send, recv = refs[2 * n:]
        x, y, c, _, chips = _place()
        sends = []
        for t, (px, py) in enumerate(chips):
            for a in range(n):
                blk = _half(outs[a].at[2 * px + py], c, stacks[idx[a]].shape[1])
                cp = _remote(blk, blk, send.at[a, t], recv.at[a, t], (x, y, 1 - c))
                cp.start()
                sends.append(cp)
        for t, (px, py) in enumerate(chips):
            for a in range(n):
                blk = _half(outs[a].at[2 * px + py], 1 - c, stacks[idx[a]].shape[1])
                _remote(blk, blk, send.at[a, t], recv.at[a, t], (x, y, 1 - c)).wait_recv()
        for cp in sends:
            cp.wait_send()

    sem = pltpu.SemaphoreType.DMA
    res = pl.pallas_call(
        body, name=f"gather_forward_{tag}", in_specs=[ANY] * n, out_specs=[ANY] * n,
        out_shape=[jax.ShapeDtypeStruct(stacks[a].shape, stacks[a].dtype) for a in idx],
        input_output_aliases={a: a for a in range(n)}, scratch_shapes=[sem((n, 3)), sem((n, 3))],
    )(*[stacks[a] for a in idx])
    out = list(stacks)
    for a, r in zip(idx, res):
        out[a] = r
    return out


def _forward_start(stacks, after, tag):
    n = len(stacks)

    def body(*refs):
        ins = refs[:n]
        send, recv = refs[n + 1], refs[n + 2]
        token = refs[2 * n + 3]
        x, y, c, _, chips = _place()
        for a in range(n):
            for t, (px, py) in enumerate(chips):
                blk = _half(ins[a].at[2 * px + py], c, stacks[a].shape[1])
                _remote(blk, blk, send.at[3 * a + t], recv.at[3 * a + t], (x, y, 1 - c)).start()
        token[...] = jnp.zeros_like(token)

    sems = pltpu.SemaphoreType.DMA((3 * n,))
    res = pl.pallas_call(
        body, name=f"forward_start_{tag}", in_specs=[HBM] * n + [ANY],
        out_specs=[SEM, SEM] + [HBM] * n + [pl.BlockSpec(memory_space=pltpu.VMEM)],
        out_shape=[sems, sems] + [pltpu.HBM(s.shape, s.dtype) for s in stacks] + [TOKEN],
        input_output_aliases={a: a + 2 for a in range(n)}, compiler_params=_in_flight_params(),
    )(*[_in_hbm(s) for s in stacks], after)
    return res[0], res[1], res[2:2 + n], res[2 + n]


def _forward_wait(send, recv, stacks, after, tag):
    n = len(stacks)

    def body(*refs):
        ins = refs[:n]
        send_ref, recv_ref = refs[n], refs[n + 1]
        x, y, c, _, chips = _place()
        for a in range(n):
            for t, (px, py) in enumerate(chips):
                mine = _half(ins[a].at[2 * px + py], c, stacks[a].shape[1])
                theirs = _half(ins[a].at[2 * px + py], 1 - c, stacks[a].shape[1])
                _remote(mine, mine, send_ref.at[3 * a + t], recv_ref.at[3 * a + t], (x, y, 1 - c)).wait_send()
                _remote(theirs, theirs, send_ref.at[3 * a + t], recv_ref.at[3 * a + t], (x, y, 1 - c)).wait_recv()

    return pl.pallas_call(
        body, name=f"forward_wait_{tag}", in_specs=[HBM] * n + [SEM, SEM, ANY], out_specs=[HBM] * n,
        out_shape=[pltpu.HBM(s.shape, s.dtype) for s in stacks],
        input_output_aliases={a: a for a in range(n)}, compiler_params=_in_flight_params(),
    )(*stacks, send, recv, after)


def _swap_start(grads, tag):
    n = len(grads)

    def body(*refs):
        ins, gots = refs[:n], refs[n:2 * n]
        send, recv = refs[2 * n], refs[2 * n + 1]
        token = refs[4 * n + 2]
        x, y, c, _, _ = _place()
        for a in range(n):
            h = grads[a].shape[1] // 2
            _remote(ins[a].at[:, pl.ds((1 - c) * h, h)], gots[a], send.at[a], recv.at[a], (x, y, 1 - c)).start()
        token[...] = jnp.zeros_like(token)

    sems = pltpu.SemaphoreType.DMA((n,))
    halves = [(g.shape[0], g.shape[1] // 2, g.shape[2]) for g in grads]
    res = pl.pallas_call(
        body, name=f"swap_start_{tag}", in_specs=[HBM] * (2 * n),
        out_specs=[SEM, SEM] + [HBM] * (2 * n) + [pl.BlockSpec(memory_space=pltpu.VMEM)],
        out_shape=[sems, sems] + [pltpu.HBM(g.shape, g.dtype) for g in grads] + [pltpu.HBM(s, F32) for s in halves] + [TOKEN],
        input_output_aliases={a: a + 2 for a in range(2 * n)}, compiler_params=_in_flight_params(),
    )(*[_in_hbm(g) for g in grads], *[_in_hbm(lax.empty(s, F32)) for s in halves])
    return res[0], res[1], res[2:2 + n], res[2 + n:2 + 2 * n], res[2 + 2 * n]


def _swap_wait(send, recv, grads, gots, after, tag):
    n = len(grads)

    def body(*refs):
        ins, lnd = refs[:n], refs[n:2 * n]
        send_ref, recv_ref = refs[2 * n], refs[2 * n + 1]
        x, y, c, _, _ = _place()
        for a in range(n):
            h = grads[a].shape[1] // 2
            cp = _remote(ins[a].at[:, pl.ds((1 - c) * h, h)], lnd[a], send_ref.at[a], recv_ref.at[a], (x, y, 1 - c))
            cp.wait_send()
            cp.wait_recv()

    bufs = [pltpu.HBM(g.shape, g.dtype) for g in grads] + [pltpu.HBM(g.shape, g.dtype) for g in gots]
    res = pl.pallas_call(
        body, name=f"swap_wait_{tag}", in_specs=[HBM] * (2 * n) + [SEM, SEM, ANY], out_specs=[HBM] * (2 * n),
        out_shape=bufs, input_output_aliases={a: a for a in range(2 * n)}, compiler_params=_in_flight_params(),
    )(*grads, *gots, send, recv, after)
    return res[:n], res[n:]


def _exchange_start(parts, tag):
    n = len(parts)

    def body(*refs):
        ins, lands = refs[:n], refs[n:2 * n]
        send, recv = refs[2 * n], refs[2 * n + 1]
        token = refs[4 * n + 2]
        _, _, c, j, chips = _place()
        for t, (px, py) in enumerate(chips):
            for a in range(n):
                _remote(ins[a].at[2 * px + py], lands[a].at[j], send.at[3 * a + t], recv.at[3 * a + t], (px, py, c)).start()
        token[...] = jnp.zeros_like(token)

    sems = pltpu.SemaphoreType.DMA((3 * n,))
    bufs = [pltpu.HBM(p.shape, p.dtype) for p in parts]
    res = pl.pallas_call(
        body, name=f"exchange_start_{tag}", in_specs=[HBM] * (2 * n),
        out_specs=[SEM, SEM] + [HBM] * (2 * n) + [pl.BlockSpec(memory_space=pltpu.VMEM)],
        out_shape=[sems, sems] + bufs + bufs + [TOKEN],
        input_output_aliases={a: a + 2 for a in range(2 * n)}, compiler_params=_in_flight_params(),
    )(*[_in_hbm(p) for p in parts], *[_in_hbm(lax.empty(p.shape, p.dtype)) for p in parts])
    return res[0], res[1], res[2:2 + n], res[2 + n:2 + 2 * n], res[2 + 2 * n]


def _exchange_wait(send, recv, parts, lands, after, tag):
    n = len(parts)

    def body(*refs):
        ins, lnd = refs[:n], refs[n:2 * n]
        send_ref, recv_ref = refs[2 * n], refs[2 * n + 1]
        _, _, c, j, chips = _place()
        for t, (px, py) in enumerate(chips):
            jt = 2 * px + py
            for a in range(n):
                _remote(ins[a].at[jt], lnd[a].at[j], send_ref.at[3 * a + t], recv_ref.at[3 * a + t], (px, py, c)).wait_send()
                _remote(ins[a].at[jt], lnd[a].at[jt], send_ref.at[3 * a + t], recv_ref.at[3 * a + t], (px, py, c)).wait_recv()

    bufs = [pltpu.HBM(p.shape, p.dtype) for p in parts]
    res = pl.pallas_call(
        body, name=f"exchange_wait_{tag}", in_specs=[HBM] * (2 * n) + [SEM, SEM, ANY], out_specs=[HBM] * (2 * n),
        out_shape=bufs + bufs, input_output_aliases={a: a for a in range(2 * n)}, compiler_params=_in_flight_params(),
    )(*parts, *lands, send, recv, after)
    return res[:n], res[n:]


def _small_chip_sums(arrs):
    n = len(arrs)

    def body(*refs):
        ins, outs = refs[:n], refs[n:2 * n]
        sib = refs[2 * n:3 * n]
        send, recv = refs[3 * n:]
        x, y, c, j, _ = _place()
        swaps = [_remote(ins[a], sib[a], send.at[a], recv.at[a], (x, y, 1 - c)) for a in range(n)]
        for cp in swaps:
            cp.start()
        for a in range(n):
            swaps[a].wait_recv()
            outs[a][j] = ins[a][...] + sib[a][...]
        for cp in swaps:
            cp.wait_send()

    sem = pltpu.SemaphoreType.DMA
    vm = pl.BlockSpec(memory_space=pltpu.VMEM)
    return pl.pallas_call(
        body, name="small_chip_sums", in_specs=[vm] * n, out_specs=[vm] * n,
        out_shape=[jax.ShapeDtypeStruct((N_SHARD, *a.shape), F32) for a in arrs],
        scratch_shapes=[pltpu.VMEM(a.shape, F32) for a in arrs] + [sem((n,)), sem((n,))],
        compiler_params=_cp(),
    )(*arrs)


def _small_totals(stacks):
    n = len(stacks)

    def body(*refs):
        for a in range(n):
            refs[n + a][...] = ((refs[a][0] + refs[a][1]) + refs[a][2]) + refs[a][3]

    return pl.pallas_call(body, name="small_totals", out_shape=[jax.ShapeDtypeStruct(s.shape[1:], F32) for s in stacks],
                          compiler_params=_cp())(*stacks)


SMALL_1024 = ("ln1_g", "ln1_b", "ln2_g", "ln2_b", "b_ple_gate", "ln3_g", "ln3_b")


def _adamw_small(red3, red1, redz, g_conv_w, redc, red_ws, red_bs, params):
    held_as = {"ln_z_g": (1, D_GMLP), "ln_z_b": (1, D_GMLP), "w_s": (N_HEADS * BLK, BLK), "b_s": (N_HEADS, BLK),
               "conv_w": (3, 1, FF_BLK), "conv_b": (1, D_FF), **{k: (1, D_MODEL) for k in SMALL_1024}}
    names = list(held_as)
    flat = [a.reshape(held_as[k]) for k in names for a in params[k]]

    def body(r3, r1, rz, gcw, rc, rws, rbs, *refs):
        ins, outs = refs[:3 * len(names)], refs[3 * len(names):]

        def grad_of(k):
            if k == "w_s":
                return rws[...]
            if k == "b_s":
                return rbs[...]
            src, row = {"ln3_g": (r3, 0), "ln3_b": (r3, 1), "b_ple_gate": (r3, 2), "ln2_g": (r3, 3), "ln2_b": (r3, 4),
                        "ln1_g": (r1, 0), "ln1_b": (r1, 1), "ln_z_g": (rz, 0), "ln_z_b": (rz, 1)}[k]
            return src[row:row + 1, :]

        for i, k in enumerate(names):
            w_ref, m_ref, v_ref = ins[3 * i:3 * i + 3]
            g_ref, d_ref, nm_ref, nv_ref = outs[4 * i:4 * i + 4]
            if k == "conv_b":
                for j in range(N_SHARD):
                    cols = slice(j * FF_BLK, (j + 1) * FF_BLK)
                    g = rc[j * STAT_ROWS + 3:j * STAT_ROWS + 4, :]
                    g_ref[:, cols] = g
                    d_ref[:, cols], nm_ref[:, cols], nv_ref[:, cols] = _adamw_math(w_ref[:, cols], g, m_ref[:, cols], v_ref[:, cols])
                continue
            if k == "conv_w":
                for tap in range(3):
                    g = gcw[tap:tap + 1, :]
                    g_ref[tap] = g
                    d_ref[tap], nm_ref[tap], nv_ref[tap] = _adamw_math(w_ref[tap], g, m_ref[tap], v_ref[tap])
                continue
            g = grad_of(k)
            g_ref[...] = g
            d_ref[...], nm_ref[...], nv_ref[...] = _adamw_math(w_ref[...], g, m_ref[...], v_ref[...])

    res = pl.pallas_call(
        body, name="adamw_small",
        out_shape=[jax.ShapeDtypeStruct(held_as[k], F32) for k in names for _ in range(4)],
        compiler_params=_cp(),
    )(red3, red1, redz, g_conv_w, redc, red_ws, red_bs, *flat)
    return {k: tuple(r.reshape(params[k][0].shape) for r in res[4 * i:4 * i + 4]) for i, k in enumerate(names)}


WEIGHTS = ("w_in", "ln_z_g", "ln_z_b", "w_s", "b_s", "w_o", "ln1_g", "ln1_b", "w_ff_a", "w_ff_b", "conv_w", "conv_b",
           "w_ff_down", "ln2_g", "ln2_b", "w_ple_gate", "b_ple_gate", "w_ple_in", "ln3_g", "ln3_b")
BIG = ("w_in", "w_o", "w_ff_a", "w_ff_b", "w_ff_down", "w_ple_gate", "w_ple_in")
TRANSPOSED = ("w_ff_a", "w_ff_b")
LATE = ("w_o", "w_ff_a", "w_ff_b", "w_ff_down", "w_ple_gate", "w_ple_in", "conv_w")


def kernel(x, p, positions, w_in, ln_z_g, ln_z_b, w_s, b_s, w_o, ln1_g, ln1_b, w_ff_a, w_ff_b, conv_w, conv_b, w_ff_down, ln2_g, ln2_b, w_ple_gate, b_ple_gate, w_ple_in, ln3_g, ln3_b, loss_target, m_w_in, m_ln_z_g, m_ln_z_b, m_w_s, m_b_s, m_w_o, m_ln1_g, m_ln1_b, m_w_ff_a, m_w_ff_b, m_conv_w, m_conv_b, m_w_ff_down, m_ln2_g, m_ln2_b, m_w_ple_gate, m_b_ple_gate, m_w_ple_in, m_ln3_g, m_ln3_b, v_w_in, v_ln_z_g, v_ln_z_b, v_w_s, v_b_s, v_w_o, v_ln1_g, v_ln1_b, v_w_ff_a, v_w_ff_b, v_conv_w, v_conv_b, v_w_ff_down, v_ln2_g, v_ln2_b, v_w_ple_gate, v_b_ple_gate, v_w_ple_in, v_ln3_g, v_ln3_b):
    args = locals()
    w = {k: args[k] for k in WEIGHTS}
    m = {k: args["m_" + k] for k in WEIGHTS}
    v = {k: args["v_" + k] for k in WEIGHTS}

    for k in TRANSPOSED:
        w[k], m[k], v[k] = (jnp.swapaxes(a, 1, 2) for a in (w[k], m[k], v[k]))

    chip = 2 * lax.axis_index("x") + lax.axis_index("y")
    place = jnp.stack([chip, lax.axis_index("c")]).astype(jnp.int32)
    stack = dict(zip(["w_in"], _place_shards("cast_w_in", [w["w_in"][0]], [MXU], place, place)))
    i_send, i_recv, in_flight, dep = _gather_start([stack["w_in"]], [True], place, "w_in")
    stack.update(zip(LATE, _place_shards("cast_late", [w[k][0] for k in LATE],
                                         [F32 if k == "conv_w" else MXU for k in LATE], place, dep)))
    split_late = [k != "conv_w" for k in LATE]
    g_send, g_recv, late_flight, start_dep = _gather_start([stack[k] for k in LATE], split_late, place, "late")
    rope = _rope_tables(positions, x.shape[1], start_dep)
    landed_in = _gather_wait(i_send, i_recv, in_flight, [True], rope[0], "w_in")
    w_in_full, = _gather_forward(landed_in, [True], "w_in")
    halves =[k for k, sp in zip(LATE, split_late) if sp]
    trips = {}

    def late_landed(after):
        fw = dict(zip(LATE, _gather_wait(g_send, g_recv, late_flight, split_late, after, "late")))
        trips["late"] = (fw, *_forward_start([fw[k] for k in halves], fw["conv_w"], "late"))
        return trips["late"][-1]

    def late_weights(after):
        fw, send, recv, flight, _ = trips["late"]
        fw.update(zip(halves, _forward_wait(send, recv, flight, after, "late")))
        return (fw["w_o"].reshape(D_MODEL, D_MODEL), fw["w_ff_a"], fw["w_ff_b"], fw["conv_w"], fw["w_ff_down"],
                fw["w_ple_gate"].reshape(D_MODEL, D_MODEL), fw["w_ple_in"])

    def swap_started(names, grads, tag):
        stacked = [g.reshape(N_SHARD, *w[k].shape[1:]) for k, g in zip(names, grads)]
        return (names, tag, *_swap_start(stacked, tag))

    def partial_sums(swap, after):
        names, tag, send, recv, stacked, gots, _ = swap
        stacked, got = _swap_wait(send, recv, stacked, gots, after, tag)
        pair = _pair_sums(f"rs_pair_{tag}", stacked, got, place)
        return (names, tag, *_exchange_start(pair, tag))

    def chip_summed(trip, after, dep):
        names, tag, send, recv, pair, lands, _ = trip
        pair, landed = _exchange_wait(send, recv, pair, lands, after, tag)
        return _chip_sums(f"rs_sum_{tag}", pair, landed, place, dep), names, tag

    def reduced(trip, after, dep):
        blocks, names, tag = chip_summed(trip, after, dep)
        return dict(zip(names, _sibling_join(blocks, tag)))

    def early_grads_landed(after):
        blocks, names, tag = chip_summed(trips["early"], after, trips["small"][-1])
        trips["join"] = (names, *_join_start(blocks, after, tag))
        return trips["join"][-1]

    def early_grads(grads):
        trips["swap"] = swap_started(list(grads), list(grads.values()), "early")
        return trips["swap"][-1]

    def early_grads_sent(after, small):
        trips["early"] = partial_sums(trips["swap"], after)
        stat3, stat1, zstat, cstat, dws, dbs = small
        sums = _small_chip_sums([stat3, stat1, zstat, cstat.reshape(N_SHARD * STAT_ROWS, FF_BLK),
                                 dws.reshape(N_HEADS * BLK, BLK), dbs])
        trips["small"] = _gather_start(sums, [False] * len(sums), trips["early"][-1], "small")
        return trips["small"][-1]

    grad_x, g_w_in = _local_step(
        x[0], p[0, 0], rope, loss_target[0], w_in_full, start_dep, late_landed, late_weights, early_grads, early_grads_sent,
        early_grads_landed, ln_z_g, ln_z_b, w_s, b_s, ln1_g, ln1_b, conv_b, ln2_g, ln2_b, b_ple_gate, ln3_g, ln3_b)

    swap_in = swap_started(["w_in"], [g_w_in], "w_in")
    trips["w_in"] = partial_sums(swap_in, swap_in[-1])
    out = {}

    def adamw(red, tag):
        names = list(red)
        steps = _adamw_shards(f"adamw_{tag}", [w[k] for k in names], [red[k] for k in names], [m[k] for k in names],
                              [v[k] for k in names])
        out.update(zip(names, steps))

    names, j_send, j_recv, j_flight, _ = trips["join"]
    adamw(dict(zip(names, _join_wait(j_send, j_recv, j_flight, trips["w_in"][-1], "early"))), "early")
    adamw(reduced(trips["w_in"], out["w_o"][3], start_dep), "w_in")
    for k in TRANSPOSED:
        out[k] = tuple(jnp.swapaxes(a, 1, 2) for a in out[k])

    s_send, s_recv, s_flight, _ = trips["small"]
    red3, red1, redz, redc, red_ws, red_bs = _small_totals(
        _gather_wait(s_send, s_recv, s_flight, [False] * len(s_flight), out["w_in"][3], "small"))
    loss = (0.5 / D_MODEL) * jnp.sum(red3[5])
    g_conv_w = lax.dynamic_slice_in_dim(redc, chip * STAT_ROWS, STAT_ROWS, 0)
    names_small = [k for k in WEIGHTS if k not in BIG]
    out.update(_adamw_small(red3, red1, redz, g_conv_w, redc, red_ws, red_bs, {k: (w[k], m[k], v[k]) for k in names_small}))

    return (loss, grad_x[None], *[out[k][0] for k in WEIGHTS], *[out[k][1] for k in WEIGHTS],
            *[out[k][2] for k in WEIGHTS], *[out[k][3] for k in WEIGHTS])
```

```python
import math

import numpy as np
import jax
import jax.numpy as jnp
from jax import lax
from jax.experimental import pallas as pl
from jax.experimental.pallas import tpu as pltpu

F32 = jnp.float32
BF16 = jnp.bfloat16
MXU = BF16

D_MODEL = 1024
HEAD_DIM = 64
N_HEADS = 8
D_ATTN = 512
D_GMLP = 512
D_IN = 2560
DILATIONS = (1, 4, 16)
BLK = 128
ROPE_THETA = 500000.0
ROPE_DIM = 16
D_FF = 2816
D_PLE = 256
LN_EPS = 1e-5
ALPHA = 2.0 ** 0.25
NEG_INF = -1e30
N_SHARD = 4
W_IN_BLK = D_IN // N_SHARD
FF_BLK = D_FF // N_SHARD
ROW_BLK = D_MODEL // N_SHARD
ADAM_LR, ADAM_B1, ADAM_B2, ADAM_EPS, ADAM_WD, ADAM_STEP = 0.001, 0.9, 0.999, 1e-08, 0.01, 10

TM = 512
HALO = 8
ROW_GROUPS = 2
VMEM_LIMIT = 56 * 1024 * 1024


def _cp(**kw):
    return pltpu.CompilerParams(vmem_limit_bytes=VMEM_LIMIT, **kw)


def _full(shape):
    n = len(shape)
    return pl.BlockSpec(shape, lambda *_: (0,) * n)


def _gelu(x):
    return 0.5 * x * (1.0 + lax.erf(x * (1.0 / math.sqrt(2.0))))


def _gelu_grad(x):
    return 0.5 * (1.0 + lax.erf(x * (1.0 / math.sqrt(2.0)))) + x * jnp.exp(-0.5 * x * x) * (1.0 / math.sqrt(2.0 * math.pi))


def _ln_fwd(r):
    mu = jnp.mean(r, axis=-1, keepdims=True)
    xc = r - mu
    var = jnp.mean(xc * xc, axis=-1, keepdims=True)
    rstd = lax.rsqrt(var + LN_EPS)
    return xc * rstd, rstd


def _ln_bwd(dy, xhat, rstd, g):
    dxh = dy * g
    m1 = jnp.mean(dxh, axis=-1, keepdims=True)
    m2 = jnp.mean(dxh * xhat, axis=-1, keepdims=True)
    return rstd * (dxh - m1 - xhat * m2)


def _dot(a, b):
    return jnp.dot(a.astype(MXU), b.astype(MXU), preferred_element_type=F32)


def _dot_nt(a, b):
    return lax.dot_general(a.astype(MXU), b.astype(MXU), (((1,), (1,)), ((), ())), preferred_element_type=F32)


def _dot_tn(a, b):
    return lax.dot_general(a.astype(MXU), b.astype(MXU), (((0,), (0,)), ((), ())), preferred_element_type=F32)


def _colsum(v):
    return jnp.sum(v, axis=0, keepdims=True)


def _rope_tables(positions, t, dep):
    inv = np.float32(ROPE_THETA) ** (-np.arange(0, ROPE_DIM, 2, dtype=np.float32) / np.float32(ROPE_DIM))
    half = ROPE_DIM // 2
    pos_rep = jnp.repeat(positions.reshape(t // 16, 16), half, axis=1)
    inv_row = jnp.asarray(np.tile(inv, 16)[None, :], F32)

    def trig_body(pos_ref, inv_ref, dep_ref, cos_ref, sin_ref):
        ang = pos_ref[...].astype(F32) * inv_ref[...]
        cos_ref[...] = jnp.cos(ang)
        sin_ref[...] = jnp.sin(ang)

    vm = pl.BlockSpec(memory_space=pltpu.VMEM)
    cos8, sin8 = pl.pallas_call(
        trig_body, name="rope_trig", in_specs=[vm, vm, pl.BlockSpec(memory_space=pl.ANY)], out_specs=[vm, vm],
        out_shape=(jax.ShapeDtypeStruct((t // 16, 128), F32), jax.ShapeDtypeStruct((t // 16, 128), F32)),
    )(pos_rep, inv_row, dep)
    cos8 = cos8.reshape(t, half)
    sin8 = sin8.reshape(t, half)

    lane = np.arange(128) % HEAD_DIM
    sel = (np.arange(half)[:, None] == (lane % half)[None, :])
    e_cos = (sel & (lane < ROPE_DIM)[None, :]).astype(np.float32)
    e_s1 = -(sel & (lane < half)[None, :]).astype(np.float32)
    e_s2 = (sel & ((lane >= half) & (lane < ROPE_DIM))[None, :]).astype(np.float32)
    ones = (lane >= ROPE_DIM).astype(np.float32)[None, :]

    def expand_body(cos_ref, sin_ref, ec_ref, e1_ref, e2_ref, ones_ref, c_ref, s1_ref, s2_ref):
        c_ref[...] = _dot_select(cos_ref[...], ec_ref[...], terms=3) + ones_ref[...]
        s1_ref[...] = _dot_select(sin_ref[...], e1_ref[...], terms=3)
        s2_ref[...] = _dot_select(sin_ref[...], e2_ref[...], terms=3)

    tab = jax.ShapeDtypeStruct((t, 128), F32)
    return pl.pallas_call(expand_body, name="rope_expand", out_shape=(tab, tab, tab), compiler_params=_cp())(
        cos8, sin8, jnp.asarray(e_cos), jnp.asarray(e_s1), jnp.asarray(e_s2), jnp.asarray(ones))


def _tile_heads(tab):
    return jnp.concatenate([tab] * (D_ATTN // 128), axis=1)


def _rope_apply(v, c, s1, s2):
    n = v.shape[1]
    half = ROPE_DIM // 2
    return v * c + pltpu.roll(v, n - half, 1) * s1 + pltpu.roll(v, half, 1) * s2


def _rope_apply_t(g, c, s1, s2):
    n = g.shape[1]
    half = ROPE_DIM // 2
    return g * c + pltpu.roll(g * s1, half, 1) + pltpu.roll(g * s2, n - half, 1)


LANE_CHUNKS = D_ATTN // 128
HEAD_LANES = 128 // N_HEADS


def _perm_shape(t, d, w, dtype):
    return jax.ShapeDtypeStruct((d, t // d, w), dtype)


def _perm_tile(d, w):
    return pl.BlockSpec((None if d == 1 else d, TM // d, w), lambda i: (0, i, 0))


def _to_planes(ref, scr, d, n_chunks, dtype):
    for r in range(d):
        for cc in range(n_chunks):
            ref[r, :, cc * 128:(cc + 1) * 128] = scr.at[cc][pl.ds(r, TM // d, stride=d), :].astype(dtype)


def _from_planes(ref, scr, d, n_chunks, accumulate=False):
    for r in range(d):
        for cc in range(n_chunks):
            rows = scr.at[cc]
            val = ref[r, :, cc * 128:(cc + 1) * 128].astype(F32)
            if accumulate:
                rows[pl.ds(r, TM // d, stride=d), :] += val
            else:
                rows[pl.ds(r, TM // d, stride=d), :] = val


def _chunks(val):
    return [val[:, cc * 128:(cc + 1) * 128] for cc in range(val.shape[1] // 128)]


def _unchunk(scr, n_chunks, base=0):
    return jnp.concatenate([scr[base + cc] for cc in range(n_chunks)], axis=1)


def _head_expand():
    src = np.arange(128)[:, None]
    dst = np.arange(D_ATTN)[None, :]
    return jnp.asarray((src == (dst // HEAD_DIM) * HEAD_LANES).astype(np.float32))


def _head_reduce():
    src = np.arange(D_ATTN)[:, None]
    dst = np.arange(128)[None, :]
    return jnp.asarray((src // HEAD_DIM == dst // HEAD_LANES).astype(np.float32))


def _dot_select(a, sel, terms=2):
    sel = sel.astype(BF16)
    out, rest = None, a
    for _ in range(terms):
        part = rest.astype(BF16)
        rest = rest - part.astype(F32)
        prod = jnp.dot(part, sel, preferred_element_type=F32)
        out = prod if out is None else out + prod
    return out


def _qkvuz(x, w_in, c_tab, s1_tab, s2_tab, ln_z_g, ln_z_b, w_s, b_full, dep):
    t = x.shape[0]
    nchunk = TM // BLK

    def body(x_ref, w_ref, c_ref, s1_ref, s2_ref, g_ref, b_ref, ws_ref, bf_ref, dep_ref,
             qkv1_ref, qkv4_ref, qkv16_ref, hu_ref, hz_ref, mixed_ref, gm_ref, xb_ref, h_scr, wm_scr, p_scr):
        @pl.when(pl.program_id(0) == 0)
        def _():
            row = lax.broadcasted_iota(jnp.int32, (BLK, BLK), 0)
            col = lax.broadcasted_iota(jnp.int32, (BLK, BLK), 1)
            for g in range(N_HEADS):
                wm_scr[g] = jnp.where(col <= row, ws_ref[g], 0.0).astype(MXU)

        xb = x_ref[...].astype(MXU)
        xb_ref[...] = xb
        for j in range(N_SHARD):
            h_scr[:, j * W_IN_BLK:(j + 1) * W_IN_BLK] = jnp.dot(xb, w_ref[j], preferred_element_type=F32)
        c, s1, s2 = _tile_heads(c_ref[...]), _tile_heads(s1_ref[...]), _tile_heads(s2_ref[...])
        q = _rope_apply(h_scr[:, 0:D_ATTN], c, s1, s2) * (1.0 / math.sqrt(HEAD_DIM))
        k = _rope_apply(h_scr[:, D_ATTN:2 * D_ATTN], c, s1, s2)
        for part, val in enumerate((q, k, h_scr[:, 2 * D_ATTN:3 * D_ATTN])):
            qkv1_ref[:, part * D_ATTN:(part + 1) * D_ATTN] = val.astype(MXU)
            for cc in range(LANE_CHUNKS):
                p_scr[part * LANE_CHUNKS + cc] = val[:, cc * 128:(cc + 1) * 128]
        _to_planes(qkv4_ref, p_scr, DILATIONS[1], 3 * LANE_CHUNKS, MXU)
        _to_planes(qkv16_ref, p_scr, DILATIONS[2], 3 * LANE_CHUNKS, MXU)
        hu = h_scr[:, 3 * D_ATTN:3 * D_ATTN + D_GMLP]
        hz = h_scr[:, 3 * D_ATTN + D_GMLP:]
        hu_ref[...] = hu
        hz_ref[...] = hz
        zhat, _ = _ln_fwd(_gelu(hz))
        zn = (zhat * g_ref[...] + b_ref[...]).astype(MXU)
        for ch in range(nchunk):
            rows = slice(ch * BLK, (ch + 1) * BLK)
            for g in range(N_HEADS):
                cols = slice(g * HEAD_DIM, (g + 1) * HEAD_DIM)
                mixed_ref[rows, cols] = jnp.dot(wm_scr[g], zn[rows, cols], preferred_element_type=F32) + bf_ref[:, cols]
        gm_ref[...] = (_gelu(hu) * mixed_ref[...]).astype(MXU)

    tok = lambda w: pl.BlockSpec((TM, w), lambda i: (i, 0))
    outs = [_perm_shape(t, d, 3 * D_ATTN, MXU) for d in DILATIONS] + [jax.ShapeDtypeStruct((t, D_GMLP), F32)] * 3 + [
        jax.ShapeDtypeStruct((t, D_GMLP), MXU), jax.ShapeDtypeStruct((t, D_MODEL), MXU)]
    return pl.pallas_call(
        body, name="qkvuz", grid=(t // TM,),
        in_specs=[tok(D_MODEL), _full(w_in.shape), tok(128), tok(128), tok(128), _full(ln_z_g.shape), _full(ln_z_b.shape),
                  _full(w_s.shape), _full(b_full.shape), pl.BlockSpec(memory_space=pl.ANY)],
        out_specs=[_perm_tile(d, 3 * D_ATTN) for d in DILATIONS] + [tok(D_ATTN)] * 4 + [tok(D_MODEL)], out_shape=outs,
        scratch_shapes=[pltpu.VMEM((TM, D_IN), F32), pltpu.VMEM((N_HEADS, BLK, BLK), MXU),
                        pltpu.VMEM((3 * LANE_CHUNKS, TM, 128), F32)],
        compiler_params=_cp(dimension_semantics=("arbitrary",)),
    )(x, w_in, c_tab, s1_tab, s2_tab, ln_z_g, ln_z_b, w_s, b_full, dep)


def _band_valid(n):
    i = lax.broadcasted_iota(jnp.int32, (BLK, 2 * BLK), 0)
    j = lax.broadcasted_iota(jnp.int32, (BLK, 2 * BLK), 1)
    return (j >= i) & (j <= i + BLK) & ((j >= BLK) | (n > 0))


def _attn_fwd(qkv, d, dep):
    _, l_sub, _ = qkv.shape
    nb = l_sub // BLK

    def body(q_ref, kp_ref, kc_ref, vp_ref, vc_ref, dep_ref, o_ref, l_ref):
        valid = _band_valid(pl.program_id(1))
        kcat = jnp.concatenate([kp_ref[...], kc_ref[...]], axis=0)
        vcat = jnp.concatenate([vp_ref[...], vc_ref[...]], axis=0)
        for h in range(N_HEADS):
            cols = slice(h * HEAD_DIM, (h + 1) * HEAD_DIM)
            s = jnp.where(valid, _dot_nt(q_ref[:, cols], kcat[:, cols]), NEG_INF)
            m = jnp.max(s, axis=-1, keepdims=True)
            e = jnp.exp(s - m)
            den = jnp.sum(e, axis=-1, keepdims=True)
            o_ref[:, cols] = _dot(e, vcat[:, cols]) * (1.0 / den)
            l_ref[:, h * HEAD_LANES:(h + 1) * HEAD_LANES] = jnp.broadcast_to(m + jnp.log(den), (BLK, HEAD_LANES))

    def blk(w, col, prev=False):
        return pl.BlockSpec((None, BLK, w), lambda r, n: (r, jnp.maximum(n - 1, 0) if prev else n, col))

    return pl.pallas_call(
        body, name=f"attn_fwd_d{d}", grid=(d, nb),
        in_specs=[blk(D_ATTN, 0), blk(D_ATTN, 1, True), blk(D_ATTN, 1), blk(D_ATTN, 2, True), blk(D_ATTN, 2),
                  pl.BlockSpec(memory_space=pl.ANY)],
        out_specs=[blk(D_ATTN, 0), blk(128, 0)],
        out_shape=[jax.ShapeDtypeStruct((d, l_sub, D_ATTN), F32), jax.ShapeDtypeStruct((d, l_sub, 128), F32)],
        compiler_params=_cp(dimension_semantics=("arbitrary", "arbitrary")),
    )(qkv, qkv, qkv, qkv, qkv, dep)


def _attn_bwd(qkv, do, lse, delta, d, dep):
    _, l_sub, _ = qkv.shape
    nb = l_sub // BLK
    whole = l_sub <= 8 * BLK

    def shares(n, q_ref, kp_ref, kc_ref, vp_ref, vc_ref, do_ref, l_ref, dl_ref, dq_ref):
        valid = _band_valid(n)
        kcat = jnp.concatenate([kp_ref[...], kc_ref[...]], axis=0)
        vcat = jnp.concatenate([vp_ref[...], vc_ref[...]], axis=0)
        for h in range(N_HEADS):
            cols = slice(h * HEAD_DIM, (h + 1) * HEAD_DIM)
            stat = slice(h * HEAD_LANES, h * HEAD_LANES + 1)
            qh, doh = q_ref[:, cols], do_ref[:, cols]
            p = jnp.where(valid, jnp.exp(_dot_nt(qh, kcat[:, cols]) - l_ref[:, stat]), 0.0)
            ds = p * (_dot_nt(doh, vcat[:, cols]) - dl_ref[:, stat])
            dq_ref[:, cols] = _dot(ds, kcat[:, cols])
            yield cols, _dot_tn(ds, qh), _dot_tn(p, doh)

    def body_whole(*refs):
        dk_ref, dv_ref = refs[10:]
        n = pl.program_id(1)
        cur = pl.ds(pl.multiple_of(n * BLK, BLK), BLK)
        prev = pl.ds(pl.multiple_of(jnp.maximum(n - 1, 0) * BLK, BLK), BLK)
        for cols, dk2, dv2 in shares(n, *refs[:8], refs[9]):
            dk_ref[cur, cols] = dk2[BLK:]
            dv_ref[cur, cols] = dv2[BLK:]
            dk_ref[prev, cols] += dk2[0:BLK]
            dv_ref[prev, cols] += dv2[0:BLK]

    def body_carry(*refs):
        dk_ref, dv_ref, ck_scr, cv_scr = refs[10:]
        n = pl.program_id(1)

        @pl.when(n == 0)
        def _():
            ck_scr[...] = jnp.zeros_like(ck_scr)
            cv_scr[...] = jnp.zeros_like(cv_scr)

        @pl.when(n < nb)
        def _():
            for cols, dk2, dv2 in shares(n, *refs[:8], refs[9]):
                dk_ref[:, cols] = ck_scr[:, cols] + dk2[0:BLK]
                dv_ref[:, cols] = cv_scr[:, cols] + dv2[0:BLK]
                ck_scr[:, cols] = dk2[BLK:]
                cv_scr[:, cols] = dv2[BLK:]

        @pl.when(n == nb)
        def _():
            dk_ref[...] = ck_scr[...]
            dv_ref[...] = cv_scr[...]

    def blk(w, col, shift=0):
        return pl.BlockSpec((None, BLK, w), lambda r, n: (r, jnp.clip(n - shift, 0, nb - 1), col))

    if whole:
        dkv_spec = pl.BlockSpec((None, l_sub, D_ATTN), lambda r, n: (r, 0, 0))
        body, steps, scratch = body_whole, nb, []
    else:
        dkv_spec = blk(D_ATTN, 0, 1)
        body, steps, scratch = body_carry, nb + 1, [pltpu.VMEM((BLK, D_ATTN), F32)] * 2
    return pl.pallas_call(
        body, name=f"attn_bwd_d{d}", grid=(d, steps),
        in_specs=[blk(D_ATTN, 0), blk(D_ATTN, 1, 1), blk(D_ATTN, 1), blk(D_ATTN, 2, 1), blk(D_ATTN, 2),
                  blk(D_ATTN, 0), blk(128, 0), blk(128, 0), pl.BlockSpec(memory_space=pl.ANY)],
        out_specs=[blk(D_ATTN, 0), dkv_spec, dkv_spec],
        out_shape=[jax.ShapeDtypeStruct((d, l_sub, D_ATTN), F32)] * 3,
        scratch_shapes=scratch,
        compiler_params=_cp(dimension_semantics=("arbitrary", "arbitrary")),
    )(qkv, qkv, qkv, qkv, qkv, do, lse, delta, dep)


def _mix_ln1(os_, ls_, gm, x, w_o, ln1_g, ln1_b, dep):
    t = x.shape[0]
    expand = _head_expand()

    def body(o1, o4, o16, l1, l4, l16, gm_ref, x_ref, wo_ref, g_ref, b_ref, ex_ref, dep_ref,
             attn_ref, lse1_ref, lse4_ref, lse16_ref, cat_ref, xhat_ref, rstd_ref, x1b_ref, o_scr, l_scr):
        _from_planes(o4, o_scr, DILATIONS[1], LANE_CHUNKS)
        _from_planes(o16, o_scr.at[pl.ds(LANE_CHUNKS, LANE_CHUNKS)], DILATIONS[2], LANE_CHUNKS)
        _from_planes(l4, l_scr, DILATIONS[1], 1)
        _from_planes(l16, l_scr.at[pl.ds(1, 1)], DILATIONS[2], 1)
        la, lb, lc = l1[...], l_scr[0], l_scr[1]
        m = jnp.maximum(jnp.maximum(la, lb), lc)
        ea, eb, ec = jnp.exp(la - m), jnp.exp(lb - m), jnp.exp(lc - m)
        den = ea + eb + ec
        inv = 1.0 / den
        wide = lambda w: _dot_select(w, ex_ref[...])
        attn = (wide(ea * inv) * o1[...] + wide(eb * inv) * _unchunk(o_scr, LANE_CHUNKS)
                + wide(ec * inv) * _unchunk(o_scr, LANE_CHUNKS, LANE_CHUNKS))
        attn_ref[...] = attn
        lse = m + jnp.log(den)
        lse1_ref[...] = lse
        l_scr[2] = lse
        _to_planes(lse4_ref, l_scr.at[pl.ds(2, 1)], DILATIONS[1], 1, F32)
        _to_planes(lse16_ref, l_scr.at[pl.ds(2, 1)], DILATIONS[2], 1, F32)
        cat_ref[:, 0:D_ATTN] = attn.astype(MXU)
        cat_ref[:, D_ATTN:] = gm_ref[...]
        mix = jnp.dot(cat_ref[...], wo_ref[...], preferred_element_type=F32)
        xhat, rstd = _ln_fwd(ALPHA * x_ref[...] + mix)
        xhat_ref[...] = xhat
        rstd_ref[...] = rstd
        x1b_ref[...] = (xhat * g_ref[...] + b_ref[...]).astype(MXU)

    tok = lambda w: pl.BlockSpec((TM, w), lambda i: (i, 0))
    outs = [jax.ShapeDtypeStruct((t, D_ATTN), F32)] + [_perm_shape(t, d, 128, F32) for d in DILATIONS] + [
        jax.ShapeDtypeStruct((t, D_MODEL), MXU), jax.ShapeDtypeStruct((t, D_MODEL), F32), jax.ShapeDtypeStruct((t, 1), F32),
        jax.ShapeDtypeStruct((t, D_MODEL), MXU)]
    return pl.pallas_call(
        body, name="mix_ln1", grid=(t // TM,),
        in_specs=[_perm_tile(d, D_ATTN) for d in DILATIONS] + [_perm_tile(d, 128) for d in DILATIONS]
        + [tok(D_GMLP), tok(D_MODEL), _full(w_o.shape), _full(ln1_g.shape), _full(ln1_b.shape), _full(expand.shape),
           pl.BlockSpec(memory_space=pl.ANY)],
        out_specs=[tok(D_ATTN)] + [_perm_tile(d, 128) for d in DILATIONS] + [tok(D_MODEL), tok(D_MODEL), tok(1), tok(D_MODEL)],
        out_shape=outs,
        scratch_shapes=[pltpu.VMEM((2 * LANE_CHUNKS, TM, 128), F32), pltpu.VMEM((3, TM, 128), F32)],
        compiler_params=_cp(dimension_semantics=("arbitrary",)),
    )(*os_, *ls_, gm, x, w_o, ln1_g, ln1_b, expand, dep)


def _conv_fwd(a_ext, w_ref, b_ref, rows):
    back = [pltpu.roll(a_ext, s, 0)[HALO:HALO + rows] for s in (1, 2)]
    return b_ref[...] + w_ref[2:3, :] * a_ext[HALO:HALO + rows] + w_ref[1:2, :] * back[0] + w_ref[0:1, :] * back[1]


def _ffn_in(x1b, w_a, w_b, conv_w, conv_b):
    t = x1b.shape[0]
    hb = TM // HALO

    def body(x_ref, xh_ref, wa_ref, wb_ref, cw_ref, cb_ref, apre_ref, act_ref, gate_ref, f_ref):
        i = pl.program_id(1)
        a_pre = _dot_nt(x_ref[...], wa_ref[...])
        a_halo = jnp.where(i > 0, _dot_nt(xh_ref[...], wa_ref[...]), 0.0)
        a = _conv_fwd(jnp.concatenate([a_halo, a_pre], axis=0), cw_ref, cb_ref, TM)
        b = _dot_nt(x_ref[...], wb_ref[...])
        cdf = 0.5 * (1.0 + lax.erf(a * (1.0 / math.sqrt(2.0))))
        pdf = jnp.exp(-0.5 * a * a) * (1.0 / math.sqrt(2.0 * math.pi))
        act = a * cdf
        apre_ref[...] = a_pre
        act_ref[...] = act
        gate_ref[...] = b * (cdf + a * pdf)
        f_ref[...] = (act * b).astype(MXU)

    blk = lambda r, c: pl.BlockSpec((None, r, c), lambda j, i: (j, 0, 0))
    tokj = pl.BlockSpec((None, TM, FF_BLK), lambda j, i: (j, i, 0))
    outs = [jax.ShapeDtypeStruct((N_SHARD, t, FF_BLK), F32)] * 3 + [jax.ShapeDtypeStruct((N_SHARD, t, FF_BLK), MXU)]
    return pl.pallas_call(
        body, name="ffn_in", grid=(N_SHARD, t // TM),
        in_specs=[pl.BlockSpec((TM, D_MODEL), lambda j, i: (i, 0)),
                  pl.BlockSpec((HALO, D_MODEL), lambda j, i: (jnp.maximum(i * hb - 1, 0), 0)),
                  blk(FF_BLK, D_MODEL), blk(FF_BLK, D_MODEL), blk(3, FF_BLK), blk(1, FF_BLK)],
        out_specs=[tokj, tokj, tokj, tokj], out_shape=outs,
        compiler_params=_cp(dimension_semantics=("arbitrary", "arbitrary")),
    )(x1b, x1b, w_a, w_b, conv_w, conv_b)


def _ffn_out_ln2(f, w_down, xhat1, ln1_g, ln1_b):
    t = xhat1.shape[0]

    def body(f_ref, wd_ref, xh_ref, g1_ref, b1_ref, xhat_ref, rstd_ref):
        half = TM // ROW_GROUPS
        for r0 in range(0, TM, half):
            rows = pl.ds(r0, half)
            ff = jnp.dot(f_ref[0, rows, :], wd_ref[0], preferred_element_type=F32)
            for j in range(1, N_SHARD):
                ff = ff + jnp.dot(f_ref[j, rows, :], wd_ref[j], preferred_element_type=F32)
            x1 = xh_ref[rows, :] * g1_ref[...] + b1_ref[...]
            xhat, rstd = _ln_fwd(ALPHA * x1 + ff)
            xhat_ref[rows, :] = xhat
            rstd_ref[rows, :] = rstd

    tok = lambda w: pl.BlockSpec((TM, w), lambda i: (i, 0))
    vec = _full((1, D_MODEL))
    outs = [jax.ShapeDtypeStruct((t, D_MODEL), F32), jax.ShapeDtypeStruct((t, 1), F32)]
    return pl.pallas_call(
        body, name="ffn_out_ln2", grid=(t // TM,),
        in_specs=[pl.BlockSpec((N_SHARD, TM, FF_BLK), lambda i: (0, i, 0)), _full(w_down.shape), tok(D_MODEL), vec, vec],
        out_specs=[tok(D_MODEL), tok(1)], out_shape=outs,
        compiler_params=_cp(dimension_semantics=("arbitrary",)),
    )(f, w_down, xhat1, ln1_g, ln1_b)


STAT_ROWS = 8


def _ple_loss_bwd(xhat2, rstd2, p, target, ln2_g, ln2_b, w_g, b_g, w_p, ln3_g, ln3_b):
    t = xhat2.shape[0]

    def body(xh2_ref, rs2_ref, p_ref, t_ref, g2_ref, b2_ref, wg_ref, bg_ref, wp_ref, g3_ref, b3_ref,
             dr2_ref, dr2b_ref, stat_ref, dwg_ref, dwp_ref, pp_scr, dwp_scr):
        @pl.when(pl.program_id(0) == 0)
        def _():
            stat_ref[...] = jnp.zeros_like(stat_ref)
            dwg_ref[...] = jnp.zeros_like(dwg_ref)
            dwp_scr[...] = jnp.zeros_like(dwp_scr)

        xhat2 = xh2_ref[...]
        x2 = xhat2 * g2_ref[...] + b2_ref[...]
        x2b = x2.astype(MXU)
        gate = jax.nn.sigmoid(jnp.dot(x2b, wg_ref[...], preferred_element_type=F32) + bg_ref[...])
        pb = p_ref[...].astype(MXU)
        for j in range(N_SHARD):
            pp_scr[:, j * ROW_BLK:(j + 1) * ROW_BLK] = jnp.dot(pb, wp_ref[j], preferred_element_type=F32)
        pp = pp_scr[...]
        xhat3, rstd3 = _ln_fwd(ALPHA * x2 + gate * pp)
        err = xhat3 * g3_ref[...] + b3_ref[...] - t_ref[...]
        dy = err * (1.0 / D_MODEL)
        dr3 = _ln_bwd(dy, xhat3, rstd3, g3_ref[...])
        dgp = dr3 * pp * gate * (1.0 - gate)
        dgp_b = dgp.astype(MXU)
        dwg_ref[...] += _dot_tn(x2b, dgp_b)
        dwp_scr[...] += _dot_tn(pb, dr3 * gate)
        dx2 = ALPHA * dr3 + _dot_nt(dgp_b, wg_ref[...])
        dr2 = _ln_bwd(dx2, xhat2, rs2_ref[...], g2_ref[...])
        dr2_ref[...] = dr2
        dr2b_ref[...] = dr2.astype(MXU)
        stat_ref[0:1, :] += _colsum(dy * xhat3)
        stat_ref[1:2, :] += _colsum(dy)
        stat_ref[2:3, :] += _colsum(dgp)
        stat_ref[3:4, :] += _colsum(dx2 * xhat2)
        stat_ref[4:5, :] += _colsum(dx2)
        stat_ref[5:6, :] += _colsum(err * err)

        @pl.when(pl.program_id(0) == t // TM - 1)
        def _():
            for j in range(N_SHARD):
                dwp_ref[j] = dwp_scr[:, j * ROW_BLK:(j + 1) * ROW_BLK]

    tok = lambda w: pl.BlockSpec((TM, w), lambda i: (i, 0))
    vec = _full((1, D_MODEL))
    outs = [jax.ShapeDtypeStruct((t, D_MODEL), F32), jax.ShapeDtypeStruct((t, D_MODEL), MXU),
            jax.ShapeDtypeStruct((STAT_ROWS, D_MODEL), F32), jax.ShapeDtypeStruct((D_MODEL, D_MODEL), F32),
            jax.ShapeDtypeStruct((N_SHARD, D_PLE, ROW_BLK), F32)]
    return pl.pallas_call(
        body, name="ple_loss_bwd", grid=(t // TM,),
        in_specs=[tok(D_MODEL), tok(1), tok(D_PLE), tok(D_MODEL), vec, vec, _full(w_g.shape), vec, _full(w_p.shape), vec, vec],
        out_specs=[tok(D_MODEL), tok(D_MODEL), _full((STAT_ROWS, D_MODEL)), _full((D_MODEL, D_MODEL)),
                   _full((N_SHARD, D_PLE, ROW_BLK))], out_shape=outs,
        scratch_shapes=[pltpu.VMEM((TM, D_MODEL), F32), pltpu.VMEM((D_PLE, D_MODEL), F32)],
        compiler_params=_cp(dimension_semantics=("arbitrary",)),
    )(xhat2, rstd2, p, target, ln2_g, ln2_b, w_g, b_g, w_p, ln3_g, ln3_b)


def _ffn_bwd(dr2, dr2b, a_pre, act, gate, w_down, w_a, w_b, conv_w, xhat1, rstd1, ln1_g, cat):
    t = dr2.shape[0]
    nt = t // TM
    hb = TM // HALO
    last_h = t // HALO - 1
    halo2 = 2 * HALO

    def body(dr_ref, drb_ref, drbn_ref, ap_ref, act_ref, gate_ref, gaten_ref, wd_ref, wa_ref, wb_ref, cw_ref,
             xh_ref, rs_ref, g1_ref, cat_ref, dap_ref, dbb_ref, dr1_ref, cstat_ref, lstat_ref, dwo_ref, acc_scr):
        i, j = pl.program_id(0), pl.program_id(1)

        @pl.when((i == 0) & (j == 0))
        def _():
            cstat_ref[...] = jnp.zeros_like(cstat_ref)
            lstat_ref[...] = jnp.zeros_like(lstat_ref)
            dwo_ref[...] = jnp.zeros_like(dwo_ref)

        half = TM // ROW_GROUPS
        parts = []
        for r0 in range(0, TM, half):
            rows = pl.ds(r0, half)
            last = r0 + half == TM

            def ext(ref, nxt):
                return jnp.concatenate([ref[rows], nxt[...]], axis=0) if last else ref[r0:r0 + half + HALO]

            drb = jnp.concatenate([drb_ref[rows, :], drbn_ref[...]], axis=0) if last else drb_ref[r0:r0 + half + halo2, :]
            df = _dot_nt(drb, wd_ref[...])[0:half + HALO]
            da = df * ext(gate_ref, gaten_ref)
            if last:
                da = jnp.concatenate([da[0:half], jnp.where(i < nt - 1, da[half:], 0.0)], axis=0)
            ahead = [da[0:half]] + [pltpu.roll(da, half + HALO - s, 0)[0:half] for s in (1, 2)]
            da_pre = cw_ref[2:3, :] * ahead[0] + cw_ref[1:2, :] * ahead[1] + cw_ref[0:1, :] * ahead[2]
            dbb = df[0:half] * act_ref[rows, :]
            dap_ref[rows, :] = da_pre.astype(MXU)
            dbb_ref[rows, :] = dbb.astype(MXU)
            for kk in range(3):
                cstat_ref[j, kk:kk + 1, :] += _colsum(ahead[2 - kk] * ap_ref[rows, :])
            cstat_ref[j, 3:4, :] += _colsum(ahead[0])
            parts.append(_dot(da_pre, wa_ref[...]) + _dot(dbb, wb_ref[...]))
        part = jnp.concatenate(parts, axis=0)

        @pl.when(j == 0)
        def _():
            acc_scr[...] = ALPHA * dr_ref[...] + part

        @pl.when(j > 0)
        def _():
            acc_scr[...] += part

        @pl.when(j == N_SHARD - 1)
        def _():
            dx1 = acc_scr[...]
            xhat1 = xh_ref[...]
            lstat_ref[0:1, :] += _colsum(dx1 * xhat1)
            lstat_ref[1:2, :] += _colsum(dx1)
            dr1 = _ln_bwd(dx1, xhat1, rs_ref[...], g1_ref[...])
            dr1_ref[...] = dr1
            dwo_ref[...] += _dot_tn(cat_ref[...], dr1)

    tok = lambda w: pl.BlockSpec((TM, w), lambda i, j: (i, 0))
    tokj = pl.BlockSpec((None, TM, FF_BLK), lambda i, j: (j, i, 0))
    nextj = pl.BlockSpec((None, HALO, FF_BLK), lambda i, j: (j, jnp.minimum((i + 1) * hb, last_h), 0))
    blk = lambda r, c: pl.BlockSpec((None, r, c), lambda i, j: (j, 0, 0))
    outs = [jax.ShapeDtypeStruct((N_SHARD, t, FF_BLK), MXU)] * 2 + [
        jax.ShapeDtypeStruct((t, D_MODEL), F32), jax.ShapeDtypeStruct((N_SHARD, STAT_ROWS, FF_BLK), F32),
        jax.ShapeDtypeStruct((STAT_ROWS, D_MODEL), F32), jax.ShapeDtypeStruct((D_MODEL, D_MODEL), F32)]
    return pl.pallas_call(
        body, name="ffn_bwd", grid=(nt, N_SHARD),
        in_specs=[tok(D_MODEL), tok(D_MODEL),
                  pl.BlockSpec((halo2, D_MODEL), lambda i, j: (jnp.minimum((i + 1) * (hb // 2), last_h // 2), 0)),
                  tokj, tokj, tokj, nextj, blk(FF_BLK, D_MODEL), blk(FF_BLK, D_MODEL), blk(FF_BLK, D_MODEL),
                  blk(3, FF_BLK), tok(D_MODEL), tok(1), _full((1, D_MODEL)), tok(D_MODEL)],
        out_specs=[tokj, tokj, tok(D_MODEL), _full((N_SHARD, STAT_ROWS, FF_BLK)), _full((STAT_ROWS, D_MODEL)),
                   _full((D_MODEL, D_MODEL))], out_shape=outs,
        scratch_shapes=[pltpu.VMEM((TM, D_MODEL), F32)],
        compiler_params=_cp(dimension_semantics=("arbitrary", "arbitrary")),
    )(dr2, dr2b, dr2b, a_pre, act, gate, gate, w_down, w_a, w_b, conv_w, xhat1, rstd1, ln1_g, cat)


def _mix_bwd(dr1, w_o, hu, hz, mixed, attn, ln_z_g, ln_z_b, w_s, dep):
    t = dr1.shape[0]
    nchunk = TM // BLK

    def body(dr_ref, wo_ref, hu_ref, hz_ref, mx_ref, attn_ref, g_ref, b_ref, ws_ref, grp_ref, red_ref, dep_ref,
             do1_ref, do4_ref, do16_ref, dl1_ref, dl4_ref, dl16_ref, duz_ref, dws_ref, dbs_ref, zstat_ref,
             wm_scr, dzn_scr, dbsum_scr, do_scr, dl_scr):
        @pl.when(pl.program_id(0) == 0)
        def _():
            row = lax.broadcasted_iota(jnp.int32, (BLK, BLK), 0)
            col = lax.broadcasted_iota(jnp.int32, (BLK, BLK), 1)
            for g in range(N_HEADS):
                wm_scr[g] = jnp.where(col <= row, ws_ref[g], 0.0).astype(MXU)
            dws_ref[...] = jnp.zeros_like(dws_ref)
            dbsum_scr[...] = jnp.zeros_like(dbsum_scr)
            zstat_ref[...] = jnp.zeros_like(zstat_ref)

        dcat = _dot_nt(dr_ref[...], wo_ref[...])
        dattn = dcat[:, 0:D_ATTN]
        do1_ref[...] = dattn.astype(MXU)
        for cc, val in enumerate(_chunks(dattn)):
            do_scr[cc] = val
        _to_planes(do4_ref, do_scr, DILATIONS[1], LANE_CHUNKS, MXU)
        _to_planes(do16_ref, do_scr, DILATIONS[2], LANE_CHUNKS, MXU)
        delta = _dot_select(dattn * attn_ref[...], red_ref[...])
        dl1_ref[...] = delta
        dl_scr[0] = delta
        _to_planes(dl4_ref, dl_scr, DILATIONS[1], 1, F32)
        _to_planes(dl16_ref, dl_scr, DILATIONS[2], 1, F32)
        dgm = dcat[:, D_ATTN:]
        hu, hz = hu_ref[...], hz_ref[...]
        u = _gelu(hu)
        duz_ref[:, 0:D_GMLP] = (dgm * mx_ref[...] * _gelu_grad(hu)).astype(MXU)
        dmixed = dgm * u
        dmb = dmixed.astype(MXU)
        zhat, rstd = _ln_fwd(_gelu(hz))
        znb = (zhat * g_ref[...] + b_ref[...]).astype(MXU)
        dbs_acc = jnp.zeros((BLK, D_GMLP), F32)
        for ch in range(nchunk):
            rows = slice(ch * BLK, (ch + 1) * BLK)
            dbs_acc = dbs_acc + dmixed[rows]
            for g in range(N_HEADS):
                cols = slice(g * HEAD_DIM, (g + 1) * HEAD_DIM)
                dzn_scr[rows, cols] = _dot_tn(wm_scr[g], dmb[rows, cols])
                dws_ref[g] += _dot_nt(dmb[rows, cols], znb[rows, cols])
        dbsum_scr[...] += dbs_acc
        dzn = dzn_scr[...]
        zstat_ref[0:1, :] += _colsum(dzn * zhat)
        zstat_ref[1:2, :] += _colsum(dzn)
        duz_ref[:, D_GMLP:] = (_ln_bwd(dzn, zhat, rstd, g_ref[...]) * _gelu_grad(hz)).astype(MXU)

        @pl.when(pl.program_id(0) == nt - 1)
        def _():
            row = lax.broadcasted_iota(jnp.int32, (BLK, BLK), 0)
            col = lax.broadcasted_iota(jnp.int32, (BLK, BLK), 1)
            for g in range(N_HEADS):
                dws_ref[g] = jnp.where(col <= row, dws_ref[g], 0.0)
            dbs_ref[...] = lax.dot_general(grp_ref[...], dbsum_scr[...], (((1,), (1,)), ((), ())),
                                           precision=lax.Precision.HIGHEST, preferred_element_type=F32)

    nt = t // TM
    tok = lambda w: pl.BlockSpec((TM, w), lambda i: (i, 0))
    grp = jnp.asarray((np.arange(D_GMLP)[None, :] // HEAD_DIM == np.arange(N_HEADS)[:, None]).astype(np.float32))
    red = _head_reduce()
    outs = [_perm_shape(t, d, D_ATTN, MXU) for d in DILATIONS] + [_perm_shape(t, d, 128, F32) for d in DILATIONS] + [
        jax.ShapeDtypeStruct((t, 2 * D_GMLP), MXU),
        jax.ShapeDtypeStruct((N_HEADS, BLK, BLK), F32), jax.ShapeDtypeStruct((N_HEADS, BLK), F32),
        jax.ShapeDtypeStruct((STAT_ROWS, D_GMLP), F32)]
    return pl.pallas_call(
        body, name="mix_bwd", grid=(t // TM,),
        in_specs=[tok(D_MODEL), _full(w_o.shape), tok(D_GMLP), tok(D_GMLP), tok(D_GMLP), tok(D_ATTN), _full(ln_z_g.shape),
                  _full(ln_z_b.shape), _full(w_s.shape), _full(grp.shape), _full(red.shape), pl.BlockSpec(memory_space=pl.ANY)],
        out_specs=[_perm_tile(d, D_ATTN) for d in DILATIONS] + [_perm_tile(d, 128) for d in DILATIONS]
        + [tok(2 * D_GMLP), _full((N_HEADS, BLK, BLK)), _full((N_HEADS, BLK)), _full((STAT_ROWS, D_GMLP))],
        out_shape=outs,
        scratch_shapes=[pltpu.VMEM((N_HEADS, BLK, BLK), MXU), pltpu.VMEM((TM, D_GMLP), F32), pltpu.VMEM((BLK, D_GMLP), F32),
                        pltpu.VMEM((LANE_CHUNKS, TM, 128), F32), pltpu.VMEM((1, TM, 128), F32)],
        compiler_params=_cp(dimension_semantics=("arbitrary",)),
    )(dr1, w_o, hu, hz, mixed, attn, ln_z_g, ln_z_b, w_s, grp, red, dep)


def _dx_in(dqs, dks, dvs, duz, dr1, w_in, c_tab, s1_tab, s2_tab):
    t = dr1.shape[0]

    def body(dq1, dq4, dq16, dk1, dk4, dk16, dv1, dv4, dv16, duz_ref, dr_ref, w_ref, c_ref, s1_ref, s2_ref,
             dh_ref, dx_ref, acc_scr):
        sums = []
        for part, (g1, g4, g16) in enumerate(((dq1, dq4, dq16), (dk1, dk4, dk16), (dv1, dv4, dv16))):
            acc = acc_scr.at[pl.ds(part * LANE_CHUNKS, LANE_CHUNKS)]
            for cc in range(LANE_CHUNKS):
                acc[cc] = g1[:, cc * 128:(cc + 1) * 128]
            _from_planes(g4, acc, DILATIONS[1], LANE_CHUNKS, accumulate=True)
            _from_planes(g16, acc, DILATIONS[2], LANE_CHUNKS, accumulate=True)
            sums.append(_unchunk(acc_scr, LANE_CHUNKS, part * LANE_CHUNKS))
        c, s1, s2 = _tile_heads(c_ref[...]), _tile_heads(s1_ref[...]), _tile_heads(s2_ref[...])
        dh_ref[:, 0:D_ATTN] = _rope_apply_t(sums[0] * (1.0 / math.sqrt(HEAD_DIM)), c, s1, s2).astype(MXU)
        dh_ref[:, D_ATTN:2 * D_ATTN] = _rope_apply_t(sums[1], c, s1, s2).astype(MXU)
        dh_ref[:, 2 * D_ATTN:3 * D_ATTN] = sums[2].astype(MXU)
        dh_ref[:, 3 * D_ATTN:] = duz_ref[...]
        dx = ALPHA * dr_ref[...]
        for j in range(N_SHARD):
            dx = dx + _dot_nt(dh_ref[:, j * W_IN_BLK:(j + 1) * W_IN_BLK], w_ref[j])
        dx_ref[...] = dx

    tok = lambda w: pl.BlockSpec((TM, w), lambda i: (i, 0))
    outs = [jax.ShapeDtypeStruct((t, D_IN), MXU), jax.ShapeDtypeStruct((t, D_MODEL), F32)]
    return pl.pallas_call(
        body, name="dx_in", grid=(t // TM,),
        in_specs=[_perm_tile(d, D_ATTN) for d in DILATIONS] * 3
        + [tok(2 * D_GMLP), tok(D_MODEL), _full(w_in.shape), tok(128), tok(128), tok(128)],
        out_specs=[tok(D_IN), tok(D_MODEL)], out_shape=outs,
        scratch_shapes=[pltpu.VMEM((3 * LANE_CHUNKS, TM, 128), F32)],
        compiler_params=_cp(dimension_semantics=("arbitrary",)),
    )(*dqs, *dks, *dvs, duz, dr1, w_in, c_tab, s1_tab, s2_tab)


def _wgrad(name, x, dy, x_spec, dy_spec, out_spec, out_shape, grid, dep=None):
    deps = [] if dep is None else [dep]

    def body(x_ref, dy_ref, *rest):
        rest[-1][...] = _dot_tn(x_ref[...], dy_ref[...])

    return pl.pallas_call(
        body, name=name, grid=grid, in_specs=[x_spec, dy_spec] + [pl.BlockSpec(memory_space=pl.ANY)] * len(deps),
        out_specs=out_spec, out_shape=jax.ShapeDtypeStruct(out_shape, F32),
        compiler_params=_cp(dimension_semantics=("arbitrary",) * len(grid)),
    )(x, dy, *deps)


def _wgrad_pair(name, xa, xb, dy, x_spec, dy_spec, out_spec, out_shape, grid):
    def body(xa_ref, xb_ref, dy_ref, oa_ref, ob_ref):
        dy = dy_ref[...]
        oa_ref[...] = _dot_tn(xa_ref[...], dy)
        ob_ref[...] = _dot_tn(xb_ref[...], dy)

    return pl.pallas_call(
        body, name=name, grid=grid, in_specs=[x_spec, x_spec, dy_spec], out_specs=[out_spec, out_spec],
        out_shape=[jax.ShapeDtypeStruct(out_shape, F32)] * 2,
        compiler_params=_cp(dimension_semantics=("arbitrary",) * len(grid)),
    )(xa, xb, dy)


def _local_step(x, p, rope, target, w_in, start_dep, late_landed, late_weights, early_grads, early_grads_sent,
                early_grads_landed,
                ln_z_g, ln_z_b, w_s, b_s, ln1_g, ln1_b, conv_b, ln2_g, ln2_b, b_g, ln3_g, ln3_b):
    t = x.shape[0]
    half = TM
    c_tab, s1_tab, s2_tab = rope
    b_full = jnp.repeat(jnp.transpose(b_s[0]), HEAD_DIM, axis=1)
    conv_b4 = conv_b.reshape(N_SHARD, 1, FF_BLK)
    *qkvs, hu, hz, mixed, gm, xb = _qkvuz(x, w_in, c_tab, s1_tab, s2_tab, ln_z_g, ln_z_b, w_s[0], b_full, start_dep)
    branches = [_attn_fwd(qkv, d, start_dep) for qkv, d in zip(qkvs[:2], DILATIONS[:2])]
    dep = late_landed(branches[-1][1])
    branches.append(_attn_fwd(qkvs[2], DILATIONS[2], dep))
    w_o, w_a, w_b, conv_w, w_down, w_g, w_p = late_weights(branches[-1][1])
    attn, *lses, cat, xhat1, rstd1, x1b = _mix_ln1(
        [o for o, _ in branches], [l for _, l in branches], gm, x, w_o, ln1_g, ln1_b, dep)
    a_pre, act, gate, f = _ffn_in(x1b, w_a, w_b, conv_w, conv_b4)
    xhat2, rstd2 = _ffn_out_ln2(f, w_down, xhat1, ln1_g, ln1_b)
    dr2, dr2b, stat3, g_w_g, g_w_p = _ple_loss_bwd(xhat2, rstd2, p, target, ln2_g, ln2_b, w_g, b_g, w_p, ln3_g, ln3_b)
    da_pre, dbb, dr1, cstat, stat1, g_w_o = _ffn_bwd(dr2, dr2b, a_pre, act, gate, w_down, w_a, w_b, conv_w, xhat1, rstd1,
                                                    ln1_g, cat)

    full_t = lambda w, im: pl.BlockSpec((t, w), im)
    ffj = pl.BlockSpec((None, t, FF_BLK), lambda j, kk: (j, 0, 0))
    early = dict(
        w_ple_gate=g_w_g, w_ple_in=g_w_p,
        w_ff_down=_wgrad("dw_down", f, dr2b, ffj, full_t(half, lambda j, n: (0, n)),
                         pl.BlockSpec((None, FF_BLK, half), lambda j, n: (j, 0, n)), (N_SHARD, FF_BLK, D_MODEL), (N_SHARD, 2)),
        **dict(zip(("w_ff_a", "w_ff_b"), _wgrad_pair(
            "dw_ab", da_pre, dbb, x1b, ffj, full_t(half, lambda j, n: (0, n)),
            pl.BlockSpec((None, FF_BLK, half), lambda j, n: (j, 0, n)), (N_SHARD, FF_BLK, D_MODEL), (N_SHARD, 2)))),
        w_o=g_w_o)
    dep = early_grads(early)

    do1, do4, do16, dl1, dl4, dl16, duz, dws, dbs, zstat = _mix_bwd(
        dr1, w_o, hu, hz, mixed, attn, ln_z_g, ln_z_b, w_s[0], dep)
    dep = early_grads_sent(duz, (stat3, stat1, zstat, cstat, dws, dbs))
    dqkv = [_attn_bwd(qkv, do, lse, dl, d, dep)
            for qkv, do, lse, dl, d in zip(qkvs, (do1, do4, do16), lses, (dl1, dl4, dl16), DILATIONS)]
    dh, grad_x = _dx_in([g[0] for g in dqkv], [g[1] for g in dqkv], [g[2] for g in dqkv], duz, dr1, w_in,
                        c_tab, s1_tab, s2_tab)
    dep = early_grads_landed(grad_x)
    g_w_in = _wgrad("dw_in", xb, dh, full_t(half, lambda j, kk: (0, kk)), full_t(W_IN_BLK, lambda j, kk: (0, j)),
                    pl.BlockSpec((None, half, W_IN_BLK), lambda j, kk: (j, kk, 0)), (N_SHARD, D_MODEL, W_IN_BLK), (N_SHARD, 2),
                    dep)
    return grad_x, g_w_in


def _tile_rows(rows, mult, steps):
    if rows % mult:
        return rows
    return next(rows // k for k in range(steps, rows + 1) if rows % k == 0 and (rows // k) % mult == 0)


def _grid_spec(grid, in_specs, out_specs):
    return pltpu.PrefetchScalarGridSpec(num_scalar_prefetch=1, grid=grid, in_specs=in_specs, out_specs=out_specs)


def _on_own_steps(i, count, steps, work):
    if count == steps:
        work()
    else:
        pl.when(i < count)(work)


def _place_shards(name, ws, dtypes, place, dep):
    n = len(ws)
    tiles = [_tile_rows(w.shape[0], 16, 2) for w in ws]
    counts = [w.shape[0] // t for w, t in zip(ws, tiles)]
    steps = max(counts)

    def body(s_ref, *refs):
        i = pl.program_id(0)
        for a in range(n):
            def work(a=a):
                refs[n + 1 + a][...] = refs[a][...].astype(dtypes[a])
            _on_own_steps(i, counts[a], steps, work)

    def tile(a, lead):
        last = counts[a] - 1
        if lead:
            return pl.BlockSpec((None, tiles[a], ws[a].shape[1]), lambda i, s: (s[0], jnp.minimum(i, last), 0))
        return pl.BlockSpec((tiles[a], ws[a].shape[1]), lambda i, s: (jnp.minimum(i, last), 0))

    return pl.pallas_call(
        body, name=name,
        grid_spec=_grid_spec((steps,), [tile(a, False) for a in range(n)] + [pl.BlockSpec(memory_space=pl.ANY)],
                             [tile(a, True) for a in range(n)]),
        out_shape=[jax.ShapeDtypeStruct((N_SHARD, *w.shape), dt) for w, dt in zip(ws, dtypes)],
        compiler_params=_cp())(place, *ws, dep)


def _pair_sums(name, mines, gots, place):
    n = len(mines)
    tiles = [_tile_rows(g.shape[1], 16, 1) for g in gots]
    per_blk = [g.shape[1] // t for g, t in zip(gots, tiles)]
    counts = [N_SHARD * nh for nh in per_blk]
    steps = max(counts)

    def body(s_ref, *refs):
        i = pl.program_id(0)
        for a in range(n):
            def work(a=a):
                refs[2 * n + a][...] = (refs[a][...] + refs[n + a][...]).astype(BF16)
            _on_own_steps(i, counts[a], steps, work)

    def tile(a, mine):
        nh, last = per_blk[a], counts[a] - 1

        def index(i, s):
            g = jnp.minimum(i, last)
            return (g // nh, (s[1] * nh if mine else 0) + g % nh, 0)

        return pl.BlockSpec((None, tiles[a], gots[a].shape[2]), index)

    return pl.pallas_call(
        body, name=name,
        grid_spec=_grid_spec((steps,), [tile(a, True) for a in range(n)] + [tile(a, False) for a in range(n)],
                             [tile(a, False) for a in range(n)]),
        out_shape=[jax.ShapeDtypeStruct(g.shape, BF16) for g in gots], compiler_params=_cp())(place, *mines, *gots)


def _chip_sums(name, owns, landeds, place, dep):
    n = len(owns)
    tiles = [_tile_rows(o.shape[1], 16, 4) for o in owns]
    counts = [o.shape[1] // t for o, t in zip(owns, tiles)]
    steps = max(counts)

    def body(s_ref, *refs):
        i = pl.program_id(0)
        for a in range(n):
            def work(a=a):
                own, l1, l2, l3 = (refs[4 * a + k][...].astype(F32) for k in range(4))
                refs[4 * n + 1 + a][...] = ((own + l1) + l2) + l3
            _on_own_steps(i, counts[a], steps, work)

    def slot(a, d):
        last = counts[a] - 1
        return pl.BlockSpec((None, tiles[a], owns[a].shape[2]), lambda i, s: ((s[0] + d) % N_SHARD, jnp.minimum(i, last), 0))

    def out(a):
        nh, last = counts[a], counts[a] - 1
        return pl.BlockSpec((tiles[a], owns[a].shape[2]), lambda i, s: (s[1] * nh + jnp.minimum(i, last), 0))

    operands = [x for o, l in zip(owns, landeds) for x in (o, l, l, l)]
    return pl.pallas_call(
        body, name=name,
        grid_spec=_grid_spec((steps,), [slot(a, d) for a in range(n) for d in range(4)] + [pl.BlockSpec(memory_space=pl.ANY)],
                             [out(a) for a in range(n)]),
        out_shape=[jax.ShapeDtypeStruct((2 * o.shape[1], o.shape[2]), F32) for o in owns],
        compiler_params=_cp())(place, *operands, dep)


def _adamw_math(w, g, m, v):
    m = ADAM_B1 * m + (1.0 - ADAM_B1) * g
    v = ADAM_B2 * v + (1.0 - ADAM_B2) * (g * g)
    m_hat = m / (1.0 - ADAM_B1 ** ADAM_STEP)
    v_hat = v / (1.0 - ADAM_B2 ** ADAM_STEP)
    delta = -ADAM_LR * (m_hat / (jnp.sqrt(v_hat) + ADAM_EPS) + ADAM_WD * w)
    return delta, m, v


def _adamw_shards(name, ws, gs, ms, vs):
    n = len(ws)
    tiles = [_tile_rows(w.shape[1], 8, 8 if n > 1 else 2) for w in ws]
    counts = [w.shape[1] // t for w, t in zip(ws, tiles)]
    steps = max(counts)

    def body(*refs):
        i = pl.program_id(0)
        for a in range(n):
            def work(a=a):
                w_ref, g_ref, m_ref, v_ref = refs[4 * a:4 * a + 4]
                go_ref, d_ref, nm_ref, nv_ref = refs[4 * n + 4 * a:4 * n + 4 * a + 4]
                g = g_ref[...]
                go_ref[...] = g
                d_ref[...], nm_ref[...], nv_ref[...] = _adamw_math(w_ref[...], g, m_ref[...], v_ref[...])
            _on_own_steps(i, counts[a], steps, work)

    def tile(a, lead):
        last, c = counts[a] - 1, ws[a].shape[2]
        if lead:
            return pl.BlockSpec((None, tiles[a], c), lambda i: (0, jnp.minimum(i, last), 0))
        return pl.BlockSpec((tiles[a], c), lambda i: (jnp.minimum(i, last), 0))

    res = pl.pallas_call(
        body, name=name, grid=(steps,),
        in_specs=[tile(a, lead) for a in range(n) for lead in (True, False, True, True)],
        out_specs=[tile(a, True) for a in range(n) for _ in range(4)],
        out_shape=[jax.ShapeDtypeStruct(w.shape, F32) for w in ws for _ in range(4)],
        compiler_params=_cp())(*[x for quad in zip(ws, gs, ms, vs) for x in quad])
    return [tuple(res[4 * a:4 * a + 4]) for a in range(n)]


MESH = pl.DeviceIdType.MESH
ANY = pl.BlockSpec(memory_space=pl.ANY)


def _place():
    x, y, c = lax.axis_index("x"), lax.axis_index("y"), lax.axis_index("c")
    chips = [(1 - x, y), (x, 1 - y), (1 - x, 1 - y)]
    return x, y, c, 2 * x + y, chips


def _remote(src, dst, send_sem, recv_sem, dev):
    return pltpu.make_async_remote_copy(src_ref=src, dst_ref=dst, send_sem=send_sem, recv_sem=recv_sem,
                                        device_id=dev, device_id_type=MESH)


def _half(ref, hc, rows):
    return ref.at[pl.ds(hc * (rows // 2), rows // 2)]


def _sibling_join(blocks, tag):
    n = len(blocks)

    def body(*refs):
        outs = refs[n:2 * n]
        send, recv = refs[2 * n:]
        x, y, c, _, _ = _place()
        cps = []
        for a in range(n):
            h = blocks[a].shape[0] // 2
            mine = outs[a].at[pl.ds(c * h, h)]
            cp = _remote(mine, mine, send.at[a], recv.at[a], (x, y, 1 - c))
            cp.start()
            cps.append(cp)
        for a, cp in enumerate(cps):
            h = blocks[a].shape[0] // 2
            theirs = outs[a].at[pl.ds((1 - c) * h, h)]
            _remote(theirs, theirs, send.at[a], recv.at[a], (x, y, 1 - c)).wait_recv()
            cp.wait_send()

    sem = pltpu.SemaphoreType.DMA
    return pl.pallas_call(body, name=f"rs_sibling_join_{tag}", in_specs=[ANY] * n, out_specs=[ANY] * n,
                          out_shape=[jax.ShapeDtypeStruct(b_.shape, b_.dtype) for b_ in blocks],
                          input_output_aliases={a: a for a in range(n)},
                          scratch_shapes=[sem((n,)), sem((n,))])(*blocks)


def _join_start(blocks, after, tag):
    n = len(blocks)

    def body(*refs):
        ins = refs[:n]
        send, recv = refs[n + 1], refs[n + 2]
        token = refs[2 * n + 3]
        x, y, c, _, _ = _place()
        for a in range(n):
            h = blocks[a].shape[0] // 2
            mine = ins[a].at[pl.ds(c * h, h)]
            _remote(mine, mine, send.at[a], recv.at[a], (x, y, 1 - c)).start()
        token[...] = jnp.zeros_like(token)

    sems = pltpu.SemaphoreType.DMA((n,))
    res = pl.pallas_call(
        body, name=f"join_start_{tag}", in_specs=[HBM] * n + [ANY],
        out_specs=[SEM, SEM] + [HBM] * n + [pl.BlockSpec(memory_space=pltpu.VMEM)],
        out_shape=[sems, sems] + [pltpu.HBM(b_.shape, b_.dtype) for b_ in blocks] + [TOKEN],
        input_output_aliases={a: a + 2 for a in range(n)}, compiler_params=_in_flight_params(),
    )(*[_in_hbm(b_) for b_ in blocks], after)
    return res[0], res[1], res[2:2 + n], res[2 + n]


def _join_wait(send, recv, blocks, after, tag):
    n = len(blocks)

    def body(*refs):
        ins = refs[:n]
        send_ref, recv_ref = refs[n], refs[n + 1]
        x, y, c, _, _ = _place()
        for a in range(n):
            h = blocks[a].shape[0] // 2
            mine, theirs = ins[a].at[pl.ds(c * h, h)], ins[a].at[pl.ds((1 - c) * h, h)]
            _remote(mine, mine, send_ref.at[a], recv_ref.at[a], (x, y, 1 - c)).wait_send()
            _remote(theirs, theirs, send_ref.at[a], recv_ref.at[a], (x, y, 1 - c)).wait_recv()

    return pl.pallas_call(
        body, name=f"join_wait_{tag}", in_specs=[HBM] * n + [SEM, SEM, ANY], out_specs=[HBM] * n,
        out_shape=[pltpu.HBM(b_.shape, b_.dtype) for b_ in blocks],
        input_output_aliases={a: a for a in range(n)}, compiler_params=_in_flight_params(),
    )(*blocks, send, recv, after)


HBM = pl.BlockSpec(memory_space=pltpu.HBM)
SEM = pl.BlockSpec(memory_space=pltpu.SEMAPHORE)
TOKEN = jax.ShapeDtypeStruct((8, 128), F32)


def _in_flight_params():
    return pltpu.CompilerParams(has_side_effects=pltpu.SideEffectType.DATAFLOW_SIDE_EFFECTING)


def _in_hbm(a):
    return pltpu.with_memory_space_constraint(a, pltpu.HBM)


def _gather_piece(ref, rows, split, slot, hc):
    return _half(ref.at[slot], hc, rows) if split else ref.at[slot]


def _gather_start(stacks, split, after, tag):
    n = len(stacks)

    def body(*refs):
        ins = refs[:n]
        send, recv = refs[n + 1], refs[n + 2]
        token = refs[2 * n + 3]
        _, _, c, j, chips = _place()
        for a in range(n):
            mine = _gather_piece(ins[a], stacks[a].shape[1], split[a], j, c)
            for t in range(3):
                _remote(mine, mine, send.at[3 * a + t], recv.at[3 * a + t], (*chips[t], c)).start()
        token[...] = jnp.zeros_like(token)

    sems = pltpu.SemaphoreType.DMA((3 * n,))
    res = pl.pallas_call(
        body, name=f"gather_start_{tag}", in_specs=[HBM] * n + [ANY],
        out_specs=[SEM, SEM] + [HBM] * n + [pl.BlockSpec(memory_space=pltpu.VMEM)],
        out_shape=[sems, sems] + [pltpu.HBM(s.shape, s.dtype) for s in stacks] + [TOKEN],
        input_output_aliases={a: a + 2 for a in range(n)}, compiler_params=_in_flight_params(),
    )(*[_in_hbm(s) for s in stacks], after)
    return res[0], res[1], res[2:2 + n], res[2 + n]


def _gather_wait(send, recv, stacks, split, after, tag):
    n = len(stacks)

    def body(*refs):
        ins = refs[:n]
        send_ref, recv_ref = refs[n], refs[n + 1]
        _, _, c, j, chips = _place()
        for a in range(n):
            rows = stacks[a].shape[1]
            mine = _gather_piece(ins[a], rows, split[a], j, c)
            for t, (px, py) in enumerate(chips):
                theirs = _gather_piece(ins[a], rows, split[a], 2 * px + py, c)
                _remote(mine, mine, send_ref.at[3 * a + t], recv_ref.at[3 * a + t], (px, py, c)).wait_send()
                _remote(theirs, theirs, send_ref.at[3 * a + t], recv_ref.at[3 * a + t], (px, py, c)).wait_recv()

    return pl.pallas_call(
        body, name=f"gather_wait_{tag}", in_specs=[HBM] * n + [SEM, SEM, ANY], out_specs=[HBM] * n,
        out_shape=[pltpu.HBM(s.shape, s.dtype) for s in stacks],
        input_output_aliases={a: a for a in range(n)}, compiler_params=_in_flight_params(),
    )(*stacks, send, recv, after)


def _gather_forward(stacks, split, tag):
    idx = [a for a in range(len(stacks)) if split[a]]
    n = len(idx)

    def body(*refs):
        outs = refs[n:2 * n]
        send, recv = refs[2 * n:]
        x, y, c, _, chips = _place()
        sends = []
        for t, (px, py) in enumerate(chips):
            for a in range(n):
                blk = _half(outs[a].at[2 * px + py], c, stacks[idx[a]].shape[1])
                cp = _remote(blk, blk, send.at[a, t], recv.at[a, t], (x, y, 1 - c))
                cp.start()
                sends.append(cp)
        for t, (px, py) in enumerate(chips):
            for a in range(n):
                blk = _half(outs[a].at[2 * px + py], 1 - c, stacks[idx[a]].shape[1])
                _remote(blk, blk, send.at[a, t], recv.at[a, t], (x, y, 1 - c)).wait_recv()
        for cp in sends:
            cp.wait_send()

    sem = pltpu.SemaphoreType.DMA
    res = pl.pallas_call(
        body, name=f"gather_forward_{tag}", in_specs=[ANY] * n, out_specs=[ANY] * n,
        out_shape=[jax.ShapeDtypeStruct(stacks[a].shape, stacks[a].dtype) for a in idx],
        input_output_aliases={a: a for a in range(n)}, scratch_shapes=[sem((n, 3)), sem((n, 3))],
    )(*[stacks[a] for a in idx])
    out = list(stacks)
    for a, r in zip(idx, res):
        out[a] = r
    return out


def _forward_start(stacks, after, tag):
    n = len(stacks)

    def body(*refs):
        ins = refs[:n]
        send, recv = refs[n + 1], refs[n + 2]
        token = refs[2 * n + 3]
        x, y, c, _, chips = _place()
        for a in range(n):
            for t, (px, py) in enumerate(chips):
                blk = _half(ins[a].at[2 * px + py], c, stacks[a].shape[1])
                _remote(blk, blk, send.at[3 * a + t], recv.at[3 * a + t], (x, y, 1 - c)).start()
        token[...] = jnp.zeros_like(token)

    sems = pltpu.SemaphoreType.DMA((3 * n,))
    res = pl.pallas_call(
        body, name=f"forward_start_{tag}", in_specs=[HBM] * n + [ANY],
        out_specs=[SEM, SEM] + [HBM] * n + [pl.BlockSpec(memory_space=pltpu.VMEM)],
        out_shape=[sems, sems] + [pltpu.HBM(s.shape, s.dtype) for s in stacks] + [TOKEN],
        input_output_aliases={a: a + 2 for a in range(n)}, compiler_params=_in_flight_params(),
    )(*[_in_hbm(s) for s in stacks], after)
    return res[0], res[1], res[2:2 + n], res[2 + n]


def _forward_wait(send, recv, stacks, after, tag):
    n = len(stacks)

    def body(*refs):
        ins = refs[:n]
        send_ref, recv_ref = refs[n], refs[n + 1]
        x, y, c, _, chips = _place()
        for a in range(n):
            for t, (px, py) in enumerate(chips):
                mine = _half(ins[a].at[2 * px + py], c, stacks[a].shape[1])
                theirs = _half(ins[a].at[2 * px + py], 1 - c, stacks[a].shape[1])
                _remote(mine, mine, send_ref.at[3 * a + t], recv_ref.at[3 * a + t], (x, y, 1 - c)).wait_send()
                _remote(theirs, theirs, send_ref.at[3 * a + t], recv_ref.at[3 * a + t], (x, y, 1 - c)).wait_recv()

    return pl.pallas_call(
        body, name=f"forward_wait_{tag}", in_specs=[HBM] * n + [SEM, SEM, ANY], out_specs=[HBM] * n,
        out_shape=[pltpu.HBM(s.shape, s.dtype) for s in stacks],
        input_output_aliases={a: a for a in range(n)}, compiler_params=_in_flight_params(),
    )(*stacks, send, recv, after)


def _swap_start(grads, tag):
    n = len(grads)

    def body(*refs):
        ins, gots = refs[:n], refs[n:2 * n]
        send, recv = refs[2 * n], refs[2 * n + 1]
        token = refs[4 * n + 2]
        x, y, c, _, _ = _place()
        for a in range(n):
            h = grads[a].shape[1] // 2
            _remote(ins[a].at[:, pl.ds((1 - c) * h, h)], gots[a], send.at[a], recv.at[a], (x, y, 1 - c)).start()
        token[...] = jnp.zeros_like(token)

    sems = pltpu.SemaphoreType.DMA((n,))
    halves = [(g.shape[0], g.shape[1] // 2, g.shape[2]) for g in grads]
    res = pl.pallas_call(
        body, name=f"swap_start_{tag}", in_specs=[HBM] * (2 * n),
        out_specs=[SEM, SEM] + [HBM] * (2 * n) + [pl.BlockSpec(memory_space=pltpu.VMEM)],
        out_shape=[sems, sems] + [pltpu.HBM(g.shape, g.dtype) for g in grads] + [pltpu.HBM(s, F32) for s in halves] + [TOKEN],
        input_output_aliases={a: a + 2 for a in range(2 * n)}, compiler_params=_in_flight_params(),
    )(*[_in_hbm(g) for g in grads], *[_in_hbm(lax.empty(s, F32)) for s in halves])
    return res[0], res[1], res[2:2 + n], res[2 + n:2 + 2 * n], res[2 + 2 * n]


def _swap_wait(send, recv, grads, gots, after, tag):
    n = len(grads)

    def body(*refs):
        ins, lnd = refs[:n], refs[n:2 * n]
        send_ref, recv_ref = refs[2 * n], refs[2 * n + 1]
        x, y, c, _, _ = _place()
        for a in range(n):
            h = grads[a].shape[1] // 2
            cp = _remote(ins[a].at[:, pl.ds((1 - c) * h, h)], lnd[a], send_ref.at[a], recv_ref.at[a], (x, y, 1 - c))
            cp.wait_send()
            cp.wait_recv()

    bufs = [pltpu.HBM(g.shape, g.dtype) for g in grads] + [pltpu.HBM(g.shape, g.dtype) for g in gots]
    res = pl.pallas_call(
        body, name=f"swap_wait_{tag}", in_specs=[HBM] * (2 * n) + [SEM, SEM, ANY], out_specs=[HBM] * (2 * n),
        out_shape=bufs, input_output_aliases={a: a for a in range(2 * n)}, compiler_params=_in_flight_params(),
    )(*grads, *gots, send, recv, after)
    return res[:n], res[n:]


def _exchange_start(parts, tag):
    n = len(parts)

    def body(*refs):
        ins, lands = refs[:n], refs[n:2 * n]
        send, recv = refs[2 * n], refs[2 * n + 1]
        token = refs[4 * n + 2]
        _, _, c, j, chips = _place()
        for t, (px, py) in enumerate(chips):
            for a in range(n):
                _remote(ins[a].at[2 * px + py], lands[a].at[j], send.at[3 * a + t], recv.at[3 * a + t], (px, py, c)).start()
        token[...] = jnp.zeros_like(token)

    sems = pltpu.SemaphoreType.DMA((3 * n,))
    bufs = [pltpu.HBM(p.shape, p.dtype) for p in parts]
    res = pl.pallas_call(
        body, name=f"exchange_start_{tag}", in_specs=[HBM] * (2 * n),
        out_specs=[SEM, SEM] + [HBM] * (2 * n) + [pl.BlockSpec(memory_space=pltpu.VMEM)],
        out_shape=[sems, sems] + bufs + bufs + [TOKEN],
        input_output_aliases={a: a + 2 for a in range(2 * n)}, compiler_params=_in_flight_params(),
    )(*[_in_hbm(p) for p in parts], *[_in_hbm(lax.empty(p.shape, p.dtype)) for p in parts])
    return res[0], res[1], res[2:2 + n], res[2 + n:2 + 2 * n], res[2 + 2 * n]


def _exchange_wait(send, recv, parts, lands, after, tag):
    n = len(parts)

    def body(*refs):
        ins, lnd = refs[:n], refs[n:2 * n]
        send_ref, recv_ref = refs[2 * n], refs[2 * n + 1]
        _, _, c, j, chips = _place()
        for t, (px, py) in enumerate(chips):
            jt = 2 * px + py
            for a in range(n):
                _remote(ins[a].at[jt], lnd[a].at[j], send_ref.at[3 * a + t], recv_ref.at[3 * a + t], (px, py, c)).wait_send()
                _remote(ins[a].at[jt], lnd[a].at[jt], send_ref.at[3 * a + t], recv_ref.at[3 * a + t], (px, py, c)).wait_recv()

    bufs = [pltpu.HBM(p.shape, p.dtype) for p in parts]
    res = pl.pallas_call(
        body, name=f"exchange_wait_{tag}", in_specs=[HBM] * (2 * n) + [SEM, SEM, ANY], out_specs=[HBM] * (2 * n),
        out_shape=bufs + bufs, input_output_aliases={a: a for a in range(2 * n)}, compiler_params=_in_flight_params(),
    )(*parts, *lands, send, recv, after)
    return res[:n], res[n:]


def _small_chip_sums(arrs):
    n = len(arrs)

    def body(*refs):
        ins, outs = refs[:n], refs[n:2 * n]
        sib = refs[2 * n:3 * n]
        send, recv = refs[3 * n:]
        x, y, c, j, _ = _place()
        swaps = [_remote(ins[a], sib[a], send.at[a], recv.at[a], (x, y, 1 - c)) for a in range(n)]
        for cp in swaps:
            cp.start()
        for a in range(n):
            swaps[a].wait_recv()
            outs[a][j] = ins[a][...] + sib[a][...]
        for cp in swaps:
            cp.wait_send()

    sem = pltpu.SemaphoreType.DMA
    vm = pl.BlockSpec(memory_space=pltpu.VMEM)
    return pl.pallas_call(
        body, name="small_chip_sums", in_specs=[vm] * n, out_specs=[vm] * n,
        out_shape=[jax.ShapeDtypeStruct((N_SHARD, *a.shape), F32) for a in arrs],
        scratch_shapes=[pltpu.VMEM(a.shape, F32) for a in arrs] + [sem((n,)), sem((n,))],
        compiler_params=_cp(),
    )(*arrs)


def _small_totals(stacks):
    n = len(stacks)

    def body(*refs):
        for a in range(n):
            refs[n + a][...] = ((refs[a][0] + refs[a][1]) + refs[a][2]) + refs[a][3]

    return pl.pallas_call(body, name="small_totals", out_shape=[jax.ShapeDtypeStruct(s.shape[1:], F32) for s in stacks],
                          compiler_params=_cp())(*stacks)


SMALL_1024 = ("ln1_g", "ln1_b", "ln2_g", "ln2_b", "b_ple_gate", "ln3_g", "ln3_b")


def _adamw_small(red3, red1, redz, g_conv_w, redc, red_ws, red_bs, params):
    held_as = {"ln_z_g": (1, D_GMLP), "ln_z_b": (1, D_GMLP), "w_s": (N_HEADS * BLK, BLK), "b_s": (N_HEADS, BLK),
               "conv_w": (3, 1, FF_BLK), "conv_b": (1, D_FF), **{k: (1, D_MODEL) for k in SMALL_1024}}
    names = list(held_as)
    flat = [a.reshape(held_as[k]) for k in names for a in params[k]]

    def body(r3, r1, rz, gcw, rc, rws, rbs, *refs):
        ins, outs = refs[:3 * len(names)], refs[3 * len(names):]

        def grad_of(k):
            if k == "w_s":
                return rws[...]
            if k == "b_s":
                return rbs[...]
            src, row = {"ln3_g": (r3, 0), "ln3_b": (r3, 1), "b_ple_gate": (r3, 2), "ln2_g": (r3, 3), "ln2_b": (r3, 4),
                        "ln1_g": (r1, 0), "ln1_b": (r1, 1), "ln_z_g": (rz, 0), "ln_z_b": (rz, 1)}[k]
            return src[row:row + 1, :]

        for i, k in enumerate(names):
            w_ref, m_ref, v_ref = ins[3 * i:3 * i + 3]
            g_ref, d_ref, nm_ref, nv_ref = outs[4 * i:4 * i + 4]
            if k == "conv_b":
                for j in range(N_SHARD):
                    cols = slice(j * FF_BLK, (j + 1) * FF_BLK)
                    g = rc[j * STAT_ROWS + 3:j * STAT_ROWS + 4, :]
                    g_ref[:, cols] = g
                    d_ref[:, cols], nm_ref[:, cols], nv_ref[:, cols] = _adamw_math(w_ref[:, cols], g, m_ref[:, cols], v_ref[:, cols])
                continue
            if k == "conv_w":
                for tap in range(3):
                    g = gcw[tap:tap + 1, :]
                    g_ref[tap] = g
                    d_ref[tap], nm_ref[tap], nv_ref[tap] = _adamw_math(w_ref[tap], g, m_ref[tap], v_ref[tap])
                continue
            g = grad_of(k)
            g_ref[...] = g
            d_ref[...], nm_ref[...], nv_ref[...] = _adamw_math(w_ref[...], g, m_ref[...], v_ref[...])

    res = pl.pallas_call(
        body, name="adamw_small",
        out_shape=[jax.ShapeDtypeStruct(held_as[k], F32) for k in names for _ in range(4)],
        compiler_params=_cp(),
    )(red3, red1, redz, g_conv_w, redc, red_ws, red_bs, *flat)
    return {k: tuple(r.reshape(params[k][0].shape) for r in res[4 * i:4 * i + 4]) for i, k in enumerate(names)}


WEIGHTS = ("w_in", "ln_z_g", "ln_z_b", "w_s", "b_s", "w_o", "ln1_g", "ln1_b", "w_ff_a", "w_ff_b", "conv_w", "conv_b",
           "w_ff_down", "ln2_g", "ln2_b", "w_ple_gate", "b_ple_gate", "w_ple_in", "ln3_g", "ln3_b")
BIG = ("w_in", "w_o", "w_ff_a", "w_ff_b", "w_ff_down", "w_ple_gate", "w_ple_in")
TRANSPOSED = ("w_ff_a", "w_ff_b")
LATE = ("w_o", "w_ff_a", "w_ff_b", "w_ff_down", "w_ple_gate", "w_ple_in", "conv_w")


def kernel(x, p, positions, w_in, ln_z_g, ln_z_b, w_s, b_s, w_o, ln1_g, ln1_b, w_ff_a, w_ff_b, conv_w, conv_b, w_ff_down, ln2_g, ln2_b, w_ple_gate, b_ple_gate, w_ple_in, ln3_g, ln3_b, loss_target, m_w_in, m_ln_z_g, m_ln_z_b, m_w_s, m_b_s, m_w_o, m_ln1_g, m_ln1_b, m_w_ff_a, m_w_ff_b, m_conv_w, m_conv_b, m_w_ff_down, m_ln2_g, m_ln2_b, m_w_ple_gate, m_b_ple_gate, m_w_ple_in, m_ln3_g, m_ln3_b, v_w_in, v_ln_z_g, v_ln_z_b, v_w_s, v_b_s, v_w_o, v_ln1_g, v_ln1_b, v_w_ff_a, v_w_ff_b, v_conv_w, v_conv_b, v_w_ff_down, v_ln2_g, v_ln2_b, v_w_ple_gate, v_b_ple_gate, v_w_ple_in, v_ln3_g, v_ln3_b):
    args = locals()
    w = {k: args[k] for k in WEIGHTS}
    m = {k: args["m_" + k] for k in WEIGHTS}
    v = {k: args["v_" + k] for k in WEIGHTS}

    for k in TRANSPOSED:
        w[k], m[k], v[k] = (jnp.swapaxes(a, 1, 2) for a in (w[k], m[k], v[k]))

    chip = 2 * lax.axis_index("x") + lax.axis_index("y")
    place = jnp.stack([chip, lax.axis_index("c")]).astype(jnp.int32)
    stack = dict(zip(["w_in"], _place_shards("cast_w_in", [w["w_in"][0]], [MXU], place, place)))
    i_send, i_recv, in_flight, dep = _gather_start([stack["w_in"]], [True], place, "w_in")
    stack.update(zip(LATE, _place_shards("cast_late", [w[k][0] for k in LATE],
                                         [F32 if k == "conv_w" else MXU for k in LATE], place, dep)))
    split_late = [k != "conv_w" for k in LATE]
    g_send, g_recv, late_flight, start_dep = _gather_start([stack[k] for k in LATE], split_late, place, "late")
    rope = _rope_tables(positions, x.shape[1], start_dep)
    landed_in = _gather_wait(i_send, i_recv, in_flight, [True], rope[0], "w_in")
    w_in_full, = _gather_forward(landed_in, [True], "w_in")
    halves =[k for k, sp in zip(LATE, split_late) if sp]
    trips = {}

    def late_landed(after):
        fw = dict(zip(LATE, _gather_wait(g_send, g_recv, late_flight, split_late, after, "late")))
        trips["late"] = (fw, *_forward_start([fw[k] for k in halves], fw["conv_w"], "late"))
        return trips["late"][-1]

    def late_weights(after):
        fw, send, recv, flight, _ = trips["late"]
        fw.update(zip(halves, _forward_wait(send, recv, flight, after, "late")))
        return (fw["w_o"].reshape(D_MODEL, D_MODEL), fw["w_ff_a"], fw["w_ff_b"], fw["conv_w"], fw["w_ff_down"],
                fw["w_ple_gate"].reshape(D_MODEL, D_MODEL), fw["w_ple_in"])

    def swap_started(names, grads, tag):
        stacked = [g.reshape(N_SHARD, *w[k].shape[1:]) for k, g in zip(names, grads)]
        return (names, tag, *_swap_start(stacked, tag))

    def partial_sums(swap, after):
        names, tag, send, recv, stacked, gots, _ = swap
        stacked, got = _swap_wait(send, recv, stacked, gots, after, tag)
        pair = _pair_sums(f"rs_pair_{tag}", stacked, got, place)
        return (names, tag, *_exchange_start(pair, tag))

    def chip_summed(trip, after, dep):
        names, tag, send, recv, pair, lands, _ = trip
        pair, landed = _exchange_wait(send, recv, pair, lands, after, tag)
        return _chip_sums(f"rs_sum_{tag}", pair, landed, place, dep), names, tag

    def reduced(trip, after, dep):
        blocks, names, tag = chip_summed(trip, after, dep)
        return dict(zip(names, _sibling_join(blocks, tag)))

    def early_grads_landed(after):
        blocks, names, tag = chip_summed(trips["early"], after, trips["small"][-1])
        trips["join"] = (names, *_join_start(blocks, after, tag))
        return trips["join"][-1]

    def early_grads(grads):
        trips["swap"] = swap_started(list(grads), list(grads.values()), "early")
        return trips["swap"][-1]

    def early_grads_sent(after, small):
        trips["early"] = partial_sums(trips["swap"], after)
        stat3, stat1, zstat, cstat, dws, dbs = small
        sums = _small_chip_sums([stat3, stat1, zstat, cstat.reshape(N_SHARD * STAT_ROWS, FF_BLK),
                                 dws.reshape(N_HEADS * BLK, BLK), dbs])
        trips["small"] = _gather_start(sums, [False] * len(sums), trips["early"][-1], "small")
        return trips["small"][-1]

    grad_x, g_w_in = _local_step(
        x[0], p[0, 0], rope, loss_target[0], w_in_full, start_dep, late_landed, late_weights, early_grads, early_grads_sent,
        early_grads_landed, ln_z_g, ln_z_b, w_s, b_s, ln1_g, ln1_b, conv_b, ln2_g, ln2_b, b_ple_gate, ln3_g, ln3_b)

    swap_in = swap_started(["w_in"], [g_w_in], "w_in")
    out = {}

    s_send, s_recv, s_flight, _ = trips["small"]
    red3, red1, redz, redc, red_ws, red_bs = _small_totals(
        _gather_wait(s_send, s_recv, s_flight, [False] * len(s_flight), swap_in[-1], "small"))
    loss = (0.5 / D_MODEL) * jnp.sum(red3[5])
    g_conv_w = lax.dynamic_slice_in_dim(redc, chip * STAT_ROWS, STAT_ROWS, 0)
    names_small = [k for k in WEIGHTS if k not in BIG]
    out.update(_adamw_small(red3, red1, redz, g_conv_w, redc, red_ws, red_bs, {k: (w[k], m[k], v[k]) for k in names_small}))
    trips["w_in"] = partial_sums(swap_in, out["ln3_b"][3])

    def adamw(red, tag):
        names = list(red)
        steps = _adamw_shards(f"adamw_{tag}", [w[k] for k in names], [red[k] for k in names], [m[k] for k in names],
                              [v[k] for k in names])
        out.update(zip(names, steps))

    names, j_send, j_recv, j_flight, _ = trips["join"]
    adamw(dict(zip(names, _join_wait(j_send, j_recv, j_flight, trips["w_in"][-1], "early"))), "early")
    adamw(reduced(trips["w_in"], out["w_o"][3], start_dep), "w_in")
    for k in TRANSPOSED:
        out[k] = tuple(jnp.swapaxes(a, 1, 2) for a in out[k])

    return (loss, grad_x[None], *[out[k][0] for k in WEIGHTS], *[out[k][1] for k in WEIGHTS],
            *[out[k][2] for k in WEIGHTS], *[out[k][3] for k in WEIGHTS])
```

```python
import math

import numpy as np
import jax
import jax.numpy as jnp
from jax import lax
from jax.experimental import pallas as pl
from jax.experimental.pallas import tpu as pltpu

F32 = jnp.float32
BF16 = jnp.bfloat16
MXU = BF16

D_MODEL = 1024
HEAD_DIM = 64
N_HEADS = 8
D_ATTN = 512
D_GMLP = 512
D_IN = 2560
DILATIONS = (1, 4, 16)
BLK = 128
ROPE_THETA = 500000.0
ROPE_DIM = 16
D_FF = 2816
D_PLE = 256
LN_EPS = 1e-5
ALPHA = 2.0 ** 0.25
NEG_INF = -1e30
N_SHARD = 4
W_IN_BLK = D_IN // N_SHARD
FF_BLK = D_FF // N_SHARD
ROW_BLK = D_MODEL // N_SHARD
ADAM_LR, ADAM_B1, ADAM_B2, ADAM_EPS, ADAM_WD, ADAM_STEP = 0.001, 0.9, 0.999, 1e-08, 0.01, 10

TM = 512
HALO = 8
ROW_GROUPS = 2
VMEM_LIMIT = 56 * 1024 * 1024


def _cp(**kw):
    return pltpu.CompilerParams(vmem_limit_bytes=VMEM_LIMIT, **kw)


def _full(shape):
    n = len(shape)
    return pl.BlockSpec(shape, lambda *_: (0,) * n)


def _gelu(x):
    return 0.5 * x * (1.0 + lax.erf(x * (1.0 / math.sqrt(2.0))))


def _gelu_grad(x):
    return 0.5 * (1.0 + lax.erf(x * (1.0 / math.sqrt(2.0)))) + x * jnp.exp(-0.5 * x * x) * (1.0 / math.sqrt(2.0 * math.pi))


def _ln_fwd(r):
    mu = jnp.mean(r, axis=-1, keepdims=True)
    xc = r - mu
    var = jnp.mean(xc * xc, axis=-1, keepdims=True)
    rstd = lax.rsqrt(var + LN_EPS)
    return xc * rstd, rstd


def _ln_bwd(dy, xhat, rstd, g):
    dxh = dy * g
    m1 = jnp.mean(dxh, axis=-1, keepdims=True)
    m2 = jnp.mean(dxh * xhat, axis=-1, keepdims=True)
    return rstd * (dxh - m1 - xhat * m2)


def _dot(a, b):
    return jnp.dot(a.astype(MXU), b.astype(MXU), preferred_element_type=F32)


def _dot_nt(a, b):
    return lax.dot_general(a.astype(MXU), b.astype(MXU), (((1,), (1,)), ((), ())), preferred_element_type=F32)


def _dot_tn(a, b):
    return lax.dot_general(a.astype(MXU), b.astype(MXU), (((0,), (0,)), ((), ())), preferred_element_type=F32)


def _colsum(v):
    return jnp.sum(v, axis=0, keepdims=True)


def _rope_tables(positions, t, dep):
    inv = np.float32(ROPE_THETA) ** (-np.arange(0, ROPE_DIM, 2, dtype=np.float32) / np.float32(ROPE_DIM))
    half = ROPE_DIM // 2
    pos_rep = jnp.repeat(positions.reshape(t // 16, 16), half, axis=1)
    inv_row = jnp.asarray(np.tile(inv, 16)[None, :], F32)

    def trig_body(pos_ref, inv_ref, dep_ref, cos_ref, sin_ref):
        ang = pos_ref[...].astype(F32) * inv_ref[...]
        cos_ref[...] = jnp.cos(ang)
        sin_ref[...] = jnp.sin(ang)

    vm = pl.BlockSpec(memory_space=pltpu.VMEM)
    cos8, sin8 = pl.pallas_call(
        trig_body, name="rope_trig", in_specs=[vm, vm, pl.BlockSpec(memory_space=pl.ANY)], out_specs=[vm, vm],
        out_shape=(jax.ShapeDtypeStruct((t // 16, 128), F32), jax.ShapeDtypeStruct((t // 16, 128), F32)),
    )(pos_rep, inv_row, dep)
    cos8 = cos8.reshape(t, half)
    sin8 = sin8.reshape(t, half)

    lane = np.arange(128) % HEAD_DIM
    sel = (np.arange(half)[:, None] == (lane % half)[None, :])
    e_cos = (sel & (lane < ROPE_DIM)[None, :]).astype(np.float32)
    e_s1 = -(sel & (lane < half)[None, :]).astype(np.float32)
    e_s2 = (sel & ((lane >= half) & (lane < ROPE_DIM))[None, :]).astype(np.float32)
    ones = (lane >= ROPE_DIM).astype(np.float32)[None, :]

    def expand_body(cos_ref, sin_ref, ec_ref, e1_ref, e2_ref, ones_ref, c_ref, s1_ref, s2_ref):
        c_ref[...] = _dot_select(cos_ref[...], ec_ref[...], terms=3) + ones_ref[...]
        s1_ref[...] = _dot_select(sin_ref[...], e1_ref[...], terms=3)
        s2_ref[...] = _dot_select(sin_ref[...], e2_ref[...], terms=3)

    tab = jax.ShapeDtypeStruct((t, 128), F32)
    return pl.pallas_call(expand_body, name="rope_expand", out_shape=(tab, tab, tab), compiler_params=_cp())(
        cos8, sin8, jnp.asarray(e_cos), jnp.asarray(e_s1), jnp.asarray(e_s2), jnp.asarray(ones))


def _tile_heads(tab):
    return jnp.concatenate([tab] * (D_ATTN // 128), axis=1)


def _rope_apply(v, c, s1, s2):
    n = v.shape[1]
    half = ROPE_DIM // 2
    return v * c + pltpu.roll(v, n - half, 1) * s1 + pltpu.roll(v, half, 1) * s2


def _rope_apply_t(g, c, s1, s2):
    n = g.shape[1]
    half = ROPE_DIM // 2
    return g * c + pltpu.roll(g * s1, half, 1) + pltpu.roll(g * s2, n - half, 1)


LANE_CHUNKS = D_ATTN // 128
HEAD_LANES = 128 // N_HEADS


def _perm_shape(t, d, w, dtype):
    return jax.ShapeDtypeStruct((d, t // d, w), dtype)


def _perm_tile(d, w):
    return pl.BlockSpec((None if d == 1 else d, TM // d, w), lambda i: (0, i, 0))


def _to_planes(ref, scr, d, n_chunks, dtype):
    for r in range(d):
        for cc in range(n_chunks):
            ref[r, :, cc * 128:(cc + 1) * 128] = scr.at[cc][pl.ds(r, TM // d, stride=d), :].astype(dtype)


def _from_planes(ref, scr, d, n_chunks, accumulate=False):
    for r in range(d):
        for cc in range(n_chunks):
            rows = scr.at[cc]
            val = ref[r, :, cc * 128:(cc + 1) * 128].astype(F32)
            if accumulate:
                rows[pl.ds(r, TM // d, stride=d), :] += val
            else:
                rows[pl.ds(r, TM // d, stride=d), :] = val


def _chunks(val):
    return [val[:, cc * 128:(cc + 1) * 128] for cc in range(val.shape[1] // 128)]


def _unchunk(scr, n_chunks, base=0):
    return jnp.concatenate([scr[base + cc] for cc in range(n_chunks)], axis=1)


def _head_expand():
    src = np.arange(128)[:, None]
    dst = np.arange(D_ATTN)[None, :]
    return jnp.asarray((src == (dst // HEAD_DIM) * HEAD_LANES).astype(np.float32))


def _head_reduce():
    src = np.arange(D_ATTN)[:, None]
    dst = np.arange(128)[None, :]
    return jnp.asarray((src // HEAD_DIM == dst // HEAD_LANES).astype(np.float32))


def _dot_select(a, sel, terms=2):
    sel = sel.astype(BF16)
    out, rest = None, a
    for _ in range(terms):
        part = rest.astype(BF16)
        rest = rest - part.astype(F32)
        prod = jnp.dot(part, sel, preferred_element_type=F32)
        out = prod if out is None else out + prod
    return out


def _qkvuz(x, w_in, c_tab, s1_tab, s2_tab, ln_z_g, ln_z_b, w_s, b_full, dep):
    t = x.shape[0]
    nchunk = TM // BLK

    def body(x_ref, w_ref, c_ref, s1_ref, s2_ref, g_ref, b_ref, ws_ref, bf_ref, dep_ref,
             qkv1_ref, qkv4_ref, qkv16_ref, hu_ref, hz_ref, mixed_ref, gm_ref, xb_ref, h_scr, wm_scr, p_scr):
        @pl.when(pl.program_id(0) == 0)
        def _():
            row = lax.broadcasted_iota(jnp.int32, (BLK, BLK), 0)
            col = lax.broadcasted_iota(jnp.int32, (BLK, BLK), 1)
            for g in range(N_HEADS):
                wm_scr[g] = jnp.where(col <= row, ws_ref[g], 0.0).astype(MXU)

        xb = x_ref[...].astype(MXU)
        xb_ref[...] = xb
        for j in range(N_SHARD):
            h_scr[:, j * W_IN_BLK:(j + 1) * W_IN_BLK] = jnp.dot(xb, w_ref[j], preferred_element_type=F32)
        c, s1, s2 = _tile_heads(c_ref[...]), _tile_heads(s1_ref[...]), _tile_heads(s2_ref[...])
        q = _rope_apply(h_scr[:, 0:D_ATTN], c, s1, s2) * (1.0 / math.sqrt(HEAD_DIM))
        k = _rope_apply(h_scr[:, D_ATTN:2 * D_ATTN], c, s1, s2)
        for part, val in enumerate((q, k, h_scr[:, 2 * D_ATTN:3 * D_ATTN])):
            qkv1_ref[:, part * D_ATTN:(part + 1) * D_ATTN] = val.astype(MXU)
            for cc in range(LANE_CHUNKS):
                p_scr[part * LANE_CHUNKS + cc] = val[:, cc * 128:(cc + 1) * 128]
        _to_planes(qkv4_ref, p_scr, DILATIONS[1], 3 * LANE_CHUNKS, MXU)
        _to_planes(qkv16_ref, p_scr, DILATIONS[2], 3 * LANE_CHUNKS, MXU)
        hu = h_scr[:, 3 * D_ATTN:3 * D_ATTN + D_GMLP]
        hz = h_scr[:, 3 * D_ATTN + D_GMLP:]
        hu_ref[...] = hu
        hz_ref[...] = hz
        zhat, _ = _ln_fwd(_gelu(hz))
        zn = (zhat * g_ref[...] + b_ref[...]).astype(MXU)
        for ch in range(nchunk):
            rows = slice(ch * BLK, (ch + 1) * BLK)
            for g in range(N_HEADS):
                cols = slice(g * HEAD_DIM, (g + 1) * HEAD_DIM)
                mixed_ref[rows, cols] = jnp.dot(wm_scr[g], zn[rows, cols], preferred_element_type=F32) + bf_ref[:, cols]
        gm_ref[...] = (_gelu(hu) * mixed_ref[...]).astype(MXU)

    tok = lambda w: pl.BlockSpec((TM, w), lambda i: (i, 0))
    outs = [_perm_shape(t, d, 3 * D_ATTN, MXU) for d in DILATIONS] + [jax.ShapeDtypeStruct((t, D_GMLP), F32)] * 3 + [
        jax.ShapeDtypeStruct((t, D_GMLP), MXU), jax.ShapeDtypeStruct((t, D_MODEL), MXU)]
    return pl.pallas_call(
        body, name="qkvuz", grid=(t // TM,),
        in_specs=[tok(D_MODEL), _full(w_in.shape), tok(128), tok(128), tok(128), _full(ln_z_g.shape), _full(ln_z_b.shape),
                  _full(w_s.shape), _full(b_full.shape), pl.BlockSpec(memory_space=pl.ANY)],
        out_specs=[_perm_tile(d, 3 * D_ATTN) for d in DILATIONS] + [tok(D_ATTN)] * 4 + [tok(D_MODEL)], out_shape=outs,
        scratch_shapes=[pltpu.VMEM((TM, D_IN), F32), pltpu.VMEM((N_HEADS, BLK, BLK), MXU),
                        pltpu.VMEM((3 * LANE_CHUNKS, TM, 128), F32)],
        compiler_params=_cp(dimension_semantics=("arbitrary",)),
    )(x, w_in, c_tab, s1_tab, s2_tab, ln_z_g, ln_z_b, w_s, b_full, dep)


def _band_valid(n):
    i = lax.broadcasted_iota(jnp.int32, (BLK, 2 * BLK), 0)
    j = lax.broadcasted_iota(jnp.int32, (BLK, 2 * BLK), 1)
    return (j >= i) & (j <= i + BLK) & ((j >= BLK) | (n > 0))


def _attn_fwd(qkv, d, dep):
    _, l_sub, _ = qkv.shape
    nb = l_sub // BLK

    def body(q_ref, kp_ref, kc_ref, vp_ref, vc_ref, dep_ref, o_ref, l_ref):
        valid = _band_valid(pl.program_id(1))
        kcat = jnp.concatenate([kp_ref[...], kc_ref[...]], axis=0)
        vcat = jnp.concatenate([vp_ref[...], vc_ref[...]], axis=0)
        for h in range(N_HEADS):
            cols = slice(h * HEAD_DIM, (h + 1) * HEAD_DIM)
            s = jnp.where(valid, _dot_nt(q_ref[:, cols], kcat[:, cols]), NEG_INF)
            m = jnp.max(s, axis=-1, keepdims=True)
            e = jnp.exp(s - m)
            den = jnp.sum(e, axis=-1, keepdims=True)
            o_ref[:, cols] = _dot(e, vcat[:, cols]) * (1.0 / den)
            l_ref[:, h * HEAD_LANES:(h + 1) * HEAD_LANES] = jnp.broadcast_to(m + jnp.log(den), (BLK, HEAD_LANES))

    def blk(w, col, prev=False):
        return pl.BlockSpec((None, BLK, w), lambda r, n: (r, jnp.maximum(n - 1, 0) if prev else n, col))

    return pl.pallas_call(
        body, name=f"attn_fwd_d{d}", grid=(d, nb),
        in_specs=[blk(D_ATTN, 0), blk(D_ATTN, 1, True), blk(D_ATTN, 1), blk(D_ATTN, 2, True), blk(D_ATTN, 2),
                  pl.BlockSpec(memory_space=pl.ANY)],
        out_specs=[blk(D_ATTN, 0), blk(128, 0)],
        out_shape=[jax.ShapeDtypeStruct((d, l_sub, D_ATTN), F32), jax.ShapeDtypeStruct((d, l_sub, 128), F32)],
        compiler_params=_cp(dimension_semantics=("arbitrary", "arbitrary")),
    )(qkv, qkv, qkv, qkv, qkv, dep)


def _attn_bwd(qkv, do, lse, delta, d, dep):
    _, l_sub, _ = qkv.shape
    nb = l_sub // BLK
    whole = l_sub <= 8 * BLK

    def shares(n, q_ref, kp_ref, kc_ref, vp_ref, vc_ref, do_ref, l_ref, dl_ref, dq_ref):
        valid = _band_valid(n)
        kcat = jnp.concatenate([kp_ref[...], kc_ref[...]], axis=0)
        vcat = jnp.concatenate([vp_ref[...], vc_ref[...]], axis=0)
        for h in range(N_HEADS):
            cols = slice(h * HEAD_DIM, (h + 1) * HEAD_DIM)
            stat = slice(h * HEAD_LANES, h * HEAD_LANES + 1)
            qh, doh = q_ref[:, cols], do_ref[:, cols]
            p = jnp.where(valid, jnp.exp(_dot_nt(qh, kcat[:, cols]) - l_ref[:, stat]), 0.0)
            ds = p * (_dot_nt(doh, vcat[:, cols]) - dl_ref[:, stat])
            dq_ref[:, cols] = _dot(ds, kcat[:, cols])
            yield cols, _dot_tn(ds, qh), _dot_tn(p, doh)

    def body_whole(*refs):
        dk_ref, dv_ref = refs[10:]
        n = pl.program_id(1)
        cur = pl.ds(pl.multiple_of(n * BLK, BLK), BLK)
        prev = pl.ds(pl.multiple_of(jnp.maximum(n - 1, 0) * BLK, BLK), BLK)
        for cols, dk2, dv2 in shares(n, *refs[:8], refs[9]):
            dk_ref[cur, cols] = dk2[BLK:]
            dv_ref[cur, cols] = dv2[BLK:]
            dk_ref[prev, cols] += dk2[0:BLK]
            dv_ref[prev, cols] += dv2[0:BLK]

    def body_carry(*refs):
        dk_ref, dv_ref, ck_scr, cv_scr = refs[10:]
        n = pl.program_id(1)

        @pl.when(n == 0)
        def _():
            ck_scr[...] = jnp.zeros_like(ck_scr)
            cv_scr[...] = jnp.zeros_like(cv_scr)

        @pl.when(n < nb)
        def _():
            for cols, dk2, dv2 in shares(n, *refs[:8], refs[9]):
                dk_ref[:, cols] = ck_scr[:, cols] + dk2[0:BLK]
                dv_ref[:, cols] = cv_scr[:, cols] + dv2[0:BLK]
                ck_scr[:, cols] = dk2[BLK:]
                cv_scr[:, cols] = dv2[BLK:]

        @pl.when(n == nb)
        def _():
            dk_ref[...] = ck_scr[...]
            dv_ref[...] = cv_scr[...]

    def blk(w, col, shift=0):
        return pl.BlockSpec((None, BLK, w), lambda r, n: (r, jnp.clip(n - shift, 0, nb - 1), col))

    if whole:
        dkv_spec = pl.BlockSpec((None, l_sub, D_ATTN), lambda r, n: (r, 0, 0))
        body, steps, scratch = body_whole, nb, []
    else:
        dkv_spec = blk(D_ATTN, 0, 1)
        body, steps, scratch = body_carry, nb + 1, [pltpu.VMEM((BLK, D_ATTN), F32)] * 2
    return pl.pallas_call(
        body, name=f"attn_bwd_d{d}", grid=(d, steps),
        in_specs=[blk(D_ATTN, 0), blk(D_ATTN, 1, 1), blk(D_ATTN, 1), blk(D_ATTN, 2, 1), blk(D_ATTN, 2),
                  blk(D_ATTN, 0), blk(128, 0), blk(128, 0), pl.BlockSpec(memory_space=pl.ANY)],
        out_specs=[blk(D_ATTN, 0), dkv_spec, dkv_spec],
        out_shape=[jax.ShapeDtypeStruct((d, l_sub, D_ATTN), F32)] * 3,
        scratch_shapes=scratch,
        compiler_params=_cp(dimension_semantics=("arbitrary", "arbitrary")),
    )(qkv, qkv, qkv, qkv, qkv, do, lse, delta, dep)


def _mix_ln1(os_, ls_, gm, x, w_o, ln1_g, ln1_b, dep):
    t = x.shape[0]
    expand = _head_expand()

    def body(o1, o4, o16, l1, l4, l16, gm_ref, x_ref, wo_ref, g_ref, b_ref, ex_ref, dep_ref,
             attn_ref, lse1_ref, lse4_ref, lse16_ref, cat_ref, xhat_ref, rstd_ref, x1b_ref, o_scr, l_scr):
        _from_planes(o4, o_scr, DILATIONS[1], LANE_CHUNKS)
        _from_planes(o16, o_scr.at[pl.ds(LANE_CHUNKS, LANE_CHUNKS)], DILATIONS[2], LANE_CHUNKS)
        _from_planes(l4, l_scr, DILATIONS[1], 1)
        _from_planes(l16, l_scr.at[pl.ds(1, 1)], DILATIONS[2], 1)
        la, lb, lc = l1[...], l_scr[0], l_scr[1]
        m = jnp.maximum(jnp.maximum(la, lb), lc)
        ea, eb, ec = jnp.exp(la - m), jnp.exp(lb - m), jnp.exp(lc - m)
        den = ea + eb + ec
        inv = 1.0 / den
        wide = lambda w: _dot_select(w, ex_ref[...])
        attn = (wide(ea * inv) * o1[...] + wide(eb * inv) * _unchunk(o_scr, LANE_CHUNKS)
                + wide(ec * inv) * _unchunk(o_scr, LANE_CHUNKS, LANE_CHUNKS))
        attn_ref[...] = attn
        lse = m + jnp.log(den)
        lse1_ref[...] = lse
        l_scr[2] = lse
        _to_planes(lse4_ref, l_scr.at[pl.ds(2, 1)], DILATIONS[1], 1, F32)
        _to_planes(lse16_ref, l_scr.at[pl.ds(2, 1)], DILATIONS[2], 1, F32)
        cat_ref[:, 0:D_ATTN] = attn.astype(MXU)
        cat_ref[:, D_ATTN:] = gm_ref[...]
        mix = jnp.dot(cat_ref[...], wo_ref[...], preferred_element_type=F32)
        xhat, rstd = _ln_fwd(ALPHA * x_ref[...] + mix)
        xhat_ref[...] = xhat
        rstd_ref[...] = rstd
        x1b_ref[...] = (xhat * g_ref[...] + b_ref[...]).astype(MXU)

    tok = lambda w: pl.BlockSpec((TM, w), lambda i: (i, 0))
    outs = [jax.ShapeDtypeStruct((t, D_ATTN), F32)] + [_perm_shape(t, d, 128, F32) for d in DILATIONS] + [
        jax.ShapeDtypeStruct((t, D_MODEL), MXU), jax.ShapeDtypeStruct((t, D_MODEL), F32), jax.ShapeDtypeStruct((t, 1), F32),
        jax.ShapeDtypeStruct((t, D_MODEL), MXU)]
    return pl.pallas_call(
        body, name="mix_ln1", grid=(t // TM,),
        in_specs=[_perm_tile(d, D_ATTN) for d in DILATIONS] + [_perm_tile(d, 128) for d in DILATIONS]
        + [tok(D_GMLP), tok(D_MODEL), _full(w_o.shape), _full(ln1_g.shape), _full(ln1_b.shape), _full(expand.shape),
           pl.BlockSpec(memory_space=pl.ANY)],
        out_specs=[tok(D_ATTN)] + [_perm_tile(d, 128) for d in DILATIONS] + [tok(D_MODEL), tok(D_MODEL), tok(1), tok(D_MODEL)],
        out_shape=outs,
        scratch_shapes=[pltpu.VMEM((2 * LANE_CHUNKS, TM, 128), F32), pltpu.VMEM((3, TM, 128), F32)],
        compiler_params=_cp(dimension_semantics=("arbitrary",)),
    )(*os_, *ls_, gm, x, w_o, ln1_g, ln1_b, expand, dep)


def _conv_fwd(a_ext, w_ref, b_ref, rows):
    back = [pltpu.roll(a_ext, s, 0)[HALO:HALO + rows] for s in (1, 2)]
    return b_ref[...] + w_ref[2:3, :] * a_ext[HALO:HALO + rows] + w_ref[1:2, :] * back[0] + w_ref[0:1, :] * back[1]


def _ffn_in(x1b, w_a, w_b, conv_w, conv_b):
    t = x1b.shape[0]
    hb = TM // HALO

    def body(x_ref, xh_ref, wa_ref, wb_ref, cw_ref, cb_ref, apre_ref, act_ref, gate_ref, f_ref):
        i = pl.program_id(1)
        a_pre = _dot_nt(x_ref[...], wa_ref[...])
        a_halo = jnp.where(i > 0, _dot_nt(xh_ref[...], wa_ref[...]), 0.0)
        a = _conv_fwd(jnp.concatenate([a_halo, a_pre], axis=0), cw_ref, cb_ref, TM)
        b = _dot_nt(x_ref[...], wb_ref[...])
        cdf = 0.5 * (1.0 + lax.erf(a * (1.0 / math.sqrt(2.0))))
        pdf = jnp.exp(-0.5 * a * a) * (1.0 / math.sqrt(2.0 * math.pi))
        act = a * cdf
        apre_ref[...] = a_pre
        act_ref[...] = act
        gate_ref[...] = b * (cdf + a * pdf)
        f_ref[...] = (act * b).astype(MXU)

    blk = lambda r, c: pl.BlockSpec((None, r, c), lambda j, i: (j, 0, 0))
    tokj = pl.BlockSpec((None, TM, FF_BLK), lambda j, i: (j, i, 0))
    outs = [jax.ShapeDtypeStruct((N_SHARD, t, FF_BLK), F32)] * 3 + [jax.ShapeDtypeStruct((N_SHARD, t, FF_BLK), MXU)]
    return pl.pallas_call(
        body, name="ffn_in", grid=(N_SHARD, t // TM),
        in_specs=[pl.BlockSpec((TM, D_MODEL), lambda j, i: (i, 0)),
                  pl.BlockSpec((HALO, D_MODEL), lambda j, i: (jnp.maximum(i * hb - 1, 0), 0)),
                  blk(FF_BLK, D_MODEL), blk(FF_BLK, D_MODEL), blk(3, FF_BLK), blk(1, FF_BLK)],
        out_specs=[tokj, tokj, tokj, tokj], out_shape=outs,
        compiler_params=_cp(dimension_semantics=("arbitrary", "arbitrary")),
    )(x1b, x1b, w_a, w_b, conv_w, conv_b)


def _ffn_out_ln2(f, w_down, xhat1, ln1_g, ln1_b):
    t = xhat1.shape[0]

    def body(f_ref, wd_ref, xh_ref, g1_ref, b1_ref, xhat_ref, rstd_ref):
        half = TM // ROW_GROUPS
        for r0 in range(0, TM, half):
            rows = pl.ds(r0, half)
            ff = jnp.dot(f_ref[0, rows, :], wd_ref[0], preferred_element_type=F32)
            for j in range(1, N_SHARD):
                ff = ff + jnp.dot(f_ref[j, rows, :], wd_ref[j], preferred_element_type=F32)
            x1 = xh_ref[rows, :] * g1_ref[...] + b1_ref[...]
            xhat, rstd = _ln_fwd(ALPHA * x1 + ff)
            xhat_ref[rows, :] = xhat
            rstd_ref[rows, :] = rstd

    tok = lambda w: pl.BlockSpec((TM, w), lambda i: (i, 0))
    vec = _full((1, D_MODEL))
    outs = [jax.ShapeDtypeStruct((t, D_MODEL), F32), jax.ShapeDtypeStruct((t, 1), F32)]
    return pl.pallas_call(
        body, name="ffn_out_ln2", grid=(t // TM,),
        in_specs=[pl.BlockSpec((N_SHARD, TM, FF_BLK), lambda i: (0, i, 0)), _full(w_down.shape), tok(D_MODEL), vec, vec],
        out_specs=[tok(D_MODEL), tok(1)], out_shape=outs,
        compiler_params=_cp(dimension_semantics=("arbitrary",)),
    )(f, w_down, xhat1, ln1_g, ln1_b)


STAT_ROWS = 8


def _ple_loss_bwd(xhat2, rstd2, p, target, ln2_g, ln2_b, w_g, b_g, w_p, ln3_g, ln3_b):
    t = xhat2.shape[0]

    def body(xh2_ref, rs2_ref, p_ref, t_ref, g2_ref, b2_ref, wg_ref, bg_ref, wp_ref, g3_ref, b3_ref,
             dr2_ref, dr2b_ref, stat_ref, dwg_ref, dwp_ref, pp_scr, dwp_scr):
        @pl.when(pl.program_id(0) == 0)
        def _():
            stat_ref[...] = jnp.zeros_like(stat_ref)
            dwg_ref[...] = jnp.zeros_like(dwg_ref)
            dwp_scr[...] = jnp.zeros_like(dwp_scr)

        xhat2 = xh2_ref[...]
        x2 = xhat2 * g2_ref[...] + b2_ref[...]
        x2b = x2.astype(MXU)
        gate = jax.nn.sigmoid(jnp.dot(x2b, wg_ref[...], preferred_element_type=F32) + bg_ref[...])
        pb = p_ref[...].astype(MXU)
        for j in range(N_SHARD):
            pp_scr[:, j * ROW_BLK:(j + 1) * ROW_BLK] = jnp.dot(pb, wp_ref[j], preferred_element_type=F32)
        pp = pp_scr[...]
        xhat3, rstd3 = _ln_fwd(ALPHA * x2 + gate * pp)
        err = xhat3 * g3_ref[...] + b3_ref[...] - t_ref[...]
        dy = err * (1.0 / D_MODEL)
        dr3 = _ln_bwd(dy, xhat3, rstd3, g3_ref[...])
        dgp = dr3 * pp * gate * (1.0 - gate)
        dgp_b = dgp.astype(MXU)
        dwg_ref[...] += _dot_tn(x2b, dgp_b)
        dwp_scr[...] += _dot_tn(pb, dr3 * gate)
        dx2 = ALPHA * dr3 + _dot_nt(dgp_b, wg_ref[...])
        dr2 = _ln_bwd(dx2, xhat2, rs2_ref[...], g2_ref[...])
        dr2_ref[...] = dr2
        dr2b_ref[...] = dr2.astype(MXU)
        stat_ref[0:1, :] += _colsum(dy * xhat3)
        stat_ref[1:2, :] += _colsum(dy)
        stat_ref[2:3, :] += _colsum(dgp)
        stat_ref[3:4, :] += _colsum(dx2 * xhat2)
        stat_ref[4:5, :] += _colsum(dx2)
        stat_ref[5:6, :] += _colsum(err * err)

        @pl.when(pl.program_id(0) == t // TM - 1)
        def _():
            for j in range(N_SHARD):
                dwp_ref[j] = dwp_scr[:, j * ROW_BLK:(j + 1) * ROW_BLK]

    tok = lambda w: pl.BlockSpec((TM, w), lambda i: (i, 0))
    vec = _full((1, D_MODEL))
    outs = [jax.ShapeDtypeStruct((t, D_MODEL), F32), jax.ShapeDtypeStruct((t, D_MODEL), MXU),
            jax.ShapeDtypeStruct((STAT_ROWS, D_MODEL), F32), jax.ShapeDtypeStruct((D_MODEL, D_MODEL), F32),
            jax.ShapeDtypeStruct((N_SHARD, D_PLE, ROW_BLK), F32)]
    return pl.pallas_call(
        body, name="ple_loss_bwd", grid=(t // TM,),
        in_specs=[tok(D_MODEL), tok(1), tok(D_PLE), tok(D_MODEL), vec, vec, _full(w_g.shape), vec, _full(w_p.shape), vec, vec],
        out_specs=[tok(D_MODEL), tok(D_MODEL), _full((STAT_ROWS, D_MODEL)), _full((D_MODEL, D_MODEL)),
                   _full((N_SHARD, D_PLE, ROW_BLK))], out_shape=outs,
        scratch_shapes=[pltpu.VMEM((TM, D_MODEL), F32), pltpu.VMEM((D_PLE, D_MODEL), F32)],
        compiler_params=_cp(dimension_semantics=("arbitrary",)),
    )(xhat2, rstd2, p, target, ln2_g, ln2_b, w_g, b_g, w_p, ln3_g, ln3_b)


def _ffn_bwd(dr2, dr2b, a_pre, act, gate, w_down, w_a, w_b, conv_w, xhat1, rstd1, ln1_g, cat):
    t = dr2.shape[0]
    nt = t // TM
    hb = TM // HALO
    last_h = t // HALO - 1
    halo2 = 2 * HALO

    def body(dr_ref, drb_ref, drbn_ref, ap_ref, act_ref, gate_ref, gaten_ref, wd_ref, wa_ref, wb_ref, cw_ref,
             xh_ref, rs_ref, g1_ref, cat_ref, dap_ref, dbb_ref, dr1_ref, cstat_ref, lstat_ref, dwo_ref, acc_scr):
        i, j = pl.program_id(0), pl.program_id(1)

        @pl.when((i == 0) & (j == 0))
        def _():
            cstat_ref[...] = jnp.zeros_like(cstat_ref)
            lstat_ref[...] = jnp.zeros_like(lstat_ref)
            dwo_ref[...] = jnp.zeros_like(dwo_ref)

        half = TM // ROW_GROUPS
        parts = []
        for r0 in range(0, TM, half):
            rows = pl.ds(r0, half)
            last = r0 + half == TM

            def ext(ref, nxt):
                return jnp.concatenate([ref[rows], nxt[...]], axis=0) if last else ref[r0:r0 + half + HALO]

            drb = jnp.concatenate([drb_ref[rows, :], drbn_ref[...]], axis=0) if last else drb_ref[r0:r0 + half + halo2, :]
            df = _dot_nt(drb, wd_ref[...])[0:half + HALO]
            da = df * ext(gate_ref, gaten_ref)
            if last:
                da = jnp.concatenate([da[0:half], jnp.where(i < nt - 1, da[half:], 0.0)], axis=0)
            ahead = [da[0:half]] + [pltpu.roll(da, half + HALO - s, 0)[0:half] for s in (1, 2)]
            da_pre = cw_ref[2:3, :] * ahead[0] + cw_ref[1:2, :] * ahead[1] + cw_ref[0:1, :] * ahead[2]
            dbb = df[0:half] * act_ref[rows, :]
            dap_ref[rows, :] = da_pre.astype(MXU)
            dbb_ref[rows, :] = dbb.astype(MXU)
            for kk in range(3):
                cstat_ref[j, kk:kk + 1, :] += _colsum(ahead[2 - kk] * ap_ref[rows, :])
            cstat_ref[j, 3:4, :] += _colsum(ahead[0])
            parts.append(_dot(da_pre, wa_ref[...]) + _dot(dbb, wb_ref[...]))
        part = jnp.concatenate(parts, axis=0)

        @pl.when(j == 0)
        def _():
            acc_scr[...] = ALPHA * dr_ref[...] + part

        @pl.when(j > 0)
        def _():
            acc_scr[...] += part

        @pl.when(j == N_SHARD - 1)
        def _():
            dx1 = acc_scr[...]
            xhat1 = xh_ref[...]
            lstat_ref[0:1, :] += _colsum(dx1 * xhat1)
            lstat_ref[1:2, :] += _colsum(dx1)
            dr1 = _ln_bwd(dx1, xhat1, rs_ref[...], g1_ref[...])
            dr1_ref[...] = dr1
            dwo_ref[...] += _dot_tn(cat_ref[...], dr1)

    tok = lambda w: pl.BlockSpec((TM, w), lambda i, j: (i, 0))
    tokj = pl.BlockSpec((None, TM, FF_BLK), lambda i, j: (j, i, 0))
    nextj = pl.BlockSpec((None, HALO, FF_BLK), lambda i, j: (j, jnp.minimum((i + 1) * hb, last_h), 0))
    blk = lambda r, c: pl.BlockSpec((None, r, c), lambda i, j: (j, 0, 0))
    outs = [jax.ShapeDtypeStruct((N_SHARD, t, FF_BLK), MXU)] * 2 + [
        jax.ShapeDtypeStruct((t, D_MODEL), F32), jax.ShapeDtypeStruct((N_SHARD, STAT_ROWS, FF_BLK), F32),
        jax.ShapeDtypeStruct((STAT_ROWS, D_MODEL), F32), jax.ShapeDtypeStruct((D_MODEL, D_MODEL), F32)]
    return pl.pallas_call(
        body, name="ffn_bwd", grid=(nt, N_SHARD),
        in_specs=[tok(D_MODEL), tok(D_MODEL),
                  pl.BlockSpec((halo2, D_MODEL), lambda i, j: (jnp.minimum((i + 1) * (hb // 2), last_h // 2), 0)),
                  tokj, tokj, tokj, nextj, blk(FF_BLK, D_MODEL), blk(FF_BLK, D_MODEL), blk(FF_BLK, D_MODEL),
                  blk(3, FF_BLK), tok(D_MODEL), tok(1), _full((1, D_MODEL)), tok(D_MODEL)],
        out_specs=[tokj, tokj, tok(D_MODEL), _full((N_SHARD, STAT_ROWS, FF_BLK)), _full((STAT_ROWS, D_MODEL)),
                   _full((D_MODEL, D_MODEL))], out_shape=outs,
        scratch_shapes=[pltpu.VMEM((TM, D_MODEL), F32)],
        compiler_params=_cp(dimension_semantics=("arbitrary", "arbitrary")),
    )(dr2, dr2b, dr2b, a_pre, act, gate, gate, w_down, w_a, w_b, conv_w, xhat1, rstd1, ln1_g, cat)


def _mix_bwd(dr1, w_o, hu, hz, mixed, attn, ln_z_g, ln_z_b, w_s, dep):
    t = dr1.shape[0]
    nchunk = TM // BLK

    def body(dr_ref, wo_ref, hu_ref, hz_ref, mx_ref, attn_ref, g_ref, b_ref, ws_ref, grp_ref, red_ref, dep_ref,
             do1_ref, do4_ref, do16_ref, dl1_ref, dl4_ref, dl16_ref, duz_ref, dws_ref, dbs_ref, zstat_ref,
             wm_scr, dzn_scr, dbsum_scr, do_scr, dl_scr):
        @pl.when(pl.program_id(0) == 0)
        def _():
            row = lax.broadcasted_iota(jnp.int32, (BLK, BLK), 0)
            col = lax.broadcasted_iota(jnp.int32, (BLK, BLK), 1)
            for g in range(N_HEADS):
                wm_scr[g] = jnp.where(col <= row, ws_ref[g], 0.0).astype(MXU)
            dws_ref[...] = jnp.zeros_like(dws_ref)
            dbsum_scr[...] = jnp.zeros_like(dbsum_scr)
            zstat_ref[...] = jnp.zeros_like(zstat_ref)

        dcat = _dot_nt(dr_ref[...], wo_ref[...])
        dattn = dcat[:, 0:D_ATTN]
        do1_ref[...] = dattn.astype(MXU)
        for cc, val in enumerate(_chunks(dattn)):
            do_scr[cc] = val
        _to_planes(do4_ref, do_scr, DILATIONS[1], LANE_CHUNKS, MXU)
        _to_planes(do16_ref, do_scr, DILATIONS[2], LANE_CHUNKS, MXU)
        delta = _dot_select(dattn * attn_ref[...], red_ref[...])
        dl1_ref[...] = delta
        dl_scr[0] = delta
        _to_planes(dl4_ref, dl_scr, DILATIONS[1], 1, F32)
        _to_planes(dl16_ref, dl_scr, DILATIONS[2], 1, F32)
        dgm = dcat[:, D_ATTN:]
        hu, hz = hu_ref[...], hz_ref[...]
        u = _gelu(hu)
        duz_ref[:, 0:D_GMLP] = (dgm * mx_ref[...] * _gelu_grad(hu)).astype(MXU)
        dmixed = dgm * u
        dmb = dmixed.astype(MXU)
        zhat, rstd = _ln_fwd(_gelu(hz))
        znb = (zhat * g_ref[...] + b_ref[...]).astype(MXU)
        dbs_acc = jnp.zeros((BLK, D_GMLP), F32)
        for ch in range(nchunk):
            rows = slice(ch * BLK, (ch + 1) * BLK)
            dbs_acc = dbs_acc + dmixed[rows]
            for g in range(N_HEADS):
                cols = slice(g * HEAD_DIM, (g + 1) * HEAD_DIM)
                dzn_scr[rows, cols] = _dot_tn(wm_scr[g], dmb[rows, cols])
                dws_ref[g] += _dot_nt(dmb[rows, cols], znb[rows, cols])
        dbsum_scr[...] += dbs_acc
        dzn = dzn_scr[...]
        zstat_ref[0:1, :] += _colsum(dzn * zhat)
        zstat_ref[1:2, :] += _colsum(dzn)
        duz_ref[:, D_GMLP:] = (_ln_bwd(dzn, zhat, rstd, g_ref[...]) * _gelu_grad(hz)).astype(MXU)

        @pl.when(pl.program_id(0) == nt - 1)
        def _():
            row = lax.broadcasted_iota(jnp.int32, (BLK, BLK), 0)
            col = lax.broadcasted_iota(jnp.int32, (BLK, BLK), 1)
            for g in range(N_HEADS):
                dws_ref[g] = jnp.where(col <= row, dws_ref[g], 0.0)
            dbs_ref[...] = lax.dot_general(grp_ref[...], dbsum_scr[...], (((1,), (1,)), ((), ())),
                                           precision=lax.Precision.HIGHEST, preferred_element_type=F32)

    nt = t // TM
    tok = lambda w: pl.BlockSpec((TM, w), lambda i: (i, 0))
    grp = jnp.asarray((np.arange(D_GMLP)[None, :] // HEAD_DIM == np.arange(N_HEADS)[:, None]).astype(np.float32))
    red = _head_reduce()
    outs = [_perm_shape(t, d, D_ATTN, MXU) for d in DILATIONS] + [_perm_shape(t, d, 128, F32) for d in DILATIONS] + [
        jax.ShapeDtypeStruct((t, 2 * D_GMLP), MXU),
        jax.ShapeDtypeStruct((N_HEADS, BLK, BLK), F32), jax.ShapeDtypeStruct((N_HEADS, BLK), F32),
        jax.ShapeDtypeStruct((STAT_ROWS, D_GMLP), F32)]
    return pl.pallas_call(
        body, name="mix_bwd", grid=(t // TM,),
        in_specs=[tok(D_MODEL), _full(w_o.shape), tok(D_GMLP), tok(D_GMLP), tok(D_GMLP), tok(D_ATTN), _full(ln_z_g.shape),
                  _full(ln_z_b.shape), _full(w_s.shape), _full(grp.shape), _full(red.shape), pl.BlockSpec(memory_space=pl.ANY)],
        out_specs=[_perm_tile(d, D_ATTN) for d in DILATIONS] + [_perm_tile(d, 128) for d in DILATIONS]
        + [tok(2 * D_GMLP), _full((N_HEADS, BLK, BLK)), _full((N_HEADS, BLK)), _full((STAT_ROWS, D_GMLP))],
        out_shape=outs,
        scratch_shapes=[pltpu.VMEM((N_HEADS, BLK, BLK), MXU), pltpu.VMEM((TM, D_GMLP), F32), pltpu.VMEM((BLK, D_GMLP), F32),
                        pltpu.VMEM((LANE_CHUNKS, TM, 128), F32), pltpu.VMEM((1, TM, 128), F32)],
        compiler_params=_cp(dimension_semantics=("arbitrary",)),
    )(dr1, w_o, hu, hz, mixed, attn, ln_z_g, ln_z_b, w_s, grp, red, dep)


def _dx_in(dqs, dks, dvs, duz, dr1, w_in, c_tab, s1_tab, s2_tab):
    t = dr1.shape[0]

    def body(dq1, dq4, dq16, dk1, dk4, dk16, dv1, dv4, dv16, duz_ref, dr_ref, w_ref, c_ref, s1_ref, s2_ref,
             dh_ref, dx_ref, acc_scr):
        sums = []
        for part, (g1, g4, g16) in enumerate(((dq1, dq4, dq16), (dk1, dk4, dk16), (dv1, dv4, dv16))):
            acc = acc_scr.at[pl.ds(part * LANE_CHUNKS, LANE_CHUNKS)]
            for cc in range(LANE_CHUNKS):
                acc[cc] = g1[:, cc * 128:(cc + 1) * 128]
            _from_planes(g4, acc, DILATIONS[1], LANE_CHUNKS, accumulate=True)
            _from_planes(g16, acc, DILATIONS[2], LANE_CHUNKS, accumulate=True)
            sums.append(_unchunk(acc_scr, LANE_CHUNKS, part * LANE_CHUNKS))
        c, s1, s2 = _tile_heads(c_ref[...]), _tile_heads(s1_ref[...]), _tile_heads(s2_ref[...])
        dh_ref[:, 0:D_ATTN] = _rope_apply_t(sums[0] * (1.0 / math.sqrt(HEAD_DIM)), c, s1, s2).astype(MXU)
        dh_ref[:, D_ATTN:2 * D_ATTN] = _rope_apply_t(sums[1], c, s1, s2).astype(MXU)
        dh_ref[:, 2 * D_ATTN:3 * D_ATTN] = sums[2].astype(MXU)
        dh_ref[:, 3 * D_ATTN:] = duz_ref[...]
        dx = ALPHA * dr_ref[...]
        for j in range(N_SHARD):
            dx = dx + _dot_nt(dh_ref[:, j * W_IN_BLK:(j + 1) * W_IN_BLK], w_ref[j])
        dx_ref[...] = dx

    tok = lambda w: pl.BlockSpec((TM, w), lambda i: (i, 0))
    outs = [jax.ShapeDtypeStruct((t, D_IN), MXU), jax.ShapeDtypeStruct((t, D_MODEL), F32)]
    return pl.pallas_call(
        body, name="dx_in", grid=(t // TM,),
        in_specs=[_perm_tile(d, D_ATTN) for d in DILATIONS] * 3
        + [tok(2 * D_GMLP), tok(D_MODEL), _full(w_in.shape), tok(128), tok(128), tok(128)],
        out_specs=[tok(D_IN), tok(D_MODEL)], out_shape=outs,
        scratch_shapes=[pltpu.VMEM((3 * LANE_CHUNKS, TM, 128), F32)],
        compiler_params=_cp(dimension_semantics=("arbitrary",)),
    )(*dqs, *dks, *dvs, duz, dr1, w_in, c_tab, s1_tab, s2_tab)


def _wgrad(name, x, dy, x_spec, dy_spec, out_spec, out_shape, grid, dep=None):
    deps = [] if dep is None else [dep]

    def body(x_ref, dy_ref, *rest):
        rest[-1][...] = _dot_tn(x_ref[...], dy_ref[...])

    return pl.pallas_call(
        body, name=name, grid=grid, in_specs=[x_spec, dy_spec] + [pl.BlockSpec(memory_space=pl.ANY)] * len(deps),
        out_specs=out_spec, out_shape=jax.ShapeDtypeStruct(out_shape, F32),
        compiler_params=_cp(dimension_semantics=("arbitrary",) * len(grid)),
    )(x, dy, *deps)


def _wgrad_pair(name, xa, xb, dy, x_spec, dy_spec, out_spec, out_shape, grid):
    def body(xa_ref, xb_ref, dy_ref, oa_ref, ob_ref):
        dy = dy_ref[...]
        oa_ref[...] = _dot_tn(xa_ref[...], dy)
        ob_ref[...] = _dot_tn(xb_ref[...], dy)

    return pl.pallas_call(
        body, name=name, grid=grid, in_specs=[x_spec, x_spec, dy_spec], out_specs=[out_spec, out_spec],
        out_shape=[jax.ShapeDtypeStruct(out_shape, F32)] * 2,
        compiler_params=_cp(dimension_semantics=("arbitrary",) * len(grid)),
    )(xa, xb, dy)


def _local_step(x, p, rope, target, w_in, start_dep, late_landed, late_weights, early_grads, early_grads_sent,
                early_grads_landed,
                ln_z_g, ln_z_b, w_s, b_s, ln1_g, ln1_b, conv_b, ln2_g, ln2_b, b_g, ln3_g, ln3_b):
    t = x.shape[0]
    half = TM
    c_tab, s1_tab, s2_tab = rope
    b_full = jnp.repeat(jnp.transpose(b_s[0]), HEAD_DIM, axis=1)
    conv_b4 = conv_b.reshape(N_SHARD, 1, FF_BLK)
    *qkvs, hu, hz, mixed, gm, xb = _qkvuz(x, w_in, c_tab, s1_tab, s2_tab, ln_z_g, ln_z_b, w_s[0], b_full, start_dep)
    branches = [_attn_fwd(qkv, d, start_dep) for qkv, d in zip(qkvs[:2], DILATIONS[:2])]
    dep = late_landed(branches[-1][1])
    branches.append(_attn_fwd(qkvs[2], DILATIONS[2], dep))
    w_o, w_a, w_b, conv_w, w_down, w_g, w_p = late_weights(branches[-1][1])
    attn, *lses, cat, xhat1, rstd1, x1b = _mix_ln1(
        [o for o, _ in branches], [l for _, l in branches], gm, x, w_o, ln1_g, ln1_b, dep)
    a_pre, act, gate, f = _ffn_in(x1b, w_a, w_b, conv_w, conv_b4)
    xhat2, rstd2 = _ffn_out_ln2(f, w_down, xhat1, ln1_g, ln1_b)
    dr2, dr2b, stat3, g_w_g, g_w_p = _ple_loss_bwd(xhat2, rstd2, p, target, ln2_g, ln2_b, w_g, b_g, w_p, ln3_g, ln3_b)
    da_pre, dbb, dr1, cstat, stat1, g_w_o = _ffn_bwd(dr2, dr2b, a_pre, act, gate, w_down, w_a, w_b, conv_w, xhat1, rstd1,
                                                    ln1_g, cat)

    full_t = lambda w, im: pl.BlockSpec((t, w), im)
    ffj = pl.BlockSpec((None, t, FF_BLK), lambda j, kk: (j, 0, 0))
    early = dict(
        w_ple_gate=g_w_g, w_ple_in=g_w_p,
        w_ff_down=_wgrad("dw_down", f, dr2b, ffj, full_t(half, lambda j, n: (0, n)),
                         pl.BlockSpec((None, FF_BLK, half), lambda j, n: (j, 0, n)), (N_SHARD, FF_BLK, D_MODEL), (N_SHARD, 2)),
        **dict(zip(("w_ff_a", "w_ff_b"), _wgrad_pair(
            "dw_ab", da_pre, dbb, x1b, ffj, full_t(half, lambda j, n: (0, n)),
            pl.BlockSpec((None, FF_BLK, half), lambda j, n: (j, 0, n)), (N_SHARD, FF_BLK, D_MODEL), (N_SHARD, 2)))),
        w_o=g_w_o)
    dep = early_grads(early)

    do1, do4, do16, dl1, dl4, dl16, duz, dws, dbs, zstat = _mix_bwd(
        dr1, w_o, hu, hz, mixed, attn, ln_z_g, ln_z_b, w_s[0], dep)
    dep = early_grads_sent(duz, (stat3, stat1, zstat, cstat, dws, dbs))
    dqkv = [_attn_bwd(qkv, do, lse, dl, d, dep)
            for qkv, do, lse, dl, d in zip(qkvs, (do1, do4, do16), lses, (dl1, dl4, dl16), DILATIONS)]
    dh, grad_x = _dx_in([g[0] for g in dqkv], [g[1] for g in dqkv], [g[2] for g in dqkv], duz, dr1, w_in,
                        c_tab, s1_tab, s2_tab)
    dep = early_grads_landed(grad_x)
    g_w_in = _wgrad("dw_in", xb, dh, full_t(half, lambda j, kk: (0, kk)), full_t(W_IN_BLK, lambda j, kk: (0, j)),
                    pl.BlockSpec((None, half, W_IN_BLK), lambda j, kk: (j, kk, 0)), (N_SHARD, D_MODEL, W_IN_BLK), (N_SHARD, 2),
                    dep)
    return grad_x, g_w_in


def _tile_rows(rows, mult, steps):
    if rows % mult:
        return rows
    return next(rows // k for k in range(steps, rows + 1) if rows % k == 0 and (rows // k) % mult == 0)


def _grid_spec(grid, in_specs, out_specs):
    return pltpu.PrefetchScalarGridSpec(num_scalar_prefetch=1, grid=grid, in_specs=in_specs, out_specs=out_specs)


def _on_own_steps(i, count, steps, work):
    if count == steps:
        work()
    else:
        pl.when(i < count)(work)


def _place_shards(name, ws, dtypes, place, dep):
    n = len(ws)
    tiles = [_tile_rows(w.shape[0], 16, 2) for w in ws]
    counts = [w.shape[0] // t for w, t in zip(ws, tiles)]
    steps = max(counts)

    def body(s_ref, *refs):
        i = pl.program_id(0)
        for a in range(n):
            def work(a=a):
                refs[n + 1 + a][...] = refs[a][...].astype(dtypes[a])
            _on_own_steps(i, counts[a], steps, work)

    def tile(a, lead):
        last = counts[a] - 1
        if lead:
            return pl.BlockSpec((None, tiles[a], ws[a].shape[1]), lambda i, s: (s[0], jnp.minimum(i, last), 0))
        return pl.BlockSpec((tiles[a], ws[a].shape[1]), lambda i, s: (jnp.minimum(i, last), 0))

    return pl.pallas_call(
        body, name=name,
        grid_spec=_grid_spec((steps,), [tile(a, False) for a in range(n)] + [pl.BlockSpec(memory_space=pl.ANY)],
                             [tile(a, True) for a in range(n)]),
        out_shape=[jax.ShapeDtypeStruct((N_SHARD, *w.shape), dt) for w, dt in zip(ws, dtypes)],
        compiler_params=_cp())(place, *ws, dep)


def _pair_sums(name, mines, gots, place):
    n = len(mines)
    tiles = [_tile_rows(g.shape[1], 16, 1) for g in gots]
    per_blk = [g.shape[1] // t for g, t in zip(gots, tiles)]
    counts = [N_SHARD * nh for nh in per_blk]
    steps = max(counts)

    def body(s_ref, *refs):
        i = pl.program_id(0)
        for a in range(n):
            def work(a=a):
                refs[2 * n + a][...] = (refs[a][...] + refs[n + a][...]).astype(BF16)
            _on_own_steps(i, counts[a], steps, work)

    def tile(a, mine):
        nh, last = per_blk[a], counts[a] - 1

        def index(i, s):
            g = jnp.minimum(i, last)
            return (g // nh, (s[1] * nh if mine else 0) + g % nh, 0)

        return pl.BlockSpec((None, tiles[a], gots[a].shape[2]), index)

    return pl.pallas_call(
        body, name=name,
        grid_spec=_grid_spec((steps,), [tile(a, True) for a in range(n)] + [tile(a, False) for a in range(n)],
                             [tile(a, False) for a in range(n)]),
        out_shape=[jax.ShapeDtypeStruct(g.shape, BF16) for g in gots], compiler_params=_cp())(place, *mines, *gots)


def _chip_sums(name, owns, landeds, place, dep):
    n = len(owns)
    tiles = [_tile_rows(o.shape[1], 16, 4) for o in owns]
    counts = [o.shape[1] // t for o, t in zip(owns, tiles)]
    steps = max(counts)

    def body(s_ref, *refs):
        i = pl.program_id(0)
        for a in range(n):
            def work(a=a):
                own, l1, l2, l3 = (refs[4 * a + k][...].astype(F32) for k in range(4))
                refs[4 * n + 1 + a][...] = ((own + l1) + l2) + l3
            _on_own_steps(i, counts[a], steps, work)

    def slot(a, d):
        last = counts[a] - 1
        return pl.BlockSpec((None, tiles[a], owns[a].shape[2]), lambda i, s: ((s[0] + d) % N_SHARD, jnp.minimum(i, last), 0))

    def out(a):
        nh, last = counts[a], counts[a] - 1
        return pl.BlockSpec((tiles[a], owns[a].shape[2]), lambda i, s: (s[1] * nh + jnp.minimum(i, last), 0))

    operands = [x for o, l in zip(owns, landeds) for x in (o, l, l, l)]
    return pl.pallas_call(
        body, name=name,
        grid_spec=_grid_spec((steps,), [slot(a, d) for a in range(n) for d in range(4)] + [pl.BlockSpec(memory_space=pl.ANY)],
                             [out(a) for a in range(n)]),
        out_shape=[jax.ShapeDtypeStruct((2 * o.shape[1], o.shape[2]), F32) for o in owns],
        compiler_params=_cp())(place, *operands, dep)


def _adamw_math(w, g, m, v):
    m = ADAM_B1 * m + (1.0 - ADAM_B1) * g
    v = ADAM_B2 * v + (1.0 - ADAM_B2) * (g * g)
    m_hat = m / (1.0 - ADAM_B1 ** ADAM_STEP)
    v_hat = v / (1.0 - ADAM_B2 ** ADAM_STEP)
    delta = -ADAM_LR * (m_hat / (jnp.sqrt(v_hat) + ADAM_EPS) + ADAM_WD * w)
    return delta, m, v


def _adamw_shards(name, ws, gs, ms, vs):
    n = len(ws)
    tiles = [_tile_rows(w.shape[1], 8, 8 if n > 1 else 2) for w in ws]
    counts = [w.shape[1] // t for w, t in zip(ws, tiles)]
    steps = max(counts)

    def body(*refs):
        i = pl.program_id(0)
        for a in range(n):
            def work(a=a):
                w_ref, g_ref, m_ref, v_ref = refs[4 * a:4 * a + 4]
                go_ref, d_ref, nm_ref, nv_ref = refs[4 * n + 4 * a:4 * n + 4 * a + 4]
                g = g_ref[...]
                go_ref[...] = g
                d_ref[...], nm_ref[...], nv_ref[...] = _adamw_math(w_ref[...], g, m_ref[...], v_ref[...])
            _on_own_steps(i, counts[a], steps, work)

    def tile(a, lead):
        last, c = counts[a] - 1, ws[a].shape[2]
        if lead:
            return pl.BlockSpec((None, tiles[a], c), lambda i: (0, jnp.minimum(i, last), 0))
        return pl.BlockSpec((tiles[a], c), lambda i: (jnp.minimum(i, last), 0))

    res = pl.pallas_call(
        body, name=name, grid=(steps,),
        in_specs=[tile(a, lead) for a in range(n) for lead in (True, False, True, True)],
        out_specs=[tile(a, True) for a in range(n) for _ in range(4)],
        out_shape=[jax.ShapeDtypeStruct(w.shape, F32) for w in ws for _ in range(4)],
        compiler_params=_cp())(*[x for quad in zip(ws, gs, ms, vs) for x in quad])
    return [tuple(res[4 * a:4 * a + 4]) for a in range(n)]


MESH = pl.DeviceIdType.MESH
ANY = pl.BlockSpec(memory_space=pl.ANY)


def _place():
    x, y, c = lax.axis_index("x"), lax.axis_index("y"), lax.axis_index("c")
    chips = [(1 - x, y), (x, 1 - y), (1 - x, 1 - y)]
    return x, y, c, 2 * x + y, chips


def _remote(src, dst, send_sem, recv_sem, dev):
    return pltpu.make_async_remote_copy(src_ref=src, dst_ref=dst, send_sem=send_sem, recv_sem=recv_sem,
                                        device_id=dev, device_id_type=MESH)


def _half(ref, hc, rows):
    return ref.at[pl.ds(hc * (rows // 2), rows // 2)]


def _sibling_join(blocks, tag):
    n = len(blocks)

    def body(*refs):
        outs = refs[n:2 * n]
        send, recv = refs[2 * n:]
        x, y, c, _, _ = _place()
        cps = []
        for a in range(n):
            h = blocks[a].shape[0] // 2
            mine = outs[a].at[pl.ds(c * h, h)]
            cp = _remote(mine, mine, send.at[a], recv.at[a], (x, y, 1 - c))
            cp.start()
            cps.append(cp)
        for a, cp in enumerate(cps):
            h = blocks[a].shape[0] // 2
            theirs = outs[a].at[pl.ds((1 - c) * h, h)]
            _remote(theirs, theirs, send.at[a], recv.at[a], (x, y, 1 - c)).wait_recv()
            cp.wait_send()

    sem = pltpu.SemaphoreType.DMA
    return pl.pallas_call(body, name=f"rs_sibling_join_{tag}", in_specs=[ANY] * n, out_specs=[ANY] * n,
                          out_shape=[jax.ShapeDtypeStruct(b_.shape, b_.dtype) for b_ in blocks],
                          input_output_aliases={a: a for a in range(n)},
                          scratch_shapes=[sem((n,)), sem((n,))])(*blocks)


def _join_start(blocks, after, tag):
    n = len(blocks)

    def body(*refs):
        ins = refs[:n]
        send, recv = refs[n + 1], refs[n + 2]
        token = refs[2 * n + 3]
        x, y, c, _, _ = _place()
        for a in range(n):
            h = blocks[a].shape[0] // 2
            mine = ins[a].at[pl.ds(c * h, h)]
            _remote(mine, mine, send.at[a], recv.at[a], (x, y, 1 - c)).start()
        token[...] = jnp.zeros_like(token)

    sems = pltpu.SemaphoreType.DMA((n,))
    res = pl.pallas_call(
        body, name=f"join_start_{tag}", in_specs=[HBM] * n + [ANY],
        out_specs=[SEM, SEM] + [HBM] * n + [pl.BlockSpec(memory_space=pltpu.VMEM)],
        out_shape=[sems, sems] + [pltpu.HBM(b_.shape, b_.dtype) for b_ in blocks] + [TOKEN],
        input_output_aliases={a: a + 2 for a in range(n)}, compiler_params=_in_flight_params(),
    )(*[_in_hbm(b_) for b_ in blocks], after)
    return res[0], res[1], res[2:2 + n], res[2 + n]


def _join_wait(send, recv, blocks, after, tag):
    n = len(blocks)

    def body(*refs):
        ins = refs[:n]
        send_ref, recv_ref = refs[n], refs[n + 1]
        x, y, c, _, _ = _place()
        for a in range(n):
            h = blocks[a].shape[0] // 2
            mine, theirs = ins[a].at[pl.ds(c * h, h)], ins[a].at[pl.ds((1 - c) * h, h)]
            _remote(mine, mine, send_ref.at[a], recv_ref.at[a], (x, y, 1 - c)).wait_send()
            _remote(theirs, theirs, send_ref.at[a], recv_ref.at[a], (x, y, 1 - c)).wait_recv()

    return pl.pallas_call(
        body, name=f"join_wait_{tag}", in_specs=[HBM] * n + [SEM, SEM, ANY], out_specs=[HBM] * n,
        out_shape=[pltpu.HBM(b_.shape, b_.dtype) for b_ in blocks],
        input_output_aliases={a: a for a in range(n)}, compiler_params=_in_flight_params(),
    )(*blocks, send, recv, after)


HBM = pl.BlockSpec(memory_space=pltpu.HBM)
SEM = pl.BlockSpec(memory_space=pltpu.SEMAPHORE)
TOKEN = jax.ShapeDtypeStruct((8, 128), F32)


def _in_flight_params():
    return pltpu.CompilerParams(has_side_effects=pltpu.SideEffectType.DATAFLOW_SIDE_EFFECTING)


def _in_hbm(a):
    return pltpu.with_memory_space_constraint(a, pltpu.HBM)


def _gather_piece(ref, rows, split, slot, hc):
    return _half(ref.at[slot], hc, rows) if split else ref.at[slot]


def _gather_start(stacks, split, after, tag):
    n = len(stacks)

    def body(*refs):
        ins = refs[:n]
        send, recv = refs[n + 1], refs[n + 2]
        token = refs[2 * n + 3]
        _, _, c, j, chips = _place()
        for a in range(n):
            mine = _gather_piece(ins[a], stacks[a].shape[1], split[a], j, c)
            for t in range(3):
                _remote(mine, mine, send.at[3 * a + t], recv.at[3 * a + t], (*chips[t], c)).start()
        token[...] = jnp.zeros_like(token)

    sems = pltpu.SemaphoreType.DMA((3 * n,))
    res = pl.pallas_call(
        body, name=f"gather_start_{tag}", in_specs=[HBM] * n + [ANY],
        out_specs=[SEM, SEM] + [HBM] * n + [pl.BlockSpec(memory_space=pltpu.VMEM)],
        out_shape=[sems, sems] + [pltpu.HBM(s.shape, s.dtype) for s in stacks] + [TOKEN],
        input_output_aliases={a: a + 2 for a in range(n)}, compiler_params=_in_flight_params(),
    )(*[_in_hbm(s) for s in stacks], after)
    return res[0], res[1], res[2:2 + n], res[2 + n]


def _gather_wait(send, recv, stacks, split, after, tag):
    n = len(stacks)

    def body(*refs):
        ins = refs[:n]
        send_ref, recv_ref = refs[n], refs[n + 1]
        _, _, c, j, chips = _place()
        for a in range(n):
            rows = stacks[a].shape[1]
            mine = _gather_piece(ins[a], rows, split[a], j, c)
            for t, (px, py) in enumerate(chips):
                theirs = _gather_piece(ins[a], rows, split[a], 2 * px + py, c)
                _remote(mine, mine, send_ref.at[3 * a + t], recv_ref.at[3 * a + t], (px, py, c)).wait_send()
                _remote(theirs, theirs, send_ref.at[3 * a + t], recv_ref.at[3 * a + t], (px, py, c)).wait_recv()

    return pl.pallas_call(
        body, name=f"gather_wait_{tag}", in_specs=[HBM] * n + [SEM, SEM, ANY], out_specs=[HBM] * n,
        out_shape=[pltpu.HBM(s.shape, s.dtype) for s in stacks],
        input_output_aliases={a: a for a in range(n)}, compiler_params=_in_flight_params(),
    )(*stacks, send, recv, after)


def _gather_forward(stacks, split, tag):
    idx = [a for a in range(len(stacks)) if split[a]]
    n = len(idx)

    def body(*refs):
        outs = refs[n:2 * n]
        send, recv = refs[2 * n:]
        x, y, c, _, chips = _place()
        sends = []
        for t, (px, py) in enumerate(chips):
            for a in range(n):
                blk = _half(outs[a].at[2 * px + py], c, stacks[idx[a]].shape[1])
                cp = _remote(blk, blk, send.at[a, t], recv.at[a, t], (x, y, 1 - c))
                cp.start()
                sends.append(cp)
        for t, (px, py) in enumerate(chips):
            for a in range(n):
                blk = _half(outs[a].at[2 * px + py], 1 - c, stacks[idx[a]].shape[1])
                _remote(blk, blk, send.at[a, t], recv.at[a, t], (x, y, 1 - c)).wait_recv()
        for cp in sends:
            cp.wait_send()

    sem = pltpu.SemaphoreType.DMA
    res = pl.pallas_call(
        body, name=f"gather_forward_{tag}", in_specs=[ANY] * n, out_specs=[ANY] * n,
        out_shape=[jax.ShapeDtypeStruct(stacks[a].shape, stacks[a].dtype) for a in idx],
        input_output_aliases={a: a for a in range(n)}, scratch_shapes=[sem((n, 3)), sem((n, 3))],
    )(*[stacks[a] for a in idx])
    out = list(stacks)
    for a, r in zip(idx, res):
        out[a] = r
    return out


def _forward_start(stacks, after, tag):
    n = len(stacks)

    def body(*refs):
        ins = refs[:n]
        send, recv = refs[n + 1], refs[n + 2]
        token = refs[2 * n + 3]
        x, y, c, _, chips = _place()
        for a in range(n):
            for t, (px, py) in enumerate(chips):
                blk = _half(ins[a].at[2 * px + py], c, stacks[a].shape[1])
                _remote(blk, blk, send.at[3 * a + t], recv.at[3 * a + t], (x, y, 1 - c)).start()
        token[...] = jnp.zeros_like(token)

    sems = pltpu.SemaphoreType.DMA((3 * n,))
    res = pl.pallas_call(
        body, name=f"forward_start_{tag}", in_specs=[HBM] * n + [ANY],
        out_specs=[SEM, SEM] + [HBM] * n + [pl.BlockSpec(memory_space=pltpu.VMEM)],
        out_shape=[sems, sems] + [pltpu.HBM(s.shape, s.dtype) for s in stacks] + [TOKEN],
        input_output_aliases={a: a + 2 for a in range(n)}, compiler_params=_in_flight_params(),
    )(*[_in_hbm(s) for s in stacks], after)
    return res[0], res[1], res[2:2 + n], res[2 + n]


def _forward_wait(send, recv, stacks, after, tag):
    n = len(stacks)

    def body(*refs):
        ins = refs[:n]
        send_ref, recv_ref = refs[n], refs[n + 1]
        x, y, c, _, chips = _place()
        for a in range(n):
            for t, (px, py) in enumerate(chips):
                mine = _half(ins[a].at[2 * px + py], c, stacks[a].shape[1])
                theirs = _half(ins[a].at[2 * px + py], 1 - c, stacks[a].shape[1])
                _remote(mine, mine, send_ref.at[3 * a + t], recv_ref.at[3 * a + t], (x, y, 1 - c)).wait_send()
                _remote(theirs, theirs, send_ref.at[3 * a + t], recv_ref.at[3 * a + t], (x, y, 1 - c)).wait_recv()

    return pl.pallas_call(
        body, name=f"forward_wait_{tag}", in_specs=[HBM] * n + [SEM, SEM, ANY], out_specs=[HBM] * n,
        out_shape=[pltpu.HBM(s.shape, s.dtype) for s in stacks],
        input_output_aliases={a: a for a in range(n)}, compiler_params=_in_flight_params(),
    )(*stacks, send, recv, after)


def _swap_start(grads, tag):
    n = len(grads)

    def body(*refs):
        ins, gots = refs[:n], refs[n:2 * n]
        send, recv = refs[2 * n], refs[2 * n + 1]
        token = refs[4 * n + 2]
        x, y, c, _, _ = _place()
        for a in range(n):
            h = grads[a].shape[1] // 2
            _remote(ins[a].at[:, pl.ds((1 - c) * h, h)], gots[a], send.at[a], recv.at[a], (x, y, 1 - c)).start()
        token[...] = jnp.zeros_like(token)

    sems = pltpu.SemaphoreType.DMA((n,))
    halves = [(g.shape[0], g.shape[1] // 2, g.shape[2]) for g in grads]
    res = pl.pallas_call(
        body, name=f"swap_start_{tag}", in_specs=[HBM] * (2 * n),
        out_specs=[SEM, SEM] + [HBM] * (2 * n) + [pl.BlockSpec(memory_space=pltpu.VMEM)],
        out_shape=[sems, sems] + [pltpu.HBM(g.shape, g.dtype) for g in grads] + [pltpu.HBM(s, F32) for s in halves] + [TOKEN],
        input_output_aliases={a: a + 2 for a in range(2 * n)}, compiler_params=_in_flight_params(),
    )(*[_in_hbm(g) for g in grads], *[_in_hbm(lax.empty(s, F32)) for s in halves])
    return res[0], res[1], res[2:2 + n], res[2 + n:2 + 2 * n], res[2 + 2 * n]


def _swap_wait(send, recv, grads, gots, after, tag):
    n = len(grads)

    def body(*refs):
        ins, lnd = refs[:n], refs[n:2 * n]
        send_ref, recv_ref = refs[2 * n], refs[2 * n + 1]
        x, y, c, _, _ = _place()
        for a in range(n):
            h = grads[a].shape[1] // 2
            cp = _remote(ins[a].at[:, pl.ds((1 - c) * h, h)], lnd[a], send_ref.at[a], recv_ref.at[a], (x, y, 1 - c))
            cp.wait_send()
            cp.wait_recv()

    bufs = [pltpu.HBM(g.shape, g.dtype) for g in grads] + [pltpu.HBM(g.shape, g.dtype) for g in gots]
    res = pl.pallas_call(
        body, name=f"swap_wait_{tag}", in_specs=[HBM] * (2 * n) + [SEM, SEM, ANY], out_specs=[HBM] * (2 * n),
        out_shape=bufs, input_output_aliases={a: a for a in range(2 * n)}, compiler_params=_in_flight_params(),
    )(*grads, *gots, send, recv, after)
    return res[:n], res[n:]


def _exchange_start(parts, tag):
    n = len(parts)

    def body(*refs):
        ins, lands = refs[:n], refs[n:2 * n]
        send, recv = refs[2 * n], refs[2 * n + 1]
        token = refs[4 * n + 2]
        _, _, c, j, chips = _place()
        for t, (px, py) in enumerate(chips):
            for a in range(n):
                _remote(ins[a].at[2 * px + py], lands[a].at[j], send.at[3 * a + t], recv.at[3 * a + t], (px, py, c)).start()
        token[...] = jnp.zeros_like(token)

    sems = pltpu.SemaphoreType.DMA((3 * n,))
    bufs = [pltpu.HBM(p.shape, p.dtype) for p in parts]
    res = pl.pallas_call(
        body, name=f"exchange_start_{tag}", in_specs=[HBM] * (2 * n),
        out_specs=[SEM, SEM] + [HBM] * (2 * n) + [pl.BlockSpec(memory_space=pltpu.VMEM)],
        out_shape=[sems, sems] + bufs + bufs + [TOKEN],
        input_output_aliases={a: a + 2 for a in range(2 * n)}, compiler_params=_in_flight_params(),
    )(*[_in_hbm(p) for p in parts], *[_in_hbm(lax.empty(p.shape, p.dtype)) for p in parts])
    return res[0], res[1], res[2:2 + n], res[2 + n:2 + 2 * n], res[2 + 2 * n]


def _exchange_wait(send, recv, parts, lands, after, tag):
    n = len(parts)

    def body(*refs):
        ins, lnd = refs[:n], refs[n:2 * n]
        send_ref, recv_ref = refs[2 * n], refs[2 * n + 1]
        _, _, c, j, chips = _place()
        for t, (px, py) in enumerate(chips):
            jt = 2 * px + py
            for a in range(n):
                _remote(ins[a].at[jt], lnd[a].at[j], send_ref.at[3 * a + t], recv_ref.at[3 * a + t], (px, py, c)).wait_send()
                _remote(ins[a].at[jt], lnd[a].at[jt], send_ref.at[3 * a + t], recv_ref.at[3 * a + t], (px, py, c)).wait_recv()

    bufs = [pltpu.HBM(p.shape, p.dtype) for p in parts]
    res = pl.pallas_call(
        body, name=f"exchange_wait_{tag}", in_specs=[HBM] * (2 * n) + [SEM, SEM, ANY], out_specs=[HBM] * (2 * n),
        out_shape=bufs + bufs, input_output_aliases={a: a for a in range(2 * n)}, compiler_params=_in_flight_params(),
    )(*parts, *lands, send, recv, after)
    return res[:n], res[n:]


def _small_chip_sums(arrs):
    n = len(arrs)

    def body(*refs):
        ins, outs = refs[:n], refs[n:2 * n]
        sib = refs[2 * n:3 * n]
        send, recv = refs[3 * n:]
        x, y, c, j, _ = _place()
        swaps = [_remote(ins[a], sib[a], send.at[a], recv.at[a], (x, y, 1 - c)) for a in range(n)]
        for cp in swaps:
            cp.start()
        for a in range(n):
            swaps[a].wait_recv()
            outs[a][j] = ins[a][...] + sib[a][...]
        for cp in swaps:
            cp.wait_send()

    sem = pltpu.SemaphoreType.DMA
    vm = pl.BlockSpec(memory_space=pltpu.VMEM)
    return pl.pallas_call(
        body, name="small_chip_sums", in_specs=[vm] * n, out_specs=[vm] * n,
        out_shape=[jax.ShapeDtypeStruct((N_SHARD, *a.shape), F32) for a in arrs],
        scratch_shapes=[pltpu.VMEM(a.shape, F32) for a in arrs] + [sem((n,)), sem((n,))],
        compiler_params=_cp(),
    )(*arrs)


def _small_totals(stacks):
    n = len(stacks)

    def body(*refs):
        for a in range(n):
            refs[n + a][...] = ((refs[a][0] + refs[a][1]) + refs[a][2]) + refs[a][3]

    return pl.pallas_call(body, name="small_totals", out_shape=[jax.ShapeDtypeStruct(s.shape[1:], F32) for s in stacks],
                          compiler_params=_cp())(*stacks)


SMALL_1024 = ("ln1_g", "ln1_b", "ln2_g", "ln2_b", "b_ple_gate", "ln3_g", "ln3_b")


def _adamw_small(red3, red1, redz, g_conv_w, redc, red_ws, red_bs, params):
    held_as = {"ln_z_g": (1, D_GMLP), "ln_z_b": (1, D_GMLP), "w_s": (N_HEADS * BLK, BLK), "b_s": (N_HEADS, BLK),
               "conv_w": (3, 1, FF_BLK), "conv_b": (1, D_FF), **{k: (1, D_MODEL) for k in SMALL_1024}}
    names = list(held_as)
    flat = [a.reshape(held_as[k]) for k in names for a in params[k]]

    def body(r3, r1, rz, gcw, rc, rws, rbs, *refs):
        ins, outs = refs[:3 * len(names)], refs[3 * len(names):]

        def grad_of(k):
            if k == "w_s":
                return rws[...]
            if k == "b_s":
                return rbs[...]
            src, row = {"ln3_g": (r3, 0), "ln3_b": (r3, 1), "b_ple_gate": (r3, 2), "ln2_g": (r3, 3), "ln2_b": (r3, 4),
                        "ln1_g": (r1, 0), "ln1_b": (r1, 1), "ln_z_g": (rz, 0), "ln_z_b": (rz, 1)}[k]
            return src[row:row + 1, :]

        for i, k in enumerate(names):
            w_ref, m_ref, v_ref = ins[3 * i:3 * i + 3]
            g_ref, d_ref, nm_ref, nv_ref = outs[4 * i:4 * i + 4]
            if k == "conv_b":
                for j in range(N_SHARD):
                    cols = slice(j * FF_BLK, (j + 1) * FF_BLK)
                    g = rc[j * STAT_ROWS + 3:j * STAT_ROWS + 4, :]
                    g_ref[:, cols] = g
                    d_ref[:, cols], nm_ref[:, cols], nv_ref[:, cols] = _adamw_math(w_ref[:, cols], g, m_ref[:, cols], v_ref[:, cols])
                continue
            if k == "conv_w":
                for tap in range(3):
                    g = gcw[tap:tap + 1, :]
                    g_ref[tap] = g
                    d_ref[tap], nm_ref[tap], nv_ref[tap] = _adamw_math(w_ref[tap], g, m_ref[tap], v_ref[tap])
                continue
            g = grad_of(k)
            g_ref[...] = g
            d_ref[...], nm_ref[...], nv_ref[...] = _adamw_math(w_ref[...], g, m_ref[...], v_ref[...])

    res = pl.pallas_call(
        body, name="adamw_small",
        out_shape=[jax.ShapeDtypeStruct(held_as[k], F32) for k in names for _ in range(4)],
        compiler_params=_cp(),
    )(red3, red1, redz, g_conv_w, redc, red_ws, red_bs, *flat)
    return {k: tuple(r.reshape(params[k][0].shape) for r in res[4 * i:4 * i + 4]) for i, k in enumerate(names)}


WEIGHTS = ("w_in", "ln_z_g", "ln_z_b", "w_s", "b_s", "w_o", "ln1_g", "ln1_b", "w_ff_a", "w_ff_b", "conv_w", "conv_b",
           "w_ff_down", "ln2_g", "ln2_b", "w_ple_gate", "b_ple_gate", "w_ple_in", "ln3_g", "ln3_b")
BIG = ("w_in", "w_o", "w_ff_a", "w_ff_b", "w_ff_down", "w_ple_gate", "w_ple_in")
TRANSPOSED = ("w_ff_a", "w_ff_b")
LATE = ("w_o", "w_ff_a", "w_ff_b", "w_ff_down", "w_ple_gate", "w_ple_in", "conv_w")


def kernel(x, p, positions, w_in, ln_z_g, ln_z_b, w_s, b_s, w_o, ln1_g, ln1_b, w_ff_a, w_ff_b, conv_w, conv_b, w_ff_down, ln2_g, ln2_b, w_ple_gate, b_ple_gate, w_ple_in, ln3_g, ln3_b, loss_target, m_w_in, m_ln_z_g, m_ln_z_b, m_w_s, m_b_s, m_w_o, m_ln1_g, m_ln1_b, m_w_ff_a, m_w_ff_b, m_conv_w, m_conv_b, m_w_ff_down, m_ln2_g, m_ln2_b, m_w_ple_gate, m_b_ple_gate, m_w_ple_in, m_ln3_g, m_ln3_b, v_w_in, v_ln_z_g, v_ln_z_b, v_w_s, v_b_s, v_w_o, v_ln1_g, v_ln1_b, v_w_ff_a, v_w_ff_b, v_conv_w, v_conv_b, v_w_ff_down, v_ln2_g, v_ln2_b, v_w_ple_gate, v_b_ple_gate, v_w_ple_in, v_ln3_g, v_ln3_b):
    args = locals()
    w = {k: args[k] for k in WEIGHTS}
    m = {k: args["m_" + k] for k in WEIGHTS}
    v = {k: args["v_" + k] for k in WEIGHTS}

    for k in TRANSPOSED:
        w[k], m[k], v[k] = (jnp.swapaxes(a, 1, 2) for a in (w[k], m[k], v[k]))

    chip = 2 * lax.axis_index("x") + lax.axis_index("y")
    place = jnp.stack([chip, lax.axis_index("c")]).astype(jnp.int32)
    stack = dict(zip(["w_in"], _place_shards("cast_w_in", [w["w_in"][0]], [MXU], place, place)))
    i_send, i_recv, in_flight, dep = _gather_start([stack["w_in"]], [True], place, "w_in")
    stack.update(zip(LATE, _place_shards("cast_late", [w[k][0] for k in LATE],
                                         [F32 if k == "conv_w" else MXU for k in LATE], place, dep)))
    split_late = [k != "conv_w" for k in LATE]
    g_send, g_recv, late_flight, start_dep = _gather_start([stack[k] for k in LATE], split_late, place, "late")
    rope = _rope_tables(positions, x.shape[1], start_dep)
    landed_in = _gather_wait(i_send, i_recv, in_flight, [True], rope[0], "w_in")
    w_in_full, = _gather_forward(landed_in, [True], "w_in")
    halves =[k for k, sp in zip(LATE, split_late) if sp]
    trips = {}

    def late_landed(after):
        fw = dict(zip(LATE, _gather_wait(g_send, g_recv, late_flight, split_late, after, "late")))
        trips["late"] = (fw, *_forward_start([fw[k] for k in halves], fw["conv_w"], "late"))
        return trips["late"][-1]

    def late_weights(after):
        fw, send, recv, flight, _ = trips["late"]
        fw.update(zip(halves, _forward_wait(send, recv, flight, after, "late")))
        return (fw["w_o"].reshape(D_MODEL, D_MODEL), fw["w_ff_a"], fw["w_ff_b"], fw["conv_w"], fw["w_ff_down"],
                fw["w_ple_gate"].reshape(D_MODEL, D_MODEL), fw["w_ple_in"])

    def swap_started(names, grads, tag):
        stacked = [g.reshape(N_SHARD, *w[k].shape[1:]) for k, g in zip(names, grads)]
        return (names, tag, *_swap_start(stacked, tag))

    def partial_sums(swap, after):
        names, tag, send, recv, stacked, gots, _ = swap
        stacked, got = _swap_wait(send, recv, stacked, gots, after, tag)
        pair = _pair_sums(f"rs_pair_{tag}", stacked, got, place)
        return (names, tag, *_exchange_start(pair, tag))

    def chip_summed(trip, after, dep):
        names, tag, send, recv, pair, lands, _ = trip
        pair, landed = _exchange_wait(send, recv, pair, lands, after, tag)
        return _chip_sums(f"rs_sum_{tag}", pair, landed, place, dep), names, tag

    def reduced(trip, after, dep):
        blocks, names, tag = chip_summed(trip, after, dep)
        return dict(zip(names, _sibling_join(blocks, tag)))

    def early_grads_landed(after):
        blocks, names, tag = chip_summed(trips["early"], after, trips["small"][-1])
        trips["join"] = (names, *_join_start(blocks, after, tag))
        return trips["join"][-1]

    def early_grads(grads):
        trips["swap"] = swap_started(list(grads), list(grads.values()), "early")
        return trips["swap"][-1]

    def early_grads_sent(after, small):
        trips["early"] = partial_sums(trips["swap"], after)
        stat3, stat1, zstat, cstat, dws, dbs = small
        sums = _small_chip_sums([stat3, stat1, zstat, cstat.reshape(N_SHARD * STAT_ROWS, FF_BLK),
                                 dws.reshape(N_HEADS * BLK, BLK), dbs])
        trips["small"] = _gather_start(sums, [False] * len(sums), trips["early"][-1], "small")
        return trips["small"][-1]

    grad_x, g_w_in = _local_step(
        x[0], p[0, 0], rope, loss_target[0], w_in_full, start_dep, late_landed, late_weights, early_grads, early_grads_sent,
        early_grads_landed, ln_z_g, ln_z_b, w_s, b_s, ln1_g, ln1_b, conv_b, ln2_g, ln2_b, b_ple_gate, ln3_g, ln3_b)

    swap_in = swap_started(["w_in"], [g_w_in], "w_in")
    out = {}

    s_send, s_recv, s_flight, _ = trips["small"]
    red3, red1, redz, redc, red_ws, red_bs = _small_totals(
        _gather_wait(s_send, s_recv, s_flight, [False] * len(s_flight), swap_in[-1], "small"))
    loss = (0.5 / D_MODEL) * jnp.sum(red3[5])
    g_conv_w = lax.dynamic_slice_in_dim(redc, chip * STAT_ROWS, STAT_ROWS, 0)
    names_small = [k for k in WEIGHTS if k not in BIG]
    out.update(_adamw_small(red3, red1, redz, g_conv_w, redc, red_ws, red_bs, {k: (w[k], m[k], v[k]) for k in names_small}))
    trips["w_in"] = partial_sums(swap_in, out["ln3_b"][3])

    def adamw(red, tag):
        names = list(red)
        steps = _adamw_shards(f"adamw_{tag}", [w[k] for k in names], [red[k] for k in names], [m[k] for k in names],
                              [v[k] for k in names])
        out.update(zip(names, steps))

    names, j_send, j_recv, j_flight, _ = trips["join"]
    early = dict(zip(names, _join_wait(j_send, j_recv, j_flight, trips["w_in"][-1], "early")))
    beside_join = ("w_ff_a",)
    adamw({k: g for k, g in early.items() if k not in beside_join}, "early")
    blocks, names_in, tag = chip_summed(trips["w_in"], out["w_o"][3], start_dep)
    wj_send, wj_recv, wj_flight, _ = _join_start(blocks, out["w_o"][3], tag)
    adamw({k: early[k] for k in beside_join}, "beside_join")
    adamw(dict(zip(names_in, _join_wait(wj_send, wj_recv, wj_flight, out[beside_join[-1]][3], tag))), "w_in")
    for k in TRANSPOSED:
        out[k] = tuple(jnp.swapaxes(a, 1, 2) for a in out[k])

    return (loss, grad_x[None], *[out[k][0] for k in WEIGHTS], *[out[k][1] for k in WEIGHTS],
            *[out[k][2] for k in WEIGHTS], *[out[k][3] for k in WEIGHTS])
```
